```python
import jax, jax.numpy as jnp
from jax import lax
import numpy as np

D_MODEL = 1024
BATCH = 8
SEQ = 8192
DEPTH = 4

N_A = DEPTH // 2
N_B = DEPTH - N_A
N_HEADS = 16
HEAD_DIM = D_MODEL // N_HEADS
CONV_WIDTH = 3
D_FF = ((8 * D_MODEL // 3 + 255) // 256) * 256
DILATED_GROUPS = ((128, 1), (512, 4), (2048, 16))
BLOCK = 128
EPS = 1e-6

kernel_name = "yoco_shortconv_dilated_alibi_trunk"


def _rms_norm(x, g):
    xf = x.astype(jnp.float32)
    y = xf * lax.rsqrt(jnp.mean(xf * xf, axis=-1, keepdims=True) + EPS)
    return (y * g.astype(jnp.float32)).astype(x.dtype)


def _swiglu(x, w_in, w_out):
    gate, up = jnp.split(x @ w_in, 2, axis=-1)
    return (jax.nn.silu(gate) * up) @ w_out


def _short_conv(x, w_in, w_conv, w_out):
    b_gate, c_gate, h = jnp.split(x @ w_in, 3, axis=-1)
    u = c_gate * h
    u = lax.conv_general_dilated(
        u, w_conv[:, None, :].astype(u.dtype), window_strides=(1,),
        padding=[(CONV_WIDTH - 1, 0)],
        dimension_numbers=("NWC", "WIO", "NWC"),
        feature_group_count=D_MODEL)
    return (b_gate * u) @ w_out


def _alibi_slopes():
    h = np.arange(N_HEADS, dtype=np.float32) + 1.0
    return jnp.asarray(np.power(2.0, -8.0 * h / N_HEADS), dtype=jnp.float32)


def _strided_blocks(t, dilation):
    b, s, h, e = t.shape
    mult = dilation * BLOCK
    sp = -(-s // mult) * mult
    t = jnp.pad(t, ((0, 0), (0, sp - s), (0, 0), (0, 0)))
    return t.reshape(b, sp // mult, BLOCK, dilation, h, e)


def _with_prev_block(tb):
    prev = jnp.concatenate([jnp.zeros_like(tb[:, :1]), tb[:, :-1]], axis=1)
    return jnp.concatenate([prev, tb], axis=2)


def _shared_kv(h, g_kv, w_kv):
    b, s, _ = h.shape
    k, v = jnp.split(_rms_norm(h, g_kv) @ w_kv, 2, axis=-1)
    k = k.reshape(b, s, N_HEADS, HEAD_DIM)
    v = v.reshape(b, s, N_HEADS, HEAD_DIM)
    return [(_with_prev_block(_strided_blocks(k, d)), _with_prev_block(_strided_blocks(v, d)))
            for (_, d) in DILATED_GROUPS]


def _branch_attention(q, k_cat, v_cat, slopes, window, dilation):
    b, s, h, e = q.shape
    qb = _strided_blocks(q, dilation)
    nb = qb.shape[1]
    sc = jnp.einsum("bnqrhe,bnkrhe->bnrhqk", qb, k_cat).astype(jnp.float32)
    u = jnp.arange(BLOCK)[:, None]
    kk = jnp.arange(2 * BLOCK)[None, :]
    delta = u + BLOCK - kk
    band = (delta >= 0) & (delta <= window // dilation)
    first = (jnp.arange(nb) == 0)[:, None, None]
    valid = band[None] & ~(first & (kk < BLOCK)[None])
    bias = -slopes[:, None, None] * (delta * dilation).astype(jnp.float32)[None]
    sc = jnp.where(valid[:, None, None], sc + bias, -jnp.inf)
    m = jnp.max(sc, axis=-1, keepdims=True)
    p = jnp.exp(sc - m)
    den = jnp.sum(p, axis=-1, keepdims=True)
    o = jnp.einsum("bnrhqk,bnkrhe->bnqrhe", p / den, v_cat.astype(jnp.float32))
    lse = (m + jnp.log(den))[..., 0]
    o = o.reshape(b, nb * BLOCK * dilation, h, e)[:, :s]
    lse = lse.transpose(0, 1, 4, 2, 3).reshape(b, nb * BLOCK * dilation, h)[:, :s]
    return o, lse


def _dilated_attention(x, w_q, w_o, shared, slopes):
    b, s, _ = x.shape
    q = (x @ w_q).reshape(b, s, N_HEADS, HEAD_DIM) * (HEAD_DIM ** -0.5)
    outs, lses = [], []
    for (window, dil), (k_cat, v_cat) in zip(DILATED_GROUPS, shared):
        o, l = _branch_attention(q, k_cat, v_cat, slopes, window, dil)
        outs.append(o)
        lses.append(l)
    wts = jax.nn.softmax(jnp.stack(lses), axis=0)
    o = jnp.sum(wts[..., None] * jnp.stack(outs), axis=0)
    return o.reshape(b, s, D_MODEL).astype(x.dtype) @ w_o


def _fwd_setup_inputs(seed: int = 0) -> dict:
    key = jax.random.key(seed)
    ks = jax.random.split(key, 12)
    f32 = jnp.float32
    nrm = lambda k, shape, fan_in: jax.random.normal(k, shape, f32) * (fan_in ** -0.5)
    return {
        "x": jax.random.normal(ks[0], (BATCH, SEQ, D_MODEL), f32),
        "norm_g": 1.0 + 0.05 * jax.random.normal(ks[1], (DEPTH, 4, D_MODEL), f32),
        "conv_in_w": nrm(ks[2], (N_A, D_MODEL, 3 * D_MODEL), D_MODEL),
        "conv_w": nrm(ks[3], (N_A, CONV_WIDTH, D_MODEL), CONV_WIDTH),
        "conv_out_w": nrm(ks[4], (N_A, D_MODEL, D_MODEL), D_MODEL),
        "kv_norm_g": 1.0 + 0.05 * jax.random.normal(ks[5], (D_MODEL,), f32),
        "kv_w": nrm(ks[6], (D_MODEL, 2 * D_MODEL), D_MODEL),
        "q_w": nrm(ks[7], (N_B, D_MODEL, D_MODEL), D_MODEL),
        "o_w": nrm(ks[8], (N_B, D_MODEL, D_MODEL), D_MODEL),
        "ffn_in_w": nrm(ks[9], (DEPTH, D_MODEL, 2 * D_FF), D_MODEL),
        "ffn_out_w": nrm(ks[10], (DEPTH, D_FF, D_MODEL), D_FF),
    }


def _fwd_reference(x, norm_g, conv_in_w, conv_w, conv_out_w, kv_norm_g, kv_w, q_w, o_w,
              ffn_in_w, ffn_out_w):
    slopes = _alibi_slopes()
    shared = None
    for layer in range(DEPTH):
        g = norm_g[layer]
        xn = _rms_norm(x, g[0])
        if layer < N_A:
            mix = _short_conv(xn, conv_in_w[layer], conv_w[layer], conv_out_w[layer])
        else:
            if shared is None:
                shared = _shared_kv(x, kv_norm_g, kv_w)
            j = layer - N_A
            mix = _dilated_attention(xn, q_w[j], o_w[j], shared, slopes)
        x = x + _rms_norm(mix, g[1])
        ff = _swiglu(_rms_norm(x, g[2]), ffn_in_w[layer], ffn_out_w[layer])
        x = x + _rms_norm(ff, g[3])
    return x


import jax as _jax
import jax.numpy as _jnp

TWIN_FORMAT = 'train_step'
FWD_PARAMS = ['x', 'norm_g', 'conv_in_w', 'conv_w', 'conv_out_w', 'kv_norm_g', 'kv_w', 'q_w', 'o_w', 'ffn_in_w', 'ffn_out_w']
TWIN_WEIGHTS = ['norm_g', 'conv_in_w', 'conv_w', 'conv_out_w', 'kv_norm_g', 'kv_w', 'q_w', 'o_w', 'ffn_in_w', 'ffn_out_w']
TWIN_DIFF_INPUT = 'x'
TWIN_INPUTS = ['x', 'norm_g', 'conv_in_w', 'conv_w', 'conv_out_w', 'kv_norm_g', 'kv_w', 'q_w', 'o_w', 'ffn_in_w', 'ffn_out_w', 'loss_target', 'm_norm_g', 'm_conv_in_w', 'm_conv_w', 'm_conv_out_w', 'm_kv_norm_g', 'm_kv_w', 'm_q_w', 'm_o_w', 'm_ffn_in_w', 'm_ffn_out_w', 'v_norm_g', 'v_conv_in_w', 'v_conv_w', 'v_conv_out_w', 'v_kv_norm_g', 'v_kv_w', 'v_q_w', 'v_o_w', 'v_ffn_in_w', 'v_ffn_out_w']
TWIN_OUTPUTS = ['loss', 'grad_x', 'grad_norm_g', 'grad_conv_in_w', 'grad_conv_w', 'grad_conv_out_w', 'grad_kv_norm_g', 'grad_kv_w', 'grad_q_w', 'grad_o_w', 'grad_ffn_in_w', 'grad_ffn_out_w', 'delta_norm_g', 'delta_conv_in_w', 'delta_conv_w', 'delta_conv_out_w', 'delta_kv_norm_g', 'delta_kv_w', 'delta_q_w', 'delta_o_w', 'delta_ffn_in_w', 'delta_ffn_out_w', 'new_m_norm_g', 'new_m_conv_in_w', 'new_m_conv_w', 'new_m_conv_out_w', 'new_m_kv_norm_g', 'new_m_kv_w', 'new_m_q_w', 'new_m_o_w', 'new_m_ffn_in_w', 'new_m_ffn_out_w', 'new_v_norm_g', 'new_v_conv_in_w', 'new_v_conv_w', 'new_v_conv_out_w', 'new_v_kv_norm_g', 'new_v_kv_w', 'new_v_q_w', 'new_v_o_w', 'new_v_ffn_in_w', 'new_v_ffn_out_w']
TWIN_LEAF_KINDS = {'loss': 'loss', 'grad_x': 'grad_x', 'grad_norm_g': 'grad_w', 'grad_conv_in_w': 'grad_w', 'grad_conv_w': 'grad_w', 'grad_conv_out_w': 'grad_w', 'grad_kv_norm_g': 'grad_w', 'grad_kv_w': 'grad_w', 'grad_q_w': 'grad_w', 'grad_o_w': 'grad_w', 'grad_ffn_in_w': 'grad_w', 'grad_ffn_out_w': 'grad_w', 'delta_norm_g': 'delta_w', 'delta_conv_in_w': 'delta_w', 'delta_conv_w': 'delta_w', 'delta_conv_out_w': 'delta_w', 'delta_kv_norm_g': 'delta_w', 'delta_kv_w': 'delta_w', 'delta_q_w': 'delta_w', 'delta_o_w': 'delta_w', 'delta_ffn_in_w': 'delta_w', 'delta_ffn_out_w': 'delta_w', 'new_m_norm_g': 'new_m', 'new_m_conv_in_w': 'new_m', 'new_m_conv_w': 'new_m', 'new_m_conv_out_w': 'new_m', 'new_m_kv_norm_g': 'new_m', 'new_m_kv_w': 'new_m', 'new_m_q_w': 'new_m', 'new_m_o_w': 'new_m', 'new_m_ffn_in_w': 'new_m', 'new_m_ffn_out_w': 'new_m', 'new_v_norm_g': 'new_v', 'new_v_conv_in_w': 'new_v', 'new_v_conv_w': 'new_v', 'new_v_conv_out_w': 'new_v', 'new_v_kv_norm_g': 'new_v', 'new_v_kv_w': 'new_v', 'new_v_q_w': 'new_v', 'new_v_o_w': 'new_v', 'new_v_ffn_in_w': 'new_v', 'new_v_ffn_out_w': 'new_v'}


def _forward(args):
    return _fwd_reference(*[args[k] for k in FWD_PARAMS])


def _output_shape():
    def fwd():
        inp = _fwd_setup_inputs(0)
        return _fwd_reference(*[inp[k] for k in FWD_PARAMS])
    out = _jax.eval_shape(fwd)
    return out.shape, out.dtype

N_MICROBATCH = 1
ADAM_LR = 0.001
ADAM_B1 = 0.9
ADAM_B2 = 0.999
ADAM_EPS = 1e-08
ADAM_WD = 0.01
ADAM_STEP = 10
PER_EXAMPLE_BATCH_AXIS = {'x': 0, 'loss_target': 0}
SHARED_INPUTS = []
_WEIGHT_DTYPES = {'norm_g': _jnp.float32, 'conv_in_w': _jnp.float32, 'conv_w': _jnp.float32, 'conv_out_w': _jnp.float32, 'kv_norm_g': _jnp.float32, 'kv_w': _jnp.float32, 'q_w': _jnp.float32, 'o_w': _jnp.float32, 'ffn_in_w': _jnp.float32, 'ffn_out_w': _jnp.float32}
MOMENT_SCALE = {'norm_g': 4.409309e+01, 'conv_in_w': 2.527615e+00, 'conv_w': 2.650895e+00, 'conv_out_w': 2.755386e+00, 'kv_norm_g': 4.469796e+00, 'kv_w': 2.855421e+00, 'q_w': 6.461136e-01, 'o_w': 3.515071e+00, 'ffn_in_w': 9.817231e-01, 'ffn_out_w': 1.839935e+00}


def _to_microbatches(a, axis):
    t = _jnp.moveaxis(a, axis, 0)
    t = t.reshape((N_MICROBATCH, t.shape[0] // N_MICROBATCH) + t.shape[1:])
    return _jnp.moveaxis(t, 1, axis + 1)


def setup_inputs(seed: int = 0) -> dict:
    inp = _fwd_setup_inputs(seed)
    key = _jax.random.fold_in(_jax.random.key(seed), 7919)
    shape, _ = _output_shape()
    out = dict(inp)
    out["loss_target"] = _jax.random.normal(_jax.random.fold_in(key, 0), shape, _jnp.float32)
    for i, name in enumerate(TWIN_WEIGHTS):
        w = inp[name].astype(_jnp.float32)
        if MOMENT_SCALE is None:
            s = _jnp.sqrt(_jnp.mean(_jnp.square(w)) + 1e-30)
        else:
            s = MOMENT_SCALE[name]
        km, kv = _jax.random.split(_jax.random.fold_in(key, i + 1))
        out[name] = w
        out["m_" + name] = s * _jax.random.normal(km, w.shape, _jnp.float32)
        out["v_" + name] = (s * s) * _jax.random.uniform(kv, w.shape, _jnp.float32, 0.5, 1.5)
    if N_MICROBATCH > 1:
        for name, axis in PER_EXAMPLE_BATCH_AXIS.items():
            out[name] = _to_microbatches(out[name], axis)
    return {'x': out['x'], 'norm_g': out['norm_g'], 'conv_in_w': out['conv_in_w'], 'conv_w': out['conv_w'], 'conv_out_w': out['conv_out_w'], 'kv_norm_g': out['kv_norm_g'], 'kv_w': out['kv_w'], 'q_w': out['q_w'], 'o_w': out['o_w'], 'ffn_in_w': out['ffn_in_w'], 'ffn_out_w': out['ffn_out_w'], 'loss_target': out['loss_target'], 'm_norm_g': out['m_norm_g'], 'm_conv_in_w': out['m_conv_in_w'], 'm_conv_w': out['m_conv_w'], 'm_conv_out_w': out['m_conv_out_w'], 'm_kv_norm_g': out['m_kv_norm_g'], 'm_kv_w': out['m_kv_w'], 'm_q_w': out['m_q_w'], 'm_o_w': out['m_o_w'], 'm_ffn_in_w': out['m_ffn_in_w'], 'm_ffn_out_w': out['m_ffn_out_w'], 'v_norm_g': out['v_norm_g'], 'v_conv_in_w': out['v_conv_in_w'], 'v_conv_w': out['v_conv_w'], 'v_conv_out_w': out['v_conv_out_w'], 'v_kv_norm_g': out['v_kv_norm_g'], 'v_kv_w': out['v_kv_w'], 'v_q_w': out['v_q_w'], 'v_o_w': out['v_o_w'], 'v_ffn_in_w': out['v_ffn_in_w'], 'v_ffn_out_w': out['v_ffn_out_w']}


def _loss(weights, diff, rest, loss_target):
    with _jax.named_scope("forward"):
        args = {**rest, TWIN_DIFF_INPUT: diff, **{k: w.astype(_WEIGHT_DTYPES[k]) for k, w in weights.items()}}
        y = _forward(args)
    with _jax.named_scope("loss_head"):
        err = _jnp.square(y.astype(_jnp.float32) - loss_target)
        return 0.5 * _jnp.sum(_jnp.mean(err, axis=-1)) if err.ndim else 0.5 * err


def _adamw(w, g, m, v):
    m = ADAM_B1 * m + (1.0 - ADAM_B1) * g
    v = ADAM_B2 * v + (1.0 - ADAM_B2) * _jnp.square(g)
    m_hat = m / (1.0 - ADAM_B1 ** ADAM_STEP)
    v_hat = v / (1.0 - ADAM_B2 ** ADAM_STEP)
    delta = -ADAM_LR * (m_hat / (_jnp.sqrt(v_hat) + ADAM_EPS) + ADAM_WD * w)
    return delta, m, v


def reference(x, norm_g, conv_in_w, conv_w, conv_out_w, kv_norm_g, kv_w, q_w, o_w, ffn_in_w, ffn_out_w, loss_target, m_norm_g, m_conv_in_w, m_conv_w, m_conv_out_w, m_kv_norm_g, m_kv_w, m_q_w, m_o_w, m_ffn_in_w, m_ffn_out_w, v_norm_g, v_conv_in_w, v_conv_w, v_conv_out_w, v_kv_norm_g, v_kv_w, v_q_w, v_o_w, v_ffn_in_w, v_ffn_out_w):
    given = dict(x=x, norm_g=norm_g, conv_in_w=conv_in_w, conv_w=conv_w, conv_out_w=conv_out_w, kv_norm_g=kv_norm_g, kv_w=kv_w, q_w=q_w, o_w=o_w, ffn_in_w=ffn_in_w, ffn_out_w=ffn_out_w, loss_target=loss_target, m_norm_g=m_norm_g, m_conv_in_w=m_conv_in_w, m_conv_w=m_conv_w, m_conv_out_w=m_conv_out_w, m_kv_norm_g=m_kv_norm_g, m_kv_w=m_kv_w, m_q_w=m_q_w, m_o_w=m_o_w, m_ffn_in_w=m_ffn_in_w, m_ffn_out_w=m_ffn_out_w, v_norm_g=v_norm_g, v_conv_in_w=v_conv_in_w, v_conv_w=v_conv_w, v_conv_out_w=v_conv_out_w, v_kv_norm_g=v_kv_norm_g, v_kv_w=v_kv_w, v_q_w=v_q_w, v_o_w=v_o_w, v_ffn_in_w=v_ffn_in_w, v_ffn_out_w=v_ffn_out_w)
    weights = {n: given[n] for n in TWIN_WEIGHTS}
    shared = {n: given[n] for n in SHARED_INPUTS}
    per_example = {n: given[n] for n in ['x']}
    grad_fn = _jax.value_and_grad(_loss, argnums=(0, 1))

    def one_microbatch(ex, loss_target):
        ex = dict(ex)
        diff = ex.pop(TWIN_DIFF_INPUT)
        return grad_fn(weights, diff, {**shared, **ex}, loss_target)

    if N_MICROBATCH == 1:
        loss, (grad_w, grad_x) = one_microbatch(per_example, given["loss_target"])
    else:
        def body(carry, xs):
            loss_sum, grad_sum = carry
            l_k, (gw_k, gx_k) = one_microbatch(xs[0], xs[1])
            with _jax.named_scope("update"):
                return (loss_sum + l_k, _jax.tree.map(_jnp.add, grad_sum, gw_k)), gx_k

        init = (_jnp.zeros((), _jnp.float32), _jax.tree.map(_jnp.zeros_like, weights))
        (loss, grad_w), grad_x = _jax.lax.scan(body, init, (per_example, given["loss_target"]))
    with _jax.named_scope("update"):
        delta_w, new_m, new_v = {}, {}, {}
        for n in TWIN_WEIGHTS:
            delta_w[n], new_m[n], new_v[n] = _adamw(weights[n], grad_w[n], given["m_" + n], given["v_" + n])
    return (loss, grad_x, *[grad_w[n] for n in TWIN_WEIGHTS], *[delta_w[n] for n in TWIN_WEIGHTS],
            *[new_m[n] for n in TWIN_WEIGHTS], *[new_v[n] for n in TWIN_WEIGHTS])
```

```python
import functools

import jax
import jax.numpy as jnp
import numpy as np
from jax import lax
from jax.experimental import pallas as pl
from jax.experimental.pallas import tpu as pltpu

F32 = jnp.float32
BF16 = jnp.bfloat16
HEAD_DIM = 64
ATT_BLOCK = 128
DILATIONS = (1, 4, 16)
NORM_EPS = 1e-6
NEG_BIG = -1e30
VMEM_LIMIT = 48 * 1024 * 1024
ROW_TILE = 256
LANE = 128
MESH = pl.DeviceIdType.MESH

ADAM_LR = 0.001
ADAM_B1 = 0.9
ADAM_B2 = 0.999
ADAM_EPS = 1e-08
ADAM_WD = 0.01
ADAM_STEP = 10

TILE_CANDIDATES = (1024, 1408, 768, 512, 384, 256, 128)


def _pick(dim, cands=TILE_CANDIDATES):
    for c in cands:
        if c <= dim and dim % c == 0:
            return c
    return dim


def _params(sem):
    return pltpu.CompilerParams(dimension_semantics=sem, vmem_limit_bytes=VMEM_LIMIT)


def _mm(a, b, mode, out_dtype, name, scale=None):
    if mode == "nn":
        m, k = a.shape
        n = b.shape[1]
    elif mode == "nt":
        m, k = a.shape
        n = b.shape[0]
    else:
        k, m = a.shape
        n = b.shape[1]
    tm, tn, tk = _pick(m), _pick(n), _pick(k)
    nk = k // tk
    if mode == "nn":
        a_spec = pl.BlockSpec((tm, tk), lambda i, j, kk: (i, kk))
        b_spec = pl.BlockSpec((tk, tn), lambda i, j, kk: (kk, j))
        dims = (((1,), (0,)), ((), ()))
    elif mode == "nt":
        a_spec = pl.BlockSpec((tm, tk), lambda i, j, kk: (i, kk))
        b_spec = pl.BlockSpec((tn, tk), lambda i, j, kk: (j, kk))
        dims = (((1,), (1,)), ((), ()))
    else:
        a_spec = pl.BlockSpec((tk, tm), lambda i, j, kk: (kk, i))
        b_spec = pl.BlockSpec((tk, tn), lambda i, j, kk: (kk, j))
        dims = (((0,), (0,)), ((), ()))

    def finish(acc):
        if scale is not None:
            acc = acc * scale
        return acc.astype(out_dtype)

    if nk == 1:
        def body(a_ref, b_ref, o_ref):
            o_ref[...] = finish(lax.dot_general(a_ref[...], b_ref[...], dims, preferred_element_type=F32))
        scratch = []
    else:
        def body(a_ref, b_ref, o_ref, acc_ref):
            kk = pl.program_id(2)

            @pl.when(kk == 0)
            def _():
                acc_ref[...] = jnp.zeros_like(acc_ref)

            acc_ref[...] += lax.dot_general(a_ref[...], b_ref[...], dims, preferred_element_type=F32)

            @pl.when(kk == nk - 1)
            def _():
                o_ref[...] = finish(acc_ref[...])
        scratch = [pltpu.VMEM((tm, tn), F32)]

    return pl.pallas_call(
        body, name=name,
        grid=(m // tm, n // tn, nk),
        in_specs=[a_spec, b_spec],
        out_specs=pl.BlockSpec((tm, tn), lambda i, j, kk: (i, j)),
        out_shape=jax.ShapeDtypeStruct((m, n), out_dtype),
        scratch_shapes=scratch,
        compiler_params=_params(("parallel", "parallel", "arbitrary")),
    )(a, b)


def _rstd(v):
    return lax.rsqrt(jnp.mean(v * v, axis=-1, keepdims=True) + NORM_EPS)


def _rms_bwd(dy, v, g, r):
    gy = dy * g
    dv = r * (gy - v * (r * r) * jnp.mean(gy * v, axis=-1, keepdims=True))
    return dv, dy * v * r


def _row_spec(t, width):
    return pl.BlockSpec((t, width), lambda i: (i, 0))


def _gain_spec(width):
    return pl.BlockSpec((1, width), lambda i: (0, 0))


def _norm_res_fwd(x, mix, g_post, pre_gains, name):
    s, d = x.shape
    t = _pick(s, (ROW_TILE,))
    has_mix = mix is not None
    n_pre = len(pre_gains)

    def body(*refs):
        x_ref = refs[0]
        pos = 1
        x1 = x_ref[...]
        if has_mix:
            mv = refs[1][...].astype(F32)
            x1 = x1 + mv * _rstd(mv) * refs[2][...]
            pos = 3
        gains = refs[pos:pos + n_pre]
        outs = refs[pos + n_pre:]
        if has_mix:
            outs[0][...] = x1
            outs = outs[1:]
        r = _rstd(x1)
        for g_ref, o_ref in zip(gains, outs):
            o_ref[...] = (x1 * r * g_ref[...]).astype(BF16)

    ins = [x] + ([mix, g_post] if has_mix else []) + list(pre_gains)
    in_specs = [_row_spec(t, d)] + ([_row_spec(t, d), _gain_spec(d)] if has_mix else []) + [_gain_spec(d)] * n_pre
    out_shape = ([jax.ShapeDtypeStruct((s, d), F32)] if has_mix else []) + [jax.ShapeDtypeStruct((s, d), BF16)] * n_pre
    out_specs = [_row_spec(t, d)] * len(out_shape)
    res = pl.pallas_call(
        body, name=name, grid=(s // t,), in_specs=in_specs, out_specs=out_specs, out_shape=out_shape,
        compiler_params=_params(("parallel",)),
    )(*ins)
    if has_mix:
        return res[0], list(res[1:])
    return x, list(res)


def _norm_res_loss(x, mix, g_post, target, name):
    s, d = x.shape
    t = _pick(s, (ROW_TILE,))

    def body(x_ref, m_ref, g_ref, t_ref, dy_ref, loss_ref):
        mv = m_ref[...].astype(F32)
        y = x_ref[...] + mv * _rstd(mv) * g_ref[...]
        err = y - t_ref[...]
        dy_ref[...] = err * (1.0 / d)

        @pl.when(pl.program_id(0) == 0)
        def _():
            loss_ref[...] = jnp.zeros_like(loss_ref)

        loss_ref[...] += jnp.sum(err * err)

    dy, acc = pl.pallas_call(
        body, name=name, grid=(s // t,),
        in_specs=[_row_spec(t, d), _row_spec(t, d), _gain_spec(d), _row_spec(t, d)],
        out_specs=[_row_spec(t, d), pl.BlockSpec((8, LANE), lambda i: (0, 0))],
        out_shape=[jax.ShapeDtypeStruct((s, d), F32), jax.ShapeDtypeStruct((8, LANE), F32)],
        compiler_params=_params(("arbitrary",)),
    )(x, mix, g_post, target)
    return dy, acc[0, 0] * (0.5 / d)


def _norm_bwd(dx_out, branches, x_in, post, name):
    s, d = dx_out.shape
    t = _pick(s, (ROW_TILE,))
    nb = len(branches)
    has_post = post is not None

    def body(*refs):
        dx_ref = refs[0]
        pos = 1
        dx = dx_ref[...]
        first = pl.program_id(0) == 0
        n_in = 1 + (1 + 2 * nb if nb else 0) + (2 if has_post else 0)
        outs = refs[n_in:]
        opos = 0
        if nb:
            xv = refs[pos][...]
            pos += 1
            r = _rstd(xv)
            dx_o = outs[0]
            opos = 1
            for _ in range(nb):
                dxn = refs[pos][...].astype(F32)
                g = refs[pos + 1][...]
                pos += 2
                dv, dg_rows = _rms_bwd(dxn, xv, g, r)
                dx = dx + dv
                dg_ref = outs[opos]
                opos += 1

                @pl.when(first)
                def _(dg_ref=dg_ref):
                    dg_ref[...] = jnp.zeros_like(dg_ref)

                dg_ref[...] += jnp.sum(dg_rows, axis=0, keepdims=True)
            dx_o[...] = dx
        if has_post:
            mv = refs[pos][...].astype(F32)
            g = refs[pos + 1][...]
            dm, dg_rows = _rms_bwd(dx, mv, g, _rstd(mv))
            outs[opos][...] = dm.astype(BF16)
            dg_ref = outs[opos + 1]

            @pl.when(first)
            def _():
                dg_ref[...] = jnp.zeros_like(dg_ref)

            dg_ref[...] += jnp.sum(dg_rows, axis=0, keepdims=True)

    ins, in_specs = [dx_out], [_row_spec(t, d)]
    out_shape, out_specs = [], []
    if nb:
        ins.append(x_in)
        in_specs.append(_row_spec(t, d))
        out_shape.append(jax.ShapeDtypeStruct((s, d), F32))
        out_specs.append(_row_spec(t, d))
        for dxn, g in branches:
            ins += [dxn, g]
            in_specs += [_row_spec(t, d), _gain_spec(d)]
            out_shape.append(jax.ShapeDtypeStruct((1, d), F32))
            out_specs.append(_gain_spec(d))
    if has_post:
        ins += [post[0], post[1]]
        in_specs += [_row_spec(t, d), _gain_spec(d)]
        out_shape += [jax.ShapeDtypeStruct((s, d), BF16), jax.ShapeDtypeStruct((1, d), F32)]
        out_specs += [_row_spec(t, d), _gain_spec(d)]
    res = pl.pallas_call(
        body, name=name, grid=(s // t,), in_specs=in_specs, out_specs=out_specs, out_shape=out_shape,
        compiler_params=_params(("arbitrary",)),
    )(*ins)
    res = list(res)
    dx_in = res.pop(0) if nb else dx_out
    dgs = [res.pop(0) for _ in range(nb)]
    dm, dg_post = (res[0], res[1]) if has_post else (None, None)
    return dx_in, dgs, dm, dg_post


HALO = 16


def _shift_down(u, prev, k):
    rows = lax.broadcasted_iota(jnp.int32, u.shape, 0)
    out = pltpu.roll(u, k, 0)
    for i in range(k):
        out = jnp.where(rows == i, prev[HALO - k + i:HALO - k + i + 1, :], out)
    return out


def _shift_up(u, nxt, k):
    n = u.shape[0]
    rows = lax.broadcasted_iota(jnp.int32, u.shape, 0)
    out = pltpu.roll(u, n - k, 0)
    for i in range(k):
        out = jnp.where(rows == n - k + i, nxt[i:i + 1, :], out)
    return out


def _conv_gate_fwd(p, cw, name):
    s, d3 = p.shape
    d = d3 // 3
    t = _pick(s, (ROW_TILE,))
    hb = t // HALO

    def body(p_ref, prev_ref, w_ref, z_ref):
        i = pl.program_id(0)
        pv = p_ref[...].astype(F32)
        b, u = pv[:, :d], pv[:, d:2 * d] * pv[:, 2 * d:]
        ph = prev_ref[...].astype(F32)
        up = jnp.where(i > 0, ph[:, d:2 * d] * ph[:, 2 * d:], 0.0)
        w = w_ref[...]
        y = w[0:1, :] * _shift_down(u, up, 2) + w[1:2, :] * _shift_down(u, up, 1) + w[2:3, :] * u
        z_ref[...] = (b * y).astype(BF16)

    return pl.pallas_call(
        body, name=name, grid=(s // t,),
        in_specs=[_row_spec(t, d3),
                  pl.BlockSpec((HALO, d3), lambda i: (jnp.maximum(i * hb - 1, 0), 0)),
                  pl.BlockSpec((3, d), lambda i: (0, 0))],
        out_specs=_row_spec(t, d),
        out_shape=jax.ShapeDtypeStruct((s, d), BF16),
        compiler_params=_params(("parallel",)),
    )(p, p, cw)


def _conv_gate_bwd(p, dz, cw, name):
    s, d3 = p.shape
    d = d3 // 3
    t = _pick(s, (ROW_TILE,))
    hb = t // HALO
    nt = s // t
    last_halo = s // HALO - 1

    def body(p_ref, prev_ref, next_ref, dz_ref, dznext_ref, w_ref, dp_ref, dw_ref):
        i = pl.program_id(0)
        pv = p_ref[...].astype(F32)
        b, c, h = pv[:, :d], pv[:, d:2 * d], pv[:, 2 * d:]
        u = c * h
        ph = prev_ref[...].astype(F32)
        up = jnp.where(i > 0, ph[:, d:2 * d] * ph[:, 2 * d:], 0.0)
        w = w_ref[...]
        u1, u2 = _shift_down(u, up, 1), _shift_down(u, up, 2)
        y = w[0:1, :] * u2 + w[1:2, :] * u1 + w[2:3, :] * u
        dz = dz_ref[...].astype(F32)
        dy = dz * b
        dyn = jnp.where(i < nt - 1, dznext_ref[...].astype(F32) * next_ref[...].astype(F32)[:, :d], 0.0)
        du = w[2:3, :] * dy + w[1:2, :] * _shift_up(dy, dyn, 1) + w[0:1, :] * _shift_up(dy, dyn, 2)
        dp_ref[:, :d] = (dz * y).astype(BF16)
        dp_ref[:, d:2 * d] = (du * h).astype(BF16)
        dp_ref[:, 2 * d:] = (du * c).astype(BF16)

        @pl.when(i == 0)
        def _():
            dw_ref[...] = jnp.zeros_like(dw_ref)

        dw_ref[0:1, :] += jnp.sum(dy * u2, axis=0, keepdims=True)
        dw_ref[1:2, :] += jnp.sum(dy * u1, axis=0, keepdims=True)
        dw_ref[2:3, :] += jnp.sum(dy * u, axis=0, keepdims=True)

    return pl.pallas_call(
        body, name=name, grid=(nt,),
        in_specs=[_row_spec(t, d3),
                  pl.BlockSpec((HALO, d3), lambda i: (jnp.maximum(i * hb - 1, 0), 0)),
                  pl.BlockSpec((HALO, d3), lambda i: (jnp.minimum((i + 1) * hb, last_halo), 0)),
                  _row_spec(t, d),
                  pl.BlockSpec((HALO, d), lambda i: (jnp.minimum((i + 1) * hb, last_halo), 0)),
                  pl.BlockSpec((3, d), lambda i: (0, 0))],
        out_specs=[_row_spec(t, d3), pl.BlockSpec((3, d), lambda i: (0, 0))],
        out_shape=[jax.ShapeDtypeStruct((s, d3), BF16), jax.ShapeDtypeStruct((3, d), F32)],
        compiler_params=_params(("arbitrary",)),
    )(p, p, p, dz, dz, cw)


def _swiglu_fwd(f, name):
    s, f2 = f.shape
    ff = f2 // 2
    t = _pick(s, (ROW_TILE,))

    def body(f_ref, a_ref):
        gate = f_ref[:, :ff].astype(F32)
        up = f_ref[:, ff:].astype(F32)
        a_ref[...] = (gate * jax.nn.sigmoid(gate) * up).astype(BF16)

    return pl.pallas_call(
        body, name=name, grid=(s // t,),
        in_specs=[_row_spec(t, f2)], out_specs=_row_spec(t, ff),
        out_shape=jax.ShapeDtypeStruct((s, ff), BF16),
        compiler_params=_params(("parallel",)),
    )(f)


def _swiglu_bwd(f, da, name):
    s, f2 = f.shape
    ff = f2 // 2
    t = _pick(s, (ROW_TILE,))

    def body(f_ref, da_ref, df_ref):
        gate = f_ref[:, :ff].astype(F32)
        up = f_ref[:, ff:].astype(F32)
        dav = da_ref[...].astype(F32)
        sg = jax.nn.sigmoid(gate)
        silu = gate * sg
        df_ref[:, :ff] = (dav * up * (sg + silu * (1.0 - sg))).astype(BF16)
        df_ref[:, ff:] = (dav * silu).astype(BF16)

    return pl.pallas_call(
        body, name=name, grid=(s // t,),
        in_specs=[_row_spec(t, f2), _row_spec(t, ff)], out_specs=_row_spec(t, f2),
        out_shape=jax.ShapeDtypeStruct((s, f2), BF16),
        compiler_params=_params(("parallel",)),
    )(f, da)


def _alibi_slopes(n_heads):
    h = np.arange(n_heads, dtype=np.float32) + 1.0
    return [float(v) for v in np.power(2.0, -8.0 * h / n_heads).astype(np.float32)]


def _put_lane(full, lane_ids, h, col):
    return jnp.where(lane_ids == h, col, full)


def _attn_fwd_branch(q, kv, dil, state, last, name):
    s, d = q.shape
    nh = d // HEAD_DIM
    slopes = _alibi_slopes(nh)
    L = ATT_BLOCK
    rows = s // dil
    nblk = rows // L
    first = state is None

    def body(*refs):
        q_ref, kc_ref, kp_ref, vc_ref, vp_ref = refs[:5]
        pos = 5
        if not first:
            m_in, l_in, acc_in = refs[5:8]
            pos = 8
        outs = refs[pos:]
        n = pl.program_id(1)
        u = lax.broadcasted_iota(jnp.int32, (L, 2 * L), 0)
        kk = lax.broadcasted_iota(jnp.int32, (L, 2 * L), 1)
        delta = u + L - kk
        valid = (delta >= 0) & (delta <= L) & ((kk >= L) | (n > 0))
        base = (delta * dil).astype(F32)
        lane_ids = lax.broadcasted_iota(jnp.int32, (L, LANE), 1)
        full_a = jnp.zeros((L, LANE), F32)
        full_b = jnp.zeros((L, LANE), F32)
        for h in range(nh):
            sl = slice(h * HEAD_DIM, (h + 1) * HEAD_DIM)
            kcat = jnp.concatenate([kp_ref[:, sl], kc_ref[:, sl]], axis=0)
            vcat = jnp.concatenate([vp_ref[:, sl], vc_ref[:, sl]], axis=0)
            sc = lax.dot_general(q_ref[:, sl], kcat, (((1,), (1,)), ((), ())), preferred_element_type=F32)
            sc = jnp.where(valid, sc + (-slopes[h]) * base, NEG_BIG)
            m_new = jnp.max(sc, axis=1, keepdims=True)
            if not first:
                m_prev = m_in[:, h:h + 1]
                m_new = jnp.maximum(m_prev, m_new)
            p = jnp.exp(sc - m_new)
            l_new = jnp.sum(p, axis=1, keepdims=True)
            acc = jnp.dot(p.astype(BF16), vcat, preferred_element_type=F32)
            if not first:
                alpha = jnp.exp(m_prev - m_new)
                l_new = l_in[:, h:h + 1] * alpha + l_new
                acc = acc_in[:, sl] * alpha + acc
            if last:
                outs[0][:, sl] = (acc / l_new).astype(BF16)
                full_a = _put_lane(full_a, lane_ids, h, m_new + jnp.log(l_new))
            else:
                outs[2][:, sl] = acc
                full_a = _put_lane(full_a, lane_ids, h, m_new)
                full_b = _put_lane(full_b, lane_ids, h, l_new)
        if last:
            outs[1][...] = full_a
        else:
            outs[0][...] = full_a
            outs[1][...] = full_b

    blk_d = lambda r, n: (n, r)
    prev = lambda n: jnp.maximum(n - 1, 0)
    ins = [q.reshape(rows, dil * d)] + [kv.reshape(rows, dil * 2 * d)] * 4
    in_specs = [pl.BlockSpec((L, d), blk_d),
                pl.BlockSpec((L, d), lambda r, n: (n, 2 * r)),
                pl.BlockSpec((L, d), lambda r, n: (prev(n), 2 * r)),
                pl.BlockSpec((L, d), lambda r, n: (n, 2 * r + 1)),
                pl.BlockSpec((L, d), lambda r, n: (prev(n), 2 * r + 1))]
    stat_spec = pl.BlockSpec((L, LANE), blk_d)
    if not first:
        m, l, acc = state
        ins += [m.reshape(rows, dil * LANE), l.reshape(rows, dil * LANE), acc.reshape(rows, dil * d)]
        in_specs += [stat_spec, stat_spec, pl.BlockSpec((L, d), blk_d)]
    if last:
        out_shape = [jax.ShapeDtypeStruct((rows, dil * d), BF16), jax.ShapeDtypeStruct((rows, dil * LANE), F32)]
        out_specs = [pl.BlockSpec((L, d), blk_d), stat_spec]
    else:
        out_shape = [jax.ShapeDtypeStruct((rows, dil * LANE), F32)] * 2 + [jax.ShapeDtypeStruct((rows, dil * d), F32)]
        out_specs = [stat_spec, stat_spec, pl.BlockSpec((L, d), blk_d)]
    res = pl.pallas_call(
        body, name=name, grid=(dil, nblk), in_specs=in_specs, out_specs=out_specs, out_shape=out_shape,
        compiler_params=_params(("parallel", "parallel")),
    )(*ins)
    if last:
        return res[0].reshape(s, d), res[1].reshape(s, LANE)
    return res[0].reshape(s, LANE), res[1].reshape(s, LANE), res[2].reshape(s, d)


def _attention_fwd(q, kv, tag):
    state = None
    for i, dil in enumerate(DILATIONS):
        last = i == len(DILATIONS) - 1
        state = _attn_fwd_branch(q, kv, dil, state, last, f"attn_fwd_{tag}_d{dil}")
    return state


def _attn_bwd_branch(q, kv, o, do, lse, dil, dq_in, dkv_in, dq_dtype, name):
    s, d = q.shape
    nh = d // HEAD_DIM
    slopes = _alibi_slopes(nh)
    L = ATT_BLOCK
    rows = s // dil
    nblk = rows // L
    has_dq = dq_in is not None
    has_dkv = dkv_in is not None

    def body(*refs):
        kc_ref, vc_ref, q0, q1, do0, do1, o0, o1, lse0, lse1 = refs[:10]
        pos = 10
        if has_dq:
            dq_in_ref = refs[pos]
            pos += 1
        if has_dkv:
            dk_in_ref, dv_in_ref = refs[pos:pos + 2]
            pos += 2
        dq_ref, dk_ref, dv_ref, carry = refs[pos:pos + 4]
        j = pl.program_id(1)

        @pl.when(j == 0)
        def _():
            carry[...] = jnp.zeros_like(carry)

        u2 = lax.broadcasted_iota(jnp.int32, (2 * L, L), 0)
        kk = lax.broadcasted_iota(jnp.int32, (2 * L, L), 1)
        delta = u2 - kk
        valid = (delta >= 0) & (delta <= L) & ((u2 < L) | (j < nblk - 1))
        base = (delta * dil).astype(F32)
        nt_dims = (((1,), (1,)), ((), ()))
        tn_dims = (((0,), (0,)), ((), ()))
        for h in range(nh):
            sl = slice(h * HEAD_DIM, (h + 1) * HEAD_DIM)
            kh, vh = kc_ref[:, sl], vc_ref[:, sl]
            q2 = jnp.concatenate([q0[:, sl], q1[:, sl]], axis=0)
            do2 = jnp.concatenate([do0[:, sl], do1[:, sl]], axis=0)
            o2 = jnp.concatenate([o0[:, sl], o1[:, sl]], axis=0)
            lse2 = jnp.concatenate([lse0[:, h:h + 1], lse1[:, h:h + 1]], axis=0)
            dsum = jnp.sum(do2.astype(F32) * o2.astype(F32), axis=1, keepdims=True)
            sc = lax.dot_general(q2, kh, nt_dims, preferred_element_type=F32)
            p = jnp.exp(jnp.where(valid, sc + (-slopes[h]) * base - lse2, NEG_BIG))
            dp = lax.dot_general(do2, vh, nt_dims, preferred_element_type=F32)
            ds = (p * (dp - dsum)).astype(BF16)
            dq2 = jnp.dot(ds, kh, preferred_element_type=F32)
            dk = lax.dot_general(ds, q2, tn_dims, preferred_element_type=F32)
            dv = lax.dot_general(p.astype(BF16), do2, tn_dims, preferred_element_type=F32)
            dq = dq2[:L] + carry[:, sl]
            if has_dq:
                dq = dq + dq_in_ref[:, sl]
            if has_dkv:
                dk = dk + dk_in_ref[:, sl]
                dv = dv + dv_in_ref[:, sl]
            carry[:, sl] = dq2[L:]
            dq_ref[:, sl] = dq.astype(dq_dtype)
            dk_ref[:, sl] = dk
            dv_ref[:, sl] = dv

    blk_d = lambda r, j: (j, r)
    nxt = lambda r, j: (jnp.minimum(j + 1, nblk - 1), r)
    wide = lambda a, w: a.reshape(rows, dil * w)
    kvw, qw, dow, ow, lsew = wide(kv, 2 * d), wide(q, d), wide(do, d), wide(o, d), wide(lse, LANE)
    ins = [kvw, kvw, qw, qw, dow, dow, ow, ow, lsew, lsew]
    in_specs = [pl.BlockSpec((L, d), lambda r, j: (j, 2 * r)),
                pl.BlockSpec((L, d), lambda r, j: (j, 2 * r + 1)),
                pl.BlockSpec((L, d), blk_d), pl.BlockSpec((L, d), nxt),
                pl.BlockSpec((L, d), blk_d), pl.BlockSpec((L, d), nxt),
                pl.BlockSpec((L, d), blk_d), pl.BlockSpec((L, d), nxt),
                pl.BlockSpec((L, LANE), blk_d), pl.BlockSpec((L, LANE), nxt)]
    if has_dq:
        ins.append(wide(dq_in, d))
        in_specs.append(pl.BlockSpec((L, d), blk_d))
    if has_dkv:
        ins += [wide(dkv_in[0], d), wide(dkv_in[1], d)]
        in_specs += [pl.BlockSpec((L, d), blk_d)] * 2
    out_shape = [jax.ShapeDtypeStruct((rows, dil * d), dq_dtype)] + [jax.ShapeDtypeStruct((rows, dil * d), F32)] * 2
    res = pl.pallas_call(
        body, name=name, grid=(dil, nblk), in_specs=in_specs,
        out_specs=[pl.BlockSpec((L, d), blk_d)] * 3, out_shape=out_shape,
        scratch_shapes=[pltpu.VMEM((L, d), F32)],
        compiler_params=_params(("arbitrary", "arbitrary")),
    )(*ins)
    return tuple(r.reshape(s, d) for r in res)


def _attention_bwd(q, kv, o, do, lse, dkv, tag):
    dq = None
    for i, dil in enumerate(DILATIONS):
        last = i == len(DILATIONS) - 1
        dq, dk, dv = _attn_bwd_branch(q, kv, o, do, lse, dil, dq, dkv, BF16 if last else F32,
                                      f"attn_bwd_{tag}_d{dil}")
        dkv = (dk, dv)
    return dq, dkv


def _coords():
    return lax.axis_index("x"), lax.axis_index("y"), lax.axis_index("c")


def _chip_peers(x, y):
    return [(1 - x, y), (x, 1 - y), (1 - x, 1 - y)]


def _block_of(ref, axis, blk, size):
    start = pl.multiple_of(blk * size, size)
    if axis == 1:
        return ref.at[:, pl.ds(start, size)]
    return ref.at[pl.ds(start, size), :]


ANY = pl.BlockSpec(memory_space=pl.ANY)


def _gather_weights(shards, axes, name):
    n = len(shards)
    sizes = [sh.shape[ax] for sh, ax in zip(shards, axes)]

    def body(*refs):
        ins, outs = refs[:n], refs[n:2 * n]
        send_sems, recv_sems, local_sems = refs[2 * n:]
        x, y, c = _coords()
        my_blk = 2 * x + y
        peers = _chip_peers(x, y)
        local, sends, recvs = [], [], []
        for t in range(n):
            mine = _block_of(outs[t], axes[t], my_blk, sizes[t])
            cp = pltpu.make_async_copy(ins[t], mine, local_sems.at[t])
            cp.start()
            local.append(cp)
            for j, (px, py) in enumerate(peers):
                k = 3 * t + j
                snd = pltpu.make_async_remote_copy(
                    src_ref=ins[t], dst_ref=mine, send_sem=send_sems.at[k], recv_sem=recv_sems.at[k],
                    device_id=(px, py, c), device_id_type=MESH)
                snd.start()
                sends.append(snd)
                theirs = _block_of(outs[t], axes[t], 2 * px + py, sizes[t])
                recvs.append(pltpu.make_async_remote_copy(
                    src_ref=ins[t], dst_ref=theirs, send_sem=send_sems.at[k], recv_sem=recv_sems.at[k],
                    device_id=(px, py, c), device_id_type=MESH))
        for cp in recvs:
            cp.wait_recv()
        for cp in sends:
            cp.wait_send()
        for cp in local:
            cp.wait()

    out_shape = []
    for sh, ax in zip(shards, axes):
        full = list(sh.shape)
        full[ax] *= 4
        out_shape.append(jax.ShapeDtypeStruct(tuple(full), sh.dtype))
    return pl.pallas_call(
        body, name=name, in_specs=[ANY] * n, out_specs=[ANY] * n, out_shape=out_shape,
        scratch_shapes=[pltpu.SemaphoreType.DMA((3 * n,)), pltpu.SemaphoreType.DMA((3 * n,)),
                        pltpu.SemaphoreType.DMA((n,))],
        compiler_params=pltpu.CompilerParams(has_side_effects=True),
    )(*shards)


def _scatter_grads(groups, name):
    flat, meta = [], []
    for gi, (layers, ax) in enumerate(groups):
        for li, g in enumerate(layers):
            flat.append(g)
            meta.append((gi, li, ax, g.shape[ax] // 4))
    n, ng = len(flat), len(groups)

    def body(*refs):
        ins = refs[:n]
        mine_refs, sib_refs = refs[n:n + ng], refs[n + ng:n + 2 * ng]
        send_sems, recv_sems, local_sems, pair_send, pair_recv = refs[n + 2 * ng:]
        x, y, c = _coords()
        my_blk = 2 * x + y
        peers = _chip_peers(x, y)
        local, sends, recvs = [], [], [[] for _ in range(ng)]
        for t, (gi, li, ax, size) in enumerate(meta):
            cp = pltpu.make_async_copy(_block_of(ins[t], ax, my_blk, size), mine_refs[gi].at[0, li], local_sems.at[t])
            cp.start()
            local.append((gi, cp))
            for j, (px, py) in enumerate(peers):
                k = 3 * t + j
                snd = pltpu.make_async_remote_copy(
                    src_ref=_block_of(ins[t], ax, 2 * px + py, size), dst_ref=mine_refs[gi].at[1 + j, li],
                    send_sem=send_sems.at[k], recv_sem=recv_sems.at[k], device_id=(px, py, c), device_id_type=MESH)
                snd.start()
                sends.append(snd)
                recvs[gi].append(snd)
        pair = []
        for gi in range(ng):
            for cp in recvs[gi]:
                cp.wait_recv()
            for g2, cp in local:
                if g2 == gi:
                    cp.wait()
            fwd = pltpu.make_async_remote_copy(
                src_ref=mine_refs[gi], dst_ref=sib_refs[gi], send_sem=pair_send.at[gi], recv_sem=pair_recv.at[gi],
                device_id=(x, y, 1 - c), device_id_type=MESH)
            fwd.start()
            pair.append(fwd)
        for cp in pair:
            cp.wait_recv()
        for cp in sends + pair:
            cp.wait_send()

    stacks = []
    for layers, ax in groups:
        blk = list(layers[0].shape)
        blk[ax] //= 4
        stacks.append(jax.ShapeDtypeStruct((4, len(layers), *blk), layers[0].dtype))
    res = pl.pallas_call(
        body, name=name, in_specs=[ANY] * n, out_specs=[ANY] * (2 * ng), out_shape=stacks + stacks,
        scratch_shapes=[pltpu.SemaphoreType.DMA((3 * n,)), pltpu.SemaphoreType.DMA((3 * n,)),
                        pltpu.SemaphoreType.DMA((n,)), pltpu.SemaphoreType.DMA((ng,)),
                        pltpu.SemaphoreType.DMA((ng,))],
        compiler_params=pltpu.CompilerParams(has_side_effects=True),
    )(*flat)
    return [(res[gi], res[ng + gi]) for gi in range(ng)]


def _allreduce_small(v, name):
    r, cdim = v.shape

    def body(v_ref, out_ref, buf, send_sems, recv_sems):
        x, y, c = _coords()
        me = 4 * x + 2 * y + c
        buf[0] = v_ref[...]
        sends = []
        for k in range(1, 8):
            peer = (x if not (k & 4) else 1 - x, y if not (k & 2) else 1 - y, c if not (k & 1) else 1 - c)
            cp = pltpu.make_async_remote_copy(
                src_ref=v_ref, dst_ref=buf.at[k], send_sem=send_sems.at[k - 1], recv_sem=recv_sems.at[k - 1],
                device_id=peer, device_id_type=MESH)
            cp.start()
            sends.append(cp)
        for cp in sends:
            cp.wait_recv()
        total = buf[me]
        for src in range(1, 8):
            total = total + buf[jnp.bitwise_xor(me, src)]
        out_ref[...] = total
        for cp in sends:
            cp.wait_send()

    return pl.pallas_call(
        body, name=name,
        in_specs=[pl.BlockSpec(memory_space=pltpu.VMEM)], out_specs=pl.BlockSpec(memory_space=pltpu.VMEM),
        out_shape=jax.ShapeDtypeStruct((r, cdim), F32),
        scratch_shapes=[pltpu.VMEM((8, r, cdim), F32), pltpu.SemaphoreType.DMA((7,)), pltpu.SemaphoreType.DMA((7,))],
        compiler_params=pltpu.CompilerParams(has_side_effects=True),
    )(v)


def _adamw_math(w, g, m, v):
    m = ADAM_B1 * m + (1.0 - ADAM_B1) * g
    v = ADAM_B2 * v + (1.0 - ADAM_B2) * jnp.square(g)
    m_hat = m / (1.0 - ADAM_B1 ** ADAM_STEP)
    v_hat = v / (1.0 - ADAM_B2 ** ADAM_STEP)
    delta = -ADAM_LR * (m_hat / (jnp.sqrt(v_hat) + ADAM_EPS) + ADAM_WD * w)
    return delta, m, v


def _adamw(w, m, v, grads, name):
    r, cdim = w.shape
    t = _pick(r, (256, 128, 64, 32, 16, 8))
    paired = isinstance(grads, tuple)

    def body(*refs):
        w_ref, m_ref, v_ref = refs[:3]
        if paired:
            a_ref, b_ref = refs[3:5]
            outs = refs[5:]
            sa = a_ref[0].astype(F32)
            sb = b_ref[0].astype(F32)
            for k in range(1, 4):
                sa = sa + a_ref[k].astype(F32)
                sb = sb + b_ref[k].astype(F32)
            g = sa + sb
        else:
            g = refs[3][...]
            outs = refs[4:]
        delta, m_new, v_new = _adamw_math(w_ref[...], g, m_ref[...], v_ref[...])
        outs[0][...] = g
        outs[1][...] = delta
        outs[2][...] = m_new
        outs[3][...] = v_new

    spec = pl.BlockSpec((t, cdim), lambda i: (i, 0))
    stack_spec = pl.BlockSpec((4, t, cdim), lambda i: (0, i, 0))
    ins = [w, m, v] + (list(grads) if paired else [grads])
    in_specs = [spec] * 3 + ([stack_spec] * 2 if paired else [spec])
    return pl.pallas_call(
        body, name=name, grid=(r // t,), in_specs=in_specs, out_specs=[spec] * 4,
        out_shape=[jax.ShapeDtypeStruct((r, cdim), F32)] * 4,
        compiler_params=_params(("parallel",)),
    )(*ins)


def _local_step(x, target, gains, conv_ws, kv_gain, w):
    depth = len(gains)
    n_a = len(w["conv_in"])
    saved = []
    kv = kvn = None
    _, (xn,) = _norm_res_fwd(x, None, None, [gains[0][0]], "norm_first")
    h = x
    for l in range(depth):
        g = gains[l]
        sv = {"x_in": h, "xn": xn}
        if l < n_a:
            p = _mm(xn, w["conv_in"][l], "nn", BF16, f"conv_in_fwd_{l}")
            z = _conv_gate_fwd(p, conv_ws[l], f"conv_gate_fwd_{l}")
            mix = _mm(z, w["conv_out"][l], "nn", BF16, f"conv_out_fwd_{l}")
            sv.update(p=p, z=z)
        else:
            j = l - n_a
            q = _mm(xn, w["q"][j], "nn", BF16, f"q_fwd_{j}", scale=HEAD_DIM ** -0.5)
            o, lse = _attention_fwd(q, kv, j)
            mix = _mm(o, w["o"][j], "nn", BF16, f"o_fwd_{j}")
            sv.update(q=q, o=o, lse=lse)
        x1, (xn2,) = _norm_res_fwd(h, mix, g[1], [g[2]], f"norm_mid_{l}")
        f = _mm(xn2, w["ffn_in"][l], "nn", BF16, f"ffn_in_fwd_{l}")
        a = _swiglu_fwd(f, f"swiglu_fwd_{l}")
        ff = _mm(a, w["ffn_out"][l], "nn", BF16, f"ffn_out_fwd_{l}")
        sv.update(mix=mix, x1=x1, xn2=xn2, f=f, a=a, ff=ff)
        saved.append(sv)
        if l == depth - 1:
            dx, loss = _norm_res_loss(x1, ff, g[3], target, "norm_loss")
        else:
            pre = [gains[l + 1][0]] + ([kv_gain] if l == n_a - 1 else [])
            h, outs = _norm_res_fwd(x1, ff, g[3], pre, f"norm_end_{l}")
            xn = outs[0]
            if l == n_a - 1:
                kvn = outs[1]
                kv = _mm(kvn, w["kv"][0], "nn", BF16, "kv_fwd")
    grads = {k: [None] * len(v) for k, v in w.items()}
    d_gains = [[None] * 4 for _ in range(depth)]
    d_conv = [None] * n_a
    d_kv_gain = None
    dkv = None
    _, _, dff, d_gains[depth - 1][3] = _norm_bwd(dx, [], None, (saved[-1]["ff"], gains[-1][3]), "norm_loss_bwd")
    for l in reversed(range(depth)):
        sv, g = saved[l], gains[l]
        da = _mm(dff, w["ffn_out"][l], "nt", BF16, f"ffn_out_dx_{l}")
        grads["ffn_out"][l] = _mm(sv["a"], dff, "tn", BF16, f"ffn_out_dw_{l}")
        df = _swiglu_bwd(sv["f"], da, f"swiglu_bwd_{l}")
        dxn2 = _mm(df, w["ffn_in"][l], "nt", BF16, f"ffn_in_dx_{l}")
        grads["ffn_in"][l] = _mm(sv["xn2"], df, "tn", BF16, f"ffn_in_dw_{l}")
        dx, (d_gains[l][2],), dmix, d_gains[l][1] = _norm_bwd(
            dx, [(dxn2, g[2])], sv["x1"], (sv["mix"], g[1]), f"norm_mid_bwd_{l}")
        if l < n_a:
            dz = _mm(dmix, w["conv_out"][l], "nt", BF16, f"conv_out_dx_{l}")
            grads["conv_out"][l] = _mm(sv["z"], dmix, "tn", BF16, f"conv_out_dw_{l}")
            dp, d_conv[l] = _conv_gate_bwd(sv["p"], dz, conv_ws[l], f"conv_gate_bwd_{l}")
            dxn = _mm(dp, w["conv_in"][l], "nt", BF16, f"conv_in_dx_{l}")
            grads["conv_in"][l] = _mm(sv["xn"], dp, "tn", BF16, f"conv_in_dw_{l}")
        else:
            j = l - n_a
            do = _mm(dmix, w["o"][j], "nt", BF16, f"o_dx_{j}")
            grads["o"][j] = _mm(sv["o"], dmix, "tn", BF16, f"o_dw_{j}")
            dq, dkv = _attention_bwd(sv["q"], kv, sv["o"], do, sv["lse"], dkv, j)
            scale = HEAD_DIM ** -0.5
            dxn = _mm(dq, w["q"][j], "nt", BF16, f"q_dx_{j}", scale=scale)
            grads["q"][j] = _mm(sv["xn"], dq, "tn", BF16, f"q_dw_{j}", scale=scale)
        branches = [(dxn, g[0])]
        if l == n_a:
            dkv_cat = jnp.concatenate([dkv[0], dkv[1]], axis=1).astype(BF16)
            dkvn = _mm(dkv_cat, w["kv"][0], "nt", BF16, "kv_dx")
            grads["kv"][0] = _mm(kvn, dkv_cat, "tn", BF16, "kv_dw")
            branches.append((dkvn, kv_gain))
        post = (saved[l - 1]["ff"], gains[l - 1][3]) if l > 0 else None
        dx, dgs, dff, dg_post = _norm_bwd(dx, branches, sv["x_in"], post, f"norm_end_bwd_{l}")
        d_gains[l][0] = dgs[0]
        if l == n_a:
            d_kv_gain = dgs[1]
        if l > 0:
            d_gains[l - 1][3] = dg_post
    return loss, dx, grads, d_gains, d_conv, d_kv_gain


BIG = (
    ("conv_in", 1), ("conv_out", 0), ("kv", 1), ("q", 0), ("o", 0), ("ffn_in", 1), ("ffn_out", 0))


def kernel(x, norm_g, conv_in_w, conv_w, conv_out_w, kv_norm_g, kv_w, q_w, o_w, ffn_in_w, ffn_out_w, loss_target, m_norm_g, m_conv_in_w, m_conv_w, m_conv_out_w, m_kv_norm_g, m_kv_w, m_q_w, m_o_w, m_ffn_in_w, m_ffn_out_w, v_norm_g, v_conv_in_w, v_conv_w, v_conv_out_w, v_kv_norm_g, v_kv_w, v_q_w, v_o_w, v_ffn_in_w, v_ffn_out_w):
    depth, _, dq = norm_g.shape
    d = 4 * dq
    n_a = conv_w.shape[0]
    big_w = {"conv_in": conv_in_w, "conv_out": conv_out_w, "kv": kv_w[None], "q": q_w, "o": o_w,
             "ffn_in": ffn_in_w, "ffn_out": ffn_out_w}
    big_m = {"conv_in": m_conv_in_w, "conv_out": m_conv_out_w, "kv": m_kv_w[None], "q": m_q_w, "o": m_o_w,
             "ffn_in": m_ffn_in_w, "ffn_out": m_ffn_out_w}
    big_v = {"conv_in": v_conv_in_w, "conv_out": v_conv_out_w, "kv": v_kv_w[None], "q": v_q_w, "o": v_o_w,
             "ffn_in": v_ffn_in_w, "ffn_out": v_ffn_out_w}

    n_gain, n_tap = depth * 4, n_a * conv_w.shape[1]
    small_rows = -(-(n_gain + n_tap + 1) // 8) * 8
    pad_rows = small_rows - n_gain - n_tap

    def pack_small(gains, taps):
        return jnp.concatenate([gains.reshape(n_gain, dq), taps.reshape(n_tap, dq), jnp.zeros((pad_rows, dq), F32)])

    shards, axes, index = [], [], {}
    for name, ax in BIG:
        wb = big_w[name].astype(BF16)
        index[name] = (len(shards), wb.shape[0])
        for l in range(wb.shape[0]):
            shards.append(wb[l])
            axes.append(ax)
    shards.append(pack_small(norm_g, conv_w))
    axes.append(1)
    full = _gather_weights(shards, axes, "gather_weights")
    w = {name: [full[s0 + l] for l in range(cnt)] for name, (s0, cnt) in index.items()}
    small = full[-1]
    gains = [[small[4 * l + i][None] for i in range(4)] for l in range(depth)]
    conv_ws = [small[n_gain + 3 * l:n_gain + 3 * l + 3] for l in range(n_a)]
    kv_gain = kv_norm_g[None]

    loss, dx, grads, d_gains, d_conv, d_kv_gain = _local_step(x[0], loss_target[0], gains, conv_ws, kv_gain, w)
    loss = lax.psum(loss, ("x", "y", "c"))

    small_g = jnp.concatenate([dg for row in d_gains for dg in row] + list(d_conv) + [d_kv_gain]
                              + [jnp.zeros((pad_rows - 1, d), F32)])
    small_g = _allreduce_small(small_g, "allreduce_small")
    blk = 2 * lax.axis_index("x") + lax.axis_index("y")
    mine_small = lax.dynamic_slice_in_dim(small_g, blk * dq, dq, axis=1)
    kv_rows = d // dq

    def pack_opt(gains_like, taps_like, kv_like):
        rows = jnp.concatenate([gains_like.reshape(n_gain, dq), taps_like.reshape(n_tap, dq), kv_like.reshape(kv_rows, dq)])
        extra = -rows.shape[0] % 8
        return jnp.concatenate([rows, jnp.zeros((extra, dq), F32)]) if extra else rows

    sw = pack_opt(norm_g, conv_w, kv_norm_g)
    sm = pack_opt(m_norm_g, m_conv_w, m_kv_norm_g)
    sv = pack_opt(v_norm_g, v_conv_w, v_kv_norm_g)
    sg = pack_opt(mine_small[:n_gain], mine_small[n_gain:n_gain + n_tap], small_g[n_gain + n_tap])
    s_out = _adamw(sw, sm, sv, sg, "adamw_small")

    def unpack(a):
        return (a[:n_gain].reshape(depth, 4, dq), a[n_gain:n_gain + n_tap].reshape(n_a, -1, dq),
                a[n_gain + n_tap:n_gain + n_tap + kv_rows].reshape(d))

    small_out = [unpack(a) for a in s_out]

    stacks = _scatter_grads([(grads[name], ax) for name, ax in BIG], "scatter_grads")
    big_out = {}
    for (name, ax), (mine, sib) in zip(BIG, stacks):
        shp = big_w[name].shape
        rows, cols = shp[0] * shp[1], shp[2]
        flat = lambda a: a.reshape(rows, cols)
        res = _adamw(flat(big_w[name]), flat(big_m[name]), flat(big_v[name]),
                     (mine.reshape(4, rows, cols), sib.reshape(4, rows, cols)), f"adamw_{name}")
        out_shp = shp[1:] if name == "kv" else shp
        big_out[name] = [a.reshape(out_shp) for a in res]

    def leaves(i):
        ng, cw_, kg = small_out[i]
        return [ng, big_out["conv_in"][i], cw_, big_out["conv_out"][i], kg, big_out["kv"][i], big_out["q"][i],
                big_out["o"][i], big_out["ffn_in"][i], big_out["ffn_out"][i]]

    return (loss, dx[None], *leaves(0), *leaves(1), *leaves(2), *leaves(3))
```

```python
import functools

import jax
import jax.numpy as jnp
import numpy as np
from jax import lax
from jax.experimental import pallas as pl
from jax.experimental.pallas import tpu as pltpu

F32 = jnp.float32
BF16 = jnp.bfloat16
HEAD_DIM = 64
DILATIONS = (1, 4, 16)
NORM_EPS = 1e-6
NEG_BIG = -1e30
VMEM_LIMIT = 48 * 1024 * 1024
ROW_TILE = 256
LANE = 128
MESH = pl.DeviceIdType.MESH

ADAM_LR = 0.001
ADAM_B1 = 0.9
ADAM_B2 = 0.999
ADAM_EPS = 1e-08
ADAM_WD = 0.01
ADAM_STEP = 10

TILE_CANDIDATES = (1024, 1408, 768, 512, 384, 256, 128)


def _pick(dim, cands=TILE_CANDIDATES):
    for c in cands:
        if c <= dim and dim % c == 0:
            return c
    return dim


def _params(sem):
    return pltpu.CompilerParams(dimension_semantics=sem, vmem_limit_bytes=VMEM_LIMIT)


def _mm(a, b, mode, out_dtype, name, scale=None):
    if mode == "nn":
        m, k = a.shape
        n = b.shape[1]
    elif mode == "nt":
        m, k = a.shape
        n = b.shape[0]
    else:
        k, m = a.shape
        n = b.shape[1]
    tm, tn, tk = _pick(m), _pick(n), _pick(k)
    nk = k // tk
    if mode == "nn":
        a_spec = pl.BlockSpec((tm, tk), lambda i, j, kk: (i, kk))
        b_spec = pl.BlockSpec((tk, tn), lambda i, j, kk: (kk, j))
        dims = (((1,), (0,)), ((), ()))
    elif mode == "nt":
        a_spec = pl.BlockSpec((tm, tk), lambda i, j, kk: (i, kk))
        b_spec = pl.BlockSpec((tn, tk), lambda i, j, kk: (j, kk))
        dims = (((1,), (1,)), ((), ()))
    else:
        a_spec = pl.BlockSpec((tk, tm), lambda i, j, kk: (kk, i))
        b_spec = pl.BlockSpec((tk, tn), lambda i, j, kk: (kk, j))
        dims = (((0,), (0,)), ((), ()))

    def finish(acc):
        if scale is not None:
            acc = acc * scale
        return acc.astype(out_dtype)

    if nk == 1:
        def body(a_ref, b_ref, o_ref):
            o_ref[...] = finish(lax.dot_general(a_ref[...].astype(BF16), b_ref[...].astype(BF16), dims, preferred_element_type=F32))
        scratch = []
    else:
        def body(a_ref, b_ref, o_ref, acc_ref):
            kk = pl.program_id(2)

            @pl.when(kk == 0)
            def _():
                acc_ref[...] = jnp.zeros_like(acc_ref)

            acc_ref[...] += lax.dot_general(a_ref[...].astype(BF16), b_ref[...].astype(BF16), dims, preferred_element_type=F32)

            @pl.when(kk == nk - 1)
            def _():
                o_ref[...] = finish(acc_ref[...])
        scratch = [pltpu.VMEM((tm, tn), F32)]

    return pl.pallas_call(
        body, name=name,
        grid=(m // tm, n // tn, nk),
        in_specs=[a_spec, b_spec],
        out_specs=pl.BlockSpec((tm, tn), lambda i, j, kk: (i, j)),
        out_shape=jax.ShapeDtypeStruct((m, n), out_dtype),
        scratch_shapes=scratch,
        compiler_params=_params(("parallel", "parallel", "arbitrary")),
    )(a, b)


def _rstd(v):
    return lax.rsqrt(jnp.mean(v * v, axis=-1, keepdims=True) + NORM_EPS)


def _rms_bwd(dy, v, g, r):
    gy = dy * g
    dv = r * (gy - v * (r * r) * jnp.mean(gy * v, axis=-1, keepdims=True))
    return dv, dy * v * r


def _row_spec(t, width):
    return pl.BlockSpec((t, width), lambda i: (i, 0))


def _gain_spec(width):
    return pl.BlockSpec((1, width), lambda i: (0, 0))


def _norm_res_fwd(x, mix, g_post, pre_gains, name):
    s, d = x.shape
    t = _pick(s, (ROW_TILE,))
    has_mix = mix is not None
    n_pre = len(pre_gains)

    def body(*refs):
        x_ref = refs[0]
        pos = 1
        x1 = x_ref[...]
        if has_mix:
            mv = refs[1][...].astype(F32)
            x1 = x1 + mv * _rstd(mv) * refs[2][...]
            pos = 3
        gains = refs[pos:pos + n_pre]
        outs = refs[pos + n_pre:]
        if has_mix:
            outs[0][...] = x1
            outs = outs[1:]
        r = _rstd(x1)
        for g_ref, o_ref in zip(gains, outs):
            o_ref[...] = (x1 * r * g_ref[...]).astype(BF16)

    ins = [x] + ([mix, g_post] if has_mix else []) + list(pre_gains)
    in_specs = [_row_spec(t, d)] + ([_row_spec(t, d), _gain_spec(d)] if has_mix else []) + [_gain_spec(d)] * n_pre
    out_shape = ([jax.ShapeDtypeStruct((s, d), F32)] if has_mix else []) + [jax.ShapeDtypeStruct((s, d), BF16)] * n_pre
    out_specs = [_row_spec(t, d)] * len(out_shape)
    res = pl.pallas_call(
        body, name=name, grid=(s // t,), in_specs=in_specs, out_specs=out_specs, out_shape=out_shape,
        compiler_params=_params(("parallel",)),
    )(*ins)
    if has_mix:
        return res[0], list(res[1:])
    return x, list(res)


def _norm_res_loss(x, mix, g_post, target, name):
    s, d = x.shape
    t = _pick(s, (ROW_TILE,))

    def body(x_ref, m_ref, g_ref, t_ref, dy_ref, loss_ref):
        mv = m_ref[...].astype(F32)
        y = x_ref[...] + mv * _rstd(mv) * g_ref[...]
        err = y - t_ref[...]
        dy_ref[...] = err * (1.0 / d)

        @pl.when(pl.program_id(0) == 0)
        def _():
            loss_ref[...] = jnp.zeros_like(loss_ref)

        loss_ref[...] += jnp.sum(err * err)

    dy, acc = pl.pallas_call(
        body, name=name, grid=(s // t,),
        in_specs=[_row_spec(t, d), _row_spec(t, d), _gain_spec(d), _row_spec(t, d)],
        out_specs=[_row_spec(t, d), pl.BlockSpec((8, LANE), lambda i: (0, 0))],
        out_shape=[jax.ShapeDtypeStruct((s, d), F32), jax.ShapeDtypeStruct((8, LANE), F32)],
        compiler_params=_params(("arbitrary",)),
    )(x, mix, g_post, target)
    return dy, acc[0, 0] * (0.5 / d)


def _norm_bwd(dx_out, branches, x_in, post, name):
    s, d = dx_out.shape
    t = _pick(s, (ROW_TILE,))
    nb = len(branches)
    has_post = post is not None

    def body(*refs):
        dx_ref = refs[0]
        pos = 1
        dx = dx_ref[...]
        first = pl.program_id(0) == 0
        n_in = 1 + (1 + 2 * nb if nb else 0) + (2 if has_post else 0)
        outs = refs[n_in:]
        opos = 0
        if nb:
            xv = refs[pos][...]
            pos += 1
            r = _rstd(xv)
            dx_o = outs[0]
            opos = 1
            for _ in range(nb):
                dxn = refs[pos][...].astype(F32)
                g = refs[pos + 1][...]
                pos += 2
                dv, dg_rows = _rms_bwd(dxn, xv, g, r)
                dx = dx + dv
                dg_ref = outs[opos]
                opos += 1

                @pl.when(first)
                def _(dg_ref=dg_ref):
                    dg_ref[...] = jnp.zeros_like(dg_ref)

                dg_ref[...] += jnp.sum(dg_rows, axis=0, keepdims=True)
            dx_o[...] = dx
        if has_post:
            mv = refs[pos][...].astype(F32)
            g = refs[pos + 1][...]
            dm, dg_rows = _rms_bwd(dx, mv, g, _rstd(mv))
            outs[opos][...] = dm.astype(BF16)
            dg_ref = outs[opos + 1]

            @pl.when(first)
            def _():
                dg_ref[...] = jnp.zeros_like(dg_ref)

            dg_ref[...] += jnp.sum(dg_rows, axis=0, keepdims=True)

    ins, in_specs = [dx_out], [_row_spec(t, d)]
    out_shape, out_specs = [], []
    if nb:
        ins.append(x_in)
        in_specs.append(_row_spec(t, d))
        out_shape.append(jax.ShapeDtypeStruct((s, d), F32))
        out_specs.append(_row_spec(t, d))
        for dxn, g in branches:
            ins += [dxn, g]
            in_specs += [_row_spec(t, d), _gain_spec(d)]
            out_shape.append(jax.ShapeDtypeStruct((1, d), F32))
            out_specs.append(_gain_spec(d))
    if has_post:
        ins += [post[0], post[1]]
        in_specs += [_row_spec(t, d), _gain_spec(d)]
        out_shape += [jax.ShapeDtypeStruct((s, d), BF16), jax.ShapeDtypeStruct((1, d), F32)]
        out_specs += [_row_spec(t, d), _gain_spec(d)]
    res = pl.pallas_call(
        body, name=name, grid=(s // t,), in_specs=in_specs, out_specs=out_specs, out_shape=out_shape,
        compiler_params=_params(("arbitrary",)),
    )(*ins)
    res = list(res)
    dx_in = res.pop(0) if nb else dx_out
    dgs = [res.pop(0) for _ in range(nb)]
    dm, dg_post = (res[0], res[1]) if has_post else (None, None)
    return dx_in, dgs, dm, dg_post


HALO = 16


def _shift_down(u, prev, k):
    rows = lax.broadcasted_iota(jnp.int32, u.shape, 0)
    out = pltpu.roll(u, k, 0)
    for i in range(k):
        out = jnp.where(rows == i, prev[HALO - k + i:HALO - k + i + 1, :], out)
    return out


def _shift_up(u, nxt, k):
    n = u.shape[0]
    rows = lax.broadcasted_iota(jnp.int32, u.shape, 0)
    out = pltpu.roll(u, n - k, 0)
    for i in range(k):
        out = jnp.where(rows == n - k + i, nxt[i:i + 1, :], out)
    return out


def _conv_gate_fwd(p, cw, name):
    s, d3 = p.shape
    d = d3 // 3
    t = _pick(s, (ROW_TILE,))
    hb = t // HALO

    def body(p_ref, prev_ref, w_ref, z_ref):
        i = pl.program_id(0)
        pv = p_ref[...].astype(F32)
        b, u = pv[:, :d], pv[:, d:2 * d] * pv[:, 2 * d:]
        ph = prev_ref[...].astype(F32)
        up = jnp.where(i > 0, ph[:, d:2 * d] * ph[:, 2 * d:], 0.0)
        w = w_ref[...]
        y = w[0:1, :] * _shift_down(u, up, 2) + w[1:2, :] * _shift_down(u, up, 1) + w[2:3, :] * u
        z_ref[...] = (b * y).astype(BF16)

    return pl.pallas_call(
        body, name=name, grid=(s // t,),
        in_specs=[_row_spec(t, d3),
                  pl.BlockSpec((HALO, d3), lambda i: (jnp.maximum(i * hb - 1, 0), 0)),
                  pl.BlockSpec((3, d), lambda i: (0, 0))],
        out_specs=_row_spec(t, d),
        out_shape=jax.ShapeDtypeStruct((s, d), BF16),
        compiler_params=_params(("parallel",)),
    )(p, p, cw)


def _conv_gate_bwd(p, dz, cw, name):
    s, d3 = p.shape
    d = d3 // 3
    t = _pick(s, (ROW_TILE,))
    hb = t // HALO
    nt = s // t
    last_halo = s // HALO - 1

    def body(p_ref, prev_ref, next_ref, dz_ref, dznext_ref, w_ref, dp_ref, dw_ref):
        i = pl.program_id(0)
        pv = p_ref[...].astype(F32)
        b, c, h = pv[:, :d], pv[:, d:2 * d], pv[:, 2 * d:]
        u = c * h
        ph = prev_ref[...].astype(F32)
        up = jnp.where(i > 0, ph[:, d:2 * d] * ph[:, 2 * d:], 0.0)
        w = w_ref[...]
        u1, u2 = _shift_down(u, up, 1), _shift_down(u, up, 2)
        y = w[0:1, :] * u2 + w[1:2, :] * u1 + w[2:3, :] * u
        dz = dz_ref[...].astype(F32)
        dy = dz * b
        dyn = jnp.where(i < nt - 1, dznext_ref[...].astype(F32) * next_ref[...].astype(F32)[:, :d], 0.0)
        du = w[2:3, :] * dy + w[1:2, :] * _shift_up(dy, dyn, 1) + w[0:1, :] * _shift_up(dy, dyn, 2)
        dp_ref[:, :d] = (dz * y).astype(BF16)
        dp_ref[:, d:2 * d] = (du * h).astype(BF16)
        dp_ref[:, 2 * d:] = (du * c).astype(BF16)

        @pl.when(i == 0)
        def _():
            dw_ref[...] = jnp.zeros_like(dw_ref)

        dw_ref[0:1, :] += jnp.sum(dy * u2, axis=0, keepdims=True)
        dw_ref[1:2, :] += jnp.sum(dy * u1, axis=0, keepdims=True)
        dw_ref[2:3, :] += jnp.sum(dy * u, axis=0, keepdims=True)

    return pl.pallas_call(
        body, name=name, grid=(nt,),
        in_specs=[_row_spec(t, d3),
                  pl.BlockSpec((HALO, d3), lambda i: (jnp.maximum(i * hb - 1, 0), 0)),
                  pl.BlockSpec((HALO, d3), lambda i: (jnp.minimum((i + 1) * hb, last_halo), 0)),
                  _row_spec(t, d),
                  pl.BlockSpec((HALO, d), lambda i: (jnp.minimum((i + 1) * hb, last_halo), 0)),
                  pl.BlockSpec((3, d), lambda i: (0, 0))],
        out_specs=[_row_spec(t, d3), pl.BlockSpec((3, d), lambda i: (0, 0))],
        out_shape=[jax.ShapeDtypeStruct((s, d3), BF16), jax.ShapeDtypeStruct((3, d), F32)],
        compiler_params=_params(("arbitrary",)),
    )(p, p, p, dz, dz, cw)


def _swiglu_fwd(f, name):
    s, f2 = f.shape
    ff = f2 // 2
    t = _pick(s, (ROW_TILE,))

    def body(f_ref, a_ref):
        gate = f_ref[:, :ff].astype(F32)
        up = f_ref[:, ff:].astype(F32)
        a_ref[...] = (gate * jax.nn.sigmoid(gate) * up).astype(BF16)

    return pl.pallas_call(
        body, name=name, grid=(s // t,),
        in_specs=[_row_spec(t, f2)], out_specs=_row_spec(t, ff),
        out_shape=jax.ShapeDtypeStruct((s, ff), BF16),
        compiler_params=_params(("parallel",)),
    )(f)


def _swiglu_bwd(f, da, name):
    s, f2 = f.shape
    ff = f2 // 2
    t = _pick(s, (ROW_TILE,))

    def body(f_ref, da_ref, df_ref):
        gate = f_ref[:, :ff].astype(F32)
        up = f_ref[:, ff:].astype(F32)
        dav = da_ref[...].astype(F32)
        sg = jax.nn.sigmoid(gate)
        silu = gate * sg
        df_ref[:, :ff] = (dav * up * (sg + silu * (1.0 - sg))).astype(BF16)
        df_ref[:, ff:] = (dav * silu).astype(BF16)

    return pl.pallas_call(
        body, name=name, grid=(s // t,),
        in_specs=[_row_spec(t, f2), _row_spec(t, ff)], out_specs=_row_spec(t, f2),
        out_shape=jax.ShapeDtypeStruct((s, f2), BF16),
        compiler_params=_params(("parallel",)),
    )(f, da)


SUPER = 2048
RES = 16
PAIR = 128
L = 128


def _alibi_slopes(n_heads):
    h = np.arange(n_heads, dtype=np.float32) + 1.0
    return np.power(2.0, -8.0 * h / n_heads).astype(np.float32)


def _permute16(x, inverse, name):
    s, d = x.shape
    cw = LANE

    def body(x_ref, o_ref):
        if inverse:
            for m in range(L):
                o_ref[RES * m:RES * (m + 1), :] = x_ref[pl.ds(m, RES, stride=L), :]
        else:
            for r in range(RES):
                o_ref[L * r:L * (r + 1), :] = x_ref[pl.ds(r, L, stride=RES), :]

    spec = pl.BlockSpec((SUPER, cw), lambda i, j: (i, j))
    return pl.pallas_call(
        body, name=name, grid=(s // SUPER, d // cw), in_specs=[spec], out_specs=spec,
        out_shape=jax.ShapeDtypeStruct((s, d), x.dtype),
        compiler_params=_params(("parallel", "parallel")),
    )(x)


def _slope_table(d):
    nh = d // HEAD_DIM
    sl = _alibi_slopes(nh)
    tab = np.repeat(sl, HEAD_DIM).reshape(d // PAIR, 1, PAIR)
    return jnp.asarray(np.broadcast_to(tab, (d // PAIR, 8, PAIR)).copy())


def _geometry(dil):
    nch = RES // dil
    return nch, L // nch


def _band(dil):
    nch, w = _geometry(dil)
    sh = w.bit_length() - 1
    i = lax.broadcasted_iota(jnp.int32, (L, 2 * L), 0)
    j = lax.broadcasted_iota(jnp.int32, (L, 2 * L), 1)

    def pos(t):
        return jnp.bitwise_and(t, w - 1) * nch + jnp.right_shift(t, sh)

    delta = pos(i) + L - (pos(jnp.bitwise_and(j, L - 1)) + jnp.bitwise_and(j, L))
    return (delta * dil).astype(F32), (delta >= 0) & (delta <= L), j < L


def _offsets(dil, res, n):
    nch, w = _geometry(dil)

    def al(v):
        return v if isinstance(v, int) else pl.multiple_of(v, w)

    q_off = [al((a * dil + res) * L + n * w) for a in range(nch)]
    k_off = [al((a * dil + res) * 2 * L + L + n * w) for a in range(nch)]
    kp_off = [al((a * dil + res) * 2 * L + L + n * w - w) for a in range(nch)]
    return q_off, k_off, kp_off, w


def _gather(ref, offs, w):
    parts = [ref[pl.ds(o, w), :] for o in offs]
    return parts[0] if len(parts) == 1 else jnp.concatenate(parts, axis=0)


def _scatter(ref, offs, w, val, add=False):
    for a, o in enumerate(offs):
        piece = val[a * w:(a + 1) * w, :]
        if add:
            ref[pl.ds(o, w), :] += piece
        else:
            ref[pl.ds(o, w), :] = piece


def _fill_key_buffer(buf, prev_ref, cur_ref):
    for r in range(RES):
        buf[2 * L * r:2 * L * r + L, :] = prev_ref[L * r:L * (r + 1), :]
        buf[2 * L * r + L:2 * L * (r + 1), :] = cur_ref[L * r:L * (r + 1), :]


def _two_heads(x, low):
    zero = jnp.zeros_like(x)
    return jnp.concatenate([jnp.where(low, x, zero), jnp.where(low, zero, x)], axis=0)


def _loop_blocks(dil, do_block):
    nch, _ = _geometry(dil)
    if dil == RES:
        def it(r, c):
            do_block(r, 0)
            return c
        lax.fori_loop(0, RES, it, 0, unroll=2)
    else:
        def it(n, c):
            for res in range(dil):
                do_block(res, n)
            return c
        lax.fori_loop(0, nch, it, 0, unroll=2 if dil == 1 else 1)


NT = (((1,), (1,)), ((), ()))
TN = (((0,), (0,)), ((), ()))


def _attention_fwd(q, kv, name):
    s, d = q.shape
    g_n, ns = d // PAIR, s // SUPER

    def body(sl_ref, q_ref, kc_ref, kp_ref, vc_ref, vp_ref, o_ref, lse_ref, kbuf, vbuf, m_s, l_s, acc_s):
        sb = pl.program_id(1)
        _fill_key_buffer(kbuf, kp_ref, kc_ref)
        _fill_key_buffer(vbuf, vp_ref, vc_ref)
        low = lax.broadcasted_iota(jnp.int32, (L, PAIR), 1) < HEAD_DIM
        low_k = lax.broadcasted_iota(jnp.int32, (2 * L, PAIR), 1) < HEAD_DIM
        nsl = (-sl_ref[0:1, 0:1], -sl_ref[0:1, HEAD_DIM:HEAD_DIM + 1])
        for bi, dil in enumerate(DILATIONS):
            first_branch, last_branch = bi == 0, bi == len(DILATIONS) - 1
            base, band, prev_half = _band(dil)

            def do_block(res, n, dil=dil, first_branch=first_branch, last_branch=last_branch,
                         base=base, band=band, prev_half=prev_half):
                q_off, k_off, kp_off, w = _offsets(dil, res, n)
                qf = _gather(q_ref, q_off, w).astype(BF16)
                kcat = jnp.concatenate([_gather(kbuf, kp_off, w), _gather(kbuf, k_off, w)], axis=0).astype(BF16)
                vcat = jnp.concatenate([_gather(vbuf, kp_off, w), _gather(vbuf, k_off, w)], axis=0).astype(BF16)
                k_bd, v_bd = _two_heads(kcat, low_k), _two_heads(vcat, low_k)
                sc = lax.dot_general(qf, k_bd, NT, preferred_element_type=F32)
                starts_sequence = jnp.logical_and(sb == 0, n == 0)
                valid = band & jnp.logical_not(prev_half & starts_sequence)
                if not first_branch:
                    m_prev = _gather(m_s, q_off, w)
                ps, m_new, l_blk = [], [], []
                for h in range(2):
                    s_h = jnp.where(valid, sc[:, 2 * L * h:2 * L * (h + 1)] + nsl[h] * base, NEG_BIG)
                    mh = jnp.max(s_h, axis=1, keepdims=True)
                    if not first_branch:
                        mh = jnp.maximum(mh, m_prev[:, HEAD_DIM * h:HEAD_DIM * h + 1])
                    p_h = jnp.exp(s_h - mh)
                    ps.append(p_h.astype(BF16))
                    m_new.append(mh)
                    l_blk.append(jnp.sum(p_h, axis=1, keepdims=True))
                m_full = jnp.where(low, m_new[0], m_new[1])
                l_full = jnp.where(low, l_blk[0], l_blk[1])
                acc = jnp.dot(jnp.concatenate(ps, axis=1), v_bd, preferred_element_type=F32)
                if not first_branch:
                    alpha = jnp.exp(m_prev - m_full)
                    l_full = _gather(l_s, q_off, w) * alpha + l_full
                    acc = _gather(acc_s, q_off, w) * alpha + acc
                if last_branch:
                    _scatter(o_ref, q_off, w, acc / l_full)
                    _scatter(lse_ref, q_off, w, m_full + jnp.log(l_full))
                else:
                    _scatter(m_s, q_off, w, m_full)
                    _scatter(l_s, q_off, w, l_full)
                    _scatter(acc_s, q_off, w, acc)

            _loop_blocks(dil, do_block)

    prev = lambda i: jnp.maximum(i - 1, 0)
    blk = pl.BlockSpec((SUPER, PAIR), lambda g, i: (i, g))
    in_specs = [pl.BlockSpec((None, 8, PAIR), lambda g, i: (g, 0, 0)), blk,
                pl.BlockSpec((SUPER, PAIR), lambda g, i: (i, g)),
                pl.BlockSpec((SUPER, PAIR), lambda g, i: (prev(i), g)),
                pl.BlockSpec((SUPER, PAIR), lambda g, i: (i, g_n + g)),
                pl.BlockSpec((SUPER, PAIR), lambda g, i: (prev(i), g_n + g))]
    return pl.pallas_call(
        body, name=name, grid=(g_n, ns), in_specs=in_specs, out_specs=[blk, blk],
        out_shape=[jax.ShapeDtypeStruct((s, d), F32)] * 2,
        scratch_shapes=[pltpu.VMEM((2 * SUPER, PAIR), F32)] * 2 + [pltpu.VMEM((SUPER, PAIR), F32)] * 3,
        compiler_params=_params(("parallel", "parallel")),
    )(_slope_table(d), q, kv, kv, kv, kv)


def _attention_bwd(q, kv, o, do, lse, dkv_in, name):
    s, d = q.shape
    g_n, ns = d // PAIR, s // SUPER
    has_in = dkv_in is not None

    def body(*refs):
        sl_ref, q_ref, do_ref, o_ref, lse_ref, kc_ref, kp_ref, vc_ref, vp_ref = refs[:9]
        pos = 9
        if has_in:
            dk_in_ref, dv_in_ref = refs[9:11]
            pos = 11
        dq_ref, dk_ref, dv_ref, kbuf, vbuf, dkbuf, dvbuf, dq_s = refs[pos:]
        step = pl.program_id(1)
        sb = ns - 1 - step
        _fill_key_buffer(kbuf, kp_ref, kc_ref)
        _fill_key_buffer(vbuf, vp_ref, vc_ref)

        @pl.when(step == 0)
        def _():
            dkbuf[...] = jnp.zeros_like(dkbuf)
            dvbuf[...] = jnp.zeros_like(dvbuf)

        @pl.when(step > 0)
        def _():
            for buf in (dkbuf, dvbuf):
                for r in range(RES):
                    buf[2 * L * r + L:2 * L * (r + 1), :] = buf[2 * L * r:2 * L * r + L, :]
                    buf[2 * L * r:2 * L * r + L, :] = jnp.zeros((L, PAIR), F32)

        low = lax.broadcasted_iota(jnp.int32, (L, PAIR), 1) < HEAD_DIM
        low_k = lax.broadcasted_iota(jnp.int32, (2 * L, PAIR), 1) < HEAD_DIM
        nsl = (-sl_ref[0:1, 0:1], -sl_ref[0:1, HEAD_DIM:HEAD_DIM + 1])
        for bi, dil in enumerate(DILATIONS):
            first_branch = bi == 0
            base, band, prev_half = _band(dil)

            def do_block(res, n, dil=dil, first_branch=first_branch, base=base, band=band, prev_half=prev_half):
                q_off, k_off, kp_off, w = _offsets(dil, res, n)
                qb = _gather(q_ref, q_off, w).astype(BF16)
                dof = _gather(do_ref, q_off, w)
                prod = dof * _gather(o_ref, q_off, w)
                dob = dof.astype(BF16)
                lse_f = _gather(lse_ref, q_off, w)
                zero = jnp.zeros_like(prod)
                dsum = (jnp.sum(jnp.where(low, prod, zero), axis=1, keepdims=True),
                        jnp.sum(jnp.where(low, zero, prod), axis=1, keepdims=True))
                kcat = jnp.concatenate([_gather(kbuf, kp_off, w), _gather(kbuf, k_off, w)], axis=0).astype(BF16)
                vcat = jnp.concatenate([_gather(vbuf, kp_off, w), _gather(vbuf, k_off, w)], axis=0).astype(BF16)
                k_bd, v_bd = _two_heads(kcat, low_k), _two_heads(vcat, low_k)
                sc = lax.dot_general(qb, k_bd, NT, preferred_element_type=F32)
                dp = lax.dot_general(dob, v_bd, NT, preferred_element_type=F32)
                starts_sequence = jnp.logical_and(sb == 0, n == 0)
                valid = band & jnp.logical_not(prev_half & starts_sequence)
                ps, dss = [], []
                for h in range(2):
                    cols = slice(2 * L * h, 2 * L * (h + 1))
                    lse_h = lse_f[:, HEAD_DIM * h:HEAD_DIM * h + 1]
                    p_h = jnp.exp(jnp.where(valid, sc[:, cols] + nsl[h] * base - lse_h, NEG_BIG))
                    dss.append((p_h * (dp[:, cols] - dsum[h])).astype(BF16))
                    ps.append(p_h.astype(BF16))
                ds_cat, p_cat = jnp.concatenate(dss, axis=1), jnp.concatenate(ps, axis=1)
                dq = jnp.dot(ds_cat, k_bd, preferred_element_type=F32)
                dk_bd = lax.dot_general(ds_cat, qb, TN, preferred_element_type=F32)
                dv_bd = lax.dot_general(p_cat, dob, TN, preferred_element_type=F32)
                dk = jnp.where(low_k, dk_bd[:2 * L], dk_bd[2 * L:])
                dv = jnp.where(low_k, dv_bd[:2 * L], dv_bd[2 * L:])
                _scatter(dq_s, q_off, w, dq, add=not first_branch)
                _scatter(dkbuf, kp_off, w, dk[:L], add=True)
                _scatter(dkbuf, k_off, w, dk[L:], add=True)
                _scatter(dvbuf, kp_off, w, dv[:L], add=True)
                _scatter(dvbuf, k_off, w, dv[L:], add=True)

            _loop_blocks(dil, do_block)

        dq_ref[...] = dq_s[...].astype(BF16)
        for r in range(RES):
            rows, cur = slice(L * r, L * (r + 1)), slice(2 * L * r + L, 2 * L * (r + 1))
            if has_in:
                dk_ref[rows, :] = dkbuf[cur, :] + dk_in_ref[rows, :]
                dv_ref[rows, :] = dvbuf[cur, :] + dv_in_ref[rows, :]
            else:
                dk_ref[rows, :] = dkbuf[cur, :]
                dv_ref[rows, :] = dvbuf[cur, :]

    rev = lambda i: ns - 1 - i
    prev = lambda i: jnp.maximum(ns - 2 - i, 0)
    blk = pl.BlockSpec((SUPER, PAIR), lambda g, i: (rev(i), g))
    in_specs = [pl.BlockSpec((None, 8, PAIR), lambda g, i: (g, 0, 0)), blk, blk, blk, blk,
                pl.BlockSpec((SUPER, PAIR), lambda g, i: (rev(i), g)),
                pl.BlockSpec((SUPER, PAIR), lambda g, i: (prev(i), g)),
                pl.BlockSpec((SUPER, PAIR), lambda g, i: (rev(i), g_n + g)),
                pl.BlockSpec((SUPER, PAIR), lambda g, i: (prev(i), g_n + g))]
    ins = [_slope_table(d), q, do, o, lse, kv, kv, kv, kv]
    if has_in:
        in_specs += [blk, blk]
        ins += list(dkv_in)
    res = pl.pallas_call(
        body, name=name, grid=(g_n, ns), in_specs=in_specs, out_specs=[blk, blk, blk],
        out_shape=[jax.ShapeDtypeStruct((s, d), BF16)] + [jax.ShapeDtypeStruct((s, d), F32)] * 2,
        scratch_shapes=[pltpu.VMEM((2 * SUPER, PAIR), F32)] * 4 + [pltpu.VMEM((SUPER, PAIR), F32)],
        compiler_params=_params(("parallel", "arbitrary")),
    )(*ins)
    return res[0], (res[1], res[2])


def _coords():
    return lax.axis_index("x"), lax.axis_index("y"), lax.axis_index("c")


def _chip_peers(x, y):
    return [(1 - x, y), (x, 1 - y), (1 - x, 1 - y)]


def _block_of(ref, axis, blk, size):
    start = pl.multiple_of(blk * size, size)
    if axis == 1:
        return ref.at[:, pl.ds(start, size)]
    return ref.at[pl.ds(start, size), :]


ANY = pl.BlockSpec(memory_space=pl.ANY)


def _gather_weights(shards, axes, name):
    n = len(shards)
    sizes = [sh.shape[ax] for sh, ax in zip(shards, axes)]

    def body(*refs):
        ins, outs = refs[:n], refs[n:2 * n]
        send_sems, recv_sems, local_sems = refs[2 * n:]
        x, y, c = _coords()
        my_blk = 2 * x + y
        peers = _chip_peers(x, y)
        local, sends, recvs = [], [], []
        for t in range(n):
            mine = _block_of(outs[t], axes[t], my_blk, sizes[t])
            cp = pltpu.make_async_copy(ins[t], mine, local_sems.at[t])
            cp.start()
            local.append(cp)
            for j, (px, py) in enumerate(peers):
                k = 3 * t + j
                snd = pltpu.make_async_remote_copy(
                    src_ref=ins[t], dst_ref=mine, send_sem=send_sems.at[k], recv_sem=recv_sems.at[k],
                    device_id=(px, py, c), device_id_type=MESH)
                snd.start()
                sends.append(snd)
                theirs = _block_of(outs[t], axes[t], 2 * px + py, sizes[t])
                recvs.append(pltpu.make_async_remote_copy(
                    src_ref=ins[t], dst_ref=theirs, send_sem=send_sems.at[k], recv_sem=recv_sems.at[k],
                    device_id=(px, py, c), device_id_type=MESH))
        for cp in recvs:
            cp.wait_recv()
        for cp in sends:
            cp.wait_send()
        for cp in local:
            cp.wait()

    out_shape = []
    for sh, ax in zip(shards, axes):
        full = list(sh.shape)
        full[ax] *= 4
        out_shape.append(jax.ShapeDtypeStruct(tuple(full), sh.dtype))
    return pl.pallas_call(
        body, name=name, in_specs=[ANY] * n, out_specs=[ANY] * n, out_shape=out_shape,
        scratch_shapes=[pltpu.SemaphoreType.DMA((3 * n,)), pltpu.SemaphoreType.DMA((3 * n,)),
                        pltpu.SemaphoreType.DMA((n,))],
        compiler_params=pltpu.CompilerParams(has_side_effects=True),
    )(*shards)


def _scatter_grads(groups, name):
    flat, meta = [], []
    for gi, (layers, ax) in enumerate(groups):
        for li, g in enumerate(layers):
            flat.append(g)
            meta.append((gi, li, ax, g.shape[ax] // 4))
    n, ng = len(flat), len(groups)

    def body(*refs):
        ins = refs[:n]
        mine_refs, sib_refs = refs[n:n + ng], refs[n + ng:n + 2 * ng]
        send_sems, recv_sems, local_sems, pair_send, pair_recv = refs[n + 2 * ng:]
        x, y, c = _coords()
        my_blk = 2 * x + y
        peers = _chip_peers(x, y)
        local, sends, recvs = [], [], [[] for _ in range(ng)]
        for t, (gi, li, ax, size) in enumerate(meta):
            cp = pltpu.make_async_copy(_block_of(ins[t], ax, my_blk, size), mine_refs[gi].at[0, li], local_sems.at[t])
            cp.start()
            local.append((gi, cp))
            for j, (px, py) in enumerate(peers):
                k = 3 * t + j
                snd = pltpu.make_async_remote_copy(
                    src_ref=_block_of(ins[t], ax, 2 * px + py, size), dst_ref=mine_refs[gi].at[1 + j, li],
                    send_sem=send_sems.at[k], recv_sem=recv_sems.at[k], device_id=(px, py, c), device_id_type=MESH)
                snd.start()
                sends.append(snd)
                recvs[gi].append(snd)
        pair = []
        for gi in range(ng):
            for cp in recvs[gi]:
                cp.wait_recv()
            for g2, cp in local:
                if g2 == gi:
                    cp.wait()
            fwd = pltpu.make_async_remote_copy(
                src_ref=mine_refs[gi], dst_ref=sib_refs[gi], send_sem=pair_send.at[gi], recv_sem=pair_recv.at[gi],
                device_id=(x, y, 1 - c), device_id_type=MESH)
            fwd.start()
            pair.append(fwd)
        for cp in pair:
            cp.wait_recv()
        for cp in sends + pair:
            cp.wait_send()

    stacks = []
    for layers, ax in groups:
        blk = list(layers[0].shape)
        blk[ax] //= 4
        stacks.append(jax.ShapeDtypeStruct((4, len(layers), *blk), layers[0].dtype))
    res = pl.pallas_call(
        body, name=name, in_specs=[ANY] * n, out_specs=[ANY] * (2 * ng), out_shape=stacks + stacks,
        scratch_shapes=[pltpu.SemaphoreType.DMA((3 * n,)), pltpu.SemaphoreType.DMA((3 * n,)),
                        pltpu.SemaphoreType.DMA((n,)), pltpu.SemaphoreType.DMA((ng,)),
                        pltpu.SemaphoreType.DMA((ng,))],
        compiler_params=pltpu.CompilerParams(has_side_effects=True),
    )(*flat)
    return [(res[gi], res[ng + gi]) for gi in range(ng)]


def _allreduce_small(v, name):
    r, cdim = v.shape

    def body(v_ref, out_ref, buf, send_sems, recv_sems):
        x, y, c = _coords()
        me = 4 * x + 2 * y + c
        buf[0] = v_ref[...]
        sends = []
        for k in range(1, 8):
            peer = (x if not (k & 4) else 1 - x, y if not (k & 2) else 1 - y, c if not (k & 1) else 1 - c)
            cp = pltpu.make_async_remote_copy(
                src_ref=v_ref, dst_ref=buf.at[k], send_sem=send_sems.at[k - 1], recv_sem=recv_sems.at[k - 1],
                device_id=peer, device_id_type=MESH)
            cp.start()
            sends.append(cp)
        for cp in sends:
            cp.wait_recv()
        total = buf[me]
        for src in range(1, 8):
            total = total + buf[jnp.bitwise_xor(me, src)]
        out_ref[...] = total
        for cp in sends:
            cp.wait_send()

    return pl.pallas_call(
        body, name=name,
        in_specs=[pl.BlockSpec(memory_space=pltpu.VMEM)], out_specs=pl.BlockSpec(memory_space=pltpu.VMEM),
        out_shape=jax.ShapeDtypeStruct((r, cdim), F32),
        scratch_shapes=[pltpu.VMEM((8, r, cdim), F32), pltpu.SemaphoreType.DMA((7,)), pltpu.SemaphoreType.DMA((7,))],
        compiler_params=pltpu.CompilerParams(has_side_effects=True),
    )(v)


def _adamw_math(w, g, m, v):
    m = ADAM_B1 * m + (1.0 - ADAM_B1) * g
    v = ADAM_B2 * v + (1.0 - ADAM_B2) * jnp.square(g)
    m_hat = m / (1.0 - ADAM_B1 ** ADAM_STEP)
    v_hat = v / (1.0 - ADAM_B2 ** ADAM_STEP)
    delta = -ADAM_LR * (m_hat / (jnp.sqrt(v_hat) + ADAM_EPS) + ADAM_WD * w)
    return delta, m, v


def _adamw(w, m, v, grads, name):
    r, cdim = w.shape
    t = _pick(r, (256, 128, 64, 32, 16, 8))
    paired = isinstance(grads, tuple)

    def body(*refs):
        w_ref, m_ref, v_ref = refs[:3]
        if paired:
            a_ref, b_ref = refs[3:5]
            outs = refs[5:]
            sa = a_ref[0].astype(F32)
            sb = b_ref[0].astype(F32)
            for k in range(1, 4):
                sa = sa + a_ref[k].astype(F32)
                sb = sb + b_ref[k].astype(F32)
            g = sa + sb
        else:
            g = refs[3][...]
            outs = refs[4:]
        delta, m_new, v_new = _adamw_math(w_ref[...], g, m_ref[...], v_ref[...])
        outs[0][...] = g
        outs[1][...] = delta
        outs[2][...] = m_new
        outs[3][...] = v_new

    spec = pl.BlockSpec((t, cdim), lambda i: (i, 0))
    stack_spec = pl.BlockSpec((4, t, cdim), lambda i: (0, i, 0))
    ins = [w, m, v] + (list(grads) if paired else [grads])
    in_specs = [spec] * 3 + ([stack_spec] * 2 if paired else [spec])
    return pl.pallas_call(
        body, name=name, grid=(r // t,), in_specs=in_specs, out_specs=[spec] * 4,
        out_shape=[jax.ShapeDtypeStruct((r, cdim), F32)] * 4,
        compiler_params=_params(("parallel",)),
    )(*ins)


def _local_step(x, target, gains, conv_ws, kv_gain, w):
    depth = len(gains)
    n_a = len(w["conv_in"])
    saved = []
    kv = kvn = None
    _, (xn,) = _norm_res_fwd(x, None, None, [gains[0][0]], "norm_first")
    h = x
    for l in range(depth):
        g = gains[l]
        sv = {"x_in": h, "xn": xn}
        if l < n_a:
            p = _mm(xn, w["conv_in"][l], "nn", BF16, f"conv_in_fwd_{l}")
            z = _conv_gate_fwd(p, conv_ws[l], f"conv_gate_fwd_{l}")
            mix = _mm(z, w["conv_out"][l], "nn", BF16, f"conv_out_fwd_{l}")
            sv.update(p=p, z=z)
        else:
            j = l - n_a
            q = _mm(xn, w["q"][j], "nn", F32, f"q_fwd_{j}", scale=HEAD_DIM ** -0.5)
            o, lse = _attention_fwd(q, kv, f"attn_fwd_{j}")
            mix = _mm(o, w["o"][j], "nn", BF16, f"o_fwd_{j}")
            sv.update(q=q, o=o, lse=lse)
        x1, (xn2,) = _norm_res_fwd(h, mix, g[1], [g[2]], f"norm_mid_{l}")
        f = _mm(xn2, w["ffn_in"][l], "nn", BF16, f"ffn_in_fwd_{l}")
        a = _swiglu_fwd(f, f"swiglu_fwd_{l}")
        ff = _mm(a, w["ffn_out"][l], "nn", BF16, f"ffn_out_fwd_{l}")
        sv.update(mix=mix, x1=x1, xn2=xn2, f=f, a=a, ff=ff)
        saved.append(sv)
        if l == depth - 1:
            dx, loss = _norm_res_loss(x1, ff, g[3], target, "norm_loss")
        else:
            if l == n_a - 1:
                h, _ = _norm_res_fwd(x1, ff, g[3], [], f"norm_end_{l}")
                h = _permute16(h, False, "permute_stream")
                target = _permute16(target, False, "permute_target")
                _, (xn, kvn) = _norm_res_fwd(h, None, None, [gains[l + 1][0], kv_gain], "norm_permuted")
                kv = _mm(kvn, w["kv"][0], "nn", F32, "kv_fwd")
            else:
                h, (xn,) = _norm_res_fwd(x1, ff, g[3], [gains[l + 1][0]], f"norm_end_{l}")
    grads = {k: [None] * len(v) for k, v in w.items()}
    d_gains = [[None] * 4 for _ in range(depth)]
    d_conv = [None] * n_a
    d_kv_gain = None
    dkv = None
    _, _, dff, d_gains[depth - 1][3] = _norm_bwd(dx, [], None, (saved[-1]["ff"], gains[-1][3]), "norm_loss_bwd")
    for l in reversed(range(depth)):
        sv, g = saved[l], gains[l]
        da = _mm(dff, w["ffn_out"][l], "nt", BF16, f"ffn_out_dx_{l}")
        grads["ffn_out"][l] = _mm(sv["a"], dff, "tn", BF16, f"ffn_out_dw_{l}")
        df = _swiglu_bwd(sv["f"], da, f"swiglu_bwd_{l}")
        dxn2 = _mm(df, w["ffn_in"][l], "nt", BF16, f"ffn_in_dx_{l}")
        grads["ffn_in"][l] = _mm(sv["xn2"], df, "tn", BF16, f"ffn_in_dw_{l}")
        dx, (d_gains[l][2],), dmix, d_gains[l][1] = _norm_bwd(
            dx, [(dxn2, g[2])], sv["x1"], (sv["mix"], g[1]), f"norm_mid_bwd_{l}")
        if l < n_a:
            dz = _mm(dmix, w["conv_out"][l], "nt", BF16, f"conv_out_dx_{l}")
            grads["conv_out"][l] = _mm(sv["z"], dmix, "tn", BF16, f"conv_out_dw_{l}")
            dp, d_conv[l] = _conv_gate_bwd(sv["p"], dz, conv_ws[l], f"conv_gate_bwd_{l}")
            dxn = _mm(dp, w["conv_in"][l], "nt", BF16, f"conv_in_dx_{l}")
            grads["conv_in"][l] = _mm(sv["xn"], dp, "tn", BF16, f"conv_in_dw_{l}")
        else:
            j = l - n_a
            do = _mm(dmix, w["o"][j], "nt", F32, f"o_dx_{j}")
            grads["o"][j] = _mm(sv["o"], dmix, "tn", BF16, f"o_dw_{j}")
            dq, dkv = _attention_bwd(sv["q"], kv, sv["o"], do, sv["lse"], dkv, f"attn_bwd_{j}")
            scale = HEAD_DIM ** -0.5
            dxn = _mm(dq, w["q"][j], "nt", BF16, f"q_dx_{j}", scale=scale)
            grads["q"][j] = _mm(sv["xn"], dq, "tn", BF16, f"q_dw_{j}", scale=scale)
        branches = [(dxn, g[0])]
        if l == n_a:
            dkv_cat = jnp.concatenate([dkv[0], dkv[1]], axis=1).astype(BF16)
            dkvn = _mm(dkv_cat, w["kv"][0], "nt", BF16, "kv_dx")
            grads["kv"][0] = _mm(kvn, dkv_cat, "tn", BF16, "kv_dw")
            branches.append((dkvn, kv_gain))
        post = (saved[l - 1]["ff"], gains[l - 1][3]) if l > 0 else None
        if l == n_a:
            dx, dgs, _, _ = _norm_bwd(dx, branches, sv["x_in"], None, f"norm_end_bwd_{l}")
            dx = _permute16(dx, True, "unpermute_stream")
            _, _, dff, dg_post = _norm_bwd(dx, [], None, post, "norm_boundary_bwd")
        else:
            dx, dgs, dff, dg_post = _norm_bwd(dx, branches, sv["x_in"], post, f"norm_end_bwd_{l}")
        d_gains[l][0] = dgs[0]
        if l == n_a:
            d_kv_gain = dgs[1]
        if l > 0:
            d_gains[l - 1][3] = dg_post
    return loss, dx, grads, d_gains, d_conv, d_kv_gain


BIG = (
    ("conv_in", 1), ("conv_out", 0), ("kv", 1), ("q", 0), ("o", 0), ("ffn_in", 1), ("ffn_out", 0))


def kernel(x, norm_g, conv_in_w, conv_w, conv_out_w, kv_norm_g, kv_w, q_w, o_w, ffn_in_w, ffn_out_w, loss_target, m_norm_g, m_conv_in_w, m_conv_w, m_conv_out_w, m_kv_norm_g, m_kv_w, m_q_w, m_o_w, m_ffn_in_w, m_ffn_out_w, v_norm_g, v_conv_in_w, v_conv_w, v_conv_out_w, v_kv_norm_g, v_kv_w, v_q_w, v_o_w, v_ffn_in_w, v_ffn_out_w):
    depth, _, dq = norm_g.shape
    d = 4 * dq
    n_a = conv_w.shape[0]
    big_w = {"conv_in": conv_in_w, "conv_out": conv_out_w, "kv": kv_w[None], "q": q_w, "o": o_w,
             "ffn_in": ffn_in_w, "ffn_out": ffn_out_w}
    big_m = {"conv_in": m_conv_in_w, "conv_out": m_conv_out_w, "kv": m_kv_w[None], "q": m_q_w, "o": m_o_w,
             "ffn_in": m_ffn_in_w, "ffn_out": m_ffn_out_w}
    big_v = {"conv_in": v_conv_in_w, "conv_out": v_conv_out_w, "kv": v_kv_w[None], "q": v_q_w, "o": v_o_w,
             "ffn_in": v_ffn_in_w, "ffn_out": v_ffn_out_w}

    n_gain, n_tap = depth * 4, n_a * conv_w.shape[1]
    small_rows = -(-(n_gain + n_tap + 1) // 8) * 8
    pad_rows = small_rows - n_gain - n_tap

    def pack_small(gains, taps):
        return jnp.concatenate([gains.reshape(n_gain, dq), taps.reshape(n_tap, dq), jnp.zeros((pad_rows, dq), F32)])

    shards, axes, index = [], [], {}
    for name, ax in BIG:
        wb = big_w[name].astype(BF16)
        index[name] = (len(shards), wb.shape[0])
        for l in range(wb.shape[0]):
            shards.append(wb[l])
            axes.append(ax)
    shards.append(pack_small(norm_g, conv_w))
    axes.append(1)
    full = _gather_weights(shards, axes, "gather_weights")
    w = {name: [full[s0 + l] for l in range(cnt)] for name, (s0, cnt) in index.items()}
    small = full[-1]
    gains = [[small[4 * l + i][None] for i in range(4)] for l in range(depth)]
    conv_ws = [small[n_gain + 3 * l:n_gain + 3 * l + 3] for l in range(n_a)]
    kv_gain = kv_norm_g[None]

    loss, dx, grads, d_gains, d_conv, d_kv_gain = _local_step(x[0], loss_target[0], gains, conv_ws, kv_gain, w)
    loss = lax.psum(loss, ("x", "y", "c"))

    small_g = jnp.concatenate([dg for row in d_gains for dg in row] + list(d_conv) + [d_kv_gain]
                              + [jnp.zeros((pad_rows - 1, d), F32)])
    small_g = _allreduce_small(small_g, "allreduce_small")
    blk = 2 * lax.axis_index("x") + lax.axis_index("y")
    mine_small = lax.dynamic_slice_in_dim(small_g, blk * dq, dq, axis=1)
    kv_rows = d // dq

    def pack_opt(gains_like, taps_like, kv_like):
        rows = jnp.concatenate([gains_like.reshape(n_gain, dq), taps_like.reshape(n_tap, dq), kv_like.reshape(kv_rows, dq)])
        extra = -rows.shape[0] % 8
        return jnp.concatenate([rows, jnp.zeros((extra, dq), F32)]) if extra else rows

    sw = pack_opt(norm_g, conv_w, kv_norm_g)
    sm = pack_opt(m_norm_g, m_conv_w, m_kv_norm_g)
    sv = pack_opt(v_norm_g, v_conv_w, v_kv_norm_g)
    sg = pack_opt(mine_small[:n_gain], mine_small[n_gain:n_gain + n_tap], small_g[n_gain + n_tap])
    s_out = _adamw(sw, sm, sv, sg, "adamw_small")

    def unpack(a):
        return (a[:n_gain].reshape(depth, 4, dq), a[n_gain:n_gain + n_tap].reshape(n_a, -1, dq),
                a[n_gain + n_tap:n_gain + n_tap + kv_rows].reshape(d))

    small_out = [unpack(a) for a in s_out]

    stacks = _scatter_grads([(grads[name], ax) for name, ax in BIG], "scatter_grads")
    big_out = {}
    for (name, ax), (mine, sib) in zip(BIG, stacks):
        shp = big_w[name].shape
        rows, cols = shp[0] * shp[1], shp[2]
        flat = lambda a: a.reshape(rows, cols)
        res = _adamw(flat(big_w[name]), flat(big_m[name]), flat(big_v[name]),
                     (mine.reshape(4, rows, cols), sib.reshape(4, rows, cols)), f"adamw_{name}")
        out_shp = shp[1:] if name == "kv" else shp
        big_out[name] = [a.reshape(out_shp) for a in res]

    def leaves(i):
        ng, cw_, kg = small_out[i]
        return [ng, big_out["conv_in"][i], cw_, big_out["conv_out"][i], kg, big_out["kv"][i], big_out["q"][i],
                big_out["o"][i], big_out["ffn_in"][i], big_out["ffn_out"][i]]

    return (loss, dx[None], *leaves(0), *leaves(1), *leaves(2), *leaves(3))
```

```python
import functools

import jax
import jax.numpy as jnp
import numpy as np
from jax import lax
from jax.experimental import pallas as pl
from jax.experimental.pallas import tpu as pltpu

F32 = jnp.float32
BF16 = jnp.bfloat16
HEAD_DIM = 64
DILATIONS = (1, 4, 16)
NORM_EPS = 1e-6
NEG_BIG = -1e30
VMEM_LIMIT = 48 * 1024 * 1024
ROW_TILE = 256
LANE = 128
MESH = pl.DeviceIdType.MESH

ADAM_LR = 0.001
ADAM_B1 = 0.9
ADAM_B2 = 0.999
ADAM_EPS = 1e-08
ADAM_WD = 0.01
ADAM_STEP = 10

TILE_CANDIDATES = (1024, 1408, 768, 512, 384, 256, 128)


def _pick(dim, cands=TILE_CANDIDATES):
    for c in cands:
        if c <= dim and dim % c == 0:
            return c
    return dim


def _params(sem):
    return pltpu.CompilerParams(dimension_semantics=sem, vmem_limit_bytes=VMEM_LIMIT)


def _mm(a, b, mode, out_dtype, name, scale=None):
    if mode == "nn":
        m, k = a.shape
        n = b.shape[1]
    elif mode == "nt":
        m, k = a.shape
        n = b.shape[0]
    else:
        k, m = a.shape
        n = b.shape[1]
    tm, tn, tk = _pick(m), _pick(n), _pick(k)
    nk = k // tk
    if mode == "nn":
        a_spec = pl.BlockSpec((tm, tk), lambda i, j, kk: (i, kk))
        b_spec = pl.BlockSpec((tk, tn), lambda i, j, kk: (kk, j))
        dims = (((1,), (0,)), ((), ()))
    elif mode == "nt":
        a_spec = pl.BlockSpec((tm, tk), lambda i, j, kk: (i, kk))
        b_spec = pl.BlockSpec((tn, tk), lambda i, j, kk: (j, kk))
        dims = (((1,), (1,)), ((), ()))
    else:
        a_spec = pl.BlockSpec((tk, tm), lambda i, j, kk: (kk, i))
        b_spec = pl.BlockSpec((tk, tn), lambda i, j, kk: (kk, j))
        dims = (((0,), (0,)), ((), ()))

    def finish(acc):
        if scale is not None:
            acc = acc * scale
        return acc.astype(out_dtype)

    if nk == 1:
        def body(a_ref, b_ref, o_ref):
            o_ref[...] = finish(lax.dot_general(a_ref[...].astype(BF16), b_ref[...].astype(BF16), dims, preferred_element_type=F32))
        scratch = []
    else:
        def body(a_ref, b_ref, o_ref, acc_ref):
            kk = pl.program_id(2)

            @pl.when(kk == 0)
            def _():
                acc_ref[...] = jnp.zeros_like(acc_ref)

            acc_ref[...] += lax.dot_general(a_ref[...].astype(BF16), b_ref[...].astype(BF16), dims, preferred_element_type=F32)

            @pl.when(kk == nk - 1)
            def _():
                o_ref[...] = finish(acc_ref[...])
        scratch = [pltpu.VMEM((tm, tn), F32)]

    return pl.pallas_call(
        body, name=name,
        grid=(m // tm, n // tn, nk),
        in_specs=[a_spec, b_spec],
        out_specs=pl.BlockSpec((tm, tn), lambda i, j, kk: (i, j)),
        out_shape=jax.ShapeDtypeStruct((m, n), out_dtype),
        scratch_shapes=scratch,
        compiler_params=_params(("parallel", "parallel", "arbitrary")),
    )(a, b)


def _rstd(v):
    return lax.rsqrt(jnp.mean(v * v, axis=-1, keepdims=True) + NORM_EPS)


def _rms_bwd(dy, v, g, r):
    gy = dy * g
    dv = r * (gy - v * (r * r) * jnp.mean(gy * v, axis=-1, keepdims=True))
    return dv, dy * v * r


def _row_spec(t, width):
    return pl.BlockSpec((t, width), lambda i: (i, 0))


def _gain_spec(width):
    return pl.BlockSpec((1, width), lambda i: (0, 0))


def _norm_res_fwd(x, mix, g_post, pre_gains, name):
    s, d = x.shape
    t = _pick(s, (ROW_TILE,))
    has_mix = mix is not None
    n_pre = len(pre_gains)

    def body(*refs):
        x_ref = refs[0]
        pos = 1
        x1 = x_ref[...]
        if has_mix:
            mv = refs[1][...].astype(F32)
            x1 = x1 + mv * _rstd(mv) * refs[2][...]
            pos = 3
        gains = refs[pos:pos + n_pre]
        outs = refs[pos + n_pre:]
        if has_mix:
            outs[0][...] = x1
            outs = outs[1:]
        r = _rstd(x1)
        for g_ref, o_ref in zip(gains, outs):
            o_ref[...] = (x1 * r * g_ref[...]).astype(BF16)

    ins = [x] + ([mix, g_post] if has_mix else []) + list(pre_gains)
    in_specs = [_row_spec(t, d)] + ([_row_spec(t, d), _gain_spec(d)] if has_mix else []) + [_gain_spec(d)] * n_pre
    out_shape = ([jax.ShapeDtypeStruct((s, d), F32)] if has_mix else []) + [jax.ShapeDtypeStruct((s, d), BF16)] * n_pre
    out_specs = [_row_spec(t, d)] * len(out_shape)
    res = pl.pallas_call(
        body, name=name, grid=(s // t,), in_specs=in_specs, out_specs=out_specs, out_shape=out_shape,
        compiler_params=_params(("parallel",)),
    )(*ins)
    if has_mix:
        return res[0], list(res[1:])
    return x, list(res)


def _norm_res_loss(x, mix, g_post, target, name):
    s, d = x.shape
    t = _pick(s, (ROW_TILE,))

    def body(x_ref, m_ref, g_ref, t_ref, dy_ref, loss_ref):
        mv = m_ref[...].astype(F32)
        y = x_ref[...] + mv * _rstd(mv) * g_ref[...]
        err = y - t_ref[...]
        dy_ref[...] = err * (1.0 / d)

        @pl.when(pl.program_id(0) == 0)
        def _():
            loss_ref[...] = jnp.zeros_like(loss_ref)

        loss_ref[...] += jnp.sum(err * err)

    dy, acc = pl.pallas_call(
        body, name=name, grid=(s // t,),
        in_specs=[_row_spec(t, d), _row_spec(t, d), _gain_spec(d), _row_spec(t, d)],
        out_specs=[_row_spec(t, d), pl.BlockSpec((8, LANE), lambda i: (0, 0))],
        out_shape=[jax.ShapeDtypeStruct((s, d), F32), jax.ShapeDtypeStruct((8, LANE), F32)],
        compiler_params=_params(("arbitrary",)),
    )(x, mix, g_post, target)
    return dy, acc[0, 0] * (0.5 / d)


def _norm_bwd(dx_out, branches, x_in, post, name):
    s, d = dx_out.shape
    t = _pick(s, (ROW_TILE,))
    nb = len(branches)
    has_post = post is not None

    def body(*refs):
        dx_ref = refs[0]
        pos = 1
        dx = dx_ref[...]
        first = pl.program_id(0) == 0
        n_in = 1 + (1 + 2 * nb if nb else 0) + (2 if has_post else 0)
        outs = refs[n_in:]
        opos = 0
        if nb:
            xv = refs[pos][...]
            pos += 1
            r = _rstd(xv)
            dx_o = outs[0]
            opos = 1
            for _ in range(nb):
                dxn = refs[pos][...].astype(F32)
                g = refs[pos + 1][...]
                pos += 2
                dv, dg_rows = _rms_bwd(dxn, xv, g, r)
                dx = dx + dv
                dg_ref = outs[opos]
                opos += 1

                @pl.when(first)
                def _(dg_ref=dg_ref):
                    dg_ref[...] = jnp.zeros_like(dg_ref)

                dg_ref[...] += jnp.sum(dg_rows, axis=0, keepdims=True)
            dx_o[...] = dx
        if has_post:
            mv = refs[pos][...].astype(F32)
            g = refs[pos + 1][...]
            dm, dg_rows = _rms_bwd(dx, mv, g, _rstd(mv))
            outs[opos][...] = dm.astype(BF16)
            dg_ref = outs[opos + 1]

            @pl.when(first)
            def _():
                dg_ref[...] = jnp.zeros_like(dg_ref)

            dg_ref[...] += jnp.sum(dg_rows, axis=0, keepdims=True)

    ins, in_specs = [dx_out], [_row_spec(t, d)]
    out_shape, out_specs = [], []
    if nb:
        ins.append(x_in)
        in_specs.append(_row_spec(t, d))
        out_shape.append(jax.ShapeDtypeStruct((s, d), F32))
        out_specs.append(_row_spec(t, d))
        for dxn, g in branches:
            ins += [dxn, g]
            in_specs += [_row_spec(t, d), _gain_spec(d)]
            out_shape.append(jax.ShapeDtypeStruct((1, d), F32))
            out_specs.append(_gain_spec(d))
    if has_post:
        ins += [post[0], post[1]]
        in_specs += [_row_spec(t, d), _gain_spec(d)]
        out_shape += [jax.ShapeDtypeStruct((s, d), BF16), jax.ShapeDtypeStruct((1, d), F32)]
        out_specs += [_row_spec(t, d), _gain_spec(d)]
    res = pl.pallas_call(
        body, name=name, grid=(s // t,), in_specs=in_specs, out_specs=out_specs, out_shape=out_shape,
        compiler_params=_params(("arbitrary",)),
    )(*ins)
    res = list(res)
    dx_in = res.pop(0) if nb else dx_out
    dgs = [res.pop(0) for _ in range(nb)]
    dm, dg_post = (res[0], res[1]) if has_post else (None, None)
    return dx_in, dgs, dm, dg_post


HALO = 16


def _shift_down(u, prev, k):
    rows = lax.broadcasted_iota(jnp.int32, u.shape, 0)
    out = pltpu.roll(u, k, 0)
    for i in range(k):
        out = jnp.where(rows == i, prev[HALO - k + i:HALO - k + i + 1, :], out)
    return out


def _shift_up(u, nxt, k):
    n = u.shape[0]
    rows = lax.broadcasted_iota(jnp.int32, u.shape, 0)
    out = pltpu.roll(u, n - k, 0)
    for i in range(k):
        out = jnp.where(rows == n - k + i, nxt[i:i + 1, :], out)
    return out


def _conv_gate_fwd(p, cw, name):
    s, d3 = p.shape
    d = d3 // 3
    t = _pick(s, (ROW_TILE,))
    hb = t // HALO

    def body(p_ref, prev_ref, w_ref, z_ref):
        i = pl.program_id(0)
        pv = p_ref[...].astype(F32)
        b, u = pv[:, :d], pv[:, d:2 * d] * pv[:, 2 * d:]
        ph = prev_ref[...].astype(F32)
        up = jnp.where(i > 0, ph[:, d:2 * d] * ph[:, 2 * d:], 0.0)
        w = w_ref[...]
        y = w[0:1, :] * _shift_down(u, up, 2) + w[1:2, :] * _shift_down(u, up, 1) + w[2:3, :] * u
        z_ref[...] = (b * y).astype(BF16)

    return pl.pallas_call(
        body, name=name, grid=(s // t,),
        in_specs=[_row_spec(t, d3),
                  pl.BlockSpec((HALO, d3), lambda i: (jnp.maximum(i * hb - 1, 0), 0)),
                  pl.BlockSpec((3, d), lambda i: (0, 0))],
        out_specs=_row_spec(t, d),
        out_shape=jax.ShapeDtypeStruct((s, d), BF16),
        compiler_params=_params(("parallel",)),
    )(p, p, cw)


def _conv_gate_bwd(p, dz, cw, name):
    s, d3 = p.shape
    d = d3 // 3
    t = _pick(s, (ROW_TILE,))
    hb = t // HALO
    nt = s // t
    last_halo = s // HALO - 1

    def body(p_ref, prev_ref, next_ref, dz_ref, dznext_ref, w_ref, dp_ref, dw_ref):
        i = pl.program_id(0)
        pv = p_ref[...].astype(F32)
        b, c, h = pv[:, :d], pv[:, d:2 * d], pv[:, 2 * d:]
        u = c * h
        ph = prev_ref[...].astype(F32)
        up = jnp.where(i > 0, ph[:, d:2 * d] * ph[:, 2 * d:], 0.0)
        w = w_ref[...]
        u1, u2 = _shift_down(u, up, 1), _shift_down(u, up, 2)
        y = w[0:1, :] * u2 + w[1:2, :] * u1 + w[2:3, :] * u
        dz = dz_ref[...].astype(F32)
        dy = dz * b
        dyn = jnp.where(i < nt - 1, dznext_ref[...].astype(F32) * next_ref[...].astype(F32)[:, :d], 0.0)
        du = w[2:3, :] * dy + w[1:2, :] * _shift_up(dy, dyn, 1) + w[0:1, :] * _shift_up(dy, dyn, 2)
        dp_ref[:, :d] = (dz * y).astype(BF16)
        dp_ref[:, d:2 * d] = (du * h).astype(BF16)
        dp_ref[:, 2 * d:] = (du * c).astype(BF16)

        @pl.when(i == 0)
        def _():
            dw_ref[...] = jnp.zeros_like(dw_ref)

        dw_ref[0:1, :] += jnp.sum(dy * u2, axis=0, keepdims=True)
        dw_ref[1:2, :] += jnp.sum(dy * u1, axis=0, keepdims=True)
        dw_ref[2:3, :] += jnp.sum(dy * u, axis=0, keepdims=True)

    return pl.pallas_call(
        body, name=name, grid=(nt,),
        in_specs=[_row_spec(t, d3),
                  pl.BlockSpec((HALO, d3), lambda i: (jnp.maximum(i * hb - 1, 0), 0)),
                  pl.BlockSpec((HALO, d3), lambda i: (jnp.minimum((i + 1) * hb, last_halo), 0)),
                  _row_spec(t, d),
                  pl.BlockSpec((HALO, d), lambda i: (jnp.minimum((i + 1) * hb, last_halo), 0)),
                  pl.BlockSpec((3, d), lambda i: (0, 0))],
        out_specs=[_row_spec(t, d3), pl.BlockSpec((3, d), lambda i: (0, 0))],
        out_shape=[jax.ShapeDtypeStruct((s, d3), BF16), jax.ShapeDtypeStruct((3, d), F32)],
        compiler_params=_params(("arbitrary",)),
    )(p, p, p, dz, dz, cw)


def _swiglu_fwd(f, name):
    s, f2 = f.shape
    ff = f2 // 2
    t = _pick(s, (ROW_TILE,))

    def body(f_ref, a_ref):
        gate = f_ref[:, :ff].astype(F32)
        up = f_ref[:, ff:].astype(F32)
        a_ref[...] = (gate * jax.nn.sigmoid(gate) * up).astype(BF16)

    return pl.pallas_call(
        body, name=name, grid=(s // t,),
        in_specs=[_row_spec(t, f2)], out_specs=_row_spec(t, ff),
        out_shape=jax.ShapeDtypeStruct((s, ff), BF16),
        compiler_params=_params(("parallel",)),
    )(f)


def _swiglu_bwd(f, da, name):
    s, f2 = f.shape
    ff = f2 // 2
    t = _pick(s, (ROW_TILE,))

    def body(f_ref, da_ref, df_ref):
        gate = f_ref[:, :ff].astype(F32)
        up = f_ref[:, ff:].astype(F32)
        dav = da_ref[...].astype(F32)
        sg = jax.nn.sigmoid(gate)
        silu = gate * sg
        df_ref[:, :ff] = (dav * up * (sg + silu * (1.0 - sg))).astype(BF16)
        df_ref[:, ff:] = (dav * silu).astype(BF16)

    return pl.pallas_call(
        body, name=name, grid=(s // t,),
        in_specs=[_row_spec(t, f2), _row_spec(t, ff)], out_specs=_row_spec(t, f2),
        out_shape=jax.ShapeDtypeStruct((s, f2), BF16),
        compiler_params=_params(("parallel",)),
    )(f, da)


SUPER = 2048
RES = 16
PAIR = 128
L = 128


def _alibi_slopes(n_heads):
    h = np.arange(n_heads, dtype=np.float32) + 1.0
    return np.power(2.0, -8.0 * h / n_heads).astype(np.float32)


def _permute16(x, inverse, name):
    s, d = x.shape
    cw = LANE

    def body(x_ref, o_ref):
        if inverse:
            for m in range(L):
                o_ref[RES * m:RES * (m + 1), :] = x_ref[pl.ds(m, RES, stride=L), :]
        else:
            for r in range(RES):
                o_ref[L * r:L * (r + 1), :] = x_ref[pl.ds(r, L, stride=RES), :]

    spec = pl.BlockSpec((SUPER, cw), lambda i, j: (i, j))
    return pl.pallas_call(
        body, name=name, grid=(s // SUPER, d // cw), in_specs=[spec], out_specs=spec,
        out_shape=jax.ShapeDtypeStruct((s, d), x.dtype),
        compiler_params=_params(("parallel", "parallel")),
    )(x)


def _slope_table(d):
    nh = d // HEAD_DIM
    sl = _alibi_slopes(nh)
    tab = np.repeat(sl, HEAD_DIM).reshape(d // PAIR, 1, PAIR)
    return jnp.asarray(np.broadcast_to(tab, (d // PAIR, 8, PAIR)).copy())


def _geometry(dil):
    nch = RES // dil
    return nch, L // nch


def _band(dil):
    nch, w = _geometry(dil)
    sh = w.bit_length() - 1
    i = lax.broadcasted_iota(jnp.int32, (L, 2 * L), 0)
    j = lax.broadcasted_iota(jnp.int32, (L, 2 * L), 1)

    def pos(t):
        return jnp.bitwise_and(t, w - 1) * nch + jnp.right_shift(t, sh)

    delta = pos(i) + L - (pos(jnp.bitwise_and(j, L - 1)) + jnp.bitwise_and(j, L))
    return (delta * dil).astype(F32), (delta >= 0) & (delta <= L), j < L


def _offsets(dil, res, n):
    nch, w = _geometry(dil)

    def al(v):
        return v if isinstance(v, int) else pl.multiple_of(v, w)

    q_off = [al((a * dil + res) * L + n * w) for a in range(nch)]
    k_off = [al((a * dil + res) * 2 * L + L + n * w) for a in range(nch)]
    kp_off = [al((a * dil + res) * 2 * L + L + n * w - w) for a in range(nch)]
    return q_off, k_off, kp_off, w


def _gather(ref, offs, w):
    parts = [ref[pl.ds(o, w), :] for o in offs]
    return parts[0] if len(parts) == 1 else jnp.concatenate(parts, axis=0)


def _scatter(ref, offs, w, val, add=False):
    for a, o in enumerate(offs):
        piece = val[a * w:(a + 1) * w, :]
        if add:
            ref[pl.ds(o, w), :] += piece
        else:
            ref[pl.ds(o, w), :] = piece


def _fill_key_buffer(buf, prev_ref, cur_ref):
    for r in range(RES):
        buf[2 * L * r:2 * L * r + L, :] = prev_ref[L * r:L * (r + 1), :]
        buf[2 * L * r + L:2 * L * (r + 1), :] = cur_ref[L * r:L * (r + 1), :]


def _two_heads(x, low):
    zero = jnp.zeros_like(x)
    return jnp.concatenate([jnp.where(low, x, zero), jnp.where(low, zero, x)], axis=0)


def _loop_blocks(dil, do_block):
    nch, _ = _geometry(dil)
    if dil == RES:
        def it(r, c):
            do_block(r, 0)
            return c
        lax.fori_loop(0, RES, it, 0, unroll=2)
    else:
        def it(n, c):
            for res in range(dil):
                do_block(res, n)
            return c
        lax.fori_loop(0, nch, it, 0, unroll=2 if dil == 1 else 1)


NT = (((1,), (1,)), ((), ()))
TN = (((0,), (0,)), ((), ()))


def _attention_fwd(q, kv, name):
    s, d = q.shape
    g_n, ns = d // PAIR, s // SUPER

    def body(sl_ref, q_ref, kc_ref, kp_ref, vc_ref, vp_ref, o_ref, lse_ref, kbuf, vbuf, m_s, l_s, acc_s):
        sb = pl.program_id(1)
        _fill_key_buffer(kbuf, kp_ref, kc_ref)
        _fill_key_buffer(vbuf, vp_ref, vc_ref)
        low = lax.broadcasted_iota(jnp.int32, (L, PAIR), 1) < HEAD_DIM
        low_k = lax.broadcasted_iota(jnp.int32, (2 * L, PAIR), 1) < HEAD_DIM
        nsl = (-sl_ref[0:1, 0:1], -sl_ref[0:1, HEAD_DIM:HEAD_DIM + 1])
        for bi, dil in enumerate(DILATIONS):
            first_branch, last_branch = bi == 0, bi == len(DILATIONS) - 1
            base, band, prev_half = _band(dil)

            def do_block(res, n, dil=dil, first_branch=first_branch, last_branch=last_branch,
                         base=base, band=band, prev_half=prev_half):
                q_off, k_off, kp_off, w = _offsets(dil, res, n)
                qf = _gather(q_ref, q_off, w).astype(BF16)
                kcat = jnp.concatenate([_gather(kbuf, kp_off, w), _gather(kbuf, k_off, w)], axis=0).astype(BF16)
                vcat = jnp.concatenate([_gather(vbuf, kp_off, w), _gather(vbuf, k_off, w)], axis=0).astype(BF16)
                k_bd, v_bd = _two_heads(kcat, low_k), _two_heads(vcat, low_k)
                sc = lax.dot_general(qf, k_bd, NT, preferred_element_type=F32)
                starts_sequence = jnp.logical_and(sb == 0, n == 0)
                valid = band & jnp.logical_not(prev_half & starts_sequence)
                if not first_branch:
                    m_prev = _gather(m_s, q_off, w)
                ps, m_new, l_blk = [], [], []
                for h in range(2):
                    s_h = jnp.where(valid, sc[:, 2 * L * h:2 * L * (h + 1)] + nsl[h] * base, NEG_BIG)
                    mh = jnp.max(s_h, axis=1, keepdims=True)
                    if not first_branch:
                        mh = jnp.maximum(mh, m_prev[:, HEAD_DIM * h:HEAD_DIM * h + 1])
                    p_h = jnp.exp(s_h - mh)
                    ps.append(p_h.astype(BF16))
                    m_new.append(mh)
                    l_blk.append(jnp.sum(p_h, axis=1, keepdims=True))
                m_full = jnp.where(low, m_new[0], m_new[1])
                l_full = jnp.where(low, l_blk[0], l_blk[1])
                acc = jnp.dot(jnp.concatenate(ps, axis=1), v_bd, preferred_element_type=F32)
                if not first_branch:
                    alpha = jnp.exp(m_prev - m_full)
                    l_full = _gather(l_s, q_off, w) * alpha + l_full
                    acc = _gather(acc_s, q_off, w) * alpha + acc
                if last_branch:
                    _scatter(o_ref, q_off, w, acc / l_full)
                    _scatter(lse_ref, q_off, w, m_full + jnp.log(l_full))
                else:
                    _scatter(m_s, q_off, w, m_full)
                    _scatter(l_s, q_off, w, l_full)
                    _scatter(acc_s, q_off, w, acc)

            _loop_blocks(dil, do_block)

    prev = lambda i: jnp.maximum(i - 1, 0)
    blk = pl.BlockSpec((SUPER, PAIR), lambda g, i: (i, g))
    in_specs = [pl.BlockSpec((None, 8, PAIR), lambda g, i: (g, 0, 0)), blk,
                pl.BlockSpec((SUPER, PAIR), lambda g, i: (i, g)),
                pl.BlockSpec((SUPER, PAIR), lambda g, i: (prev(i), g)),
                pl.BlockSpec((SUPER, PAIR), lambda g, i: (i, g_n + g)),
                pl.BlockSpec((SUPER, PAIR), lambda g, i: (prev(i), g_n + g))]
    return pl.pallas_call(
        body, name=name, grid=(g_n, ns), in_specs=in_specs, out_specs=[blk, blk],
        out_shape=[jax.ShapeDtypeStruct((s, d), F32)] * 2,
        scratch_shapes=[pltpu.VMEM((2 * SUPER, PAIR), F32)] * 2 + [pltpu.VMEM((SUPER, PAIR), F32)] * 3,
        compiler_params=_params(("parallel", "parallel")),
    )(_slope_table(d), q, kv, kv, kv, kv)


def _attention_bwd(q, kv, o, do, lse, dkv_in, name):
    s, d = q.shape
    g_n, ns = d // PAIR, s // SUPER
    has_in = dkv_in is not None

    def body(*refs):
        sl_ref, q_ref, do_ref, o_ref, lse_ref, kc_ref, kp_ref, vc_ref, vp_ref = refs[:9]
        pos = 9
        if has_in:
            dk_in_ref, dv_in_ref = refs[9:11]
            pos = 11
        dq_ref, dk_ref, dv_ref, kbuf, vbuf, dkbuf, dvbuf, dq_s = refs[pos:]
        step = pl.program_id(1)
        sb = ns - 1 - step
        _fill_key_buffer(kbuf, kp_ref, kc_ref)
        _fill_key_buffer(vbuf, vp_ref, vc_ref)

        @pl.when(step == 0)
        def _():
            dkbuf[...] = jnp.zeros_like(dkbuf)
            dvbuf[...] = jnp.zeros_like(dvbuf)

        @pl.when(step > 0)
        def _():
            for buf in (dkbuf, dvbuf):
                for r in range(RES):
                    buf[2 * L * r + L:2 * L * (r + 1), :] = buf[2 * L * r:2 * L * r + L, :]
                    buf[2 * L * r:2 * L * r + L, :] = jnp.zeros((L, PAIR), F32)

        low = lax.broadcasted_iota(jnp.int32, (L, PAIR), 1) < HEAD_DIM
        low_k = lax.broadcasted_iota(jnp.int32, (2 * L, PAIR), 1) < HEAD_DIM
        nsl = (-sl_ref[0:1, 0:1], -sl_ref[0:1, HEAD_DIM:HEAD_DIM + 1])
        for bi, dil in enumerate(DILATIONS):
            first_branch = bi == 0
            base, band, prev_half = _band(dil)

            def do_block(res, n, dil=dil, first_branch=first_branch, base=base, band=band, prev_half=prev_half):
                q_off, k_off, kp_off, w = _offsets(dil, res, n)
                qb = _gather(q_ref, q_off, w).astype(BF16)
                dof = _gather(do_ref, q_off, w)
                prod = dof * _gather(o_ref, q_off, w)
                dob = dof.astype(BF16)
                lse_f = _gather(lse_ref, q_off, w)
                zero = jnp.zeros_like(prod)
                dsum = (jnp.sum(jnp.where(low, prod, zero), axis=1, keepdims=True),
                        jnp.sum(jnp.where(low, zero, prod), axis=1, keepdims=True))
                kcat = jnp.concatenate([_gather(kbuf, kp_off, w), _gather(kbuf, k_off, w)], axis=0).astype(BF16)
                vcat = jnp.concatenate([_gather(vbuf, kp_off, w), _gather(vbuf, k_off, w)], axis=0).astype(BF16)
                k_bd, v_bd = _two_heads(kcat, low_k), _two_heads(vcat, low_k)
                sc = lax.dot_general(qb, k_bd, NT, preferred_element_type=F32)
                dp = lax.dot_general(dob, v_bd, NT, preferred_element_type=F32)
                starts_sequence = jnp.logical_and(sb == 0, n == 0)
                valid = band & jnp.logical_not(prev_half & starts_sequence)
                ps, dss = [], []
                for h in range(2):
                    cols = slice(2 * L * h, 2 * L * (h + 1))
                    lse_h = lse_f[:, HEAD_DIM * h:HEAD_DIM * h + 1]
                    p_h = jnp.exp(jnp.where(valid, sc[:, cols] + nsl[h] * base - lse_h, NEG_BIG))
                    dss.append((p_h * (dp[:, cols] - dsum[h])).astype(BF16))
                    ps.append(p_h.astype(BF16))
                ds_cat, p_cat = jnp.concatenate(dss, axis=1), jnp.concatenate(ps, axis=1)
                dq = jnp.dot(ds_cat, k_bd, preferred_element_type=F32)
                dk_bd = lax.dot_general(ds_cat, qb, TN, preferred_element_type=F32)
                dv_bd = lax.dot_general(p_cat, dob, TN, preferred_element_type=F32)
                dk = jnp.where(low_k, dk_bd[:2 * L], dk_bd[2 * L:])
                dv = jnp.where(low_k, dv_bd[:2 * L], dv_bd[2 * L:])
                _scatter(dq_s, q_off, w, dq, add=not first_branch)
                _scatter(dkbuf, kp_off, w, dk[:L], add=True)
                _scatter(dkbuf, k_off, w, dk[L:], add=True)
                _scatter(dvbuf, kp_off, w, dv[:L], add=True)
                _scatter(dvbuf, k_off, w, dv[L:], add=True)

            _loop_blocks(dil, do_block)

        dq_ref[...] = dq_s[...].astype(BF16)
        for r in range(RES):
            rows, cur = slice(L * r, L * (r + 1)), slice(2 * L * r + L, 2 * L * (r + 1))
            if has_in:
                dk_ref[rows, :] = dkbuf[cur, :] + dk_in_ref[rows, :]
                dv_ref[rows, :] = dvbuf[cur, :] + dv_in_ref[rows, :]
            else:
                dk_ref[rows, :] = dkbuf[cur, :]
                dv_ref[rows, :] = dvbuf[cur, :]

    rev = lambda i: ns - 1 - i
    prev = lambda i: jnp.maximum(ns - 2 - i, 0)
    blk = pl.BlockSpec((SUPER, PAIR), lambda g, i: (rev(i), g))
    in_specs = [pl.BlockSpec((None, 8, PAIR), lambda g, i: (g, 0, 0)), blk, blk, blk, blk,
                pl.BlockSpec((SUPER, PAIR), lambda g, i: (rev(i), g)),
                pl.BlockSpec((SUPER, PAIR), lambda g, i: (prev(i), g)),
                pl.BlockSpec((SUPER, PAIR), lambda g, i: (rev(i), g_n + g)),
                pl.BlockSpec((SUPER, PAIR), lambda g, i: (prev(i), g_n + g))]
    ins = [_slope_table(d), q, do, o, lse, kv, kv, kv, kv]
    if has_in:
        in_specs += [blk, blk]
        ins += list(dkv_in)
    res = pl.pallas_call(
        body, name=name, grid=(g_n, ns), in_specs=in_specs, out_specs=[blk, blk, blk],
        out_shape=[jax.ShapeDtypeStruct((s, d), BF16)] + [jax.ShapeDtypeStruct((s, d), F32)] * 2,
        scratch_shapes=[pltpu.VMEM((2 * SUPER, PAIR), F32)] * 4 + [pltpu.VMEM((SUPER, PAIR), F32)],
        compiler_params=_params(("parallel", "arbitrary")),
    )(*ins)
    return res[0], (res[1], res[2])


def _coords():
    return lax.axis_index("x"), lax.axis_index("y"), lax.axis_index("c")


def _chip_peers(x, y):
    return [(1 - x, y), (x, 1 - y), (1 - x, 1 - y)]


def _block_of(ref, axis, blk, size):
    start = pl.multiple_of(blk * size, size)
    if axis == 1:
        return ref.at[:, pl.ds(start, size)]
    return ref.at[pl.ds(start, size), :]


ANY = pl.BlockSpec(memory_space=pl.ANY)


HBM = pl.BlockSpec(memory_space=pltpu.HBM)
SEM = pl.BlockSpec(memory_space=pltpu.SEMAPHORE)
SPLIT = pltpu.CompilerParams(has_side_effects=pltpu.SideEffectType.DATAFLOW_SIDE_EFFECTING)


def _in_hbm(a):
    return pltpu.with_memory_space_constraint(a, pltpu.HBM)


def _thru(arrays):
    return [pltpu.HBM(a.shape, a.dtype) for a in arrays]


def _gather_start(groups, name):
    flat = [item for grp in groups for item in grp]
    n, ng = len(flat), len(groups)
    first = [sum(len(g) for g in groups[:gi]) for gi in range(ng + 1)]

    def body(*refs):
        blocks, fulls = refs[:n], refs[n:2 * n]
        sends, recvs = refs[2 * n:2 * n + ng], refs[2 * n + ng:2 * n + 2 * ng]
        local_sems = refs[-1]
        x, y, c = _coords()
        local = []
        for gi in range(ng):
            for ti in range(first[gi], first[gi + 1]):
                ax = flat[ti][1]
                mine = _block_of(fulls[ti], ax, 2 * x + y, flat[ti][0].shape[ax])
                cp = pltpu.make_async_copy(blocks[ti], mine, local_sems.at[ti])
                cp.start()
                local.append(cp)
                for j, (px, py) in enumerate(_chip_peers(x, y)):
                    k = 3 * (ti - first[gi]) + j
                    pltpu.make_async_remote_copy(
                        src_ref=blocks[ti], dst_ref=mine, send_sem=sends[gi].at[k], recv_sem=recvs[gi].at[k],
                        device_id=(px, py, c), device_id_type=MESH).start()
        for cp in local:
            cp.wait()

    blocks = [b for b, _ in flat]
    fulls = []
    for b, ax in flat:
        shape = list(b.shape)
        shape[ax] *= 4
        fulls.append(lax.empty(tuple(shape), b.dtype))
    sems = [pltpu.SemaphoreType.DMA((3 * len(g),)) for g in groups]
    res = pl.pallas_call(
        body, name=name, in_specs=[HBM] * (2 * n), out_specs=[SEM] * (2 * ng) + [HBM] * (2 * n),
        out_shape=sems + sems + _thru(blocks) + _thru(fulls),
        input_output_aliases={i: 2 * ng + i for i in range(2 * n)},
        scratch_shapes=[pltpu.SemaphoreType.DMA((n,))],
        compiler_params=SPLIT,
    )(*[_in_hbm(a) for a in blocks + fulls])
    sends, recvs, thru = res[:ng], res[ng:2 * ng], res[2 * ng:]
    return [(sends[gi], recvs[gi], list(thru[first[gi]:first[gi + 1]]), list(thru[n + first[gi]:n + first[gi + 1]]))
            for gi in range(ng)]


def _gather_wait(group, started, after, name):
    sends, recvs, blocks, fulls = started
    m = len(group)

    def body(*refs):
        blk_refs, full_refs, send_sem, recv_sem = refs[:m], refs[m:2 * m], refs[2 * m], refs[2 * m + 1]
        x, y, c = _coords()
        for t, (b, ax) in enumerate(group):
            for j, (px, py) in enumerate(_chip_peers(x, y)):
                cp = pltpu.make_async_remote_copy(
                    src_ref=blk_refs[t], dst_ref=_block_of(full_refs[t], ax, 2 * px + py, b.shape[ax]),
                    send_sem=send_sem.at[3 * t + j], recv_sem=recv_sem.at[3 * t + j],
                    device_id=(px, py, c), device_id_type=MESH)
                cp.wait_send()
                cp.wait_recv()

    extra = [] if after is None else [after]
    res = pl.pallas_call(
        body, name=name, in_specs=[HBM] * (2 * m) + [SEM, SEM] + [ANY] * len(extra), out_specs=[HBM] * (2 * m),
        out_shape=_thru(blocks) + _thru(fulls), input_output_aliases={i: i for i in range(2 * m)},
        compiler_params=SPLIT,
    )(*blocks, *fulls, sends, recvs, *extra)
    return list(res[m:])


def _scatter_start(grads, name):
    n = len(grads)

    def body(*refs):
        g_refs, st_refs, send_sem, recv_sem, local_sems = refs[:n], refs[n:2 * n], refs[2 * n], refs[2 * n + 1], refs[-1]
        x, y, c = _coords()
        local = []
        for t, (g, ax) in enumerate(grads):
            size = g.shape[ax] // 4
            cp = pltpu.make_async_copy(_block_of(g_refs[t], ax, 2 * x + y, size), st_refs[t].at[0], local_sems.at[t])
            cp.start()
            local.append(cp)
            for j, (px, py) in enumerate(_chip_peers(x, y)):
                pltpu.make_async_remote_copy(
                    src_ref=_block_of(g_refs[t], ax, 2 * px + py, size), dst_ref=st_refs[t].at[1 + j],
                    send_sem=send_sem.at[3 * t + j], recv_sem=recv_sem.at[3 * t + j],
                    device_id=(px, py, c), device_id_type=MESH).start()
        for cp in local:
            cp.wait()

    full = [g for g, _ in grads]
    stacks = []
    for g, ax in grads:
        shape = list(g.shape)
        shape[ax] //= 4
        stacks.append(lax.empty((4, *shape), g.dtype))
    sems = [pltpu.SemaphoreType.DMA((3 * n,))] * 2
    res = pl.pallas_call(
        body, name=name, in_specs=[HBM] * (2 * n), out_specs=[SEM, SEM] + [HBM] * (2 * n),
        out_shape=sems + _thru(full) + _thru(stacks), input_output_aliases={i: 2 + i for i in range(2 * n)},
        scratch_shapes=[pltpu.SemaphoreType.DMA((n,))],
        compiler_params=SPLIT,
    )(*[_in_hbm(a) for a in full + stacks])
    return res[0], res[1], list(res[2:2 + n]), list(res[2 + n:])


def _scatter_wait(axes, started, name):
    sends, recvs, full, stacks = started
    n = len(full)

    def body(*refs):
        g_refs, st_refs, send_sem, recv_sem = refs[:n], refs[n:2 * n], refs[2 * n], refs[2 * n + 1]
        x, y, c = _coords()
        for t, ax in enumerate(axes):
            size = full[t].shape[ax] // 4
            for j, (px, py) in enumerate(_chip_peers(x, y)):
                cp = pltpu.make_async_remote_copy(
                    src_ref=_block_of(g_refs[t], ax, 2 * px + py, size), dst_ref=st_refs[t].at[1 + j],
                    send_sem=send_sem.at[3 * t + j], recv_sem=recv_sem.at[3 * t + j],
                    device_id=(px, py, c), device_id_type=MESH)
                cp.wait_send()
                cp.wait_recv()

    res = pl.pallas_call(
        body, name=name, in_specs=[HBM] * (2 * n) + [SEM, SEM], out_specs=[HBM] * (2 * n),
        out_shape=_thru(full) + _thru(stacks), input_output_aliases={i: i for i in range(2 * n)},
        compiler_params=SPLIT,
    )(*full, *stacks, sends, recvs)
    return list(res[n:])


def _pair_exchange(stacks, name):
    n = len(stacks)

    def body(*refs):
        ins, outs, send_sems, recv_sems = refs[:n], refs[n:2 * n], refs[2 * n], refs[2 * n + 1]
        x, y, c = _coords()
        copies = []
        for t in range(n):
            cp = pltpu.make_async_remote_copy(
                src_ref=ins[t], dst_ref=outs[t], send_sem=send_sems.at[t], recv_sem=recv_sems.at[t],
                device_id=(x, y, 1 - c), device_id_type=MESH)
            cp.start()
            copies.append(cp)
        for cp in copies:
            cp.wait()

    return pl.pallas_call(
        body, name=name, in_specs=[ANY] * n, out_specs=[ANY] * n,
        out_shape=[jax.ShapeDtypeStruct(s.shape, s.dtype) for s in stacks],
        scratch_shapes=[pltpu.SemaphoreType.DMA((n,)), pltpu.SemaphoreType.DMA((n,))],
        compiler_params=pltpu.CompilerParams(has_side_effects=True),
    )(*stacks)


def _allreduce_small(v, name):
    r, cdim = v.shape

    def body(v_ref, out_ref, buf, send_sems, recv_sems):
        x, y, c = _coords()
        me = 4 * x + 2 * y + c
        buf[0] = v_ref[...]
        sends = []
        for k in range(1, 8):
            peer = (x if not (k & 4) else 1 - x, y if not (k & 2) else 1 - y, c if not (k & 1) else 1 - c)
            cp = pltpu.make_async_remote_copy(
                src_ref=v_ref, dst_ref=buf.at[k], send_sem=send_sems.at[k - 1], recv_sem=recv_sems.at[k - 1],
                device_id=peer, device_id_type=MESH)
            cp.start()
            sends.append(cp)
        for cp in sends:
            cp.wait_recv()
        total = buf[me]
        for src in range(1, 8):
            total = total + buf[jnp.bitwise_xor(me, src)]
        out_ref[...] = total
        for cp in sends:
            cp.wait_send()

    return pl.pallas_call(
        body, name=name,
        in_specs=[pl.BlockSpec(memory_space=pltpu.VMEM)], out_specs=pl.BlockSpec(memory_space=pltpu.VMEM),
        out_shape=jax.ShapeDtypeStruct((r, cdim), F32),
        scratch_shapes=[pltpu.VMEM((8, r, cdim), F32), pltpu.SemaphoreType.DMA((7,)), pltpu.SemaphoreType.DMA((7,))],
        compiler_params=pltpu.CompilerParams(has_side_effects=True),
    )(v)


def _adamw_math(w, g, m, v):
    m = ADAM_B1 * m + (1.0 - ADAM_B1) * g
    v = ADAM_B2 * v + (1.0 - ADAM_B2) * jnp.square(g)
    m_hat = m / (1.0 - ADAM_B1 ** ADAM_STEP)
    v_hat = v / (1.0 - ADAM_B2 ** ADAM_STEP)
    delta = -ADAM_LR * (m_hat / (jnp.sqrt(v_hat) + ADAM_EPS) + ADAM_WD * w)
    return delta, m, v


def _adamw(w, m, v, grads, name):
    r, cdim = w.shape
    paired = isinstance(grads, list)
    layers = len(grads) if paired else 1
    t = _pick(r // layers, (64, 32, 16, 8))
    per_layer = r // layers // t

    def body(*refs):
        w_ref, m_ref, v_ref = refs[:3]
        outs = refs[3 + (2 * layers if paired else 1):]

        def update(g):
            delta, m_new, v_new = _adamw_math(w_ref[...], g, m_ref[...], v_ref[...])
            outs[0][...] = g
            outs[1][...] = delta
            outs[2][...] = m_new
            outs[3][...] = v_new

        if not paired:
            update(refs[3][...])
            return
        layer = pl.program_id(0) // per_layer
        for l in range(layers):
            @pl.when(layer == l)
            def _(a_ref=refs[3 + 2 * l], b_ref=refs[4 + 2 * l]):
                sa = a_ref[0].astype(F32)
                sb = b_ref[0].astype(F32)
                for k in range(1, 4):
                    sa = sa + a_ref[k].astype(F32)
                    sb = sb + b_ref[k].astype(F32)
                update(sa + sb)

    spec = pl.BlockSpec((t, cdim), lambda i: (i, 0))
    if paired:
        ins = [w, m, v] + [s for pair in grads for s in pair]
        stack_specs = [pl.BlockSpec((4, t, cdim), lambda i, l=l: (0, jnp.clip(i - l * per_layer, 0, per_layer - 1), 0))
                       for l in range(layers) for _ in range(2)]
        in_specs = [spec] * 3 + stack_specs
    else:
        ins = [w, m, v, grads]
        in_specs = [spec] * 4
    return pl.pallas_call(
        body, name=name, grid=(r // t,), in_specs=in_specs, out_specs=[spec] * 4,
        out_shape=[jax.ShapeDtypeStruct((r, cdim), F32)] * 4,
        compiler_params=_params(("parallel",)),
    )(*ins)


def _local_step(x, target, gains, conv_ws, kv_gain, weights_of, send_grads):
    depth = len(gains)
    n_a = len(conv_ws)
    saved, ws = [], []
    kv = kvn = None
    _, (xn,) = _norm_res_fwd(x, None, None, [gains[0][0]], "norm_first")
    h = x
    for l in range(depth):
        g = gains[l]
        sv = {"x_in": h, "xn": xn}
        w = weights_of(l, h)
        ws.append(w)
        if l == n_a:
            kv = _mm(kvn, w["kv"], "nn", F32, "kv_fwd")
        if l < n_a:
            p = _mm(xn, w["conv_in"], "nn", BF16, f"conv_in_fwd_{l}")
            z = _conv_gate_fwd(p, conv_ws[l], f"conv_gate_fwd_{l}")
            mix = _mm(z, w["conv_out"], "nn", BF16, f"conv_out_fwd_{l}")
            sv.update(p=p, z=z)
        else:
            j = l - n_a
            q = _mm(xn, w["q"], "nn", F32, f"q_fwd_{j}", scale=HEAD_DIM ** -0.5)
            o, lse = _attention_fwd(q, kv, f"attn_fwd_{j}")
            mix = _mm(o, w["o"], "nn", BF16, f"o_fwd_{j}")
            sv.update(q=q, o=o, lse=lse)
        x1, (xn2,) = _norm_res_fwd(h, mix, g[1], [g[2]], f"norm_mid_{l}")
        f = _mm(xn2, w["ffn_in"], "nn", BF16, f"ffn_in_fwd_{l}")
        a = _swiglu_fwd(f, f"swiglu_fwd_{l}")
        ff = _mm(a, w["ffn_out"], "nn", BF16, f"ffn_out_fwd_{l}")
        sv.update(mix=mix, x1=x1, xn2=xn2, f=f, a=a, ff=ff)
        saved.append(sv)
        if l == depth - 1:
            dx, loss = _norm_res_loss(x1, ff, g[3], target, "norm_loss")
        else:
            if l == n_a - 1:
                h, _ = _norm_res_fwd(x1, ff, g[3], [], f"norm_end_{l}")
                h = _permute16(h, False, "permute_stream")
                target = _permute16(target, False, "permute_target")
                _, (xn, kvn) = _norm_res_fwd(h, None, None, [gains[l + 1][0], kv_gain], "norm_permuted")
            else:
                h, (xn,) = _norm_res_fwd(x1, ff, g[3], [gains[l + 1][0]], f"norm_end_{l}")
    d_gains = [[None] * 4 for _ in range(depth)]
    d_conv = [None] * n_a
    d_kv_gain = None
    dkv = None
    _, _, dff, d_gains[depth - 1][3] = _norm_bwd(dx, [], None, (saved[-1]["ff"], gains[-1][3]), "norm_loss_bwd")
    for l in reversed(range(depth)):
        sv, g, w, grads = saved[l], gains[l], ws[l], {}
        da = _mm(dff, w["ffn_out"], "nt", BF16, f"ffn_out_dx_{l}")
        grads["ffn_out"] =_mm(sv["a"], dff, "tn", BF16, f"ffn_out_dw_{l}")
        df = _swiglu_bwd(sv["f"], da, f"swiglu_bwd_{l}")
        dxn2 = _mm(df, w["ffn_in"], "nt", BF16, f"ffn_in_dx_{l}")
        grads["ffn_in"] =_mm(sv["xn2"], df, "tn", BF16, f"ffn_in_dw_{l}")
        dx, (d_gains[l][2],), dmix, d_gains[l][1] = _norm_bwd(
            dx, [(dxn2, g[2])], sv["x1"], (sv["mix"], g[1]), f"norm_mid_bwd_{l}")
        if l < n_a:
            dz = _mm(dmix, w["conv_out"], "nt", BF16, f"conv_out_dx_{l}")
            grads["conv_out"] =_mm(sv["z"], dmix, "tn", BF16, f"conv_out_dw_{l}")
            dp, d_conv[l] = _conv_gate_bwd(sv["p"], dz, conv_ws[l], f"conv_gate_bwd_{l}")
            dxn = _mm(dp, w["conv_in"], "nt", BF16, f"conv_in_dx_{l}")
            grads["conv_in"] =_mm(sv["xn"], dp, "tn", BF16, f"conv_in_dw_{l}")
        else:
            j = l - n_a
            do = _mm(dmix, w["o"], "nt", F32, f"o_dx_{j}")
            grads["o"] =_mm(sv["o"], dmix, "tn", BF16, f"o_dw_{j}")
            dq, dkv = _attention_bwd(sv["q"], kv, sv["o"], do, sv["lse"], dkv, f"attn_bwd_{j}")
            scale = HEAD_DIM ** -0.5
            dxn = _mm(dq, w["q"], "nt", BF16, f"q_dx_{j}", scale=scale)
            grads["q"] =_mm(sv["xn"], dq, "tn", BF16, f"q_dw_{j}", scale=scale)
        branches = [(dxn, g[0])]
        if l == n_a:
            dkv_cat = jnp.concatenate([dkv[0], dkv[1]], axis=1).astype(BF16)
            dkvn = _mm(dkv_cat, w["kv"], "nt", BF16, "kv_dx")
            grads["kv"] =_mm(kvn, dkv_cat, "tn", BF16, "kv_dw")
            branches.append((dkvn, kv_gain))
        post = (saved[l - 1]["ff"], gains[l - 1][3]) if l > 0 else None
        if l == n_a:
            dx, dgs, _, _ = _norm_bwd(dx, branches, sv["x_in"], None, f"norm_end_bwd_{l}")
            dx = _permute16(dx, True, "unpermute_stream")
            _, _, dff, dg_post = _norm_bwd(dx, [], None, post, "norm_boundary_bwd")
        else:
            dx, dgs, dff, dg_post = _norm_bwd(dx, branches, sv["x_in"], post, f"norm_end_bwd_{l}")
        send_grads(l, grads)
        d_gains[l][0] = dgs[0]
        if l == n_a:
            d_kv_gain = dgs[1]
        if l > 0:
            d_gains[l - 1][3] = dg_post
    return loss, dx, d_gains, d_conv, d_kv_gain


BIG = (
    ("conv_in", 1), ("conv_out", 0), ("kv", 1), ("q", 0), ("o", 0), ("ffn_in", 1), ("ffn_out", 0))


def kernel(x, norm_g, conv_in_w, conv_w, conv_out_w, kv_norm_g, kv_w, q_w, o_w, ffn_in_w, ffn_out_w, loss_target, m_norm_g, m_conv_in_w, m_conv_w, m_conv_out_w, m_kv_norm_g, m_kv_w, m_q_w, m_o_w, m_ffn_in_w, m_ffn_out_w, v_norm_g, v_conv_in_w, v_conv_w, v_conv_out_w, v_kv_norm_g, v_kv_w, v_q_w, v_o_w, v_ffn_in_w, v_ffn_out_w):
    depth, _, dq = norm_g.shape
    d = 4 * dq
    n_a = conv_w.shape[0]
    big_w = {"conv_in": conv_in_w, "conv_out": conv_out_w, "kv": kv_w[None], "q": q_w, "o": o_w,
             "ffn_in": ffn_in_w, "ffn_out": ffn_out_w}
    big_m = {"conv_in": m_conv_in_w, "conv_out": m_conv_out_w, "kv": m_kv_w[None], "q": m_q_w, "o": m_o_w,
             "ffn_in": m_ffn_in_w, "ffn_out": m_ffn_out_w}
    big_v = {"conv_in": v_conv_in_w, "conv_out": v_conv_out_w, "kv": v_kv_w[None], "q": v_q_w, "o": v_o_w,
             "ffn_in": v_ffn_in_w, "ffn_out": v_ffn_out_w}

    n_gain, n_tap = depth * 4, n_a * conv_w.shape[1]
    small_rows = -(-(n_gain + n_tap + 1) // 8) * 8
    pad_rows = small_rows - n_gain - n_tap

    def pack_small(gains, taps):
        return jnp.concatenate([gains.reshape(n_gain, dq), taps.reshape(n_tap, dq), jnp.zeros((pad_rows, dq), F32)])

    axis_of = dict(BIG)

    def matrices_of(l):
        if l < n_a:
            return [("conv_in", l), ("conv_out", l), ("ffn_in", l), ("ffn_out", l)]
        return ([("kv", 0)] if l == n_a else []) + [("q", l - n_a), ("o", l - n_a), ("ffn_in", l), ("ffn_out", l)]

    blocks = {name: big_w[name].astype(BF16) for name, _ in BIG}
    groups = [[(blocks[name][i], axis_of[name]) for name, i in matrices_of(l)] for l in range(depth)]
    groups[0].append((pack_small(norm_g, conv_w), 1))
    started = _gather_start(groups, "gather_start")
    first = _gather_wait(groups[0], started[0], None, "gather_wait_0")
    small = first[-1]
    gains = [[small[4 * l + i][None] for i in range(4)] for l in range(depth)]
    conv_ws = [small[n_gain + 3 * l:n_gain + 3 * l + 3] for l in range(n_a)]
    kv_gain = kv_norm_g[None]

    def weights_of(l, after):
        full = first if l == 0 else _gather_wait(groups[l], started[l], after, f"gather_wait_{l}")
        return {name: full[t] for t, (name, _) in enumerate(matrices_of(l))}

    sent = {}

    def send_grads(l, grads):
        sent[l] = _scatter_start([(grads[name], axis_of[name]) for name, _ in matrices_of(l)], f"scatter_start_{l}")

    loss, dx, d_gains, d_conv, d_kv_gain = _local_step(
        x[0], loss_target[0], gains, conv_ws, kv_gain, weights_of, send_grads)
    loss = lax.psum(loss, ("x", "y", "c"))

    small_g = jnp.concatenate([dg for row in d_gains for dg in row] + list(d_conv) + [d_kv_gain]
                              + [jnp.zeros((pad_rows - 1, d), F32)])
    small_g = _allreduce_small(small_g, "allreduce_small")
    blk = 2 * lax.axis_index("x") + lax.axis_index("y")
    mine_small = lax.dynamic_slice_in_dim(small_g, blk * dq, dq, axis=1)
    kv_rows = d // dq

    def pack_opt(gains_like, taps_like, kv_like):
        rows = jnp.concatenate([gains_like.reshape(n_gain, dq), taps_like.reshape(n_tap, dq), kv_like.reshape(kv_rows, dq)])
        extra = -rows.shape[0] % 8
        return jnp.concatenate([rows, jnp.zeros((extra, dq), F32)]) if extra else rows

    sw = pack_opt(norm_g, conv_w, kv_norm_g)
    sm = pack_opt(m_norm_g, m_conv_w, m_kv_norm_g)
    sv = pack_opt(v_norm_g, v_conv_w, v_kv_norm_g)
    sg = pack_opt(mine_small[:n_gain], mine_small[n_gain:n_gain + n_tap], small_g[n_gain + n_tap])
    s_out = _adamw(sw, sm, sv, sg, "adamw_small")

    def unpack(a):
        return (a[:n_gain].reshape(depth, 4, dq), a[n_gain:n_gain + n_tap].reshape(n_a, -1, dq),
                a[n_gain + n_tap:n_gain + n_tap + kv_rows].reshape(d))

    small_out = [unpack(a) for a in s_out]

    mine = {}
    for l in reversed(range(depth)):
        got = _scatter_wait([axis_of[name] for name, _ in matrices_of(l)], sent[l], f"scatter_wait_{l}")
        mine.update(zip(matrices_of(l), got))
    order = [(name, i) for name, _ in BIG for i in range(big_w[name].shape[0])]
    sibling = dict(zip(order, _pair_exchange([mine[k] for k in order], "pair_exchange")))
    big_out = {}
    for name, _ in BIG:
        shp = big_w[name].shape
        rows, cols = shp[0] * shp[1], shp[2]
        flat = lambda a: a.reshape(rows, cols)
        res = _adamw(flat(big_w[name]), flat(big_m[name]), flat(big_v[name]),
                     [(mine[name, i], sibling[name, i]) for i in range(shp[0])], f"adamw_{name}")
        out_shp = shp[1:] if name == "kv" else shp
        big_out[name] = [a.reshape(out_shp) for a in res]

    def leaves(i):
        ng, cw_, kg = small_out[i]
        return [ng, big_out["conv_in"][i], cw_, big_out["conv_out"][i], kg, big_out["kv"][i], big_out["q"][i],
                big_out["o"][i], big_out["ffn_in"][i], big_out["ffn_out"][i]]

    return (loss, dx[None], *leaves(0), *leaves(1), *leaves(2), *leaves(3))
```

```python
import functools

import jax
import jax.numpy as jnp
import numpy as np
from jax import lax
from jax.experimental import pallas as pl
from jax.experimental.pallas import tpu as pltpu

F32 = jnp.float32
BF16 = jnp.bfloat16
HEAD_DIM = 64
DILATIONS = (1, 4, 16)
NORM_EPS = 1e-6
NEG_BIG = -1e30
VMEM_LIMIT = 48 * 1024 * 1024
ROW_TILE = 256
LANE = 128
MESH = pl.DeviceIdType.MESH

ADAM_LR = 0.001
ADAM_B1 = 0.9
ADAM_B2 = 0.999
ADAM_EPS = 1e-08
ADAM_WD = 0.01
ADAM_STEP = 10

TILE_CANDIDATES = (1024, 1408, 768, 512, 384, 256, 128)


def _pick(dim, cands=TILE_CANDIDATES):
    for c in cands:
        if c <= dim and dim % c == 0:
            return c
    return dim


def _params(sem):
    return pltpu.CompilerParams(dimension_semantics=sem, vmem_limit_bytes=VMEM_LIMIT)


def _mm(a, b, mode, out_dtype, name, scale=None):
    if mode == "nn":
        m, k = a.shape
        n = b.shape[1]
    elif mode == "nt":
        m, k = a.shape
        n = b.shape[0]
    else:
        k, m = a.shape
        n = b.shape[1]
    tm, tn, tk = _pick(m), _pick(n), _pick(k)
    nk = k // tk
    if mode == "nn":
        a_spec = pl.BlockSpec((tm, tk), lambda i, j, kk: (i, kk))
        b_spec = pl.BlockSpec((tk, tn), lambda i, j, kk: (kk, j))
        dims = (((1,), (0,)), ((), ()))
    elif mode == "nt":
        a_spec = pl.BlockSpec((tm, tk), lambda i, j, kk: (i, kk))
        b_spec = pl.BlockSpec((tn, tk), lambda i, j, kk: (j, kk))
        dims = (((1,), (1,)), ((), ()))
    else:
        a_spec = pl.BlockSpec((tk, tm), lambda i, j, kk: (kk, i))
        b_spec = pl.BlockSpec((tk, tn), lambda i, j, kk: (kk, j))
        dims = (((0,), (0,)), ((), ()))

    def finish(acc):
        if scale is not None:
            acc = acc * scale
        return acc.astype(out_dtype)

    if nk == 1:
        def body(a_ref, b_ref, o_ref):
            o_ref[...] = finish(lax.dot_general(a_ref[...].astype(BF16), b_ref[...].astype(BF16), dims, preferred_element_type=F32))
        scratch = []
    else:
        def body(a_ref, b_ref, o_ref, acc_ref):
            kk = pl.program_id(2)

            @pl.when(kk == 0)
            def _():
                acc_ref[...] = jnp.zeros_like(acc_ref)

            acc_ref[...] += lax.dot_general(a_ref[...].astype(BF16), b_ref[...].astype(BF16), dims, preferred_element_type=F32)

            @pl.when(kk == nk - 1)
            def _():
                o_ref[...] = finish(acc_ref[...])
        scratch = [pltpu.VMEM((tm, tn), F32)]

    return pl.pallas_call(
        body, name=name,
        grid=(m // tm, n // tn, nk),
        in_specs=[a_spec, b_spec],
        out_specs=pl.BlockSpec((tm, tn), lambda i, j, kk: (i, j)),
        out_shape=jax.ShapeDtypeStruct((m, n), out_dtype),
        scratch_shapes=scratch,
        compiler_params=_params(("parallel", "parallel", "arbitrary")),
    )(a, b)


def _rstd(v):
    return lax.rsqrt(jnp.mean(v * v, axis=-1, keepdims=True) + NORM_EPS)


def _rms_bwd(dy, v, g, r):
    gy = dy * g
    dv = r * (gy - v * (r * r) * jnp.mean(gy * v, axis=-1, keepdims=True))
    return dv, dy * v * r


def _row_spec(t, width):
    return pl.BlockSpec((t, width), lambda i: (i, 0))


def _gain_spec(width):
    return pl.BlockSpec((1, width), lambda i: (0, 0))


def _norm_res_fwd(x, mix, g_post, pre_gains, name):
    s, d = x.shape
    t = _pick(s, (ROW_TILE,))
    has_mix = mix is not None
    n_pre = len(pre_gains)

    def body(*refs):
        x_ref = refs[0]
        pos = 1
        x1 = x_ref[...]
        if has_mix:
            mv = refs[1][...].astype(F32)
            x1 = x1 + mv * _rstd(mv) * refs[2][...]
            pos = 3
        gains = refs[pos:pos + n_pre]
        outs = refs[pos + n_pre:]
        if has_mix:
            outs[0][...] = x1
            outs = outs[1:]
        r = _rstd(x1)
        for g_ref, o_ref in zip(gains, outs):
            o_ref[...] = (x1 * r * g_ref[...]).astype(BF16)

    ins = [x] + ([mix, g_post] if has_mix else []) + list(pre_gains)
    in_specs = [_row_spec(t, d)] + ([_row_spec(t, d), _gain_spec(d)] if has_mix else []) + [_gain_spec(d)] * n_pre
    out_shape = ([jax.ShapeDtypeStruct((s, d), F32)] if has_mix else []) + [jax.ShapeDtypeStruct((s, d), BF16)] * n_pre
    out_specs = [_row_spec(t, d)] * len(out_shape)
    res = pl.pallas_call(
        body, name=name, grid=(s // t,), in_specs=in_specs, out_specs=out_specs, out_shape=out_shape,
        compiler_params=_params(("parallel",)),
    )(*ins)
    if has_mix:
        return res[0], list(res[1:])
    return x, list(res)


def _norm_res_loss(x, mix, g_post, target, name):
    s, d = x.shape
    t = _pick(s, (ROW_TILE,))

    def body(x_ref, m_ref, g_ref, t_ref, dy_ref, loss_ref):
        mv = m_ref[...].astype(F32)
        y = x_ref[...] + mv * _rstd(mv) * g_ref[...]
        err = y - t_ref[...]
        dy_ref[...] = err * (1.0 / d)

        @pl.when(pl.program_id(0) == 0)
        def _():
            loss_ref[...] = jnp.zeros_like(loss_ref)

        loss_ref[...] += jnp.sum(err * err)

    dy, acc = pl.pallas_call(
        body, name=name, grid=(s // t,),
        in_specs=[_row_spec(t, d), _row_spec(t, d), _gain_spec(d), _row_spec(t, d)],
        out_specs=[_row_spec(t, d), pl.BlockSpec((8, LANE), lambda i: (0, 0))],
        out_shape=[jax.ShapeDtypeStruct((s, d), F32), jax.ShapeDtypeStruct((8, LANE), F32)],
        compiler_params=_params(("arbitrary",)),
    )(x, mix, g_post, target)
    return dy, acc[0, 0] * (0.5 / d)


def _norm_bwd(dx_out, branches, x_in, post, name):
    s, d = dx_out.shape
    t = _pick(s, (ROW_TILE,))
    nb = len(branches)
    has_post = post is not None

    def body(*refs):
        dx_ref = refs[0]
        pos = 1
        dx = dx_ref[...]
        first = pl.program_id(0) == 0
        n_in = 1 + (1 + 2 * nb if nb else 0) + (2 if has_post else 0)
        outs = refs[n_in:]
        opos = 0
        if nb:
            xv = refs[pos][...]
            pos += 1
            r = _rstd(xv)
            dx_o = outs[0]
            opos = 1
            for _ in range(nb):
                dxn = refs[pos][...].astype(F32)
                g = refs[pos + 1][...]
                pos += 2
                dv, dg_rows = _rms_bwd(dxn, xv, g, r)
                dx = dx + dv
                dg_ref = outs[opos]
                opos += 1

                @pl.when(first)
                def _(dg_ref=dg_ref):
                    dg_ref[...] = jnp.zeros_like(dg_ref)

                dg_ref[...] += jnp.sum(dg_rows, axis=0, keepdims=True)
            dx_o[...] = dx
        if has_post:
            mv = refs[pos][...].astype(F32)
            g = refs[pos + 1][...]
            dm, dg_rows = _rms_bwd(dx, mv, g, _rstd(mv))
            outs[opos][...] = dm.astype(BF16)
            dg_ref = outs[opos + 1]

            @pl.when(first)
            def _():
                dg_ref[...] = jnp.zeros_like(dg_ref)

            dg_ref[...] += jnp.sum(dg_rows, axis=0, keepdims=True)

    ins, in_specs = [dx_out], [_row_spec(t, d)]
    out_shape, out_specs = [], []
    if nb:
        ins.append(x_in)
        in_specs.append(_row_spec(t, d))
        out_shape.append(jax.ShapeDtypeStruct((s, d), F32))
        out_specs.append(_row_spec(t, d))
        for dxn, g in branches:
            ins += [dxn, g]
            in_specs += [_row_spec(t, d), _gain_spec(d)]
            out_shape.append(jax.ShapeDtypeStruct((1, d), F32))
            out_specs.append(_gain_spec(d))
    if has_post:
        ins += [post[0], post[1]]
        in_specs += [_row_spec(t, d), _gain_spec(d)]
        out_shape += [jax.ShapeDtypeStruct((s, d), BF16), jax.ShapeDtypeStruct((1, d), F32)]
        out_specs += [_row_spec(t, d), _gain_spec(d)]
    res = pl.pallas_call(
        body, name=name, grid=(s // t,), in_specs=in_specs, out_specs=out_specs, out_shape=out_shape,
        compiler_params=_params(("arbitrary",)),
    )(*ins)
    res = list(res)
    dx_in = res.pop(0) if nb else dx_out
    dgs = [res.pop(0) for _ in range(nb)]
    dm, dg_post = (res[0], res[1]) if has_post else (None, None)
    return dx_in, dgs, dm, dg_post


HALO = 16


def _shift_down(u, prev, k):
    rows = lax.broadcasted_iota(jnp.int32, u.shape, 0)
    out = pltpu.roll(u, k, 0)
    for i in range(k):
        out = jnp.where(rows == i, prev[HALO - k + i:HALO - k + i + 1, :], out)
    return out


def _shift_up(u, nxt, k):
    n = u.shape[0]
    rows = lax.broadcasted_iota(jnp.int32, u.shape, 0)
    out = pltpu.roll(u, n - k, 0)
    for i in range(k):
        out = jnp.where(rows == n - k + i, nxt[i:i + 1, :], out)
    return out


def _conv_gate_fwd(p, cw, name):
    s, d3 = p.shape
    d = d3 // 3
    t = _pick(s, (ROW_TILE,))
    hb = t // HALO

    def body(p_ref, prev_ref, w_ref, z_ref):
        i = pl.program_id(0)
        pv = p_ref[...].astype(F32)
        b, u = pv[:, :d], pv[:, d:2 * d] * pv[:, 2 * d:]
        ph = prev_ref[...].astype(F32)
        up = jnp.where(i > 0, ph[:, d:2 * d] * ph[:, 2 * d:], 0.0)
        w = w_ref[...]
        y = w[0:1, :] * _shift_down(u, up, 2) + w[1:2, :] * _shift_down(u, up, 1) + w[2:3, :] * u
        z_ref[...] = (b * y).astype(BF16)

    return pl.pallas_call(
        body, name=name, grid=(s // t,),
        in_specs=[_row_spec(t, d3),
                  pl.BlockSpec((HALO, d3), lambda i: (jnp.maximum(i * hb - 1, 0), 0)),
                  pl.BlockSpec((3, d), lambda i: (0, 0))],
        out_specs=_row_spec(t, d),
        out_shape=jax.ShapeDtypeStruct((s, d), BF16),
        compiler_params=_params(("parallel",)),
    )(p, p, cw)


def _conv_gate_bwd(p, dz, cw, name):
    s, d3 = p.shape
    d = d3 // 3
    t = _pick(s, (ROW_TILE,))
    hb = t // HALO
    nt = s // t
    last_halo = s // HALO - 1

    def body(p_ref, prev_ref, next_ref, dz_ref, dznext_ref, w_ref, dp_ref, dw_ref):
        i = pl.program_id(0)
        pv = p_ref[...].astype(F32)
        b, c, h = pv[:, :d], pv[:, d:2 * d], pv[:, 2 * d:]
        u = c * h
        ph = prev_ref[...].astype(F32)
        up = jnp.where(i > 0, ph[:, d:2 * d] * ph[:, 2 * d:], 0.0)
        w = w_ref[...]
        u1, u2 = _shift_down(u, up, 1), _shift_down(u, up, 2)
        y = w[0:1, :] * u2 + w[1:2, :] * u1 + w[2:3, :] * u
        dz = dz_ref[...].astype(F32)
        dy = dz * b
        dyn = jnp.where(i < nt - 1, dznext_ref[...].astype(F32) * next_ref[...].astype(F32)[:, :d], 0.0)
        du = w[2:3, :] * dy + w[1:2, :] * _shift_up(dy, dyn, 1) + w[0:1, :] * _shift_up(dy, dyn, 2)
        dp_ref[:, :d] = (dz * y).astype(BF16)
        dp_ref[:, d:2 * d] = (du * h).astype(BF16)
        dp_ref[:, 2 * d:] = (du * c).astype(BF16)

        @pl.when(i == 0)
        def _():
            dw_ref[...] = jnp.zeros_like(dw_ref)

        dw_ref[0:1, :] += jnp.sum(dy * u2, axis=0, keepdims=True)
        dw_ref[1:2, :] += jnp.sum(dy * u1, axis=0, keepdims=True)
        dw_ref[2:3, :] += jnp.sum(dy * u, axis=0, keepdims=True)

    return pl.pallas_call(
        body, name=name, grid=(nt,),
        in_specs=[_row_spec(t, d3),
                  pl.BlockSpec((HALO, d3), lambda i: (jnp.maximum(i * hb - 1, 0), 0)),
                  pl.BlockSpec((HALO, d3), lambda i: (jnp.minimum((i + 1) * hb, last_halo), 0)),
                  _row_spec(t, d),
                  pl.BlockSpec((HALO, d), lambda i: (jnp.minimum((i + 1) * hb, last_halo), 0)),
                  pl.BlockSpec((3, d), lambda i: (0, 0))],
        out_specs=[_row_spec(t, d3), pl.BlockSpec((3, d), lambda i: (0, 0))],
        out_shape=[jax.ShapeDtypeStruct((s, d3), BF16), jax.ShapeDtypeStruct((3, d), F32)],
        compiler_params=_params(("arbitrary",)),
    )(p, p, p, dz, dz, cw)


def _swiglu_fwd(f, name):
    s, f2 = f.shape
    ff = f2 // 2
    t = _pick(s, (ROW_TILE,))

    def body(f_ref, a_ref):
        gate = f_ref[:, :ff].astype(F32)
        up = f_ref[:, ff:].astype(F32)
        a_ref[...] = (gate * jax.nn.sigmoid(gate) * up).astype(BF16)

    return pl.pallas_call(
        body, name=name, grid=(s // t,),
        in_specs=[_row_spec(t, f2)], out_specs=_row_spec(t, ff),
        out_shape=jax.ShapeDtypeStruct((s, ff), BF16),
        compiler_params=_params(("parallel",)),
    )(f)


def _swiglu_bwd(f, da, name):
    s, f2 = f.shape
    ff = f2 // 2
    t = _pick(s, (ROW_TILE,))

    def body(f_ref, da_ref, df_ref):
        gate = f_ref[:, :ff].astype(F32)
        up = f_ref[:, ff:].astype(F32)
        dav = da_ref[...].astype(F32)
        sg = jax.nn.sigmoid(gate)
        silu = gate * sg
        df_ref[:, :ff] = (dav * up * (sg + silu * (1.0 - sg))).astype(BF16)
        df_ref[:, ff:] = (dav * silu).astype(BF16)

    return pl.pallas_call(
        body, name=name, grid=(s // t,),
        in_specs=[_row_spec(t, f2), _row_spec(t, ff)], out_specs=_row_spec(t, f2),
        out_shape=jax.ShapeDtypeStruct((s, f2), BF16),
        compiler_params=_params(("parallel",)),
    )(f, da)


SUPER = 2048
RES = 16
PAIR = 128
L = 128


def _alibi_slopes(n_heads):
    h = np.arange(n_heads, dtype=np.float32) + 1.0
    return np.power(2.0, -8.0 * h / n_heads).astype(np.float32)


def _permute16(x, inverse, name):
    s, d = x.shape
    cw = LANE

    def body(x_ref, o_ref):
        if inverse:
            for m in range(L):
                o_ref[RES * m:RES * (m + 1), :] = x_ref[pl.ds(m, RES, stride=L), :]
        else:
            for r in range(RES):
                o_ref[L * r:L * (r + 1), :] = x_ref[pl.ds(r, L, stride=RES), :]

    spec = pl.BlockSpec((SUPER, cw), lambda i, j: (i, j))
    return pl.pallas_call(
        body, name=name, grid=(s // SUPER, d // cw), in_specs=[spec], out_specs=spec,
        out_shape=jax.ShapeDtypeStruct((s, d), x.dtype),
        compiler_params=_params(("parallel", "parallel")),
    )(x)


def _slope_table(d):
    nh = d // HEAD_DIM
    sl = _alibi_slopes(nh)
    tab = np.repeat(sl, HEAD_DIM).reshape(d // PAIR, 1, PAIR)
    return jnp.asarray(np.broadcast_to(tab, (d // PAIR, 8, PAIR)).copy())


def _geometry(dil):
    nch = RES // dil
    return nch, L // nch


def _band(dil):
    nch, w = _geometry(dil)
    sh = w.bit_length() - 1
    i = lax.broadcasted_iota(jnp.int32, (L, 2 * L), 0)
    j = lax.broadcasted_iota(jnp.int32, (L, 2 * L), 1)

    def pos(t):
        return jnp.bitwise_and(t, w - 1) * nch + jnp.right_shift(t, sh)

    delta = pos(i) + L - (pos(jnp.bitwise_and(j, L - 1)) + jnp.bitwise_and(j, L))
    return (delta * dil).astype(F32), (delta >= 0) & (delta <= L), j < L


def _offsets(dil, res, n):
    nch, w = _geometry(dil)

    def al(v):
        return v if isinstance(v, int) else pl.multiple_of(v, w)

    q_off = [al((a * dil + res) * L + n * w) for a in range(nch)]
    k_off = [al((a * dil + res) * 2 * L + L + n * w) for a in range(nch)]
    kp_off = [al((a * dil + res) * 2 * L + L + n * w - w) for a in range(nch)]
    return q_off, k_off, kp_off, w


def _gather(ref, offs, w):
    parts = [ref[pl.ds(o, w), :] for o in offs]
    return parts[0] if len(parts) == 1 else jnp.concatenate(parts, axis=0)


def _scatter(ref, offs, w, val, add=False):
    for a, o in enumerate(offs):
        piece = val[a * w:(a + 1) * w, :]
        if add:
            ref[pl.ds(o, w), :] += piece
        else:
            ref[pl.ds(o, w), :] = piece


def _fill_key_buffer(buf, prev_ref, cur_ref):
    for r in range(RES):
        buf[2 * L * r:2 * L * r + L, :] = prev_ref[L * r:L * (r + 1), :]
        buf[2 * L * r + L:2 * L * (r + 1), :] = cur_ref[L * r:L * (r + 1), :]


def _two_heads(x, low):
    zero = jnp.zeros_like(x)
    return jnp.concatenate([jnp.where(low, x, zero), jnp.where(low, zero, x)], axis=0)


def _loop_blocks(dil, do_block):
    nch, _ = _geometry(dil)
    if dil == RES:
        def it(r, c):
            do_block(r, 0)
            return c
        lax.fori_loop(0, RES, it, 0, unroll=2)
    else:
        def it(n, c):
            for res in range(dil):
                do_block(res, n)
            return c
        lax.fori_loop(0, nch, it, 0, unroll=2 if dil == 1 else 1)


NT = (((1,), (1,)), ((), ()))
TN = (((0,), (0,)), ((), ()))


def _attention_fwd(q, kv, name):
    s, d = q.shape
    g_n, ns = d // PAIR, s // SUPER

    def body(sl_ref, q_ref, kc_ref, kp_ref, vc_ref, vp_ref, o_ref, lse_ref, kbuf, vbuf, m_s, l_s, acc_s):
        sb = pl.program_id(1)
        _fill_key_buffer(kbuf, kp_ref, kc_ref)
        _fill_key_buffer(vbuf, vp_ref, vc_ref)
        low = lax.broadcasted_iota(jnp.int32, (L, PAIR), 1) < HEAD_DIM
        low_k = lax.broadcasted_iota(jnp.int32, (2 * L, PAIR), 1) < HEAD_DIM
        nsl = (-sl_ref[0:1, 0:1], -sl_ref[0:1, HEAD_DIM:HEAD_DIM + 1])
        for bi, dil in enumerate(DILATIONS):
            first_branch, last_branch = bi == 0, bi == len(DILATIONS) - 1
            base, band, prev_half = _band(dil)

            def do_block(res, n, dil=dil, first_branch=first_branch, last_branch=last_branch,
                         base=base, band=band, prev_half=prev_half):
                q_off, k_off, kp_off, w = _offsets(dil, res, n)
                qf = _gather(q_ref, q_off, w).astype(BF16)
                kcat = jnp.concatenate([_gather(kbuf, kp_off, w), _gather(kbuf, k_off, w)], axis=0).astype(BF16)
                vcat = jnp.concatenate([_gather(vbuf, kp_off, w), _gather(vbuf, k_off, w)], axis=0).astype(BF16)
                k_bd, v_bd = _two_heads(kcat, low_k), _two_heads(vcat, low_k)
                sc = lax.dot_general(qf, k_bd, NT, preferred_element_type=F32)
                starts_sequence = jnp.logical_and(sb == 0, n == 0)
                valid = band & jnp.logical_not(prev_half & starts_sequence)
                if not first_branch:
                    m_prev = _gather(m_s, q_off, w)
                ps, m_new, l_blk = [], [], []
                for h in range(2):
                    s_h = jnp.where(valid, sc[:, 2 * L * h:2 * L * (h + 1)] + nsl[h] * base, NEG_BIG)
                    mh = jnp.max(s_h, axis=1, keepdims=True)
                    if not first_branch:
                        mh = jnp.maximum(mh, m_prev[:, HEAD_DIM * h:HEAD_DIM * h + 1])
                    p_h = jnp.exp(s_h - mh)
                    ps.append(p_h.astype(BF16))
                    m_new.append(mh)
                    l_blk.append(jnp.sum(p_h, axis=1, keepdims=True))
                m_full = jnp.where(low, m_new[0], m_new[1])
                l_full = jnp.where(low, l_blk[0], l_blk[1])
                acc = jnp.dot(jnp.concatenate(ps, axis=1), v_bd, preferred_element_type=F32)
                if not first_branch:
                    alpha = jnp.exp(m_prev - m_full)
                    l_full = _gather(l_s, q_off, w) * alpha + l_full
                    acc = _gather(acc_s, q_off, w) * alpha + acc
                if last_branch:
                    _scatter(o_ref, q_off, w, acc / l_full)
                    _scatter(lse_ref, q_off, w, m_full + jnp.log(l_full))
                else:
                    _scatter(m_s, q_off, w, m_full)
                    _scatter(l_s, q_off, w, l_full)
                    _scatter(acc_s, q_off, w, acc)

            _loop_blocks(dil, do_block)

    prev = lambda i: jnp.maximum(i - 1, 0)
    blk = pl.BlockSpec((SUPER, PAIR), lambda g, i: (i, g))
    in_specs = [pl.BlockSpec((None, 8, PAIR), lambda g, i: (g, 0, 0)), blk,
                pl.BlockSpec((SUPER, PAIR), lambda g, i: (i, g)),
                pl.BlockSpec((SUPER, PAIR), lambda g, i: (prev(i), g)),
                pl.BlockSpec((SUPER, PAIR), lambda g, i: (i, g_n + g)),
                pl.BlockSpec((SUPER, PAIR), lambda g, i: (prev(i), g_n + g))]
    return pl.pallas_call(
        body, name=name, grid=(g_n, ns), in_specs=in_specs, out_specs=[blk, blk],
        out_shape=[jax.ShapeDtypeStruct((s, d), F32)] * 2,
        scratch_shapes=[pltpu.VMEM((2 * SUPER, PAIR), F32)] * 2 + [pltpu.VMEM((SUPER, PAIR), F32)] * 3,
        compiler_params=_params(("parallel", "parallel")),
    )(_slope_table(d), q, kv, kv, kv, kv)


def _attention_bwd(q, kv, o, do, lse, dkv_in, name):
    s, d = q.shape
    g_n, ns = d // PAIR, s // SUPER
    has_in = dkv_in is not None

    def body(*refs):
        sl_ref, q_ref, do_ref, o_ref, lse_ref, kc_ref, kp_ref, vc_ref, vp_ref = refs[:9]
        pos = 9
        if has_in:
            dk_in_ref, dv_in_ref = refs[9:11]
            pos = 11
        dq_ref, dk_ref, dv_ref, kbuf, vbuf, dkbuf, dvbuf, dq_s = refs[pos:]
        step = pl.program_id(1)
        sb = ns - 1 - step
        _fill_key_buffer(kbuf, kp_ref, kc_ref)
        _fill_key_buffer(vbuf, vp_ref, vc_ref)

        @pl.when(step == 0)
        def _():
            dkbuf[...] = jnp.zeros_like(dkbuf)
            dvbuf[...] = jnp.zeros_like(dvbuf)

        @pl.when(step > 0)
        def _():
            for buf in (dkbuf, dvbuf):
                for r in range(RES):
                    buf[2 * L * r + L:2 * L * (r + 1), :] = buf[2 * L * r:2 * L * r + L, :]
                    buf[2 * L * r:2 * L * r + L, :] = jnp.zeros((L, PAIR), F32)

        low = lax.broadcasted_iota(jnp.int32, (L, PAIR), 1) < HEAD_DIM
        low_k = lax.broadcasted_iota(jnp.int32, (2 * L, PAIR), 1) < HEAD_DIM
        nsl = (-sl_ref[0:1, 0:1], -sl_ref[0:1, HEAD_DIM:HEAD_DIM + 1])
        for bi, dil in enumerate(DILATIONS):
            first_branch = bi == 0
            base, band, prev_half = _band(dil)

            def do_block(res, n, dil=dil, first_branch=first_branch, base=base, band=band, prev_half=prev_half):
                q_off, k_off, kp_off, w = _offsets(dil, res, n)
                qb = _gather(q_ref, q_off, w).astype(BF16)
                dof = _gather(do_ref, q_off, w)
                prod = dof * _gather(o_ref, q_off, w)
                dob = dof.astype(BF16)
                lse_f = _gather(lse_ref, q_off, w)
                zero = jnp.zeros_like(prod)
                dsum = (jnp.sum(jnp.where(low, prod, zero), axis=1, keepdims=True),
                        jnp.sum(jnp.where(low, zero, prod), axis=1, keepdims=True))
                kcat = jnp.concatenate([_gather(kbuf, kp_off, w), _gather(kbuf, k_off, w)], axis=0).astype(BF16)
                vcat = jnp.concatenate([_gather(vbuf, kp_off, w), _gather(vbuf, k_off, w)], axis=0).astype(BF16)
                k_bd, v_bd = _two_heads(kcat, low_k), _two_heads(vcat, low_k)
                sc = lax.dot_general(qb, k_bd, NT, preferred_element_type=F32)
                dp = lax.dot_general(dob, v_bd, NT, preferred_element_type=F32)
                starts_sequence = jnp.logical_and(sb == 0, n == 0)
                valid = band & jnp.logical_not(prev_half & starts_sequence)
                ps, dss = [], []
                for h in range(2):
                    cols = slice(2 * L * h, 2 * L * (h + 1))
                    lse_h = lse_f[:, HEAD_DIM * h:HEAD_DIM * h + 1]
                    p_h = jnp.exp(jnp.where(valid, sc[:, cols] + nsl[h] * base - lse_h, NEG_BIG))
                    dss.append((p_h * (dp[:, cols] - dsum[h])).astype(BF16))
                    ps.append(p_h.astype(BF16))
                ds_cat, p_cat = jnp.concatenate(dss, axis=1), jnp.concatenate(ps, axis=1)
                dq = jnp.dot(ds_cat, k_bd, preferred_element_type=F32)
                dk_bd = lax.dot_general(ds_cat, qb, TN, preferred_element_type=F32)
                dv_bd = lax.dot_general(p_cat, dob, TN, preferred_element_type=F32)
                dk = jnp.where(low_k, dk_bd[:2 * L], dk_bd[2 * L:])
                dv = jnp.where(low_k, dv_bd[:2 * L], dv_bd[2 * L:])
                _scatter(dq_s, q_off, w, dq, add=not first_branch)
                _scatter(dkbuf, kp_off, w, dk[:L], add=True)
                _scatter(dkbuf, k_off, w, dk[L:], add=True)
                _scatter(dvbuf, kp_off, w, dv[:L], add=True)
                _scatter(dvbuf, k_off, w, dv[L:], add=True)

            _loop_blocks(dil, do_block)

        dq_ref[...] = dq_s[...].astype(BF16)
        for r in range(RES):
            rows, cur = slice(L * r, L * (r + 1)), slice(2 * L * r + L, 2 * L * (r + 1))
            if has_in:
                dk_ref[rows, :] = dkbuf[cur, :] + dk_in_ref[rows, :]
                dv_ref[rows, :] = dvbuf[cur, :] + dv_in_ref[rows, :]
            else:
                dk_ref[rows, :] = dkbuf[cur, :]
                dv_ref[rows, :] = dvbuf[cur, :]

    rev = lambda i: ns - 1 - i
    prev = lambda i: jnp.maximum(ns - 2 - i, 0)
    blk = pl.BlockSpec((SUPER, PAIR), lambda g, i: (rev(i), g))
    in_specs = [pl.BlockSpec((None, 8, PAIR), lambda g, i: (g, 0, 0)), blk, blk, blk, blk,
                pl.BlockSpec((SUPER, PAIR), lambda g, i: (rev(i), g)),
                pl.BlockSpec((SUPER, PAIR), lambda g, i: (prev(i), g)),
                pl.BlockSpec((SUPER, PAIR), lambda g, i: (rev(i), g_n + g)),
                pl.BlockSpec((SUPER, PAIR), lambda g, i: (prev(i), g_n + g))]
    ins = [_slope_table(d), q, do, o, lse, kv, kv, kv, kv]
    if has_in:
        in_specs += [blk, blk]
        ins += list(dkv_in)
    res = pl.pallas_call(
        body, name=name, grid=(g_n, ns), in_specs=in_specs, out_specs=[blk, blk, blk],
        out_shape=[jax.ShapeDtypeStruct((s, d), BF16)] + [jax.ShapeDtypeStruct((s, d), F32)] * 2,
        scratch_shapes=[pltpu.VMEM((2 * SUPER, PAIR), F32)] * 4 + [pltpu.VMEM((SUPER, PAIR), F32)],
        compiler_params=_params(("parallel", "arbitrary")),
    )(*ins)
    return res[0], (res[1], res[2])


def _coords():
    return lax.axis_index("x"), lax.axis_index("y"), lax.axis_index("c")


def _chip_peers(x, y):
    return [(1 - x, y), (x, 1 - y), (1 - x, 1 - y)]


def _block_of(ref, axis, blk, size):
    start = pl.multiple_of(blk * size, size)
    if axis == 1:
        return ref.at[:, pl.ds(start, size)]
    return ref.at[pl.ds(start, size), :]


ANY = pl.BlockSpec(memory_space=pl.ANY)


HBM = pl.BlockSpec(memory_space=pltpu.HBM)
SEM = pl.BlockSpec(memory_space=pltpu.SEMAPHORE)
SPLIT = pltpu.CompilerParams(has_side_effects=pltpu.SideEffectType.DATAFLOW_SIDE_EFFECTING)


def _in_hbm(a):
    return pltpu.with_memory_space_constraint(a, pltpu.HBM)


def _thru(arrays):
    return [pltpu.HBM(a.shape, a.dtype) for a in arrays]


def _gather_start(group, carry, name):
    n, nc = len(group), len(carry)

    def body(*refs):
        blocks, fulls, send_sem, recv_sem, local_sems = refs[:n], refs[n:2 * n], refs[2 * n + nc], refs[2 * n + nc + 1], refs[-1]
        x, y, c = _coords()
        mine = [_block_of(fulls[t], ax, 2 * x + y, b.shape[ax]) for t, (b, ax) in enumerate(group)]
        local = [pltpu.make_async_copy(blocks[t], mine[t], local_sems.at[t]) for t in range(n)]
        for cp in local:
            cp.start()
        for cp in local:
            cp.wait()
        for t in range(n):
            for j, (px, py) in enumerate(_chip_peers(x, y)):
                pltpu.make_async_remote_copy(
                    src_ref=blocks[t], dst_ref=mine[t], send_sem=send_sem.at[3 * t + j], recv_sem=recv_sem.at[3 * t + j],
                    device_id=(px, py, c), device_id_type=MESH).start()

    blocks = [b for b, _ in group]
    fulls = []
    for b, ax in group:
        shape = list(b.shape)
        shape[ax] *= 4
        fulls.append(lax.empty(tuple(shape), b.dtype))
    sems = [pltpu.SemaphoreType.DMA((3 * n,))] * 2
    n_in = 2 * n + nc
    res = pl.pallas_call(
        body, name=name, in_specs=[HBM] * n_in, out_specs=[SEM, SEM] + [HBM] * n_in,
        out_shape=sems + _thru(blocks + fulls + list(carry)),
        input_output_aliases={i: 2 + i for i in range(n_in)},
        scratch_shapes=[pltpu.SemaphoreType.DMA((n,))],
        compiler_params=SPLIT,
    )(*[_in_hbm(a) for a in blocks + fulls + list(carry)])
    return (res[0], res[1], list(res[2:2 + n]), list(res[2 + n:2 + 2 * n])), list(res[2 + 2 * n:])


def _gather_wait(group, started, after, name):
    sends, recvs, blocks, fulls = started
    m = len(group)

    def body(*refs):
        blk_refs, full_refs, send_sem, recv_sem = refs[:m], refs[m:2 * m], refs[2 * m], refs[2 * m + 1]
        x, y, c = _coords()
        for t, (b, ax) in enumerate(group):
            for j, (px, py) in enumerate(_chip_peers(x, y)):
                cp = pltpu.make_async_remote_copy(
                    src_ref=blk_refs[t], dst_ref=_block_of(full_refs[t], ax, 2 * px + py, b.shape[ax]),
                    send_sem=send_sem.at[3 * t + j], recv_sem=recv_sem.at[3 * t + j],
                    device_id=(px, py, c), device_id_type=MESH)
                cp.wait_send()
                cp.wait_recv()

    extra = [] if after is None else [after]
    res = pl.pallas_call(
        body, name=name, in_specs=[HBM] * (2 * m) + [SEM, SEM] + [ANY] * len(extra), out_specs=[HBM] * (2 * m),
        out_shape=_thru(blocks) + _thru(fulls), input_output_aliases={i: i for i in range(2 * m)},
        compiler_params=SPLIT,
    )(*blocks, *fulls, sends, recvs, *extra)
    return list(res[m:])


def _scatter_start(grads, name):
    n = len(grads)

    def body(*refs):
        g_refs, st_refs, send_sem, recv_sem, local_sems = refs[:n], refs[n:2 * n], refs[2 * n], refs[2 * n + 1], refs[-1]
        x, y, c = _coords()
        local = [pltpu.make_async_copy(_block_of(g_refs[t], ax, 2 * x + y, g.shape[ax] // 4), st_refs[t].at[0], local_sems.at[t])
                 for t, (g, ax) in enumerate(grads)]
        for cp in local:
            cp.start()
        for cp in local:
            cp.wait()
        for t, (g, ax) in enumerate(grads):
            for j, (px, py) in enumerate(_chip_peers(x, y)):
                pltpu.make_async_remote_copy(
                    src_ref=_block_of(g_refs[t], ax, 2 * px + py, g.shape[ax] // 4), dst_ref=st_refs[t].at[1 + j],
                    send_sem=send_sem.at[3 * t + j], recv_sem=recv_sem.at[3 * t + j],
                    device_id=(px, py, c), device_id_type=MESH).start()

    full = [g for g, _ in grads]
    stacks = []
    for g, ax in grads:
        shape = list(g.shape)
        shape[ax] //= 4
        stacks.append(lax.empty((4, *shape), g.dtype))
    sems = [pltpu.SemaphoreType.DMA((3 * n,))] * 2
    res = pl.pallas_call(
        body, name=name, in_specs=[HBM] * (2 * n), out_specs=[SEM, SEM] + [HBM] * (2 * n),
        out_shape=sems + _thru(full) + _thru(stacks), input_output_aliases={i: 2 + i for i in range(2 * n)},
        scratch_shapes=[pltpu.SemaphoreType.DMA((n,))],
        compiler_params=SPLIT,
    )(*[_in_hbm(a) for a in full + stacks])
    return res[0], res[1], list(res[2:2 + n]), list(res[2 + n:])


def _scatter_wait(axes, started, name):
    sends, recvs, full, stacks = started
    n = len(full)

    def body(*refs):
        g_refs, st_refs, send_sem, recv_sem = refs[:n], refs[n:2 * n], refs[2 * n], refs[2 * n + 1]
        x, y, c = _coords()
        for t, ax in enumerate(axes):
            size = full[t].shape[ax] // 4
            for j, (px, py) in enumerate(_chip_peers(x, y)):
                cp = pltpu.make_async_remote_copy(
                    src_ref=_block_of(g_refs[t], ax, 2 * px + py, size), dst_ref=st_refs[t].at[1 + j],
                    send_sem=send_sem.at[3 * t + j], recv_sem=recv_sem.at[3 * t + j],
                    device_id=(px, py, c), device_id_type=MESH)
                cp.wait_send()
                cp.wait_recv()

    res = pl.pallas_call(
        body, name=name, in_specs=[HBM] * (2 * n) + [SEM, SEM], out_specs=[HBM] * (2 * n),
        out_shape=_thru(full) + _thru(stacks), input_output_aliases={i: i for i in range(2 * n)},
        compiler_params=SPLIT,
    )(*full, *stacks, sends, recvs)
    return list(res[n:])


def _pair_exchange(stacks, name):
    n = len(stacks)

    def body(*refs):
        ins, outs, send_sems, recv_sems = refs[:n], refs[n:2 * n], refs[2 * n], refs[2 * n + 1]
        x, y, c = _coords()
        copies = []
        for t in range(n):
            cp = pltpu.make_async_remote_copy(
                src_ref=ins[t], dst_ref=outs[t], send_sem=send_sems.at[t], recv_sem=recv_sems.at[t],
                device_id=(x, y, 1 - c), device_id_type=MESH)
            cp.start()
            copies.append(cp)
        for cp in copies:
            cp.wait()

    return pl.pallas_call(
        body, name=name, in_specs=[ANY] * n, out_specs=[ANY] * n,
        out_shape=[jax.ShapeDtypeStruct(s.shape, s.dtype) for s in stacks],
        scratch_shapes=[pltpu.SemaphoreType.DMA((n,)), pltpu.SemaphoreType.DMA((n,))],
        compiler_params=pltpu.CompilerParams(has_side_effects=True),
    )(*stacks)


def _allreduce_small(v, name):
    r, cdim = v.shape

    def body(v_ref, out_ref, buf, send_sems, recv_sems):
        x, y, c = _coords()
        me = 4 * x + 2 * y + c
        buf[0] = v_ref[...]
        sends = []
        for k in range(1, 8):
            peer = (x if not (k & 4) else 1 - x, y if not (k & 2) else 1 - y, c if not (k & 1) else 1 - c)
            cp = pltpu.make_async_remote_copy(
                src_ref=v_ref, dst_ref=buf.at[k], send_sem=send_sems.at[k - 1], recv_sem=recv_sems.at[k - 1],
                device_id=peer, device_id_type=MESH)
            cp.start()
            sends.append(cp)
        for cp in sends:
            cp.wait_recv()
        total = buf[me]
        for src in range(1, 8):
            total = total + buf[jnp.bitwise_xor(me, src)]
        out_ref[...] = total
        for cp in sends:
            cp.wait_send()

    return pl.pallas_call(
        body, name=name,
        in_specs=[pl.BlockSpec(memory_space=pltpu.VMEM)], out_specs=pl.BlockSpec(memory_space=pltpu.VMEM),
        out_shape=jax.ShapeDtypeStruct((r, cdim), F32),
        scratch_shapes=[pltpu.VMEM((8, r, cdim), F32), pltpu.SemaphoreType.DMA((7,)), pltpu.SemaphoreType.DMA((7,))],
        compiler_params=pltpu.CompilerParams(has_side_effects=True),
    )(v)


def _adamw_math(w, g, m, v):
    m = ADAM_B1 * m + (1.0 - ADAM_B1) * g
    v = ADAM_B2 * v + (1.0 - ADAM_B2) * jnp.square(g)
    m_hat = m / (1.0 - ADAM_B1 ** ADAM_STEP)
    v_hat = v / (1.0 - ADAM_B2 ** ADAM_STEP)
    delta = -ADAM_LR * (m_hat / (jnp.sqrt(v_hat) + ADAM_EPS) + ADAM_WD * w)
    return delta, m, v


def _adamw(w, m, v, grads, name):
    r, cdim = w.shape
    paired = isinstance(grads, list)
    layers = len(grads) if paired else 1
    t = _pick(r // layers, (64, 32, 16, 8))
    per_layer = r // layers // t

    def body(*refs):
        w_ref, m_ref, v_ref = refs[:3]
        outs = refs[3 + (2 * layers if paired else 1):]

        def update(g):
            delta, m_new, v_new = _adamw_math(w_ref[...], g, m_ref[...], v_ref[...])
            outs[0][...] = g
            outs[1][...] = delta
            outs[2][...] = m_new
            outs[3][...] = v_new

        if not paired:
            update(refs[3][...])
            return
        layer = pl.program_id(0) // per_layer
        for l in range(layers):
            @pl.when(layer == l)
            def _(a_ref=refs[3 + 2 * l], b_ref=refs[4 + 2 * l]):
                sa = a_ref[0].astype(F32)
                sb = b_ref[0].astype(F32)
                for k in range(1, 4):
                    sa = sa + a_ref[k].astype(F32)
                    sb = sb + b_ref[k].astype(F32)
                update(sa + sb)

    spec = pl.BlockSpec((t, cdim), lambda i: (i, 0))
    if paired:
        ins = [w, m, v] + [s for pair in grads for s in pair]
        stack_specs = [pl.BlockSpec((4, t, cdim), lambda i, l=l: (0, jnp.clip(i - l * per_layer, 0, per_layer - 1), 0))
                       for l in range(layers) for _ in range(2)]
        in_specs = [spec] * 3 + stack_specs
    else:
        ins = [w, m, v, grads]
        in_specs = [spec] * 4
    return pl.pallas_call(
        body, name=name, grid=(r // t,), in_specs=in_specs, out_specs=[spec] * 4,
        out_shape=[jax.ShapeDtypeStruct((r, cdim), F32)] * 4,
        compiler_params=_params(("parallel",)),
    )(*ins)


def _local_step(x, target, gains, conv_ws, kv_gain, weights_of, send_grads):
    depth = len(gains)
    n_a = len(conv_ws)
    saved, ws = [], []
    kv = kvn = None
    _, (xn,) = _norm_res_fwd(x, None, None, [gains[0][0]], "norm_first")
    h = x
    for l in range(depth):
        g = gains[l]
        sv = {"x_in": h, "xn": xn}
        w = weights_of(l, h)
        ws.append(w)
        if l == n_a:
            kv = _mm(kvn, w["kv"], "nn", F32, "kv_fwd")
        if l < n_a:
            p = _mm(xn, w["conv_in"], "nn", BF16, f"conv_in_fwd_{l}")
            z = _conv_gate_fwd(p, conv_ws[l], f"conv_gate_fwd_{l}")
            mix = _mm(z, w["conv_out"], "nn", BF16, f"conv_out_fwd_{l}")
            sv.update(p=p, z=z)
        else:
            j = l - n_a
            q = _mm(xn, w["q"], "nn", F32, f"q_fwd_{j}", scale=HEAD_DIM ** -0.5)
            o, lse = _attention_fwd(q, kv, f"attn_fwd_{j}")
            mix = _mm(o, w["o"], "nn", BF16, f"o_fwd_{j}")
            sv.update(q=q, o=o, lse=lse)
        x1, (xn2,) = _norm_res_fwd(h, mix, g[1], [g[2]], f"norm_mid_{l}")
        f = _mm(xn2, w["ffn_in"], "nn", BF16, f"ffn_in_fwd_{l}")
        a = _swiglu_fwd(f, f"swiglu_fwd_{l}")
        ff = _mm(a, w["ffn_out"], "nn", BF16, f"ffn_out_fwd_{l}")
        sv.update(mix=mix, x1=x1, xn2=xn2, f=f, a=a, ff=ff)
        saved.append(sv)
        if l == depth - 1:
            dx, loss = _norm_res_loss(x1, ff, g[3], target, "norm_loss")
        else:
            if l == n_a - 1:
                h, _ = _norm_res_fwd(x1, ff, g[3], [], f"norm_end_{l}")
                h = _permute16(h, False, "permute_stream")
                target = _permute16(target, False, "permute_target")
                _, (xn, kvn) = _norm_res_fwd(h, None, None, [gains[l + 1][0], kv_gain], "norm_permuted")
            else:
                h, (xn,) = _norm_res_fwd(x1, ff, g[3], [gains[l + 1][0]], f"norm_end_{l}")
    d_gains = [[None] * 4 for _ in range(depth)]
    d_conv = [None] * n_a
    d_kv_gain = None
    dkv = None
    _, _, dff, d_gains[depth - 1][3] = _norm_bwd(dx, [], None, (saved[-1]["ff"], gains[-1][3]), "norm_loss_bwd")
    for l in reversed(range(depth)):
        sv, g, w, grads = saved[l], gains[l], ws[l], {}
        da = _mm(dff, w["ffn_out"], "nt", BF16, f"ffn_out_dx_{l}")
        grads["ffn_out"] =_mm(sv["a"], dff, "tn", BF16, f"ffn_out_dw_{l}")
        df = _swiglu_bwd(sv["f"], da, f"swiglu_bwd_{l}")
        dxn2 = _mm(df, w["ffn_in"], "nt", BF16, f"ffn_in_dx_{l}")
        grads["ffn_in"] =_mm(sv["xn2"], df, "tn", BF16, f"ffn_in_dw_{l}")
        dx, (d_gains[l][2],), dmix, d_gains[l][1] = _norm_bwd(
            dx, [(dxn2, g[2])], sv["x1"], (sv["mix"], g[1]), f"norm_mid_bwd_{l}")
        if l < n_a:
            dz = _mm(dmix, w["conv_out"], "nt", BF16, f"conv_out_dx_{l}")
            grads["conv_out"] =_mm(sv["z"], dmix, "tn", BF16, f"conv_out_dw_{l}")
            dp, d_conv[l] = _conv_gate_bwd(sv["p"], dz, conv_ws[l], f"conv_gate_bwd_{l}")
            dxn = _mm(dp, w["conv_in"], "nt", BF16, f"conv_in_dx_{l}")
            grads["conv_in"] =_mm(sv["xn"], dp, "tn", BF16, f"conv_in_dw_{l}")
        else:
            j = l - n_a
            do = _mm(dmix, w["o"], "nt", F32, f"o_dx_{j}")
            grads["o"] =_mm(sv["o"], dmix, "tn", BF16, f"o_dw_{j}")
            dq, dkv = _attention_bwd(sv["q"], kv, sv["o"], do, sv["lse"], dkv, f"attn_bwd_{j}")
            scale = HEAD_DIM ** -0.5
            dxn = _mm(dq, w["q"], "nt", BF16, f"q_dx_{j}", scale=scale)
            grads["q"] =_mm(sv["xn"], dq, "tn", BF16, f"q_dw_{j}", scale=scale)
        branches = [(dxn, g[0])]
        if l == n_a:
            dkv_cat = jnp.concatenate([dkv[0], dkv[1]], axis=1).astype(BF16)
            dkvn = _mm(dkv_cat, w["kv"], "nt", BF16, "kv_dx")
            grads["kv"] =_mm(kvn, dkv_cat, "tn", BF16, "kv_dw")
            branches.append((dkvn, kv_gain))
        post = (saved[l - 1]["ff"], gains[l - 1][3]) if l > 0 else None
        if l == n_a:
            dx, dgs, _, _ = _norm_bwd(dx, branches, sv["x_in"], None, f"norm_end_bwd_{l}")
            dx = _permute16(dx, True, "unpermute_stream")
            _, _, dff, dg_post = _norm_bwd(dx, [], None, post, "norm_boundary_bwd")
        else:
            dx, dgs, dff, dg_post = _norm_bwd(dx, branches, sv["x_in"], post, f"norm_end_bwd_{l}")
        send_grads(l, grads)
        d_gains[l][0] = dgs[0]
        if l == n_a:
            d_kv_gain = dgs[1]
        if l > 0:
            d_gains[l - 1][3] = dg_post
    return loss, dx, d_gains, d_conv, d_kv_gain


BIG = (
    ("conv_in", 1), ("conv_out", 0), ("kv", 1), ("q", 0), ("o", 0), ("ffn_in", 1), ("ffn_out", 0))


def kernel(x, norm_g, conv_in_w, conv_w, conv_out_w, kv_norm_g, kv_w, q_w, o_w, ffn_in_w, ffn_out_w, loss_target, m_norm_g, m_conv_in_w, m_conv_w, m_conv_out_w, m_kv_norm_g, m_kv_w, m_q_w, m_o_w, m_ffn_in_w, m_ffn_out_w, v_norm_g, v_conv_in_w, v_conv_w, v_conv_out_w, v_kv_norm_g, v_kv_w, v_q_w, v_o_w, v_ffn_in_w, v_ffn_out_w):
    depth, _, dq = norm_g.shape
    d = 4 * dq
    n_a = conv_w.shape[0]
    big_w = {"conv_in": conv_in_w, "conv_out": conv_out_w, "kv": kv_w[None], "q": q_w, "o": o_w,
             "ffn_in": ffn_in_w, "ffn_out": ffn_out_w}
    big_m = {"conv_in": m_conv_in_w, "conv_out": m_conv_out_w, "kv": m_kv_w[None], "q": m_q_w, "o": m_o_w,
             "ffn_in": m_ffn_in_w, "ffn_out": m_ffn_out_w}
    big_v = {"conv_in": v_conv_in_w, "conv_out": v_conv_out_w, "kv": v_kv_w[None], "q": v_q_w, "o": v_o_w,
             "ffn_in": v_ffn_in_w, "ffn_out": v_ffn_out_w}

    n_gain, n_tap = depth * 4, n_a * conv_w.shape[1]
    small_rows = -(-(n_gain + n_tap + 1) // 8) * 8
    pad_rows = small_rows - n_gain - n_tap

    def pack_small(gains, taps):
        return jnp.concatenate([gains.reshape(n_gain, dq), taps.reshape(n_tap, dq), jnp.zeros((pad_rows, dq), F32)])

    axis_of = dict(BIG)

    def matrices_of(l):
        if l < n_a:
            return [("conv_in", l), ("conv_out", l), ("ffn_in", l), ("ffn_out", l)]
        return ([("kv", 0)] if l == n_a else []) + [("q", l - n_a), ("o", l - n_a), ("ffn_in", l), ("ffn_out", l)]

    blocks = {name: big_w[name].astype(BF16) for name, _ in BIG}
    groups = [[(blocks[name][i], axis_of[name]) for name, i in matrices_of(l)] for l in range(depth)]
    groups[0].append((pack_small(norm_g, conv_w), 1))
    started = {0: _gather_start(groups[0], [], "gather_start_0")[0]}

    def fetch(l, after):
        full = _gather_wait(groups[l], started[l], after, f"gather_wait_{l}")
        if l + 1 < depth:
            started[l + 1], full = _gather_start(groups[l + 1], full, f"gather_start_{l + 1}")
        return full

    first = fetch(0, None)
    small = first[-1]
    gains = [[small[4 * l + i][None] for i in range(4)] for l in range(depth)]
    conv_ws = [small[n_gain + 3 * l:n_gain + 3 * l + 3] for l in range(n_a)]
    kv_gain = kv_norm_g[None]

    def weights_of(l, after):
        full = first if l == 0 else fetch(l, after)
        return {name: full[t] for t, (name, _) in enumerate(matrices_of(l))}

    sent = {}

    def send_grads(l, grads):
        sent[l] = _scatter_start([(grads[name], axis_of[name]) for name, _ in matrices_of(l)], f"scatter_start_{l}")

    loss, dx, d_gains, d_conv, d_kv_gain = _local_step(
        x[0], loss_target[0], gains, conv_ws, kv_gain, weights_of, send_grads)
    loss = lax.psum(loss, ("x", "y", "c"))

    small_g = jnp.concatenate([dg for row in d_gains for dg in row] + list(d_conv) + [d_kv_gain]
                              + [jnp.zeros((pad_rows - 1, d), F32)])
    small_g = _allreduce_small(small_g, "allreduce_small")
    blk = 2 * lax.axis_index("x") + lax.axis_index("y")
    mine_small = lax.dynamic_slice_in_dim(small_g, blk * dq, dq, axis=1)
    kv_rows = d // dq

    def pack_opt(gains_like, taps_like, kv_like):
        rows = jnp.concatenate([gains_like.reshape(n_gain, dq), taps_like.reshape(n_tap, dq), kv_like.reshape(kv_rows, dq)])
        extra = -rows.shape[0] % 8
        return jnp.concatenate([rows, jnp.zeros((extra, dq), F32)]) if extra else rows

    sw = pack_opt(norm_g, conv_w, kv_norm_g)
    sm = pack_opt(m_norm_g, m_conv_w, m_kv_norm_g)
    sv = pack_opt(v_norm_g, v_conv_w, v_kv_norm_g)
    sg = pack_opt(mine_small[:n_gain], mine_small[n_gain:n_gain + n_tap], small_g[n_gain + n_tap])
    s_out = _adamw(sw, sm, sv, sg, "adamw_small")

    def unpack(a):
        return (a[:n_gain].reshape(depth, 4, dq), a[n_gain:n_gain + n_tap].reshape(n_a, -1, dq),
                a[n_gain + n_tap:n_gain + n_tap + kv_rows].reshape(d))

    small_out = [unpack(a) for a in s_out]

    mine = {}
    for l in reversed(range(depth)):
        got = _scatter_wait([axis_of[name] for name, _ in matrices_of(l)], sent[l], f"scatter_wait_{l}")
        mine.update(zip(matrices_of(l), got))
    order = [(name, i) for name, _ in BIG for i in range(big_w[name].shape[0])]
    sibling = dict(zip(order, _pair_exchange([mine[k] for k in order], "pair_exchange")))
    big_out = {}
    for name, _ in BIG:
        shp = big_w[name].shape
        rows, cols = shp[0] * shp[1], shp[2]
        flat = lambda a: a.reshape(rows, cols)
        res = _adamw(flat(big_w[name]), flat(big_m[name]), flat(big_v[name]),
                     [(mine[name, i], sibling[name, i]) for i in range(shp[0])], f"adamw_{name}")
        out_shp = shp[1:] if name == "kv" else shp
        big_out[name] = [a.reshape(out_shp) for a in res]

    def leaves(i):
        ng, cw_, kg = small_out[i]
        return [ng, big_out["conv_in"][i], cw_, big_out["conv_out"][i], kg, big_out["kv"][i], big_out["q"][i],
                big_out["o"][i], big_out["ffn_in"][i], big_out["ffn_out"][i]]

    return (loss, dx[None], *leaves(0), *leaves(1), *leaves(2), *leaves(3))
```

```python
import functools

import jax
import jax.numpy as jnp
import numpy as np
from jax import lax
from jax.experimental import pallas as pl
from jax.experimental.pallas import tpu as pltpu

F32 = jnp.float32
BF16 = jnp.bfloat16
HEAD_DIM = 64
DILATIONS = (1, 4, 16)
NORM_EPS = 1e-6
NEG_BIG = -1e30
VMEM_LIMIT = 48 * 1024 * 1024
ROW_TILE = 256
LANE = 128
MESH = pl.DeviceIdType.MESH

ADAM_LR = 0.001
ADAM_B1 = 0.9
ADAM_B2 = 0.999
ADAM_EPS = 1e-08
ADAM_WD = 0.01
ADAM_STEP = 10

TILE_CANDIDATES = (1024, 1408, 768, 512, 384, 256, 128)


def _pick(dim, cands=TILE_CANDIDATES):
    for c in cands:
        if c <= dim and dim % c == 0:
            return c
    return dim


def _params(sem):
    return pltpu.CompilerParams(dimension_semantics=sem, vmem_limit_bytes=VMEM_LIMIT)


def _mm(a, b, mode, out_dtype, name, scale=None):
    if mode == "nn":
        m, k = a.shape
        n = b.shape[1]
    elif mode == "nt":
        m, k = a.shape
        n = b.shape[0]
    else:
        k, m = a.shape
        n = b.shape[1]
    tm, tn, tk = _pick(m), _pick(n), _pick(k)
    nk = k // tk
    if mode == "nn":
        a_spec = pl.BlockSpec((tm, tk), lambda i, j, kk: (i, kk))
        b_spec = pl.BlockSpec((tk, tn), lambda i, j, kk: (kk, j))
        dims = (((1,), (0,)), ((), ()))
    elif mode == "nt":
        a_spec = pl.BlockSpec((tm, tk), lambda i, j, kk: (i, kk))
        b_spec = pl.BlockSpec((tn, tk), lambda i, j, kk: (j, kk))
        dims = (((1,), (1,)), ((), ()))
    else:
        a_spec = pl.BlockSpec((tk, tm), lambda i, j, kk: (kk, i))
        b_spec = pl.BlockSpec((tk, tn), lambda i, j, kk: (kk, j))
        dims = (((0,), (0,)), ((), ()))

    def finish(acc):
        if scale is not None:
            acc = acc * scale
        return acc.astype(out_dtype)

    if nk == 1:
        def body(a_ref, b_ref, o_ref):
            o_ref[...] = finish(lax.dot_general(a_ref[...].astype(BF16), b_ref[...].astype(BF16), dims, preferred_element_type=F32))
        scratch = []
    else:
        def body(a_ref, b_ref, o_ref, acc_ref):
            kk = pl.program_id(2)

            @pl.when(kk == 0)
            def _():
                acc_ref[...] = jnp.zeros_like(acc_ref)

            acc_ref[...] += lax.dot_general(a_ref[...].astype(BF16), b_ref[...].astype(BF16), dims, preferred_element_type=F32)

            @pl.when(kk == nk - 1)
            def _():
                o_ref[...] = finish(acc_ref[...])
        scratch = [pltpu.VMEM((tm, tn), F32)]

    return pl.pallas_call(
        body, name=name,
        grid=(m // tm, n // tn, nk),
        in_specs=[a_spec, b_spec],
        out_specs=pl.BlockSpec((tm, tn), lambda i, j, kk: (i, j)),
        out_shape=jax.ShapeDtypeStruct((m, n), out_dtype),
        scratch_shapes=scratch,
        compiler_params=_params(("parallel", "parallel", "arbitrary")),
    )(a, b)


def _rstd(v):
    return lax.rsqrt(jnp.mean(v * v, axis=-1, keepdims=True) + NORM_EPS)


def _rms_bwd(dy, v, g, r):
    gy = dy * g
    dv = r * (gy - v * (r * r) * jnp.mean(gy * v, axis=-1, keepdims=True))
    return dv, dy * v * r


def _row_spec(t, width):
    return pl.BlockSpec((t, width), lambda i: (i, 0))


def _gain_spec(width):
    return pl.BlockSpec((1, width), lambda i: (0, 0))


def _norm_res_fwd(x, mix, g_post, pre_gains, name):
    s, d = x.shape
    t = _pick(s, (ROW_TILE,))
    has_mix = mix is not None
    n_pre = len(pre_gains)

    def body(*refs):
        x_ref = refs[0]
        pos = 1
        x1 = x_ref[...]
        if has_mix:
            mv = refs[1][...].astype(F32)
            x1 = x1 + mv * _rstd(mv) * refs[2][...]
            pos = 3
        gains = refs[pos:pos + n_pre]
        outs = refs[pos + n_pre:]
        if has_mix:
            outs[0][...] = x1
            outs = outs[1:]
        r = _rstd(x1)
        for g_ref, o_ref in zip(gains, outs):
            o_ref[...] = (x1 * r * g_ref[...]).astype(BF16)

    ins = [x] + ([mix, g_post] if has_mix else []) + list(pre_gains)
    in_specs = [_row_spec(t, d)] + ([_row_spec(t, d), _gain_spec(d)] if has_mix else []) + [_gain_spec(d)] * n_pre
    out_shape = ([jax.ShapeDtypeStruct((s, d), F32)] if has_mix else []) + [jax.ShapeDtypeStruct((s, d), BF16)] * n_pre
    out_specs = [_row_spec(t, d)] * len(out_shape)
    res = pl.pallas_call(
        body, name=name, grid=(s // t,), in_specs=in_specs, out_specs=out_specs, out_shape=out_shape,
        compiler_params=_params(("parallel",)),
    )(*ins)
    if has_mix:
        return res[0], list(res[1:])
    return x, list(res)


def _norm_res_loss(x, mix, g_post, target, name):
    s, d = x.shape
    t = _pick(s, (ROW_TILE,))

    def body(x_ref, m_ref, g_ref, t_ref, dy_ref, loss_ref):
        mv = m_ref[...].astype(F32)
        y = x_ref[...] + mv * _rstd(mv) * g_ref[...]
        err = y - t_ref[...]
        dy_ref[...] = err * (1.0 / d)

        @pl.when(pl.program_id(0) == 0)
        def _():
            loss_ref[...] = jnp.zeros_like(loss_ref)

        loss_ref[...] += jnp.sum(err * err)

    dy, acc = pl.pallas_call(
        body, name=name, grid=(s // t,),
        in_specs=[_row_spec(t, d), _row_spec(t, d), _gain_spec(d), _row_spec(t, d)],
        out_specs=[_row_spec(t, d), pl.BlockSpec((8, LANE), lambda i: (0, 0))],
        out_shape=[jax.ShapeDtypeStruct((s, d), F32), jax.ShapeDtypeStruct((8, LANE), F32)],
        compiler_params=_params(("arbitrary",)),
    )(x, mix, g_post, target)
    return dy, acc[0, 0] * (0.5 / d)


def _norm_bwd(dx_out, branches, x_in, post, name):
    s, d = dx_out.shape
    t = _pick(s, (ROW_TILE,))
    nb = len(branches)
    has_post = post is not None

    def body(*refs):
        dx_ref = refs[0]
        pos = 1
        dx = dx_ref[...]
        first = pl.program_id(0) == 0
        n_in = 1 + (1 + 2 * nb if nb else 0) + (2 if has_post else 0)
        outs = refs[n_in:]
        opos = 0
        if nb:
            xv = refs[pos][...]
            pos += 1
            r = _rstd(xv)
            dx_o = outs[0]
            opos = 1
            for _ in range(nb):
                dxn = refs[pos][...].astype(F32)
                g = refs[pos + 1][...]
                pos += 2
                dv, dg_rows = _rms_bwd(dxn, xv, g, r)
                dx = dx + dv
                dg_ref = outs[opos]
                opos += 1

                @pl.when(first)
                def _(dg_ref=dg_ref):
                    dg_ref[...] = jnp.zeros_like(dg_ref)

                dg_ref[...] += jnp.sum(dg_rows, axis=0, keepdims=True)
            dx_o[...] = dx
        if has_post:
            mv = refs[pos][...].astype(F32)
            g = refs[pos + 1][...]
            dm, dg_rows = _rms_bwd(dx, mv, g, _rstd(mv))
            outs[opos][...] = dm.astype(BF16)
            dg_ref = outs[opos + 1]

            @pl.when(first)
            def _():
                dg_ref[...] = jnp.zeros_like(dg_ref)

            dg_ref[...] += jnp.sum(dg_rows, axis=0, keepdims=True)

    ins, in_specs = [dx_out], [_row_spec(t, d)]
    out_shape, out_specs = [], []
    if nb:
        ins.append(x_in)
        in_specs.append(_row_spec(t, d))
        out_shape.append(jax.ShapeDtypeStruct((s, d), F32))
        out_specs.append(_row_spec(t, d))
        for dxn, g in branches:
            ins += [dxn, g]
            in_specs += [_row_spec(t, d), _gain_spec(d)]
            out_shape.append(jax.ShapeDtypeStruct((1, d), F32))
            out_specs.append(_gain_spec(d))
    if has_post:
        ins += [post[0], post[1]]
        in_specs += [_row_spec(t, d), _gain_spec(d)]
        out_shape += [jax.ShapeDtypeStruct((s, d), BF16), jax.ShapeDtypeStruct((1, d), F32)]
        out_specs += [_row_spec(t, d), _gain_spec(d)]
    res = pl.pallas_call(
        body, name=name, grid=(s // t,), in_specs=in_specs, out_specs=out_specs, out_shape=out_shape,
        compiler_params=_params(("arbitrary",)),
    )(*ins)
    res = list(res)
    dx_in = res.pop(0) if nb else dx_out
    dgs = [res.pop(0) for _ in range(nb)]
    dm, dg_post = (res[0], res[1]) if has_post else (None, None)
    return dx_in, dgs, dm, dg_post


HALO = 16


def _shift_down(u, prev, k):
    rows = lax.broadcasted_iota(jnp.int32, u.shape, 0)
    out = pltpu.roll(u, k, 0)
    for i in range(k):
        out = jnp.where(rows == i, prev[HALO - k + i:HALO - k + i + 1, :], out)
    return out


def _shift_up(u, nxt, k):
    n = u.shape[0]
    rows = lax.broadcasted_iota(jnp.int32, u.shape, 0)
    out = pltpu.roll(u, n - k, 0)
    for i in range(k):
        out = jnp.where(rows == n - k + i, nxt[i:i + 1, :], out)
    return out


def _conv_gate_fwd(p, cw, name):
    s, d3 = p.shape
    d = d3 // 3
    t = _pick(s, (ROW_TILE,))
    hb = t // HALO

    def body(p_ref, prev_ref, w_ref, z_ref):
        i = pl.program_id(0)
        pv = p_ref[...].astype(F32)
        b, u = pv[:, :d], pv[:, d:2 * d] * pv[:, 2 * d:]
        ph = prev_ref[...].astype(F32)
        up = jnp.where(i > 0, ph[:, d:2 * d] * ph[:, 2 * d:], 0.0)
        w = w_ref[...]
        y = w[0:1, :] * _shift_down(u, up, 2) + w[1:2, :] * _shift_down(u, up, 1) + w[2:3, :] * u
        z_ref[...] = (b * y).astype(BF16)

    return pl.pallas_call(
        body, name=name, grid=(s // t,),
        in_specs=[_row_spec(t, d3),
                  pl.BlockSpec((HALO, d3), lambda i: (jnp.maximum(i * hb - 1, 0), 0)),
                  pl.BlockSpec((3, d), lambda i: (0, 0))],
        out_specs=_row_spec(t, d),
        out_shape=jax.ShapeDtypeStruct((s, d), BF16),
        compiler_params=_params(("parallel",)),
    )(p, p, cw)


def _conv_gate_bwd(p, dz, cw, name):
    s, d3 = p.shape
    d = d3 // 3
    t = _pick(s, (ROW_TILE,))
    hb = t // HALO
    nt = s // t
    last_halo = s // HALO - 1

    def body(p_ref, prev_ref, next_ref, dz_ref, dznext_ref, w_ref, dp_ref, dw_ref):
        i = pl.program_id(0)
        pv = p_ref[...].astype(F32)
        b, c, h = pv[:, :d], pv[:, d:2 * d], pv[:, 2 * d:]
        u = c * h
        ph = prev_ref[...].astype(F32)
        up = jnp.where(i > 0, ph[:, d:2 * d] * ph[:, 2 * d:], 0.0)
        w = w_ref[...]
        u1, u2 = _shift_down(u, up, 1), _shift_down(u, up, 2)
        y = w[0:1, :] * u2 + w[1:2, :] * u1 + w[2:3, :] * u
        dz = dz_ref[...].astype(F32)
        dy = dz * b
        dyn = jnp.where(i < nt - 1, dznext_ref[...].astype(F32) * next_ref[...].astype(F32)[:, :d], 0.0)
        du = w[2:3, :] * dy + w[1:2, :] * _shift_up(dy, dyn, 1) + w[0:1, :] * _shift_up(dy, dyn, 2)
        dp_ref[:, :d] = (dz * y).astype(BF16)
        dp_ref[:, d:2 * d] = (du * h).astype(BF16)
        dp_ref[:, 2 * d:] = (du * c).astype(BF16)

        @pl.when(i == 0)
        def _():
            dw_ref[...] = jnp.zeros_like(dw_ref)

        dw_ref[0:1, :] += jnp.sum(dy * u2, axis=0, keepdims=True)
        dw_ref[1:2, :] += jnp.sum(dy * u1, axis=0, keepdims=True)
        dw_ref[2:3, :] += jnp.sum(dy * u, axis=0, keepdims=True)

    return pl.pallas_call(
        body, name=name, grid=(nt,),
        in_specs=[_row_spec(t, d3),
                  pl.BlockSpec((HALO, d3), lambda i: (jnp.maximum(i * hb - 1, 0), 0)),
                  pl.BlockSpec((HALO, d3), lambda i: (jnp.minimum((i + 1) * hb, last_halo), 0)),
                  _row_spec(t, d),
                  pl.BlockSpec((HALO, d), lambda i: (jnp.minimum((i + 1) * hb, last_halo), 0)),
                  pl.BlockSpec((3, d), lambda i: (0, 0))],
        out_specs=[_row_spec(t, d3), pl.BlockSpec((3, d), lambda i: (0, 0))],
        out_shape=[jax.ShapeDtypeStruct((s, d3), BF16), jax.ShapeDtypeStruct((3, d), F32)],
        compiler_params=_params(("arbitrary",)),
    )(p, p, p, dz, dz, cw)


def _swiglu_fwd(f, name):
    s, f2 = f.shape
    ff = f2 // 2
    t = _pick(s, (ROW_TILE,))

    def body(f_ref, a_ref):
        gate = f_ref[:, :ff].astype(F32)
        up = f_ref[:, ff:].astype(F32)
        a_ref[...] = (gate * jax.nn.sigmoid(gate) * up).astype(BF16)

    return pl.pallas_call(
        body, name=name, grid=(s // t,),
        in_specs=[_row_spec(t, f2)], out_specs=_row_spec(t, ff),
        out_shape=jax.ShapeDtypeStruct((s, ff), BF16),
        compiler_params=_params(("parallel",)),
    )(f)


def _swiglu_bwd(f, da, name):
    s, f2 = f.shape
    ff = f2 // 2
    t = _pick(s, (ROW_TILE,))

    def body(f_ref, da_ref, df_ref):
        gate = f_ref[:, :ff].astype(F32)
        up = f_ref[:, ff:].astype(F32)
        dav = da_ref[...].astype(F32)
        sg = jax.nn.sigmoid(gate)
        silu = gate * sg
        df_ref[:, :ff] = (dav * up * (sg + silu * (1.0 - sg))).astype(BF16)
        df_ref[:, ff:] = (dav * silu).astype(BF16)

    return pl.pallas_call(
        body, name=name, grid=(s // t,),
        in_specs=[_row_spec(t, f2), _row_spec(t, ff)], out_specs=_row_spec(t, f2),
        out_shape=jax.ShapeDtypeStruct((s, f2), BF16),
        compiler_params=_params(("parallel",)),
    )(f, da)


SUPER = 2048
RES = 16
PAIR = 128
L = 128


def _alibi_slopes(n_heads):
    h = np.arange(n_heads, dtype=np.float32) + 1.0
    return np.power(2.0, -8.0 * h / n_heads).astype(np.float32)


def _permute16(x, inverse, name):
    s, d = x.shape
    cw = LANE

    def body(x_ref, o_ref):
        if inverse:
            for m in range(L):
                o_ref[RES * m:RES * (m + 1), :] = x_ref[pl.ds(m, RES, stride=L), :]
        else:
            for r in range(RES):
                o_ref[L * r:L * (r + 1), :] = x_ref[pl.ds(r, L, stride=RES), :]

    spec = pl.BlockSpec((SUPER, cw), lambda i, j: (i, j))
    return pl.pallas_call(
        body, name=name, grid=(s // SUPER, d // cw), in_specs=[spec], out_specs=spec,
        out_shape=jax.ShapeDtypeStruct((s, d), x.dtype),
        compiler_params=_params(("parallel", "parallel")),
    )(x)


def _slope_table(d):
    nh = d // HEAD_DIM
    sl = _alibi_slopes(nh)
    tab = np.repeat(sl, HEAD_DIM).reshape(d // PAIR, 1, PAIR)
    return jnp.asarray(np.broadcast_to(tab, (d // PAIR, 8, PAIR)).copy())


def _geometry(dil):
    nch = RES // dil
    return nch, L // nch


def _band(dil):
    nch, w = _geometry(dil)
    sh = w.bit_length() - 1
    i = lax.broadcasted_iota(jnp.int32, (L, 2 * L), 0)
    j = lax.broadcasted_iota(jnp.int32, (L, 2 * L), 1)

    def pos(t):
        return jnp.bitwise_and(t, w - 1) * nch + jnp.right_shift(t, sh)

    delta = pos(i) + L - (pos(jnp.bitwise_and(j, L - 1)) + jnp.bitwise_and(j, L))
    return (delta * dil).astype(F32), (delta >= 0) & (delta <= L), j < L


def _offsets(dil, res, n):
    nch, w = _geometry(dil)

    def al(v):
        return v if isinstance(v, int) else pl.multiple_of(v, w)

    q_off = [al((a * dil + res) * L + n * w) for a in range(nch)]
    k_off = [al((a * dil + res) * 2 * L + L + n * w) for a in range(nch)]
    kp_off = [al((a * dil + res) * 2 * L + L + n * w - w) for a in range(nch)]
    return q_off, k_off, kp_off, w


def _gather(ref, offs, w):
    parts = [ref[pl.ds(o, w), :] for o in offs]
    return parts[0] if len(parts) == 1 else jnp.concatenate(parts, axis=0)


def _scatter(ref, offs, w, val, add=False):
    for a, o in enumerate(offs):
        piece = val[a * w:(a + 1) * w, :]
        if add:
            ref[pl.ds(o, w), :] += piece
        else:
            ref[pl.ds(o, w), :] = piece


def _fill_key_buffer(buf, prev_ref, cur_ref):
    for r in range(RES):
        buf[2 * L * r:2 * L * r + L, :] = prev_ref[L * r:L * (r + 1), :]
        buf[2 * L * r + L:2 * L * (r + 1), :] = cur_ref[L * r:L * (r + 1), :]


def _two_heads(x, low):
    zero = jnp.zeros_like(x)
    return jnp.concatenate([jnp.where(low, x, zero), jnp.where(low, zero, x)], axis=0)


def _loop_blocks(dil, do_block):
    nch, _ = _geometry(dil)
    if dil == RES:
        def it(r, c):
            do_block(r, 0)
            return c
        lax.fori_loop(0, RES, it, 0, unroll=2)
    else:
        def it(n, c):
            for res in range(dil):
                do_block(res, n)
            return c
        lax.fori_loop(0, nch, it, 0, unroll=2 if dil == 1 else 1)


NT = (((1,), (1,)), ((), ()))
TN = (((0,), (0,)), ((), ()))


def _attention_fwd(q, kv, name):
    s, d = q.shape
    g_n, ns = d // PAIR, s // SUPER

    def body(sl_ref, q_ref, kc_ref, kp_ref, vc_ref, vp_ref, o_ref, lse_ref, kbuf, vbuf, m_s, l_s, acc_s):
        sb = pl.program_id(1)
        _fill_key_buffer(kbuf, kp_ref, kc_ref)
        _fill_key_buffer(vbuf, vp_ref, vc_ref)
        low = lax.broadcasted_iota(jnp.int32, (L, PAIR), 1) < HEAD_DIM
        low_k = lax.broadcasted_iota(jnp.int32, (2 * L, PAIR), 1) < HEAD_DIM
        nsl = (-sl_ref[0:1, 0:1], -sl_ref[0:1, HEAD_DIM:HEAD_DIM + 1])
        for bi, dil in enumerate(DILATIONS):
            first_branch, last_branch = bi == 0, bi == len(DILATIONS) - 1
            base, band, prev_half = _band(dil)

            def do_block(res, n, dil=dil, first_branch=first_branch, last_branch=last_branch,
                         base=base, band=band, prev_half=prev_half):
                q_off, k_off, kp_off, w = _offsets(dil, res, n)
                qf = _gather(q_ref, q_off, w).astype(BF16)
                kcat = jnp.concatenate([_gather(kbuf, kp_off, w), _gather(kbuf, k_off, w)], axis=0).astype(BF16)
                vcat = jnp.concatenate([_gather(vbuf, kp_off, w), _gather(vbuf, k_off, w)], axis=0).astype(BF16)
                k_bd, v_bd = _two_heads(kcat, low_k), _two_heads(vcat, low_k)
                sc = lax.dot_general(qf, k_bd, NT, preferred_element_type=F32)
                starts_sequence = jnp.logical_and(sb == 0, n == 0)
                valid = band & jnp.logical_not(prev_half & starts_sequence)
                if not first_branch:
                    m_prev = _gather(m_s, q_off, w)
                ps, m_new, l_blk = [], [], []
                for h in range(2):
                    s_h = jnp.where(valid, sc[:, 2 * L * h:2 * L * (h + 1)] + nsl[h] * base, NEG_BIG)
                    mh = jnp.max(s_h, axis=1, keepdims=True)
                    if not first_branch:
                        mh = jnp.maximum(mh, m_prev[:, HEAD_DIM * h:HEAD_DIM * h + 1])
                    p_h = jnp.exp(s_h - mh)
                    ps.append(p_h.astype(BF16))
                    m_new.append(mh)
                    l_blk.append(jnp.sum(p_h, axis=1, keepdims=True))
                m_full = jnp.where(low, m_new[0], m_new[1])
                l_full = jnp.where(low, l_blk[0], l_blk[1])
                acc = jnp.dot(jnp.concatenate(ps, axis=1), v_bd, preferred_element_type=F32)
                if not first_branch:
                    alpha = jnp.exp(m_prev - m_full)
                    l_full = _gather(l_s, q_off, w) * alpha + l_full
                    acc = _gather(acc_s, q_off, w) * alpha + acc
                if last_branch:
                    _scatter(o_ref, q_off, w, acc / l_full)
                    _scatter(lse_ref, q_off, w, m_full + jnp.log(l_full))
                else:
                    _scatter(m_s, q_off, w, m_full)
                    _scatter(l_s, q_off, w, l_full)
                    _scatter(acc_s, q_off, w, acc)

            _loop_blocks(dil, do_block)

    prev = lambda i: jnp.maximum(i - 1, 0)
    blk = pl.BlockSpec((SUPER, PAIR), lambda g, i: (i, g))
    in_specs = [pl.BlockSpec((None, 8, PAIR), lambda g, i: (g, 0, 0)), blk,
                pl.BlockSpec((SUPER, PAIR), lambda g, i: (i, g)),
                pl.BlockSpec((SUPER, PAIR), lambda g, i: (prev(i), g)),
                pl.BlockSpec((SUPER, PAIR), lambda g, i: (i, g_n + g)),
                pl.BlockSpec((SUPER, PAIR), lambda g, i: (prev(i), g_n + g))]
    return pl.pallas_call(
        body, name=name, grid=(g_n, ns), in_specs=in_specs, out_specs=[blk, blk],
        out_shape=[jax.ShapeDtypeStruct((s, d), F32)] * 2,
        scratch_shapes=[pltpu.VMEM((2 * SUPER, PAIR), F32)] * 2 + [pltpu.VMEM((SUPER, PAIR), F32)] * 3,
        compiler_params=_params(("parallel", "parallel")),
    )(_slope_table(d), q, kv, kv, kv, kv)


def _attention_bwd(q, kv, o, do, lse, dkv_in, name):
    s, d = q.shape
    g_n, ns = d // PAIR, s // SUPER
    has_in = dkv_in is not None

    def body(*refs):
        sl_ref, q_ref, do_ref, o_ref, lse_ref, kc_ref, kp_ref, vc_ref, vp_ref = refs[:9]
        pos = 9
        if has_in:
            dk_in_ref, dv_in_ref = refs[9:11]
            pos = 11
        dq_ref, dk_ref, dv_ref, kbuf, vbuf, dkbuf, dvbuf, dq_s = refs[pos:]
        step = pl.program_id(1)
        sb = ns - 1 - step
        _fill_key_buffer(kbuf, kp_ref, kc_ref)
        _fill_key_buffer(vbuf, vp_ref, vc_ref)

        @pl.when(step == 0)
        def _():
            dkbuf[...] = jnp.zeros_like(dkbuf)
            dvbuf[...] = jnp.zeros_like(dvbuf)

        @pl.when(step > 0)
        def _():
            for buf in (dkbuf, dvbuf):
                for r in range(RES):
                    buf[2 * L * r + L:2 * L * (r + 1), :] = buf[2 * L * r:2 * L * r + L, :]
                    buf[2 * L * r:2 * L * r + L, :] = jnp.zeros((L, PAIR), F32)

        low = lax.broadcasted_iota(jnp.int32, (L, PAIR), 1) < HEAD_DIM
        low_k = lax.broadcasted_iota(jnp.int32, (2 * L, PAIR), 1) < HEAD_DIM
        nsl = (-sl_ref[0:1, 0:1], -sl_ref[0:1, HEAD_DIM:HEAD_DIM + 1])
        for bi, dil in enumerate(DILATIONS):
            first_branch = bi == 0
            base, band, prev_half = _band(dil)

            def do_block(res, n, dil=dil, first_branch=first_branch, base=base, band=band, prev_half=prev_half):
                q_off, k_off, kp_off, w = _offsets(dil, res, n)
                qb = _gather(q_ref, q_off, w).astype(BF16)
                dof = _gather(do_ref, q_off, w)
                prod = dof * _gather(o_ref, q_off, w)
                dob = dof.astype(BF16)
                lse_f = _gather(lse_ref, q_off, w)
                zero = jnp.zeros_like(prod)
                dsum = (jnp.sum(jnp.where(low, prod, zero), axis=1, keepdims=True),
                        jnp.sum(jnp.where(low, zero, prod), axis=1, keepdims=True))
                kcat = jnp.concatenate([_gather(kbuf, kp_off, w), _gather(kbuf, k_off, w)], axis=0).astype(BF16)
                vcat = jnp.concatenate([_gather(vbuf, kp_off, w), _gather(vbuf, k_off, w)], axis=0).astype(BF16)
                k_bd, v_bd = _two_heads(kcat, low_k), _two_heads(vcat, low_k)
                sc = lax.dot_general(qb, k_bd, NT, preferred_element_type=F32)
                dp = lax.dot_general(dob, v_bd, NT, preferred_element_type=F32)
                starts_sequence = jnp.logical_and(sb == 0, n == 0)
                valid = band & jnp.logical_not(prev_half & starts_sequence)
                ps, dss = [], []
                for h in range(2):
                    cols = slice(2 * L * h, 2 * L * (h + 1))
                    lse_h = lse_f[:, HEAD_DIM * h:HEAD_DIM * h + 1]
                    p_h = jnp.exp(jnp.where(valid, sc[:, cols] + nsl[h] * base - lse_h, NEG_BIG))
                    dss.append((p_h * (dp[:, cols] - dsum[h])).astype(BF16))
                    ps.append(p_h.astype(BF16))
                ds_cat, p_cat = jnp.concatenate(dss, axis=1), jnp.concatenate(ps, axis=1)
                dq = jnp.dot(ds_cat, k_bd, preferred_element_type=F32)
                dk_bd = lax.dot_general(ds_cat, qb, TN, preferred_element_type=F32)
                dv_bd = lax.dot_general(p_cat, dob, TN, preferred_element_type=F32)
                dk = jnp.where(low_k, dk_bd[:2 * L], dk_bd[2 * L:])
                dv = jnp.where(low_k, dv_bd[:2 * L], dv_bd[2 * L:])
                _scatter(dq_s, q_off, w, dq, add=not first_branch)
                _scatter(dkbuf, kp_off, w, dk[:L], add=True)
                _scatter(dkbuf, k_off, w, dk[L:], add=True)
                _scatter(dvbuf, kp_off, w, dv[:L], add=True)
                _scatter(dvbuf, k_off, w, dv[L:], add=True)

            _loop_blocks(dil, do_block)

        dq_ref[...] = dq_s[...].astype(BF16)
        for r in range(RES):
            rows, cur = slice(L * r, L * (r + 1)), slice(2 * L * r + L, 2 * L * (r + 1))
            if has_in:
                dk_ref[rows, :] = dkbuf[cur, :] + dk_in_ref[rows, :]
                dv_ref[rows, :] = dvbuf[cur, :] + dv_in_ref[rows, :]
            else:
                dk_ref[rows, :] = dkbuf[cur, :]
                dv_ref[rows, :] = dvbuf[cur, :]

    rev = lambda i: ns - 1 - i
    prev = lambda i: jnp.maximum(ns - 2 - i, 0)
    blk = pl.BlockSpec((SUPER, PAIR), lambda g, i: (rev(i), g))
    in_specs = [pl.BlockSpec((None, 8, PAIR), lambda g, i: (g, 0, 0)), blk, blk, blk, blk,
                pl.BlockSpec((SUPER, PAIR), lambda g, i: (rev(i), g)),
                pl.BlockSpec((SUPER, PAIR), lambda g, i: (prev(i), g)),
                pl.BlockSpec((SUPER, PAIR), lambda g, i: (rev(i), g_n + g)),
                pl.BlockSpec((SUPER, PAIR), lambda g, i: (prev(i), g_n + g))]
    ins = [_slope_table(d), q, do, o, lse, kv, kv, kv, kv]
    if has_in:
        in_specs += [blk, blk]
        ins += list(dkv_in)
    res = pl.pallas_call(
        body, name=name, grid=(g_n, ns), in_specs=in_specs, out_specs=[blk, blk, blk],
        out_shape=[jax.ShapeDtypeStruct((s, d), BF16)] + [jax.ShapeDtypeStruct((s, d), F32)] * 2,
        scratch_shapes=[pltpu.VMEM((2 * SUPER, PAIR), F32)] * 4 + [pltpu.VMEM((SUPER, PAIR), F32)],
        compiler_params=_params(("parallel", "arbitrary")),
    )(*ins)
    return res[0], (res[1], res[2])


def _coords():
    return lax.axis_index("x"), lax.axis_index("y"), lax.axis_index("c")


def _chip_peers(x, y):
    return [(1 - x, y), (x, 1 - y), (1 - x, 1 - y)]


def _block_of(ref, axis, blk, size):
    start = pl.multiple_of(blk * size, size)
    if axis == 1:
        return ref.at[:, pl.ds(start, size)]
    return ref.at[pl.ds(start, size), :]


ANY = pl.BlockSpec(memory_space=pl.ANY)


HBM = pl.BlockSpec(memory_space=pltpu.HBM)
SEM = pl.BlockSpec(memory_space=pltpu.SEMAPHORE)
SPLIT = pltpu.CompilerParams(has_side_effects=pltpu.SideEffectType.DATAFLOW_SIDE_EFFECTING)


def _in_hbm(a):
    return pltpu.with_memory_space_constraint(a, pltpu.HBM)


def _thru(arrays):
    return [pltpu.HBM(a.shape, a.dtype) for a in arrays]


def _cast_place(w, ax, dtype, name):
    k, n = w.shape
    t = _pick(k, (512, 256, 128))
    nb = k // t

    def body(blk_ref, w_ref, b_ref, f_ref):
        v = w_ref[...].astype(dtype)
        b_ref[...] = v
        f_ref[...] = v

    full_shape = (k, 4 * n) if ax == 1 else (4 * k, n)
    place = (lambda i, blk: (i, blk[0])) if ax == 1 else (lambda i, blk: (blk[0] * nb + i, 0))
    spec = pl.BlockSpec((t, n), lambda i, blk: (i, 0))
    return pl.pallas_call(
        body, name=name,
        grid_spec=pltpu.PrefetchScalarGridSpec(
            num_scalar_prefetch=1, grid=(nb,), in_specs=[spec], out_specs=[spec, pl.BlockSpec((t, n), place)]),
        out_shape=[jax.ShapeDtypeStruct((k, n), dtype), jax.ShapeDtypeStruct(full_shape, dtype)],
        compiler_params=_params(("parallel",)),
    )(_my_block()[None], w)


def _my_block():
    return (2 * lax.axis_index("x") + lax.axis_index("y")).astype(jnp.int32)


def _gather_start(group, carry, name):
    n, nc = len(group), len(carry)

    def body(*refs):
        blocks, fulls, send_sem, recv_sem = refs[:n], refs[n:2 * n], refs[2 * n + nc], refs[2 * n + nc + 1]
        x, y, c = _coords()
        for t, (b, _, ax) in enumerate(group):
            mine = _block_of(fulls[t], ax, 2 * x + y, b.shape[ax])
            for j, (px, py) in enumerate(_chip_peers(x, y)):
                pltpu.make_async_remote_copy(
                    src_ref=blocks[t], dst_ref=mine, send_sem=send_sem.at[3 * t + j], recv_sem=recv_sem.at[3 * t + j],
                    device_id=(px, py, c), device_id_type=MESH).start()

    arrays = [b for b, _, _ in group] + [f for _, f, _ in group] + list(carry)
    sems = [pltpu.SemaphoreType.DMA((3 * n,))] * 2
    res = pl.pallas_call(
        body, name=name, in_specs=[HBM] * len(arrays), out_specs=[SEM, SEM] + [HBM] * len(arrays),
        out_shape=sems + _thru(arrays), input_output_aliases={i: 2 + i for i in range(len(arrays))},
        compiler_params=SPLIT,
    )(*[_in_hbm(a) for a in arrays])
    return (res[0], res[1], list(res[2:2 + n]), list(res[2 + n:2 + 2 * n])), list(res[2 + 2 * n:])


def _gather_wait(group, started, after, name):
    sends, recvs, blocks, fulls = started
    m = len(group)

    def body(*refs):
        blk_refs, full_refs, send_sem, recv_sem = refs[:m], refs[m:2 * m], refs[2 * m], refs[2 * m + 1]
        x, y, c = _coords()
        for t, (b, _, ax) in enumerate(group):
            for j, (px, py) in enumerate(_chip_peers(x, y)):
                cp = pltpu.make_async_remote_copy(
                    src_ref=blk_refs[t], dst_ref=_block_of(full_refs[t], ax, 2 * px + py, b.shape[ax]),
                    send_sem=send_sem.at[3 * t + j], recv_sem=recv_sem.at[3 * t + j],
                    device_id=(px, py, c), device_id_type=MESH)
                cp.wait_send()
                cp.wait_recv()

    extra = [] if after is None else [after]
    res = pl.pallas_call(
        body, name=name, in_specs=[HBM] * (2 * m) + [SEM, SEM] + [ANY] * len(extra), out_specs=[HBM] * (2 * m),
        out_shape=_thru(blocks) + _thru(fulls), input_output_aliases={i: i for i in range(2 * m)},
        compiler_params=SPLIT,
    )(*blocks, *fulls, sends, recvs, *extra)
    return list(res[m:])


def _scatter_start(grads, carry, name):
    n = len(grads)
    n_in = 2 * n + len(carry)

    def body(*refs):
        g_refs, st_refs, send_sem, recv_sem = refs[:n], refs[n:2 * n], refs[n_in], refs[n_in + 1]
        x, y, c = _coords()
        for t, (g, ax) in enumerate(grads):
            for j, (px, py) in enumerate(_chip_peers(x, y)):
                pltpu.make_async_remote_copy(
                    src_ref=_block_of(g_refs[t], ax, 2 * px + py, g.shape[ax] // 4), dst_ref=st_refs[t].at[j],
                    send_sem=send_sem.at[3 * t + j], recv_sem=recv_sem.at[3 * t + j],
                    device_id=(px, py, c), device_id_type=MESH).start()

    arrays = [g for g, _ in grads]
    for g, ax in grads:
        shape = list(g.shape)
        shape[ax] //= 4
        arrays.append(lax.empty((3, *shape), g.dtype))
    arrays += list(carry)
    sems = [pltpu.SemaphoreType.DMA((3 * n,))] * 2
    res = pl.pallas_call(
        body, name=name, in_specs=[HBM] * n_in, out_specs=[SEM, SEM] + [HBM] * n_in,
        out_shape=sems + _thru(arrays), input_output_aliases={i: 2 + i for i in range(n_in)},
        compiler_params=SPLIT,
    )(*[_in_hbm(a) for a in arrays])
    return (res[0], res[1], list(res[2:2 + n]), list(res[2 + n:2 + 2 * n])), list(res[2 + 2 * n:])


def _scatter_wait(axes, started, name):
    sends, recvs, full, stacks = started
    n = len(full)

    def body(*refs):
        g_refs, st_refs, send_sem, recv_sem = refs[:n], refs[n:2 * n], refs[2 * n], refs[2 * n + 1]
        x, y, c = _coords()
        for t, ax in enumerate(axes):
            size = full[t].shape[ax] // 4
            for j, (px, py) in enumerate(_chip_peers(x, y)):
                cp = pltpu.make_async_remote_copy(
                    src_ref=_block_of(g_refs[t], ax, 2 * px + py, size), dst_ref=st_refs[t].at[j],
                    send_sem=send_sem.at[3 * t + j], recv_sem=recv_sem.at[3 * t + j],
                    device_id=(px, py, c), device_id_type=MESH)
                cp.wait_send()
                cp.wait_recv()

    res = pl.pallas_call(
        body, name=name, in_specs=[HBM] * (2 * n) + [SEM, SEM], out_specs=[HBM] * (2 * n),
        out_shape=_thru(full) + _thru(stacks), input_output_aliases={i: i for i in range(2 * n)},
        compiler_params=SPLIT,
    )(*full, *stacks, sends, recvs)
    return list(res[:n]), list(res[n:])


def _pair_exchange(items, name):
    n = len(items)

    def body(*refs):
        g_refs, st_refs, outs, send_sems, recv_sems = refs[:n], refs[n:2 * n], refs[2 * n:3 * n], refs[3 * n], refs[3 * n + 1]
        x, y, c = _coords()
        copies = []
        for t, (g, ax, _) in enumerate(items):
            own = _block_of(g_refs[t], ax, 2 * x + y, g.shape[ax] // 4)
            for k, (src, dst) in enumerate([(own, outs[t].at[0]), (st_refs[t], outs[t].at[pl.ds(1, 3)])]):
                cp = pltpu.make_async_remote_copy(
                    src_ref=src, dst_ref=dst, send_sem=send_sems.at[2 * t + k], recv_sem=recv_sems.at[2 * t + k],
                    device_id=(x, y, 1 - c), device_id_type=MESH)
                cp.start()
                copies.append(cp)
        for cp in copies:
            cp.wait()

    return pl.pallas_call(
        body, name=name, in_specs=[ANY] * (2 * n), out_specs=[ANY] * n,
        out_shape=[jax.ShapeDtypeStruct((4, *st.shape[1:]), st.dtype) for _, _, st in items],
        scratch_shapes=[pltpu.SemaphoreType.DMA((2 * n,)), pltpu.SemaphoreType.DMA((2 * n,))],
        compiler_params=pltpu.CompilerParams(has_side_effects=True),
    )(*[g for g, _, _ in items], *[st for _, _, st in items])


def _allreduce_small(v, name):
    r, cdim = v.shape

    def body(v_ref, out_ref, buf, send_sems, recv_sems):
        x, y, c = _coords()
        me = 4 * x + 2 * y + c
        buf[0] = v_ref[...]
        sends = []
        for k in range(1, 8):
            peer = (x if not (k & 4) else 1 - x, y if not (k & 2) else 1 - y, c if not (k & 1) else 1 - c)
            cp = pltpu.make_async_remote_copy(
                src_ref=v_ref, dst_ref=buf.at[k], send_sem=send_sems.at[k - 1], recv_sem=recv_sems.at[k - 1],
                device_id=peer, device_id_type=MESH)
            cp.start()
            sends.append(cp)
        for cp in sends:
            cp.wait_recv()
        total = buf[me]
        for src in range(1, 8):
            total = total + buf[jnp.bitwise_xor(me, src)]
        out_ref[...] = total
        for cp in sends:
            cp.wait_send()

    return pl.pallas_call(
        body, name=name,
        in_specs=[pl.BlockSpec(memory_space=pltpu.VMEM)], out_specs=pl.BlockSpec(memory_space=pltpu.VMEM),
        out_shape=jax.ShapeDtypeStruct((r, cdim), F32),
        scratch_shapes=[pltpu.VMEM((8, r, cdim), F32), pltpu.SemaphoreType.DMA((7,)), pltpu.SemaphoreType.DMA((7,))],
        compiler_params=pltpu.CompilerParams(has_side_effects=True),
    )(v)


def _adamw_math(w, g, m, v):
    m = ADAM_B1 * m + (1.0 - ADAM_B1) * g
    v = ADAM_B2 * v + (1.0 - ADAM_B2) * jnp.square(g)
    m_hat = m / (1.0 - ADAM_B1 ** ADAM_STEP)
    v_hat = v / (1.0 - ADAM_B2 ** ADAM_STEP)
    delta = -ADAM_LR * (m_hat / (jnp.sqrt(v_hat) + ADAM_EPS) + ADAM_WD * w)
    return delta, m, v


def _adamw(w, m, v, grads, name):
    r, cdim = w.shape
    paired = isinstance(grads, list)
    layers = len(grads) if paired else 1
    t = _pick(r // layers, (64, 32, 16, 8))
    per_layer = r // layers // t
    n_grad = 3 * layers if paired else 1

    def body(*refs):
        refs = refs[1:] if paired else refs
        w_ref, m_ref, v_ref = refs[:3]
        outs = refs[3 + n_grad:]

        def update(g):
            delta, m_new, v_new = _adamw_math(w_ref[...], g, m_ref[...], v_ref[...])
            outs[0][...] = g
            outs[1][...] = delta
            outs[2][...] = m_new
            outs[3][...] = v_new

        if not paired:
            update(refs[3][...])
            return
        layer = pl.program_id(0) // per_layer
        for l in range(layers):
            @pl.when(layer == l)
            def _(own_ref=refs[3 + 3 * l], st_ref=refs[4 + 3 * l], sib_ref=refs[5 + 3 * l]):
                sa = own_ref[...].astype(F32)
                sb = sib_ref[0].astype(F32)
                for k in range(3):
                    sa = sa + st_ref[k].astype(F32)
                    sb = sb + sib_ref[k + 1].astype(F32)
                update(sa + sb)

    out_shape = [jax.ShapeDtypeStruct((r, cdim), F32)] * 4
    if not paired:
        spec = pl.BlockSpec((t, cdim), lambda i: (i, 0))
        return pl.pallas_call(
            body, name=name, grid=(r // t,), in_specs=[spec] * 4, out_specs=[spec] * 4, out_shape=out_shape,
            compiler_params=_params(("parallel",)),
        )(w, m, v, grads)

    spec = pl.BlockSpec((t, cdim), lambda i, blk: (i, 0))
    ins, in_specs = [w, m, v], [spec] * 3
    for l, (g, ax, stack, sib) in enumerate(grads):
        row = lambda i, l=l: jnp.clip(i - l * per_layer, 0, per_layer - 1)
        own = ((lambda i, blk, row=row: (row(i), blk[0])) if ax == 1
               else (lambda i, blk, row=row: (blk[0] * per_layer + row(i), 0)))
        ins += [g, stack, sib]
        in_specs += [pl.BlockSpec((t, cdim), own),
                     pl.BlockSpec((3, t, cdim), lambda i, blk, row=row: (0, row(i), 0)),
                     pl.BlockSpec((4, t, cdim), lambda i, blk, row=row: (0, row(i), 0))]
    return pl.pallas_call(
        body, name=name,
        grid_spec=pltpu.PrefetchScalarGridSpec(
            num_scalar_prefetch=1, grid=(r // t,), in_specs=in_specs, out_specs=[spec] * 4),
        out_shape=out_shape, compiler_params=_params(("parallel",)),
    )(_my_block()[None], *ins)


def _local_step(x, target, gains, conv_ws, kv_gain, weights_of, send_grads):
    depth = len(gains)
    n_a = len(conv_ws)
    saved, ws = [], []
    kv = kvn = None
    _, (xn,) = _norm_res_fwd(x, None, None, [gains[0][0]], "norm_first")
    h = x
    for l in range(depth):
        g = gains[l]
        sv = {"x_in": h, "xn": xn}
        w = weights_of(l, h)
        ws.append(w)
        if l == n_a:
            kv = _mm(kvn, w["kv"], "nn", F32, "kv_fwd")
        if l < n_a:
            p = _mm(xn, w["conv_in"], "nn", BF16, f"conv_in_fwd_{l}")
            z = _conv_gate_fwd(p, conv_ws[l], f"conv_gate_fwd_{l}")
            mix = _mm(z, w["conv_out"], "nn", BF16, f"conv_out_fwd_{l}")
            sv.update(p=p, z=z)
        else:
            j = l - n_a
            q = _mm(xn, w["q"], "nn", F32, f"q_fwd_{j}", scale=HEAD_DIM ** -0.5)
            o, lse = _attention_fwd(q, kv, f"attn_fwd_{j}")
            mix = _mm(o, w["o"], "nn", BF16, f"o_fwd_{j}")
            sv.update(q=q, o=o, lse=lse)
        x1, (xn2,) = _norm_res_fwd(h, mix, g[1], [g[2]], f"norm_mid_{l}")
        f = _mm(xn2, w["ffn_in"], "nn", BF16, f"ffn_in_fwd_{l}")
        a = _swiglu_fwd(f, f"swiglu_fwd_{l}")
        ff = _mm(a, w["ffn_out"], "nn", BF16, f"ffn_out_fwd_{l}")
        sv.update(mix=mix, x1=x1, xn2=xn2, f=f, a=a, ff=ff)
        saved.append(sv)
        if l == depth - 1:
            dx, loss = _norm_res_loss(x1, ff, g[3], target, "norm_loss")
        else:
            if l == n_a - 1:
                h, _ = _norm_res_fwd(x1, ff, g[3], [], f"norm_end_{l}")
                h = _permute16(h, False, "permute_stream")
                target = _permute16(target, False, "permute_target")
                _, (xn, kvn) = _norm_res_fwd(h, None, None, [gains[l + 1][0], kv_gain], "norm_permuted")
            else:
                h, (xn,) = _norm_res_fwd(x1, ff, g[3], [gains[l + 1][0]], f"norm_end_{l}")
    d_gains = [[None] * 4 for _ in range(depth)]
    d_conv = [None] * n_a
    d_kv_gain = None
    dkv = None
    _, _, dff, d_gains[depth - 1][3] = _norm_bwd(dx, [], None, (saved[-1]["ff"], gains[-1][3]), "norm_loss_bwd")
    for l in reversed(range(depth)):
        sv, g, w, grads = saved[l], gains[l], ws[l], {}
        da = _mm(dff, w["ffn_out"], "nt", BF16, f"ffn_out_dx_{l}")
        grads["ffn_out"] =_mm(sv["a"], dff, "tn", BF16, f"ffn_out_dw_{l}")
        df = _swiglu_bwd(sv["f"], da, f"swiglu_bwd_{l}")
        dxn2 = _mm(df, w["ffn_in"], "nt", BF16, f"ffn_in_dx_{l}")
        grads["ffn_in"] =_mm(sv["xn2"], df, "tn", BF16, f"ffn_in_dw_{l}")
        dx, (d_gains[l][2],), dmix, d_gains[l][1] = _norm_bwd(
            dx, [(dxn2, g[2])], sv["x1"], (sv["mix"], g[1]), f"norm_mid_bwd_{l}")
        if l < n_a:
            dz = _mm(dmix, w["conv_out"], "nt", BF16, f"conv_out_dx_{l}")
            grads["conv_out"] =_mm(sv["z"], dmix, "tn", BF16, f"conv_out_dw_{l}")
            dp, d_conv[l] = _conv_gate_bwd(sv["p"], dz, conv_ws[l], f"conv_gate_bwd_{l}")
            dxn = _mm(dp, w["conv_in"], "nt", BF16, f"conv_in_dx_{l}")
            grads["conv_in"] =_mm(sv["xn"], dp, "tn", BF16, f"conv_in_dw_{l}")
        else:
            j = l - n_a
            do = _mm(dmix, w["o"], "nt", F32, f"o_dx_{j}")
            grads["o"] =_mm(sv["o"], dmix, "tn", BF16, f"o_dw_{j}")
            dq, dkv = _attention_bwd(sv["q"], kv, sv["o"], do, sv["lse"], dkv, f"attn_bwd_{j}")
            scale = HEAD_DIM ** -0.5
            dxn = _mm(dq, w["q"], "nt", BF16, f"q_dx_{j}", scale=scale)
            grads["q"] =_mm(sv["xn"], dq, "tn", BF16, f"q_dw_{j}", scale=scale)
        branches = [(dxn, g[0])]
        if l == n_a:
            dkv_cat = jnp.concatenate([dkv[0], dkv[1]], axis=1).astype(BF16)
            dkvn = _mm(dkv_cat, w["kv"], "nt", BF16, "kv_dx")
            grads["kv"] =_mm(kvn, dkv_cat, "tn", BF16, "kv_dw")
            branches.append((dkvn, kv_gain))
        post = (saved[l - 1]["ff"], gains[l - 1][3]) if l > 0 else None
        if l == n_a:
            dx, dgs, _, _ = _norm_bwd(dx, branches, sv["x_in"], None, f"norm_end_bwd_{l}")
            dx = _permute16(dx, True, "unpermute_stream")
            _, _, dff, dg_post = _norm_bwd(dx, [], None, post, "norm_boundary_bwd")
        else:
            dx, dgs, dff, dg_post = _norm_bwd(dx, branches, sv["x_in"], post, f"norm_end_bwd_{l}")
        dx = send_grads(l, grads, dx)
        d_gains[l][0] = dgs[0]
        if l == n_a:
            d_kv_gain = dgs[1]
        if l > 0:
            d_gains[l - 1][3] = dg_post
    return loss, dx, d_gains, d_conv, d_kv_gain


BIG = (
    ("conv_in", 1), ("conv_out", 0), ("kv", 1), ("q", 0), ("o", 0), ("ffn_in", 1), ("ffn_out", 0))


def kernel(x, norm_g, conv_in_w, conv_w, conv_out_w, kv_norm_g, kv_w, q_w, o_w, ffn_in_w, ffn_out_w, loss_target, m_norm_g, m_conv_in_w, m_conv_w, m_conv_out_w, m_kv_norm_g, m_kv_w, m_q_w, m_o_w, m_ffn_in_w, m_ffn_out_w, v_norm_g, v_conv_in_w, v_conv_w, v_conv_out_w, v_kv_norm_g, v_kv_w, v_q_w, v_o_w, v_ffn_in_w, v_ffn_out_w):
    depth, _, dq = norm_g.shape
    d = 4 * dq
    n_a = conv_w.shape[0]
    big_w = {"conv_in": conv_in_w, "conv_out": conv_out_w, "kv": kv_w[None], "q": q_w, "o": o_w,
             "ffn_in": ffn_in_w, "ffn_out": ffn_out_w}
    big_m = {"conv_in": m_conv_in_w, "conv_out": m_conv_out_w, "kv": m_kv_w[None], "q": m_q_w, "o": m_o_w,
             "ffn_in": m_ffn_in_w, "ffn_out": m_ffn_out_w}
    big_v = {"conv_in": v_conv_in_w, "conv_out": v_conv_out_w, "kv": v_kv_w[None], "q": v_q_w, "o": v_o_w,
             "ffn_in": v_ffn_in_w, "ffn_out": v_ffn_out_w}

    n_gain, n_tap = depth * 4, n_a * conv_w.shape[1]
    small_rows = -(-(n_gain + n_tap + 1) // 8) * 8
    pad_rows = small_rows - n_gain - n_tap

    def pack_small(gains, taps):
        return jnp.concatenate([gains.reshape(n_gain, dq), taps.reshape(n_tap, dq), jnp.zeros((pad_rows, dq), F32)])

    axis_of = dict(BIG)

    def matrices_of(l):
        if l < n_a:
            return [("conv_in", l), ("conv_out", l), ("ffn_in", l), ("ffn_out", l)]
        return ([("kv", 0)] if l == n_a else []) + [("q", l - n_a), ("o", l - n_a), ("ffn_in", l), ("ffn_out", l)]

    groups = [[(*_cast_place(big_w[name][i], axis_of[name], BF16, f"place_{name}_{i}"), axis_of[name])
               for name, i in matrices_of(l)] for l in range(depth)]
    groups[0].append((*_cast_place(pack_small(norm_g, conv_w), 1, F32, "place_small"), 1))
    started = {0: _gather_start(groups[0], [], "gather_start_0")[0]}

    def fetch(l, after):
        full = _gather_wait(groups[l], started[l], after, f"gather_wait_{l}")
        if l + 1 < depth:
            started[l + 1], full = _gather_start(groups[l + 1], full, f"gather_start_{l + 1}")
        return full

    first = fetch(0, None)
    small = first[-1]
    gains = [[small[4 * l + i][None] for i in range(4)] for l in range(depth)]
    conv_ws = [small[n_gain + 3 * l:n_gain + 3 * l + 3] for l in range(n_a)]
    kv_gain = kv_norm_g[None]

    def weights_of(l, after):
        full = first if l == 0 else fetch(l, after)
        return {name: full[t] for t, (name, _) in enumerate(matrices_of(l))}

    sent = {}

    def send_grads(l, grads, dx):
        sent[l], (dx,) = _scatter_start([(grads[name], axis_of[name]) for name, _ in matrices_of(l)], [dx],
                                        f"scatter_start_{l}")
        return dx

    loss, dx, d_gains, d_conv, d_kv_gain = _local_step(
        x[0], loss_target[0], gains, conv_ws, kv_gain, weights_of, send_grads)
    loss = lax.psum(loss, ("x", "y", "c"))

    small_g = jnp.concatenate([dg for row in d_gains for dg in row] + list(d_conv) + [d_kv_gain]
                              + [jnp.zeros((pad_rows - 1, d), F32)])
    small_g = _allreduce_small(small_g, "allreduce_small")
    blk = 2 * lax.axis_index("x") + lax.axis_index("y")
    mine_small = lax.dynamic_slice_in_dim(small_g, blk * dq, dq, axis=1)
    kv_rows = d // dq

    def pack_opt(gains_like, taps_like, kv_like):
        rows = jnp.concatenate([gains_like.reshape(n_gain, dq), taps_like.reshape(n_tap, dq), kv_like.reshape(kv_rows, dq)])
        extra = -rows.shape[0] % 8
        return jnp.concatenate([rows, jnp.zeros((extra, dq), F32)]) if extra else rows

    sw = pack_opt(norm_g, conv_w, kv_norm_g)
    sm = pack_opt(m_norm_g, m_conv_w, m_kv_norm_g)
    sv = pack_opt(v_norm_g, v_conv_w, v_kv_norm_g)
    sg = pack_opt(mine_small[:n_gain], mine_small[n_gain:n_gain + n_tap], small_g[n_gain + n_tap])
    s_out = _adamw(sw, sm, sv, sg, "adamw_small")

    def unpack(a):
        return (a[:n_gain].reshape(depth, 4, dq), a[n_gain:n_gain + n_tap].reshape(n_a, -1, dq),
                a[n_gain + n_tap:n_gain + n_tap + kv_rows].reshape(d))

    small_out = [unpack(a) for a in s_out]

    mine = {}
    for l in reversed(range(depth)):
        full, stacks = _scatter_wait([axis_of[name] for name, _ in matrices_of(l)], sent[l], f"scatter_wait_{l}")
        mine.update(zip(matrices_of(l), zip(full, stacks)))
    order = [(name, i) for name, _ in BIG for i in range(big_w[name].shape[0])]
    sibling = dict(zip(order, _pair_exchange([(mine[k][0], axis_of[k[0]], mine[k][1]) for k in order], "pair_exchange")))
    big_out = {}
    for name, ax in BIG:
        shp = big_w[name].shape
        rows, cols = shp[0] * shp[1], shp[2]
        flat = lambda a: a.reshape(rows, cols)
        res = _adamw(flat(big_w[name]), flat(big_m[name]), flat(big_v[name]),
                     [(mine[name, i][0], ax, mine[name, i][1], sibling[name, i]) for i in range(shp[0])],
                     f"adamw_{name}")
        out_shp = shp[1:] if name == "kv" else shp
        big_out[name] = [a.reshape(out_shp) for a in res]

    def leaves(i):
        ng, cw_, kg = small_out[i]
        return [ng, big_out["conv_in"][i], cw_, big_out["conv_out"][i], kg, big_out["kv"][i], big_out["q"][i],
                big_out["o"][i], big_out["ffn_in"][i], big_out["ffn_out"][i]]

    return (loss, dx[None], *leaves(0), *leaves(1), *leaves(2), *leaves(3))
```

```python
import functools

import jax
import jax.numpy as jnp
import numpy as np
from jax import lax
from jax.experimental import pallas as pl
from jax.experimental.pallas import tpu as pltpu

F32 = jnp.float32
BF16 = jnp.bfloat16
HEAD_DIM = 64
DILATIONS = (1, 4, 16)
NORM_EPS = 1e-6
NEG_BIG = -1e30
VMEM_LIMIT = 48 * 1024 * 1024
ROW_TILE = 256
LANE = 128
MESH = pl.DeviceIdType.MESH

ADAM_LR = 0.001
ADAM_B1 = 0.9
ADAM_B2 = 0.999
ADAM_EPS = 1e-08
ADAM_WD = 0.01
ADAM_STEP = 10

TILE_CANDIDATES = (1024, 1408, 768, 512, 384, 256, 128)


def _pick(dim, cands=TILE_CANDIDATES):
    for c in cands:
        if c <= dim and dim % c == 0:
            return c
    return dim


def _params(sem):
    return pltpu.CompilerParams(dimension_semantics=sem, vmem_limit_bytes=VMEM_LIMIT)


def _mm(a, b, mode, out_dtype, name, scale=None):
    if mode == "nn":
        m, k = a.shape
        n = b.shape[1]
    elif mode == "nt":
        m, k = a.shape
        n = b.shape[0]
    else:
        k, m = a.shape
        n = b.shape[1]
    tm, tn, tk = _pick(m), _pick(n), _pick(k)
    nk = k // tk
    if mode == "nn":
        a_spec = pl.BlockSpec((tm, tk), lambda i, j, kk: (i, kk))
        b_spec = pl.BlockSpec((tk, tn), lambda i, j, kk: (kk, j))
        dims = (((1,), (0,)), ((), ()))
    elif mode == "nt":
        a_spec = pl.BlockSpec((tm, tk), lambda i, j, kk: (i, kk))
        b_spec = pl.BlockSpec((tn, tk), lambda i, j, kk: (j, kk))
        dims = (((1,), (1,)), ((), ()))
    else:
        a_spec = pl.BlockSpec((tk, tm), lambda i, j, kk: (kk, i))
        b_spec = pl.BlockSpec((tk, tn), lambda i, j, kk: (kk, j))
        dims = (((0,), (0,)), ((), ()))

    def finish(acc):
        if scale is not None:
            acc = acc * scale
        return acc.astype(out_dtype)

    if nk == 1:
        def body(a_ref, b_ref, o_ref):
            o_ref[...] = finish(lax.dot_general(a_ref[...].astype(BF16), b_ref[...].astype(BF16), dims, preferred_element_type=F32))
        scratch = []
    else:
        def body(a_ref, b_ref, o_ref, acc_ref):
            kk = pl.program_id(2)

            @pl.when(kk == 0)
            def _():
                acc_ref[...] = jnp.zeros_like(acc_ref)

            acc_ref[...] += lax.dot_general(a_ref[...].astype(BF16), b_ref[...].astype(BF16), dims, preferred_element_type=F32)

            @pl.when(kk == nk - 1)
            def _():
                o_ref[...] = finish(acc_ref[...])
        scratch = [pltpu.VMEM((tm, tn), F32)]

    return pl.pallas_call(
        body, name=name,
        grid=(m // tm, n // tn, nk),
        in_specs=[a_spec, b_spec],
        out_specs=pl.BlockSpec((tm, tn), lambda i, j, kk: (i, j)),
        out_shape=jax.ShapeDtypeStruct((m, n), out_dtype),
        scratch_shapes=scratch,
        compiler_params=_params(("parallel", "parallel", "arbitrary")),
    )(a, b)


def _rstd(v):
    return lax.rsqrt(jnp.mean(v * v, axis=-1, keepdims=True) + NORM_EPS)


def _rms_bwd(dy, v, g, r):
    gy = dy * g
    dv = r * (gy - v * (r * r) * jnp.mean(gy * v, axis=-1, keepdims=True))
    return dv, dy * v * r


def _row_spec(t, width):
    return pl.BlockSpec((t, width), lambda i: (i, 0))


def _gain_spec(width):
    return pl.BlockSpec((1, width), lambda i: (0, 0))


def _norm_res_fwd(x, mix, g_post, pre_gains, name):
    s, d = x.shape
    t = _pick(s, (ROW_TILE,))
    has_mix = mix is not None
    n_pre = len(pre_gains)

    def body(*refs):
        x_ref = refs[0]
        pos = 1
        x1 = x_ref[...]
        if has_mix:
            mv = refs[1][...].astype(F32)
            x1 = x1 + mv * _rstd(mv) * refs[2][...]
            pos = 3
        gains = refs[pos:pos + n_pre]
        outs = refs[pos + n_pre:]
        if has_mix:
            outs[0][...] = x1
            outs = outs[1:]
        r = _rstd(x1)
        for g_ref, o_ref in zip(gains, outs):
            o_ref[...] = (x1 * r * g_ref[...]).astype(BF16)

    ins = [x] + ([mix, g_post] if has_mix else []) + list(pre_gains)
    in_specs = [_row_spec(t, d)] + ([_row_spec(t, d), _gain_spec(d)] if has_mix else []) + [_gain_spec(d)] * n_pre
    out_shape = ([jax.ShapeDtypeStruct((s, d), F32)] if has_mix else []) + [jax.ShapeDtypeStruct((s, d), BF16)] * n_pre
    out_specs = [_row_spec(t, d)] * len(out_shape)
    res = pl.pallas_call(
        body, name=name, grid=(s // t,), in_specs=in_specs, out_specs=out_specs, out_shape=out_shape,
        compiler_params=_params(("parallel",)),
    )(*ins)
    if has_mix:
        return res[0], list(res[1:])
    return x, list(res)


def _norm_res_loss(x, mix, g_post, target, name):
    s, d = x.shape
    t = _pick(s, (ROW_TILE,))

    def body(x_ref, m_ref, g_ref, t_ref, dy_ref, loss_ref):
        mv = m_ref[...].astype(F32)
        y = x_ref[...] + mv * _rstd(mv) * g_ref[...]
        err = y - t_ref[...]
        dy_ref[...] = err * (1.0 / d)

        @pl.when(pl.program_id(0) == 0)
        def _():
            loss_ref[...] = jnp.zeros_like(loss_ref)

        loss_ref[...] += jnp.sum(err * err)

    dy, acc = pl.pallas_call(
        body, name=name, grid=(s // t,),
        in_specs=[_row_spec(t, d), _row_spec(t, d), _gain_spec(d), _row_spec(t, d)],
        out_specs=[_row_spec(t, d), pl.BlockSpec((8, LANE), lambda i: (0, 0))],
        out_shape=[jax.ShapeDtypeStruct((s, d), F32), jax.ShapeDtypeStruct((8, LANE), F32)],
        compiler_params=_params(("arbitrary",)),
    )(x, mix, g_post, target)
    return dy, acc[0, 0] * (0.5 / d)


def _norm_bwd(dx_out, branches, x_in, post, name):
    s, d = dx_out.shape
    t = _pick(s, (ROW_TILE,))
    nb = len(branches)
    has_post = post is not None

    def body(*refs):
        dx_ref = refs[0]
        pos = 1
        dx = dx_ref[...]
        first = pl.program_id(0) == 0
        n_in = 1 + (1 + 2 * nb if nb else 0) + (2 if has_post else 0)
        outs = refs[n_in:]
        opos = 0
        if nb:
            xv = refs[pos][...]
            pos += 1
            r = _rstd(xv)
            dx_o = outs[0]
            opos = 1
            for _ in range(nb):
                dxn = refs[pos][...].astype(F32)
                g = refs[pos + 1][...]
                pos += 2
                dv, dg_rows = _rms_bwd(dxn, xv, g, r)
                dx = dx + dv
                dg_ref = outs[opos]
                opos += 1

                @pl.when(first)
                def _(dg_ref=dg_ref):
                    dg_ref[...] = jnp.zeros_like(dg_ref)

                dg_ref[...] += jnp.sum(dg_rows, axis=0, keepdims=True)
            dx_o[...] = dx
        if has_post:
            mv = refs[pos][...].astype(F32)
            g = refs[pos + 1][...]
            dm, dg_rows = _rms_bwd(dx, mv, g, _rstd(mv))
            outs[opos][...] = dm.astype(BF16)
            dg_ref = outs[opos + 1]

            @pl.when(first)
            def _():
                dg_ref[...] = jnp.zeros_like(dg_ref)

            dg_ref[...] += jnp.sum(dg_rows, axis=0, keepdims=True)

    ins, in_specs = [dx_out], [_row_spec(t, d)]
    out_shape, out_specs = [], []
    if nb:
        ins.append(x_in)
        in_specs.append(_row_spec(t, d))
        out_shape.append(jax.ShapeDtypeStruct((s, d), F32))
        out_specs.append(_row_spec(t, d))
        for dxn, g in branches:
            ins += [dxn, g]
            in_specs += [_row_spec(t, d), _gain_spec(d)]
            out_shape.append(jax.ShapeDtypeStruct((1, d), F32))
            out_specs.append(_gain_spec(d))
    if has_post:
        ins += [post[0], post[1]]
        in_specs += [_row_spec(t, d), _gain_spec(d)]
        out_shape += [jax.ShapeDtypeStruct((s, d), BF16), jax.ShapeDtypeStruct((1, d), F32)]
        out_specs += [_row_spec(t, d), _gain_spec(d)]
    res = pl.pallas_call(
        body, name=name, grid=(s // t,), in_specs=in_specs, out_specs=out_specs, out_shape=out_shape,
        compiler_params=_params(("arbitrary",)),
    )(*ins)
    res = list(res)
    dx_in = res.pop(0) if nb else dx_out
    dgs = [res.pop(0) for _ in range(nb)]
    dm, dg_post = (res[0], res[1]) if has_post else (None, None)
    return dx_in, dgs, dm, dg_post


HALO = 16


def _shift_down(u, prev, k):
    rows = lax.broadcasted_iota(jnp.int32, u.shape, 0)
    out = pltpu.roll(u, k, 0)
    for i in range(k):
        out = jnp.where(rows == i, prev[HALO - k + i:HALO - k + i + 1, :], out)
    return out


def _shift_up(u, nxt, k):
    n = u.shape[0]
    rows = lax.broadcasted_iota(jnp.int32, u.shape, 0)
    out = pltpu.roll(u, n - k, 0)
    for i in range(k):
        out = jnp.where(rows == n - k + i, nxt[i:i + 1, :], out)
    return out


def _conv_gate_fwd(p, cw, name):
    s, d3 = p.shape
    d = d3 // 3
    t = _pick(s, (ROW_TILE,))
    hb = t // HALO

    def body(p_ref, prev_ref, w_ref, z_ref):
        i = pl.program_id(0)
        pv = p_ref[...].astype(F32)
        b, u = pv[:, :d], pv[:, d:2 * d] * pv[:, 2 * d:]
        ph = prev_ref[...].astype(F32)
        up = jnp.where(i > 0, ph[:, d:2 * d] * ph[:, 2 * d:], 0.0)
        w = w_ref[...]
        y = w[0:1, :] * _shift_down(u, up, 2) + w[1:2, :] * _shift_down(u, up, 1) + w[2:3, :] * u
        z_ref[...] = (b * y).astype(BF16)

    return pl.pallas_call(
        body, name=name, grid=(s // t,),
        in_specs=[_row_spec(t, d3),
                  pl.BlockSpec((HALO, d3), lambda i: (jnp.maximum(i * hb - 1, 0), 0)),
                  pl.BlockSpec((3, d), lambda i: (0, 0))],
        out_specs=_row_spec(t, d),
        out_shape=jax.ShapeDtypeStruct((s, d), BF16),
        compiler_params=_params(("parallel",)),
    )(p, p, cw)


def _conv_gate_bwd(p, dz, cw, name):
    s, d3 = p.shape
    d = d3 // 3
    t = _pick(s, (ROW_TILE,))
    hb = t // HALO
    nt = s // t
    last_halo = s // HALO - 1

    def body(p_ref, prev_ref, next_ref, dz_ref, dznext_ref, w_ref, dp_ref, dw_ref):
        i = pl.program_id(0)
        pv = p_ref[...].astype(F32)
        b, c, h = pv[:, :d], pv[:, d:2 * d], pv[:, 2 * d:]
        u = c * h
        ph = prev_ref[...].astype(F32)
        up = jnp.where(i > 0, ph[:, d:2 * d] * ph[:, 2 * d:], 0.0)
        w = w_ref[...]
        u1, u2 = _shift_down(u, up, 1), _shift_down(u, up, 2)
        y = w[0:1, :] * u2 + w[1:2, :] * u1 + w[2:3, :] * u
        dz = dz_ref[...].astype(F32)
        dy = dz * b
        dyn = jnp.where(i < nt - 1, dznext_ref[...].astype(F32) * next_ref[...].astype(F32)[:, :d], 0.0)
        du = w[2:3, :] * dy + w[1:2, :] * _shift_up(dy, dyn, 1) + w[0:1, :] * _shift_up(dy, dyn, 2)
        dp_ref[:, :d] = (dz * y).astype(BF16)
        dp_ref[:, d:2 * d] = (du * h).astype(BF16)
        dp_ref[:, 2 * d:] = (du * c).astype(BF16)

        @pl.when(i == 0)
        def _():
            dw_ref[...] = jnp.zeros_like(dw_ref)

        dw_ref[0:1, :] += jnp.sum(dy * u2, axis=0, keepdims=True)
        dw_ref[1:2, :] += jnp.sum(dy * u1, axis=0, keepdims=True)
        dw_ref[2:3, :] += jnp.sum(dy * u, axis=0, keepdims=True)

    return pl.pallas_call(
        body, name=name, grid=(nt,),
        in_specs=[_row_spec(t, d3),
                  pl.BlockSpec((HALO, d3), lambda i: (jnp.maximum(i * hb - 1, 0), 0)),
                  pl.BlockSpec((HALO, d3), lambda i: (jnp.minimum((i + 1) * hb, last_halo), 0)),
                  _row_spec(t, d),
                  pl.BlockSpec((HALO, d), lambda i: (jnp.minimum((i + 1) * hb, last_halo), 0)),
                  pl.BlockSpec((3, d), lambda i: (0, 0))],
        out_specs=[_row_spec(t, d3), pl.BlockSpec((3, d), lambda i: (0, 0))],
        out_shape=[jax.ShapeDtypeStruct((s, d3), BF16), jax.ShapeDtypeStruct((3, d), F32)],
        compiler_params=_params(("arbitrary",)),
    )(p, p, p, dz, dz, cw)


def _swiglu_fwd(f, name):
    s, f2 = f.shape
    ff = f2 // 2
    t = _pick(s, (ROW_TILE,))

    def body(f_ref, a_ref):
        gate = f_ref[:, :ff].astype(F32)
        up = f_ref[:, ff:].astype(F32)
        a_ref[...] = (gate * jax.nn.sigmoid(gate) * up).astype(BF16)

    return pl.pallas_call(
        body, name=name, grid=(s // t,),
        in_specs=[_row_spec(t, f2)], out_specs=_row_spec(t, ff),
        out_shape=jax.ShapeDtypeStruct((s, ff), BF16),
        compiler_params=_params(("parallel",)),
    )(f)


def _swiglu_bwd(f, da, name):
    s, f2 = f.shape
    ff = f2 // 2
    t = _pick(s, (ROW_TILE,))

    def body(f_ref, da_ref, df_ref):
        gate = f_ref[:, :ff].astype(F32)
        up = f_ref[:, ff:].astype(F32)
        dav = da_ref[...].astype(F32)
        sg = jax.nn.sigmoid(gate)
        silu = gate * sg
        df_ref[:, :ff] = (dav * up * (sg + silu * (1.0 - sg))).astype(BF16)
        df_ref[:, ff:] = (dav * silu).astype(BF16)

    return pl.pallas_call(
        body, name=name, grid=(s // t,),
        in_specs=[_row_spec(t, f2), _row_spec(t, ff)], out_specs=_row_spec(t, f2),
        out_shape=jax.ShapeDtypeStruct((s, f2), BF16),
        compiler_params=_params(("parallel",)),
    )(f, da)


SUPER = 2048
RES = 16
PAIR = 128
L = 128


def _alibi_slopes(n_heads):
    h = np.arange(n_heads, dtype=np.float32) + 1.0
    return np.power(2.0, -8.0 * h / n_heads).astype(np.float32)


def _permute16(x, inverse, name):
    s, d = x.shape
    cw = LANE

    def body(x_ref, o_ref):
        if inverse:
            for m in range(L):
                o_ref[RES * m:RES * (m + 1), :] = x_ref[pl.ds(m, RES, stride=L), :]
        else:
            for r in range(RES):
                o_ref[L * r:L * (r + 1), :] = x_ref[pl.ds(r, L, stride=RES), :]

    spec = pl.BlockSpec((SUPER, cw), lambda i, j: (i, j))
    return pl.pallas_call(
        body, name=name, grid=(s // SUPER, d // cw), in_specs=[spec], out_specs=spec,
        out_shape=jax.ShapeDtypeStruct((s, d), x.dtype),
        compiler_params=_params(("parallel", "parallel")),
    )(x)


def _slope_table(d):
    nh = d // HEAD_DIM
    sl = _alibi_slopes(nh)
    tab = np.repeat(sl, HEAD_DIM).reshape(d // PAIR, 1, PAIR)
    return jnp.asarray(np.broadcast_to(tab, (d // PAIR, 8, PAIR)).copy())


def _geometry(dil):
    nch = RES // dil
    return nch, L // nch


def _band(dil):
    nch, w = _geometry(dil)
    sh = w.bit_length() - 1
    i = lax.broadcasted_iota(jnp.int32, (L, 2 * L), 0)
    j = lax.broadcasted_iota(jnp.int32, (L, 2 * L), 1)

    def pos(t):
        return jnp.bitwise_and(t, w - 1) * nch + jnp.right_shift(t, sh)

    delta = pos(i) + L - (pos(jnp.bitwise_and(j, L - 1)) + jnp.bitwise_and(j, L))
    return (delta * dil).astype(F32), (delta >= 0) & (delta <= L), j < L


def _offsets(dil, res, n):
    nch, w = _geometry(dil)

    def al(v):
        return v if isinstance(v, int) else pl.multiple_of(v, w)

    q_off = [al((a * dil + res) * L + n * w) for a in range(nch)]
    k_off = [al((a * dil + res) * 2 * L + L + n * w) for a in range(nch)]
    kp_off = [al((a * dil + res) * 2 * L + L + n * w - w) for a in range(nch)]
    return q_off, k_off, kp_off, w


def _gather(ref, offs, w):
    parts = [ref[pl.ds(o, w), :] for o in offs]
    return parts[0] if len(parts) == 1 else jnp.concatenate(parts, axis=0)


def _scatter(ref, offs, w, val, add=False):
    for a, o in enumerate(offs):
        piece = val[a * w:(a + 1) * w, :]
        if add:
            ref[pl.ds(o, w), :] += piece
        else:
            ref[pl.ds(o, w), :] = piece


def _fill_key_buffer(buf, prev_ref, cur_ref):
    for r in range(RES):
        buf[2 * L * r:2 * L * r + L, :] = prev_ref[L * r:L * (r + 1), :]
        buf[2 * L * r + L:2 * L * (r + 1), :] = cur_ref[L * r:L * (r + 1), :]


def _two_heads(x, low):
    zero = jnp.zeros_like(x)
    return jnp.concatenate([jnp.where(low, x, zero), jnp.where(low, zero, x)], axis=0)


def _loop_blocks(dil, do_block):
    nch, _ = _geometry(dil)
    if dil == RES:
        def it(r, c):
            do_block(r, 0)
            return c
        lax.fori_loop(0, RES, it, 0, unroll=2)
    else:
        def it(n, c):
            for res in range(dil):
                do_block(res, n)
            return c
        lax.fori_loop(0, nch, it, 0, unroll=2 if dil == 1 else 1)


NT = (((1,), (1,)), ((), ()))
TN = (((0,), (0,)), ((), ()))


def _attention_fwd(q, kv, name):
    s, d = q.shape
    g_n, ns = d // PAIR, s // SUPER

    def body(sl_ref, q_ref, kc_ref, kp_ref, vc_ref, vp_ref, o_ref, lse_ref, kbuf, vbuf, m_s, l_s, acc_s):
        sb = pl.program_id(1)
        _fill_key_buffer(kbuf, kp_ref, kc_ref)
        _fill_key_buffer(vbuf, vp_ref, vc_ref)
        low = lax.broadcasted_iota(jnp.int32, (L, PAIR), 1) < HEAD_DIM
        low_k = lax.broadcasted_iota(jnp.int32, (2 * L, PAIR), 1) < HEAD_DIM
        nsl = (-sl_ref[0:1, 0:1], -sl_ref[0:1, HEAD_DIM:HEAD_DIM + 1])
        for bi, dil in enumerate(DILATIONS):
            first_branch, last_branch = bi == 0, bi == len(DILATIONS) - 1
            base, band, prev_half = _band(dil)

            def do_block(res, n, dil=dil, first_branch=first_branch, last_branch=last_branch,
                         base=base, band=band, prev_half=prev_half):
                q_off, k_off, kp_off, w = _offsets(dil, res, n)
                qf = _gather(q_ref, q_off, w).astype(BF16)
                kcat = jnp.concatenate([_gather(kbuf, kp_off, w), _gather(kbuf, k_off, w)], axis=0).astype(BF16)
                vcat = jnp.concatenate([_gather(vbuf, kp_off, w), _gather(vbuf, k_off, w)], axis=0).astype(BF16)
                k_bd, v_bd = _two_heads(kcat, low_k), _two_heads(vcat, low_k)
                sc = lax.dot_general(qf, k_bd, NT, preferred_element_type=F32)
                starts_sequence = jnp.logical_and(sb == 0, n == 0)
                valid = band & jnp.logical_not(prev_half & starts_sequence)
                if not first_branch:
                    m_prev = _gather(m_s, q_off, w)
                ps, m_new, l_blk = [], [], []
                for h in range(2):
                    s_h = jnp.where(valid, sc[:, 2 * L * h:2 * L * (h + 1)] + nsl[h] * base, NEG_BIG)
                    mh = jnp.max(s_h, axis=1, keepdims=True)
                    if not first_branch:
                        mh = jnp.maximum(mh, m_prev[:, HEAD_DIM * h:HEAD_DIM * h + 1])
                    p_h = jnp.exp(s_h - mh)
                    ps.append(p_h.astype(BF16))
                    m_new.append(mh)
                    l_blk.append(jnp.sum(p_h, axis=1, keepdims=True))
                m_full = jnp.where(low, m_new[0], m_new[1])
                l_full = jnp.where(low, l_blk[0], l_blk[1])
                acc = jnp.dot(jnp.concatenate(ps, axis=1), v_bd, preferred_element_type=F32)
                if not first_branch:
                    alpha = jnp.exp(m_prev - m_full)
                    l_full = _gather(l_s, q_off, w) * alpha + l_full
                    acc = _gather(acc_s, q_off, w) * alpha + acc
                if last_branch:
                    _scatter(o_ref, q_off, w, acc / l_full)
                    _scatter(lse_ref, q_off, w, m_full + jnp.log(l_full))
                else:
                    _scatter(m_s, q_off, w, m_full)
                    _scatter(l_s, q_off, w, l_full)
                    _scatter(acc_s, q_off, w, acc)

            _loop_blocks(dil, do_block)

    prev = lambda i: jnp.maximum(i - 1, 0)
    blk = pl.BlockSpec((SUPER, PAIR), lambda g, i: (i, g))
    in_specs = [pl.BlockSpec((None, 8, PAIR), lambda g, i: (g, 0, 0)), blk,
                pl.BlockSpec((SUPER, PAIR), lambda g, i: (i, g)),
                pl.BlockSpec((SUPER, PAIR), lambda g, i: (prev(i), g)),
                pl.BlockSpec((SUPER, PAIR), lambda g, i: (i, g_n + g)),
                pl.BlockSpec((SUPER, PAIR), lambda g, i: (prev(i), g_n + g))]
    return pl.pallas_call(
        body, name=name, grid=(g_n, ns), in_specs=in_specs, out_specs=[blk, blk],
        out_shape=[jax.ShapeDtypeStruct((s, d), F32)] * 2,
        scratch_shapes=[pltpu.VMEM((2 * SUPER, PAIR), F32)] * 2 + [pltpu.VMEM((SUPER, PAIR), F32)] * 3,
        compiler_params=_params(("parallel", "parallel")),
    )(_slope_table(d), q, kv, kv, kv, kv)


def _attention_bwd(q, kv, o, do, lse, dkv_in, name):
    s, d = q.shape
    g_n, ns = d // PAIR, s // SUPER
    has_in = dkv_in is not None

    def body(*refs):
        sl_ref, q_ref, do_ref, o_ref, lse_ref, kc_ref, kp_ref, vc_ref, vp_ref = refs[:9]
        pos = 9
        if has_in:
            dk_in_ref, dv_in_ref = refs[9:11]
            pos = 11
        dq_ref, dk_ref, dv_ref, kbuf, vbuf, dkbuf, dvbuf, dq_s = refs[pos:]
        step = pl.program_id(1)
        sb = ns - 1 - step
        _fill_key_buffer(kbuf, kp_ref, kc_ref)
        _fill_key_buffer(vbuf, vp_ref, vc_ref)

        @pl.when(step == 0)
        def _():
            dkbuf[...] = jnp.zeros_like(dkbuf)
            dvbuf[...] = jnp.zeros_like(dvbuf)

        @pl.when(step > 0)
        def _():
            for buf in (dkbuf, dvbuf):
                for r in range(RES):
                    buf[2 * L * r + L:2 * L * (r + 1), :] = buf[2 * L * r:2 * L * r + L, :]
                    buf[2 * L * r:2 * L * r + L, :] = jnp.zeros((L, PAIR), F32)

        low = lax.broadcasted_iota(jnp.int32, (L, PAIR), 1) < HEAD_DIM
        low_k = lax.broadcasted_iota(jnp.int32, (2 * L, PAIR), 1) < HEAD_DIM
        nsl = (-sl_ref[0:1, 0:1], -sl_ref[0:1, HEAD_DIM:HEAD_DIM + 1])
        for bi, dil in enumerate(DILATIONS):
            first_branch = bi == 0
            base, band, prev_half = _band(dil)

            def do_block(res, n, dil=dil, first_branch=first_branch, base=base, band=band, prev_half=prev_half):
                q_off, k_off, kp_off, w = _offsets(dil, res, n)
                qb = _gather(q_ref, q_off, w).astype(BF16)
                dof = _gather(do_ref, q_off, w)
                prod = dof * _gather(o_ref, q_off, w)
                dob = dof.astype(BF16)
                lse_f = _gather(lse_ref, q_off, w)
                zero = jnp.zeros_like(prod)
                dsum = (jnp.sum(jnp.where(low, prod, zero), axis=1, keepdims=True),
                        jnp.sum(jnp.where(low, zero, prod), axis=1, keepdims=True))
                kcat = jnp.concatenate([_gather(kbuf, kp_off, w), _gather(kbuf, k_off, w)], axis=0).astype(BF16)
                vcat = jnp.concatenate([_gather(vbuf, kp_off, w), _gather(vbuf, k_off, w)], axis=0).astype(BF16)
                k_bd, v_bd = _two_heads(kcat, low_k), _two_heads(vcat, low_k)
                sc = lax.dot_general(qb, k_bd, NT, preferred_element_type=F32)
                dp = lax.dot_general(dob, v_bd, NT, preferred_element_type=F32)
                starts_sequence = jnp.logical_and(sb == 0, n == 0)
                valid = band & jnp.logical_not(prev_half & starts_sequence)
                ps, dss = [], []
                for h in range(2):
                    cols = slice(2 * L * h, 2 * L * (h + 1))
                    lse_h = lse_f[:, HEAD_DIM * h:HEAD_DIM * h + 1]
                    p_h = jnp.exp(jnp.where(valid, sc[:, cols] + nsl[h] * base - lse_h, NEG_BIG))
                    dss.append((p_h * (dp[:, cols] - dsum[h])).astype(BF16))
                    ps.append(p_h.astype(BF16))
                ds_cat, p_cat = jnp.concatenate(dss, axis=1), jnp.concatenate(ps, axis=1)
                dq = jnp.dot(ds_cat, k_bd, preferred_element_type=F32)
                dk_bd = lax.dot_general(ds_cat, qb, TN, preferred_element_type=F32)
                dv_bd = lax.dot_general(p_cat, dob, TN, preferred_element_type=F32)
                dk = jnp.where(low_k, dk_bd[:2 * L], dk_bd[2 * L:])
                dv = jnp.where(low_k, dv_bd[:2 * L], dv_bd[2 * L:])
                _scatter(dq_s, q_off, w, dq, add=not first_branch)
                _scatter(dkbuf, kp_off, w, dk[:L], add=True)
                _scatter(dkbuf, k_off, w, dk[L:], add=True)
                _scatter(dvbuf, kp_off, w, dv[:L], add=True)
                _scatter(dvbuf, k_off, w, dv[L:], add=True)

            _loop_blocks(dil, do_block)

        dq_ref[...] = dq_s[...].astype(BF16)
        for r in range(RES):
            rows, cur = slice(L * r, L * (r + 1)), slice(2 * L * r + L, 2 * L * (r + 1))
            if has_in:
                dk_ref[rows, :] = dkbuf[cur, :] + dk_in_ref[rows, :]
                dv_ref[rows, :] = dvbuf[cur, :] + dv_in_ref[rows, :]
            else:
                dk_ref[rows, :] = dkbuf[cur, :]
                dv_ref[rows, :] = dvbuf[cur, :]

    rev = lambda i: ns - 1 - i
    prev = lambda i: jnp.maximum(ns - 2 - i, 0)
    blk = pl.BlockSpec((SUPER, PAIR), lambda g, i: (rev(i), g))
    in_specs = [pl.BlockSpec((None, 8, PAIR), lambda g, i: (g, 0, 0)), blk, blk, blk, blk,
                pl.BlockSpec((SUPER, PAIR), lambda g, i: (rev(i), g)),
                pl.BlockSpec((SUPER, PAIR), lambda g, i: (prev(i), g)),
                pl.BlockSpec((SUPER, PAIR), lambda g, i: (rev(i), g_n + g)),
                pl.BlockSpec((SUPER, PAIR), lambda g, i: (prev(i), g_n + g))]
    ins = [_slope_table(d), q, do, o, lse, kv, kv, kv, kv]
    if has_in:
        in_specs += [blk, blk]
        ins += list(dkv_in)
    res = pl.pallas_call(
        body, name=name, grid=(g_n, ns), in_specs=in_specs, out_specs=[blk, blk, blk],
        out_shape=[jax.ShapeDtypeStruct((s, d), BF16)] + [jax.ShapeDtypeStruct((s, d), F32)] * 2,
        scratch_shapes=[pltpu.VMEM((2 * SUPER, PAIR), F32)] * 4 + [pltpu.VMEM((SUPER, PAIR), F32)],
        compiler_params=_params(("parallel", "arbitrary")),
    )(*ins)
    return res[0], (res[1], res[2])


def _coords():
    return lax.axis_index("x"), lax.axis_index("y"), lax.axis_index("c")


def _chip_peers(x, y):
    return [(1 - x, y), (x, 1 - y), (1 - x, 1 - y)]


def _block_of(ref, axis, blk, size):
    start = pl.multiple_of(blk * size, size)
    if axis == 1:
        return ref.at[:, pl.ds(start, size)]
    return ref.at[pl.ds(start, size), :]


ANY = pl.BlockSpec(memory_space=pl.ANY)


HBM = pl.BlockSpec(memory_space=pltpu.HBM)
SEM = pl.BlockSpec(memory_space=pltpu.SEMAPHORE)
SPLIT = pltpu.CompilerParams(has_side_effects=pltpu.SideEffectType.DATAFLOW_SIDE_EFFECTING)


def _in_hbm(a):
    return pltpu.with_memory_space_constraint(a, pltpu.HBM)


def _thru(arrays):
    return [pltpu.HBM(a.shape, a.dtype) for a in arrays]


def _cast_place(w, ax, dtype, name):
    k, n = w.shape
    t = _pick(k, (512, 256, 128))
    nb = k // t

    def body(blk_ref, w_ref, b_ref, f_ref):
        v = w_ref[...].astype(dtype)
        b_ref[...] = v
        f_ref[...] = v

    full_shape = (k, 4 * n) if ax == 1 else (4 * k, n)
    place = (lambda i, blk: (i, blk[0])) if ax == 1 else (lambda i, blk: (blk[0] * nb + i, 0))
    spec = pl.BlockSpec((t, n), lambda i, blk: (i, 0))
    return pl.pallas_call(
        body, name=name,
        grid_spec=pltpu.PrefetchScalarGridSpec(
            num_scalar_prefetch=1, grid=(nb,), in_specs=[spec], out_specs=[spec, pl.BlockSpec((t, n), place)]),
        out_shape=[jax.ShapeDtypeStruct((k, n), dtype), jax.ShapeDtypeStruct(full_shape, dtype)],
        compiler_params=_params(("parallel",)),
    )(_my_block()[None], w)


def _my_block():
    return (2 * lax.axis_index("x") + lax.axis_index("y")).astype(jnp.int32)


def _gather_start(group, carry, name):
    n, nc = len(group), len(carry)

    def body(*refs):
        blocks, fulls, send_sem, recv_sem = refs[:n], refs[n:2 * n], refs[2 * n + nc], refs[2 * n + nc + 1]
        x, y, c = _coords()
        for t, (b, _, ax) in enumerate(group):
            mine = _block_of(fulls[t], ax, 2 * x + y, b.shape[ax])
            for j, (px, py) in enumerate(_chip_peers(x, y)):
                pltpu.make_async_remote_copy(
                    src_ref=blocks[t], dst_ref=mine, send_sem=send_sem.at[3 * t + j], recv_sem=recv_sem.at[3 * t + j],
                    device_id=(px, py, c), device_id_type=MESH).start()

    arrays = [b for b, _, _ in group] + [f for _, f, _ in group] + list(carry)
    sems = [pltpu.SemaphoreType.DMA((3 * n,))] * 2
    res = pl.pallas_call(
        body, name=name, in_specs=[HBM] * len(arrays), out_specs=[SEM, SEM] + [HBM] * len(arrays),
        out_shape=sems + _thru(arrays), input_output_aliases={i: 2 + i for i in range(len(arrays))},
        compiler_params=SPLIT,
    )(*[_in_hbm(a) for a in arrays])
    return (res[0], res[1], list(res[2:2 + n]), list(res[2 + n:2 + 2 * n])), list(res[2 + 2 * n:])


def _gather_wait(group, started, after, name):
    sends, recvs, blocks, fulls = started
    m = len(group)

    def body(*refs):
        blk_refs, full_refs, send_sem, recv_sem = refs[:m], refs[m:2 * m], refs[2 * m], refs[2 * m + 1]
        x, y, c = _coords()
        for t, (b, _, ax) in enumerate(group):
            for j, (px, py) in enumerate(_chip_peers(x, y)):
                cp = pltpu.make_async_remote_copy(
                    src_ref=blk_refs[t], dst_ref=_block_of(full_refs[t], ax, 2 * px + py, b.shape[ax]),
                    send_sem=send_sem.at[3 * t + j], recv_sem=recv_sem.at[3 * t + j],
                    device_id=(px, py, c), device_id_type=MESH)
                cp.wait_send()
                cp.wait_recv()

    extra = [] if after is None else [after]
    res = pl.pallas_call(
        body, name=name, in_specs=[HBM] * (2 * m) + [SEM, SEM] + [ANY] * len(extra), out_specs=[HBM] * (2 * m),
        out_shape=_thru(blocks) + _thru(fulls), input_output_aliases={i: i for i in range(2 * m)},
        compiler_params=SPLIT,
    )(*blocks, *fulls, sends, recvs, *extra)
    return list(res[m:])


def _scatter_start(grads, carry, name):
    n = len(grads)
    n_in = 2 * n + len(carry)

    def body(*refs):
        g_refs, st_refs, send_sem, recv_sem = refs[:n], refs[n:2 * n], refs[n_in], refs[n_in + 1]
        x, y, c = _coords()
        for t, (g, ax) in enumerate(grads):
            for j, (px, py) in enumerate(_chip_peers(x, y)):
                pltpu.make_async_remote_copy(
                    src_ref=_block_of(g_refs[t], ax, 2 * px + py, g.shape[ax] // 4), dst_ref=st_refs[t].at[j],
                    send_sem=send_sem.at[3 * t + j], recv_sem=recv_sem.at[3 * t + j],
                    device_id=(px, py, c), device_id_type=MESH).start()

    arrays = [g for g, _ in grads]
    for g, ax in grads:
        shape = list(g.shape)
        shape[ax] //= 4
        arrays.append(lax.empty((3, *shape), g.dtype))
    arrays += list(carry)
    sems = [pltpu.SemaphoreType.DMA((3 * n,))] * 2
    res = pl.pallas_call(
        body, name=name, in_specs=[HBM] * n_in, out_specs=[SEM, SEM] + [HBM] * n_in,
        out_shape=sems + _thru(arrays), input_output_aliases={i: 2 + i for i in range(n_in)},
        compiler_params=SPLIT,
    )(*[_in_hbm(a) for a in arrays])
    return (res[0], res[1], list(res[2:2 + n]), list(res[2 + n:2 + 2 * n])), list(res[2 + 2 * n:])


def _scatter_wait(axes, started, name):
    sends, recvs, full, stacks = started
    n = len(full)

    def body(*refs):
        g_refs, st_refs, send_sem, recv_sem = refs[:n], refs[n:2 * n], refs[2 * n], refs[2 * n + 1]
        x, y, c = _coords()
        for t, ax in enumerate(axes):
            size = full[t].shape[ax] // 4
            for j, (px, py) in enumerate(_chip_peers(x, y)):
                cp = pltpu.make_async_remote_copy(
                    src_ref=_block_of(g_refs[t], ax, 2 * px + py, size), dst_ref=st_refs[t].at[j],
                    send_sem=send_sem.at[3 * t + j], recv_sem=recv_sem.at[3 * t + j],
                    device_id=(px, py, c), device_id_type=MESH)
                cp.wait_send()
                cp.wait_recv()

    res = pl.pallas_call(
        body, name=name, in_specs=[HBM] * (2 * n) + [SEM, SEM], out_specs=[HBM] * (2 * n),
        out_shape=_thru(full) + _thru(stacks), input_output_aliases={i: i for i in range(2 * n)},
        compiler_params=SPLIT,
    )(*full, *stacks, sends, recvs)
    return list(res[:n]), list(res[n:])


def _pair_copies(g_refs, st_refs, out_refs, items, send_sem, recv_sem):
    x, y, c = _coords()
    copies = []
    for u, (g, ax, _) in enumerate(items):
        own = _block_of(g_refs[u], ax, 2 * x + y, g.shape[ax] // 4)
        for k, (src, dst) in enumerate([(own, out_refs[u].at[0]), (st_refs[u], out_refs[u].at[pl.ds(1, 3)])]):
            copies.append(pltpu.make_async_remote_copy(
                src_ref=src, dst_ref=dst, send_sem=send_sem.at[2 * u + k], recv_sem=recv_sem.at[2 * u + k],
                device_id=(x, y, 1 - c), device_id_type=MESH))
    return copies


def _pair_start(tensors, name):
    flat = [it for ts in tensors for it in ts]
    n, nt = len(flat), len(tensors)
    first = [sum(len(ts) for ts in tensors[:i]) for i in range(nt + 1)]

    def body(*refs):
        sends, recvs = refs[3 * n:3 * n + nt], refs[3 * n + nt:3 * n + 2 * nt]
        for i, ts in enumerate(tensors):
            rng = slice(first[i], first[i + 1])
            for cp in _pair_copies(refs[:n][rng], refs[n:2 * n][rng], refs[2 * n:3 * n][rng], ts, sends[i], recvs[i]):
                cp.start()

    arrays = ([g for g, _, _ in flat] + [st for _, _, st in flat]
              + [lax.empty((4, *st.shape[1:]), st.dtype) for _, _, st in flat])
    sems = [pltpu.SemaphoreType.DMA((2 * len(ts),)) for ts in tensors]
    res = pl.pallas_call(
        body, name=name, in_specs=[HBM] * (3 * n), out_specs=[SEM] * (2 * nt) + [HBM] * (3 * n),
        out_shape=sems + sems + _thru(arrays), input_output_aliases={i: 2 * nt + i for i in range(3 * n)},
        compiler_params=SPLIT,
    )(*[_in_hbm(a) for a in arrays])
    thru = res[2 * nt:]
    return [(res[i], res[nt + i], *(list(thru[k * n + first[i]:k * n + first[i + 1]]) for k in range(3)))
            for i in range(nt)]


def _pair_wait(axes, started, name):
    send, recv, full, stacks, landing = started
    n = len(full)
    items = [(full[u], axes[u], stacks[u]) for u in range(n)]

    def body(*refs):
        for cp in _pair_copies(refs[:n], refs[n:2 * n], refs[2 * n:3 * n], items, refs[3 * n], refs[3 * n + 1]):
            cp.wait_send()
            cp.wait_recv()

    res = pl.pallas_call(
        body, name=name, in_specs=[HBM] * (3 * n) + [SEM, SEM], out_specs=[HBM] * (3 * n),
        out_shape=_thru(full + stacks + landing), input_output_aliases={i: i for i in range(3 * n)},
        compiler_params=SPLIT,
    )(*full, *stacks, *landing, send, recv)
    return list(res[:n]), list(res[n:2 * n]), list(res[2 * n:])


def _allreduce_small(v, name):
    r, cdim = v.shape

    def body(v_ref, out_ref, buf, send_sems, recv_sems):
        x, y, c = _coords()
        me = 4 * x + 2 * y + c
        buf[0] = v_ref[...]
        sends = []
        for k in range(1, 8):
            peer = (x if not (k & 4) else 1 - x, y if not (k & 2) else 1 - y, c if not (k & 1) else 1 - c)
            cp = pltpu.make_async_remote_copy(
                src_ref=v_ref, dst_ref=buf.at[k], send_sem=send_sems.at[k - 1], recv_sem=recv_sems.at[k - 1],
                device_id=peer, device_id_type=MESH)
            cp.start()
            sends.append(cp)
        for cp in sends:
            cp.wait_recv()
        total = buf[me]
        for src in range(1, 8):
            total = total + buf[jnp.bitwise_xor(me, src)]
        out_ref[...] = total
        for cp in sends:
            cp.wait_send()

    return pl.pallas_call(
        body, name=name,
        in_specs=[pl.BlockSpec(memory_space=pltpu.VMEM)], out_specs=pl.BlockSpec(memory_space=pltpu.VMEM),
        out_shape=jax.ShapeDtypeStruct((r, cdim), F32),
        scratch_shapes=[pltpu.VMEM((8, r, cdim), F32), pltpu.SemaphoreType.DMA((7,)), pltpu.SemaphoreType.DMA((7,))],
        compiler_params=pltpu.CompilerParams(has_side_effects=True),
    )(v)


def _adamw_math(w, g, m, v):
    m = ADAM_B1 * m + (1.0 - ADAM_B1) * g
    v = ADAM_B2 * v + (1.0 - ADAM_B2) * jnp.square(g)
    m_hat = m / (1.0 - ADAM_B1 ** ADAM_STEP)
    v_hat = v / (1.0 - ADAM_B2 ** ADAM_STEP)
    delta = -ADAM_LR * (m_hat / (jnp.sqrt(v_hat) + ADAM_EPS) + ADAM_WD * w)
    return delta, m, v


def _adamw(w, m, v, grads, name):
    r, cdim = w.shape
    paired = isinstance(grads, list)
    layers = len(grads) if paired else 1
    t = _pick(r // layers, (64, 32, 16, 8))
    per_layer = r // layers // t
    n_grad = 3 * layers if paired else 1

    def body(*refs):
        refs = refs[1:] if paired else refs
        w_ref, m_ref, v_ref = refs[:3]
        outs = refs[3 + n_grad:]

        def update(g):
            delta, m_new, v_new = _adamw_math(w_ref[...], g, m_ref[...], v_ref[...])
            outs[0][...] = g
            outs[1][...] = delta
            outs[2][...] = m_new
            outs[3][...] = v_new

        if not paired:
            update(refs[3][...])
            return
        layer = pl.program_id(0) // per_layer
        for l in range(layers):
            @pl.when(layer == l)
            def _(own_ref=refs[3 + 3 * l], st_ref=refs[4 + 3 * l], sib_ref=refs[5 + 3 * l]):
                sa = own_ref[...].astype(F32)
                sb = sib_ref[0].astype(F32)
                for k in range(3):
                    sa = sa + st_ref[k].astype(F32)
                    sb = sb + sib_ref[k + 1].astype(F32)
                update(sa + sb)

    out_shape = [jax.ShapeDtypeStruct((r, cdim), F32)] * 4
    if not paired:
        spec = pl.BlockSpec((t, cdim), lambda i: (i, 0))
        return pl.pallas_call(
            body, name=name, grid=(r // t,), in_specs=[spec] * 4, out_specs=[spec] * 4, out_shape=out_shape,
            compiler_params=_params(("parallel",)),
        )(w, m, v, grads)

    spec = pl.BlockSpec((t, cdim), lambda i, blk: (i, 0))
    ins, in_specs = [w, m, v], [spec] * 3
    for l, (g, ax, stack, sib) in enumerate(grads):
        row = lambda i, l=l: jnp.clip(i - l * per_layer, 0, per_layer - 1)
        own = ((lambda i, blk, row=row: (row(i), blk[0])) if ax == 1
               else (lambda i, blk, row=row: (blk[0] * per_layer + row(i), 0)))
        ins += [g, stack, sib]
        in_specs += [pl.BlockSpec((t, cdim), own),
                     pl.BlockSpec((3, t, cdim), lambda i, blk, row=row: (0, row(i), 0)),
                     pl.BlockSpec((4, t, cdim), lambda i, blk, row=row: (0, row(i), 0))]
    return pl.pallas_call(
        body, name=name,
        grid_spec=pltpu.PrefetchScalarGridSpec(
            num_scalar_prefetch=1, grid=(r // t,), in_specs=in_specs, out_specs=[spec] * 4),
        out_shape=out_shape, compiler_params=_params(("parallel",)),
    )(_my_block()[None], *ins)


def _local_step(x, target, gains, conv_ws, kv_gain, weights_of, send_grads):
    depth = len(gains)
    n_a = len(conv_ws)
    saved, ws = [], []
    kv = kvn = None
    _, (xn,) = _norm_res_fwd(x, None, None, [gains[0][0]], "norm_first")
    h = x
    for l in range(depth):
        g = gains[l]
        sv = {"x_in": h, "xn": xn}
        w = weights_of(l, "mix", h)
        ws.append(w)
        if l == n_a:
            kv = _mm(kvn, w["kv"], "nn", F32, "kv_fwd")
        if l < n_a:
            p = _mm(xn, w["conv_in"], "nn", BF16, f"conv_in_fwd_{l}")
            z = _conv_gate_fwd(p, conv_ws[l], f"conv_gate_fwd_{l}")
            mix = _mm(z, w["conv_out"], "nn", BF16, f"conv_out_fwd_{l}")
            sv.update(p=p, z=z)
        else:
            j = l - n_a
            q = _mm(xn, w["q"], "nn", F32, f"q_fwd_{j}", scale=HEAD_DIM ** -0.5)
            o, lse = _attention_fwd(q, kv, f"attn_fwd_{j}")
            mix = _mm(o, w["o"], "nn", BF16, f"o_fwd_{j}")
            sv.update(q=q, o=o, lse=lse)
        x1, (xn2,) = _norm_res_fwd(h, mix, g[1], [g[2]], f"norm_mid_{l}")
        w.update(weights_of(l, "ffn", mix))
        f = _mm(xn2, w["ffn_in"], "nn", BF16, f"ffn_in_fwd_{l}")
        a = _swiglu_fwd(f, f"swiglu_fwd_{l}")
        ff = _mm(a, w["ffn_out"], "nn", BF16, f"ffn_out_fwd_{l}")
        sv.update(mix=mix, x1=x1, xn2=xn2, f=f, a=a, ff=ff)
        saved.append(sv)
        if l == depth - 1:
            dx, loss = _norm_res_loss(x1, ff, g[3], target, "norm_loss")
        else:
            if l == n_a - 1:
                h, _ = _norm_res_fwd(x1, ff, g[3], [], f"norm_end_{l}")
                h = _permute16(h, False, "permute_stream")
                target = _permute16(target, False, "permute_target")
                _, (xn, kvn) = _norm_res_fwd(h, None, None, [gains[l + 1][0], kv_gain], "norm_permuted")
            else:
                h, (xn,) = _norm_res_fwd(x1, ff, g[3], [gains[l + 1][0]], f"norm_end_{l}")
    d_gains = [[None] * 4 for _ in range(depth)]
    d_conv = [None] * n_a
    d_kv_gain = None
    dkv = None
    _, _, dff, d_gains[depth - 1][3] = _norm_bwd(dx, [], None, (saved[-1]["ff"], gains[-1][3]), "norm_loss_bwd")
    for l in reversed(range(depth)):
        sv, g, w, grads = saved[l], gains[l], ws[l], {}
        da = _mm(dff, w["ffn_out"], "nt", BF16, f"ffn_out_dx_{l}")
        grads["ffn_out"] =_mm(sv["a"], dff, "tn", BF16, f"ffn_out_dw_{l}")
        df = _swiglu_bwd(sv["f"], da, f"swiglu_bwd_{l}")
        dxn2 = _mm(df, w["ffn_in"], "nt", BF16, f"ffn_in_dx_{l}")
        grads["ffn_in"] =_mm(sv["xn2"], df, "tn", BF16, f"ffn_in_dw_{l}")
        dx, (d_gains[l][2],), dmix, d_gains[l][1] = _norm_bwd(
            dx, [(dxn2, g[2])], sv["x1"], (sv["mix"], g[1]), f"norm_mid_bwd_{l}")
        dx, dmix = send_grads(l, "ffn", grads, [dx, dmix])
        if l < n_a:
            dz = _mm(dmix, w["conv_out"], "nt", BF16, f"conv_out_dx_{l}")
            grads["conv_out"] =_mm(sv["z"], dmix, "tn", BF16, f"conv_out_dw_{l}")
            dp, d_conv[l] = _conv_gate_bwd(sv["p"], dz, conv_ws[l], f"conv_gate_bwd_{l}")
            dxn = _mm(dp, w["conv_in"], "nt", BF16, f"conv_in_dx_{l}")
            grads["conv_in"] =_mm(sv["xn"], dp, "tn", BF16, f"conv_in_dw_{l}")
        else:
            j = l - n_a
            do = _mm(dmix, w["o"], "nt", F32, f"o_dx_{j}")
            grads["o"] =_mm(sv["o"], dmix, "tn", BF16, f"o_dw_{j}")
            dq, dkv = _attention_bwd(sv["q"], kv, sv["o"], do, sv["lse"], dkv, f"attn_bwd_{j}")
            scale = HEAD_DIM ** -0.5
            dxn = _mm(dq, w["q"], "nt", BF16, f"q_dx_{j}", scale=scale)
            grads["q"] =_mm(sv["xn"], dq, "tn", BF16, f"q_dw_{j}", scale=scale)
        branches = [(dxn, g[0])]
        if l == n_a:
            dkv_cat = jnp.concatenate([dkv[0], dkv[1]], axis=1).astype(BF16)
            dkvn = _mm(dkv_cat, w["kv"], "nt", BF16, "kv_dx")
            grads["kv"] =_mm(kvn, dkv_cat, "tn", BF16, "kv_dw")
            branches.append((dkvn, kv_gain))
        post = (saved[l - 1]["ff"], gains[l - 1][3]) if l > 0 else None
        if l == n_a:
            dx, dgs, _, _ = _norm_bwd(dx, branches, sv["x_in"], None, f"norm_end_bwd_{l}")
            dx = _permute16(dx, True, "unpermute_stream")
            _, _, dff, dg_post = _norm_bwd(dx, [], None, post, "norm_boundary_bwd")
        else:
            dx, dgs, dff, dg_post = _norm_bwd(dx, branches, sv["x_in"], post, f"norm_end_bwd_{l}")
        if dff is None:
            (dx,) = send_grads(l, "mix", grads, [dx])
        else:
            dx, dff = send_grads(l, "mix", grads, [dx, dff])
        d_gains[l][0] = dgs[0]
        if l == n_a:
            d_kv_gain = dgs[1]
        if l > 0:
            d_gains[l - 1][3] = dg_post
    return loss, dx, d_gains, d_conv, d_kv_gain


BIG = (
    ("conv_in", 1), ("conv_out", 0), ("kv", 1), ("q", 0), ("o", 0), ("ffn_in", 1), ("ffn_out", 0))


def kernel(x, norm_g, conv_in_w, conv_w, conv_out_w, kv_norm_g, kv_w, q_w, o_w, ffn_in_w, ffn_out_w, loss_target, m_norm_g, m_conv_in_w, m_conv_w, m_conv_out_w, m_kv_norm_g, m_kv_w, m_q_w, m_o_w, m_ffn_in_w, m_ffn_out_w, v_norm_g, v_conv_in_w, v_conv_w, v_conv_out_w, v_kv_norm_g, v_kv_w, v_q_w, v_o_w, v_ffn_in_w, v_ffn_out_w):
    depth, _, dq = norm_g.shape
    d = 4 * dq
    n_a = conv_w.shape[0]
    big_w = {"conv_in": conv_in_w, "conv_out": conv_out_w, "kv": kv_w[None], "q": q_w, "o": o_w,
             "ffn_in": ffn_in_w, "ffn_out": ffn_out_w}
    big_m = {"conv_in": m_conv_in_w, "conv_out": m_conv_out_w, "kv": m_kv_w[None], "q": m_q_w, "o": m_o_w,
             "ffn_in": m_ffn_in_w, "ffn_out": m_ffn_out_w}
    big_v = {"conv_in": v_conv_in_w, "conv_out": v_conv_out_w, "kv": v_kv_w[None], "q": v_q_w, "o": v_o_w,
             "ffn_in": v_ffn_in_w, "ffn_out": v_ffn_out_w}

    n_gain, n_tap = depth * 4, n_a * conv_w.shape[1]
    small_rows = -(-(n_gain + n_tap + 1) // 8) * 8
    pad_rows = small_rows - n_gain - n_tap

    def pack_small(gains, taps):
        return jnp.concatenate([gains.reshape(n_gain, dq), taps.reshape(n_tap, dq), jnp.zeros((pad_rows, dq), F32)])

    axis_of = dict(BIG)

    def matrices_of(l, part):
        if part == "ffn":
            return [("ffn_in", l), ("ffn_out", l)]
        if l < n_a:
            return [("conv_in", l), ("conv_out", l)]
        return ([("kv", 0)] if l == n_a else []) + [("q", l - n_a), ("o", l - n_a)]

    halves = [(l, part) for l in range(depth) for part in ("mix", "ffn")]
    groups = {(l, part): [(*_cast_place(big_w[name][i], axis_of[name], BF16, f"place_{name}_{i}"), axis_of[name])
                          for name, i in matrices_of(l, part)] for l, part in halves}
    groups[halves[0]].append((*_cast_place(pack_small(norm_g, conv_w), 1, F32, "place_small"), 1))
    started = {halves[0]: _gather_start(groups[halves[0]], [], "gather_start_0_mix")[0]}

    def fetch(half, after):
        full = _gather_wait(groups[half], started[half], after, "gather_wait_%d_%s" % half)
        nxt = halves.index(half) + 1
        if nxt < len(halves):
            started[halves[nxt]], full = _gather_start(groups[halves[nxt]], full, "gather_start_%d_%s" % halves[nxt])
        return full

    first = fetch(halves[0], None)
    small = first[-1]
    gains = [[small[4 * l + i][None] for i in range(4)] for l in range(depth)]
    conv_ws = [small[n_gain + 3 * l:n_gain + 3 * l + 3] for l in range(n_a)]
    kv_gain = kv_norm_g[None]

    def weights_of(l, part, after):
        full = first if (l, part) == halves[0] else fetch((l, part), after)
        return {name: full[t] for t, (name, _) in enumerate(matrices_of(l, part))}

    sent = {}

    def send_grads(l, part, grads, carry):
        sent[l, part], carry = _scatter_start(
            [(grads[name], axis_of[name]) for name, _ in matrices_of(l, part)], carry, f"scatter_start_{l}_{part}")
        return carry

    loss, dx, d_gains, d_conv, d_kv_gain = _local_step(
        x[0], loss_target[0], gains, conv_ws, kv_gain, weights_of, send_grads)
    loss = lax.psum(loss, ("x", "y", "c"))

    small_g = jnp.concatenate([dg for row in d_gains for dg in row] + list(d_conv) + [d_kv_gain]
                              + [jnp.zeros((pad_rows - 1, d), F32)])
    small_g = _allreduce_small(small_g, "allreduce_small")
    blk = 2 * lax.axis_index("x") + lax.axis_index("y")
    mine_small = lax.dynamic_slice_in_dim(small_g, blk * dq, dq, axis=1)
    kv_rows = d // dq

    def pack_opt(gains_like, taps_like, kv_like):
        rows = jnp.concatenate([gains_like.reshape(n_gain, dq), taps_like.reshape(n_tap, dq), kv_like.reshape(kv_rows, dq)])
        extra = -rows.shape[0] % 8
        return jnp.concatenate([rows, jnp.zeros((extra, dq), F32)]) if extra else rows

    sw = pack_opt(norm_g, conv_w, kv_norm_g)
    sm = pack_opt(m_norm_g, m_conv_w, m_kv_norm_g)
    sv = pack_opt(v_norm_g, v_conv_w, v_kv_norm_g)
    sg = pack_opt(mine_small[:n_gain], mine_small[n_gain:n_gain + n_tap], small_g[n_gain + n_tap])
    s_out = _adamw(sw, sm, sv, sg, "adamw_small")

    def unpack(a):
        return (a[:n_gain].reshape(depth, 4, dq), a[n_gain:n_gain + n_tap].reshape(n_a, -1, dq),
                a[n_gain + n_tap:n_gain + n_tap + kv_rows].reshape(d))

    small_out = [unpack(a) for a in s_out]

    mine = {}
    for half in reversed(halves):
        axes = [axis_of[name] for name, _ in matrices_of(*half)]
        full, stacks = _scatter_wait(axes, sent[half], "scatter_wait_%d_%s" % half)
        mine.update(zip(matrices_of(*half), zip(full, stacks)))
    paired = _pair_start([[(*mine[name, i][:1], ax, mine[name, i][1]) for i in range(big_w[name].shape[0])]
                          for name, ax in BIG], "pair_start")
    big_out = {}
    for (name, ax), begun in zip(BIG, paired):
        shp = big_w[name].shape
        rows, cols = shp[0] * shp[1], shp[2]
        flat = lambda a: a.reshape(rows, cols)
        full, stacks, sibling = _pair_wait([ax] * shp[0], begun, f"pair_wait_{name}")
        res = _adamw(flat(big_w[name]), flat(big_m[name]), flat(big_v[name]),
                     [(full[i], ax, stacks[i], sibling[i]) for i in range(shp[0])], f"adamw_{name}")
        out_shp = shp[1:] if name == "kv" else shp
        big_out[name] = [a.reshape(out_shp) for a in res]

    def leaves(i):
        ng, cw_, kg = small_out[i]
        return [ng, big_out["conv_in"][i], cw_, big_out["conv_out"][i], kg, big_out["kv"][i], big_out["q"][i],
                big_out["o"][i], big_out["ffn_in"][i], big_out["ffn_out"][i]]

    return (loss, dx[None], *leaves(0), *leaves(1), *leaves(2), *leaves(3))
```

```python
import functools

import jax
import jax.numpy as jnp
import numpy as np
from jax import lax
from jax.experimental import pallas as pl
from jax.experimental.pallas import tpu as pltpu

F32 = jnp.float32
BF16 = jnp.bfloat16
HEAD_DIM = 64
DILATIONS = (1, 4, 16)
NORM_EPS = 1e-6
NEG_BIG = -1e30
VMEM_LIMIT = 48 * 1024 * 1024
ROW_TILE = 256
LANE = 128
MESH = pl.DeviceIdType.MESH

ADAM_LR = 0.001
ADAM_B1 = 0.9
ADAM_B2 = 0.999
ADAM_EPS = 1e-08
ADAM_WD = 0.01
ADAM_STEP = 10

TILE_CANDIDATES = (1024, 1408, 768, 512, 384, 256, 128)


def _pick(dim, cands=TILE_CANDIDATES):
    for c in cands:
        if c <= dim and dim % c == 0:
            return c
    return dim


def _params(sem):
    return pltpu.CompilerParams(dimension_semantics=sem, vmem_limit_bytes=VMEM_LIMIT)


def _mm(a, b, mode, out_dtype, name, scale=None):
    if mode == "nn":
        m, k = a.shape
        n = b.shape[1]
    elif mode == "nt":
        m, k = a.shape
        n = b.shape[0]
    else:
        k, m = a.shape
        n = b.shape[1]
    tm, tn, tk = _pick(m), _pick(n), _pick(k)
    nk = k // tk
    if mode == "nn":
        a_spec = pl.BlockSpec((tm, tk), lambda i, j, kk: (i, kk))
        b_spec = pl.BlockSpec((tk, tn), lambda i, j, kk: (kk, j))
        dims = (((1,), (0,)), ((), ()))
    elif mode == "nt":
        a_spec = pl.BlockSpec((tm, tk), lambda i, j, kk: (i, kk))
        b_spec = pl.BlockSpec((tn, tk), lambda i, j, kk: (j, kk))
        dims = (((1,), (1,)), ((), ()))
    else:
        a_spec = pl.BlockSpec((tk, tm), lambda i, j, kk: (kk, i))
        b_spec = pl.BlockSpec((tk, tn), lambda i, j, kk: (kk, j))
        dims = (((0,), (0,)), ((), ()))

    def finish(acc):
        if scale is not None:
            acc = acc * scale
        return acc.astype(out_dtype)

    if nk == 1:
        def body(a_ref, b_ref, o_ref):
            o_ref[...] = finish(lax.dot_general(a_ref[...].astype(BF16), b_ref[...].astype(BF16), dims, preferred_element_type=F32))
        scratch = []
    else:
        def body(a_ref, b_ref, o_ref, acc_ref):
            kk = pl.program_id(2)

            @pl.when(kk == 0)
            def _():
                acc_ref[...] = jnp.zeros_like(acc_ref)

            acc_ref[...] += lax.dot_general(a_ref[...].astype(BF16), b_ref[...].astype(BF16), dims, preferred_element_type=F32)

            @pl.when(kk == nk - 1)
            def _():
                o_ref[...] = finish(acc_ref[...])
        scratch = [pltpu.VMEM((tm, tn), F32)]

    return pl.pallas_call(
        body, name=name,
        grid=(m // tm, n // tn, nk),
        in_specs=[a_spec, b_spec],
        out_specs=pl.BlockSpec((tm, tn), lambda i, j, kk: (i, j)),
        out_shape=jax.ShapeDtypeStruct((m, n), out_dtype),
        scratch_shapes=scratch,
        compiler_params=_params(("parallel", "parallel", "arbitrary")),
    )(a, b)


def _rstd(v):
    return lax.rsqrt(jnp.mean(v * v, axis=-1, keepdims=True) + NORM_EPS)


def _rms_bwd(dy, v, g, r):
    gy = dy * g
    dv = r * (gy - v * (r * r) * jnp.mean(gy * v, axis=-1, keepdims=True))
    return dv, dy * v * r


def _row_spec(t, width):
    return pl.BlockSpec((t, width), lambda i: (i, 0))


def _gain_spec(width):
    return pl.BlockSpec((1, width), lambda i: (0, 0))


def _norm_res_fwd(x, mix, g_post, pre_gains, name):
    s, d = x.shape
    t = _pick(s, (ROW_TILE,))
    has_mix = mix is not None
    n_pre = len(pre_gains)

    def body(*refs):
        x_ref = refs[0]
        pos = 1
        x1 = x_ref[...]
        if has_mix:
            mv = refs[1][...].astype(F32)
            x1 = x1 + mv * _rstd(mv) * refs[2][...]
            pos = 3
        gains = refs[pos:pos + n_pre]
        outs = refs[pos + n_pre:]
        if has_mix:
            outs[0][...] = x1
            outs = outs[1:]
        r = _rstd(x1)
        for g_ref, o_ref in zip(gains, outs):
            o_ref[...] = (x1 * r * g_ref[...]).astype(BF16)

    ins = [x] + ([mix, g_post] if has_mix else []) + list(pre_gains)
    in_specs = [_row_spec(t, d)] + ([_row_spec(t, d), _gain_spec(d)] if has_mix else []) + [_gain_spec(d)] * n_pre
    out_shape = ([jax.ShapeDtypeStruct((s, d), F32)] if has_mix else []) + [jax.ShapeDtypeStruct((s, d), BF16)] * n_pre
    out_specs = [_row_spec(t, d)] * len(out_shape)
    res = pl.pallas_call(
        body, name=name, grid=(s // t,), in_specs=in_specs, out_specs=out_specs, out_shape=out_shape,
        compiler_params=_params(("parallel",)),
    )(*ins)
    if has_mix:
        return res[0], list(res[1:])
    return x, list(res)


def _norm_res_loss(x, mix, g_post, target, name):
    s, d = x.shape
    t = _pick(s, (ROW_TILE,))

    def body(x_ref, m_ref, g_ref, t_ref, dy_ref, loss_ref):
        mv = m_ref[...].astype(F32)
        y = x_ref[...] + mv * _rstd(mv) * g_ref[...]
        err = y - t_ref[...]
        dy_ref[...] = err * (1.0 / d)

        @pl.when(pl.program_id(0) == 0)
        def _():
            loss_ref[...] = jnp.zeros_like(loss_ref)

        loss_ref[...] += jnp.sum(err * err)

    dy, acc = pl.pallas_call(
        body, name=name, grid=(s // t,),
        in_specs=[_row_spec(t, d), _row_spec(t, d), _gain_spec(d), _row_spec(t, d)],
        out_specs=[_row_spec(t, d), pl.BlockSpec((8, LANE), lambda i: (0, 0))],
        out_shape=[jax.ShapeDtypeStruct((s, d), F32), jax.ShapeDtypeStruct((8, LANE), F32)],
        compiler_params=_params(("arbitrary",)),
    )(x, mix, g_post, target)
    return dy, acc[0, 0] * (0.5 / d)


def _norm_bwd(dx_out, branches, x_in, post, name):
    s, d = dx_out.shape
    t = _pick(s, (ROW_TILE,))
    nb = len(branches)
    has_post = post is not None

    def body(*refs):
        dx_ref = refs[0]
        pos = 1
        dx = dx_ref[...]
        first = pl.program_id(0) == 0
        n_in = 1 + (1 + 2 * nb if nb else 0) + (2 if has_post else 0)
        outs = refs[n_in:]
        opos = 0
        if nb:
            xv = refs[pos][...]
            pos += 1
            r = _rstd(xv)
            dx_o = outs[0]
            opos = 1
            for _ in range(nb):
                dxn = refs[pos][...].astype(F32)
                g = refs[pos + 1][...]
                pos += 2
                dv, dg_rows = _rms_bwd(dxn, xv, g, r)
                dx = dx + dv
                dg_ref = outs[opos]
                opos += 1

                @pl.when(first)
                def _(dg_ref=dg_ref):
                    dg_ref[...] = jnp.zeros_like(dg_ref)

                dg_ref[...] += jnp.sum(dg_rows, axis=0, keepdims=True)
            dx_o[...] = dx
        if has_post:
            mv = refs[pos][...].astype(F32)
            g = refs[pos + 1][...]
            dm, dg_rows = _rms_bwd(dx, mv, g, _rstd(mv))
            outs[opos][...] = dm.astype(BF16)
            dg_ref = outs[opos + 1]

            @pl.when(first)
            def _():
                dg_ref[...] = jnp.zeros_like(dg_ref)

            dg_ref[...] += jnp.sum(dg_rows, axis=0, keepdims=True)

    ins, in_specs = [dx_out], [_row_spec(t, d)]
    out_shape, out_specs = [], []
    if nb:
        ins.append(x_in)
        in_specs.append(_row_spec(t, d))
        out_shape.append(jax.ShapeDtypeStruct((s, d), F32))
        out_specs.append(_row_spec(t, d))
        for dxn, g in branches:
            ins += [dxn, g]
            in_specs += [_row_spec(t, d), _gain_spec(d)]
            out_shape.append(jax.ShapeDtypeStruct((1, d), F32))
            out_specs.append(_gain_spec(d))
    if has_post:
        ins += [post[0], post[1]]
        in_specs += [_row_spec(t, d), _gain_spec(d)]
        out_shape += [jax.ShapeDtypeStruct((s, d), BF16), jax.ShapeDtypeStruct((1, d), F32)]
        out_specs += [_row_spec(t, d), _gain_spec(d)]
    res = pl.pallas_call(
        body, name=name, grid=(s // t,), in_specs=in_specs, out_specs=out_specs, out_shape=out_shape,
        compiler_params=_params(("arbitrary",)),
    )(*ins)
    res = list(res)
    dx_in = res.pop(0) if nb else dx_out
    dgs = [res.pop(0) for _ in range(nb)]
    dm, dg_post = (res[0], res[1]) if has_post else (None, None)
    return dx_in, dgs, dm, dg_post


HALO = 16


def _shift_down(u, prev, k):
    rows = lax.broadcasted_iota(jnp.int32, u.shape, 0)
    out = pltpu.roll(u, k, 0)
    for i in range(k):
        out = jnp.where(rows == i, prev[HALO - k + i:HALO - k + i + 1, :], out)
    return out


def _shift_up(u, nxt, k):
    n = u.shape[0]
    rows = lax.broadcasted_iota(jnp.int32, u.shape, 0)
    out = pltpu.roll(u, n - k, 0)
    for i in range(k):
        out = jnp.where(rows == n - k + i, nxt[i:i + 1, :], out)
    return out


def _conv_gate_fwd(p, cw, name):
    s, d3 = p.shape
    d = d3 // 3
    t = _pick(s, (ROW_TILE,))
    hb = t // HALO

    def body(p_ref, prev_ref, w_ref, z_ref):
        i = pl.program_id(0)
        pv = p_ref[...].astype(F32)
        b, u = pv[:, :d], pv[:, d:2 * d] * pv[:, 2 * d:]
        ph = prev_ref[...].astype(F32)
        up = jnp.where(i > 0, ph[:, d:2 * d] * ph[:, 2 * d:], 0.0)
        w = w_ref[...]
        y = w[0:1, :] * _shift_down(u, up, 2) + w[1:2, :] * _shift_down(u, up, 1) + w[2:3, :] * u
        z_ref[...] = (b * y).astype(BF16)

    return pl.pallas_call(
        body, name=name, grid=(s // t,),
        in_specs=[_row_spec(t, d3),
                  pl.BlockSpec((HALO, d3), lambda i: (jnp.maximum(i * hb - 1, 0), 0)),
                  pl.BlockSpec((3, d), lambda i: (0, 0))],
        out_specs=_row_spec(t, d),
        out_shape=jax.ShapeDtypeStruct((s, d), BF16),
        compiler_params=_params(("parallel",)),
    )(p, p, cw)


def _conv_gate_bwd(p, dz, cw, name):
    s, d3 = p.shape
    d = d3 // 3
    t = _pick(s, (ROW_TILE,))
    hb = t // HALO
    nt = s // t
    last_halo = s // HALO - 1

    def body(p_ref, prev_ref, next_ref, dz_ref, dznext_ref, w_ref, dp_ref, dw_ref):
        i = pl.program_id(0)
        pv = p_ref[...].astype(F32)
        b, c, h = pv[:, :d], pv[:, d:2 * d], pv[:, 2 * d:]
        u = c * h
        ph = prev_ref[...].astype(F32)
        up = jnp.where(i > 0, ph[:, d:2 * d] * ph[:, 2 * d:], 0.0)
        w = w_ref[...]
        u1, u2 = _shift_down(u, up, 1), _shift_down(u, up, 2)
        y = w[0:1, :] * u2 + w[1:2, :] * u1 + w[2:3, :] * u
        dz = dz_ref[...].astype(F32)
        dy = dz * b
        dyn = jnp.where(i < nt - 1, dznext_ref[...].astype(F32) * next_ref[...].astype(F32)[:, :d], 0.0)
        du = w[2:3, :] * dy + w[1:2, :] * _shift_up(dy, dyn, 1) + w[0:1, :] * _shift_up(dy, dyn, 2)
        dp_ref[:, :d] = (dz * y).astype(BF16)
        dp_ref[:, d:2 * d] = (du * h).astype(BF16)
        dp_ref[:, 2 * d:] = (du * c).astype(BF16)

        @pl.when(i == 0)
        def _():
            dw_ref[...] = jnp.zeros_like(dw_ref)

        dw_ref[0:1, :] += jnp.sum(dy * u2, axis=0, keepdims=True)
        dw_ref[1:2, :] += jnp.sum(dy * u1, axis=0, keepdims=True)
        dw_ref[2:3, :] += jnp.sum(dy * u, axis=0, keepdims=True)

    return pl.pallas_call(
        body, name=name, grid=(nt,),
        in_specs=[_row_spec(t, d3),
                  pl.BlockSpec((HALO, d3), lambda i: (jnp.maximum(i * hb - 1, 0), 0)),
                  pl.BlockSpec((HALO, d3), lambda i: (jnp.minimum((i + 1) * hb, last_halo), 0)),
                  _row_spec(t, d),
                  pl.BlockSpec((HALO, d), lambda i: (jnp.minimum((i + 1) * hb, last_halo), 0)),
                  pl.BlockSpec((3, d), lambda i: (0, 0))],
        out_specs=[_row_spec(t, d3), pl.BlockSpec((3, d), lambda i: (0, 0))],
        out_shape=[jax.ShapeDtypeStruct((s, d3), BF16), jax.ShapeDtypeStruct((3, d), F32)],
        compiler_params=_params(("arbitrary",)),
    )(p, p, p, dz, dz, cw)


def _swiglu_fwd(f, name):
    s, f2 = f.shape
    ff = f2 // 2
    t = _pick(s, (ROW_TILE,))

    def body(f_ref, a_ref):
        gate = f_ref[:, :ff].astype(F32)
        up = f_ref[:, ff:].astype(F32)
        a_ref[...] = (gate * jax.nn.sigmoid(gate) * up).astype(BF16)

    return pl.pallas_call(
        body, name=name, grid=(s // t,),
        in_specs=[_row_spec(t, f2)], out_specs=_row_spec(t, ff),
        out_shape=jax.ShapeDtypeStruct((s, ff), BF16),
        compiler_params=_params(("parallel",)),
    )(f)


def _swiglu_bwd(f, da, name):
    s, f2 = f.shape
    ff = f2 // 2
    t = _pick(s, (ROW_TILE,))

    def body(f_ref, da_ref, df_ref):
        gate = f_ref[:, :ff].astype(F32)
        up = f_ref[:, ff:].astype(F32)
        dav = da_ref[...].astype(F32)
        sg = jax.nn.sigmoid(gate)
        silu = gate * sg
        df_ref[:, :ff] = (dav * up * (sg + silu * (1.0 - sg))).astype(BF16)
        df_ref[:, ff:] = (dav * silu).astype(BF16)

    return pl.pallas_call(
        body, name=name, grid=(s // t,),
        in_specs=[_row_spec(t, f2), _row_spec(t, ff)], out_specs=_row_spec(t, f2),
        out_shape=jax.ShapeDtypeStruct((s, f2), BF16),
        compiler_params=_params(("parallel",)),
    )(f, da)


SUPER = 2048
RES = 16
PAIR = 128
L = 128
FWD_TOGETHER = 4
BWD_TOGETHER = 4


def _alibi_slopes(n_heads):
    h = np.arange(n_heads, dtype=np.float32) + 1.0
    return np.power(2.0, -8.0 * h / n_heads).astype(np.float32)


def _permute16(x, inverse, name):
    s, d = x.shape
    cw = LANE

    def body(x_ref, o_ref):
        if inverse:
            for m in range(L):
                o_ref[RES * m:RES * (m + 1), :] = x_ref[pl.ds(m, RES, stride=L), :]
        else:
            for r in range(RES):
                o_ref[L * r:L * (r + 1), :] = x_ref[pl.ds(r, L, stride=RES), :]

    spec = pl.BlockSpec((SUPER, cw), lambda i, j: (i, j))
    return pl.pallas_call(
        body, name=name, grid=(s // SUPER, d // cw), in_specs=[spec], out_specs=spec,
        out_shape=jax.ShapeDtypeStruct((s, d), x.dtype),
        compiler_params=_params(("parallel", "parallel")),
    )(x)


def _slope_table(d):
    nh = d // HEAD_DIM
    sl = _alibi_slopes(nh)
    tab = np.repeat(sl, HEAD_DIM).reshape(d // PAIR, 1, PAIR)
    return jnp.asarray(np.broadcast_to(tab, (d // PAIR, 8, PAIR)).copy())


def _geometry(dil):
    nch = RES // dil
    return nch, L // nch


def _band(dil):
    nch, w = _geometry(dil)
    sh = w.bit_length() - 1
    i = lax.broadcasted_iota(jnp.int32, (L, 2 * L), 0)
    j = lax.broadcasted_iota(jnp.int32, (L, 2 * L), 1)

    def pos(t):
        return jnp.bitwise_and(t, w - 1) * nch + jnp.right_shift(t, sh)

    delta = pos(i) + L - (pos(jnp.bitwise_and(j, L - 1)) + jnp.bitwise_and(j, L))
    return (delta * dil).astype(F32), (delta >= 0) & (delta <= L), j < L


def _fill_bias(bias_s, sl_ref):
    for b, dil in enumerate(DILATIONS):
        base, band, prev_half = _band(dil)
        for first in range(2):
            valid = band & jnp.logical_not(prev_half) if first else band
            for h in range(2):
                slope = sl_ref[0:1, HEAD_DIM * h:HEAD_DIM * h + 1]
                bias_s[(2 * b + first) * 2 + h] = jnp.where(valid, -slope * base, NEG_BIG)


def _bias_index(b, sb, n):
    first = jnp.logical_and(sb == 0, n == 0).astype(jnp.int32)
    return (2 * b + first) * 2


def _offsets(dil, res, n):
    nch, w = _geometry(dil)

    def al(v):
        return v if isinstance(v, int) else pl.multiple_of(v, w)

    q_off = [al((a * dil + res) * L + n * w) for a in range(nch)]
    k_off = [al((a * dil + res) * 2 * L + L + n * w) for a in range(nch)]
    kp_off = [al((a * dil + res) * 2 * L + L + n * w - w) for a in range(nch)]
    return q_off, k_off, kp_off, w


def _gather(ref, offs, w):
    parts = [ref[pl.ds(o, w), :] for o in offs]
    return parts[0] if len(parts) == 1 else jnp.concatenate(parts, axis=0)


def _scatter(ref, offs, w, val, add=False):
    for a, o in enumerate(offs):
        piece = val[a * w:(a + 1) * w, :]
        if add:
            ref[pl.ds(o, w), :] += piece
        else:
            ref[pl.ds(o, w), :] = piece


def _fill_key_buffer(buf, prev_ref, cur_ref):
    for r in range(RES):
        buf[2 * L * r:2 * L * r + L, :] = prev_ref[L * r:L * (r + 1), :]
        buf[2 * L * r + L:2 * L * (r + 1), :] = cur_ref[L * r:L * (r + 1), :]


def _two_heads(x, low):
    zero = jnp.zeros_like(x)
    return jnp.concatenate([jnp.where(low, x, zero), jnp.where(low, zero, x)], axis=0)


def _loop_blocks(dil, stages, together):
    together = max(together, dil) if dil < RES else together

    def it(i, c):
        if dil == RES:
            blocks = [(i * together + k, 0) for k in range(together)]
        else:
            blocks = [(res, i * (together // dil) + k) for k in range(together // dil) for res in range(dil)]
        state = [stages[0](res, n) for res, n in blocks]
        for stage in stages[1:]:
            state = [stage(res, n, prev) for (res, n), prev in zip(blocks, state)]
        for writes in state:
            for args in writes:
                _scatter(*args)
        return c

    lax.fori_loop(0, RES // together, it, 0)


NT = (((1,), (1,)), ((), ()))
TN = (((0,), (0,)), ((), ()))


def _attention_fwd(q, kv, name):
    s, d = q.shape
    g_n, ns = d // PAIR, s // SUPER

    def body(sl_ref, q_ref, kc_ref, kp_ref, vc_ref, vp_ref, o_ref, lse_ref, kbuf, vbuf, m_s, l_s, acc_s, bias_s):
        sb = pl.program_id(1)
        _fill_key_buffer(kbuf, kp_ref, kc_ref)
        _fill_key_buffer(vbuf, vp_ref, vc_ref)
        _fill_bias(bias_s, sl_ref)
        low = lax.broadcasted_iota(jnp.int32, (L, PAIR), 1) < HEAD_DIM
        low_k = lax.broadcasted_iota(jnp.int32, (2 * L, PAIR), 1) < HEAD_DIM
        ones_bd = _two_heads(jnp.ones((2 * L, PAIR), BF16), low_k)
        for bi, dil in enumerate(DILATIONS):
            first_branch, last_branch = bi == 0, bi == len(DILATIONS) - 1

            def scores(res, n, dil=dil):
                q_off, k_off, kp_off, w = _offsets(dil, res, n)
                qf = _gather(q_ref, q_off, w).astype(BF16)
                kcat = jnp.concatenate([_gather(kbuf, kp_off, w), _gather(kbuf, k_off, w)], axis=0).astype(BF16)
                return lax.dot_general(qf, _two_heads(kcat, low_k), NT, preferred_element_type=F32)

            def update(res, n, sc, bi=bi, dil=dil, first_branch=first_branch, last_branch=last_branch):
                q_off, k_off, kp_off, w = _offsets(dil, res, n)
                vcat = jnp.concatenate([_gather(vbuf, kp_off, w), _gather(vbuf, k_off, w)], axis=0).astype(BF16)
                v_ones = jnp.concatenate([_two_heads(vcat, low_k), ones_bd], axis=1)
                bias_at = _bias_index(bi, sb, n)
                if not first_branch:
                    m_prev = _gather(m_s, q_off, w)
                ps, m_new = [], []
                for h in range(2):
                    s_h = sc[:, 2 * L * h:2 * L * (h + 1)] + bias_s[bias_at + h]
                    mh = jnp.max(s_h, axis=1, keepdims=True)
                    if not first_branch:
                        mh = jnp.maximum(mh, m_prev[:, HEAD_DIM * h:HEAD_DIM * h + 1])
                    ps.append(jnp.exp(s_h - mh).astype(BF16))
                    m_new.append(mh)
                m_full = jnp.where(low, m_new[0], m_new[1])
                both = jnp.dot(jnp.concatenate(ps, axis=1), v_ones, preferred_element_type=F32)
                acc, l_full = both[:, :PAIR], both[:, PAIR:]
                if not first_branch:
                    alpha = jnp.exp(m_prev - m_full)
                    l_full = _gather(l_s, q_off, w) * alpha + l_full
                    acc = _gather(acc_s, q_off, w) * alpha + acc
                if last_branch:
                    return [(o_ref, q_off, w, acc / l_full, False), (lse_ref, q_off, w, m_full + jnp.log(l_full), False)]
                return [(m_s, q_off, w, m_full, False), (l_s, q_off, w, l_full, False), (acc_s, q_off, w, acc, False)]

            _loop_blocks(dil, [scores, update], FWD_TOGETHER)

    prev = lambda i: jnp.maximum(i - 1, 0)
    blk = pl.BlockSpec((SUPER, PAIR), lambda g, i: (i, g))
    in_specs = [pl.BlockSpec((None, 8, PAIR), lambda g, i: (g, 0, 0)), blk,
                pl.BlockSpec((SUPER, PAIR), lambda g, i: (i, g)),
                pl.BlockSpec((SUPER, PAIR), lambda g, i: (prev(i), g)),
                pl.BlockSpec((SUPER, PAIR), lambda g, i: (i, g_n + g)),
                pl.BlockSpec((SUPER, PAIR), lambda g, i: (prev(i), g_n + g))]
    return pl.pallas_call(
        body, name=name, grid=(g_n, ns), in_specs=in_specs, out_specs=[blk, blk],
        out_shape=[jax.ShapeDtypeStruct((s, d), F32)] * 2,
        scratch_shapes=([pltpu.VMEM((2 * SUPER, PAIR), F32)] * 2 + [pltpu.VMEM((SUPER, PAIR), F32)] * 3
                        + [pltpu.VMEM((4 * len(DILATIONS), L, 2 * L), F32)]),
        compiler_params=_params(("parallel", "parallel")),
    )(_slope_table(d), q, kv, kv, kv, kv)


def _attention_bwd(q, kv, o, do, lse, dkv_in, name):
    s, d = q.shape
    g_n, ns = d // PAIR, s // SUPER
    has_in = dkv_in is not None

    def body(*refs):
        sl_ref, q_ref, do_ref, o_ref, lse_ref, kc_ref, kp_ref, vc_ref, vp_ref = refs[:9]
        pos = 9
        if has_in:
            dk_in_ref, dv_in_ref = refs[9:11]
            pos = 11
        dq_ref, dk_ref, dv_ref, kbuf, vbuf, dkbuf, dvbuf, dq_s, bias_s = refs[pos:]
        step = pl.program_id(1)
        sb = ns - 1 - step
        _fill_key_buffer(kbuf, kp_ref, kc_ref)
        _fill_key_buffer(vbuf, vp_ref, vc_ref)
        _fill_bias(bias_s, sl_ref)

        @pl.when(step == 0)
        def _():
            dkbuf[...] = jnp.zeros_like(dkbuf)
            dvbuf[...] = jnp.zeros_like(dvbuf)

        @pl.when(step > 0)
        def _():
            for buf in (dkbuf, dvbuf):
                for r in range(RES):
                    buf[2 * L * r + L:2 * L * (r + 1), :] = buf[2 * L * r:2 * L * r + L, :]
                    buf[2 * L * r:2 * L * r + L, :] = jnp.zeros((L, PAIR), F32)

        low = lax.broadcasted_iota(jnp.int32, (L, PAIR), 1) < HEAD_DIM
        low_k = lax.broadcasted_iota(jnp.int32, (2 * L, PAIR), 1) < HEAD_DIM
        for bi, dil in enumerate(DILATIONS):
            first_branch = bi == 0

            def scores(res, n, dil=dil):
                q_off, k_off, kp_off, w = _offsets(dil, res, n)
                qb = _gather(q_ref, q_off, w).astype(BF16)
                dof = _gather(do_ref, q_off, w)
                prod = dof * _gather(o_ref, q_off, w)
                dob = dof.astype(BF16)
                lse_f = _gather(lse_ref, q_off, w)
                zero = jnp.zeros_like(prod)
                dsum = (jnp.sum(jnp.where(low, prod, zero), axis=1, keepdims=True),
                        jnp.sum(jnp.where(low, zero, prod), axis=1, keepdims=True))
                kcat = jnp.concatenate([_gather(kbuf, kp_off, w), _gather(kbuf, k_off, w)], axis=0).astype(BF16)
                vcat = jnp.concatenate([_gather(vbuf, kp_off, w), _gather(vbuf, k_off, w)], axis=0).astype(BF16)
                k_bd, v_bd = _two_heads(kcat, low_k), _two_heads(vcat, low_k)
                sc = lax.dot_general(qb, k_bd, NT, preferred_element_type=F32)
                dp = lax.dot_general(dob, v_bd, NT, preferred_element_type=F32)
                return qb, dob, lse_f, dsum, k_bd, sc, dp

            def gradients(res, n, given, bi=bi, dil=dil, first_branch=first_branch):
                qb, dob, lse_f, dsum, k_bd, sc, dp = given
                q_off, k_off, kp_off, w = _offsets(dil, res, n)
                bias_at = _bias_index(bi, sb, n)
                ps, dss = [], []
                for h in range(2):
                    cols = slice(2 * L * h, 2 * L * (h + 1))
                    lse_h = lse_f[:, HEAD_DIM * h:HEAD_DIM * h + 1]
                    p_h = jnp.exp(sc[:, cols] + bias_s[bias_at + h] - lse_h)
                    dss.append((p_h * (dp[:, cols] - dsum[h])).astype(BF16))
                    ps.append(p_h.astype(BF16))
                ds_cat, p_cat = jnp.concatenate(dss, axis=1), jnp.concatenate(ps, axis=1)
                dq = jnp.dot(ds_cat, k_bd, preferred_element_type=F32)
                dk_bd = lax.dot_general(ds_cat, qb, TN, preferred_element_type=F32)
                dv_bd = lax.dot_general(p_cat, dob, TN, preferred_element_type=F32)
                dk = jnp.where(low_k, dk_bd[:2 * L], dk_bd[2 * L:])
                dv = jnp.where(low_k, dv_bd[:2 * L], dv_bd[2 * L:])
                return [(dq_s, q_off, w, dq, not first_branch),
                        (dkbuf, kp_off, w, dk[:L], True), (dkbuf, k_off, w, dk[L:], True),
                        (dvbuf, kp_off, w, dv[:L], True), (dvbuf, k_off, w, dv[L:], True)]

            _loop_blocks(dil, [scores, gradients], BWD_TOGETHER)

        dq_ref[...] = dq_s[...].astype(BF16)
        for r in range(RES):
            rows, cur = slice(L * r, L * (r + 1)), slice(2 * L * r + L, 2 * L * (r + 1))
            if has_in:
                dk_ref[rows, :] = dkbuf[cur, :] + dk_in_ref[rows, :]
                dv_ref[rows, :] = dvbuf[cur, :] + dv_in_ref[rows, :]
            else:
                dk_ref[rows, :] = dkbuf[cur, :]
                dv_ref[rows, :] = dvbuf[cur, :]

    rev = lambda i: ns - 1 - i
    prev = lambda i: jnp.maximum(ns - 2 - i, 0)
    blk = pl.BlockSpec((SUPER, PAIR), lambda g, i: (rev(i), g))
    in_specs = [pl.BlockSpec((None, 8, PAIR), lambda g, i: (g, 0, 0)), blk, blk, blk, blk,
                pl.BlockSpec((SUPER, PAIR), lambda g, i: (rev(i), g)),
                pl.BlockSpec((SUPER, PAIR), lambda g, i: (prev(i), g)),
                pl.BlockSpec((SUPER, PAIR), lambda g, i: (rev(i), g_n + g)),
                pl.BlockSpec((SUPER, PAIR), lambda g, i: (prev(i), g_n + g))]
    ins = [_slope_table(d), q, do, o, lse, kv, kv, kv, kv]
    if has_in:
        in_specs += [blk, blk]
        ins += list(dkv_in)
    res = pl.pallas_call(
        body, name=name, grid=(g_n, ns), in_specs=in_specs, out_specs=[blk, blk, blk],
        out_shape=[jax.ShapeDtypeStruct((s, d), BF16)] + [jax.ShapeDtypeStruct((s, d), F32)] * 2,
        scratch_shapes=([pltpu.VMEM((2 * SUPER, PAIR), F32)] * 4 + [pltpu.VMEM((SUPER, PAIR), F32)]
                        + [pltpu.VMEM((4 * len(DILATIONS), L, 2 * L), F32)]),
        compiler_params=_params(("parallel", "arbitrary")),
    )(*ins)
    return res[0], (res[1], res[2])


def _coords():
    return lax.axis_index("x"), lax.axis_index("y"), lax.axis_index("c")


def _chip_peers(x, y):
    return [(1 - x, y), (x, 1 - y), (1 - x, 1 - y)]


def _block_of(ref, axis, blk, size):
    start = pl.multiple_of(blk * size, size)
    if axis == 1:
        return ref.at[:, pl.ds(start, size)]
    return ref.at[pl.ds(start, size), :]


ANY = pl.BlockSpec(memory_space=pl.ANY)


HBM = pl.BlockSpec(memory_space=pltpu.HBM)
SEM = pl.BlockSpec(memory_space=pltpu.SEMAPHORE)
SPLIT = pltpu.CompilerParams(has_side_effects=pltpu.SideEffectType.DATAFLOW_SIDE_EFFECTING)


def _in_hbm(a):
    return pltpu.with_memory_space_constraint(a, pltpu.HBM)


def _thru(arrays):
    return [pltpu.HBM(a.shape, a.dtype) for a in arrays]


def _cast_place(w, ax, dtype, name):
    k, n = w.shape
    t = _pick(k, (512, 256, 128))
    nb = k // t

    def body(blk_ref, w_ref, b_ref, f_ref):
        v = w_ref[...].astype(dtype)
        b_ref[...] = v
        f_ref[...] = v

    full_shape = (k, 4 * n) if ax == 1 else (4 * k, n)
    place = (lambda i, blk: (i, blk[0])) if ax == 1 else (lambda i, blk: (blk[0] * nb + i, 0))
    spec = pl.BlockSpec((t, n), lambda i, blk: (i, 0))
    return pl.pallas_call(
        body, name=name,
        grid_spec=pltpu.PrefetchScalarGridSpec(
            num_scalar_prefetch=1, grid=(nb,), in_specs=[spec], out_specs=[spec, pl.BlockSpec((t, n), place)]),
        out_shape=[jax.ShapeDtypeStruct((k, n), dtype), jax.ShapeDtypeStruct(full_shape, dtype)],
        compiler_params=_params(("parallel",)),
    )(_my_block()[None], w)


def _my_block():
    return (2 * lax.axis_index("x") + lax.axis_index("y")).astype(jnp.int32)


def _gather_start(group, carry, name):
    n, nc = len(group), len(carry)

    def body(*refs):
        blocks, fulls, send_sem, recv_sem = refs[:n], refs[n:2 * n], refs[2 * n + nc], refs[2 * n + nc + 1]
        x, y, c = _coords()
        for t, (b, _, ax) in enumerate(group):
            mine = _block_of(fulls[t], ax, 2 * x + y, b.shape[ax])
            for j, (px, py) in enumerate(_chip_peers(x, y)):
                pltpu.make_async_remote_copy(
                    src_ref=blocks[t], dst_ref=mine, send_sem=send_sem.at[3 * t + j], recv_sem=recv_sem.at[3 * t + j],
                    device_id=(px, py, c), device_id_type=MESH).start()

    arrays = [b for b, _, _ in group] + [f for _, f, _ in group] + list(carry)
    sems = [pltpu.SemaphoreType.DMA((3 * n,))] * 2
    res = pl.pallas_call(
        body, name=name, in_specs=[HBM] * len(arrays), out_specs=[SEM, SEM] + [HBM] * len(arrays),
        out_shape=sems + _thru(arrays), input_output_aliases={i: 2 + i for i in range(len(arrays))},
        compiler_params=SPLIT,
    )(*[_in_hbm(a) for a in arrays])
    return (res[0], res[1], list(res[2:2 + n]), list(res[2 + n:2 + 2 * n])), list(res[2 + 2 * n:])


def _gather_wait(group, started, after, name):
    sends, recvs, blocks, fulls = started
    m = len(group)

    def body(*refs):
        blk_refs, full_refs, send_sem, recv_sem = refs[:m], refs[m:2 * m], refs[2 * m], refs[2 * m + 1]
        x, y, c = _coords()
        for t, (b, _, ax) in enumerate(group):
            for j, (px, py) in enumerate(_chip_peers(x, y)):
                cp = pltpu.make_async_remote_copy(
                    src_ref=blk_refs[t], dst_ref=_block_of(full_refs[t], ax, 2 * px + py, b.shape[ax]),
                    send_sem=send_sem.at[3 * t + j], recv_sem=recv_sem.at[3 * t + j],
                    device_id=(px, py, c), device_id_type=MESH)
                cp.wait_send()
                cp.wait_recv()

    extra = [] if after is None else [after]
    res = pl.pallas_call(
        body, name=name, in_specs=[HBM] * (2 * m) + [SEM, SEM] + [ANY] * len(extra), out_specs=[HBM] * (2 * m),
        out_shape=_thru(blocks) + _thru(fulls), input_output_aliases={i: i for i in range(2 * m)},
        compiler_params=SPLIT,
    )(*blocks, *fulls, sends, recvs, *extra)
    return list(res[m:])


def _scatter_start(grads, carry, name):
    n = len(grads)
    n_in = 2 * n + len(carry)

    def body(*refs):
        g_refs, st_refs, send_sem, recv_sem = refs[:n], refs[n:2 * n], refs[n_in], refs[n_in + 1]
        x, y, c = _coords()
        for t, (g, ax) in enumerate(grads):
            for j, (px, py) in enumerate(_chip_peers(x, y)):
                pltpu.make_async_remote_copy(
                    src_ref=_block_of(g_refs[t], ax, 2 * px + py, g.shape[ax] // 4), dst_ref=st_refs[t].at[j],
                    send_sem=send_sem.at[3 * t + j], recv_sem=recv_sem.at[3 * t + j],
                    device_id=(px, py, c), device_id_type=MESH).start()

    arrays = [g for g, _ in grads]
    for g, ax in grads:
        shape = list(g.shape)
        shape[ax] //= 4
        arrays.append(lax.empty((3, *shape), g.dtype))
    arrays += list(carry)
    sems = [pltpu.SemaphoreType.DMA((3 * n,))] * 2
    res = pl.pallas_call(
        body, name=name, in_specs=[HBM] * n_in, out_specs=[SEM, SEM] + [HBM] * n_in,
        out_shape=sems + _thru(arrays), input_output_aliases={i: 2 + i for i in range(n_in)},
        compiler_params=SPLIT,
    )(*[_in_hbm(a) for a in arrays])
    return (res[0], res[1], list(res[2:2 + n]), list(res[2 + n:2 + 2 * n])), list(res[2 + 2 * n:])


def _scatter_wait(axes, started, name):
    sends, recvs, full, stacks = started
    n = len(full)

    def body(*refs):
        g_refs, st_refs, send_sem, recv_sem = refs[:n], refs[n:2 * n], refs[2 * n], refs[2 * n + 1]
        x, y, c = _coords()
        for t, ax in enumerate(axes):
            size = full[t].shape[ax] // 4
            for j, (px, py) in enumerate(_chip_peers(x, y)):
                cp = pltpu.make_async_remote_copy(
                    src_ref=_block_of(g_refs[t], ax, 2 * px + py, size), dst_ref=st_refs[t].at[j],
                    send_sem=send_sem.at[3 * t + j], recv_sem=recv_sem.at[3 * t + j],
                    device_id=(px, py, c), device_id_type=MESH)
                cp.wait_send()
                cp.wait_recv()

    res = pl.pallas_call(
        body, name=name, in_specs=[HBM] * (2 * n) + [SEM, SEM], out_specs=[HBM] * (2 * n),
        out_shape=_thru(full) + _thru(stacks), input_output_aliases={i: i for i in range(2 * n)},
        compiler_params=SPLIT,
    )(*full, *stacks, sends, recvs)
    return list(res[:n]), list(res[n:])


def _pair_copies(g_refs, st_refs, out_refs, items, send_sem, recv_sem):
    x, y, c = _coords()
    copies = []
    for u, (g, ax, _) in enumerate(items):
        own = _block_of(g_refs[u], ax, 2 * x + y, g.shape[ax] // 4)
        for k, (src, dst) in enumerate([(own, out_refs[u].at[0]), (st_refs[u], out_refs[u].at[pl.ds(1, 3)])]):
            copies.append(pltpu.make_async_remote_copy(
                src_ref=src, dst_ref=dst, send_sem=send_sem.at[2 * u + k], recv_sem=recv_sem.at[2 * u + k],
                device_id=(x, y, 1 - c), device_id_type=MESH))
    return copies


def _pair_start(tensors, name):
    flat = [it for ts in tensors for it in ts]
    n, nt = len(flat), len(tensors)
    first = [sum(len(ts) for ts in tensors[:i]) for i in range(nt + 1)]

    def body(*refs):
        sends, recvs = refs[3 * n:3 * n + nt], refs[3 * n + nt:3 * n + 2 * nt]
        for i, ts in enumerate(tensors):
            rng = slice(first[i], first[i + 1])
            for cp in _pair_copies(refs[:n][rng], refs[n:2 * n][rng], refs[2 * n:3 * n][rng], ts, sends[i], recvs[i]):
                cp.start()

    arrays = ([g for g, _, _ in flat] + [st for _, _, st in flat]
              + [lax.empty((4, *st.shape[1:]), st.dtype) for _, _, st in flat])
    sems = [pltpu.SemaphoreType.DMA((2 * len(ts),)) for ts in tensors]
    res = pl.pallas_call(
        body, name=name, in_specs=[HBM] * (3 * n), out_specs=[SEM] * (2 * nt) + [HBM] * (3 * n),
        out_shape=sems + sems + _thru(arrays), input_output_aliases={i: 2 * nt + i for i in range(3 * n)},
        compiler_params=SPLIT,
    )(*[_in_hbm(a) for a in arrays])
    thru = res[2 * nt:]
    return [(res[i], res[nt + i], *(list(thru[k * n + first[i]:k * n + first[i + 1]]) for k in range(3)))
            for i in range(nt)]


def _pair_wait(axes, started, name):
    send, recv, full, stacks, landing = started
    n = len(full)
    items = [(full[u], axes[u], stacks[u]) for u in range(n)]

    def body(*refs):
        for cp in _pair_copies(refs[:n], refs[n:2 * n], refs[2 * n:3 * n], items, refs[3 * n], refs[3 * n + 1]):
            cp.wait_send()
            cp.wait_recv()

    res = pl.pallas_call(
        body, name=name, in_specs=[HBM] * (3 * n) + [SEM, SEM], out_specs=[HBM] * (3 * n),
        out_shape=_thru(full + stacks + landing), input_output_aliases={i: i for i in range(3 * n)},
        compiler_params=SPLIT,
    )(*full, *stacks, *landing, send, recv)
    return list(res[:n]), list(res[n:2 * n]), list(res[2 * n:])


def _allreduce_small(v, name):
    r, cdim = v.shape

    def body(v_ref, out_ref, buf, send_sems, recv_sems):
        x, y, c = _coords()
        me = 4 * x + 2 * y + c
        buf[0] = v_ref[...]
        sends = []
        for k in range(1, 8):
            peer = (x if not (k & 4) else 1 - x, y if not (k & 2) else 1 - y, c if not (k & 1) else 1 - c)
            cp = pltpu.make_async_remote_copy(
                src_ref=v_ref, dst_ref=buf.at[k], send_sem=send_sems.at[k - 1], recv_sem=recv_sems.at[k - 1],
                device_id=peer, device_id_type=MESH)
            cp.start()
            sends.append(cp)
        for cp in sends:
            cp.wait_recv()
        total = buf[me]
        for src in range(1, 8):
            total = total + buf[jnp.bitwise_xor(me, src)]
        out_ref[...] = total
        for cp in sends:
            cp.wait_send()

    return pl.pallas_call(
        body, name=name,
        in_specs=[pl.BlockSpec(memory_space=pltpu.VMEM)], out_specs=pl.BlockSpec(memory_space=pltpu.VMEM),
        out_shape=jax.ShapeDtypeStruct((r, cdim), F32),
        scratch_shapes=[pltpu.VMEM((8, r, cdim), F32), pltpu.SemaphoreType.DMA((7,)), pltpu.SemaphoreType.DMA((7,))],
        compiler_params=pltpu.CompilerParams(has_side_effects=True),
    )(v)


def _adamw_math(w, g, m, v):
    m = ADAM_B1 * m + (1.0 - ADAM_B1) * g
    v = ADAM_B2 * v + (1.0 - ADAM_B2) * jnp.square(g)
    m_hat = m / (1.0 - ADAM_B1 ** ADAM_STEP)
    v_hat = v / (1.0 - ADAM_B2 ** ADAM_STEP)
    delta = -ADAM_LR * (m_hat / (jnp.sqrt(v_hat) + ADAM_EPS) + ADAM_WD * w)
    return delta, m, v


def _adamw(w, m, v, grads, name):
    r, cdim = w.shape
    paired = isinstance(grads, list)
    layers = len(grads) if paired else 1
    t = _pick(r // layers, (64, 32, 16, 8))
    per_layer = r // layers // t
    n_grad = 3 * layers if paired else 1

    def body(*refs):
        refs = refs[1:] if paired else refs
        w_ref, m_ref, v_ref = refs[:3]
        outs = refs[3 + n_grad:]

        def update(g):
            delta, m_new, v_new = _adamw_math(w_ref[...], g, m_ref[...], v_ref[...])
            outs[0][...] = g
            outs[1][...] = delta
            outs[2][...] = m_new
            outs[3][...] = v_new

        if not paired:
            update(refs[3][...])
            return
        layer = pl.program_id(0) // per_layer
        for l in range(layers):
            @pl.when(layer == l)
            def _(own_ref=refs[3 + 3 * l], st_ref=refs[4 + 3 * l], sib_ref=refs[5 + 3 * l]):
                sa = own_ref[...].astype(F32)
                sb = sib_ref[0].astype(F32)
                for k in range(3):
                    sa = sa + st_ref[k].astype(F32)
                    sb = sb + sib_ref[k + 1].astype(F32)
                update(sa + sb)

    out_shape = [jax.ShapeDtypeStruct((r, cdim), F32)] * 4
    if not paired:
        spec = pl.BlockSpec((t, cdim), lambda i: (i, 0))
        return pl.pallas_call(
            body, name=name, grid=(r // t,), in_specs=[spec] * 4, out_specs=[spec] * 4, out_shape=out_shape,
            compiler_params=_params(("parallel",)),
        )(w, m, v, grads)

    spec = pl.BlockSpec((t, cdim), lambda i, blk: (i, 0))
    ins, in_specs = [w, m, v], [spec] * 3
    for l, (g, ax, stack, sib) in enumerate(grads):
        row = lambda i, l=l: jnp.clip(i - l * per_layer, 0, per_layer - 1)
        own = ((lambda i, blk, row=row: (row(i), blk[0])) if ax == 1
               else (lambda i, blk, row=row: (blk[0] * per_layer + row(i), 0)))
        ins += [g, stack, sib]
        in_specs += [pl.BlockSpec((t, cdim), own),
                     pl.BlockSpec((3, t, cdim), lambda i, blk, row=row: (0, row(i), 0)),
                     pl.BlockSpec((4, t, cdim), lambda i, blk, row=row: (0, row(i), 0))]
    return pl.pallas_call(
        body, name=name,
        grid_spec=pltpu.PrefetchScalarGridSpec(
            num_scalar_prefetch=1, grid=(r // t,), in_specs=in_specs, out_specs=[spec] * 4),
        out_shape=out_shape, compiler_params=_params(("parallel",)),
    )(_my_block()[None], *ins)


def _local_step(x, target, gains, conv_ws, kv_gain, weights_of, send_grads):
    depth = len(gains)
    n_a = len(conv_ws)
    saved, ws = [], []
    kv = kvn = None
    _, (xn,) = _norm_res_fwd(x, None, None, [gains[0][0]], "norm_first")
    h = x
    for l in range(depth):
        g = gains[l]
        sv = {"x_in": h, "xn": xn}
        w = weights_of(l, "mix", h)
        ws.append(w)
        if l == n_a:
            kv = _mm(kvn, w["kv"], "nn", F32, "kv_fwd")
        if l < n_a:
            p = _mm(xn, w["conv_in"], "nn", BF16, f"conv_in_fwd_{l}")
            z = _conv_gate_fwd(p, conv_ws[l], f"conv_gate_fwd_{l}")
            mix = _mm(z, w["conv_out"], "nn", BF16, f"conv_out_fwd_{l}")
            sv.update(p=p, z=z)
        else:
            j = l - n_a
            q = _mm(xn, w["q"], "nn", F32, f"q_fwd_{j}", scale=HEAD_DIM ** -0.5)
            o, lse = _attention_fwd(q, kv, f"attn_fwd_{j}")
            mix = _mm(o, w["o"], "nn", BF16, f"o_fwd_{j}")
            sv.update(q=q, o=o, lse=lse)
        x1, (xn2,) = _norm_res_fwd(h, mix, g[1], [g[2]], f"norm_mid_{l}")
        w.update(weights_of(l, "ffn", mix))
        f = _mm(xn2, w["ffn_in"], "nn", BF16, f"ffn_in_fwd_{l}")
        a = _swiglu_fwd(f, f"swiglu_fwd_{l}")
        ff = _mm(a, w["ffn_out"], "nn", BF16, f"ffn_out_fwd_{l}")
        sv.update(mix=mix, x1=x1, xn2=xn2, f=f, a=a, ff=ff)
        saved.append(sv)
        if l == depth - 1:
            dx, loss = _norm_res_loss(x1, ff, g[3], target, "norm_loss")
        else:
            if l == n_a - 1:
                h, _ = _norm_res_fwd(x1, ff, g[3], [], f"norm_end_{l}")
                h = _permute16(h, False, "permute_stream")
                target = _permute16(target, False, "permute_target")
                _, (xn, kvn) = _norm_res_fwd(h, None, None, [gains[l + 1][0], kv_gain], "norm_permuted")
            else:
                h, (xn,) = _norm_res_fwd(x1, ff, g[3], [gains[l + 1][0]], f"norm_end_{l}")
    d_gains = [[None] * 4 for _ in range(depth)]
    d_conv = [None] * n_a
    d_kv_gain = None
    dkv = None
    _, _, dff, d_gains[depth - 1][3] = _norm_bwd(dx, [], None, (saved[-1]["ff"], gains[-1][3]), "norm_loss_bwd")
    for l in reversed(range(depth)):
        sv, g, w, grads = saved[l], gains[l], ws[l], {}
        da = _mm(dff, w["ffn_out"], "nt", BF16, f"ffn_out_dx_{l}")
        grads["ffn_out"] =_mm(sv["a"], dff, "tn", BF16, f"ffn_out_dw_{l}")
        df = _swiglu_bwd(sv["f"], da, f"swiglu_bwd_{l}")
        dxn2 = _mm(df, w["ffn_in"], "nt", BF16, f"ffn_in_dx_{l}")
        grads["ffn_in"] =_mm(sv["xn2"], df, "tn", BF16, f"ffn_in_dw_{l}")
        dx, (d_gains[l][2],), dmix, d_gains[l][1] = _norm_bwd(
            dx, [(dxn2, g[2])], sv["x1"], (sv["mix"], g[1]), f"norm_mid_bwd_{l}")
        dx, dmix = send_grads(l, "ffn", grads, [dx, dmix])
        if l < n_a:
            dz = _mm(dmix, w["conv_out"], "nt", BF16, f"conv_out_dx_{l}")
            grads["conv_out"] =_mm(sv["z"], dmix, "tn", BF16, f"conv_out_dw_{l}")
            dp, d_conv[l] = _conv_gate_bwd(sv["p"], dz, conv_ws[l], f"conv_gate_bwd_{l}")
            dxn = _mm(dp, w["conv_in"], "nt", BF16, f"conv_in_dx_{l}")
            grads["conv_in"] =_mm(sv["xn"], dp, "tn", BF16, f"conv_in_dw_{l}")
        else:
            j = l - n_a
            do = _mm(dmix, w["o"], "nt", F32, f"o_dx_{j}")
            grads["o"] =_mm(sv["o"], dmix, "tn", BF16, f"o_dw_{j}")
            dq, dkv = _attention_bwd(sv["q"], kv, sv["o"], do, sv["lse"], dkv, f"attn_bwd_{j}")
            scale = HEAD_DIM ** -0.5
            dxn = _mm(dq, w["q"], "nt", BF16, f"q_dx_{j}", scale=scale)
            grads["q"] =_mm(sv["xn"], dq, "tn", BF16, f"q_dw_{j}", scale=scale)
        branches = [(dxn, g[0])]
        if l == n_a:
            dkv_cat = jnp.concatenate([dkv[0], dkv[1]], axis=1).astype(BF16)
            dkvn = _mm(dkv_cat, w["kv"], "nt", BF16, "kv_dx")
            grads["kv"] =_mm(kvn, dkv_cat, "tn", BF16, "kv_dw")
            branches.append((dkvn, kv_gain))
        post = (saved[l - 1]["ff"], gains[l - 1][3]) if l > 0 else None
        if l == n_a:
            dx, dgs, _, _ = _norm_bwd(dx, branches, sv["x_in"], None, f"norm_end_bwd_{l}")
            dx = _permute16(dx, True, "unpermute_stream")
            _, _, dff, dg_post = _norm_bwd(dx, [], None, post, "norm_boundary_bwd")
        else:
            dx, dgs, dff, dg_post = _norm_bwd(dx, branches, sv["x_in"], post, f"norm_end_bwd_{l}")
        if dff is None:
            (dx,) = send_grads(l, "mix", grads, [dx])
        else:
            dx, dff = send_grads(l, "mix", grads, [dx, dff])
        d_gains[l][0] = dgs[0]
        if l == n_a:
            d_kv_gain = dgs[1]
        if l > 0:
            d_gains[l - 1][3] = dg_post
    return loss, dx, d_gains, d_conv, d_kv_gain


BIG = (
    ("conv_in", 1), ("conv_out", 0), ("kv", 1), ("q", 0), ("o", 0), ("ffn_in", 1), ("ffn_out", 0))


def kernel(x, norm_g, conv_in_w, conv_w, conv_out_w, kv_norm_g, kv_w, q_w, o_w, ffn_in_w, ffn_out_w, loss_target, m_norm_g, m_conv_in_w, m_conv_w, m_conv_out_w, m_kv_norm_g, m_kv_w, m_q_w, m_o_w, m_ffn_in_w, m_ffn_out_w, v_norm_g, v_conv_in_w, v_conv_w, v_conv_out_w, v_kv_norm_g, v_kv_w, v_q_w, v_o_w, v_ffn_in_w, v_ffn_out_w):
    depth, _, dq = norm_g.shape
    d = 4 * dq
    n_a = conv_w.shape[0]
    big_w = {"conv_in": conv_in_w, "conv_out": conv_out_w, "kv": kv_w[None], "q": q_w, "o": o_w,
             "ffn_in": ffn_in_w, "ffn_out": ffn_out_w}
    big_m = {"conv_in": m_conv_in_w, "conv_out": m_conv_out_w, "kv": m_kv_w[None], "q": m_q_w, "o": m_o_w,
             "ffn_in": m_ffn_in_w, "ffn_out": m_ffn_out_w}
    big_v = {"conv_in": v_conv_in_w, "conv_out": v_conv_out_w, "kv": v_kv_w[None], "q": v_q_w, "o": v_o_w,
             "ffn_in": v_ffn_in_w, "ffn_out": v_ffn_out_w}

    n_gain, n_tap = depth * 4, n_a * conv_w.shape[1]
    small_rows = -(-(n_gain + n_tap + 1) // 8) * 8
    pad_rows = small_rows - n_gain - n_tap

    def pack_small(gains, taps):
        return jnp.concatenate([gains.reshape(n_gain, dq), taps.reshape(n_tap, dq), jnp.zeros((pad_rows, dq), F32)])

    axis_of = dict(BIG)

    def matrices_of(l, part):
        if part == "ffn":
            return [("ffn_in", l), ("ffn_out", l)]
        if l < n_a:
            return [("conv_in", l), ("conv_out", l)]
        return ([("kv", 0)] if l == n_a else []) + [("q", l - n_a), ("o", l - n_a)]

    halves = [(l, part) for l in range(depth) for part in ("mix", "ffn")]
    groups = {(l, part): [(*_cast_place(big_w[name][i], axis_of[name], BF16, f"place_{name}_{i}"), axis_of[name])
                          for name, i in matrices_of(l, part)] for l, part in halves}
    groups[halves[0]].append((*_cast_place(pack_small(norm_g, conv_w), 1, F32, "place_small"), 1))
    started = {halves[0]: _gather_start(groups[halves[0]], [], "gather_start_0_mix")[0]}

    def fetch(half, after):
        full = _gather_wait(groups[half], started[half], after, "gather_wait_%d_%s" % half)
        nxt = halves.index(half) + 1
        if nxt < len(halves):
            started[halves[nxt]], full = _gather_start(groups[halves[nxt]], full, "gather_start_%d_%s" % halves[nxt])
        return full

    first = fetch(halves[0], None)
    small = first[-1]
    gains = [[small[4 * l + i][None] for i in range(4)] for l in range(depth)]
    conv_ws = [small[n_gain + 3 * l:n_gain + 3 * l + 3] for l in range(n_a)]
    kv_gain = kv_norm_g[None]

    def weights_of(l, part, after):
        full = first if (l, part) == halves[0] else fetch((l, part), after)
        return {name: full[t] for t, (name, _) in enumerate(matrices_of(l, part))}

    sent = {}

    def send_grads(l, part, grads, carry):
        sent[l, part], carry = _scatter_start(
            [(grads[name], axis_of[name]) for name, _ in matrices_of(l, part)], carry, f"scatter_start_{l}_{part}")
        return carry

    loss, dx, d_gains, d_conv, d_kv_gain = _local_step(
        x[0], loss_target[0], gains, conv_ws, kv_gain, weights_of, send_grads)
    loss = lax.psum(loss, ("x", "y", "c"))

    small_g = jnp.concatenate([dg for row in d_gains for dg in row] + list(d_conv) + [d_kv_gain]
                              + [jnp.zeros((pad_rows - 1, d), F32)])
    small_g = _allreduce_small(small_g, "allreduce_small")
    blk = 2 * lax.axis_index("x") + lax.axis_index("y")
    mine_small = lax.dynamic_slice_in_dim(small_g, blk * dq, dq, axis=1)
    kv_rows = d // dq

    def pack_opt(gains_like, taps_like, kv_like):
        rows = jnp.concatenate([gains_like.reshape(n_gain, dq), taps_like.reshape(n_tap, dq), kv_like.reshape(kv_rows, dq)])
        extra = -rows.shape[0] % 8
        return jnp.concatenate([rows, jnp.zeros((extra, dq), F32)]) if extra else rows

    sw = pack_opt(norm_g, conv_w, kv_norm_g)
    sm = pack_opt(m_norm_g, m_conv_w, m_kv_norm_g)
    sv = pack_opt(v_norm_g, v_conv_w, v_kv_norm_g)
    sg = pack_opt(mine_small[:n_gain], mine_small[n_gain:n_gain + n_tap], small_g[n_gain + n_tap])
    s_out = _adamw(sw, sm, sv, sg, "adamw_small")

    def unpack(a):
        return (a[:n_gain].reshape(depth, 4, dq), a[n_gain:n_gain + n_tap].reshape(n_a, -1, dq),
                a[n_gain + n_tap:n_gain + n_tap + kv_rows].reshape(d))

    small_out = [unpack(a) for a in s_out]

    mine = {}
    for half in reversed(halves):
        axes = [axis_of[name] for name, _ in matrices_of(*half)]
        full, stacks = _scatter_wait(axes, sent[half], "scatter_wait_%d_%s" % half)
        mine.update(zip(matrices_of(*half), zip(full, stacks)))
    paired = _pair_start([[(*mine[name, i][:1], ax, mine[name, i][1]) for i in range(big_w[name].shape[0])]
                          for name, ax in BIG], "pair_start")
    big_out = {}
    for (name, ax), begun in zip(BIG, paired):
        shp = big_w[name].shape
        rows, cols = shp[0] * shp[1], shp[2]
        flat = lambda a: a.reshape(rows, cols)
        full, stacks, sibling = _pair_wait([ax] * shp[0], begun, f"pair_wait_{name}")
        res = _adamw(flat(big_w[name]), flat(big_m[name]), flat(big_v[name]),
                     [(full[i], ax, stacks[i], sibling[i]) for i in range(shp[0])], f"adamw_{name}")
        out_shp = shp[1:] if name == "kv" else shp
        big_out[name] = [a.reshape(out_shp) for a in res]

    def leaves(i):
        ng, cw_, kg = small_out[i]
        return [ng, big_out["conv_in"][i], cw_, big_out["conv_out"][i], kg, big_out["kv"][i], big_out["q"][i],
                big_out["o"][i], big_out["ffn_in"][i], big_out["ffn_out"][i]]

    return (loss, dx[None], *leaves(0), *leaves(1), *leaves(2), *leaves(3))
```

```python
import functools

import jax
import jax.numpy as jnp
import numpy as np
from jax import lax
from jax.experimental import pallas as pl
from jax.experimental.pallas import tpu as pltpu

F32 = jnp.float32
BF16 = jnp.bfloat16
HEAD_DIM = 64
DILATIONS = (1, 4, 16)
NORM_EPS = 1e-6
NEG_BIG = -1e30
VMEM_LIMIT = 48 * 1024 * 1024
ROW_TILE = 256
NORM_TILE = 512
LANE = 128
MESH = pl.DeviceIdType.MESH

ADAM_LR = 0.001
ADAM_B1 = 0.9
ADAM_B2 = 0.999
ADAM_EPS = 1e-08
ADAM_WD = 0.01
ADAM_STEP = 10

TILE_CANDIDATES = (1024, 1408, 768, 512, 384, 256, 128)


def _pick(dim, cands=TILE_CANDIDATES):
    for c in cands:
        if c <= dim and dim % c == 0:
            return c
    return dim


def _params(sem):
    return pltpu.CompilerParams(dimension_semantics=sem, vmem_limit_bytes=VMEM_LIMIT)


def _mm(a, b, mode, out_dtype, name, scale=None):
    a_planes = a.shape[0] if a.ndim == 3 else 1
    b_planes = b.shape[0] if b.ndim == 3 else 1
    if mode == "nn":
        m, k = a.shape[-2], a.shape[-1] * a_planes
        n = b.shape[1]
    elif mode == "nt":
        m, k = a.shape[-2], a.shape[-1] * a_planes
        n = b.shape[0]
    else:
        k, m = a.shape
        n = b.shape[-1] * b_planes
    tm, tn, tk = _pick(m), _pick(n // b_planes), _pick(k // a_planes)
    nk = k // tk
    ka, nb = k // a_planes // tk, n // b_planes // tn
    if a_planes > 1:
        a_spec = pl.BlockSpec((None, tm, tk), lambda i, j, kk: (kk // ka, i, kk % ka))
    elif mode == "tn":
        a_spec = pl.BlockSpec((tk, tm), lambda i, j, kk: (kk, i))
    else:
        a_spec = pl.BlockSpec((tm, tk), lambda i, j, kk: (i, kk))
    if mode == "nn":
        b_spec = pl.BlockSpec((tk, tn), lambda i, j, kk: (kk, j))
        dims = (((1,), (0,)), ((), ()))
    elif mode == "nt":
        b_spec = pl.BlockSpec((tn, tk), lambda i, j, kk: (j, kk))
        dims = (((1,), (1,)), ((), ()))
    else:
        b_spec = (pl.BlockSpec((None, tk, tn), lambda i, j, kk: (j // nb, kk, j % nb)) if b_planes > 1
                  else pl.BlockSpec((tk, tn), lambda i, j, kk: (kk, j)))
        dims = (((0,), (0,)), ((), ()))

    def finish(acc):
        if scale is not None:
            acc = acc * scale
        return acc.astype(out_dtype)

    if nk == 1:
        def body(a_ref, b_ref, o_ref):
            o_ref[...] = finish(lax.dot_general(a_ref[...].astype(BF16), b_ref[...].astype(BF16), dims, preferred_element_type=F32))
        scratch = []
    else:
        def body(a_ref, b_ref, o_ref, acc_ref):
            kk = pl.program_id(2)

            @pl.when(kk == 0)
            def _():
                acc_ref[...] = jnp.zeros_like(acc_ref)

            acc_ref[...] += lax.dot_general(a_ref[...].astype(BF16), b_ref[...].astype(BF16), dims, preferred_element_type=F32)

            @pl.when(kk == nk - 1)
            def _():
                o_ref[...] = finish(acc_ref[...])
        scratch = [pltpu.VMEM((tm, tn), F32)]

    return pl.pallas_call(
        body, name=name,
        grid=(m // tm, n // tn, nk),
        in_specs=[a_spec, b_spec],
        out_specs=pl.BlockSpec((tm, tn), lambda i, j, kk: (i, j)),
        out_shape=jax.ShapeDtypeStruct((m, n), out_dtype),
        scratch_shapes=scratch,
        compiler_params=_params(("parallel", "parallel", "arbitrary")),
    )(a, b)


def _rstd(v):
    return lax.rsqrt(jnp.mean(v * v, axis=-1, keepdims=True) + NORM_EPS)


def _rms_bwd(dy, v, g, r):
    gy = dy * g
    dv = r * (gy - v * (r * r) * jnp.mean(gy * v, axis=-1, keepdims=True))
    return dv, dy * v * r


def _row_spec(t, width):
    return pl.BlockSpec((t, width), lambda i: (i, 0))


def _gain_spec(width):
    return pl.BlockSpec((1, width), lambda i: (0, 0))


def _norm_res_fwd(x, mix, g_post, pre_gains, name):
    s, d = x.shape
    t = _pick(s, (NORM_TILE, ROW_TILE))
    has_mix = mix is not None
    n_pre = len(pre_gains)

    def body(*refs):
        x_ref = refs[0]
        pos = 1
        x1 = x_ref[...]
        if has_mix:
            mv = refs[1][...].astype(F32)
            x1 = x1 + mv * _rstd(mv) * refs[2][...]
            pos = 3
        gains = refs[pos:pos + n_pre]
        outs = refs[pos + n_pre:]
        if has_mix:
            outs[0][...] = x1
            outs = outs[1:]
        r = _rstd(x1)
        for g_ref, o_ref in zip(gains, outs):
            o_ref[...] = (x1 * r * g_ref[...]).astype(BF16)

    ins = [x] + ([mix, g_post] if has_mix else []) + list(pre_gains)
    in_specs = [_row_spec(t, d)] + ([_row_spec(t, d), _gain_spec(d)] if has_mix else []) + [_gain_spec(d)] * n_pre
    out_shape = ([jax.ShapeDtypeStruct((s, d), F32)] if has_mix else []) + [jax.ShapeDtypeStruct((s, d), BF16)] * n_pre
    out_specs = [_row_spec(t, d)] * len(out_shape)
    res = pl.pallas_call(
        body, name=name, grid=(s // t,), in_specs=in_specs, out_specs=out_specs, out_shape=out_shape,
        compiler_params=_params(("parallel",)),
    )(*ins)
    if has_mix:
        return res[0], list(res[1:])
    return x, list(res)


def _norm_res_loss(x, mix, g_post, target, name):
    s, d = x.shape
    t = _pick(s, (NORM_TILE, ROW_TILE))

    def body(x_ref, m_ref, g_ref, t_ref, dy_ref, loss_ref):
        mv = m_ref[...].astype(F32)
        y = x_ref[...] + mv * _rstd(mv) * g_ref[...]
        err = y - t_ref[...]
        dy_ref[...] = err * (1.0 / d)

        @pl.when(pl.program_id(0) == 0)
        def _():
            loss_ref[...] = jnp.zeros_like(loss_ref)

        loss_ref[...] += jnp.sum(err * err)

    dy, acc = pl.pallas_call(
        body, name=name, grid=(s // t,),
        in_specs=[_row_spec(t, d), _row_spec(t, d), _gain_spec(d), _row_spec(t, d)],
        out_specs=[_row_spec(t, d), pl.BlockSpec((8, LANE), lambda i: (0, 0))],
        out_shape=[jax.ShapeDtypeStruct((s, d), F32), jax.ShapeDtypeStruct((8, LANE), F32)],
        compiler_params=_params(("arbitrary",)),
    )(x, mix, g_post, target)
    return dy, acc[0, 0] * (0.5 / d)


def _norm_bwd(dx_out, branches, x_in, post, name):
    s, d = dx_out.shape
    t = _pick(s, (NORM_TILE, ROW_TILE))
    nb = len(branches)
    has_post = post is not None

    def body(*refs):
        dx_ref = refs[0]
        pos = 1
        dx = dx_ref[...]
        first = pl.program_id(0) == 0
        n_in = 1 + (1 + 2 * nb if nb else 0) + (2 if has_post else 0)
        outs = refs[n_in:]
        opos = 0
        if nb:
            xv = refs[pos][...]
            pos += 1
            r = _rstd(xv)
            dx_o = outs[0]
            opos = 1
            for _ in range(nb):
                dxn = refs[pos][...].astype(F32)
                g = refs[pos + 1][...]
                pos += 2
                dv, dg_rows = _rms_bwd(dxn, xv, g, r)
                dx = dx + dv
                dg_ref = outs[opos]
                opos += 1

                @pl.when(first)
                def _(dg_ref=dg_ref):
                    dg_ref[...] = jnp.zeros_like(dg_ref)

                dg_ref[...] += jnp.sum(dg_rows, axis=0, keepdims=True)
            dx_o[...] = dx
        if has_post:
            mv = refs[pos][...].astype(F32)
            g = refs[pos + 1][...]
            dm, dg_rows = _rms_bwd(dx, mv, g, _rstd(mv))
            outs[opos][...] = dm.astype(BF16)
            dg_ref = outs[opos + 1]

            @pl.when(first)
            def _():
                dg_ref[...] = jnp.zeros_like(dg_ref)

            dg_ref[...] += jnp.sum(dg_rows, axis=0, keepdims=True)

    ins, in_specs = [dx_out], [_row_spec(t, d)]
    out_shape, out_specs = [], []
    if nb:
        ins.append(x_in)
        in_specs.append(_row_spec(t, d))
        out_shape.append(jax.ShapeDtypeStruct((s, d), F32))
        out_specs.append(_row_spec(t, d))
        for dxn, g in branches:
            ins += [dxn, g]
            in_specs += [_row_spec(t, d), _gain_spec(d)]
            out_shape.append(jax.ShapeDtypeStruct((1, d), F32))
            out_specs.append(_gain_spec(d))
    if has_post:
        ins += [post[0], post[1]]
        in_specs += [_row_spec(t, d), _gain_spec(d)]
        out_shape += [jax.ShapeDtypeStruct((s, d), BF16), jax.ShapeDtypeStruct((1, d), F32)]
        out_specs += [_row_spec(t, d), _gain_spec(d)]
    res = pl.pallas_call(
        body, name=name, grid=(s // t,), in_specs=in_specs, out_specs=out_specs, out_shape=out_shape,
        compiler_params=_params(("arbitrary",)),
    )(*ins)
    res = list(res)
    dx_in = res.pop(0) if nb else dx_out
    dgs = [res.pop(0) for _ in range(nb)]
    dm, dg_post = (res[0], res[1]) if has_post else (None, None)
    return dx_in, dgs, dm, dg_post


HALO = 16


def _shift_down(u, prev, k):
    rows = lax.broadcasted_iota(jnp.int32, u.shape, 0)
    out = pltpu.roll(u, k, 0)
    for i in range(k):
        out = jnp.where(rows == i, prev[HALO - k + i:HALO - k + i + 1, :], out)
    return out


def _shift_up(u, nxt, k):
    n = u.shape[0]
    rows = lax.broadcasted_iota(jnp.int32, u.shape, 0)
    out = pltpu.roll(u, n - k, 0)
    for i in range(k):
        out = jnp.where(rows == n - k + i, nxt[i:i + 1, :], out)
    return out


def _conv_gate_fwd(p, cw, name):
    s, d3 = p.shape
    d = d3 // 3
    t = _pick(s, (ROW_TILE,))
    hb = t // HALO

    def body(p_ref, prev_ref, w_ref, z_ref):
        i = pl.program_id(0)
        pv = p_ref[...].astype(F32)
        b, u = pv[:, :d], pv[:, d:2 * d] * pv[:, 2 * d:]
        ph = prev_ref[...].astype(F32)
        up = jnp.where(i > 0, ph[:, d:2 * d] * ph[:, 2 * d:], 0.0)
        w = w_ref[...]
        y = w[0:1, :] * _shift_down(u, up, 2) + w[1:2, :] * _shift_down(u, up, 1) + w[2:3, :] * u
        z_ref[...] = (b * y).astype(BF16)

    return pl.pallas_call(
        body, name=name, grid=(s // t,),
        in_specs=[_row_spec(t, d3),
                  pl.BlockSpec((HALO, d3), lambda i: (jnp.maximum(i * hb - 1, 0), 0)),
                  pl.BlockSpec((3, d), lambda i: (0, 0))],
        out_specs=_row_spec(t, d),
        out_shape=jax.ShapeDtypeStruct((s, d), BF16),
        compiler_params=_params(("parallel",)),
    )(p, p, cw)


def _conv_gate_bwd(p, dz, cw, name):
    s, d3 = p.shape
    d = d3 // 3
    t = _pick(s, (ROW_TILE,))
    hb = t // HALO
    nt = s // t
    last_halo = s // HALO - 1

    def body(p_ref, prev_ref, next_ref, dz_ref, dznext_ref, w_ref, dp_ref, dw_ref):
        i = pl.program_id(0)
        pv = p_ref[...].astype(F32)
        b, c, h = pv[:, :d], pv[:, d:2 * d], pv[:, 2 * d:]
        u = c * h
        ph = prev_ref[...].astype(F32)
        up = jnp.where(i > 0, ph[:, d:2 * d] * ph[:, 2 * d:], 0.0)
        w = w_ref[...]
        u1, u2 = _shift_down(u, up, 1), _shift_down(u, up, 2)
        y = w[0:1, :] * u2 + w[1:2, :] * u1 + w[2:3, :] * u
        dz = dz_ref[...].astype(F32)
        dy = dz * b
        dyn = jnp.where(i < nt - 1, dznext_ref[...].astype(F32) * next_ref[...].astype(F32)[:, :d], 0.0)
        du = w[2:3, :] * dy + w[1:2, :] * _shift_up(dy, dyn, 1) + w[0:1, :] * _shift_up(dy, dyn, 2)
        dp_ref[:, :d] = (dz * y).astype(BF16)
        dp_ref[:, d:2 * d] = (du * h).astype(BF16)
        dp_ref[:, 2 * d:] = (du * c).astype(BF16)

        @pl.when(i == 0)
        def _():
            dw_ref[...] = jnp.zeros_like(dw_ref)

        dw_ref[0:1, :] += jnp.sum(dy * u2, axis=0, keepdims=True)
        dw_ref[1:2, :] += jnp.sum(dy * u1, axis=0, keepdims=True)
        dw_ref[2:3, :] += jnp.sum(dy * u, axis=0, keepdims=True)

    return pl.pallas_call(
        body, name=name, grid=(nt,),
        in_specs=[_row_spec(t, d3),
                  pl.BlockSpec((HALO, d3), lambda i: (jnp.maximum(i * hb - 1, 0), 0)),
                  pl.BlockSpec((HALO, d3), lambda i: (jnp.minimum((i + 1) * hb, last_halo), 0)),
                  _row_spec(t, d),
                  pl.BlockSpec((HALO, d), lambda i: (jnp.minimum((i + 1) * hb, last_halo), 0)),
                  pl.BlockSpec((3, d), lambda i: (0, 0))],
        out_specs=[_row_spec(t, d3), pl.BlockSpec((3, d), lambda i: (0, 0))],
        out_shape=[jax.ShapeDtypeStruct((s, d3), BF16), jax.ShapeDtypeStruct((3, d), F32)],
        compiler_params=_params(("arbitrary",)),
    )(p, p, p, dz, dz, cw)


FFN_ROWS, FFN_COLS = (1024, 512, 256), (256, 128)


def _ffn_in_swiglu(xn, w_in, name):
    s, k = xn.shape
    ff = w_in.shape[1] // 2
    tm, tn = _pick(s, FFN_ROWS), _pick(ff, FFN_COLS)
    nj = ff // tn

    def body(x_ref, wg_ref, wu_ref, f_ref, a_ref):
        xv = x_ref[...]
        gate = jnp.dot(xv, wg_ref[...], preferred_element_type=F32)
        up = jnp.dot(xv, wu_ref[...], preferred_element_type=F32)
        f_ref[0] = gate.astype(BF16)
        f_ref[1] = up.astype(BF16)
        a_ref[...] = (gate * jax.nn.sigmoid(gate) * up).astype(BF16)

    return pl.pallas_call(
        body, name=name, grid=(s // tm, nj),
        in_specs=[pl.BlockSpec((tm, k), lambda i, j: (i, 0)),
                  pl.BlockSpec((k, tn), lambda i, j: (0, j)),
                  pl.BlockSpec((k, tn), lambda i, j: (0, nj + j))],
        out_specs=[pl.BlockSpec((2, tm, tn), lambda i, j: (0, i, j)), pl.BlockSpec((tm, tn), lambda i, j: (i, j))],
        out_shape=[jax.ShapeDtypeStruct((2, s, ff), BF16), jax.ShapeDtypeStruct((s, ff), BF16)],
        compiler_params=_params(("parallel", "parallel")),
    )(xn, w_in, w_in)


def _ffn_out_dx_swiglu(dff, w_out, f, name):
    s, d = dff.shape
    ff = w_out.shape[0]
    tm, tn = _pick(s, FFN_ROWS), _pick(ff, FFN_COLS)

    def body(d_ref, w_ref, f_ref, df_ref):
        da = lax.dot_general(d_ref[...], w_ref[...], (((1,), (1,)), ((), ())), preferred_element_type=F32)
        gate = f_ref[0].astype(F32)
        up = f_ref[1].astype(F32)
        sg = jax.nn.sigmoid(gate)
        silu = gate * sg
        df_ref[0] = (da * up * (sg + silu * (1.0 - sg))).astype(BF16)
        df_ref[1] = (da * silu).astype(BF16)

    planes = pl.BlockSpec((2, tm, tn), lambda i, j: (0, i, j))
    return pl.pallas_call(
        body, name=name, grid=(s // tm, ff // tn),
        in_specs=[pl.BlockSpec((tm, d), lambda i, j: (i, 0)), pl.BlockSpec((tn, d), lambda i, j: (j, 0)), planes],
        out_specs=planes, out_shape=jax.ShapeDtypeStruct((2, s, ff), BF16),
        compiler_params=_params(("parallel", "parallel")),
    )(dff, w_out, f)


SUPER = 2048
RES = 16
PAIR = 128
L = 128
FWD_TOGETHER = 4
BWD_TOGETHER = 4


def _alibi_slopes(n_heads):
    h = np.arange(n_heads, dtype=np.float32) + 1.0
    return np.power(2.0, -8.0 * h / n_heads).astype(np.float32)


def _permute16(x, inverse, name):
    s, d = x.shape
    cw = LANE

    def body(x_ref, o_ref):
        if inverse:
            for m in range(L):
                o_ref[RES * m:RES * (m + 1), :] = x_ref[pl.ds(m, RES, stride=L), :]
        else:
            for r in range(RES):
                o_ref[L * r:L * (r + 1), :] = x_ref[pl.ds(r, L, stride=RES), :]

    spec = pl.BlockSpec((SUPER, cw), lambda i, j: (i, j))
    return pl.pallas_call(
        body, name=name, grid=(s // SUPER, d // cw), in_specs=[spec], out_specs=spec,
        out_shape=jax.ShapeDtypeStruct((s, d), x.dtype),
        compiler_params=_params(("parallel", "parallel")),
    )(x)


def _slope_table(d):
    nh = d // HEAD_DIM
    sl = _alibi_slopes(nh)
    tab = np.repeat(sl, HEAD_DIM).reshape(d // PAIR, 1, PAIR)
    return jnp.asarray(np.broadcast_to(tab, (d // PAIR, 8, PAIR)).copy())


def _geometry(dil):
    nch = RES // dil
    return nch, L // nch


def _band(dil):
    nch, w = _geometry(dil)
    sh = w.bit_length() - 1
    i = lax.broadcasted_iota(jnp.int32, (L, 2 * L), 0)
    j = lax.broadcasted_iota(jnp.int32, (L, 2 * L), 1)

    def pos(t):
        return jnp.bitwise_and(t, w - 1) * nch + jnp.right_shift(t, sh)

    delta = pos(i) + L - (pos(jnp.bitwise_and(j, L - 1)) + jnp.bitwise_and(j, L))
    return (delta * dil).astype(F32), (delta >= 0) & (delta <= L), j < L


def _fill_bias(bias_s, sl_ref):
    for b, dil in enumerate(DILATIONS):
        base, band, prev_half = _band(dil)
        for first in range(2):
            valid = band & jnp.logical_not(prev_half) if first else band
            for h in range(2):
                slope = sl_ref[0:1, HEAD_DIM * h:HEAD_DIM * h + 1]
                bias_s[(2 * b + first) * 2 + h] = jnp.where(valid, -slope * base, NEG_BIG)


def _bias_index(b, sb, n):
    first = jnp.logical_and(sb == 0, n == 0).astype(jnp.int32)
    return (2 * b + first) * 2


def _offsets(dil, res, n):
    nch, w = _geometry(dil)

    def al(v):
        return v if isinstance(v, int) else pl.multiple_of(v, w)

    q_off = [al((a * dil + res) * L + n * w) for a in range(nch)]
    k_off = [al((a * dil + res) * 2 * L + L + n * w) for a in range(nch)]
    kp_off = [al((a * dil + res) * 2 * L + L + n * w - w) for a in range(nch)]
    return q_off, k_off, kp_off, w


def _gather(ref, offs, w):
    parts = [ref[pl.ds(o, w), :] for o in offs]
    return parts[0] if len(parts) == 1 else jnp.concatenate(parts, axis=0)


def _scatter(ref, offs, w, val, add=False):
    for a, o in enumerate(offs):
        piece = val[a * w:(a + 1) * w, :]
        if add:
            ref[pl.ds(o, w), :] += piece
        else:
            ref[pl.ds(o, w), :] = piece


def _fill_key_buffer(buf, prev_ref, cur_ref):
    for r in range(RES):
        buf[2 * L * r:2 * L * r + L, :] = prev_ref[L * r:L * (r + 1), :]
        buf[2 * L * r + L:2 * L * (r + 1), :] = cur_ref[L * r:L * (r + 1), :]


def _two_heads(x, low):
    zero = jnp.zeros_like(x)
    return jnp.concatenate([jnp.where(low, x, zero), jnp.where(low, zero, x)], axis=0)


def _loop_blocks(dil, stages, together):
    together = max(together, dil) if dil < RES else together

    def it(i, c):
        if dil == RES:
            blocks = [(i * together + k, 0) for k in range(together)]
        else:
            blocks = [(res, i * (together // dil) + k) for k in range(together // dil) for res in range(dil)]
        state = [stages[0](res, n) for res, n in blocks]
        for stage in stages[1:]:
            state = [stage(res, n, prev) for (res, n), prev in zip(blocks, state)]
        for writes in state:
            for args in writes:
                _scatter(*args)
        return c

    lax.fori_loop(0, RES // together, it, 0)


NT = (((1,), (1,)), ((), ()))
TN = (((0,), (0,)), ((), ()))


def _attention_fwd(q, kv, name):
    s, d = q.shape
    g_n, ns = d // PAIR, s // SUPER

    def body(sl_ref, q_ref, kc_ref, kp_ref, vc_ref, vp_ref, o_ref, lse_ref, kbuf, vbuf, m_s, l_s, acc_s, bias_s):
        sb = pl.program_id(1)
        _fill_key_buffer(kbuf, kp_ref, kc_ref)
        _fill_key_buffer(vbuf, vp_ref, vc_ref)
        _fill_bias(bias_s, sl_ref)
        low = lax.broadcasted_iota(jnp.int32, (L, PAIR), 1) < HEAD_DIM
        low_k = lax.broadcasted_iota(jnp.int32, (2 * L, PAIR), 1) < HEAD_DIM
        ones_bd = _two_heads(jnp.ones((2 * L, PAIR), BF16), low_k)
        for bi, dil in enumerate(DILATIONS):
            first_branch, last_branch = bi == 0, bi == len(DILATIONS) - 1

            def scores(res, n, dil=dil):
                q_off, k_off, kp_off, w = _offsets(dil, res, n)
                qf = _gather(q_ref, q_off, w).astype(BF16)
                kcat = jnp.concatenate([_gather(kbuf, kp_off, w), _gather(kbuf, k_off, w)], axis=0).astype(BF16)
                return lax.dot_general(qf, _two_heads(kcat, low_k), NT, preferred_element_type=F32)

            def update(res, n, sc, bi=bi, dil=dil, first_branch=first_branch, last_branch=last_branch):
                q_off, k_off, kp_off, w = _offsets(dil, res, n)
                vcat = jnp.concatenate([_gather(vbuf, kp_off, w), _gather(vbuf, k_off, w)], axis=0).astype(BF16)
                v_ones = jnp.concatenate([_two_heads(vcat, low_k), ones_bd], axis=1)
                bias_at = _bias_index(bi, sb, n)
                if not first_branch:
                    m_prev = _gather(m_s, q_off, w)
                ps, m_new = [], []
                for h in range(2):
                    s_h = sc[:, 2 * L * h:2 * L * (h + 1)] + bias_s[bias_at + h]
                    mh = jnp.max(s_h, axis=1, keepdims=True)
                    if not first_branch:
                        mh = jnp.maximum(mh, m_prev[:, HEAD_DIM * h:HEAD_DIM * h + 1])
                    ps.append(jnp.exp(s_h - mh).astype(BF16))
                    m_new.append(mh)
                m_full = jnp.where(low, m_new[0], m_new[1])
                both = jnp.dot(jnp.concatenate(ps, axis=1), v_ones, preferred_element_type=F32)
                acc, l_full = both[:, :PAIR], both[:, PAIR:]
                if not first_branch:
                    alpha = jnp.exp(m_prev - m_full)
                    l_full = _gather(l_s, q_off, w) * alpha + l_full
                    acc = _gather(acc_s, q_off, w) * alpha + acc
                if last_branch:
                    return [(o_ref, q_off, w, acc / l_full, False), (lse_ref, q_off, w, m_full + jnp.log(l_full), False)]
                return [(m_s, q_off, w, m_full, False), (l_s, q_off, w, l_full, False), (acc_s, q_off, w, acc, False)]

            _loop_blocks(dil, [scores, update], FWD_TOGETHER)

    prev = lambda i: jnp.maximum(i - 1, 0)
    blk = pl.BlockSpec((SUPER, PAIR), lambda g, i: (i, g))
    in_specs = [pl.BlockSpec((None, 8, PAIR), lambda g, i: (g, 0, 0)), blk,
                pl.BlockSpec((SUPER, PAIR), lambda g, i: (i, g)),
                pl.BlockSpec((SUPER, PAIR), lambda g, i: (prev(i), g)),
                pl.BlockSpec((SUPER, PAIR), lambda g, i: (i, g_n + g)),
                pl.BlockSpec((SUPER, PAIR), lambda g, i: (prev(i), g_n + g))]
    return pl.pallas_call(
        body, name=name, grid=(g_n, ns), in_specs=in_specs, out_specs=[blk, blk],
        out_shape=[jax.ShapeDtypeStruct((s, d), F32)] * 2,
        scratch_shapes=([pltpu.VMEM((2 * SUPER, PAIR), F32)] * 2 + [pltpu.VMEM((SUPER, PAIR), F32)] * 3
                        + [pltpu.VMEM((4 * len(DILATIONS), L, 2 * L), F32)]),
        compiler_params=_params(("parallel", "parallel")),
    )(_slope_table(d), q, kv, kv, kv, kv)


def _attention_bwd(q, kv, o, do, lse, dkv_in, name):
    s, d = q.shape
    g_n, ns = d // PAIR, s // SUPER
    has_in = dkv_in is not None

    def body(*refs):
        sl_ref, q_ref, do_ref, o_ref, lse_ref, kc_ref, kp_ref, vc_ref, vp_ref = refs[:9]
        pos = 9
        if has_in:
            dkv_in_ref = refs[9]
            pos = 10
        dq_ref, dkv_ref, kbuf, vbuf, dkbuf, dvbuf, dq_s, bias_s = refs[pos:]
        step = pl.program_id(1)
        sb = ns - 1 - step
        _fill_key_buffer(kbuf, kp_ref, kc_ref)
        _fill_key_buffer(vbuf, vp_ref, vc_ref)
        _fill_bias(bias_s, sl_ref)

        @pl.when(step == 0)
        def _():
            dkbuf[...] = jnp.zeros_like(dkbuf)
            dvbuf[...] = jnp.zeros_like(dvbuf)

        @pl.when(step > 0)
        def _():
            for buf in (dkbuf, dvbuf):
                for r in range(RES):
                    buf[2 * L * r + L:2 * L * (r + 1), :] = buf[2 * L * r:2 * L * r + L, :]
                    buf[2 * L * r:2 * L * r + L, :] = jnp.zeros((L, PAIR), F32)

        low = lax.broadcasted_iota(jnp.int32, (L, PAIR), 1) < HEAD_DIM
        low_k = lax.broadcasted_iota(jnp.int32, (2 * L, PAIR), 1) < HEAD_DIM
        for bi, dil in enumerate(DILATIONS):
            first_branch = bi == 0

            def scores(res, n, dil=dil):
                q_off, k_off, kp_off, w = _offsets(dil, res, n)
                qb = _gather(q_ref, q_off, w).astype(BF16)
                dof = _gather(do_ref, q_off, w)
                prod = dof * _gather(o_ref, q_off, w)
                dob = dof.astype(BF16)
                lse_f = _gather(lse_ref, q_off, w)
                zero = jnp.zeros_like(prod)
                dsum = (jnp.sum(jnp.where(low, prod, zero), axis=1, keepdims=True),
                        jnp.sum(jnp.where(low, zero, prod), axis=1, keepdims=True))
                kcat = jnp.concatenate([_gather(kbuf, kp_off, w), _gather(kbuf, k_off, w)], axis=0).astype(BF16)
                vcat = jnp.concatenate([_gather(vbuf, kp_off, w), _gather(vbuf, k_off, w)], axis=0).astype(BF16)
                k_bd, v_bd = _two_heads(kcat, low_k), _two_heads(vcat, low_k)
                sc = lax.dot_general(qb, k_bd, NT, preferred_element_type=F32)
                dp = lax.dot_general(dob, v_bd, NT, preferred_element_type=F32)
                return qb, dob, lse_f, dsum, k_bd, sc, dp

            def gradients(res, n, given, bi=bi, dil=dil, first_branch=first_branch):
                qb, dob, lse_f, dsum, k_bd, sc, dp = given
                q_off, k_off, kp_off, w = _offsets(dil, res, n)
                bias_at = _bias_index(bi, sb, n)
                ps, dss = [], []
                for h in range(2):
                    cols = slice(2 * L * h, 2 * L * (h + 1))
                    lse_h = lse_f[:, HEAD_DIM * h:HEAD_DIM * h + 1]
                    p_h = jnp.exp(sc[:, cols] + bias_s[bias_at + h] - lse_h)
                    dss.append((p_h * (dp[:, cols] - dsum[h])).astype(BF16))
                    ps.append(p_h.astype(BF16))
                ds_cat, p_cat = jnp.concatenate(dss, axis=1), jnp.concatenate(ps, axis=1)
                dq = jnp.dot(ds_cat, k_bd, preferred_element_type=F32)
                dk_bd = lax.dot_general(ds_cat, qb, TN, preferred_element_type=F32)
                dv_bd = lax.dot_general(p_cat, dob, TN, preferred_element_type=F32)
                dk = jnp.where(low_k, dk_bd[:2 * L], dk_bd[2 * L:])
                dv = jnp.where(low_k, dv_bd[:2 * L], dv_bd[2 * L:])
                return [(dq_s, q_off, w, dq, not first_branch),
                        (dkbuf, kp_off, w, dk[:L], True), (dkbuf, k_off, w, dk[L:], True),
                        (dvbuf, kp_off, w, dv[:L], True), (dvbuf, k_off, w, dv[L:], True)]

            _loop_blocks(dil, [scores, gradients], BWD_TOGETHER)

        dq_ref[...] = dq_s[...].astype(BF16)
        for r in range(RES):
            rows, cur = slice(L * r, L * (r + 1)), slice(2 * L * r + L, 2 * L * (r + 1))
            for plane, buf in enumerate((dkbuf, dvbuf)):
                if has_in:
                    dkv_ref[plane, rows, :] = buf[cur, :] + dkv_in_ref[plane, rows, :]
                else:
                    dkv_ref[plane, rows, :] = buf[cur, :]

    rev = lambda i: ns - 1 - i
    prev = lambda i: jnp.maximum(ns - 2 - i, 0)
    blk = pl.BlockSpec((SUPER, PAIR), lambda g, i: (rev(i), g))
    in_specs = [pl.BlockSpec((None, 8, PAIR), lambda g, i: (g, 0, 0)), blk, blk, blk, blk,
                pl.BlockSpec((SUPER, PAIR), lambda g, i: (rev(i), g)),
                pl.BlockSpec((SUPER, PAIR), lambda g, i: (prev(i), g)),
                pl.BlockSpec((SUPER, PAIR), lambda g, i: (rev(i), g_n + g)),
                pl.BlockSpec((SUPER, PAIR), lambda g, i: (prev(i), g_n + g))]
    ins = [_slope_table(d), q, do, o, lse, kv, kv, kv, kv]
    planes = pl.BlockSpec((2, SUPER, PAIR), lambda g, i: (0, rev(i), g))
    if has_in:
        in_specs.append(planes)
        ins.append(dkv_in)
    res = pl.pallas_call(
        body, name=name, grid=(g_n, ns), in_specs=in_specs, out_specs=[blk, planes],
        out_shape=[jax.ShapeDtypeStruct((s, d), BF16), jax.ShapeDtypeStruct((2, s, d), F32)],
        scratch_shapes=([pltpu.VMEM((2 * SUPER, PAIR), F32)] * 4 + [pltpu.VMEM((SUPER, PAIR), F32)]
                        + [pltpu.VMEM((4 * len(DILATIONS), L, 2 * L), F32)]),
        compiler_params=_params(("parallel", "arbitrary")),
    )(*ins)
    return res[0], res[1]


def _coords():
    return lax.axis_index("x"), lax.axis_index("y"), lax.axis_index("c")


def _chip_peers(x, y):
    return [(1 - x, y), (x, 1 - y), (1 - x, 1 - y)]


def _block_of(ref, axis, blk, size):
    start = pl.multiple_of(blk * size, size)
    if axis == 1:
        return ref.at[:, pl.ds(start, size)]
    return ref.at[pl.ds(start, size), :]


ANY = pl.BlockSpec(memory_space=pl.ANY)


HBM = pl.BlockSpec(memory_space=pltpu.HBM)
SEM = pl.BlockSpec(memory_space=pltpu.SEMAPHORE)
SPLIT = pltpu.CompilerParams(has_side_effects=pltpu.SideEffectType.DATAFLOW_SIDE_EFFECTING)


def _in_hbm(a):
    return pltpu.with_memory_space_constraint(a, pltpu.HBM)


def _thru(arrays):
    return [pltpu.HBM(a.shape, a.dtype) for a in arrays]


def _cast_place(w, ax, dtype, name):
    k, n = w.shape
    t = _pick(k, (512, 256, 128))
    nb = k // t

    def body(blk_ref, w_ref, b_ref, f_ref):
        v = w_ref[...].astype(dtype)
        b_ref[...] = v
        f_ref[...] = v

    full_shape = (k, 4 * n) if ax == 1 else (4 * k, n)
    place = (lambda i, blk: (i, blk[0])) if ax == 1 else (lambda i, blk: (blk[0] * nb + i, 0))
    spec = pl.BlockSpec((t, n), lambda i, blk: (i, 0))
    return pl.pallas_call(
        body, name=name,
        grid_spec=pltpu.PrefetchScalarGridSpec(
            num_scalar_prefetch=1, grid=(nb,), in_specs=[spec], out_specs=[spec, pl.BlockSpec((t, n), place)]),
        out_shape=[jax.ShapeDtypeStruct((k, n), dtype), jax.ShapeDtypeStruct(full_shape, dtype)],
        compiler_params=_params(("parallel",)),
    )(_my_block()[None], w)


def _my_block():
    return (2 * lax.axis_index("x") + lax.axis_index("y")).astype(jnp.int32)


def _gather_start(group, carry, name):
    n, nc = len(group), len(carry)

    def body(*refs):
        blocks, fulls, send_sem, recv_sem = refs[:n], refs[n:2 * n], refs[2 * n + nc], refs[2 * n + nc + 1]
        x, y, c = _coords()
        for t, (b, _, ax) in enumerate(group):
            mine = _block_of(fulls[t], ax, 2 * x + y, b.shape[ax])
            for j, (px, py) in enumerate(_chip_peers(x, y)):
                pltpu.make_async_remote_copy(
                    src_ref=blocks[t], dst_ref=mine, send_sem=send_sem.at[3 * t + j], recv_sem=recv_sem.at[3 * t + j],
                    device_id=(px, py, c), device_id_type=MESH).start()

    arrays = [b for b, _, _ in group] + [f for _, f, _ in group] + list(carry)
    sems = [pltpu.SemaphoreType.DMA((3 * n,))] * 2
    res = pl.pallas_call(
        body, name=name, in_specs=[HBM] * len(arrays), out_specs=[SEM, SEM] + [HBM] * len(arrays),
        out_shape=sems + _thru(arrays), input_output_aliases={i: 2 + i for i in range(len(arrays))},
        compiler_params=SPLIT,
    )(*[_in_hbm(a) for a in arrays])
    return (res[0], res[1], list(res[2:2 + n]), list(res[2 + n:2 + 2 * n])), list(res[2 + 2 * n:])


def _gather_wait(group, started, after, name):
    sends, recvs, blocks, fulls = started
    m = len(group)

    def body(*refs):
        blk_refs, full_refs, send_sem, recv_sem = refs[:m], refs[m:2 * m], refs[2 * m], refs[2 * m + 1]
        x, y, c = _coords()
        for t, (b, _, ax) in enumerate(group):
            for j, (px, py) in enumerate(_chip_peers(x, y)):
                cp = pltpu.make_async_remote_copy(
                    src_ref=blk_refs[t], dst_ref=_block_of(full_refs[t], ax, 2 * px + py, b.shape[ax]),
                    send_sem=send_sem.at[3 * t + j], recv_sem=recv_sem.at[3 * t + j],
                    device_id=(px, py, c), device_id_type=MESH)
                cp.wait_send()
                cp.wait_recv()

    extra = [] if after is None else [after]
    res = pl.pallas_call(
        body, name=name, in_specs=[HBM] * (2 * m) + [SEM, SEM] + [ANY] * len(extra), out_specs=[HBM] * (2 * m),
        out_shape=_thru(blocks) + _thru(fulls), input_output_aliases={i: i for i in range(2 * m)},
        compiler_params=SPLIT,
    )(*blocks, *fulls, sends, recvs, *extra)
    return list(res[m:])


def _scatter_start(grads, carry, name):
    n = len(grads)
    n_in = 2 * n + len(carry)

    def body(*refs):
        g_refs, st_refs, send_sem, recv_sem = refs[:n], refs[n:2 * n], refs[n_in], refs[n_in + 1]
        x, y, c = _coords()
        for t, (g, ax) in enumerate(grads):
            for j, (px, py) in enumerate(_chip_peers(x, y)):
                pltpu.make_async_remote_copy(
                    src_ref=_block_of(g_refs[t], ax, 2 * px + py, g.shape[ax] // 4), dst_ref=st_refs[t].at[j],
                    send_sem=send_sem.at[3 * t + j], recv_sem=recv_sem.at[3 * t + j],
                    device_id=(px, py, c), device_id_type=MESH).start()

    arrays = [g for g, _ in grads]
    for g, ax in grads:
        shape = list(g.shape)
        shape[ax] //= 4
        arrays.append(lax.empty((3, *shape), g.dtype))
    arrays += list(carry)
    sems = [pltpu.SemaphoreType.DMA((3 * n,))] * 2
    res = pl.pallas_call(
        body, name=name, in_specs=[HBM] * n_in, out_specs=[SEM, SEM] + [HBM] * n_in,
        out_shape=sems + _thru(arrays), input_output_aliases={i: 2 + i for i in range(n_in)},
        compiler_params=SPLIT,
    )(*[_in_hbm(a) for a in arrays])
    return (res[0], res[1], list(res[2:2 + n]), list(res[2 + n:2 + 2 * n])), list(res[2 + 2 * n:])


def _scatter_wait(axes, started, name):
    sends, recvs, full, stacks = started
    n = len(full)

    def body(*refs):
        g_refs, st_refs, send_sem, recv_sem = refs[:n], refs[n:2 * n], refs[2 * n], refs[2 * n + 1]
        x, y, c = _coords()
        for t, ax in enumerate(axes):
            size = full[t].shape[ax] // 4
            for j, (px, py) in enumerate(_chip_peers(x, y)):
                cp = pltpu.make_async_remote_copy(
                    src_ref=_block_of(g_refs[t], ax, 2 * px + py, size), dst_ref=st_refs[t].at[j],
                    send_sem=send_sem.at[3 * t + j], recv_sem=recv_sem.at[3 * t + j],
                    device_id=(px, py, c), device_id_type=MESH)
                cp.wait_send()
                cp.wait_recv()

    res = pl.pallas_call(
        body, name=name, in_specs=[HBM] * (2 * n) + [SEM, SEM], out_specs=[HBM] * (2 * n),
        out_shape=_thru(full) + _thru(stacks), input_output_aliases={i: i for i in range(2 * n)},
        compiler_params=SPLIT,
    )(*full, *stacks, sends, recvs)
    return list(res[:n]), list(res[n:])


def _pair_copies(g_refs, st_refs, out_refs, items, send_sem, recv_sem):
    x, y, c = _coords()
    copies = []
    for u, (g, ax, _) in enumerate(items):
        own = _block_of(g_refs[u], ax, 2 * x + y, g.shape[ax] // 4)
        for k, (src, dst) in enumerate([(own, out_refs[u].at[0]), (st_refs[u], out_refs[u].at[pl.ds(1, 3)])]):
            copies.append(pltpu.make_async_remote_copy(
                src_ref=src, dst_ref=dst, send_sem=send_sem.at[2 * u + k], recv_sem=recv_sem.at[2 * u + k],
                device_id=(x, y, 1 - c), device_id_type=MESH))
    return copies


def _pair_start(tensors, name):
    flat = [it for ts in tensors for it in ts]
    n, nt = len(flat), len(tensors)
    first = [sum(len(ts) for ts in tensors[:i]) for i in range(nt + 1)]

    def body(*refs):
        sends, recvs = refs[3 * n:3 * n + nt], refs[3 * n + nt:3 * n + 2 * nt]
        for i, ts in enumerate(tensors):
            rng = slice(first[i], first[i + 1])
            for cp in _pair_copies(refs[:n][rng], refs[n:2 * n][rng], refs[2 * n:3 * n][rng], ts, sends[i], recvs[i]):
                cp.start()

    arrays = ([g for g, _, _ in flat] + [st for _, _, st in flat]
              + [lax.empty((4, *st.shape[1:]), st.dtype) for _, _, st in flat])
    sems = [pltpu.SemaphoreType.DMA((2 * len(ts),)) for ts in tensors]
    res = pl.pallas_call(
        body, name=name, in_specs=[HBM] * (3 * n), out_specs=[SEM] * (2 * nt) + [HBM] * (3 * n),
        out_shape=sems + sems + _thru(arrays), input_output_aliases={i: 2 * nt + i for i in range(3 * n)},
        compiler_params=SPLIT,
    )(*[_in_hbm(a) for a in arrays])
    thru = res[2 * nt:]
    return [(res[i], res[nt + i], *(list(thru[k * n + first[i]:k * n + first[i + 1]]) for k in range(3)))
            for i in range(nt)]


def _pair_wait(axes, started, name):
    send, recv, full, stacks, landing = started
    n = len(full)
    items = [(full[u], axes[u], stacks[u]) for u in range(n)]

    def body(*refs):
        for cp in _pair_copies(refs[:n], refs[n:2 * n], refs[2 * n:3 * n], items, refs[3 * n], refs[3 * n + 1]):
            cp.wait_send()
            cp.wait_recv()

    res = pl.pallas_call(
        body, name=name, in_specs=[HBM] * (3 * n) + [SEM, SEM], out_specs=[HBM] * (3 * n),
        out_shape=_thru(full + stacks + landing), input_output_aliases={i: i for i in range(3 * n)},
        compiler_params=SPLIT,
    )(*full, *stacks, *landing, send, recv)
    return list(res[:n]), list(res[n:2 * n]), list(res[2 * n:])


def _allreduce_small(v, name):
    r, cdim = v.shape

    def body(v_ref, out_ref, buf, send_sems, recv_sems):
        x, y, c = _coords()
        me = 4 * x + 2 * y + c
        buf[0] = v_ref[...]
        sends = []
        for k in range(1, 8):
            peer = (x if not (k & 4) else 1 - x, y if not (k & 2) else 1 - y, c if not (k & 1) else 1 - c)
            cp = pltpu.make_async_remote_copy(
                src_ref=v_ref, dst_ref=buf.at[k], send_sem=send_sems.at[k - 1], recv_sem=recv_sems.at[k - 1],
                device_id=peer, device_id_type=MESH)
            cp.start()
            sends.append(cp)
        for cp in sends:
            cp.wait_recv()
        total = buf[me]
        for src in range(1, 8):
            total = total + buf[jnp.bitwise_xor(me, src)]
        out_ref[...] = total
        for cp in sends:
            cp.wait_send()

    return pl.pallas_call(
        body, name=name,
        in_specs=[pl.BlockSpec(memory_space=pltpu.VMEM)], out_specs=pl.BlockSpec(memory_space=pltpu.VMEM),
        out_shape=jax.ShapeDtypeStruct((r, cdim), F32),
        scratch_shapes=[pltpu.VMEM((8, r, cdim), F32), pltpu.SemaphoreType.DMA((7,)), pltpu.SemaphoreType.DMA((7,))],
        compiler_params=pltpu.CompilerParams(has_side_effects=True),
    )(v)


def _adamw_math(w, g, m, v):
    m = ADAM_B1 * m + (1.0 - ADAM_B1) * g
    v = ADAM_B2 * v + (1.0 - ADAM_B2) * jnp.square(g)
    m_hat = m / (1.0 - ADAM_B1 ** ADAM_STEP)
    v_hat = v / (1.0 - ADAM_B2 ** ADAM_STEP)
    delta = -ADAM_LR * (m_hat / (jnp.sqrt(v_hat) + ADAM_EPS) + ADAM_WD * w)
    return delta, m, v


def _adamw(w, m, v, grads, name):
    r, cdim = w.shape
    paired = isinstance(grads, list)
    layers = len(grads) if paired else 1
    t = _pick(r // layers, (128, 64, 32, 16, 8))
    per_layer = r // layers // t
    n_grad = 3 * layers if paired else 1

    def body(*refs):
        refs = refs[1:] if paired else refs
        w_ref, m_ref, v_ref = refs[:3]
        outs = refs[3 + n_grad:]

        def update(g):
            delta, m_new, v_new = _adamw_math(w_ref[...], g, m_ref[...], v_ref[...])
            outs[0][...] = g
            outs[1][...] = delta
            outs[2][...] = m_new
            outs[3][...] = v_new

        if not paired:
            update(refs[3][...])
            return
        layer = pl.program_id(0) // per_layer
        for l in range(layers):
            @pl.when(layer == l)
            def _(own_ref=refs[3 + 3 * l], st_ref=refs[4 + 3 * l], sib_ref=refs[5 + 3 * l]):
                sa = own_ref[...].astype(F32)
                sb = sib_ref[0].astype(F32)
                for k in range(3):
                    sa = sa + st_ref[k].astype(F32)
                    sb = sb + sib_ref[k + 1].astype(F32)
                update(sa + sb)

    out_shape = [jax.ShapeDtypeStruct((r, cdim), F32)] * 4
    if not paired:
        spec = pl.BlockSpec((t, cdim), lambda i: (i, 0))
        return pl.pallas_call(
            body, name=name, grid=(r // t,), in_specs=[spec] * 4, out_specs=[spec] * 4, out_shape=out_shape,
            compiler_params=_params(("parallel",)),
        )(w, m, v, grads)

    spec = pl.BlockSpec((t, cdim), lambda i, blk: (i, 0))
    ins, in_specs = [w, m, v], [spec] * 3
    for l, (g, ax, stack, sib) in enumerate(grads):
        row = lambda i, l=l: jnp.clip(i - l * per_layer, 0, per_layer - 1)
        own = ((lambda i, blk, row=row: (row(i), blk[0])) if ax == 1
               else (lambda i, blk, row=row: (blk[0] * per_layer + row(i), 0)))
        ins += [g, stack, sib]
        in_specs += [pl.BlockSpec((t, cdim), own),
                     pl.BlockSpec((3, t, cdim), lambda i, blk, row=row: (0, row(i), 0)),
                     pl.BlockSpec((4, t, cdim), lambda i, blk, row=row: (0, row(i), 0))]
    return pl.pallas_call(
        body, name=name,
        grid_spec=pltpu.PrefetchScalarGridSpec(
            num_scalar_prefetch=1, grid=(r // t,), in_specs=in_specs, out_specs=[spec] * 4),
        out_shape=out_shape, compiler_params=_params(("parallel",)),
    )(_my_block()[None], *ins)


def _local_step(x, target, gains, conv_ws, kv_gain, weights_of, send_grads):
    depth = len(gains)
    n_a = len(conv_ws)
    saved, ws = [], []
    kv = kvn = None
    _, (xn,) = _norm_res_fwd(x, None, None, [gains[0][0]], "norm_first")
    h = x
    for l in range(depth):
        g = gains[l]
        sv = {"x_in": h, "xn": xn}
        w = weights_of(l, "mix", h)
        ws.append(w)
        if l == n_a:
            kv = _mm(kvn, w["kv"], "nn", F32, "kv_fwd")
        if l < n_a:
            p = _mm(xn, w["conv_in"], "nn", BF16, f"conv_in_fwd_{l}")
            z = _conv_gate_fwd(p, conv_ws[l], f"conv_gate_fwd_{l}")
            mix = _mm(z, w["conv_out"], "nn", BF16, f"conv_out_fwd_{l}")
            sv.update(p=p, z=z)
        else:
            j = l - n_a
            q = _mm(xn, w["q"], "nn", F32, f"q_fwd_{j}", scale=HEAD_DIM ** -0.5)
            o, lse = _attention_fwd(q, kv, f"attn_fwd_{j}")
            mix = _mm(o, w["o"], "nn", BF16, f"o_fwd_{j}")
            sv.update(q=q, o=o, lse=lse)
        x1, (xn2,) = _norm_res_fwd(h, mix, g[1], [g[2]], f"norm_mid_{l}")
        w.update(weights_of(l, "ffn", mix))
        f, a = _ffn_in_swiglu(xn2, w["ffn_in"], f"ffn_in_fwd_{l}")
        ff = _mm(a, w["ffn_out"], "nn", BF16, f"ffn_out_fwd_{l}")
        sv.update(mix=mix, x1=x1, xn2=xn2, f=f, a=a, ff=ff)
        saved.append(sv)
        if l == depth - 1:
            dx, loss = _norm_res_loss(x1, ff, g[3], target, "norm_loss")
        else:
            if l == n_a - 1:
                h, _ = _norm_res_fwd(x1, ff, g[3], [], f"norm_end_{l}")
                h = _permute16(h, False, "permute_stream")
                target = _permute16(target, False, "permute_target")
                _, (xn, kvn) = _norm_res_fwd(h, None, None, [gains[l + 1][0], kv_gain], "norm_permuted")
            else:
                h, (xn,) = _norm_res_fwd(x1, ff, g[3], [gains[l + 1][0]], f"norm_end_{l}")
    d_gains = [[None] * 4 for _ in range(depth)]
    d_conv = [None] * n_a
    d_kv_gain = None
    dkv = None
    _, _, dff, d_gains[depth - 1][3] = _norm_bwd(dx, [], None, (saved[-1]["ff"], gains[-1][3]), "norm_loss_bwd")
    for l in reversed(range(depth)):
        sv, g, w, grads = saved[l], gains[l], ws[l], {}
        grads["ffn_out"] =_mm(sv["a"], dff, "tn", BF16, f"ffn_out_dw_{l}")
        df = _ffn_out_dx_swiglu(dff, w["ffn_out"], sv["f"], f"ffn_out_dx_{l}")
        dxn2 = _mm(df, w["ffn_in"], "nt", BF16, f"ffn_in_dx_{l}")
        grads["ffn_in"] =_mm(sv["xn2"], df, "tn", BF16, f"ffn_in_dw_{l}")
        dx, (d_gains[l][2],), dmix, d_gains[l][1] = _norm_bwd(
            dx, [(dxn2, g[2])], sv["x1"], (sv["mix"], g[1]), f"norm_mid_bwd_{l}")
        dx, dmix = send_grads(l, "ffn", grads, [dx, dmix])
        if l < n_a:
            dz = _mm(dmix, w["conv_out"], "nt", BF16, f"conv_out_dx_{l}")
            grads["conv_out"] =_mm(sv["z"], dmix, "tn", BF16, f"conv_out_dw_{l}")
            dp, d_conv[l] = _conv_gate_bwd(sv["p"], dz, conv_ws[l], f"conv_gate_bwd_{l}")
            dxn = _mm(dp, w["conv_in"], "nt", BF16, f"conv_in_dx_{l}")
            grads["conv_in"] =_mm(sv["xn"], dp, "tn", BF16, f"conv_in_dw_{l}")
        else:
            j = l - n_a
            do = _mm(dmix, w["o"], "nt", F32, f"o_dx_{j}")
            grads["o"] =_mm(sv["o"], dmix, "tn", BF16, f"o_dw_{j}")
            dq, dkv = _attention_bwd(sv["q"], kv, sv["o"], do, sv["lse"], dkv, f"attn_bwd_{j}")
            scale = HEAD_DIM ** -0.5
            dxn = _mm(dq, w["q"], "nt", BF16, f"q_dx_{j}", scale=scale)
            grads["q"] =_mm(sv["xn"], dq, "tn", BF16, f"q_dw_{j}", scale=scale)
        branches = [(dxn, g[0])]
        if l == n_a:
            dkvn = _mm(dkv, w["kv"], "nt", BF16, "kv_dx")
            grads["kv"] =_mm(kvn, dkv, "tn", BF16, "kv_dw")
            branches.append((dkvn, kv_gain))
        post = (saved[l - 1]["ff"], gains[l - 1][3]) if l > 0 else None
        if l == n_a:
            dx, dgs, _, _ = _norm_bwd(dx, branches, sv["x_in"], None, f"norm_end_bwd_{l}")
            dx = _permute16(dx, True, "unpermute_stream")
            _, _, dff, dg_post = _norm_bwd(dx, [], None, post, "norm_boundary_bwd")
        else:
            dx, dgs, dff, dg_post = _norm_bwd(dx, branches, sv["x_in"], post, f"norm_end_bwd_{l}")
        if dff is None:
            send_grads(l, "mix", grads, [])
        else:
            dx, dff = send_grads(l, "mix", grads, [dx, dff])
        d_gains[l][0] = dgs[0]
        if l == n_a:
            d_kv_gain = dgs[1]
        if l > 0:
            d_gains[l - 1][3] = dg_post
    return loss, dx, d_gains, d_conv, d_kv_gain


BIG = (
    ("conv_in", 1), ("conv_out", 0), ("kv", 1), ("q", 0), ("o", 0), ("ffn_in", 1), ("ffn_out", 0))


def kernel(x, norm_g, conv_in_w, conv_w, conv_out_w, kv_norm_g, kv_w, q_w, o_w, ffn_in_w, ffn_out_w, loss_target, m_norm_g, m_conv_in_w, m_conv_w, m_conv_out_w, m_kv_norm_g, m_kv_w, m_q_w, m_o_w, m_ffn_in_w, m_ffn_out_w, v_norm_g, v_conv_in_w, v_conv_w, v_conv_out_w, v_kv_norm_g, v_kv_w, v_q_w, v_o_w, v_ffn_in_w, v_ffn_out_w):
    depth, _, dq = norm_g.shape
    d = 4 * dq
    n_a = conv_w.shape[0]
    big_w = {"conv_in": conv_in_w, "conv_out": conv_out_w, "kv": kv_w[None], "q": q_w, "o": o_w,
             "ffn_in": ffn_in_w, "ffn_out": ffn_out_w}
    big_m = {"conv_in": m_conv_in_w, "conv_out": m_conv_out_w, "kv": m_kv_w[None], "q": m_q_w, "o": m_o_w,
             "ffn_in": m_ffn_in_w, "ffn_out": m_ffn_out_w}
    big_v = {"conv_in": v_conv_in_w, "conv_out": v_conv_out_w, "kv": v_kv_w[None], "q": v_q_w, "o": v_o_w,
             "ffn_in": v_ffn_in_w, "ffn_out": v_ffn_out_w}

    n_gain, n_tap = depth * 4, n_a * conv_w.shape[1]
    small_rows = -(-(n_gain + n_tap + 1) // 8) * 8
    pad_rows = small_rows - n_gain - n_tap

    def pack_small(gains, taps):
        return jnp.concatenate([gains.reshape(n_gain, dq), taps.reshape(n_tap, dq), jnp.zeros((pad_rows, dq), F32)])

    axis_of = dict(BIG)

    def matrices_of(l, part):
        if part == "ffn":
            return [("ffn_in", l), ("ffn_out", l)]
        if l < n_a:
            return [("conv_in", l), ("conv_out", l)]
        return ([("kv", 0)] if l == n_a else []) + [("q", l - n_a), ("o", l - n_a)]

    halves = [(l, part) for l in range(depth) for part in ("mix", "ffn")]
    groups = {(l, part): [(*_cast_place(big_w[name][i], axis_of[name], BF16, f"place_{name}_{i}"), axis_of[name])
                          for name, i in matrices_of(l, part)] for l, part in halves}
    groups[halves[0]].append((*_cast_place(pack_small(norm_g, conv_w), 1, F32, "place_small"), 1))
    started = {halves[0]: _gather_start(groups[halves[0]], [], "gather_start_0_mix")[0]}

    def fetch(half, after):
        full = _gather_wait(groups[half], started[half], after, "gather_wait_%d_%s" % half)
        nxt = halves.index(half) + 1
        if nxt < len(halves):
            started[halves[nxt]], full = _gather_start(groups[halves[nxt]], full, "gather_start_%d_%s" % halves[nxt])
        return full

    first = fetch(halves[0], None)
    small = first[-1]
    gains = [[small[4 * l + i][None] for i in range(4)] for l in range(depth)]
    conv_ws = [small[n_gain + 3 * l:n_gain + 3 * l + 3] for l in range(n_a)]
    kv_gain = kv_norm_g[None]

    def weights_of(l, part, after):
        full = first if (l, part) == halves[0] else fetch((l, part), after)
        return {name: full[t] for t, (name, _) in enumerate(matrices_of(l, part))}

    sent = {}

    def send_grads(l, part, grads, carry):
        sent[l, part], carry = _scatter_start(
            [(grads[name], axis_of[name]) for name, _ in matrices_of(l, part)], carry, f"scatter_start_{l}_{part}")
        return carry

    loss, dx, d_gains, d_conv, d_kv_gain = _local_step(
        x[0], loss_target[0], gains, conv_ws, kv_gain, weights_of, send_grads)
    loss = lax.psum(loss, ("x", "y", "c"))

    small_g = jnp.concatenate([dg for row in d_gains for dg in row] + list(d_conv) + [d_kv_gain]
                              + [jnp.zeros((pad_rows - 1, d), F32)])
    small_g = _allreduce_small(small_g, "allreduce_small")
    blk = 2 * lax.axis_index("x") + lax.axis_index("y")
    mine_small = lax.dynamic_slice_in_dim(small_g, blk * dq, dq, axis=1)
    kv_rows = d // dq

    def pack_opt(gains_like, taps_like, kv_like):
        rows = jnp.concatenate([gains_like.reshape(n_gain, dq), taps_like.reshape(n_tap, dq), kv_like.reshape(kv_rows, dq)])
        extra = -rows.shape[0] % 8
        return jnp.concatenate([rows, jnp.zeros((extra, dq), F32)]) if extra else rows

    sw = pack_opt(norm_g, conv_w, kv_norm_g)
    sm = pack_opt(m_norm_g, m_conv_w, m_kv_norm_g)
    sv = pack_opt(v_norm_g, v_conv_w, v_kv_norm_g)
    sg = pack_opt(mine_small[:n_gain], mine_small[n_gain:n_gain + n_tap], small_g[n_gain + n_tap])
    s_out = _adamw(sw, sm, sv, sg, "adamw_small")

    def unpack(a):
        return (a[:n_gain].reshape(depth, 4, dq), a[n_gain:n_gain + n_tap].reshape(n_a, -1, dq),
                a[n_gain + n_tap:n_gain + n_tap + kv_rows].reshape(d))

    small_out = [unpack(a) for a in s_out]

    mine = {}
    for half in reversed(halves):
        axes = [axis_of[name] for name, _ in matrices_of(*half)]
        full, stacks = _scatter_wait(axes, sent[half], "scatter_wait_%d_%s" % half)
        mine.update(zip(matrices_of(*half), zip(full, stacks)))
    paired = _pair_start([[(*mine[name, i][:1], ax, mine[name, i][1]) for i in range(big_w[name].shape[0])]
                          for name, ax in BIG], "pair_start")
    big_out = {}
    for (name, ax), begun in zip(BIG, paired):
        shp = big_w[name].shape
        rows, cols = shp[0] * shp[1], shp[2]
        flat = lambda a: a.reshape(rows, cols)
        full, stacks, sibling = _pair_wait([ax] * shp[0], begun, f"pair_wait_{name}")
        res = _adamw(flat(big_w[name]), flat(big_m[name]), flat(big_v[name]),
                     [(full[i], ax, stacks[i], sibling[i]) for i in range(shp[0])], f"adamw_{name}")
        out_shp = shp[1:] if name == "kv" else shp
        big_out[name] = [a.reshape(out_shp) for a in res]

    def leaves(i):
        ng, cw_, kg = small_out[i]
        return [ng, big_out["conv_in"][i], cw_, big_out["conv_out"][i], kg, big_out["kv"][i], big_out["q"][i],
                big_out["o"][i], big_out["ffn_in"][i], big_out["ffn_out"][i]]

    return (loss, dx[None], *leaves(0), *leaves(1), *leaves(2), *leaves(3))
```

```python
import functools

import jax
import jax.numpy as jnp
import numpy as np
from jax import lax
from jax.experimental import pallas as pl
from jax.experimental.pallas import tpu as pltpu

F32 = jnp.float32
BF16 = jnp.bfloat16
HEAD_DIM = 64
DILATIONS = (1, 4, 16)
NORM_EPS = 1e-6
NEG_BIG = -1e30
VMEM_LIMIT = 48 * 1024 * 1024
ROW_TILE = 256
NORM_TILE = 512
LANE = 128
MESH = pl.DeviceIdType.MESH

ADAM_LR = 0.001
ADAM_B1 = 0.9
ADAM_B2 = 0.999
ADAM_EPS = 1e-08
ADAM_WD = 0.01
ADAM_STEP = 10

TILE_CANDIDATES = (1024, 1408, 768, 512, 384, 256, 128)


def _pick(dim, cands=TILE_CANDIDATES):
    for c in cands:
        if c <= dim and dim % c == 0:
            return c
    return dim


def _params(sem):
    return pltpu.CompilerParams(dimension_semantics=sem, vmem_limit_bytes=VMEM_LIMIT)


def _mm(a, b, mode, out_dtype, name, scale=None):
    a_planes = a.shape[0] if a.ndim == 3 else 1
    b_planes = b.shape[0] if b.ndim == 3 else 1
    if mode == "nn":
        m, k = a.shape[-2], a.shape[-1] * a_planes
        n = b.shape[1]
    elif mode == "nt":
        m, k = a.shape[-2], a.shape[-1] * a_planes
        n = b.shape[0]
    else:
        k, m = a.shape
        n = b.shape[-1] * b_planes
    tm, tn, tk = _pick(m), _pick(n // b_planes), _pick(k // a_planes, ((2048,) if mode == "tn" else ()) + TILE_CANDIDATES)
    nk = k // tk
    ka, nb = k // a_planes // tk, n // b_planes // tn
    if a_planes > 1:
        a_spec = pl.BlockSpec((None, tm, tk), lambda i, j, kk: (kk // ka, i, kk % ka))
    elif mode == "tn":
        a_spec = pl.BlockSpec((tk, tm), lambda i, j, kk: (kk, i))
    else:
        a_spec = pl.BlockSpec((tm, tk), lambda i, j, kk: (i, kk))
    if mode == "nn":
        b_spec = pl.BlockSpec((tk, tn), lambda i, j, kk: (kk, j))
        dims = (((1,), (0,)), ((), ()))
    elif mode == "nt":
        b_spec = pl.BlockSpec((tn, tk), lambda i, j, kk: (j, kk))
        dims = (((1,), (1,)), ((), ()))
    else:
        b_spec = (pl.BlockSpec((None, tk, tn), lambda i, j, kk: (j // nb, kk, j % nb)) if b_planes > 1
                  else pl.BlockSpec((tk, tn), lambda i, j, kk: (kk, j)))
        dims = (((0,), (0,)), ((), ()))

    def finish(acc):
        if scale is not None:
            acc = acc * scale
        return acc.astype(out_dtype)

    if nk == 1:
        def body(a_ref, b_ref, o_ref):
            o_ref[...] = finish(lax.dot_general(a_ref[...].astype(BF16), b_ref[...].astype(BF16), dims, preferred_element_type=F32))
        scratch = []
    else:
        def body(a_ref, b_ref, o_ref, acc_ref):
            kk = pl.program_id(2)

            @pl.when(kk == 0)
            def _():
                acc_ref[...] = jnp.zeros_like(acc_ref)

            acc_ref[...] += lax.dot_general(a_ref[...].astype(BF16), b_ref[...].astype(BF16), dims, preferred_element_type=F32)

            @pl.when(kk == nk - 1)
            def _():
                o_ref[...] = finish(acc_ref[...])
        scratch = [pltpu.VMEM((tm, tn), F32)]

    return pl.pallas_call(
        body, name=name,
        grid=(m // tm, n // tn, nk),
        in_specs=[a_spec, b_spec],
        out_specs=pl.BlockSpec((tm, tn), lambda i, j, kk: (i, j)),
        out_shape=jax.ShapeDtypeStruct((m, n), out_dtype),
        scratch_shapes=scratch,
        compiler_params=_params(("parallel", "parallel", "arbitrary")),
    )(a, b)


def _rstd(v):
    return lax.rsqrt(jnp.mean(v * v, axis=-1, keepdims=True) + NORM_EPS)


def _rms_bwd(dy, v, g, r):
    gy = dy * g
    dv = r * (gy - v * (r * r) * jnp.mean(gy * v, axis=-1, keepdims=True))
    return dv, dy * v * r


def _row_spec(t, width):
    return pl.BlockSpec((t, width), lambda i: (i, 0))


def _gain_spec(width):
    return pl.BlockSpec((1, width), lambda i: (0, 0))


def _norm_res_fwd(x, mix, g_post, pre_gains, name):
    s, d = x.shape
    t = _pick(s, (NORM_TILE, ROW_TILE))
    has_mix = mix is not None
    n_pre = len(pre_gains)

    def body(*refs):
        x_ref = refs[0]
        pos = 1
        x1 = x_ref[...]
        if has_mix:
            mv = refs[1][...].astype(F32)
            x1 = x1 + mv * _rstd(mv) * refs[2][...]
            pos = 3
        gains = refs[pos:pos + n_pre]
        outs = refs[pos + n_pre:]
        if has_mix:
            outs[0][...] = x1
            outs = outs[1:]
        r = _rstd(x1)
        for g_ref, o_ref in zip(gains, outs):
            o_ref[...] = (x1 * r * g_ref[...]).astype(BF16)

    ins = [x] + ([mix, g_post] if has_mix else []) + list(pre_gains)
    in_specs = [_row_spec(t, d)] + ([_row_spec(t, d), _gain_spec(d)] if has_mix else []) + [_gain_spec(d)] * n_pre
    out_shape = ([jax.ShapeDtypeStruct((s, d), F32)] if has_mix else []) + [jax.ShapeDtypeStruct((s, d), BF16)] * n_pre
    out_specs = [_row_spec(t, d)] * len(out_shape)
    res = pl.pallas_call(
        body, name=name, grid=(s // t,), in_specs=in_specs, out_specs=out_specs, out_shape=out_shape,
        compiler_params=_params(("parallel",)),
    )(*ins)
    if has_mix:
        return res[0], list(res[1:])
    return x, list(res)


def _norm_res_loss(x, mix, g_post, target, name):
    s, d = x.shape
    t = _pick(s, (NORM_TILE, ROW_TILE))

    def body(x_ref, m_ref, g_ref, t_ref, dy_ref, loss_ref):
        mv = m_ref[...].astype(F32)
        y = x_ref[...] + mv * _rstd(mv) * g_ref[...]
        err = y - t_ref[...]
        dy_ref[...] = err * (1.0 / d)

        @pl.when(pl.program_id(0) == 0)
        def _():
            loss_ref[...] = jnp.zeros_like(loss_ref)

        loss_ref[...] += jnp.sum(err * err)

    dy, acc = pl.pallas_call(
        body, name=name, grid=(s // t,),
        in_specs=[_row_spec(t, d), _row_spec(t, d), _gain_spec(d), _row_spec(t, d)],
        out_specs=[_row_spec(t, d), pl.BlockSpec((8, LANE), lambda i: (0, 0))],
        out_shape=[jax.ShapeDtypeStruct((s, d), F32), jax.ShapeDtypeStruct((8, LANE), F32)],
        compiler_params=_params(("arbitrary",)),
    )(x, mix, g_post, target)
    return dy, acc[0, 0] * (0.5 / d)


def _norm_bwd(dx_out, branches, x_in, post, name):
    s, d = dx_out.shape
    t = _pick(s, (NORM_TILE, ROW_TILE))
    nb = len(branches)
    has_post = post is not None

    def body(*refs):
        dx_ref = refs[0]
        pos = 1
        dx = dx_ref[...]
        first = pl.program_id(0) == 0
        n_in = 1 + (1 + 2 * nb if nb else 0) + (2 if has_post else 0)
        outs = refs[n_in:]
        opos = 0
        if nb:
            xv = refs[pos][...]
            pos += 1
            r = _rstd(xv)
            dx_o = outs[0]
            opos = 1
            for _ in range(nb):
                dxn = refs[pos][...].astype(F32)
                g = refs[pos + 1][...]
                pos += 2
                dv, dg_rows = _rms_bwd(dxn, xv, g, r)
                dx = dx + dv
                dg_ref = outs[opos]
                opos += 1

                @pl.when(first)
                def _(dg_ref=dg_ref):
                    dg_ref[...] = jnp.zeros_like(dg_ref)

                dg_ref[...] += jnp.sum(dg_rows, axis=0, keepdims=True)
            dx_o[...] = dx
        if has_post:
            mv = refs[pos][...].astype(F32)
            g = refs[pos + 1][...]
            dm, dg_rows = _rms_bwd(dx, mv, g, _rstd(mv))
            outs[opos][...] = dm.astype(BF16)
            dg_ref = outs[opos + 1]

            @pl.when(first)
            def _():
                dg_ref[...] = jnp.zeros_like(dg_ref)

            dg_ref[...] += jnp.sum(dg_rows, axis=0, keepdims=True)

    ins, in_specs = [dx_out], [_row_spec(t, d)]
    out_shape, out_specs = [], []
    if nb:
        ins.append(x_in)
        in_specs.append(_row_spec(t, d))
        out_shape.append(jax.ShapeDtypeStruct((s, d), F32))
        out_specs.append(_row_spec(t, d))
        for dxn, g in branches:
            ins += [dxn, g]
            in_specs += [_row_spec(t, d), _gain_spec(d)]
            out_shape.append(jax.ShapeDtypeStruct((1, d), F32))
            out_specs.append(_gain_spec(d))
    if has_post:
        ins += [post[0], post[1]]
        in_specs += [_row_spec(t, d), _gain_spec(d)]
        out_shape += [jax.ShapeDtypeStruct((s, d), BF16), jax.ShapeDtypeStruct((1, d), F32)]
        out_specs += [_row_spec(t, d), _gain_spec(d)]
    res = pl.pallas_call(
        body, name=name, grid=(s // t,), in_specs=in_specs, out_specs=out_specs, out_shape=out_shape,
        compiler_params=_params(("arbitrary",)),
    )(*ins)
    res = list(res)
    dx_in = res.pop(0) if nb else dx_out
    dgs = [res.pop(0) for _ in range(nb)]
    dm, dg_post = (res[0], res[1]) if has_post else (None, None)
    return dx_in, dgs, dm, dg_post


HALO = 16


def _shift_down(u, prev, k):
    rows = lax.broadcasted_iota(jnp.int32, u.shape, 0)
    out = pltpu.roll(u, k, 0)
    for i in range(k):
        out = jnp.where(rows == i, prev[HALO - k + i:HALO - k + i + 1, :], out)
    return out


def _shift_up(u, nxt, k):
    n = u.shape[0]
    rows = lax.broadcasted_iota(jnp.int32, u.shape, 0)
    out = pltpu.roll(u, n - k, 0)
    for i in range(k):
        out = jnp.where(rows == n - k + i, nxt[i:i + 1, :], out)
    return out


def _conv_gate_fwd(p, cw, name):
    s, d3 = p.shape
    d = d3 // 3
    t = _pick(s, (ROW_TILE,))
    hb = t // HALO

    def body(p_ref, prev_ref, w_ref, z_ref):
        i = pl.program_id(0)
        pv = p_ref[...].astype(F32)
        b, u = pv[:, :d], pv[:, d:2 * d] * pv[:, 2 * d:]
        ph = prev_ref[...].astype(F32)
        up = jnp.where(i > 0, ph[:, d:2 * d] * ph[:, 2 * d:], 0.0)
        w = w_ref[...]
        y = w[0:1, :] * _shift_down(u, up, 2) + w[1:2, :] * _shift_down(u, up, 1) + w[2:3, :] * u
        z_ref[...] = (b * y).astype(BF16)

    return pl.pallas_call(
        body, name=name, grid=(s // t,),
        in_specs=[_row_spec(t, d3),
                  pl.BlockSpec((HALO, d3), lambda i: (jnp.maximum(i * hb - 1, 0), 0)),
                  pl.BlockSpec((3, d), lambda i: (0, 0))],
        out_specs=_row_spec(t, d),
        out_shape=jax.ShapeDtypeStruct((s, d), BF16),
        compiler_params=_params(("parallel",)),
    )(p, p, cw)


def _conv_gate_bwd(p, dz, cw, name):
    s, d3 = p.shape
    d = d3 // 3
    t = _pick(s, (ROW_TILE,))
    hb = t // HALO
    nt = s // t
    last_halo = s // HALO - 1

    def body(p_ref, prev_ref, next_ref, dz_ref, dznext_ref, w_ref, dp_ref, dw_ref):
        i = pl.program_id(0)
        pv = p_ref[...].astype(F32)
        b, c, h = pv[:, :d], pv[:, d:2 * d], pv[:, 2 * d:]
        u = c * h
        ph = prev_ref[...].astype(F32)
        up = jnp.where(i > 0, ph[:, d:2 * d] * ph[:, 2 * d:], 0.0)
        w = w_ref[...]
        u1, u2 = _shift_down(u, up, 1), _shift_down(u, up, 2)
        y = w[0:1, :] * u2 + w[1:2, :] * u1 + w[2:3, :] * u
        dz = dz_ref[...].astype(F32)
        dy = dz * b
        dyn = jnp.where(i < nt - 1, dznext_ref[...].astype(F32) * next_ref[...].astype(F32)[:, :d], 0.0)
        du = w[2:3, :] * dy + w[1:2, :] * _shift_up(dy, dyn, 1) + w[0:1, :] * _shift_up(dy, dyn, 2)
        dp_ref[:, :d] = (dz * y).astype(BF16)
        dp_ref[:, d:2 * d] = (du * h).astype(BF16)
        dp_ref[:, 2 * d:] = (du * c).astype(BF16)

        @pl.when(i == 0)
        def _():
            dw_ref[...] = jnp.zeros_like(dw_ref)

        dw_ref[0:1, :] += jnp.sum(dy * u2, axis=0, keepdims=True)
        dw_ref[1:2, :] += jnp.sum(dy * u1, axis=0, keepdims=True)
        dw_ref[2:3, :] += jnp.sum(dy * u, axis=0, keepdims=True)

    return pl.pallas_call(
        body, name=name, grid=(nt,),
        in_specs=[_row_spec(t, d3),
                  pl.BlockSpec((HALO, d3), lambda i: (jnp.maximum(i * hb - 1, 0), 0)),
                  pl.BlockSpec((HALO, d3), lambda i: (jnp.minimum((i + 1) * hb, last_halo), 0)),
                  _row_spec(t, d),
                  pl.BlockSpec((HALO, d), lambda i: (jnp.minimum((i + 1) * hb, last_halo), 0)),
                  pl.BlockSpec((3, d), lambda i: (0, 0))],
        out_specs=[_row_spec(t, d3), pl.BlockSpec((3, d), lambda i: (0, 0))],
        out_shape=[jax.ShapeDtypeStruct((s, d3), BF16), jax.ShapeDtypeStruct((3, d), F32)],
        compiler_params=_params(("arbitrary",)),
    )(p, p, p, dz, dz, cw)


FFN_ROWS, FFN_COLS = (1024, 512, 256), (256, 128)


def _ffn_in_swiglu(xn, w_in, name):
    s, k = xn.shape
    ff = w_in.shape[1] // 2
    tm, tn = _pick(s, FFN_ROWS), _pick(ff, FFN_COLS)
    nj = ff // tn

    def body(x_ref, wg_ref, wu_ref, f_ref, a_ref):
        xv = x_ref[...]
        gate = jnp.dot(xv, wg_ref[...], preferred_element_type=F32)
        up = jnp.dot(xv, wu_ref[...], preferred_element_type=F32)
        f_ref[0] = gate.astype(BF16)
        f_ref[1] = up.astype(BF16)
        a_ref[...] = (gate * jax.nn.sigmoid(gate) * up).astype(BF16)

    return pl.pallas_call(
        body, name=name, grid=(s // tm, nj),
        in_specs=[pl.BlockSpec((tm, k), lambda i, j: (i, 0)),
                  pl.BlockSpec((k, tn), lambda i, j: (0, j)),
                  pl.BlockSpec((k, tn), lambda i, j: (0, nj + j))],
        out_specs=[pl.BlockSpec((2, tm, tn), lambda i, j: (0, i, j)), pl.BlockSpec((tm, tn), lambda i, j: (i, j))],
        out_shape=[jax.ShapeDtypeStruct((2, s, ff), BF16), jax.ShapeDtypeStruct((s, ff), BF16)],
        compiler_params=_params(("parallel", "parallel")),
    )(xn, w_in, w_in)


def _ffn_out_dx_swiglu(dff, w_out, f, name):
    s, d = dff.shape
    ff = w_out.shape[0]
    tm, tn = _pick(s, FFN_ROWS), _pick(ff, FFN_COLS)

    def body(d_ref, w_ref, f_ref, df_ref):
        da = lax.dot_general(d_ref[...], w_ref[...], (((1,), (1,)), ((), ())), preferred_element_type=F32)
        gate = f_ref[0].astype(F32)
        up = f_ref[1].astype(F32)
        sg = jax.nn.sigmoid(gate)
        silu = gate * sg
        df_ref[0] = (da * up * (sg + silu * (1.0 - sg))).astype(BF16)
        df_ref[1] = (da * silu).astype(BF16)

    planes = pl.BlockSpec((2, tm, tn), lambda i, j: (0, i, j))
    return pl.pallas_call(
        body, name=name, grid=(s // tm, ff // tn),
        in_specs=[pl.BlockSpec((tm, d), lambda i, j: (i, 0)), pl.BlockSpec((tn, d), lambda i, j: (j, 0)), planes],
        out_specs=planes, out_shape=jax.ShapeDtypeStruct((2, s, ff), BF16),
        compiler_params=_params(("parallel", "parallel")),
    )(dff, w_out, f)


SUPER = 2048
RES = 16
PAIR = 128
L = 128
FWD_TOGETHER = 4
BWD_TOGETHER = 4


def _alibi_slopes(n_heads):
    h = np.arange(n_heads, dtype=np.float32) + 1.0
    return np.power(2.0, -8.0 * h / n_heads).astype(np.float32)


def _permute16(x, inverse, name):
    s, d = x.shape
    cw = LANE

    def body(x_ref, o_ref):
        if inverse:
            for m in range(L):
                o_ref[RES * m:RES * (m + 1), :] = x_ref[pl.ds(m, RES, stride=L), :]
        else:
            for r in range(RES):
                o_ref[L * r:L * (r + 1), :] = x_ref[pl.ds(r, L, stride=RES), :]

    spec = pl.BlockSpec((SUPER, cw), lambda i, j: (i, j))
    return pl.pallas_call(
        body, name=name, grid=(s // SUPER, d // cw), in_specs=[spec], out_specs=spec,
        out_shape=jax.ShapeDtypeStruct((s, d), x.dtype),
        compiler_params=_params(("parallel", "parallel")),
    )(x)


def _slope_table(d):
    nh = d // HEAD_DIM
    sl = _alibi_slopes(nh)
    tab = np.repeat(sl, HEAD_DIM).reshape(d // PAIR, 1, PAIR)
    return jnp.asarray(np.broadcast_to(tab, (d // PAIR, 8, PAIR)).copy())


def _geometry(dil):
    nch = RES // dil
    return nch, L // nch


def _band(dil):
    nch, w = _geometry(dil)
    sh = w.bit_length() - 1
    i = lax.broadcasted_iota(jnp.int32, (L, 2 * L), 0)
    j = lax.broadcasted_iota(jnp.int32, (L, 2 * L), 1)

    def pos(t):
        return jnp.bitwise_and(t, w - 1) * nch + jnp.right_shift(t, sh)

    delta = pos(i) + L - (pos(jnp.bitwise_and(j, L - 1)) + jnp.bitwise_and(j, L))
    return (delta * dil).astype(F32), (delta >= 0) & (delta <= L), j < L


def _fill_bias(bias_s, sl_ref):
    for b, dil in enumerate(DILATIONS):
        base, band, prev_half = _band(dil)
        for first in range(2):
            valid = band & jnp.logical_not(prev_half) if first else band
            for h in range(2):
                slope = sl_ref[0:1, HEAD_DIM * h:HEAD_DIM * h + 1]
                bias_s[(2 * b + first) * 2 + h] = jnp.where(valid, -slope * base, NEG_BIG)


def _bias_index(b, sb, n):
    first = jnp.logical_and(sb == 0, n == 0).astype(jnp.int32)
    return (2 * b + first) * 2


def _offsets(dil, res, n):
    nch, w = _geometry(dil)

    def al(v):
        return v if isinstance(v, int) else pl.multiple_of(v, w)

    q_off = [al((a * dil + res) * L + n * w) for a in range(nch)]
    k_off = [al((a * dil + res) * 2 * L + L + n * w) for a in range(nch)]
    kp_off = [al((a * dil + res) * 2 * L + L + n * w - w) for a in range(nch)]
    return q_off, k_off, kp_off, w


def _gather(ref, offs, w):
    parts = [ref[pl.ds(o, w), :] for o in offs]
    return parts[0] if len(parts) == 1 else jnp.concatenate(parts, axis=0)


def _scatter(ref, offs, w, val, add=False):
    for a, o in enumerate(offs):
        piece = val[a * w:(a + 1) * w, :]
        if add:
            ref[pl.ds(o, w), :] += piece
        else:
            ref[pl.ds(o, w), :] = piece


def _fill_key_buffer(buf, prev_ref, cur_ref):
    for r in range(RES):
        buf[2 * L * r:2 * L * r + L, :] = prev_ref[L * r:L * (r + 1), :]
        buf[2 * L * r + L:2 * L * (r + 1), :] = cur_ref[L * r:L * (r + 1), :]


def _two_heads(x, low):
    zero = jnp.zeros_like(x)
    return jnp.concatenate([jnp.where(low, x, zero), jnp.where(low, zero, x)], axis=0)


def _loop_blocks(dil, stages, together):
    together = max(together, dil) if dil < RES else together

    def it(i, c):
        if dil == RES:
            blocks = [(i * together + k, 0) for k in range(together)]
        else:
            blocks = [(res, i * (together // dil) + k) for k in range(together // dil) for res in range(dil)]
        state = [stages[0](res, n) for res, n in blocks]
        for stage in stages[1:]:
            state = [stage(res, n, prev) for (res, n), prev in zip(blocks, state)]
        for writes in state:
            for args in writes:
                _scatter(*args)
        return c

    lax.fori_loop(0, RES // together, it, 0)


NT = (((1,), (1,)), ((), ()))
TN = (((0,), (0,)), ((), ()))


def _attention_fwd(q, kv, name):
    s, d = q.shape
    g_n, ns = d // PAIR, s // SUPER

    def body(sl_ref, q_ref, kc_ref, kp_ref, vc_ref, vp_ref, o_ref, lse_ref, kbuf, vbuf, m_s, l_s, acc_s, bias_s):
        sb = pl.program_id(1)
        _fill_key_buffer(kbuf, kp_ref, kc_ref)
        _fill_key_buffer(vbuf, vp_ref, vc_ref)
        _fill_bias(bias_s, sl_ref)
        low = lax.broadcasted_iota(jnp.int32, (L, PAIR), 1) < HEAD_DIM
        low_k = lax.broadcasted_iota(jnp.int32, (2 * L, PAIR), 1) < HEAD_DIM
        ones_bd = _two_heads(jnp.ones((2 * L, PAIR), BF16), low_k)
        for bi, dil in enumerate(DILATIONS):
            first_branch, last_branch = bi == 0, bi == len(DILATIONS) - 1

            def scores(res, n, dil=dil):
                q_off, k_off, kp_off, w = _offsets(dil, res, n)
                qf = _gather(q_ref, q_off, w).astype(BF16)
                kcat = jnp.concatenate([_gather(kbuf, kp_off, w), _gather(kbuf, k_off, w)], axis=0).astype(BF16)
                return lax.dot_general(qf, _two_heads(kcat, low_k), NT, preferred_element_type=F32)

            def update(res, n, sc, bi=bi, dil=dil, first_branch=first_branch, last_branch=last_branch):
                q_off, k_off, kp_off, w = _offsets(dil, res, n)
                vcat = jnp.concatenate([_gather(vbuf, kp_off, w), _gather(vbuf, k_off, w)], axis=0).astype(BF16)
                v_ones = jnp.concatenate([_two_heads(vcat, low_k), ones_bd], axis=1)
                bias_at = _bias_index(bi, sb, n)
                if not first_branch:
                    m_prev = _gather(m_s, q_off, w)
                ps, m_new = [], []
                for h in range(2):
                    s_h = sc[:, 2 * L * h:2 * L * (h + 1)] + bias_s[bias_at + h]
                    mh = jnp.max(s_h, axis=1, keepdims=True)
                    if not first_branch:
                        mh = jnp.maximum(mh, m_prev[:, HEAD_DIM * h:HEAD_DIM * h + 1])
                    ps.append(jnp.exp(s_h - mh).astype(BF16))
                    m_new.append(mh)
                m_full = jnp.where(low, m_new[0], m_new[1])
                both = jnp.dot(jnp.concatenate(ps, axis=1), v_ones, preferred_element_type=F32)
                acc, l_full = both[:, :PAIR], both[:, PAIR:]
                if not first_branch:
                    alpha = jnp.exp(m_prev - m_full)
                    l_full = _gather(l_s, q_off, w) * alpha + l_full
                    acc = _gather(acc_s, q_off, w) * alpha + acc
                if last_branch:
                    return [(o_ref, q_off, w, acc / l_full, False), (lse_ref, q_off, w, m_full + jnp.log(l_full), False)]
                return [(m_s, q_off, w, m_full, False), (l_s, q_off, w, l_full, False), (acc_s, q_off, w, acc, False)]

            _loop_blocks(dil, [scores, update], FWD_TOGETHER)

    prev = lambda i: jnp.maximum(i - 1, 0)
    blk = pl.BlockSpec((SUPER, PAIR), lambda g, i: (i, g))
    in_specs = [pl.BlockSpec((None, 8, PAIR), lambda g, i: (g, 0, 0)), blk,
                pl.BlockSpec((SUPER, PAIR), lambda g, i: (i, g)),
                pl.BlockSpec((SUPER, PAIR), lambda g, i: (prev(i), g)),
                pl.BlockSpec((SUPER, PAIR), lambda g, i: (i, g_n + g)),
                pl.BlockSpec((SUPER, PAIR), lambda g, i: (prev(i), g_n + g))]
    return pl.pallas_call(
        body, name=name, grid=(g_n, ns), in_specs=in_specs, out_specs=[blk, blk],
        out_shape=[jax.ShapeDtypeStruct((s, d), F32)] * 2,
        scratch_shapes=([pltpu.VMEM((2 * SUPER, PAIR), F32)] * 2 + [pltpu.VMEM((SUPER, PAIR), F32)] * 3
                        + [pltpu.VMEM((4 * len(DILATIONS), L, 2 * L), F32)]),
        compiler_params=_params(("parallel", "parallel")),
    )(_slope_table(d), q, kv, kv, kv, kv)


def _attention_bwd(q, kv, o, do, lse, dkv_in, name):
    s, d = q.shape
    g_n, ns = d // PAIR, s // SUPER
    has_in = dkv_in is not None

    def body(*refs):
        sl_ref, q_ref, do_ref, o_ref, lse_ref, kc_ref, kp_ref, vc_ref, vp_ref = refs[:9]
        pos = 9
        if has_in:
            dkv_in_ref = refs[9]
            pos = 10
        dq_ref, dkv_ref, kbuf, vbuf, dkbuf, dvbuf, dq_s, bias_s = refs[pos:]
        step = pl.program_id(1)
        sb = ns - 1 - step
        _fill_key_buffer(kbuf, kp_ref, kc_ref)
        _fill_key_buffer(vbuf, vp_ref, vc_ref)
        _fill_bias(bias_s, sl_ref)

        @pl.when(step == 0)
        def _():
            dkbuf[...] = jnp.zeros_like(dkbuf)
            dvbuf[...] = jnp.zeros_like(dvbuf)

        @pl.when(step > 0)
        def _():
            for buf in (dkbuf, dvbuf):
                for r in range(RES):
                    buf[2 * L * r + L:2 * L * (r + 1), :] = buf[2 * L * r:2 * L * r + L, :]
                    buf[2 * L * r:2 * L * r + L, :] = jnp.zeros((L, PAIR), F32)

        low = lax.broadcasted_iota(jnp.int32, (L, PAIR), 1) < HEAD_DIM
        low_k = lax.broadcasted_iota(jnp.int32, (2 * L, PAIR), 1) < HEAD_DIM
        low_t = lax.broadcasted_iota(jnp.int32, (PAIR, 2 * L), 0) < HEAD_DIM
        for bi, dil in enumerate(DILATIONS):
            first_branch = bi == 0

            def scores(res, n, dil=dil):
                q_off, k_off, kp_off, w = _offsets(dil, res, n)
                qb = _gather(q_ref, q_off, w).astype(BF16)
                dof = _gather(do_ref, q_off, w)
                prod = dof * _gather(o_ref, q_off, w)
                dob = dof.astype(BF16)
                lse_f = _gather(lse_ref, q_off, w)
                zero = jnp.zeros_like(prod)
                dsum = (jnp.sum(jnp.where(low, prod, zero), axis=1, keepdims=True),
                        jnp.sum(jnp.where(low, zero, prod), axis=1, keepdims=True))
                kcat = jnp.concatenate([_gather(kbuf, kp_off, w), _gather(kbuf, k_off, w)], axis=0).astype(BF16)
                vcat = jnp.concatenate([_gather(vbuf, kp_off, w), _gather(vbuf, k_off, w)], axis=0).astype(BF16)
                k_bd, v_bd = _two_heads(kcat, low_k), _two_heads(vcat, low_k)
                sc = lax.dot_general(qb, k_bd, NT, preferred_element_type=F32)
                dp = lax.dot_general(dob, v_bd, NT, preferred_element_type=F32)
                return qb, dob, lse_f, dsum, k_bd, sc, dp

            def gradients(res, n, given, bi=bi, dil=dil, first_branch=first_branch):
                qb, dob, lse_f, dsum, k_bd, sc, dp = given
                q_off, k_off, kp_off, w = _offsets(dil, res, n)
                bias_at = _bias_index(bi, sb, n)
                ps, dss = [], []
                for h in range(2):
                    cols = slice(2 * L * h, 2 * L * (h + 1))
                    lse_h = lse_f[:, HEAD_DIM * h:HEAD_DIM * h + 1]
                    p_h = jnp.exp(sc[:, cols] + bias_s[bias_at + h] - lse_h)
                    dss.append((p_h * (dp[:, cols] - dsum[h])).astype(BF16))
                    ps.append(p_h.astype(BF16))
                ds_cat, p_cat = jnp.concatenate(dss, axis=1), jnp.concatenate(ps, axis=1)
                dq = jnp.dot(ds_cat, k_bd, preferred_element_type=F32)
                dk_t = lax.dot_general(qb, ds_cat, TN, preferred_element_type=F32)
                dv_t = lax.dot_general(dob, p_cat, TN, preferred_element_type=F32)
                dk = jnp.where(low_t, dk_t[:, :2 * L], dk_t[:, 2 * L:]).T
                dv = jnp.where(low_t, dv_t[:, :2 * L], dv_t[:, 2 * L:]).T
                return [(dq_s, q_off, w, dq, not first_branch),
                        (dkbuf, kp_off, w, dk[:L], True), (dkbuf, k_off, w, dk[L:], True),
                        (dvbuf, kp_off, w, dv[:L], True), (dvbuf, k_off, w, dv[L:], True)]

            _loop_blocks(dil, [scores, gradients], BWD_TOGETHER)

        dq_ref[...] = dq_s[...].astype(BF16)
        for r in range(RES):
            rows, cur = slice(L * r, L * (r + 1)), slice(2 * L * r + L, 2 * L * (r + 1))
            for plane, buf in enumerate((dkbuf, dvbuf)):
                if has_in:
                    dkv_ref[plane, rows, :] = buf[cur, :] + dkv_in_ref[plane, rows, :]
                else:
                    dkv_ref[plane, rows, :] = buf[cur, :]

    rev = lambda i: ns - 1 - i
    prev = lambda i: jnp.maximum(ns - 2 - i, 0)
    blk = pl.BlockSpec((SUPER, PAIR), lambda g, i: (rev(i), g))
    in_specs = [pl.BlockSpec((None, 8, PAIR), lambda g, i: (g, 0, 0)), blk, blk, blk, blk,
                pl.BlockSpec((SUPER, PAIR), lambda g, i: (rev(i), g)),
                pl.BlockSpec((SUPER, PAIR), lambda g, i: (prev(i), g)),
                pl.BlockSpec((SUPER, PAIR), lambda g, i: (rev(i), g_n + g)),
                pl.BlockSpec((SUPER, PAIR), lambda g, i: (prev(i), g_n + g))]
    ins = [_slope_table(d), q, do, o, lse, kv, kv, kv, kv]
    planes = pl.BlockSpec((2, SUPER, PAIR), lambda g, i: (0, rev(i), g))
    if has_in:
        in_specs.append(planes)
        ins.append(dkv_in)
    res = pl.pallas_call(
        body, name=name, grid=(g_n, ns), in_specs=in_specs, out_specs=[blk, planes],
        out_shape=[jax.ShapeDtypeStruct((s, d), BF16), jax.ShapeDtypeStruct((2, s, d), F32)],
        scratch_shapes=([pltpu.VMEM((2 * SUPER, PAIR), F32)] * 4 + [pltpu.VMEM((SUPER, PAIR), F32)]
                        + [pltpu.VMEM((4 * len(DILATIONS), L, 2 * L), F32)]),
        compiler_params=_params(("parallel", "arbitrary")),
    )(*ins)
    return res[0], res[1]


def _coords():
    return lax.axis_index("x"), lax.axis_index("y"), lax.axis_index("c")


def _chip_peers(x, y):
    return [(1 - x, y), (x, 1 - y), (1 - x, 1 - y)]


def _block_of(ref, axis, blk, size):
    start = pl.multiple_of(blk * size, size)
    if axis == 1:
        return ref.at[:, pl.ds(start, size)]
    return ref.at[pl.ds(start, size), :]


ANY = pl.BlockSpec(memory_space=pl.ANY)


HBM = pl.BlockSpec(memory_space=pltpu.HBM)
SEM = pl.BlockSpec(memory_space=pltpu.SEMAPHORE)
SPLIT = pltpu.CompilerParams(has_side_effects=pltpu.SideEffectType.DATAFLOW_SIDE_EFFECTING)


def _in_hbm(a):
    return pltpu.with_memory_space_constraint(a, pltpu.HBM)


def _thru(arrays):
    return [pltpu.HBM(a.shape, a.dtype) for a in arrays]


def _cast_place(w, layer, ax, dtype, name):
    _, k, n = w.shape
    t = _pick(k, (512, 256, 128))
    nb = k // t

    def body(blk_ref, w_ref, b_ref, f_ref):
        v = w_ref[...].astype(dtype)
        b_ref[...] = v
        f_ref[...] = v

    full_shape = (k, 4 * n) if ax == 1 else (4 * k, n)
    place = (lambda i, blk: (i, blk[0])) if ax == 1 else (lambda i, blk: (blk[0] * nb + i, 0))
    return pl.pallas_call(
        body, name=name,
        grid_spec=pltpu.PrefetchScalarGridSpec(
            num_scalar_prefetch=1, grid=(nb,),
            in_specs=[pl.BlockSpec((None, t, n), lambda i, blk: (layer, i, 0))],
            out_specs=[pl.BlockSpec((t, n), lambda i, blk: (i, 0)), pl.BlockSpec((t, n), place)]),
        out_shape=[jax.ShapeDtypeStruct((k, n), dtype), jax.ShapeDtypeStruct(full_shape, dtype)],
        compiler_params=_params(("parallel",)),
    )(_my_block()[None], w)


def _my_block():
    return (2 * lax.axis_index("x") + lax.axis_index("y")).astype(jnp.int32)


def _gather_start(group, carry, name):
    n, nc = len(group), len(carry)

    def body(*refs):
        blocks, fulls, send_sem, recv_sem = refs[:n], refs[n:2 * n], refs[2 * n + nc], refs[2 * n + nc + 1]
        x, y, c = _coords()
        for t, (b, _, ax) in enumerate(group):
            mine = _block_of(fulls[t], ax, 2 * x + y, b.shape[ax])
            for j, (px, py) in enumerate(_chip_peers(x, y)):
                pltpu.make_async_remote_copy(
                    src_ref=blocks[t], dst_ref=mine, send_sem=send_sem.at[3 * t + j], recv_sem=recv_sem.at[3 * t + j],
                    device_id=(px, py, c), device_id_type=MESH).start()

    arrays = [b for b, _, _ in group] + [f for _, f, _ in group] + list(carry)
    sems = [pltpu.SemaphoreType.DMA((3 * n,))] * 2
    res = pl.pallas_call(
        body, name=name, in_specs=[HBM] * len(arrays), out_specs=[SEM, SEM] + [HBM] * len(arrays),
        out_shape=sems + _thru(arrays), input_output_aliases={i: 2 + i for i in range(len(arrays))},
        compiler_params=SPLIT,
    )(*[_in_hbm(a) for a in arrays])
    return (res[0], res[1], list(res[2:2 + n]), list(res[2 + n:2 + 2 * n])), list(res[2 + 2 * n:])


def _gather_wait(group, started, after, name):
    sends, recvs, blocks, fulls = started
    m = len(group)

    def body(*refs):
        blk_refs, full_refs, send_sem, recv_sem = refs[:m], refs[m:2 * m], refs[2 * m], refs[2 * m + 1]
        x, y, c = _coords()
        for t, (b, _, ax) in enumerate(group):
            for j, (px, py) in enumerate(_chip_peers(x, y)):
                cp = pltpu.make_async_remote_copy(
                    src_ref=blk_refs[t], dst_ref=_block_of(full_refs[t], ax, 2 * px + py, b.shape[ax]),
                    send_sem=send_sem.at[3 * t + j], recv_sem=recv_sem.at[3 * t + j],
                    device_id=(px, py, c), device_id_type=MESH)
                cp.wait_send()
                cp.wait_recv()

    extra = [] if after is None else [after]
    res = pl.pallas_call(
        body, name=name, in_specs=[HBM] * (2 * m) + [SEM, SEM] + [ANY] * len(extra), out_specs=[HBM] * (2 * m),
        out_shape=_thru(blocks) + _thru(fulls), input_output_aliases={i: i for i in range(2 * m)},
        compiler_params=SPLIT,
    )(*blocks, *fulls, sends, recvs, *extra)
    return list(res[m:])


def _scatter_start(grads, carry, name):
    n = len(grads)
    n_in = 2 * n + len(carry)

    def body(*refs):
        g_refs, st_refs, send_sem, recv_sem = refs[:n], refs[n:2 * n], refs[n_in], refs[n_in + 1]
        x, y, c = _coords()
        for t, (g, ax) in enumerate(grads):
            for j, (px, py) in enumerate(_chip_peers(x, y)):
                pltpu.make_async_remote_copy(
                    src_ref=_block_of(g_refs[t], ax, 2 * px + py, g.shape[ax] // 4), dst_ref=st_refs[t].at[j],
                    send_sem=send_sem.at[3 * t + j], recv_sem=recv_sem.at[3 * t + j],
                    device_id=(px, py, c), device_id_type=MESH).start()

    arrays = [g for g, _ in grads]
    for g, ax in grads:
        shape = list(g.shape)
        shape[ax] //= 4
        arrays.append(lax.empty((3, *shape), g.dtype))
    arrays += list(carry)
    sems = [pltpu.SemaphoreType.DMA((3 * n,))] * 2
    res = pl.pallas_call(
        body, name=name, in_specs=[HBM] * n_in, out_specs=[SEM, SEM] + [HBM] * n_in,
        out_shape=sems + _thru(arrays), input_output_aliases={i: 2 + i for i in range(n_in)},
        compiler_params=SPLIT,
    )(*[_in_hbm(a) for a in arrays])
    return (res[0], res[1], list(res[2:2 + n]), list(res[2 + n:2 + 2 * n])), list(res[2 + 2 * n:])


def _scatter_wait(axes, started, after, name):
    sends, recvs, full, stacks = started
    n = len(full)
    extra = [] if after is None else [after]

    def body(*refs):
        g_refs, st_refs, send_sem, recv_sem = refs[:n], refs[n:2 * n], refs[2 * n], refs[2 * n + 1]
        x, y, c = _coords()
        for t, ax in enumerate(axes):
            size = full[t].shape[ax] // 4
            for j, (px, py) in enumerate(_chip_peers(x, y)):
                cp = pltpu.make_async_remote_copy(
                    src_ref=_block_of(g_refs[t], ax, 2 * px + py, size), dst_ref=st_refs[t].at[j],
                    send_sem=send_sem.at[3 * t + j], recv_sem=recv_sem.at[3 * t + j],
                    device_id=(px, py, c), device_id_type=MESH)
                cp.wait_send()
                cp.wait_recv()

    res = pl.pallas_call(
        body, name=name, in_specs=[HBM] * (2 * n) + [SEM, SEM] + [ANY] * len(extra), out_specs=[HBM] * (2 * n),
        out_shape=_thru(full) + _thru(stacks), input_output_aliases={i: i for i in range(2 * n)},
        compiler_params=SPLIT,
    )(*full, *stacks, sends, recvs, *extra)
    return list(res[:n]), list(res[n:])


def _pair_copies(g_refs, st_refs, out_refs, items, send_sem, recv_sem):
    x, y, c = _coords()
    copies = []
    for u, (g, ax, _) in enumerate(items):
        own = _block_of(g_refs[u], ax, 2 * x + y, g.shape[ax] // 4)
        for k, (src, dst) in enumerate([(own, out_refs[u].at[0]), (st_refs[u], out_refs[u].at[pl.ds(1, 3)])]):
            copies.append(pltpu.make_async_remote_copy(
                src_ref=src, dst_ref=dst, send_sem=send_sem.at[2 * u + k], recv_sem=recv_sem.at[2 * u + k],
                device_id=(x, y, 1 - c), device_id_type=MESH))
    return copies


def _pair_start(tensors, name):
    flat = [it for ts in tensors for it in ts]
    n, nt = len(flat), len(tensors)
    first = [sum(len(ts) for ts in tensors[:i]) for i in range(nt + 1)]

    def body(*refs):
        sends, recvs = refs[3 * n:3 * n + nt], refs[3 * n + nt:3 * n + 2 * nt]
        for i, ts in enumerate(tensors):
            rng = slice(first[i], first[i + 1])
            for cp in _pair_copies(refs[:n][rng], refs[n:2 * n][rng], refs[2 * n:3 * n][rng], ts, sends[i], recvs[i]):
                cp.start()

    arrays = ([g for g, _, _ in flat] + [st for _, _, st in flat]
              + [lax.empty((4, *st.shape[1:]), st.dtype) for _, _, st in flat])
    sems = [pltpu.SemaphoreType.DMA((2 * len(ts),)) for ts in tensors]
    res = pl.pallas_call(
        body, name=name, in_specs=[HBM] * (3 * n), out_specs=[SEM] * (2 * nt) + [HBM] * (3 * n),
        out_shape=sems + sems + _thru(arrays), input_output_aliases={i: 2 * nt + i for i in range(3 * n)},
        compiler_params=SPLIT,
    )(*[_in_hbm(a) for a in arrays])
    thru = res[2 * nt:]
    return [(res[i], res[nt + i], *(list(thru[k * n + first[i]:k * n + first[i + 1]]) for k in range(3)))
            for i in range(nt)]


def _pair_wait(axes, started, name):
    send, recv, full, stacks, landing = started
    n = len(full)
    items = [(full[u], axes[u], stacks[u]) for u in range(n)]

    def body(*refs):
        for cp in _pair_copies(refs[:n], refs[n:2 * n], refs[2 * n:3 * n], items, refs[3 * n], refs[3 * n + 1]):
            cp.wait_send()
            cp.wait_recv()

    res = pl.pallas_call(
        body, name=name, in_specs=[HBM] * (3 * n) + [SEM, SEM], out_specs=[HBM] * (3 * n),
        out_shape=_thru(full + stacks + landing), input_output_aliases={i: i for i in range(3 * n)},
        compiler_params=SPLIT,
    )(*full, *stacks, *landing, send, recv)
    return list(res[:n]), list(res[n:2 * n]), list(res[2 * n:])


def _allreduce_small(v, name):
    r, cdim = v.shape

    def body(v_ref, out_ref, buf, send_sems, recv_sems):
        x, y, c = _coords()
        me = 4 * x + 2 * y + c
        buf[0] = v_ref[...]
        sends = []
        for k in range(1, 8):
            peer = (x if not (k & 4) else 1 - x, y if not (k & 2) else 1 - y, c if not (k & 1) else 1 - c)
            cp = pltpu.make_async_remote_copy(
                src_ref=v_ref, dst_ref=buf.at[k], send_sem=send_sems.at[k - 1], recv_sem=recv_sems.at[k - 1],
                device_id=peer, device_id_type=MESH)
            cp.start()
            sends.append(cp)
        for cp in sends:
            cp.wait_recv()
        total = buf[me]
        for src in range(1, 8):
            total = total + buf[jnp.bitwise_xor(me, src)]
        out_ref[...] = total
        for cp in sends:
            cp.wait_send()

    return pl.pallas_call(
        body, name=name,
        in_specs=[pl.BlockSpec(memory_space=pltpu.VMEM)], out_specs=pl.BlockSpec(memory_space=pltpu.VMEM),
        out_shape=jax.ShapeDtypeStruct((r, cdim), F32),
        scratch_shapes=[pltpu.VMEM((8, r, cdim), F32), pltpu.SemaphoreType.DMA((7,)), pltpu.SemaphoreType.DMA((7,))],
        compiler_params=pltpu.CompilerParams(has_side_effects=True),
    )(v)


def _adamw_math(w, g, m, v):
    m = ADAM_B1 * m + (1.0 - ADAM_B1) * g
    v = ADAM_B2 * v + (1.0 - ADAM_B2) * jnp.square(g)
    m_hat = m / (1.0 - ADAM_B1 ** ADAM_STEP)
    v_hat = v / (1.0 - ADAM_B2 ** ADAM_STEP)
    delta = -ADAM_LR * (m_hat / (jnp.sqrt(v_hat) + ADAM_EPS) + ADAM_WD * w)
    return delta, m, v


def _adamw(w, m, v, grads, name):
    r, cdim = w.shape
    paired = isinstance(grads, list)
    layers = len(grads) if paired else 1
    t = _pick(r // layers, (128, 64, 32, 16, 8))
    per_layer = r // layers // t
    n_grad = 3 * layers if paired else 1

    def body(*refs):
        refs = refs[1:] if paired else refs
        w_ref, m_ref, v_ref = refs[:3]
        outs = refs[3 + n_grad:]

        def update(g):
            delta, m_new, v_new = _adamw_math(w_ref[...], g, m_ref[...], v_ref[...])
            outs[0][...] = g
            outs[1][...] = delta
            outs[2][...] = m_new
            outs[3][...] = v_new

        if not paired:
            update(refs[3][...])
            return
        layer = pl.program_id(0) // per_layer
        for l in range(layers):
            @pl.when(layer == l)
            def _(own_ref=refs[3 + 3 * l], st_ref=refs[4 + 3 * l], sib_ref=refs[5 + 3 * l]):
                sa = own_ref[...].astype(F32)
                sb = sib_ref[0].astype(F32)
                for k in range(3):
                    sa = sa + st_ref[k].astype(F32)
                    sb = sb + sib_ref[k + 1].astype(F32)
                update(sa + sb)

    out_shape = [jax.ShapeDtypeStruct((r, cdim), F32)] * 4
    if not paired:
        spec = pl.BlockSpec((t, cdim), lambda i: (i, 0))
        return pl.pallas_call(
            body, name=name, grid=(r // t,), in_specs=[spec] * 4, out_specs=[spec] * 4, out_shape=out_shape,
            compiler_params=_params(("parallel",)),
        )(w, m, v, grads)

    spec = pl.BlockSpec((t, cdim), lambda i, blk: (i, 0))
    ins, in_specs = [w, m, v], [spec] * 3
    for l, (g, ax, stack, sib) in enumerate(grads):
        row = lambda i, l=l: jnp.clip(i - l * per_layer, 0, per_layer - 1)
        own = ((lambda i, blk, row=row: (row(i), blk[0])) if ax == 1
               else (lambda i, blk, row=row: (blk[0] * per_layer + row(i), 0)))
        ins += [g, stack, sib]
        in_specs += [pl.BlockSpec((t, cdim), own),
                     pl.BlockSpec((3, t, cdim), lambda i, blk, row=row: (0, row(i), 0)),
                     pl.BlockSpec((4, t, cdim), lambda i, blk, row=row: (0, row(i), 0))]
    return pl.pallas_call(
        body, name=name,
        grid_spec=pltpu.PrefetchScalarGridSpec(
            num_scalar_prefetch=1, grid=(r // t,), in_specs=in_specs, out_specs=[spec] * 4),
        out_shape=out_shape, compiler_params=_params(("parallel",)),
    )(_my_block()[None], *ins)


def _local_step(x, target, gains, conv_ws, kv_gain, weights_of, send_grads):
    depth = len(gains)
    n_a = len(conv_ws)
    saved, ws = [], []
    kv = kvn = None
    _, (xn,) = _norm_res_fwd(x, None, None, [gains[0][0]], "norm_first")
    h = x
    for l in range(depth):
        g = gains[l]
        sv = {"x_in": h, "xn": xn}
        w = weights_of(l, "mix", h)
        ws.append(w)
        if l == n_a:
            kv = _mm(kvn, w["kv"], "nn", F32, "kv_fwd")
        if l < n_a:
            p = _mm(xn, w["conv_in"], "nn", BF16, f"conv_in_fwd_{l}")
            z = _conv_gate_fwd(p, conv_ws[l], f"conv_gate_fwd_{l}")
            mix = _mm(z, w["conv_out"], "nn", BF16, f"conv_out_fwd_{l}")
            sv.update(p=p, z=z)
        else:
            j = l - n_a
            q = _mm(xn, w["q"], "nn", F32, f"q_fwd_{j}", scale=HEAD_DIM ** -0.5)
            o, lse = _attention_fwd(q, kv, f"attn_fwd_{j}")
            mix = _mm(o, w["o"], "nn", BF16, f"o_fwd_{j}")
            sv.update(q=q, o=o, lse=lse)
        x1, (xn2,) = _norm_res_fwd(h, mix, g[1], [g[2]], f"norm_mid_{l}")
        w.update(weights_of(l, "ffn", mix))
        f, a = _ffn_in_swiglu(xn2, w["ffn_in"], f"ffn_in_fwd_{l}")
        ff = _mm(a, w["ffn_out"], "nn", BF16, f"ffn_out_fwd_{l}")
        sv.update(mix=mix, x1=x1, xn2=xn2, f=f, a=a, ff=ff)
        saved.append(sv)
        if l == depth - 1:
            dx, loss = _norm_res_loss(x1, ff, g[3], target, "norm_loss")
        else:
            if l == n_a - 1:
                h, _ = _norm_res_fwd(x1, ff, g[3], [], f"norm_end_{l}")
                h = _permute16(h, False, "permute_stream")
                target = _permute16(target, False, "permute_target")
                _, (xn, kvn) = _norm_res_fwd(h, None, None, [gains[l + 1][0], kv_gain], "norm_permuted")
            else:
                h, (xn,) = _norm_res_fwd(x1, ff, g[3], [gains[l + 1][0]], f"norm_end_{l}")
    d_gains = [[None] * 4 for _ in range(depth)]
    d_conv = [None] * n_a
    d_kv_gain = None
    dkv = None
    _, _, dff, d_gains[depth - 1][3] = _norm_bwd(dx, [], None, (saved[-1]["ff"], gains[-1][3]), "norm_loss_bwd")
    for l in reversed(range(depth)):
        sv, g, w, grads = saved[l], gains[l], ws[l], {}
        grads["ffn_out"] =_mm(sv["a"], dff, "tn", BF16, f"ffn_out_dw_{l}")
        df = _ffn_out_dx_swiglu(dff, w["ffn_out"], sv["f"], f"ffn_out_dx_{l}")
        dxn2 = _mm(df, w["ffn_in"], "nt", BF16, f"ffn_in_dx_{l}")
        grads["ffn_in"] =_mm(sv["xn2"], df, "tn", BF16, f"ffn_in_dw_{l}")
        dx, (d_gains[l][2],), dmix, d_gains[l][1] = _norm_bwd(
            dx, [(dxn2, g[2])], sv["x1"], (sv["mix"], g[1]), f"norm_mid_bwd_{l}")
        dx, dmix = send_grads(l, "ffn", grads, [dx, dmix])
        if l < n_a:
            dz = _mm(dmix, w["conv_out"], "nt", BF16, f"conv_out_dx_{l}")
            grads["conv_out"] =_mm(sv["z"], dmix, "tn", BF16, f"conv_out_dw_{l}")
            dp, d_conv[l] = _conv_gate_bwd(sv["p"], dz, conv_ws[l], f"conv_gate_bwd_{l}")
            dxn = _mm(dp, w["conv_in"], "nt", BF16, f"conv_in_dx_{l}")
            grads["conv_in"] =_mm(sv["xn"], dp, "tn", BF16, f"conv_in_dw_{l}")
        else:
            j = l - n_a
            do = _mm(dmix, w["o"], "nt", F32, f"o_dx_{j}")
            grads["o"] =_mm(sv["o"], dmix, "tn", BF16, f"o_dw_{j}")
            dq, dkv = _attention_bwd(sv["q"], kv, sv["o"], do, sv["lse"], dkv, f"attn_bwd_{j}")
            scale = HEAD_DIM ** -0.5
            dxn = _mm(dq, w["q"], "nt", BF16, f"q_dx_{j}", scale=scale)
            grads["q"] =_mm(sv["xn"], dq, "tn", BF16, f"q_dw_{j}", scale=scale)
        branches = [(dxn, g[0])]
        if l == n_a:
            dkvn = _mm(dkv, w["kv"], "nt", BF16, "kv_dx")
            grads["kv"] =_mm(kvn, dkv, "tn", BF16, "kv_dw")
            branches.append((dkvn, kv_gain))
        post = (saved[l - 1]["ff"], gains[l - 1][3]) if l > 0 else None
        if l == n_a:
            dx, dgs, _, _ = _norm_bwd(dx, branches, sv["x_in"], None, f"norm_end_bwd_{l}")
            dx = _permute16(dx, True, "unpermute_stream")
            _, _, dff, dg_post = _norm_bwd(dx, [], None, post, "norm_boundary_bwd")
        else:
            dx, dgs, dff, dg_post = _norm_bwd(dx, branches, sv["x_in"], post, f"norm_end_bwd_{l}")
        if dff is None:
            send_grads(l, "mix", grads, [])
        else:
            dx, dff = send_grads(l, "mix", grads, [dx, dff])
        d_gains[l][0] = dgs[0]
        if l == n_a:
            d_kv_gain = dgs[1]
        if l > 0:
            d_gains[l - 1][3] = dg_post
    return loss, dx, d_gains, d_conv, d_kv_gain


BIG = (
    ("conv_in", 1), ("conv_out", 0), ("kv", 1), ("q", 0), ("o", 0), ("ffn_in", 1), ("ffn_out", 0))


def kernel(x, norm_g, conv_in_w, conv_w, conv_out_w, kv_norm_g, kv_w, q_w, o_w, ffn_in_w, ffn_out_w, loss_target, m_norm_g, m_conv_in_w, m_conv_w, m_conv_out_w, m_kv_norm_g, m_kv_w, m_q_w, m_o_w, m_ffn_in_w, m_ffn_out_w, v_norm_g, v_conv_in_w, v_conv_w, v_conv_out_w, v_kv_norm_g, v_kv_w, v_q_w, v_o_w, v_ffn_in_w, v_ffn_out_w):
    depth, _, dq = norm_g.shape
    d = 4 * dq
    n_a = conv_w.shape[0]
    big_w = {"conv_in": conv_in_w, "conv_out": conv_out_w, "kv": kv_w[None], "q": q_w, "o": o_w,
             "ffn_in": ffn_in_w, "ffn_out": ffn_out_w}
    big_m = {"conv_in": m_conv_in_w, "conv_out": m_conv_out_w, "kv": m_kv_w[None], "q": m_q_w, "o": m_o_w,
             "ffn_in": m_ffn_in_w, "ffn_out": m_ffn_out_w}
    big_v = {"conv_in": v_conv_in_w, "conv_out": v_conv_out_w, "kv": v_kv_w[None], "q": v_q_w, "o": v_o_w,
             "ffn_in": v_ffn_in_w, "ffn_out": v_ffn_out_w}

    n_gain, n_tap = depth * 4, n_a * conv_w.shape[1]
    small_rows = -(-(n_gain + n_tap + 1) // 8) * 8
    pad_rows = small_rows - n_gain - n_tap

    def pack_small(gains, taps):
        return jnp.concatenate([gains.reshape(n_gain, dq), taps.reshape(n_tap, dq), jnp.zeros((pad_rows, dq), F32)])

    axis_of = dict(BIG)

    def matrices_of(l, part):
        if part == "ffn":
            return [("ffn_in", l), ("ffn_out", l)]
        if l < n_a:
            return [("conv_in", l), ("conv_out", l)]
        return ([("kv", 0)] if l == n_a else []) + [("q", l - n_a), ("o", l - n_a)]

    halves = [(l, part) for l in range(depth) for part in ("mix", "ffn")]
    groups = {(l, part): [(*_cast_place(big_w[name], i, axis_of[name], BF16, f"place_{name}_{i}"), axis_of[name])
                          for name, i in matrices_of(l, part)] for l, part in halves}
    groups[halves[0]].append((*_cast_place(pack_small(norm_g, conv_w)[None], 0, 1, F32, "place_small"), 1))
    started = {halves[0]: _gather_start(groups[halves[0]], [], "gather_start_0_mix")[0]}

    def fetch(half, after):
        full = _gather_wait(groups[half], started[half], after, "gather_wait_%d_%s" % half)
        nxt = halves.index(half) + 1
        if nxt < len(halves):
            started[halves[nxt]], full = _gather_start(groups[halves[nxt]], full, "gather_start_%d_%s" % halves[nxt])
        return full

    first = fetch(halves[0], None)
    small = first[-1]
    gains = [[small[4 * l + i][None] for i in range(4)] for l in range(depth)]
    conv_ws = [small[n_gain + 3 * l:n_gain + 3 * l + 3] for l in range(n_a)]
    kv_gain = kv_norm_g[None]

    def weights_of(l, part, after):
        full = first if (l, part) == halves[0] else fetch((l, part), after)
        return {name: full[t] for t, (name, _) in enumerate(matrices_of(l, part))}

    sent = {}

    def send_grads(l, part, grads, carry):
        sent[l, part], carry = _scatter_start(
            [(grads[name], axis_of[name]) for name, _ in matrices_of(l, part)], carry, f"scatter_start_{l}_{part}")
        return carry

    loss, dx, d_gains, d_conv, d_kv_gain = _local_step(
        x[0], loss_target[0], gains, conv_ws, kv_gain, weights_of, send_grads)
    loss = lax.psum(loss, ("x", "y", "c"))

    small_g = jnp.concatenate([dg for row in d_gains for dg in row] + list(d_conv) + [d_kv_gain]
                              + [jnp.zeros((pad_rows - 1, d), F32)])
    small_g = _allreduce_small(small_g, "allreduce_small")
    blk = 2 * lax.axis_index("x") + lax.axis_index("y")
    mine_small = lax.dynamic_slice_in_dim(small_g, blk * dq, dq, axis=1)
    kv_rows = d // dq

    def pack_opt(gains_like, taps_like, kv_like):
        rows = jnp.concatenate([gains_like.reshape(n_gain, dq), taps_like.reshape(n_tap, dq), kv_like.reshape(kv_rows, dq)])
        extra = -rows.shape[0] % 8
        return jnp.concatenate([rows, jnp.zeros((extra, dq), F32)]) if extra else rows

    sw = pack_opt(norm_g, conv_w, kv_norm_g)
    sm = pack_opt(m_norm_g, m_conv_w, m_kv_norm_g)
    sv = pack_opt(v_norm_g, v_conv_w, v_kv_norm_g)
    sg = pack_opt(mine_small[:n_gain], mine_small[n_gain:n_gain + n_tap], small_g[n_gain + n_tap])
    s_out = _adamw(sw, sm, sv, sg, "adamw_small")

    def unpack(a):
        return (a[:n_gain].reshape(depth, 4, dq), a[n_gain:n_gain + n_tap].reshape(n_a, -1, dq),
                a[n_gain + n_tap:n_gain + n_tap + kv_rows].reshape(d))

    small_out = [unpack(a) for a in s_out]

    mine, big_out = {}, {}

    def collect(half, after):
        axes = [axis_of[name] for name, _ in matrices_of(*half)]
        full, stacks = _scatter_wait(axes, sent[half], after, "scatter_wait_%d_%s" % half)
        mine.update(zip(matrices_of(*half), zip(full, stacks)))

    def update(names, tag):
        paired = _pair_start([[(mine[name, i][0], axis_of[name], mine[name, i][1]) for i in range(big_w[name].shape[0])]
                              for name in names], f"pair_start_{tag}")
        for name, begun in zip(names, paired):
            shp, ax = big_w[name].shape, axis_of[name]
            rows, cols = shp[0] * shp[1], shp[2]
            flat = lambda a: a.reshape(rows, cols)
            full, stacks, sibling = _pair_wait([ax] * shp[0], begun, f"pair_wait_{name}")
            res = _adamw(flat(big_w[name]), flat(big_m[name]), flat(big_v[name]),
                         [(full[i], ax, stacks[i], sibling[i]) for i in range(shp[0])], f"adamw_{name}")
            big_out[name] = [a.reshape(shp[1:] if name == "kv" else shp) for a in res]

    last = halves[0]
    late = [name for name, _ in BIG if any(n == name for n, _ in matrices_of(*last))]
    for half in reversed(halves[1:]):
        collect(half, None)
    update([name for name, _ in BIG if name not in late], "early")
    collect(last, big_out["ffn_out"][0])
    update(late, "late")

    def leaves(i):
        ng, cw_, kg = small_out[i]
        return [ng, big_out["conv_in"][i], cw_, big_out["conv_out"][i], kg, big_out["kv"][i], big_out["q"][i],
                big_out["o"][i], big_out["ffn_in"][i], big_out["ffn_out"][i]]

    return (loss, dx[None], *leaves(0), *leaves(1), *leaves(2), *leaves(3))
```

```python
import functools

import jax
import jax.numpy as jnp
import numpy as np
from jax import lax
from jax.experimental import pallas as pl
from jax.experimental.pallas import tpu as pltpu

F32 = jnp.float32
BF16 = jnp.bfloat16
HEAD_DIM = 64
DILATIONS = (1, 4, 16)
NORM_EPS = 1e-6
NEG_BIG = -1e30
VMEM_LIMIT = 48 * 1024 * 1024
ROW_TILE = 256
NORM_TILE = 512
LANE = 128
MESH = pl.DeviceIdType.MESH

ADAM_LR = 0.001
ADAM_B1 = 0.9
ADAM_B2 = 0.999
ADAM_EPS = 1e-08
ADAM_WD = 0.01
ADAM_STEP = 10

TILE_CANDIDATES = (1024, 1408, 768, 512, 384, 256, 128)


def _pick(dim, cands=TILE_CANDIDATES):
    for c in cands:
        if c <= dim and dim % c == 0:
            return c
    return dim


def _params(sem):
    return pltpu.CompilerParams(dimension_semantics=sem, vmem_limit_bytes=VMEM_LIMIT)


def _mm(a, b, mode, out_dtype, name, scale=None):
    a_planes = a.shape[0] if a.ndim == 3 else 1
    b_planes = b.shape[0] if b.ndim == 3 else 1
    if mode == "nn":
        m, k = a.shape[-2], a.shape[-1] * a_planes
        n = b.shape[1]
    elif mode == "nt":
        m, k = a.shape[-2], a.shape[-1] * a_planes
        n = b.shape[0]
    else:
        k, m = a.shape
        n = b.shape[-1] * b_planes
    tm, tn, tk = _pick(m), _pick(n // b_planes), _pick(k // a_planes, ((2048,) if mode == "tn" else ()) + TILE_CANDIDATES)
    nk = k // tk
    ka, nb = k // a_planes // tk, n // b_planes // tn
    if a_planes > 1:
        a_spec = pl.BlockSpec((None, tm, tk), lambda i, j, kk: (kk // ka, i, kk % ka))
    elif mode == "tn":
        a_spec = pl.BlockSpec((tk, tm), lambda i, j, kk: (kk, i))
    else:
        a_spec = pl.BlockSpec((tm, tk), lambda i, j, kk: (i, kk))
    if mode == "nn":
        b_spec = pl.BlockSpec((tk, tn), lambda i, j, kk: (kk, j))
        dims = (((1,), (0,)), ((), ()))
    elif mode == "nt":
        b_spec = pl.BlockSpec((tn, tk), lambda i, j, kk: (j, kk))
        dims = (((1,), (1,)), ((), ()))
    else:
        b_spec = (pl.BlockSpec((None, tk, tn), lambda i, j, kk: (j // nb, kk, j % nb)) if b_planes > 1
                  else pl.BlockSpec((tk, tn), lambda i, j, kk: (kk, j)))
        dims = (((0,), (0,)), ((), ()))

    def finish(acc):
        if scale is not None:
            acc = acc * scale
        return acc.astype(out_dtype)

    if nk == 1:
        def body(a_ref, b_ref, o_ref):
            o_ref[...] = finish(lax.dot_general(a_ref[...].astype(BF16), b_ref[...].astype(BF16), dims, preferred_element_type=F32))
        scratch = []
    else:
        def body(a_ref, b_ref, o_ref, acc_ref):
            kk = pl.program_id(2)

            @pl.when(kk == 0)
            def _():
                acc_ref[...] = jnp.zeros_like(acc_ref)

            acc_ref[...] += lax.dot_general(a_ref[...].astype(BF16), b_ref[...].astype(BF16), dims, preferred_element_type=F32)

            @pl.when(kk == nk - 1)
            def _():
                o_ref[...] = finish(acc_ref[...])
        scratch = [pltpu.VMEM((tm, tn), F32)]

    return pl.pallas_call(
        body, name=name,
        grid=(m // tm, n // tn, nk),
        in_specs=[a_spec, b_spec],
        out_specs=pl.BlockSpec((tm, tn), lambda i, j, kk: (i, j)),
        out_shape=jax.ShapeDtypeStruct((m, n), out_dtype),
        scratch_shapes=scratch,
        compiler_params=_params(("parallel", "parallel", "arbitrary")),
    )(a, b)


def _rstd(v):
    return lax.rsqrt(jnp.mean(v * v, axis=-1, keepdims=True) + NORM_EPS)


def _rms_bwd(dy, v, g, r):
    gy = dy * g
    dv = r * (gy - v * (r * r) * jnp.mean(gy * v, axis=-1, keepdims=True))
    return dv, dy * v * r


def _row_spec(t, width):
    return pl.BlockSpec((t, width), lambda i: (i, 0))


def _gain_spec(width):
    return pl.BlockSpec((1, width), lambda i: (0, 0))


def _norm_res_fwd(x, mix, g_post, pre_gains, name):
    s, d = x.shape
    t = _pick(s, (NORM_TILE, ROW_TILE))
    has_mix = mix is not None
    n_pre = len(pre_gains)

    def body(*refs):
        x_ref = refs[0]
        pos = 1
        x1 = x_ref[...]
        if has_mix:
            mv = refs[1][...].astype(F32)
            x1 = x1 + mv * _rstd(mv) * refs[2][...]
            pos = 3
        gains = refs[pos:pos + n_pre]
        outs = refs[pos + n_pre:]
        if has_mix:
            outs[0][...] = x1
            outs = outs[1:]
        r = _rstd(x1)
        for g_ref, o_ref in zip(gains, outs):
            o_ref[...] = (x1 * r * g_ref[...]).astype(BF16)

    ins = [x] + ([mix, g_post] if has_mix else []) + list(pre_gains)
    in_specs = [_row_spec(t, d)] + ([_row_spec(t, d), _gain_spec(d)] if has_mix else []) + [_gain_spec(d)] * n_pre
    out_shape = ([jax.ShapeDtypeStruct((s, d), F32)] if has_mix else []) + [jax.ShapeDtypeStruct((s, d), BF16)] * n_pre
    out_specs = [_row_spec(t, d)] * len(out_shape)
    res = pl.pallas_call(
        body, name=name, grid=(s // t,), in_specs=in_specs, out_specs=out_specs, out_shape=out_shape,
        compiler_params=_params(("parallel",)),
    )(*ins)
    if has_mix:
        return res[0], list(res[1:])
    return x, list(res)


def _norm_res_loss(x, mix, g_post, target, name):
    s, d = x.shape
    t = _pick(s, (NORM_TILE, ROW_TILE))

    def body(x_ref, m_ref, g_ref, t_ref, dy_ref, loss_ref):
        mv = m_ref[...].astype(F32)
        y = x_ref[...] + mv * _rstd(mv) * g_ref[...]
        err = y - t_ref[...]
        dy_ref[...] = err * (1.0 / d)

        @pl.when(pl.program_id(0) == 0)
        def _():
            loss_ref[...] = jnp.zeros_like(loss_ref)

        loss_ref[...] += jnp.sum(err * err)

    dy, acc = pl.pallas_call(
        body, name=name, grid=(s // t,),
        in_specs=[_row_spec(t, d), _row_spec(t, d), _gain_spec(d), _row_spec(t, d)],
        out_specs=[_row_spec(t, d), pl.BlockSpec((8, LANE), lambda i: (0, 0))],
        out_shape=[jax.ShapeDtypeStruct((s, d), F32), jax.ShapeDtypeStruct((8, LANE), F32)],
        compiler_params=_params(("arbitrary",)),
    )(x, mix, g_post, target)
    return dy, acc[0, 0] * (0.5 / d)


def _norm_bwd(dx_out, branches, x_in, post, name):
    s, d = dx_out.shape
    t = _pick(s, (NORM_TILE, ROW_TILE))
    nb = len(branches)
    has_post = post is not None

    def body(*refs):
        dx_ref = refs[0]
        pos = 1
        dx = dx_ref[...]
        first = pl.program_id(0) == 0
        n_in = 1 + (1 + 2 * nb if nb else 0) + (2 if has_post else 0)
        outs = refs[n_in:]
        opos = 0
        if nb:
            xv = refs[pos][...]
            pos += 1
            r = _rstd(xv)
            dx_o = outs[0]
            opos = 1
            for _ in range(nb):
                dxn = refs[pos][...].astype(F32)
                g = refs[pos + 1][...]
                pos += 2
                dv, dg_rows = _rms_bwd(dxn, xv, g, r)
                dx = dx + dv
                dg_ref = outs[opos]
                opos += 1

                @pl.when(first)
                def _(dg_ref=dg_ref):
                    dg_ref[...] = jnp.zeros_like(dg_ref)

                dg_ref[...] += jnp.sum(dg_rows, axis=0, keepdims=True)
            dx_o[...] = dx
        if has_post:
            mv = refs[pos][...].astype(F32)
            g = refs[pos + 1][...]
            dm, dg_rows = _rms_bwd(dx, mv, g, _rstd(mv))
            outs[opos][...] = dm.astype(BF16)
            dg_ref = outs[opos + 1]

            @pl.when(first)
            def _():
                dg_ref[...] = jnp.zeros_like(dg_ref)

            dg_ref[...] += jnp.sum(dg_rows, axis=0, keepdims=True)

    ins, in_specs = [dx_out], [_row_spec(t, d)]
    out_shape, out_specs = [], []
    if nb:
        ins.append(x_in)
        in_specs.append(_row_spec(t, d))
        out_shape.append(jax.ShapeDtypeStruct((s, d), F32))
        out_specs.append(_row_spec(t, d))
        for dxn, g in branches:
            ins += [dxn, g]
            in_specs += [_row_spec(t, d), _gain_spec(d)]
            out_shape.append(jax.ShapeDtypeStruct((1, d), F32))
            out_specs.append(_gain_spec(d))
    if has_post:
        ins += [post[0], post[1]]
        in_specs += [_row_spec(t, d), _gain_spec(d)]
        out_shape += [jax.ShapeDtypeStruct((s, d), BF16), jax.ShapeDtypeStruct((1, d), F32)]
        out_specs += [_row_spec(t, d), _gain_spec(d)]
    res = pl.pallas_call(
        body, name=name, grid=(s // t,), in_specs=in_specs, out_specs=out_specs, out_shape=out_shape,
        compiler_params=_params(("arbitrary",)),
    )(*ins)
    res = list(res)
    dx_in = res.pop(0) if nb else dx_out
    dgs = [res.pop(0) for _ in range(nb)]
    dm, dg_post = (res[0], res[1]) if has_post else (None, None)
    return dx_in, dgs, dm, dg_post


HALO = 16


def _shift_down(u, prev, k):
    rows = lax.broadcasted_iota(jnp.int32, u.shape, 0)
    out = pltpu.roll(u, k, 0)
    for i in range(k):
        out = jnp.where(rows == i, prev[HALO - k + i:HALO - k + i + 1, :], out)
    return out


def _shift_up(u, nxt, k):
    n = u.shape[0]
    rows = lax.broadcasted_iota(jnp.int32, u.shape, 0)
    out = pltpu.roll(u, n - k, 0)
    for i in range(k):
        out = jnp.where(rows == n - k + i, nxt[i:i + 1, :], out)
    return out


def _conv_gate_fwd(p, cw, name):
    s, d3 = p.shape
    d = d3 // 3
    t = _pick(s, (ROW_TILE,))
    hb = t // HALO

    def body(p_ref, prev_ref, w_ref, z_ref):
        i = pl.program_id(0)
        pv = p_ref[...].astype(F32)
        b, u = pv[:, :d], pv[:, d:2 * d] * pv[:, 2 * d:]
        ph = prev_ref[...].astype(F32)
        up = jnp.where(i > 0, ph[:, d:2 * d] * ph[:, 2 * d:], 0.0)
        w = w_ref[...]
        y = w[0:1, :] * _shift_down(u, up, 2) + w[1:2, :] * _shift_down(u, up, 1) + w[2:3, :] * u
        z_ref[...] = (b * y).astype(BF16)

    return pl.pallas_call(
        body, name=name, grid=(s // t,),
        in_specs=[_row_spec(t, d3),
                  pl.BlockSpec((HALO, d3), lambda i: (jnp.maximum(i * hb - 1, 0), 0)),
                  pl.BlockSpec((3, d), lambda i: (0, 0))],
        out_specs=_row_spec(t, d),
        out_shape=jax.ShapeDtypeStruct((s, d), BF16),
        compiler_params=_params(("parallel",)),
    )(p, p, cw)


def _conv_gate_bwd(p, dz, cw, name):
    s, d3 = p.shape
    d = d3 // 3
    t = _pick(s, (ROW_TILE,))
    hb = t // HALO
    nt = s // t
    last_halo = s // HALO - 1

    def body(p_ref, prev_ref, next_ref, dz_ref, dznext_ref, w_ref, dp_ref, dw_ref):
        i = pl.program_id(0)
        pv = p_ref[...].astype(F32)
        b, c, h = pv[:, :d], pv[:, d:2 * d], pv[:, 2 * d:]
        u = c * h
        ph = prev_ref[...].astype(F32)
        up = jnp.where(i > 0, ph[:, d:2 * d] * ph[:, 2 * d:], 0.0)
        w = w_ref[...]
        u1, u2 = _shift_down(u, up, 1), _shift_down(u, up, 2)
        y = w[0:1, :] * u2 + w[1:2, :] * u1 + w[2:3, :] * u
        dz = dz_ref[...].astype(F32)
        dy = dz * b
        dyn = jnp.where(i < nt - 1, dznext_ref[...].astype(F32) * next_ref[...].astype(F32)[:, :d], 0.0)
        du = w[2:3, :] * dy + w[1:2, :] * _shift_up(dy, dyn, 1) + w[0:1, :] * _shift_up(dy, dyn, 2)
        dp_ref[:, :d] = (dz * y).astype(BF16)
        dp_ref[:, d:2 * d] = (du * h).astype(BF16)
        dp_ref[:, 2 * d:] = (du * c).astype(BF16)

        @pl.when(i == 0)
        def _():
            dw_ref[...] = jnp.zeros_like(dw_ref)

        dw_ref[0:1, :] += jnp.sum(dy * u2, axis=0, keepdims=True)
        dw_ref[1:2, :] += jnp.sum(dy * u1, axis=0, keepdims=True)
        dw_ref[2:3, :] += jnp.sum(dy * u, axis=0, keepdims=True)

    return pl.pallas_call(
        body, name=name, grid=(nt,),
        in_specs=[_row_spec(t, d3),
                  pl.BlockSpec((HALO, d3), lambda i: (jnp.maximum(i * hb - 1, 0), 0)),
                  pl.BlockSpec((HALO, d3), lambda i: (jnp.minimum((i + 1) * hb, last_halo), 0)),
                  _row_spec(t, d),
                  pl.BlockSpec((HALO, d), lambda i: (jnp.minimum((i + 1) * hb, last_halo), 0)),
                  pl.BlockSpec((3, d), lambda i: (0, 0))],
        out_specs=[_row_spec(t, d3), pl.BlockSpec((3, d), lambda i: (0, 0))],
        out_shape=[jax.ShapeDtypeStruct((s, d3), BF16), jax.ShapeDtypeStruct((3, d), F32)],
        compiler_params=_params(("arbitrary",)),
    )(p, p, p, dz, dz, cw)


FFN_ROWS, FFN_COLS = (1024, 512, 256), (256, 128)


def _ffn_in_swiglu(xn, w_in, name):
    s, k = xn.shape
    ff = w_in.shape[1] // 2
    tm, tn = _pick(s, FFN_ROWS), _pick(ff, FFN_COLS)
    nj = ff // tn

    def body(x_ref, wg_ref, wu_ref, f_ref, a_ref):
        xv = x_ref[...]
        gate = jnp.dot(xv, wg_ref[...], preferred_element_type=F32)
        up = jnp.dot(xv, wu_ref[...], preferred_element_type=F32)
        f_ref[0] = gate.astype(BF16)
        f_ref[1] = up.astype(BF16)
        a_ref[...] = (gate * jax.nn.sigmoid(gate) * up).astype(BF16)

    return pl.pallas_call(
        body, name=name, grid=(s // tm, nj),
        in_specs=[pl.BlockSpec((tm, k), lambda i, j: (i, 0)),
                  pl.BlockSpec((k, tn), lambda i, j: (0, j)),
                  pl.BlockSpec((k, tn), lambda i, j: (0, nj + j))],
        out_specs=[pl.BlockSpec((2, tm, tn), lambda i, j: (0, i, j)), pl.BlockSpec((tm, tn), lambda i, j: (i, j))],
        out_shape=[jax.ShapeDtypeStruct((2, s, ff), BF16), jax.ShapeDtypeStruct((s, ff), BF16)],
        compiler_params=_params(("parallel", "parallel")),
    )(xn, w_in, w_in)


def _ffn_out_dx_swiglu(dff, w_out, f, name):
    s, d = dff.shape
    ff = w_out.shape[0]
    tm, tn = _pick(s, FFN_ROWS), _pick(ff, FFN_COLS)

    def body(d_ref, w_ref, f_ref, df_ref):
        da = lax.dot_general(d_ref[...], w_ref[...], (((1,), (1,)), ((), ())), preferred_element_type=F32)
        gate = f_ref[0].astype(F32)
        up = f_ref[1].astype(F32)
        sg = jax.nn.sigmoid(gate)
        silu = gate * sg
        df_ref[0] = (da * up * (sg + silu * (1.0 - sg))).astype(BF16)
        df_ref[1] = (da * silu).astype(BF16)

    planes = pl.BlockSpec((2, tm, tn), lambda i, j: (0, i, j))
    return pl.pallas_call(
        body, name=name, grid=(s // tm, ff // tn),
        in_specs=[pl.BlockSpec((tm, d), lambda i, j: (i, 0)), pl.BlockSpec((tn, d), lambda i, j: (j, 0)), planes],
        out_specs=planes, out_shape=jax.ShapeDtypeStruct((2, s, ff), BF16),
        compiler_params=_params(("parallel", "parallel")),
    )(dff, w_out, f)


SUPER = 2048
RES = 16
PAIR = 128
L = 128
FWD_TOGETHER = 4
BWD_TOGETHER = 4


def _alibi_slopes(n_heads):
    h = np.arange(n_heads, dtype=np.float32) + 1.0
    return np.power(2.0, -8.0 * h / n_heads).astype(np.float32)


def _permute16(x, inverse, name):
    s, d = x.shape
    cw = LANE

    def body(x_ref, o_ref):
        if inverse:
            for m in range(L):
                o_ref[RES * m:RES * (m + 1), :] = x_ref[pl.ds(m, RES, stride=L), :]
        else:
            for r in range(RES):
                o_ref[L * r:L * (r + 1), :] = x_ref[pl.ds(r, L, stride=RES), :]

    spec = pl.BlockSpec((SUPER, cw), lambda i, j: (i, j))
    return pl.pallas_call(
        body, name=name, grid=(s // SUPER, d // cw), in_specs=[spec], out_specs=spec,
        out_shape=jax.ShapeDtypeStruct((s, d), x.dtype),
        compiler_params=_params(("parallel", "parallel")),
    )(x)


def _slope_table(d):
    nh = d // HEAD_DIM
    sl = _alibi_slopes(nh)
    tab = np.repeat(sl, HEAD_DIM).reshape(d // PAIR, 1, PAIR)
    return jnp.asarray(np.broadcast_to(tab, (d // PAIR, 8, PAIR)).copy())


def _geometry(dil):
    nch = RES // dil
    return nch, L // nch


def _band(dil):
    nch, w = _geometry(dil)
    sh = w.bit_length() - 1
    i = lax.broadcasted_iota(jnp.int32, (L, 2 * L), 0)
    j = lax.broadcasted_iota(jnp.int32, (L, 2 * L), 1)

    def pos(t):
        return jnp.bitwise_and(t, w - 1) * nch + jnp.right_shift(t, sh)

    delta = pos(i) + L - (pos(jnp.bitwise_and(j, L - 1)) + jnp.bitwise_and(j, L))
    return (delta * dil).astype(F32), (delta >= 0) & (delta <= L), j < L


def _fill_bias(bias_s, sl_ref):
    for b, dil in enumerate(DILATIONS):
        base, band, prev_half = _band(dil)
        for first in range(2):
            valid = band & jnp.logical_not(prev_half) if first else band
            for h in range(2):
                slope = sl_ref[0:1, HEAD_DIM * h:HEAD_DIM * h + 1]
                bias_s[(2 * b + first) * 2 + h] = jnp.where(valid, -slope * base, NEG_BIG)


def _bias_index(b, sb, n):
    first = jnp.logical_and(sb == 0, n == 0).astype(jnp.int32)
    return (2 * b + first) * 2


def _offsets(dil, res, n):
    nch, w = _geometry(dil)

    def al(v):
        return v if isinstance(v, int) else pl.multiple_of(v, w)

    q_off = [al((a * dil + res) * L + n * w) for a in range(nch)]
    k_off = [al((a * dil + res) * 2 * L + L + n * w) for a in range(nch)]
    kp_off = [al((a * dil + res) * 2 * L + L + n * w - w) for a in range(nch)]
    return q_off, k_off, kp_off, w


def _gather(ref, offs, w):
    parts = [ref[pl.ds(o, w), :] for o in offs]
    return parts[0] if len(parts) == 1 else jnp.concatenate(parts, axis=0)


def _scatter(ref, offs, w, val, add=False):
    for a, o in enumerate(offs):
        piece = val[a * w:(a + 1) * w, :]
        if add:
            ref[pl.ds(o, w), :] += piece
        else:
            ref[pl.ds(o, w), :] = piece


def _fill_key_buffer(buf, prev_ref, cur_ref):
    for r in range(RES):
        buf[2 * L * r:2 * L * r + L, :] = prev_ref[L * r:L * (r + 1), :]
        buf[2 * L * r + L:2 * L * (r + 1), :] = cur_ref[L * r:L * (r + 1), :]


def _two_heads(x, low):
    zero = jnp.zeros_like(x)
    return jnp.concatenate([jnp.where(low, x, zero), jnp.where(low, zero, x)], axis=0)


def _loop_blocks(dil, stages, together):
    together = max(together, dil) if dil < RES else together

    def it(i, c):
        if dil == RES:
            blocks = [(i * together + k, 0) for k in range(together)]
        else:
            blocks = [(res, i * (together // dil) + k) for k in range(together // dil) for res in range(dil)]
        state = [stages[0](res, n) for res, n in blocks]
        for stage in stages[1:]:
            state = [stage(res, n, prev) for (res, n), prev in zip(blocks, state)]
        for writes in state:
            for args in writes:
                _scatter(*args)
        return c

    lax.fori_loop(0, RES // together, it, 0)


NT = (((1,), (1,)), ((), ()))
TN = (((0,), (0,)), ((), ()))


def _attention_fwd(q, kv, name):
    s, d = q.shape
    g_n, ns = d // PAIR, s // SUPER

    def body(sl_ref, q_ref, kc_ref, kp_ref, vc_ref, vp_ref, o_ref, lse_ref, kbuf, vbuf, m_s, l_s, acc_s, bias_s):
        sb = pl.program_id(1)
        _fill_key_buffer(kbuf, kp_ref, kc_ref)
        _fill_key_buffer(vbuf, vp_ref, vc_ref)
        _fill_bias(bias_s, sl_ref)
        low = lax.broadcasted_iota(jnp.int32, (L, PAIR), 1) < HEAD_DIM
        low_k = lax.broadcasted_iota(jnp.int32, (2 * L, PAIR), 1) < HEAD_DIM
        ones_bd = _two_heads(jnp.ones((2 * L, PAIR), BF16), low_k)
        for bi, dil in enumerate(DILATIONS):
            first_branch, last_branch = bi == 0, bi == len(DILATIONS) - 1

            def scores(res, n, dil=dil):
                q_off, k_off, kp_off, w = _offsets(dil, res, n)
                qf = _gather(q_ref, q_off, w).astype(BF16)
                kcat = jnp.concatenate([_gather(kbuf, kp_off, w), _gather(kbuf, k_off, w)], axis=0).astype(BF16)
                return lax.dot_general(qf, _two_heads(kcat, low_k), NT, preferred_element_type=F32)

            def update(res, n, sc, bi=bi, dil=dil, first_branch=first_branch, last_branch=last_branch):
                q_off, k_off, kp_off, w = _offsets(dil, res, n)
                vcat = jnp.concatenate([_gather(vbuf, kp_off, w), _gather(vbuf, k_off, w)], axis=0).astype(BF16)
                v_ones = jnp.concatenate([_two_heads(vcat, low_k), ones_bd], axis=1)
                bias_at = _bias_index(bi, sb, n)
                if not first_branch:
                    m_prev = _gather(m_s, q_off, w)
                ps, m_new = [], []
                for h in range(2):
                    s_h = sc[:, 2 * L * h:2 * L * (h + 1)] + bias_s[bias_at + h]
                    mh = jnp.max(s_h, axis=1, keepdims=True)
                    if not first_branch:
                        mh = jnp.maximum(mh, m_prev[:, HEAD_DIM * h:HEAD_DIM * h + 1])
                    ps.append(jnp.exp(s_h - mh).astype(BF16))
                    m_new.append(mh)
                m_full = jnp.where(low, m_new[0], m_new[1])
                both = jnp.dot(jnp.concatenate(ps, axis=1), v_ones, preferred_element_type=F32)
                acc, l_full = both[:, :PAIR], both[:, PAIR:]
                if not first_branch:
                    alpha = jnp.exp(m_prev - m_full)
                    l_full = _gather(l_s, q_off, w) * alpha + l_full
                    acc = _gather(acc_s, q_off, w) * alpha + acc
                if last_branch:
                    return [(o_ref, q_off, w, acc / l_full, False), (lse_ref, q_off, w, m_full + jnp.log(l_full), False)]
                return [(m_s, q_off, w, m_full, False), (l_s, q_off, w, l_full, False), (acc_s, q_off, w, acc, False)]

            _loop_blocks(dil, [scores, update], FWD_TOGETHER)

    prev = lambda i: jnp.maximum(i - 1, 0)
    blk = pl.BlockSpec((SUPER, PAIR), lambda g, i: (i, g))
    in_specs = [pl.BlockSpec((None, 8, PAIR), lambda g, i: (g, 0, 0)), blk,
                pl.BlockSpec((SUPER, PAIR), lambda g, i: (i, g)),
                pl.BlockSpec((SUPER, PAIR), lambda g, i: (prev(i), g)),
                pl.BlockSpec((SUPER, PAIR), lambda g, i: (i, g_n + g)),
                pl.BlockSpec((SUPER, PAIR), lambda g, i: (prev(i), g_n + g))]
    return pl.pallas_call(
        body, name=name, grid=(g_n, ns), in_specs=in_specs, out_specs=[blk, blk],
        out_shape=[jax.ShapeDtypeStruct((s, d), F32)] * 2,
        scratch_shapes=([pltpu.VMEM((2 * SUPER, PAIR), F32)] * 2 + [pltpu.VMEM((SUPER, PAIR), F32)] * 3
                        + [pltpu.VMEM((4 * len(DILATIONS), L, 2 * L), F32)]),
        compiler_params=_params(("parallel", "parallel")),
    )(_slope_table(d), q, kv, kv, kv, kv)


def _attention_bwd(q, kv, o, do, lse, dkv_in, name):
    s, d = q.shape
    g_n, ns = d // PAIR, s // SUPER
    has_in = dkv_in is not None

    def body(*refs):
        sl_ref, q_ref, do_ref, o_ref, lse_ref, kc_ref, kp_ref, vc_ref, vp_ref = refs[:9]
        pos = 9
        if has_in:
            dkv_in_ref = refs[9]
            pos = 10
        dq_ref, dkv_ref, kbuf, vbuf, dkbuf, dvbuf, dq_s, bias_s = refs[pos:]
        step = pl.program_id(1)
        sb = ns - 1 - step
        _fill_key_buffer(kbuf, kp_ref, kc_ref)
        _fill_key_buffer(vbuf, vp_ref, vc_ref)
        _fill_bias(bias_s, sl_ref)

        @pl.when(step == 0)
        def _():
            dkbuf[...] = jnp.zeros_like(dkbuf)
            dvbuf[...] = jnp.zeros_like(dvbuf)

        @pl.when(step > 0)
        def _():
            for buf in (dkbuf, dvbuf):
                for r in range(RES):
                    buf[2 * L * r + L:2 * L * (r + 1), :] = buf[2 * L * r:2 * L * r + L, :]
                    buf[2 * L * r:2 * L * r + L, :] = jnp.zeros((L, PAIR), F32)

        low = lax.broadcasted_iota(jnp.int32, (L, PAIR), 1) < HEAD_DIM
        low_k = lax.broadcasted_iota(jnp.int32, (2 * L, PAIR), 1) < HEAD_DIM
        low_t = lax.broadcasted_iota(jnp.int32, (PAIR, 2 * L), 0) < HEAD_DIM
        for bi, dil in enumerate(DILATIONS):
            first_branch = bi == 0

            def scores(res, n, dil=dil):
                q_off, k_off, kp_off, w = _offsets(dil, res, n)
                qb = _gather(q_ref, q_off, w).astype(BF16)
                dof = _gather(do_ref, q_off, w)
                prod = dof * _gather(o_ref, q_off, w)
                dob = dof.astype(BF16)
                lse_f = _gather(lse_ref, q_off, w)
                zero = jnp.zeros_like(prod)
                dsum = (jnp.sum(jnp.where(low, prod, zero), axis=1, keepdims=True),
                        jnp.sum(jnp.where(low, zero, prod), axis=1, keepdims=True))
                kcat = jnp.concatenate([_gather(kbuf, kp_off, w), _gather(kbuf, k_off, w)], axis=0).astype(BF16)
                vcat = jnp.concatenate([_gather(vbuf, kp_off, w), _gather(vbuf, k_off, w)], axis=0).astype(BF16)
                k_bd, v_bd = _two_heads(kcat, low_k), _two_heads(vcat, low_k)
                sc = lax.dot_general(qb, k_bd, NT, preferred_element_type=F32)
                dp = lax.dot_general(dob, v_bd, NT, preferred_element_type=F32)
                return qb, dob, lse_f, dsum, k_bd, sc, dp

            def gradients(res, n, given, bi=bi, dil=dil, first_branch=first_branch):
                qb, dob, lse_f, dsum, k_bd, sc, dp = given
                q_off, k_off, kp_off, w = _offsets(dil, res, n)
                bias_at = _bias_index(bi, sb, n)
                ps, dss = [], []
                for h in range(2):
                    cols = slice(2 * L * h, 2 * L * (h + 1))
                    lse_h = lse_f[:, HEAD_DIM * h:HEAD_DIM * h + 1]
                    p_h = jnp.exp(sc[:, cols] + bias_s[bias_at + h] - lse_h)
                    dss.append((p_h * (dp[:, cols] - dsum[h])).astype(BF16))
                    ps.append(p_h.astype(BF16))
                ds_cat, p_cat = jnp.concatenate(dss, axis=1), jnp.concatenate(ps, axis=1)
                dq = jnp.dot(ds_cat, k_bd, preferred_element_type=F32)
                dk_t = lax.dot_general(qb, ds_cat, TN, preferred_element_type=F32)
                dv_t = lax.dot_general(dob, p_cat, TN, preferred_element_type=F32)
                dk = jnp.where(low_t, dk_t[:, :2 * L], dk_t[:, 2 * L:]).T
                dv = jnp.where(low_t, dv_t[:, :2 * L], dv_t[:, 2 * L:]).T
                return [(dq_s, q_off, w, dq, not first_branch),
                        (dkbuf, kp_off, w, dk[:L], True), (dkbuf, k_off, w, dk[L:], True),
                        (dvbuf, kp_off, w, dv[:L], True), (dvbuf, k_off, w, dv[L:], True)]

            _loop_blocks(dil, [scores, gradients], BWD_TOGETHER)

        dq_ref[...] = dq_s[...].astype(BF16)
        for r in range(RES):
            rows, cur = slice(L * r, L * (r + 1)), slice(2 * L * r + L, 2 * L * (r + 1))
            for plane, buf in enumerate((dkbuf, dvbuf)):
                if has_in:
                    dkv_ref[plane, rows, :] = buf[cur, :] + dkv_in_ref[plane, rows, :]
                else:
                    dkv_ref[plane, rows, :] = buf[cur, :]

    rev = lambda i: ns - 1 - i
    prev = lambda i: jnp.maximum(ns - 2 - i, 0)
    blk = pl.BlockSpec((SUPER, PAIR), lambda g, i: (rev(i), g))
    in_specs = [pl.BlockSpec((None, 8, PAIR), lambda g, i: (g, 0, 0)), blk, blk, blk, blk,
                pl.BlockSpec((SUPER, PAIR), lambda g, i: (rev(i), g)),
                pl.BlockSpec((SUPER, PAIR), lambda g, i: (prev(i), g)),
                pl.BlockSpec((SUPER, PAIR), lambda g, i: (rev(i), g_n + g)),
                pl.BlockSpec((SUPER, PAIR), lambda g, i: (prev(i), g_n + g))]
    ins = [_slope_table(d), q, do, o, lse, kv, kv, kv, kv]
    planes = pl.BlockSpec((2, SUPER, PAIR), lambda g, i: (0, rev(i), g))
    if has_in:
        in_specs.append(planes)
        ins.append(dkv_in)
    res = pl.pallas_call(
        body, name=name, grid=(g_n, ns), in_specs=in_specs, out_specs=[blk, planes],
        out_shape=[jax.ShapeDtypeStruct((s, d), BF16), jax.ShapeDtypeStruct((2, s, d), F32)],
        scratch_shapes=([pltpu.VMEM((2 * SUPER, PAIR), F32)] * 4 + [pltpu.VMEM((SUPER, PAIR), F32)]
                        + [pltpu.VMEM((4 * len(DILATIONS), L, 2 * L), F32)]),
        compiler_params=_params(("parallel", "arbitrary")),
    )(*ins)
    return res[0], res[1]


def _coords():
    return lax.axis_index("x"), lax.axis_index("y"), lax.axis_index("c")


def _chip_peers(x, y):
    return [(1 - x, y), (x, 1 - y), (1 - x, 1 - y)]


def _block_of(ref, axis, blk, size):
    start = pl.multiple_of(blk * size, size)
    if axis == 1:
        return ref.at[:, pl.ds(start, size)]
    return ref.at[pl.ds(start, size), :]


ANY = pl.BlockSpec(memory_space=pl.ANY)


HBM = pl.BlockSpec(memory_space=pltpu.HBM)
SEM = pl.BlockSpec(memory_space=pltpu.SEMAPHORE)
SPLIT = pltpu.CompilerParams(has_side_effects=pltpu.SideEffectType.DATAFLOW_SIDE_EFFECTING)


def _in_hbm(a):
    return pltpu.with_memory_space_constraint(a, pltpu.HBM)


def _thru(arrays):
    return [pltpu.HBM(a.shape, a.dtype) for a in arrays]


def _cast_place(w, layer, ax, dtype, name):
    _, k, n = w.shape
    t = _pick(k, (512, 256, 128))
    nb = k // t

    def body(blk_ref, w_ref, b_ref, f_ref):
        v = w_ref[...].astype(dtype)
        b_ref[...] = v
        f_ref[...] = v

    full_shape = (k, 4 * n) if ax == 1 else (4 * k, n)
    place = (lambda i, blk: (i, blk[0])) if ax == 1 else (lambda i, blk: (blk[0] * nb + i, 0))
    return pl.pallas_call(
        body, name=name,
        grid_spec=pltpu.PrefetchScalarGridSpec(
            num_scalar_prefetch=1, grid=(nb,),
            in_specs=[pl.BlockSpec((None, t, n), lambda i, blk: (layer, i, 0))],
            out_specs=[pl.BlockSpec((t, n), lambda i, blk: (i, 0)), pl.BlockSpec((t, n), place)]),
        out_shape=[jax.ShapeDtypeStruct((k, n), dtype), jax.ShapeDtypeStruct(full_shape, dtype)],
        compiler_params=_params(("parallel",)),
    )(_my_block()[None], w)


def _my_block():
    return (2 * lax.axis_index("x") + lax.axis_index("y")).astype(jnp.int32)


def _gather_start(group, carry, name):
    n, nc = len(group), len(carry)

    def body(*refs):
        blocks, fulls, send_sem, recv_sem = refs[:n], refs[n:2 * n], refs[2 * n + nc], refs[2 * n + nc + 1]
        x, y, c = _coords()
        for t, (b, _, ax) in enumerate(group):
            mine = _block_of(fulls[t], ax, 2 * x + y, b.shape[ax])
            for j, (px, py) in enumerate(_chip_peers(x, y)):
                pltpu.make_async_remote_copy(
                    src_ref=blocks[t], dst_ref=mine, send_sem=send_sem.at[3 * t + j], recv_sem=recv_sem.at[3 * t + j],
                    device_id=(px, py, c), device_id_type=MESH).start()

    arrays = [b for b, _, _ in group] + [f for _, f, _ in group] + list(carry)
    sems = [pltpu.SemaphoreType.DMA((3 * n,))] * 2
    res = pl.pallas_call(
        body, name=name, in_specs=[HBM] * len(arrays), out_specs=[SEM, SEM] + [HBM] * len(arrays),
        out_shape=sems + _thru(arrays), input_output_aliases={i: 2 + i for i in range(len(arrays))},
        compiler_params=SPLIT,
    )(*[_in_hbm(a) for a in arrays])
    return (res[0], res[1], list(res[2:2 + n]), list(res[2 + n:2 + 2 * n])), list(res[2 + 2 * n:])


def _gather_wait(group, started, after, name):
    sends, recvs, blocks, fulls = started
    m = len(group)

    def body(*refs):
        blk_refs, full_refs, send_sem, recv_sem = refs[:m], refs[m:2 * m], refs[2 * m], refs[2 * m + 1]
        x, y, c = _coords()
        for t, (b, _, ax) in enumerate(group):
            for j, (px, py) in enumerate(_chip_peers(x, y)):
                cp = pltpu.make_async_remote_copy(
                    src_ref=blk_refs[t], dst_ref=_block_of(full_refs[t], ax, 2 * px + py, b.shape[ax]),
                    send_sem=send_sem.at[3 * t + j], recv_sem=recv_sem.at[3 * t + j],
                    device_id=(px, py, c), device_id_type=MESH)
                cp.wait_send()
                cp.wait_recv()

    extra = [] if after is None else [after]
    res = pl.pallas_call(
        body, name=name, in_specs=[HBM] * (2 * m) + [SEM, SEM] + [ANY] * len(extra), out_specs=[HBM] * (2 * m),
        out_shape=_thru(blocks) + _thru(fulls), input_output_aliases={i: i for i in range(2 * m)},
        compiler_params=SPLIT,
    )(*blocks, *fulls, sends, recvs, *extra)
    return list(res[m:])


def _scatter_start(grads, carry, name):
    n = len(grads)
    n_in = 2 * n + len(carry)

    def body(*refs):
        g_refs, st_refs, send_sem, recv_sem = refs[:n], refs[n:2 * n], refs[n_in], refs[n_in + 1]
        x, y, c = _coords()
        for t, (g, ax) in enumerate(grads):
            for j, (px, py) in enumerate(_chip_peers(x, y)):
                pltpu.make_async_remote_copy(
                    src_ref=_block_of(g_refs[t], ax, 2 * px + py, g.shape[ax] // 4), dst_ref=st_refs[t].at[j],
                    send_sem=send_sem.at[3 * t + j], recv_sem=recv_sem.at[3 * t + j],
                    device_id=(px, py, c), device_id_type=MESH).start()

    arrays = [g for g, _ in grads]
    for g, ax in grads:
        shape = list(g.shape)
        shape[ax] //= 4
        arrays.append(lax.empty((3, *shape), g.dtype))
    arrays += list(carry)
    sems = [pltpu.SemaphoreType.DMA((3 * n,))] * 2
    res = pl.pallas_call(
        body, name=name, in_specs=[HBM] * n_in, out_specs=[SEM, SEM] + [HBM] * n_in,
        out_shape=sems + _thru(arrays), input_output_aliases={i: 2 + i for i in range(n_in)},
        compiler_params=SPLIT,
    )(*[_in_hbm(a) for a in arrays])
    return (res[0], res[1], list(res[2:2 + n]), list(res[2 + n:2 + 2 * n])), list(res[2 + 2 * n:])


def _scatter_wait(axes, started, after, name):
    sends, recvs, full, stacks = started
    n = len(full)
    extra = [] if after is None else [after]

    def body(*refs):
        g_refs, st_refs, send_sem, recv_sem = refs[:n], refs[n:2 * n], refs[2 * n], refs[2 * n + 1]
        x, y, c = _coords()
        for t, ax in enumerate(axes):
            size = full[t].shape[ax] // 4
            for j, (px, py) in enumerate(_chip_peers(x, y)):
                cp = pltpu.make_async_remote_copy(
                    src_ref=_block_of(g_refs[t], ax, 2 * px + py, size), dst_ref=st_refs[t].at[j],
                    send_sem=send_sem.at[3 * t + j], recv_sem=recv_sem.at[3 * t + j],
                    device_id=(px, py, c), device_id_type=MESH)
                cp.wait_send()
                cp.wait_recv()

    res = pl.pallas_call(
        body, name=name, in_specs=[HBM] * (2 * n) + [SEM, SEM] + [ANY] * len(extra), out_specs=[HBM] * (2 * n),
        out_shape=_thru(full) + _thru(stacks), input_output_aliases={i: i for i in range(2 * n)},
        compiler_params=SPLIT,
    )(*full, *stacks, sends, recvs, *extra)
    return list(res[:n]), list(res[n:])


def _pair_copies(g_refs, st_refs, out_refs, items, send_sem, recv_sem):
    x, y, c = _coords()
    copies = []
    for u, (g, ax, _) in enumerate(items):
        own = _block_of(g_refs[u], ax, 2 * x + y, g.shape[ax] // 4)
        for k, (src, dst) in enumerate([(own, out_refs[u].at[0]), (st_refs[u], out_refs[u].at[pl.ds(1, 3)])]):
            copies.append(pltpu.make_async_remote_copy(
                src_ref=src, dst_ref=dst, send_sem=send_sem.at[2 * u + k], recv_sem=recv_sem.at[2 * u + k],
                device_id=(x, y, 1 - c), device_id_type=MESH))
    return copies


def _pair_start(items, carry, name):
    n = len(items)
    n_in = 3 * n + len(carry)

    def body(*refs):
        for cp in _pair_copies(refs[:n], refs[n:2 * n], refs[2 * n:3 * n], items, refs[n_in], refs[n_in + 1]):
            cp.start()

    arrays = ([g for g, _, _ in items] + [st for _, _, st in items]
              + [lax.empty((4, *st.shape[1:]), st.dtype) for _, _, st in items] + list(carry))
    sems = [pltpu.SemaphoreType.DMA((2 * n,))] * 2
    res = pl.pallas_call(
        body, name=name, in_specs=[HBM] * n_in, out_specs=[SEM, SEM] + [HBM] * n_in,
        out_shape=sems + _thru(arrays), input_output_aliases={i: 2 + i for i in range(n_in)},
        compiler_params=SPLIT,
    )(*[_in_hbm(a) for a in arrays])
    thru = res[2:]
    return (res[0], res[1], *(list(thru[k * n:(k + 1) * n]) for k in range(3))), list(thru[3 * n:])


def _pair_wait(axes, started, name):
    send, recv, full, stacks, landing = started
    n = len(full)
    items = [(full[u], axes[u], stacks[u]) for u in range(n)]

    def body(*refs):
        for cp in _pair_copies(refs[:n], refs[n:2 * n], refs[2 * n:3 * n], items, refs[3 * n], refs[3 * n + 1]):
            cp.wait_send()
            cp.wait_recv()

    res = pl.pallas_call(
        body, name=name, in_specs=[HBM] * (3 * n) + [SEM, SEM], out_specs=[HBM] * (3 * n),
        out_shape=_thru(full + stacks + landing), input_output_aliases={i: i for i in range(3 * n)},
        compiler_params=SPLIT,
    )(*full, *stacks, *landing, send, recv)
    return list(res[:n]), list(res[n:2 * n]), list(res[2 * n:])


def _allreduce_small(v, name):
    r, cdim = v.shape

    def body(v_ref, out_ref, buf, send_sems, recv_sems):
        x, y, c = _coords()
        me = 4 * x + 2 * y + c
        buf[0] = v_ref[...]
        sends = []
        for k in range(1, 8):
            peer = (x if not (k & 4) else 1 - x, y if not (k & 2) else 1 - y, c if not (k & 1) else 1 - c)
            cp = pltpu.make_async_remote_copy(
                src_ref=v_ref, dst_ref=buf.at[k], send_sem=send_sems.at[k - 1], recv_sem=recv_sems.at[k - 1],
                device_id=peer, device_id_type=MESH)
            cp.start()
            sends.append(cp)
        for cp in sends:
            cp.wait_recv()
        total = buf[me]
        for src in range(1, 8):
            total = total + buf[jnp.bitwise_xor(me, src)]
        out_ref[...] = total
        for cp in sends:
            cp.wait_send()

    return pl.pallas_call(
        body, name=name,
        in_specs=[pl.BlockSpec(memory_space=pltpu.VMEM)], out_specs=pl.BlockSpec(memory_space=pltpu.VMEM),
        out_shape=jax.ShapeDtypeStruct((r, cdim), F32),
        scratch_shapes=[pltpu.VMEM((8, r, cdim), F32), pltpu.SemaphoreType.DMA((7,)), pltpu.SemaphoreType.DMA((7,))],
        compiler_params=pltpu.CompilerParams(has_side_effects=True),
    )(v)


def _adamw_math(w, g, m, v):
    m = ADAM_B1 * m + (1.0 - ADAM_B1) * g
    v = ADAM_B2 * v + (1.0 - ADAM_B2) * jnp.square(g)
    m_hat = m / (1.0 - ADAM_B1 ** ADAM_STEP)
    v_hat = v / (1.0 - ADAM_B2 ** ADAM_STEP)
    delta = -ADAM_LR * (m_hat / (jnp.sqrt(v_hat) + ADAM_EPS) + ADAM_WD * w)
    return delta, m, v


def _adamw(w, m, v, grads, name):
    r, cdim = w.shape
    paired = isinstance(grads, list)
    layers = len(grads) if paired else 1
    t = _pick(r // layers, (128, 64, 32, 16, 8))
    per_layer = r // layers // t
    n_grad = 3 * layers if paired else 1

    def body(*refs):
        refs = refs[1:] if paired else refs
        w_ref, m_ref, v_ref = refs[:3]
        outs = refs[3 + n_grad:]

        def update(g):
            delta, m_new, v_new = _adamw_math(w_ref[...], g, m_ref[...], v_ref[...])
            outs[0][...] = g
            outs[1][...] = delta
            outs[2][...] = m_new
            outs[3][...] = v_new

        if not paired:
            update(refs[3][...])
            return
        layer = pl.program_id(0) // per_layer
        for l in range(layers):
            @pl.when(layer == l)
            def _(own_ref=refs[3 + 3 * l], st_ref=refs[4 + 3 * l], sib_ref=refs[5 + 3 * l]):
                sa = own_ref[...].astype(F32)
                sb = sib_ref[0].astype(F32)
                for k in range(3):
                    sa = sa + st_ref[k].astype(F32)
                    sb = sb + sib_ref[k + 1].astype(F32)
                update(sa + sb)

    out_shape = [jax.ShapeDtypeStruct((r, cdim), F32)] * 4
    if not paired:
        spec = pl.BlockSpec((t, cdim), lambda i: (i, 0))
        return pl.pallas_call(
            body, name=name, grid=(r // t,), in_specs=[spec] * 4, out_specs=[spec] * 4, out_shape=out_shape,
            compiler_params=_params(("parallel",)),
        )(w, m, v, grads)

    spec = pl.BlockSpec((t, cdim), lambda i, blk: (i, 0))
    ins, in_specs = [w, m, v], [spec] * 3
    for l, (g, ax, stack, sib) in enumerate(grads):
        row = lambda i, l=l: jnp.clip(i - l * per_layer, 0, per_layer - 1)
        own = ((lambda i, blk, row=row: (row(i), blk[0])) if ax == 1
               else (lambda i, blk, row=row: (blk[0] * per_layer + row(i), 0)))
        ins += [g, stack, sib]
        in_specs += [pl.BlockSpec((t, cdim), own),
                     pl.BlockSpec((3, t, cdim), lambda i, blk, row=row: (0, row(i), 0)),
                     pl.BlockSpec((4, t, cdim), lambda i, blk, row=row: (0, row(i), 0))]
    return pl.pallas_call(
        body, name=name,
        grid_spec=pltpu.PrefetchScalarGridSpec(
            num_scalar_prefetch=1, grid=(r // t,), in_specs=in_specs, out_specs=[spec] * 4),
        out_shape=out_shape, compiler_params=_params(("parallel",)),
    )(_my_block()[None], *ins)


def _local_step(x, target, gains, conv_ws, kv_gain, weights_of, send_grads):
    depth = len(gains)
    n_a = len(conv_ws)
    saved, ws = [], []
    kv = kvn = None
    _, (xn,) = _norm_res_fwd(x, None, None, [gains[0][0]], "norm_first")
    h = x
    for l in range(depth):
        g = gains[l]
        sv = {"x_in": h, "xn": xn}
        w = weights_of(l, "mix", h)
        ws.append(w)
        if l == n_a:
            kv = _mm(kvn, w["kv"], "nn", F32, "kv_fwd")
        if l < n_a:
            p = _mm(xn, w["conv_in"], "nn", BF16, f"conv_in_fwd_{l}")
            z = _conv_gate_fwd(p, conv_ws[l], f"conv_gate_fwd_{l}")
            mix = _mm(z, w["conv_out"], "nn", BF16, f"conv_out_fwd_{l}")
            sv.update(p=p, z=z)
        else:
            j = l - n_a
            q = _mm(xn, w["q"], "nn", F32, f"q_fwd_{j}", scale=HEAD_DIM ** -0.5)
            o, lse = _attention_fwd(q, kv, f"attn_fwd_{j}")
            mix = _mm(o, w["o"], "nn", BF16, f"o_fwd_{j}")
            sv.update(q=q, o=o, lse=lse)
        x1, (xn2,) = _norm_res_fwd(h, mix, g[1], [g[2]], f"norm_mid_{l}")
        w.update(weights_of(l, "ffn", mix))
        f, a = _ffn_in_swiglu(xn2, w["ffn_in"], f"ffn_in_fwd_{l}")
        ff = _mm(a, w["ffn_out"], "nn", BF16, f"ffn_out_fwd_{l}")
        sv.update(mix=mix, x1=x1, xn2=xn2, f=f, a=a, ff=ff)
        saved.append(sv)
        if l == depth - 1:
            dx, loss = _norm_res_loss(x1, ff, g[3], target, "norm_loss")
        else:
            if l == n_a - 1:
                h, _ = _norm_res_fwd(x1, ff, g[3], [], f"norm_end_{l}")
                h = _permute16(h, False, "permute_stream")
                target = _permute16(target, False, "permute_target")
                _, (xn, kvn) = _norm_res_fwd(h, None, None, [gains[l + 1][0], kv_gain], "norm_permuted")
            else:
                h, (xn,) = _norm_res_fwd(x1, ff, g[3], [gains[l + 1][0]], f"norm_end_{l}")
    d_gains = [[None] * 4 for _ in range(depth)]
    d_conv = [None] * n_a
    d_kv_gain = None
    dkv = None
    _, _, dff, d_gains[depth - 1][3] = _norm_bwd(dx, [], None, (saved[-1]["ff"], gains[-1][3]), "norm_loss_bwd")
    for l in reversed(range(depth)):
        sv, g, w, grads = saved[l], gains[l], ws[l], {}
        grads["ffn_out"] =_mm(sv["a"], dff, "tn", BF16, f"ffn_out_dw_{l}")
        df = _ffn_out_dx_swiglu(dff, w["ffn_out"], sv["f"], f"ffn_out_dx_{l}")
        dxn2 = _mm(df, w["ffn_in"], "nt", BF16, f"ffn_in_dx_{l}")
        grads["ffn_in"] =_mm(sv["xn2"], df, "tn", BF16, f"ffn_in_dw_{l}")
        dx, (d_gains[l][2],), dmix, d_gains[l][1] = _norm_bwd(
            dx, [(dxn2, g[2])], sv["x1"], (sv["mix"], g[1]), f"norm_mid_bwd_{l}")
        dx, dmix = send_grads(l, "ffn", grads, [dx, dmix])
        if l < n_a:
            dz = _mm(dmix, w["conv_out"], "nt", BF16, f"conv_out_dx_{l}")
            grads["conv_out"] =_mm(sv["z"], dmix, "tn", BF16, f"conv_out_dw_{l}")
            dp, d_conv[l] = _conv_gate_bwd(sv["p"], dz, conv_ws[l], f"conv_gate_bwd_{l}")
            dxn = _mm(dp, w["conv_in"], "nt", BF16, f"conv_in_dx_{l}")
            grads["conv_in"] =_mm(sv["xn"], dp, "tn", BF16, f"conv_in_dw_{l}")
        else:
            j = l - n_a
            do = _mm(dmix, w["o"], "nt", F32, f"o_dx_{j}")
            grads["o"] =_mm(sv["o"], dmix, "tn", BF16, f"o_dw_{j}")
            dq, dkv = _attention_bwd(sv["q"], kv, sv["o"], do, sv["lse"], dkv, f"attn_bwd_{j}")
            scale = HEAD_DIM ** -0.5
            dxn = _mm(dq, w["q"], "nt", BF16, f"q_dx_{j}", scale=scale)
            grads["q"] =_mm(sv["xn"], dq, "tn", BF16, f"q_dw_{j}", scale=scale)
        branches = [(dxn, g[0])]
        if l == n_a:
            dkvn = _mm(dkv, w["kv"], "nt", BF16, "kv_dx")
            grads["kv"] =_mm(kvn, dkv, "tn", BF16, "kv_dw")
            branches.append((dkvn, kv_gain))
        post = (saved[l - 1]["ff"], gains[l - 1][3]) if l > 0 else None
        if l == n_a:
            dx, dgs, _, _ = _norm_bwd(dx, branches, sv["x_in"], None, f"norm_end_bwd_{l}")
            dx = _permute16(dx, True, "unpermute_stream")
            _, _, dff, dg_post = _norm_bwd(dx, [], None, post, "norm_boundary_bwd")
        else:
            dx, dgs, dff, dg_post = _norm_bwd(dx, branches, sv["x_in"], post, f"norm_end_bwd_{l}")
        if dff is None:
            send_grads(l, "mix", grads, [])
        else:
            dx, dff = send_grads(l, "mix", grads, [dx, dff])
        d_gains[l][0] = dgs[0]
        if l == n_a:
            d_kv_gain = dgs[1]
        if l > 0:
            d_gains[l - 1][3] = dg_post
    return loss, dx, d_gains, d_conv, d_kv_gain


BIG = (
    ("conv_in", 1), ("conv_out", 0), ("kv", 1), ("q", 0), ("o", 0), ("ffn_in", 1), ("ffn_out", 0))


def kernel(x, norm_g, conv_in_w, conv_w, conv_out_w, kv_norm_g, kv_w, q_w, o_w, ffn_in_w, ffn_out_w, loss_target, m_norm_g, m_conv_in_w, m_conv_w, m_conv_out_w, m_kv_norm_g, m_kv_w, m_q_w, m_o_w, m_ffn_in_w, m_ffn_out_w, v_norm_g, v_conv_in_w, v_conv_w, v_conv_out_w, v_kv_norm_g, v_kv_w, v_q_w, v_o_w, v_ffn_in_w, v_ffn_out_w):
    depth, _, dq = norm_g.shape
    d = 4 * dq
    n_a = conv_w.shape[0]
    big_w = {"conv_in": conv_in_w, "conv_out": conv_out_w, "kv": kv_w[None], "q": q_w, "o": o_w,
             "ffn_in": ffn_in_w, "ffn_out": ffn_out_w}
    big_m = {"conv_in": m_conv_in_w, "conv_out": m_conv_out_w, "kv": m_kv_w[None], "q": m_q_w, "o": m_o_w,
             "ffn_in": m_ffn_in_w, "ffn_out": m_ffn_out_w}
    big_v = {"conv_in": v_conv_in_w, "conv_out": v_conv_out_w, "kv": v_kv_w[None], "q": v_q_w, "o": v_o_w,
             "ffn_in": v_ffn_in_w, "ffn_out": v_ffn_out_w}

    n_gain, n_tap = depth * 4, n_a * conv_w.shape[1]
    small_rows = -(-(n_gain + n_tap + 1) // 8) * 8
    pad_rows = small_rows - n_gain - n_tap

    def pack_small(gains, taps):
        return jnp.concatenate([gains.reshape(n_gain, dq), taps.reshape(n_tap, dq), jnp.zeros((pad_rows, dq), F32)])

    axis_of = dict(BIG)

    def matrices_of(l, part):
        if part == "ffn":
            return [("ffn_in", l), ("ffn_out", l)]
        if l < n_a:
            return [("conv_in", l), ("conv_out", l)]
        return ([("kv", 0)] if l == n_a else []) + [("q", l - n_a), ("o", l - n_a)]

    halves = [(l, part) for l in range(depth) for part in ("mix", "ffn")]
    groups = {(l, part): [(*_cast_place(big_w[name], i, axis_of[name], BF16, f"place_{name}_{i}"), axis_of[name])
                          for name, i in matrices_of(l, part)] for l, part in halves}
    groups[halves[0]].append((*_cast_place(pack_small(norm_g, conv_w)[None], 0, 1, F32, "place_small"), 1))
    started = {halves[0]: _gather_start(groups[halves[0]], [], "gather_start_0_mix")[0]}

    def fetch(half, after):
        full = _gather_wait(groups[half], started[half], after, "gather_wait_%d_%s" % half)
        nxt = halves.index(half) + 1
        if nxt < len(halves):
            started[halves[nxt]], full = _gather_start(groups[halves[nxt]], full, "gather_start_%d_%s" % halves[nxt])
        return full

    first = fetch(halves[0], None)
    small = first[-1]
    gains = [[small[4 * l + i][None] for i in range(4)] for l in range(depth)]
    conv_ws = [small[n_gain + 3 * l:n_gain + 3 * l + 3] for l in range(n_a)]
    kv_gain = kv_norm_g[None]

    def weights_of(l, part, after):
        full = first if (l, part) == halves[0] else fetch((l, part), after)
        return {name: full[t] for t, (name, _) in enumerate(matrices_of(l, part))}

    sent, paired = {}, {}
    LAG = 2

    def to_sibling(half, carry, after):
        axes = [axis_of[name] for name, _ in matrices_of(*half)]
        full, stacks = _scatter_wait(axes, sent[half], after, "scatter_wait_%d_%s" % half)
        paired[half], carry = _pair_start(list(zip(full, axes, stacks)), carry, "pair_start_%d_%s" % half)
        return carry

    def send_grads(l, part, grads, carry):
        sent[l, part], carry = _scatter_start(
            [(grads[name], axis_of[name]) for name, _ in matrices_of(l, part)], carry, f"scatter_start_{l}_{part}")
        older = halves.index((l, part)) + LAG
        if older < len(halves) and carry:
            carry = to_sibling(halves[older], carry, carry[0])
        return carry

    loss, dx, d_gains, d_conv, d_kv_gain = _local_step(
        x[0], loss_target[0], gains, conv_ws, kv_gain, weights_of, send_grads)
    loss = lax.psum(loss, ("x", "y", "c"))

    small_g = jnp.concatenate([dg for row in d_gains for dg in row] + list(d_conv) + [d_kv_gain]
                              + [jnp.zeros((pad_rows - 1, d), F32)])
    small_g = _allreduce_small(small_g, "allreduce_small")
    blk = 2 * lax.axis_index("x") + lax.axis_index("y")
    mine_small = lax.dynamic_slice_in_dim(small_g, blk * dq, dq, axis=1)
    kv_rows = d // dq

    def pack_opt(gains_like, taps_like, kv_like):
        rows = jnp.concatenate([gains_like.reshape(n_gain, dq), taps_like.reshape(n_tap, dq), kv_like.reshape(kv_rows, dq)])
        extra = -rows.shape[0] % 8
        return jnp.concatenate([rows, jnp.zeros((extra, dq), F32)]) if extra else rows

    sw = pack_opt(norm_g, conv_w, kv_norm_g)
    sm = pack_opt(m_norm_g, m_conv_w, m_kv_norm_g)
    sv = pack_opt(v_norm_g, v_conv_w, v_kv_norm_g)
    sg = pack_opt(mine_small[:n_gain], mine_small[n_gain:n_gain + n_tap], small_g[n_gain + n_tap])
    s_out = _adamw(sw, sm, sv, sg, "adamw_small")

    def unpack(a):
        return (a[:n_gain].reshape(depth, 4, dq), a[n_gain:n_gain + n_tap].reshape(n_a, -1, dq),
                a[n_gain + n_tap:n_gain + n_tap + kv_rows].reshape(d))

    small_out = [unpack(a) for a in s_out]

    landed, big_out = {}, {}

    def update(name):
        for half in halves:
            if matrices_of(*half)[0] not in landed and any(n == name for n, _ in matrices_of(*half)):
                axes = [axis_of[n] for n, _ in matrices_of(*half)]
                landed.update(zip(matrices_of(*half), zip(*_pair_wait(axes, paired[half], "pair_wait_%d_%s" % half))))
        shp, ax = big_w[name].shape, axis_of[name]
        rows, cols = shp[0] * shp[1], shp[2]
        flat = lambda a: a.reshape(rows, cols)
        full, stacks, sibling = zip(*[landed[name, i] for i in range(shp[0])])
        res = _adamw(flat(big_w[name]), flat(big_m[name]), flat(big_v[name]),
                     [(full[i], ax, stacks[i], sibling[i]) for i in range(shp[0])], f"adamw_{name}")
        big_out[name] = [a.reshape(shp[1:] if name == "kv" else shp) for a in res]

    pending = [half for half in reversed(halves) if half not in paired]
    for half in pending[:-1]:
        to_sibling(half, [], None)
    late = [name for name, _ in BIG if any(n == name for n, _ in matrices_of(*pending[-1]))]
    for name, _ in BIG:
        if name not in late:
            update(name)
    to_sibling(pending[-1], [], big_out["ffn_out"][0])
    for name in late:
        update(name)

    def leaves(i):
        ng, cw_, kg = small_out[i]
        return [ng, big_out["conv_in"][i], cw_, big_out["conv_out"][i], kg, big_out["kv"][i], big_out["q"][i],
                big_out["o"][i], big_out["ffn_in"][i], big_out["ffn_out"][i]]

    return (loss, dx[None], *leaves(0), *leaves(1), *leaves(2), *leaves(3))
```

```python
import functools

import jax
import jax.numpy as jnp
import numpy as np
from jax import lax
from jax.experimental import pallas as pl
from jax.experimental.pallas import tpu as pltpu

F32 = jnp.float32
BF16 = jnp.bfloat16
HEAD_DIM = 64
DILATIONS = (1, 4, 16)
NORM_EPS = 1e-6
NEG_BIG = -1e30
VMEM_LIMIT = 48 * 1024 * 1024
ROW_TILE = 256
NORM_TILE = 512
LANE = 128
MESH = pl.DeviceIdType.MESH

ADAM_LR = 0.001
ADAM_B1 = 0.9
ADAM_B2 = 0.999
ADAM_EPS = 1e-08
ADAM_WD = 0.01
ADAM_STEP = 10

TILE_CANDIDATES = (1024, 1408, 768, 512, 384, 256, 128)


def _pick(dim, cands=TILE_CANDIDATES):
    for c in cands:
        if c <= dim and dim % c == 0:
            return c
    return dim


def _params(sem):
    return pltpu.CompilerParams(dimension_semantics=sem, vmem_limit_bytes=VMEM_LIMIT)


def _mm(a, b, mode, out_dtype, name, scale=None):
    a_planes = a.shape[0] if a.ndim == 3 else 1
    b_planes = b.shape[0] if b.ndim == 3 else 1
    if mode == "nn":
        m, k = a.shape[-2], a.shape[-1] * a_planes
        n = b.shape[1]
    elif mode == "nt":
        m, k = a.shape[-2], a.shape[-1] * a_planes
        n = b.shape[0]
    else:
        k, m = a.shape
        n = b.shape[-1] * b_planes
    tm, tn, tk = _pick(m), _pick(n // b_planes), _pick(k // a_planes, ((2048,) if mode == "tn" else ()) + TILE_CANDIDATES)
    nk = k // tk
    ka, nb = k // a_planes // tk, n // b_planes // tn
    if a_planes > 1:
        a_spec = pl.BlockSpec((None, tm, tk), lambda i, j, kk: (kk // ka, i, kk % ka))
    elif mode == "tn":
        a_spec = pl.BlockSpec((tk, tm), lambda i, j, kk: (kk, i))
    else:
        a_spec = pl.BlockSpec((tm, tk), lambda i, j, kk: (i, kk))
    if mode == "nn":
        b_spec = pl.BlockSpec((tk, tn), lambda i, j, kk: (kk, j))
        dims = (((1,), (0,)), ((), ()))
    elif mode == "nt":
        b_spec = pl.BlockSpec((tn, tk), lambda i, j, kk: (j, kk))
        dims = (((1,), (1,)), ((), ()))
    else:
        b_spec = (pl.BlockSpec((None, tk, tn), lambda i, j, kk: (j // nb, kk, j % nb)) if b_planes > 1
                  else pl.BlockSpec((tk, tn), lambda i, j, kk: (kk, j)))
        dims = (((0,), (0,)), ((), ()))

    def finish(acc):
        if scale is not None:
            acc = acc * scale
        return acc.astype(out_dtype)

    if nk == 1:
        def body(a_ref, b_ref, o_ref):
            o_ref[...] = finish(lax.dot_general(a_ref[...].astype(BF16), b_ref[...].astype(BF16), dims, preferred_element_type=F32))
        scratch = []
    else:
        def body(a_ref, b_ref, o_ref, acc_ref):
            kk = pl.program_id(2)

            @pl.when(kk == 0)
            def _():
                acc_ref[...] = jnp.zeros_like(acc_ref)

            acc_ref[...] += lax.dot_general(a_ref[...].astype(BF16), b_ref[...].astype(BF16), dims, preferred_element_type=F32)

            @pl.when(kk == nk - 1)
            def _():
                o_ref[...] = finish(acc_ref[...])
        scratch = [pltpu.VMEM((tm, tn), F32)]

    return pl.pallas_call(
        body, name=name,
        grid=(m // tm, n // tn, nk),
        in_specs=[a_spec, b_spec],
        out_specs=pl.BlockSpec((tm, tn), lambda i, j, kk: (i, j)),
        out_shape=jax.ShapeDtypeStruct((m, n), out_dtype),
        scratch_shapes=scratch,
        compiler_params=_params(("parallel", "parallel", "arbitrary")),
    )(a, b)


def _rstd(v):
    return lax.rsqrt(jnp.mean(v * v, axis=-1, keepdims=True) + NORM_EPS)


def _rms_bwd(dy, v, g, r):
    gy = dy * g
    dv = r * (gy - v * (r * r) * jnp.mean(gy * v, axis=-1, keepdims=True))
    return dv, dy * v * r


def _row_spec(t, width):
    return pl.BlockSpec((t, width), lambda i: (i, 0))


def _gain_spec(width):
    return pl.BlockSpec((1, width), lambda i: (0, 0))


def _norm_res_fwd(x, mix, g_post, pre_gains, name):
    s, d = x.shape
    t = _pick(s, (NORM_TILE, ROW_TILE))
    has_mix = mix is not None
    n_pre = len(pre_gains)

    def body(*refs):
        x_ref = refs[0]
        pos = 1
        x1 = x_ref[...]
        if has_mix:
            mv = refs[1][...].astype(F32)
            x1 = x1 + mv * _rstd(mv) * refs[2][...]
            pos = 3
        gains = refs[pos:pos + n_pre]
        outs = refs[pos + n_pre:]
        if has_mix:
            outs[0][...] = x1
            outs = outs[1:]
        r = _rstd(x1)
        for g_ref, o_ref in zip(gains, outs):
            o_ref[...] = (x1 * r * g_ref[...]).astype(BF16)

    ins = [x] + ([mix, g_post] if has_mix else []) + list(pre_gains)
    in_specs = [_row_spec(t, d)] + ([_row_spec(t, d), _gain_spec(d)] if has_mix else []) + [_gain_spec(d)] * n_pre
    out_shape = ([jax.ShapeDtypeStruct((s, d), F32)] if has_mix else []) + [jax.ShapeDtypeStruct((s, d), BF16)] * n_pre
    out_specs = [_row_spec(t, d)] * len(out_shape)
    res = pl.pallas_call(
        body, name=name, grid=(s // t,), in_specs=in_specs, out_specs=out_specs, out_shape=out_shape,
        compiler_params=_params(("parallel",)),
    )(*ins)
    if has_mix:
        return res[0], list(res[1:])
    return x, list(res)


def _norm_res_loss(x, mix, g_post, target, name):
    s, d = x.shape
    t = _pick(s, (NORM_TILE, ROW_TILE))

    def body(x_ref, m_ref, g_ref, t_ref, dy_ref, loss_ref):
        mv = m_ref[...].astype(F32)
        y = x_ref[...] + mv * _rstd(mv) * g_ref[...]
        err = y - t_ref[...]
        dy_ref[...] = err * (1.0 / d)

        @pl.when(pl.program_id(0) == 0)
        def _():
            loss_ref[...] = jnp.zeros_like(loss_ref)

        loss_ref[...] += jnp.sum(err * err)

    dy, acc = pl.pallas_call(
        body, name=name, grid=(s // t,),
        in_specs=[_row_spec(t, d), _row_spec(t, d), _gain_spec(d), _row_spec(t, d)],
        out_specs=[_row_spec(t, d), pl.BlockSpec((8, LANE), lambda i: (0, 0))],
        out_shape=[jax.ShapeDtypeStruct((s, d), F32), jax.ShapeDtypeStruct((8, LANE), F32)],
        compiler_params=_params(("arbitrary",)),
    )(x, mix, g_post, target)
    return dy, acc[0, 0] * (0.5 / d)


def _norm_bwd(dx_out, branches, x_in, post, name):
    s, d = dx_out.shape
    t = _pick(s, (NORM_TILE, ROW_TILE))
    nb = len(branches)
    has_post = post is not None

    def body(*refs):
        dx_ref = refs[0]
        pos = 1
        dx = dx_ref[...]
        first = pl.program_id(0) == 0
        n_in = 1 + (1 + 2 * nb if nb else 0) + (2 if has_post else 0)
        outs = refs[n_in:]
        opos = 0
        if nb:
            xv = refs[pos][...]
            pos += 1
            r = _rstd(xv)
            dx_o = outs[0]
            opos = 1
            for _ in range(nb):
                dxn = refs[pos][...].astype(F32)
                g = refs[pos + 1][...]
                pos += 2
                dv, dg_rows = _rms_bwd(dxn, xv, g, r)
                dx = dx + dv
                dg_ref = outs[opos]
                opos += 1

                @pl.when(first)
                def _(dg_ref=dg_ref):
                    dg_ref[...] = jnp.zeros_like(dg_ref)

                dg_ref[...] += jnp.sum(dg_rows, axis=0, keepdims=True)
            dx_o[...] = dx
        if has_post:
            mv = refs[pos][...].astype(F32)
            g = refs[pos + 1][...]
            dm, dg_rows = _rms_bwd(dx, mv, g, _rstd(mv))
            outs[opos][...] = dm.astype(BF16)
            dg_ref = outs[opos + 1]

            @pl.when(first)
            def _():
                dg_ref[...] = jnp.zeros_like(dg_ref)

            dg_ref[...] += jnp.sum(dg_rows, axis=0, keepdims=True)

    ins, in_specs = [dx_out], [_row_spec(t, d)]
    out_shape, out_specs = [], []
    if nb:
        ins.append(x_in)
        in_specs.append(_row_spec(t, d))
        out_shape.append(jax.ShapeDtypeStruct((s, d), F32))
        out_specs.append(_row_spec(t, d))
        for dxn, g in branches:
            ins += [dxn, g]
            in_specs += [_row_spec(t, d), _gain_spec(d)]
            out_shape.append(jax.ShapeDtypeStruct((1, d), F32))
            out_specs.append(_gain_spec(d))
    if has_post:
        ins += [post[0], post[1]]
        in_specs += [_row_spec(t, d), _gain_spec(d)]
        out_shape += [jax.ShapeDtypeStruct((s, d), BF16), jax.ShapeDtypeStruct((1, d), F32)]
        out_specs += [_row_spec(t, d), _gain_spec(d)]
    res = pl.pallas_call(
        body, name=name, grid=(s // t,), in_specs=in_specs, out_specs=out_specs, out_shape=out_shape,
        compiler_params=_params(("arbitrary",)),
    )(*ins)
    res = list(res)
    dx_in = res.pop(0) if nb else dx_out
    dgs = [res.pop(0) for _ in range(nb)]
    dm, dg_post = (res[0], res[1]) if has_post else (None, None)
    return dx_in, dgs, dm, dg_post


HALO = 16


def _shift_down(u, prev, k):
    rows = lax.broadcasted_iota(jnp.int32, u.shape, 0)
    out = pltpu.roll(u, k, 0)
    for i in range(k):
        out = jnp.where(rows == i, prev[HALO - k + i:HALO - k + i + 1, :], out)
    return out


def _shift_up(u, nxt, k):
    n = u.shape[0]
    rows = lax.broadcasted_iota(jnp.int32, u.shape, 0)
    out = pltpu.roll(u, n - k, 0)
    for i in range(k):
        out = jnp.where(rows == n - k + i, nxt[i:i + 1, :], out)
    return out


def _conv_gate_fwd(p, cw, name):
    s, d3 = p.shape
    d = d3 // 3
    t = _pick(s, (ROW_TILE,))
    hb = t // HALO

    def body(p_ref, prev_ref, w_ref, z_ref):
        i = pl.program_id(0)
        pv = p_ref[...].astype(F32)
        b, u = pv[:, :d], pv[:, d:2 * d] * pv[:, 2 * d:]
        ph = prev_ref[...].astype(F32)
        up = jnp.where(i > 0, ph[:, d:2 * d] * ph[:, 2 * d:], 0.0)
        w = w_ref[...]
        y = w[0:1, :] * _shift_down(u, up, 2) + w[1:2, :] * _shift_down(u, up, 1) + w[2:3, :] * u
        z_ref[...] = (b * y).astype(BF16)

    return pl.pallas_call(
        body, name=name, grid=(s // t,),
        in_specs=[_row_spec(t, d3),
                  pl.BlockSpec((HALO, d3), lambda i: (jnp.maximum(i * hb - 1, 0), 0)),
                  pl.BlockSpec((3, d), lambda i: (0, 0))],
        out_specs=_row_spec(t, d),
        out_shape=jax.ShapeDtypeStruct((s, d), BF16),
        compiler_params=_params(("parallel",)),
    )(p, p, cw)


def _conv_gate_bwd(p, dz, cw, name):
    s, d3 = p.shape
    d = d3 // 3
    t = _pick(s, (ROW_TILE,))
    hb = t // HALO
    nt = s // t
    last_halo = s // HALO - 1

    def body(p_ref, prev_ref, next_ref, dz_ref, dznext_ref, w_ref, dp_ref, dw_ref):
        i = pl.program_id(0)
        pv = p_ref[...].astype(F32)
        b, c, h = pv[:, :d], pv[:, d:2 * d], pv[:, 2 * d:]
        u = c * h
        ph = prev_ref[...].astype(F32)
        up = jnp.where(i > 0, ph[:, d:2 * d] * ph[:, 2 * d:], 0.0)
        w = w_ref[...]
        u1, u2 = _shift_down(u, up, 1), _shift_down(u, up, 2)
        y = w[0:1, :] * u2 + w[1:2, :] * u1 + w[2:3, :] * u
        dz = dz_ref[...].astype(F32)
        dy = dz * b
        dyn = jnp.where(i < nt - 1, dznext_ref[...].astype(F32) * next_ref[...].astype(F32)[:, :d], 0.0)
        du = w[2:3, :] * dy + w[1:2, :] * _shift_up(dy, dyn, 1) + w[0:1, :] * _shift_up(dy, dyn, 2)
        dp_ref[:, :d] = (dz * y).astype(BF16)
        dp_ref[:, d:2 * d] = (du * h).astype(BF16)
        dp_ref[:, 2 * d:] = (du * c).astype(BF16)

        @pl.when(i == 0)
        def _():
            dw_ref[...] = jnp.zeros_like(dw_ref)

        dw_ref[0:1, :] += jnp.sum(dy * u2, axis=0, keepdims=True)
        dw_ref[1:2, :] += jnp.sum(dy * u1, axis=0, keepdims=True)
        dw_ref[2:3, :] += jnp.sum(dy * u, axis=0, keepdims=True)

    return pl.pallas_call(
        body, name=name, grid=(nt,),
        in_specs=[_row_spec(t, d3),
                  pl.BlockSpec((HALO, d3), lambda i: (jnp.maximum(i * hb - 1, 0), 0)),
                  pl.BlockSpec((HALO, d3), lambda i: (jnp.minimum((i + 1) * hb, last_halo), 0)),
                  _row_spec(t, d),
                  pl.BlockSpec((HALO, d), lambda i: (jnp.minimum((i + 1) * hb, last_halo), 0)),
                  pl.BlockSpec((3, d), lambda i: (0, 0))],
        out_specs=[_row_spec(t, d3), pl.BlockSpec((3, d), lambda i: (0, 0))],
        out_shape=[jax.ShapeDtypeStruct((s, d3), BF16), jax.ShapeDtypeStruct((3, d), F32)],
        compiler_params=_params(("arbitrary",)),
    )(p, p, p, dz, dz, cw)


FFN_ROWS, FFN_COLS = (1024, 512, 256), (256, 128)


def _row_chunks(tm, rows=256):
    return [slice(r, r + min(rows, tm)) for r in range(0, tm, min(rows, tm))]


def _ffn_in_swiglu(xn, w_in, name):
    s, k = xn.shape
    ff = w_in.shape[1] // 2
    tm, tn = _pick(s, FFN_ROWS), _pick(ff, FFN_COLS)
    nj = ff // tn

    def body(x_ref, wg_ref, wu_ref, f_ref, a_ref):
        for rows in _row_chunks(tm):
            xv = x_ref[rows, :]
            gate = jnp.dot(xv, wg_ref[...], preferred_element_type=F32)
            up = jnp.dot(xv, wu_ref[...], preferred_element_type=F32)
            f_ref[0, rows, :] = gate.astype(BF16)
            f_ref[1, rows, :] = up.astype(BF16)
            a_ref[rows, :] = (gate * jax.nn.sigmoid(gate) * up).astype(BF16)

    return pl.pallas_call(
        body, name=name, grid=(s // tm, nj),
        in_specs=[pl.BlockSpec((tm, k), lambda i, j: (i, 0)),
                  pl.BlockSpec((k, tn), lambda i, j: (0, j)),
                  pl.BlockSpec((k, tn), lambda i, j: (0, nj + j))],
        out_specs=[pl.BlockSpec((2, tm, tn), lambda i, j: (0, i, j)), pl.BlockSpec((tm, tn), lambda i, j: (i, j))],
        out_shape=[jax.ShapeDtypeStruct((2, s, ff), BF16), jax.ShapeDtypeStruct((s, ff), BF16)],
        compiler_params=_params(("parallel", "parallel")),
    )(xn, w_in, w_in)


def _ffn_out_dx_swiglu(dff, w_out, f, name):
    s, d = dff.shape
    ff = w_out.shape[0]
    tm, tn = _pick(s, FFN_ROWS), _pick(ff, FFN_COLS)

    def body(d_ref, w_ref, f_ref, df_ref):
        for rows in _row_chunks(tm):
            da = lax.dot_general(d_ref[rows, :], w_ref[...], (((1,), (1,)), ((), ())), preferred_element_type=F32)
            gate = f_ref[0, rows, :].astype(F32)
            up = f_ref[1, rows, :].astype(F32)
            sg = jax.nn.sigmoid(gate)
            silu = gate * sg
            df_ref[0, rows, :] = (da * up * (sg + silu * (1.0 - sg))).astype(BF16)
            df_ref[1, rows, :] = (da * silu).astype(BF16)

    planes = pl.BlockSpec((2, tm, tn), lambda i, j: (0, i, j))
    return pl.pallas_call(
        body, name=name, grid=(s // tm, ff // tn),
        in_specs=[pl.BlockSpec((tm, d), lambda i, j: (i, 0)), pl.BlockSpec((tn, d), lambda i, j: (j, 0)), planes],
        out_specs=planes, out_shape=jax.ShapeDtypeStruct((2, s, ff), BF16),
        compiler_params=_params(("parallel", "parallel")),
    )(dff, w_out, f)


SUPER = 2048
RES = 16
PAIR = 128
L = 128
FWD_TOGETHER = 8
BWD_TOGETHER = 4


def _alibi_slopes(n_heads):
    h = np.arange(n_heads, dtype=np.float32) + 1.0
    return np.power(2.0, -8.0 * h / n_heads).astype(np.float32)


def _permute16(x, inverse, name):
    s, d = x.shape
    cw = LANE

    def body(x_ref, o_ref):
        if inverse:
            for m in range(L):
                o_ref[RES * m:RES * (m + 1), :] = x_ref[pl.ds(m, RES, stride=L), :]
        else:
            for r in range(RES):
                o_ref[L * r:L * (r + 1), :] = x_ref[pl.ds(r, L, stride=RES), :]

    spec = pl.BlockSpec((SUPER, cw), lambda i, j: (i, j))
    return pl.pallas_call(
        body, name=name, grid=(s // SUPER, d // cw), in_specs=[spec], out_specs=spec,
        out_shape=jax.ShapeDtypeStruct((s, d), x.dtype),
        compiler_params=_params(("parallel", "parallel")),
    )(x)


def _slope_table(d):
    nh = d // HEAD_DIM
    sl = _alibi_slopes(nh)
    tab = np.repeat(sl, HEAD_DIM).reshape(d // PAIR, 1, PAIR)
    return jnp.asarray(np.broadcast_to(tab, (d // PAIR, 8, PAIR)).copy())


def _geometry(dil):
    nch = RES // dil
    return nch, L // nch


def _band(dil):
    nch, w = _geometry(dil)
    sh = w.bit_length() - 1
    i = lax.broadcasted_iota(jnp.int32, (L, 2 * L), 0)
    j = lax.broadcasted_iota(jnp.int32, (L, 2 * L), 1)

    def pos(t):
        return jnp.bitwise_and(t, w - 1) * nch + jnp.right_shift(t, sh)

    delta = pos(i) + L - (pos(jnp.bitwise_and(j, L - 1)) + jnp.bitwise_and(j, L))
    return (delta * dil).astype(F32), (delta >= 0) & (delta <= L), j < L


def _fill_bias(bias_s, sl_ref):
    for b, dil in enumerate(DILATIONS):
        base, band, prev_half = _band(dil)
        for first in range(2):
            valid = band & jnp.logical_not(prev_half) if first else band
            for h in range(2):
                slope = sl_ref[0:1, HEAD_DIM * h:HEAD_DIM * h + 1]
                bias_s[(2 * b + first) * 2 + h] = jnp.where(valid, -slope * base, NEG_BIG)


def _bias_index(b, sb, n):
    first = jnp.logical_and(sb == 0, n == 0).astype(jnp.int32)
    return (2 * b + first) * 2


def _offsets(dil, res, n):
    nch, w = _geometry(dil)

    def al(v):
        return v if isinstance(v, int) else pl.multiple_of(v, w)

    q_off = [al((a * dil + res) * L + n * w) for a in range(nch)]
    k_off = [al((a * dil + res) * 2 * L + L + n * w) for a in range(nch)]
    kp_off = [al((a * dil + res) * 2 * L + L + n * w - w) for a in range(nch)]
    return q_off, k_off, kp_off, w


def _gather(ref, offs, w):
    parts = [ref[pl.ds(o, w), :] for o in offs]
    return parts[0] if len(parts) == 1 else jnp.concatenate(parts, axis=0)


def _scatter(ref, offs, w, val, add=False):
    for a, o in enumerate(offs):
        piece = val[a * w:(a + 1) * w, :]
        if add:
            ref[pl.ds(o, w), :] += piece
        else:
            ref[pl.ds(o, w), :] = piece


def _fill_key_buffer(buf, prev_ref, cur_ref):
    for r in range(RES):
        buf[2 * L * r:2 * L * r + L, :] = prev_ref[L * r:L * (r + 1), :]
        buf[2 * L * r + L:2 * L * (r + 1), :] = cur_ref[L * r:L * (r + 1), :]


def _two_heads(x, low):
    zero = jnp.zeros_like(x)
    return jnp.concatenate([jnp.where(low, x, zero), jnp.where(low, zero, x)], axis=0)


def _loop_blocks(dil, stages, together):
    together = max(together, dil) if dil < RES else together

    def it(i, c):
        if dil == RES:
            blocks = [(i * together + k, 0) for k in range(together)]
        else:
            blocks = [(res, i * (together // dil) + k) for k in range(together // dil) for res in range(dil)]
        state = [stages[0](res, n) for res, n in blocks]
        for stage in stages[1:]:
            state = [stage(res, n, prev) for (res, n), prev in zip(blocks, state)]
        for writes in state:
            for args in writes:
                _scatter(*args)
        return c

    lax.fori_loop(0, RES // together, it, 0)


NT = (((1,), (1,)), ((), ()))
TN = (((0,), (0,)), ((), ()))


def _attention_fwd(q, kv, name):
    s, d = q.shape
    g_n, ns = d // PAIR, s // SUPER

    def body(sl_ref, q_ref, kc_ref, kp_ref, vc_ref, vp_ref, o_ref, lse_ref, kbuf, vbuf, m_s, l_s, acc_s, bias_s):
        sb = pl.program_id(1)
        _fill_key_buffer(kbuf, kp_ref, kc_ref)
        _fill_key_buffer(vbuf, vp_ref, vc_ref)
        _fill_bias(bias_s, sl_ref)
        low = lax.broadcasted_iota(jnp.int32, (L, PAIR), 1) < HEAD_DIM
        low_k = lax.broadcasted_iota(jnp.int32, (2 * L, PAIR), 1) < HEAD_DIM
        ones_bd = _two_heads(jnp.ones((2 * L, PAIR), BF16), low_k)
        for bi, dil in enumerate(DILATIONS):
            first_branch, last_branch = bi == 0, bi == len(DILATIONS) - 1

            def scores(res, n, dil=dil):
                q_off, k_off, kp_off, w = _offsets(dil, res, n)
                qf = _gather(q_ref, q_off, w).astype(BF16)
                kcat = jnp.concatenate([_gather(kbuf, kp_off, w), _gather(kbuf, k_off, w)], axis=0).astype(BF16)
                return lax.dot_general(qf, _two_heads(kcat, low_k), NT, preferred_element_type=F32)

            def update(res, n, sc, bi=bi, dil=dil, first_branch=first_branch, last_branch=last_branch):
                q_off, k_off, kp_off, w = _offsets(dil, res, n)
                vcat = jnp.concatenate([_gather(vbuf, kp_off, w), _gather(vbuf, k_off, w)], axis=0).astype(BF16)
                v_ones = jnp.concatenate([_two_heads(vcat, low_k), ones_bd], axis=1)
                bias_at = _bias_index(bi, sb, n)
                if not first_branch:
                    m_prev = _gather(m_s, q_off, w)
                ps, m_new = [], []
                for h in range(2):
                    s_h = sc[:, 2 * L * h:2 * L * (h + 1)] + bias_s[bias_at + h]
                    mh = jnp.max(s_h, axis=1, keepdims=True)
                    if not first_branch:
                        mh = jnp.maximum(mh, m_prev[:, HEAD_DIM * h:HEAD_DIM * h + 1])
                    ps.append(jnp.exp(s_h - mh).astype(BF16))
                    m_new.append(mh)
                m_full = jnp.where(low, m_new[0], m_new[1])
                both = jnp.dot(jnp.concatenate(ps, axis=1), v_ones, preferred_element_type=F32)
                acc, l_full = both[:, :PAIR], both[:, PAIR:]
                if not first_branch:
                    alpha = jnp.exp(m_prev - m_full)
                    l_full = _gather(l_s, q_off, w) * alpha + l_full
                    acc = _gather(acc_s, q_off, w) * alpha + acc
                if last_branch:
                    return [(o_ref, q_off, w, acc / l_full, False), (lse_ref, q_off, w, m_full + jnp.log(l_full), False)]
                return [(m_s, q_off, w, m_full, False), (l_s, q_off, w, l_full, False), (acc_s, q_off, w, acc, False)]

            _loop_blocks(dil, [scores, update], FWD_TOGETHER)

    prev = lambda i: jnp.maximum(i - 1, 0)
    blk = pl.BlockSpec((SUPER, PAIR), lambda g, i: (i, g))
    in_specs = [pl.BlockSpec((None, 8, PAIR), lambda g, i: (g, 0, 0)), blk,
                pl.BlockSpec((SUPER, PAIR), lambda g, i: (i, g)),
                pl.BlockSpec((SUPER, PAIR), lambda g, i: (prev(i), g)),
                pl.BlockSpec((SUPER, PAIR), lambda g, i: (i, g_n + g)),
                pl.BlockSpec((SUPER, PAIR), lambda g, i: (prev(i), g_n + g))]
    return pl.pallas_call(
        body, name=name, grid=(g_n, ns), in_specs=in_specs, out_specs=[blk, blk],
        out_shape=[jax.ShapeDtypeStruct((s, d), F32)] * 2,
        scratch_shapes=([pltpu.VMEM((2 * SUPER, PAIR), F32)] * 2 + [pltpu.VMEM((SUPER, PAIR), F32)] * 3
                        + [pltpu.VMEM((4 * len(DILATIONS), L, 2 * L), F32)]),
        compiler_params=_params(("parallel", "parallel")),
    )(_slope_table(d), q, kv, kv, kv, kv)


def _attention_bwd(q, kv, o, do, lse, dkv_in, name):
    s, d = q.shape
    g_n, ns = d // PAIR, s // SUPER
    has_in = dkv_in is not None

    def body(*refs):
        sl_ref, q_ref, do_ref, o_ref, lse_ref, kc_ref, kp_ref, vc_ref, vp_ref = refs[:9]
        pos = 9
        if has_in:
            dkv_in_ref = refs[9]
            pos = 10
        dq_ref, dkv_ref, kbuf, vbuf, dkbuf, dvbuf, dq_s, bias_s = refs[pos:]
        step = pl.program_id(1)
        sb = ns - 1 - step
        _fill_key_buffer(kbuf, kp_ref, kc_ref)
        _fill_key_buffer(vbuf, vp_ref, vc_ref)
        _fill_bias(bias_s, sl_ref)

        @pl.when(step == 0)
        def _():
            dkbuf[...] = jnp.zeros_like(dkbuf)
            dvbuf[...] = jnp.zeros_like(dvbuf)

        @pl.when(step > 0)
        def _():
            for buf in (dkbuf, dvbuf):
                for r in range(RES):
                    buf[2 * L * r + L:2 * L * (r + 1), :] = buf[2 * L * r:2 * L * r + L, :]
                    buf[2 * L * r:2 * L * r + L, :] = jnp.zeros((L, PAIR), F32)

        low = lax.broadcasted_iota(jnp.int32, (L, PAIR), 1) < HEAD_DIM
        low_k = lax.broadcasted_iota(jnp.int32, (2 * L, PAIR), 1) < HEAD_DIM
        low_t = lax.broadcasted_iota(jnp.int32, (PAIR, 2 * L), 0) < HEAD_DIM
        for bi, dil in enumerate(DILATIONS):
            first_branch = bi == 0

            def scores(res, n, dil=dil):
                q_off, k_off, kp_off, w = _offsets(dil, res, n)
                qb = _gather(q_ref, q_off, w).astype(BF16)
                dof = _gather(do_ref, q_off, w)
                prod = dof * _gather(o_ref, q_off, w)
                dob = dof.astype(BF16)
                lse_f = _gather(lse_ref, q_off, w)
                zero = jnp.zeros_like(prod)
                dsum = (jnp.sum(jnp.where(low, prod, zero), axis=1, keepdims=True),
                        jnp.sum(jnp.where(low, zero, prod), axis=1, keepdims=True))
                kcat = jnp.concatenate([_gather(kbuf, kp_off, w), _gather(kbuf, k_off, w)], axis=0).astype(BF16)
                vcat = jnp.concatenate([_gather(vbuf, kp_off, w), _gather(vbuf, k_off, w)], axis=0).astype(BF16)
                k_bd, v_bd = _two_heads(kcat, low_k), _two_heads(vcat, low_k)
                sc = lax.dot_general(qb, k_bd, NT, preferred_element_type=F32)
                dp = lax.dot_general(dob, v_bd, NT, preferred_element_type=F32)
                return qb, dob, lse_f, dsum, k_bd, sc, dp

            def gradients(res, n, given, bi=bi, dil=dil, first_branch=first_branch):
                qb, dob, lse_f, dsum, k_bd, sc, dp = given
                q_off, k_off, kp_off, w = _offsets(dil, res, n)
                bias_at = _bias_index(bi, sb, n)
                ps, dss = [], []
                for h in range(2):
                    cols = slice(2 * L * h, 2 * L * (h + 1))
                    lse_h = lse_f[:, HEAD_DIM * h:HEAD_DIM * h + 1]
                    p_h = jnp.exp(sc[:, cols] + bias_s[bias_at + h] - lse_h)
                    dss.append((p_h * (dp[:, cols] - dsum[h])).astype(BF16))
                    ps.append(p_h.astype(BF16))
                ds_cat, p_cat = jnp.concatenate(dss, axis=1), jnp.concatenate(ps, axis=1)
                dq = jnp.dot(ds_cat, k_bd, preferred_element_type=F32)
                dk_t = lax.dot_general(qb, ds_cat, TN, preferred_element_type=F32)
                dv_t = lax.dot_general(dob, p_cat, TN, preferred_element_type=F32)
                dk = jnp.where(low_t, dk_t[:, :2 * L], dk_t[:, 2 * L:]).T
                dv = jnp.where(low_t, dv_t[:, :2 * L], dv_t[:, 2 * L:]).T
                return [(dq_s, q_off, w, dq, not first_branch),
                        (dkbuf, kp_off, w, dk[:L], True), (dkbuf, k_off, w, dk[L:], True),
                        (dvbuf, kp_off, w, dv[:L], True), (dvbuf, k_off, w, dv[L:], True)]

            _loop_blocks(dil, [scores, gradients], BWD_TOGETHER)

        dq_ref[...] = dq_s[...].astype(BF16)
        for r in range(RES):
            rows, cur = slice(L * r, L * (r + 1)), slice(2 * L * r + L, 2 * L * (r + 1))
            for plane, buf in enumerate((dkbuf, dvbuf)):
                if has_in:
                    dkv_ref[plane, rows, :] = buf[cur, :] + dkv_in_ref[plane, rows, :]
                else:
                    dkv_ref[plane, rows, :] = buf[cur, :]

    rev = lambda i: ns - 1 - i
    prev = lambda i: jnp.maximum(ns - 2 - i, 0)
    blk = pl.BlockSpec((SUPER, PAIR), lambda g, i: (rev(i), g))
    in_specs = [pl.BlockSpec((None, 8, PAIR), lambda g, i: (g, 0, 0)), blk, blk, blk, blk,
                pl.BlockSpec((SUPER, PAIR), lambda g, i: (rev(i), g)),
                pl.BlockSpec((SUPER, PAIR), lambda g, i: (prev(i), g)),
                pl.BlockSpec((SUPER, PAIR), lambda g, i: (rev(i), g_n + g)),
                pl.BlockSpec((SUPER, PAIR), lambda g, i: (prev(i), g_n + g))]
    ins = [_slope_table(d), q, do, o, lse, kv, kv, kv, kv]
    planes = pl.BlockSpec((2, SUPER, PAIR), lambda g, i: (0, rev(i), g))
    if has_in:
        in_specs.append(planes)
        ins.append(dkv_in)
    res = pl.pallas_call(
        body, name=name, grid=(g_n, ns), in_specs=in_specs, out_specs=[blk, planes],
        out_shape=[jax.ShapeDtypeStruct((s, d), BF16), jax.ShapeDtypeStruct((2, s, d), F32)],
        scratch_shapes=([pltpu.VMEM((2 * SUPER, PAIR), F32)] * 4 + [pltpu.VMEM((SUPER, PAIR), F32)]
                        + [pltpu.VMEM((4 * len(DILATIONS), L, 2 * L), F32)]),
        compiler_params=_params(("parallel", "arbitrary")),
    )(*ins)
    return res[0], res[1]


def _coords():
    return lax.axis_index("x"), lax.axis_index("y"), lax.axis_index("c")


def _chip_peers(x, y):
    return [(1 - x, y), (x, 1 - y), (1 - x, 1 - y)]


def _block_of(ref, axis, blk, size):
    start = pl.multiple_of(blk * size, size)
    if axis == 1:
        return ref.at[:, pl.ds(start, size)]
    return ref.at[pl.ds(start, size), :]


ANY = pl.BlockSpec(memory_space=pl.ANY)


HBM = pl.BlockSpec(memory_space=pltpu.HBM)
SEM = pl.BlockSpec(memory_space=pltpu.SEMAPHORE)
SPLIT = pltpu.CompilerParams(has_side_effects=pltpu.SideEffectType.DATAFLOW_SIDE_EFFECTING)


def _in_hbm(a):
    return pltpu.with_memory_space_constraint(a, pltpu.HBM)


def _thru(arrays):
    return [pltpu.HBM(a.shape, a.dtype) for a in arrays]


def _cast_place(w, layer, ax, dtype, name):
    _, k, n = w.shape
    t = _pick(k, (512, 256, 128))
    nb = k // t

    def body(blk_ref, w_ref, b_ref, f_ref):
        v = w_ref[...].astype(dtype)
        b_ref[...] = v
        f_ref[...] = v

    full_shape = (k, 4 * n) if ax == 1 else (4 * k, n)
    place = (lambda i, blk: (i, blk[0])) if ax == 1 else (lambda i, blk: (blk[0] * nb + i, 0))
    return pl.pallas_call(
        body, name=name,
        grid_spec=pltpu.PrefetchScalarGridSpec(
            num_scalar_prefetch=1, grid=(nb,),
            in_specs=[pl.BlockSpec((None, t, n), lambda i, blk: (layer, i, 0))],
            out_specs=[pl.BlockSpec((t, n), lambda i, blk: (i, 0)), pl.BlockSpec((t, n), place)]),
        out_shape=[jax.ShapeDtypeStruct((k, n), dtype), jax.ShapeDtypeStruct(full_shape, dtype)],
        compiler_params=_params(("parallel",)),
    )(_my_block()[None], w)


def _my_block():
    return (2 * lax.axis_index("x") + lax.axis_index("y")).astype(jnp.int32)


def _gather_start(group, carry, name):
    n, nc = len(group), len(carry)

    def body(*refs):
        blocks, fulls, send_sem, recv_sem = refs[:n], refs[n:2 * n], refs[2 * n + nc], refs[2 * n + nc + 1]
        x, y, c = _coords()
        for t, (b, _, ax) in enumerate(group):
            mine = _block_of(fulls[t], ax, 2 * x + y, b.shape[ax])
            for j, (px, py) in enumerate(_chip_peers(x, y)):
                pltpu.make_async_remote_copy(
                    src_ref=blocks[t], dst_ref=mine, send_sem=send_sem.at[3 * t + j], recv_sem=recv_sem.at[3 * t + j],
                    device_id=(px, py, c), device_id_type=MESH).start()

    arrays = [b for b, _, _ in group] + [f for _, f, _ in group] + list(carry)
    sems = [pltpu.SemaphoreType.DMA((3 * n,))] * 2
    res = pl.pallas_call(
        body, name=name, in_specs=[HBM] * len(arrays), out_specs=[SEM, SEM] + [HBM] * len(arrays),
        out_shape=sems + _thru(arrays), input_output_aliases={i: 2 + i for i in range(len(arrays))},
        compiler_params=SPLIT,
    )(*[_in_hbm(a) for a in arrays])
    return (res[0], res[1], list(res[2:2 + n]), list(res[2 + n:2 + 2 * n])), list(res[2 + 2 * n:])


def _gather_wait(group, started, after, name):
    sends, recvs, blocks, fulls = started
    m = len(group)

    def body(*refs):
        blk_refs, full_refs, send_sem, recv_sem = refs[:m], refs[m:2 * m], refs[2 * m], refs[2 * m + 1]
        x, y, c = _coords()
        for t, (b, _, ax) in enumerate(group):
            for j, (px, py) in enumerate(_chip_peers(x, y)):
                cp = pltpu.make_async_remote_copy(
                    src_ref=blk_refs[t], dst_ref=_block_of(full_refs[t], ax, 2 * px + py, b.shape[ax]),
                    send_sem=send_sem.at[3 * t + j], recv_sem=recv_sem.at[3 * t + j],
                    device_id=(px, py, c), device_id_type=MESH)
                cp.wait_send()
                cp.wait_recv()

    extra = [] if after is None else [after]
    res = pl.pallas_call(
        body, name=name, in_specs=[HBM] * (2 * m) + [SEM, SEM] + [ANY] * len(extra), out_specs=[HBM] * (2 * m),
        out_shape=_thru(blocks) + _thru(fulls), input_output_aliases={i: i for i in range(2 * m)},
        compiler_params=SPLIT,
    )(*blocks, *fulls, sends, recvs, *extra)
    return list(res[m:])


def _scatter_start(grads, carry, name):
    n = len(grads)
    n_in = 2 * n + len(carry)

    def body(*refs):
        g_refs, st_refs, send_sem, recv_sem = refs[:n], refs[n:2 * n], refs[n_in], refs[n_in + 1]
        x, y, c = _coords()
        for t, (g, ax) in enumerate(grads):
            for j, (px, py) in enumerate(_chip_peers(x, y)):
                pltpu.make_async_remote_copy(
                    src_ref=_block_of(g_refs[t], ax, 2 * px + py, g.shape[ax] // 4), dst_ref=st_refs[t].at[j],
                    send_sem=send_sem.at[3 * t + j], recv_sem=recv_sem.at[3 * t + j],
                    device_id=(px, py, c), device_id_type=MESH).start()

    arrays = [g for g, _ in grads]
    for g, ax in grads:
        shape = list(g.shape)
        shape[ax] //= 4
        arrays.append(lax.empty((3, *shape), g.dtype))
    arrays += list(carry)
    sems = [pltpu.SemaphoreType.DMA((3 * n,))] * 2
    res = pl.pallas_call(
        body, name=name, in_specs=[HBM] * n_in, out_specs=[SEM, SEM] + [HBM] * n_in,
        out_shape=sems + _thru(arrays), input_output_aliases={i: 2 + i for i in range(n_in)},
        compiler_params=SPLIT,
    )(*[_in_hbm(a) for a in arrays])
    return (res[0], res[1], list(res[2:2 + n]), list(res[2 + n:2 + 2 * n])), list(res[2 + 2 * n:])


def _scatter_wait(axes, started, after, name):
    sends, recvs, full, stacks = started
    n = len(full)
    extra = [] if after is None else [after]

    def body(*refs):
        g_refs, st_refs, send_sem, recv_sem = refs[:n], refs[n:2 * n], refs[2 * n], refs[2 * n + 1]
        x, y, c = _coords()
        for t, ax in enumerate(axes):
            size = full[t].shape[ax] // 4
            for j, (px, py) in enumerate(_chip_peers(x, y)):
                cp = pltpu.make_async_remote_copy(
                    src_ref=_block_of(g_refs[t], ax, 2 * px + py, size), dst_ref=st_refs[t].at[j],
                    send_sem=send_sem.at[3 * t + j], recv_sem=recv_sem.at[3 * t + j],
                    device_id=(px, py, c), device_id_type=MESH)
                cp.wait_send()
                cp.wait_recv()

    res = pl.pallas_call(
        body, name=name, in_specs=[HBM] * (2 * n) + [SEM, SEM] + [ANY] * len(extra), out_specs=[HBM] * (2 * n),
        out_shape=_thru(full) + _thru(stacks), input_output_aliases={i: i for i in range(2 * n)},
        compiler_params=SPLIT,
    )(*full, *stacks, sends, recvs, *extra)
    return list(res[:n]), list(res[n:])


def _pair_copies(g_refs, st_refs, out_refs, items, send_sem, recv_sem):
    x, y, c = _coords()
    copies = []
    for u, (g, ax, _) in enumerate(items):
        own = _block_of(g_refs[u], ax, 2 * x + y, g.shape[ax] // 4)
        for k, (src, dst) in enumerate([(own, out_refs[u].at[0]), (st_refs[u], out_refs[u].at[pl.ds(1, 3)])]):
            copies.append(pltpu.make_async_remote_copy(
                src_ref=src, dst_ref=dst, send_sem=send_sem.at[2 * u + k], recv_sem=recv_sem.at[2 * u + k],
                device_id=(x, y, 1 - c), device_id_type=MESH))
    return copies


def _pair_start(items, carry, name):
    n = len(items)
    n_in = 3 * n + len(carry)

    def body(*refs):
        for cp in _pair_copies(refs[:n], refs[n:2 * n], refs[2 * n:3 * n], items, refs[n_in], refs[n_in + 1]):
            cp.start()

    arrays = ([g for g, _, _ in items] + [st for _, _, st in items]
              + [lax.empty((4, *st.shape[1:]), st.dtype) for _, _, st in items] + list(carry))
    sems = [pltpu.SemaphoreType.DMA((2 * n,))] * 2
    res = pl.pallas_call(
        body, name=name, in_specs=[HBM] * n_in, out_specs=[SEM, SEM] + [HBM] * n_in,
        out_shape=sems + _thru(arrays), input_output_aliases={i: 2 + i for i in range(n_in)},
        compiler_params=SPLIT,
    )(*[_in_hbm(a) for a in arrays])
    thru = res[2:]
    return (res[0], res[1], *(list(thru[k * n:(k + 1) * n]) for k in range(3))), list(thru[3 * n:])


def _pair_wait(axes, started, name):
    send, recv, full, stacks, landing = started
    n = len(full)
    items = [(full[u], axes[u], stacks[u]) for u in range(n)]

    def body(*refs):
        for cp in _pair_copies(refs[:n], refs[n:2 * n], refs[2 * n:3 * n], items, refs[3 * n], refs[3 * n + 1]):
            cp.wait_send()
            cp.wait_recv()

    res = pl.pallas_call(
        body, name=name, in_specs=[HBM] * (3 * n) + [SEM, SEM], out_specs=[HBM] * (3 * n),
        out_shape=_thru(full + stacks + landing), input_output_aliases={i: i for i in range(3 * n)},
        compiler_params=SPLIT,
    )(*full, *stacks, *landing, send, recv)
    return list(res[:n]), list(res[n:2 * n]), list(res[2 * n:])


def _allreduce_small(v, name):
    r, cdim = v.shape

    def body(v_ref, out_ref, buf, send_sems, recv_sems):
        x, y, c = _coords()
        me = 4 * x + 2 * y + c
        buf[0] = v_ref[...]
        sends = []
        for k in range(1, 8):
            peer = (x if not (k & 4) else 1 - x, y if not (k & 2) else 1 - y, c if not (k & 1) else 1 - c)
            cp = pltpu.make_async_remote_copy(
                src_ref=v_ref, dst_ref=buf.at[k], send_sem=send_sems.at[k - 1], recv_sem=recv_sems.at[k - 1],
                device_id=peer, device_id_type=MESH)
            cp.start()
            sends.append(cp)
        for cp in sends:
            cp.wait_recv()
        total = buf[me]
        for src in range(1, 8):
            total = total + buf[jnp.bitwise_xor(me, src)]
        out_ref[...] = total
        for cp in sends:
            cp.wait_send()

    return pl.pallas_call(
        body, name=name,
        in_specs=[pl.BlockSpec(memory_space=pltpu.VMEM)], out_specs=pl.BlockSpec(memory_space=pltpu.VMEM),
        out_shape=jax.ShapeDtypeStruct((r, cdim), F32),
        scratch_shapes=[pltpu.VMEM((8, r, cdim), F32), pltpu.SemaphoreType.DMA((7,)), pltpu.SemaphoreType.DMA((7,))],
        compiler_params=pltpu.CompilerParams(has_side_effects=True),
    )(v)


def _adamw_math(w, g, m, v):
    m = ADAM_B1 * m + (1.0 - ADAM_B1) * g
    v = ADAM_B2 * v + (1.0 - ADAM_B2) * jnp.square(g)
    m_hat = m / (1.0 - ADAM_B1 ** ADAM_STEP)
    v_hat = v / (1.0 - ADAM_B2 ** ADAM_STEP)
    delta = -ADAM_LR * (m_hat / (jnp.sqrt(v_hat) + ADAM_EPS) + ADAM_WD * w)
    return delta, m, v


def _adamw(w, m, v, grads, name):
    r, cdim = w.shape
    paired = isinstance(grads, list)
    layers = len(grads) if paired else 1
    t = _pick(r // layers, (128, 64, 32, 16, 8))
    per_layer = r // layers // t
    n_grad = 3 * layers if paired else 1

    def body(*refs):
        refs = refs[1:] if paired else refs
        w_ref, m_ref, v_ref = refs[:3]
        outs = refs[3 + n_grad:]

        def update(g):
            delta, m_new, v_new = _adamw_math(w_ref[...], g, m_ref[...], v_ref[...])
            outs[0][...] = g
            outs[1][...] = delta
            outs[2][...] = m_new
            outs[3][...] = v_new

        if not paired:
            update(refs[3][...])
            return
        layer = pl.program_id(0) // per_layer
        for l in range(layers):
            @pl.when(layer == l)
            def _(own_ref=refs[3 + 3 * l], st_ref=refs[4 + 3 * l], sib_ref=refs[5 + 3 * l]):
                sa = own_ref[...].astype(F32)
                sb = sib_ref[0].astype(F32)
                for k in range(3):
                    sa = sa + st_ref[k].astype(F32)
                    sb = sb + sib_ref[k + 1].astype(F32)
                update(sa + sb)

    out_shape = [jax.ShapeDtypeStruct((r, cdim), F32)] * 4
    if not paired:
        spec = pl.BlockSpec((t, cdim), lambda i: (i, 0))
        return pl.pallas_call(
            body, name=name, grid=(r // t,), in_specs=[spec] * 4, out_specs=[spec] * 4, out_shape=out_shape,
            compiler_params=_params(("parallel",)),
        )(w, m, v, grads)

    spec = pl.BlockSpec((t, cdim), lambda i, blk: (i, 0))
    ins, in_specs = [w, m, v], [spec] * 3
    for l, (g, ax, stack, sib) in enumerate(grads):
        row = lambda i, l=l: jnp.clip(i - l * per_layer, 0, per_layer - 1)
        own = ((lambda i, blk, row=row: (row(i), blk[0])) if ax == 1
               else (lambda i, blk, row=row: (blk[0] * per_layer + row(i), 0)))
        ins += [g, stack, sib]
        in_specs += [pl.BlockSpec((t, cdim), own),
                     pl.BlockSpec((3, t, cdim), lambda i, blk, row=row: (0, row(i), 0)),
                     pl.BlockSpec((4, t, cdim), lambda i, blk, row=row: (0, row(i), 0))]
    return pl.pallas_call(
        body, name=name,
        grid_spec=pltpu.PrefetchScalarGridSpec(
            num_scalar_prefetch=1, grid=(r // t,), in_specs=in_specs, out_specs=[spec] * 4),
        out_shape=out_shape, compiler_params=_params(("parallel",)),
    )(_my_block()[None], *ins)


def _local_step(x, target, gains, conv_ws, kv_gain, weights_of, send_grads):
    depth = len(gains)
    n_a = len(conv_ws)
    saved, ws = [], []
    kv = kvn = None
    _, (xn,) = _norm_res_fwd(x, None, None, [gains[0][0]], "norm_first")
    h = x
    for l in range(depth):
        g = gains[l]
        sv = {"x_in": h, "xn": xn}
        w = weights_of(l, "mix", h)
        ws.append(w)
        if l == n_a:
            kv = _mm(kvn, w["kv"], "nn", F32, "kv_fwd")
        if l < n_a:
            p = _mm(xn, w["conv_in"], "nn", BF16, f"conv_in_fwd_{l}")
            z = _conv_gate_fwd(p, conv_ws[l], f"conv_gate_fwd_{l}")
            mix = _mm(z, w["conv_out"], "nn", BF16, f"conv_out_fwd_{l}")
            sv.update(p=p, z=z)
        else:
            j = l - n_a
            q = _mm(xn, w["q"], "nn", F32, f"q_fwd_{j}", scale=HEAD_DIM ** -0.5)
            o, lse = _attention_fwd(q, kv, f"attn_fwd_{j}")
            mix = _mm(o, w["o"], "nn", BF16, f"o_fwd_{j}")
            sv.update(q=q, o=o, lse=lse)
        x1, (xn2,) = _norm_res_fwd(h, mix, g[1], [g[2]], f"norm_mid_{l}")
        w.update(weights_of(l, "ffn", mix))
        f, a = _ffn_in_swiglu(xn2, w["ffn_in"], f"ffn_in_fwd_{l}")
        ff = _mm(a, w["ffn_out"], "nn", BF16, f"ffn_out_fwd_{l}")
        sv.update(mix=mix, x1=x1, xn2=xn2, f=f, a=a, ff=ff)
        saved.append(sv)
        if l == depth - 1:
            dx, loss = _norm_res_loss(x1, ff, g[3], target, "norm_loss")
        else:
            if l == n_a - 1:
                h, _ = _norm_res_fwd(x1, ff, g[3], [], f"norm_end_{l}")
                h = _permute16(h, False, "permute_stream")
                target = _permute16(target, False, "permute_target")
                _, (xn, kvn) = _norm_res_fwd(h, None, None, [gains[l + 1][0], kv_gain], "norm_permuted")
            else:
                h, (xn,) = _norm_res_fwd(x1, ff, g[3], [gains[l + 1][0]], f"norm_end_{l}")
    d_gains = [[None] * 4 for _ in range(depth)]
    d_conv = [None] * n_a
    d_kv_gain = None
    dkv = None
    _, _, dff, d_gains[depth - 1][3] = _norm_bwd(dx, [], None, (saved[-1]["ff"], gains[-1][3]), "norm_loss_bwd")
    for l in reversed(range(depth)):
        sv, g, w, grads = saved[l], gains[l], ws[l], {}
        grads["ffn_out"] =_mm(sv["a"], dff, "tn", BF16, f"ffn_out_dw_{l}")
        df = _ffn_out_dx_swiglu(dff, w["ffn_out"], sv["f"], f"ffn_out_dx_{l}")
        dxn2 = _mm(df, w["ffn_in"], "nt", BF16, f"ffn_in_dx_{l}")
        grads["ffn_in"] =_mm(sv["xn2"], df, "tn", BF16, f"ffn_in_dw_{l}")
        dx, (d_gains[l][2],), dmix, d_gains[l][1] = _norm_bwd(
            dx, [(dxn2, g[2])], sv["x1"], (sv["mix"], g[1]), f"norm_mid_bwd_{l}")
        dx, dmix = send_grads(l, "ffn", grads, [dx, dmix])
        if l < n_a:
            dz = _mm(dmix, w["conv_out"], "nt", BF16, f"conv_out_dx_{l}")
            grads["conv_out"] =_mm(sv["z"], dmix, "tn", BF16, f"conv_out_dw_{l}")
            dp, d_conv[l] = _conv_gate_bwd(sv["p"], dz, conv_ws[l], f"conv_gate_bwd_{l}")
            dxn = _mm(dp, w["conv_in"], "nt", BF16, f"conv_in_dx_{l}")
            grads["conv_in"] =_mm(sv["xn"], dp, "tn", BF16, f"conv_in_dw_{l}")
        else:
            j = l - n_a
            do = _mm(dmix, w["o"], "nt", F32, f"o_dx_{j}")
            grads["o"] =_mm(sv["o"], dmix, "tn", BF16, f"o_dw_{j}")
            dq, dkv = _attention_bwd(sv["q"], kv, sv["o"], do, sv["lse"], dkv, f"attn_bwd_{j}")
            scale = HEAD_DIM ** -0.5
            dxn = _mm(dq, w["q"], "nt", BF16, f"q_dx_{j}", scale=scale)
            grads["q"] =_mm(sv["xn"], dq, "tn", BF16, f"q_dw_{j}", scale=scale)
        branches = [(dxn, g[0])]
        if l == n_a:
            dkvn = _mm(dkv, w["kv"], "nt", BF16, "kv_dx")
            grads["kv"] =_mm(kvn, dkv, "tn", BF16, "kv_dw")
            branches.append((dkvn, kv_gain))
        post = (saved[l - 1]["ff"], gains[l - 1][3]) if l > 0 else None
        if l == n_a:
            dx, dgs, _, _ = _norm_bwd(dx, branches, sv["x_in"], None, f"norm_end_bwd_{l}")
            dx = _permute16(dx, True, "unpermute_stream")
            _, _, dff, dg_post = _norm_bwd(dx, [], None, post, "norm_boundary_bwd")
        else:
            dx, dgs, dff, dg_post = _norm_bwd(dx, branches, sv["x_in"], post, f"norm_end_bwd_{l}")
        if dff is None:
            send_grads(l, "mix", grads, [])
        else:
            dx, dff = send_grads(l, "mix", grads, [dx, dff])
        d_gains[l][0] = dgs[0]
        if l == n_a:
            d_kv_gain = dgs[1]
        if l > 0:
            d_gains[l - 1][3] = dg_post
    return loss, dx, d_gains, d_conv, d_kv_gain


BIG = (
    ("conv_in", 1), ("conv_out", 0), ("kv", 1), ("q", 0), ("o", 0), ("ffn_in", 1), ("ffn_out", 0))


def kernel(x, norm_g, conv_in_w, conv_w, conv_out_w, kv_norm_g, kv_w, q_w, o_w, ffn_in_w, ffn_out_w, loss_target, m_norm_g, m_conv_in_w, m_conv_w, m_conv_out_w, m_kv_norm_g, m_kv_w, m_q_w, m_o_w, m_ffn_in_w, m_ffn_out_w, v_norm_g, v_conv_in_w, v_conv_w, v_conv_out_w, v_kv_norm_g, v_kv_w, v_q_w, v_o_w, v_ffn_in_w, v_ffn_out_w):
    depth, _, dq = norm_g.shape
    d = 4 * dq
    n_a = conv_w.shape[0]
    big_w = {"conv_in": conv_in_w, "conv_out": conv_out_w, "kv": kv_w[None], "q": q_w, "o": o_w,
             "ffn_in": ffn_in_w, "ffn_out": ffn_out_w}
    big_m = {"conv_in": m_conv_in_w, "conv_out": m_conv_out_w, "kv": m_kv_w[None], "q": m_q_w, "o": m_o_w,
             "ffn_in": m_ffn_in_w, "ffn_out": m_ffn_out_w}
    big_v = {"conv_in": v_conv_in_w, "conv_out": v_conv_out_w, "kv": v_kv_w[None], "q": v_q_w, "o": v_o_w,
             "ffn_in": v_ffn_in_w, "ffn_out": v_ffn_out_w}

    n_gain, n_tap = depth * 4, n_a * conv_w.shape[1]
    small_rows = -(-(n_gain + n_tap + 1) // 8) * 8
    pad_rows = small_rows - n_gain - n_tap

    def pack_small(gains, taps):
        return jnp.concatenate([gains.reshape(n_gain, dq), taps.reshape(n_tap, dq), jnp.zeros((pad_rows, dq), F32)])

    axis_of = dict(BIG)

    def matrices_of(l, part):
        if part == "ffn":
            return [("ffn_in", l), ("ffn_out", l)]
        if l < n_a:
            return [("conv_in", l), ("conv_out", l)]
        return ([("kv", 0)] if l == n_a else []) + [("q", l - n_a), ("o", l - n_a)]

    halves = [(l, part) for l in range(depth) for part in ("mix", "ffn")]
    groups = {(l, part): [(*_cast_place(big_w[name], i, axis_of[name], BF16, f"place_{name}_{i}"), axis_of[name])
                          for name, i in matrices_of(l, part)] for l, part in halves}
    groups[halves[0]].append((*_cast_place(pack_small(norm_g, conv_w)[None], 0, 1, F32, "place_small"), 1))
    started = {halves[0]: _gather_start(groups[halves[0]], [], "gather_start_0_mix")[0]}

    def fetch(half, after):
        full = _gather_wait(groups[half], started[half], after, "gather_wait_%d_%s" % half)
        nxt = halves.index(half) + 1
        if nxt < len(halves):
            started[halves[nxt]], full = _gather_start(groups[halves[nxt]], full, "gather_start_%d_%s" % halves[nxt])
        return full

    first = fetch(halves[0], None)
    small = first[-1]
    gains = [[small[4 * l + i][None] for i in range(4)] for l in range(depth)]
    conv_ws = [small[n_gain + 3 * l:n_gain + 3 * l + 3] for l in range(n_a)]
    kv_gain = kv_norm_g[None]

    def weights_of(l, part, after):
        full = first if (l, part) == halves[0] else fetch((l, part), after)
        return {name: full[t] for t, (name, _) in enumerate(matrices_of(l, part))}

    sent, paired = {}, {}
    LAG = 2

    def to_sibling(half, carry, after):
        axes = [axis_of[name] for name, _ in matrices_of(*half)]
        full, stacks = _scatter_wait(axes, sent[half], after, "scatter_wait_%d_%s" % half)
        paired[half], carry = _pair_start(list(zip(full, axes, stacks)), carry, "pair_start_%d_%s" % half)
        return carry

    def send_grads(l, part, grads, carry):
        sent[l, part], carry = _scatter_start(
            [(grads[name], axis_of[name]) for name, _ in matrices_of(l, part)], carry, f"scatter_start_{l}_{part}")
        older = halves.index((l, part)) + LAG
        if older < len(halves) and carry:
            carry = to_sibling(halves[older], carry, carry[0])
        return carry

    loss, dx, d_gains, d_conv, d_kv_gain = _local_step(
        x[0], loss_target[0], gains, conv_ws, kv_gain, weights_of, send_grads)
    loss = lax.psum(loss, ("x", "y", "c"))

    small_g = jnp.concatenate([dg for row in d_gains for dg in row] + list(d_conv) + [d_kv_gain]
                              + [jnp.zeros((pad_rows - 1, d), F32)])
    small_g = _allreduce_small(small_g, "allreduce_small")
    blk = 2 * lax.axis_index("x") + lax.axis_index("y")
    mine_small = lax.dynamic_slice_in_dim(small_g, blk * dq, dq, axis=1)
    kv_rows = d // dq

    def pack_opt(gains_like, taps_like, kv_like):
        rows = jnp.concatenate([gains_like.reshape(n_gain, dq), taps_like.reshape(n_tap, dq), kv_like.reshape(kv_rows, dq)])
        extra = -rows.shape[0] % 8
        return jnp.concatenate([rows, jnp.zeros((extra, dq), F32)]) if extra else rows

    sw = pack_opt(norm_g, conv_w, kv_norm_g)
    sm = pack_opt(m_norm_g, m_conv_w, m_kv_norm_g)
    sv = pack_opt(v_norm_g, v_conv_w, v_kv_norm_g)
    sg = pack_opt(mine_small[:n_gain], mine_small[n_gain:n_gain + n_tap], small_g[n_gain + n_tap])
    s_out = _adamw(sw, sm, sv, sg, "adamw_small")

    def unpack(a):
        return (a[:n_gain].reshape(depth, 4, dq), a[n_gain:n_gain + n_tap].reshape(n_a, -1, dq),
                a[n_gain + n_tap:n_gain + n_tap + kv_rows].reshape(d))

    small_out = [unpack(a) for a in s_out]

    landed, big_out = {}, {}

    def update(name):
        for half in halves:
            if matrices_of(*half)[0] not in landed and any(n == name for n, _ in matrices_of(*half)):
                axes = [axis_of[n] for n, _ in matrices_of(*half)]
                landed.update(zip(matrices_of(*half), zip(*_pair_wait(axes, paired[half], "pair_wait_%d_%s" % half))))
        shp, ax = big_w[name].shape, axis_of[name]
        rows, cols = shp[0] * shp[1], shp[2]
        flat = lambda a: a.reshape(rows, cols)
        full, stacks, sibling = zip(*[landed[name, i] for i in range(shp[0])])
        res = _adamw(flat(big_w[name]), flat(big_m[name]), flat(big_v[name]),
                     [(full[i], ax, stacks[i], sibling[i]) for i in range(shp[0])], f"adamw_{name}")
        big_out[name] = [a.reshape(shp[1:] if name == "kv" else shp) for a in res]

    pending = [half for half in reversed(halves) if half not in paired]
    for half in pending[:-1]:
        to_sibling(half, [], None)
    late = [name for name, _ in BIG if any(n == name for n, _ in matrices_of(*pending[-1]))]
    for name, _ in BIG:
        if name not in late:
            update(name)
    to_sibling(pending[-1], [], big_out["ffn_out"][0])
    for name in late:
        update(name)

    def leaves(i):
        ng, cw_, kg = small_out[i]
        return [ng, big_out["conv_in"][i], cw_, big_out["conv_out"][i], kg, big_out["kv"][i], big_out["q"][i],
                big_out["o"][i], big_out["ffn_in"][i], big_out["ffn_out"][i]]

    return (loss, dx[None], *leaves(0), *leaves(1), *leaves(2), *leaves(3))
```

```python
import functools

import jax
import jax.numpy as jnp
import numpy as np
from jax import lax
from jax.experimental import pallas as pl
from jax.experimental.pallas import tpu as pltpu

F32 = jnp.float32
BF16 = jnp.bfloat16
HEAD_DIM = 64
DILATIONS = (1, 4, 16)
NORM_EPS = 1e-6
NEG_BIG = -1e30
VMEM_LIMIT = 48 * 1024 * 1024
ROW_TILE = 256
NORM_TILE = 512
LANE = 128
MESH = pl.DeviceIdType.MESH

ADAM_LR = 0.001
ADAM_B1 = 0.9
ADAM_B2 = 0.999
ADAM_EPS = 1e-08
ADAM_WD = 0.01
ADAM_STEP = 10

TILE_CANDIDATES = (1024, 1408, 768, 512, 384, 256, 128)


def _pick(dim, cands=TILE_CANDIDATES):
    for c in cands:
        if c <= dim and dim % c == 0:
            return c
    return dim


def _params(sem):
    return pltpu.CompilerParams(dimension_semantics=sem, vmem_limit_bytes=VMEM_LIMIT)


def _mm(a, b, mode, out_dtype, name, scale=None):
    a_planes = a.shape[0] if a.ndim == 3 else 1
    b_planes = b.shape[0] if b.ndim == 3 else 1
    if mode == "nn":
        m, k = a.shape[-2], a.shape[-1] * a_planes
        n = b.shape[1]
    elif mode == "nt":
        m, k = a.shape[-2], a.shape[-1] * a_planes
        n = b.shape[0]
    else:
        k, m = a.shape
        n = b.shape[-1] * b_planes
    tm, tn, tk = _pick(m), _pick(n // b_planes), _pick(k // a_planes, ((2048,) if mode == "tn" else ()) + TILE_CANDIDATES)
    nk = k // tk
    ka, nb = k // a_planes // tk, n // b_planes // tn
    if a_planes > 1:
        a_spec = pl.BlockSpec((None, tm, tk), lambda i, j, kk: (kk // ka, i, kk % ka))
    elif mode == "tn":
        a_spec = pl.BlockSpec((tk, tm), lambda i, j, kk: (kk, i))
    else:
        a_spec = pl.BlockSpec((tm, tk), lambda i, j, kk: (i, kk))
    if mode == "nn":
        b_spec = pl.BlockSpec((tk, tn), lambda i, j, kk: (kk, j))
        dims = (((1,), (0,)), ((), ()))
    elif mode == "nt":
        b_spec = pl.BlockSpec((tn, tk), lambda i, j, kk: (j, kk))
        dims = (((1,), (1,)), ((), ()))
    else:
        b_spec = (pl.BlockSpec((None, tk, tn), lambda i, j, kk: (j // nb, kk, j % nb)) if b_planes > 1
                  else pl.BlockSpec((tk, tn), lambda i, j, kk: (kk, j)))
        dims = (((0,), (0,)), ((), ()))

    def finish(acc):
        if scale is not None:
            acc = acc * scale
        return acc.astype(out_dtype)

    if nk == 1:
        def body(a_ref, b_ref, o_ref):
            o_ref[...] = finish(lax.dot_general(a_ref[...].astype(BF16), b_ref[...].astype(BF16), dims, preferred_element_type=F32))
        scratch = []
    else:
        def body(a_ref, b_ref, o_ref, acc_ref):
            kk = pl.program_id(2)

            @pl.when(kk == 0)
            def _():
                acc_ref[...] = jnp.zeros_like(acc_ref)

            acc_ref[...] += lax.dot_general(a_ref[...].astype(BF16), b_ref[...].astype(BF16), dims, preferred_element_type=F32)

            @pl.when(kk == nk - 1)
            def _():
                o_ref[...] = finish(acc_ref[...])
        scratch = [pltpu.VMEM((tm, tn), F32)]

    return pl.pallas_call(
        body, name=name,
        grid=(m // tm, n // tn, nk),
        in_specs=[a_spec, b_spec],
        out_specs=pl.BlockSpec((tm, tn), lambda i, j, kk: (i, j)),
        out_shape=jax.ShapeDtypeStruct((m, n), out_dtype),
        scratch_shapes=scratch,
        compiler_params=_params(("parallel", "parallel", "arbitrary")),
    )(a, b)


def _rstd(v):
    return lax.rsqrt(jnp.mean(v * v, axis=-1, keepdims=True) + NORM_EPS)


def _rms_bwd(dy, v, g, r):
    gy = dy * g
    dv = r * (gy - v * (r * r) * jnp.mean(gy * v, axis=-1, keepdims=True))
    return dv, dy * v * r


def _row_spec(t, width):
    return pl.BlockSpec((t, width), lambda i: (i, 0))


def _gain_spec(width):
    return pl.BlockSpec((1, width), lambda i: (0, 0))


def _norm_res_fwd(x, mix, g_post, pre_gains, name):
    s, d = x.shape
    t = _pick(s, (NORM_TILE, ROW_TILE))
    has_mix = mix is not None
    n_pre = len(pre_gains)

    def body(*refs):
        x_ref = refs[0]
        pos = 1
        x1 = x_ref[...]
        if has_mix:
            mv = refs[1][...].astype(F32)
            x1 = x1 + mv * _rstd(mv) * refs[2][...]
            pos = 3
        gains = refs[pos:pos + n_pre]
        outs = refs[pos + n_pre:]
        if has_mix:
            outs[0][...] = x1
            outs = outs[1:]
        r = _rstd(x1)
        for g_ref, o_ref in zip(gains, outs):
            o_ref[...] = (x1 * r * g_ref[...]).astype(BF16)

    ins = [x] + ([mix, g_post] if has_mix else []) + list(pre_gains)
    in_specs = [_row_spec(t, d)] + ([_row_spec(t, d), _gain_spec(d)] if has_mix else []) + [_gain_spec(d)] * n_pre
    out_shape = ([jax.ShapeDtypeStruct((s, d), F32)] if has_mix else []) + [jax.ShapeDtypeStruct((s, d), BF16)] * n_pre
    out_specs = [_row_spec(t, d)] * len(out_shape)
    res = pl.pallas_call(
        body, name=name, grid=(s // t,), in_specs=in_specs, out_specs=out_specs, out_shape=out_shape,
        compiler_params=_params(("parallel",)),
    )(*ins)
    if has_mix:
        return res[0], list(res[1:])
    return x, list(res)


def _norm_res_loss(x, mix, g_post, target, name):
    s, d = x.shape
    t = _pick(s, (NORM_TILE, ROW_TILE))

    def body(x_ref, m_ref, g_ref, t_ref, dy_ref, loss_ref):
        mv = m_ref[...].astype(F32)
        y = x_ref[...] + mv * _rstd(mv) * g_ref[...]
        err = y - t_ref[...]
        dy_ref[...] = err * (1.0 / d)

        @pl.when(pl.program_id(0) == 0)
        def _():
            loss_ref[...] = jnp.zeros_like(loss_ref)

        loss_ref[...] += jnp.sum(err * err)

    dy, acc = pl.pallas_call(
        body, name=name, grid=(s // t,),
        in_specs=[_row_spec(t, d), _row_spec(t, d), _gain_spec(d), _row_spec(t, d)],
        out_specs=[_row_spec(t, d), pl.BlockSpec((8, LANE), lambda i: (0, 0))],
        out_shape=[jax.ShapeDtypeStruct((s, d), F32), jax.ShapeDtypeStruct((8, LANE), F32)],
        compiler_params=_params(("arbitrary",)),
    )(x, mix, g_post, target)
    return dy, acc[0, 0] * (0.5 / d)


def _norm_bwd(dx_out, branches, x_in, post, name):
    s, d = dx_out.shape
    t = _pick(s, (NORM_TILE, ROW_TILE))
    nb = len(branches)
    has_post = post is not None

    def body(*refs):
        dx_ref = refs[0]
        pos = 1
        dx = dx_ref[...]
        first = pl.program_id(0) == 0
        n_in = 1 + (1 + 2 * nb if nb else 0) + (2 if has_post else 0)
        outs = refs[n_in:]
        opos = 0
        if nb:
            xv = refs[pos][...]
            pos += 1
            r = _rstd(xv)
            dx_o = outs[0]
            opos = 1
            for _ in range(nb):
                dxn = refs[pos][...].astype(F32)
                g = refs[pos + 1][...]
                pos += 2
                dv, dg_rows = _rms_bwd(dxn, xv, g, r)
                dx = dx + dv
                dg_ref = outs[opos]
                opos += 1

                @pl.when(first)
                def _(dg_ref=dg_ref):
                    dg_ref[...] = jnp.zeros_like(dg_ref)

                dg_ref[...] += jnp.sum(dg_rows, axis=0, keepdims=True)
            dx_o[...] = dx
        if has_post:
            mv = refs[pos][...].astype(F32)
            g = refs[pos + 1][...]
            dm, dg_rows = _rms_bwd(dx, mv, g, _rstd(mv))
            outs[opos][...] = dm.astype(BF16)
            dg_ref = outs[opos + 1]

            @pl.when(first)
            def _():
                dg_ref[...] = jnp.zeros_like(dg_ref)

            dg_ref[...] += jnp.sum(dg_rows, axis=0, keepdims=True)

    ins, in_specs = [dx_out], [_row_spec(t, d)]
    out_shape, out_specs = [], []
    if nb:
        ins.append(x_in)
        in_specs.append(_row_spec(t, d))
        out_shape.append(jax.ShapeDtypeStruct((s, d), F32))
        out_specs.append(_row_spec(t, d))
        for dxn, g in branches:
            ins += [dxn, g]
            in_specs += [_row_spec(t, d), _gain_spec(d)]
            out_shape.append(jax.ShapeDtypeStruct((1, d), F32))
            out_specs.append(_gain_spec(d))
    if has_post:
        ins += [post[0], post[1]]
        in_specs += [_row_spec(t, d), _gain_spec(d)]
        out_shape += [jax.ShapeDtypeStruct((s, d), BF16), jax.ShapeDtypeStruct((1, d), F32)]
        out_specs += [_row_spec(t, d), _gain_spec(d)]
    res = pl.pallas_call(
        body, name=name, grid=(s // t,), in_specs=in_specs, out_specs=out_specs, out_shape=out_shape,
        compiler_params=_params(("arbitrary",)),
    )(*ins)
    res = list(res)
    dx_in = res.pop(0) if nb else dx_out
    dgs = [res.pop(0) for _ in range(nb)]
    dm, dg_post = (res[0], res[1]) if has_post else (None, None)
    return dx_in, dgs, dm, dg_post


HALO = 16


def _shift_down(u, prev, k):
    rows = lax.broadcasted_iota(jnp.int32, u.shape, 0)
    out = pltpu.roll(u, k, 0)
    for i in range(k):
        out = jnp.where(rows == i, prev[HALO - k + i:HALO - k + i + 1, :], out)
    return out


def _shift_up(u, nxt, k):
    n = u.shape[0]
    rows = lax.broadcasted_iota(jnp.int32, u.shape, 0)
    out = pltpu.roll(u, n - k, 0)
    for i in range(k):
        out = jnp.where(rows == n - k + i, nxt[i:i + 1, :], out)
    return out


def _conv_gate_fwd(p, cw, name):
    s, d3 = p.shape
    d = d3 // 3
    t = _pick(s, (ROW_TILE,))
    hb = t // HALO

    def body(p_ref, prev_ref, w_ref, z_ref):
        i = pl.program_id(0)
        pv = p_ref[...].astype(F32)
        b, u = pv[:, :d], pv[:, d:2 * d] * pv[:, 2 * d:]
        ph = prev_ref[...].astype(F32)
        up = jnp.where(i > 0, ph[:, d:2 * d] * ph[:, 2 * d:], 0.0)
        w = w_ref[...]
        y = w[0:1, :] * _shift_down(u, up, 2) + w[1:2, :] * _shift_down(u, up, 1) + w[2:3, :] * u
        z_ref[...] = (b * y).astype(BF16)

    return pl.pallas_call(
        body, name=name, grid=(s // t,),
        in_specs=[_row_spec(t, d3),
                  pl.BlockSpec((HALO, d3), lambda i: (jnp.maximum(i * hb - 1, 0), 0)),
                  pl.BlockSpec((3, d), lambda i: (0, 0))],
        out_specs=_row_spec(t, d),
        out_shape=jax.ShapeDtypeStruct((s, d), BF16),
        compiler_params=_params(("parallel",)),
    )(p, p, cw)


def _conv_gate_bwd(p, dz, cw, name):
    s, d3 = p.shape
    d = d3 // 3
    t = _pick(s, (ROW_TILE,))
    hb = t // HALO
    nt = s // t
    last_halo = s // HALO - 1

    def body(p_ref, prev_ref, next_ref, dz_ref, dznext_ref, w_ref, dp_ref, dw_ref):
        i = pl.program_id(0)
        pv = p_ref[...].astype(F32)
        b, c, h = pv[:, :d], pv[:, d:2 * d], pv[:, 2 * d:]
        u = c * h
        ph = prev_ref[...].astype(F32)
        up = jnp.where(i > 0, ph[:, d:2 * d] * ph[:, 2 * d:], 0.0)
        w = w_ref[...]
        u1, u2 = _shift_down(u, up, 1), _shift_down(u, up, 2)
        y = w[0:1, :] * u2 + w[1:2, :] * u1 + w[2:3, :] * u
        dz = dz_ref[...].astype(F32)
        dy = dz * b
        dyn = jnp.where(i < nt - 1, dznext_ref[...].astype(F32) * next_ref[...].astype(F32)[:, :d], 0.0)
        du = w[2:3, :] * dy + w[1:2, :] * _shift_up(dy, dyn, 1) + w[0:1, :] * _shift_up(dy, dyn, 2)
        dp_ref[:, :d] = (dz * y).astype(BF16)
        dp_ref[:, d:2 * d] = (du * h).astype(BF16)
        dp_ref[:, 2 * d:] = (du * c).astype(BF16)

        @pl.when(i == 0)
        def _():
            dw_ref[...] = jnp.zeros_like(dw_ref)

        dw_ref[0:1, :] += jnp.sum(dy * u2, axis=0, keepdims=True)
        dw_ref[1:2, :] += jnp.sum(dy * u1, axis=0, keepdims=True)
        dw_ref[2:3, :] += jnp.sum(dy * u, axis=0, keepdims=True)

    return pl.pallas_call(
        body, name=name, grid=(nt,),
        in_specs=[_row_spec(t, d3),
                  pl.BlockSpec((HALO, d3), lambda i: (jnp.maximum(i * hb - 1, 0), 0)),
                  pl.BlockSpec((HALO, d3), lambda i: (jnp.minimum((i + 1) * hb, last_halo), 0)),
                  _row_spec(t, d),
                  pl.BlockSpec((HALO, d), lambda i: (jnp.minimum((i + 1) * hb, last_halo), 0)),
                  pl.BlockSpec((3, d), lambda i: (0, 0))],
        out_specs=[_row_spec(t, d3), pl.BlockSpec((3, d), lambda i: (0, 0))],
        out_shape=[jax.ShapeDtypeStruct((s, d3), BF16), jax.ShapeDtypeStruct((3, d), F32)],
        compiler_params=_params(("arbitrary",)),
    )(p, p, p, dz, dz, cw)


FFN_ROWS, FFN_COLS = (512, 256), (1408, 768, 256, 128)


def _row_chunks(tm, rows=256):
    return [slice(r, r + min(rows, tm)) for r in range(0, tm, min(rows, tm))]


def _ffn_in_swiglu(xn, w_in, name):
    s, k = xn.shape
    ff = w_in.shape[1] // 2
    tm, tn = _pick(s, FFN_ROWS), _pick(ff, FFN_COLS)
    nj = ff // tn

    def body(x_ref, wg_ref, wu_ref, f_ref, a_ref):
        for rows in _row_chunks(tm):
            xv = x_ref[rows, :]
            gate = jnp.dot(xv, wg_ref[...], preferred_element_type=F32)
            up = jnp.dot(xv, wu_ref[...], preferred_element_type=F32)
            f_ref[0, rows, :] = gate.astype(BF16)
            f_ref[1, rows, :] = up.astype(BF16)
            a_ref[rows, :] = (gate * jax.nn.sigmoid(gate) * up).astype(BF16)

    return pl.pallas_call(
        body, name=name, grid=(nj, s // tm),
        in_specs=[pl.BlockSpec((tm, k), lambda j, i: (i, 0)),
                  pl.BlockSpec((k, tn), lambda j, i: (0, j)),
                  pl.BlockSpec((k, tn), lambda j, i: (0, nj + j))],
        out_specs=[pl.BlockSpec((2, tm, tn), lambda j, i: (0, i, j)), pl.BlockSpec((tm, tn), lambda j, i: (i, j))],
        out_shape=[jax.ShapeDtypeStruct((2, s, ff), BF16), jax.ShapeDtypeStruct((s, ff), BF16)],
        compiler_params=_params(("parallel", "parallel")),
    )(xn, w_in, w_in)


def _ffn_out_dx_swiglu(dff, w_out, f, name):
    s, d = dff.shape
    ff = w_out.shape[0]
    tm, tn = _pick(s, FFN_ROWS), _pick(ff, FFN_COLS)

    def body(d_ref, w_ref, f_ref, df_ref):
        for rows in _row_chunks(tm):
            da = lax.dot_general(d_ref[rows, :], w_ref[...], (((1,), (1,)), ((), ())), preferred_element_type=F32)
            gate = f_ref[0, rows, :].astype(F32)
            up = f_ref[1, rows, :].astype(F32)
            sg = jax.nn.sigmoid(gate)
            silu = gate * sg
            df_ref[0, rows, :] = (da * up * (sg + silu * (1.0 - sg))).astype(BF16)
            df_ref[1, rows, :] = (da * silu).astype(BF16)

    planes = pl.BlockSpec((2, tm, tn), lambda j, i: (0, i, j))
    return pl.pallas_call(
        body, name=name, grid=(ff // tn, s // tm),
        in_specs=[pl.BlockSpec((tm, d), lambda j, i: (i, 0)), pl.BlockSpec((tn, d), lambda j, i: (j, 0)), planes],
        out_specs=planes, out_shape=jax.ShapeDtypeStruct((2, s, ff), BF16),
        compiler_params=_params(("parallel", "parallel")),
    )(dff, w_out, f)


SUPER = 2048
RES = 16
PAIR = 128
L = 128
FWD_TOGETHER = 8
BWD_TOGETHER = 4


def _alibi_slopes(n_heads):
    h = np.arange(n_heads, dtype=np.float32) + 1.0
    return np.power(2.0, -8.0 * h / n_heads).astype(np.float32)


def _permute16(x, inverse, name):
    s, d = x.shape
    cw = LANE

    def body(x_ref, o_ref):
        if inverse:
            for m in range(L):
                o_ref[RES * m:RES * (m + 1), :] = x_ref[pl.ds(m, RES, stride=L), :]
        else:
            for r in range(RES):
                o_ref[L * r:L * (r + 1), :] = x_ref[pl.ds(r, L, stride=RES), :]

    spec = pl.BlockSpec((SUPER, cw), lambda i, j: (i, j))
    return pl.pallas_call(
        body, name=name, grid=(s // SUPER, d // cw), in_specs=[spec], out_specs=spec,
        out_shape=jax.ShapeDtypeStruct((s, d), x.dtype),
        compiler_params=_params(("parallel", "parallel")),
    )(x)


def _slope_table(d):
    nh = d // HEAD_DIM
    sl = _alibi_slopes(nh)
    tab = np.repeat(sl, HEAD_DIM).reshape(d // PAIR, 1, PAIR)
    return jnp.asarray(np.broadcast_to(tab, (d // PAIR, 8, PAIR)).copy())


def _geometry(dil):
    nch = RES // dil
    return nch, L // nch


def _band(dil):
    nch, w = _geometry(dil)
    sh = w.bit_length() - 1
    i = lax.broadcasted_iota(jnp.int32, (L, 2 * L), 0)
    j = lax.broadcasted_iota(jnp.int32, (L, 2 * L), 1)

    def pos(t):
        return jnp.bitwise_and(t, w - 1) * nch + jnp.right_shift(t, sh)

    delta = pos(i) + L - (pos(jnp.bitwise_and(j, L - 1)) + jnp.bitwise_and(j, L))
    return (delta * dil).astype(F32), (delta >= 0) & (delta <= L), j < L


def _fill_bias(bias_s, sl_ref):
    for b, dil in enumerate(DILATIONS):
        base, band, prev_half = _band(dil)
        for first in range(2):
            valid = band & jnp.logical_not(prev_half) if first else band
            for h in range(2):
                slope = sl_ref[0:1, HEAD_DIM * h:HEAD_DIM * h + 1]
                bias_s[(2 * b + first) * 2 + h] = jnp.where(valid, -slope * base, NEG_BIG)


def _bias_index(b, sb, n):
    first = jnp.logical_and(sb == 0, n == 0).astype(jnp.int32)
    return (2 * b + first) * 2


def _offsets(dil, res, n):
    nch, w = _geometry(dil)

    def al(v):
        return v if isinstance(v, int) else pl.multiple_of(v, w)

    q_off = [al((a * dil + res) * L + n * w) for a in range(nch)]
    k_off = [al((a * dil + res) * 2 * L + L + n * w) for a in range(nch)]
    kp_off = [al((a * dil + res) * 2 * L + L + n * w - w) for a in range(nch)]
    return q_off, k_off, kp_off, w


def _gather(ref, offs, w):
    parts = [ref[pl.ds(o, w), :] for o in offs]
    return parts[0] if len(parts) == 1 else jnp.concatenate(parts, axis=0)


def _scatter(ref, offs, w, val, add=False):
    for a, o in enumerate(offs):
        piece = val[a * w:(a + 1) * w, :]
        if add:
            ref[pl.ds(o, w), :] += piece
        else:
            ref[pl.ds(o, w), :] = piece


def _fill_key_buffer(buf, prev_ref, cur_ref):
    for r in range(RES):
        buf[2 * L * r:2 * L * r + L, :] = prev_ref[L * r:L * (r + 1), :]
        buf[2 * L * r + L:2 * L * (r + 1), :] = cur_ref[L * r:L * (r + 1), :]


def _two_heads(x, low):
    zero = jnp.zeros_like(x)
    return jnp.concatenate([jnp.where(low, x, zero), jnp.where(low, zero, x)], axis=0)


def _loop_blocks(dil, stages, together):
    together = max(together, dil) if dil < RES else together

    def it(i, c):
        if dil == RES:
            blocks = [(i * together + k, 0) for k in range(together)]
        else:
            blocks = [(res, i * (together // dil) + k) for k in range(together // dil) for res in range(dil)]
        state = [stages[0](res, n) for res, n in blocks]
        for stage in stages[1:]:
            state = [stage(res, n, prev) for (res, n), prev in zip(blocks, state)]
        for writes in state:
            for args in writes:
                _scatter(*args)
        return c

    lax.fori_loop(0, RES // together, it, 0)


NT = (((1,), (1,)), ((), ()))
TN = (((0,), (0,)), ((), ()))


def _attention_fwd(q, kv, name):
    s, d = q.shape
    g_n, ns = d // PAIR, s // SUPER

    def body(sl_ref, q_ref, kc_ref, kp_ref, vc_ref, vp_ref, o_ref, lse_ref, kbuf, vbuf, m_s, l_s, acc_s, bias_s):
        sb = pl.program_id(1)
        _fill_key_buffer(kbuf, kp_ref, kc_ref)
        _fill_key_buffer(vbuf, vp_ref, vc_ref)
        _fill_bias(bias_s, sl_ref)
        low = lax.broadcasted_iota(jnp.int32, (L, PAIR), 1) < HEAD_DIM
        low_k = lax.broadcasted_iota(jnp.int32, (2 * L, PAIR), 1) < HEAD_DIM
        ones_bd = _two_heads(jnp.ones((2 * L, PAIR), BF16), low_k)
        for bi, dil in enumerate(DILATIONS):
            first_branch, last_branch = bi == 0, bi == len(DILATIONS) - 1

            def scores(res, n, dil=dil):
                q_off, k_off, kp_off, w = _offsets(dil, res, n)
                qf = _gather(q_ref, q_off, w).astype(BF16)
                kcat = jnp.concatenate([_gather(kbuf, kp_off, w), _gather(kbuf, k_off, w)], axis=0).astype(BF16)
                return lax.dot_general(qf, _two_heads(kcat, low_k), NT, preferred_element_type=F32)

            def update(res, n, sc, bi=bi, dil=dil, first_branch=first_branch, last_branch=last_branch):
                q_off, k_off, kp_off, w = _offsets(dil, res, n)
                vcat = jnp.concatenate([_gather(vbuf, kp_off, w), _gather(vbuf, k_off, w)], axis=0).astype(BF16)
                v_ones = jnp.concatenate([_two_heads(vcat, low_k), ones_bd], axis=1)
                bias_at = _bias_index(bi, sb, n)
                if not first_branch:
                    m_prev = _gather(m_s, q_off, w)
                ps, m_new = [], []
                for h in range(2):
                    s_h = sc[:, 2 * L * h:2 * L * (h + 1)] + bias_s[bias_at + h]
                    mh = jnp.max(s_h, axis=1, keepdims=True)
                    if not first_branch:
                        mh = jnp.maximum(mh, m_prev[:, HEAD_DIM * h:HEAD_DIM * h + 1])
                    ps.append(jnp.exp(s_h - mh).astype(BF16))
                    m_new.append(mh)
                m_full = jnp.where(low, m_new[0], m_new[1])
                both = jnp.dot(jnp.concatenate(ps, axis=1), v_ones, preferred_element_type=F32)
                acc, l_full = both[:, :PAIR], both[:, PAIR:]
                if not first_branch:
                    alpha = jnp.exp(m_prev - m_full)
                    l_full = _gather(l_s, q_off, w) * alpha + l_full
                    acc = _gather(acc_s, q_off, w) * alpha + acc
                if last_branch:
                    return [(o_ref, q_off, w, acc / l_full, False), (lse_ref, q_off, w, m_full + jnp.log(l_full), False)]
                return [(m_s, q_off, w, m_full, False), (l_s, q_off, w, l_full, False), (acc_s, q_off, w, acc, False)]

            _loop_blocks(dil, [scores, update], FWD_TOGETHER)

    prev = lambda i: jnp.maximum(i - 1, 0)
    blk = pl.BlockSpec((SUPER, PAIR), lambda g, i: (i, g))
    in_specs = [pl.BlockSpec((None, 8, PAIR), lambda g, i: (g, 0, 0)), blk,
                pl.BlockSpec((SUPER, PAIR), lambda g, i: (i, g)),
                pl.BlockSpec((SUPER, PAIR), lambda g, i: (prev(i), g)),
                pl.BlockSpec((SUPER, PAIR), lambda g, i: (i, g_n + g)),
                pl.BlockSpec((SUPER, PAIR), lambda g, i: (prev(i), g_n + g))]
    return pl.pallas_call(
        body, name=name, grid=(g_n, ns), in_specs=in_specs, out_specs=[blk, blk],
        out_shape=[jax.ShapeDtypeStruct((s, d), F32)] * 2,
        scratch_shapes=([pltpu.VMEM((2 * SUPER, PAIR), F32)] * 2 + [pltpu.VMEM((SUPER, PAIR), F32)] * 3
                        + [pltpu.VMEM((4 * len(DILATIONS), L, 2 * L), F32)]),
        compiler_params=_params(("parallel", "parallel")),
    )(_slope_table(d), q, kv, kv, kv, kv)


def _attention_bwd(q, kv, o, do, lse, dkv_in, name):
    s, d = q.shape
    g_n, ns = d // PAIR, s // SUPER
    has_in = dkv_in is not None

    def body(*refs):
        sl_ref, q_ref, do_ref, o_ref, lse_ref, kc_ref, kp_ref, vc_ref, vp_ref = refs[:9]
        pos = 9
        if has_in:
            dkv_in_ref = refs[9]
            pos = 10
        dq_ref, dkv_ref, kbuf, vbuf, dkbuf, dvbuf, dq_s, bias_s = refs[pos:]
        step = pl.program_id(1)
        sb = ns - 1 - step
        _fill_key_buffer(kbuf, kp_ref, kc_ref)
        _fill_key_buffer(vbuf, vp_ref, vc_ref)
        _fill_bias(bias_s, sl_ref)

        @pl.when(step == 0)
        def _():
            dkbuf[...] = jnp.zeros_like(dkbuf)
            dvbuf[...] = jnp.zeros_like(dvbuf)

        @pl.when(step > 0)
        def _():
            for buf in (dkbuf, dvbuf):
                for r in range(RES):
                    buf[2 * L * r + L:2 * L * (r + 1), :] = buf[2 * L * r:2 * L * r + L, :]
                    buf[2 * L * r:2 * L * r + L, :] = jnp.zeros((L, PAIR), F32)

        low = lax.broadcasted_iota(jnp.int32, (L, PAIR), 1) < HEAD_DIM
        low_k = lax.broadcasted_iota(jnp.int32, (2 * L, PAIR), 1) < HEAD_DIM
        low_t = lax.broadcasted_iota(jnp.int32, (PAIR, 2 * L), 0) < HEAD_DIM
        for bi, dil in enumerate(DILATIONS):
            first_branch = bi == 0

            def scores(res, n, dil=dil):
                q_off, k_off, kp_off, w = _offsets(dil, res, n)
                qb = _gather(q_ref, q_off, w).astype(BF16)
                dof = _gather(do_ref, q_off, w)
                prod = dof * _gather(o_ref, q_off, w)
                dob = dof.astype(BF16)
                lse_f = _gather(lse_ref, q_off, w)
                zero = jnp.zeros_like(prod)
                dsum = (jnp.sum(jnp.where(low, prod, zero), axis=1, keepdims=True),
                        jnp.sum(jnp.where(low, zero, prod), axis=1, keepdims=True))
                kcat = jnp.concatenate([_gather(kbuf, kp_off, w), _gather(kbuf, k_off, w)], axis=0).astype(BF16)
                vcat = jnp.concatenate([_gather(vbuf, kp_off, w), _gather(vbuf, k_off, w)], axis=0).astype(BF16)
                k_bd, v_bd = _two_heads(kcat, low_k), _two_heads(vcat, low_k)
                sc = lax.dot_general(qb, k_bd, NT, preferred_element_type=F32)
                dp = lax.dot_general(dob, v_bd, NT, preferred_element_type=F32)
                return qb, dob, lse_f, dsum, k_bd, sc, dp

            def gradients(res, n, given, bi=bi, dil=dil, first_branch=first_branch):
                qb, dob, lse_f, dsum, k_bd, sc, dp = given
                q_off, k_off, kp_off, w = _offsets(dil, res, n)
                bias_at = _bias_index(bi, sb, n)
                ps, dss = [], []
                for h in range(2):
                    cols = slice(2 * L * h, 2 * L * (h + 1))
                    lse_h = lse_f[:, HEAD_DIM * h:HEAD_DIM * h + 1]
                    p_h = jnp.exp(sc[:, cols] + bias_s[bias_at + h] - lse_h)
                    dss.append((p_h * (dp[:, cols] - dsum[h])).astype(BF16))
                    ps.append(p_h.astype(BF16))
                ds_cat, p_cat = jnp.concatenate(dss, axis=1), jnp.concatenate(ps, axis=1)
                dq = jnp.dot(ds_cat, k_bd, preferred_element_type=F32)
                dk_t = lax.dot_general(qb, ds_cat, TN, preferred_element_type=F32)
                dv_t = lax.dot_general(dob, p_cat, TN, preferred_element_type=F32)
                dk = jnp.where(low_t, dk_t[:, :2 * L], dk_t[:, 2 * L:]).T
                dv = jnp.where(low_t, dv_t[:, :2 * L], dv_t[:, 2 * L:]).T
                return [(dq_s, q_off, w, dq, not first_branch),
                        (dkbuf, kp_off, w, dk[:L], True), (dkbuf, k_off, w, dk[L:], True),
                        (dvbuf, kp_off, w, dv[:L], True), (dvbuf, k_off, w, dv[L:], True)]

            _loop_blocks(dil, [scores, gradients], BWD_TOGETHER)

        dq_ref[...] = dq_s[...].astype(BF16)
        for r in range(RES):
            rows, cur = slice(L * r, L * (r + 1)), slice(2 * L * r + L, 2 * L * (r + 1))
            for plane, buf in enumerate((dkbuf, dvbuf)):
                if has_in:
                    dkv_ref[plane, rows, :] = buf[cur, :] + dkv_in_ref[plane, rows, :]
                else:
                    dkv_ref[plane, rows, :] = buf[cur, :]

    rev = lambda i: ns - 1 - i
    prev = lambda i: jnp.maximum(ns - 2 - i, 0)
    blk = pl.BlockSpec((SUPER, PAIR), lambda g, i: (rev(i), g))
    in_specs = [pl.BlockSpec((None, 8, PAIR), lambda g, i: (g, 0, 0)), blk, blk, blk, blk,
                pl.BlockSpec((SUPER, PAIR), lambda g, i: (rev(i), g)),
                pl.BlockSpec((SUPER, PAIR), lambda g, i: (prev(i), g)),
                pl.BlockSpec((SUPER, PAIR), lambda g, i: (rev(i), g_n + g)),
                pl.BlockSpec((SUPER, PAIR), lambda g, i: (prev(i), g_n + g))]
    ins = [_slope_table(d), q, do, o, lse, kv, kv, kv, kv]
    planes = pl.BlockSpec((2, SUPER, PAIR), lambda g, i: (0, rev(i), g))
    if has_in:
        in_specs.append(planes)
        ins.append(dkv_in)
    res = pl.pallas_call(
        body, name=name, grid=(g_n, ns), in_specs=in_specs, out_specs=[blk, planes],
        out_shape=[jax.ShapeDtypeStruct((s, d), BF16), jax.ShapeDtypeStruct((2, s, d), F32)],
        scratch_shapes=([pltpu.VMEM((2 * SUPER, PAIR), F32)] * 4 + [pltpu.VMEM((SUPER, PAIR), F32)]
                        + [pltpu.VMEM((4 * len(DILATIONS), L, 2 * L), F32)]),
        compiler_params=_params(("parallel", "arbitrary")),
    )(*ins)
    return res[0], res[1]


def _coords():
    return lax.axis_index("x"), lax.axis_index("y"), lax.axis_index("c")


def _chip_peers(x, y):
    return [(1 - x, y), (x, 1 - y), (1 - x, 1 - y)]


def _block_of(ref, axis, blk, size):
    start = pl.multiple_of(blk * size, size)
    if axis == 1:
        return ref.at[:, pl.ds(start, size)]
    return ref.at[pl.ds(start, size), :]


ANY = pl.BlockSpec(memory_space=pl.ANY)


HBM = pl.BlockSpec(memory_space=pltpu.HBM)
SEM = pl.BlockSpec(memory_space=pltpu.SEMAPHORE)
SPLIT = pltpu.CompilerParams(has_side_effects=pltpu.SideEffectType.DATAFLOW_SIDE_EFFECTING)


def _in_hbm(a):
    return pltpu.with_memory_space_constraint(a, pltpu.HBM)


def _thru(arrays):
    return [pltpu.HBM(a.shape, a.dtype) for a in arrays]


def _cast_place(w, layer, ax, dtype, name):
    _, k, n = w.shape
    t = _pick(k, (512, 256, 128))
    nb = k // t

    def body(blk_ref, w_ref, b_ref, f_ref):
        v = w_ref[...].astype(dtype)
        b_ref[...] = v
        f_ref[...] = v

    full_shape = (k, 4 * n) if ax == 1 else (4 * k, n)
    place = (lambda i, blk: (i, blk[0])) if ax == 1 else (lambda i, blk: (blk[0] * nb + i, 0))
    return pl.pallas_call(
        body, name=name,
        grid_spec=pltpu.PrefetchScalarGridSpec(
            num_scalar_prefetch=1, grid=(nb,),
            in_specs=[pl.BlockSpec((None, t, n), lambda i, blk: (layer, i, 0))],
            out_specs=[pl.BlockSpec((t, n), lambda i, blk: (i, 0)), pl.BlockSpec((t, n), place)]),
        out_shape=[jax.ShapeDtypeStruct((k, n), dtype), jax.ShapeDtypeStruct(full_shape, dtype)],
        compiler_params=_params(("parallel",)),
    )(_my_block()[None], w)


def _my_block():
    return (2 * lax.axis_index("x") + lax.axis_index("y")).astype(jnp.int32)


def _gather_start(group, carry, name):
    n, nc = len(group), len(carry)

    def body(*refs):
        blocks, fulls, send_sem, recv_sem = refs[:n], refs[n:2 * n], refs[2 * n + nc], refs[2 * n + nc + 1]
        x, y, c = _coords()
        for t, (b, _, ax) in enumerate(group):
            mine = _block_of(fulls[t], ax, 2 * x + y, b.shape[ax])
            for j, (px, py) in enumerate(_chip_peers(x, y)):
                pltpu.make_async_remote_copy(
                    src_ref=blocks[t], dst_ref=mine, send_sem=send_sem.at[3 * t + j], recv_sem=recv_sem.at[3 * t + j],
                    device_id=(px, py, c), device_id_type=MESH).start()

    arrays = [b for b, _, _ in group] + [f for _, f, _ in group] + list(carry)
    sems = [pltpu.SemaphoreType.DMA((3 * n,))] * 2
    res = pl.pallas_call(
        body, name=name, in_specs=[HBM] * len(arrays), out_specs=[SEM, SEM] + [HBM] * len(arrays),
        out_shape=sems + _thru(arrays), input_output_aliases={i: 2 + i for i in range(len(arrays))},
        compiler_params=SPLIT,
    )(*[_in_hbm(a) for a in arrays])
    return (res[0], res[1], list(res[2:2 + n]), list(res[2 + n:2 + 2 * n])), list(res[2 + 2 * n:])


def _gather_wait(group, started, after, name):
    sends, recvs, blocks, fulls = started
    m = len(group)

    def body(*refs):
        blk_refs, full_refs, send_sem, recv_sem = refs[:m], refs[m:2 * m], refs[2 * m], refs[2 * m + 1]
        x, y, c = _coords()
        for t, (b, _, ax) in enumerate(group):
            for j, (px, py) in enumerate(_chip_peers(x, y)):
                cp = pltpu.make_async_remote_copy(
                    src_ref=blk_refs[t], dst_ref=_block_of(full_refs[t], ax, 2 * px + py, b.shape[ax]),
                    send_sem=send_sem.at[3 * t + j], recv_sem=recv_sem.at[3 * t + j],
                    device_id=(px, py, c), device_id_type=MESH)
                cp.wait_send()
                cp.wait_recv()

    extra = [] if after is None else [after]
    res = pl.pallas_call(
        body, name=name, in_specs=[HBM] * (2 * m) + [SEM, SEM] + [ANY] * len(extra), out_specs=[HBM] * (2 * m),
        out_shape=_thru(blocks) + _thru(fulls), input_output_aliases={i: i for i in range(2 * m)},
        compiler_params=SPLIT,
    )(*blocks, *fulls, sends, recvs, *extra)
    return list(res[m:])


def _scatter_start(grads, carry, name):
    n = len(grads)
    n_in = 2 * n + len(carry)

    def body(*refs):
        g_refs, st_refs, send_sem, recv_sem = refs[:n], refs[n:2 * n], refs[n_in], refs[n_in + 1]
        x, y, c = _coords()
        for t, (g, ax) in enumerate(grads):
            for j, (px, py) in enumerate(_chip_peers(x, y)):
                pltpu.make_async_remote_copy(
                    src_ref=_block_of(g_refs[t], ax, 2 * px + py, g.shape[ax] // 4), dst_ref=st_refs[t].at[j],
                    send_sem=send_sem.at[3 * t + j], recv_sem=recv_sem.at[3 * t + j],
                    device_id=(px, py, c), device_id_type=MESH).start()

    arrays = [g for g, _ in grads]
    for g, ax in grads:
        shape = list(g.shape)
        shape[ax] //= 4
        arrays.append(lax.empty((3, *shape), g.dtype))
    arrays += list(carry)
    sems = [pltpu.SemaphoreType.DMA((3 * n,))] * 2
    res = pl.pallas_call(
        body, name=name, in_specs=[HBM] * n_in, out_specs=[SEM, SEM] + [HBM] * n_in,
        out_shape=sems + _thru(arrays), input_output_aliases={i: 2 + i for i in range(n_in)},
        compiler_params=SPLIT,
    )(*[_in_hbm(a) for a in arrays])
    return (res[0], res[1], list(res[2:2 + n]), list(res[2 + n:2 + 2 * n])), list(res[2 + 2 * n:])


def _scatter_wait(axes, started, after, name):
    sends, recvs, full, stacks = started
    n = len(full)
    extra = [] if after is None else [after]

    def body(*refs):
        g_refs, st_refs, send_sem, recv_sem = refs[:n], refs[n:2 * n], refs[2 * n], refs[2 * n + 1]
        x, y, c = _coords()
        for t, ax in enumerate(axes):
            size = full[t].shape[ax] // 4
            for j, (px, py) in enumerate(_chip_peers(x, y)):
                cp = pltpu.make_async_remote_copy(
                    src_ref=_block_of(g_refs[t], ax, 2 * px + py, size), dst_ref=st_refs[t].at[j],
                    send_sem=send_sem.at[3 * t + j], recv_sem=recv_sem.at[3 * t + j],
                    device_id=(px, py, c), device_id_type=MESH)
                cp.wait_send()
                cp.wait_recv()

    res = pl.pallas_call(
        body, name=name, in_specs=[HBM] * (2 * n) + [SEM, SEM] + [ANY] * len(extra), out_specs=[HBM] * (2 * n),
        out_shape=_thru(full) + _thru(stacks), input_output_aliases={i: i for i in range(2 * n)},
        compiler_params=SPLIT,
    )(*full, *stacks, sends, recvs, *extra)
    return list(res[:n]), list(res[n:])


def _pair_copies(g_refs, st_refs, out_refs, items, send_sem, recv_sem):
    x, y, c = _coords()
    copies = []
    for u, (g, ax, _) in enumerate(items):
        own = _block_of(g_refs[u], ax, 2 * x + y, g.shape[ax] // 4)
        for k, (src, dst) in enumerate([(own, out_refs[u].at[0]), (st_refs[u], out_refs[u].at[pl.ds(1, 3)])]):
            copies.append(pltpu.make_async_remote_copy(
                src_ref=src, dst_ref=dst, send_sem=send_sem.at[2 * u + k], recv_sem=recv_sem.at[2 * u + k],
                device_id=(x, y, 1 - c), device_id_type=MESH))
    return copies


def _pair_start(items, carry, name):
    n = len(items)
    n_in = 3 * n + len(carry)

    def body(*refs):
        for cp in _pair_copies(refs[:n], refs[n:2 * n], refs[2 * n:3 * n], items, refs[n_in], refs[n_in + 1]):
            cp.start()

    arrays = ([g for g, _, _ in items] + [st for _, _, st in items]
              + [lax.empty((4, *st.shape[1:]), st.dtype) for _, _, st in items] + list(carry))
    sems = [pltpu.SemaphoreType.DMA((2 * n,))] * 2
    res = pl.pallas_call(
        body, name=name, in_specs=[HBM] * n_in, out_specs=[SEM, SEM] + [HBM] * n_in,
        out_shape=sems + _thru(arrays), input_output_aliases={i: 2 + i for i in range(n_in)},
        compiler_params=SPLIT,
    )(*[_in_hbm(a) for a in arrays])
    thru = res[2:]
    return (res[0], res[1], *(list(thru[k * n:(k + 1) * n]) for k in range(3))), list(thru[3 * n:])


def _pair_wait(axes, started, name):
    send, recv, full, stacks, landing = started
    n = len(full)
    items = [(full[u], axes[u], stacks[u]) for u in range(n)]

    def body(*refs):
        for cp in _pair_copies(refs[:n], refs[n:2 * n], refs[2 * n:3 * n], items, refs[3 * n], refs[3 * n + 1]):
            cp.wait_send()
            cp.wait_recv()

    res = pl.pallas_call(
        body, name=name, in_specs=[HBM] * (3 * n) + [SEM, SEM], out_specs=[HBM] * (3 * n),
        out_shape=_thru(full + stacks + landing), input_output_aliases={i: i for i in range(3 * n)},
        compiler_params=SPLIT,
    )(*full, *stacks, *landing, send, recv)
    return list(res[:n]), list(res[n:2 * n]), list(res[2 * n:])


def _allreduce_small(v, name):
    r, cdim = v.shape

    def body(v_ref, out_ref, buf, send_sems, recv_sems):
        x, y, c = _coords()
        me = 4 * x + 2 * y + c
        buf[0] = v_ref[...]
        sends = []
        for k in range(1, 8):
            peer = (x if not (k & 4) else 1 - x, y if not (k & 2) else 1 - y, c if not (k & 1) else 1 - c)
            cp = pltpu.make_async_remote_copy(
                src_ref=v_ref, dst_ref=buf.at[k], send_sem=send_sems.at[k - 1], recv_sem=recv_sems.at[k - 1],
                device_id=peer, device_id_type=MESH)
            cp.start()
            sends.append(cp)
        for cp in sends:
            cp.wait_recv()
        total = buf[me]
        for src in range(1, 8):
            total = total + buf[jnp.bitwise_xor(me, src)]
        out_ref[...] = total
        for cp in sends:
            cp.wait_send()

    return pl.pallas_call(
        body, name=name,
        in_specs=[pl.BlockSpec(memory_space=pltpu.VMEM)], out_specs=pl.BlockSpec(memory_space=pltpu.VMEM),
        out_shape=jax.ShapeDtypeStruct((r, cdim), F32),
        scratch_shapes=[pltpu.VMEM((8, r, cdim), F32), pltpu.SemaphoreType.DMA((7,)), pltpu.SemaphoreType.DMA((7,))],
        compiler_params=pltpu.CompilerParams(has_side_effects=True),
    )(v)


def _adamw_math(w, g, m, v):
    m = ADAM_B1 * m + (1.0 - ADAM_B1) * g
    v = ADAM_B2 * v + (1.0 - ADAM_B2) * jnp.square(g)
    m_hat = m / (1.0 - ADAM_B1 ** ADAM_STEP)
    v_hat = v / (1.0 - ADAM_B2 ** ADAM_STEP)
    delta = -ADAM_LR * (m_hat / (jnp.sqrt(v_hat) + ADAM_EPS) + ADAM_WD * w)
    return delta, m, v


def _adamw(w, m, v, grads, name):
    r, cdim = w.shape
    paired = isinstance(grads, list)
    layers = len(grads) if paired else 1
    t = _pick(r // layers, (128, 64, 32, 16, 8))
    per_layer = r // layers // t
    n_grad = 3 * layers if paired else 1

    def body(*refs):
        refs = refs[1:] if paired else refs
        w_ref, m_ref, v_ref = refs[:3]
        outs = refs[3 + n_grad:]

        def update(g):
            delta, m_new, v_new = _adamw_math(w_ref[...], g, m_ref[...], v_ref[...])
            outs[0][...] = g
            outs[1][...] = delta
            outs[2][...] = m_new
            outs[3][...] = v_new

        if not paired:
            update(refs[3][...])
            return
        layer = pl.program_id(0) // per_layer
        for l in range(layers):
            @pl.when(layer == l)
            def _(own_ref=refs[3 + 3 * l], st_ref=refs[4 + 3 * l], sib_ref=refs[5 + 3 * l]):
                sa = own_ref[...].astype(F32)
                sb = sib_ref[0].astype(F32)
                for k in range(3):
                    sa = sa + st_ref[k].astype(F32)
                    sb = sb + sib_ref[k + 1].astype(F32)
                update(sa + sb)

    out_shape = [jax.ShapeDtypeStruct((r, cdim), F32)] * 4
    if not paired:
        spec = pl.BlockSpec((t, cdim), lambda i: (i, 0))
        return pl.pallas_call(
            body, name=name, grid=(r // t,), in_specs=[spec] * 4, out_specs=[spec] * 4, out_shape=out_shape,
            compiler_params=_params(("parallel",)),
        )(w, m, v, grads)

    spec = pl.BlockSpec((t, cdim), lambda i, blk: (i, 0))
    ins, in_specs = [w, m, v], [spec] * 3
    for l, (g, ax, stack, sib) in enumerate(grads):
        row = lambda i, l=l: jnp.clip(i - l * per_layer, 0, per_layer - 1)
        own = ((lambda i, blk, row=row: (row(i), blk[0])) if ax == 1
               else (lambda i, blk, row=row: (blk[0] * per_layer + row(i), 0)))
        ins += [g, stack, sib]
        in_specs += [pl.BlockSpec((t, cdim), own),
                     pl.BlockSpec((3, t, cdim), lambda i, blk, row=row: (0, row(i), 0)),
                     pl.BlockSpec((4, t, cdim), lambda i, blk, row=row: (0, row(i), 0))]
    return pl.pallas_call(
        body, name=name,
        grid_spec=pltpu.PrefetchScalarGridSpec(
            num_scalar_prefetch=1, grid=(r // t,), in_specs=in_specs, out_specs=[spec] * 4),
        out_shape=out_shape, compiler_params=_params(("parallel",)),
    )(_my_block()[None], *ins)


def _local_step(x, target, gains, conv_ws, kv_gain, weights_of, send_grads):
    depth = len(gains)
    n_a = len(conv_ws)
    saved, ws = [], []
    kv = kvn = None
    _, (xn,) = _norm_res_fwd(x, None, None, [gains[0][0]], "norm_first")
    h = x
    for l in range(depth):
        g = gains[l]
        sv = {"x_in": h, "xn": xn}
        w = weights_of(l, "mix", h)
        ws.append(w)
        if l == n_a:
            kv = _mm(kvn, w["kv"], "nn", F32, "kv_fwd")
        if l < n_a:
            p = _mm(xn, w["conv_in"], "nn", BF16, f"conv_in_fwd_{l}")
            z = _conv_gate_fwd(p, conv_ws[l], f"conv_gate_fwd_{l}")
            mix = _mm(z, w["conv_out"], "nn", BF16, f"conv_out_fwd_{l}")
            sv.update(p=p, z=z)
        else:
            j = l - n_a
            q = _mm(xn, w["q"], "nn", F32, f"q_fwd_{j}", scale=HEAD_DIM ** -0.5)
            o, lse = _attention_fwd(q, kv, f"attn_fwd_{j}")
            mix = _mm(o, w["o"], "nn", BF16, f"o_fwd_{j}")
            sv.update(q=q, o=o, lse=lse)
        x1, (xn2,) = _norm_res_fwd(h, mix, g[1], [g[2]], f"norm_mid_{l}")
        w.update(weights_of(l, "ffn", mix))
        f, a = _ffn_in_swiglu(xn2, w["ffn_in"], f"ffn_in_fwd_{l}")
        ff = _mm(a, w["ffn_out"], "nn", BF16, f"ffn_out_fwd_{l}")
        sv.update(mix=mix, x1=x1, xn2=xn2, f=f, a=a, ff=ff)
        saved.append(sv)
        if l == depth - 1:
            dx, loss = _norm_res_loss(x1, ff, g[3], target, "norm_loss")
        else:
            if l == n_a - 1:
                h, _ = _norm_res_fwd(x1, ff, g[3], [], f"norm_end_{l}")
                h = _permute16(h, False, "permute_stream")
                target = _permute16(target, False, "permute_target")
                _, (xn, kvn) = _norm_res_fwd(h, None, None, [gains[l + 1][0], kv_gain], "norm_permuted")
            else:
                h, (xn,) = _norm_res_fwd(x1, ff, g[3], [gains[l + 1][0]], f"norm_end_{l}")
    d_gains = [[None] * 4 for _ in range(depth)]
    d_conv = [None] * n_a
    d_kv_gain = None
    dkv = None
    _, _, dff, d_gains[depth - 1][3] = _norm_bwd(dx, [], None, (saved[-1]["ff"], gains[-1][3]), "norm_loss_bwd")
    for l in reversed(range(depth)):
        sv, g, w, grads = saved[l], gains[l], ws[l], {}
        grads["ffn_out"] =_mm(sv["a"], dff, "tn", BF16, f"ffn_out_dw_{l}")
        df = _ffn_out_dx_swiglu(dff, w["ffn_out"], sv["f"], f"ffn_out_dx_{l}")
        dxn2 = _mm(df, w["ffn_in"], "nt", BF16, f"ffn_in_dx_{l}")
        grads["ffn_in"] =_mm(sv["xn2"], df, "tn", BF16, f"ffn_in_dw_{l}")
        dx, (d_gains[l][2],), dmix, d_gains[l][1] = _norm_bwd(
            dx, [(dxn2, g[2])], sv["x1"], (sv["mix"], g[1]), f"norm_mid_bwd_{l}")
        dx, dmix = send_grads(l, "ffn", grads, [dx, dmix])
        if l < n_a:
            dz = _mm(dmix, w["conv_out"], "nt", BF16, f"conv_out_dx_{l}")
            grads["conv_out"] =_mm(sv["z"], dmix, "tn", BF16, f"conv_out_dw_{l}")
            dp, d_conv[l] = _conv_gate_bwd(sv["p"], dz, conv_ws[l], f"conv_gate_bwd_{l}")
            dxn = _mm(dp, w["conv_in"], "nt", BF16, f"conv_in_dx_{l}")
            grads["conv_in"] =_mm(sv["xn"], dp, "tn", BF16, f"conv_in_dw_{l}")
        else:
            j = l - n_a
            do = _mm(dmix, w["o"], "nt", F32, f"o_dx_{j}")
            grads["o"] =_mm(sv["o"], dmix, "tn", BF16, f"o_dw_{j}")
            dq, dkv = _attention_bwd(sv["q"], kv, sv["o"], do, sv["lse"], dkv, f"attn_bwd_{j}")
            scale = HEAD_DIM ** -0.5
            dxn = _mm(dq, w["q"], "nt", BF16, f"q_dx_{j}", scale=scale)
            grads["q"] =_mm(sv["xn"], dq, "tn", BF16, f"q_dw_{j}", scale=scale)
        branches = [(dxn, g[0])]
        if l == n_a:
            dkvn = _mm(dkv, w["kv"], "nt", BF16, "kv_dx")
            grads["kv"] =_mm(kvn, dkv, "tn", BF16, "kv_dw")
            branches.append((dkvn, kv_gain))
        post = (saved[l - 1]["ff"], gains[l - 1][3]) if l > 0 else None
        if l == n_a:
            dx, dgs, _, _ = _norm_bwd(dx, branches, sv["x_in"], None, f"norm_end_bwd_{l}")
            dx = _permute16(dx, True, "unpermute_stream")
            _, _, dff, dg_post = _norm_bwd(dx, [], None, post, "norm_boundary_bwd")
        else:
            dx, dgs, dff, dg_post = _norm_bwd(dx, branches, sv["x_in"], post, f"norm_end_bwd_{l}")
        if dff is None:
            send_grads(l, "mix", grads, [])
        else:
            dx, dff = send_grads(l, "mix", grads, [dx, dff])
        d_gains[l][0] = dgs[0]
        if l == n_a:
            d_kv_gain = dgs[1]
        if l > 0:
            d_gains[l - 1][3] = dg_post
    return loss, dx, d_gains, d_conv, d_kv_gain


BIG = (
    ("conv_in", 1), ("conv_out", 0), ("kv", 1), ("q", 0), ("o", 0), ("ffn_in", 1), ("ffn_out", 0))


def kernel(x, norm_g, conv_in_w, conv_w, conv_out_w, kv_norm_g, kv_w, q_w, o_w, ffn_in_w, ffn_out_w, loss_target, m_norm_g, m_conv_in_w, m_conv_w, m_conv_out_w, m_kv_norm_g, m_kv_w, m_q_w, m_o_w, m_ffn_in_w, m_ffn_out_w, v_norm_g, v_conv_in_w, v_conv_w, v_conv_out_w, v_kv_norm_g, v_kv_w, v_q_w, v_o_w, v_ffn_in_w, v_ffn_out_w):
    depth, _, dq = norm_g.shape
    d = 4 * dq
    n_a = conv_w.shape[0]
    big_w = {"conv_in": conv_in_w, "conv_out": conv_out_w, "kv": kv_w[None], "q": q_w, "o": o_w,
             "ffn_in": ffn_in_w, "ffn_out": ffn_out_w}
    big_m = {"conv_in": m_conv_in_w, "conv_out": m_conv_out_w, "kv": m_kv_w[None], "q": m_q_w, "o": m_o_w,
             "ffn_in": m_ffn_in_w, "ffn_out": m_ffn_out_w}
    big_v = {"conv_in": v_conv_in_w, "conv_out": v_conv_out_w, "kv": v_kv_w[None], "q": v_q_w, "o": v_o_w,
             "ffn_in": v_ffn_in_w, "ffn_out": v_ffn_out_w}

    n_gain, n_tap = depth * 4, n_a * conv_w.shape[1]
    small_rows = -(-(n_gain + n_tap + 1) // 8) * 8
    pad_rows = small_rows - n_gain - n_tap

    def pack_small(gains, taps):
        return jnp.concatenate([gains.reshape(n_gain, dq), taps.reshape(n_tap, dq), jnp.zeros((pad_rows, dq), F32)])

    axis_of = dict(BIG)

    def matrices_of(l, part):
        if part == "ffn":
            return [("ffn_in", l), ("ffn_out", l)]
        if l < n_a:
            return [("conv_in", l), ("conv_out", l)]
        return ([("kv", 0)] if l == n_a else []) + [("q", l - n_a), ("o", l - n_a)]

    halves = [(l, part) for l in range(depth) for part in ("mix", "ffn")]
    groups = {(l, part): [(*_cast_place(big_w[name], i, axis_of[name], BF16, f"place_{name}_{i}"), axis_of[name])
                          for name, i in matrices_of(l, part)] for l, part in halves}
    groups[halves[0]].append((*_cast_place(pack_small(norm_g, conv_w)[None], 0, 1, F32, "place_small"), 1))
    started = {halves[0]: _gather_start(groups[halves[0]], [], "gather_start_0_mix")[0]}

    def fetch(half, after):
        full = _gather_wait(groups[half], started[half], after, "gather_wait_%d_%s" % half)
        nxt = halves.index(half) + 1
        if nxt < len(halves):
            started[halves[nxt]], full = _gather_start(groups[halves[nxt]], full, "gather_start_%d_%s" % halves[nxt])
        return full

    first = fetch(halves[0], None)
    small = first[-1]
    gains = [[small[4 * l + i][None] for i in range(4)] for l in range(depth)]
    conv_ws = [small[n_gain + 3 * l:n_gain + 3 * l + 3] for l in range(n_a)]
    kv_gain = kv_norm_g[None]

    def weights_of(l, part, after):
        full = first if (l, part) == halves[0] else fetch((l, part), after)
        return {name: full[t] for t, (name, _) in enumerate(matrices_of(l, part))}

    sent, paired = {}, {}
    LAG = 2

    def to_sibling(half, carry, after):
        axes = [axis_of[name] for name, _ in matrices_of(*half)]
        full, stacks = _scatter_wait(axes, sent[half], after, "scatter_wait_%d_%s" % half)
        paired[half], carry = _pair_start(list(zip(full, axes, stacks)), carry, "pair_start_%d_%s" % half)
        return carry

    def send_grads(l, part, grads, carry):
        sent[l, part], carry = _scatter_start(
            [(grads[name], axis_of[name]) for name, _ in matrices_of(l, part)], carry, f"scatter_start_{l}_{part}")
        older = halves.index((l, part)) + LAG
        if older < len(halves) and carry:
            carry = to_sibling(halves[older], carry, carry[0])
        return carry

    loss, dx, d_gains, d_conv, d_kv_gain = _local_step(
        x[0], loss_target[0], gains, conv_ws, kv_gain, weights_of, send_grads)
    loss = lax.psum(loss, ("x", "y", "c"))

    small_g = jnp.concatenate([dg for row in d_gains for dg in row] + list(d_conv) + [d_kv_gain]
                              + [jnp.zeros((pad_rows - 1, d), F32)])
    small_g = _allreduce_small(small_g, "allreduce_small")
    blk = 2 * lax.axis_index("x") + lax.axis_index("y")
    mine_small = lax.dynamic_slice_in_dim(small_g, blk * dq, dq, axis=1)
    kv_rows = d // dq

    def pack_opt(gains_like, taps_like, kv_like):
        rows = jnp.concatenate([gains_like.reshape(n_gain, dq), taps_like.reshape(n_tap, dq), kv_like.reshape(kv_rows, dq)])
        extra = -rows.shape[0] % 8
        return jnp.concatenate([rows, jnp.zeros((extra, dq), F32)]) if extra else rows

    sw = pack_opt(norm_g, conv_w, kv_norm_g)
    sm = pack_opt(m_norm_g, m_conv_w, m_kv_norm_g)
    sv = pack_opt(v_norm_g, v_conv_w, v_kv_norm_g)
    sg = pack_opt(mine_small[:n_gain], mine_small[n_gain:n_gain + n_tap], small_g[n_gain + n_tap])
    s_out = _adamw(sw, sm, sv, sg, "adamw_small")

    def unpack(a):
        return (a[:n_gain].reshape(depth, 4, dq), a[n_gain:n_gain + n_tap].reshape(n_a, -1, dq),
                a[n_gain + n_tap:n_gain + n_tap + kv_rows].reshape(d))

    small_out = [unpack(a) for a in s_out]

    landed, big_out = {}, {}

    def update(name):
        for half in halves:
            if matrices_of(*half)[0] not in landed and any(n == name for n, _ in matrices_of(*half)):
                axes = [axis_of[n] for n, _ in matrices_of(*half)]
                landed.update(zip(matrices_of(*half), zip(*_pair_wait(axes, paired[half], "pair_wait_%d_%s" % half))))
        shp, ax = big_w[name].shape, axis_of[name]
        rows, cols = shp[0] * shp[1], shp[2]
        flat = lambda a: a.reshape(rows, cols)
        full, stacks, sibling = zip(*[landed[name, i] for i in range(shp[0])])
        res = _adamw(flat(big_w[name]), flat(big_m[name]), flat(big_v[name]),
                     [(full[i], ax, stacks[i], sibling[i]) for i in range(shp[0])], f"adamw_{name}")
        big_out[name] = [a.reshape(shp[1:] if name == "kv" else shp) for a in res]

    pending = [half for half in reversed(halves) if half not in paired]
    for half in pending[:-1]:
        to_sibling(half, [], None)
    late = [name for name, _ in BIG if any(n == name for n, _ in matrices_of(*pending[-1]))]
    for name, _ in BIG:
        if name not in late:
            update(name)
    to_sibling(pending[-1], [], big_out["ffn_out"][0])
    for name in late:
        update(name)

    def leaves(i):
        ng, cw_, kg = small_out[i]
        return [ng, big_out["conv_in"][i], cw_, big_out["conv_out"][i], kg, big_out["kv"][i], big_out["q"][i],
                big_out["o"][i], big_out["ffn_in"][i], big_out["ffn_out"][i]]

    return (loss, dx[None], *leaves(0), *leaves(1), *leaves(2), *leaves(3))
```

```python
import functools

import jax
import jax.numpy as jnp
import numpy as np
from jax import lax
from jax.experimental import pallas as pl
from jax.experimental.pallas import tpu as pltpu

F32 = jnp.float32
BF16 = jnp.bfloat16
HEAD_DIM = 64
DILATIONS = (1, 4, 16)
NORM_EPS = 1e-6
NEG_BIG = -1e30
VMEM_LIMIT = 48 * 1024 * 1024
ROW_TILE = 256
NORM_TILE = 512
LANE = 128
MESH = pl.DeviceIdType.MESH

ADAM_LR = 0.001
ADAM_B1 = 0.9
ADAM_B2 = 0.999
ADAM_EPS = 1e-08
ADAM_WD = 0.01
ADAM_STEP = 10

TILE_CANDIDATES = (1024, 1408, 768, 512, 384, 256, 128)


def _pick(dim, cands=TILE_CANDIDATES):
    for c in cands:
        if c <= dim and dim % c == 0:
            return c
    return dim


def _params(sem):
    return pltpu.CompilerParams(dimension_semantics=sem, vmem_limit_bytes=VMEM_LIMIT)


def _mm(a, b, mode, out_dtype, name, scale=None):
    a_planes = a.shape[0] if a.ndim == 3 else 1
    b_planes = b.shape[0] if b.ndim == 3 else 1
    if mode == "nn":
        m, k = a.shape[-2], a.shape[-1] * a_planes
        n = b.shape[1]
    elif mode == "nt":
        m, k = a.shape[-2], a.shape[-1] * a_planes
        n = b.shape[0]
    else:
        k, m = a.shape
        n = b.shape[-1] * b_planes
    tm, tn, tk = _pick(m), _pick(n // b_planes), _pick(k // a_planes, ((2048,) if mode == "tn" else ()) + TILE_CANDIDATES)
    nk = k // tk
    ka, nb = k // a_planes // tk, n // b_planes // tn
    if a_planes > 1:
        a_spec = pl.BlockSpec((None, tm, tk), lambda i, j, kk: (kk // ka, i, kk % ka))
    elif mode == "tn":
        a_spec = pl.BlockSpec((tk, tm), lambda i, j, kk: (kk, i))
    else:
        a_spec = pl.BlockSpec((tm, tk), lambda i, j, kk: (i, kk))
    if mode == "nn":
        b_spec = pl.BlockSpec((tk, tn), lambda i, j, kk: (kk, j))
        dims = (((1,), (0,)), ((), ()))
    elif mode == "nt":
        b_spec = pl.BlockSpec((tn, tk), lambda i, j, kk: (j, kk))
        dims = (((1,), (1,)), ((), ()))
    else:
        b_spec = (pl.BlockSpec((None, tk, tn), lambda i, j, kk: (j // nb, kk, j % nb)) if b_planes > 1
                  else pl.BlockSpec((tk, tn), lambda i, j, kk: (kk, j)))
        dims = (((0,), (0,)), ((), ()))

    def finish(acc):
        if scale is not None:
            acc = acc * scale
        return acc.astype(out_dtype)

    if nk == 1:
        def body(a_ref, b_ref, o_ref):
            o_ref[...] = finish(lax.dot_general(a_ref[...].astype(BF16), b_ref[...].astype(BF16), dims, preferred_element_type=F32))
        scratch = []
    else:
        def body(a_ref, b_ref, o_ref, acc_ref):
            kk = pl.program_id(2)

            @pl.when(kk == 0)
            def _():
                acc_ref[...] = jnp.zeros_like(acc_ref)

            acc_ref[...] += lax.dot_general(a_ref[...].astype(BF16), b_ref[...].astype(BF16), dims, preferred_element_type=F32)

            @pl.when(kk == nk - 1)
            def _():
                o_ref[...] = finish(acc_ref[...])
        scratch = [pltpu.VMEM((tm, tn), F32)]

    return pl.pallas_call(
        body, name=name,
        grid=(m // tm, n // tn, nk),
        in_specs=[a_spec, b_spec],
        out_specs=pl.BlockSpec((tm, tn), lambda i, j, kk: (i, j)),
        out_shape=jax.ShapeDtypeStruct((m, n), out_dtype),
        scratch_shapes=scratch,
        compiler_params=_params(("parallel", "parallel", "arbitrary")),
    )(a, b)


def _rstd(v):
    return lax.rsqrt(jnp.mean(v * v, axis=-1, keepdims=True) + NORM_EPS)


def _rms_bwd(dy, v, g, r):
    gy = dy * g
    dv = r * (gy - v * (r * r) * jnp.mean(gy * v, axis=-1, keepdims=True))
    return dv, dy * v * r


def _row_spec(t, width):
    return pl.BlockSpec((t, width), lambda i: (i, 0))


def _gain_spec(width):
    return pl.BlockSpec((1, width), lambda i: (0, 0))


def _norm_res_fwd(x, mix, g_post, pre_gains, name):
    s, d = x.shape
    t = _pick(s, (NORM_TILE, ROW_TILE))
    has_mix = mix is not None
    n_pre = len(pre_gains)

    def body(*refs):
        x_ref = refs[0]
        pos = 1
        x1 = x_ref[...]
        if has_mix:
            mv = refs[1][...].astype(F32)
            x1 = x1 + mv * _rstd(mv) * refs[2][...]
            pos = 3
        gains = refs[pos:pos + n_pre]
        outs = refs[pos + n_pre:]
        if has_mix:
            outs[0][...] = x1
            outs = outs[1:]
        r = _rstd(x1)
        for g_ref, o_ref in zip(gains, outs):
            o_ref[...] = (x1 * r * g_ref[...]).astype(BF16)

    ins = [x] + ([mix, g_post] if has_mix else []) + list(pre_gains)
    in_specs = [_row_spec(t, d)] + ([_row_spec(t, d), _gain_spec(d)] if has_mix else []) + [_gain_spec(d)] * n_pre
    out_shape = ([jax.ShapeDtypeStruct((s, d), F32)] if has_mix else []) + [jax.ShapeDtypeStruct((s, d), BF16)] * n_pre
    out_specs = [_row_spec(t, d)] * len(out_shape)
    res = pl.pallas_call(
        body, name=name, grid=(s // t,), in_specs=in_specs, out_specs=out_specs, out_shape=out_shape,
        compiler_params=_params(("parallel",)),
    )(*ins)
    if has_mix:
        return res[0], list(res[1:])
    return x, list(res)


def _norm_res_loss(x, mix, g_post, target, name):
    s, d = x.shape
    t = _pick(s, (NORM_TILE, ROW_TILE))

    def body(x_ref, m_ref, g_ref, t_ref, dy_ref, loss_ref):
        mv = m_ref[...].astype(F32)
        y = x_ref[...] + mv * _rstd(mv) * g_ref[...]
        err = y - t_ref[...]
        dy_ref[...] = err * (1.0 / d)

        @pl.when(pl.program_id(0) == 0)
        def _():
            loss_ref[...] = jnp.zeros_like(loss_ref)

        loss_ref[...] += jnp.sum(err * err)

    dy, acc = pl.pallas_call(
        body, name=name, grid=(s // t,),
        in_specs=[_row_spec(t, d), _row_spec(t, d), _gain_spec(d), _row_spec(t, d)],
        out_specs=[_row_spec(t, d), pl.BlockSpec((8, LANE), lambda i: (0, 0))],
        out_shape=[jax.ShapeDtypeStruct((s, d), F32), jax.ShapeDtypeStruct((8, LANE), F32)],
        compiler_params=_params(("arbitrary",)),
    )(x, mix, g_post, target)
    return dy, acc[0, 0] * (0.5 / d)


def _norm_bwd(dx_out, branches, x_in, post, name):
    s, d = dx_out.shape
    t = _pick(s, (NORM_TILE, ROW_TILE))
    nb = len(branches)
    has_post = post is not None

    def body(*refs):
        dx_ref = refs[0]
        pos = 1
        dx = dx_ref[...]
        first = pl.program_id(0) == 0
        n_in = 1 + (1 + 2 * nb if nb else 0) + (2 if has_post else 0)
        outs = refs[n_in:]
        opos = 0
        if nb:
            xv = refs[pos][...]
            pos += 1
            r = _rstd(xv)
            dx_o = outs[0]
            opos = 1
            for _ in range(nb):
                dxn = refs[pos][...].astype(F32)
                g = refs[pos + 1][...]
                pos += 2
                dv, dg_rows = _rms_bwd(dxn, xv, g, r)
                dx = dx + dv
                dg_ref = outs[opos]
                opos += 1

                @pl.when(first)
                def _(dg_ref=dg_ref):
                    dg_ref[...] = jnp.zeros_like(dg_ref)

                dg_ref[...] += jnp.sum(dg_rows, axis=0, keepdims=True)
            dx_o[...] = dx
        if has_post:
            mv = refs[pos][...].astype(F32)
            g = refs[pos + 1][...]
            dm, dg_rows = _rms_bwd(dx, mv, g, _rstd(mv))
            outs[opos][...] = dm.astype(BF16)
            dg_ref = outs[opos + 1]

            @pl.when(first)
            def _():
                dg_ref[...] = jnp.zeros_like(dg_ref)

            dg_ref[...] += jnp.sum(dg_rows, axis=0, keepdims=True)

    ins, in_specs = [dx_out], [_row_spec(t, d)]
    out_shape, out_specs = [], []
    if nb:
        ins.append(x_in)
        in_specs.append(_row_spec(t, d))
        out_shape.append(jax.ShapeDtypeStruct((s, d), F32))
        out_specs.append(_row_spec(t, d))
        for dxn, g in branches:
            ins += [dxn, g]
            in_specs += [_row_spec(t, d), _gain_spec(d)]
            out_shape.append(jax.ShapeDtypeStruct((1, d), F32))
            out_specs.append(_gain_spec(d))
    if has_post:
        ins += [post[0], post[1]]
        in_specs += [_row_spec(t, d), _gain_spec(d)]
        out_shape += [jax.ShapeDtypeStruct((s, d), BF16), jax.ShapeDtypeStruct((1, d), F32)]
        out_specs += [_row_spec(t, d), _gain_spec(d)]
    res = pl.pallas_call(
        body, name=name, grid=(s // t,), in_specs=in_specs, out_specs=out_specs, out_shape=out_shape,
        compiler_params=_params(("arbitrary",)),
    )(*ins)
    res = list(res)
    dx_in = res.pop(0) if nb else dx_out
    dgs = [res.pop(0) for _ in range(nb)]
    dm, dg_post = (res[0], res[1]) if has_post else (None, None)
    return dx_in, dgs, dm, dg_post


HALO = 16


def _shift_down(u, prev, k):
    rows = lax.broadcasted_iota(jnp.int32, u.shape, 0)
    out = pltpu.roll(u, k, 0)
    for i in range(k):
        out = jnp.where(rows == i, prev[HALO - k + i:HALO - k + i + 1, :], out)
    return out


def _shift_up(u, nxt, k):
    n = u.shape[0]
    rows = lax.broadcasted_iota(jnp.int32, u.shape, 0)
    out = pltpu.roll(u, n - k, 0)
    for i in range(k):
        out = jnp.where(rows == n - k + i, nxt[i:i + 1, :], out)
    return out


def _conv_gate_fwd(p, cw, name):
    s, d3 = p.shape
    d = d3 // 3
    t = _pick(s, (ROW_TILE,))
    hb = t // HALO

    def body(p_ref, prev_ref, w_ref, z_ref):
        i = pl.program_id(0)
        pv = p_ref[...].astype(F32)
        b, u = pv[:, :d], pv[:, d:2 * d] * pv[:, 2 * d:]
        ph = prev_ref[...].astype(F32)
        up = jnp.where(i > 0, ph[:, d:2 * d] * ph[:, 2 * d:], 0.0)
        w = w_ref[...]
        y = w[0:1, :] * _shift_down(u, up, 2) + w[1:2, :] * _shift_down(u, up, 1) + w[2:3, :] * u
        z_ref[...] = (b * y).astype(BF16)

    return pl.pallas_call(
        body, name=name, grid=(s // t,),
        in_specs=[_row_spec(t, d3),
                  pl.BlockSpec((HALO, d3), lambda i: (jnp.maximum(i * hb - 1, 0), 0)),
                  pl.BlockSpec((3, d), lambda i: (0, 0))],
        out_specs=_row_spec(t, d),
        out_shape=jax.ShapeDtypeStruct((s, d), BF16),
        compiler_params=_params(("parallel",)),
    )(p, p, cw)


def _conv_gate_bwd(p, dz, cw, name):
    s, d3 = p.shape
    d = d3 // 3
    t = _pick(s, (ROW_TILE,))
    hb = t // HALO
    nt = s // t
    last_halo = s // HALO - 1

    def body(p_ref, prev_ref, next_ref, dz_ref, dznext_ref, w_ref, dp_ref, dw_ref):
        i = pl.program_id(0)
        pv = p_ref[...].astype(F32)
        b, c, h = pv[:, :d], pv[:, d:2 * d], pv[:, 2 * d:]
        u = c * h
        ph = prev_ref[...].astype(F32)
        up = jnp.where(i > 0, ph[:, d:2 * d] * ph[:, 2 * d:], 0.0)
        w = w_ref[...]
        u1, u2 = _shift_down(u, up, 1), _shift_down(u, up, 2)
        y = w[0:1, :] * u2 + w[1:2, :] * u1 + w[2:3, :] * u
        dz = dz_ref[...].astype(F32)
        dy = dz * b
        dyn = jnp.where(i < nt - 1, dznext_ref[...].astype(F32) * next_ref[...].astype(F32)[:, :d], 0.0)
        du = w[2:3, :] * dy + w[1:2, :] * _shift_up(dy, dyn, 1) + w[0:1, :] * _shift_up(dy, dyn, 2)
        dp_ref[:, :d] = (dz * y).astype(BF16)
        dp_ref[:, d:2 * d] = (du * h).astype(BF16)
        dp_ref[:, 2 * d:] = (du * c).astype(BF16)

        @pl.when(i == 0)
        def _():
            dw_ref[...] = jnp.zeros_like(dw_ref)

        dw_ref[0:1, :] += jnp.sum(dy * u2, axis=0, keepdims=True)
        dw_ref[1:2, :] += jnp.sum(dy * u1, axis=0, keepdims=True)
        dw_ref[2:3, :] += jnp.sum(dy * u, axis=0, keepdims=True)

    return pl.pallas_call(
        body, name=name, grid=(nt,),
        in_specs=[_row_spec(t, d3),
                  pl.BlockSpec((HALO, d3), lambda i: (jnp.maximum(i * hb - 1, 0), 0)),
                  pl.BlockSpec((HALO, d3), lambda i: (jnp.minimum((i + 1) * hb, last_halo), 0)),
                  _row_spec(t, d),
                  pl.BlockSpec((HALO, d), lambda i: (jnp.minimum((i + 1) * hb, last_halo), 0)),
                  pl.BlockSpec((3, d), lambda i: (0, 0))],
        out_specs=[_row_spec(t, d3), pl.BlockSpec((3, d), lambda i: (0, 0))],
        out_shape=[jax.ShapeDtypeStruct((s, d3), BF16), jax.ShapeDtypeStruct((3, d), F32)],
        compiler_params=_params(("arbitrary",)),
    )(p, p, p, dz, dz, cw)


FFN_ROWS, FFN_COLS = (512, 256), (1408, 768, 256, 128)


def _row_chunks(tm, rows=256):
    return [slice(r, r + min(rows, tm)) for r in range(0, tm, min(rows, tm))]


def _ffn_in_swiglu(xn, w_in, name):
    s, k = xn.shape
    ff = w_in.shape[1] // 2
    tm, tn = _pick(s, FFN_ROWS), _pick(ff, FFN_COLS)
    nj = ff // tn

    def body(x_ref, wg_ref, wu_ref, f_ref, a_ref):
        for rows in _row_chunks(tm):
            xv = x_ref[rows, :]
            gate = jnp.dot(xv, wg_ref[...], preferred_element_type=F32).astype(BF16)
            up = jnp.dot(xv, wu_ref[...], preferred_element_type=F32).astype(BF16)
            f_ref[0, rows, :] = gate
            f_ref[1, rows, :] = up
            a_ref[rows, :] = gate * jax.nn.sigmoid(gate) * up

    return pl.pallas_call(
        body, name=name, grid=(nj, s // tm),
        in_specs=[pl.BlockSpec((tm, k), lambda j, i: (i, 0)),
                  pl.BlockSpec((k, tn), lambda j, i: (0, j)),
                  pl.BlockSpec((k, tn), lambda j, i: (0, nj + j))],
        out_specs=[pl.BlockSpec((2, tm, tn), lambda j, i: (0, i, j)), pl.BlockSpec((tm, tn), lambda j, i: (i, j))],
        out_shape=[jax.ShapeDtypeStruct((2, s, ff), BF16), jax.ShapeDtypeStruct((s, ff), BF16)],
        compiler_params=_params(("parallel", "parallel")),
    )(xn, w_in, w_in)


def _ffn_out_dx_swiglu(dff, w_out, f, name):
    s, d = dff.shape
    ff = w_out.shape[0]
    tm, tn = _pick(s, FFN_ROWS), _pick(ff, FFN_COLS)

    def body(d_ref, w_ref, f_ref, df_ref):
        for rows in _row_chunks(tm):
            da = lax.dot_general(d_ref[rows, :], w_ref[...], (((1,), (1,)), ((), ())), preferred_element_type=F32)
            gate, up = f_ref[0, rows, :], f_ref[1, rows, :]
            sg = jax.nn.sigmoid(gate)
            silu = gate * sg
            dsilu = sg + silu * (1.0 - sg)
            df_ref[0, rows, :] = (da * (up * dsilu).astype(F32)).astype(BF16)
            df_ref[1, rows, :] = (da * silu.astype(F32)).astype(BF16)

    planes = pl.BlockSpec((2, tm, tn), lambda j, i: (0, i, j))
    return pl.pallas_call(
        body, name=name, grid=(ff // tn, s // tm),
        in_specs=[pl.BlockSpec((tm, d), lambda j, i: (i, 0)), pl.BlockSpec((tn, d), lambda j, i: (j, 0)), planes],
        out_specs=planes, out_shape=jax.ShapeDtypeStruct((2, s, ff), BF16),
        compiler_params=_params(("parallel", "parallel")),
    )(dff, w_out, f)


SUPER = 2048
RES = 16
PAIR = 128
L = 128
FWD_TOGETHER = 8
BWD_TOGETHER = 4


def _alibi_slopes(n_heads):
    h = np.arange(n_heads, dtype=np.float32) + 1.0
    return np.power(2.0, -8.0 * h / n_heads).astype(np.float32)


def _permute16(x, inverse, name):
    s, d = x.shape
    cw = LANE

    def body(x_ref, o_ref):
        if inverse:
            for m in range(L):
                o_ref[RES * m:RES * (m + 1), :] = x_ref[pl.ds(m, RES, stride=L), :]
        else:
            for r in range(RES):
                o_ref[L * r:L * (r + 1), :] = x_ref[pl.ds(r, L, stride=RES), :]

    spec = pl.BlockSpec((SUPER, cw), lambda i, j: (i, j))
    return pl.pallas_call(
        body, name=name, grid=(s // SUPER, d // cw), in_specs=[spec], out_specs=spec,
        out_shape=jax.ShapeDtypeStruct((s, d), x.dtype),
        compiler_params=_params(("parallel", "parallel")),
    )(x)


def _slope_table(d):
    nh = d // HEAD_DIM
    sl = _alibi_slopes(nh)
    tab = np.repeat(sl, HEAD_DIM).reshape(d // PAIR, 1, PAIR)
    return jnp.asarray(np.broadcast_to(tab, (d // PAIR, 8, PAIR)).copy())


def _geometry(dil):
    nch = RES // dil
    return nch, L // nch


def _band(dil):
    nch, w = _geometry(dil)
    sh = w.bit_length() - 1
    i = lax.broadcasted_iota(jnp.int32, (L, 2 * L), 0)
    j = lax.broadcasted_iota(jnp.int32, (L, 2 * L), 1)

    def pos(t):
        return jnp.bitwise_and(t, w - 1) * nch + jnp.right_shift(t, sh)

    delta = pos(i) + L - (pos(jnp.bitwise_and(j, L - 1)) + jnp.bitwise_and(j, L))
    return (delta * dil).astype(F32), (delta >= 0) & (delta <= L), j < L


def _fill_bias(bias_s, sl_ref):
    for b, dil in enumerate(DILATIONS):
        base, band, prev_half = _band(dil)
        for first in range(2):
            valid = band & jnp.logical_not(prev_half) if first else band
            for h in range(2):
                slope = sl_ref[0:1, HEAD_DIM * h:HEAD_DIM * h + 1]
                bias_s[(2 * b + first) * 2 + h] = jnp.where(valid, -slope * base, NEG_BIG)


def _bias_index(b, sb, n):
    first = jnp.logical_and(sb == 0, n == 0).astype(jnp.int32)
    return (2 * b + first) * 2


def _offsets(dil, res, n):
    nch, w = _geometry(dil)

    def al(v):
        return v if isinstance(v, int) else pl.multiple_of(v, w)

    q_off = [al((a * dil + res) * L + n * w) for a in range(nch)]
    k_off = [al((a * dil + res) * 2 * L + L + n * w) for a in range(nch)]
    kp_off = [al((a * dil + res) * 2 * L + L + n * w - w) for a in range(nch)]
    return q_off, k_off, kp_off, w


def _gather(ref, offs, w):
    parts = [ref[pl.ds(o, w), :] for o in offs]
    return parts[0] if len(parts) == 1 else jnp.concatenate(parts, axis=0)


def _scatter(ref, offs, w, val, add=False):
    for a, o in enumerate(offs):
        piece = val[a * w:(a + 1) * w, :]
        if add:
            ref[pl.ds(o, w), :] += piece
        else:
            ref[pl.ds(o, w), :] = piece


def _fill_key_buffer(buf, prev_ref, cur_ref):
    for r in range(RES):
        buf[2 * L * r:2 * L * r + L, :] = prev_ref[L * r:L * (r + 1), :]
        buf[2 * L * r + L:2 * L * (r + 1), :] = cur_ref[L * r:L * (r + 1), :]


def _two_heads(x, low):
    zero = jnp.zeros_like(x)
    return jnp.concatenate([jnp.where(low, x, zero), jnp.where(low, zero, x)], axis=0)


def _loop_blocks(dil, stages, together):
    together = max(together, dil) if dil < RES else together

    def it(i, c):
        if dil == RES:
            blocks = [(i * together + k, 0) for k in range(together)]
        else:
            blocks = [(res, i * (together // dil) + k) for k in range(together // dil) for res in range(dil)]
        state = [stages[0](res, n) for res, n in blocks]
        for stage in stages[1:]:
            state = [stage(res, n, prev) for (res, n), prev in zip(blocks, state)]
        for writes in state:
            for args in writes:
                _scatter(*args)
        return c

    lax.fori_loop(0, RES // together, it, 0)


NT = (((1,), (1,)), ((), ()))
TN = (((0,), (0,)), ((), ()))


def _attention_fwd(q, kv, name):
    s, d = q.shape
    g_n, ns = d // PAIR, s // SUPER

    def body(sl_ref, q_ref, kc_ref, kp_ref, vc_ref, vp_ref, o_ref, lse_ref, kbuf, vbuf, m_s, l_s, acc_s, bias_s):
        sb = pl.program_id(1)
        _fill_key_buffer(kbuf, kp_ref, kc_ref)
        _fill_key_buffer(vbuf, vp_ref, vc_ref)
        _fill_bias(bias_s, sl_ref)
        low = lax.broadcasted_iota(jnp.int32, (L, PAIR), 1) < HEAD_DIM
        low_k = lax.broadcasted_iota(jnp.int32, (2 * L, PAIR), 1) < HEAD_DIM
        ones_bd = _two_heads(jnp.ones((2 * L, PAIR), BF16), low_k)
        for bi, dil in enumerate(DILATIONS):
            first_branch, last_branch = bi == 0, bi == len(DILATIONS) - 1

            def scores(res, n, dil=dil):
                q_off, k_off, kp_off, w = _offsets(dil, res, n)
                qf = _gather(q_ref, q_off, w).astype(BF16)
                kcat = jnp.concatenate([_gather(kbuf, kp_off, w), _gather(kbuf, k_off, w)], axis=0).astype(BF16)
                return lax.dot_general(qf, _two_heads(kcat, low_k), NT, preferred_element_type=F32)

            def update(res, n, sc, bi=bi, dil=dil, first_branch=first_branch, last_branch=last_branch):
                q_off, k_off, kp_off, w = _offsets(dil, res, n)
                vcat = jnp.concatenate([_gather(vbuf, kp_off, w), _gather(vbuf, k_off, w)], axis=0).astype(BF16)
                v_ones = jnp.concatenate([_two_heads(vcat, low_k), ones_bd], axis=1)
                bias_at = _bias_index(bi, sb, n)
                if not first_branch:
                    m_prev = _gather(m_s, q_off, w)
                ps, m_new = [], []
                for h in range(2):
                    s_h = sc[:, 2 * L * h:2 * L * (h + 1)] + bias_s[bias_at + h]
                    mh = jnp.max(s_h, axis=1, keepdims=True)
                    if not first_branch:
                        mh = jnp.maximum(mh, m_prev[:, HEAD_DIM * h:HEAD_DIM * h + 1])
                    ps.append(jnp.exp(s_h - mh).astype(BF16))
                    m_new.append(mh)
                m_full = jnp.where(low, m_new[0], m_new[1])
                both = jnp.dot(jnp.concatenate(ps, axis=1), v_ones, preferred_element_type=F32)
                acc, l_full = both[:, :PAIR], both[:, PAIR:]
                if not first_branch:
                    alpha = jnp.exp(m_prev - m_full)
                    l_full = _gather(l_s, q_off, w) * alpha + l_full
                    acc = _gather(acc_s, q_off, w) * alpha + acc
                if last_branch:
                    return [(o_ref, q_off, w, acc / l_full, False), (lse_ref, q_off, w, m_full + jnp.log(l_full), False)]
                return [(m_s, q_off, w, m_full, False), (l_s, q_off, w, l_full, False), (acc_s, q_off, w, acc, False)]

            _loop_blocks(dil, [scores, update], FWD_TOGETHER)

    prev = lambda i: jnp.maximum(i - 1, 0)
    blk = pl.BlockSpec((SUPER, PAIR), lambda g, i: (i, g))
    in_specs = [pl.BlockSpec((None, 8, PAIR), lambda g, i: (g, 0, 0)), blk,
                pl.BlockSpec((SUPER, PAIR), lambda g, i: (i, g)),
                pl.BlockSpec((SUPER, PAIR), lambda g, i: (prev(i), g)),
                pl.BlockSpec((SUPER, PAIR), lambda g, i: (i, g_n + g)),
                pl.BlockSpec((SUPER, PAIR), lambda g, i: (prev(i), g_n + g))]
    return pl.pallas_call(
        body, name=name, grid=(g_n, ns), in_specs=in_specs, out_specs=[blk, blk],
        out_shape=[jax.ShapeDtypeStruct((s, d), F32)] * 2,
        scratch_shapes=([pltpu.VMEM((2 * SUPER, PAIR), F32)] * 2 + [pltpu.VMEM((SUPER, PAIR), F32)] * 3
                        + [pltpu.VMEM((4 * len(DILATIONS), L, 2 * L), F32)]),
        compiler_params=_params(("parallel", "parallel")),
    )(_slope_table(d), q, kv, kv, kv, kv)


def _attention_bwd(q, kv, o, do, lse, dkv_in, name):
    s, d = q.shape
    g_n, ns = d // PAIR, s // SUPER
    has_in = dkv_in is not None

    def body(*refs):
        sl_ref, q_ref, do_ref, o_ref, lse_ref, kc_ref, kp_ref, vc_ref, vp_ref = refs[:9]
        pos = 9
        if has_in:
            dkv_in_ref = refs[9]
            pos = 10
        dq_ref, dkv_ref, kbuf, vbuf, dkbuf, dvbuf, dq_s, bias_s = refs[pos:]
        step = pl.program_id(1)
        sb = ns - 1 - step
        _fill_key_buffer(kbuf, kp_ref, kc_ref)
        _fill_key_buffer(vbuf, vp_ref, vc_ref)
        _fill_bias(bias_s, sl_ref)

        @pl.when(step == 0)
        def _():
            dkbuf[...] = jnp.zeros_like(dkbuf)
            dvbuf[...] = jnp.zeros_like(dvbuf)

        @pl.when(step > 0)
        def _():
            for buf in (dkbuf, dvbuf):
                for r in range(RES):
                    buf[2 * L * r + L:2 * L * (r + 1), :] = buf[2 * L * r:2 * L * r + L, :]
                    buf[2 * L * r:2 * L * r + L, :] = jnp.zeros((L, PAIR), F32)

        low = lax.broadcasted_iota(jnp.int32, (L, PAIR), 1) < HEAD_DIM
        low_k = lax.broadcasted_iota(jnp.int32, (2 * L, PAIR), 1) < HEAD_DIM
        low_t = lax.broadcasted_iota(jnp.int32, (PAIR, 2 * L), 0) < HEAD_DIM
        for bi, dil in enumerate(DILATIONS):
            first_branch = bi == 0

            def scores(res, n, dil=dil):
                q_off, k_off, kp_off, w = _offsets(dil, res, n)
                qb = _gather(q_ref, q_off, w).astype(BF16)
                dof = _gather(do_ref, q_off, w)
                prod = dof * _gather(o_ref, q_off, w)
                dob = dof.astype(BF16)
                lse_f = _gather(lse_ref, q_off, w)
                zero = jnp.zeros_like(prod)
                dsum = (jnp.sum(jnp.where(low, prod, zero), axis=1, keepdims=True),
                        jnp.sum(jnp.where(low, zero, prod), axis=1, keepdims=True))
                kcat = jnp.concatenate([_gather(kbuf, kp_off, w), _gather(kbuf, k_off, w)], axis=0).astype(BF16)
                vcat = jnp.concatenate([_gather(vbuf, kp_off, w), _gather(vbuf, k_off, w)], axis=0).astype(BF16)
                k_bd, v_bd = _two_heads(kcat, low_k), _two_heads(vcat, low_k)
                sc = lax.dot_general(qb, k_bd, NT, preferred_element_type=F32)
                dp = lax.dot_general(dob, v_bd, NT, preferred_element_type=F32)
                return qb, dob, lse_f, dsum, k_bd, sc, dp

            def gradients(res, n, given, bi=bi, dil=dil, first_branch=first_branch):
                qb, dob, lse_f, dsum, k_bd, sc, dp = given
                q_off, k_off, kp_off, w = _offsets(dil, res, n)
                bias_at = _bias_index(bi, sb, n)
                ps, dss = [], []
                for h in range(2):
                    cols = slice(2 * L * h, 2 * L * (h + 1))
                    lse_h = lse_f[:, HEAD_DIM * h:HEAD_DIM * h + 1]
                    p_h = jnp.exp(sc[:, cols] + bias_s[bias_at + h] - lse_h)
                    dss.append((p_h * (dp[:, cols] - dsum[h])).astype(BF16))
                    ps.append(p_h.astype(BF16))
                ds_cat, p_cat = jnp.concatenate(dss, axis=1), jnp.concatenate(ps, axis=1)
                dq = jnp.dot(ds_cat, k_bd, preferred_element_type=F32)
                dk_t = lax.dot_general(qb, ds_cat, TN, preferred_element_type=F32)
                dv_t = lax.dot_general(dob, p_cat, TN, preferred_element_type=F32)
                dk = jnp.where(low_t, dk_t[:, :2 * L], dk_t[:, 2 * L:]).T
                dv = jnp.where(low_t, dv_t[:, :2 * L], dv_t[:, 2 * L:]).T
                return [(dq_s, q_off, w, dq, not first_branch),
                        (dkbuf, kp_off, w, dk[:L], True), (dkbuf, k_off, w, dk[L:], True),
                        (dvbuf, kp_off, w, dv[:L], True), (dvbuf, k_off, w, dv[L:], True)]

            _loop_blocks(dil, [scores, gradients], BWD_TOGETHER)

        dq_ref[...] = dq_s[...].astype(BF16)
        for r in range(RES):
            rows, cur = slice(L * r, L * (r + 1)), slice(2 * L * r + L, 2 * L * (r + 1))
            for plane, buf in enumerate((dkbuf, dvbuf)):
                if has_in:
                    dkv_ref[plane, rows, :] = buf[cur, :] + dkv_in_ref[plane, rows, :]
                else:
                    dkv_ref[plane, rows, :] = buf[cur, :]

    rev = lambda i: ns - 1 - i
    prev = lambda i: jnp.maximum(ns - 2 - i, 0)
    blk = pl.BlockSpec((SUPER, PAIR), lambda g, i: (rev(i), g))
    in_specs = [pl.BlockSpec((None, 8, PAIR), lambda g, i: (g, 0, 0)), blk, blk, blk, blk,
                pl.BlockSpec((SUPER, PAIR), lambda g, i: (rev(i), g)),
                pl.BlockSpec((SUPER, PAIR), lambda g, i: (prev(i), g)),
                pl.BlockSpec((SUPER, PAIR), lambda g, i: (rev(i), g_n + g)),
                pl.BlockSpec((SUPER, PAIR), lambda g, i: (prev(i), g_n + g))]
    ins = [_slope_table(d), q, do, o, lse, kv, kv, kv, kv]
    planes = pl.BlockSpec((2, SUPER, PAIR), lambda g, i: (0, rev(i), g))
    if has_in:
        in_specs.append(planes)
        ins.append(dkv_in)
    res = pl.pallas_call(
        body, name=name, grid=(g_n, ns), in_specs=in_specs, out_specs=[blk, planes],
        out_shape=[jax.ShapeDtypeStruct((s, d), BF16), jax.ShapeDtypeStruct((2, s, d), F32)],
        scratch_shapes=([pltpu.VMEM((2 * SUPER, PAIR), F32)] * 4 + [pltpu.VMEM((SUPER, PAIR), F32)]
                        + [pltpu.VMEM((4 * len(DILATIONS), L, 2 * L), F32)]),
        compiler_params=_params(("parallel", "arbitrary")),
    )(*ins)
    return res[0], res[1]


def _coords():
    return lax.axis_index("x"), lax.axis_index("y"), lax.axis_index("c")


def _chip_peers(x, y):
    return [(1 - x, y), (x, 1 - y), (1 - x, 1 - y)]


def _block_of(ref, axis, blk, size):
    start = pl.multiple_of(blk * size, size)
    if axis == 1:
        return ref.at[:, pl.ds(start, size)]
    return ref.at[pl.ds(start, size), :]


ANY = pl.BlockSpec(memory_space=pl.ANY)


HBM = pl.BlockSpec(memory_space=pltpu.HBM)
SEM = pl.BlockSpec(memory_space=pltpu.SEMAPHORE)
SPLIT = pltpu.CompilerParams(has_side_effects=pltpu.SideEffectType.DATAFLOW_SIDE_EFFECTING)


def _in_hbm(a):
    return pltpu.with_memory_space_constraint(a, pltpu.HBM)


def _thru(arrays):
    return [pltpu.HBM(a.shape, a.dtype) for a in arrays]


def _cast_place(w, layer, ax, dtype, name):
    _, k, n = w.shape
    t = _pick(k, (512, 256, 128))
    nb = k // t

    def body(blk_ref, w_ref, b_ref, f_ref):
        v = w_ref[...].astype(dtype)
        b_ref[...] = v
        f_ref[...] = v

    full_shape = (k, 4 * n) if ax == 1 else (4 * k, n)
    place = (lambda i, blk: (i, blk[0])) if ax == 1 else (lambda i, blk: (blk[0] * nb + i, 0))
    return pl.pallas_call(
        body, name=name,
        grid_spec=pltpu.PrefetchScalarGridSpec(
            num_scalar_prefetch=1, grid=(nb,),
            in_specs=[pl.BlockSpec((None, t, n), lambda i, blk: (layer, i, 0))],
            out_specs=[pl.BlockSpec((t, n), lambda i, blk: (i, 0)), pl.BlockSpec((t, n), place)]),
        out_shape=[jax.ShapeDtypeStruct((k, n), dtype), jax.ShapeDtypeStruct(full_shape, dtype)],
        compiler_params=_params(("parallel",)),
    )(_my_block()[None], w)


def _my_block():
    return (2 * lax.axis_index("x") + lax.axis_index("y")).astype(jnp.int32)


def _gather_start(group, carry, name):
    n, nc = len(group), len(carry)

    def body(*refs):
        blocks, fulls, send_sem, recv_sem = refs[:n], refs[n:2 * n], refs[2 * n + nc], refs[2 * n + nc + 1]
        x, y, c = _coords()
        for t, (b, _, ax) in enumerate(group):
            mine = _block_of(fulls[t], ax, 2 * x + y, b.shape[ax])
            for j, (px, py) in enumerate(_chip_peers(x, y)):
                pltpu.make_async_remote_copy(
                    src_ref=blocks[t], dst_ref=mine, send_sem=send_sem.at[3 * t + j], recv_sem=recv_sem.at[3 * t + j],
                    device_id=(px, py, c), device_id_type=MESH).start()

    arrays = [b for b, _, _ in group] + [f for _, f, _ in group] + list(carry)
    sems = [pltpu.SemaphoreType.DMA((3 * n,))] * 2
    res = pl.pallas_call(
        body, name=name, in_specs=[HBM] * len(arrays), out_specs=[SEM, SEM] + [HBM] * len(arrays),
        out_shape=sems + _thru(arrays), input_output_aliases={i: 2 + i for i in range(len(arrays))},
        compiler_params=SPLIT,
    )(*[_in_hbm(a) for a in arrays])
    return (res[0], res[1], list(res[2:2 + n]), list(res[2 + n:2 + 2 * n])), list(res[2 + 2 * n:])


def _gather_wait(group, started, after, name):
    sends, recvs, blocks, fulls = started
    m = len(group)

    def body(*refs):
        blk_refs, full_refs, send_sem, recv_sem = refs[:m], refs[m:2 * m], refs[2 * m], refs[2 * m + 1]
        x, y, c = _coords()
        for t, (b, _, ax) in enumerate(group):
            for j, (px, py) in enumerate(_chip_peers(x, y)):
                cp = pltpu.make_async_remote_copy(
                    src_ref=blk_refs[t], dst_ref=_block_of(full_refs[t], ax, 2 * px + py, b.shape[ax]),
                    send_sem=send_sem.at[3 * t + j], recv_sem=recv_sem.at[3 * t + j],
                    device_id=(px, py, c), device_id_type=MESH)
                cp.wait_send()
                cp.wait_recv()

    extra = [] if after is None else [after]
    res = pl.pallas_call(
        body, name=name, in_specs=[HBM] * (2 * m) + [SEM, SEM] + [ANY] * len(extra), out_specs=[HBM] * (2 * m),
        out_shape=_thru(blocks) + _thru(fulls), input_output_aliases={i: i for i in range(2 * m)},
        compiler_params=SPLIT,
    )(*blocks, *fulls, sends, recvs, *extra)
    return list(res[m:])


def _scatter_start(grads, carry, name):
    n = len(grads)
    n_in = 2 * n + len(carry)

    def body(*refs):
        g_refs, st_refs, send_sem, recv_sem = refs[:n], refs[n:2 * n], refs[n_in], refs[n_in + 1]
        x, y, c = _coords()
        for t, (g, ax) in enumerate(grads):
            for j, (px, py) in enumerate(_chip_peers(x, y)):
                pltpu.make_async_remote_copy(
                    src_ref=_block_of(g_refs[t], ax, 2 * px + py, g.shape[ax] // 4), dst_ref=st_refs[t].at[j],
                    send_sem=send_sem.at[3 * t + j], recv_sem=recv_sem.at[3 * t + j],
                    device_id=(px, py, c), device_id_type=MESH).start()

    arrays = [g for g, _ in grads]
    for g, ax in grads:
        shape = list(g.shape)
        shape[ax] //= 4
        arrays.append(lax.empty((3, *shape), g.dtype))
    arrays += list(carry)
    sems = [pltpu.SemaphoreType.DMA((3 * n,))] * 2
    res = pl.pallas_call(
        body, name=name, in_specs=[HBM] * n_in, out_specs=[SEM, SEM] + [HBM] * n_in,
        out_shape=sems + _thru(arrays), input_output_aliases={i: 2 + i for i in range(n_in)},
        compiler_params=SPLIT,
    )(*[_in_hbm(a) for a in arrays])
    return (res[0], res[1], list(res[2:2 + n]), list(res[2 + n:2 + 2 * n])), list(res[2 + 2 * n:])


def _scatter_wait(axes, started, after, name):
    sends, recvs, full, stacks = started
    n = len(full)
    extra = [] if after is None else [after]

    def body(*refs):
        g_refs, st_refs, send_sem, recv_sem = refs[:n], refs[n:2 * n], refs[2 * n], refs[2 * n + 1]
        x, y, c = _coords()
        for t, ax in enumerate(axes):
            size = full[t].shape[ax] // 4
            for j, (px, py) in enumerate(_chip_peers(x, y)):
                cp = pltpu.make_async_remote_copy(
                    src_ref=_block_of(g_refs[t], ax, 2 * px + py, size), dst_ref=st_refs[t].at[j],
                    send_sem=send_sem.at[3 * t + j], recv_sem=recv_sem.at[3 * t + j],
                    device_id=(px, py, c), device_id_type=MESH)
                cp.wait_send()
                cp.wait_recv()

    res = pl.pallas_call(
        body, name=name, in_specs=[HBM] * (2 * n) + [SEM, SEM] + [ANY] * len(extra), out_specs=[HBM] * (2 * n),
        out_shape=_thru(full) + _thru(stacks), input_output_aliases={i: i for i in range(2 * n)},
        compiler_params=SPLIT,
    )(*full, *stacks, sends, recvs, *extra)
    return list(res[:n]), list(res[n:])


def _pair_copies(g_refs, st_refs, out_refs, items, send_sem, recv_sem):
    x, y, c = _coords()
    copies = []
    for u, (g, ax, _) in enumerate(items):
        own = _block_of(g_refs[u], ax, 2 * x + y, g.shape[ax] // 4)
        for k, (src, dst) in enumerate([(own, out_refs[u].at[0]), (st_refs[u], out_refs[u].at[pl.ds(1, 3)])]):
            copies.append(pltpu.make_async_remote_copy(
                src_ref=src, dst_ref=dst, send_sem=send_sem.at[2 * u + k], recv_sem=recv_sem.at[2 * u + k],
                device_id=(x, y, 1 - c), device_id_type=MESH))
    return copies


def _pair_start(items, carry, name):
    n = len(items)
    n_in = 3 * n + len(carry)

    def body(*refs):
        for cp in _pair_copies(refs[:n], refs[n:2 * n], refs[2 * n:3 * n], items, refs[n_in], refs[n_in + 1]):
            cp.start()

    arrays = ([g for g, _, _ in items] + [st for _, _, st in items]
              + [lax.empty((4, *st.shape[1:]), st.dtype) for _, _, st in items] + list(carry))
    sems = [pltpu.SemaphoreType.DMA((2 * n,))] * 2
    res = pl.pallas_call(
        body, name=name, in_specs=[HBM] * n_in, out_specs=[SEM, SEM] + [HBM] * n_in,
        out_shape=sems + _thru(arrays), input_output_aliases={i: 2 + i for i in range(n_in)},
        compiler_params=SPLIT,
    )(*[_in_hbm(a) for a in arrays])
    thru = res[2:]
    return (res[0], res[1], *(list(thru[k * n:(k + 1) * n]) for k in range(3))), list(thru[3 * n:])


def _pair_wait(axes, started, name):
    send, recv, full, stacks, landing = started
    n = len(full)
    items = [(full[u], axes[u], stacks[u]) for u in range(n)]

    def body(*refs):
        for cp in _pair_copies(refs[:n], refs[n:2 * n], refs[2 * n:3 * n], items, refs[3 * n], refs[3 * n + 1]):
            cp.wait_send()
            cp.wait_recv()

    res = pl.pallas_call(
        body, name=name, in_specs=[HBM] * (3 * n) + [SEM, SEM], out_specs=[HBM] * (3 * n),
        out_shape=_thru(full + stacks + landing), input_output_aliases={i: i for i in range(3 * n)},
        compiler_params=SPLIT,
    )(*full, *stacks, *landing, send, recv)
    return list(res[:n]), list(res[n:2 * n]), list(res[2 * n:])


def _allreduce_small(v, name):
    r, cdim = v.shape

    def body(v_ref, out_ref, buf, send_sems, recv_sems):
        x, y, c = _coords()
        me = 4 * x + 2 * y + c
        buf[0] = v_ref[...]
        sends = []
        for k in range(1, 8):
            peer = (x if not (k & 4) else 1 - x, y if not (k & 2) else 1 - y, c if not (k & 1) else 1 - c)
            cp = pltpu.make_async_remote_copy(
                src_ref=v_ref, dst_ref=buf.at[k], send_sem=send_sems.at[k - 1], recv_sem=recv_sems.at[k - 1],
                device_id=peer, device_id_type=MESH)
            cp.start()
            sends.append(cp)
        for cp in sends:
            cp.wait_recv()
        total = buf[me]
        for src in range(1, 8):
            total = total + buf[jnp.bitwise_xor(me, src)]
        out_ref[...] = total
        for cp in sends:
            cp.wait_send()

    return pl.pallas_call(
        body, name=name,
        in_specs=[pl.BlockSpec(memory_space=pltpu.VMEM)], out_specs=pl.BlockSpec(memory_space=pltpu.VMEM),
        out_shape=jax.ShapeDtypeStruct((r, cdim), F32),
        scratch_shapes=[pltpu.VMEM((8, r, cdim), F32), pltpu.SemaphoreType.DMA((7,)), pltpu.SemaphoreType.DMA((7,))],
        compiler_params=pltpu.CompilerParams(has_side_effects=True),
    )(v)


def _adamw_math(w, g, m, v):
    m = ADAM_B1 * m + (1.0 - ADAM_B1) * g
    v = ADAM_B2 * v + (1.0 - ADAM_B2) * jnp.square(g)
    m_hat = m / (1.0 - ADAM_B1 ** ADAM_STEP)
    v_hat = v / (1.0 - ADAM_B2 ** ADAM_STEP)
    delta = -ADAM_LR * (m_hat / (jnp.sqrt(v_hat) + ADAM_EPS) + ADAM_WD * w)
    return delta, m, v


def _adamw(w, m, v, grads, name):
    r, cdim = w.shape
    paired = isinstance(grads, list)
    layers = len(grads) if paired else 1
    t = _pick(r // layers, (128, 64, 32, 16, 8))
    per_layer = r // layers // t
    n_grad = 3 * layers if paired else 1

    def body(*refs):
        refs = refs[1:] if paired else refs
        w_ref, m_ref, v_ref = refs[:3]
        outs = refs[3 + n_grad:]

        def update(g):
            delta, m_new, v_new = _adamw_math(w_ref[...], g, m_ref[...], v_ref[...])
            outs[0][...] = g
            outs[1][...] = delta
            outs[2][...] = m_new
            outs[3][...] = v_new

        if not paired:
            update(refs[3][...])
            return
        layer = pl.program_id(0) // per_layer
        for l in range(layers):
            @pl.when(layer == l)
            def _(own_ref=refs[3 + 3 * l], st_ref=refs[4 + 3 * l], sib_ref=refs[5 + 3 * l]):
                sa = own_ref[...].astype(F32)
                sb = sib_ref[0].astype(F32)
                for k in range(3):
                    sa = sa + st_ref[k].astype(F32)
                    sb = sb + sib_ref[k + 1].astype(F32)
                update(sa + sb)

    out_shape = [jax.ShapeDtypeStruct((r, cdim), F32)] * 4
    if not paired:
        spec = pl.BlockSpec((t, cdim), lambda i: (i, 0))
        return pl.pallas_call(
            body, name=name, grid=(r // t,), in_specs=[spec] * 4, out_specs=[spec] * 4, out_shape=out_shape,
            compiler_params=_params(("parallel",)),
        )(w, m, v, grads)

    spec = pl.BlockSpec((t, cdim), lambda i, blk: (i, 0))
    ins, in_specs = [w, m, v], [spec] * 3
    for l, (g, ax, stack, sib) in enumerate(grads):
        row = lambda i, l=l: jnp.clip(i - l * per_layer, 0, per_layer - 1)
        own = ((lambda i, blk, row=row: (row(i), blk[0])) if ax == 1
               else (lambda i, blk, row=row: (blk[0] * per_layer + row(i), 0)))
        ins += [g, stack, sib]
        in_specs += [pl.BlockSpec((t, cdim), own),
                     pl.BlockSpec((3, t, cdim), lambda i, blk, row=row: (0, row(i), 0)),
                     pl.BlockSpec((4, t, cdim), lambda i, blk, row=row: (0, row(i), 0))]
    return pl.pallas_call(
        body, name=name,
        grid_spec=pltpu.PrefetchScalarGridSpec(
            num_scalar_prefetch=1, grid=(r // t,), in_specs=in_specs, out_specs=[spec] * 4),
        out_shape=out_shape, compiler_params=_params(("parallel",)),
    )(_my_block()[None], *ins)


def _local_step(x, target, gains, conv_ws, kv_gain, weights_of, send_grads):
    depth = len(gains)
    n_a = len(conv_ws)
    saved, ws = [], []
    kv = kvn = None
    _, (xn,) = _norm_res_fwd(x, None, None, [gains[0][0]], "norm_first")
    h = x
    for l in range(depth):
        g = gains[l]
        sv = {"x_in": h, "xn": xn}
        w = weights_of(l, "mix", h)
        ws.append(w)
        if l == n_a:
            kv = _mm(kvn, w["kv"], "nn", F32, "kv_fwd")
        if l < n_a:
            p = _mm(xn, w["conv_in"], "nn", BF16, f"conv_in_fwd_{l}")
            z = _conv_gate_fwd(p, conv_ws[l], f"conv_gate_fwd_{l}")
            mix = _mm(z, w["conv_out"], "nn", BF16, f"conv_out_fwd_{l}")
            sv.update(p=p, z=z)
        else:
            j = l - n_a
            q = _mm(xn, w["q"], "nn", F32, f"q_fwd_{j}", scale=HEAD_DIM ** -0.5)
            o, lse = _attention_fwd(q, kv, f"attn_fwd_{j}")
            mix = _mm(o, w["o"], "nn", BF16, f"o_fwd_{j}")
            sv.update(q=q, o=o, lse=lse)
        x1, (xn2,) = _norm_res_fwd(h, mix, g[1], [g[2]], f"norm_mid_{l}")
        w.update(weights_of(l, "ffn", mix))
        f, a = _ffn_in_swiglu(xn2, w["ffn_in"], f"ffn_in_fwd_{l}")
        ff = _mm(a, w["ffn_out"], "nn", BF16, f"ffn_out_fwd_{l}")
        sv.update(mix=mix, x1=x1, xn2=xn2, f=f, a=a, ff=ff)
        saved.append(sv)
        if l == depth - 1:
            dx, loss = _norm_res_loss(x1, ff, g[3], target, "norm_loss")
        else:
            if l == n_a - 1:
                h, _ = _norm_res_fwd(x1, ff, g[3], [], f"norm_end_{l}")
                h = _permute16(h, False, "permute_stream")
                target = _permute16(target, False, "permute_target")
                _, (xn, kvn) = _norm_res_fwd(h, None, None, [gains[l + 1][0], kv_gain], "norm_permuted")
            else:
                h, (xn,) = _norm_res_fwd(x1, ff, g[3], [gains[l + 1][0]], f"norm_end_{l}")
    d_gains = [[None] * 4 for _ in range(depth)]
    d_conv = [None] * n_a
    d_kv_gain = None
    dkv = None
    _, _, dff, d_gains[depth - 1][3] = _norm_bwd(dx, [], None, (saved[-1]["ff"], gains[-1][3]), "norm_loss_bwd")
    for l in reversed(range(depth)):
        sv, g, w, grads = saved[l], gains[l], ws[l], {}
        grads["ffn_out"] =_mm(sv["a"], dff, "tn", BF16, f"ffn_out_dw_{l}")
        df = _ffn_out_dx_swiglu(dff, w["ffn_out"], sv["f"], f"ffn_out_dx_{l}")
        dxn2 = _mm(df, w["ffn_in"], "nt", BF16, f"ffn_in_dx_{l}")
        grads["ffn_in"] =_mm(sv["xn2"], df, "tn", BF16, f"ffn_in_dw_{l}")
        dx, (d_gains[l][2],), dmix, d_gains[l][1] = _norm_bwd(
            dx, [(dxn2, g[2])], sv["x1"], (sv["mix"], g[1]), f"norm_mid_bwd_{l}")
        dx, dmix = send_grads(l, "ffn", grads, [dx, dmix])
        if l < n_a:
            dz = _mm(dmix, w["conv_out"], "nt", BF16, f"conv_out_dx_{l}")
            grads["conv_out"] =_mm(sv["z"], dmix, "tn", BF16, f"conv_out_dw_{l}")
            dp, d_conv[l] = _conv_gate_bwd(sv["p"], dz, conv_ws[l], f"conv_gate_bwd_{l}")
            dxn = _mm(dp, w["conv_in"], "nt", BF16, f"conv_in_dx_{l}")
            grads["conv_in"] =_mm(sv["xn"], dp, "tn", BF16, f"conv_in_dw_{l}")
        else:
            j = l - n_a
            do = _mm(dmix, w["o"], "nt", F32, f"o_dx_{j}")
            grads["o"] =_mm(sv["o"], dmix, "tn", BF16, f"o_dw_{j}")
            dq, dkv = _attention_bwd(sv["q"], kv, sv["o"], do, sv["lse"], dkv, f"attn_bwd_{j}")
            scale = HEAD_DIM ** -0.5
            dxn = _mm(dq, w["q"], "nt", BF16, f"q_dx_{j}", scale=scale)
            grads["q"] =_mm(sv["xn"], dq, "tn", BF16, f"q_dw_{j}", scale=scale)
        branches = [(dxn, g[0])]
        if l == n_a:
            dkvn = _mm(dkv, w["kv"], "nt", BF16, "kv_dx")
            grads["kv"] =_mm(kvn, dkv, "tn", BF16, "kv_dw")
            branches.append((dkvn, kv_gain))
        post = (saved[l - 1]["ff"], gains[l - 1][3]) if l > 0 else None
        if l == n_a:
            dx, dgs, _, _ = _norm_bwd(dx, branches, sv["x_in"], None, f"norm_end_bwd_{l}")
            dx = _permute16(dx, True, "unpermute_stream")
            _, _, dff, dg_post = _norm_bwd(dx, [], None, post, "norm_boundary_bwd")
        else:
            dx, dgs, dff, dg_post = _norm_bwd(dx, branches, sv["x_in"], post, f"norm_end_bwd_{l}")
        if dff is None:
            send_grads(l, "mix", grads, [])
        else:
            dx, dff = send_grads(l, "mix", grads, [dx, dff])
        d_gains[l][0] = dgs[0]
        if l == n_a:
            d_kv_gain = dgs[1]
        if l > 0:
            d_gains[l - 1][3] = dg_post
    return loss, dx, d_gains, d_conv, d_kv_gain


BIG = (
    ("conv_in", 1), ("conv_out", 0), ("kv", 1), ("q", 0), ("o", 0), ("ffn_in", 1), ("ffn_out", 0))


def kernel(x, norm_g, conv_in_w, conv_w, conv_out_w, kv_norm_g, kv_w, q_w, o_w, ffn_in_w, ffn_out_w, loss_target, m_norm_g, m_conv_in_w, m_conv_w, m_conv_out_w, m_kv_norm_g, m_kv_w, m_q_w, m_o_w, m_ffn_in_w, m_ffn_out_w, v_norm_g, v_conv_in_w, v_conv_w, v_conv_out_w, v_kv_norm_g, v_kv_w, v_q_w, v_o_w, v_ffn_in_w, v_ffn_out_w):
    depth, _, dq = norm_g.shape
    d = 4 * dq
    n_a = conv_w.shape[0]
    big_w = {"conv_in": conv_in_w, "conv_out": conv_out_w, "kv": kv_w[None], "q": q_w, "o": o_w,
             "ffn_in": ffn_in_w, "ffn_out": ffn_out_w}
    big_m = {"conv_in": m_conv_in_w, "conv_out": m_conv_out_w, "kv": m_kv_w[None], "q": m_q_w, "o": m_o_w,
             "ffn_in": m_ffn_in_w, "ffn_out": m_ffn_out_w}
    big_v = {"conv_in": v_conv_in_w, "conv_out": v_conv_out_w, "kv": v_kv_w[None], "q": v_q_w, "o": v_o_w,
             "ffn_in": v_ffn_in_w, "ffn_out": v_ffn_out_w}

    n_gain, n_tap = depth * 4, n_a * conv_w.shape[1]
    small_rows = -(-(n_gain + n_tap + 1) // 8) * 8
    pad_rows = small_rows - n_gain - n_tap

    def pack_small(gains, taps):
        return jnp.concatenate([gains.reshape(n_gain, dq), taps.reshape(n_tap, dq), jnp.zeros((pad_rows, dq), F32)])

    axis_of = dict(BIG)

    def matrices_of(l, part):
        if part == "ffn":
            return [("ffn_in", l), ("ffn_out", l)]
        if l < n_a:
            return [("conv_in", l), ("conv_out", l)]
        return ([("kv", 0)] if l == n_a else []) + [("q", l - n_a), ("o", l - n_a)]

    halves = [(l, part) for l in range(depth) for part in ("mix", "ffn")]
    groups = {(l, part): [(*_cast_place(big_w[name], i, axis_of[name], BF16, f"place_{name}_{i}"), axis_of[name])
                          for name, i in matrices_of(l, part)] for l, part in halves}
    groups[halves[0]].append((*_cast_place(pack_small(norm_g, conv_w)[None], 0, 1, F32, "place_small"), 1))
    started = {halves[0]: _gather_start(groups[halves[0]], [], "gather_start_0_mix")[0]}

    def fetch(half, after):
        full = _gather_wait(groups[half], started[half], after, "gather_wait_%d_%s" % half)
        nxt = halves.index(half) + 1
        if nxt < len(halves):
            started[halves[nxt]], full = _gather_start(groups[halves[nxt]], full, "gather_start_%d_%s" % halves[nxt])
        return full

    first = fetch(halves[0], None)
    small = first[-1]
    gains = [[small[4 * l + i][None] for i in range(4)] for l in range(depth)]
    conv_ws = [small[n_gain + 3 * l:n_gain + 3 * l + 3] for l in range(n_a)]
    kv_gain = kv_norm_g[None]

    def weights_of(l, part, after):
        full = first if (l, part) == halves[0] else fetch((l, part), after)
        return {name: full[t] for t, (name, _) in enumerate(matrices_of(l, part))}

    sent, paired = {}, {}
    LAG = 2

    def to_sibling(half, carry, after):
        axes = [axis_of[name] for name, _ in matrices_of(*half)]
        full, stacks = _scatter_wait(axes, sent[half], after, "scatter_wait_%d_%s" % half)
        paired[half], carry = _pair_start(list(zip(full, axes, stacks)), carry, "pair_start_%d_%s" % half)
        return carry

    def send_grads(l, part, grads, carry):
        sent[l, part], carry = _scatter_start(
            [(grads[name], axis_of[name]) for name, _ in matrices_of(l, part)], carry, f"scatter_start_{l}_{part}")
        older = halves.index((l, part)) + LAG
        if older < len(halves) and carry:
            carry = to_sibling(halves[older], carry, carry[0])
        return carry

    loss, dx, d_gains, d_conv, d_kv_gain = _local_step(
        x[0], loss_target[0], gains, conv_ws, kv_gain, weights_of, send_grads)
    loss = lax.psum(loss, ("x", "y", "c"))

    small_g = jnp.concatenate([dg for row in d_gains for dg in row] + list(d_conv) + [d_kv_gain]
                              + [jnp.zeros((pad_rows - 1, d), F32)])
    small_g = _allreduce_small(small_g, "allreduce_small")
    blk = 2 * lax.axis_index("x") + lax.axis_index("y")
    mine_small = lax.dynamic_slice_in_dim(small_g, blk * dq, dq, axis=1)
    kv_rows = d // dq

    def pack_opt(gains_like, taps_like, kv_like):
        rows = jnp.concatenate([gains_like.reshape(n_gain, dq), taps_like.reshape(n_tap, dq), kv_like.reshape(kv_rows, dq)])
        extra = -rows.shape[0] % 8
        return jnp.concatenate([rows, jnp.zeros((extra, dq), F32)]) if extra else rows

    sw = pack_opt(norm_g, conv_w, kv_norm_g)
    sm = pack_opt(m_norm_g, m_conv_w, m_kv_norm_g)
    sv = pack_opt(v_norm_g, v_conv_w, v_kv_norm_g)
    sg = pack_opt(mine_small[:n_gain], mine_small[n_gain:n_gain + n_tap], small_g[n_gain + n_tap])
    s_out = _adamw(sw, sm, sv, sg, "adamw_small")

    def unpack(a):
        return (a[:n_gain].reshape(depth, 4, dq), a[n_gain:n_gain + n_tap].reshape(n_a, -1, dq),
                a[n_gain + n_tap:n_gain + n_tap + kv_rows].reshape(d))

    small_out = [unpack(a) for a in s_out]

    landed, big_out = {}, {}

    def update(name):
        for half in halves:
            if matrices_of(*half)[0] not in landed and any(n == name for n, _ in matrices_of(*half)):
                axes = [axis_of[n] for n, _ in matrices_of(*half)]
                landed.update(zip(matrices_of(*half), zip(*_pair_wait(axes, paired[half], "pair_wait_%d_%s" % half))))
        shp, ax = big_w[name].shape, axis_of[name]
        rows, cols = shp[0] * shp[1], shp[2]
        flat = lambda a: a.reshape(rows, cols)
        full, stacks, sibling = zip(*[landed[name, i] for i in range(shp[0])])
        res = _adamw(flat(big_w[name]), flat(big_m[name]), flat(big_v[name]),
                     [(full[i], ax, stacks[i], sibling[i]) for i in range(shp[0])], f"adamw_{name}")
        big_out[name] = [a.reshape(shp[1:] if name == "kv" else shp) for a in res]

    pending = [half for half in reversed(halves) if half not in paired]
    for half in pending[:-1]:
        to_sibling(half, [], None)
    late = [name for name, _ in BIG if any(n == name for n, _ in matrices_of(*pending[-1]))]
    for name, _ in BIG:
        if name not in late:
            update(name)
    to_sibling(pending[-1], [], big_out["ffn_out"][0])
    for name in late:
        update(name)

    def leaves(i):
        ng, cw_, kg = small_out[i]
        return [ng, big_out["conv_in"][i], cw_, big_out["conv_out"][i], kg, big_out["kv"][i], big_out["q"][i],
                big_out["o"][i], big_out["ffn_in"][i], big_out["ffn_out"][i]]

    return (loss, dx[None], *leaves(0), *leaves(1), *leaves(2), *leaves(3))
```

```python
import functools

import jax
import jax.numpy as jnp
import numpy as np
from jax import lax
from jax.experimental import pallas as pl
from jax.experimental.pallas import tpu as pltpu

F32 = jnp.float32
BF16 = jnp.bfloat16
HEAD_DIM = 64
DILATIONS = (1, 4, 16)
NORM_EPS = 1e-6
NEG_BIG = -1e30
VMEM_LIMIT = 48 * 1024 * 1024
ROW_TILE = 256
NORM_TILE = 512
LANE = 128
MESH = pl.DeviceIdType.MESH

ADAM_LR = 0.001
ADAM_B1 = 0.9
ADAM_B2 = 0.999
ADAM_EPS = 1e-08
ADAM_WD = 0.01
ADAM_STEP = 10

TILE_CANDIDATES = (1024, 1408, 768, 512, 384, 256, 128)


def _pick(dim, cands=TILE_CANDIDATES):
    for c in cands:
        if c <= dim and dim % c == 0:
            return c
    return dim


def _params(sem):
    return pltpu.CompilerParams(dimension_semantics=sem, vmem_limit_bytes=VMEM_LIMIT)


def _mm(a, b, mode, out_dtype, name, scale=None):
    a_planes = a.shape[0] if a.ndim == 3 else 1
    b_planes = b.shape[0] if b.ndim == 3 else 1
    if mode == "nn":
        m, k = a.shape[-2], a.shape[-1] * a_planes
        n = b.shape[1]
    elif mode == "nt":
        m, k = a.shape[-2], a.shape[-1] * a_planes
        n = b.shape[0]
    else:
        k, m = a.shape
        n = b.shape[-1] * b_planes
    tm, tn, tk = _pick(m), _pick(n // b_planes), _pick(k // a_planes, ((2048,) if mode == "tn" else ()) + TILE_CANDIDATES)
    nk = k // tk
    ka, nb = k // a_planes // tk, n // b_planes // tn
    if a_planes > 1:
        a_spec = pl.BlockSpec((None, tm, tk), lambda i, j, kk: (kk // ka, i, kk % ka))
    elif mode == "tn":
        a_spec = pl.BlockSpec((tk, tm), lambda i, j, kk: (kk, i))
    else:
        a_spec = pl.BlockSpec((tm, tk), lambda i, j, kk: (i, kk))
    if mode == "nn":
        b_spec = pl.BlockSpec((tk, tn), lambda i, j, kk: (kk, j))
        dims = (((1,), (0,)), ((), ()))
    elif mode == "nt":
        b_spec = pl.BlockSpec((tn, tk), lambda i, j, kk: (j, kk))
        dims = (((1,), (1,)), ((), ()))
    else:
        b_spec = (pl.BlockSpec((None, tk, tn), lambda i, j, kk: (j // nb, kk, j % nb)) if b_planes > 1
                  else pl.BlockSpec((tk, tn), lambda i, j, kk: (kk, j)))
        dims = (((0,), (0,)), ((), ()))

    def finish(acc):
        if scale is not None:
            acc = acc * scale
        return acc.astype(out_dtype)

    if nk == 1:
        def body(a_ref, b_ref, o_ref):
            o_ref[...] = finish(lax.dot_general(a_ref[...].astype(BF16), b_ref[...].astype(BF16), dims, preferred_element_type=F32))
        scratch = []
    else:
        def body(a_ref, b_ref, o_ref, acc_ref):
            kk = pl.program_id(2)

            @pl.when(kk == 0)
            def _():
                acc_ref[...] = jnp.zeros_like(acc_ref)

            acc_ref[...] += lax.dot_general(a_ref[...].astype(BF16), b_ref[...].astype(BF16), dims, preferred_element_type=F32)

            @pl.when(kk == nk - 1)
            def _():
                o_ref[...] = finish(acc_ref[...])
        scratch = [pltpu.VMEM((tm, tn), F32)]

    return pl.pallas_call(
        body, name=name,
        grid=(m // tm, n // tn, nk),
        in_specs=[a_spec, b_spec],
        out_specs=pl.BlockSpec((tm, tn), lambda i, j, kk: (i, j)),
        out_shape=jax.ShapeDtypeStruct((m, n), out_dtype),
        scratch_shapes=scratch,
        compiler_params=_params(("parallel", "parallel", "arbitrary")),
    )(a, b)


def _rstd(v):
    return lax.rsqrt(jnp.mean(v * v, axis=-1, keepdims=True) + NORM_EPS)


def _rms_bwd(dy, v, g, r):
    gy = dy * g
    dv = r * (gy - v * (r * r) * jnp.mean(gy * v, axis=-1, keepdims=True))
    return dv, dy * v * r


def _row_spec(t, width):
    return pl.BlockSpec((t, width), lambda i: (i, 0))


def _gain_spec(width):
    return pl.BlockSpec((1, width), lambda i: (0, 0))


def _norm_res_fwd(x, mix, g_post, pre_gains, name):
    s, d = x.shape
    t = _pick(s, (NORM_TILE, ROW_TILE))
    has_mix = mix is not None
    n_pre = len(pre_gains)

    def body(*refs):
        x_ref = refs[0]
        pos = 1
        x1 = x_ref[...]
        if has_mix:
            mv = refs[1][...].astype(F32)
            x1 = x1 + mv * _rstd(mv) * refs[2][...]
            pos = 3
        gains = refs[pos:pos + n_pre]
        outs = refs[pos + n_pre:]
        if has_mix:
            outs[0][...] = x1
            outs = outs[1:]
        r = _rstd(x1)
        for g_ref, o_ref in zip(gains, outs):
            o_ref[...] = (x1 * r * g_ref[...]).astype(BF16)

    ins = [x] + ([mix, g_post] if has_mix else []) + list(pre_gains)
    in_specs = [_row_spec(t, d)] + ([_row_spec(t, d), _gain_spec(d)] if has_mix else []) + [_gain_spec(d)] * n_pre
    out_shape = ([jax.ShapeDtypeStruct((s, d), F32)] if has_mix else []) + [jax.ShapeDtypeStruct((s, d), BF16)] * n_pre
    out_specs = [_row_spec(t, d)] * len(out_shape)
    res = pl.pallas_call(
        body, name=name, grid=(s // t,), in_specs=in_specs, out_specs=out_specs, out_shape=out_shape,
        compiler_params=_params(("parallel",)),
    )(*ins)
    if has_mix:
        return res[0], list(res[1:])
    return x, list(res)


def _mm_norm_res(a, b, x, g_post, pre_gain, name):
    m, k = a.shape
    d = b.shape[1]
    tm, tk = _pick(m, (512, 256)), _pick(k)
    nk = k // tk

    def body(a_ref, b_ref, x_ref, gp_ref, g_ref, mix_ref, x1_ref, xn_ref, acc_ref):
        kk = pl.program_id(1)

        @pl.when(kk == 0)
        def _():
            acc_ref[...] = jnp.zeros_like(acc_ref)

        acc_ref[...] += jnp.dot(a_ref[...].astype(BF16), b_ref[...], preferred_element_type=F32)

        @pl.when(kk == nk - 1)
        def _():
            mix = acc_ref[...].astype(BF16)
            mix_ref[...] = mix
            mv = mix.astype(F32)
            x1 = x_ref[...] + mv * _rstd(mv) * gp_ref[...]
            x1_ref[...] = x1
            xn_ref[...] = (x1 * _rstd(x1) * g_ref[...]).astype(BF16)

    rows = pl.BlockSpec((tm, d), lambda i, kk: (i, 0))
    gain = pl.BlockSpec((1, d), lambda i, kk: (0, 0))
    return pl.pallas_call(
        body, name=name, grid=(m // tm, nk),
        in_specs=[pl.BlockSpec((tm, tk), lambda i, kk: (i, kk)), pl.BlockSpec((tk, d), lambda i, kk: (kk, 0)),
                  rows, gain, gain],
        out_specs=[rows, rows, rows],
        out_shape=[jax.ShapeDtypeStruct((m, d), BF16), jax.ShapeDtypeStruct((m, d), F32),
                   jax.ShapeDtypeStruct((m, d), BF16)],
        scratch_shapes=[pltpu.VMEM((tm, d), F32)],
        compiler_params=_params(("parallel", "arbitrary")),
    )(a, b, x, g_post, pre_gain)


def _norm_res_loss(x, mix, g_post, target, name):
    s, d = x.shape
    t = _pick(s, (NORM_TILE, ROW_TILE))

    def body(x_ref, m_ref, g_ref, t_ref, dy_ref, loss_ref):
        mv = m_ref[...].astype(F32)
        y = x_ref[...] + mv * _rstd(mv) * g_ref[...]
        err = y - t_ref[...]
        dy_ref[...] = err * (1.0 / d)

        @pl.when(pl.program_id(0) == 0)
        def _():
            loss_ref[...] = jnp.zeros_like(loss_ref)

        loss_ref[...] += jnp.sum(err * err)

    dy, acc = pl.pallas_call(
        body, name=name, grid=(s // t,),
        in_specs=[_row_spec(t, d), _row_spec(t, d), _gain_spec(d), _row_spec(t, d)],
        out_specs=[_row_spec(t, d), pl.BlockSpec((8, LANE), lambda i: (0, 0))],
        out_shape=[jax.ShapeDtypeStruct((s, d), F32), jax.ShapeDtypeStruct((8, LANE), F32)],
        compiler_params=_params(("arbitrary",)),
    )(x, mix, g_post, target)
    return dy, acc[0, 0] * (0.5 / d)


def _norm_bwd(dx_out, branches, x_in, post, name):
    s, d = dx_out.shape
    t = _pick(s, (NORM_TILE, ROW_TILE))
    nb = len(branches)
    has_post = post is not None

    def body(*refs):
        dx_ref = refs[0]
        pos = 1
        dx = dx_ref[...]
        first = pl.program_id(0) == 0
        n_in = 1 + (1 + 2 * nb if nb else 0) + (2 if has_post else 0)
        outs = refs[n_in:]
        opos = 0
        if nb:
            xv = refs[pos][...]
            pos += 1
            r = _rstd(xv)
            dx_o = outs[0]
            opos = 1
            for _ in range(nb):
                dxn = refs[pos][...].astype(F32)
                g = refs[pos + 1][...]
                pos += 2
                dv, dg_rows = _rms_bwd(dxn, xv, g, r)
                dx = dx + dv
                dg_ref = outs[opos]
                opos += 1

                @pl.when(first)
                def _(dg_ref=dg_ref):
                    dg_ref[...] = jnp.zeros_like(dg_ref)

                dg_ref[...] += jnp.sum(dg_rows, axis=0, keepdims=True)
            dx_o[...] = dx
        if has_post:
            mv = refs[pos][...].astype(F32)
            g = refs[pos + 1][...]
            dm, dg_rows = _rms_bwd(dx, mv, g, _rstd(mv))
            outs[opos][...] = dm.astype(BF16)
            dg_ref = outs[opos + 1]

            @pl.when(first)
            def _():
                dg_ref[...] = jnp.zeros_like(dg_ref)

            dg_ref[...] += jnp.sum(dg_rows, axis=0, keepdims=True)

    ins, in_specs = [dx_out], [_row_spec(t, d)]
    out_shape, out_specs = [], []
    if nb:
        ins.append(x_in)
        in_specs.append(_row_spec(t, d))
        out_shape.append(jax.ShapeDtypeStruct((s, d), F32))
        out_specs.append(_row_spec(t, d))
        for dxn, g in branches:
            ins += [dxn, g]
            in_specs += [_row_spec(t, d), _gain_spec(d)]
            out_shape.append(jax.ShapeDtypeStruct((1, d), F32))
            out_specs.append(_gain_spec(d))
    if has_post:
        ins += [post[0], post[1]]
        in_specs += [_row_spec(t, d), _gain_spec(d)]
        out_shape += [jax.ShapeDtypeStruct((s, d), BF16), jax.ShapeDtypeStruct((1, d), F32)]
        out_specs += [_row_spec(t, d), _gain_spec(d)]
    res = pl.pallas_call(
        body, name=name, grid=(s // t,), in_specs=in_specs, out_specs=out_specs, out_shape=out_shape,
        compiler_params=_params(("arbitrary",)),
    )(*ins)
    res = list(res)
    dx_in = res.pop(0) if nb else dx_out
    dgs = [res.pop(0) for _ in range(nb)]
    dm, dg_post = (res[0], res[1]) if has_post else (None, None)
    return dx_in, dgs, dm, dg_post


HALO = 16


def _shift_down(u, prev, k):
    rows = lax.broadcasted_iota(jnp.int32, u.shape, 0)
    out = pltpu.roll(u, k, 0)
    for i in range(k):
        out = jnp.where(rows == i, prev[HALO - k + i:HALO - k + i + 1, :], out)
    return out


def _shift_up(u, nxt, k):
    n = u.shape[0]
    rows = lax.broadcasted_iota(jnp.int32, u.shape, 0)
    out = pltpu.roll(u, n - k, 0)
    for i in range(k):
        out = jnp.where(rows == n - k + i, nxt[i:i + 1, :], out)
    return out


def _conv_gate_fwd(p, cw, name):
    s, d3 = p.shape
    d = d3 // 3
    t = _pick(s, (ROW_TILE,))
    hb = t // HALO

    def body(p_ref, prev_ref, w_ref, z_ref):
        i = pl.program_id(0)
        pv = p_ref[...].astype(F32)
        b, u = pv[:, :d], pv[:, d:2 * d] * pv[:, 2 * d:]
        ph = prev_ref[...].astype(F32)
        up = jnp.where(i > 0, ph[:, d:2 * d] * ph[:, 2 * d:], 0.0)
        w = w_ref[...]
        y = w[0:1, :] * _shift_down(u, up, 2) + w[1:2, :] * _shift_down(u, up, 1) + w[2:3, :] * u
        z_ref[...] = (b * y).astype(BF16)

    return pl.pallas_call(
        body, name=name, grid=(s // t,),
        in_specs=[_row_spec(t, d3),
                  pl.BlockSpec((HALO, d3), lambda i: (jnp.maximum(i * hb - 1, 0), 0)),
                  pl.BlockSpec((3, d), lambda i: (0, 0))],
        out_specs=_row_spec(t, d),
        out_shape=jax.ShapeDtypeStruct((s, d), BF16),
        compiler_params=_params(("parallel",)),
    )(p, p, cw)


def _conv_gate_bwd(p, dz, cw, name):
    s, d3 = p.shape
    d = d3 // 3
    t = _pick(s, (ROW_TILE,))
    hb = t // HALO
    nt = s // t
    last_halo = s // HALO - 1

    def body(p_ref, prev_ref, next_ref, dz_ref, dznext_ref, w_ref, dp_ref, dw_ref):
        i = pl.program_id(0)
        pv = p_ref[...].astype(F32)
        b, c, h = pv[:, :d], pv[:, d:2 * d], pv[:, 2 * d:]
        u = c * h
        ph = prev_ref[...].astype(F32)
        up = jnp.where(i > 0, ph[:, d:2 * d] * ph[:, 2 * d:], 0.0)
        w = w_ref[...]
        u1, u2 = _shift_down(u, up, 1), _shift_down(u, up, 2)
        y = w[0:1, :] * u2 + w[1:2, :] * u1 + w[2:3, :] * u
        dz = dz_ref[...].astype(F32)
        dy = dz * b
        dyn = jnp.where(i < nt - 1, dznext_ref[...].astype(F32) * next_ref[...].astype(F32)[:, :d], 0.0)
        du = w[2:3, :] * dy + w[1:2, :] * _shift_up(dy, dyn, 1) + w[0:1, :] * _shift_up(dy, dyn, 2)
        dp_ref[:, :d] = (dz * y).astype(BF16)
        dp_ref[:, d:2 * d] = (du * h).astype(BF16)
        dp_ref[:, 2 * d:] = (du * c).astype(BF16)

        @pl.when(i == 0)
        def _():
            dw_ref[...] = jnp.zeros_like(dw_ref)

        dw_ref[0:1, :] += jnp.sum(dy * u2, axis=0, keepdims=True)
        dw_ref[1:2, :] += jnp.sum(dy * u1, axis=0, keepdims=True)
        dw_ref[2:3, :] += jnp.sum(dy * u, axis=0, keepdims=True)

    return pl.pallas_call(
        body, name=name, grid=(nt,),
        in_specs=[_row_spec(t, d3),
                  pl.BlockSpec((HALO, d3), lambda i: (jnp.maximum(i * hb - 1, 0), 0)),
                  pl.BlockSpec((HALO, d3), lambda i: (jnp.minimum((i + 1) * hb, last_halo), 0)),
                  _row_spec(t, d),
                  pl.BlockSpec((HALO, d), lambda i: (jnp.minimum((i + 1) * hb, last_halo), 0)),
                  pl.BlockSpec((3, d), lambda i: (0, 0))],
        out_specs=[_row_spec(t, d3), pl.BlockSpec((3, d), lambda i: (0, 0))],
        out_shape=[jax.ShapeDtypeStruct((s, d3), BF16), jax.ShapeDtypeStruct((3, d), F32)],
        compiler_params=_params(("arbitrary",)),
    )(p, p, p, dz, dz, cw)


FFN_ROWS, FFN_COLS = (512, 256), (1408, 768, 256, 128)


def _row_chunks(tm, rows=256):
    return [slice(r, r + min(rows, tm)) for r in range(0, tm, min(rows, tm))]


def _ffn_in_swiglu(xn, w_in, name):
    s, k = xn.shape
    ff = w_in.shape[1] // 2
    tm, tn = _pick(s, FFN_ROWS), _pick(ff, FFN_COLS)
    nj = ff // tn

    def body(x_ref, wg_ref, wu_ref, f_ref, a_ref):
        for rows in _row_chunks(tm):
            xv = x_ref[rows, :]
            gate = jnp.dot(xv, wg_ref[...], preferred_element_type=F32)
            up = jnp.dot(xv, wu_ref[...], preferred_element_type=F32)
            f_ref[0, rows, :] = gate.astype(BF16)
            f_ref[1, rows, :] = up.astype(BF16)
            a_ref[rows, :] = (gate * jax.nn.sigmoid(gate) * up).astype(BF16)

    return pl.pallas_call(
        body, name=name, grid=(nj, s // tm),
        in_specs=[pl.BlockSpec((tm, k), lambda j, i: (i, 0)),
                  pl.BlockSpec((k, tn), lambda j, i: (0, j)),
                  pl.BlockSpec((k, tn), lambda j, i: (0, nj + j))],
        out_specs=[pl.BlockSpec((2, tm, tn), lambda j, i: (0, i, j)), pl.BlockSpec((tm, tn), lambda j, i: (i, j))],
        out_shape=[jax.ShapeDtypeStruct((2, s, ff), BF16), jax.ShapeDtypeStruct((s, ff), BF16)],
        compiler_params=_params(("parallel", "parallel")),
    )(xn, w_in, w_in)


def _ffn_out_dx_swiglu(dff, w_out, f, name):
    s, d = dff.shape
    ff = w_out.shape[0]
    tm, tn = _pick(s, FFN_ROWS), _pick(ff, FFN_COLS)

    def body(d_ref, w_ref, f_ref, df_ref):
        for rows in _row_chunks(tm):
            da = lax.dot_general(d_ref[rows, :], w_ref[...], (((1,), (1,)), ((), ())), preferred_element_type=F32)
            gate = f_ref[0, rows, :].astype(F32)
            up = f_ref[1, rows, :].astype(F32)
            sg = jax.nn.sigmoid(gate)
            silu = gate * sg
            df_ref[0, rows, :] = (da * up * (sg + silu * (1.0 - sg))).astype(BF16)
            df_ref[1, rows, :] = (da * silu).astype(BF16)

    planes = pl.BlockSpec((2, tm, tn), lambda j, i: (0, i, j))
    return pl.pallas_call(
        body, name=name, grid=(ff // tn, s // tm),
        in_specs=[pl.BlockSpec((tm, d), lambda j, i: (i, 0)), pl.BlockSpec((tn, d), lambda j, i: (j, 0)), planes],
        out_specs=planes, out_shape=jax.ShapeDtypeStruct((2, s, ff), BF16),
        compiler_params=_params(("parallel", "parallel")),
    )(dff, w_out, f)


SUPER = 2048
RES = 16
PAIR = 128
L = 128
FWD_TOGETHER = 8
BWD_TOGETHER = 4


def _alibi_slopes(n_heads):
    h = np.arange(n_heads, dtype=np.float32) + 1.0
    return np.power(2.0, -8.0 * h / n_heads).astype(np.float32)


def _permute16(x, inverse, name):
    s, d = x.shape
    cw = LANE

    def body(x_ref, o_ref):
        if inverse:
            for m in range(L):
                o_ref[RES * m:RES * (m + 1), :] = x_ref[pl.ds(m, RES, stride=L), :]
        else:
            for r in range(RES):
                o_ref[L * r:L * (r + 1), :] = x_ref[pl.ds(r, L, stride=RES), :]

    spec = pl.BlockSpec((SUPER, cw), lambda i, j: (i, j))
    return pl.pallas_call(
        body, name=name, grid=(s // SUPER, d // cw), in_specs=[spec], out_specs=spec,
        out_shape=jax.ShapeDtypeStruct((s, d), x.dtype),
        compiler_params=_params(("parallel", "parallel")),
    )(x)


def _slope_table(d):
    nh = d // HEAD_DIM
    sl = _alibi_slopes(nh)
    tab = np.repeat(sl, HEAD_DIM).reshape(d // PAIR, 1, PAIR)
    return jnp.asarray(np.broadcast_to(tab, (d // PAIR, 8, PAIR)).copy())


def _geometry(dil):
    nch = RES // dil
    return nch, L // nch


def _band(dil):
    nch, w = _geometry(dil)
    sh = w.bit_length() - 1
    i = lax.broadcasted_iota(jnp.int32, (L, 2 * L), 0)
    j = lax.broadcasted_iota(jnp.int32, (L, 2 * L), 1)

    def pos(t):
        return jnp.bitwise_and(t, w - 1) * nch + jnp.right_shift(t, sh)

    delta = pos(i) + L - (pos(jnp.bitwise_and(j, L - 1)) + jnp.bitwise_and(j, L))
    return (delta * dil).astype(F32), (delta >= 0) & (delta <= L), j < L


def _fill_bias(bias_s, sl_ref):
    for b, dil in enumerate(DILATIONS):
        base, band, prev_half = _band(dil)
        for first in range(2):
            valid = band & jnp.logical_not(prev_half) if first else band
            for h in range(2):
                slope = sl_ref[0:1, HEAD_DIM * h:HEAD_DIM * h + 1]
                bias_s[(2 * b + first) * 2 + h] = jnp.where(valid, -slope * base, NEG_BIG)


def _bias_index(b, sb, n):
    first = jnp.logical_and(sb == 0, n == 0).astype(jnp.int32)
    return (2 * b + first) * 2


def _offsets(dil, res, n):
    nch, w = _geometry(dil)

    def al(v):
        return v if isinstance(v, int) else pl.multiple_of(v, w)

    q_off = [al((a * dil + res) * L + n * w) for a in range(nch)]
    k_off = [al((a * dil + res) * 2 * L + L + n * w) for a in range(nch)]
    kp_off = [al((a * dil + res) * 2 * L + L + n * w - w) for a in range(nch)]
    return q_off, k_off, kp_off, w


def _gather(ref, offs, w):
    parts = [ref[pl.ds(o, w), :] for o in offs]
    return parts[0] if len(parts) == 1 else jnp.concatenate(parts, axis=0)


def _scatter(ref, offs, w, val, add=False):
    for a, o in enumerate(offs):
        piece = val[a * w:(a + 1) * w, :]
        if add:
            ref[pl.ds(o, w), :] += piece
        else:
            ref[pl.ds(o, w), :] = piece


def _fill_key_buffer(buf, prev_ref, cur_ref):
    for r in range(RES):
        buf[2 * L * r:2 * L * r + L, :] = prev_ref[L * r:L * (r + 1), :]
        buf[2 * L * r + L:2 * L * (r + 1), :] = cur_ref[L * r:L * (r + 1), :]


def _two_heads(x, low):
    zero = jnp.zeros_like(x)
    return jnp.concatenate([jnp.where(low, x, zero), jnp.where(low, zero, x)], axis=0)


def _loop_blocks(dil, stages, together):
    together = max(together, dil) if dil < RES else together

    def it(i, c):
        if dil == RES:
            blocks = [(i * together + k, 0) for k in range(together)]
        else:
            blocks = [(res, i * (together // dil) + k) for k in range(together // dil) for res in range(dil)]
        state = [stages[0](res, n) for res, n in blocks]
        for stage in stages[1:]:
            state = [stage(res, n, prev) for (res, n), prev in zip(blocks, state)]
        for writes in state:
            for args in writes:
                _scatter(*args)
        return c

    lax.fori_loop(0, RES // together, it, 0)


NT = (((1,), (1,)), ((), ()))
TN = (((0,), (0,)), ((), ()))


def _attention_fwd(q, kv, name):
    s, d = q.shape
    g_n, ns = d // PAIR, s // SUPER

    def body(sl_ref, q_ref, kc_ref, kp_ref, vc_ref, vp_ref, o_ref, lse_ref, kbuf, vbuf, m_s, l_s, acc_s, bias_s):
        sb = pl.program_id(1)
        _fill_key_buffer(kbuf, kp_ref, kc_ref)
        _fill_key_buffer(vbuf, vp_ref, vc_ref)
        _fill_bias(bias_s, sl_ref)
        low = lax.broadcasted_iota(jnp.int32, (L, PAIR), 1) < HEAD_DIM
        low_k = lax.broadcasted_iota(jnp.int32, (2 * L, PAIR), 1) < HEAD_DIM
        ones_bd = _two_heads(jnp.ones((2 * L, PAIR), BF16), low_k)
        for bi, dil in enumerate(DILATIONS):
            first_branch, last_branch = bi == 0, bi == len(DILATIONS) - 1

            def scores(res, n, dil=dil):
                q_off, k_off, kp_off, w = _offsets(dil, res, n)
                qf = _gather(q_ref, q_off, w).astype(BF16)
                kcat = jnp.concatenate([_gather(kbuf, kp_off, w), _gather(kbuf, k_off, w)], axis=0).astype(BF16)
                return lax.dot_general(qf, _two_heads(kcat, low_k), NT, preferred_element_type=F32)

            def update(res, n, sc, bi=bi, dil=dil, first_branch=first_branch, last_branch=last_branch):
                q_off, k_off, kp_off, w = _offsets(dil, res, n)
                vcat = jnp.concatenate([_gather(vbuf, kp_off, w), _gather(vbuf, k_off, w)], axis=0).astype(BF16)
                v_ones = jnp.concatenate([_two_heads(vcat, low_k), ones_bd], axis=1)
                bias_at = _bias_index(bi, sb, n)
                if not first_branch:
                    m_prev = _gather(m_s, q_off, w)
                ps, m_new = [], []
                for h in range(2):
                    s_h = sc[:, 2 * L * h:2 * L * (h + 1)] + bias_s[bias_at + h]
                    mh = jnp.max(s_h, axis=1, keepdims=True)
                    if not first_branch:
                        mh = jnp.maximum(mh, m_prev[:, HEAD_DIM * h:HEAD_DIM * h + 1])
                    ps.append(jnp.exp(s_h - mh).astype(BF16))
                    m_new.append(mh)
                m_full = jnp.where(low, m_new[0], m_new[1])
                both = jnp.dot(jnp.concatenate(ps, axis=1), v_ones, preferred_element_type=F32)
                acc, l_full = both[:, :PAIR], both[:, PAIR:]
                if not first_branch:
                    alpha = jnp.exp(m_prev - m_full)
                    l_full = _gather(l_s, q_off, w) * alpha + l_full
                    acc = _gather(acc_s, q_off, w) * alpha + acc
                if last_branch:
                    return [(o_ref, q_off, w, acc / l_full, False), (lse_ref, q_off, w, m_full + jnp.log(l_full), False)]
                return [(m_s, q_off, w, m_full, False), (l_s, q_off, w, l_full, False), (acc_s, q_off, w, acc, False)]

            _loop_blocks(dil, [scores, update], FWD_TOGETHER)

    prev = lambda i: jnp.maximum(i - 1, 0)
    blk = pl.BlockSpec((SUPER, PAIR), lambda g, i: (i, g))
    in_specs = [pl.BlockSpec((None, 8, PAIR), lambda g, i: (g, 0, 0)), blk,
                pl.BlockSpec((SUPER, PAIR), lambda g, i: (i, g)),
                pl.BlockSpec((SUPER, PAIR), lambda g, i: (prev(i), g)),
                pl.BlockSpec((SUPER, PAIR), lambda g, i: (i, g_n + g)),
                pl.BlockSpec((SUPER, PAIR), lambda g, i: (prev(i), g_n + g))]
    return pl.pallas_call(
        body, name=name, grid=(g_n, ns), in_specs=in_specs, out_specs=[blk, blk],
        out_shape=[jax.ShapeDtypeStruct((s, d), F32)] * 2,
        scratch_shapes=([pltpu.VMEM((2 * SUPER, PAIR), F32)] * 2 + [pltpu.VMEM((SUPER, PAIR), F32)] * 3
                        + [pltpu.VMEM((4 * len(DILATIONS), L, 2 * L), F32)]),
        compiler_params=_params(("parallel", "parallel")),
    )(_slope_table(d), q, kv, kv, kv, kv)


def _attention_bwd(q, kv, o, do, lse, dkv_in, name):
    s, d = q.shape
    g_n, ns = d // PAIR, s // SUPER
    has_in = dkv_in is not None

    def body(*refs):
        sl_ref, q_ref, do_ref, o_ref, lse_ref, kc_ref, kp_ref, vc_ref, vp_ref = refs[:9]
        pos = 9
        if has_in:
            dkv_in_ref = refs[9]
            pos = 10
        dq_ref, dkv_ref, kbuf, vbuf, dkbuf, dvbuf, dq_s, bias_s = refs[pos:]
        step = pl.program_id(1)
        sb = ns - 1 - step
        _fill_key_buffer(kbuf, kp_ref, kc_ref)
        _fill_key_buffer(vbuf, vp_ref, vc_ref)
        _fill_bias(bias_s, sl_ref)

        @pl.when(step == 0)
        def _():
            dkbuf[...] = jnp.zeros_like(dkbuf)
            dvbuf[...] = jnp.zeros_like(dvbuf)

        @pl.when(step > 0)
        def _():
            for buf in (dkbuf, dvbuf):
                for r in range(RES):
                    buf[2 * L * r + L:2 * L * (r + 1), :] = buf[2 * L * r:2 * L * r + L, :]
                    buf[2 * L * r:2 * L * r + L, :] = jnp.zeros((L, PAIR), F32)

        low = lax.broadcasted_iota(jnp.int32, (L, PAIR), 1) < HEAD_DIM
        low_k = lax.broadcasted_iota(jnp.int32, (2 * L, PAIR), 1) < HEAD_DIM
        low_t = lax.broadcasted_iota(jnp.int32, (PAIR, 2 * L), 0) < HEAD_DIM
        for bi, dil in enumerate(DILATIONS):
            first_branch = bi == 0

            def scores(res, n, dil=dil):
                q_off, k_off, kp_off, w = _offsets(dil, res, n)
                qb = _gather(q_ref, q_off, w).astype(BF16)
                dof = _gather(do_ref, q_off, w)
                prod = dof * _gather(o_ref, q_off, w)
                dob = dof.astype(BF16)
                lse_f = _gather(lse_ref, q_off, w)
                zero = jnp.zeros_like(prod)
                dsum = (jnp.sum(jnp.where(low, prod, zero), axis=1, keepdims=True),
                        jnp.sum(jnp.where(low, zero, prod), axis=1, keepdims=True))
                kcat = jnp.concatenate([_gather(kbuf, kp_off, w), _gather(kbuf, k_off, w)], axis=0).astype(BF16)
                vcat = jnp.concatenate([_gather(vbuf, kp_off, w), _gather(vbuf, k_off, w)], axis=0).astype(BF16)
                k_bd, v_bd = _two_heads(kcat, low_k), _two_heads(vcat, low_k)
                sc = lax.dot_general(qb, k_bd, NT, preferred_element_type=F32)
                dp = lax.dot_general(dob, v_bd, NT, preferred_element_type=F32)
                return qb, dob, lse_f, dsum, k_bd, sc, dp

            def gradients(res, n, given, bi=bi, dil=dil, first_branch=first_branch):
                qb, dob, lse_f, dsum, k_bd, sc, dp = given
                q_off, k_off, kp_off, w = _offsets(dil, res, n)
                bias_at = _bias_index(bi, sb, n)
                ps, dss = [], []
                for h in range(2):
                    cols = slice(2 * L * h, 2 * L * (h + 1))
                    lse_h = lse_f[:, HEAD_DIM * h:HEAD_DIM * h + 1]
                    p_h = jnp.exp(sc[:, cols] + bias_s[bias_at + h] - lse_h)
                    dss.append((p_h * (dp[:, cols] - dsum[h])).astype(BF16))
                    ps.append(p_h.astype(BF16))
                ds_cat, p_cat = jnp.concatenate(dss, axis=1), jnp.concatenate(ps, axis=1)
                dq = jnp.dot(ds_cat, k_bd, preferred_element_type=F32)
                dk_t = lax.dot_general(qb, ds_cat, TN, preferred_element_type=F32)
                dv_t = lax.dot_general(dob, p_cat, TN, preferred_element_type=F32)
                dk = jnp.where(low_t, dk_t[:, :2 * L], dk_t[:, 2 * L:]).T
                dv = jnp.where(low_t, dv_t[:, :2 * L], dv_t[:, 2 * L:]).T
                return [(dq_s, q_off, w, dq, not first_branch),
                        (dkbuf, kp_off, w, dk[:L], True), (dkbuf, k_off, w, dk[L:], True),
                        (dvbuf, kp_off, w, dv[:L], True), (dvbuf, k_off, w, dv[L:], True)]

            _loop_blocks(dil, [scores, gradients], BWD_TOGETHER)

        dq_ref[...] = dq_s[...].astype(BF16)
        for r in range(RES):
            rows, cur = slice(L * r, L * (r + 1)), slice(2 * L * r + L, 2 * L * (r + 1))
            for plane, buf in enumerate((dkbuf, dvbuf)):
                if has_in:
                    dkv_ref[plane, rows, :] = buf[cur, :] + dkv_in_ref[plane, rows, :]
                else:
                    dkv_ref[plane, rows, :] = buf[cur, :]

    rev = lambda i: ns - 1 - i
    prev = lambda i: jnp.maximum(ns - 2 - i, 0)
    blk = pl.BlockSpec((SUPER, PAIR), lambda g, i: (rev(i), g))
    in_specs = [pl.BlockSpec((None, 8, PAIR), lambda g, i: (g, 0, 0)), blk, blk, blk, blk,
                pl.BlockSpec((SUPER, PAIR), lambda g, i: (rev(i), g)),
                pl.BlockSpec((SUPER, PAIR), lambda g, i: (prev(i), g)),
                pl.BlockSpec((SUPER, PAIR), lambda g, i: (rev(i), g_n + g)),
                pl.BlockSpec((SUPER, PAIR), lambda g, i: (prev(i), g_n + g))]
    ins = [_slope_table(d), q, do, o, lse, kv, kv, kv, kv]
    planes = pl.BlockSpec((2, SUPER, PAIR), lambda g, i: (0, rev(i), g))
    if has_in:
        in_specs.append(planes)
        ins.append(dkv_in)
    res = pl.pallas_call(
        body, name=name, grid=(g_n, ns), in_specs=in_specs, out_specs=[blk, planes],
        out_shape=[jax.ShapeDtypeStruct((s, d), BF16), jax.ShapeDtypeStruct((2, s, d), F32)],
        scratch_shapes=([pltpu.VMEM((2 * SUPER, PAIR), F32)] * 4 + [pltpu.VMEM((SUPER, PAIR), F32)]
                        + [pltpu.VMEM((4 * len(DILATIONS), L, 2 * L), F32)]),
        compiler_params=_params(("parallel", "arbitrary")),
    )(*ins)
    return res[0], res[1]


def _coords():
    return lax.axis_index("x"), lax.axis_index("y"), lax.axis_index("c")


def _chip_peers(x, y):
    return [(1 - x, y), (x, 1 - y), (1 - x, 1 - y)]


def _block_of(ref, axis, blk, size):
    start = pl.multiple_of(blk * size, size)
    if axis == 1:
        return ref.at[:, pl.ds(start, size)]
    return ref.at[pl.ds(start, size), :]


ANY = pl.BlockSpec(memory_space=pl.ANY)


HBM = pl.BlockSpec(memory_space=pltpu.HBM)
SEM = pl.BlockSpec(memory_space=pltpu.SEMAPHORE)
SPLIT = pltpu.CompilerParams(has_side_effects=pltpu.SideEffectType.DATAFLOW_SIDE_EFFECTING)


def _in_hbm(a):
    return pltpu.with_memory_space_constraint(a, pltpu.HBM)


def _thru(arrays):
    return [pltpu.HBM(a.shape, a.dtype) for a in arrays]


def _cast_place(w, layer, ax, dtype, name):
    _, k, n = w.shape
    t = _pick(k, (512, 256, 128))
    nb = k // t

    def body(blk_ref, w_ref, b_ref, f_ref):
        v = w_ref[...].astype(dtype)
        b_ref[...] = v
        f_ref[...] = v

    full_shape = (k, 4 * n) if ax == 1 else (4 * k, n)
    place = (lambda i, blk: (i, blk[0])) if ax == 1 else (lambda i, blk: (blk[0] * nb + i, 0))
    return pl.pallas_call(
        body, name=name,
        grid_spec=pltpu.PrefetchScalarGridSpec(
            num_scalar_prefetch=1, grid=(nb,),
            in_specs=[pl.BlockSpec((None, t, n), lambda i, blk: (layer, i, 0))],
            out_specs=[pl.BlockSpec((t, n), lambda i, blk: (i, 0)), pl.BlockSpec((t, n), place)]),
        out_shape=[jax.ShapeDtypeStruct((k, n), dtype), jax.ShapeDtypeStruct(full_shape, dtype)],
        compiler_params=_params(("parallel",)),
    )(_my_block()[None], w)


def _my_block():
    return (2 * lax.axis_index("x") + lax.axis_index("y")).astype(jnp.int32)


def _gather_start(group, carry, name):
    n, nc = len(group), len(carry)

    def body(*refs):
        blocks, fulls, send_sem, recv_sem = refs[:n], refs[n:2 * n], refs[2 * n + nc], refs[2 * n + nc + 1]
        x, y, c = _coords()
        for t, (b, _, ax) in enumerate(group):
            mine = _block_of(fulls[t], ax, 2 * x + y, b.shape[ax])
            for j, (px, py) in enumerate(_chip_peers(x, y)):
                pltpu.make_async_remote_copy(
                    src_ref=blocks[t], dst_ref=mine, send_sem=send_sem.at[3 * t + j], recv_sem=recv_sem.at[3 * t + j],
                    device_id=(px, py, c), device_id_type=MESH).start()

    arrays = [b for b, _, _ in group] + [f for _, f, _ in group] + list(carry)
    sems = [pltpu.SemaphoreType.DMA((3 * n,))] * 2
    res = pl.pallas_call(
        body, name=name, in_specs=[HBM] * len(arrays), out_specs=[SEM, SEM] + [HBM] * len(arrays),
        out_shape=sems + _thru(arrays), input_output_aliases={i: 2 + i for i in range(len(arrays))},
        compiler_params=SPLIT,
    )(*[_in_hbm(a) for a in arrays])
    return (res[0], res[1], list(res[2:2 + n]), list(res[2 + n:2 + 2 * n])), list(res[2 + 2 * n:])


def _gather_wait(group, started, after, name):
    sends, recvs, blocks, fulls = started
    m = len(group)

    def body(*refs):
        blk_refs, full_refs, send_sem, recv_sem = refs[:m], refs[m:2 * m], refs[2 * m], refs[2 * m + 1]
        x, y, c = _coords()
        for t, (b, _, ax) in enumerate(group):
            for j, (px, py) in enumerate(_chip_peers(x, y)):
                cp = pltpu.make_async_remote_copy(
                    src_ref=blk_refs[t], dst_ref=_block_of(full_refs[t], ax, 2 * px + py, b.shape[ax]),
                    send_sem=send_sem.at[3 * t + j], recv_sem=recv_sem.at[3 * t + j],
                    device_id=(px, py, c), device_id_type=MESH)
                cp.wait_send()
                cp.wait_recv()

    extra = [] if after is None else [after]
    res = pl.pallas_call(
        body, name=name, in_specs=[HBM] * (2 * m) + [SEM, SEM] + [ANY] * len(extra), out_specs=[HBM] * (2 * m),
        out_shape=_thru(blocks) + _thru(fulls), input_output_aliases={i: i for i in range(2 * m)},
        compiler_params=SPLIT,
    )(*blocks, *fulls, sends, recvs, *extra)
    return list(res[m:])


def _scatter_start(grads, carry, name):
    n = len(grads)
    n_in = 2 * n + len(carry)

    def body(*refs):
        g_refs, st_refs, send_sem, recv_sem = refs[:n], refs[n:2 * n], refs[n_in], refs[n_in + 1]
        x, y, c = _coords()
        for t, (g, ax) in enumerate(grads):
            for j, (px, py) in enumerate(_chip_peers(x, y)):
                pltpu.make_async_remote_copy(
                    src_ref=_block_of(g_refs[t], ax, 2 * px + py, g.shape[ax] // 4), dst_ref=st_refs[t].at[j],
                    send_sem=send_sem.at[3 * t + j], recv_sem=recv_sem.at[3 * t + j],
                    device_id=(px, py, c), device_id_type=MESH).start()

    arrays = [g for g, _ in grads]
    for g, ax in grads:
        shape = list(g.shape)
        shape[ax] //= 4
        arrays.append(lax.empty((3, *shape), g.dtype))
    arrays += list(carry)
    sems = [pltpu.SemaphoreType.DMA((3 * n,))] * 2
    res = pl.pallas_call(
        body, name=name, in_specs=[HBM] * n_in, out_specs=[SEM, SEM] + [HBM] * n_in,
        out_shape=sems + _thru(arrays), input_output_aliases={i: 2 + i for i in range(n_in)},
        compiler_params=SPLIT,
    )(*[_in_hbm(a) for a in arrays])
    return (res[0], res[1], list(res[2:2 + n]), list(res[2 + n:2 + 2 * n])), list(res[2 + 2 * n:])


def _scatter_wait(axes, started, after, name):
    sends, recvs, full, stacks = started
    n = len(full)
    extra = [] if after is None else [after]

    def body(*refs):
        g_refs, st_refs, send_sem, recv_sem = refs[:n], refs[n:2 * n], refs[2 * n], refs[2 * n + 1]
        x, y, c = _coords()
        for t, ax in enumerate(axes):
            size = full[t].shape[ax] // 4
            for j, (px, py) in enumerate(_chip_peers(x, y)):
                cp = pltpu.make_async_remote_copy(
                    src_ref=_block_of(g_refs[t], ax, 2 * px + py, size), dst_ref=st_refs[t].at[j],
                    send_sem=send_sem.at[3 * t + j], recv_sem=recv_sem.at[3 * t + j],
                    device_id=(px, py, c), device_id_type=MESH)
                cp.wait_send()
                cp.wait_recv()

    res = pl.pallas_call(
        body, name=name, in_specs=[HBM] * (2 * n) + [SEM, SEM] + [ANY] * len(extra), out_specs=[HBM] * (2 * n),
        out_shape=_thru(full) + _thru(stacks), input_output_aliases={i: i for i in range(2 * n)},
        compiler_params=SPLIT,
    )(*full, *stacks, sends, recvs, *extra)
    return list(res[:n]), list(res[n:])


def _pair_copies(g_refs, st_refs, out_refs, items, send_sem, recv_sem):
    x, y, c = _coords()
    copies = []
    for u, (g, ax, _) in enumerate(items):
        own = _block_of(g_refs[u], ax, 2 * x + y, g.shape[ax] // 4)
        for k, (src, dst) in enumerate([(own, out_refs[u].at[0]), (st_refs[u], out_refs[u].at[pl.ds(1, 3)])]):
            copies.append(pltpu.make_async_remote_copy(
                src_ref=src, dst_ref=dst, send_sem=send_sem.at[2 * u + k], recv_sem=recv_sem.at[2 * u + k],
                device_id=(x, y, 1 - c), device_id_type=MESH))
    return copies


def _pair_start(items, carry, name):
    n = len(items)
    n_in = 3 * n + len(carry)

    def body(*refs):
        for cp in _pair_copies(refs[:n], refs[n:2 * n], refs[2 * n:3 * n], items, refs[n_in], refs[n_in + 1]):
            cp.start()

    arrays = ([g for g, _, _ in items] + [st for _, _, st in items]
              + [lax.empty((4, *st.shape[1:]), st.dtype) for _, _, st in items] + list(carry))
    sems = [pltpu.SemaphoreType.DMA((2 * n,))] * 2
    res = pl.pallas_call(
        body, name=name, in_specs=[HBM] * n_in, out_specs=[SEM, SEM] + [HBM] * n_in,
        out_shape=sems + _thru(arrays), input_output_aliases={i: 2 + i for i in range(n_in)},
        compiler_params=SPLIT,
    )(*[_in_hbm(a) for a in arrays])
    thru = res[2:]
    return (res[0], res[1], *(list(thru[k * n:(k + 1) * n]) for k in range(3))), list(thru[3 * n:])


def _pair_wait(axes, started, name):
    send, recv, full, stacks, landing = started
    n = len(full)
    items = [(full[u], axes[u], stacks[u]) for u in range(n)]

    def body(*refs):
        for cp in _pair_copies(refs[:n], refs[n:2 * n], refs[2 * n:3 * n], items, refs[3 * n], refs[3 * n + 1]):
            cp.wait_send()
            cp.wait_recv()

    res = pl.pallas_call(
        body, name=name, in_specs=[HBM] * (3 * n) + [SEM, SEM], out_specs=[HBM] * (3 * n),
        out_shape=_thru(full + stacks + landing), input_output_aliases={i: i for i in range(3 * n)},
        compiler_params=SPLIT,
    )(*full, *stacks, *landing, send, recv)
    return list(res[:n]), list(res[n:2 * n]), list(res[2 * n:])


def _allreduce_small(v, name):
    r, cdim = v.shape

    def body(v_ref, out_ref, buf, send_sems, recv_sems):
        x, y, c = _coords()
        me = 4 * x + 2 * y + c
        buf[0] = v_ref[...]
        sends = []
        for k in range(1, 8):
            peer = (x if not (k & 4) else 1 - x, y if not (k & 2) else 1 - y, c if not (k & 1) else 1 - c)
            cp = pltpu.make_async_remote_copy(
                src_ref=v_ref, dst_ref=buf.at[k], send_sem=send_sems.at[k - 1], recv_sem=recv_sems.at[k - 1],
                device_id=peer, device_id_type=MESH)
            cp.start()
            sends.append(cp)
        for cp in sends:
            cp.wait_recv()
        total = buf[me]
        for src in range(1, 8):
            total = total + buf[jnp.bitwise_xor(me, src)]
        out_ref[...] = total
        for cp in sends:
            cp.wait_send()

    return pl.pallas_call(
        body, name=name,
        in_specs=[pl.BlockSpec(memory_space=pltpu.VMEM)], out_specs=pl.BlockSpec(memory_space=pltpu.VMEM),
        out_shape=jax.ShapeDtypeStruct((r, cdim), F32),
        scratch_shapes=[pltpu.VMEM((8, r, cdim), F32), pltpu.SemaphoreType.DMA((7,)), pltpu.SemaphoreType.DMA((7,))],
        compiler_params=pltpu.CompilerParams(has_side_effects=True),
    )(v)


def _adamw_math(w, g, m, v):
    m = ADAM_B1 * m + (1.0 - ADAM_B1) * g
    v = ADAM_B2 * v + (1.0 - ADAM_B2) * jnp.square(g)
    m_hat = m / (1.0 - ADAM_B1 ** ADAM_STEP)
    v_hat = v / (1.0 - ADAM_B2 ** ADAM_STEP)
    delta = -ADAM_LR * (m_hat / (jnp.sqrt(v_hat) + ADAM_EPS) + ADAM_WD * w)
    return delta, m, v


def _adamw(w, m, v, grads, name):
    r, cdim = w.shape
    paired = isinstance(grads, list)
    layers = len(grads) if paired else 1
    t = _pick(r // layers, (128, 64, 32, 16, 8))
    per_layer = r // layers // t
    n_grad = 3 * layers if paired else 1

    def body(*refs):
        refs = refs[1:] if paired else refs
        w_ref, m_ref, v_ref = refs[:3]
        outs = refs[3 + n_grad:]

        def update(g):
            delta, m_new, v_new = _adamw_math(w_ref[...], g, m_ref[...], v_ref[...])
            outs[0][...] = g
            outs[1][...] = delta
            outs[2][...] = m_new
            outs[3][...] = v_new

        if not paired:
            update(refs[3][...])
            return
        layer = pl.program_id(0) // per_layer
        for l in range(layers):
            @pl.when(layer == l)
            def _(own_ref=refs[3 + 3 * l], st_ref=refs[4 + 3 * l], sib_ref=refs[5 + 3 * l]):
                sa = own_ref[...].astype(F32)
                sb = sib_ref[0].astype(F32)
                for k in range(3):
                    sa = sa + st_ref[k].astype(F32)
                    sb = sb + sib_ref[k + 1].astype(F32)
                update(sa + sb)

    out_shape = [jax.ShapeDtypeStruct((r, cdim), F32)] * 4
    if not paired:
        spec = pl.BlockSpec((t, cdim), lambda i: (i, 0))
        return pl.pallas_call(
            body, name=name, grid=(r // t,), in_specs=[spec] * 4, out_specs=[spec] * 4, out_shape=out_shape,
            compiler_params=_params(("parallel",)),
        )(w, m, v, grads)

    spec = pl.BlockSpec((t, cdim), lambda i, blk: (i, 0))
    ins, in_specs = [w, m, v], [spec] * 3
    for l, (g, ax, stack, sib) in enumerate(grads):
        row = lambda i, l=l: jnp.clip(i - l * per_layer, 0, per_layer - 1)
        own = ((lambda i, blk, row=row: (row(i), blk[0])) if ax == 1
               else (lambda i, blk, row=row: (blk[0] * per_layer + row(i), 0)))
        ins += [g, stack, sib]
        in_specs += [pl.BlockSpec((t, cdim), own),
                     pl.BlockSpec((3, t, cdim), lambda i, blk, row=row: (0, row(i), 0)),
                     pl.BlockSpec((4, t, cdim), lambda i, blk, row=row: (0, row(i), 0))]
    return pl.pallas_call(
        body, name=name,
        grid_spec=pltpu.PrefetchScalarGridSpec(
            num_scalar_prefetch=1, grid=(r // t,), in_specs=in_specs, out_specs=[spec] * 4),
        out_shape=out_shape, compiler_params=_params(("parallel",)),
    )(_my_block()[None], *ins)


def _local_step(x, target, gains, conv_ws, kv_gain, weights_of, send_grads):
    depth = len(gains)
    n_a = len(conv_ws)
    saved, ws = [], []
    kv = kvn = None
    _, (xn,) = _norm_res_fwd(x, None, None, [gains[0][0]], "norm_first")
    h = x
    for l in range(depth):
        g = gains[l]
        sv = {"x_in": h, "xn": xn}
        w = weights_of(l, "mix", h)
        ws.append(w)
        if l == n_a:
            kv = _mm(kvn, w["kv"], "nn", F32, "kv_fwd")
        if l < n_a:
            p = _mm(xn, w["conv_in"], "nn", BF16, f"conv_in_fwd_{l}")
            z = _conv_gate_fwd(p, conv_ws[l], f"conv_gate_fwd_{l}")
            mix, x1, xn2 = _mm_norm_res(z, w["conv_out"], h, g[1], g[2], f"conv_out_fwd_{l}")
            sv.update(p=p, z=z)
        else:
            j = l - n_a
            q = _mm(xn, w["q"], "nn", F32, f"q_fwd_{j}", scale=HEAD_DIM ** -0.5)
            o, lse = _attention_fwd(q, kv, f"attn_fwd_{j}")
            mix, x1, xn2 = _mm_norm_res(o, w["o"], h, g[1], g[2], f"o_fwd_{j}")
            sv.update(q=q, o=o, lse=lse)
        w.update(weights_of(l, "ffn", mix))
        f, a = _ffn_in_swiglu(xn2, w["ffn_in"], f"ffn_in_fwd_{l}")
        if l == depth - 1:
            ff = _mm(a, w["ffn_out"], "nn", BF16, f"ffn_out_fwd_{l}")
            dx, loss = _norm_res_loss(x1, ff, g[3], target, "norm_loss")
        elif l == n_a - 1:
            ff = _mm(a, w["ffn_out"], "nn", BF16, f"ffn_out_fwd_{l}")
            h, _ = _norm_res_fwd(x1, ff, g[3], [], f"norm_end_{l}")
            h = _permute16(h, False, "permute_stream")
            _, (xn, kvn) = _norm_res_fwd(h, None, None, [gains[l + 1][0], kv_gain], "norm_permuted")
        else:
            ff, h, xn = _mm_norm_res(a, w["ffn_out"], x1, g[3], gains[l + 1][0], f"ffn_out_fwd_{l}")
        sv.update(mix=mix, x1=x1, xn2=xn2, f=f, a=a, ff=ff)
        saved.append(sv)
    d_gains = [[None] * 4 for _ in range(depth)]
    d_conv = [None] * n_a
    d_kv_gain = None
    dkv = None
    _, _, dff, d_gains[depth - 1][3] = _norm_bwd(dx, [], None, (saved[-1]["ff"], gains[-1][3]), "norm_loss_bwd")
    for l in reversed(range(depth)):
        sv, g, w, grads = saved[l], gains[l], ws[l], {}
        grads["ffn_out"] =_mm(sv["a"], dff, "tn", BF16, f"ffn_out_dw_{l}")
        df = _ffn_out_dx_swiglu(dff, w["ffn_out"], sv["f"], f"ffn_out_dx_{l}")
        dxn2 = _mm(df, w["ffn_in"], "nt", BF16, f"ffn_in_dx_{l}")
        grads["ffn_in"] =_mm(sv["xn2"], df, "tn", BF16, f"ffn_in_dw_{l}")
        dx, (d_gains[l][2],), dmix, d_gains[l][1] = _norm_bwd(
            dx, [(dxn2, g[2])], sv["x1"], (sv["mix"], g[1]), f"norm_mid_bwd_{l}")
        dx, dmix = send_grads(l, "ffn", grads, [dx, dmix])
        if l < n_a:
            dz = _mm(dmix, w["conv_out"], "nt", BF16, f"conv_out_dx_{l}")
            grads["conv_out"] =_mm(sv["z"], dmix, "tn", BF16, f"conv_out_dw_{l}")
            dp, d_conv[l] = _conv_gate_bwd(sv["p"], dz, conv_ws[l], f"conv_gate_bwd_{l}")
            dxn = _mm(dp, w["conv_in"], "nt", BF16, f"conv_in_dx_{l}")
            grads["conv_in"] =_mm(sv["xn"], dp, "tn", BF16, f"conv_in_dw_{l}")
        else:
            j = l - n_a
            do = _mm(dmix, w["o"], "nt", F32, f"o_dx_{j}")
            grads["o"] =_mm(sv["o"], dmix, "tn", BF16, f"o_dw_{j}")
            dq, dkv = _attention_bwd(sv["q"], kv, sv["o"], do, sv["lse"], dkv, f"attn_bwd_{j}")
            scale = HEAD_DIM ** -0.5
            dxn = _mm(dq, w["q"], "nt", BF16, f"q_dx_{j}", scale=scale)
            grads["q"] =_mm(sv["xn"], dq, "tn", BF16, f"q_dw_{j}", scale=scale)
        branches = [(dxn, g[0])]
        if l == n_a:
            dkvn = _mm(dkv, w["kv"], "nt", BF16, "kv_dx")
            grads["kv"] =_mm(kvn, dkv, "tn", BF16, "kv_dw")
            branches.append((dkvn, kv_gain))
        post = (saved[l - 1]["ff"], gains[l - 1][3]) if l > 0 else None
        if l == n_a:
            dx, dgs, _, _ = _norm_bwd(dx, branches, sv["x_in"], None, f"norm_end_bwd_{l}")
            dx = _permute16(dx, True, "unpermute_stream")
            _, _, dff, dg_post = _norm_bwd(dx, [], None, post, "norm_boundary_bwd")
        else:
            dx, dgs, dff, dg_post = _norm_bwd(dx, branches, sv["x_in"], post, f"norm_end_bwd_{l}")
        if dff is None:
            send_grads(l, "mix", grads, [])
        else:
            dx, dff = send_grads(l, "mix", grads, [dx, dff])
        d_gains[l][0] = dgs[0]
        if l == n_a:
            d_kv_gain = dgs[1]
        if l > 0:
            d_gains[l - 1][3] = dg_post
    return loss, dx, d_gains, d_conv, d_kv_gain


BIG = (
    ("conv_in", 1), ("conv_out", 0), ("kv", 1), ("q", 0), ("o", 0), ("ffn_in", 1), ("ffn_out", 0))


def kernel(x, norm_g, conv_in_w, conv_w, conv_out_w, kv_norm_g, kv_w, q_w, o_w, ffn_in_w, ffn_out_w, loss_target, m_norm_g, m_conv_in_w, m_conv_w, m_conv_out_w, m_kv_norm_g, m_kv_w, m_q_w, m_o_w, m_ffn_in_w, m_ffn_out_w, v_norm_g, v_conv_in_w, v_conv_w, v_conv_out_w, v_kv_norm_g, v_kv_w, v_q_w, v_o_w, v_ffn_in_w, v_ffn_out_w):
    depth, _, dq = norm_g.shape
    d = 4 * dq
    n_a = conv_w.shape[0]
    big_w = {"conv_in": conv_in_w, "conv_out": conv_out_w, "kv": kv_w[None], "q": q_w, "o": o_w,
             "ffn_in": ffn_in_w, "ffn_out": ffn_out_w}
    big_m = {"conv_in": m_conv_in_w, "conv_out": m_conv_out_w, "kv": m_kv_w[None], "q": m_q_w, "o": m_o_w,
             "ffn_in": m_ffn_in_w, "ffn_out": m_ffn_out_w}
    big_v = {"conv_in": v_conv_in_w, "conv_out": v_conv_out_w, "kv": v_kv_w[None], "q": v_q_w, "o": v_o_w,
             "ffn_in": v_ffn_in_w, "ffn_out": v_ffn_out_w}

    n_gain, n_tap = depth * 4, n_a * conv_w.shape[1]
    small_rows = -(-(n_gain + n_tap + 1) // 8) * 8
    pad_rows = small_rows - n_gain - n_tap

    def pack_small(gains, taps):
        return jnp.concatenate([gains.reshape(n_gain, dq), taps.reshape(n_tap, dq), jnp.zeros((pad_rows, dq), F32)])

    axis_of = dict(BIG)

    def matrices_of(l, part):
        if part == "ffn":
            return [("ffn_in", l), ("ffn_out", l)]
        if l < n_a:
            return [("conv_in", l), ("conv_out", l)]
        return ([("kv", 0)] if l == n_a else []) + [("q", l - n_a), ("o", l - n_a)]

    halves = [(l, part) for l in range(depth) for part in ("mix", "ffn")]
    groups = {(l, part): [(*_cast_place(big_w[name], i, axis_of[name], BF16, f"place_{name}_{i}"), axis_of[name])
                          for name, i in matrices_of(l, part)] for l, part in halves}
    groups[halves[0]].append((*_cast_place(pack_small(norm_g, conv_w)[None], 0, 1, F32, "place_small"), 1))
    started = {halves[0]: _gather_start(groups[halves[0]], [], "gather_start_0_mix")[0]}

    def fetch(half, after):
        full = _gather_wait(groups[half], started[half], after, "gather_wait_%d_%s" % half)
        nxt = halves.index(half) + 1
        if nxt < len(halves):
            started[halves[nxt]], full = _gather_start(groups[halves[nxt]], full, "gather_start_%d_%s" % halves[nxt])
        return full

    target = _permute16(loss_target.reshape(x.shape[1:]), False, "permute_target")
    first = fetch(halves[0], target)
    small = first[-1]
    gains = [[small[4 * l + i][None] for i in range(4)] for l in range(depth)]
    conv_ws = [small[n_gain + 3 * l:n_gain + 3 * l + 3] for l in range(n_a)]
    kv_gain = kv_norm_g[None]

    def weights_of(l, part, after):
        full = first if (l, part) == halves[0] else fetch((l, part), after)
        return {name: full[t] for t, (name, _) in enumerate(matrices_of(l, part))}

    sent, paired = {}, {}
    LAG = 2

    def to_sibling(half, carry, after):
        axes = [axis_of[name] for name, _ in matrices_of(*half)]
        full, stacks = _scatter_wait(axes, sent[half], after, "scatter_wait_%d_%s" % half)
        paired[half], carry = _pair_start(list(zip(full, axes, stacks)), carry, "pair_start_%d_%s" % half)
        return carry

    def send_grads(l, part, grads, carry):
        sent[l, part], carry = _scatter_start(
            [(grads[name], axis_of[name]) for name, _ in matrices_of(l, part)], carry, f"scatter_start_{l}_{part}")
        older = halves.index((l, part)) + LAG
        if older < len(halves) and carry:
            carry = to_sibling(halves[older], carry, carry[0])
        return carry

    loss, dx, d_gains, d_conv, d_kv_gain = _local_step(
        x.reshape(x.shape[1:]), target, gains, conv_ws, kv_gain, weights_of, send_grads)
    loss = lax.psum(loss, ("x", "y", "c"))

    small_g = jnp.concatenate([dg for row in d_gains for dg in row] + list(d_conv) + [d_kv_gain]
                              + [jnp.zeros((pad_rows - 1, d), F32)])
    small_g = _allreduce_small(small_g, "allreduce_small")
    blk = 2 * lax.axis_index("x") + lax.axis_index("y")
    mine_small = lax.dynamic_slice_in_dim(small_g, blk * dq, dq, axis=1)
    kv_rows = d // dq

    def pack_opt(gains_like, taps_like, kv_like):
        rows = jnp.concatenate([gains_like.reshape(n_gain, dq), taps_like.reshape(n_tap, dq), kv_like.reshape(kv_rows, dq)])
        extra = -rows.shape[0] % 8
        return jnp.concatenate([rows, jnp.zeros((extra, dq), F32)]) if extra else rows

    sw = pack_opt(norm_g, conv_w, kv_norm_g)
    sm = pack_opt(m_norm_g, m_conv_w, m_kv_norm_g)
    sv = pack_opt(v_norm_g, v_conv_w, v_kv_norm_g)
    sg = pack_opt(mine_small[:n_gain], mine_small[n_gain:n_gain + n_tap], small_g[n_gain + n_tap])
    s_out = _adamw(sw, sm, sv, sg, "adamw_small")

    def unpack(a):
        return (a[:n_gain].reshape(depth, 4, dq), a[n_gain:n_gain + n_tap].reshape(n_a, -1, dq),
                a[n_gain + n_tap:n_gain + n_tap + kv_rows].reshape(d))

    small_out = [unpack(a) for a in s_out]

    landed, big_out = {}, {}

    def update(name):
        for half in halves:
            if matrices_of(*half)[0] not in landed and any(n == name for n, _ in matrices_of(*half)):
                axes = [axis_of[n] for n, _ in matrices_of(*half)]
                landed.update(zip(matrices_of(*half), zip(*_pair_wait(axes, paired[half], "pair_wait_%d_%s" % half))))
        shp, ax = big_w[name].shape, axis_of[name]
        rows, cols = shp[0] * shp[1], shp[2]
        flat = lambda a: a.reshape(rows, cols)
        full, stacks, sibling = zip(*[landed[name, i] for i in range(shp[0])])
        res = _adamw(flat(big_w[name]), flat(big_m[name]), flat(big_v[name]),
                     [(full[i], ax, stacks[i], sibling[i]) for i in range(shp[0])], f"adamw_{name}")
        big_out[name] = [a.reshape(shp[1:] if name == "kv" else shp) for a in res]

    pending = [half for half in reversed(halves) if half not in paired]
    for half in pending[:-1]:
        to_sibling(half, [], None)
    late = [name for name, _ in BIG if any(n == name for n, _ in matrices_of(*pending[-1]))]
    for name, _ in BIG:
        if name not in late:
            update(name)
    to_sibling(pending[-1], [], big_out["ffn_out"][0])
    for name in late:
        update(name)

    def leaves(i):
        ng, cw_, kg = small_out[i]
        return [ng, big_out["conv_in"][i], cw_, big_out["conv_out"][i], kg, big_out["kv"][i], big_out["q"][i],
                big_out["o"][i], big_out["ffn_in"][i], big_out["ffn_out"][i]]

    return (loss, dx.reshape(x.shape), *leaves(0), *leaves(1), *leaves(2), *leaves(3))
```

```python
import functools

import jax
import jax.numpy as jnp
import numpy as np
from jax import lax
from jax.experimental import pallas as pl
from jax.experimental.pallas import tpu as pltpu

F32 = jnp.float32
BF16 = jnp.bfloat16
HEAD_DIM = 64
DILATIONS = (1, 4, 16)
NORM_EPS = 1e-6
NEG_BIG = -1e30
VMEM_LIMIT = 48 * 1024 * 1024
ROW_TILE = 256
NORM_TILE = 512
LANE = 128
MESH = pl.DeviceIdType.MESH

ADAM_LR = 0.001
ADAM_B1 = 0.9
ADAM_B2 = 0.999
ADAM_EPS = 1e-08
ADAM_WD = 0.01
ADAM_STEP = 10

TILE_CANDIDATES = (1024, 1408, 768, 512, 384, 256, 128)


def _pick(dim, cands=TILE_CANDIDATES):
    for c in cands:
        if c <= dim and dim % c == 0:
            return c
    return dim


def _params(sem):
    return pltpu.CompilerParams(dimension_semantics=sem, vmem_limit_bytes=VMEM_LIMIT)


def _mm(a, b, mode, out_dtype, name, scale=None):
    a_planes = a.shape[0] if a.ndim == 3 else 1
    b_planes = b.shape[0] if b.ndim == 3 else 1
    if mode == "nn":
        m, k = a.shape[-2], a.shape[-1] * a_planes
        n = b.shape[1]
    elif mode == "nt":
        m, k = a.shape[-2], a.shape[-1] * a_planes
        n = b.shape[0]
    else:
        k, m = a.shape
        n = b.shape[-1] * b_planes
    tm, tn, tk = _pick(m), _pick(n // b_planes), _pick(k // a_planes, ((2048,) if mode == "tn" else ()) + TILE_CANDIDATES)
    nk = k // tk
    ka, nb = k // a_planes // tk, n // b_planes // tn
    if a_planes > 1:
        a_spec = pl.BlockSpec((None, tm, tk), lambda i, j, kk: (kk // ka, i, kk % ka))
    elif mode == "tn":
        a_spec = pl.BlockSpec((tk, tm), lambda i, j, kk: (kk, i))
    else:
        a_spec = pl.BlockSpec((tm, tk), lambda i, j, kk: (i, kk))
    if mode == "nn":
        b_spec = pl.BlockSpec((tk, tn), lambda i, j, kk: (kk, j))
        dims = (((1,), (0,)), ((), ()))
    elif mode == "nt":
        b_spec = pl.BlockSpec((tn, tk), lambda i, j, kk: (j, kk))
        dims = (((1,), (1,)), ((), ()))
    else:
        b_spec = (pl.BlockSpec((None, tk, tn), lambda i, j, kk: (j // nb, kk, j % nb)) if b_planes > 1
                  else pl.BlockSpec((tk, tn), lambda i, j, kk: (kk, j)))
        dims = (((0,), (0,)), ((), ()))

    def finish(acc):
        if scale is not None:
            acc = acc * scale
        return acc.astype(out_dtype)

    if nk == 1:
        def body(a_ref, b_ref, o_ref):
            o_ref[...] = finish(lax.dot_general(a_ref[...].astype(BF16), b_ref[...].astype(BF16), dims, preferred_element_type=F32))
        scratch = []
    else:
        def body(a_ref, b_ref, o_ref, acc_ref):
            kk = pl.program_id(2)

            @pl.when(kk == 0)
            def _():
                acc_ref[...] = jnp.zeros_like(acc_ref)

            acc_ref[...] += lax.dot_general(a_ref[...].astype(BF16), b_ref[...].astype(BF16), dims, preferred_element_type=F32)

            @pl.when(kk == nk - 1)
            def _():
                o_ref[...] = finish(acc_ref[...])
        scratch = [pltpu.VMEM((tm, tn), F32)]

    return pl.pallas_call(
        body, name=name,
        grid=(m // tm, n // tn, nk),
        in_specs=[a_spec, b_spec],
        out_specs=pl.BlockSpec((tm, tn), lambda i, j, kk: (i, j)),
        out_shape=jax.ShapeDtypeStruct((m, n), out_dtype),
        scratch_shapes=scratch,
        compiler_params=_params(("parallel", "parallel", "arbitrary")),
    )(a, b)


def _rstd(v):
    return lax.rsqrt(jnp.mean(v * v, axis=-1, keepdims=True) + NORM_EPS)


def _rms_bwd(dy, v, g, r):
    gy = dy * g
    dv = r * (gy - v * (r * r) * jnp.mean(gy * v, axis=-1, keepdims=True))
    return dv, dy * v * r


def _row_spec(t, width):
    return pl.BlockSpec((t, width), lambda i: (i, 0))


def _gain_spec(width):
    return pl.BlockSpec((1, width), lambda i: (0, 0))


def _norm_res_fwd(x, mix, g_post, pre_gains, name):
    s, d = x.shape
    t = _pick(s, (NORM_TILE, ROW_TILE))
    has_mix = mix is not None
    n_pre = len(pre_gains)

    def body(*refs):
        x_ref = refs[0]
        pos = 1
        x1 = x_ref[...]
        if has_mix:
            mv = refs[1][...].astype(F32)
            x1 = x1 + mv * _rstd(mv) * refs[2][...]
            pos = 3
        gains = refs[pos:pos + n_pre]
        outs = refs[pos + n_pre:]
        if has_mix:
            outs[0][...] = x1
            outs = outs[1:]
        r = _rstd(x1)
        for g_ref, o_ref in zip(gains, outs):
            o_ref[...] = (x1 * r * g_ref[...]).astype(BF16)

    ins = [x] + ([mix, g_post] if has_mix else []) + list(pre_gains)
    in_specs = [_row_spec(t, d)] + ([_row_spec(t, d), _gain_spec(d)] if has_mix else []) + [_gain_spec(d)] * n_pre
    out_shape = ([jax.ShapeDtypeStruct((s, d), F32)] if has_mix else []) + [jax.ShapeDtypeStruct((s, d), BF16)] * n_pre
    out_specs = [_row_spec(t, d)] * len(out_shape)
    res = pl.pallas_call(
        body, name=name, grid=(s // t,), in_specs=in_specs, out_specs=out_specs, out_shape=out_shape,
        compiler_params=_params(("parallel",)),
    )(*ins)
    if has_mix:
        return res[0], list(res[1:])
    return x, list(res)


def _mm_norm_res(a, b, x, g_post, pre_gain, name):
    m, k = a.shape
    d = b.shape[1]
    tm, tk = _pick(m, (512, 256)), _pick(k)
    nk = k // tk

    def body(a_ref, b_ref, x_ref, gp_ref, g_ref, mix_ref, x1_ref, xn_ref, acc_ref):
        kk = pl.program_id(1)

        @pl.when(kk == 0)
        def _():
            acc_ref[...] = jnp.zeros_like(acc_ref)

        acc_ref[...] += jnp.dot(a_ref[...].astype(BF16), b_ref[...], preferred_element_type=F32)

        @pl.when(kk == nk - 1)
        def _():
            mix = acc_ref[...].astype(BF16)
            mix_ref[...] = mix
            mv = mix.astype(F32)
            x1 = x_ref[...] + mv * _rstd(mv) * gp_ref[...]
            x1_ref[...] = x1
            xn_ref[...] = (x1 * _rstd(x1) * g_ref[...]).astype(BF16)

    rows = pl.BlockSpec((tm, d), lambda i, kk: (i, 0))
    gain = pl.BlockSpec((1, d), lambda i, kk: (0, 0))
    return pl.pallas_call(
        body, name=name, grid=(m // tm, nk),
        in_specs=[pl.BlockSpec((tm, tk), lambda i, kk: (i, kk)), pl.BlockSpec((tk, d), lambda i, kk: (kk, 0)),
                  rows, gain, gain],
        out_specs=[rows, rows, rows],
        out_shape=[jax.ShapeDtypeStruct((m, d), BF16), jax.ShapeDtypeStruct((m, d), F32),
                   jax.ShapeDtypeStruct((m, d), BF16)],
        scratch_shapes=[pltpu.VMEM((tm, d), F32)],
        compiler_params=_params(("parallel", "arbitrary")),
    )(a, b, x, g_post, pre_gain)


def _norm_res_loss(x, mix, g_post, target, name):
    s, d = x.shape
    t = _pick(s, (NORM_TILE, ROW_TILE))

    def body(x_ref, m_ref, g_ref, t_ref, dy_ref, loss_ref):
        mv = m_ref[...].astype(F32)
        y = x_ref[...] + mv * _rstd(mv) * g_ref[...]
        err = y - t_ref[...]
        dy_ref[...] = err * (1.0 / d)

        @pl.when(pl.program_id(0) == 0)
        def _():
            loss_ref[...] = jnp.zeros_like(loss_ref)

        loss_ref[...] += jnp.sum(err * err)

    dy, acc = pl.pallas_call(
        body, name=name, grid=(s // t,),
        in_specs=[_row_spec(t, d), _row_spec(t, d), _gain_spec(d), _row_spec(t, d)],
        out_specs=[_row_spec(t, d), pl.BlockSpec((8, LANE), lambda i: (0, 0))],
        out_shape=[jax.ShapeDtypeStruct((s, d), F32), jax.ShapeDtypeStruct((8, LANE), F32)],
        compiler_params=_params(("arbitrary",)),
    )(x, mix, g_post, target)
    return dy, acc[0, 0] * (0.5 / d)


def _norm_bwd(dx_out, branches, x_in, post, name):
    s, d = dx_out.shape
    t = _pick(s, (NORM_TILE, ROW_TILE))
    nb = len(branches)
    has_post = post is not None

    def body(*refs):
        dx_ref = refs[0]
        pos = 1
        dx = dx_ref[...]
        first = pl.program_id(0) == 0
        n_in = 1 + (1 + 2 * nb if nb else 0) + (2 if has_post else 0)
        outs = refs[n_in:]
        opos = 0
        if nb:
            xv = refs[pos][...]
            pos += 1
            r = _rstd(xv)
            dx_o = outs[0]
            opos = 1
            for _ in range(nb):
                dxn = refs[pos][...].astype(F32)
                g = refs[pos + 1][...]
                pos += 2
                dv, dg_rows = _rms_bwd(dxn, xv, g, r)
                dx = dx + dv
                dg_ref = outs[opos]
                opos += 1

                @pl.when(first)
                def _(dg_ref=dg_ref):
                    dg_ref[...] = jnp.zeros_like(dg_ref)

                dg_ref[...] += jnp.sum(dg_rows, axis=0, keepdims=True)
            dx_o[...] = dx
        if has_post:
            mv = refs[pos][...].astype(F32)
            g = refs[pos + 1][...]
            dm, dg_rows = _rms_bwd(dx, mv, g, _rstd(mv))
            outs[opos][...] = dm.astype(BF16)
            dg_ref = outs[opos + 1]

            @pl.when(first)
            def _():
                dg_ref[...] = jnp.zeros_like(dg_ref)

            dg_ref[...] += jnp.sum(dg_rows, axis=0, keepdims=True)

    ins, in_specs = [dx_out], [_row_spec(t, d)]
    out_shape, out_specs = [], []
    if nb:
        ins.append(x_in)
        in_specs.append(_row_spec(t, d))
        out_shape.append(jax.ShapeDtypeStruct((s, d), F32))
        out_specs.append(_row_spec(t, d))
        for dxn, g in branches:
            ins += [dxn, g]
            in_specs += [_row_spec(t, d), _gain_spec(d)]
            out_shape.append(jax.ShapeDtypeStruct((1, d), F32))
            out_specs.append(_gain_spec(d))
    if has_post:
        ins += [post[0], post[1]]
        in_specs += [_row_spec(t, d), _gain_spec(d)]
        out_shape += [jax.ShapeDtypeStruct((s, d), BF16), jax.ShapeDtypeStruct((1, d), F32)]
        out_specs += [_row_spec(t, d), _gain_spec(d)]
    res = pl.pallas_call(
        body, name=name, grid=(s // t,), in_specs=in_specs, out_specs=out_specs, out_shape=out_shape,
        compiler_params=_params(("arbitrary",)),
    )(*ins)
    res = list(res)
    dx_in = res.pop(0) if nb else dx_out
    dgs = [res.pop(0) for _ in range(nb)]
    dm, dg_post = (res[0], res[1]) if has_post else (None, None)
    return dx_in, dgs, dm, dg_post


HALO = 16


def _shift_down(u, prev, k):
    rows = lax.broadcasted_iota(jnp.int32, u.shape, 0)
    out = pltpu.roll(u, k, 0)
    for i in range(k):
        out = jnp.where(rows == i, prev[HALO - k + i:HALO - k + i + 1, :], out)
    return out


def _shift_up(u, nxt, k):
    n = u.shape[0]
    rows = lax.broadcasted_iota(jnp.int32, u.shape, 0)
    out = pltpu.roll(u, n - k, 0)
    for i in range(k):
        out = jnp.where(rows == n - k + i, nxt[i:i + 1, :], out)
    return out


def _conv_gate_fwd(p, cw, name):
    s, d3 = p.shape
    d = d3 // 3
    t = _pick(s, (ROW_TILE,))
    hb = t // HALO

    def body(p_ref, prev_ref, w_ref, z_ref):
        i = pl.program_id(0)
        pv = p_ref[...].astype(F32)
        b, u = pv[:, :d], pv[:, d:2 * d] * pv[:, 2 * d:]
        ph = prev_ref[...].astype(F32)
        up = jnp.where(i > 0, ph[:, d:2 * d] * ph[:, 2 * d:], 0.0)
        w = w_ref[...]
        y = w[0:1, :] * _shift_down(u, up, 2) + w[1:2, :] * _shift_down(u, up, 1) + w[2:3, :] * u
        z_ref[...] = (b * y).astype(BF16)

    return pl.pallas_call(
        body, name=name, grid=(s // t,),
        in_specs=[_row_spec(t, d3),
                  pl.BlockSpec((HALO, d3), lambda i: (jnp.maximum(i * hb - 1, 0), 0)),
                  pl.BlockSpec((3, d), lambda i: (0, 0))],
        out_specs=_row_spec(t, d),
        out_shape=jax.ShapeDtypeStruct((s, d), BF16),
        compiler_params=_params(("parallel",)),
    )(p, p, cw)


def _conv_gate_bwd(p, dz, cw, name):
    s, d3 = p.shape
    d = d3 // 3
    t = _pick(s, (ROW_TILE,))
    hb = t // HALO
    nt = s // t
    last_halo = s // HALO - 1

    def body(p_ref, prev_ref, next_ref, dz_ref, dznext_ref, w_ref, dp_ref, dw_ref):
        i = pl.program_id(0)
        pv = p_ref[...].astype(F32)
        b, c, h = pv[:, :d], pv[:, d:2 * d], pv[:, 2 * d:]
        u = c * h
        ph = prev_ref[...].astype(F32)
        up = jnp.where(i > 0, ph[:, d:2 * d] * ph[:, 2 * d:], 0.0)
        w = w_ref[...]
        u1, u2 = _shift_down(u, up, 1), _shift_down(u, up, 2)
        y = w[0:1, :] * u2 + w[1:2, :] * u1 + w[2:3, :] * u
        dz = dz_ref[...].astype(F32)
        dy = dz * b
        dyn = jnp.where(i < nt - 1, dznext_ref[...].astype(F32) * next_ref[...].astype(F32)[:, :d], 0.0)
        du = w[2:3, :] * dy + w[1:2, :] * _shift_up(dy, dyn, 1) + w[0:1, :] * _shift_up(dy, dyn, 2)
        dp_ref[:, :d] = (dz * y).astype(BF16)
        dp_ref[:, d:2 * d] = (du * h).astype(BF16)
        dp_ref[:, 2 * d:] = (du * c).astype(BF16)

        @pl.when(i == 0)
        def _():
            dw_ref[...] = jnp.zeros_like(dw_ref)

        dw_ref[0:1, :] += jnp.sum(dy * u2, axis=0, keepdims=True)
        dw_ref[1:2, :] += jnp.sum(dy * u1, axis=0, keepdims=True)
        dw_ref[2:3, :] += jnp.sum(dy * u, axis=0, keepdims=True)

    return pl.pallas_call(
        body, name=name, grid=(nt,),
        in_specs=[_row_spec(t, d3),
                  pl.BlockSpec((HALO, d3), lambda i: (jnp.maximum(i * hb - 1, 0), 0)),
                  pl.BlockSpec((HALO, d3), lambda i: (jnp.minimum((i + 1) * hb, last_halo), 0)),
                  _row_spec(t, d),
                  pl.BlockSpec((HALO, d), lambda i: (jnp.minimum((i + 1) * hb, last_halo), 0)),
                  pl.BlockSpec((3, d), lambda i: (0, 0))],
        out_specs=[_row_spec(t, d3), pl.BlockSpec((3, d), lambda i: (0, 0))],
        out_shape=[jax.ShapeDtypeStruct((s, d3), BF16), jax.ShapeDtypeStruct((3, d), F32)],
        compiler_params=_params(("arbitrary",)),
    )(p, p, p, dz, dz, cw)


FFN_ROWS, FFN_COLS = (512, 256), (1408, 768, 256, 128)


def _row_chunks(tm, rows=256):
    return [slice(r, r + min(rows, tm)) for r in range(0, tm, min(rows, tm))]


def _ffn_in_swiglu(xn, w_in, name):
    s, k = xn.shape
    ff = w_in.shape[1] // 2
    tm, tn = _pick(s, FFN_ROWS), _pick(ff, FFN_COLS)
    nj = ff // tn

    def body(x_ref, wg_ref, wu_ref, f_ref, a_ref):
        for rows in _row_chunks(tm):
            xv = x_ref[rows, :]
            gate = jnp.dot(xv, wg_ref[...], preferred_element_type=F32)
            up = jnp.dot(xv, wu_ref[...], preferred_element_type=F32)
            f_ref[0, rows, :] = gate.astype(BF16)
            f_ref[1, rows, :] = up.astype(BF16)
            a_ref[rows, :] = (gate * jax.nn.sigmoid(gate) * up).astype(BF16)

    return pl.pallas_call(
        body, name=name, grid=(nj, s // tm),
        in_specs=[pl.BlockSpec((tm, k), lambda j, i: (i, 0)),
                  pl.BlockSpec((k, tn), lambda j, i: (0, j)),
                  pl.BlockSpec((k, tn), lambda j, i: (0, nj + j))],
        out_specs=[pl.BlockSpec((2, tm, tn), lambda j, i: (0, i, j)), pl.BlockSpec((tm, tn), lambda j, i: (i, j))],
        out_shape=[jax.ShapeDtypeStruct((2, s, ff), BF16), jax.ShapeDtypeStruct((s, ff), BF16)],
        compiler_params=_params(("parallel", "parallel")),
    )(xn, w_in, w_in)


def _ffn_out_dx_swiglu(dff, w_out, f, name):
    s, d = dff.shape
    ff = w_out.shape[0]
    tm, tn = _pick(s, FFN_ROWS), _pick(ff, FFN_COLS)

    def body(d_ref, w_ref, f_ref, df_ref):
        for rows in _row_chunks(tm):
            da = lax.dot_general(d_ref[rows, :], w_ref[...], (((1,), (1,)), ((), ())), preferred_element_type=F32)
            gate = f_ref[0, rows, :].astype(F32)
            up = f_ref[1, rows, :].astype(F32)
            sg = jax.nn.sigmoid(gate)
            silu = gate * sg
            df_ref[0, rows, :] = (da * up * (sg + silu * (1.0 - sg))).astype(BF16)
            df_ref[1, rows, :] = (da * silu).astype(BF16)

    planes = pl.BlockSpec((2, tm, tn), lambda j, i: (0, i, j))
    return pl.pallas_call(
        body, name=name, grid=(ff // tn, s // tm),
        in_specs=[pl.BlockSpec((tm, d), lambda j, i: (i, 0)), pl.BlockSpec((tn, d), lambda j, i: (j, 0)), planes],
        out_specs=planes, out_shape=jax.ShapeDtypeStruct((2, s, ff), BF16),
        compiler_params=_params(("parallel", "parallel")),
    )(dff, w_out, f)


SUPER = 2048
RES = 16
PAIR = 128
L = 128
FWD_TOGETHER = 8
BWD_TOGETHER = 4


def _alibi_slopes(n_heads):
    h = np.arange(n_heads, dtype=np.float32) + 1.0
    return np.power(2.0, -8.0 * h / n_heads).astype(np.float32)


def _permute16(x, inverse, name):
    s, d = x.shape
    cw = LANE

    def body(x_ref, o_ref):
        if inverse:
            for m in range(L):
                o_ref[RES * m:RES * (m + 1), :] = x_ref[pl.ds(m, RES, stride=L), :]
        else:
            for r in range(RES):
                o_ref[L * r:L * (r + 1), :] = x_ref[pl.ds(r, L, stride=RES), :]

    spec = pl.BlockSpec((SUPER, cw), lambda i, j: (i, j))
    return pl.pallas_call(
        body, name=name, grid=(s // SUPER, d // cw), in_specs=[spec], out_specs=spec,
        out_shape=jax.ShapeDtypeStruct((s, d), x.dtype),
        compiler_params=_params(("parallel", "parallel")),
    )(x)


def _slope_table(d):
    nh = d // HEAD_DIM
    sl = _alibi_slopes(nh)
    tab = np.repeat(sl, HEAD_DIM).reshape(d // PAIR, 1, PAIR)
    return jnp.asarray(np.broadcast_to(tab, (d // PAIR, 8, PAIR)).copy())


def _geometry(dil):
    nch = RES // dil
    return nch, L // nch


def _band(dil):
    nch, w = _geometry(dil)
    sh = w.bit_length() - 1
    i = lax.broadcasted_iota(jnp.int32, (L, 2 * L), 0)
    j = lax.broadcasted_iota(jnp.int32, (L, 2 * L), 1)

    def pos(t):
        return jnp.bitwise_and(t, w - 1) * nch + jnp.right_shift(t, sh)

    delta = pos(i) + L - (pos(jnp.bitwise_and(j, L - 1)) + jnp.bitwise_and(j, L))
    return (delta * dil).astype(F32), (delta >= 0) & (delta <= L), j < L


def _fill_bias(bias_s, sl_ref):
    for b, dil in enumerate(DILATIONS):
        base, band, prev_half = _band(dil)
        for first in range(2):
            valid = band & jnp.logical_not(prev_half) if first else band
            for h in range(2):
                slope = sl_ref[0:1, HEAD_DIM * h:HEAD_DIM * h + 1]
                bias_s[(2 * b + first) * 2 + h] = jnp.where(valid, -slope * base, NEG_BIG)


def _bias_index(b, sb, n):
    first = jnp.logical_and(sb == 0, n == 0).astype(jnp.int32)
    return (2 * b + first) * 2


def _offsets(dil, res, n):
    nch, w = _geometry(dil)

    def al(v):
        return v if isinstance(v, int) else pl.multiple_of(v, w)

    q_off = [al((a * dil + res) * L + n * w) for a in range(nch)]
    k_off = [al((a * dil + res) * 2 * L + L + n * w) for a in range(nch)]
    kp_off = [al((a * dil + res) * 2 * L + L + n * w - w) for a in range(nch)]
    return q_off, k_off, kp_off, w


def _gather(ref, offs, w):
    parts = [ref[pl.ds(o, w), :] for o in offs]
    return parts[0] if len(parts) == 1 else jnp.concatenate(parts, axis=0)


def _scatter(ref, offs, w, val, add=False):
    for a, o in enumerate(offs):
        piece = val[a * w:(a + 1) * w, :]
        if add:
            ref[pl.ds(o, w), :] += piece
        else:
            ref[pl.ds(o, w), :] = piece


def _fill_key_buffer(buf, prev_ref, cur_ref):
    for r in range(RES):
        buf[2 * L * r:2 * L * r + L, :] = prev_ref[L * r:L * (r + 1), :]
        buf[2 * L * r + L:2 * L * (r + 1), :] = cur_ref[L * r:L * (r + 1), :]


def _two_heads(x, low):
    zero = jnp.zeros_like(x)
    return jnp.concatenate([jnp.where(low, x, zero), jnp.where(low, zero, x)], axis=0)


def _loop_blocks(dil, stages, together):
    together = max(together, dil) if dil < RES else together

    def it(i, c):
        if dil == RES:
            blocks = [(i * together + k, 0) for k in range(together)]
        else:
            blocks = [(res, i * (together // dil) + k) for k in range(together // dil) for res in range(dil)]
        state = [stages[0](res, n) for res, n in blocks]
        for stage in stages[1:]:
            state = [stage(res, n, prev) for (res, n), prev in zip(blocks, state)]
        for writes in state:
            for args in writes:
                _scatter(*args)
        return c

    lax.fori_loop(0, RES // together, it, 0)


NT = (((1,), (1,)), ((), ()))
TN = (((0,), (0,)), ((), ()))


def _attention_fwd(q, kv, name):
    s, d = q.shape
    g_n, ns = d // PAIR, s // SUPER

    def body(sl_ref, q_ref, kc_ref, kp_ref, vc_ref, vp_ref, o_ref, lse_ref, kbuf, vbuf, m_s, l_s, acc_s, bias_s):
        sb = pl.program_id(1)
        _fill_key_buffer(kbuf, kp_ref, kc_ref)
        _fill_key_buffer(vbuf, vp_ref, vc_ref)
        _fill_bias(bias_s, sl_ref)
        low = lax.broadcasted_iota(jnp.int32, (L, PAIR), 1) < HEAD_DIM
        low_k = lax.broadcasted_iota(jnp.int32, (2 * L, PAIR), 1) < HEAD_DIM
        ones_bd = _two_heads(jnp.ones((2 * L, PAIR), BF16), low_k)
        for bi, dil in enumerate(DILATIONS):
            first_branch, last_branch = bi == 0, bi == len(DILATIONS) - 1

            def scores(res, n, dil=dil):
                q_off, k_off, kp_off, w = _offsets(dil, res, n)
                qf = _gather(q_ref, q_off, w).astype(BF16)
                kcat = jnp.concatenate([_gather(kbuf, kp_off, w), _gather(kbuf, k_off, w)], axis=0).astype(BF16)
                return lax.dot_general(qf, _two_heads(kcat, low_k), NT, preferred_element_type=F32)

            def update(res, n, sc, bi=bi, dil=dil, first_branch=first_branch, last_branch=last_branch):
                q_off, k_off, kp_off, w = _offsets(dil, res, n)
                vcat = jnp.concatenate([_gather(vbuf, kp_off, w), _gather(vbuf, k_off, w)], axis=0).astype(BF16)
                v_ones = jnp.concatenate([_two_heads(vcat, low_k), ones_bd], axis=1)
                bias_at = _bias_index(bi, sb, n)
                if not first_branch:
                    m_prev = _gather(m_s, q_off, w)
                ps, m_new = [], []
                for h in range(2):
                    s_h = sc[:, 2 * L * h:2 * L * (h + 1)] + bias_s[bias_at + h]
                    mh = jnp.max(s_h, axis=1, keepdims=True)
                    if not first_branch:
                        mh = jnp.maximum(mh, m_prev[:, HEAD_DIM * h:HEAD_DIM * h + 1])
                    ps.append(jnp.exp(s_h - mh).astype(BF16))
                    m_new.append(mh)
                m_full = jnp.where(low, m_new[0], m_new[1])
                both = jnp.dot(jnp.concatenate(ps, axis=1), v_ones, preferred_element_type=F32)
                acc, l_full = both[:, :PAIR], both[:, PAIR:]
                if not first_branch:
                    alpha = jnp.exp(m_prev - m_full)
                    l_full = _gather(l_s, q_off, w) * alpha + l_full
                    acc = _gather(acc_s, q_off, w) * alpha + acc
                if last_branch:
                    return [(o_ref, q_off, w, acc / l_full, False), (lse_ref, q_off, w, m_full + jnp.log(l_full), False)]
                return [(m_s, q_off, w, m_full, False), (l_s, q_off, w, l_full, False), (acc_s, q_off, w, acc, False)]

            _loop_blocks(dil, [scores, update], FWD_TOGETHER)

    prev = lambda i: jnp.maximum(i - 1, 0)
    blk = pl.BlockSpec((SUPER, PAIR), lambda g, i: (i, g))
    in_specs = [pl.BlockSpec((None, 8, PAIR), lambda g, i: (g, 0, 0)), blk,
                pl.BlockSpec((SUPER, PAIR), lambda g, i: (i, g)),
                pl.BlockSpec((SUPER, PAIR), lambda g, i: (prev(i), g)),
                pl.BlockSpec((SUPER, PAIR), lambda g, i: (i, g_n + g)),
                pl.BlockSpec((SUPER, PAIR), lambda g, i: (prev(i), g_n + g))]
    return pl.pallas_call(
        body, name=name, grid=(g_n, ns), in_specs=in_specs, out_specs=[blk, blk],
        out_shape=[jax.ShapeDtypeStruct((s, d), F32)] * 2,
        scratch_shapes=([pltpu.VMEM((2 * SUPER, PAIR), F32)] * 2 + [pltpu.VMEM((SUPER, PAIR), F32)] * 3
                        + [pltpu.VMEM((4 * len(DILATIONS), L, 2 * L), F32)]),
        compiler_params=_params(("parallel", "parallel")),
    )(_slope_table(d), q, kv, kv, kv, kv)


def _attention_bwd(q, kv, o, do, lse, dkv_in, name):
    s, d = q.shape
    g_n, ns = d // PAIR, s // SUPER
    has_in = dkv_in is not None

    def body(*refs):
        sl_ref, q_ref, do_ref, o_ref, lse_ref, kc_ref, kp_ref, vc_ref, vp_ref = refs[:9]
        pos = 9
        if has_in:
            dkv_in_ref = refs[9]
            pos = 10
        dq_ref, dkv_ref, kbuf, vbuf, dkbuf, dvbuf, dq_s, bias_s = refs[pos:]
        step = pl.program_id(1)
        sb = ns - 1 - step
        _fill_key_buffer(kbuf, kp_ref, kc_ref)
        _fill_key_buffer(vbuf, vp_ref, vc_ref)
        _fill_bias(bias_s, sl_ref)

        @pl.when(step == 0)
        def _():
            dkbuf[...] = jnp.zeros_like(dkbuf)
            dvbuf[...] = jnp.zeros_like(dvbuf)

        @pl.when(step > 0)
        def _():
            for buf in (dkbuf, dvbuf):
                for r in range(RES):
                    buf[2 * L * r + L:2 * L * (r + 1), :] = buf[2 * L * r:2 * L * r + L, :]
                    buf[2 * L * r:2 * L * r + L, :] = jnp.zeros((L, PAIR), F32)

        low = lax.broadcasted_iota(jnp.int32, (L, PAIR), 1) < HEAD_DIM
        low_k = lax.broadcasted_iota(jnp.int32, (2 * L, PAIR), 1) < HEAD_DIM
        low_t = lax.broadcasted_iota(jnp.int32, (PAIR, 2 * L), 0) < HEAD_DIM
        for bi, dil in enumerate(DILATIONS):
            first_branch = bi == 0

            def scores(res, n, dil=dil):
                q_off, k_off, kp_off, w = _offsets(dil, res, n)
                qb = _gather(q_ref, q_off, w).astype(BF16)
                dof = _gather(do_ref, q_off, w)
                prod = dof * _gather(o_ref, q_off, w)
                dob = dof.astype(BF16)
                lse_f = _gather(lse_ref, q_off, w)
                zero = jnp.zeros_like(prod)
                dsum = (jnp.sum(jnp.where(low, prod, zero), axis=1, keepdims=True),
                        jnp.sum(jnp.where(low, zero, prod), axis=1, keepdims=True))
                kcat = jnp.concatenate([_gather(kbuf, kp_off, w), _gather(kbuf, k_off, w)], axis=0).astype(BF16)
                vcat = jnp.concatenate([_gather(vbuf, kp_off, w), _gather(vbuf, k_off, w)], axis=0).astype(BF16)
                k_bd, v_bd = _two_heads(kcat, low_k), _two_heads(vcat, low_k)
                sc = lax.dot_general(qb, k_bd, NT, preferred_element_type=F32)
                dp = lax.dot_general(dob, v_bd, NT, preferred_element_type=F32)
                return qb, dob, lse_f, dsum, k_bd, sc, dp

            def gradients(res, n, given, bi=bi, dil=dil, first_branch=first_branch):
                qb, dob, lse_f, dsum, k_bd, sc, dp = given
                q_off, k_off, kp_off, w = _offsets(dil, res, n)
                bias_at = _bias_index(bi, sb, n)
                ps, dss = [], []
                for h in range(2):
                    cols = slice(2 * L * h, 2 * L * (h + 1))
                    lse_h = lse_f[:, HEAD_DIM * h:HEAD_DIM * h + 1]
                    p_h = jnp.exp(sc[:, cols] + bias_s[bias_at + h] - lse_h)
                    dss.append((p_h * (dp[:, cols] - dsum[h])).astype(BF16))
                    ps.append(p_h.astype(BF16))
                ds_cat, p_cat = jnp.concatenate(dss, axis=1), jnp.concatenate(ps, axis=1)
                dq = jnp.dot(ds_cat, k_bd, preferred_element_type=F32)
                dk_t = lax.dot_general(qb, ds_cat, TN, preferred_element_type=F32)
                dv_t = lax.dot_general(dob, p_cat, TN, preferred_element_type=F32)
                dk = jnp.where(low_t, dk_t[:, :2 * L], dk_t[:, 2 * L:]).T
                dv = jnp.where(low_t, dv_t[:, :2 * L], dv_t[:, 2 * L:]).T
                return [(dq_s, q_off, w, dq, not first_branch),
                        (dkbuf, kp_off, w, dk[:L], True), (dkbuf, k_off, w, dk[L:], True),
                        (dvbuf, kp_off, w, dv[:L], True), (dvbuf, k_off, w, dv[L:], True)]

            _loop_blocks(dil, [scores, gradients], BWD_TOGETHER)

        dq_ref[...] = dq_s[...].astype(BF16)
        for r in range(RES):
            rows, cur = slice(L * r, L * (r + 1)), slice(2 * L * r + L, 2 * L * (r + 1))
            for plane, buf in enumerate((dkbuf, dvbuf)):
                if has_in:
                    dkv_ref[plane, rows, :] = buf[cur, :] + dkv_in_ref[plane, rows, :]
                else:
                    dkv_ref[plane, rows, :] = buf[cur, :]

    rev = lambda i: ns - 1 - i
    prev = lambda i: jnp.maximum(ns - 2 - i, 0)
    blk = pl.BlockSpec((SUPER, PAIR), lambda g, i: (rev(i), g))
    in_specs = [pl.BlockSpec((None, 8, PAIR), lambda g, i: (g, 0, 0)), blk, blk, blk, blk,
                pl.BlockSpec((SUPER, PAIR), lambda g, i: (rev(i), g)),
                pl.BlockSpec((SUPER, PAIR), lambda g, i: (prev(i), g)),
                pl.BlockSpec((SUPER, PAIR), lambda g, i: (rev(i), g_n + g)),
                pl.BlockSpec((SUPER, PAIR), lambda g, i: (prev(i), g_n + g))]
    ins = [_slope_table(d), q, do, o, lse, kv, kv, kv, kv]
    planes = pl.BlockSpec((2, SUPER, PAIR), lambda g, i: (0, rev(i), g))
    if has_in:
        in_specs.append(planes)
        ins.append(dkv_in)
    res = pl.pallas_call(
        body, name=name, grid=(g_n, ns), in_specs=in_specs, out_specs=[blk, planes],
        out_shape=[jax.ShapeDtypeStruct((s, d), BF16), jax.ShapeDtypeStruct((2, s, d), F32)],
        scratch_shapes=([pltpu.VMEM((2 * SUPER, PAIR), F32)] * 4 + [pltpu.VMEM((SUPER, PAIR), F32)]
                        + [pltpu.VMEM((4 * len(DILATIONS), L, 2 * L), F32)]),
        compiler_params=_params(("parallel", "arbitrary")),
    )(*ins)
    return res[0], res[1]


def _coords():
    return lax.axis_index("x"), lax.axis_index("y"), lax.axis_index("c")


def _chip_peers(x, y):
    return [(1 - x, y), (x, 1 - y), (1 - x, 1 - y)]


def _block_of(ref, axis, blk, size):
    start = pl.multiple_of(blk * size, size)
    if axis == 1:
        return ref.at[:, pl.ds(start, size)]
    return ref.at[pl.ds(start, size), :]


ANY = pl.BlockSpec(memory_space=pl.ANY)


HBM = pl.BlockSpec(memory_space=pltpu.HBM)
SEM = pl.BlockSpec(memory_space=pltpu.SEMAPHORE)
SPLIT = pltpu.CompilerParams(has_side_effects=pltpu.SideEffectType.DATAFLOW_SIDE_EFFECTING)


def _in_hbm(a):
    return pltpu.with_memory_space_constraint(a, pltpu.HBM)


def _thru(arrays):
    return [pltpu.HBM(a.shape, a.dtype) for a in arrays]


def _cast_place(w, layer, ax, dtype, name):
    _, k, n = w.shape
    t = _pick(k, (512, 256, 128))
    nb = k // t

    def body(blk_ref, w_ref, b_ref, f_ref):
        v = w_ref[...].astype(dtype)
        b_ref[...] = v
        f_ref[...] = v

    full_shape = (k, 4 * n) if ax == 1 else (4 * k, n)
    place = (lambda i, blk: (i, blk[0])) if ax == 1 else (lambda i, blk: (blk[0] * nb + i, 0))
    return pl.pallas_call(
        body, name=name,
        grid_spec=pltpu.PrefetchScalarGridSpec(
            num_scalar_prefetch=1, grid=(nb,),
            in_specs=[pl.BlockSpec((None, t, n), lambda i, blk: (layer, i, 0))],
            out_specs=[pl.BlockSpec((t, n), lambda i, blk: (i, 0)), pl.BlockSpec((t, n), place)]),
        out_shape=[jax.ShapeDtypeStruct((k, n), dtype), jax.ShapeDtypeStruct(full_shape, dtype)],
        compiler_params=_params(("parallel",)),
    )(_my_block()[None], w)


def _my_block():
    return (2 * lax.axis_index("x") + lax.axis_index("y")).astype(jnp.int32)


def _gather_start(group, carry, name):
    n, nc = len(group), len(carry)

    def body(*refs):
        blocks, fulls, send_sem, recv_sem = refs[:n], refs[n:2 * n], refs[2 * n + nc], refs[2 * n + nc + 1]
        x, y, c = _coords()
        for t, (b, _, ax) in enumerate(group):
            mine = _block_of(fulls[t], ax, 2 * x + y, b.shape[ax])
            for j, (px, py) in enumerate(_chip_peers(x, y)):
                pltpu.make_async_remote_copy(
                    src_ref=blocks[t], dst_ref=mine, send_sem=send_sem.at[3 * t + j], recv_sem=recv_sem.at[3 * t + j],
                    device_id=(px, py, c), device_id_type=MESH).start()

    arrays = [b for b, _, _ in group] + [f for _, f, _ in group] + list(carry)
    sems = [pltpu.SemaphoreType.DMA((3 * n,))] * 2
    res = pl.pallas_call(
        body, name=name, in_specs=[HBM] * len(arrays), out_specs=[SEM, SEM] + [HBM] * len(arrays),
        out_shape=sems + _thru(arrays), input_output_aliases={i: 2 + i for i in range(len(arrays))},
        compiler_params=SPLIT,
    )(*[_in_hbm(a) for a in arrays])
    return (res[0], res[1], list(res[2:2 + n]), list(res[2 + n:2 + 2 * n])), list(res[2 + 2 * n:])


def _gather_wait(group, started, after, name):
    sends, recvs, blocks, fulls = started
    m = len(group)

    def body(*refs):
        blk_refs, full_refs, send_sem, recv_sem = refs[:m], refs[m:2 * m], refs[2 * m], refs[2 * m + 1]
        x, y, c = _coords()
        for t, (b, _, ax) in enumerate(group):
            for j, (px, py) in enumerate(_chip_peers(x, y)):
                cp = pltpu.make_async_remote_copy(
                    src_ref=blk_refs[t], dst_ref=_block_of(full_refs[t], ax, 2 * px + py, b.shape[ax]),
                    send_sem=send_sem.at[3 * t + j], recv_sem=recv_sem.at[3 * t + j],
                    device_id=(px, py, c), device_id_type=MESH)
                cp.wait_send()
                cp.wait_recv()

    extra = [] if after is None else [after]
    res = pl.pallas_call(
        body, name=name, in_specs=[HBM] * (2 * m) + [SEM, SEM] + [ANY] * len(extra), out_specs=[HBM] * (2 * m),
        out_shape=_thru(blocks) + _thru(fulls), input_output_aliases={i: i for i in range(2 * m)},
        compiler_params=SPLIT,
    )(*blocks, *fulls, sends, recvs, *extra)
    return list(res[m:])


def _scatter_start(grads, carry, name):
    n = len(grads)
    n_in = 2 * n + len(carry)

    def body(*refs):
        g_refs, st_refs, send_sem, recv_sem = refs[:n], refs[n:2 * n], refs[n_in], refs[n_in + 1]
        x, y, c = _coords()
        for t, (g, ax) in enumerate(grads):
            for j, (px, py) in enumerate(_chip_peers(x, y)):
                pltpu.make_async_remote_copy(
                    src_ref=_block_of(g_refs[t], ax, 2 * px + py, g.shape[ax] // 4), dst_ref=st_refs[t].at[j],
                    send_sem=send_sem.at[3 * t + j], recv_sem=recv_sem.at[3 * t + j],
                    device_id=(px, py, c), device_id_type=MESH).start()

    arrays = [g for g, _ in grads]
    for g, ax in grads:
        shape = list(g.shape)
        shape[ax] //= 4
        arrays.append(lax.empty((3, *shape), g.dtype))
    arrays += list(carry)
    sems = [pltpu.SemaphoreType.DMA((3 * n,))] * 2
    res = pl.pallas_call(
        body, name=name, in_specs=[HBM] * n_in, out_specs=[SEM, SEM] + [HBM] * n_in,
        out_shape=sems + _thru(arrays), input_output_aliases={i: 2 + i for i in range(n_in)},
        compiler_params=SPLIT,
    )(*[_in_hbm(a) for a in arrays])
    return (res[0], res[1], list(res[2:2 + n]), list(res[2 + n:2 + 2 * n])), list(res[2 + 2 * n:])


def _scatter_wait(axes, started, after, name):
    sends, recvs, full, stacks = started
    n = len(full)
    extra = [] if after is None else [after]

    def body(*refs):
        g_refs, st_refs, send_sem, recv_sem = refs[:n], refs[n:2 * n], refs[2 * n], refs[2 * n + 1]
        x, y, c = _coords()
        for t, ax in enumerate(axes):
            size = full[t].shape[ax] // 4
            for j, (px, py) in enumerate(_chip_peers(x, y)):
                cp = pltpu.make_async_remote_copy(
                    src_ref=_block_of(g_refs[t], ax, 2 * px + py, size), dst_ref=st_refs[t].at[j],
                    send_sem=send_sem.at[3 * t + j], recv_sem=recv_sem.at[3 * t + j],
                    device_id=(px, py, c), device_id_type=MESH)
                cp.wait_send()
                cp.wait_recv()

    res = pl.pallas_call(
        body, name=name, in_specs=[HBM] * (2 * n) + [SEM, SEM] + [ANY] * len(extra), out_specs=[HBM] * (2 * n),
        out_shape=_thru(full) + _thru(stacks), input_output_aliases={i: i for i in range(2 * n)},
        compiler_params=SPLIT,
    )(*full, *stacks, sends, recvs, *extra)
    return list(res[:n]), list(res[n:])


def _pair_copies(g_refs, st_refs, out_refs, items, send_sem, recv_sem):
    x, y, c = _coords()
    copies = []
    for u, (g, ax, _) in enumerate(items):
        own = _block_of(g_refs[u], ax, 2 * x + y, g.shape[ax] // 4)
        for k, (src, dst) in enumerate([(own, out_refs[u].at[0]), (st_refs[u], out_refs[u].at[pl.ds(1, 3)])]):
            copies.append(pltpu.make_async_remote_copy(
                src_ref=src, dst_ref=dst, send_sem=send_sem.at[2 * u + k], recv_sem=recv_sem.at[2 * u + k],
                device_id=(x, y, 1 - c), device_id_type=MESH))
    return copies


def _pair_start(items, carry, name):
    n = len(items)
    n_in = 3 * n + len(carry)

    def body(*refs):
        for cp in _pair_copies(refs[:n], refs[n:2 * n], refs[2 * n:3 * n], items, refs[n_in], refs[n_in + 1]):
            cp.start()

    arrays = ([g for g, _, _ in items] + [st for _, _, st in items]
              + [lax.empty((4, *st.shape[1:]), st.dtype) for _, _, st in items] + list(carry))
    sems = [pltpu.SemaphoreType.DMA((2 * n,))] * 2
    res = pl.pallas_call(
        body, name=name, in_specs=[HBM] * n_in, out_specs=[SEM, SEM] + [HBM] * n_in,
        out_shape=sems + _thru(arrays), input_output_aliases={i: 2 + i for i in range(n_in)},
        compiler_params=SPLIT,
    )(*[_in_hbm(a) for a in arrays])
    thru = res[2:]
    return (res[0], res[1], *(list(thru[k * n:(k + 1) * n]) for k in range(3))), list(thru[3 * n:])


def _pair_wait(axes, started, name):
    send, recv, full, stacks, landing = started
    n = len(full)
    items = [(full[u], axes[u], stacks[u]) for u in range(n)]

    def body(*refs):
        for cp in _pair_copies(refs[:n], refs[n:2 * n], refs[2 * n:3 * n], items, refs[3 * n], refs[3 * n + 1]):
            cp.wait_send()
            cp.wait_recv()

    res = pl.pallas_call(
        body, name=name, in_specs=[HBM] * (3 * n) + [SEM, SEM], out_specs=[HBM] * (3 * n),
        out_shape=_thru(full + stacks + landing), input_output_aliases={i: i for i in range(3 * n)},
        compiler_params=SPLIT,
    )(*full, *stacks, *landing, send, recv)
    return list(res[:n]), list(res[n:2 * n]), list(res[2 * n:])


def _allreduce_small(v, name):
    r, cdim = v.shape

    def body(v_ref, out_ref, buf, send_sems, recv_sems):
        x, y, c = _coords()
        me = 4 * x + 2 * y + c
        buf[0] = v_ref[...]
        sends = []
        for k in range(1, 8):
            peer = (x if not (k & 4) else 1 - x, y if not (k & 2) else 1 - y, c if not (k & 1) else 1 - c)
            cp = pltpu.make_async_remote_copy(
                src_ref=v_ref, dst_ref=buf.at[k], send_sem=send_sems.at[k - 1], recv_sem=recv_sems.at[k - 1],
                device_id=peer, device_id_type=MESH)
            cp.start()
            sends.append(cp)
        for cp in sends:
            cp.wait_recv()
        total = buf[me]
        for src in range(1, 8):
            total = total + buf[jnp.bitwise_xor(me, src)]
        out_ref[...] = total
        for cp in sends:
            cp.wait_send()

    return pl.pallas_call(
        body, name=name,
        in_specs=[pl.BlockSpec(memory_space=pltpu.VMEM)], out_specs=pl.BlockSpec(memory_space=pltpu.VMEM),
        out_shape=jax.ShapeDtypeStruct((r, cdim), F32),
        scratch_shapes=[pltpu.VMEM((8, r, cdim), F32), pltpu.SemaphoreType.DMA((7,)), pltpu.SemaphoreType.DMA((7,))],
        compiler_params=pltpu.CompilerParams(has_side_effects=True),
    )(v)


def _adamw_math(w, g, m, v):
    m = ADAM_B1 * m + (1.0 - ADAM_B1) * g
    v = ADAM_B2 * v + (1.0 - ADAM_B2) * jnp.square(g)
    m_hat = m / (1.0 - ADAM_B1 ** ADAM_STEP)
    v_hat = v / (1.0 - ADAM_B2 ** ADAM_STEP)
    delta = -ADAM_LR * (m_hat / (jnp.sqrt(v_hat) + ADAM_EPS) + ADAM_WD * w)
    return delta, m, v


def _adamw(w, m, v, grads, name):
    r, cdim = w.shape
    paired = isinstance(grads, list)
    layers = len(grads) if paired else 1
    t = _pick(r // layers, (128, 64, 32, 16, 8))
    per_layer = r // layers // t
    n_grad = 3 * layers if paired else 1

    def body(*refs):
        refs = refs[1:] if paired else refs
        w_ref, m_ref, v_ref = refs[:3]
        outs = refs[3 + n_grad:]

        def update(g):
            delta, m_new, v_new = _adamw_math(w_ref[...], g, m_ref[...], v_ref[...])
            outs[0][...] = g
            outs[1][...] = delta
            outs[2][...] = m_new
            outs[3][...] = v_new

        if not paired:
            update(refs[3][...])
            return
        layer = pl.program_id(0) // per_layer
        for l in range(layers):
            @pl.when(layer == l)
            def _(own_ref=refs[3 + 3 * l], st_ref=refs[4 + 3 * l], sib_ref=refs[5 + 3 * l]):
                sa = own_ref[...].astype(F32)
                sb = sib_ref[0].astype(F32)
                for k in range(3):
                    sa = sa + st_ref[k].astype(F32)
                    sb = sb + sib_ref[k + 1].astype(F32)
                update(sa + sb)

    out_shape = [jax.ShapeDtypeStruct((r, cdim), F32)] * 4
    if not paired:
        spec = pl.BlockSpec((t, cdim), lambda i: (i, 0))
        return pl.pallas_call(
            body, name=name, grid=(r // t,), in_specs=[spec] * 4, out_specs=[spec] * 4, out_shape=out_shape,
            compiler_params=_params(("parallel",)),
        )(w, m, v, grads)

    spec = pl.BlockSpec((t, cdim), lambda i, blk: (i, 0))
    ins, in_specs = [w, m, v], [spec] * 3
    for l, (g, ax, stack, sib) in enumerate(grads):
        row = lambda i, l=l: jnp.clip(i - l * per_layer, 0, per_layer - 1)
        own = ((lambda i, blk, row=row: (row(i), blk[0])) if ax == 1
               else (lambda i, blk, row=row: (blk[0] * per_layer + row(i), 0)))
        ins += [g, stack, sib]
        in_specs += [pl.BlockSpec((t, cdim), own),
                     pl.BlockSpec((3, t, cdim), lambda i, blk, row=row: (0, row(i), 0)),
                     pl.BlockSpec((4, t, cdim), lambda i, blk, row=row: (0, row(i), 0))]
    return pl.pallas_call(
        body, name=name,
        grid_spec=pltpu.PrefetchScalarGridSpec(
            num_scalar_prefetch=1, grid=(r // t,), in_specs=in_specs, out_specs=[spec] * 4),
        out_shape=out_shape, compiler_params=_params(("parallel",)),
    )(_my_block()[None], *ins)


def _local_step(x, target, gains, conv_ws, kv_gain, weights_of, send_grads):
    depth = len(gains)
    n_a = len(conv_ws)
    saved, ws = [], []
    kv = kvn = None
    _, (xn,) = _norm_res_fwd(x, None, None, [gains[0][0]], "norm_first")
    h = x
    for l in range(depth):
        g = gains[l]
        sv = {"x_in": h, "xn": xn}
        w = weights_of(l, "mix", h)
        ws.append(w)
        if l == n_a:
            kv = _mm(kvn, w["kv"], "nn", F32, "kv_fwd")
        if l < n_a:
            p = _mm(xn, w["conv_in"], "nn", BF16, f"conv_in_fwd_{l}")
            z = _conv_gate_fwd(p, conv_ws[l], f"conv_gate_fwd_{l}")
            mix, x1, xn2 = _mm_norm_res(z, w["conv_out"], h, g[1], g[2], f"conv_out_fwd_{l}")
            sv.update(p=p, z=z)
        else:
            j = l - n_a
            q = _mm(xn, w["q"], "nn", F32, f"q_fwd_{j}", scale=HEAD_DIM ** -0.5)
            o, lse = _attention_fwd(q, kv, f"attn_fwd_{j}")
            mix, x1, xn2 = _mm_norm_res(o, w["o"], h, g[1], g[2], f"o_fwd_{j}")
            sv.update(q=q, o=o, lse=lse)
        w.update(weights_of(l, "ffn", mix))
        f, a = _ffn_in_swiglu(xn2, w["ffn_in"], f"ffn_in_fwd_{l}")
        ff = _mm(a, w["ffn_out"], "nn", BF16, f"ffn_out_fwd_{l}")
        sv.update(mix=mix, x1=x1, xn2=xn2, f=f, a=a, ff=ff)
        saved.append(sv)
        if l == depth - 1:
            dx, loss = _norm_res_loss(x1, ff, g[3], target, "norm_loss")
        elif l == n_a - 1:
            h, _ = _norm_res_fwd(x1, ff, g[3], [], f"norm_end_{l}")
            h = _permute16(h, False, "permute_stream")
            _, (xn, kvn) = _norm_res_fwd(h, None, None, [gains[l + 1][0], kv_gain], "norm_permuted")
        else:
            h, (xn,) = _norm_res_fwd(x1, ff, g[3], [gains[l + 1][0]], f"norm_end_{l}")
    d_gains = [[None] * 4 for _ in range(depth)]
    d_conv = [None] * n_a
    d_kv_gain = None
    dkv = None
    _, _, dff, d_gains[depth - 1][3] = _norm_bwd(dx, [], None, (saved[-1]["ff"], gains[-1][3]), "norm_loss_bwd")
    for l in reversed(range(depth)):
        sv, g, w, grads = saved[l], gains[l], ws[l], {}
        grads["ffn_out"] =_mm(sv["a"], dff, "tn", BF16, f"ffn_out_dw_{l}")
        df = _ffn_out_dx_swiglu(dff, w["ffn_out"], sv["f"], f"ffn_out_dx_{l}")
        dxn2 = _mm(df, w["ffn_in"], "nt", BF16, f"ffn_in_dx_{l}")
        grads["ffn_in"] =_mm(sv["xn2"], df, "tn", BF16, f"ffn_in_dw_{l}")
        dx, (d_gains[l][2],), dmix, d_gains[l][1] = _norm_bwd(
            dx, [(dxn2, g[2])], sv["x1"], (sv["mix"], g[1]), f"norm_mid_bwd_{l}")
        dx, dmix = send_grads(l, "ffn", grads, [dx, dmix])
        if l < n_a:
            dz = _mm(dmix, w["conv_out"], "nt", BF16, f"conv_out_dx_{l}")
            grads["conv_out"] =_mm(sv["z"], dmix, "tn", BF16, f"conv_out_dw_{l}")
            dp, d_conv[l] = _conv_gate_bwd(sv["p"], dz, conv_ws[l], f"conv_gate_bwd_{l}")
            dxn = _mm(dp, w["conv_in"], "nt", BF16, f"conv_in_dx_{l}")
            grads["conv_in"] =_mm(sv["xn"], dp, "tn", BF16, f"conv_in_dw_{l}")
        else:
            j = l - n_a
            do = _mm(dmix, w["o"], "nt", F32, f"o_dx_{j}")
            grads["o"] =_mm(sv["o"], dmix, "tn", BF16, f"o_dw_{j}")
            dq, dkv = _attention_bwd(sv["q"], kv, sv["o"], do, sv["lse"], dkv, f"attn_bwd_{j}")
            scale = HEAD_DIM ** -0.5
            dxn = _mm(dq, w["q"], "nt", BF16, f"q_dx_{j}", scale=scale)
            grads["q"] =_mm(sv["xn"], dq, "tn", BF16, f"q_dw_{j}", scale=scale)
        branches = [(dxn, g[0])]
        if l == n_a:
            dkvn = _mm(dkv, w["kv"], "nt", BF16, "kv_dx")
            grads["kv"] =_mm(kvn, dkv, "tn", BF16, "kv_dw")
            branches.append((dkvn, kv_gain))
        post = (saved[l - 1]["ff"], gains[l - 1][3]) if l > 0 else None
        if l == n_a:
            dx, dgs, _, _ = _norm_bwd(dx, branches, sv["x_in"], None, f"norm_end_bwd_{l}")
            dx = _permute16(dx, True, "unpermute_stream")
            _, _, dff, dg_post = _norm_bwd(dx, [], None, post, "norm_boundary_bwd")
        else:
            dx, dgs, dff, dg_post = _norm_bwd(dx, branches, sv["x_in"], post, f"norm_end_bwd_{l}")
        if dff is None:
            send_grads(l, "mix", grads, [])
        else:
            dx, dff = send_grads(l, "mix", grads, [dx, dff])
        d_gains[l][0] = dgs[0]
        if l == n_a:
            d_kv_gain = dgs[1]
        if l > 0:
            d_gains[l - 1][3] = dg_post
    return loss, dx, d_gains, d_conv, d_kv_gain


BIG = (
    ("conv_in", 1), ("conv_out", 0), ("kv", 1), ("q", 0), ("o", 0), ("ffn_in", 1), ("ffn_out", 0))


def kernel(x, norm_g, conv_in_w, conv_w, conv_out_w, kv_norm_g, kv_w, q_w, o_w, ffn_in_w, ffn_out_w, loss_target, m_norm_g, m_conv_in_w, m_conv_w, m_conv_out_w, m_kv_norm_g, m_kv_w, m_q_w, m_o_w, m_ffn_in_w, m_ffn_out_w, v_norm_g, v_conv_in_w, v_conv_w, v_conv_out_w, v_kv_norm_g, v_kv_w, v_q_w, v_o_w, v_ffn_in_w, v_ffn_out_w):
    depth, _, dq = norm_g.shape
    d = 4 * dq
    n_a = conv_w.shape[0]
    big_w = {"conv_in": conv_in_w, "conv_out": conv_out_w, "kv": kv_w[None], "q": q_w, "o": o_w,
             "ffn_in": ffn_in_w, "ffn_out": ffn_out_w}
    big_m = {"conv_in": m_conv_in_w, "conv_out": m_conv_out_w, "kv": m_kv_w[None], "q": m_q_w, "o": m_o_w,
             "ffn_in": m_ffn_in_w, "ffn_out": m_ffn_out_w}
    big_v = {"conv_in": v_conv_in_w, "conv_out": v_conv_out_w, "kv": v_kv_w[None], "q": v_q_w, "o": v_o_w,
             "ffn_in": v_ffn_in_w, "ffn_out": v_ffn_out_w}

    n_gain, n_tap = depth * 4, n_a * conv_w.shape[1]
    small_rows = -(-(n_gain + n_tap + 1) // 8) * 8
    pad_rows = small_rows - n_gain - n_tap

    def pack_small(gains, taps):
        return jnp.concatenate([gains.reshape(n_gain, dq), taps.reshape(n_tap, dq), jnp.zeros((pad_rows, dq), F32)])

    axis_of = dict(BIG)

    def matrices_of(l, part):
        if part == "ffn":
            return [("ffn_in", l), ("ffn_out", l)]
        if l < n_a:
            return [("conv_in", l), ("conv_out", l)]
        return ([("kv", 0)] if l == n_a else []) + [("q", l - n_a), ("o", l - n_a)]

    halves = [(l, part) for l in range(depth) for part in ("mix", "ffn")]
    groups = {(l, part): [(*_cast_place(big_w[name], i, axis_of[name], BF16, f"place_{name}_{i}"), axis_of[name])
                          for name, i in matrices_of(l, part)] for l, part in halves}
    groups[halves[0]].append((*_cast_place(pack_small(norm_g, conv_w)[None], 0, 1, F32, "place_small"), 1))
    started = {halves[0]: _gather_start(groups[halves[0]], [], "gather_start_0_mix")[0]}

    def fetch(half, after):
        full = _gather_wait(groups[half], started[half], after, "gather_wait_%d_%s" % half)
        nxt = halves.index(half) + 1
        if nxt < len(halves):
            started[halves[nxt]], full = _gather_start(groups[halves[nxt]], full, "gather_start_%d_%s" % halves[nxt])
        return full

    target = _permute16(loss_target.reshape(x.shape[1:]), False, "permute_target")
    first = fetch(halves[0], target)
    small = first[-1]
    gains = [[small[4 * l + i][None] for i in range(4)] for l in range(depth)]
    conv_ws = [small[n_gain + 3 * l:n_gain + 3 * l + 3] for l in range(n_a)]
    kv_gain = kv_norm_g[None]

    def weights_of(l, part, after):
        full = first if (l, part) == halves[0] else fetch((l, part), after)
        return {name: full[t] for t, (name, _) in enumerate(matrices_of(l, part))}

    sent, paired = {}, {}
    LAG = 2

    def to_sibling(half, carry, after):
        axes = [axis_of[name] for name, _ in matrices_of(*half)]
        full, stacks = _scatter_wait(axes, sent[half], after, "scatter_wait_%d_%s" % half)
        paired[half], carry = _pair_start(list(zip(full, axes, stacks)), carry, "pair_start_%d_%s" % half)
        return carry

    def send_grads(l, part, grads, carry):
        sent[l, part], carry = _scatter_start(
            [(grads[name], axis_of[name]) for name, _ in matrices_of(l, part)], carry, f"scatter_start_{l}_{part}")
        older = halves.index((l, part)) + LAG
        if older < len(halves) and carry:
            carry = to_sibling(halves[older], carry, carry[0])
        return carry

    loss, dx, d_gains, d_conv, d_kv_gain = _local_step(
        x.reshape(x.shape[1:]), target, gains, conv_ws, kv_gain, weights_of, send_grads)
    loss = lax.psum(loss, ("x", "y", "c"))

    small_g = jnp.concatenate([dg for row in d_gains for dg in row] + list(d_conv) + [d_kv_gain]
                              + [jnp.zeros((pad_rows - 1, d), F32)])
    small_g = _allreduce_small(small_g, "allreduce_small")
    blk = 2 * lax.axis_index("x") + lax.axis_index("y")
    mine_small = lax.dynamic_slice_in_dim(small_g, blk * dq, dq, axis=1)
    kv_rows = d // dq

    def pack_opt(gains_like, taps_like, kv_like):
        rows = jnp.concatenate([gains_like.reshape(n_gain, dq), taps_like.reshape(n_tap, dq), kv_like.reshape(kv_rows, dq)])
        extra = -rows.shape[0] % 8
        return jnp.concatenate([rows, jnp.zeros((extra, dq), F32)]) if extra else rows

    sw = pack_opt(norm_g, conv_w, kv_norm_g)
    sm = pack_opt(m_norm_g, m_conv_w, m_kv_norm_g)
    sv = pack_opt(v_norm_g, v_conv_w, v_kv_norm_g)
    sg = pack_opt(mine_small[:n_gain], mine_small[n_gain:n_gain + n_tap], small_g[n_gain + n_tap])
    s_out = _adamw(sw, sm, sv, sg, "adamw_small")

    def unpack(a):
        return (a[:n_gain].reshape(depth, 4, dq), a[n_gain:n_gain + n_tap].reshape(n_a, -1, dq),
                a[n_gain + n_tap:n_gain + n_tap + kv_rows].reshape(d))

    small_out = [unpack(a) for a in s_out]

    landed, big_out = {}, {}

    def update(name):
        for half in halves:
            if matrices_of(*half)[0] not in landed and any(n == name for n, _ in matrices_of(*half)):
                axes = [axis_of[n] for n, _ in matrices_of(*half)]
                landed.update(zip(matrices_of(*half), zip(*_pair_wait(axes, paired[half], "pair_wait_%d_%s" % half))))
        shp, ax = big_w[name].shape, axis_of[name]
        rows, cols = shp[0] * shp[1], shp[2]
        flat = lambda a: a.reshape(rows, cols)
        full, stacks, sibling = zip(*[landed[name, i] for i in range(shp[0])])
        res = _adamw(flat(big_w[name]), flat(big_m[name]), flat(big_v[name]),
                     [(full[i], ax, stacks[i], sibling[i]) for i in range(shp[0])], f"adamw_{name}")
        big_out[name] = [a.reshape(shp[1:] if name == "kv" else shp) for a in res]

    pending = [half for half in reversed(halves) if half not in paired]
    for half in pending[:-1]:
        to_sibling(half, [], None)
    late = [name for name, _ in BIG if any(n == name for n, _ in matrices_of(*pending[-1]))]
    for name, _ in BIG:
        if name not in late:
            update(name)
    to_sibling(pending[-1], [], big_out["ffn_out"][0])
    for name in late:
        update(name)

    def leaves(i):
        ng, cw_, kg = small_out[i]
        return [ng, big_out["conv_in"][i], cw_, big_out["conv_out"][i], kg, big_out["kv"][i], big_out["q"][i],
                big_out["o"][i], big_out["ffn_in"][i], big_out["ffn_out"][i]]

    return (loss, dx.reshape(x.shape), *leaves(0), *leaves(1), *leaves(2), *leaves(3))
```

```python
import jax
import jax.numpy as jnp
import numpy as np
from jax import lax
from jax.experimental import pallas as pl
from jax.experimental.pallas import tpu as pltpu

F32 = jnp.float32
BF16 = jnp.bfloat16
HEAD_DIM = 64
DILATIONS = (1, 4, 16)
NORM_EPS = 1e-6
NEG_BIG = -1e30
VMEM_LIMIT = 48 * 1024 * 1024
ROW_TILE = 256
NORM_TILE = 512
LANE = 128
MESH = pl.DeviceIdType.MESH

ADAM_LR = 0.001
ADAM_B1 = 0.9
ADAM_B2 = 0.999
ADAM_EPS = 1e-08
ADAM_WD = 0.01
ADAM_STEP = 10

TILE_CANDIDATES = (1024, 1408, 768, 512, 384, 256, 128)


def _pick(dim, cands=TILE_CANDIDATES):
    for c in cands:
        if c <= dim and dim % c == 0:
            return c
    return dim


def _params(sem):
    return pltpu.CompilerParams(dimension_semantics=sem, vmem_limit_bytes=VMEM_LIMIT)


def _mm(a, b, mode, out_dtype, name, scale=None):
    a_planes = a.shape[0] if a.ndim == 3 else 1
    b_planes = b.shape[0] if b.ndim == 3 else 1
    if mode == "nn":
        m, k = a.shape[-2], a.shape[-1] * a_planes
        n = b.shape[1]
    elif mode == "nt":
        m, k = a.shape[-2], a.shape[-1] * a_planes
        n = b.shape[0]
    else:
        k, m = a.shape
        n = b.shape[-1] * b_planes
    tm, tn = _pick(m), _pick(n // b_planes)
    tk = _pick(k // a_planes, ((2048,) if mode == "tn" else (3072, 2816)) + TILE_CANDIDATES)
    nk = k // tk
    ka, nb = k // a_planes // tk, n // b_planes // tn
    if a_planes > 1:
        a_spec = pl.BlockSpec((None, tm, tk), lambda i, j, kk: (kk // ka, i, kk % ka))
    elif mode == "tn":
        a_spec = pl.BlockSpec((tk, tm), lambda i, j, kk: (kk, i))
    else:
        a_spec = pl.BlockSpec((tm, tk), lambda i, j, kk: (i, kk))
    if mode == "nn":
        b_spec = pl.BlockSpec((tk, tn), lambda i, j, kk: (kk, j))
        dims = (((1,), (0,)), ((), ()))
    elif mode == "nt":
        b_spec = pl.BlockSpec((tn, tk), lambda i, j, kk: (j, kk))
        dims = (((1,), (1,)), ((), ()))
    else:
        b_spec = (pl.BlockSpec((None, tk, tn), lambda i, j, kk: (j // nb, kk, j % nb)) if b_planes > 1
                  else pl.BlockSpec((tk, tn), lambda i, j, kk: (kk, j)))
        dims = (((0,), (0,)), ((), ()))

    def finish(acc):
        if scale is not None:
            acc = acc * scale
        return acc.astype(out_dtype)

    if nk == 1:
        def body(a_ref, b_ref, o_ref):
            o_ref[...] = finish(lax.dot_general(a_ref[...].astype(BF16), b_ref[...].astype(BF16), dims, preferred_element_type=F32))
        scratch = []
    else:
        def body(a_ref, b_ref, o_ref, acc_ref):
            kk = pl.program_id(2)

            @pl.when(kk == 0)
            def _():
                acc_ref[...] = jnp.zeros_like(acc_ref)

            acc_ref[...] += lax.dot_general(a_ref[...].astype(BF16), b_ref[...].astype(BF16), dims, preferred_element_type=F32)

            @pl.when(kk == nk - 1)
            def _():
                o_ref[...] = finish(acc_ref[...])
        scratch = [pltpu.VMEM((tm, tn), F32)]

    return pl.pallas_call(
        body, name=name,
        grid=(m // tm, n // tn, nk),
        in_specs=[a_spec, b_spec],
        out_specs=pl.BlockSpec((tm, tn), lambda i, j, kk: (i, j)),
        out_shape=jax.ShapeDtypeStruct((m, n), out_dtype),
        scratch_shapes=scratch,
        compiler_params=_params(("parallel", "parallel", "arbitrary")),
    )(a, b)


def _rstd(v):
    return lax.rsqrt(jnp.mean(v * v, axis=-1, keepdims=True) + NORM_EPS)


def _rms_bwd(dy, v, g, r):
    gy = dy * g
    dv = r * (gy - v * (r * r) * jnp.mean(gy * v, axis=-1, keepdims=True))
    return dv, dy * v * r


def _row_spec(t, width):
    return pl.BlockSpec((t, width), lambda i: (i, 0))


def _gain_spec(width):
    return pl.BlockSpec((1, width), lambda i: (0, 0))


def _norm_res_fwd(x, mix, g_post, pre_gains, name):
    s, d = x.shape
    t = _pick(s, (NORM_TILE, ROW_TILE))
    has_mix = mix is not None
    n_pre = len(pre_gains)

    def body(*refs):
        x_ref = refs[0]
        pos = 1
        x1 = x_ref[...]
        if has_mix:
            mv = refs[1][...].astype(F32)
            x1 = x1 + mv * _rstd(mv) * refs[2][...]
            pos = 3
        gains = refs[pos:pos + n_pre]
        outs = refs[pos + n_pre:]
        if has_mix:
            outs[0][...] = x1
            outs = outs[1:]
        r = _rstd(x1)
        for g_ref, o_ref in zip(gains, outs):
            o_ref[...] = (x1 * r * g_ref[...]).astype(BF16)

    ins = [x] + ([mix, g_post] if has_mix else []) + list(pre_gains)
    in_specs = [_row_spec(t, d)] + ([_row_spec(t, d), _gain_spec(d)] if has_mix else []) + [_gain_spec(d)] * n_pre
    out_shape = ([jax.ShapeDtypeStruct((s, d), F32)] if has_mix else []) + [jax.ShapeDtypeStruct((s, d), BF16)] * n_pre
    out_specs = [_row_spec(t, d)] * len(out_shape)
    res = pl.pallas_call(
        body, name=name, grid=(s // t,), in_specs=in_specs, out_specs=out_specs, out_shape=out_shape,
        compiler_params=_params(("parallel",)),
    )(*ins)
    if has_mix:
        return res[0], list(res[1:])
    return x, list(res)


def _mm_norm_res(a, b, x, g_post, pre_gain, name):
    m, k = a.shape
    d = b.shape[1]
    tm, tk = _pick(m, (512, 256)), _pick(k)
    nk = k // tk

    def body(a_ref, b_ref, x_ref, gp_ref, g_ref, mix_ref, x1_ref, xn_ref, acc_ref):
        kk = pl.program_id(1)

        @pl.when(kk == 0)
        def _():
            acc_ref[...] = jnp.zeros_like(acc_ref)

        acc_ref[...] += jnp.dot(a_ref[...].astype(BF16), b_ref[...], preferred_element_type=F32)

        @pl.when(kk == nk - 1)
        def _():
            mix = acc_ref[...].astype(BF16)
            mix_ref[...] = mix
            mv = mix.astype(F32)
            x1 = x_ref[...] + mv * _rstd(mv) * gp_ref[...]
            x1_ref[...] = x1
            xn_ref[...] = (x1 * _rstd(x1) * g_ref[...]).astype(BF16)

    rows = pl.BlockSpec((tm, d), lambda i, kk: (i, 0))
    gain = pl.BlockSpec((1, d), lambda i, kk: (0, 0))
    return pl.pallas_call(
        body, name=name, grid=(m // tm, nk),
        in_specs=[pl.BlockSpec((tm, tk), lambda i, kk: (i, kk)), pl.BlockSpec((tk, d), lambda i, kk: (kk, 0)),
                  rows, gain, gain],
        out_specs=[rows, rows, rows],
        out_shape=[jax.ShapeDtypeStruct((m, d), BF16), jax.ShapeDtypeStruct((m, d), F32),
                   jax.ShapeDtypeStruct((m, d), BF16)],
        scratch_shapes=[pltpu.VMEM((tm, d), F32)],
        compiler_params=_params(("parallel", "arbitrary")),
    )(a, b, x, g_post, pre_gain)


def _norm_res_loss(x, mix, g_post, target, name):
    s, d = x.shape
    t = _pick(s, (NORM_TILE, ROW_TILE))

    def body(x_ref, m_ref, g_ref, t_ref, dy_ref, loss_ref):
        mv = m_ref[...].astype(F32)
        y = x_ref[...] + mv * _rstd(mv) * g_ref[...]
        err = y - t_ref[...]
        dy_ref[...] = err * (1.0 / d)

        @pl.when(pl.program_id(0) == 0)
        def _():
            loss_ref[...] = jnp.zeros_like(loss_ref)

        loss_ref[...] += jnp.sum(err * err)

    dy, acc = pl.pallas_call(
        body, name=name, grid=(s // t,),
        in_specs=[_row_spec(t, d), _row_spec(t, d), _gain_spec(d), _row_spec(t, d)],
        out_specs=[_row_spec(t, d), pl.BlockSpec((8, LANE), lambda i: (0, 0))],
        out_shape=[jax.ShapeDtypeStruct((s, d), F32), jax.ShapeDtypeStruct((8, LANE), F32)],
        compiler_params=_params(("arbitrary",)),
    )(x, mix, g_post, target)
    return dy, acc[0, 0] * (0.5 / d)


def _norm_bwd(dx_out, branches, x_in, post, name):
    s, d = dx_out.shape
    t = _pick(s, (NORM_TILE, ROW_TILE))
    nb = len(branches)
    has_post = post is not None

    def body(*refs):
        dx_ref = refs[0]
        pos = 1
        dx = dx_ref[...]
        first = pl.program_id(0) == 0
        n_in = 1 + (1 + 2 * nb if nb else 0) + (2 if has_post else 0)
        outs = refs[n_in:]
        opos = 0
        if nb:
            xv = refs[pos][...]
            pos += 1
            r = _rstd(xv)
            dx_o = outs[0]
            opos = 1
            for _ in range(nb):
                dxn = refs[pos][...].astype(F32)
                g = refs[pos + 1][...]
                pos += 2
                dv, dg_rows = _rms_bwd(dxn, xv, g, r)
                dx = dx + dv
                dg_ref = outs[opos]
                opos += 1

                @pl.when(first)
                def _(dg_ref=dg_ref):
                    dg_ref[...] = jnp.zeros_like(dg_ref)

                dg_ref[...] += jnp.sum(dg_rows, axis=0, keepdims=True)
            dx_o[...] = dx
        if has_post:
            mv = refs[pos][...].astype(F32)
            g = refs[pos + 1][...]
            dm, dg_rows = _rms_bwd(dx, mv, g, _rstd(mv))
            outs[opos][...] = dm.astype(BF16)
            dg_ref = outs[opos + 1]

            @pl.when(first)
            def _():
                dg_ref[...] = jnp.zeros_like(dg_ref)

            dg_ref[...] += jnp.sum(dg_rows, axis=0, keepdims=True)

    ins, in_specs = [dx_out], [_row_spec(t, d)]
    out_shape, out_specs = [], []
    if nb:
        ins.append(x_in)
        in_specs.append(_row_spec(t, d))
        out_shape.append(jax.ShapeDtypeStruct((s, d), F32))
        out_specs.append(_row_spec(t, d))
        for dxn, g in branches:
            ins += [dxn, g]
            in_specs += [_row_spec(t, d), _gain_spec(d)]
            out_shape.append(jax.ShapeDtypeStruct((1, d), F32))
            out_specs.append(_gain_spec(d))
    if has_post:
        ins += [post[0], post[1]]
        in_specs += [_row_spec(t, d), _gain_spec(d)]
        out_shape += [jax.ShapeDtypeStruct((s, d), BF16), jax.ShapeDtypeStruct((1, d), F32)]
        out_specs += [_row_spec(t, d), _gain_spec(d)]
    res = pl.pallas_call(
        body, name=name, grid=(s // t,), in_specs=in_specs, out_specs=out_specs, out_shape=out_shape,
        compiler_params=_params(("arbitrary",)),
    )(*ins)
    res = list(res)
    dx_in = res.pop(0) if nb else dx_out
    dgs = [res.pop(0) for _ in range(nb)]
    dm, dg_post = (res[0], res[1]) if has_post else (None, None)
    return dx_in, dgs, dm, dg_post


HALO = 16


def _shift_down(u, prev, k):
    rows = lax.broadcasted_iota(jnp.int32, u.shape, 0)
    out = pltpu.roll(u, k, 0)
    for i in range(k):
        out = jnp.where(rows == i, prev[HALO - k + i:HALO - k + i + 1, :], out)
    return out


def _shift_up(u, nxt, k):
    n = u.shape[0]
    rows = lax.broadcasted_iota(jnp.int32, u.shape, 0)
    out = pltpu.roll(u, n - k, 0)
    for i in range(k):
        out = jnp.where(rows == n - k + i, nxt[i:i + 1, :], out)
    return out


def _conv_gate_fwd(p, cw, name):
    s, d3 = p.shape
    d = d3 // 3
    t = _pick(s, (ROW_TILE,))
    hb = t // HALO

    def body(p_ref, prev_ref, w_ref, z_ref):
        i = pl.program_id(0)
        pv = p_ref[...].astype(F32)
        b, u = pv[:, :d], pv[:, d:2 * d] * pv[:, 2 * d:]
        ph = prev_ref[...].astype(F32)
        up = jnp.where(i > 0, ph[:, d:2 * d] * ph[:, 2 * d:], 0.0)
        w = w_ref[...]
        y = w[0:1, :] * _shift_down(u, up, 2) + w[1:2, :] * _shift_down(u, up, 1) + w[2:3, :] * u
        z_ref[...] = (b * y).astype(BF16)

    return pl.pallas_call(
        body, name=name, grid=(s // t,),
        in_specs=[_row_spec(t, d3),
                  pl.BlockSpec((HALO, d3), lambda i: (jnp.maximum(i * hb - 1, 0), 0)),
                  pl.BlockSpec((3, d), lambda i: (0, 0))],
        out_specs=_row_spec(t, d),
        out_shape=jax.ShapeDtypeStruct((s, d), BF16),
        compiler_params=_params(("parallel",)),
    )(p, p, cw)


def _conv_gate_bwd(p, dz, cw, name):
    s, d3 = p.shape
    d = d3 // 3
    t = _pick(s, (ROW_TILE,))
    hb = t // HALO
    nt = s // t
    last_halo = s // HALO - 1

    def body(p_ref, prev_ref, next_ref, dz_ref, dznext_ref, w_ref, dp_ref, dw_ref):
        i = pl.program_id(0)
        pv = p_ref[...].astype(F32)
        b, c, h = pv[:, :d], pv[:, d:2 * d], pv[:, 2 * d:]
        u = c * h
        ph = prev_ref[...].astype(F32)
        up = jnp.where(i > 0, ph[:, d:2 * d] * ph[:, 2 * d:], 0.0)
        w = w_ref[...]
        u1, u2 = _shift_down(u, up, 1), _shift_down(u, up, 2)
        y = w[0:1, :] * u2 + w[1:2, :] * u1 + w[2:3, :] * u
        dz = dz_ref[...].astype(F32)
        dy = dz * b
        dyn = jnp.where(i < nt - 1, dznext_ref[...].astype(F32) * next_ref[...].astype(F32)[:, :d], 0.0)
        du = w[2:3, :] * dy + w[1:2, :] * _shift_up(dy, dyn, 1) + w[0:1, :] * _shift_up(dy, dyn, 2)
        dp_ref[:, :d] = (dz * y).astype(BF16)
        dp_ref[:, d:2 * d] = (du * h).astype(BF16)
        dp_ref[:, 2 * d:] = (du * c).astype(BF16)

        @pl.when(i == 0)
        def _():
            dw_ref[...] = jnp.zeros_like(dw_ref)

        dw_ref[0:1, :] += jnp.sum(dy * u2, axis=0, keepdims=True)
        dw_ref[1:2, :] += jnp.sum(dy * u1, axis=0, keepdims=True)
        dw_ref[2:3, :] += jnp.sum(dy * u, axis=0, keepdims=True)

    return pl.pallas_call(
        body, name=name, grid=(nt,),
        in_specs=[_row_spec(t, d3),
                  pl.BlockSpec((HALO, d3), lambda i: (jnp.maximum(i * hb - 1, 0), 0)),
                  pl.BlockSpec((HALO, d3), lambda i: (jnp.minimum((i + 1) * hb, last_halo), 0)),
                  _row_spec(t, d),
                  pl.BlockSpec((HALO, d), lambda i: (jnp.minimum((i + 1) * hb, last_halo), 0)),
                  pl.BlockSpec((3, d), lambda i: (0, 0))],
        out_specs=[_row_spec(t, d3), pl.BlockSpec((3, d), lambda i: (0, 0))],
        out_shape=[jax.ShapeDtypeStruct((s, d3), BF16), jax.ShapeDtypeStruct((3, d), F32)],
        compiler_params=_params(("arbitrary",)),
    )(p, p, p, dz, dz, cw)


FFN_ROWS, FFN_COLS = (512, 256), (1408, 768, 256, 128)


def _row_chunks(tm, rows=256):
    return [slice(r, r + min(rows, tm)) for r in range(0, tm, min(rows, tm))]


def _ffn_in_swiglu(xn, w_in, name):
    s, k = xn.shape
    ff = w_in.shape[1] // 2
    tm, tn = _pick(s, FFN_ROWS), _pick(ff, FFN_COLS)
    nj = ff // tn

    def body(x_ref, wg_ref, wu_ref, f_ref, a_ref):
        for rows in _row_chunks(tm):
            xv = x_ref[rows, :]
            gate = jnp.dot(xv, wg_ref[...], preferred_element_type=F32)
            up = jnp.dot(xv, wu_ref[...], preferred_element_type=F32)
            f_ref[0, rows, :] = gate.astype(BF16)
            f_ref[1, rows, :] = up.astype(BF16)
            a_ref[rows, :] = (gate * jax.nn.sigmoid(gate) * up).astype(BF16)

    return pl.pallas_call(
        body, name=name, grid=(nj, s // tm),
        in_specs=[pl.BlockSpec((tm, k), lambda j, i: (i, 0)),
                  pl.BlockSpec((k, tn), lambda j, i: (0, j)),
                  pl.BlockSpec((k, tn), lambda j, i: (0, nj + j))],
        out_specs=[pl.BlockSpec((2, tm, tn), lambda j, i: (0, i, j)), pl.BlockSpec((tm, tn), lambda j, i: (i, j))],
        out_shape=[jax.ShapeDtypeStruct((2, s, ff), BF16), jax.ShapeDtypeStruct((s, ff), BF16)],
        compiler_params=_params(("parallel", "parallel")),
    )(xn, w_in, w_in)


def _ffn_out_dx_swiglu(dff, w_out, f, name):
    s, d = dff.shape
    ff = w_out.shape[0]
    tm, tn = _pick(s, FFN_ROWS), _pick(ff, FFN_COLS)

    def body(d_ref, w_ref, f_ref, df_ref):
        for rows in _row_chunks(tm):
            da = lax.dot_general(d_ref[rows, :], w_ref[...], (((1,), (1,)), ((), ())), preferred_element_type=F32)
            gate = f_ref[0, rows, :].astype(F32)
            up = f_ref[1, rows, :].astype(F32)
            sg = jax.nn.sigmoid(gate)
            silu = gate * sg
            df_ref[0, rows, :] = (da * up * (sg + silu * (1.0 - sg))).astype(BF16)
            df_ref[1, rows, :] = (da * silu).astype(BF16)

    planes = pl.BlockSpec((2, tm, tn), lambda j, i: (0, i, j))
    return pl.pallas_call(
        body, name=name, grid=(ff // tn, s // tm),
        in_specs=[pl.BlockSpec((tm, d), lambda j, i: (i, 0)), pl.BlockSpec((tn, d), lambda j, i: (j, 0)), planes],
        out_specs=planes, out_shape=jax.ShapeDtypeStruct((2, s, ff), BF16),
        compiler_params=_params(("parallel", "parallel")),
    )(dff, w_out, f)


SUPER = 2048
RES = 16
PAIR = 128
L = 128
FWD_TOGETHER = 8
BWD_TOGETHER = 4


def _alibi_slopes(n_heads):
    h = np.arange(n_heads, dtype=np.float32) + 1.0
    return np.power(2.0, -8.0 * h / n_heads).astype(np.float32)


def _permute16(x, inverse, name):
    s, d = x.shape
    cw = LANE

    def body(x_ref, o_ref):
        if inverse:
            for m in range(L):
                o_ref[RES * m:RES * (m + 1), :] = x_ref[pl.ds(m, RES, stride=L), :]
        else:
            for r in range(RES):
                o_ref[L * r:L * (r + 1), :] = x_ref[pl.ds(r, L, stride=RES), :]

    spec = pl.BlockSpec((SUPER, cw), lambda i, j: (i, j))
    return pl.pallas_call(
        body, name=name, grid=(s // SUPER, d // cw), in_specs=[spec], out_specs=spec,
        out_shape=jax.ShapeDtypeStruct((s, d), x.dtype),
        compiler_params=_params(("parallel", "parallel")),
    )(x)


def _slope_table(d):
    nh = d // HEAD_DIM
    sl = _alibi_slopes(nh)
    tab = np.repeat(sl, HEAD_DIM).reshape(d // PAIR, 1, PAIR)
    return jnp.asarray(np.broadcast_to(tab, (d // PAIR, 8, PAIR)).copy())


def _geometry(dil):
    nch = RES // dil
    return nch, L // nch


def _band(dil):
    nch, w = _geometry(dil)
    sh = w.bit_length() - 1
    i = lax.broadcasted_iota(jnp.int32, (L, 2 * L), 0)
    j = lax.broadcasted_iota(jnp.int32, (L, 2 * L), 1)

    def pos(t):
        return jnp.bitwise_and(t, w - 1) * nch + jnp.right_shift(t, sh)

    delta = pos(i) + L - (pos(jnp.bitwise_and(j, L - 1)) + jnp.bitwise_and(j, L))
    return (delta * dil).astype(F32), (delta >= 0) & (delta <= L), j < L


def _fill_bias(bias_s, sl_ref):
    for b, dil in enumerate(DILATIONS):
        base, band, prev_half = _band(dil)
        for first in range(2):
            valid = band & jnp.logical_not(prev_half) if first else band
            for h in range(2):
                slope = sl_ref[0:1, HEAD_DIM * h:HEAD_DIM * h + 1]
                bias_s[(2 * b + first) * 2 + h] = jnp.where(valid, -slope * base, NEG_BIG)


def _bias_index(b, sb, n):
    first = jnp.logical_and(sb == 0, n == 0).astype(jnp.int32)
    return (2 * b + first) * 2


def _offsets(dil, res, n):
    nch, w = _geometry(dil)

    def al(v):
        return v if isinstance(v, int) else pl.multiple_of(v, w)

    q_off = [al((a * dil + res) * L + n * w) for a in range(nch)]
    k_off = [al((a * dil + res) * 2 * L + L + n * w) for a in range(nch)]
    kp_off = [al((a * dil + res) * 2 * L + L + n * w - w) for a in range(nch)]
    return q_off, k_off, kp_off, w


def _gather(ref, offs, w):
    parts = [ref[pl.ds(o, w), :] for o in offs]
    return parts[0] if len(parts) == 1 else jnp.concatenate(parts, axis=0)


def _scatter(ref, offs, w, val, add=False):
    for a, o in enumerate(offs):
        piece = val[a * w:(a + 1) * w, :]
        if add:
            ref[pl.ds(o, w), :] += piece
        else:
            ref[pl.ds(o, w), :] = piece


def _fill_key_buffer(buf, prev_ref, cur_ref):
    for r in range(RES):
        buf[2 * L * r:2 * L * r + L, :] = prev_ref[L * r:L * (r + 1), :]
        buf[2 * L * r + L:2 * L * (r + 1), :] = cur_ref[L * r:L * (r + 1), :]


def _two_heads(x, low):
    zero = jnp.zeros_like(x)
    return jnp.concatenate([jnp.where(low, x, zero), jnp.where(low, zero, x)], axis=0)


def _loop_blocks(dil, stages, together):
    together = max(together, dil) if dil < RES else together

    def it(i, c):
        if dil == RES:
            blocks = [(i * together + k, 0) for k in range(together)]
        else:
            blocks = [(res, i * (together // dil) + k) for k in range(together // dil) for res in range(dil)]
        state = [stages[0](res, n) for res, n in blocks]
        for stage in stages[1:]:
            state = [stage(res, n, prev) for (res, n), prev in zip(blocks, state)]
        for writes in state:
            for args in writes:
                _scatter(*args)
        return c

    lax.fori_loop(0, RES // together, it, 0)


NT = (((1,), (1,)), ((), ()))
TN = (((0,), (0,)), ((), ()))


def _attention_fwd(q, kv, name):
    s, d = q.shape
    g_n, ns = d // PAIR, s // SUPER

    def body(sl_ref, q_ref, kc_ref, kp_ref, vc_ref, vp_ref, o_ref, lse_ref, kbuf, vbuf, m_s, l_s, acc_s, bias_s):
        sb = pl.program_id(1)
        _fill_key_buffer(kbuf, kp_ref, kc_ref)
        _fill_key_buffer(vbuf, vp_ref, vc_ref)
        pl.when(sb == 0)(lambda: _fill_bias(bias_s, sl_ref))
        low = lax.broadcasted_iota(jnp.int32, (L, PAIR), 1) < HEAD_DIM
        low_k = lax.broadcasted_iota(jnp.int32, (2 * L, PAIR), 1) < HEAD_DIM
        ones_bd = _two_heads(jnp.ones((2 * L, PAIR), BF16), low_k)
        for bi, dil in enumerate(DILATIONS):
            first_branch, last_branch = bi == 0, bi == len(DILATIONS) - 1

            def scores(res, n, dil=dil):
                q_off, k_off, kp_off, w = _offsets(dil, res, n)
                qf = _gather(q_ref, q_off, w).astype(BF16)
                kcat = jnp.concatenate([_gather(kbuf, kp_off, w), _gather(kbuf, k_off, w)], axis=0).astype(BF16)
                return lax.dot_general(qf, _two_heads(kcat, low_k), NT, preferred_element_type=F32)

            def update(res, n, sc, bi=bi, dil=dil, first_branch=first_branch, last_branch=last_branch):
                q_off, k_off, kp_off, w = _offsets(dil, res, n)
                vcat = jnp.concatenate([_gather(vbuf, kp_off, w), _gather(vbuf, k_off, w)], axis=0).astype(BF16)
                v_ones = jnp.concatenate([_two_heads(vcat, low_k), ones_bd], axis=1)
                bias_at = _bias_index(bi, sb, n)
                if not first_branch:
                    m_prev = _gather(m_s, q_off, w)
                ps, m_new = [], []
                for h in range(2):
                    s_h = sc[:, 2 * L * h:2 * L * (h + 1)] + bias_s[bias_at + h]
                    mh = jnp.max(s_h, axis=1, keepdims=True)
                    if not first_branch:
                        mh = jnp.maximum(mh, m_prev[:, HEAD_DIM * h:HEAD_DIM * h + 1])
                    ps.append(jnp.exp(s_h - mh).astype(BF16))
                    m_new.append(mh)
                m_full = jnp.where(low, m_new[0], m_new[1])
                both = jnp.dot(jnp.concatenate(ps, axis=1), v_ones, preferred_element_type=F32)
                acc, l_full = both[:, :PAIR], both[:, PAIR:]
                if not first_branch:
                    alpha = jnp.exp(m_prev - m_full)
                    l_full = _gather(l_s, q_off, w) * alpha + l_full
                    acc = _gather(acc_s, q_off, w) * alpha + acc
                if last_branch:
                    return [(o_ref, q_off, w, acc / l_full, False), (lse_ref, q_off, w, m_full + jnp.log(l_full), False)]
                return [(m_s, q_off, w, m_full, False), (l_s, q_off, w, l_full, False), (acc_s, q_off, w, acc, False)]

            _loop_blocks(dil, [scores, update], FWD_TOGETHER)

    prev = lambda i: jnp.maximum(i - 1, 0)
    blk = pl.BlockSpec((SUPER, PAIR), lambda g, i: (i, g))
    in_specs = [pl.BlockSpec((None, 8, PAIR), lambda g, i: (g, 0, 0)), blk,
                pl.BlockSpec((SUPER, PAIR), lambda g, i: (i, g)),
                pl.BlockSpec((SUPER, PAIR), lambda g, i: (prev(i), g)),
                pl.BlockSpec((SUPER, PAIR), lambda g, i: (i, g_n + g)),
                pl.BlockSpec((SUPER, PAIR), lambda g, i: (prev(i), g_n + g))]
    return pl.pallas_call(
        body, name=name, grid=(g_n, ns), in_specs=in_specs, out_specs=[blk, blk],
        out_shape=[jax.ShapeDtypeStruct((s, d), F32)] * 2,
        scratch_shapes=([pltpu.VMEM((2 * SUPER, PAIR), F32)] * 2 + [pltpu.VMEM((SUPER, PAIR), F32)] * 3
                        + [pltpu.VMEM((4 * len(DILATIONS), L, 2 * L), F32)]),
        compiler_params=_params(("parallel", "arbitrary")),
    )(_slope_table(d), q, kv, kv, kv, kv)


def _attention_bwd(q, kv, o, do, lse, dkv_in, name):
    s, d = q.shape
    g_n, ns = d // PAIR, s // SUPER
    has_in = dkv_in is not None

    def body(*refs):
        sl_ref, q_ref, do_ref, o_ref, lse_ref, kc_ref, kp_ref, vc_ref, vp_ref = refs[:9]
        pos = 9
        if has_in:
            dkv_in_ref = refs[9]
            pos = 10
        dq_ref, dkv_ref, kbuf, vbuf, dkbuf, dvbuf, dq_s, bias_s = refs[pos:]
        step = pl.program_id(1)
        sb = ns - 1 - step
        _fill_key_buffer(kbuf, kp_ref, kc_ref)
        _fill_key_buffer(vbuf, vp_ref, vc_ref)

        @pl.when(step == 0)
        def _():
            _fill_bias(bias_s, sl_ref)
            dkbuf[...] = jnp.zeros_like(dkbuf)
            dvbuf[...] = jnp.zeros_like(dvbuf)

        @pl.when(step > 0)
        def _():
            for buf in (dkbuf, dvbuf):
                for r in range(RES):
                    buf[2 * L * r + L:2 * L * (r + 1), :] = buf[2 * L * r:2 * L * r + L, :]
                    buf[2 * L * r:2 * L * r + L, :] = jnp.zeros((L, PAIR), F32)

        low = lax.broadcasted_iota(jnp.int32, (L, PAIR), 1) < HEAD_DIM
        low_k = lax.broadcasted_iota(jnp.int32, (2 * L, PAIR), 1) < HEAD_DIM
        low_t = lax.broadcasted_iota(jnp.int32, (PAIR, 2 * L), 0) < HEAD_DIM
        for bi, dil in enumerate(DILATIONS):
            first_branch = bi == 0

            def scores(res, n, dil=dil):
                q_off, k_off, kp_off, w = _offsets(dil, res, n)
                qb = _gather(q_ref, q_off, w).astype(BF16)
                dof = _gather(do_ref, q_off, w)
                prod = dof * _gather(o_ref, q_off, w)
                dob = dof.astype(BF16)
                lse_f = _gather(lse_ref, q_off, w)
                zero = jnp.zeros_like(prod)
                dsum = (jnp.sum(jnp.where(low, prod, zero), axis=1, keepdims=True),
                        jnp.sum(jnp.where(low, zero, prod), axis=1, keepdims=True))
                kcat = jnp.concatenate([_gather(kbuf, kp_off, w), _gather(kbuf, k_off, w)], axis=0).astype(BF16)
                vcat = jnp.concatenate([_gather(vbuf, kp_off, w), _gather(vbuf, k_off, w)], axis=0).astype(BF16)
                k_bd, v_bd = _two_heads(kcat, low_k), _two_heads(vcat, low_k)
                sc = lax.dot_general(qb, k_bd, NT, preferred_element_type=F32)
                dp = lax.dot_general(dob, v_bd, NT, preferred_element_type=F32)
                return qb, dob, lse_f, dsum, k_bd, sc, dp

            def gradients(res, n, given, bi=bi, dil=dil, first_branch=first_branch):
                qb, dob, lse_f, dsum, k_bd, sc, dp = given
                q_off, k_off, kp_off, w = _offsets(dil, res, n)
                bias_at = _bias_index(bi, sb, n)
                ps, dss = [], []
                for h in range(2):
                    cols = slice(2 * L * h, 2 * L * (h + 1))
                    lse_h = lse_f[:, HEAD_DIM * h:HEAD_DIM * h + 1]
                    p_h = jnp.exp(sc[:, cols] + bias_s[bias_at + h] - lse_h)
                    dss.append((p_h * (dp[:, cols] - dsum[h])).astype(BF16))
                    ps.append(p_h.astype(BF16))
                ds_cat, p_cat = jnp.concatenate(dss, axis=1), jnp.concatenate(ps, axis=1)
                dq = jnp.dot(ds_cat, k_bd, preferred_element_type=F32)
                dk_t = lax.dot_general(qb, ds_cat, TN, preferred_element_type=F32)
                dv_t = lax.dot_general(dob, p_cat, TN, preferred_element_type=F32)
                dk = jnp.where(low_t, dk_t[:, :2 * L], dk_t[:, 2 * L:]).T
                dv = jnp.where(low_t, dv_t[:, :2 * L], dv_t[:, 2 * L:]).T
                return [(dq_s, q_off, w, dq, not first_branch),
                        (dkbuf, kp_off, w, dk[:L], True), (dkbuf, k_off, w, dk[L:], True),
                        (dvbuf, kp_off, w, dv[:L], True), (dvbuf, k_off, w, dv[L:], True)]

            _loop_blocks(dil, [scores, gradients], BWD_TOGETHER)

        dq_ref[...] = dq_s[...].astype(BF16)
        for r in range(RES):
            rows, cur = slice(L * r, L * (r + 1)), slice(2 * L * r + L, 2 * L * (r + 1))
            for plane, buf in enumerate((dkbuf, dvbuf)):
                if has_in:
                    dkv_ref[plane, rows, :] = buf[cur, :] + dkv_in_ref[plane, rows, :]
                else:
                    dkv_ref[plane, rows, :] = buf[cur, :]

    rev = lambda i: ns - 1 - i
    prev = lambda i: jnp.maximum(ns - 2 - i, 0)
    blk = pl.BlockSpec((SUPER, PAIR), lambda g, i: (rev(i), g))
    in_specs = [pl.BlockSpec((None, 8, PAIR), lambda g, i: (g, 0, 0)), blk, blk, blk, blk,
                pl.BlockSpec((SUPER, PAIR), lambda g, i: (rev(i), g)),
                pl.BlockSpec((SUPER, PAIR), lambda g, i: (prev(i), g)),
                pl.BlockSpec((SUPER, PAIR), lambda g, i: (rev(i), g_n + g)),
                pl.BlockSpec((SUPER, PAIR), lambda g, i: (prev(i), g_n + g))]
    ins = [_slope_table(d), q, do, o, lse, kv, kv, kv, kv]
    planes = pl.BlockSpec((2, SUPER, PAIR), lambda g, i: (0, rev(i), g))
    if has_in:
        in_specs.append(planes)
        ins.append(dkv_in)
    res = pl.pallas_call(
        body, name=name, grid=(g_n, ns), in_specs=in_specs, out_specs=[blk, planes],
        out_shape=[jax.ShapeDtypeStruct((s, d), BF16), jax.ShapeDtypeStruct((2, s, d), F32)],
        scratch_shapes=([pltpu.VMEM((2 * SUPER, PAIR), F32)] * 4 + [pltpu.VMEM((SUPER, PAIR), F32)]
                        + [pltpu.VMEM((4 * len(DILATIONS), L, 2 * L), F32)]),
        compiler_params=_params(("parallel", "arbitrary")),
    )(*ins)
    return res[0], res[1]


def _coords():
    return lax.axis_index("x"), lax.axis_index("y"), lax.axis_index("c")


def _chip_peers(x, y):
    return [(1 - x, y), (x, 1 - y), (1 - x, 1 - y)]


def _block_of(ref, axis, blk, size):
    start = pl.multiple_of(blk * size, size)
    if axis == 1:
        return ref.at[:, pl.ds(start, size)]
    return ref.at[pl.ds(start, size), :]


ANY = pl.BlockSpec(memory_space=pl.ANY)


HBM = pl.BlockSpec(memory_space=pltpu.HBM)
SEM = pl.BlockSpec(memory_space=pltpu.SEMAPHORE)
SPLIT = pltpu.CompilerParams(has_side_effects=pltpu.SideEffectType.DATAFLOW_SIDE_EFFECTING)


def _in_hbm(a):
    return pltpu.with_memory_space_constraint(a, pltpu.HBM)


def _thru(arrays):
    return [pltpu.HBM(a.shape, a.dtype) for a in arrays]


def _cast_place(w, layer, ax, dtype, name):
    _, k, n = w.shape
    t = _pick(k, (512, 256, 128))
    nb = k // t

    def body(blk_ref, w_ref, b_ref, f_ref):
        v = w_ref[...].astype(dtype)
        b_ref[...] = v
        f_ref[...] = v

    full_shape = (k, 4 * n) if ax == 1 else (4 * k, n)
    place = (lambda i, blk: (i, blk[0])) if ax == 1 else (lambda i, blk: (blk[0] * nb + i, 0))
    return pl.pallas_call(
        body, name=name,
        grid_spec=pltpu.PrefetchScalarGridSpec(
            num_scalar_prefetch=1, grid=(nb,),
            in_specs=[pl.BlockSpec((None, t, n), lambda i, blk: (layer, i, 0))],
            out_specs=[pl.BlockSpec((t, n), lambda i, blk: (i, 0)), pl.BlockSpec((t, n), place)]),
        out_shape=[jax.ShapeDtypeStruct((k, n), dtype), jax.ShapeDtypeStruct(full_shape, dtype)],
        compiler_params=_params(("parallel",)),
    )(_my_block()[None], w)


def _my_block():
    return (2 * lax.axis_index("x") + lax.axis_index("y")).astype(jnp.int32)


def _gather_start(group, carry, name):
    n, nc = len(group), len(carry)

    def body(*refs):
        blocks, fulls, send_sem, recv_sem = refs[:n], refs[n:2 * n], refs[2 * n + nc], refs[2 * n + nc + 1]
        x, y, c = _coords()
        for t, (b, _, ax) in enumerate(group):
            mine = _block_of(fulls[t], ax, 2 * x + y, b.shape[ax])
            for j, (px, py) in enumerate(_chip_peers(x, y)):
                pltpu.make_async_remote_copy(
                    src_ref=blocks[t], dst_ref=mine, send_sem=send_sem.at[3 * t + j], recv_sem=recv_sem.at[3 * t + j],
                    device_id=(px, py, c), device_id_type=MESH).start()

    arrays = [b for b, _, _ in group] + [f for _, f, _ in group] + list(carry)
    sems = [pltpu.SemaphoreType.DMA((3 * n,))] * 2
    res = pl.pallas_call(
        body, name=name, in_specs=[HBM] * len(arrays), out_specs=[SEM, SEM] + [HBM] * len(arrays),
        out_shape=sems + _thru(arrays), input_output_aliases={i: 2 + i for i in range(len(arrays))},
        compiler_params=SPLIT,
    )(*[_in_hbm(a) for a in arrays])
    return (res[0], res[1], list(res[2:2 + n]), list(res[2 + n:2 + 2 * n])), list(res[2 + 2 * n:])


def _gather_wait(group, started, after, name):
    sends, recvs, blocks, fulls = started
    m = len(group)

    def body(*refs):
        blk_refs, full_refs, send_sem, recv_sem = refs[:m], refs[m:2 * m], refs[2 * m], refs[2 * m + 1]
        x, y, c = _coords()
        for t, (b, _, ax) in enumerate(group):
            for j, (px, py) in enumerate(_chip_peers(x, y)):
                cp = pltpu.make_async_remote_copy(
                    src_ref=blk_refs[t], dst_ref=_block_of(full_refs[t], ax, 2 * px + py, b.shape[ax]),
                    send_sem=send_sem.at[3 * t + j], recv_sem=recv_sem.at[3 * t + j],
                    device_id=(px, py, c), device_id_type=MESH)
                cp.wait_send()
                cp.wait_recv()

    extra = [] if after is None else [after]
    res = pl.pallas_call(
        body, name=name, in_specs=[HBM] * (2 * m) + [SEM, SEM] + [ANY] * len(extra), out_specs=[HBM] * (2 * m),
        out_shape=_thru(blocks) + _thru(fulls), input_output_aliases={i: i for i in range(2 * m)},
        compiler_params=SPLIT,
    )(*blocks, *fulls, sends, recvs, *extra)
    return list(res[m:])


def _scatter_start(grads, carry, name):
    n = len(grads)
    n_in = 2 * n + len(carry)

    def body(*refs):
        g_refs, st_refs, send_sem, recv_sem = refs[:n], refs[n:2 * n], refs[n_in], refs[n_in + 1]
        x, y, c = _coords()
        for t, (g, ax) in enumerate(grads):
            for j, (px, py) in enumerate(_chip_peers(x, y)):
                pltpu.make_async_remote_copy(
                    src_ref=_block_of(g_refs[t], ax, 2 * px + py, g.shape[ax] // 4), dst_ref=st_refs[t].at[j],
                    send_sem=send_sem.at[3 * t + j], recv_sem=recv_sem.at[3 * t + j],
                    device_id=(px, py, c), device_id_type=MESH).start()

    arrays = [g for g, _ in grads]
    for g, ax in grads:
        shape = list(g.shape)
        shape[ax] //= 4
        arrays.append(lax.empty((3, *shape), g.dtype))
    arrays += list(carry)
    sems = [pltpu.SemaphoreType.DMA((3 * n,))] * 2
    res = pl.pallas_call(
        body, name=name, in_specs=[HBM] * n_in, out_specs=[SEM, SEM] + [HBM] * n_in,
        out_shape=sems + _thru(arrays), input_output_aliases={i: 2 + i for i in range(n_in)},
        compiler_params=SPLIT,
    )(*[_in_hbm(a) for a in arrays])
    return (res[0], res[1], list(res[2:2 + n]), list(res[2 + n:2 + 2 * n])), list(res[2 + 2 * n:])


def _scatter_wait(axes, started, after, name):
    sends, recvs, full, stacks = started
    n = len(full)
    extra = [] if after is None else [after]

    def body(*refs):
        g_refs, st_refs, send_sem, recv_sem = refs[:n], refs[n:2 * n], refs[2 * n], refs[2 * n + 1]
        x, y, c = _coords()
        for t, ax in enumerate(axes):
            size = full[t].shape[ax] // 4
            for j, (px, py) in enumerate(_chip_peers(x, y)):
                cp = pltpu.make_async_remote_copy(
                    src_ref=_block_of(g_refs[t], ax, 2 * px + py, size), dst_ref=st_refs[t].at[j],
                    send_sem=send_sem.at[3 * t + j], recv_sem=recv_sem.at[3 * t + j],
                    device_id=(px, py, c), device_id_type=MESH)
                cp.wait_send()
                cp.wait_recv()

    res = pl.pallas_call(
        body, name=name, in_specs=[HBM] * (2 * n) + [SEM, SEM] + [ANY] * len(extra), out_specs=[HBM] * (2 * n),
        out_shape=_thru(full) + _thru(stacks), input_output_aliases={i: i for i in range(2 * n)},
        compiler_params=SPLIT,
    )(*full, *stacks, sends, recvs, *extra)
    return list(res[:n]), list(res[n:])


def _pair_copies(g_refs, st_refs, out_refs, items, send_sem, recv_sem):
    x, y, c = _coords()
    copies = []
    for u, (g, ax, _) in enumerate(items):
        own = _block_of(g_refs[u], ax, 2 * x + y, g.shape[ax] // 4)
        for k, (src, dst) in enumerate([(own, out_refs[u].at[0]), (st_refs[u], out_refs[u].at[pl.ds(1, 3)])]):
            copies.append(pltpu.make_async_remote_copy(
                src_ref=src, dst_ref=dst, send_sem=send_sem.at[2 * u + k], recv_sem=recv_sem.at[2 * u + k],
                device_id=(x, y, 1 - c), device_id_type=MESH))
    return copies


def _pair_start(items, carry, name):
    n = len(items)
    n_in = 3 * n + len(carry)

    def body(*refs):
        for cp in _pair_copies(refs[:n], refs[n:2 * n], refs[2 * n:3 * n], items, refs[n_in], refs[n_in + 1]):
            cp.start()

    arrays = ([g for g, _, _ in items] + [st for _, _, st in items]
              + [lax.empty((4, *st.shape[1:]), st.dtype) for _, _, st in items] + list(carry))
    sems = [pltpu.SemaphoreType.DMA((2 * n,))] * 2
    res = pl.pallas_call(
        body, name=name, in_specs=[HBM] * n_in, out_specs=[SEM, SEM] + [HBM] * n_in,
        out_shape=sems + _thru(arrays), input_output_aliases={i: 2 + i for i in range(n_in)},
        compiler_params=SPLIT,
    )(*[_in_hbm(a) for a in arrays])
    thru = res[2:]
    return (res[0], res[1], *(list(thru[k * n:(k + 1) * n]) for k in range(3))), list(thru[3 * n:])


def _pair_wait(axes, started, name):
    send, recv, full, stacks, landing = started
    n = len(full)
    items = [(full[u], axes[u], stacks[u]) for u in range(n)]

    def body(*refs):
        for cp in _pair_copies(refs[:n], refs[n:2 * n], refs[2 * n:3 * n], items, refs[3 * n], refs[3 * n + 1]):
            cp.wait_send()
            cp.wait_recv()

    res = pl.pallas_call(
        body, name=name, in_specs=[HBM] * (3 * n) + [SEM, SEM], out_specs=[HBM] * (3 * n),
        out_shape=_thru(full + stacks + landing), input_output_aliases={i: i for i in range(3 * n)},
        compiler_params=SPLIT,
    )(*full, *stacks, *landing, send, recv)
    return list(res[:n]), list(res[n:2 * n]), list(res[2 * n:])


def _allreduce_small(v, name):
    r, cdim = v.shape

    def body(v_ref, out_ref, buf, send_sems, recv_sems):
        x, y, c = _coords()
        me = 4 * x + 2 * y + c
        buf[0] = v_ref[...]
        sends = []
        for k in range(1, 8):
            peer = (x if not (k & 4) else 1 - x, y if not (k & 2) else 1 - y, c if not (k & 1) else 1 - c)
            cp = pltpu.make_async_remote_copy(
                src_ref=v_ref, dst_ref=buf.at[k], send_sem=send_sems.at[k - 1], recv_sem=recv_sems.at[k - 1],
                device_id=peer, device_id_type=MESH)
            cp.start()
            sends.append(cp)
        for cp in sends:
            cp.wait_recv()
        total = buf[me]
        for src in range(1, 8):
            total = total + buf[jnp.bitwise_xor(me, src)]
        out_ref[...] = total
        for cp in sends:
            cp.wait_send()

    return pl.pallas_call(
        body, name=name,
        in_specs=[pl.BlockSpec(memory_space=pltpu.VMEM)], out_specs=pl.BlockSpec(memory_space=pltpu.VMEM),
        out_shape=jax.ShapeDtypeStruct((r, cdim), F32),
        scratch_shapes=[pltpu.VMEM((8, r, cdim), F32), pltpu.SemaphoreType.DMA((7,)), pltpu.SemaphoreType.DMA((7,))],
        compiler_params=pltpu.CompilerParams(has_side_effects=True),
    )(v)


def _adamw_math(w, g, m, v):
    m = ADAM_B1 * m + (1.0 - ADAM_B1) * g
    v = ADAM_B2 * v + (1.0 - ADAM_B2) * jnp.square(g)
    m_hat = m / (1.0 - ADAM_B1 ** ADAM_STEP)
    v_hat = v / (1.0 - ADAM_B2 ** ADAM_STEP)
    delta = -ADAM_LR * (m_hat / (jnp.sqrt(v_hat) + ADAM_EPS) + ADAM_WD * w)
    return delta, m, v


def _adamw(w, m, v, grads, name):
    r, cdim = w.shape
    paired = isinstance(grads, list)
    layers = len(grads) if paired else 1
    t = _pick(r // layers, (128, 64, 32, 16, 8))
    per_layer = r // layers // t
    n_grad = 3 * layers if paired else 1

    def body(*refs):
        refs = refs[1:] if paired else refs
        w_ref, m_ref, v_ref = refs[:3]
        outs = refs[3 + n_grad:]

        def update(g):
            delta, m_new, v_new = _adamw_math(w_ref[...], g, m_ref[...], v_ref[...])
            outs[0][...] = g
            outs[1][...] = delta
            outs[2][...] = m_new
            outs[3][...] = v_new

        if not paired:
            update(refs[3][...])
            return
        layer = pl.program_id(0) // per_layer
        for l in range(layers):
            @pl.when(layer == l)
            def _(own_ref=refs[3 + 3 * l], st_ref=refs[4 + 3 * l], sib_ref=refs[5 + 3 * l]):
                sa = own_ref[...].astype(F32)
                sb = sib_ref[0].astype(F32)
                for k in range(3):
                    sa = sa + st_ref[k].astype(F32)
                    sb = sb + sib_ref[k + 1].astype(F32)
                update(sa + sb)

    out_shape = [jax.ShapeDtypeStruct((r, cdim), F32)] * 4
    if not paired:
        spec = pl.BlockSpec((t, cdim), lambda i: (i, 0))
        return pl.pallas_call(
            body, name=name, grid=(r // t,), in_specs=[spec] * 4, out_specs=[spec] * 4, out_shape=out_shape,
            compiler_params=_params(("parallel",)),
        )(w, m, v, grads)

    spec = pl.BlockSpec((t, cdim), lambda i, blk: (i, 0))
    ins, in_specs = [w, m, v], [spec] * 3
    for l, (g, ax, stack, sib) in enumerate(grads):
        row = lambda i, l=l: jnp.clip(i - l * per_layer, 0, per_layer - 1)
        own = ((lambda i, blk, row=row: (row(i), blk[0])) if ax == 1
               else (lambda i, blk, row=row: (blk[0] * per_layer + row(i), 0)))
        ins += [g, stack, sib]
        in_specs += [pl.BlockSpec((t, cdim), own),
                     pl.BlockSpec((3, t, cdim), lambda i, blk, row=row: (0, row(i), 0)),
                     pl.BlockSpec((4, t, cdim), lambda i, blk, row=row: (0, row(i), 0))]
    return pl.pallas_call(
        body, name=name,
        grid_spec=pltpu.PrefetchScalarGridSpec(
            num_scalar_prefetch=1, grid=(r // t,), in_specs=in_specs, out_specs=[spec] * 4),
        out_shape=out_shape, compiler_params=_params(("parallel",)),
    )(_my_block()[None], *ins)


def _local_step(x, target, gains, conv_ws, kv_gain, weights_of, send_grads):
    depth = len(gains)
    n_a = len(conv_ws)
    saved, ws = [], []
    kv = kvn = None
    _, (xn,) = _norm_res_fwd(x, None, None, [gains[0][0]], "norm_first")
    h = x
    for l in range(depth):
        g = gains[l]
        sv = {"x_in": h, "xn": xn}
        w = weights_of(l, "mix", h)
        ws.append(w)
        if l == n_a:
            kv = _mm(kvn, w["kv"], "nn", F32, "kv_fwd")
        if l < n_a:
            p = _mm(xn, w["conv_in"], "nn", BF16, f"conv_in_fwd_{l}")
            z = _conv_gate_fwd(p, conv_ws[l], f"conv_gate_fwd_{l}")
            mix, x1, xn2 = _mm_norm_res(z, w["conv_out"], h, g[1], g[2], f"conv_out_fwd_{l}")
            sv.update(p=p, z=z)
        else:
            j = l - n_a
            q = _mm(xn, w["q"], "nn", F32, f"q_fwd_{j}", scale=HEAD_DIM ** -0.5)
            o, lse = _attention_fwd(q, kv, f"attn_fwd_{j}")
            mix, x1, xn2 = _mm_norm_res(o, w["o"], h, g[1], g[2], f"o_fwd_{j}")
            sv.update(q=q, o=o, lse=lse)
        w.update(weights_of(l, "ffn", mix))
        f, a = _ffn_in_swiglu(xn2, w["ffn_in"], f"ffn_in_fwd_{l}")
        ff = _mm(a, w["ffn_out"], "nn", BF16, f"ffn_out_fwd_{l}")
        sv.update(mix=mix, x1=x1, xn2=xn2, f=f, a=a, ff=ff)
        saved.append(sv)
        if l == depth - 1:
            dx, loss = _norm_res_loss(x1, ff, g[3], target, "norm_loss")
        elif l == n_a - 1:
            h, _ = _norm_res_fwd(x1, ff, g[3], [], f"norm_end_{l}")
            h = _permute16(h, False, "permute_stream")
            _, (xn, kvn) = _norm_res_fwd(h, None, None, [gains[l + 1][0], kv_gain], "norm_permuted")
        else:
            h, (xn,) = _norm_res_fwd(x1, ff, g[3], [gains[l + 1][0]], f"norm_end_{l}")
    d_gains = [[None] * 4 for _ in range(depth)]
    d_conv = [None] * n_a
    d_kv_gain = None
    dkv = None
    _, _, dff, d_gains[depth - 1][3] = _norm_bwd(dx, [], None, (saved[-1]["ff"], gains[-1][3]), "norm_loss_bwd")
    for l in reversed(range(depth)):
        sv, g, w, grads = saved[l], gains[l], ws[l], {}
        grads["ffn_out"] =_mm(sv["a"], dff, "tn", BF16, f"ffn_out_dw_{l}")
        df = _ffn_out_dx_swiglu(dff, w["ffn_out"], sv["f"], f"ffn_out_dx_{l}")
        dxn2 = _mm(df, w["ffn_in"], "nt", BF16, f"ffn_in_dx_{l}")
        grads["ffn_in"] =_mm(sv["xn2"], df, "tn", BF16, f"ffn_in_dw_{l}")
        dx, (d_gains[l][2],), dmix, d_gains[l][1] = _norm_bwd(
            dx, [(dxn2, g[2])], sv["x1"], (sv["mix"], g[1]), f"norm_mid_bwd_{l}")
        dx, dmix = send_grads(l, "ffn", grads, [dx, dmix])
        if l < n_a:
            dz = _mm(dmix, w["conv_out"], "nt", BF16, f"conv_out_dx_{l}")
            grads["conv_out"] =_mm(sv["z"], dmix, "tn", BF16, f"conv_out_dw_{l}")
            dp, d_conv[l] = _conv_gate_bwd(sv["p"], dz, conv_ws[l], f"conv_gate_bwd_{l}")
            dxn = _mm(dp, w["conv_in"], "nt", BF16, f"conv_in_dx_{l}")
            grads["conv_in"] =_mm(sv["xn"], dp, "tn", BF16, f"conv_in_dw_{l}")
        else:
            j = l - n_a
            do = _mm(dmix, w["o"], "nt", F32, f"o_dx_{j}")
            grads["o"] =_mm(sv["o"], dmix, "tn", BF16, f"o_dw_{j}")
            dq, dkv = _attention_bwd(sv["q"], kv, sv["o"], do, sv["lse"], dkv, f"attn_bwd_{j}")
            scale = HEAD_DIM ** -0.5
            dxn = _mm(dq, w["q"], "nt", BF16, f"q_dx_{j}", scale=scale)
            grads["q"] =_mm(sv["xn"], dq, "tn", BF16, f"q_dw_{j}", scale=scale)
        branches = [(dxn, g[0])]
        if l == n_a:
            dkvn = _mm(dkv, w["kv"], "nt", BF16, "kv_dx")
            grads["kv"] =_mm(kvn, dkv, "tn", BF16, "kv_dw")
            branches.append((dkvn, kv_gain))
        post = (saved[l - 1]["ff"], gains[l - 1][3]) if l > 0 else None
        if l == n_a:
            dx, dgs, _, _ = _norm_bwd(dx, branches, sv["x_in"], None, f"norm_end_bwd_{l}")
            dx = _permute16(dx, True, "unpermute_stream")
            _, _, dff, dg_post = _norm_bwd(dx, [], None, post, "norm_boundary_bwd")
        else:
            dx, dgs, dff, dg_post = _norm_bwd(dx, branches, sv["x_in"], post, f"norm_end_bwd_{l}")
        if dff is None:
            send_grads(l, "mix", grads, [])
        else:
            dx, dff = send_grads(l, "mix", grads, [dx, dff])
        d_gains[l][0] = dgs[0]
        if l == n_a:
            d_kv_gain = dgs[1]
        if l > 0:
            d_gains[l - 1][3] = dg_post
    return loss, dx, d_gains, d_conv, d_kv_gain


BIG = (
    ("conv_in", 1), ("conv_out", 0), ("kv", 1), ("q", 0), ("o", 0), ("ffn_in", 1), ("ffn_out", 0))


def kernel(x, norm_g, conv_in_w, conv_w, conv_out_w, kv_norm_g, kv_w, q_w, o_w, ffn_in_w, ffn_out_w, loss_target, m_norm_g, m_conv_in_w, m_conv_w, m_conv_out_w, m_kv_norm_g, m_kv_w, m_q_w, m_o_w, m_ffn_in_w, m_ffn_out_w, v_norm_g, v_conv_in_w, v_conv_w, v_conv_out_w, v_kv_norm_g, v_kv_w, v_q_w, v_o_w, v_ffn_in_w, v_ffn_out_w):
    depth, _, dq = norm_g.shape
    d = 4 * dq
    n_a = conv_w.shape[0]
    big_w = {"conv_in": conv_in_w, "conv_out": conv_out_w, "kv": kv_w[None], "q": q_w, "o": o_w,
             "ffn_in": ffn_in_w, "ffn_out": ffn_out_w}
    big_m = {"conv_in": m_conv_in_w, "conv_out": m_conv_out_w, "kv": m_kv_w[None], "q": m_q_w, "o": m_o_w,
             "ffn_in": m_ffn_in_w, "ffn_out": m_ffn_out_w}
    big_v = {"conv_in": v_conv_in_w, "conv_out": v_conv_out_w, "kv": v_kv_w[None], "q": v_q_w, "o": v_o_w,
             "ffn_in": v_ffn_in_w, "ffn_out": v_ffn_out_w}

    n_gain, n_tap = depth * 4, n_a * conv_w.shape[1]
    small_rows = -(-(n_gain + n_tap + 1) // 8) * 8
    pad_rows = small_rows - n_gain - n_tap

    def pack_small(gains, taps):
        return jnp.concatenate([gains.reshape(n_gain, dq), taps.reshape(n_tap, dq), jnp.zeros((pad_rows, dq), F32)])

    axis_of = dict(BIG)

    def matrices_of(l, part):
        if part == "ffn":
            return [("ffn_in", l), ("ffn_out", l)]
        if l < n_a:
            return [("conv_in", l), ("conv_out", l)]
        return ([("kv", 0)] if l == n_a else []) + [("q", l - n_a), ("o", l - n_a)]

    halves = [(l, part) for l in range(depth) for part in ("mix", "ffn")]
    groups = {(l, part): [(*_cast_place(big_w[name], i, axis_of[name], BF16, f"place_{name}_{i}"), axis_of[name])
                          for name, i in matrices_of(l, part)] for l, part in halves}
    groups[halves[0]].append((*_cast_place(pack_small(norm_g, conv_w)[None], 0, 1, F32, "place_small"), 1))
    started = {halves[0]: _gather_start(groups[halves[0]], [], "gather_start_0_mix")[0]}

    def fetch(half, after):
        full = _gather_wait(groups[half], started[half], after, "gather_wait_%d_%s" % half)
        nxt = halves.index(half) + 1
        if nxt < len(halves):
            started[halves[nxt]], full = _gather_start(groups[halves[nxt]], full, "gather_start_%d_%s" % halves[nxt])
        return full

    target = _permute16(loss_target.reshape(x.shape[1:]), False, "permute_target")
    first = fetch(halves[0], target)
    small = first[-1]
    gains = [[small[4 * l + i][None] for i in range(4)] for l in range(depth)]
    conv_ws = [small[n_gain + 3 * l:n_gain + 3 * l + 3] for l in range(n_a)]
    kv_gain = kv_norm_g[None]

    def weights_of(l, part, after):
        full = first if (l, part) == halves[0] else fetch((l, part), after)
        return {name: full[t] for t, (name, _) in enumerate(matrices_of(l, part))}

    sent, paired = {}, {}
    LAG = 2

    def to_sibling(half, carry, after):
        axes = [axis_of[name] for name, _ in matrices_of(*half)]
        full, stacks = _scatter_wait(axes, sent[half], after, "scatter_wait_%d_%s" % half)
        paired[half], carry = _pair_start(list(zip(full, axes, stacks)), carry, "pair_start_%d_%s" % half)
        return carry

    def send_grads(l, part, grads, carry):
        sent[l, part], carry = _scatter_start(
            [(grads[name], axis_of[name]) for name, _ in matrices_of(l, part)], carry, f"scatter_start_{l}_{part}")
        older = halves.index((l, part)) + LAG
        if older < len(halves) and carry:
            carry = to_sibling(halves[older], carry, carry[0])
        return carry

    loss, dx, d_gains, d_conv, d_kv_gain = _local_step(
        x.reshape(x.shape[1:]), target, gains, conv_ws, kv_gain, weights_of, send_grads)
    loss = lax.psum(loss, ("x", "y", "c"))

    small_g = jnp.concatenate([dg for row in d_gains for dg in row] + list(d_conv) + [d_kv_gain]
                              + [jnp.zeros((pad_rows - 1, d), F32)])
    small_g = _allreduce_small(small_g, "allreduce_small")
    blk = 2 * lax.axis_index("x") + lax.axis_index("y")
    mine_small = lax.dynamic_slice_in_dim(small_g, blk * dq, dq, axis=1)
    kv_rows = d // dq

    def pack_opt(gains_like, taps_like, kv_like):
        rows = jnp.concatenate([gains_like.reshape(n_gain, dq), taps_like.reshape(n_tap, dq), kv_like.reshape(kv_rows, dq)])
        extra = -rows.shape[0] % 8
        return jnp.concatenate([rows, jnp.zeros((extra, dq), F32)]) if extra else rows

    sw = pack_opt(norm_g, conv_w, kv_norm_g)
    sm = pack_opt(m_norm_g, m_conv_w, m_kv_norm_g)
    sv = pack_opt(v_norm_g, v_conv_w, v_kv_norm_g)
    sg = pack_opt(mine_small[:n_gain], mine_small[n_gain:n_gain + n_tap], small_g[n_gain + n_tap])
    s_out = _adamw(sw, sm, sv, sg, "adamw_small")

    def unpack(a):
        return (a[:n_gain].reshape(depth, 4, dq), a[n_gain:n_gain + n_tap].reshape(n_a, -1, dq),
                a[n_gain + n_tap:n_gain + n_tap + kv_rows].reshape(d))

    small_out = [unpack(a) for a in s_out]

    landed, big_out = {}, {}

    def update(name):
        for half in halves:
            if matrices_of(*half)[0] not in landed and any(n == name for n, _ in matrices_of(*half)):
                axes = [axis_of[n] for n, _ in matrices_of(*half)]
                landed.update(zip(matrices_of(*half), zip(*_pair_wait(axes, paired[half], "pair_wait_%d_%s" % half))))
        shp, ax = big_w[name].shape, axis_of[name]
        rows, cols = shp[0] * shp[1], shp[2]
        flat = lambda a: a.reshape(rows, cols)
        full, stacks, sibling = zip(*[landed[name, i] for i in range(shp[0])])
        res = _adamw(flat(big_w[name]), flat(big_m[name]), flat(big_v[name]),
                     [(full[i], ax, stacks[i], sibling[i]) for i in range(shp[0])], f"adamw_{name}")
        big_out[name] = [a.reshape(shp[1:] if name == "kv" else shp) for a in res]

    pending = [half for half in reversed(halves) if half not in paired]
    for half in pending[:-1]:
        to_sibling(half, [], None)
    late = [name for name, _ in BIG if any(n == name for n, _ in matrices_of(*pending[-1]))]
    for name, _ in BIG:
        if name not in late:
            update(name)
    to_sibling(pending[-1], [], big_out["ffn_out"][0])
    for name in late:
        update(name)

    def leaves(i):
        ng, cw_, kg = small_out[i]
        return [ng, big_out["conv_in"][i], cw_, big_out["conv_out"][i], kg, big_out["kv"][i], big_out["q"][i],
                big_out["o"][i], big_out["ffn_in"][i], big_out["ffn_out"][i]]

    return (loss, dx.reshape(x.shape), *leaves(0), *leaves(1), *leaves(2), *leaves(3))
```

```python
import jax
import jax.numpy as jnp
import numpy as np
from jax import lax
from jax.experimental import pallas as pl
from jax.experimental.pallas import tpu as pltpu

F32 = jnp.float32
BF16 = jnp.bfloat16
HEAD_DIM = 64
DILATIONS = (1, 4, 16)
NORM_EPS = 1e-6
NEG_BIG = -1e30
VMEM_LIMIT = 48 * 1024 * 1024
ROW_TILE = 256
NORM_TILE = 512
LANE = 128
MESH = pl.DeviceIdType.MESH

ADAM_LR = 0.001
ADAM_B1 = 0.9
ADAM_B2 = 0.999
ADAM_EPS = 1e-08
ADAM_WD = 0.01
ADAM_STEP = 10

TILE_CANDIDATES = (1024, 1408, 768, 512, 384, 256, 128)


def _pick(dim, cands=TILE_CANDIDATES):
    for c in cands:
        if c <= dim and dim % c == 0:
            return c
    return dim


def _params(sem):
    return pltpu.CompilerParams(dimension_semantics=sem, vmem_limit_bytes=VMEM_LIMIT)


def _mm(a, b, mode, out_dtype, name, scale=None):
    a_planes = a.shape[0] if a.ndim == 3 else 1
    b_planes = b.shape[0] if b.ndim == 3 else 1
    if mode == "nn":
        m, k = a.shape[-2], a.shape[-1] * a_planes
        n = b.shape[1]
    elif mode == "nt":
        m, k = a.shape[-2], a.shape[-1] * a_planes
        n = b.shape[0]
    else:
        k, m = a.shape
        n = b.shape[-1] * b_planes
    tm, tn = _pick(m), _pick(n // b_planes)
    tk = _pick(k // a_planes, ((2048,) if mode == "tn" else (3072, 2816)) + TILE_CANDIDATES)
    nk = k // tk
    ka, nb = k // a_planes // tk, n // b_planes // tn
    if a_planes > 1:
        a_spec = pl.BlockSpec((None, tm, tk), lambda i, j, kk: (kk // ka, i, kk % ka))
    elif mode == "tn":
        a_spec = pl.BlockSpec((tk, tm), lambda i, j, kk: (kk, i))
    else:
        a_spec = pl.BlockSpec((tm, tk), lambda i, j, kk: (i, kk))
    if mode == "nn":
        b_spec = pl.BlockSpec((tk, tn), lambda i, j, kk: (kk, j))
        dims = (((1,), (0,)), ((), ()))
    elif mode == "nt":
        b_spec = pl.BlockSpec((tn, tk), lambda i, j, kk: (j, kk))
        dims = (((1,), (1,)), ((), ()))
    else:
        b_spec = (pl.BlockSpec((None, tk, tn), lambda i, j, kk: (j // nb, kk, j % nb)) if b_planes > 1
                  else pl.BlockSpec((tk, tn), lambda i, j, kk: (kk, j)))
        dims = (((0,), (0,)), ((), ()))

    def finish(acc):
        if scale is not None:
            acc = acc * scale
        return acc.astype(out_dtype)

    if nk == 1:
        def body(a_ref, b_ref, o_ref):
            o_ref[...] = finish(lax.dot_general(a_ref[...].astype(BF16), b_ref[...].astype(BF16), dims, preferred_element_type=F32))
        scratch = []
    else:
        def body(a_ref, b_ref, o_ref, acc_ref):
            kk = pl.program_id(2)

            @pl.when(kk == 0)
            def _():
                acc_ref[...] = jnp.zeros_like(acc_ref)

            acc_ref[...] += lax.dot_general(a_ref[...].astype(BF16), b_ref[...].astype(BF16), dims, preferred_element_type=F32)

            @pl.when(kk == nk - 1)
            def _():
                o_ref[...] = finish(acc_ref[...])
        scratch = [pltpu.VMEM((tm, tn), F32)]

    return pl.pallas_call(
        body, name=name,
        grid=(m // tm, n // tn, nk),
        in_specs=[a_spec, b_spec],
        out_specs=pl.BlockSpec((tm, tn), lambda i, j, kk: (i, j)),
        out_shape=jax.ShapeDtypeStruct((m, n), out_dtype),
        scratch_shapes=scratch,
        compiler_params=_params(("parallel", "parallel", "arbitrary")),
    )(a, b)


def _rstd(v):
    return lax.rsqrt(jnp.mean(v * v, axis=-1, keepdims=True) + NORM_EPS)


def _rms_bwd(dy, v, g, r):
    gy = dy * g
    dv = r * (gy - v * (r * r) * jnp.mean(gy * v, axis=-1, keepdims=True))
    return dv, dy * v * r


def _row_spec(t, width):
    return pl.BlockSpec((t, width), lambda i: (i, 0))


def _gain_spec(width):
    return pl.BlockSpec((1, width), lambda i: (0, 0))


def _norm_res_fwd(x, mix, g_post, pre_gains, name):
    s, d = x.shape
    t = _pick(s, (NORM_TILE, ROW_TILE))
    has_mix = mix is not None
    n_pre = len(pre_gains)

    def body(*refs):
        x_ref = refs[0]
        pos = 1
        x1 = x_ref[...]
        if has_mix:
            mv = refs[1][...].astype(F32)
            x1 = x1 + mv * _rstd(mv) * refs[2][...]
            pos = 3
        gains = refs[pos:pos + n_pre]
        outs = refs[pos + n_pre:]
        if has_mix:
            outs[0][...] = x1
            outs = outs[1:]
        r = _rstd(x1)
        for g_ref, o_ref in zip(gains, outs):
            o_ref[...] = (x1 * r * g_ref[...]).astype(BF16)

    ins = [x] + ([mix, g_post] if has_mix else []) + list(pre_gains)
    in_specs = [_row_spec(t, d)] + ([_row_spec(t, d), _gain_spec(d)] if has_mix else []) + [_gain_spec(d)] * n_pre
    out_shape = ([jax.ShapeDtypeStruct((s, d), F32)] if has_mix else []) + [jax.ShapeDtypeStruct((s, d), BF16)] * n_pre
    out_specs = [_row_spec(t, d)] * len(out_shape)
    res = pl.pallas_call(
        body, name=name, grid=(s // t,), in_specs=in_specs, out_specs=out_specs, out_shape=out_shape,
        compiler_params=_params(("parallel",)),
    )(*ins)
    if has_mix:
        return res[0], list(res[1:])
    return x, list(res)


def _mm_norm_res(a, b, x, g_post, pre_gain, name):
    m, k = a.shape
    d = b.shape[1]
    tm, tk = _pick(m, (1024, 512, 256)), _pick(k)
    nk = k // tk

    def body(a_ref, b_ref, x_ref, gp_ref, g_ref, mix_ref, x1_ref, xn_ref, acc_ref):
        kk = pl.program_id(1)

        @pl.when(kk == 0)
        def _():
            acc_ref[...] = jnp.zeros_like(acc_ref)

        acc_ref[...] += jnp.dot(a_ref[...].astype(BF16), b_ref[...], preferred_element_type=F32)

        @pl.when(kk == nk - 1)
        def _():
            mix = acc_ref[...].astype(BF16)
            mix_ref[...] = mix
            mv = mix.astype(F32)
            x1 = x_ref[...] + mv * _rstd(mv) * gp_ref[...]
            x1_ref[...] = x1
            xn_ref[...] = (x1 * _rstd(x1) * g_ref[...]).astype(BF16)

    rows = pl.BlockSpec((tm, d), lambda i, kk: (i, 0))
    gain = pl.BlockSpec((1, d), lambda i, kk: (0, 0))
    return pl.pallas_call(
        body, name=name, grid=(m // tm, nk),
        in_specs=[pl.BlockSpec((tm, tk), lambda i, kk: (i, kk)), pl.BlockSpec((tk, d), lambda i, kk: (kk, 0)),
                  rows, gain, gain],
        out_specs=[rows, rows, rows],
        out_shape=[jax.ShapeDtypeStruct((m, d), BF16), jax.ShapeDtypeStruct((m, d), F32),
                   jax.ShapeDtypeStruct((m, d), BF16)],
        scratch_shapes=[pltpu.VMEM((tm, d), F32)],
        compiler_params=_params(("parallel", "arbitrary")),
    )(a, b, x, g_post, pre_gain)


def _norm_res_loss(x, mix, g_post, target, name):
    s, d = x.shape
    t = _pick(s, (NORM_TILE, ROW_TILE))

    def body(x_ref, m_ref, g_ref, t_ref, dy_ref, loss_ref):
        mv = m_ref[...].astype(F32)
        y = x_ref[...] + mv * _rstd(mv) * g_ref[...]
        err = y - t_ref[...]
        dy_ref[...] = err * (1.0 / d)

        @pl.when(pl.program_id(0) == 0)
        def _():
            loss_ref[...] = jnp.zeros_like(loss_ref)

        loss_ref[...] += jnp.sum(err * err)

    dy, acc = pl.pallas_call(
        body, name=name, grid=(s // t,),
        in_specs=[_row_spec(t, d), _row_spec(t, d), _gain_spec(d), _row_spec(t, d)],
        out_specs=[_row_spec(t, d), pl.BlockSpec((8, LANE), lambda i: (0, 0))],
        out_shape=[jax.ShapeDtypeStruct((s, d), F32), jax.ShapeDtypeStruct((8, LANE), F32)],
        compiler_params=_params(("arbitrary",)),
    )(x, mix, g_post, target)
    return dy, acc[0, 0] * (0.5 / d)


def _norm_bwd(dx_out, branches, x_in, post, name):
    s, d = dx_out.shape
    t = _pick(s, (NORM_TILE, ROW_TILE))
    nb = len(branches)
    has_post = post is not None

    def body(*refs):
        dx_ref = refs[0]
        pos = 1
        dx = dx_ref[...]
        first = pl.program_id(0) == 0
        n_in = 1 + (1 + 2 * nb if nb else 0) + (2 if has_post else 0)
        outs = refs[n_in:]
        opos = 0
        if nb:
            xv = refs[pos][...]
            pos += 1
            r = _rstd(xv)
            dx_o = outs[0]
            opos = 1
            for _ in range(nb):
                dxn = refs[pos][...].astype(F32)
                g = refs[pos + 1][...]
                pos += 2
                dv, dg_rows = _rms_bwd(dxn, xv, g, r)
                dx = dx + dv
                dg_ref = outs[opos]
                opos += 1

                @pl.when(first)
                def _(dg_ref=dg_ref):
                    dg_ref[...] = jnp.zeros_like(dg_ref)

                dg_ref[...] += jnp.sum(dg_rows, axis=0, keepdims=True)
            dx_o[...] = dx
        if has_post:
            mv = refs[pos][...].astype(F32)
            g = refs[pos + 1][...]
            dm, dg_rows = _rms_bwd(dx, mv, g, _rstd(mv))
            outs[opos][...] = dm.astype(BF16)
            dg_ref = outs[opos + 1]

            @pl.when(first)
            def _():
                dg_ref[...] = jnp.zeros_like(dg_ref)

            dg_ref[...] += jnp.sum(dg_rows, axis=0, keepdims=True)

    ins, in_specs = [dx_out], [_row_spec(t, d)]
    out_shape, out_specs = [], []
    if nb:
        ins.append(x_in)
        in_specs.append(_row_spec(t, d))
        out_shape.append(jax.ShapeDtypeStruct((s, d), F32))
        out_specs.append(_row_spec(t, d))
        for dxn, g in branches:
            ins += [dxn, g]
            in_specs += [_row_spec(t, d), _gain_spec(d)]
            out_shape.append(jax.ShapeDtypeStruct((1, d), F32))
            out_specs.append(_gain_spec(d))
    if has_post:
        ins += [post[0], post[1]]
        in_specs += [_row_spec(t, d), _gain_spec(d)]
        out_shape += [jax.ShapeDtypeStruct((s, d), BF16), jax.ShapeDtypeStruct((1, d), F32)]
        out_specs += [_row_spec(t, d), _gain_spec(d)]
    res = pl.pallas_call(
        body, name=name, grid=(s // t,), in_specs=in_specs, out_specs=out_specs, out_shape=out_shape,
        compiler_params=_params(("arbitrary",)),
    )(*ins)
    res = list(res)
    dx_in = res.pop(0) if nb else dx_out
    dgs = [res.pop(0) for _ in range(nb)]
    dm, dg_post = (res[0], res[1]) if has_post else (None, None)
    return dx_in, dgs, dm, dg_post


HALO = 16


def _shift_down(u, prev, k):
    rows = lax.broadcasted_iota(jnp.int32, u.shape, 0)
    out = pltpu.roll(u, k, 0)
    for i in range(k):
        out = jnp.where(rows == i, prev[HALO - k + i:HALO - k + i + 1, :], out)
    return out


def _shift_up(u, nxt, k):
    n = u.shape[0]
    rows = lax.broadcasted_iota(jnp.int32, u.shape, 0)
    out = pltpu.roll(u, n - k, 0)
    for i in range(k):
        out = jnp.where(rows == n - k + i, nxt[i:i + 1, :], out)
    return out


def _conv_gate_fwd(p, cw, name):
    s, d3 = p.shape
    d = d3 // 3
    t = _pick(s, (ROW_TILE,))
    hb = t // HALO

    def body(p_ref, prev_ref, w_ref, z_ref):
        i = pl.program_id(0)
        pv = p_ref[...].astype(F32)
        b, u = pv[:, :d], pv[:, d:2 * d] * pv[:, 2 * d:]
        ph = prev_ref[...].astype(F32)
        up = jnp.where(i > 0, ph[:, d:2 * d] * ph[:, 2 * d:], 0.0)
        w = w_ref[...]
        y = w[0:1, :] * _shift_down(u, up, 2) + w[1:2, :] * _shift_down(u, up, 1) + w[2:3, :] * u
        z_ref[...] = (b * y).astype(BF16)

    return pl.pallas_call(
        body, name=name, grid=(s // t,),
        in_specs=[_row_spec(t, d3),
                  pl.BlockSpec((HALO, d3), lambda i: (jnp.maximum(i * hb - 1, 0), 0)),
                  pl.BlockSpec((3, d), lambda i: (0, 0))],
        out_specs=_row_spec(t, d),
        out_shape=jax.ShapeDtypeStruct((s, d), BF16),
        compiler_params=_params(("parallel",)),
    )(p, p, cw)


def _conv_gate_bwd(p, dz, cw, name):
    s, d3 = p.shape
    d = d3 // 3
    t = _pick(s, (ROW_TILE,))
    hb = t // HALO
    nt = s // t
    last_halo = s // HALO - 1

    def body(p_ref, prev_ref, next_ref, dz_ref, dznext_ref, w_ref, dp_ref, dw_ref):
        i = pl.program_id(0)
        pv = p_ref[...].astype(F32)
        b, c, h = pv[:, :d], pv[:, d:2 * d], pv[:, 2 * d:]
        u = c * h
        ph = prev_ref[...].astype(F32)
        up = jnp.where(i > 0, ph[:, d:2 * d] * ph[:, 2 * d:], 0.0)
        w = w_ref[...]
        u1, u2 = _shift_down(u, up, 1), _shift_down(u, up, 2)
        y = w[0:1, :] * u2 + w[1:2, :] * u1 + w[2:3, :] * u
        dz = dz_ref[...].astype(F32)
        dy = dz * b
        dyn = jnp.where(i < nt - 1, dznext_ref[...].astype(F32) * next_ref[...].astype(F32)[:, :d], 0.0)
        du = w[2:3, :] * dy + w[1:2, :] * _shift_up(dy, dyn, 1) + w[0:1, :] * _shift_up(dy, dyn, 2)
        dp_ref[:, :d] = (dz * y).astype(BF16)
        dp_ref[:, d:2 * d] = (du * h).astype(BF16)
        dp_ref[:, 2 * d:] = (du * c).astype(BF16)

        @pl.when(i == 0)
        def _():
            dw_ref[...] = jnp.zeros_like(dw_ref)

        dw_ref[0:1, :] += jnp.sum(dy * u2, axis=0, keepdims=True)
        dw_ref[1:2, :] += jnp.sum(dy * u1, axis=0, keepdims=True)
        dw_ref[2:3, :] += jnp.sum(dy * u, axis=0, keepdims=True)

    return pl.pallas_call(
        body, name=name, grid=(nt,),
        in_specs=[_row_spec(t, d3),
                  pl.BlockSpec((HALO, d3), lambda i: (jnp.maximum(i * hb - 1, 0), 0)),
                  pl.BlockSpec((HALO, d3), lambda i: (jnp.minimum((i + 1) * hb, last_halo), 0)),
                  _row_spec(t, d),
                  pl.BlockSpec((HALO, d), lambda i: (jnp.minimum((i + 1) * hb, last_halo), 0)),
                  pl.BlockSpec((3, d), lambda i: (0, 0))],
        out_specs=[_row_spec(t, d3), pl.BlockSpec((3, d), lambda i: (0, 0))],
        out_shape=[jax.ShapeDtypeStruct((s, d3), BF16), jax.ShapeDtypeStruct((3, d), F32)],
        compiler_params=_params(("arbitrary",)),
    )(p, p, p, dz, dz, cw)


FFN_ROWS, FFN_COLS = (512, 256), (1408, 768, 256, 128)


def _row_chunks(tm, rows=256):
    return [slice(r, r + min(rows, tm)) for r in range(0, tm, min(rows, tm))]


def _ffn_in_swiglu(xn, w_in, name):
    s, k = xn.shape
    ff = w_in.shape[1] // 2
    tm, tn = _pick(s, FFN_ROWS), _pick(ff, FFN_COLS)
    nj = ff // tn

    def body(x_ref, wg_ref, wu_ref, f_ref, a_ref):
        for rows in _row_chunks(tm):
            xv = x_ref[rows, :]
            gate = jnp.dot(xv, wg_ref[...], preferred_element_type=F32)
            up = jnp.dot(xv, wu_ref[...], preferred_element_type=F32)
            f_ref[0, rows, :] = gate.astype(BF16)
            f_ref[1, rows, :] = up.astype(BF16)
            a_ref[rows, :] = (gate * jax.nn.sigmoid(gate) * up).astype(BF16)

    return pl.pallas_call(
        body, name=name, grid=(nj, s // tm),
        in_specs=[pl.BlockSpec((tm, k), lambda j, i: (i, 0)),
                  pl.BlockSpec((k, tn), lambda j, i: (0, j)),
                  pl.BlockSpec((k, tn), lambda j, i: (0, nj + j))],
        out_specs=[pl.BlockSpec((2, tm, tn), lambda j, i: (0, i, j)), pl.BlockSpec((tm, tn), lambda j, i: (i, j))],
        out_shape=[jax.ShapeDtypeStruct((2, s, ff), BF16), jax.ShapeDtypeStruct((s, ff), BF16)],
        compiler_params=_params(("parallel", "parallel")),
    )(xn, w_in, w_in)


def _ffn_out_dx_swiglu(dff, w_out, f, name):
    s, d = dff.shape
    ff = w_out.shape[0]
    tm, tn = _pick(s, FFN_ROWS), _pick(ff, FFN_COLS)

    def body(d_ref, w_ref, f_ref, df_ref):
        for rows in _row_chunks(tm):
            da = lax.dot_general(d_ref[rows, :], w_ref[...], (((1,), (1,)), ((), ())), preferred_element_type=F32)
            gate = f_ref[0, rows, :].astype(F32)
            up = f_ref[1, rows, :].astype(F32)
            sg = jax.nn.sigmoid(gate)
            silu = gate * sg
            df_ref[0, rows, :] = (da * up * (sg + silu * (1.0 - sg))).astype(BF16)
            df_ref[1, rows, :] = (da * silu).astype(BF16)

    planes = pl.BlockSpec((2, tm, tn), lambda j, i: (0, i, j))
    return pl.pallas_call(
        body, name=name, grid=(ff // tn, s // tm),
        in_specs=[pl.BlockSpec((tm, d), lambda j, i: (i, 0)), pl.BlockSpec((tn, d), lambda j, i: (j, 0)), planes],
        out_specs=planes, out_shape=jax.ShapeDtypeStruct((2, s, ff), BF16),
        compiler_params=_params(("parallel", "parallel")),
    )(dff, w_out, f)


SUPER = 2048
RES = 16
PAIR = 128
L = 128
FWD_TOGETHER = 16
BWD_TOGETHER = 4


def _alibi_slopes(n_heads):
    h = np.arange(n_heads, dtype=np.float32) + 1.0
    return np.power(2.0, -8.0 * h / n_heads).astype(np.float32)


def _permute16(x, inverse, name):
    s, d = x.shape
    cw = LANE

    def body(x_ref, o_ref):
        if inverse:
            for m in range(L):
                o_ref[RES * m:RES * (m + 1), :] = x_ref[pl.ds(m, RES, stride=L), :]
        else:
            for r in range(RES):
                o_ref[L * r:L * (r + 1), :] = x_ref[pl.ds(r, L, stride=RES), :]

    spec = pl.BlockSpec((SUPER, cw), lambda i, j: (i, j))
    return pl.pallas_call(
        body, name=name, grid=(s // SUPER, d // cw), in_specs=[spec], out_specs=spec,
        out_shape=jax.ShapeDtypeStruct((s, d), x.dtype),
        compiler_params=_params(("parallel", "parallel")),
    )(x)


def _slope_table(d):
    nh = d // HEAD_DIM
    sl = _alibi_slopes(nh)
    tab = np.repeat(sl, HEAD_DIM).reshape(d // PAIR, 1, PAIR)
    return jnp.asarray(np.broadcast_to(tab, (d // PAIR, 8, PAIR)).copy())


def _geometry(dil):
    nch = RES // dil
    return nch, L // nch


def _band(dil):
    nch, w = _geometry(dil)
    sh = w.bit_length() - 1
    i = lax.broadcasted_iota(jnp.int32, (L, 2 * L), 0)
    j = lax.broadcasted_iota(jnp.int32, (L, 2 * L), 1)

    def pos(t):
        return jnp.bitwise_and(t, w - 1) * nch + jnp.right_shift(t, sh)

    delta = pos(i) + L - (pos(jnp.bitwise_and(j, L - 1)) + jnp.bitwise_and(j, L))
    return (delta * dil).astype(F32), (delta >= 0) & (delta <= L), j < L


def _fill_bias(bias_s, sl_ref):
    for b, dil in enumerate(DILATIONS):
        base, band, prev_half = _band(dil)
        for first in range(2):
            valid = band & jnp.logical_not(prev_half) if first else band
            for h in range(2):
                slope = sl_ref[0:1, HEAD_DIM * h:HEAD_DIM * h + 1]
                bias_s[(2 * b + first) * 2 + h] = jnp.where(valid, -slope * base, NEG_BIG)


def _bias_index(b, sb, n):
    first = jnp.logical_and(sb == 0, n == 0).astype(jnp.int32)
    return (2 * b + first) * 2


def _offsets(dil, res, n):
    nch, w = _geometry(dil)

    def al(v):
        return v if isinstance(v, int) else pl.multiple_of(v, w)

    q_off = [al((a * dil + res) * L + n * w) for a in range(nch)]
    k_off = [al((a * dil + res) * 2 * L + L + n * w) for a in range(nch)]
    kp_off = [al((a * dil + res) * 2 * L + L + n * w - w) for a in range(nch)]
    return q_off, k_off, kp_off, w


def _gather(ref, offs, w):
    parts = [ref[pl.ds(o, w), :] for o in offs]
    return parts[0] if len(parts) == 1 else jnp.concatenate(parts, axis=0)


def _scatter(ref, offs, w, val, add=False):
    for a, o in enumerate(offs):
        piece = val[a * w:(a + 1) * w, :]
        if add:
            ref[pl.ds(o, w), :] += piece
        else:
            ref[pl.ds(o, w), :] = piece


def _fill_key_buffer(buf, prev_ref, cur_ref):
    for r in range(RES):
        buf[2 * L * r:2 * L * r + L, :] = prev_ref[L * r:L * (r + 1), :]
        buf[2 * L * r + L:2 * L * (r + 1), :] = cur_ref[L * r:L * (r + 1), :]


def _two_heads(x, low):
    zero = jnp.zeros_like(x)
    return jnp.concatenate([jnp.where(low, x, zero), jnp.where(low, zero, x)], axis=0)


def _loop_blocks(dil, stages, together):
    together = max(together, dil) if dil < RES else together

    def it(i, c):
        if dil == RES:
            blocks = [(i * together + k, 0) for k in range(together)]
        else:
            blocks = [(res, i * (together // dil) + k) for k in range(together // dil) for res in range(dil)]
        state = [stages[0](res, n) for res, n in blocks]
        for stage in stages[1:]:
            state = [stage(res, n, prev) for (res, n), prev in zip(blocks, state)]
        for writes in state:
            for args in writes:
                _scatter(*args)
        return c

    lax.fori_loop(0, RES // together, it, 0)


NT = (((1,), (1,)), ((), ()))
TN = (((0,), (0,)), ((), ()))


def _attention_fwd(q, kv, name):
    s, d = q.shape
    g_n, ns = d // PAIR, s // SUPER

    def body(sl_ref, q_ref, kc_ref, kp_ref, vc_ref, vp_ref, o_ref, lse_ref, kbuf, vbuf, m_s, l_s, acc_s, bias_s):
        sb = pl.program_id(1)
        _fill_key_buffer(kbuf, kp_ref, kc_ref)
        _fill_key_buffer(vbuf, vp_ref, vc_ref)
        pl.when(sb == 0)(lambda: _fill_bias(bias_s, sl_ref))
        low = lax.broadcasted_iota(jnp.int32, (L, PAIR), 1) < HEAD_DIM
        low_k = lax.broadcasted_iota(jnp.int32, (2 * L, PAIR), 1) < HEAD_DIM
        ones_bd = _two_heads(jnp.ones((2 * L, PAIR), BF16), low_k)
        for bi, dil in enumerate(DILATIONS):
            first_branch, last_branch = bi == 0, bi == len(DILATIONS) - 1

            def scores(res, n, dil=dil):
                q_off, k_off, kp_off, w = _offsets(dil, res, n)
                qf = _gather(q_ref, q_off, w).astype(BF16)
                kcat = jnp.concatenate([_gather(kbuf, kp_off, w), _gather(kbuf, k_off, w)], axis=0).astype(BF16)
                return lax.dot_general(qf, _two_heads(kcat, low_k), NT, preferred_element_type=F32)

            def update(res, n, sc, bi=bi, dil=dil, first_branch=first_branch, last_branch=last_branch):
                q_off, k_off, kp_off, w = _offsets(dil, res, n)
                vcat = jnp.concatenate([_gather(vbuf, kp_off, w), _gather(vbuf, k_off, w)], axis=0).astype(BF16)
                v_ones = jnp.concatenate([_two_heads(vcat, low_k), ones_bd], axis=1)
                bias_at = _bias_index(bi, sb, n)
                if not first_branch:
                    m_prev = _gather(m_s, q_off, w)
                ps, m_new = [], []
                for h in range(2):
                    s_h = sc[:, 2 * L * h:2 * L * (h + 1)] + bias_s[bias_at + h]
                    mh = jnp.max(s_h, axis=1, keepdims=True)
                    if not first_branch:
                        mh = jnp.maximum(mh, m_prev[:, HEAD_DIM * h:HEAD_DIM * h + 1])
                    ps.append(jnp.exp(s_h - mh).astype(BF16))
                    m_new.append(mh)
                m_full = jnp.where(low, m_new[0], m_new[1])
                both = jnp.dot(jnp.concatenate(ps, axis=1), v_ones, preferred_element_type=F32)
                acc, l_full = both[:, :PAIR], both[:, PAIR:]
                if not first_branch:
                    alpha = jnp.exp(m_prev - m_full)
                    l_full = _gather(l_s, q_off, w) * alpha + l_full
                    acc = _gather(acc_s, q_off, w) * alpha + acc
                if last_branch:
                    return [(o_ref, q_off, w, acc / l_full, False), (lse_ref, q_off, w, m_full + jnp.log(l_full), False)]
                return [(m_s, q_off, w, m_full, False), (l_s, q_off, w, l_full, False), (acc_s, q_off, w, acc, False)]

            _loop_blocks(dil, [scores, update], FWD_TOGETHER)

    prev = lambda i: jnp.maximum(i - 1, 0)
    blk = pl.BlockSpec((SUPER, PAIR), lambda g, i: (i, g))
    in_specs = [pl.BlockSpec((None, 8, PAIR), lambda g, i: (g, 0, 0)), blk,
                pl.BlockSpec((SUPER, PAIR), lambda g, i: (i, g)),
                pl.BlockSpec((SUPER, PAIR), lambda g, i: (prev(i), g)),
                pl.BlockSpec((SUPER, PAIR), lambda g, i: (i, g_n + g)),
                pl.BlockSpec((SUPER, PAIR), lambda g, i: (prev(i), g_n + g))]
    return pl.pallas_call(
        body, name=name, grid=(g_n, ns), in_specs=in_specs, out_specs=[blk, blk],
        out_shape=[jax.ShapeDtypeStruct((s, d), F32)] * 2,
        scratch_shapes=([pltpu.VMEM((2 * SUPER, PAIR), F32)] * 2 + [pltpu.VMEM((SUPER, PAIR), F32)] * 3
                        + [pltpu.VMEM((4 * len(DILATIONS), L, 2 * L), F32)]),
        compiler_params=_params(("parallel", "arbitrary")),
    )(_slope_table(d), q, kv, kv, kv, kv)


def _attention_bwd(q, kv, o, do, lse, dkv_in, name):
    s, d = q.shape
    g_n, ns = d // PAIR, s // SUPER
    has_in = dkv_in is not None

    def body(*refs):
        sl_ref, q_ref, do_ref, o_ref, lse_ref, kc_ref, kp_ref, vc_ref, vp_ref = refs[:9]
        pos = 9
        if has_in:
            dkv_in_ref = refs[9]
            pos = 10
        dq_ref, dkv_ref, kbuf, vbuf, dkbuf, dvbuf, dq_s, bias_s = refs[pos:]
        step = pl.program_id(1)
        sb = ns - 1 - step
        _fill_key_buffer(kbuf, kp_ref, kc_ref)
        _fill_key_buffer(vbuf, vp_ref, vc_ref)

        @pl.when(step == 0)
        def _():
            _fill_bias(bias_s, sl_ref)
            dkbuf[...] = jnp.zeros_like(dkbuf)
            dvbuf[...] = jnp.zeros_like(dvbuf)

        @pl.when(step > 0)
        def _():
            for buf in (dkbuf, dvbuf):
                for r in range(RES):
                    buf[2 * L * r + L:2 * L * (r + 1), :] = buf[2 * L * r:2 * L * r + L, :]
                    buf[2 * L * r:2 * L * r + L, :] = jnp.zeros((L, PAIR), F32)

        low = lax.broadcasted_iota(jnp.int32, (L, PAIR), 1) < HEAD_DIM
        low_k = lax.broadcasted_iota(jnp.int32, (2 * L, PAIR), 1) < HEAD_DIM
        low_t = lax.broadcasted_iota(jnp.int32, (PAIR, 2 * L), 0) < HEAD_DIM
        for bi, dil in enumerate(DILATIONS):
            first_branch = bi == 0

            def scores(res, n, dil=dil):
                q_off, k_off, kp_off, w = _offsets(dil, res, n)
                qb = _gather(q_ref, q_off, w).astype(BF16)
                dof = _gather(do_ref, q_off, w)
                prod = dof * _gather(o_ref, q_off, w)
                dob = dof.astype(BF16)
                lse_f = _gather(lse_ref, q_off, w)
                zero = jnp.zeros_like(prod)
                dsum = (jnp.sum(jnp.where(low, prod, zero), axis=1, keepdims=True),
                        jnp.sum(jnp.where(low, zero, prod), axis=1, keepdims=True))
                kcat = jnp.concatenate([_gather(kbuf, kp_off, w), _gather(kbuf, k_off, w)], axis=0).astype(BF16)
                vcat = jnp.concatenate([_gather(vbuf, kp_off, w), _gather(vbuf, k_off, w)], axis=0).astype(BF16)
                k_bd, v_bd = _two_heads(kcat, low_k), _two_heads(vcat, low_k)
                sc = lax.dot_general(qb, k_bd, NT, preferred_element_type=F32)
                dp = lax.dot_general(dob, v_bd, NT, preferred_element_type=F32)
                return qb, dob, lse_f, dsum, k_bd, sc, dp

            def gradients(res, n, given, bi=bi, dil=dil, first_branch=first_branch):
                qb, dob, lse_f, dsum, k_bd, sc, dp = given
                q_off, k_off, kp_off, w = _offsets(dil, res, n)
                bias_at = _bias_index(bi, sb, n)
                ps, dss = [], []
                for h in range(2):
                    cols = slice(2 * L * h, 2 * L * (h + 1))
                    lse_h = lse_f[:, HEAD_DIM * h:HEAD_DIM * h + 1]
                    p_h = jnp.exp(sc[:, cols] + bias_s[bias_at + h] - lse_h)
                    dss.append((p_h * (dp[:, cols] - dsum[h])).astype(BF16))
                    ps.append(p_h.astype(BF16))
                ds_cat, p_cat = jnp.concatenate(dss, axis=1), jnp.concatenate(ps, axis=1)
                dq = jnp.dot(ds_cat, k_bd, preferred_element_type=F32)
                dk_t = lax.dot_general(qb, ds_cat, TN, preferred_element_type=F32)
                dv_t = lax.dot_general(dob, p_cat, TN, preferred_element_type=F32)
                dk = jnp.where(low_t, dk_t[:, :2 * L], dk_t[:, 2 * L:]).T
                dv = jnp.where(low_t, dv_t[:, :2 * L], dv_t[:, 2 * L:]).T
                return [(dq_s, q_off, w, dq, not first_branch),
                        (dkbuf, kp_off, w, dk[:L], True), (dkbuf, k_off, w, dk[L:], True),
                        (dvbuf, kp_off, w, dv[:L], True), (dvbuf, k_off, w, dv[L:], True)]

            _loop_blocks(dil, [scores, gradients], BWD_TOGETHER)

        dq_ref[...] = dq_s[...].astype(BF16)
        for r in range(RES):
            rows, cur = slice(L * r, L * (r + 1)), slice(2 * L * r + L, 2 * L * (r + 1))
            for plane, buf in enumerate((dkbuf, dvbuf)):
                if has_in:
                    dkv_ref[plane, rows, :] = buf[cur, :] + dkv_in_ref[plane, rows, :]
                else:
                    dkv_ref[plane, rows, :] = buf[cur, :]

    rev = lambda i: ns - 1 - i
    prev = lambda i: jnp.maximum(ns - 2 - i, 0)
    blk = pl.BlockSpec((SUPER, PAIR), lambda g, i: (rev(i), g))
    in_specs = [pl.BlockSpec((None, 8, PAIR), lambda g, i: (g, 0, 0)), blk, blk, blk, blk,
                pl.BlockSpec((SUPER, PAIR), lambda g, i: (rev(i), g)),
                pl.BlockSpec((SUPER, PAIR), lambda g, i: (prev(i), g)),
                pl.BlockSpec((SUPER, PAIR), lambda g, i: (rev(i), g_n + g)),
                pl.BlockSpec((SUPER, PAIR), lambda g, i: (prev(i), g_n + g))]
    ins = [_slope_table(d), q, do, o, lse, kv, kv, kv, kv]
    planes = pl.BlockSpec((2, SUPER, PAIR), lambda g, i: (0, rev(i), g))
    if has_in:
        in_specs.append(planes)
        ins.append(dkv_in)
    res = pl.pallas_call(
        body, name=name, grid=(g_n, ns), in_specs=in_specs, out_specs=[blk, planes],
        out_shape=[jax.ShapeDtypeStruct((s, d), BF16), jax.ShapeDtypeStruct((2, s, d), F32)],
        scratch_shapes=([pltpu.VMEM((2 * SUPER, PAIR), F32)] * 4 + [pltpu.VMEM((SUPER, PAIR), F32)]
                        + [pltpu.VMEM((4 * len(DILATIONS), L, 2 * L), F32)]),
        compiler_params=_params(("parallel", "arbitrary")),
    )(*ins)
    return res[0], res[1]


def _coords():
    return lax.axis_index("x"), lax.axis_index("y"), lax.axis_index("c")


def _chip_peers(x, y):
    return [(1 - x, y), (x, 1 - y), (1 - x, 1 - y)]


def _block_of(ref, axis, blk, size):
    start = pl.multiple_of(blk * size, size)
    if axis == 1:
        return ref.at[:, pl.ds(start, size)]
    return ref.at[pl.ds(start, size), :]


ANY = pl.BlockSpec(memory_space=pl.ANY)


HBM = pl.BlockSpec(memory_space=pltpu.HBM)
SEM = pl.BlockSpec(memory_space=pltpu.SEMAPHORE)
SPLIT = pltpu.CompilerParams(has_side_effects=pltpu.SideEffectType.DATAFLOW_SIDE_EFFECTING)


def _in_hbm(a):
    return pltpu.with_memory_space_constraint(a, pltpu.HBM)


def _thru(arrays):
    return [pltpu.HBM(a.shape, a.dtype) for a in arrays]


def _cast_place(w, layer, ax, dtype, name):
    _, k, n = w.shape
    t = _pick(k, (512, 256, 128))
    nb = k // t

    def body(blk_ref, w_ref, b_ref, f_ref):
        v = w_ref[...].astype(dtype)
        b_ref[...] = v
        f_ref[...] = v

    full_shape = (k, 4 * n) if ax == 1 else (4 * k, n)
    place = (lambda i, blk: (i, blk[0])) if ax == 1 else (lambda i, blk: (blk[0] * nb + i, 0))
    return pl.pallas_call(
        body, name=name,
        grid_spec=pltpu.PrefetchScalarGridSpec(
            num_scalar_prefetch=1, grid=(nb,),
            in_specs=[pl.BlockSpec((None, t, n), lambda i, blk: (layer, i, 0))],
            out_specs=[pl.BlockSpec((t, n), lambda i, blk: (i, 0)), pl.BlockSpec((t, n), place)]),
        out_shape=[jax.ShapeDtypeStruct((k, n), dtype), jax.ShapeDtypeStruct(full_shape, dtype)],
        compiler_params=_params(("parallel",)),
    )(_my_block()[None], w)


def _my_block():
    return (2 * lax.axis_index("x") + lax.axis_index("y")).astype(jnp.int32)


def _gather_start(group, carry, name):
    n, nc = len(group), len(carry)

    def body(*refs):
        blocks, fulls, send_sem, recv_sem = refs[:n], refs[n:2 * n], refs[2 * n + nc], refs[2 * n + nc + 1]
        x, y, c = _coords()
        for t, (b, _, ax) in enumerate(group):
            mine = _block_of(fulls[t], ax, 2 * x + y, b.shape[ax])
            for j, (px, py) in enumerate(_chip_peers(x, y)):
                pltpu.make_async_remote_copy(
                    src_ref=blocks[t], dst_ref=mine, send_sem=send_sem.at[3 * t + j], recv_sem=recv_sem.at[3 * t + j],
                    device_id=(px, py, c), device_id_type=MESH).start()

    arrays = [b for b, _, _ in group] + [f for _, f, _ in group] + list(carry)
    sems = [pltpu.SemaphoreType.DMA((3 * n,))] * 2
    res = pl.pallas_call(
        body, name=name, in_specs=[HBM] * len(arrays), out_specs=[SEM, SEM] + [HBM] * len(arrays),
        out_shape=sems + _thru(arrays), input_output_aliases={i: 2 + i for i in range(len(arrays))},
        compiler_params=SPLIT,
    )(*[_in_hbm(a) for a in arrays])
    return (res[0], res[1], list(res[2:2 + n]), list(res[2 + n:2 + 2 * n])), list(res[2 + 2 * n:])


def _gather_wait(group, started, after, name):
    sends, recvs, blocks, fulls = started
    m = len(group)

    def body(*refs):
        blk_refs, full_refs, send_sem, recv_sem = refs[:m], refs[m:2 * m], refs[2 * m], refs[2 * m + 1]
        x, y, c = _coords()
        for t, (b, _, ax) in enumerate(group):
            for j, (px, py) in enumerate(_chip_peers(x, y)):
                cp = pltpu.make_async_remote_copy(
                    src_ref=blk_refs[t], dst_ref=_block_of(full_refs[t], ax, 2 * px + py, b.shape[ax]),
                    send_sem=send_sem.at[3 * t + j], recv_sem=recv_sem.at[3 * t + j],
                    device_id=(px, py, c), device_id_type=MESH)
                cp.wait_send()
                cp.wait_recv()

    extra = [] if after is None else [after]
    res = pl.pallas_call(
        body, name=name, in_specs=[HBM] * (2 * m) + [SEM, SEM] + [ANY] * len(extra), out_specs=[HBM] * (2 * m),
        out_shape=_thru(blocks) + _thru(fulls), input_output_aliases={i: i for i in range(2 * m)},
        compiler_params=SPLIT,
    )(*blocks, *fulls, sends, recvs, *extra)
    return list(res[m:])


def _scatter_start(grads, carry, name):
    n = len(grads)
    n_in = 2 * n + len(carry)

    def body(*refs):
        g_refs, st_refs, send_sem, recv_sem = refs[:n], refs[n:2 * n], refs[n_in], refs[n_in + 1]
        x, y, c = _coords()
        for t, (g, ax) in enumerate(grads):
            for j, (px, py) in enumerate(_chip_peers(x, y)):
                pltpu.make_async_remote_copy(
                    src_ref=_block_of(g_refs[t], ax, 2 * px + py, g.shape[ax] // 4), dst_ref=st_refs[t].at[j],
                    send_sem=send_sem.at[3 * t + j], recv_sem=recv_sem.at[3 * t + j],
                    device_id=(px, py, c), device_id_type=MESH).start()

    arrays = [g for g, _ in grads]
    for g, ax in grads:
        shape = list(g.shape)
        shape[ax] //= 4
        arrays.append(lax.empty((3, *shape), g.dtype))
    arrays += list(carry)
    sems = [pltpu.SemaphoreType.DMA((3 * n,))] * 2
    res = pl.pallas_call(
        body, name=name, in_specs=[HBM] * n_in, out_specs=[SEM, SEM] + [HBM] * n_in,
        out_shape=sems + _thru(arrays), input_output_aliases={i: 2 + i for i in range(n_in)},
        compiler_params=SPLIT,
    )(*[_in_hbm(a) for a in arrays])
    return (res[0], res[1], list(res[2:2 + n]), list(res[2 + n:2 + 2 * n])), list(res[2 + 2 * n:])


def _scatter_wait(axes, started, after, name):
    sends, recvs, full, stacks = started
    n = len(full)
    extra = [] if after is None else [after]

    def body(*refs):
        g_refs, st_refs, send_sem, recv_sem = refs[:n], refs[n:2 * n], refs[2 * n], refs[2 * n + 1]
        x, y, c = _coords()
        for t, ax in enumerate(axes):
            size = full[t].shape[ax] // 4
            for j, (px, py) in enumerate(_chip_peers(x, y)):
                cp = pltpu.make_async_remote_copy(
                    src_ref=_block_of(g_refs[t], ax, 2 * px + py, size), dst_ref=st_refs[t].at[j],
                    send_sem=send_sem.at[3 * t + j], recv_sem=recv_sem.at[3 * t + j],
                    device_id=(px, py, c), device_id_type=MESH)
                cp.wait_send()
                cp.wait_recv()

    res = pl.pallas_call(
        body, name=name, in_specs=[HBM] * (2 * n) + [SEM, SEM] + [ANY] * len(extra), out_specs=[HBM] * (2 * n),
        out_shape=_thru(full) + _thru(stacks), input_output_aliases={i: i for i in range(2 * n)},
        compiler_params=SPLIT,
    )(*full, *stacks, sends, recvs, *extra)
    return list(res[:n]), list(res[n:])


def _pair_copies(g_refs, st_refs, out_refs, items, send_sem, recv_sem):
    x, y, c = _coords()
    copies = []
    for u, (g, ax, _) in enumerate(items):
        own = _block_of(g_refs[u], ax, 2 * x + y, g.shape[ax] // 4)
        for k, (src, dst) in enumerate([(own, out_refs[u].at[0]), (st_refs[u], out_refs[u].at[pl.ds(1, 3)])]):
            copies.append(pltpu.make_async_remote_copy(
                src_ref=src, dst_ref=dst, send_sem=send_sem.at[2 * u + k], recv_sem=recv_sem.at[2 * u + k],
                device_id=(x, y, 1 - c), device_id_type=MESH))
    return copies


def _pair_start(items, carry, name):
    n = len(items)
    n_in = 3 * n + len(carry)

    def body(*refs):
        for cp in _pair_copies(refs[:n], refs[n:2 * n], refs[2 * n:3 * n], items, refs[n_in], refs[n_in + 1]):
            cp.start()

    arrays = ([g for g, _, _ in items] + [st for _, _, st in items]
              + [lax.empty((4, *st.shape[1:]), st.dtype) for _, _, st in items] + list(carry))
    sems = [pltpu.SemaphoreType.DMA((2 * n,))] * 2
    res = pl.pallas_call(
        body, name=name, in_specs=[HBM] * n_in, out_specs=[SEM, SEM] + [HBM] * n_in,
        out_shape=sems + _thru(arrays), input_output_aliases={i: 2 + i for i in range(n_in)},
        compiler_params=SPLIT,
    )(*[_in_hbm(a) for a in arrays])
    thru = res[2:]
    return (res[0], res[1], *(list(thru[k * n:(k + 1) * n]) for k in range(3))), list(thru[3 * n:])


def _pair_wait(axes, started, name):
    send, recv, full, stacks, landing = started
    n = len(full)
    items = [(full[u], axes[u], stacks[u]) for u in range(n)]

    def body(*refs):
        for cp in _pair_copies(refs[:n], refs[n:2 * n], refs[2 * n:3 * n], items, refs[3 * n], refs[3 * n + 1]):
            cp.wait_send()
            cp.wait_recv()

    res = pl.pallas_call(
        body, name=name, in_specs=[HBM] * (3 * n) + [SEM, SEM], out_specs=[HBM] * (3 * n),
        out_shape=_thru(full + stacks + landing), input_output_aliases={i: i for i in range(3 * n)},
        compiler_params=SPLIT,
    )(*full, *stacks, *landing, send, recv)
    return list(res[:n]), list(res[n:2 * n]), list(res[2 * n:])


def _allreduce_small(v, name):
    r, cdim = v.shape

    def body(v_ref, out_ref, buf, send_sems, recv_sems):
        x, y, c = _coords()
        me = 4 * x + 2 * y + c
        buf[0] = v_ref[...]
        sends = []
        for k in range(1, 8):
            peer = (x if not (k & 4) else 1 - x, y if not (k & 2) else 1 - y, c if not (k & 1) else 1 - c)
            cp = pltpu.make_async_remote_copy(
                src_ref=v_ref, dst_ref=buf.at[k], send_sem=send_sems.at[k - 1], recv_sem=recv_sems.at[k - 1],
                device_id=peer, device_id_type=MESH)
            cp.start()
            sends.append(cp)
        for cp in sends:
            cp.wait_recv()
        total = buf[me]
        for src in range(1, 8):
            total = total + buf[jnp.bitwise_xor(me, src)]
        out_ref[...] = total
        for cp in sends:
            cp.wait_send()

    return pl.pallas_call(
        body, name=name,
        in_specs=[pl.BlockSpec(memory_space=pltpu.VMEM)], out_specs=pl.BlockSpec(memory_space=pltpu.VMEM),
        out_shape=jax.ShapeDtypeStruct((r, cdim), F32),
        scratch_shapes=[pltpu.VMEM((8, r, cdim), F32), pltpu.SemaphoreType.DMA((7,)), pltpu.SemaphoreType.DMA((7,))],
        compiler_params=pltpu.CompilerParams(has_side_effects=True),
    )(v)


def _adamw_math(w, g, m, v):
    m = ADAM_B1 * m + (1.0 - ADAM_B1) * g
    v = ADAM_B2 * v + (1.0 - ADAM_B2) * jnp.square(g)
    m_hat = m / (1.0 - ADAM_B1 ** ADAM_STEP)
    v_hat = v / (1.0 - ADAM_B2 ** ADAM_STEP)
    delta = -ADAM_LR * (m_hat / (jnp.sqrt(v_hat) + ADAM_EPS) + ADAM_WD * w)
    return delta, m, v


def _adamw(w, m, v, grads, name):
    r, cdim = w.shape
    paired = isinstance(grads, list)
    layers = len(grads) if paired else 1
    t = _pick(r // layers, (128, 64, 32, 16, 8))
    per_layer = r // layers // t
    n_grad = 3 * layers if paired else 1

    def body(*refs):
        refs = refs[1:] if paired else refs
        w_ref, m_ref, v_ref = refs[:3]
        outs = refs[3 + n_grad:]

        def update(g):
            delta, m_new, v_new = _adamw_math(w_ref[...], g, m_ref[...], v_ref[...])
            outs[0][...] = g
            outs[1][...] = delta
            outs[2][...] = m_new
            outs[3][...] = v_new

        if not paired:
            update(refs[3][...])
            return
        layer = pl.program_id(0) // per_layer
        for l in range(layers):
            @pl.when(layer == l)
            def _(own_ref=refs[3 + 3 * l], st_ref=refs[4 + 3 * l], sib_ref=refs[5 + 3 * l]):
                sa = own_ref[...].astype(F32)
                sb = sib_ref[0].astype(F32)
                for k in range(3):
                    sa = sa + st_ref[k].astype(F32)
                    sb = sb + sib_ref[k + 1].astype(F32)
                update(sa + sb)

    out_shape = [jax.ShapeDtypeStruct((r, cdim), F32)] * 4
    if not paired:
        spec = pl.BlockSpec((t, cdim), lambda i: (i, 0))
        return pl.pallas_call(
            body, name=name, grid=(r // t,), in_specs=[spec] * 4, out_specs=[spec] * 4, out_shape=out_shape,
            compiler_params=_params(("parallel",)),
        )(w, m, v, grads)

    spec = pl.BlockSpec((t, cdim), lambda i, blk: (i, 0))
    ins, in_specs = [w, m, v], [spec] * 3
    for l, (g, ax, stack, sib) in enumerate(grads):
        row = lambda i, l=l: jnp.clip(i - l * per_layer, 0, per_layer - 1)
        own = ((lambda i, blk, row=row: (row(i), blk[0])) if ax == 1
               else (lambda i, blk, row=row: (blk[0] * per_layer + row(i), 0)))
        ins += [g, stack, sib]
        in_specs += [pl.BlockSpec((t, cdim), own),
                     pl.BlockSpec((3, t, cdim), lambda i, blk, row=row: (0, row(i), 0)),
                     pl.BlockSpec((4, t, cdim), lambda i, blk, row=row: (0, row(i), 0))]
    return pl.pallas_call(
        body, name=name,
        grid_spec=pltpu.PrefetchScalarGridSpec(
            num_scalar_prefetch=1, grid=(r // t,), in_specs=in_specs, out_specs=[spec] * 4),
        out_shape=out_shape, compiler_params=_params(("parallel",)),
    )(_my_block()[None], *ins)


def _local_step(x, target, gains, conv_ws, kv_gain, weights_of, send_grads):
    depth = len(gains)
    n_a = len(conv_ws)
    saved, ws = [], []
    kv = kvn = None
    _, (xn,) = _norm_res_fwd(x, None, None, [gains[0][0]], "norm_first")
    h = x
    for l in range(depth):
        g = gains[l]
        sv = {"x_in": h, "xn": xn}
        w = weights_of(l, "mix", h)
        ws.append(w)
        if l == n_a:
            kv = _mm(kvn, w["kv"], "nn", F32, "kv_fwd")
        if l < n_a:
            p = _mm(xn, w["conv_in"], "nn", BF16, f"conv_in_fwd_{l}")
            z = _conv_gate_fwd(p, conv_ws[l], f"conv_gate_fwd_{l}")
            mix, x1, xn2 = _mm_norm_res(z, w["conv_out"], h, g[1], g[2], f"conv_out_fwd_{l}")
            sv.update(p=p, z=z)
        else:
            j = l - n_a
            q = _mm(xn, w["q"], "nn", F32, f"q_fwd_{j}", scale=HEAD_DIM ** -0.5)
            o, lse = _attention_fwd(q, kv, f"attn_fwd_{j}")
            mix, x1, xn2 = _mm_norm_res(o, w["o"], h, g[1], g[2], f"o_fwd_{j}")
            sv.update(q=q, o=o, lse=lse)
        w.update(weights_of(l, "ffn", mix))
        f, a = _ffn_in_swiglu(xn2, w["ffn_in"], f"ffn_in_fwd_{l}")
        ff = _mm(a, w["ffn_out"], "nn", BF16, f"ffn_out_fwd_{l}")
        sv.update(mix=mix, x1=x1, xn2=xn2, f=f, a=a, ff=ff)
        saved.append(sv)
        if l == depth - 1:
            dx, loss = _norm_res_loss(x1, ff, g[3], target, "norm_loss")
        elif l == n_a - 1:
            h, _ = _norm_res_fwd(x1, ff, g[3], [], f"norm_end_{l}")
            h = _permute16(h, False, "permute_stream")
            _, (xn, kvn) = _norm_res_fwd(h, None, None, [gains[l + 1][0], kv_gain], "norm_permuted")
        else:
            h, (xn,) = _norm_res_fwd(x1, ff, g[3], [gains[l + 1][0]], f"norm_end_{l}")
    d_gains = [[None] * 4 for _ in range(depth)]
    d_conv = [None] * n_a
    d_kv_gain = None
    dkv = None
    _, _, dff, d_gains[depth - 1][3] = _norm_bwd(dx, [], None, (saved[-1]["ff"], gains[-1][3]), "norm_loss_bwd")
    for l in reversed(range(depth)):
        sv, g, w, grads = saved[l], gains[l], ws[l], {}
        grads["ffn_out"] =_mm(sv["a"], dff, "tn", BF16, f"ffn_out_dw_{l}")
        df = _ffn_out_dx_swiglu(dff, w["ffn_out"], sv["f"], f"ffn_out_dx_{l}")
        dxn2 = _mm(df, w["ffn_in"], "nt", BF16, f"ffn_in_dx_{l}")
        grads["ffn_in"] =_mm(sv["xn2"], df, "tn", BF16, f"ffn_in_dw_{l}")
        dx, (d_gains[l][2],), dmix, d_gains[l][1] = _norm_bwd(
            dx, [(dxn2, g[2])], sv["x1"], (sv["mix"], g[1]), f"norm_mid_bwd_{l}")
        dx, dmix = send_grads(l, "ffn", grads, [dx, dmix])
        if l < n_a:
            dz = _mm(dmix, w["conv_out"], "nt", BF16, f"conv_out_dx_{l}")
            grads["conv_out"] =_mm(sv["z"], dmix, "tn", BF16, f"conv_out_dw_{l}")
            dp, d_conv[l] = _conv_gate_bwd(sv["p"], dz, conv_ws[l], f"conv_gate_bwd_{l}")
            dxn = _mm(dp, w["conv_in"], "nt", BF16, f"conv_in_dx_{l}")
            grads["conv_in"] =_mm(sv["xn"], dp, "tn", BF16, f"conv_in_dw_{l}")
        else:
            j = l - n_a
            do = _mm(dmix, w["o"], "nt", F32, f"o_dx_{j}")
            grads["o"] =_mm(sv["o"], dmix, "tn", BF16, f"o_dw_{j}")
            dq, dkv = _attention_bwd(sv["q"], kv, sv["o"], do, sv["lse"], dkv, f"attn_bwd_{j}")
            scale = HEAD_DIM ** -0.5
            dxn = _mm(dq, w["q"], "nt", BF16, f"q_dx_{j}", scale=scale)
            grads["q"] =_mm(sv["xn"], dq, "tn", BF16, f"q_dw_{j}", scale=scale)
        branches = [(dxn, g[0])]
        if l == n_a:
            dkvn = _mm(dkv, w["kv"], "nt", BF16, "kv_dx")
            grads["kv"] =_mm(kvn, dkv, "tn", BF16, "kv_dw")
            branches.append((dkvn, kv_gain))
        post = (saved[l - 1]["ff"], gains[l - 1][3]) if l > 0 else None
        if l == n_a:
            dx, dgs, _, _ = _norm_bwd(dx, branches, sv["x_in"], None, f"norm_end_bwd_{l}")
            dx = _permute16(dx, True, "unpermute_stream")
            _, _, dff, dg_post = _norm_bwd(dx, [], None, post, "norm_boundary_bwd")
        else:
            dx, dgs, dff, dg_post = _norm_bwd(dx, branches, sv["x_in"], post, f"norm_end_bwd_{l}")
        if dff is None:
            send_grads(l, "mix", grads, [])
        else:
            dx, dff = send_grads(l, "mix", grads, [dx, dff])
        d_gains[l][0] = dgs[0]
        if l == n_a:
            d_kv_gain = dgs[1]
        if l > 0:
            d_gains[l - 1][3] = dg_post
    return loss, dx, d_gains, d_conv, d_kv_gain


BIG = (
    ("conv_in", 1), ("conv_out", 0), ("kv", 1), ("q", 0), ("o", 0), ("ffn_in", 1), ("ffn_out", 0))


def kernel(x, norm_g, conv_in_w, conv_w, conv_out_w, kv_norm_g, kv_w, q_w, o_w, ffn_in_w, ffn_out_w, loss_target, m_norm_g, m_conv_in_w, m_conv_w, m_conv_out_w, m_kv_norm_g, m_kv_w, m_q_w, m_o_w, m_ffn_in_w, m_ffn_out_w, v_norm_g, v_conv_in_w, v_conv_w, v_conv_out_w, v_kv_norm_g, v_kv_w, v_q_w, v_o_w, v_ffn_in_w, v_ffn_out_w):
    depth, _, dq = norm_g.shape
    d = 4 * dq
    n_a = conv_w.shape[0]
    big_w = {"conv_in": conv_in_w, "conv_out": conv_out_w, "kv": kv_w[None], "q": q_w, "o": o_w,
             "ffn_in": ffn_in_w, "ffn_out": ffn_out_w}
    big_m = {"conv_in": m_conv_in_w, "conv_out": m_conv_out_w, "kv": m_kv_w[None], "q": m_q_w, "o": m_o_w,
             "ffn_in": m_ffn_in_w, "ffn_out": m_ffn_out_w}
    big_v = {"conv_in": v_conv_in_w, "conv_out": v_conv_out_w, "kv": v_kv_w[None], "q": v_q_w, "o": v_o_w,
             "ffn_in": v_ffn_in_w, "ffn_out": v_ffn_out_w}

    n_gain, n_tap = depth * 4, n_a * conv_w.shape[1]
    small_rows = -(-(n_gain + n_tap + 1) // 8) * 8
    pad_rows = small_rows - n_gain - n_tap

    def pack_small(gains, taps):
        return jnp.concatenate([gains.reshape(n_gain, dq), taps.reshape(n_tap, dq), jnp.zeros((pad_rows, dq), F32)])

    axis_of = dict(BIG)

    def matrices_of(l, part):
        if part == "ffn":
            return [("ffn_in", l), ("ffn_out", l)]
        if l < n_a:
            return [("conv_in", l), ("conv_out", l)]
        return ([("kv", 0)] if l == n_a else []) + [("q", l - n_a), ("o", l - n_a)]

    halves = [(l, part) for l in range(depth) for part in ("mix", "ffn")]
    groups = {(l, part): [(*_cast_place(big_w[name], i, axis_of[name], BF16, f"place_{name}_{i}"), axis_of[name])
                          for name, i in matrices_of(l, part)] for l, part in halves}
    groups[halves[0]].append((*_cast_place(pack_small(norm_g, conv_w)[None], 0, 1, F32, "place_small"), 1))
    started = {halves[0]: _gather_start(groups[halves[0]], [], "gather_start_0_mix")[0]}

    def fetch(half, after):
        full = _gather_wait(groups[half], started[half], after, "gather_wait_%d_%s" % half)
        nxt = halves.index(half) + 1
        if nxt < len(halves):
            started[halves[nxt]], full = _gather_start(groups[halves[nxt]], full, "gather_start_%d_%s" % halves[nxt])
        return full

    target = _permute16(loss_target.reshape(x.shape[1:]), False, "permute_target")
    first = fetch(halves[0], target)
    small = first[-1]
    gains = [[small[4 * l + i][None] for i in range(4)] for l in range(depth)]
    conv_ws = [small[n_gain + 3 * l:n_gain + 3 * l + 3] for l in range(n_a)]
    kv_gain = kv_norm_g[None]

    def weights_of(l, part, after):
        full = first if (l, part) == halves[0] else fetch((l, part), after)
        return {name: full[t] for t, (name, _) in enumerate(matrices_of(l, part))}

    sent, paired = {}, {}
    LAG = 2

    def to_sibling(half, carry, after):
        axes = [axis_of[name] for name, _ in matrices_of(*half)]
        full, stacks = _scatter_wait(axes, sent[half], after, "scatter_wait_%d_%s" % half)
        paired[half], carry = _pair_start(list(zip(full, axes, stacks)), carry, "pair_start_%d_%s" % half)
        return carry

    def send_grads(l, part, grads, carry):
        sent[l, part], carry = _scatter_start(
            [(grads[name], axis_of[name]) for name, _ in matrices_of(l, part)], carry, f"scatter_start_{l}_{part}")
        older = halves.index((l, part)) + LAG
        if older < len(halves) and carry:
            carry = to_sibling(halves[older], carry, carry[0])
        return carry

    loss, dx, d_gains, d_conv, d_kv_gain = _local_step(
        x.reshape(x.shape[1:]), target, gains, conv_ws, kv_gain, weights_of, send_grads)
    loss = lax.psum(loss, ("x", "y", "c"))

    small_g = jnp.concatenate([dg for row in d_gains for dg in row] + list(d_conv) + [d_kv_gain]
                              + [jnp.zeros((pad_rows - 1, d), F32)])
    small_g = _allreduce_small(small_g, "allreduce_small")
    blk = 2 * lax.axis_index("x") + lax.axis_index("y")
    mine_small = lax.dynamic_slice_in_dim(small_g, blk * dq, dq, axis=1)
    kv_rows = d // dq

    def pack_opt(gains_like, taps_like, kv_like):
        rows = jnp.concatenate([gains_like.reshape(n_gain, dq), taps_like.reshape(n_tap, dq), kv_like.reshape(kv_rows, dq)])
        extra = -rows.shape[0] % 8
        return jnp.concatenate([rows, jnp.zeros((extra, dq), F32)]) if extra else rows

    sw = pack_opt(norm_g, conv_w, kv_norm_g)
    sm = pack_opt(m_norm_g, m_conv_w, m_kv_norm_g)
    sv = pack_opt(v_norm_g, v_conv_w, v_kv_norm_g)
    sg = pack_opt(mine_small[:n_gain], mine_small[n_gain:n_gain + n_tap], small_g[n_gain + n_tap])
    s_out = _adamw(sw, sm, sv, sg, "adamw_small")

    def unpack(a):
        return (a[:n_gain].reshape(depth, 4, dq), a[n_gain:n_gain + n_tap].reshape(n_a, -1, dq),
                a[n_gain + n_tap:n_gain + n_tap + kv_rows].reshape(d))

    small_out = [unpack(a) for a in s_out]

    landed, big_out = {}, {}

    def update(name):
        for half in halves:
            if matrices_of(*half)[0] not in landed and any(n == name for n, _ in matrices_of(*half)):
                axes = [axis_of[n] for n, _ in matrices_of(*half)]
                landed.update(zip(matrices_of(*half), zip(*_pair_wait(axes, paired[half], "pair_wait_%d_%s" % half))))
        shp, ax = big_w[name].shape, axis_of[name]
        rows, cols = shp[0] * shp[1], shp[2]
        flat = lambda a: a.reshape(rows, cols)
        full, stacks, sibling = zip(*[landed[name, i] for i in range(shp[0])])
        res = _adamw(flat(big_w[name]), flat(big_m[name]), flat(big_v[name]),
                     [(full[i], ax, stacks[i], sibling[i]) for i in range(shp[0])], f"adamw_{name}")
        big_out[name] = [a.reshape(shp[1:] if name == "kv" else shp) for a in res]

    pending = [half for half in reversed(halves) if half not in paired]
    for half in pending[:-1]:
        to_sibling(half, [], None)
    late = [name for name, _ in BIG if any(n == name for n, _ in matrices_of(*pending[-1]))]
    for name, _ in BIG:
        if name not in late:
            update(name)
    to_sibling(pending[-1], [], big_out["ffn_out"][0])
    for name in late:
        update(name)

    def leaves(i):
        ng, cw_, kg = small_out[i]
        return [ng, big_out["conv_in"][i], cw_, big_out["conv_out"][i], kg, big_out["kv"][i], big_out["q"][i],
                big_out["o"][i], big_out["ffn_in"][i], big_out["ffn_out"][i]]

    return (loss, dx.reshape(x.shape), *leaves(0), *leaves(1), *leaves(2), *leaves(3))
```

```python
import jax
import jax.numpy as jnp
import numpy as np
from jax import lax
from jax.experimental import pallas as pl
from jax.experimental.pallas import tpu as pltpu

F32 = jnp.float32
BF16 = jnp.bfloat16
HEAD_DIM = 64
DILATIONS = (1, 4, 16)
NORM_EPS = 1e-6
NEG_BIG = -1e30
VMEM_LIMIT = 48 * 1024 * 1024
ROW_TILE = 256
NORM_TILE = 512
LANE = 128
MESH = pl.DeviceIdType.MESH

ADAM_LR = 0.001
ADAM_B1 = 0.9
ADAM_B2 = 0.999
ADAM_EPS = 1e-08
ADAM_WD = 0.01
ADAM_STEP = 10

TILE_CANDIDATES = (1024, 1408, 768, 512, 384, 256, 128)


def _pick(dim, cands=TILE_CANDIDATES):
    for c in cands:
        if c <= dim and dim % c == 0:
            return c
    return dim


def _params(sem):
    return pltpu.CompilerParams(dimension_semantics=sem, vmem_limit_bytes=VMEM_LIMIT)


def _mm(a, b, mode, out_dtype, name, scale=None):
    a_planes = a.shape[0] if a.ndim == 3 else 1
    b_planes = b.shape[0] if b.ndim == 3 else 1
    if mode == "nn":
        m, k = a.shape[-2], a.shape[-1] * a_planes
        n = b.shape[1]
    elif mode == "nt":
        m, k = a.shape[-2], a.shape[-1] * a_planes
        n = b.shape[0]
    else:
        k, m = a.shape
        n = b.shape[-1] * b_planes
    tm, tn = _pick(m), _pick(n // b_planes)
    tk = _pick(k // a_planes, ((2048,) if mode == "tn" else (3072, 2816)) + TILE_CANDIDATES)
    nk = k // tk
    ka, nb = k // a_planes // tk, n // b_planes // tn
    if a_planes > 1:
        a_spec = pl.BlockSpec((None, tm, tk), lambda i, j, kk: (kk // ka, i, kk % ka))
    elif mode == "tn":
        a_spec = pl.BlockSpec((tk, tm), lambda i, j, kk: (kk, i))
    else:
        a_spec = pl.BlockSpec((tm, tk), lambda i, j, kk: (i, kk))
    if mode == "nn":
        b_spec = pl.BlockSpec((tk, tn), lambda i, j, kk: (kk, j))
        dims = (((1,), (0,)), ((), ()))
    elif mode == "nt":
        b_spec = pl.BlockSpec((tn, tk), lambda i, j, kk: (j, kk))
        dims = (((1,), (1,)), ((), ()))
    else:
        b_spec = (pl.BlockSpec((None, tk, tn), lambda i, j, kk: (j // nb, kk, j % nb)) if b_planes > 1
                  else pl.BlockSpec((tk, tn), lambda i, j, kk: (kk, j)))
        dims = (((0,), (0,)), ((), ()))

    def finish(acc):
        if scale is not None:
            acc = acc * scale
        return acc.astype(out_dtype)

    if nk == 1:
        def body(a_ref, b_ref, o_ref):
            o_ref[...] = finish(lax.dot_general(a_ref[...].astype(BF16), b_ref[...].astype(BF16), dims, preferred_element_type=F32))
        scratch = []
    else:
        def body(a_ref, b_ref, o_ref, acc_ref):
            kk = pl.program_id(2)

            @pl.when(kk == 0)
            def _():
                acc_ref[...] = jnp.zeros_like(acc_ref)

            acc_ref[...] += lax.dot_general(a_ref[...].astype(BF16), b_ref[...].astype(BF16), dims, preferred_element_type=F32)

            @pl.when(kk == nk - 1)
            def _():
                o_ref[...] = finish(acc_ref[...])
        scratch = [pltpu.VMEM((tm, tn), F32)]

    return pl.pallas_call(
        body, name=name,
        grid=(m // tm, n // tn, nk),
        in_specs=[a_spec, b_spec],
        out_specs=pl.BlockSpec((tm, tn), lambda i, j, kk: (i, j)),
        out_shape=jax.ShapeDtypeStruct((m, n), out_dtype),
        scratch_shapes=scratch,
        compiler_params=_params(("parallel", "parallel", "arbitrary")),
    )(a, b)


def _rstd(v):
    return lax.rsqrt(jnp.mean(v * v, axis=-1, keepdims=True) + NORM_EPS)


def _rms_bwd(dy, v, g, r):
    gy = dy * g
    dv = r * (gy - v * (r * r) * jnp.mean(gy * v, axis=-1, keepdims=True))
    return dv, dy * v * r


def _row_spec(t, width):
    return pl.BlockSpec((t, width), lambda i: (i, 0))


def _gain_spec(width):
    return pl.BlockSpec((1, width), lambda i: (0, 0))


def _norm_res_fwd(x, mix, g_post, pre_gains, name):
    s, d = x.shape
    t = _pick(s, (NORM_TILE, ROW_TILE))
    has_mix = mix is not None
    n_pre = len(pre_gains)

    def body(*refs):
        x_ref = refs[0]
        pos = 1
        x1 = x_ref[...]
        if has_mix:
            mv = refs[1][...].astype(F32)
            x1 = x1 + mv * _rstd(mv) * refs[2][...]
            pos = 3
        gains = refs[pos:pos + n_pre]
        outs = refs[pos + n_pre:]
        if has_mix:
            outs[0][...] = x1
            outs = outs[1:]
        r = _rstd(x1)
        for g_ref, o_ref in zip(gains, outs):
            o_ref[...] = (x1 * r * g_ref[...]).astype(BF16)

    ins = [x] + ([mix, g_post] if has_mix else []) + list(pre_gains)
    in_specs = [_row_spec(t, d)] + ([_row_spec(t, d), _gain_spec(d)] if has_mix else []) + [_gain_spec(d)] * n_pre
    out_shape = ([jax.ShapeDtypeStruct((s, d), F32)] if has_mix else []) + [jax.ShapeDtypeStruct((s, d), BF16)] * n_pre
    out_specs = [_row_spec(t, d)] * len(out_shape)
    res = pl.pallas_call(
        body, name=name, grid=(s // t,), in_specs=in_specs, out_specs=out_specs, out_shape=out_shape,
        compiler_params=_params(("parallel",)),
    )(*ins)
    if has_mix:
        return res[0], list(res[1:])
    return x, list(res)


def _mm_norm_res(a, b, x, g_post, pre_gain, name):
    m, k = a.shape
    d = b.shape[1]
    tm, tk = _pick(m, (1024, 512, 256)), _pick(k)
    nk = k // tk

    def body(a_ref, b_ref, x_ref, gp_ref, g_ref, mix_ref, x1_ref, xn_ref, acc_ref):
        kk = pl.program_id(1)

        @pl.when(kk == 0)
        def _():
            acc_ref[...] = jnp.zeros_like(acc_ref)

        acc_ref[...] += jnp.dot(a_ref[...].astype(BF16), b_ref[...], preferred_element_type=F32)

        @pl.when(kk == nk - 1)
        def _():
            mix = acc_ref[...].astype(BF16)
            mix_ref[...] = mix
            mv = mix.astype(F32)
            x1 = x_ref[...] + mv * _rstd(mv) * gp_ref[...]
            x1_ref[...] = x1
            xn_ref[...] = (x1 * _rstd(x1) * g_ref[...]).astype(BF16)

    rows = pl.BlockSpec((tm, d), lambda i, kk: (i, 0))
    gain = pl.BlockSpec((1, d), lambda i, kk: (0, 0))
    return pl.pallas_call(
        body, name=name, grid=(m // tm, nk),
        in_specs=[pl.BlockSpec((tm, tk), lambda i, kk: (i, kk)), pl.BlockSpec((tk, d), lambda i, kk: (kk, 0)),
                  rows, gain, gain],
        out_specs=[rows, rows, rows],
        out_shape=[jax.ShapeDtypeStruct((m, d), BF16), jax.ShapeDtypeStruct((m, d), F32),
                   jax.ShapeDtypeStruct((m, d), BF16)],
        scratch_shapes=[pltpu.VMEM((tm, d), F32)],
        compiler_params=_params(("parallel", "arbitrary")),
    )(a, b, x, g_post, pre_gain)


def _norm_res_loss(x, mix, g_post, target, name):
    s, d = x.shape
    t = _pick(s, (NORM_TILE, ROW_TILE))

    def body(x_ref, m_ref, g_ref, t_ref, dy_ref, loss_ref):
        mv = m_ref[...].astype(F32)
        y = x_ref[...] + mv * _rstd(mv) * g_ref[...]
        err = y - t_ref[...]
        dy_ref[...] = err * (1.0 / d)

        @pl.when(pl.program_id(0) == 0)
        def _():
            loss_ref[...] = jnp.zeros_like(loss_ref)

        loss_ref[...] += jnp.sum(err * err)

    dy, acc = pl.pallas_call(
        body, name=name, grid=(s // t,),
        in_specs=[_row_spec(t, d), _row_spec(t, d), _gain_spec(d), _row_spec(t, d)],
        out_specs=[_row_spec(t, d), pl.BlockSpec((8, LANE), lambda i: (0, 0))],
        out_shape=[jax.ShapeDtypeStruct((s, d), F32), jax.ShapeDtypeStruct((8, LANE), F32)],
        compiler_params=_params(("arbitrary",)),
    )(x, mix, g_post, target)
    return dy, acc[0, 0] * (0.5 / d)


def _norm_bwd(dx_out, branches, x_in, post, name):
    s, d = dx_out.shape
    t = _pick(s, (NORM_TILE, ROW_TILE))
    nb = len(branches)
    has_post = post is not None

    def body(*refs):
        dx_ref = refs[0]
        pos = 1
        dx = dx_ref[...]
        first = pl.program_id(0) == 0
        n_in = 1 + (1 + 2 * nb if nb else 0) + (2 if has_post else 0)
        outs = refs[n_in:]
        opos = 0
        if nb:
            xv = refs[pos][...]
            pos += 1
            r = _rstd(xv)
            dx_o = outs[0]
            opos = 1
            for _ in range(nb):
                dxn = refs[pos][...].astype(F32)
                g = refs[pos + 1][...]
                pos += 2
                dv, dg_rows = _rms_bwd(dxn, xv, g, r)
                dx = dx + dv
                dg_ref = outs[opos]
                opos += 1

                @pl.when(first)
                def _(dg_ref=dg_ref):
                    dg_ref[...] = jnp.zeros_like(dg_ref)

                dg_ref[...] += jnp.sum(dg_rows, axis=0, keepdims=True)
            dx_o[...] = dx
        if has_post:
            mv = refs[pos][...].astype(F32)
            g = refs[pos + 1][...]
            dm, dg_rows = _rms_bwd(dx, mv, g, _rstd(mv))
            outs[opos][...] = dm.astype(BF16)
            dg_ref = outs[opos + 1]

            @pl.when(first)
            def _():
                dg_ref[...] = jnp.zeros_like(dg_ref)

            dg_ref[...] += jnp.sum(dg_rows, axis=0, keepdims=True)

    ins, in_specs = [dx_out], [_row_spec(t, d)]
    out_shape, out_specs = [], []
    if nb:
        ins.append(x_in)
        in_specs.append(_row_spec(t, d))
        out_shape.append(jax.ShapeDtypeStruct((s, d), F32))
        out_specs.append(_row_spec(t, d))
        for dxn, g in branches:
            ins += [dxn, g]
            in_specs += [_row_spec(t, d), _gain_spec(d)]
            out_shape.append(jax.ShapeDtypeStruct((1, d), F32))
            out_specs.append(_gain_spec(d))
    if has_post:
        ins += [post[0], post[1]]
        in_specs += [_row_spec(t, d), _gain_spec(d)]
        out_shape += [jax.ShapeDtypeStruct((s, d), BF16), jax.ShapeDtypeStruct((1, d), F32)]
        out_specs += [_row_spec(t, d), _gain_spec(d)]
    res = pl.pallas_call(
        body, name=name, grid=(s // t,), in_specs=in_specs, out_specs=out_specs, out_shape=out_shape,
        compiler_params=_params(("arbitrary",)),
    )(*ins)
    res = list(res)
    dx_in = res.pop(0) if nb else dx_out
    dgs = [res.pop(0) for _ in range(nb)]
    dm, dg_post = (res[0], res[1]) if has_post else (None, None)
    return dx_in, dgs, dm, dg_post


HALO = 16


def _shift_down(u, prev, k):
    rows = lax.broadcasted_iota(jnp.int32, u.shape, 0)
    out = pltpu.roll(u, k, 0)
    for i in range(k):
        out = jnp.where(rows == i, prev[HALO - k + i:HALO - k + i + 1, :], out)
    return out


def _shift_up(u, nxt, k):
    n = u.shape[0]
    rows = lax.broadcasted_iota(jnp.int32, u.shape, 0)
    out = pltpu.roll(u, n - k, 0)
    for i in range(k):
        out = jnp.where(rows == n - k + i, nxt[i:i + 1, :], out)
    return out


def _conv_gate_fwd(p, cw, name):
    s, d3 = p.shape
    d = d3 // 3
    t = _pick(s, (ROW_TILE,))
    hb = t // HALO

    def body(p_ref, prev_ref, w_ref, z_ref):
        i = pl.program_id(0)
        pv = p_ref[...].astype(F32)
        b, u = pv[:, :d], pv[:, d:2 * d] * pv[:, 2 * d:]
        ph = prev_ref[...].astype(F32)
        up = jnp.where(i > 0, ph[:, d:2 * d] * ph[:, 2 * d:], 0.0)
        w = w_ref[...]
        y = w[0:1, :] * _shift_down(u, up, 2) + w[1:2, :] * _shift_down(u, up, 1) + w[2:3, :] * u
        z_ref[...] = (b * y).astype(BF16)

    return pl.pallas_call(
        body, name=name, grid=(s // t,),
        in_specs=[_row_spec(t, d3),
                  pl.BlockSpec((HALO, d3), lambda i: (jnp.maximum(i * hb - 1, 0), 0)),
                  pl.BlockSpec((3, d), lambda i: (0, 0))],
        out_specs=_row_spec(t, d),
        out_shape=jax.ShapeDtypeStruct((s, d), BF16),
        compiler_params=_params(("parallel",)),
    )(p, p, cw)


def _conv_gate_bwd(p, dz, cw, name):
    s, d3 = p.shape
    d = d3 // 3
    t = _pick(s, (ROW_TILE,))
    hb = t // HALO
    nt = s // t
    last_halo = s // HALO - 1

    def body(p_ref, prev_ref, next_ref, dz_ref, dznext_ref, w_ref, dp_ref, dw_ref):
        i = pl.program_id(0)
        pv = p_ref[...].astype(F32)
        b, c, h = pv[:, :d], pv[:, d:2 * d], pv[:, 2 * d:]
        u = c * h
        ph = prev_ref[...].astype(F32)
        up = jnp.where(i > 0, ph[:, d:2 * d] * ph[:, 2 * d:], 0.0)
        w = w_ref[...]
        u1, u2 = _shift_down(u, up, 1), _shift_down(u, up, 2)
        y = w[0:1, :] * u2 + w[1:2, :] * u1 + w[2:3, :] * u
        dz = dz_ref[...].astype(F32)
        dy = dz * b
        dyn = jnp.where(i < nt - 1, dznext_ref[...].astype(F32) * next_ref[...].astype(F32)[:, :d], 0.0)
        du = w[2:3, :] * dy + w[1:2, :] * _shift_up(dy, dyn, 1) + w[0:1, :] * _shift_up(dy, dyn, 2)
        dp_ref[:, :d] = (dz * y).astype(BF16)
        dp_ref[:, d:2 * d] = (du * h).astype(BF16)
        dp_ref[:, 2 * d:] = (du * c).astype(BF16)

        @pl.when(i == 0)
        def _():
            dw_ref[...] = jnp.zeros_like(dw_ref)

        dw_ref[0:1, :] += jnp.sum(dy * u2, axis=0, keepdims=True)
        dw_ref[1:2, :] += jnp.sum(dy * u1, axis=0, keepdims=True)
        dw_ref[2:3, :] += jnp.sum(dy * u, axis=0, keepdims=True)

    return pl.pallas_call(
        body, name=name, grid=(nt,),
        in_specs=[_row_spec(t, d3),
                  pl.BlockSpec((HALO, d3), lambda i: (jnp.maximum(i * hb - 1, 0), 0)),
                  pl.BlockSpec((HALO, d3), lambda i: (jnp.minimum((i + 1) * hb, last_halo), 0)),
                  _row_spec(t, d),
                  pl.BlockSpec((HALO, d), lambda i: (jnp.minimum((i + 1) * hb, last_halo), 0)),
                  pl.BlockSpec((3, d), lambda i: (0, 0))],
        out_specs=[_row_spec(t, d3), pl.BlockSpec((3, d), lambda i: (0, 0))],
        out_shape=[jax.ShapeDtypeStruct((s, d3), BF16), jax.ShapeDtypeStruct((3, d), F32)],
        compiler_params=_params(("arbitrary",)),
    )(p, p, p, dz, dz, cw)


FFN_ROWS, FFN_COLS = (512, 256), (1408, 768, 256, 128)


def _row_chunks(tm, rows=256):
    return [slice(r, r + min(rows, tm)) for r in range(0, tm, min(rows, tm))]


def _ffn_in_swiglu(xn, w_in, name):
    s, k = xn.shape
    ff = w_in.shape[1] // 2
    tm, tn = _pick(s, FFN_ROWS), _pick(ff, FFN_COLS)
    nj = ff // tn

    def body(x_ref, wg_ref, wu_ref, f_ref, a_ref):
        for rows in _row_chunks(tm):
            xv = x_ref[rows, :]
            gate = jnp.dot(xv, wg_ref[...], preferred_element_type=F32)
            up = jnp.dot(xv, wu_ref[...], preferred_element_type=F32)
            f_ref[0, rows, :] = gate.astype(BF16)
            f_ref[1, rows, :] = up.astype(BF16)
            a_ref[rows, :] = (gate * jax.nn.sigmoid(gate) * up).astype(BF16)

    return pl.pallas_call(
        body, name=name, grid=(nj, s // tm),
        in_specs=[pl.BlockSpec((tm, k), lambda j, i: (i, 0)),
                  pl.BlockSpec((k, tn), lambda j, i: (0, j)),
                  pl.BlockSpec((k, tn), lambda j, i: (0, nj + j))],
        out_specs=[pl.BlockSpec((2, tm, tn), lambda j, i: (0, i, j)), pl.BlockSpec((tm, tn), lambda j, i: (i, j))],
        out_shape=[jax.ShapeDtypeStruct((2, s, ff), BF16), jax.ShapeDtypeStruct((s, ff), BF16)],
        compiler_params=_params(("parallel", "parallel")),
    )(xn, w_in, w_in)


def _ffn_out_dx_swiglu(dff, w_out, f, name):
    s, d = dff.shape
    ff = w_out.shape[0]
    tm, tn = _pick(s, FFN_ROWS), _pick(ff, FFN_COLS)

    def body(d_ref, w_ref, f_ref, df_ref):
        for rows in _row_chunks(tm):
            da = lax.dot_general(d_ref[rows, :], w_ref[...], (((1,), (1,)), ((), ())), preferred_element_type=F32)
            gate = f_ref[0, rows, :].astype(F32)
            up = f_ref[1, rows, :].astype(F32)
            sg = jax.nn.sigmoid(gate)
            silu = gate * sg
            df_ref[0, rows, :] = (da * up * (sg + silu * (1.0 - sg))).astype(BF16)
            df_ref[1, rows, :] = (da * silu).astype(BF16)

    planes = pl.BlockSpec((2, tm, tn), lambda j, i: (0, i, j))
    return pl.pallas_call(
        body, name=name, grid=(ff // tn, s // tm),
        in_specs=[pl.BlockSpec((tm, d), lambda j, i: (i, 0)), pl.BlockSpec((tn, d), lambda j, i: (j, 0)), planes],
        out_specs=planes, out_shape=jax.ShapeDtypeStruct((2, s, ff), BF16),
        compiler_params=_params(("parallel", "parallel")),
    )(dff, w_out, f)


SUPER = 2048
RES = 16
PAIR = 128
L = 128
FWD_TOGETHER = 16
BWD_TOGETHER = 4


def _alibi_slopes(n_heads):
    h = np.arange(n_heads, dtype=np.float32) + 1.0
    return np.power(2.0, -8.0 * h / n_heads).astype(np.float32)


def _permute16(x, inverse, name):
    s, d = x.shape
    cw = LANE

    def body(x_ref, o_ref):
        if inverse:
            for m in range(L):
                o_ref[RES * m:RES * (m + 1), :] = x_ref[pl.ds(m, RES, stride=L), :]
        else:
            for r in range(RES):
                o_ref[L * r:L * (r + 1), :] = x_ref[pl.ds(r, L, stride=RES), :]

    spec = pl.BlockSpec((SUPER, cw), lambda i, j: (i, j))
    return pl.pallas_call(
        body, name=name, grid=(s // SUPER, d // cw), in_specs=[spec], out_specs=spec,
        out_shape=jax.ShapeDtypeStruct((s, d), x.dtype),
        compiler_params=_params(("parallel", "parallel")),
    )(x)


def _slope_table(d):
    nh = d // HEAD_DIM
    sl = _alibi_slopes(nh)
    tab = np.repeat(sl, HEAD_DIM).reshape(d // PAIR, 1, PAIR)
    return jnp.asarray(np.broadcast_to(tab, (d // PAIR, 8, PAIR)).copy())


def _geometry(dil):
    nch = RES // dil
    return nch, L // nch


def _band(dil):
    nch, w = _geometry(dil)
    sh = w.bit_length() - 1
    i = lax.broadcasted_iota(jnp.int32, (L, 2 * L), 0)
    j = lax.broadcasted_iota(jnp.int32, (L, 2 * L), 1)

    def pos(t):
        return jnp.bitwise_and(t, w - 1) * nch + jnp.right_shift(t, sh)

    delta = pos(i) + L - (pos(jnp.bitwise_and(j, L - 1)) + jnp.bitwise_and(j, L))
    return (delta * dil).astype(F32), (delta >= 0) & (delta <= L), j < L


def _fill_bias(bias_s, sl_ref):
    for b, dil in enumerate(DILATIONS):
        base, band, prev_half = _band(dil)
        for first in range(2):
            valid = band & jnp.logical_not(prev_half) if first else band
            for h in range(2):
                slope = sl_ref[0:1, HEAD_DIM * h:HEAD_DIM * h + 1]
                bias_s[(2 * b + first) * 2 + h] = jnp.where(valid, -slope * base, NEG_BIG)


def _bias_index(b, sb, n):
    first = jnp.logical_and(sb == 0, n == 0).astype(jnp.int32)
    return (2 * b + first) * 2


def _offsets(dil, res, n):
    nch, w = _geometry(dil)

    def al(v):
        return v if isinstance(v, int) else pl.multiple_of(v, w)

    q_off = [al((a * dil + res) * L + n * w) for a in range(nch)]
    k_off = [al((a * dil + res) * 2 * L + L + n * w) for a in range(nch)]
    kp_off = [al((a * dil + res) * 2 * L + L + n * w - w) for a in range(nch)]
    return q_off, k_off, kp_off, w


def _gather(ref, offs, w):
    parts = [ref[pl.ds(o, w), :] for o in offs]
    return parts[0] if len(parts) == 1 else jnp.concatenate(parts, axis=0)


def _scatter(ref, offs, w, val, add=False):
    for a, o in enumerate(offs):
        piece = val[a * w:(a + 1) * w, :]
        if add:
            ref[pl.ds(o, w), :] += piece
        else:
            ref[pl.ds(o, w), :] = piece


def _fill_key_buffer(buf, prev_ref, cur_ref):
    for r in range(RES):
        buf[2 * L * r:2 * L * r + L, :] = prev_ref[L * r:L * (r + 1), :]
        buf[2 * L * r + L:2 * L * (r + 1), :] = cur_ref[L * r:L * (r + 1), :]


def _two_heads(x, low):
    zero = jnp.zeros_like(x)
    return jnp.concatenate([jnp.where(low, x, zero), jnp.where(low, zero, x)], axis=0)


def _loop_blocks(dil, stages, together):
    together = max(together, dil) if dil < RES else together

    def it(i, c):
        if dil == RES:
            blocks = [(i * together + k, 0) for k in range(together)]
        else:
            blocks = [(res, i * (together // dil) + k) for k in range(together // dil) for res in range(dil)]
        state = [stages[0](res, n) for res, n in blocks]
        for stage in stages[1:]:
            state = [stage(res, n, prev) for (res, n), prev in zip(blocks, state)]
        for writes in state:
            for args in writes:
                _scatter(*args)
        return c

    lax.fori_loop(0, RES // together, it, 0)


NT = (((1,), (1,)), ((), ()))
TN = (((0,), (0,)), ((), ()))


def _attention_fwd(q, kv, name):
    s, d = q.shape
    g_n, ns = d // PAIR, s // SUPER

    def body(sl_ref, q_ref, kc_ref, kp_ref, vc_ref, vp_ref, o_ref, lse_ref, kbuf, vbuf, m_s, l_s, acc_s, bias_s):
        sb = pl.program_id(1)
        _fill_key_buffer(kbuf, kp_ref, kc_ref)
        _fill_key_buffer(vbuf, vp_ref, vc_ref)
        pl.when(sb == 0)(lambda: _fill_bias(bias_s, sl_ref))
        low = lax.broadcasted_iota(jnp.int32, (L, PAIR), 1) < HEAD_DIM
        low_k = lax.broadcasted_iota(jnp.int32, (2 * L, PAIR), 1) < HEAD_DIM
        ones_bd = _two_heads(jnp.ones((2 * L, PAIR), BF16), low_k)
        for bi, dil in enumerate(DILATIONS):
            first_branch, last_branch = bi == 0, bi == len(DILATIONS) - 1

            def scores(res, n, dil=dil):
                q_off, k_off, kp_off, w = _offsets(dil, res, n)
                qf = _gather(q_ref, q_off, w).astype(BF16)
                kcat = jnp.concatenate([_gather(kbuf, kp_off, w), _gather(kbuf, k_off, w)], axis=0).astype(BF16)
                return lax.dot_general(qf, _two_heads(kcat, low_k), NT, preferred_element_type=F32)

            def update(res, n, sc, bi=bi, dil=dil, first_branch=first_branch, last_branch=last_branch):
                q_off, k_off, kp_off, w = _offsets(dil, res, n)
                vcat = jnp.concatenate([_gather(vbuf, kp_off, w), _gather(vbuf, k_off, w)], axis=0).astype(BF16)
                v_ones = jnp.concatenate([_two_heads(vcat, low_k), ones_bd], axis=1)
                bias_at = _bias_index(bi, sb, n)
                if not first_branch:
                    m_prev = _gather(m_s, q_off, w)
                ps, m_new = [], []
                for h in range(2):
                    s_h = sc[:, 2 * L * h:2 * L * (h + 1)] + bias_s[bias_at + h]
                    mh = jnp.max(s_h, axis=1, keepdims=True)
                    if not first_branch:
                        mh = jnp.maximum(mh, m_prev[:, HEAD_DIM * h:HEAD_DIM * h + 1])
                    ps.append(jnp.exp(s_h - mh).astype(BF16))
                    m_new.append(mh)
                m_full = jnp.where(low, m_new[0], m_new[1])
                both = jnp.dot(jnp.concatenate(ps, axis=1), v_ones, preferred_element_type=F32)
                acc, l_full = both[:, :PAIR], both[:, PAIR:]
                if not first_branch:
                    alpha = jnp.exp(m_prev - m_full)
                    l_full = _gather(l_s, q_off, w) * alpha + l_full
                    acc = _gather(acc_s, q_off, w) * alpha + acc
                if last_branch:
                    return [(o_ref, q_off, w, acc / l_full, False), (lse_ref, q_off, w, m_full + jnp.log(l_full), False)]
                return [(m_s, q_off, w, m_full, False), (l_s, q_off, w, l_full, False), (acc_s, q_off, w, acc, False)]

            _loop_blocks(dil, [scores, update], FWD_TOGETHER)

    prev = lambda i: jnp.maximum(i - 1, 0)
    blk = pl.BlockSpec((SUPER, PAIR), lambda g, i: (i, g))
    in_specs = [pl.BlockSpec((None, 8, PAIR), lambda g, i: (g, 0, 0)), blk,
                pl.BlockSpec((SUPER, PAIR), lambda g, i: (i, g)),
                pl.BlockSpec((SUPER, PAIR), lambda g, i: (prev(i), g)),
                pl.BlockSpec((SUPER, PAIR), lambda g, i: (i, g_n + g)),
                pl.BlockSpec((SUPER, PAIR), lambda g, i: (prev(i), g_n + g))]
    return pl.pallas_call(
        body, name=name, grid=(g_n, ns), in_specs=in_specs, out_specs=[blk, blk],
        out_shape=[jax.ShapeDtypeStruct((s, d), F32)] * 2,
        scratch_shapes=([pltpu.VMEM((2 * SUPER, PAIR), F32)] * 2 + [pltpu.VMEM((SUPER, PAIR), F32)] * 3
                        + [pltpu.VMEM((4 * len(DILATIONS), L, 2 * L), F32)]),
        compiler_params=_params(("parallel", "arbitrary")),
    )(_slope_table(d), q, kv, kv, kv, kv)


def _attention_bwd(q, kv, o, do, lse, dkv_in, name):
    s, d = q.shape
    g_n, ns = d // PAIR, s // SUPER
    has_in = dkv_in is not None

    def body(*refs):
        sl_ref, q_ref, do_ref, o_ref, lse_ref, kc_ref, kp_ref, vc_ref, vp_ref = refs[:9]
        pos = 9
        if has_in:
            dkv_in_ref = refs[9]
            pos = 10
        dq_ref, dkv_ref, kbuf, vbuf, dkbuf, dvbuf, dq_s, bias_s = refs[pos:]
        step = pl.program_id(1)
        sb = ns - 1 - step
        _fill_key_buffer(kbuf, kp_ref, kc_ref)
        _fill_key_buffer(vbuf, vp_ref, vc_ref)

        @pl.when(step == 0)
        def _():
            _fill_bias(bias_s, sl_ref)
            dkbuf[...] = jnp.zeros_like(dkbuf)
            dvbuf[...] = jnp.zeros_like(dvbuf)

        @pl.when(step > 0)
        def _():
            for buf in (dkbuf, dvbuf):
                for r in range(RES):
                    buf[2 * L * r + L:2 * L * (r + 1), :] = buf[2 * L * r:2 * L * r + L, :]
                    buf[2 * L * r:2 * L * r + L, :] = jnp.zeros((L, PAIR), F32)

        low = lax.broadcasted_iota(jnp.int32, (L, PAIR), 1) < HEAD_DIM
        low_k = lax.broadcasted_iota(jnp.int32, (2 * L, PAIR), 1) < HEAD_DIM
        low_t = lax.broadcasted_iota(jnp.int32, (PAIR, 2 * L), 0) < HEAD_DIM
        for bi, dil in enumerate(DILATIONS):
            first_branch = bi == 0

            def scores(res, n, dil=dil):
                q_off, k_off, kp_off, w = _offsets(dil, res, n)
                qb = _gather(q_ref, q_off, w).astype(BF16)
                dof = _gather(do_ref, q_off, w)
                prod = dof * _gather(o_ref, q_off, w)
                dob = dof.astype(BF16)
                lse_f = _gather(lse_ref, q_off, w)
                zero = jnp.zeros_like(prod)
                dsum = (jnp.sum(jnp.where(low, prod, zero), axis=1, keepdims=True),
                        jnp.sum(jnp.where(low, zero, prod), axis=1, keepdims=True))
                kcat = jnp.concatenate([_gather(kbuf, kp_off, w), _gather(kbuf, k_off, w)], axis=0).astype(BF16)
                vcat = jnp.concatenate([_gather(vbuf, kp_off, w), _gather(vbuf, k_off, w)], axis=0).astype(BF16)
                k_bd, v_bd = _two_heads(kcat, low_k), _two_heads(vcat, low_k)
                sc = lax.dot_general(qb, k_bd, NT, preferred_element_type=F32)
                dp = lax.dot_general(dob, v_bd, NT, preferred_element_type=F32)
                return qb, dob, lse_f, dsum, k_bd, sc, dp

            def gradients(res, n, given, bi=bi, dil=dil, first_branch=first_branch):
                qb, dob, lse_f, dsum, k_bd, sc, dp = given
                q_off, k_off, kp_off, w = _offsets(dil, res, n)
                bias_at = _bias_index(bi, sb, n)
                ps, dss = [], []
                for h in range(2):
                    cols = slice(2 * L * h, 2 * L * (h + 1))
                    lse_h = lse_f[:, HEAD_DIM * h:HEAD_DIM * h + 1]
                    p_h = jnp.exp(sc[:, cols] + bias_s[bias_at + h] - lse_h)
                    dss.append((p_h * (dp[:, cols] - dsum[h])).astype(BF16))
                    ps.append(p_h.astype(BF16))
                ds_cat, p_cat = jnp.concatenate(dss, axis=1), jnp.concatenate(ps, axis=1)
                dq = jnp.dot(ds_cat, k_bd, preferred_element_type=F32)
                dk_t = lax.dot_general(qb, ds_cat, TN, preferred_element_type=F32)
                dv_t = lax.dot_general(dob, p_cat, TN, preferred_element_type=F32)
                dk = jnp.where(low_t, dk_t[:, :2 * L], dk_t[:, 2 * L:]).T
                dv = jnp.where(low_t, dv_t[:, :2 * L], dv_t[:, 2 * L:]).T
                return [(dq_s, q_off, w, dq, not first_branch),
                        (dkbuf, kp_off, w, dk[:L], True), (dkbuf, k_off, w, dk[L:], True),
                        (dvbuf, kp_off, w, dv[:L], True), (dvbuf, k_off, w, dv[L:], True)]

            _loop_blocks(dil, [scores, gradients], BWD_TOGETHER)

        dq_ref[...] = dq_s[...].astype(BF16)
        for r in range(RES):
            rows, cur = slice(L * r, L * (r + 1)), slice(2 * L * r + L, 2 * L * (r + 1))
            for plane, buf in enumerate((dkbuf, dvbuf)):
                if has_in:
                    dkv_ref[plane, rows, :] = buf[cur, :] + dkv_in_ref[plane, rows, :]
                else:
                    dkv_ref[plane, rows, :] = buf[cur, :]

    rev = lambda i: ns - 1 - i
    prev = lambda i: jnp.maximum(ns - 2 - i, 0)
    blk = pl.BlockSpec((SUPER, PAIR), lambda g, i: (rev(i), g))
    in_specs = [pl.BlockSpec((None, 8, PAIR), lambda g, i: (g, 0, 0)), blk, blk, blk, blk,
                pl.BlockSpec((SUPER, PAIR), lambda g, i: (rev(i), g)),
                pl.BlockSpec((SUPER, PAIR), lambda g, i: (prev(i), g)),
                pl.BlockSpec((SUPER, PAIR), lambda g, i: (rev(i), g_n + g)),
                pl.BlockSpec((SUPER, PAIR), lambda g, i: (prev(i), g_n + g))]
    ins = [_slope_table(d), q, do, o, lse, kv, kv, kv, kv]
    planes = pl.BlockSpec((2, SUPER, PAIR), lambda g, i: (0, rev(i), g))
    if has_in:
        in_specs.append(planes)
        ins.append(dkv_in)
    res = pl.pallas_call(
        body, name=name, grid=(g_n, ns), in_specs=in_specs, out_specs=[blk, planes],
        out_shape=[jax.ShapeDtypeStruct((s, d), BF16), jax.ShapeDtypeStruct((2, s, d), F32)],
        scratch_shapes=([pltpu.VMEM((2 * SUPER, PAIR), F32)] * 4 + [pltpu.VMEM((SUPER, PAIR), F32)]
                        + [pltpu.VMEM((4 * len(DILATIONS), L, 2 * L), F32)]),
        compiler_params=_params(("parallel", "arbitrary")),
    )(*ins)
    return res[0], res[1]


def _coords():
    return lax.axis_index("x"), lax.axis_index("y"), lax.axis_index("c")


def _chip_peers(x, y):
    return [(1 - x, y), (x, 1 - y), (1 - x, 1 - y)]


def _block_of(ref, axis, blk, size):
    start = pl.multiple_of(blk * size, size)
    if axis == 1:
        return ref.at[:, pl.ds(start, size)]
    return ref.at[pl.ds(start, size), :]


ANY = pl.BlockSpec(memory_space=pl.ANY)


HBM = pl.BlockSpec(memory_space=pltpu.HBM)
SEM = pl.BlockSpec(memory_space=pltpu.SEMAPHORE)
SPLIT = pltpu.CompilerParams(has_side_effects=pltpu.SideEffectType.DATAFLOW_SIDE_EFFECTING)


def _in_hbm(a):
    return pltpu.with_memory_space_constraint(a, pltpu.HBM)


def _thru(arrays):
    return [pltpu.HBM(a.shape, a.dtype) for a in arrays]


def _cast_place(w, layer, ax, dtype, after, name):
    _, k, n = w.shape
    t = _pick(k, (512, 256, 128))
    nb = k // t
    extra = [] if after is None else [after]

    def body(blk_ref, w_ref, *refs):
        b_ref, f_ref = refs[len(extra):]
        v = w_ref[...].astype(dtype)
        b_ref[...] = v
        f_ref[...] = v

    full_shape = (k, 4 * n) if ax == 1 else (4 * k, n)
    place = (lambda i, blk: (i, blk[0])) if ax == 1 else (lambda i, blk: (blk[0] * nb + i, 0))
    return pl.pallas_call(
        body, name=name,
        grid_spec=pltpu.PrefetchScalarGridSpec(
            num_scalar_prefetch=1, grid=(nb,),
            in_specs=[pl.BlockSpec((None, t, n), lambda i, blk: (layer, i, 0))] + [ANY] * len(extra),
            out_specs=[pl.BlockSpec((t, n), lambda i, blk: (i, 0)), pl.BlockSpec((t, n), place)]),
        out_shape=[jax.ShapeDtypeStruct((k, n), dtype), jax.ShapeDtypeStruct(full_shape, dtype)],
        compiler_params=_params(("parallel",)),
    )(_my_block()[None], w, *extra)


def _my_block():
    return (2 * lax.axis_index("x") + lax.axis_index("y")).astype(jnp.int32)


def _gather_start(group, carry, name):
    n, nc = len(group), len(carry)

    def body(*refs):
        blocks, fulls, send_sem, recv_sem = refs[:n], refs[n:2 * n], refs[2 * n + nc], refs[2 * n + nc + 1]
        x, y, c = _coords()
        for t, (b, _, ax) in enumerate(group):
            mine = _block_of(fulls[t], ax, 2 * x + y, b.shape[ax])
            for j, (px, py) in enumerate(_chip_peers(x, y)):
                pltpu.make_async_remote_copy(
                    src_ref=blocks[t], dst_ref=mine, send_sem=send_sem.at[3 * t + j], recv_sem=recv_sem.at[3 * t + j],
                    device_id=(px, py, c), device_id_type=MESH).start()

    arrays = [b for b, _, _ in group] + [f for _, f, _ in group] + list(carry)
    sems = [pltpu.SemaphoreType.DMA((3 * n,))] * 2
    res = pl.pallas_call(
        body, name=name, in_specs=[HBM] * len(arrays), out_specs=[SEM, SEM] + [HBM] * len(arrays),
        out_shape=sems + _thru(arrays), input_output_aliases={i: 2 + i for i in range(len(arrays))},
        compiler_params=SPLIT,
    )(*[_in_hbm(a) for a in arrays])
    return (res[0], res[1], list(res[2:2 + n]), list(res[2 + n:2 + 2 * n])), list(res[2 + 2 * n:])


def _gather_wait(group, started, after, name):
    sends, recvs, blocks, fulls = started
    m = len(group)

    def body(*refs):
        blk_refs, full_refs, send_sem, recv_sem = refs[:m], refs[m:2 * m], refs[2 * m], refs[2 * m + 1]
        x, y, c = _coords()
        for t, (b, _, ax) in enumerate(group):
            for j, (px, py) in enumerate(_chip_peers(x, y)):
                cp = pltpu.make_async_remote_copy(
                    src_ref=blk_refs[t], dst_ref=_block_of(full_refs[t], ax, 2 * px + py, b.shape[ax]),
                    send_sem=send_sem.at[3 * t + j], recv_sem=recv_sem.at[3 * t + j],
                    device_id=(px, py, c), device_id_type=MESH)
                cp.wait_send()
                cp.wait_recv()

    extra = [] if after is None else [after]
    res = pl.pallas_call(
        body, name=name, in_specs=[HBM] * (2 * m) + [SEM, SEM] + [ANY] * len(extra), out_specs=[HBM] * (2 * m),
        out_shape=_thru(blocks) + _thru(fulls), input_output_aliases={i: i for i in range(2 * m)},
        compiler_params=SPLIT,
    )(*blocks, *fulls, sends, recvs, *extra)
    return list(res[m:])


def _scatter_start(grads, carry, name):
    n = len(grads)
    n_in = 2 * n + len(carry)

    def body(*refs):
        g_refs, st_refs, send_sem, recv_sem = refs[:n], refs[n:2 * n], refs[n_in], refs[n_in + 1]
        x, y, c = _coords()
        for t, (g, ax) in enumerate(grads):
            for j, (px, py) in enumerate(_chip_peers(x, y)):
                pltpu.make_async_remote_copy(
                    src_ref=_block_of(g_refs[t], ax, 2 * px + py, g.shape[ax] // 4), dst_ref=st_refs[t].at[j],
                    send_sem=send_sem.at[3 * t + j], recv_sem=recv_sem.at[3 * t + j],
                    device_id=(px, py, c), device_id_type=MESH).start()

    arrays = [g for g, _ in grads]
    for g, ax in grads:
        shape = list(g.shape)
        shape[ax] //= 4
        arrays.append(lax.empty((3, *shape), g.dtype))
    arrays += list(carry)
    sems = [pltpu.SemaphoreType.DMA((3 * n,))] * 2
    res = pl.pallas_call(
        body, name=name, in_specs=[HBM] * n_in, out_specs=[SEM, SEM] + [HBM] * n_in,
        out_shape=sems + _thru(arrays), input_output_aliases={i: 2 + i for i in range(n_in)},
        compiler_params=SPLIT,
    )(*[_in_hbm(a) for a in arrays])
    return (res[0], res[1], list(res[2:2 + n]), list(res[2 + n:2 + 2 * n])), list(res[2 + 2 * n:])


def _scatter_wait(axes, started, after, name):
    sends, recvs, full, stacks = started
    n = len(full)
    extra = [] if after is None else [after]

    def body(*refs):
        g_refs, st_refs, send_sem, recv_sem = refs[:n], refs[n:2 * n], refs[2 * n], refs[2 * n + 1]
        x, y, c = _coords()
        for t, ax in enumerate(axes):
            size = full[t].shape[ax] // 4
            for j, (px, py) in enumerate(_chip_peers(x, y)):
                cp = pltpu.make_async_remote_copy(
                    src_ref=_block_of(g_refs[t], ax, 2 * px + py, size), dst_ref=st_refs[t].at[j],
                    send_sem=send_sem.at[3 * t + j], recv_sem=recv_sem.at[3 * t + j],
                    device_id=(px, py, c), device_id_type=MESH)
                cp.wait_send()
                cp.wait_recv()

    res = pl.pallas_call(
        body, name=name, in_specs=[HBM] * (2 * n) + [SEM, SEM] + [ANY] * len(extra), out_specs=[HBM] * (2 * n),
        out_shape=_thru(full) + _thru(stacks), input_output_aliases={i: i for i in range(2 * n)},
        compiler_params=SPLIT,
    )(*full, *stacks, sends, recvs, *extra)
    return list(res[:n]), list(res[n:])


def _pair_copies(g_refs, st_refs, out_refs, items, send_sem, recv_sem):
    x, y, c = _coords()
    copies = []
    for u, (g, ax, _) in enumerate(items):
        own = _block_of(g_refs[u], ax, 2 * x + y, g.shape[ax] // 4)
        for k, (src, dst) in enumerate([(own, out_refs[u].at[0]), (st_refs[u], out_refs[u].at[pl.ds(1, 3)])]):
            copies.append(pltpu.make_async_remote_copy(
                src_ref=src, dst_ref=dst, send_sem=send_sem.at[2 * u + k], recv_sem=recv_sem.at[2 * u + k],
                device_id=(x, y, 1 - c), device_id_type=MESH))
    return copies


def _pair_start(items, carry, name):
    n = len(items)
    n_in = 3 * n + len(carry)

    def body(*refs):
        for cp in _pair_copies(refs[:n], refs[n:2 * n], refs[2 * n:3 * n], items, refs[n_in], refs[n_in + 1]):
            cp.start()

    arrays = ([g for g, _, _ in items] + [st for _, _, st in items]
              + [lax.empty((4, *st.shape[1:]), st.dtype) for _, _, st in items] + list(carry))
    sems = [pltpu.SemaphoreType.DMA((2 * n,))] * 2
    res = pl.pallas_call(
        body, name=name, in_specs=[HBM] * n_in, out_specs=[SEM, SEM] + [HBM] * n_in,
        out_shape=sems + _thru(arrays), input_output_aliases={i: 2 + i for i in range(n_in)},
        compiler_params=SPLIT,
    )(*[_in_hbm(a) for a in arrays])
    thru = res[2:]
    return (res[0], res[1], *(list(thru[k * n:(k + 1) * n]) for k in range(3))), list(thru[3 * n:])


def _pair_wait(axes, started, name):
    send, recv, full, stacks, landing = started
    n = len(full)
    items = [(full[u], axes[u], stacks[u]) for u in range(n)]

    def body(*refs):
        for cp in _pair_copies(refs[:n], refs[n:2 * n], refs[2 * n:3 * n], items, refs[3 * n], refs[3 * n + 1]):
            cp.wait_send()
            cp.wait_recv()

    res = pl.pallas_call(
        body, name=name, in_specs=[HBM] * (3 * n) + [SEM, SEM], out_specs=[HBM] * (3 * n),
        out_shape=_thru(full + stacks + landing), input_output_aliases={i: i for i in range(3 * n)},
        compiler_params=SPLIT,
    )(*full, *stacks, *landing, send, recv)
    return list(res[:n]), list(res[n:2 * n]), list(res[2 * n:])


def _allreduce_small(v, name):
    r, cdim = v.shape

    def body(v_ref, out_ref, buf, send_sems, recv_sems):
        x, y, c = _coords()
        me = 4 * x + 2 * y + c
        buf[0] = v_ref[...]
        sends = []
        for k in range(1, 8):
            peer = (x if not (k & 4) else 1 - x, y if not (k & 2) else 1 - y, c if not (k & 1) else 1 - c)
            cp = pltpu.make_async_remote_copy(
                src_ref=v_ref, dst_ref=buf.at[k], send_sem=send_sems.at[k - 1], recv_sem=recv_sems.at[k - 1],
                device_id=peer, device_id_type=MESH)
            cp.start()
            sends.append(cp)
        for cp in sends:
            cp.wait_recv()
        total = buf[me]
        for src in range(1, 8):
            total = total + buf[jnp.bitwise_xor(me, src)]
        out_ref[...] = total
        for cp in sends:
            cp.wait_send()

    return pl.pallas_call(
        body, name=name,
        in_specs=[pl.BlockSpec(memory_space=pltpu.VMEM)], out_specs=pl.BlockSpec(memory_space=pltpu.VMEM),
        out_shape=jax.ShapeDtypeStruct((r, cdim), F32),
        scratch_shapes=[pltpu.VMEM((8, r, cdim), F32), pltpu.SemaphoreType.DMA((7,)), pltpu.SemaphoreType.DMA((7,))],
        compiler_params=pltpu.CompilerParams(has_side_effects=True),
    )(v)


def _adamw_math(w, g, m, v):
    m = ADAM_B1 * m + (1.0 - ADAM_B1) * g
    v = ADAM_B2 * v + (1.0 - ADAM_B2) * jnp.square(g)
    m_hat = m / (1.0 - ADAM_B1 ** ADAM_STEP)
    v_hat = v / (1.0 - ADAM_B2 ** ADAM_STEP)
    delta = -ADAM_LR * (m_hat / (jnp.sqrt(v_hat) + ADAM_EPS) + ADAM_WD * w)
    return delta, m, v


def _adamw(w, m, v, grads, name):
    r, cdim = w.shape
    paired = isinstance(grads, list)
    layers = len(grads) if paired else 1
    t = _pick(r // layers, (128, 64, 32, 16, 8))
    per_layer = r // layers // t
    n_grad = 3 * layers if paired else 1

    def body(*refs):
        refs = refs[1:] if paired else refs
        w_ref, m_ref, v_ref = refs[:3]
        outs = refs[3 + n_grad:]

        def update(g):
            delta, m_new, v_new = _adamw_math(w_ref[...], g, m_ref[...], v_ref[...])
            outs[0][...] = g
            outs[1][...] = delta
            outs[2][...] = m_new
            outs[3][...] = v_new

        if not paired:
            update(refs[3][...])
            return
        layer = pl.program_id(0) // per_layer
        for l in range(layers):
            @pl.when(layer == l)
            def _(own_ref=refs[3 + 3 * l], st_ref=refs[4 + 3 * l], sib_ref=refs[5 + 3 * l]):
                sa = own_ref[...].astype(F32)
                sb = sib_ref[0].astype(F32)
                for k in range(3):
                    sa = sa + st_ref[k].astype(F32)
                    sb = sb + sib_ref[k + 1].astype(F32)
                update(sa + sb)

    out_shape = [jax.ShapeDtypeStruct((r, cdim), F32)] * 4
    if not paired:
        spec = pl.BlockSpec((t, cdim), lambda i: (i, 0))
        return pl.pallas_call(
            body, name=name, grid=(r // t,), in_specs=[spec] * 4, out_specs=[spec] * 4, out_shape=out_shape,
            compiler_params=_params(("parallel",)),
        )(w, m, v, grads)

    spec = pl.BlockSpec((t, cdim), lambda i, blk: (i, 0))
    ins, in_specs = [w, m, v], [spec] * 3
    for l, (g, ax, stack, sib) in enumerate(grads):
        row = lambda i, l=l: jnp.clip(i - l * per_layer, 0, per_layer - 1)
        own = ((lambda i, blk, row=row: (row(i), blk[0])) if ax == 1
               else (lambda i, blk, row=row: (blk[0] * per_layer + row(i), 0)))
        ins += [g, stack, sib]
        in_specs += [pl.BlockSpec((t, cdim), own),
                     pl.BlockSpec((3, t, cdim), lambda i, blk, row=row: (0, row(i), 0)),
                     pl.BlockSpec((4, t, cdim), lambda i, blk, row=row: (0, row(i), 0))]
    return pl.pallas_call(
        body, name=name,
        grid_spec=pltpu.PrefetchScalarGridSpec(
            num_scalar_prefetch=1, grid=(r // t,), in_specs=in_specs, out_specs=[spec] * 4),
        out_shape=out_shape, compiler_params=_params(("parallel",)),
    )(_my_block()[None], *ins)


def _local_step(x, target, gains, conv_ws, kv_gain, weights_of, send_grads):
    depth = len(gains)
    n_a = len(conv_ws)
    saved, ws = [], []
    kv = kvn = None
    _, (xn,) = _norm_res_fwd(x, None, None, [gains[0][0]], "norm_first")
    h = x
    for l in range(depth):
        g = gains[l]
        sv = {"x_in": h, "xn": xn}
        w = weights_of(l, "mix", h)
        ws.append(w)
        if l == n_a:
            kv = _mm(kvn, w["kv"], "nn", F32, "kv_fwd")
        if l < n_a:
            p = _mm(xn, w["conv_in"], "nn", BF16, f"conv_in_fwd_{l}")
            z = _conv_gate_fwd(p, conv_ws[l], f"conv_gate_fwd_{l}")
            mix, x1, xn2 = _mm_norm_res(z, w["conv_out"], h, g[1], g[2], f"conv_out_fwd_{l}")
            sv.update(p=p, z=z)
        else:
            j = l - n_a
            q = _mm(xn, w["q"], "nn", F32, f"q_fwd_{j}", scale=HEAD_DIM ** -0.5)
            o, lse = _attention_fwd(q, kv, f"attn_fwd_{j}")
            mix, x1, xn2 = _mm_norm_res(o, w["o"], h, g[1], g[2], f"o_fwd_{j}")
            sv.update(q=q, o=o, lse=lse)
        w.update(weights_of(l, "ffn", mix))
        f, a = _ffn_in_swiglu(xn2, w["ffn_in"], f"ffn_in_fwd_{l}")
        ff = _mm(a, w["ffn_out"], "nn", BF16, f"ffn_out_fwd_{l}")
        sv.update(mix=mix, x1=x1, xn2=xn2, f=f, a=a, ff=ff)
        saved.append(sv)
        if l == depth - 1:
            dx, loss = _norm_res_loss(x1, ff, g[3], target, "norm_loss")
        elif l == n_a - 1:
            h, _ = _norm_res_fwd(x1, ff, g[3], [], f"norm_end_{l}")
            h = _permute16(h, False, "permute_stream")
            _, (xn, kvn) = _norm_res_fwd(h, None, None, [gains[l + 1][0], kv_gain], "norm_permuted")
        else:
            h, (xn,) = _norm_res_fwd(x1, ff, g[3], [gains[l + 1][0]], f"norm_end_{l}")
    d_gains = [[None] * 4 for _ in range(depth)]
    d_conv = [None] * n_a
    d_kv_gain = None
    dkv = None
    _, _, dff, d_gains[depth - 1][3] = _norm_bwd(dx, [], None, (saved[-1]["ff"], gains[-1][3]), "norm_loss_bwd")
    for l in reversed(range(depth)):
        sv, g, w, grads = saved[l], gains[l], ws[l], {}
        grads["ffn_out"] =_mm(sv["a"], dff, "tn", BF16, f"ffn_out_dw_{l}")
        df = _ffn_out_dx_swiglu(dff, w["ffn_out"], sv["f"], f"ffn_out_dx_{l}")
        dxn2 = _mm(df, w["ffn_in"], "nt", BF16, f"ffn_in_dx_{l}")
        grads["ffn_in"] =_mm(sv["xn2"], df, "tn", BF16, f"ffn_in_dw_{l}")
        dx, (d_gains[l][2],), dmix, d_gains[l][1] = _norm_bwd(
            dx, [(dxn2, g[2])], sv["x1"], (sv["mix"], g[1]), f"norm_mid_bwd_{l}")
        dx, dmix = send_grads(l, "ffn", grads, [dx, dmix])
        if l < n_a:
            dz = _mm(dmix, w["conv_out"], "nt", BF16, f"conv_out_dx_{l}")
            grads["conv_out"] =_mm(sv["z"], dmix, "tn", BF16, f"conv_out_dw_{l}")
            dp, d_conv[l] = _conv_gate_bwd(sv["p"], dz, conv_ws[l], f"conv_gate_bwd_{l}")
            dxn = _mm(dp, w["conv_in"], "nt", BF16, f"conv_in_dx_{l}")
            grads["conv_in"] =_mm(sv["xn"], dp, "tn", BF16, f"conv_in_dw_{l}")
        else:
            j = l - n_a
            do = _mm(dmix, w["o"], "nt", F32, f"o_dx_{j}")
            grads["o"] =_mm(sv["o"], dmix, "tn", BF16, f"o_dw_{j}")
            dq, dkv = _attention_bwd(sv["q"], kv, sv["o"], do, sv["lse"], dkv, f"attn_bwd_{j}")
            scale = HEAD_DIM ** -0.5
            dxn = _mm(dq, w["q"], "nt", BF16, f"q_dx_{j}", scale=scale)
            grads["q"] =_mm(sv["xn"], dq, "tn", BF16, f"q_dw_{j}", scale=scale)
        branches = [(dxn, g[0])]
        if l == n_a:
            dkvn = _mm(dkv, w["kv"], "nt", BF16, "kv_dx")
            grads["kv"] =_mm(kvn, dkv, "tn", BF16, "kv_dw")
            branches.append((dkvn, kv_gain))
        post = (saved[l - 1]["ff"], gains[l - 1][3]) if l > 0 else None
        if l == n_a:
            dx, dgs, _, _ = _norm_bwd(dx, branches, sv["x_in"], None, f"norm_end_bwd_{l}")
            dx = _permute16(dx, True, "unpermute_stream")
            _, _, dff, dg_post = _norm_bwd(dx, [], None, post, "norm_boundary_bwd")
        else:
            dx, dgs, dff, dg_post = _norm_bwd(dx, branches, sv["x_in"], post, f"norm_end_bwd_{l}")
        if dff is None:
            send_grads(l, "mix", grads, [])
        else:
            dx, dff = send_grads(l, "mix", grads, [dx, dff])
        d_gains[l][0] = dgs[0]
        if l == n_a:
            d_kv_gain = dgs[1]
        if l > 0:
            d_gains[l - 1][3] = dg_post
    return loss, dx, d_gains, d_conv, d_kv_gain


BIG = (
    ("conv_in", 1), ("conv_out", 0), ("kv", 1), ("q", 0), ("o", 0), ("ffn_in", 1), ("ffn_out", 0))


def kernel(x, norm_g, conv_in_w, conv_w, conv_out_w, kv_norm_g, kv_w, q_w, o_w, ffn_in_w, ffn_out_w, loss_target, m_norm_g, m_conv_in_w, m_conv_w, m_conv_out_w, m_kv_norm_g, m_kv_w, m_q_w, m_o_w, m_ffn_in_w, m_ffn_out_w, v_norm_g, v_conv_in_w, v_conv_w, v_conv_out_w, v_kv_norm_g, v_kv_w, v_q_w, v_o_w, v_ffn_in_w, v_ffn_out_w):
    depth, _, dq = norm_g.shape
    d = 4 * dq
    n_a = conv_w.shape[0]
    big_w = {"conv_in": conv_in_w, "conv_out": conv_out_w, "kv": kv_w[None], "q": q_w, "o": o_w,
             "ffn_in": ffn_in_w, "ffn_out": ffn_out_w}
    big_m = {"conv_in": m_conv_in_w, "conv_out": m_conv_out_w, "kv": m_kv_w[None], "q": m_q_w, "o": m_o_w,
             "ffn_in": m_ffn_in_w, "ffn_out": m_ffn_out_w}
    big_v = {"conv_in": v_conv_in_w, "conv_out": v_conv_out_w, "kv": v_kv_w[None], "q": v_q_w, "o": v_o_w,
             "ffn_in": v_ffn_in_w, "ffn_out": v_ffn_out_w}

    n_gain, n_tap = depth * 4, n_a * conv_w.shape[1]
    small_rows = -(-(n_gain + n_tap + 1) // 8) * 8
    pad_rows = small_rows - n_gain - n_tap

    def pack_small(gains, taps):
        return jnp.concatenate([gains.reshape(n_gain, dq), taps.reshape(n_tap, dq), jnp.zeros((pad_rows, dq), F32)])

    axis_of = dict(BIG)

    def matrices_of(l, part):
        if part == "ffn":
            return [("ffn_in", l), ("ffn_out", l)]
        if l < n_a:
            return [("conv_in", l), ("conv_out", l)]
        return ([("kv", 0)] if l == n_a else []) + [("q", l - n_a), ("o", l - n_a)]

    halves = [(l, part) for l in range(depth) for part in ("mix", "ffn")]
    def placed(half, after):
        return [(*_cast_place(big_w[name], i, axis_of[name], BF16, after, f"place_{name}_{i}"), axis_of[name])
                for name, i in matrices_of(*half)]

    groups = {halves[0]: placed(halves[0], None)}
    groups[halves[0]].append((*_cast_place(pack_small(norm_g, conv_w)[None], 0, 1, F32, None, "place_small"), 1))
    started = {halves[0]: _gather_start(groups[halves[0]], [], "gather_start_0_mix")[0]}
    for half in halves[1:]:
        groups[half] = placed(half, started[halves[0]][2][0])

    def fetch(half, after):
        full = _gather_wait(groups[half], started[half], after, "gather_wait_%d_%s" % half)
        nxt = halves.index(half) + 1
        if nxt < len(halves):
            started[halves[nxt]], full = _gather_start(groups[halves[nxt]], full, "gather_start_%d_%s" % halves[nxt])
        return full

    target = _permute16(loss_target.reshape(x.shape[1:]), False, "permute_target")
    first = fetch(halves[0], target)
    small = first[-1]
    gains = [[small[4 * l + i][None] for i in range(4)] for l in range(depth)]
    conv_ws = [small[n_gain + 3 * l:n_gain + 3 * l + 3] for l in range(n_a)]
    kv_gain = kv_norm_g[None]

    def weights_of(l, part, after):
        full = first if (l, part) == halves[0] else fetch((l, part), after)
        return {name: full[t] for t, (name, _) in enumerate(matrices_of(l, part))}

    sent, paired = {}, {}
    LAG = 2

    def to_sibling(half, carry, after):
        axes = [axis_of[name] for name, _ in matrices_of(*half)]
        full, stacks = _scatter_wait(axes, sent[half], after, "scatter_wait_%d_%s" % half)
        paired[half], carry = _pair_start(list(zip(full, axes, stacks)), carry, "pair_start_%d_%s" % half)
        return carry

    def send_grads(l, part, grads, carry):
        sent[l, part], carry = _scatter_start(
            [(grads[name], axis_of[name]) for name, _ in matrices_of(l, part)], carry, f"scatter_start_{l}_{part}")
        at = halves.index((l, part))
        if carry:
            if at + LAG < len(halves):
                carry = to_sibling(halves[at + LAG], carry, carry[0])
        else:
            for older in range(at + LAG, at, -1):
                to_sibling(halves[older], [], sent[l, part][2][0])
        return carry

    loss, dx, d_gains, d_conv, d_kv_gain = _local_step(
        x.reshape(x.shape[1:]), target, gains, conv_ws, kv_gain, weights_of, send_grads)
    loss = lax.psum(loss, ("x", "y", "c"))

    small_g = jnp.concatenate([dg for row in d_gains for dg in row] + list(d_conv) + [d_kv_gain]
                              + [jnp.zeros((pad_rows - 1, d), F32)])
    small_g = _allreduce_small(small_g, "allreduce_small")
    blk = 2 * lax.axis_index("x") + lax.axis_index("y")
    mine_small = lax.dynamic_slice_in_dim(small_g, blk * dq, dq, axis=1)
    kv_rows = d // dq

    def pack_opt(gains_like, taps_like, kv_like):
        rows = jnp.concatenate([gains_like.reshape(n_gain, dq), taps_like.reshape(n_tap, dq), kv_like.reshape(kv_rows, dq)])
        extra = -rows.shape[0] % 8
        return jnp.concatenate([rows, jnp.zeros((extra, dq), F32)]) if extra else rows

    sw = pack_opt(norm_g, conv_w, kv_norm_g)
    sm = pack_opt(m_norm_g, m_conv_w, m_kv_norm_g)
    sv = pack_opt(v_norm_g, v_conv_w, v_kv_norm_g)
    sg = pack_opt(mine_small[:n_gain], mine_small[n_gain:n_gain + n_tap], small_g[n_gain + n_tap])
    s_out = _adamw(sw, sm, sv, sg, "adamw_small")

    def unpack(a):
        return (a[:n_gain].reshape(depth, 4, dq), a[n_gain:n_gain + n_tap].reshape(n_a, -1, dq),
                a[n_gain + n_tap:n_gain + n_tap + kv_rows].reshape(d))

    small_out = [unpack(a) for a in s_out]

    landed, big_out = {}, {}

    def update(name):
        for half in halves:
            if matrices_of(*half)[0] not in landed and any(n == name for n, _ in matrices_of(*half)):
                axes = [axis_of[n] for n, _ in matrices_of(*half)]
                landed.update(zip(matrices_of(*half), zip(*_pair_wait(axes, paired[half], "pair_wait_%d_%s" % half))))
        shp, ax = big_w[name].shape, axis_of[name]
        rows, cols = shp[0] * shp[1], shp[2]
        flat = lambda a: a.reshape(rows, cols)
        full, stacks, sibling = zip(*[landed[name, i] for i in range(shp[0])])
        res = _adamw(flat(big_w[name]), flat(big_m[name]), flat(big_v[name]),
                     [(full[i], ax, stacks[i], sibling[i]) for i in range(shp[0])], f"adamw_{name}")
        big_out[name] = [a.reshape(shp[1:] if name == "kv" else shp) for a in res]

    pending = [half for half in reversed(halves) if half not in paired]
    for half in pending[:-1]:
        to_sibling(half, [], None)
    late = [name for name, _ in BIG if any(n == name for n, _ in matrices_of(*pending[-1]))]
    for name, _ in BIG:
        if name not in late:
            update(name)
    to_sibling(pending[-1], [], big_out["ffn_out"][0])
    for name in late:
        update(name)

    def leaves(i):
        ng, cw_, kg = small_out[i]
        return [ng, big_out["conv_in"][i], cw_, big_out["conv_out"][i], kg, big_out["kv"][i], big_out["q"][i],
                big_out["o"][i], big_out["ffn_in"][i], big_out["ffn_out"][i]]

    return (loss, dx.reshape(x.shape), *leaves(0), *leaves(1), *leaves(2), *leaves(3))
```

```python
import jax
import jax.numpy as jnp
import numpy as np
from jax import lax
from jax.experimental import pallas as pl
from jax.experimental.pallas import tpu as pltpu

F32 = jnp.float32
BF16 = jnp.bfloat16
HEAD_DIM = 64
DILATIONS = (1, 4, 16)
NORM_EPS = 1e-6
NEG_BIG = -1e30
VMEM_LIMIT = 48 * 1024 * 1024
ROW_TILE = 256
NORM_TILE = 512
LANE = 128
MESH = pl.DeviceIdType.MESH

ADAM_LR = 0.001
ADAM_B1 = 0.9
ADAM_B2 = 0.999
ADAM_EPS = 1e-08
ADAM_WD = 0.01
ADAM_STEP = 10

TILE_CANDIDATES = (1024, 1408, 768, 512, 384, 256, 128)


def _pick(dim, cands=TILE_CANDIDATES):
    for c in cands:
        if c <= dim and dim % c == 0:
            return c
    return dim


def _params(sem):
    return pltpu.CompilerParams(dimension_semantics=sem, vmem_limit_bytes=VMEM_LIMIT)


def _mm(a, b, mode, out_dtype, name, scale=None):
    a_planes = a.shape[0] if a.ndim == 3 else 1
    b_planes = b.shape[0] if b.ndim == 3 else 1
    if mode == "nn":
        m, k = a.shape[-2], a.shape[-1] * a_planes
        n = b.shape[1]
    elif mode == "nt":
        m, k = a.shape[-2], a.shape[-1] * a_planes
        n = b.shape[0]
    else:
        k, m = a.shape
        n = b.shape[-1] * b_planes
    tm, tn = _pick(m), _pick(n // b_planes)
    tk = _pick(k // a_planes, ((2048,) if mode == "tn" else (3072, 2816)) + TILE_CANDIDATES)
    nk = k // tk
    ka, nb = k // a_planes // tk, n // b_planes // tn
    if a_planes > 1:
        a_spec = pl.BlockSpec((None, tm, tk), lambda i, j, kk: (kk // ka, i, kk % ka))
    elif mode == "tn":
        a_spec = pl.BlockSpec((tk, tm), lambda i, j, kk: (kk, i))
    else:
        a_spec = pl.BlockSpec((tm, tk), lambda i, j, kk: (i, kk))
    if mode == "nn":
        b_spec = pl.BlockSpec((tk, tn), lambda i, j, kk: (kk, j))
        dims = (((1,), (0,)), ((), ()))
    elif mode == "nt":
        b_spec = pl.BlockSpec((tn, tk), lambda i, j, kk: (j, kk))
        dims = (((1,), (1,)), ((), ()))
    else:
        b_spec = (pl.BlockSpec((None, tk, tn), lambda i, j, kk: (j // nb, kk, j % nb)) if b_planes > 1
                  else pl.BlockSpec((tk, tn), lambda i, j, kk: (kk, j)))
        dims = (((0,), (0,)), ((), ()))

    def finish(acc):
        if scale is not None:
            acc = acc * scale
        return acc.astype(out_dtype)

    if nk == 1:
        def body(a_ref, b_ref, o_ref):
            o_ref[...] = finish(lax.dot_general(a_ref[...].astype(BF16), b_ref[...].astype(BF16), dims, preferred_element_type=F32))
        scratch = []
    else:
        def body(a_ref, b_ref, o_ref, acc_ref):
            kk = pl.program_id(2)

            @pl.when(kk == 0)
            def _():
                acc_ref[...] = jnp.zeros_like(acc_ref)

            acc_ref[...] += lax.dot_general(a_ref[...].astype(BF16), b_ref[...].astype(BF16), dims, preferred_element_type=F32)

            @pl.when(kk == nk - 1)
            def _():
                o_ref[...] = finish(acc_ref[...])
        scratch = [pltpu.VMEM((tm, tn), F32)]

    return pl.pallas_call(
        body, name=name,
        grid=(m // tm, n // tn, nk),
        in_specs=[a_spec, b_spec],
        out_specs=pl.BlockSpec((tm, tn), lambda i, j, kk: (i, j)),
        out_shape=jax.ShapeDtypeStruct((m, n), out_dtype),
        scratch_shapes=scratch,
        compiler_params=_params(("parallel", "parallel", "arbitrary")),
    )(a, b)


def _rstd(v):
    return lax.rsqrt(jnp.mean(v * v, axis=-1, keepdims=True) + NORM_EPS)


def _rms_bwd(dy, v, g, r):
    gy = dy * g
    dv = r * (gy - v * (r * r) * jnp.mean(gy * v, axis=-1, keepdims=True))
    return dv, dy * v * r


def _row_spec(t, width):
    return pl.BlockSpec((t, width), lambda i: (i, 0))


def _gain_spec(width):
    return pl.BlockSpec((1, width), lambda i: (0, 0))


def _norm_res_fwd(x, mix, g_post, pre_gains, name):
    s, d = x.shape
    t = _pick(s, (NORM_TILE, ROW_TILE))
    has_mix = mix is not None
    n_pre = len(pre_gains)

    def body(*refs):
        x_ref = refs[0]
        pos = 1
        x1 = x_ref[...]
        if has_mix:
            mv = refs[1][...].astype(F32)
            x1 = x1 + mv * _rstd(mv) * refs[2][...]
            pos = 3
        gains = refs[pos:pos + n_pre]
        outs = refs[pos + n_pre:]
        if has_mix:
            outs[0][...] = x1
            outs = outs[1:]
        r = _rstd(x1)
        for g_ref, o_ref in zip(gains, outs):
            o_ref[...] = (x1 * r * g_ref[...]).astype(BF16)

    ins = [x] + ([mix, g_post] if has_mix else []) + list(pre_gains)
    in_specs = [_row_spec(t, d)] + ([_row_spec(t, d), _gain_spec(d)] if has_mix else []) + [_gain_spec(d)] * n_pre
    out_shape = ([jax.ShapeDtypeStruct((s, d), F32)] if has_mix else []) + [jax.ShapeDtypeStruct((s, d), BF16)] * n_pre
    out_specs = [_row_spec(t, d)] * len(out_shape)
    res = pl.pallas_call(
        body, name=name, grid=(s // t,), in_specs=in_specs, out_specs=out_specs, out_shape=out_shape,
        compiler_params=_params(("parallel",)),
    )(*ins)
    if has_mix:
        return res[0], list(res[1:])
    return x, list(res)


def _mm_norm_res(a, b, x, g_post, pre_gain, name):
    m, k = a.shape
    d = b.shape[1]
    tm, tk = _pick(m, (1024, 512, 256)), _pick(k)
    nk = k // tk

    def body(a_ref, b_ref, x_ref, gp_ref, g_ref, mix_ref, x1_ref, xn_ref, acc_ref):
        kk = pl.program_id(1)

        @pl.when(kk == 0)
        def _():
            acc_ref[...] = jnp.zeros_like(acc_ref)

        acc_ref[...] += jnp.dot(a_ref[...].astype(BF16), b_ref[...], preferred_element_type=F32)

        @pl.when(kk == nk - 1)
        def _():
            mix = acc_ref[...].astype(BF16)
            mix_ref[...] = mix
            mv = mix.astype(F32)
            x1 = x_ref[...] + mv * _rstd(mv) * gp_ref[...]
            x1_ref[...] = x1
            xn_ref[...] = (x1 * _rstd(x1) * g_ref[...]).astype(BF16)

    rows = pl.BlockSpec((tm, d), lambda i, kk: (i, 0))
    gain = pl.BlockSpec((1, d), lambda i, kk: (0, 0))
    return pl.pallas_call(
        body, name=name, grid=(m // tm, nk),
        in_specs=[pl.BlockSpec((tm, tk), lambda i, kk: (i, kk)), pl.BlockSpec((tk, d), lambda i, kk: (kk, 0)),
                  rows, gain, gain],
        out_specs=[rows, rows, rows],
        out_shape=[jax.ShapeDtypeStruct((m, d), BF16), jax.ShapeDtypeStruct((m, d), F32),
                   jax.ShapeDtypeStruct((m, d), BF16)],
        scratch_shapes=[pltpu.VMEM((tm, d), F32)],
        compiler_params=_params(("parallel", "arbitrary")),
    )(a, b, x, g_post, pre_gain)


def _norm_res_loss(x, mix, g_post, target, name):
    s, d = x.shape
    t = _pick(s, (NORM_TILE, ROW_TILE))

    def body(x_ref, m_ref, g_ref, t_ref, dy_ref, loss_ref):
        mv = m_ref[...].astype(F32)
        y = x_ref[...] + mv * _rstd(mv) * g_ref[...]
        err = y - t_ref[...]
        dy_ref[...] = err * (1.0 / d)

        @pl.when(pl.program_id(0) == 0)
        def _():
            loss_ref[...] = jnp.zeros_like(loss_ref)

        loss_ref[...] += jnp.sum(err * err)

    dy, acc = pl.pallas_call(
        body, name=name, grid=(s // t,),
        in_specs=[_row_spec(t, d), _row_spec(t, d), _gain_spec(d), _row_spec(t, d)],
        out_specs=[_row_spec(t, d), pl.BlockSpec((8, LANE), lambda i: (0, 0))],
        out_shape=[jax.ShapeDtypeStruct((s, d), F32), jax.ShapeDtypeStruct((8, LANE), F32)],
        compiler_params=_params(("arbitrary",)),
    )(x, mix, g_post, target)
    return dy, acc[0, 0] * (0.5 / d)


def _norm_bwd(dx_out, branches, x_in, post, name):
    s, d = dx_out.shape
    t = _pick(s, (NORM_TILE, ROW_TILE))
    nb = len(branches)
    has_post = post is not None

    def body(*refs):
        dx_ref = refs[0]
        pos = 1
        dx = dx_ref[...]
        first = pl.program_id(0) == 0
        n_in = 1 + (1 + 2 * nb if nb else 0) + (2 if has_post else 0)
        outs = refs[n_in:]
        opos = 0
        if nb:
            xv = refs[pos][...]
            pos += 1
            r = _rstd(xv)
            dx_o = outs[0]
            opos = 1
            for _ in range(nb):
                dxn = refs[pos][...].astype(F32)
                g = refs[pos + 1][...]
                pos += 2
                dv, dg_rows = _rms_bwd(dxn, xv, g, r)
                dx = dx + dv
                dg_ref = outs[opos]
                opos += 1

                @pl.when(first)
                def _(dg_ref=dg_ref):
                    dg_ref[...] = jnp.zeros_like(dg_ref)

                dg_ref[...] += jnp.sum(dg_rows, axis=0, keepdims=True)
            dx_o[...] = dx
        if has_post:
            mv = refs[pos][...].astype(F32)
            g = refs[pos + 1][...]
            dm, dg_rows = _rms_bwd(dx, mv, g, _rstd(mv))
            outs[opos][...] = dm.astype(BF16)
            dg_ref = outs[opos + 1]

            @pl.when(first)
            def _():
                dg_ref[...] = jnp.zeros_like(dg_ref)

            dg_ref[...] += jnp.sum(dg_rows, axis=0, keepdims=True)

    ins, in_specs = [dx_out], [_row_spec(t, d)]
    out_shape, out_specs = [], []
    if nb:
        ins.append(x_in)
        in_specs.append(_row_spec(t, d))
        out_shape.append(jax.ShapeDtypeStruct((s, d), F32))
        out_specs.append(_row_spec(t, d))
        for dxn, g in branches:
            ins += [dxn, g]
            in_specs += [_row_spec(t, d), _gain_spec(d)]
            out_shape.append(jax.ShapeDtypeStruct((1, d), F32))
            out_specs.append(_gain_spec(d))
    if has_post:
        ins += [post[0], post[1]]
        in_specs += [_row_spec(t, d), _gain_spec(d)]
        out_shape += [jax.ShapeDtypeStruct((s, d), BF16), jax.ShapeDtypeStruct((1, d), F32)]
        out_specs += [_row_spec(t, d), _gain_spec(d)]
    res = pl.pallas_call(
        body, name=name, grid=(s // t,), in_specs=in_specs, out_specs=out_specs, out_shape=out_shape,
        compiler_params=_params(("arbitrary",)),
    )(*ins)
    res = list(res)
    dx_in = res.pop(0) if nb else dx_out
    dgs = [res.pop(0) for _ in range(nb)]
    dm, dg_post = (res[0], res[1]) if has_post else (None, None)
    return dx_in, dgs, dm, dg_post


HALO = 16


def _shift_down(u, prev, k):
    rows = lax.broadcasted_iota(jnp.int32, u.shape, 0)
    out = pltpu.roll(u, k, 0)
    for i in range(k):
        out = jnp.where(rows == i, prev[HALO - k + i:HALO - k + i + 1, :], out)
    return out


def _shift_up(u, nxt, k):
    n = u.shape[0]
    rows = lax.broadcasted_iota(jnp.int32, u.shape, 0)
    out = pltpu.roll(u, n - k, 0)
    for i in range(k):
        out = jnp.where(rows == n - k + i, nxt[i:i + 1, :], out)
    return out


def _conv_gate_fwd(p, cw, name):
    s, d3 = p.shape
    d = d3 // 3
    t = _pick(s, (ROW_TILE,))
    hb = t // HALO

    def body(p_ref, prev_ref, w_ref, z_ref):
        i = pl.program_id(0)
        pv = p_ref[...].astype(F32)
        b, u = pv[:, :d], pv[:, d:2 * d] * pv[:, 2 * d:]
        ph = prev_ref[...].astype(F32)
        up = jnp.where(i > 0, ph[:, d:2 * d] * ph[:, 2 * d:], 0.0)
        w = w_ref[...]
        y = w[0:1, :] * _shift_down(u, up, 2) + w[1:2, :] * _shift_down(u, up, 1) + w[2:3, :] * u
        z_ref[...] = (b * y).astype(BF16)

    return pl.pallas_call(
        body, name=name, grid=(s // t,),
        in_specs=[_row_spec(t, d3),
                  pl.BlockSpec((HALO, d3), lambda i: (jnp.maximum(i * hb - 1, 0), 0)),
                  pl.BlockSpec((3, d), lambda i: (0, 0))],
        out_specs=_row_spec(t, d),
        out_shape=jax.ShapeDtypeStruct((s, d), BF16),
        compiler_params=_params(("parallel",)),
    )(p, p, cw)


def _conv_gate_bwd(p, dz, cw, name):
    s, d3 = p.shape
    d = d3 // 3
    t = _pick(s, (ROW_TILE,))
    hb = t // HALO
    nt = s // t
    last_halo = s // HALO - 1

    def body(p_ref, prev_ref, next_ref, dz_ref, dznext_ref, w_ref, dp_ref, dw_ref):
        i = pl.program_id(0)
        pv = p_ref[...].astype(F32)
        b, c, h = pv[:, :d], pv[:, d:2 * d], pv[:, 2 * d:]
        u = c * h
        ph = prev_ref[...].astype(F32)
        up = jnp.where(i > 0, ph[:, d:2 * d] * ph[:, 2 * d:], 0.0)
        w = w_ref[...]
        u1, u2 = _shift_down(u, up, 1), _shift_down(u, up, 2)
        y = w[0:1, :] * u2 + w[1:2, :] * u1 + w[2:3, :] * u
        dz = dz_ref[...].astype(F32)
        dy = dz * b
        dyn = jnp.where(i < nt - 1, dznext_ref[...].astype(F32) * next_ref[...].astype(F32)[:, :d], 0.0)
        du = w[2:3, :] * dy + w[1:2, :] * _shift_up(dy, dyn, 1) + w[0:1, :] * _shift_up(dy, dyn, 2)
        dp_ref[:, :d] = (dz * y).astype(BF16)
        dp_ref[:, d:2 * d] = (du * h).astype(BF16)
        dp_ref[:, 2 * d:] = (du * c).astype(BF16)

        @pl.when(i == 0)
        def _():
            dw_ref[...] = jnp.zeros_like(dw_ref)

        dw_ref[0:1, :] += jnp.sum(dy * u2, axis=0, keepdims=True)
        dw_ref[1:2, :] += jnp.sum(dy * u1, axis=0, keepdims=True)
        dw_ref[2:3, :] += jnp.sum(dy * u, axis=0, keepdims=True)

    return pl.pallas_call(
        body, name=name, grid=(nt,),
        in_specs=[_row_spec(t, d3),
                  pl.BlockSpec((HALO, d3), lambda i: (jnp.maximum(i * hb - 1, 0), 0)),
                  pl.BlockSpec((HALO, d3), lambda i: (jnp.minimum((i + 1) * hb, last_halo), 0)),
                  _row_spec(t, d),
                  pl.BlockSpec((HALO, d), lambda i: (jnp.minimum((i + 1) * hb, last_halo), 0)),
                  pl.BlockSpec((3, d), lambda i: (0, 0))],
        out_specs=[_row_spec(t, d3), pl.BlockSpec((3, d), lambda i: (0, 0))],
        out_shape=[jax.ShapeDtypeStruct((s, d3), BF16), jax.ShapeDtypeStruct((3, d), F32)],
        compiler_params=_params(("arbitrary",)),
    )(p, p, p, dz, dz, cw)


FFN_ROWS, FFN_COLS = (512, 256), (1408, 768, 256, 128)


def _row_chunks(tm, rows=256):
    return [slice(r, r + min(rows, tm)) for r in range(0, tm, min(rows, tm))]


def _ffn_in_swiglu(xn, w_in, name):
    s, k = xn.shape
    ff = w_in.shape[1] // 2
    tm, tn = _pick(s, FFN_ROWS), _pick(ff, FFN_COLS)
    nj = ff // tn

    def body(x_ref, wg_ref, wu_ref, f_ref, a_ref):
        for rows in _row_chunks(tm):
            xv = x_ref[rows, :]
            gate = jnp.dot(xv, wg_ref[...], preferred_element_type=F32)
            up = jnp.dot(xv, wu_ref[...], preferred_element_type=F32)
            f_ref[0, rows, :] = gate.astype(BF16)
            f_ref[1, rows, :] = up.astype(BF16)
            a_ref[rows, :] = (gate * jax.nn.sigmoid(gate) * up).astype(BF16)

    return pl.pallas_call(
        body, name=name, grid=(nj, s // tm),
        in_specs=[pl.BlockSpec((tm, k), lambda j, i: (i, 0)),
                  pl.BlockSpec((k, tn), lambda j, i: (0, j)),
                  pl.BlockSpec((k, tn), lambda j, i: (0, nj + j))],
        out_specs=[pl.BlockSpec((2, tm, tn), lambda j, i: (0, i, j)), pl.BlockSpec((tm, tn), lambda j, i: (i, j))],
        out_shape=[jax.ShapeDtypeStruct((2, s, ff), BF16), jax.ShapeDtypeStruct((s, ff), BF16)],
        compiler_params=_params(("parallel", "parallel")),
    )(xn, w_in, w_in)


def _ffn_out_dx_swiglu(dff, w_out, f, name):
    s, d = dff.shape
    ff = w_out.shape[0]
    tm, tn = _pick(s, FFN_ROWS), _pick(ff, FFN_COLS)

    def body(d_ref, w_ref, f_ref, df_ref):
        for rows in _row_chunks(tm):
            da = lax.dot_general(d_ref[rows, :], w_ref[...], (((1,), (1,)), ((), ())), preferred_element_type=F32)
            gate = f_ref[0, rows, :].astype(F32)
            up = f_ref[1, rows, :].astype(F32)
            sg = jax.nn.sigmoid(gate)
            silu = gate * sg
            df_ref[0, rows, :] = (da * up * (sg + silu * (1.0 - sg))).astype(BF16)
            df_ref[1, rows, :] = (da * silu).astype(BF16)

    planes = pl.BlockSpec((2, tm, tn), lambda j, i: (0, i, j))
    return pl.pallas_call(
        body, name=name, grid=(ff // tn, s // tm),
        in_specs=[pl.BlockSpec((tm, d), lambda j, i: (i, 0)), pl.BlockSpec((tn, d), lambda j, i: (j, 0)), planes],
        out_specs=planes, out_shape=jax.ShapeDtypeStruct((2, s, ff), BF16),
        compiler_params=_params(("parallel", "parallel")),
    )(dff, w_out, f)


SUPER = 2048
RES = 16
PAIR = 128
L = 128
FWD_TOGETHER = 16
BWD_TOGETHER = 4


def _alibi_slopes(n_heads):
    h = np.arange(n_heads, dtype=np.float32) + 1.0
    return np.power(2.0, -8.0 * h / n_heads).astype(np.float32)


def _permute16(x, inverse, name):
    s, d = x.shape
    cw = LANE

    def body(x_ref, o_ref):
        if inverse:
            for m in range(L):
                o_ref[RES * m:RES * (m + 1), :] = x_ref[pl.ds(m, RES, stride=L), :]
        else:
            for r in range(RES):
                o_ref[L * r:L * (r + 1), :] = x_ref[pl.ds(r, L, stride=RES), :]

    spec = pl.BlockSpec((SUPER, cw), lambda i, j: (i, j))
    return pl.pallas_call(
        body, name=name, grid=(s // SUPER, d // cw), in_specs=[spec], out_specs=spec,
        out_shape=jax.ShapeDtypeStruct((s, d), x.dtype),
        compiler_params=_params(("parallel", "parallel")),
    )(x)


def _slope_table(d):
    nh = d // HEAD_DIM
    sl = _alibi_slopes(nh)
    tab = np.repeat(sl, HEAD_DIM).reshape(d // PAIR, 1, PAIR)
    return jnp.asarray(np.broadcast_to(tab, (d // PAIR, 8, PAIR)).copy())


def _geometry(dil):
    nch = RES // dil
    return nch, L // nch


def _band(dil):
    nch, w = _geometry(dil)
    sh = w.bit_length() - 1
    i = lax.broadcasted_iota(jnp.int32, (L, 2 * L), 0)
    j = lax.broadcasted_iota(jnp.int32, (L, 2 * L), 1)

    def pos(t):
        return jnp.bitwise_and(t, w - 1) * nch + jnp.right_shift(t, sh)

    delta = pos(i) + L - (pos(jnp.bitwise_and(j, L - 1)) + jnp.bitwise_and(j, L))
    return (delta * dil).astype(F32), (delta >= 0) & (delta <= L), j < L


def _fill_bias(bias_s, sl_ref):
    for b, dil in enumerate(DILATIONS):
        base, band, prev_half = _band(dil)
        for first in range(2):
            valid = band & jnp.logical_not(prev_half) if first else band
            for h in range(2):
                slope = sl_ref[0:1, HEAD_DIM * h:HEAD_DIM * h + 1]
                bias_s[(2 * b + first) * 2 + h] = jnp.where(valid, -slope * base, NEG_BIG)


def _bias_index(b, sb, n):
    first = jnp.logical_and(sb == 0, n == 0).astype(jnp.int32)
    return (2 * b + first) * 2


def _offsets(dil, res, n):
    nch, w = _geometry(dil)

    def al(v):
        return v if isinstance(v, int) else pl.multiple_of(v, w)

    q_off = [al((a * dil + res) * L + n * w) for a in range(nch)]
    k_off = [al((a * dil + res) * 2 * L + L + n * w) for a in range(nch)]
    kp_off = [al((a * dil + res) * 2 * L + L + n * w - w) for a in range(nch)]
    return q_off, k_off, kp_off, w


def _gather(ref, offs, w):
    parts = [ref[pl.ds(o, w), :] for o in offs]
    return parts[0] if len(parts) == 1 else jnp.concatenate(parts, axis=0)


def _scatter(ref, offs, w, val, add=False):
    for a, o in enumerate(offs):
        piece = val[a * w:(a + 1) * w, :]
        if add:
            ref[pl.ds(o, w), :] += piece
        else:
            ref[pl.ds(o, w), :] = piece


def _fill_key_buffer(buf, prev_ref, cur_ref):
    for r in range(RES):
        buf[2 * L * r:2 * L * r + L, :] = prev_ref[L * r:L * (r + 1), :]
        buf[2 * L * r + L:2 * L * (r + 1), :] = cur_ref[L * r:L * (r + 1), :]


def _two_heads(x, low):
    zero = jnp.zeros_like(x)
    return jnp.concatenate([jnp.where(low, x, zero), jnp.where(low, zero, x)], axis=0)


def _loop_blocks(dil, stages, together):
    together = max(together, dil) if dil < RES else together

    def it(i, c):
        if dil == RES:
            blocks = [(i * together + k, 0) for k in range(together)]
        else:
            blocks = [(res, i * (together // dil) + k) for k in range(together // dil) for res in range(dil)]
        state = [stages[0](res, n) for res, n in blocks]
        for stage in stages[1:]:
            state = [stage(res, n, prev) for (res, n), prev in zip(blocks, state)]
        for writes in state:
            for args in writes:
                _scatter(*args)
        return c

    lax.fori_loop(0, RES // together, it, 0)


NT = (((1,), (1,)), ((), ()))
TN = (((0,), (0,)), ((), ()))


def _attention_fwd(q, kv, name):
    s, d = q.shape
    g_n, ns = d // PAIR, s // SUPER

    def body(sl_ref, q_ref, kc_ref, kp_ref, vc_ref, vp_ref, o_ref, lse_ref, kbuf, vbuf, m_s, l_s, acc_s, bias_s):
        sb = pl.program_id(1)
        _fill_key_buffer(kbuf, kp_ref, kc_ref)
        _fill_key_buffer(vbuf, vp_ref, vc_ref)
        pl.when(sb == 0)(lambda: _fill_bias(bias_s, sl_ref))
        low = lax.broadcasted_iota(jnp.int32, (L, PAIR), 1) < HEAD_DIM
        low_k = lax.broadcasted_iota(jnp.int32, (2 * L, PAIR), 1) < HEAD_DIM
        ones_bd = _two_heads(jnp.ones((2 * L, PAIR), BF16), low_k)
        for bi, dil in enumerate(DILATIONS):
            first_branch, last_branch = bi == 0, bi == len(DILATIONS) - 1

            def scores(res, n, dil=dil):
                q_off, k_off, kp_off, w = _offsets(dil, res, n)
                qf = _gather(q_ref, q_off, w).astype(BF16)
                kcat = jnp.concatenate([_gather(kbuf, kp_off, w), _gather(kbuf, k_off, w)], axis=0).astype(BF16)
                return lax.dot_general(qf, _two_heads(kcat, low_k), NT, preferred_element_type=F32)

            def update(res, n, sc, bi=bi, dil=dil, first_branch=first_branch, last_branch=last_branch):
                q_off, k_off, kp_off, w = _offsets(dil, res, n)
                vcat = jnp.concatenate([_gather(vbuf, kp_off, w), _gather(vbuf, k_off, w)], axis=0).astype(BF16)
                v_ones = jnp.concatenate([_two_heads(vcat, low_k), ones_bd], axis=1)
                bias_at = _bias_index(bi, sb, n)
                if not first_branch:
                    m_prev = _gather(m_s, q_off, w)
                ps, m_new = [], []
                for h in range(2):
                    s_h = sc[:, 2 * L * h:2 * L * (h + 1)] + bias_s[bias_at + h]
                    mh = jnp.max(s_h, axis=1, keepdims=True)
                    if not first_branch:
                        mh = jnp.maximum(mh, m_prev[:, HEAD_DIM * h:HEAD_DIM * h + 1])
                    ps.append(jnp.exp(s_h - mh).astype(BF16))
                    m_new.append(mh)
                m_full = jnp.where(low, m_new[0], m_new[1])
                both = jnp.dot(jnp.concatenate(ps, axis=1), v_ones, preferred_element_type=F32)
                acc, l_full = both[:, :PAIR], both[:, PAIR:]
                if not first_branch:
                    alpha = jnp.exp(m_prev - m_full)
                    l_full = _gather(l_s, q_off, w) * alpha + l_full
                    acc = _gather(acc_s, q_off, w) * alpha + acc
                if last_branch:
                    return [(o_ref, q_off, w, acc / l_full, False), (lse_ref, q_off, w, m_full + jnp.log(l_full), False)]
                return [(m_s, q_off, w, m_full, False), (l_s, q_off, w, l_full, False), (acc_s, q_off, w, acc, False)]

            _loop_blocks(dil, [scores, update], FWD_TOGETHER)

    prev = lambda i: jnp.maximum(i - 1, 0)
    blk = pl.BlockSpec((SUPER, PAIR), lambda g, i: (i, g))
    in_specs = [pl.BlockSpec((None, 8, PAIR), lambda g, i: (g, 0, 0)), blk,
                pl.BlockSpec((SUPER, PAIR), lambda g, i: (i, g)),
                pl.BlockSpec((SUPER, PAIR), lambda g, i: (prev(i), g)),
                pl.BlockSpec((SUPER, PAIR), lambda g, i: (i, g_n + g)),
                pl.BlockSpec((SUPER, PAIR), lambda g, i: (prev(i), g_n + g))]
    return pl.pallas_call(
        body, name=name, grid=(g_n, ns), in_specs=in_specs, out_specs=[blk, blk],
        out_shape=[jax.ShapeDtypeStruct((s, d), F32)] * 2,
        scratch_shapes=([pltpu.VMEM((2 * SUPER, PAIR), F32)] * 2 + [pltpu.VMEM((SUPER, PAIR), F32)] * 3
                        + [pltpu.VMEM((4 * len(DILATIONS), L, 2 * L), F32)]),
        compiler_params=_params(("parallel", "arbitrary")),
    )(_slope_table(d), q, kv, kv, kv, kv)


def _attention_bwd(q, kv, o, do, lse, dkv_in, name):
    s, d = q.shape
    g_n, ns = d // PAIR, s // SUPER
    has_in = dkv_in is not None

    def body(*refs):
        sl_ref, q_ref, do_ref, o_ref, lse_ref, kc_ref, kp_ref, vc_ref, vp_ref = refs[:9]
        pos = 9
        if has_in:
            dkv_in_ref = refs[9]
            pos = 10
        dq_ref, dkv_ref, kbuf, vbuf, dkbuf, dvbuf, dq_s, bias_s = refs[pos:]
        step = pl.program_id(1)
        sb = ns - 1 - step
        _fill_key_buffer(kbuf, kp_ref, kc_ref)
        _fill_key_buffer(vbuf, vp_ref, vc_ref)

        @pl.when(step == 0)
        def _():
            _fill_bias(bias_s, sl_ref)
            dkbuf[...] = jnp.zeros_like(dkbuf)
            dvbuf[...] = jnp.zeros_like(dvbuf)

        @pl.when(step > 0)
        def _():
            for buf in (dkbuf, dvbuf):
                for r in range(RES):
                    buf[2 * L * r + L:2 * L * (r + 1), :] = buf[2 * L * r:2 * L * r + L, :]
                    buf[2 * L * r:2 * L * r + L, :] = jnp.zeros((L, PAIR), F32)

        low = lax.broadcasted_iota(jnp.int32, (L, PAIR), 1) < HEAD_DIM
        low_k = lax.broadcasted_iota(jnp.int32, (2 * L, PAIR), 1) < HEAD_DIM
        low_t = lax.broadcasted_iota(jnp.int32, (PAIR, 2 * L), 0) < HEAD_DIM
        for bi, dil in enumerate(DILATIONS):
            first_branch = bi == 0

            def scores(res, n, dil=dil):
                q_off, k_off, kp_off, w = _offsets(dil, res, n)
                qb = _gather(q_ref, q_off, w).astype(BF16)
                dof = _gather(do_ref, q_off, w)
                prod = dof * _gather(o_ref, q_off, w)
                dob = dof.astype(BF16)
                lse_f = _gather(lse_ref, q_off, w)
                zero = jnp.zeros_like(prod)
                dsum = (jnp.sum(jnp.where(low, prod, zero), axis=1, keepdims=True),
                        jnp.sum(jnp.where(low, zero, prod), axis=1, keepdims=True))
                kcat = jnp.concatenate([_gather(kbuf, kp_off, w), _gather(kbuf, k_off, w)], axis=0).astype(BF16)
                vcat = jnp.concatenate([_gather(vbuf, kp_off, w), _gather(vbuf, k_off, w)], axis=0).astype(BF16)
                k_bd, v_bd = _two_heads(kcat, low_k), _two_heads(vcat, low_k)
                sc = lax.dot_general(qb, k_bd, NT, preferred_element_type=F32)
                dp = lax.dot_general(dob, v_bd, NT, preferred_element_type=F32)
                return qb, dob, lse_f, dsum, k_bd, sc, dp

            def gradients(res, n, given, bi=bi, dil=dil, first_branch=first_branch):
                qb, dob, lse_f, dsum, k_bd, sc, dp = given
                q_off, k_off, kp_off, w = _offsets(dil, res, n)
                bias_at = _bias_index(bi, sb, n)
                ps, dss = [], []
                for h in range(2):
                    cols = slice(2 * L * h, 2 * L * (h + 1))
                    lse_h = lse_f[:, HEAD_DIM * h:HEAD_DIM * h + 1]
                    p_h = jnp.exp(sc[:, cols] + bias_s[bias_at + h] - lse_h)
                    dss.append((p_h * (dp[:, cols] - dsum[h])).astype(BF16))
                    ps.append(p_h.astype(BF16))
                ds_cat, p_cat = jnp.concatenate(dss, axis=1), jnp.concatenate(ps, axis=1)
                dq = jnp.dot(ds_cat, k_bd, preferred_element_type=F32)
                dk_t = lax.dot_general(qb, ds_cat, TN, preferred_element_type=F32)
                dv_t = lax.dot_general(dob, p_cat, TN, preferred_element_type=F32)
                dk = jnp.where(low_t, dk_t[:, :2 * L], dk_t[:, 2 * L:]).T
                dv = jnp.where(low_t, dv_t[:, :2 * L], dv_t[:, 2 * L:]).T
                return [(dq_s, q_off, w, dq, not first_branch),
                        (dkbuf, kp_off, w, dk[:L], True), (dkbuf, k_off, w, dk[L:], True),
                        (dvbuf, kp_off, w, dv[:L], True), (dvbuf, k_off, w, dv[L:], True)]

            _loop_blocks(dil, [scores, gradients], BWD_TOGETHER)

        dq_ref[...] = dq_s[...].astype(BF16)
        for r in range(RES):
            rows, cur = slice(L * r, L * (r + 1)), slice(2 * L * r + L, 2 * L * (r + 1))
            for plane, buf in enumerate((dkbuf, dvbuf)):
                if has_in:
                    dkv_ref[plane, rows, :] = buf[cur, :] + dkv_in_ref[plane, rows, :]
                else:
                    dkv_ref[plane, rows, :] = buf[cur, :]

    rev = lambda i: ns - 1 - i
    prev = lambda i: jnp.maximum(ns - 2 - i, 0)
    blk = pl.BlockSpec((SUPER, PAIR), lambda g, i: (rev(i), g))
    in_specs = [pl.BlockSpec((None, 8, PAIR), lambda g, i: (g, 0, 0)), blk, blk, blk, blk,
                pl.BlockSpec((SUPER, PAIR), lambda g, i: (rev(i), g)),
                pl.BlockSpec((SUPER, PAIR), lambda g, i: (prev(i), g)),
                pl.BlockSpec((SUPER, PAIR), lambda g, i: (rev(i), g_n + g)),
                pl.BlockSpec((SUPER, PAIR), lambda g, i: (prev(i), g_n + g))]
    ins = [_slope_table(d), q, do, o, lse, kv, kv, kv, kv]
    planes = pl.BlockSpec((2, SUPER, PAIR), lambda g, i: (0, rev(i), g))
    if has_in:
        in_specs.append(planes)
        ins.append(dkv_in)
    res = pl.pallas_call(
        body, name=name, grid=(g_n, ns), in_specs=in_specs, out_specs=[blk, planes],
        out_shape=[jax.ShapeDtypeStruct((s, d), BF16), jax.ShapeDtypeStruct((2, s, d), F32)],
        scratch_shapes=([pltpu.VMEM((2 * SUPER, PAIR), F32)] * 4 + [pltpu.VMEM((SUPER, PAIR), F32)]
                        + [pltpu.VMEM((4 * len(DILATIONS), L, 2 * L), F32)]),
        compiler_params=_params(("parallel", "arbitrary")),
    )(*ins)
    return res[0], res[1]


def _coords():
    return lax.axis_index("x"), lax.axis_index("y"), lax.axis_index("c")


def _chip_peers(x, y):
    return [(1 - x, y), (x, 1 - y), (1 - x, 1 - y)]


def _block_of(ref, axis, blk, size):
    start = pl.multiple_of(blk * size, size)
    if axis == 1:
        return ref.at[:, pl.ds(start, size)]
    return ref.at[pl.ds(start, size), :]


ANY = pl.BlockSpec(memory_space=pl.ANY)


HBM = pl.BlockSpec(memory_space=pltpu.HBM)
SEM = pl.BlockSpec(memory_space=pltpu.SEMAPHORE)
SPLIT = pltpu.CompilerParams(has_side_effects=pltpu.SideEffectType.DATAFLOW_SIDE_EFFECTING)


def _in_hbm(a):
    return pltpu.with_memory_space_constraint(a, pltpu.HBM)


def _thru(arrays):
    return [pltpu.HBM(a.shape, a.dtype) for a in arrays]


def _cast_place(w, layer, ax, dtype, after, name):
    _, k, n = w.shape
    t = _pick(k, (512, 256, 128))
    nb = k // t
    extra = [] if after is None else [after]

    def body(blk_ref, w_ref, *refs):
        b_ref, f_ref = refs[len(extra):]
        v = w_ref[...].astype(dtype)
        b_ref[...] = v
        f_ref[...] = v

    full_shape = (k, 4 * n) if ax == 1 else (4 * k, n)
    place = (lambda i, blk: (i, blk[0])) if ax == 1 else (lambda i, blk: (blk[0] * nb + i, 0))
    return pl.pallas_call(
        body, name=name,
        grid_spec=pltpu.PrefetchScalarGridSpec(
            num_scalar_prefetch=1, grid=(nb,),
            in_specs=[pl.BlockSpec((None, t, n), lambda i, blk: (layer, i, 0))] + [ANY] * len(extra),
            out_specs=[pl.BlockSpec((t, n), lambda i, blk: (i, 0)), pl.BlockSpec((t, n), place)]),
        out_shape=[jax.ShapeDtypeStruct((k, n), dtype), jax.ShapeDtypeStruct(full_shape, dtype)],
        compiler_params=_params(("parallel",)),
    )(_my_block()[None], w, *extra)


def _my_block():
    return (2 * lax.axis_index("x") + lax.axis_index("y")).astype(jnp.int32)


def _gather_start(group, carry, name):
    n, nc = len(group), len(carry)

    def body(*refs):
        blocks, fulls, send_sem, recv_sem = refs[:n], refs[n:2 * n], refs[2 * n + nc], refs[2 * n + nc + 1]
        x, y, c = _coords()
        for t, (b, _, ax) in enumerate(group):
            mine = _block_of(fulls[t], ax, 2 * x + y, b.shape[ax])
            for j, (px, py) in enumerate(_chip_peers(x, y)):
                pltpu.make_async_remote_copy(
                    src_ref=blocks[t], dst_ref=mine, send_sem=send_sem.at[3 * t + j], recv_sem=recv_sem.at[3 * t + j],
                    device_id=(px, py, c), device_id_type=MESH).start()

    arrays = [b for b, _, _ in group] + [f for _, f, _ in group] + list(carry)
    sems = [pltpu.SemaphoreType.DMA((3 * n,))] * 2
    res = pl.pallas_call(
        body, name=name, in_specs=[HBM] * len(arrays), out_specs=[SEM, SEM] + [HBM] * len(arrays),
        out_shape=sems + _thru(arrays), input_output_aliases={i: 2 + i for i in range(len(arrays))},
        compiler_params=SPLIT,
    )(*[_in_hbm(a) for a in arrays])
    return (res[0], res[1], list(res[2:2 + n]), list(res[2 + n:2 + 2 * n])), list(res[2 + 2 * n:])


def _gather_wait(group, started, after, name):
    sends, recvs, blocks, fulls = started
    m = len(group)

    def body(*refs):
        blk_refs, full_refs, send_sem, recv_sem = refs[:m], refs[m:2 * m], refs[2 * m], refs[2 * m + 1]
        x, y, c = _coords()
        for t, (b, _, ax) in enumerate(group):
            for j, (px, py) in enumerate(_chip_peers(x, y)):
                cp = pltpu.make_async_remote_copy(
                    src_ref=blk_refs[t], dst_ref=_block_of(full_refs[t], ax, 2 * px + py, b.shape[ax]),
                    send_sem=send_sem.at[3 * t + j], recv_sem=recv_sem.at[3 * t + j],
                    device_id=(px, py, c), device_id_type=MESH)
                cp.wait_send()
                cp.wait_recv()

    extra = [] if after is None else [after]
    res = pl.pallas_call(
        body, name=name, in_specs=[HBM] * (2 * m) + [SEM, SEM] + [ANY] * len(extra), out_specs=[HBM] * (2 * m),
        out_shape=_thru(blocks) + _thru(fulls), input_output_aliases={i: i for i in range(2 * m)},
        compiler_params=SPLIT,
    )(*blocks, *fulls, sends, recvs, *extra)
    return list(res[m:])


def _scatter_start(grads, carry, name):
    n = len(grads)
    n_in = 2 * n + len(carry)

    def body(*refs):
        g_refs, st_refs, send_sem, recv_sem = refs[:n], refs[n:2 * n], refs[n_in], refs[n_in + 1]
        x, y, c = _coords()
        for t, (g, ax) in enumerate(grads):
            for j, (px, py) in enumerate(_chip_peers(x, y)):
                pltpu.make_async_remote_copy(
                    src_ref=_block_of(g_refs[t], ax, 2 * px + py, g.shape[ax] // 4), dst_ref=st_refs[t].at[j],
                    send_sem=send_sem.at[3 * t + j], recv_sem=recv_sem.at[3 * t + j],
                    device_id=(px, py, c), device_id_type=MESH).start()

    arrays = [g for g, _ in grads]
    for g, ax in grads:
        shape = list(g.shape)
        shape[ax] //= 4
        arrays.append(lax.empty((3, *shape), g.dtype))
    arrays += list(carry)
    sems = [pltpu.SemaphoreType.DMA((3 * n,))] * 2
    res = pl.pallas_call(
        body, name=name, in_specs=[HBM] * n_in, out_specs=[SEM, SEM] + [HBM] * n_in,
        out_shape=sems + _thru(arrays), input_output_aliases={i: 2 + i for i in range(n_in)},
        compiler_params=SPLIT,
    )(*[_in_hbm(a) for a in arrays])
    return (res[0], res[1], list(res[2:2 + n]), list(res[2 + n:2 + 2 * n])), list(res[2 + 2 * n:])


def _scatter_wait(axes, started, after, name):
    sends, recvs, full, stacks = started
    n = len(full)
    extra = [] if after is None else [after]

    def body(*refs):
        g_refs, st_refs, send_sem, recv_sem = refs[:n], refs[n:2 * n], refs[2 * n], refs[2 * n + 1]
        x, y, c = _coords()
        for t, ax in enumerate(axes):
            size = full[t].shape[ax] // 4
            for j, (px, py) in enumerate(_chip_peers(x, y)):
                cp = pltpu.make_async_remote_copy(
                    src_ref=_block_of(g_refs[t], ax, 2 * px + py, size), dst_ref=st_refs[t].at[j],
                    send_sem=send_sem.at[3 * t + j], recv_sem=recv_sem.at[3 * t + j],
                    device_id=(px, py, c), device_id_type=MESH)
                cp.wait_send()
                cp.wait_recv()

    res = pl.pallas_call(
        body, name=name, in_specs=[HBM] * (2 * n) + [SEM, SEM] + [ANY] * len(extra), out_specs=[HBM] * (2 * n),
        out_shape=_thru(full) + _thru(stacks), input_output_aliases={i: i for i in range(2 * n)},
        compiler_params=SPLIT,
    )(*full, *stacks, sends, recvs, *extra)
    return list(res[:n]), list(res[n:])


def _pair_copies(g_refs, st_refs, out_refs, items, send_sem, recv_sem):
    x, y, c = _coords()
    copies = []
    for u, (g, ax, _) in enumerate(items):
        own = _block_of(g_refs[u], ax, 2 * x + y, g.shape[ax] // 4)
        for k, (src, dst) in enumerate([(own, out_refs[u].at[0]), (st_refs[u], out_refs[u].at[pl.ds(1, 3)])]):
            copies.append(pltpu.make_async_remote_copy(
                src_ref=src, dst_ref=dst, send_sem=send_sem.at[2 * u + k], recv_sem=recv_sem.at[2 * u + k],
                device_id=(x, y, 1 - c), device_id_type=MESH))
    return copies


def _pair_start(items, carry, name):
    n = len(items)
    n_in = 3 * n + len(carry)

    def body(*refs):
        for cp in _pair_copies(refs[:n], refs[n:2 * n], refs[2 * n:3 * n], items, refs[n_in], refs[n_in + 1]):
            cp.start()

    arrays = ([g for g, _, _ in items] + [st for _, _, st in items]
              + [lax.empty((4, *st.shape[1:]), st.dtype) for _, _, st in items] + list(carry))
    sems = [pltpu.SemaphoreType.DMA((2 * n,))] * 2
    res = pl.pallas_call(
        body, name=name, in_specs=[HBM] * n_in, out_specs=[SEM, SEM] + [HBM] * n_in,
        out_shape=sems + _thru(arrays), input_output_aliases={i: 2 + i for i in range(n_in)},
        compiler_params=SPLIT,
    )(*[_in_hbm(a) for a in arrays])
    thru = res[2:]
    return (res[0], res[1], *(list(thru[k * n:(k + 1) * n]) for k in range(3))), list(thru[3 * n:])


def _pair_wait(axes, started, name):
    send, recv, full, stacks, landing = started
    n = len(full)
    items = [(full[u], axes[u], stacks[u]) for u in range(n)]

    def body(*refs):
        for cp in _pair_copies(refs[:n], refs[n:2 * n], refs[2 * n:3 * n], items, refs[3 * n], refs[3 * n + 1]):
            cp.wait_send()
            cp.wait_recv()

    res = pl.pallas_call(
        body, name=name, in_specs=[HBM] * (3 * n) + [SEM, SEM], out_specs=[HBM] * (3 * n),
        out_shape=_thru(full + stacks + landing), input_output_aliases={i: i for i in range(3 * n)},
        compiler_params=SPLIT,
    )(*full, *stacks, *landing, send, recv)
    return list(res[:n]), list(res[n:2 * n]), list(res[2 * n:])


def _allreduce_small(v, name):
    r, cdim = v.shape

    def body(v_ref, out_ref, buf, send_sems, recv_sems):
        x, y, c = _coords()
        me = 4 * x + 2 * y + c
        buf[0] = v_ref[...]
        sends = []
        for k in range(1, 8):
            peer = (x if not (k & 4) else 1 - x, y if not (k & 2) else 1 - y, c if not (k & 1) else 1 - c)
            cp = pltpu.make_async_remote_copy(
                src_ref=v_ref, dst_ref=buf.at[k], send_sem=send_sems.at[k - 1], recv_sem=recv_sems.at[k - 1],
                device_id=peer, device_id_type=MESH)
            cp.start()
            sends.append(cp)
        for cp in sends:
            cp.wait_recv()
        total = buf[me]
        for src in range(1, 8):
            total = total + buf[jnp.bitwise_xor(me, src)]
        out_ref[...] = total
        for cp in sends:
            cp.wait_send()

    return pl.pallas_call(
        body, name=name,
        in_specs=[pl.BlockSpec(memory_space=pltpu.VMEM)], out_specs=pl.BlockSpec(memory_space=pltpu.VMEM),
        out_shape=jax.ShapeDtypeStruct((r, cdim), F32),
        scratch_shapes=[pltpu.VMEM((8, r, cdim), F32), pltpu.SemaphoreType.DMA((7,)), pltpu.SemaphoreType.DMA((7,))],
        compiler_params=pltpu.CompilerParams(has_side_effects=True),
    )(v)


def _adamw_math(w, g, m, v):
    m = ADAM_B1 * m + (1.0 - ADAM_B1) * g
    v = ADAM_B2 * v + (1.0 - ADAM_B2) * jnp.square(g)
    m_hat = m / (1.0 - ADAM_B1 ** ADAM_STEP)
    v_hat = v / (1.0 - ADAM_B2 ** ADAM_STEP)
    delta = -ADAM_LR * (m_hat / (jnp.sqrt(v_hat) + ADAM_EPS) + ADAM_WD * w)
    return delta, m, v


def _adamw(w, m, v, grads, name):
    r, cdim = w.shape
    paired = isinstance(grads, list)
    layers = len(grads) if paired else 1
    t = _pick(r // layers, (128, 64, 32, 16, 8))
    per_layer = r // layers // t
    n_grad = 3 * layers if paired else 1

    def body(*refs):
        refs = refs[1:] if paired else refs
        w_ref, m_ref, v_ref = refs[:3]
        outs = refs[3 + n_grad:]

        def update(g):
            delta, m_new, v_new = _adamw_math(w_ref[...], g, m_ref[...], v_ref[...])
            outs[0][...] = g
            outs[1][...] = delta
            outs[2][...] = m_new
            outs[3][...] = v_new

        if not paired:
            update(refs[3][...])
            return
        layer = pl.program_id(0) // per_layer
        for l in range(layers):
            @pl.when(layer == l)
            def _(own_ref=refs[3 + 3 * l], st_ref=refs[4 + 3 * l], sib_ref=refs[5 + 3 * l]):
                sa = own_ref[...].astype(F32)
                sb = sib_ref[0].astype(F32)
                for k in range(3):
                    sa = sa + st_ref[k].astype(F32)
                    sb = sb + sib_ref[k + 1].astype(F32)
                update(sa + sb)

    out_shape = [jax.ShapeDtypeStruct((r, cdim), F32)] * 4
    if not paired:
        spec = pl.BlockSpec((t, cdim), lambda i: (i, 0))
        return pl.pallas_call(
            body, name=name, grid=(r // t,), in_specs=[spec] * 4, out_specs=[spec] * 4, out_shape=out_shape,
            compiler_params=_params(("parallel",)),
        )(w, m, v, grads)

    spec = pl.BlockSpec((t, cdim), lambda i, blk: (i, 0))
    ins, in_specs = [w, m, v], [spec] * 3
    for l, (g, ax, stack, sib) in enumerate(grads):
        row = lambda i, l=l: jnp.clip(i - l * per_layer, 0, per_layer - 1)
        own = ((lambda i, blk, row=row: (row(i), blk[0])) if ax == 1
               else (lambda i, blk, row=row: (blk[0] * per_layer + row(i), 0)))
        ins += [g, stack, sib]
        in_specs += [pl.BlockSpec((t, cdim), own),
                     pl.BlockSpec((3, t, cdim), lambda i, blk, row=row: (0, row(i), 0)),
                     pl.BlockSpec((4, t, cdim), lambda i, blk, row=row: (0, row(i), 0))]
    return pl.pallas_call(
        body, name=name,
        grid_spec=pltpu.PrefetchScalarGridSpec(
            num_scalar_prefetch=1, grid=(r // t,), in_specs=in_specs, out_specs=[spec] * 4),
        out_shape=out_shape, compiler_params=_params(("parallel",)),
    )(_my_block()[None], *ins)


def _local_step(x, target, gains, conv_ws, kv_gain, weights_of, send_grads):
    depth = len(gains)
    n_a = len(conv_ws)
    saved, ws = [], []
    kv = kvn = None
    _, (xn,) = _norm_res_fwd(x, None, None, [gains[0][0]], "norm_first")
    h = x
    for l in range(depth):
        g = gains[l]
        sv = {"x_in": h, "xn": xn}
        w = weights_of(l, "mix", h)
        ws.append(w)
        if l == n_a:
            kv = _mm(kvn, w["kv"], "nn", F32, "kv_fwd")
        if l < n_a:
            p = _mm(xn, w["conv_in"], "nn", BF16, f"conv_in_fwd_{l}")
            z = _conv_gate_fwd(p, conv_ws[l], f"conv_gate_fwd_{l}")
            mix, x1, xn2 = _mm_norm_res(z, w["conv_out"], h, g[1], g[2], f"conv_out_fwd_{l}")
            sv.update(p=p, z=z)
        else:
            j = l - n_a
            q = _mm(xn, w["q"], "nn", F32, f"q_fwd_{j}", scale=HEAD_DIM ** -0.5)
            o, lse = _attention_fwd(q, kv, f"attn_fwd_{j}")
            mix, x1, xn2 = _mm_norm_res(o, w["o"], h, g[1], g[2], f"o_fwd_{j}")
            sv.update(q=q, o=o, lse=lse)
        w.update(weights_of(l, "ffn", mix))
        f, a = _ffn_in_swiglu(xn2, w["ffn_in"], f"ffn_in_fwd_{l}")
        ff = _mm(a, w["ffn_out"], "nn", BF16, f"ffn_out_fwd_{l}")
        sv.update(mix=mix, x1=x1, xn2=xn2, f=f, a=a, ff=ff)
        saved.append(sv)
        if l == depth - 1:
            dx, loss = _norm_res_loss(x1, ff, g[3], target, "norm_loss")
        elif l == n_a - 1:
            h, _ = _norm_res_fwd(x1, ff, g[3], [], f"norm_end_{l}")
            h = _permute16(h, False, "permute_stream")
            _, (xn, kvn) = _norm_res_fwd(h, None, None, [gains[l + 1][0], kv_gain], "norm_permuted")
        else:
            h, (xn,) = _norm_res_fwd(x1, ff, g[3], [gains[l + 1][0]], f"norm_end_{l}")
    d_gains = [[None] * 4 for _ in range(depth)]
    d_conv = [None] * n_a
    d_kv_gain = None
    dkv = None
    _, _, dff, d_gains[depth - 1][3] = _norm_bwd(dx, [], None, (saved[-1]["ff"], gains[-1][3]), "norm_loss_bwd")
    for l in reversed(range(depth)):
        sv, g, w, grads = saved[l], gains[l], ws[l], {}
        grads["ffn_out"] =_mm(sv["a"], dff, "tn", BF16, f"ffn_out_dw_{l}")
        df = _ffn_out_dx_swiglu(dff, w["ffn_out"], sv["f"], f"ffn_out_dx_{l}")
        dxn2 = _mm(df, w["ffn_in"], "nt", BF16, f"ffn_in_dx_{l}")
        grads["ffn_in"] =_mm(sv["xn2"], df, "tn", BF16, f"ffn_in_dw_{l}")
        dx, (d_gains[l][2],), dmix, d_gains[l][1] = _norm_bwd(
            dx, [(dxn2, g[2])], sv["x1"], (sv["mix"], g[1]), f"norm_mid_bwd_{l}")
        dx, dmix = send_grads(l, "ffn", grads, [dx, dmix])
        if l < n_a:
            dz = _mm(dmix, w["conv_out"], "nt", BF16, f"conv_out_dx_{l}")
            grads["conv_out"] =_mm(sv["z"], dmix, "tn", BF16, f"conv_out_dw_{l}")
            dp, d_conv[l] = _conv_gate_bwd(sv["p"], dz, conv_ws[l], f"conv_gate_bwd_{l}")
            dxn = _mm(dp, w["conv_in"], "nt", BF16, f"conv_in_dx_{l}")
            grads["conv_in"] =_mm(sv["xn"], dp, "tn", BF16, f"conv_in_dw_{l}")
        else:
            j = l - n_a
            do = _mm(dmix, w["o"], "nt", F32, f"o_dx_{j}")
            grads["o"] =_mm(sv["o"], dmix, "tn", BF16, f"o_dw_{j}")
            dq, dkv = _attention_bwd(sv["q"], kv, sv["o"], do, sv["lse"], dkv, f"attn_bwd_{j}")
            scale = HEAD_DIM ** -0.5
            dxn = _mm(dq, w["q"], "nt", BF16, f"q_dx_{j}", scale=scale)
            grads["q"] =_mm(sv["xn"], dq, "tn", BF16, f"q_dw_{j}", scale=scale)
        branches = [(dxn, g[0])]
        if l == n_a:
            dkvn = _mm(dkv, w["kv"], "nt", BF16, "kv_dx")
            grads["kv"] =_mm(kvn, dkv, "tn", BF16, "kv_dw")
            branches.append((dkvn, kv_gain))
        post = (saved[l - 1]["ff"], gains[l - 1][3]) if l > 0 else None
        if l == n_a:
            dx, dgs, _, _ = _norm_bwd(dx, branches, sv["x_in"], None, f"norm_end_bwd_{l}")
            dx = _permute16(dx, True, "unpermute_stream")
            _, _, dff, dg_post = _norm_bwd(dx, [], None, post, "norm_boundary_bwd")
        else:
            dx, dgs, dff, dg_post = _norm_bwd(dx, branches, sv["x_in"], post, f"norm_end_bwd_{l}")
        if dff is None:
            send_grads(l, "mix", grads, [])
        else:
            dx, dff = send_grads(l, "mix", grads, [dx, dff])
        d_gains[l][0] = dgs[0]
        if l == n_a:
            d_kv_gain = dgs[1]
        if l > 0:
            d_gains[l - 1][3] = dg_post
    return loss, dx, d_gains, d_conv, d_kv_gain


BIG = (
    ("conv_in", 1), ("conv_out", 0), ("kv", 1), ("q", 0), ("o", 0), ("ffn_in", 1), ("ffn_out", 0))


def kernel(x, norm_g, conv_in_w, conv_w, conv_out_w, kv_norm_g, kv_w, q_w, o_w, ffn_in_w, ffn_out_w, loss_target, m_norm_g, m_conv_in_w, m_conv_w, m_conv_out_w, m_kv_norm_g, m_kv_w, m_q_w, m_o_w, m_ffn_in_w, m_ffn_out_w, v_norm_g, v_conv_in_w, v_conv_w, v_conv_out_w, v_kv_norm_g, v_kv_w, v_q_w, v_o_w, v_ffn_in_w, v_ffn_out_w):
    depth, _, dq = norm_g.shape
    d = 4 * dq
    n_a = conv_w.shape[0]
    big_w = {"conv_in": conv_in_w, "conv_out": conv_out_w, "kv": kv_w[None], "q": q_w, "o": o_w,
             "ffn_in": ffn_in_w, "ffn_out": ffn_out_w}
    big_m = {"conv_in": m_conv_in_w, "conv_out": m_conv_out_w, "kv": m_kv_w[None], "q": m_q_w, "o": m_o_w,
             "ffn_in": m_ffn_in_w, "ffn_out": m_ffn_out_w}
    big_v = {"conv_in": v_conv_in_w, "conv_out": v_conv_out_w, "kv": v_kv_w[None], "q": v_q_w, "o": v_o_w,
             "ffn_in": v_ffn_in_w, "ffn_out": v_ffn_out_w}

    n_gain, n_tap = depth * 4, n_a * conv_w.shape[1]
    small_rows = -(-(n_gain + n_tap + 1) // 8) * 8
    pad_rows = small_rows - n_gain - n_tap

    def pack_small(gains, taps):
        return jnp.concatenate([gains.reshape(n_gain, dq), taps.reshape(n_tap, dq), jnp.zeros((pad_rows, dq), F32)])

    axis_of = dict(BIG)

    def matrices_of(l, part):
        if part == "ffn":
            return [("ffn_in", l), ("ffn_out", l)]
        if l < n_a:
            return [("conv_in", l), ("conv_out", l)]
        return ([("kv", 0)] if l == n_a else []) + [("q", l - n_a), ("o", l - n_a)]

    halves = [(l, part) for l in range(depth) for part in ("mix", "ffn")]
    def placed(half, after):
        return [(*_cast_place(big_w[name], i, axis_of[name], BF16, after, f"place_{name}_{i}"), axis_of[name])
                for name, i in matrices_of(*half)]

    groups = {halves[0]: placed(halves[0], None)}
    groups[halves[0]].append((*_cast_place(pack_small(norm_g, conv_w)[None], 0, 1, F32, None, "place_small"), 1))
    started = {halves[0]: _gather_start(groups[halves[0]], [], "gather_start_0_mix")[0]}
    for half in halves[1:]:
        groups[half] = placed(half, started[halves[0]][2][0])

    AHEAD = 2

    def fetch(half, after):
        full = _gather_wait(groups[half], started[half], after, "gather_wait_%d_%s" % half)
        nxt = halves.index(half) + AHEAD
        if nxt < len(halves):
            started[halves[nxt]], full = _gather_start(groups[halves[nxt]], full, "gather_start_%d_%s" % halves[nxt])
        return full

    target = _permute16(loss_target.reshape(x.shape[1:]), False, "permute_target")
    for half in halves[1:AHEAD]:
        started[half], (target,) = _gather_start(groups[half], [target], "gather_start_%d_%s" % half)
    first = fetch(halves[0], target)
    small = first[-1]
    gains = [[small[4 * l + i][None] for i in range(4)] for l in range(depth)]
    conv_ws = [small[n_gain + 3 * l:n_gain + 3 * l + 3] for l in range(n_a)]
    kv_gain = kv_norm_g[None]

    def weights_of(l, part, after):
        full = first if (l, part) == halves[0] else fetch((l, part), after)
        return {name: full[t] for t, (name, _) in enumerate(matrices_of(l, part))}

    sent, paired = {}, {}
    LAG = 2

    def to_sibling(half, carry, after):
        axes = [axis_of[name] for name, _ in matrices_of(*half)]
        full, stacks = _scatter_wait(axes, sent[half], after, "scatter_wait_%d_%s" % half)
        paired[half], carry = _pair_start(list(zip(full, axes, stacks)), carry, "pair_start_%d_%s" % half)
        return carry

    def send_grads(l, part, grads, carry):
        sent[l, part], carry = _scatter_start(
            [(grads[name], axis_of[name]) for name, _ in matrices_of(l, part)], carry, f"scatter_start_{l}_{part}")
        at = halves.index((l, part))
        if carry:
            if at + LAG < len(halves):
                carry = to_sibling(halves[at + LAG], carry, carry[0])
        else:
            for older in range(at + LAG, at, -1):
                to_sibling(halves[older], [], sent[l, part][2][0])
        return carry

    loss, dx, d_gains, d_conv, d_kv_gain = _local_step(
        x.reshape(x.shape[1:]), target, gains, conv_ws, kv_gain, weights_of, send_grads)
    loss = lax.psum(loss, ("x", "y", "c"))

    small_g = jnp.concatenate([dg for row in d_gains for dg in row] + list(d_conv) + [d_kv_gain]
                              + [jnp.zeros((pad_rows - 1, d), F32)])
    small_g = _allreduce_small(small_g, "allreduce_small")
    blk = 2 * lax.axis_index("x") + lax.axis_index("y")
    mine_small = lax.dynamic_slice_in_dim(small_g, blk * dq, dq, axis=1)
    kv_rows = d // dq

    def pack_opt(gains_like, taps_like, kv_like):
        rows = jnp.concatenate([gains_like.reshape(n_gain, dq), taps_like.reshape(n_tap, dq), kv_like.reshape(kv_rows, dq)])
        extra = -rows.shape[0] % 8
        return jnp.concatenate([rows, jnp.zeros((extra, dq), F32)]) if extra else rows

    sw = pack_opt(norm_g, conv_w, kv_norm_g)
    sm = pack_opt(m_norm_g, m_conv_w, m_kv_norm_g)
    sv = pack_opt(v_norm_g, v_conv_w, v_kv_norm_g)
    sg = pack_opt(mine_small[:n_gain], mine_small[n_gain:n_gain + n_tap], small_g[n_gain + n_tap])
    s_out = _adamw(sw, sm, sv, sg, "adamw_small")

    def unpack(a):
        return (a[:n_gain].reshape(depth, 4, dq), a[n_gain:n_gain + n_tap].reshape(n_a, -1, dq),
                a[n_gain + n_tap:n_gain + n_tap + kv_rows].reshape(d))

    small_out = [unpack(a) for a in s_out]

    landed, big_out = {}, {}

    def update(name):
        for half in halves:
            if matrices_of(*half)[0] not in landed and any(n == name for n, _ in matrices_of(*half)):
                axes = [axis_of[n] for n, _ in matrices_of(*half)]
                landed.update(zip(matrices_of(*half), zip(*_pair_wait(axes, paired[half], "pair_wait_%d_%s" % half))))
        shp, ax = big_w[name].shape, axis_of[name]
        rows, cols = shp[0] * shp[1], shp[2]
        flat = lambda a: a.reshape(rows, cols)
        full, stacks, sibling = zip(*[landed[name, i] for i in range(shp[0])])
        res = _adamw(flat(big_w[name]), flat(big_m[name]), flat(big_v[name]),
                     [(full[i], ax, stacks[i], sibling[i]) for i in range(shp[0])], f"adamw_{name}")
        big_out[name] = [a.reshape(shp[1:] if name == "kv" else shp) for a in res]

    pending = [half for half in reversed(halves) if half not in paired]
    for half in pending[:-1]:
        to_sibling(half, [], None)
    late = [name for name, _ in BIG if any(n == name for n, _ in matrices_of(*pending[-1]))]
    for name, _ in BIG:
        if name not in late:
            update(name)
    to_sibling(pending[-1], [], big_out["ffn_out"][0])
    for name in late:
        update(name)

    def leaves(i):
        ng, cw_, kg = small_out[i]
        return [ng, big_out["conv_in"][i], cw_, big_out["conv_out"][i], kg, big_out["kv"][i], big_out["q"][i],
                big_out["o"][i], big_out["ffn_in"][i], big_out["ffn_out"][i]]

    return (loss, dx.reshape(x.shape), *leaves(0), *leaves(1), *leaves(2), *leaves(3))
```

```python
import jax
import jax.numpy as jnp
import numpy as np
from jax import lax
from jax.experimental import pallas as pl
from jax.experimental.pallas import tpu as pltpu

F32 = jnp.float32
BF16 = jnp.bfloat16
HEAD_DIM = 64
DILATIONS = (1, 4, 16)
NORM_EPS = 1e-6
NEG_BIG = -1e30
VMEM_LIMIT = 48 * 1024 * 1024
ROW_TILE = 256
NORM_TILE = 512
LANE = 128
MESH = pl.DeviceIdType.MESH

ADAM_LR = 0.001
ADAM_B1 = 0.9
ADAM_B2 = 0.999
ADAM_EPS = 1e-08
ADAM_WD = 0.01
ADAM_STEP = 10

TILE_CANDIDATES = (1024, 1408, 768, 512, 384, 256, 128)


def _pick(dim, cands=TILE_CANDIDATES):
    for c in cands:
        if c <= dim and dim % c == 0:
            return c
    return dim


def _params(sem):
    return pltpu.CompilerParams(dimension_semantics=sem, vmem_limit_bytes=VMEM_LIMIT)


def _mm(a, b, mode, out_dtype, name, scale=None):
    a_planes = a.shape[0] if a.ndim == 3 else 1
    b_planes = b.shape[0] if b.ndim == 3 else 1
    if mode == "nn":
        m, k = a.shape[-2], a.shape[-1] * a_planes
        n = b.shape[1]
    elif mode == "nt":
        m, k = a.shape[-2], a.shape[-1] * a_planes
        n = b.shape[0]
    else:
        k, m = a.shape
        n = b.shape[-1] * b_planes
    tm, tn = _pick(m), _pick(n // b_planes)
    tk = _pick(k // a_planes, ((2048,) if mode == "tn" else (3072, 2816)) + TILE_CANDIDATES)
    nk = k // tk
    ka, nb = k // a_planes // tk, n // b_planes // tn
    if a_planes > 1:
        a_spec = pl.BlockSpec((None, tm, tk), lambda i, j, kk: (kk // ka, i, kk % ka))
    elif mode == "tn":
        a_spec = pl.BlockSpec((tk, tm), lambda i, j, kk: (kk, i))
    else:
        a_spec = pl.BlockSpec((tm, tk), lambda i, j, kk: (i, kk))
    if mode == "nn":
        b_spec = pl.BlockSpec((tk, tn), lambda i, j, kk: (kk, j))
        dims = (((1,), (0,)), ((), ()))
    elif mode == "nt":
        b_spec = pl.BlockSpec((tn, tk), lambda i, j, kk: (j, kk))
        dims = (((1,), (1,)), ((), ()))
    else:
        b_spec = (pl.BlockSpec((None, tk, tn), lambda i, j, kk: (j // nb, kk, j % nb)) if b_planes > 1
                  else pl.BlockSpec((tk, tn), lambda i, j, kk: (kk, j)))
        dims = (((0,), (0,)), ((), ()))

    def finish(acc):
        if scale is not None:
            acc = acc * scale
        return acc.astype(out_dtype)

    if nk == 1:
        def body(a_ref, b_ref, o_ref):
            o_ref[...] = finish(lax.dot_general(a_ref[...].astype(BF16), b_ref[...].astype(BF16), dims, preferred_element_type=F32))
        scratch = []
    else:
        def body(a_ref, b_ref, o_ref, acc_ref):
            kk = pl.program_id(2)

            @pl.when(kk == 0)
            def _():
                acc_ref[...] = jnp.zeros_like(acc_ref)

            acc_ref[...] += lax.dot_general(a_ref[...].astype(BF16), b_ref[...].astype(BF16), dims, preferred_element_type=F32)

            @pl.when(kk == nk - 1)
            def _():
                o_ref[...] = finish(acc_ref[...])
        scratch = [pltpu.VMEM((tm, tn), F32)]

    return pl.pallas_call(
        body, name=name,
        grid=(m // tm, n // tn, nk),
        in_specs=[a_spec, b_spec],
        out_specs=pl.BlockSpec((tm, tn), lambda i, j, kk: (i, j)),
        out_shape=jax.ShapeDtypeStruct((m, n), out_dtype),
        scratch_shapes=scratch,
        compiler_params=_params(("parallel", "parallel", "arbitrary")),
    )(a, b)


def _rstd(v):
    return lax.rsqrt(jnp.mean(v * v, axis=-1, keepdims=True) + NORM_EPS)


def _rms_bwd(dy, v, g, r):
    xhat = v * r
    gy = dy * g
    dv = r * (gy - xhat * jnp.mean(gy * xhat, axis=-1, keepdims=True))
    return dv, dy * xhat


def _row_spec(t, width):
    return pl.BlockSpec((t, width), lambda i: (i, 0))


def _gain_spec(width):
    return pl.BlockSpec((1, width), lambda i: (0, 0))


def _norm_res_fwd(x, mix, g_post, pre_gains, name):
    s, d = x.shape
    t = _pick(s, (NORM_TILE, ROW_TILE))
    has_mix = mix is not None
    n_pre = len(pre_gains)

    def body(*refs):
        x_ref = refs[0]
        pos = 1
        x1 = x_ref[...]
        if has_mix:
            mv = refs[1][...].astype(F32)
            x1 = x1 + mv * _rstd(mv) * refs[2][...]
            pos = 3
        gains = refs[pos:pos + n_pre]
        outs = refs[pos + n_pre:]
        if has_mix:
            outs[0][...] = x1
            outs = outs[1:]
        r = _rstd(x1)
        for g_ref, o_ref in zip(gains, outs):
            o_ref[...] = (x1 * r * g_ref[...]).astype(BF16)

    ins = [x] + ([mix, g_post] if has_mix else []) + list(pre_gains)
    in_specs = [_row_spec(t, d)] + ([_row_spec(t, d), _gain_spec(d)] if has_mix else []) + [_gain_spec(d)] * n_pre
    out_shape = ([jax.ShapeDtypeStruct((s, d), F32)] if has_mix else []) + [jax.ShapeDtypeStruct((s, d), BF16)] * n_pre
    out_specs = [_row_spec(t, d)] * len(out_shape)
    res = pl.pallas_call(
        body, name=name, grid=(s // t,), in_specs=in_specs, out_specs=out_specs, out_shape=out_shape,
        compiler_params=_params(("parallel",)),
    )(*ins)
    if has_mix:
        return res[0], list(res[1:])
    return x, list(res)


def _mm_norm_res(a, b, x, g_post, pre_gain, name):
    m, k = a.shape
    d = b.shape[1]
    tm, tk = _pick(m, (1024, 512, 256)), _pick(k)
    nk = k // tk

    def body(a_ref, b_ref, x_ref, gp_ref, g_ref, mix_ref, x1_ref, xn_ref, acc_ref):
        kk = pl.program_id(1)

        @pl.when(kk == 0)
        def _():
            acc_ref[...] = jnp.zeros_like(acc_ref)

        acc_ref[...] += jnp.dot(a_ref[...].astype(BF16), b_ref[...], preferred_element_type=F32)

        @pl.when(kk == nk - 1)
        def _():
            mix = acc_ref[...].astype(BF16)
            mix_ref[...] = mix
            mv = mix.astype(F32)
            x1 = x_ref[...] + mv * _rstd(mv) * gp_ref[...]
            x1_ref[...] = x1
            xn_ref[...] = (x1 * _rstd(x1) * g_ref[...]).astype(BF16)

    rows = pl.BlockSpec((tm, d), lambda i, kk: (i, 0))
    gain = pl.BlockSpec((1, d), lambda i, kk: (0, 0))
    return pl.pallas_call(
        body, name=name, grid=(m // tm, nk),
        in_specs=[pl.BlockSpec((tm, tk), lambda i, kk: (i, kk)), pl.BlockSpec((tk, d), lambda i, kk: (kk, 0)),
                  rows, gain, gain],
        out_specs=[rows, rows, rows],
        out_shape=[jax.ShapeDtypeStruct((m, d), BF16), jax.ShapeDtypeStruct((m, d), F32),
                   jax.ShapeDtypeStruct((m, d), BF16)],
        scratch_shapes=[pltpu.VMEM((tm, d), F32)],
        compiler_params=_params(("parallel", "arbitrary")),
    )(a, b, x, g_post, pre_gain)


def _norm_res_loss(x, mix, g_post, target, name):
    s, d = x.shape
    t = _pick(s, (NORM_TILE, ROW_TILE))

    def body(x_ref, m_ref, g_ref, t_ref, dy_ref, loss_ref):
        mv = m_ref[...].astype(F32)
        y = x_ref[...] + mv * _rstd(mv) * g_ref[...]
        err = y - t_ref[...]
        dy_ref[...] = err * (1.0 / d)

        @pl.when(pl.program_id(0) == 0)
        def _():
            loss_ref[...] = jnp.zeros_like(loss_ref)

        loss_ref[...] += jnp.sum(err * err)

    dy, acc = pl.pallas_call(
        body, name=name, grid=(s // t,),
        in_specs=[_row_spec(t, d), _row_spec(t, d), _gain_spec(d), _row_spec(t, d)],
        out_specs=[_row_spec(t, d), pl.BlockSpec((8, LANE), lambda i: (0, 0))],
        out_shape=[jax.ShapeDtypeStruct((s, d), F32), jax.ShapeDtypeStruct((8, LANE), F32)],
        compiler_params=_params(("arbitrary",)),
    )(x, mix, g_post, target)
    return dy, acc[0, 0] * (0.5 / d)


def _norm_bwd(dx_out, branches, x_in, post, name):
    s, d = dx_out.shape
    t = _pick(s, (NORM_TILE, ROW_TILE))
    nb = len(branches)
    has_post = post is not None

    def body(*refs):
        dx_ref = refs[0]
        pos = 1
        dx = dx_ref[...]
        first = pl.program_id(0) == 0
        n_in = 1 + (1 + 2 * nb if nb else 0) + (2 if has_post else 0)
        outs = refs[n_in:]
        opos = 0
        if nb:
            xv = refs[pos][...]
            pos += 1
            r = _rstd(xv)
            dx_o = outs[0]
            opos = 1
            for _ in range(nb):
                dxn = refs[pos][...].astype(F32)
                g = refs[pos + 1][...]
                pos += 2
                dv, dg_rows = _rms_bwd(dxn, xv, g, r)
                dx = dx + dv
                dg_ref = outs[opos]
                opos += 1

                @pl.when(first)
                def _(dg_ref=dg_ref):
                    dg_ref[...] = jnp.zeros_like(dg_ref)

                dg_ref[...] += jnp.sum(dg_rows, axis=0, keepdims=True)
            dx_o[...] = dx
        if has_post:
            mv = refs[pos][...].astype(F32)
            g = refs[pos + 1][...]
            dm, dg_rows = _rms_bwd(dx, mv, g, _rstd(mv))
            outs[opos][...] = dm.astype(BF16)
            dg_ref = outs[opos + 1]

            @pl.when(first)
            def _():
                dg_ref[...] = jnp.zeros_like(dg_ref)

            dg_ref[...] += jnp.sum(dg_rows, axis=0, keepdims=True)

    ins, in_specs = [dx_out], [_row_spec(t, d)]
    out_shape, out_specs = [], []
    if nb:
        ins.append(x_in)
        in_specs.append(_row_spec(t, d))
        out_shape.append(jax.ShapeDtypeStruct((s, d), F32))
        out_specs.append(_row_spec(t, d))
        for dxn, g in branches:
            ins += [dxn, g]
            in_specs += [_row_spec(t, d), _gain_spec(d)]
            out_shape.append(jax.ShapeDtypeStruct((1, d), F32))
            out_specs.append(_gain_spec(d))
    if has_post:
        ins += [post[0], post[1]]
        in_specs += [_row_spec(t, d), _gain_spec(d)]
        out_shape += [jax.ShapeDtypeStruct((s, d), BF16), jax.ShapeDtypeStruct((1, d), F32)]
        out_specs += [_row_spec(t, d), _gain_spec(d)]
    res = pl.pallas_call(
        body, name=name, grid=(s // t,), in_specs=in_specs, out_specs=out_specs, out_shape=out_shape,
        compiler_params=_params(("arbitrary",)),
    )(*ins)
    res = list(res)
    dx_in = res.pop(0) if nb else dx_out
    dgs = [res.pop(0) for _ in range(nb)]
    dm, dg_post = (res[0], res[1]) if has_post else (None, None)
    return dx_in, dgs, dm, dg_post


HALO = 16


def _shift_down(u, prev, k):
    rows = lax.broadcasted_iota(jnp.int32, u.shape, 0)
    out = pltpu.roll(u, k, 0)
    for i in range(k):
        out = jnp.where(rows == i, prev[HALO - k + i:HALO - k + i + 1, :], out)
    return out


def _shift_up(u, nxt, k):
    n = u.shape[0]
    rows = lax.broadcasted_iota(jnp.int32, u.shape, 0)
    out = pltpu.roll(u, n - k, 0)
    for i in range(k):
        out = jnp.where(rows == n - k + i, nxt[i:i + 1, :], out)
    return out


def _conv_gate_fwd(p, cw, name):
    s, d3 = p.shape
    d = d3 // 3
    t = _pick(s, (ROW_TILE,))
    hb = t // HALO

    def body(p_ref, prev_ref, w_ref, z_ref):
        i = pl.program_id(0)
        pv = p_ref[...].astype(F32)
        b, u = pv[:, :d], pv[:, d:2 * d] * pv[:, 2 * d:]
        ph = prev_ref[...].astype(F32)
        up = jnp.where(i > 0, ph[:, d:2 * d] * ph[:, 2 * d:], 0.0)
        w = w_ref[...]
        y = w[0:1, :] * _shift_down(u, up, 2) + w[1:2, :] * _shift_down(u, up, 1) + w[2:3, :] * u
        z_ref[...] = (b * y).astype(BF16)

    return pl.pallas_call(
        body, name=name, grid=(s // t,),
        in_specs=[_row_spec(t, d3),
                  pl.BlockSpec((HALO, d3), lambda i: (jnp.maximum(i * hb - 1, 0), 0)),
                  pl.BlockSpec((3, d), lambda i: (0, 0))],
        out_specs=_row_spec(t, d),
        out_shape=jax.ShapeDtypeStruct((s, d), BF16),
        compiler_params=_params(("parallel",)),
    )(p, p, cw)


def _conv_gate_bwd(p, dz, cw, name):
    s, d3 = p.shape
    d = d3 // 3
    t = _pick(s, (ROW_TILE,))
    hb = t // HALO
    nt = s // t
    last_halo = s // HALO - 1

    def body(p_ref, prev_ref, next_ref, dz_ref, dznext_ref, w_ref, dp_ref, dw_ref):
        i = pl.program_id(0)
        pv = p_ref[...].astype(F32)
        b, c, h = pv[:, :d], pv[:, d:2 * d], pv[:, 2 * d:]
        u = c * h
        ph = prev_ref[...].astype(F32)
        up = jnp.where(i > 0, ph[:, d:2 * d] * ph[:, 2 * d:], 0.0)
        w = w_ref[...]
        u1, u2 = _shift_down(u, up, 1), _shift_down(u, up, 2)
        y = w[0:1, :] * u2 + w[1:2, :] * u1 + w[2:3, :] * u
        dz = dz_ref[...].astype(F32)
        dy = dz * b
        dyn = jnp.where(i < nt - 1, dznext_ref[...].astype(F32) * next_ref[...].astype(F32)[:, :d], 0.0)
        du = w[2:3, :] * dy + w[1:2, :] * _shift_up(dy, dyn, 1) + w[0:1, :] * _shift_up(dy, dyn, 2)
        dp_ref[:, :d] = (dz * y).astype(BF16)
        dp_ref[:, d:2 * d] = (du * h).astype(BF16)
        dp_ref[:, 2 * d:] = (du * c).astype(BF16)

        @pl.when(i == 0)
        def _():
            dw_ref[...] = jnp.zeros_like(dw_ref)

        dw_ref[0:1, :] += jnp.sum(dy * u2, axis=0, keepdims=True)
        dw_ref[1:2, :] += jnp.sum(dy * u1, axis=0, keepdims=True)
        dw_ref[2:3, :] += jnp.sum(dy * u, axis=0, keepdims=True)

    return pl.pallas_call(
        body, name=name, grid=(nt,),
        in_specs=[_row_spec(t, d3),
                  pl.BlockSpec((HALO, d3), lambda i: (jnp.maximum(i * hb - 1, 0), 0)),
                  pl.BlockSpec((HALO, d3), lambda i: (jnp.minimum((i + 1) * hb, last_halo), 0)),
                  _row_spec(t, d),
                  pl.BlockSpec((HALO, d), lambda i: (jnp.minimum((i + 1) * hb, last_halo), 0)),
                  pl.BlockSpec((3, d), lambda i: (0, 0))],
        out_specs=[_row_spec(t, d3), pl.BlockSpec((3, d), lambda i: (0, 0))],
        out_shape=[jax.ShapeDtypeStruct((s, d3), BF16), jax.ShapeDtypeStruct((3, d), F32)],
        compiler_params=_params(("arbitrary",)),
    )(p, p, p, dz, dz, cw)


FFN_ROWS, FFN_COLS = (512, 256), (1408, 768, 256, 128)


def _row_chunks(tm, rows=256):
    return [slice(r, r + min(rows, tm)) for r in range(0, tm, min(rows, tm))]


def _ffn_in_swiglu(xn, w_in, name):
    s, k = xn.shape
    ff = w_in.shape[1] // 2
    tm, tn = _pick(s, FFN_ROWS), _pick(ff, FFN_COLS)
    nj = ff // tn

    def body(x_ref, wg_ref, wu_ref, f_ref, a_ref):
        for rows in _row_chunks(tm):
            xv = x_ref[rows, :]
            gate = jnp.dot(xv, wg_ref[...], preferred_element_type=F32)
            up = jnp.dot(xv, wu_ref[...], preferred_element_type=F32)
            f_ref[0, rows, :] = gate.astype(BF16)
            f_ref[1, rows, :] = up.astype(BF16)
            a_ref[rows, :] = (gate * jax.nn.sigmoid(gate) * up).astype(BF16)

    return pl.pallas_call(
        body, name=name, grid=(nj, s // tm),
        in_specs=[pl.BlockSpec((tm, k), lambda j, i: (i, 0)),
                  pl.BlockSpec((k, tn), lambda j, i: (0, j)),
                  pl.BlockSpec((k, tn), lambda j, i: (0, nj + j))],
        out_specs=[pl.BlockSpec((2, tm, tn), lambda j, i: (0, i, j)), pl.BlockSpec((tm, tn), lambda j, i: (i, j))],
        out_shape=[jax.ShapeDtypeStruct((2, s, ff), BF16), jax.ShapeDtypeStruct((s, ff), BF16)],
        compiler_params=_params(("parallel", "parallel")),
    )(xn, w_in, w_in)


def _ffn_out_dx_swiglu(dff, w_out, f, name):
    s, d = dff.shape
    ff = w_out.shape[0]
    tm, tn = _pick(s, FFN_ROWS), _pick(ff, FFN_COLS)

    def body(d_ref, w_ref, f_ref, df_ref):
        for rows in _row_chunks(tm):
            da = lax.dot_general(d_ref[rows, :], w_ref[...], (((1,), (1,)), ((), ())), preferred_element_type=F32)
            gate = f_ref[0, rows, :].astype(F32)
            up = f_ref[1, rows, :].astype(F32)
            sg = jax.nn.sigmoid(gate)
            silu = gate * sg
            df_ref[0, rows, :] = (da * up * (sg + silu * (1.0 - sg))).astype(BF16)
            df_ref[1, rows, :] = (da * silu).astype(BF16)

    planes = pl.BlockSpec((2, tm, tn), lambda j, i: (0, i, j))
    return pl.pallas_call(
        body, name=name, grid=(ff // tn, s // tm),
        in_specs=[pl.BlockSpec((tm, d), lambda j, i: (i, 0)), pl.BlockSpec((tn, d), lambda j, i: (j, 0)), planes],
        out_specs=planes, out_shape=jax.ShapeDtypeStruct((2, s, ff), BF16),
        compiler_params=_params(("parallel", "parallel")),
    )(dff, w_out, f)


SUPER = 2048
RES = 16
PAIR = 128
L = 128
FWD_TOGETHER = 16
BWD_TOGETHER = 4


def _alibi_slopes(n_heads):
    h = np.arange(n_heads, dtype=np.float32) + 1.0
    return np.power(2.0, -8.0 * h / n_heads).astype(np.float32)


def _permute16(x, inverse, name):
    s, d = x.shape
    cw = LANE

    def body(x_ref, o_ref):
        if inverse:
            for m in range(L):
                o_ref[RES * m:RES * (m + 1), :] = x_ref[pl.ds(m, RES, stride=L), :]
        else:
            for r in range(RES):
                o_ref[L * r:L * (r + 1), :] = x_ref[pl.ds(r, L, stride=RES), :]

    spec = pl.BlockSpec((SUPER, cw), lambda i, j: (i, j))
    return pl.pallas_call(
        body, name=name, grid=(s // SUPER, d // cw), in_specs=[spec], out_specs=spec,
        out_shape=jax.ShapeDtypeStruct((s, d), x.dtype),
        compiler_params=_params(("parallel", "parallel")),
    )(x)


def _slope_table(d):
    nh = d // HEAD_DIM
    sl = _alibi_slopes(nh)
    tab = np.repeat(sl, HEAD_DIM).reshape(d // PAIR, 1, PAIR)
    return jnp.asarray(np.broadcast_to(tab, (d // PAIR, 8, PAIR)).copy())


def _geometry(dil):
    nch = RES // dil
    return nch, L // nch


def _band(dil):
    nch, w = _geometry(dil)
    sh = w.bit_length() - 1
    i = lax.broadcasted_iota(jnp.int32, (L, 2 * L), 0)
    j = lax.broadcasted_iota(jnp.int32, (L, 2 * L), 1)

    def pos(t):
        return jnp.bitwise_and(t, w - 1) * nch + jnp.right_shift(t, sh)

    delta = pos(i) + L - (pos(jnp.bitwise_and(j, L - 1)) + jnp.bitwise_and(j, L))
    return (delta * dil).astype(F32), (delta >= 0) & (delta <= L), j < L


def _fill_bias(bias_s, sl_ref):
    for b, dil in enumerate(DILATIONS):
        base, band, prev_half = _band(dil)
        for first in range(2):
            valid = band & jnp.logical_not(prev_half) if first else band
            for h in range(2):
                slope = sl_ref[0:1, HEAD_DIM * h:HEAD_DIM * h + 1]
                bias_s[(2 * b + first) * 2 + h] = jnp.where(valid, -slope * base, NEG_BIG)


def _bias_index(b, sb, n):
    first = jnp.logical_and(sb == 0, n == 0).astype(jnp.int32)
    return (2 * b + first) * 2


def _offsets(dil, res, n):
    nch, w = _geometry(dil)

    def al(v):
        return v if isinstance(v, int) else pl.multiple_of(v, w)

    q_off = [al((a * dil + res) * L + n * w) for a in range(nch)]
    k_off = [al((a * dil + res) * 2 * L + L + n * w) for a in range(nch)]
    kp_off = [al((a * dil + res) * 2 * L + L + n * w - w) for a in range(nch)]
    return q_off, k_off, kp_off, w


def _gather(ref, offs, w):
    parts = [ref[pl.ds(o, w), :] for o in offs]
    return parts[0] if len(parts) == 1 else jnp.concatenate(parts, axis=0)


def _scatter(ref, offs, w, val, add=False):
    for a, o in enumerate(offs):
        piece = val[a * w:(a + 1) * w, :]
        if add:
            ref[pl.ds(o, w), :] += piece
        else:
            ref[pl.ds(o, w), :] = piece


def _fill_key_buffer(buf, prev_ref, cur_ref):
    for r in range(RES):
        buf[2 * L * r:2 * L * r + L, :] = prev_ref[L * r:L * (r + 1), :]
        buf[2 * L * r + L:2 * L * (r + 1), :] = cur_ref[L * r:L * (r + 1), :]


def _two_heads(x, low):
    zero = jnp.zeros_like(x)
    return jnp.concatenate([jnp.where(low, x, zero), jnp.where(low, zero, x)], axis=0)


def _loop_blocks(dil, stages, together):
    together = max(together, dil) if dil < RES else together

    def it(i, c):
        if dil == RES:
            blocks = [(i * together + k, 0) for k in range(together)]
        else:
            blocks = [(res, i * (together // dil) + k) for k in range(together // dil) for res in range(dil)]
        state = [stages[0](res, n) for res, n in blocks]
        for stage in stages[1:]:
            state = [stage(res, n, prev) for (res, n), prev in zip(blocks, state)]
        for writes in state:
            for args in writes:
                _scatter(*args)
        return c

    lax.fori_loop(0, RES // together, it, 0)


NT = (((1,), (1,)), ((), ()))
TN = (((0,), (0,)), ((), ()))


def _attention_fwd(q, kv, name):
    s, d = q.shape
    g_n, ns = d // PAIR, s // SUPER

    def body(sl_ref, q_ref, kc_ref, kp_ref, vc_ref, vp_ref, o_ref, lse_ref, kbuf, vbuf, m_s, l_s, acc_s, bias_s):
        sb = pl.program_id(1)
        _fill_key_buffer(kbuf, kp_ref, kc_ref)
        _fill_key_buffer(vbuf, vp_ref, vc_ref)
        pl.when(sb == 0)(lambda: _fill_bias(bias_s, sl_ref))
        low = lax.broadcasted_iota(jnp.int32, (L, PAIR), 1) < HEAD_DIM
        low_k = lax.broadcasted_iota(jnp.int32, (2 * L, PAIR), 1) < HEAD_DIM
        ones_bd = _two_heads(jnp.ones((2 * L, PAIR), BF16), low_k)
        for bi, dil in enumerate(DILATIONS):
            first_branch, last_branch = bi == 0, bi == len(DILATIONS) - 1

            def scores(res, n, dil=dil):
                q_off, k_off, kp_off, w = _offsets(dil, res, n)
                qf = _gather(q_ref, q_off, w).astype(BF16)
                kcat = jnp.concatenate([_gather(kbuf, kp_off, w), _gather(kbuf, k_off, w)], axis=0).astype(BF16)
                return lax.dot_general(qf, _two_heads(kcat, low_k), NT, preferred_element_type=F32)

            def update(res, n, sc, bi=bi, dil=dil, first_branch=first_branch, last_branch=last_branch):
                q_off, k_off, kp_off, w = _offsets(dil, res, n)
                vcat = jnp.concatenate([_gather(vbuf, kp_off, w), _gather(vbuf, k_off, w)], axis=0).astype(BF16)
                v_ones = jnp.concatenate([_two_heads(vcat, low_k), ones_bd], axis=1)
                bias_at = _bias_index(bi, sb, n)
                if not first_branch:
                    m_prev = _gather(m_s, q_off, w)
                ps, m_new = [], []
                for h in range(2):
                    s_h = sc[:, 2 * L * h:2 * L * (h + 1)] + bias_s[bias_at + h]
                    mh = jnp.max(s_h, axis=1, keepdims=True)
                    if not first_branch:
                        mh = jnp.maximum(mh, m_prev[:, HEAD_DIM * h:HEAD_DIM * h + 1])
                    ps.append(jnp.exp(s_h - mh).astype(BF16))
                    m_new.append(mh)
                m_full = jnp.where(low, m_new[0], m_new[1])
                both = jnp.dot(jnp.concatenate(ps, axis=1), v_ones, preferred_element_type=F32)
                acc, l_full = both[:, :PAIR], both[:, PAIR:]
                if not first_branch:
                    alpha = jnp.exp(m_prev - m_full)
                    l_full = _gather(l_s, q_off, w) * alpha + l_full
                    acc = _gather(acc_s, q_off, w) * alpha + acc
                if last_branch:
                    return [(o_ref, q_off, w, acc / l_full, False), (lse_ref, q_off, w, m_full + jnp.log(l_full), False)]
                return [(m_s, q_off, w, m_full, False), (l_s, q_off, w, l_full, False), (acc_s, q_off, w, acc, False)]

            _loop_blocks(dil, [scores, update], FWD_TOGETHER)

    prev = lambda i: jnp.maximum(i - 1, 0)
    blk = pl.BlockSpec((SUPER, PAIR), lambda g, i: (i, g))
    in_specs = [pl.BlockSpec((None, 8, PAIR), lambda g, i: (g, 0, 0)), blk,
                pl.BlockSpec((SUPER, PAIR), lambda g, i: (i, g)),
                pl.BlockSpec((SUPER, PAIR), lambda g, i: (prev(i), g)),
                pl.BlockSpec((SUPER, PAIR), lambda g, i: (i, g_n + g)),
                pl.BlockSpec((SUPER, PAIR), lambda g, i: (prev(i), g_n + g))]
    return pl.pallas_call(
        body, name=name, grid=(g_n, ns), in_specs=in_specs, out_specs=[blk, blk],
        out_shape=[jax.ShapeDtypeStruct((s, d), F32)] * 2,
        scratch_shapes=([pltpu.VMEM((2 * SUPER, PAIR), F32)] * 2 + [pltpu.VMEM((SUPER, PAIR), F32)] * 3
                        + [pltpu.VMEM((4 * len(DILATIONS), L, 2 * L), F32)]),
        compiler_params=_params(("parallel", "arbitrary")),
    )(_slope_table(d), q, kv, kv, kv, kv)


def _attention_bwd(q, kv, o, do, lse, dkv_in, name):
    s, d = q.shape
    g_n, ns = d // PAIR, s // SUPER
    has_in = dkv_in is not None

    def body(*refs):
        sl_ref, q_ref, do_ref, o_ref, lse_ref, kc_ref, kp_ref, vc_ref, vp_ref = refs[:9]
        pos = 9
        if has_in:
            dkv_in_ref = refs[9]
            pos = 10
        dq_ref, dkv_ref, kbuf, vbuf, dkbuf, dvbuf, dq_s, bias_s = refs[pos:]
        step = pl.program_id(1)
        sb = ns - 1 - step
        _fill_key_buffer(kbuf, kp_ref, kc_ref)
        _fill_key_buffer(vbuf, vp_ref, vc_ref)

        @pl.when(step == 0)
        def _():
            _fill_bias(bias_s, sl_ref)
            dkbuf[...] = jnp.zeros_like(dkbuf)
            dvbuf[...] = jnp.zeros_like(dvbuf)

        @pl.when(step > 0)
        def _():
            for buf in (dkbuf, dvbuf):
                for r in range(RES):
                    buf[2 * L * r + L:2 * L * (r + 1), :] = buf[2 * L * r:2 * L * r + L, :]
                    buf[2 * L * r:2 * L * r + L, :] = jnp.zeros((L, PAIR), F32)

        low = lax.broadcasted_iota(jnp.int32, (L, PAIR), 1) < HEAD_DIM
        low_k = lax.broadcasted_iota(jnp.int32, (2 * L, PAIR), 1) < HEAD_DIM
        low_t = lax.broadcasted_iota(jnp.int32, (PAIR, 2 * L), 0) < HEAD_DIM
        for bi, dil in enumerate(DILATIONS):
            first_branch = bi == 0

            def scores(res, n, dil=dil):
                q_off, k_off, kp_off, w = _offsets(dil, res, n)
                qb = _gather(q_ref, q_off, w).astype(BF16)
                dof = _gather(do_ref, q_off, w)
                prod = dof * _gather(o_ref, q_off, w)
                dob = dof.astype(BF16)
                lse_f = _gather(lse_ref, q_off, w)
                zero = jnp.zeros_like(prod)
                dsum = (jnp.sum(jnp.where(low, prod, zero), axis=1, keepdims=True),
                        jnp.sum(jnp.where(low, zero, prod), axis=1, keepdims=True))
                kcat = jnp.concatenate([_gather(kbuf, kp_off, w), _gather(kbuf, k_off, w)], axis=0).astype(BF16)
                vcat = jnp.concatenate([_gather(vbuf, kp_off, w), _gather(vbuf, k_off, w)], axis=0).astype(BF16)
                k_bd, v_bd = _two_heads(kcat, low_k), _two_heads(vcat, low_k)
                sc = lax.dot_general(qb, k_bd, NT, preferred_element_type=F32)
                dp = lax.dot_general(dob, v_bd, NT, preferred_element_type=F32)
                return qb, dob, lse_f, dsum, k_bd, sc, dp

            def gradients(res, n, given, bi=bi, dil=dil, first_branch=first_branch):
                qb, dob, lse_f, dsum, k_bd, sc, dp = given
                q_off, k_off, kp_off, w = _offsets(dil, res, n)
                bias_at = _bias_index(bi, sb, n)
                ps, dss = [], []
                for h in range(2):
                    cols = slice(2 * L * h, 2 * L * (h + 1))
                    lse_h = lse_f[:, HEAD_DIM * h:HEAD_DIM * h + 1]
                    p_h = jnp.exp(sc[:, cols] + bias_s[bias_at + h] - lse_h)
                    dss.append((p_h * (dp[:, cols] - dsum[h])).astype(BF16))
                    ps.append(p_h.astype(BF16))
                ds_cat, p_cat = jnp.concatenate(dss, axis=1), jnp.concatenate(ps, axis=1)
                dq = jnp.dot(ds_cat, k_bd, preferred_element_type=F32)
                dk_t = lax.dot_general(qb, ds_cat, TN, preferred_element_type=F32)
                dv_t = lax.dot_general(dob, p_cat, TN, preferred_element_type=F32)
                dk = jnp.where(low_t, dk_t[:, :2 * L], dk_t[:, 2 * L:]).T
                dv = jnp.where(low_t, dv_t[:, :2 * L], dv_t[:, 2 * L:]).T
                return [(dq_s, q_off, w, dq, not first_branch),
                        (dkbuf, kp_off, w, dk[:L], True), (dkbuf, k_off, w, dk[L:], True),
                        (dvbuf, kp_off, w, dv[:L], True), (dvbuf, k_off, w, dv[L:], True)]

            _loop_blocks(dil, [scores, gradients], BWD_TOGETHER)

        dq_ref[...] = dq_s[...].astype(BF16)
        for r in range(RES):
            rows, cur = slice(L * r, L * (r + 1)), slice(2 * L * r + L, 2 * L * (r + 1))
            for plane, buf in enumerate((dkbuf, dvbuf)):
                if has_in:
                    dkv_ref[plane, rows, :] = buf[cur, :] + dkv_in_ref[plane, rows, :]
                else:
                    dkv_ref[plane, rows, :] = buf[cur, :]

    rev = lambda i: ns - 1 - i
    prev = lambda i: jnp.maximum(ns - 2 - i, 0)
    blk = pl.BlockSpec((SUPER, PAIR), lambda g, i: (rev(i), g))
    in_specs = [pl.BlockSpec((None, 8, PAIR), lambda g, i: (g, 0, 0)), blk, blk, blk, blk,
                pl.BlockSpec((SUPER, PAIR), lambda g, i: (rev(i), g)),
                pl.BlockSpec((SUPER, PAIR), lambda g, i: (prev(i), g)),
                pl.BlockSpec((SUPER, PAIR), lambda g, i: (rev(i), g_n + g)),
                pl.BlockSpec((SUPER, PAIR), lambda g, i: (prev(i), g_n + g))]
    ins = [_slope_table(d), q, do, o, lse, kv, kv, kv, kv]
    planes = pl.BlockSpec((2, SUPER, PAIR), lambda g, i: (0, rev(i), g))
    if has_in:
        in_specs.append(planes)
        ins.append(dkv_in)
    res = pl.pallas_call(
        body, name=name, grid=(g_n, ns), in_specs=in_specs, out_specs=[blk, planes],
        out_shape=[jax.ShapeDtypeStruct((s, d), BF16), jax.ShapeDtypeStruct((2, s, d), F32)],
        scratch_shapes=([pltpu.VMEM((2 * SUPER, PAIR), F32)] * 4 + [pltpu.VMEM((SUPER, PAIR), F32)]
                        + [pltpu.VMEM((4 * len(DILATIONS), L, 2 * L), F32)]),
        compiler_params=_params(("parallel", "arbitrary")),
    )(*ins)
    return res[0], res[1]


def _coords():
    return lax.axis_index("x"), lax.axis_index("y"), lax.axis_index("c")


def _chip_peers(x, y):
    return [(1 - x, y), (x, 1 - y), (1 - x, 1 - y)]


def _block_of(ref, axis, blk, size):
    start = pl.multiple_of(blk * size, size)
    if axis == 1:
        return ref.at[:, pl.ds(start, size)]
    return ref.at[pl.ds(start, size), :]


ANY = pl.BlockSpec(memory_space=pl.ANY)


HBM = pl.BlockSpec(memory_space=pltpu.HBM)
SEM = pl.BlockSpec(memory_space=pltpu.SEMAPHORE)
SPLIT = pltpu.CompilerParams(has_side_effects=pltpu.SideEffectType.DATAFLOW_SIDE_EFFECTING)


def _in_hbm(a):
    return pltpu.with_memory_space_constraint(a, pltpu.HBM)


def _thru(arrays):
    return [pltpu.HBM(a.shape, a.dtype) for a in arrays]


def _cast_place(w, layer, ax, dtype, after, name):
    _, k, n = w.shape
    t = _pick(k, (512, 256, 128))
    nb = k // t
    extra = [] if after is None else [after]

    def body(blk_ref, w_ref, *refs):
        b_ref, f_ref = refs[len(extra):]
        v = w_ref[...].astype(dtype)
        b_ref[...] = v
        f_ref[...] = v

    full_shape = (k, 4 * n) if ax == 1 else (4 * k, n)
    place = (lambda i, blk: (i, blk[0])) if ax == 1 else (lambda i, blk: (blk[0] * nb + i, 0))
    return pl.pallas_call(
        body, name=name,
        grid_spec=pltpu.PrefetchScalarGridSpec(
            num_scalar_prefetch=1, grid=(nb,),
            in_specs=[pl.BlockSpec((None, t, n), lambda i, blk: (layer, i, 0))] + [ANY] * len(extra),
            out_specs=[pl.BlockSpec((t, n), lambda i, blk: (i, 0)), pl.BlockSpec((t, n), place)]),
        out_shape=[jax.ShapeDtypeStruct((k, n), dtype), jax.ShapeDtypeStruct(full_shape, dtype)],
        compiler_params=_params(("parallel",)),
    )(_my_block()[None], w, *extra)


def _my_block():
    return (2 * lax.axis_index("x") + lax.axis_index("y")).astype(jnp.int32)


def _gather_start(group, carry, name):
    n, nc = len(group), len(carry)

    def body(*refs):
        blocks, fulls, send_sem, recv_sem = refs[:n], refs[n:2 * n], refs[2 * n + nc], refs[2 * n + nc + 1]
        x, y, c = _coords()
        for t, (b, _, ax) in enumerate(group):
            mine = _block_of(fulls[t], ax, 2 * x + y, b.shape[ax])
            for j, (px, py) in enumerate(_chip_peers(x, y)):
                pltpu.make_async_remote_copy(
                    src_ref=blocks[t], dst_ref=mine, send_sem=send_sem.at[3 * t + j], recv_sem=recv_sem.at[3 * t + j],
                    device_id=(px, py, c), device_id_type=MESH).start()

    arrays = [b for b, _, _ in group] + [f for _, f, _ in group] + list(carry)
    sems = [pltpu.SemaphoreType.DMA((3 * n,))] * 2
    res = pl.pallas_call(
        body, name=name, in_specs=[HBM] * len(arrays), out_specs=[SEM, SEM] + [HBM] * len(arrays),
        out_shape=sems + _thru(arrays), input_output_aliases={i: 2 + i for i in range(len(arrays))},
        compiler_params=SPLIT,
    )(*[_in_hbm(a) for a in arrays])
    return (res[0], res[1], list(res[2:2 + n]), list(res[2 + n:2 + 2 * n])), list(res[2 + 2 * n:])


def _gather_wait(group, started, after, name):
    sends, recvs, blocks, fulls = started
    m = len(group)

    def body(*refs):
        blk_refs, full_refs, send_sem, recv_sem = refs[:m], refs[m:2 * m], refs[2 * m], refs[2 * m + 1]
        x, y, c = _coords()
        for t, (b, _, ax) in enumerate(group):
            for j, (px, py) in enumerate(_chip_peers(x, y)):
                cp = pltpu.make_async_remote_copy(
                    src_ref=blk_refs[t], dst_ref=_block_of(full_refs[t], ax, 2 * px + py, b.shape[ax]),
                    send_sem=send_sem.at[3 * t + j], recv_sem=recv_sem.at[3 * t + j],
                    device_id=(px, py, c), device_id_type=MESH)
                cp.wait_send()
                cp.wait_recv()

    extra = [] if after is None else [after]
    res = pl.pallas_call(
        body, name=name, in_specs=[HBM] * (2 * m) + [SEM, SEM] + [ANY] * len(extra), out_specs=[HBM] * (2 * m),
        out_shape=_thru(blocks) + _thru(fulls), input_output_aliases={i: i for i in range(2 * m)},
        compiler_params=SPLIT,
    )(*blocks, *fulls, sends, recvs, *extra)
    return list(res[m:])


def _scatter_start(grads, carry, name):
    n = len(grads)
    n_in = 2 * n + len(carry)

    def body(*refs):
        g_refs, st_refs, send_sem, recv_sem = refs[:n], refs[n:2 * n], refs[n_in], refs[n_in + 1]
        x, y, c = _coords()
        for t, (g, ax) in enumerate(grads):
            for j, (px, py) in enumerate(_chip_peers(x, y)):
                pltpu.make_async_remote_copy(
                    src_ref=_block_of(g_refs[t], ax, 2 * px + py, g.shape[ax] // 4), dst_ref=st_refs[t].at[j],
                    send_sem=send_sem.at[3 * t + j], recv_sem=recv_sem.at[3 * t + j],
                    device_id=(px, py, c), device_id_type=MESH).start()

    arrays = [g for g, _ in grads]
    for g, ax in grads:
        shape = list(g.shape)
        shape[ax] //= 4
        arrays.append(lax.empty((3, *shape), g.dtype))
    arrays += list(carry)
    sems = [pltpu.SemaphoreType.DMA((3 * n,))] * 2
    res = pl.pallas_call(
        body, name=name, in_specs=[HBM] * n_in, out_specs=[SEM, SEM] + [HBM] * n_in,
        out_shape=sems + _thru(arrays), input_output_aliases={i: 2 + i for i in range(n_in)},
        compiler_params=SPLIT,
    )(*[_in_hbm(a) for a in arrays])
    return (res[0], res[1], list(res[2:2 + n]), list(res[2 + n:2 + 2 * n])), list(res[2 + 2 * n:])


def _scatter_wait(axes, started, after, name):
    sends, recvs, full, stacks = started
    n = len(full)
    extra = [] if after is None else [after]

    def body(*refs):
        g_refs, st_refs, send_sem, recv_sem = refs[:n], refs[n:2 * n], refs[2 * n], refs[2 * n + 1]
        x, y, c = _coords()
        for t, ax in enumerate(axes):
            size = full[t].shape[ax] // 4
            for j, (px, py) in enumerate(_chip_peers(x, y)):
                cp = pltpu.make_async_remote_copy(
                    src_ref=_block_of(g_refs[t], ax, 2 * px + py, size), dst_ref=st_refs[t].at[j],
                    send_sem=send_sem.at[3 * t + j], recv_sem=recv_sem.at[3 * t + j],
                    device_id=(px, py, c), device_id_type=MESH)
                cp.wait_send()
                cp.wait_recv()

    res = pl.pallas_call(
        body, name=name, in_specs=[HBM] * (2 * n) + [SEM, SEM] + [ANY] * len(extra), out_specs=[HBM] * (2 * n),
        out_shape=_thru(full) + _thru(stacks), input_output_aliases={i: i for i in range(2 * n)},
        compiler_params=SPLIT,
    )(*full, *stacks, sends, recvs, *extra)
    return list(res[:n]), list(res[n:])


def _pair_copies(g_refs, st_refs, out_refs, items, send_sem, recv_sem):
    x, y, c = _coords()
    copies = []
    for u, (g, ax, _) in enumerate(items):
        own = _block_of(g_refs[u], ax, 2 * x + y, g.shape[ax] // 4)
        for k, (src, dst) in enumerate([(own, out_refs[u].at[0]), (st_refs[u], out_refs[u].at[pl.ds(1, 3)])]):
            copies.append(pltpu.make_async_remote_copy(
                src_ref=src, dst_ref=dst, send_sem=send_sem.at[2 * u + k], recv_sem=recv_sem.at[2 * u + k],
                device_id=(x, y, 1 - c), device_id_type=MESH))
    return copies


def _pair_start(items, carry, name):
    n = len(items)
    n_in = 3 * n + len(carry)

    def body(*refs):
        for cp in _pair_copies(refs[:n], refs[n:2 * n], refs[2 * n:3 * n], items, refs[n_in], refs[n_in + 1]):
            cp.start()

    arrays = ([g for g, _, _ in items] + [st for _, _, st in items]
              + [lax.empty((4, *st.shape[1:]), st.dtype) for _, _, st in items] + list(carry))
    sems = [pltpu.SemaphoreType.DMA((2 * n,))] * 2
    res = pl.pallas_call(
        body, name=name, in_specs=[HBM] * n_in, out_specs=[SEM, SEM] + [HBM] * n_in,
        out_shape=sems + _thru(arrays), input_output_aliases={i: 2 + i for i in range(n_in)},
        compiler_params=SPLIT,
    )(*[_in_hbm(a) for a in arrays])
    thru = res[2:]
    return (res[0], res[1], *(list(thru[k * n:(k + 1) * n]) for k in range(3))), list(thru[3 * n:])


def _pair_wait(axes, started, name):
    send, recv, full, stacks, landing = started
    n = len(full)
    items = [(full[u], axes[u], stacks[u]) for u in range(n)]

    def body(*refs):
        for cp in _pair_copies(refs[:n], refs[n:2 * n], refs[2 * n:3 * n], items, refs[3 * n], refs[3 * n + 1]):
            cp.wait_send()
            cp.wait_recv()

    res = pl.pallas_call(
        body, name=name, in_specs=[HBM] * (3 * n) + [SEM, SEM], out_specs=[HBM] * (3 * n),
        out_shape=_thru(full + stacks + landing), input_output_aliases={i: i for i in range(3 * n)},
        compiler_params=SPLIT,
    )(*full, *stacks, *landing, send, recv)
    return list(res[:n]), list(res[n:2 * n]), list(res[2 * n:])


def _allreduce_small(v, name):
    r, cdim = v.shape

    def body(v_ref, out_ref, buf, send_sems, recv_sems):
        x, y, c = _coords()
        me = 4 * x + 2 * y + c
        buf[0] = v_ref[...]
        sends = []
        for k in range(1, 8):
            peer = (x if not (k & 4) else 1 - x, y if not (k & 2) else 1 - y, c if not (k & 1) else 1 - c)
            cp = pltpu.make_async_remote_copy(
                src_ref=v_ref, dst_ref=buf.at[k], send_sem=send_sems.at[k - 1], recv_sem=recv_sems.at[k - 1],
                device_id=peer, device_id_type=MESH)
            cp.start()
            sends.append(cp)
        for cp in sends:
            cp.wait_recv()
        total = buf[me]
        for src in range(1, 8):
            total = total + buf[jnp.bitwise_xor(me, src)]
        out_ref[...] = total
        for cp in sends:
            cp.wait_send()

    return pl.pallas_call(
        body, name=name,
        in_specs=[pl.BlockSpec(memory_space=pltpu.VMEM)], out_specs=pl.BlockSpec(memory_space=pltpu.VMEM),
        out_shape=jax.ShapeDtypeStruct((r, cdim), F32),
        scratch_shapes=[pltpu.VMEM((8, r, cdim), F32), pltpu.SemaphoreType.DMA((7,)), pltpu.SemaphoreType.DMA((7,))],
        compiler_params=pltpu.CompilerParams(has_side_effects=True),
    )(v)


def _adamw_math(w, g, m, v):
    m = ADAM_B1 * m + (1.0 - ADAM_B1) * g
    v = ADAM_B2 * v + (1.0 - ADAM_B2) * jnp.square(g)
    m_hat = m / (1.0 - ADAM_B1 ** ADAM_STEP)
    v_hat = v / (1.0 - ADAM_B2 ** ADAM_STEP)
    delta = -ADAM_LR * (m_hat / (jnp.sqrt(v_hat) + ADAM_EPS) + ADAM_WD * w)
    return delta, m, v


def _adamw(w, m, v, grads, name):
    r, cdim = w.shape
    paired = isinstance(grads, list)
    layers = len(grads) if paired else 1
    t = _pick(r // layers, (128, 64, 32, 16, 8))
    per_layer = r // layers // t
    n_grad = 3 * layers if paired else 1

    def body(*refs):
        refs = refs[1:] if paired else refs
        w_ref, m_ref, v_ref = refs[:3]
        outs = refs[3 + n_grad:]

        def update(g):
            delta, m_new, v_new = _adamw_math(w_ref[...], g, m_ref[...], v_ref[...])
            outs[0][...] = g
            outs[1][...] = delta
            outs[2][...] = m_new
            outs[3][...] = v_new

        if not paired:
            update(refs[3][...])
            return
        layer = pl.program_id(0) // per_layer
        for l in range(layers):
            @pl.when(layer == l)
            def _(own_ref=refs[3 + 3 * l], st_ref=refs[4 + 3 * l], sib_ref=refs[5 + 3 * l]):
                sa = own_ref[...].astype(F32)
                sb = sib_ref[0].astype(F32)
                for k in range(3):
                    sa = sa + st_ref[k].astype(F32)
                    sb = sb + sib_ref[k + 1].astype(F32)
                update(sa + sb)

    out_shape = [jax.ShapeDtypeStruct((r, cdim), F32)] * 4
    if not paired:
        spec = pl.BlockSpec((t, cdim), lambda i: (i, 0))
        return pl.pallas_call(
            body, name=name, grid=(r // t,), in_specs=[spec] * 4, out_specs=[spec] * 4, out_shape=out_shape,
            compiler_params=_params(("parallel",)),
        )(w, m, v, grads)

    spec = pl.BlockSpec((t, cdim), lambda i, blk: (i, 0))
    ins, in_specs = [w, m, v], [spec] * 3
    for l, (g, ax, stack, sib) in enumerate(grads):
        row = lambda i, l=l: jnp.clip(i - l * per_layer, 0, per_layer - 1)
        own = ((lambda i, blk, row=row: (row(i), blk[0])) if ax == 1
               else (lambda i, blk, row=row: (blk[0] * per_layer + row(i), 0)))
        ins += [g, stack, sib]
        in_specs += [pl.BlockSpec((t, cdim), own),
                     pl.BlockSpec((3, t, cdim), lambda i, blk, row=row: (0, row(i), 0)),
                     pl.BlockSpec((4, t, cdim), lambda i, blk, row=row: (0, row(i), 0))]
    return pl.pallas_call(
        body, name=name,
        grid_spec=pltpu.PrefetchScalarGridSpec(
            num_scalar_prefetch=1, grid=(r // t,), in_specs=in_specs, out_specs=[spec] * 4),
        out_shape=out_shape, compiler_params=_params(("parallel",)),
    )(_my_block()[None], *ins)


def _local_step(x, target, gains, conv_ws, kv_gain, weights_of, send_grads):
    depth = len(gains)
    n_a = len(conv_ws)
    saved, ws = [], []
    kv = kvn = None
    _, (xn,) = _norm_res_fwd(x, None, None, [gains[0][0]], "norm_first")
    h = x
    for l in range(depth):
        g = gains[l]
        sv = {"x_in": h, "xn": xn}
        w = weights_of(l, "mix", h)
        ws.append(w)
        if l == n_a:
            kv = _mm(kvn, w["kv"], "nn", F32, "kv_fwd")
        if l < n_a:
            p = _mm(xn, w["conv_in"], "nn", BF16, f"conv_in_fwd_{l}")
            z = _conv_gate_fwd(p, conv_ws[l], f"conv_gate_fwd_{l}")
            mix, x1, xn2 = _mm_norm_res(z, w["conv_out"], h, g[1], g[2], f"conv_out_fwd_{l}")
            sv.update(p=p, z=z)
        else:
            j = l - n_a
            q = _mm(xn, w["q"], "nn", F32, f"q_fwd_{j}", scale=HEAD_DIM ** -0.5)
            o, lse = _attention_fwd(q, kv, f"attn_fwd_{j}")
            mix, x1, xn2 = _mm_norm_res(o, w["o"], h, g[1], g[2], f"o_fwd_{j}")
            sv.update(q=q, o=o, lse=lse)
        w.update(weights_of(l, "ffn", mix))
        f, a = _ffn_in_swiglu(xn2, w["ffn_in"], f"ffn_in_fwd_{l}")
        ff = _mm(a, w["ffn_out"], "nn", BF16, f"ffn_out_fwd_{l}")
        sv.update(mix=mix, x1=x1, xn2=xn2, f=f, a=a, ff=ff)
        saved.append(sv)
        if l == depth - 1:
            dx, loss = _norm_res_loss(x1, ff, g[3], target, "norm_loss")
        elif l == n_a - 1:
            h, _ = _norm_res_fwd(x1, ff, g[3], [], f"norm_end_{l}")
            h = _permute16(h, False, "permute_stream")
            _, (xn, kvn) = _norm_res_fwd(h, None, None, [gains[l + 1][0], kv_gain], "norm_permuted")
        else:
            h, (xn,) = _norm_res_fwd(x1, ff, g[3], [gains[l + 1][0]], f"norm_end_{l}")
    d_gains = [[None] * 4 for _ in range(depth)]
    d_conv = [None] * n_a
    d_kv_gain = None
    dkv = None
    _, _, dff, d_gains[depth - 1][3] = _norm_bwd(dx, [], None, (saved[-1]["ff"], gains[-1][3]), "norm_loss_bwd")
    for l in reversed(range(depth)):
        sv, g, w, grads = saved[l], gains[l], ws[l], {}
        grads["ffn_out"] =_mm(sv["a"], dff, "tn", BF16, f"ffn_out_dw_{l}")
        df = _ffn_out_dx_swiglu(dff, w["ffn_out"], sv["f"], f"ffn_out_dx_{l}")
        dxn2 = _mm(df, w["ffn_in"], "nt", BF16, f"ffn_in_dx_{l}")
        grads["ffn_in"] =_mm(sv["xn2"], df, "tn", BF16, f"ffn_in_dw_{l}")
        dx, (d_gains[l][2],), dmix, d_gains[l][1] = _norm_bwd(
            dx, [(dxn2, g[2])], sv["x1"], (sv["mix"], g[1]), f"norm_mid_bwd_{l}")
        dx, dmix = send_grads(l, "ffn", grads, [dx, dmix])
        if l < n_a:
            dz = _mm(dmix, w["conv_out"], "nt", BF16, f"conv_out_dx_{l}")
            grads["conv_out"] =_mm(sv["z"], dmix, "tn", BF16, f"conv_out_dw_{l}")
            dp, d_conv[l] = _conv_gate_bwd(sv["p"], dz, conv_ws[l], f"conv_gate_bwd_{l}")
            dxn = _mm(dp, w["conv_in"], "nt", BF16, f"conv_in_dx_{l}")
            grads["conv_in"] =_mm(sv["xn"], dp, "tn", BF16, f"conv_in_dw_{l}")
        else:
            j = l - n_a
            do = _mm(dmix, w["o"], "nt", F32, f"o_dx_{j}")
            grads["o"] =_mm(sv["o"], dmix, "tn", BF16, f"o_dw_{j}")
            dq, dkv = _attention_bwd(sv["q"], kv, sv["o"], do, sv["lse"], dkv, f"attn_bwd_{j}")
            scale = HEAD_DIM ** -0.5
            dxn = _mm(dq, w["q"], "nt", BF16, f"q_dx_{j}", scale=scale)
            grads["q"] =_mm(sv["xn"], dq, "tn", BF16, f"q_dw_{j}", scale=scale)
        branches = [(dxn, g[0])]
        if l == n_a:
            dkvn = _mm(dkv, w["kv"], "nt", BF16, "kv_dx")
            grads["kv"] =_mm(kvn, dkv, "tn", BF16, "kv_dw")
            branches.append((dkvn, kv_gain))
        post = (saved[l - 1]["ff"], gains[l - 1][3]) if l > 0 else None
        if l == n_a:
            dx, dgs, _, _ = _norm_bwd(dx, branches, sv["x_in"], None, f"norm_end_bwd_{l}")
            dx = _permute16(dx, True, "unpermute_stream")
            _, _, dff, dg_post = _norm_bwd(dx, [], None, post, "norm_boundary_bwd")
        else:
            dx, dgs, dff, dg_post = _norm_bwd(dx, branches, sv["x_in"], post, f"norm_end_bwd_{l}")
        if dff is None:
            send_grads(l, "mix", grads, [])
        else:
            dx, dff = send_grads(l, "mix", grads, [dx, dff])
        d_gains[l][0] = dgs[0]
        if l == n_a:
            d_kv_gain = dgs[1]
        if l > 0:
            d_gains[l - 1][3] = dg_post
    return loss, dx, d_gains, d_conv, d_kv_gain


BIG = (
    ("conv_in", 1), ("conv_out", 0), ("kv", 1), ("q", 0), ("o", 0), ("ffn_in", 1), ("ffn_out", 0))


def kernel(x, norm_g, conv_in_w, conv_w, conv_out_w, kv_norm_g, kv_w, q_w, o_w, ffn_in_w, ffn_out_w, loss_target, m_norm_g, m_conv_in_w, m_conv_w, m_conv_out_w, m_kv_norm_g, m_kv_w, m_q_w, m_o_w, m_ffn_in_w, m_ffn_out_w, v_norm_g, v_conv_in_w, v_conv_w, v_conv_out_w, v_kv_norm_g, v_kv_w, v_q_w, v_o_w, v_ffn_in_w, v_ffn_out_w):
    depth, _, dq = norm_g.shape
    d = 4 * dq
    n_a = conv_w.shape[0]
    big_w = {"conv_in": conv_in_w, "conv_out": conv_out_w, "kv": kv_w[None], "q": q_w, "o": o_w,
             "ffn_in": ffn_in_w, "ffn_out": ffn_out_w}
    big_m = {"conv_in": m_conv_in_w, "conv_out": m_conv_out_w, "kv": m_kv_w[None], "q": m_q_w, "o": m_o_w,
             "ffn_in": m_ffn_in_w, "ffn_out": m_ffn_out_w}
    big_v = {"conv_in": v_conv_in_w, "conv_out": v_conv_out_w, "kv": v_kv_w[None], "q": v_q_w, "o": v_o_w,
             "ffn_in": v_ffn_in_w, "ffn_out": v_ffn_out_w}

    n_gain, n_tap = depth * 4, n_a * conv_w.shape[1]
    small_rows = -(-(n_gain + n_tap + 1) // 8) * 8
    pad_rows = small_rows - n_gain - n_tap

    def pack_small(gains, taps):
        return jnp.concatenate([gains.reshape(n_gain, dq), taps.reshape(n_tap, dq), jnp.zeros((pad_rows, dq), F32)])

    axis_of = dict(BIG)

    def matrices_of(l, part):
        if part == "ffn":
            return [("ffn_in", l), ("ffn_out", l)]
        if l < n_a:
            return [("conv_in", l), ("conv_out", l)]
        return ([("kv", 0)] if l == n_a else []) + [("q", l - n_a), ("o", l - n_a)]

    halves = [(l, part) for l in range(depth) for part in ("mix", "ffn")]
    def placed(half, after):
        return [(*_cast_place(big_w[name], i, axis_of[name], BF16, after, f"place_{name}_{i}"), axis_of[name])
                for name, i in matrices_of(*half)]

    groups = {halves[0]: placed(halves[0], None)}
    groups[halves[0]].append((*_cast_place(pack_small(norm_g, conv_w)[None], 0, 1, F32, None, "place_small"), 1))
    started = {halves[0]: _gather_start(groups[halves[0]], [], "gather_start_0_mix")[0]}
    for half in halves[1:]:
        groups[half] = placed(half, started[halves[0]][2][0])

    AHEAD = 2

    def fetch(half, after):
        full = _gather_wait(groups[half], started[half], after, "gather_wait_%d_%s" % half)
        nxt = halves.index(half) + AHEAD
        if nxt < len(halves):
            started[halves[nxt]], full = _gather_start(groups[halves[nxt]], full, "gather_start_%d_%s" % halves[nxt])
        return full

    target = _permute16(loss_target.reshape(x.shape[1:]), False, "permute_target")
    for half in halves[1:AHEAD]:
        started[half], (target,) = _gather_start(groups[half], [target], "gather_start_%d_%s" % half)
    first = fetch(halves[0], target)
    small = first[-1]
    gains = [[small[4 * l + i][None] for i in range(4)] for l in range(depth)]
    conv_ws = [small[n_gain + 3 * l:n_gain + 3 * l + 3] for l in range(n_a)]
    kv_gain = kv_norm_g[None]

    def weights_of(l, part, after):
        full = first if (l, part) == halves[0] else fetch((l, part), after)
        return {name: full[t] for t, (name, _) in enumerate(matrices_of(l, part))}

    sent, paired = {}, {}
    LAG = 2

    def to_sibling(half, carry, after):
        axes = [axis_of[name] for name, _ in matrices_of(*half)]
        full, stacks = _scatter_wait(axes, sent[half], after, "scatter_wait_%d_%s" % half)
        paired[half], carry = _pair_start(list(zip(full, axes, stacks)), carry, "pair_start_%d_%s" % half)
        return carry

    def send_grads(l, part, grads, carry):
        sent[l, part], carry = _scatter_start(
            [(grads[name], axis_of[name]) for name, _ in matrices_of(l, part)], carry, f"scatter_start_{l}_{part}")
        at = halves.index((l, part))
        if carry:
            if at + LAG < len(halves):
                carry = to_sibling(halves[at + LAG], carry, carry[0])
        else:
            for older in range(at + LAG, at, -1):
                if halves[older] not in paired:
                    to_sibling(halves[older], [], sent[l, part][2][0])
        if at == 1 and carry:
            carry = to_sibling(halves[at + 1], carry, carry[0])
        return carry

    loss, dx, d_gains, d_conv, d_kv_gain = _local_step(
        x.reshape(x.shape[1:]), target, gains, conv_ws, kv_gain, weights_of, send_grads)
    loss = lax.psum(loss, ("x", "y", "c"))

    small_g = jnp.concatenate([dg for row in d_gains for dg in row] + list(d_conv) + [d_kv_gain]
                              + [jnp.zeros((pad_rows - 1, d), F32)])
    small_g = _allreduce_small(small_g, "allreduce_small")
    blk = 2 * lax.axis_index("x") + lax.axis_index("y")
    mine_small = lax.dynamic_slice_in_dim(small_g, blk * dq, dq, axis=1)
    kv_rows = d // dq

    def pack_opt(gains_like, taps_like, kv_like):
        rows = jnp.concatenate([gains_like.reshape(n_gain, dq), taps_like.reshape(n_tap, dq), kv_like.reshape(kv_rows, dq)])
        extra = -rows.shape[0] % 8
        return jnp.concatenate([rows, jnp.zeros((extra, dq), F32)]) if extra else rows

    sw = pack_opt(norm_g, conv_w, kv_norm_g)
    sm = pack_opt(m_norm_g, m_conv_w, m_kv_norm_g)
    sv = pack_opt(v_norm_g, v_conv_w, v_kv_norm_g)
    sg = pack_opt(mine_small[:n_gain], mine_small[n_gain:n_gain + n_tap], small_g[n_gain + n_tap])
    s_out = _adamw(sw, sm, sv, sg, "adamw_small")

    def unpack(a):
        return (a[:n_gain].reshape(depth, 4, dq), a[n_gain:n_gain + n_tap].reshape(n_a, -1, dq),
                a[n_gain + n_tap:n_gain + n_tap + kv_rows].reshape(d))

    small_out = [unpack(a) for a in s_out]

    landed, big_out = {}, {}

    def update(name):
        for half in halves:
            if matrices_of(*half)[0] not in landed and any(n == name for n, _ in matrices_of(*half)):
                axes = [axis_of[n] for n, _ in matrices_of(*half)]
                landed.update(zip(matrices_of(*half), zip(*_pair_wait(axes, paired[half], "pair_wait_%d_%s" % half))))
        shp, ax = big_w[name].shape, axis_of[name]
        rows, cols = shp[0] * shp[1], shp[2]
        flat = lambda a: a.reshape(rows, cols)
        full, stacks, sibling = zip(*[landed[name, i] for i in range(shp[0])])
        res = _adamw(flat(big_w[name]), flat(big_m[name]), flat(big_v[name]),
                     [(full[i], ax, stacks[i], sibling[i]) for i in range(shp[0])], f"adamw_{name}")
        big_out[name] = [a.reshape(shp[1:] if name == "kv" else shp) for a in res]

    pending = [half for half in reversed(halves) if half not in paired]
    for half in pending[:-1]:
        to_sibling(half, [], None)
    late = [name for name, _ in BIG if any(n == name for n, _ in matrices_of(*pending[-1]))]
    for name, _ in BIG:
        if name not in late:
            update(name)
    to_sibling(pending[-1], [], big_out["ffn_out"][0])
    for name in late:
        update(name)

    def leaves(i):
        ng, cw_, kg = small_out[i]
        return [ng, big_out["conv_in"][i], cw_, big_out["conv_out"][i], kg, big_out["kv"][i], big_out["q"][i],
                big_out["o"][i], big_out["ffn_in"][i], big_out["ffn_out"][i]]

    return (loss, dx.reshape(x.shape), *leaves(0), *leaves(1), *leaves(2), *leaves(3))
```

```python
import jax
import jax.numpy as jnp
import numpy as np
from jax import lax
from jax.experimental import pallas as pl
from jax.experimental.pallas import tpu as pltpu

F32 = jnp.float32
BF16 = jnp.bfloat16
HEAD_DIM = 64
DILATIONS = (1, 4, 16)
NORM_EPS = 1e-6
NEG_BIG = -1e30
VMEM_LIMIT = 48 * 1024 * 1024
ROW_TILE = 256
NORM_TILE = 512
LANE = 128
MESH = pl.DeviceIdType.MESH

ADAM_LR = 0.001
ADAM_B1 = 0.9
ADAM_B2 = 0.999
ADAM_EPS = 1e-08
ADAM_WD = 0.01
ADAM_STEP = 10

TILE_CANDIDATES = (1024, 1408, 768, 512, 384, 256, 128)


def _pick(dim, cands=TILE_CANDIDATES):
    for c in cands:
        if c <= dim and dim % c == 0:
            return c
    return dim


def _params(sem):
    return pltpu.CompilerParams(dimension_semantics=sem, vmem_limit_bytes=VMEM_LIMIT)


def _mm(a, b, mode, out_dtype, name, scale=None):
    a_planes = a.shape[0] if a.ndim == 3 else 1
    b_planes = b.shape[0] if b.ndim == 3 else 1
    if mode == "nn":
        m, k = a.shape[-2], a.shape[-1] * a_planes
        n = b.shape[1]
    elif mode == "nt":
        m, k = a.shape[-2], a.shape[-1] * a_planes
        n = b.shape[0]
    else:
        k, m = a.shape
        n = b.shape[-1] * b_planes
    tm, tn = _pick(m), _pick(n // b_planes)
    tk = _pick(k // a_planes, ((2048,) if mode == "tn" else (3072, 2816)) + TILE_CANDIDATES)
    nk = k // tk
    ka, nb = k // a_planes // tk, n // b_planes // tn
    if a_planes > 1:
        a_spec = pl.BlockSpec((None, tm, tk), lambda i, j, kk: (kk // ka, i, kk % ka))
    elif mode == "tn":
        a_spec = pl.BlockSpec((tk, tm), lambda i, j, kk: (kk, i))
    else:
        a_spec = pl.BlockSpec((tm, tk), lambda i, j, kk: (i, kk))
    if mode == "nn":
        b_spec = pl.BlockSpec((tk, tn), lambda i, j, kk: (kk, j))
        dims = (((1,), (0,)), ((), ()))
    elif mode == "nt":
        b_spec = pl.BlockSpec((tn, tk), lambda i, j, kk: (j, kk))
        dims = (((1,), (1,)), ((), ()))
    else:
        b_spec = (pl.BlockSpec((None, tk, tn), lambda i, j, kk: (j // nb, kk, j % nb)) if b_planes > 1
                  else pl.BlockSpec((tk, tn), lambda i, j, kk: (kk, j)))
        dims = (((0,), (0,)), ((), ()))

    def finish(acc):
        if scale is not None:
            acc = acc * scale
        return acc.astype(out_dtype)

    if nk == 1:
        def body(a_ref, b_ref, o_ref):
            o_ref[...] = finish(lax.dot_general(a_ref[...].astype(BF16), b_ref[...].astype(BF16), dims, preferred_element_type=F32))
        scratch = []
    else:
        def body(a_ref, b_ref, o_ref, acc_ref):
            kk = pl.program_id(2)

            @pl.when(kk == 0)
            def _():
                acc_ref[...] = jnp.zeros_like(acc_ref)

            acc_ref[...] += lax.dot_general(a_ref[...].astype(BF16), b_ref[...].astype(BF16), dims, preferred_element_type=F32)

            @pl.when(kk == nk - 1)
            def _():
                o_ref[...] = finish(acc_ref[...])
        scratch = [pltpu.VMEM((tm, tn), F32)]

    return pl.pallas_call(
        body, name=name,
        grid=(m // tm, n // tn, nk),
        in_specs=[a_spec, b_spec],
        out_specs=pl.BlockSpec((tm, tn), lambda i, j, kk: (i, j)),
        out_shape=jax.ShapeDtypeStruct((m, n), out_dtype),
        scratch_shapes=scratch,
        compiler_params=_params(("parallel", "parallel", "arbitrary")),
    )(a, b)


def _rstd(v):
    return lax.rsqrt(jnp.mean(v * v, axis=-1, keepdims=True) + NORM_EPS)


def _rms_bwd(dy, v, g, r):
    xhat = v * r
    gy = dy * g
    dv = r * (gy - xhat * jnp.mean(gy * xhat, axis=-1, keepdims=True))
    return dv, dy * xhat


def _row_spec(t, width):
    return pl.BlockSpec((t, width), lambda i: (i, 0))


def _gain_spec(width):
    return pl.BlockSpec((1, width), lambda i: (0, 0))


def _norm_res_fwd(x, mix, g_post, pre_gains, name):
    s, d = x.shape
    t = _pick(s, (NORM_TILE, ROW_TILE))
    has_mix = mix is not None
    n_pre = len(pre_gains)

    def body(*refs):
        x_ref = refs[0]
        pos = 1
        x1 = x_ref[...]
        if has_mix:
            mv = refs[1][...].astype(F32)
            x1 = x1 + mv * _rstd(mv) * refs[2][...]
            pos = 3
        gains = refs[pos:pos + n_pre]
        outs = refs[pos + n_pre:]
        if has_mix:
            outs[0][...] = x1
            outs = outs[1:]
        r = _rstd(x1)
        for g_ref, o_ref in zip(gains, outs):
            o_ref[...] = (x1 * r * g_ref[...]).astype(BF16)

    ins = [x] + ([mix, g_post] if has_mix else []) + list(pre_gains)
    in_specs = [_row_spec(t, d)] + ([_row_spec(t, d), _gain_spec(d)] if has_mix else []) + [_gain_spec(d)] * n_pre
    out_shape = ([jax.ShapeDtypeStruct((s, d), F32)] if has_mix else []) + [jax.ShapeDtypeStruct((s, d), BF16)] * n_pre
    out_specs = [_row_spec(t, d)] * len(out_shape)
    res = pl.pallas_call(
        body, name=name, grid=(s // t,), in_specs=in_specs, out_specs=out_specs, out_shape=out_shape,
        compiler_params=_params(("parallel",)),
    )(*ins)
    if has_mix:
        return res[0], list(res[1:])
    return x, list(res)


def _mm_norm_res(a, b, x, g_post, pre_gain, name):
    m, k = a.shape
    d = b.shape[1]
    tm, tk = _pick(m, (1024, 512, 256)), _pick(k)
    nk = k // tk

    def body(a_ref, b_ref, x_ref, gp_ref, g_ref, mix_ref, x1_ref, xn_ref, acc_ref):
        kk = pl.program_id(1)

        @pl.when(kk == 0)
        def _():
            acc_ref[...] = jnp.zeros_like(acc_ref)

        acc_ref[...] += jnp.dot(a_ref[...].astype(BF16), b_ref[...], preferred_element_type=F32)

        @pl.when(kk == nk - 1)
        def _():
            mix = acc_ref[...].astype(BF16)
            mix_ref[...] = mix
            mv = mix.astype(F32)
            x1 = x_ref[...] + mv * _rstd(mv) * gp_ref[...]
            x1_ref[...] = x1
            xn_ref[...] = (x1 * _rstd(x1) * g_ref[...]).astype(BF16)

    rows = pl.BlockSpec((tm, d), lambda i, kk: (i, 0))
    gain = pl.BlockSpec((1, d), lambda i, kk: (0, 0))
    return pl.pallas_call(
        body, name=name, grid=(m // tm, nk),
        in_specs=[pl.BlockSpec((tm, tk), lambda i, kk: (i, kk)), pl.BlockSpec((tk, d), lambda i, kk: (kk, 0)),
                  rows, gain, gain],
        out_specs=[rows, rows, rows],
        out_shape=[jax.ShapeDtypeStruct((m, d), BF16), jax.ShapeDtypeStruct((m, d), F32),
                   jax.ShapeDtypeStruct((m, d), BF16)],
        scratch_shapes=[pltpu.VMEM((tm, d), F32)],
        compiler_params=_params(("parallel", "arbitrary")),
    )(a, b, x, g_post, pre_gain)


def _norm_res_loss(x, mix, g_post, target, name):
    s, d = x.shape
    t = _pick(s, (NORM_TILE, ROW_TILE))

    def body(x_ref, m_ref, g_ref, t_ref, dy_ref, loss_ref):
        mv = m_ref[...].astype(F32)
        y = x_ref[...] + mv * _rstd(mv) * g_ref[...]
        err = y - t_ref[...]
        dy_ref[...] = err * (1.0 / d)

        @pl.when(pl.program_id(0) == 0)
        def _():
            loss_ref[...] = jnp.zeros_like(loss_ref)

        loss_ref[...] += jnp.sum(err * err)

    dy, acc = pl.pallas_call(
        body, name=name, grid=(s // t,),
        in_specs=[_row_spec(t, d), _row_spec(t, d), _gain_spec(d), _row_spec(t, d)],
        out_specs=[_row_spec(t, d), pl.BlockSpec((8, LANE), lambda i: (0, 0))],
        out_shape=[jax.ShapeDtypeStruct((s, d), F32), jax.ShapeDtypeStruct((8, LANE), F32)],
        compiler_params=_params(("arbitrary",)),
    )(x, mix, g_post, target)
    return dy, acc[0, 0] * (0.5 / d)


def _norm_bwd(dx_out, branches, x_in, post, name):
    s, d = dx_out.shape
    t = _pick(s, (NORM_TILE, ROW_TILE))
    nb = len(branches)
    has_post = post is not None

    def body(*refs):
        dx_ref = refs[0]
        pos = 1
        dx = dx_ref[...]
        first = pl.program_id(0) == 0
        n_in = 1 + (1 + 2 * nb if nb else 0) + (2 if has_post else 0)
        outs = refs[n_in:]
        opos = 0
        if nb:
            xv = refs[pos][...]
            pos += 1
            r = _rstd(xv)
            dx_o = outs[0]
            opos = 1
            for _ in range(nb):
                dxn = refs[pos][...].astype(F32)
                g = refs[pos + 1][...]
                pos += 2
                dv, dg_rows = _rms_bwd(dxn, xv, g, r)
                dx = dx + dv
                dg_ref = outs[opos]
                opos += 1

                @pl.when(first)
                def _(dg_ref=dg_ref):
                    dg_ref[...] = jnp.zeros_like(dg_ref)

                dg_ref[...] += jnp.sum(dg_rows, axis=0, keepdims=True)
            dx_o[...] = dx
        if has_post:
            mv = refs[pos][...].astype(F32)
            g = refs[pos + 1][...]
            dm, dg_rows = _rms_bwd(dx, mv, g, _rstd(mv))
            outs[opos][...] = dm.astype(BF16)
            dg_ref = outs[opos + 1]

            @pl.when(first)
            def _():
                dg_ref[...] = jnp.zeros_like(dg_ref)

            dg_ref[...] += jnp.sum(dg_rows, axis=0, keepdims=True)

    ins, in_specs = [dx_out], [_row_spec(t, d)]
    out_shape, out_specs = [], []
    if nb:
        ins.append(x_in)
        in_specs.append(_row_spec(t, d))
        out_shape.append(jax.ShapeDtypeStruct((s, d), F32))
        out_specs.append(_row_spec(t, d))
        for dxn, g in branches:
            ins += [dxn, g]
            in_specs += [_row_spec(t, d), _gain_spec(d)]
            out_shape.append(jax.ShapeDtypeStruct((1, d), F32))
            out_specs.append(_gain_spec(d))
    if has_post:
        ins += [post[0], post[1]]
        in_specs += [_row_spec(t, d), _gain_spec(d)]
        out_shape += [jax.ShapeDtypeStruct((s, d), BF16), jax.ShapeDtypeStruct((1, d), F32)]
        out_specs += [_row_spec(t, d), _gain_spec(d)]
    res = pl.pallas_call(
        body, name=name, grid=(s // t,), in_specs=in_specs, out_specs=out_specs, out_shape=out_shape,
        compiler_params=_params(("arbitrary",)),
    )(*ins)
    res = list(res)
    dx_in = res.pop(0) if nb else dx_out
    dgs = [res.pop(0) for _ in range(nb)]
    dm, dg_post = (res[0], res[1]) if has_post else (None, None)
    return dx_in, dgs, dm, dg_post


HALO = 16


def _shift_down(u, prev, k):
    rows = lax.broadcasted_iota(jnp.int32, u.shape, 0)
    out = pltpu.roll(u, k, 0)
    for i in range(k):
        out = jnp.where(rows == i, prev[HALO - k + i:HALO - k + i + 1, :], out)
    return out


def _shift_up(u, nxt, k):
    n = u.shape[0]
    rows = lax.broadcasted_iota(jnp.int32, u.shape, 0)
    out = pltpu.roll(u, n - k, 0)
    for i in range(k):
        out = jnp.where(rows == n - k + i, nxt[i:i + 1, :], out)
    return out


def _conv_gate_fwd(p, cw, name):
    s, d3 = p.shape
    d = d3 // 3
    t = _pick(s, (ROW_TILE,))
    hb = t // HALO

    def body(p_ref, prev_ref, w_ref, z_ref):
        i = pl.program_id(0)
        pv = p_ref[...].astype(F32)
        b, u = pv[:, :d], pv[:, d:2 * d] * pv[:, 2 * d:]
        ph = prev_ref[...].astype(F32)
        up = jnp.where(i > 0, ph[:, d:2 * d] * ph[:, 2 * d:], 0.0)
        w = w_ref[...]
        y = w[0:1, :] * _shift_down(u, up, 2) + w[1:2, :] * _shift_down(u, up, 1) + w[2:3, :] * u
        z_ref[...] = (b * y).astype(BF16)

    return pl.pallas_call(
        body, name=name, grid=(s // t,),
        in_specs=[_row_spec(t, d3),
                  pl.BlockSpec((HALO, d3), lambda i: (jnp.maximum(i * hb - 1, 0), 0)),
                  pl.BlockSpec((3, d), lambda i: (0, 0))],
        out_specs=_row_spec(t, d),
        out_shape=jax.ShapeDtypeStruct((s, d), BF16),
        compiler_params=_params(("parallel",)),
    )(p, p, cw)


def _conv_gate_bwd(p, dz, cw, name):
    s, d3 = p.shape
    d = d3 // 3
    t = _pick(s, (ROW_TILE,))
    hb = t // HALO
    nt = s // t
    last_halo = s // HALO - 1

    def body(p_ref, prev_ref, next_ref, dz_ref, dznext_ref, w_ref, dp_ref, dw_ref):
        i = pl.program_id(0)
        pv = p_ref[...].astype(F32)
        b, c, h = pv[:, :d], pv[:, d:2 * d], pv[:, 2 * d:]
        u = c * h
        ph = prev_ref[...].astype(F32)
        up = jnp.where(i > 0, ph[:, d:2 * d] * ph[:, 2 * d:], 0.0)
        w = w_ref[...]
        u1, u2 = _shift_down(u, up, 1), _shift_down(u, up, 2)
        y = w[0:1, :] * u2 + w[1:2, :] * u1 + w[2:3, :] * u
        dz = dz_ref[...].astype(F32)
        dy = dz * b
        dyn = jnp.where(i < nt - 1, dznext_ref[...].astype(F32) * next_ref[...].astype(F32)[:, :d], 0.0)
        du = w[2:3, :] * dy + w[1:2, :] * _shift_up(dy, dyn, 1) + w[0:1, :] * _shift_up(dy, dyn, 2)
        dp_ref[:, :d] = (dz * y).astype(BF16)
        dp_ref[:, d:2 * d] = (du * h).astype(BF16)
        dp_ref[:, 2 * d:] = (du * c).astype(BF16)

        @pl.when(i == 0)
        def _():
            dw_ref[...] = jnp.zeros_like(dw_ref)

        dw_ref[0:1, :] += jnp.sum(dy * u2, axis=0, keepdims=True)
        dw_ref[1:2, :] += jnp.sum(dy * u1, axis=0, keepdims=True)
        dw_ref[2:3, :] += jnp.sum(dy * u, axis=0, keepdims=True)

    return pl.pallas_call(
        body, name=name, grid=(nt,),
        in_specs=[_row_spec(t, d3),
                  pl.BlockSpec((HALO, d3), lambda i: (jnp.maximum(i * hb - 1, 0), 0)),
                  pl.BlockSpec((HALO, d3), lambda i: (jnp.minimum((i + 1) * hb, last_halo), 0)),
                  _row_spec(t, d),
                  pl.BlockSpec((HALO, d), lambda i: (jnp.minimum((i + 1) * hb, last_halo), 0)),
                  pl.BlockSpec((3, d), lambda i: (0, 0))],
        out_specs=[_row_spec(t, d3), pl.BlockSpec((3, d), lambda i: (0, 0))],
        out_shape=[jax.ShapeDtypeStruct((s, d3), BF16), jax.ShapeDtypeStruct((3, d), F32)],
        compiler_params=_params(("arbitrary",)),
    )(p, p, p, dz, dz, cw)


FFN_ROWS, FFN_COLS = (1024, 512, 256), (1408, 768, 256, 128)


def _row_chunks(tm, rows=256):
    return [slice(r, r + min(rows, tm)) for r in range(0, tm, min(rows, tm))]


def _ffn_in_swiglu(xn, w_in, name):
    s, k = xn.shape
    ff = w_in.shape[1] // 2
    tm, tn = _pick(s, FFN_ROWS), _pick(ff, FFN_COLS)
    nj = ff // tn

    def body(x_ref, wg_ref, wu_ref, f_ref, a_ref):
        for rows in _row_chunks(tm):
            xv = x_ref[rows, :]
            gate = jnp.dot(xv, wg_ref[...], preferred_element_type=F32)
            up = jnp.dot(xv, wu_ref[...], preferred_element_type=F32)
            f_ref[0, rows, :] = gate.astype(BF16)
            f_ref[1, rows, :] = up.astype(BF16)
            a_ref[rows, :] = (gate * jax.nn.sigmoid(gate) * up).astype(BF16)

    return pl.pallas_call(
        body, name=name, grid=(nj, s // tm),
        in_specs=[pl.BlockSpec((tm, k), lambda j, i: (i, 0)),
                  pl.BlockSpec((k, tn), lambda j, i: (0, j)),
                  pl.BlockSpec((k, tn), lambda j, i: (0, nj + j))],
        out_specs=[pl.BlockSpec((2, tm, tn), lambda j, i: (0, i, j)), pl.BlockSpec((tm, tn), lambda j, i: (i, j))],
        out_shape=[jax.ShapeDtypeStruct((2, s, ff), BF16), jax.ShapeDtypeStruct((s, ff), BF16)],
        compiler_params=_params(("parallel", "parallel")),
    )(xn, w_in, w_in)


def _ffn_out_dx_swiglu(dff, w_out, f, name):
    s, d = dff.shape
    ff = w_out.shape[0]
    tm, tn = _pick(s, FFN_ROWS), _pick(ff, FFN_COLS)

    def body(d_ref, w_ref, f_ref, df_ref):
        for rows in _row_chunks(tm):
            da = lax.dot_general(d_ref[rows, :], w_ref[...], (((1,), (1,)), ((), ())), preferred_element_type=F32)
            gate = f_ref[0, rows, :].astype(F32)
            up = f_ref[1, rows, :].astype(F32)
            sg = jax.nn.sigmoid(gate)
            silu = gate * sg
            df_ref[0, rows, :] = (da * up * (sg + silu * (1.0 - sg))).astype(BF16)
            df_ref[1, rows, :] = (da * silu).astype(BF16)

    planes = pl.BlockSpec((2, tm, tn), lambda j, i: (0, i, j))
    return pl.pallas_call(
        body, name=name, grid=(ff // tn, s // tm),
        in_specs=[pl.BlockSpec((tm, d), lambda j, i: (i, 0)), pl.BlockSpec((tn, d), lambda j, i: (j, 0)), planes],
        out_specs=planes, out_shape=jax.ShapeDtypeStruct((2, s, ff), BF16),
        compiler_params=_params(("parallel", "parallel")),
    )(dff, w_out, f)


SUPER = 2048
RES = 16
PAIR = 128
L = 128
FWD_TOGETHER = 16
BWD_TOGETHER = 4


def _alibi_slopes(n_heads):
    h = np.arange(n_heads, dtype=np.float32) + 1.0
    return np.power(2.0, -8.0 * h / n_heads).astype(np.float32)


def _permute16(x, inverse, name):
    s, d = x.shape
    cw = LANE

    def body(x_ref, o_ref):
        if inverse:
            for m in range(L):
                o_ref[RES * m:RES * (m + 1), :] = x_ref[pl.ds(m, RES, stride=L), :]
        else:
            for r in range(RES):
                o_ref[L * r:L * (r + 1), :] = x_ref[pl.ds(r, L, stride=RES), :]

    spec = pl.BlockSpec((SUPER, cw), lambda i, j: (i, j))
    return pl.pallas_call(
        body, name=name, grid=(s // SUPER, d // cw), in_specs=[spec], out_specs=spec,
        out_shape=jax.ShapeDtypeStruct((s, d), x.dtype),
        compiler_params=_params(("parallel", "parallel")),
    )(x)


def _slope_table(d):
    nh = d // HEAD_DIM
    sl = _alibi_slopes(nh)
    tab = np.repeat(sl, HEAD_DIM).reshape(d // PAIR, 1, PAIR)
    return jnp.asarray(np.broadcast_to(tab, (d // PAIR, 8, PAIR)).copy())


def _geometry(dil):
    nch = RES // dil
    return nch, L // nch


def _band(dil):
    nch, w = _geometry(dil)
    sh = w.bit_length() - 1
    i = lax.broadcasted_iota(jnp.int32, (L, 2 * L), 0)
    j = lax.broadcasted_iota(jnp.int32, (L, 2 * L), 1)

    def pos(t):
        return jnp.bitwise_and(t, w - 1) * nch + jnp.right_shift(t, sh)

    delta = pos(i) + L - (pos(jnp.bitwise_and(j, L - 1)) + jnp.bitwise_and(j, L))
    return (delta * dil).astype(F32), (delta >= 0) & (delta <= L), j < L


def _fill_bias(bias_s, sl_ref):
    for b, dil in enumerate(DILATIONS):
        base, band, prev_half = _band(dil)
        for first in range(2):
            valid = band & jnp.logical_not(prev_half) if first else band
            for h in range(2):
                slope = sl_ref[0:1, HEAD_DIM * h:HEAD_DIM * h + 1]
                bias_s[(2 * b + first) * 2 + h] = jnp.where(valid, -slope * base, NEG_BIG)


def _bias_index(b, sb, n):
    first = jnp.logical_and(sb == 0, n == 0).astype(jnp.int32)
    return (2 * b + first) * 2


def _offsets(dil, res, n):
    nch, w = _geometry(dil)

    def al(v):
        return v if isinstance(v, int) else pl.multiple_of(v, w)

    q_off = [al((a * dil + res) * L + n * w) for a in range(nch)]
    k_off = [al((a * dil + res) * 2 * L + L + n * w) for a in range(nch)]
    kp_off = [al((a * dil + res) * 2 * L + L + n * w - w) for a in range(nch)]
    return q_off, k_off, kp_off, w


def _gather(ref, offs, w):
    parts = [ref[pl.ds(o, w), :] for o in offs]
    return parts[0] if len(parts) == 1 else jnp.concatenate(parts, axis=0)


def _scatter(ref, offs, w, val, add=False):
    for a, o in enumerate(offs):
        piece = val[a * w:(a + 1) * w, :]
        if add:
            ref[pl.ds(o, w), :] += piece
        else:
            ref[pl.ds(o, w), :] = piece


def _fill_key_buffer(buf, prev_ref, cur_ref):
    for r in range(RES):
        buf[2 * L * r:2 * L * r + L, :] = prev_ref[L * r:L * (r + 1), :]
        buf[2 * L * r + L:2 * L * (r + 1), :] = cur_ref[L * r:L * (r + 1), :]


def _two_heads(x, low):
    zero = jnp.zeros_like(x)
    return jnp.concatenate([jnp.where(low, x, zero), jnp.where(low, zero, x)], axis=0)


def _loop_blocks(dil, stages, together):
    together = max(together, dil) if dil < RES else together

    def it(i, c):
        if dil == RES:
            blocks = [(i * together + k, 0) for k in range(together)]
        else:
            blocks = [(res, i * (together // dil) + k) for k in range(together // dil) for res in range(dil)]
        state = [stages[0](res, n) for res, n in blocks]
        for stage in stages[1:]:
            state = [stage(res, n, prev) for (res, n), prev in zip(blocks, state)]
        for writes in state:
            for args in writes:
                _scatter(*args)
        return c

    lax.fori_loop(0, RES // together, it, 0)


NT = (((1,), (1,)), ((), ()))
TN = (((0,), (0,)), ((), ()))


def _attention_fwd(q, kv, name):
    s, d = q.shape
    g_n, ns = d // PAIR, s // SUPER

    def body(sl_ref, q_ref, kc_ref, kp_ref, vc_ref, vp_ref, o_ref, lse_ref, kbuf, vbuf, m_s, l_s, acc_s, bias_s):
        sb = pl.program_id(1)
        _fill_key_buffer(kbuf, kp_ref, kc_ref)
        _fill_key_buffer(vbuf, vp_ref, vc_ref)
        pl.when(sb == 0)(lambda: _fill_bias(bias_s, sl_ref))
        low = lax.broadcasted_iota(jnp.int32, (L, PAIR), 1) < HEAD_DIM
        low_k = lax.broadcasted_iota(jnp.int32, (2 * L, PAIR), 1) < HEAD_DIM
        ones_bd = _two_heads(jnp.ones((2 * L, PAIR), BF16), low_k)
        for bi, dil in enumerate(DILATIONS):
            first_branch, last_branch = bi == 0, bi == len(DILATIONS) - 1

            def scores(res, n, dil=dil):
                q_off, k_off, kp_off, w = _offsets(dil, res, n)
                qf = _gather(q_ref, q_off, w).astype(BF16)
                kcat = jnp.concatenate([_gather(kbuf, kp_off, w), _gather(kbuf, k_off, w)], axis=0).astype(BF16)
                return lax.dot_general(qf, _two_heads(kcat, low_k), NT, preferred_element_type=F32)

            def update(res, n, sc, bi=bi, dil=dil, first_branch=first_branch, last_branch=last_branch):
                q_off, k_off, kp_off, w = _offsets(dil, res, n)
                vcat = jnp.concatenate([_gather(vbuf, kp_off, w), _gather(vbuf, k_off, w)], axis=0).astype(BF16)
                v_ones = jnp.concatenate([_two_heads(vcat, low_k), ones_bd], axis=1)
                bias_at = _bias_index(bi, sb, n)
                if not first_branch:
                    m_prev = _gather(m_s, q_off, w)
                ps, m_new = [], []
                for h in range(2):
                    s_h = sc[:, 2 * L * h:2 * L * (h + 1)] + bias_s[bias_at + h]
                    mh = jnp.max(s_h, axis=1, keepdims=True)
                    if not first_branch:
                        mh = jnp.maximum(mh, m_prev[:, HEAD_DIM * h:HEAD_DIM * h + 1])
                    ps.append(jnp.exp(s_h - mh).astype(BF16))
                    m_new.append(mh)
                m_full = jnp.where(low, m_new[0], m_new[1])
                both = jnp.dot(jnp.concatenate(ps, axis=1), v_ones, preferred_element_type=F32)
                acc, l_full = both[:, :PAIR], both[:, PAIR:]
                if not first_branch:
                    alpha = jnp.exp(m_prev - m_full)
                    l_full = _gather(l_s, q_off, w) * alpha + l_full
                    acc = _gather(acc_s, q_off, w) * alpha + acc
                if last_branch:
                    return [(o_ref, q_off, w, acc / l_full, False), (lse_ref, q_off, w, m_full + jnp.log(l_full), False)]
                return [(m_s, q_off, w, m_full, False), (l_s, q_off, w, l_full, False), (acc_s, q_off, w, acc, False)]

            _loop_blocks(dil, [scores, update], FWD_TOGETHER)

    prev = lambda i: jnp.maximum(i - 1, 0)
    blk = pl.BlockSpec((SUPER, PAIR), lambda g, i: (i, g))
    in_specs = [pl.BlockSpec((None, 8, PAIR), lambda g, i: (g, 0, 0)), blk,
                pl.BlockSpec((SUPER, PAIR), lambda g, i: (i, g)),
                pl.BlockSpec((SUPER, PAIR), lambda g, i: (prev(i), g)),
                pl.BlockSpec((SUPER, PAIR), lambda g, i: (i, g_n + g)),
                pl.BlockSpec((SUPER, PAIR), lambda g, i: (prev(i), g_n + g))]
    return pl.pallas_call(
        body, name=name, grid=(g_n, ns), in_specs=in_specs, out_specs=[blk, blk],
        out_shape=[jax.ShapeDtypeStruct((s, d), F32)] * 2,
        scratch_shapes=([pltpu.VMEM((2 * SUPER, PAIR), F32)] * 2 + [pltpu.VMEM((SUPER, PAIR), F32)] * 3
                        + [pltpu.VMEM((4 * len(DILATIONS), L, 2 * L), F32)]),
        compiler_params=_params(("parallel", "arbitrary")),
    )(_slope_table(d), q, kv, kv, kv, kv)


def _attention_bwd(q, kv, o, do, lse, dkv_in, name):
    s, d = q.shape
    g_n, ns = d // PAIR, s // SUPER
    has_in = dkv_in is not None

    def body(*refs):
        sl_ref, q_ref, do_ref, o_ref, lse_ref, kc_ref, kp_ref, vc_ref, vp_ref = refs[:9]
        pos = 9
        if has_in:
            dkv_in_ref = refs[9]
            pos = 10
        dq_ref, dkv_ref, kbuf, vbuf, dkbuf, dvbuf, dq_s, bias_s = refs[pos:]
        step = pl.program_id(1)
        sb = ns - 1 - step
        _fill_key_buffer(kbuf, kp_ref, kc_ref)
        _fill_key_buffer(vbuf, vp_ref, vc_ref)

        @pl.when(step == 0)
        def _():
            _fill_bias(bias_s, sl_ref)
            dkbuf[...] = jnp.zeros_like(dkbuf)
            dvbuf[...] = jnp.zeros_like(dvbuf)

        @pl.when(step > 0)
        def _():
            for buf in (dkbuf, dvbuf):
                for r in range(RES):
                    buf[2 * L * r + L:2 * L * (r + 1), :] = buf[2 * L * r:2 * L * r + L, :]
                    buf[2 * L * r:2 * L * r + L, :] = jnp.zeros((L, PAIR), F32)

        low = lax.broadcasted_iota(jnp.int32, (L, PAIR), 1) < HEAD_DIM
        low_k = lax.broadcasted_iota(jnp.int32, (2 * L, PAIR), 1) < HEAD_DIM
        low_t = lax.broadcasted_iota(jnp.int32, (PAIR, 2 * L), 0) < HEAD_DIM
        for bi, dil in enumerate(DILATIONS):
            first_branch = bi == 0

            def scores(res, n, dil=dil):
                q_off, k_off, kp_off, w = _offsets(dil, res, n)
                qb = _gather(q_ref, q_off, w).astype(BF16)
                dof = _gather(do_ref, q_off, w)
                prod = dof * _gather(o_ref, q_off, w)
                dob = dof.astype(BF16)
                lse_f = _gather(lse_ref, q_off, w)
                zero = jnp.zeros_like(prod)
                dsum = (jnp.sum(jnp.where(low, prod, zero), axis=1, keepdims=True),
                        jnp.sum(jnp.where(low, zero, prod), axis=1, keepdims=True))
                kcat = jnp.concatenate([_gather(kbuf, kp_off, w), _gather(kbuf, k_off, w)], axis=0).astype(BF16)
                vcat = jnp.concatenate([_gather(vbuf, kp_off, w), _gather(vbuf, k_off, w)], axis=0).astype(BF16)
                k_bd, v_bd = _two_heads(kcat, low_k), _two_heads(vcat, low_k)
                sc = lax.dot_general(qb, k_bd, NT, preferred_element_type=F32)
                dp = lax.dot_general(dob, v_bd, NT, preferred_element_type=F32)
                return qb, dob, lse_f, dsum, k_bd, sc, dp

            def gradients(res, n, given, bi=bi, dil=dil, first_branch=first_branch):
                qb, dob, lse_f, dsum, k_bd, sc, dp = given
                q_off, k_off, kp_off, w = _offsets(dil, res, n)
                bias_at = _bias_index(bi, sb, n)
                ps, dss = [], []
                for h in range(2):
                    cols = slice(2 * L * h, 2 * L * (h + 1))
                    lse_h = lse_f[:, HEAD_DIM * h:HEAD_DIM * h + 1]
                    p_h = jnp.exp(sc[:, cols] + bias_s[bias_at + h] - lse_h)
                    dss.append((p_h * (dp[:, cols] - dsum[h])).astype(BF16))
                    ps.append(p_h.astype(BF16))
                ds_cat, p_cat = jnp.concatenate(dss, axis=1), jnp.concatenate(ps, axis=1)
                dq = jnp.dot(ds_cat, k_bd, preferred_element_type=F32)
                dk_t = lax.dot_general(qb, ds_cat, TN, preferred_element_type=F32)
                dv_t = lax.dot_general(dob, p_cat, TN, preferred_element_type=F32)
                dk = jnp.where(low_t, dk_t[:, :2 * L], dk_t[:, 2 * L:]).T
                dv = jnp.where(low_t, dv_t[:, :2 * L], dv_t[:, 2 * L:]).T
                return [(dq_s, q_off, w, dq, not first_branch),
                        (dkbuf, kp_off, w, dk[:L], True), (dkbuf, k_off, w, dk[L:], True),
                        (dvbuf, kp_off, w, dv[:L], True), (dvbuf, k_off, w, dv[L:], True)]

            _loop_blocks(dil, [scores, gradients], BWD_TOGETHER)

        dq_ref[...] = dq_s[...].astype(BF16)
        for r in range(RES):
            rows, cur = slice(L * r, L * (r + 1)), slice(2 * L * r + L, 2 * L * (r + 1))
            for plane, buf in enumerate((dkbuf, dvbuf)):
                if has_in:
                    dkv_ref[plane, rows, :] = buf[cur, :] + dkv_in_ref[plane, rows, :]
                else:
                    dkv_ref[plane, rows, :] = buf[cur, :]

    rev = lambda i: ns - 1 - i
    prev = lambda i: jnp.maximum(ns - 2 - i, 0)
    blk = pl.BlockSpec((SUPER, PAIR), lambda g, i: (rev(i), g))
    in_specs = [pl.BlockSpec((None, 8, PAIR), lambda g, i: (g, 0, 0)), blk, blk, blk, blk,
                pl.BlockSpec((SUPER, PAIR), lambda g, i: (rev(i), g)),
                pl.BlockSpec((SUPER, PAIR), lambda g, i: (prev(i), g)),
                pl.BlockSpec((SUPER, PAIR), lambda g, i: (rev(i), g_n + g)),
                pl.BlockSpec((SUPER, PAIR), lambda g, i: (prev(i), g_n + g))]
    ins = [_slope_table(d), q, do, o, lse, kv, kv, kv, kv]
    planes = pl.BlockSpec((2, SUPER, PAIR), lambda g, i: (0, rev(i), g))
    if has_in:
        in_specs.append(planes)
        ins.append(dkv_in)
    res = pl.pallas_call(
        body, name=name, grid=(g_n, ns), in_specs=in_specs, out_specs=[blk, planes],
        out_shape=[jax.ShapeDtypeStruct((s, d), BF16), jax.ShapeDtypeStruct((2, s, d), F32)],
        scratch_shapes=([pltpu.VMEM((2 * SUPER, PAIR), F32)] * 4 + [pltpu.VMEM((SUPER, PAIR), F32)]
                        + [pltpu.VMEM((4 * len(DILATIONS), L, 2 * L), F32)]),
        compiler_params=_params(("parallel", "arbitrary")),
    )(*ins)
    return res[0], res[1]


def _coords():
    return lax.axis_index("x"), lax.axis_index("y"), lax.axis_index("c")


def _chip_peers(x, y):
    return [(1 - x, y), (x, 1 - y), (1 - x, 1 - y)]


def _block_of(ref, axis, blk, size):
    start = pl.multiple_of(blk * size, size)
    if axis == 1:
        return ref.at[:, pl.ds(start, size)]
    return ref.at[pl.ds(start, size), :]


ANY = pl.BlockSpec(memory_space=pl.ANY)


HBM = pl.BlockSpec(memory_space=pltpu.HBM)
SEM = pl.BlockSpec(memory_space=pltpu.SEMAPHORE)
SPLIT = pltpu.CompilerParams(has_side_effects=pltpu.SideEffectType.DATAFLOW_SIDE_EFFECTING)


def _in_hbm(a):
    return pltpu.with_memory_space_constraint(a, pltpu.HBM)


def _thru(arrays):
    return [pltpu.HBM(a.shape, a.dtype) for a in arrays]


def _cast_place(w, layer, ax, dtype, after, name):
    _, k, n = w.shape
    t = _pick(k, (512, 256, 128))
    nb = k // t
    extra = [] if after is None else [after]

    def body(blk_ref, w_ref, *refs):
        b_ref, f_ref = refs[len(extra):]
        v = w_ref[...].astype(dtype)
        b_ref[...] = v
        f_ref[...] = v

    full_shape = (k, 4 * n) if ax == 1 else (4 * k, n)
    place = (lambda i, blk: (i, blk[0])) if ax == 1 else (lambda i, blk: (blk[0] * nb + i, 0))
    return pl.pallas_call(
        body, name=name,
        grid_spec=pltpu.PrefetchScalarGridSpec(
            num_scalar_prefetch=1, grid=(nb,),
            in_specs=[pl.BlockSpec((None, t, n), lambda i, blk: (layer, i, 0))] + [ANY] * len(extra),
            out_specs=[pl.BlockSpec((t, n), lambda i, blk: (i, 0)), pl.BlockSpec((t, n), place)]),
        out_shape=[jax.ShapeDtypeStruct((k, n), dtype), jax.ShapeDtypeStruct(full_shape, dtype)],
        compiler_params=_params(("parallel",)),
    )(_my_block()[None], w, *extra)


def _my_block():
    return (2 * lax.axis_index("x") + lax.axis_index("y")).astype(jnp.int32)


def _gather_start(group, carry, name):
    n, nc = len(group), len(carry)

    def body(*refs):
        blocks, fulls, send_sem, recv_sem = refs[:n], refs[n:2 * n], refs[2 * n + nc], refs[2 * n + nc + 1]
        x, y, c = _coords()
        for t, (b, _, ax) in enumerate(group):
            mine = _block_of(fulls[t], ax, 2 * x + y, b.shape[ax])
            for j, (px, py) in enumerate(_chip_peers(x, y)):
                pltpu.make_async_remote_copy(
                    src_ref=blocks[t], dst_ref=mine, send_sem=send_sem.at[3 * t + j], recv_sem=recv_sem.at[3 * t + j],
                    device_id=(px, py, c), device_id_type=MESH).start()

    arrays = [b for b, _, _ in group] + [f for _, f, _ in group] + list(carry)
    sems = [pltpu.SemaphoreType.DMA((3 * n,))] * 2
    res = pl.pallas_call(
        body, name=name, in_specs=[HBM] * len(arrays), out_specs=[SEM, SEM] + [HBM] * len(arrays),
        out_shape=sems + _thru(arrays), input_output_aliases={i: 2 + i for i in range(len(arrays))},
        compiler_params=SPLIT,
    )(*[_in_hbm(a) for a in arrays])
    return (res[0], res[1], list(res[2:2 + n]), list(res[2 + n:2 + 2 * n])), list(res[2 + 2 * n:])


def _gather_wait(group, started, after, name):
    sends, recvs, blocks, fulls = started
    m = len(group)

    def body(*refs):
        blk_refs, full_refs, send_sem, recv_sem = refs[:m], refs[m:2 * m], refs[2 * m], refs[2 * m + 1]
        x, y, c = _coords()
        for t, (b, _, ax) in enumerate(group):
            for j, (px, py) in enumerate(_chip_peers(x, y)):
                cp = pltpu.make_async_remote_copy(
                    src_ref=blk_refs[t], dst_ref=_block_of(full_refs[t], ax, 2 * px + py, b.shape[ax]),
                    send_sem=send_sem.at[3 * t + j], recv_sem=recv_sem.at[3 * t + j],
                    device_id=(px, py, c), device_id_type=MESH)
                cp.wait_send()
                cp.wait_recv()

    extra = [] if after is None else [after]
    res = pl.pallas_call(
        body, name=name, in_specs=[HBM] * (2 * m) + [SEM, SEM] + [ANY] * len(extra), out_specs=[HBM] * (2 * m),
        out_shape=_thru(blocks) + _thru(fulls), input_output_aliases={i: i for i in range(2 * m)},
        compiler_params=SPLIT,
    )(*blocks, *fulls, sends, recvs, *extra)
    return list(res[m:])


def _scatter_start(grads, carry, name):
    n = len(grads)
    n_in = 2 * n + len(carry)

    def body(*refs):
        g_refs, st_refs, send_sem, recv_sem = refs[:n], refs[n:2 * n], refs[n_in], refs[n_in + 1]
        x, y, c = _coords()
        for t, (g, ax) in enumerate(grads):
            for j, (px, py) in enumerate(_chip_peers(x, y)):
                pltpu.make_async_remote_copy(
                    src_ref=_block_of(g_refs[t], ax, 2 * px + py, g.shape[ax] // 4), dst_ref=st_refs[t].at[j],
                    send_sem=send_sem.at[3 * t + j], recv_sem=recv_sem.at[3 * t + j],
                    device_id=(px, py, c), device_id_type=MESH).start()

    arrays = [g for g, _ in grads]
    for g, ax in grads:
        shape = list(g.shape)
        shape[ax] //= 4
        arrays.append(lax.empty((3, *shape), g.dtype))
    arrays += list(carry)
    sems = [pltpu.SemaphoreType.DMA((3 * n,))] * 2
    res = pl.pallas_call(
        body, name=name, in_specs=[HBM] * n_in, out_specs=[SEM, SEM] + [HBM] * n_in,
        out_shape=sems + _thru(arrays), input_output_aliases={i: 2 + i for i in range(n_in)},
        compiler_params=SPLIT,
    )(*[_in_hbm(a) for a in arrays])
    return (res[0], res[1], list(res[2:2 + n]), list(res[2 + n:2 + 2 * n])), list(res[2 + 2 * n:])


def _scatter_wait(axes, started, after, name):
    sends, recvs, full, stacks = started
    n = len(full)
    extra = [] if after is None else [after]

    def body(*refs):
        g_refs, st_refs, send_sem, recv_sem = refs[:n], refs[n:2 * n], refs[2 * n], refs[2 * n + 1]
        x, y, c = _coords()
        for t, ax in enumerate(axes):
            size = full[t].shape[ax] // 4
            for j, (px, py) in enumerate(_chip_peers(x, y)):
                cp = pltpu.make_async_remote_copy(
                    src_ref=_block_of(g_refs[t], ax, 2 * px + py, size), dst_ref=st_refs[t].at[j],
                    send_sem=send_sem.at[3 * t + j], recv_sem=recv_sem.at[3 * t + j],
                    device_id=(px, py, c), device_id_type=MESH)
                cp.wait_send()
                cp.wait_recv()

    res = pl.pallas_call(
        body, name=name, in_specs=[HBM] * (2 * n) + [SEM, SEM] + [ANY] * len(extra), out_specs=[HBM] * (2 * n),
        out_shape=_thru(full) + _thru(stacks), input_output_aliases={i: i for i in range(2 * n)},
        compiler_params=SPLIT,
    )(*full, *stacks, sends, recvs, *extra)
    return list(res[:n]), list(res[n:])


def _pair_copies(g_refs, st_refs, out_refs, items, send_sem, recv_sem):
    x, y, c = _coords()
    copies = []
    for u, (g, ax, _) in enumerate(items):
        own = _block_of(g_refs[u], ax, 2 * x + y, g.shape[ax] // 4)
        for k, (src, dst) in enumerate([(own, out_refs[u].at[0]), (st_refs[u], out_refs[u].at[pl.ds(1, 3)])]):
            copies.append(pltpu.make_async_remote_copy(
                src_ref=src, dst_ref=dst, send_sem=send_sem.at[2 * u + k], recv_sem=recv_sem.at[2 * u + k],
                device_id=(x, y, 1 - c), device_id_type=MESH))
    return copies


def _pair_start(items, carry, name):
    n = len(items)
    n_in = 3 * n + len(carry)

    def body(*refs):
        for cp in _pair_copies(refs[:n], refs[n:2 * n], refs[2 * n:3 * n], items, refs[n_in], refs[n_in + 1]):
            cp.start()

    arrays = ([g for g, _, _ in items] + [st for _, _, st in items]
              + [lax.empty((4, *st.shape[1:]), st.dtype) for _, _, st in items] + list(carry))
    sems = [pltpu.SemaphoreType.DMA((2 * n,))] * 2
    res = pl.pallas_call(
        body, name=name, in_specs=[HBM] * n_in, out_specs=[SEM, SEM] + [HBM] * n_in,
        out_shape=sems + _thru(arrays), input_output_aliases={i: 2 + i for i in range(n_in)},
        compiler_params=SPLIT,
    )(*[_in_hbm(a) for a in arrays])
    thru = res[2:]
    return (res[0], res[1], *(list(thru[k * n:(k + 1) * n]) for k in range(3))), list(thru[3 * n:])


def _pair_wait(axes, started, name):
    send, recv, full, stacks, landing = started
    n = len(full)
    items = [(full[u], axes[u], stacks[u]) for u in range(n)]

    def body(*refs):
        for cp in _pair_copies(refs[:n], refs[n:2 * n], refs[2 * n:3 * n], items, refs[3 * n], refs[3 * n + 1]):
            cp.wait_send()
            cp.wait_recv()

    res = pl.pallas_call(
        body, name=name, in_specs=[HBM] * (3 * n) + [SEM, SEM], out_specs=[HBM] * (3 * n),
        out_shape=_thru(full + stacks + landing), input_output_aliases={i: i for i in range(3 * n)},
        compiler_params=SPLIT,
    )(*full, *stacks, *landing, send, recv)
    return list(res[:n]), list(res[n:2 * n]), list(res[2 * n:])


def _allreduce_small(v, name):
    r, cdim = v.shape

    def body(v_ref, out_ref, buf, send_sems, recv_sems):
        x, y, c = _coords()
        me = 4 * x + 2 * y + c
        buf[0] = v_ref[...]
        sends = []
        for k in range(1, 8):
            peer = (x if not (k & 4) else 1 - x, y if not (k & 2) else 1 - y, c if not (k & 1) else 1 - c)
            cp = pltpu.make_async_remote_copy(
                src_ref=v_ref, dst_ref=buf.at[k], send_sem=send_sems.at[k - 1], recv_sem=recv_sems.at[k - 1],
                device_id=peer, device_id_type=MESH)
            cp.start()
            sends.append(cp)
        for cp in sends:
            cp.wait_recv()
        total = buf[me]
        for src in range(1, 8):
            total = total + buf[jnp.bitwise_xor(me, src)]
        out_ref[...] = total
        for cp in sends:
            cp.wait_send()

    return pl.pallas_call(
        body, name=name,
        in_specs=[pl.BlockSpec(memory_space=pltpu.VMEM)], out_specs=pl.BlockSpec(memory_space=pltpu.VMEM),
        out_shape=jax.ShapeDtypeStruct((r, cdim), F32),
        scratch_shapes=[pltpu.VMEM((8, r, cdim), F32), pltpu.SemaphoreType.DMA((7,)), pltpu.SemaphoreType.DMA((7,))],
        compiler_params=pltpu.CompilerParams(has_side_effects=True),
    )(v)


def _adamw_math(w, g, m, v):
    m = ADAM_B1 * m + (1.0 - ADAM_B1) * g
    v = ADAM_B2 * v + (1.0 - ADAM_B2) * jnp.square(g)
    m_hat = m / (1.0 - ADAM_B1 ** ADAM_STEP)
    v_hat = v / (1.0 - ADAM_B2 ** ADAM_STEP)
    delta = -ADAM_LR * (m_hat / (jnp.sqrt(v_hat) + ADAM_EPS) + ADAM_WD * w)
    return delta, m, v


def _adamw(w, m, v, grads, name):
    r, cdim = w.shape
    paired = isinstance(grads, list)
    layers = len(grads) if paired else 1
    t = _pick(r // layers, (128, 64, 32, 16, 8))
    per_layer = r // layers // t
    n_grad = 3 * layers if paired else 1

    def body(*refs):
        refs = refs[1:] if paired else refs
        w_ref, m_ref, v_ref = refs[:3]
        outs = refs[3 + n_grad:]

        def update(g):
            delta, m_new, v_new = _adamw_math(w_ref[...], g, m_ref[...], v_ref[...])
            outs[0][...] = g
            outs[1][...] = delta
            outs[2][...] = m_new
            outs[3][...] = v_new

        if not paired:
            update(refs[3][...])
            return
        layer = pl.program_id(0) // per_layer
        for l in range(layers):
            @pl.when(layer == l)
            def _(own_ref=refs[3 + 3 * l], st_ref=refs[4 + 3 * l], sib_ref=refs[5 + 3 * l]):
                sa = own_ref[...].astype(F32)
                sb = sib_ref[0].astype(F32)
                for k in range(3):
                    sa = sa + st_ref[k].astype(F32)
                    sb = sb + sib_ref[k + 1].astype(F32)
                update(sa + sb)

    out_shape = [jax.ShapeDtypeStruct((r, cdim), F32)] * 4
    if not paired:
        spec = pl.BlockSpec((t, cdim), lambda i: (i, 0))
        return pl.pallas_call(
            body, name=name, grid=(r // t,), in_specs=[spec] * 4, out_specs=[spec] * 4, out_shape=out_shape,
            compiler_params=_params(("parallel",)),
        )(w, m, v, grads)

    spec = pl.BlockSpec((t, cdim), lambda i, blk: (i, 0))
    ins, in_specs = [w, m, v], [spec] * 3
    for l, (g, ax, stack, sib) in enumerate(grads):
        row = lambda i, l=l: jnp.clip(i - l * per_layer, 0, per_layer - 1)
        own = ((lambda i, blk, row=row: (row(i), blk[0])) if ax == 1
               else (lambda i, blk, row=row: (blk[0] * per_layer + row(i), 0)))
        ins += [g, stack, sib]
        in_specs += [pl.BlockSpec((t, cdim), own),
                     pl.BlockSpec((3, t, cdim), lambda i, blk, row=row: (0, row(i), 0)),
                     pl.BlockSpec((4, t, cdim), lambda i, blk, row=row: (0, row(i), 0))]
    return pl.pallas_call(
        body, name=name,
        grid_spec=pltpu.PrefetchScalarGridSpec(
            num_scalar_prefetch=1, grid=(r // t,), in_specs=in_specs, out_specs=[spec] * 4),
        out_shape=out_shape, compiler_params=_params(("parallel",)),
    )(_my_block()[None], *ins)


def _local_step(x, target, gains, conv_ws, kv_gain, weights_of, send_grads):
    depth = len(gains)
    n_a = len(conv_ws)
    saved, ws = [], []
    kv = kvn = None
    _, (xn,) = _norm_res_fwd(x, None, None, [gains[0][0]], "norm_first")
    h = x
    for l in range(depth):
        g = gains[l]
        sv = {"x_in": h, "xn": xn}
        w = weights_of(l, "mix", h)
        ws.append(w)
        if l == n_a:
            kv = _mm(kvn, w["kv"], "nn", F32, "kv_fwd")
        if l < n_a:
            p = _mm(xn, w["conv_in"], "nn", BF16, f"conv_in_fwd_{l}")
            z = _conv_gate_fwd(p, conv_ws[l], f"conv_gate_fwd_{l}")
            mix, x1, xn2 = _mm_norm_res(z, w["conv_out"], h, g[1], g[2], f"conv_out_fwd_{l}")
            sv.update(p=p, z=z)
        else:
            j = l - n_a
            q = _mm(xn, w["q"], "nn", F32, f"q_fwd_{j}", scale=HEAD_DIM ** -0.5)
            o, lse = _attention_fwd(q, kv, f"attn_fwd_{j}")
            mix, x1, xn2 = _mm_norm_res(o, w["o"], h, g[1], g[2], f"o_fwd_{j}")
            sv.update(q=q, o=o, lse=lse)
        w.update(weights_of(l, "ffn", mix))
        f, a = _ffn_in_swiglu(xn2, w["ffn_in"], f"ffn_in_fwd_{l}")
        ff = _mm(a, w["ffn_out"], "nn", BF16, f"ffn_out_fwd_{l}")
        sv.update(mix=mix, x1=x1, xn2=xn2, f=f, a=a, ff=ff)
        saved.append(sv)
        if l == depth - 1:
            dx, loss = _norm_res_loss(x1, ff, g[3], target, "norm_loss")
        elif l == n_a - 1:
            h, _ = _norm_res_fwd(x1, ff, g[3], [], f"norm_end_{l}")
            h = _permute16(h, False, "permute_stream")
            _, (xn, kvn) = _norm_res_fwd(h, None, None, [gains[l + 1][0], kv_gain], "norm_permuted")
        else:
            h, (xn,) = _norm_res_fwd(x1, ff, g[3], [gains[l + 1][0]], f"norm_end_{l}")
    d_gains = [[None] * 4 for _ in range(depth)]
    d_conv = [None] * n_a
    d_kv_gain = None
    dkv = None
    _, _, dff, d_gains[depth - 1][3] = _norm_bwd(dx, [], None, (saved[-1]["ff"], gains[-1][3]), "norm_loss_bwd")
    for l in reversed(range(depth)):
        sv, g, w, grads = saved[l], gains[l], ws[l], {}
        grads["ffn_out"] =_mm(sv["a"], dff, "tn", BF16, f"ffn_out_dw_{l}")
        df = _ffn_out_dx_swiglu(dff, w["ffn_out"], sv["f"], f"ffn_out_dx_{l}")
        dxn2 = _mm(df, w["ffn_in"], "nt", BF16, f"ffn_in_dx_{l}")
        grads["ffn_in"] =_mm(sv["xn2"], df, "tn", BF16, f"ffn_in_dw_{l}")
        dx, (d_gains[l][2],), dmix, d_gains[l][1] = _norm_bwd(
            dx, [(dxn2, g[2])], sv["x1"], (sv["mix"], g[1]), f"norm_mid_bwd_{l}")
        dx, dmix = send_grads(l, "ffn", grads, [dx, dmix])
        if l < n_a:
            dz = _mm(dmix, w["conv_out"], "nt", BF16, f"conv_out_dx_{l}")
            grads["conv_out"] =_mm(sv["z"], dmix, "tn", BF16, f"conv_out_dw_{l}")
            dp, d_conv[l] = _conv_gate_bwd(sv["p"], dz, conv_ws[l], f"conv_gate_bwd_{l}")
            dxn = _mm(dp, w["conv_in"], "nt", BF16, f"conv_in_dx_{l}")
            grads["conv_in"] =_mm(sv["xn"], dp, "tn", BF16, f"conv_in_dw_{l}")
        else:
            j = l - n_a
            do = _mm(dmix, w["o"], "nt", F32, f"o_dx_{j}")
            grads["o"] =_mm(sv["o"], dmix, "tn", BF16, f"o_dw_{j}")
            dq, dkv = _attention_bwd(sv["q"], kv, sv["o"], do, sv["lse"], dkv, f"attn_bwd_{j}")
            scale = HEAD_DIM ** -0.5
            dxn = _mm(dq, w["q"], "nt", BF16, f"q_dx_{j}", scale=scale)
            grads["q"] =_mm(sv["xn"], dq, "tn", BF16, f"q_dw_{j}", scale=scale)
        branches = [(dxn, g[0])]
        if l == n_a:
            dkvn = _mm(dkv, w["kv"], "nt", BF16, "kv_dx")
            grads["kv"] =_mm(kvn, dkv, "tn", BF16, "kv_dw")
            branches.append((dkvn, kv_gain))
        post = (saved[l - 1]["ff"], gains[l - 1][3]) if l > 0 else None
        if l == n_a:
            dx, dgs, _, _ = _norm_bwd(dx, branches, sv["x_in"], None, f"norm_end_bwd_{l}")
            dx = _permute16(dx, True, "unpermute_stream")
            _, _, dff, dg_post = _norm_bwd(dx, [], None, post, "norm_boundary_bwd")
        else:
            dx, dgs, dff, dg_post = _norm_bwd(dx, branches, sv["x_in"], post, f"norm_end_bwd_{l}")
        if dff is None:
            send_grads(l, "mix", grads, [])
        else:
            dx, dff = send_grads(l, "mix", grads, [dx, dff])
        d_gains[l][0] = dgs[0]
        if l == n_a:
            d_kv_gain = dgs[1]
        if l > 0:
            d_gains[l - 1][3] = dg_post
    return loss, dx, d_gains, d_conv, d_kv_gain


BIG = (
    ("conv_in", 1), ("conv_out", 0), ("kv", 1), ("q", 0), ("o", 0), ("ffn_in", 1), ("ffn_out", 0))


def kernel(x, norm_g, conv_in_w, conv_w, conv_out_w, kv_norm_g, kv_w, q_w, o_w, ffn_in_w, ffn_out_w, loss_target, m_norm_g, m_conv_in_w, m_conv_w, m_conv_out_w, m_kv_norm_g, m_kv_w, m_q_w, m_o_w, m_ffn_in_w, m_ffn_out_w, v_norm_g, v_conv_in_w, v_conv_w, v_conv_out_w, v_kv_norm_g, v_kv_w, v_q_w, v_o_w, v_ffn_in_w, v_ffn_out_w):
    depth, _, dq = norm_g.shape
    d = 4 * dq
    n_a = conv_w.shape[0]
    big_w = {"conv_in": conv_in_w, "conv_out": conv_out_w, "kv": kv_w[None], "q": q_w, "o": o_w,
             "ffn_in": ffn_in_w, "ffn_out": ffn_out_w}
    big_m = {"conv_in": m_conv_in_w, "conv_out": m_conv_out_w, "kv": m_kv_w[None], "q": m_q_w, "o": m_o_w,
             "ffn_in": m_ffn_in_w, "ffn_out": m_ffn_out_w}
    big_v = {"conv_in": v_conv_in_w, "conv_out": v_conv_out_w, "kv": v_kv_w[None], "q": v_q_w, "o": v_o_w,
             "ffn_in": v_ffn_in_w, "ffn_out": v_ffn_out_w}

    n_gain, n_tap = depth * 4, n_a * conv_w.shape[1]
    small_rows = -(-(n_gain + n_tap + 1) // 8) * 8
    pad_rows = small_rows - n_gain - n_tap

    def pack_small(gains, taps):
        return jnp.concatenate([gains.reshape(n_gain, dq), taps.reshape(n_tap, dq), jnp.zeros((pad_rows, dq), F32)])

    axis_of = dict(BIG)

    def matrices_of(l, part):
        if part == "ffn":
            return [("ffn_in", l), ("ffn_out", l)]
        if l < n_a:
            return [("conv_in", l), ("conv_out", l)]
        return ([("kv", 0)] if l == n_a else []) + [("q", l - n_a), ("o", l - n_a)]

    halves = [(l, part) for l in range(depth) for part in ("mix", "ffn")]
    def placed(half, after):
        return [(*_cast_place(big_w[name], i, axis_of[name], BF16, after, f"place_{name}_{i}"), axis_of[name])
                for name, i in matrices_of(*half)]

    groups = {halves[0]: placed(halves[0], None)}
    groups[halves[0]].append((*_cast_place(pack_small(norm_g, conv_w)[None], 0, 1, F32, None, "place_small"), 1))
    started = {halves[0]: _gather_start(groups[halves[0]], [], "gather_start_0_mix")[0]}
    for half in halves[1:]:
        groups[half] = placed(half, started[halves[0]][2][0])

    AHEAD = 2

    def fetch(half, after):
        full = _gather_wait(groups[half], started[half], after, "gather_wait_%d_%s" % half)
        nxt = halves.index(half) + AHEAD
        if nxt < len(halves):
            started[halves[nxt]], full = _gather_start(groups[halves[nxt]], full, "gather_start_%d_%s" % halves[nxt])
        return full

    target = _permute16(loss_target.reshape(x.shape[1:]), False, "permute_target")
    for half in halves[1:AHEAD]:
        started[half], (target,) = _gather_start(groups[half], [target], "gather_start_%d_%s" % half)
    first = fetch(halves[0], target)
    small = first[-1]
    gains = [[small[4 * l + i][None] for i in range(4)] for l in range(depth)]
    conv_ws = [small[n_gain + 3 * l:n_gain + 3 * l + 3] for l in range(n_a)]
    kv_gain = kv_norm_g[None]

    def weights_of(l, part, after):
        full = first if (l, part) == halves[0] else fetch((l, part), after)
        return {name: full[t] for t, (name, _) in enumerate(matrices_of(l, part))}

    sent, paired = {}, {}
    LAG = 2

    def to_sibling(half, carry, after):
        axes = [axis_of[name] for name, _ in matrices_of(*half)]
        full, stacks = _scatter_wait(axes, sent[half], after, "scatter_wait_%d_%s" % half)
        paired[half], carry = _pair_start(list(zip(full, axes, stacks)), carry, "pair_start_%d_%s" % half)
        return carry

    def send_grads(l, part, grads, carry):
        sent[l, part], carry = _scatter_start(
            [(grads[name], axis_of[name]) for name, _ in matrices_of(l, part)], carry, f"scatter_start_{l}_{part}")
        at = halves.index((l, part))
        if carry:
            if at + LAG < len(halves):
                carry = to_sibling(halves[at + LAG], carry, carry[0])
        else:
            for older in range(at + LAG, at, -1):
                if halves[older] not in paired:
                    to_sibling(halves[older], [], sent[l, part][2][0])
        if at == 1 and carry:
            carry = to_sibling(halves[at + 1], carry, carry[0])
        return carry

    loss, dx, d_gains, d_conv, d_kv_gain = _local_step(
        x.reshape(x.shape[1:]), target, gains, conv_ws, kv_gain, weights_of, send_grads)
    loss = lax.psum(loss, ("x", "y", "c"))

    small_g = jnp.concatenate([dg for row in d_gains for dg in row] + list(d_conv) + [d_kv_gain]
                              + [jnp.zeros((pad_rows - 1, d), F32)])
    small_g = _allreduce_small(small_g, "allreduce_small")
    blk = 2 * lax.axis_index("x") + lax.axis_index("y")
    mine_small = lax.dynamic_slice_in_dim(small_g, blk * dq, dq, axis=1)
    kv_rows = d // dq

    def pack_opt(gains_like, taps_like, kv_like):
        rows = jnp.concatenate([gains_like.reshape(n_gain, dq), taps_like.reshape(n_tap, dq), kv_like.reshape(kv_rows, dq)])
        extra = -rows.shape[0] % 8
        return jnp.concatenate([rows, jnp.zeros((extra, dq), F32)]) if extra else rows

    sw = pack_opt(norm_g, conv_w, kv_norm_g)
    sm = pack_opt(m_norm_g, m_conv_w, m_kv_norm_g)
    sv = pack_opt(v_norm_g, v_conv_w, v_kv_norm_g)
    sg = pack_opt(mine_small[:n_gain], mine_small[n_gain:n_gain + n_tap], small_g[n_gain + n_tap])
    s_out = _adamw(sw, sm, sv, sg, "adamw_small")

    def unpack(a):
        return (a[:n_gain].reshape(depth, 4, dq), a[n_gain:n_gain + n_tap].reshape(n_a, -1, dq),
                a[n_gain + n_tap:n_gain + n_tap + kv_rows].reshape(d))

    small_out = [unpack(a) for a in s_out]

    landed, big_out = {}, {}

    def update(name):
        for half in halves:
            if matrices_of(*half)[0] not in landed and any(n == name for n, _ in matrices_of(*half)):
                axes = [axis_of[n] for n, _ in matrices_of(*half)]
                landed.update(zip(matrices_of(*half), zip(*_pair_wait(axes, paired[half], "pair_wait_%d_%s" % half))))
        shp, ax = big_w[name].shape, axis_of[name]
        rows, cols = shp[0] * shp[1], shp[2]
        flat = lambda a: a.reshape(rows, cols)
        full, stacks, sibling = zip(*[landed[name, i] for i in range(shp[0])])
        res = _adamw(flat(big_w[name]), flat(big_m[name]), flat(big_v[name]),
                     [(full[i], ax, stacks[i], sibling[i]) for i in range(shp[0])], f"adamw_{name}")
        big_out[name] = [a.reshape(shp[1:] if name == "kv" else shp) for a in res]

    pending = [half for half in reversed(halves) if half not in paired]
    for half in pending[:-1]:
        to_sibling(half, [], None)
    late = [name for name, _ in BIG if any(n == name for n, _ in matrices_of(*pending[-1]))]
    for name, _ in BIG:
        if name not in late:
            update(name)
    to_sibling(pending[-1], [], big_out["ffn_out"][0])
    for name in late:
        update(name)

    def leaves(i):
        ng, cw_, kg = small_out[i]
        return [ng, big_out["conv_in"][i], cw_, big_out["conv_out"][i], kg, big_out["kv"][i], big_out["q"][i],
                big_out["o"][i], big_out["ffn_in"][i], big_out["ffn_out"][i]]

    return (loss, dx.reshape(x.shape), *leaves(0), *leaves(1), *leaves(2), *leaves(3))
```

```python
import jax
import jax.numpy as jnp
import numpy as np
from jax import lax
from jax.experimental import pallas as pl
from jax.experimental.pallas import tpu as pltpu

F32 = jnp.float32
BF16 = jnp.bfloat16
HEAD_DIM = 64
DILATIONS = (1, 4, 16)
NORM_EPS = 1e-6
NEG_BIG = -1e30
VMEM_LIMIT = 48 * 1024 * 1024
VMEM_BUDGET = VMEM_LIMIT * 3 // 4
ROW_TILE = 256
NORM_TILE = 512
LANE = 128
MESH = pl.DeviceIdType.MESH

ADAM_LR = 0.001
ADAM_B1 = 0.9
ADAM_B2 = 0.999
ADAM_EPS = 1e-08
ADAM_WD = 0.01
ADAM_STEP = 10

TILE_CANDIDATES = (1024, 1408, 768, 512, 384, 256, 128)


def _pick(dim, cands=TILE_CANDIDATES):
    for c in cands:
        if c <= dim and dim % c == 0:
            return c
    return dim


def _params(sem):
    return pltpu.CompilerParams(dimension_semantics=sem, vmem_limit_bytes=VMEM_LIMIT)


def _mm(a, b, mode, out_dtype, name, scale=None):
    a_planes = a.shape[0] if a.ndim == 3 else 1
    b_planes = b.shape[0] if b.ndim == 3 else 1
    if mode == "nn":
        m, k = a.shape[-2], a.shape[-1] * a_planes
        n = b.shape[1]
    elif mode == "nt":
        m, k = a.shape[-2], a.shape[-1] * a_planes
        n = b.shape[0]
    else:
        k, m = a.shape
        n = b.shape[-1] * b_planes
    tn = _pick(n // b_planes)
    tk = _pick(k // a_planes, ((2048,) if mode == "tn" else (3072, 2816)) + TILE_CANDIDATES)
    nk = k // tk

    def vmem_bytes(rows):
        blocks = rows * tk * a.dtype.itemsize + tk * tn * b.dtype.itemsize + rows * tn * jnp.dtype(out_dtype).itemsize
        return 2 * blocks + (rows * tn * 4 if nk > 1 else 0)

    tm = next(t for t in (2048,) + TILE_CANDIDATES + (m,) if t <= m and m % t == 0 and (vmem_bytes(t) <= VMEM_BUDGET or t <= 256))
    ka, nb = k // a_planes // tk, n // b_planes // tn
    if a_planes > 1:
        a_spec = pl.BlockSpec((None, tm, tk), lambda i, j, kk: (kk // ka, i, kk % ka))
    elif mode == "tn":
        a_spec = pl.BlockSpec((tk, tm), lambda i, j, kk: (kk, i))
    else:
        a_spec = pl.BlockSpec((tm, tk), lambda i, j, kk: (i, kk))
    if mode == "nn":
        b_spec = pl.BlockSpec((tk, tn), lambda i, j, kk: (kk, j))
        dims = (((1,), (0,)), ((), ()))
    elif mode == "nt":
        b_spec = pl.BlockSpec((tn, tk), lambda i, j, kk: (j, kk))
        dims = (((1,), (1,)), ((), ()))
    else:
        b_spec = (pl.BlockSpec((None, tk, tn), lambda i, j, kk: (j // nb, kk, j % nb)) if b_planes > 1
                  else pl.BlockSpec((tk, tn), lambda i, j, kk: (kk, j)))
        dims = (((0,), (0,)), ((), ()))

    def finish(acc):
        if scale is not None:
            acc = acc * scale
        return acc.astype(out_dtype)

    if nk == 1:
        def body(a_ref, b_ref, o_ref):
            o_ref[...] = finish(lax.dot_general(a_ref[...].astype(BF16), b_ref[...].astype(BF16), dims, preferred_element_type=F32))
        scratch = []
    else:
        def body(a_ref, b_ref, o_ref, acc_ref):
            kk = pl.program_id(2)

            @pl.when(kk == 0)
            def _():
                acc_ref[...] = jnp.zeros_like(acc_ref)

            acc_ref[...] += lax.dot_general(a_ref[...].astype(BF16), b_ref[...].astype(BF16), dims, preferred_element_type=F32)

            @pl.when(kk == nk - 1)
            def _():
                o_ref[...] = finish(acc_ref[...])
        scratch = [pltpu.VMEM((tm, tn), F32)]

    return pl.pallas_call(
        body, name=name,
        grid=(m // tm, n // tn, nk),
        in_specs=[a_spec, b_spec],
        out_specs=pl.BlockSpec((tm, tn), lambda i, j, kk: (i, j)),
        out_shape=jax.ShapeDtypeStruct((m, n), out_dtype),
        scratch_shapes=scratch,
        compiler_params=_params(("parallel", "parallel", "arbitrary")),
    )(a, b)


def _rstd(v):
    return lax.rsqrt(jnp.mean(v * v, axis=-1, keepdims=True) + NORM_EPS)


def _rms_bwd(dy, v, g, r):
    xhat = v * r
    gy = dy * g
    dv = r * (gy - xhat * jnp.mean(gy * xhat, axis=-1, keepdims=True))
    return dv, dy * xhat


def _row_spec(t, width):
    return pl.BlockSpec((t, width), lambda i: (i, 0))


def _gain_spec(width):
    return pl.BlockSpec((1, width), lambda i: (0, 0))


def _norm_res_fwd(x, mix, g_post, pre_gains, name):
    s, d = x.shape
    t = _pick(s, (NORM_TILE, ROW_TILE))
    has_mix = mix is not None
    n_pre = len(pre_gains)

    def body(*refs):
        x_ref = refs[0]
        pos = 1
        x1 = x_ref[...]
        if has_mix:
            mv = refs[1][...].astype(F32)
            x1 = x1 + mv * _rstd(mv) * refs[2][...]
            pos = 3
        gains = refs[pos:pos + n_pre]
        outs = refs[pos + n_pre:]
        if has_mix:
            outs[0][...] = x1
            outs = outs[1:]
        r = _rstd(x1)
        for g_ref, o_ref in zip(gains, outs):
            o_ref[...] = (x1 * r * g_ref[...]).astype(BF16)

    ins = [x] + ([mix, g_post] if has_mix else []) + list(pre_gains)
    in_specs = [_row_spec(t, d)] + ([_row_spec(t, d), _gain_spec(d)] if has_mix else []) + [_gain_spec(d)] * n_pre
    out_shape = ([jax.ShapeDtypeStruct((s, d), F32)] if has_mix else []) + [jax.ShapeDtypeStruct((s, d), BF16)] * n_pre
    out_specs = [_row_spec(t, d)] * len(out_shape)
    res = pl.pallas_call(
        body, name=name, grid=(s // t,), in_specs=in_specs, out_specs=out_specs, out_shape=out_shape,
        compiler_params=_params(("parallel",)),
    )(*ins)
    if has_mix:
        return res[0], list(res[1:])
    return x, list(res)


def _mm_norm_res(a, b, x, g_post, pre_gain, name):
    m, k = a.shape
    d = b.shape[1]
    tm, tk = _pick(m, (1024, 512, 256)), _pick(k)
    nk = k // tk

    def body(a_ref, b_ref, x_ref, gp_ref, g_ref, mix_ref, x1_ref, xn_ref, acc_ref):
        kk = pl.program_id(1)

        @pl.when(kk == 0)
        def _():
            acc_ref[...] = jnp.zeros_like(acc_ref)

        acc_ref[...] += jnp.dot(a_ref[...].astype(BF16), b_ref[...], preferred_element_type=F32)

        @pl.when(kk == nk - 1)
        def _():
            mix = acc_ref[...].astype(BF16)
            mix_ref[...] = mix
            mv = mix.astype(F32)
            x1 = x_ref[...] + mv * _rstd(mv) * gp_ref[...]
            x1_ref[...] = x1
            xn_ref[...] = (x1 * _rstd(x1) * g_ref[...]).astype(BF16)

    rows = pl.BlockSpec((tm, d), lambda i, kk: (i, 0))
    gain = pl.BlockSpec((1, d), lambda i, kk: (0, 0))
    return pl.pallas_call(
        body, name=name, grid=(m // tm, nk),
        in_specs=[pl.BlockSpec((tm, tk), lambda i, kk: (i, kk)), pl.BlockSpec((tk, d), lambda i, kk: (kk, 0)),
                  rows, gain, gain],
        out_specs=[rows, rows, rows],
        out_shape=[jax.ShapeDtypeStruct((m, d), BF16), jax.ShapeDtypeStruct((m, d), F32),
                   jax.ShapeDtypeStruct((m, d), BF16)],
        scratch_shapes=[pltpu.VMEM((tm, d), F32)],
        compiler_params=_params(("parallel", "arbitrary")),
    )(a, b, x, g_post, pre_gain)


def _norm_res_loss(x, mix, g_post, target, name):
    s, d = x.shape
    t = _pick(s, (NORM_TILE, ROW_TILE))

    def body(x_ref, m_ref, g_ref, t_ref, dy_ref, loss_ref):
        mv = m_ref[...].astype(F32)
        y = x_ref[...] + mv * _rstd(mv) * g_ref[...]
        err = y - t_ref[...]
        dy_ref[...] = err * (1.0 / d)

        @pl.when(pl.program_id(0) == 0)
        def _():
            loss_ref[...] = jnp.zeros_like(loss_ref)

        loss_ref[...] += jnp.sum(err * err)

    dy, acc = pl.pallas_call(
        body, name=name, grid=(s // t,),
        in_specs=[_row_spec(t, d), _row_spec(t, d), _gain_spec(d), _row_spec(t, d)],
        out_specs=[_row_spec(t, d), pl.BlockSpec((8, LANE), lambda i: (0, 0))],
        out_shape=[jax.ShapeDtypeStruct((s, d), F32), jax.ShapeDtypeStruct((8, LANE), F32)],
        compiler_params=_params(("arbitrary",)),
    )(x, mix, g_post, target)
    return dy, acc[0, 0] * (0.5 / d)


def _norm_bwd(dx_out, branches, x_in, post, name):
    s, d = dx_out.shape
    t = _pick(s, (NORM_TILE, ROW_TILE))
    nb = len(branches)
    has_post = post is not None

    def body(*refs):
        dx_ref = refs[0]
        pos = 1
        dx = dx_ref[...]
        first = pl.program_id(0) == 0
        n_in = 1 + (1 + 2 * nb if nb else 0) + (2 if has_post else 0)
        outs = refs[n_in:]
        opos = 0
        if nb:
            xv = refs[pos][...]
            pos += 1
            r = _rstd(xv)
            dx_o = outs[0]
            opos = 1
            for _ in range(nb):
                dxn = refs[pos][...].astype(F32)
                g = refs[pos + 1][...]
                pos += 2
                dv, dg_rows = _rms_bwd(dxn, xv, g, r)
                dx = dx + dv
                dg_ref = outs[opos]
                opos += 1

                @pl.when(first)
                def _(dg_ref=dg_ref):
                    dg_ref[...] = jnp.zeros_like(dg_ref)

                dg_ref[...] += jnp.sum(dg_rows, axis=0, keepdims=True)
            dx_o[...] = dx
        if has_post:
            mv = refs[pos][...].astype(F32)
            g = refs[pos + 1][...]
            dm, dg_rows = _rms_bwd(dx, mv, g, _rstd(mv))
            outs[opos][...] = dm.astype(BF16)
            dg_ref = outs[opos + 1]

            @pl.when(first)
            def _():
                dg_ref[...] = jnp.zeros_like(dg_ref)

            dg_ref[...] += jnp.sum(dg_rows, axis=0, keepdims=True)

    ins, in_specs = [dx_out], [_row_spec(t, d)]
    out_shape, out_specs = [], []
    if nb:
        ins.append(x_in)
        in_specs.append(_row_spec(t, d))
        out_shape.append(jax.ShapeDtypeStruct((s, d), F32))
        out_specs.append(_row_spec(t, d))
        for dxn, g in branches:
            ins += [dxn, g]
            in_specs += [_row_spec(t, d), _gain_spec(d)]
            out_shape.append(jax.ShapeDtypeStruct((1, d), F32))
            out_specs.append(_gain_spec(d))
    if has_post:
        ins += [post[0], post[1]]
        in_specs += [_row_spec(t, d), _gain_spec(d)]
        out_shape += [jax.ShapeDtypeStruct((s, d), BF16), jax.ShapeDtypeStruct((1, d), F32)]
        out_specs += [_row_spec(t, d), _gain_spec(d)]
    res = pl.pallas_call(
        body, name=name, grid=(s // t,), in_specs=in_specs, out_specs=out_specs, out_shape=out_shape,
        compiler_params=_params(("arbitrary",)),
    )(*ins)
    res = list(res)
    dx_in = res.pop(0) if nb else dx_out
    dgs = [res.pop(0) for _ in range(nb)]
    dm, dg_post = (res[0], res[1]) if has_post else (None, None)
    return dx_in, dgs, dm, dg_post


HALO = 16


def _shift_down(u, prev, k):
    rows = lax.broadcasted_iota(jnp.int32, u.shape, 0)
    out = pltpu.roll(u, k, 0)
    for i in range(k):
        out = jnp.where(rows == i, prev[HALO - k + i:HALO - k + i + 1, :], out)
    return out


def _shift_up(u, nxt, k):
    n = u.shape[0]
    rows = lax.broadcasted_iota(jnp.int32, u.shape, 0)
    out = pltpu.roll(u, n - k, 0)
    for i in range(k):
        out = jnp.where(rows == n - k + i, nxt[i:i + 1, :], out)
    return out


def _conv_gate_fwd(p, cw, name):
    s, d3 = p.shape
    d = d3 // 3
    t = _pick(s, (ROW_TILE,))
    hb = t // HALO

    def body(p_ref, prev_ref, w_ref, z_ref):
        i = pl.program_id(0)
        pv = p_ref[...].astype(F32)
        b, u = pv[:, :d], pv[:, d:2 * d] * pv[:, 2 * d:]
        ph = prev_ref[...].astype(F32)
        up = jnp.where(i > 0, ph[:, d:2 * d] * ph[:, 2 * d:], 0.0)
        w = w_ref[...]
        y = w[0:1, :] * _shift_down(u, up, 2) + w[1:2, :] * _shift_down(u, up, 1) + w[2:3, :] * u
        z_ref[...] = (b * y).astype(BF16)

    return pl.pallas_call(
        body, name=name, grid=(s // t,),
        in_specs=[_row_spec(t, d3),
                  pl.BlockSpec((HALO, d3), lambda i: (jnp.maximum(i * hb - 1, 0), 0)),
                  pl.BlockSpec((3, d), lambda i: (0, 0))],
        out_specs=_row_spec(t, d),
        out_shape=jax.ShapeDtypeStruct((s, d), BF16),
        compiler_params=_params(("parallel",)),
    )(p, p, cw)


def _conv_gate_bwd(p, dz, cw, name):
    s, d3 = p.shape
    d = d3 // 3
    t = _pick(s, (ROW_TILE,))
    hb = t // HALO
    nt = s // t
    last_halo = s // HALO - 1

    def body(p_ref, prev_ref, next_ref, dz_ref, dznext_ref, w_ref, dp_ref, dw_ref):
        i = pl.program_id(0)
        pv = p_ref[...].astype(F32)
        b, c, h = pv[:, :d], pv[:, d:2 * d], pv[:, 2 * d:]
        u = c * h
        ph = prev_ref[...].astype(F32)
        up = jnp.where(i > 0, ph[:, d:2 * d] * ph[:, 2 * d:], 0.0)
        w = w_ref[...]
        u1, u2 = _shift_down(u, up, 1), _shift_down(u, up, 2)
        y = w[0:1, :] * u2 + w[1:2, :] * u1 + w[2:3, :] * u
        dz = dz_ref[...].astype(F32)
        dy = dz * b
        dyn = jnp.where(i < nt - 1, dznext_ref[...].astype(F32) * next_ref[...].astype(F32)[:, :d], 0.0)
        du = w[2:3, :] * dy + w[1:2, :] * _shift_up(dy, dyn, 1) + w[0:1, :] * _shift_up(dy, dyn, 2)
        dp_ref[:, :d] = (dz * y).astype(BF16)
        dp_ref[:, d:2 * d] = (du * h).astype(BF16)
        dp_ref[:, 2 * d:] = (du * c).astype(BF16)

        @pl.when(i == 0)
        def _():
            dw_ref[...] = jnp.zeros_like(dw_ref)

        dw_ref[0:1, :] += jnp.sum(dy * u2, axis=0, keepdims=True)
        dw_ref[1:2, :] += jnp.sum(dy * u1, axis=0, keepdims=True)
        dw_ref[2:3, :] += jnp.sum(dy * u, axis=0, keepdims=True)

    return pl.pallas_call(
        body, name=name, grid=(nt,),
        in_specs=[_row_spec(t, d3),
                  pl.BlockSpec((HALO, d3), lambda i: (jnp.maximum(i * hb - 1, 0), 0)),
                  pl.BlockSpec((HALO, d3), lambda i: (jnp.minimum((i + 1) * hb, last_halo), 0)),
                  _row_spec(t, d),
                  pl.BlockSpec((HALO, d), lambda i: (jnp.minimum((i + 1) * hb, last_halo), 0)),
                  pl.BlockSpec((3, d), lambda i: (0, 0))],
        out_specs=[_row_spec(t, d3), pl.BlockSpec((3, d), lambda i: (0, 0))],
        out_shape=[jax.ShapeDtypeStruct((s, d3), BF16), jax.ShapeDtypeStruct((3, d), F32)],
        compiler_params=_params(("arbitrary",)),
    )(p, p, p, dz, dz, cw)


FFN_ROWS, FFN_COLS = (1024, 512, 256), (1408, 768, 256, 128)


def _row_chunks(tm, rows=256):
    return [slice(r, r + min(rows, tm)) for r in range(0, tm, min(rows, tm))]


def _ffn_in_swiglu(xn, w_in, name):
    s, k = xn.shape
    ff = w_in.shape[1] // 2
    tm, tn = _pick(s, FFN_ROWS), _pick(ff, FFN_COLS)
    nj = ff // tn

    def body(x_ref, wg_ref, wu_ref, f_ref, a_ref):
        for rows in _row_chunks(tm):
            xv = x_ref[rows, :]
            gate = jnp.dot(xv, wg_ref[...], preferred_element_type=F32)
            up = jnp.dot(xv, wu_ref[...], preferred_element_type=F32)
            f_ref[0, rows, :] = gate.astype(BF16)
            f_ref[1, rows, :] = up.astype(BF16)
            a_ref[rows, :] = (gate * jax.nn.sigmoid(gate) * up).astype(BF16)

    return pl.pallas_call(
        body, name=name, grid=(nj, s // tm),
        in_specs=[pl.BlockSpec((tm, k), lambda j, i: (i, 0)),
                  pl.BlockSpec((k, tn), lambda j, i: (0, j)),
                  pl.BlockSpec((k, tn), lambda j, i: (0, nj + j))],
        out_specs=[pl.BlockSpec((2, tm, tn), lambda j, i: (0, i, j)), pl.BlockSpec((tm, tn), lambda j, i: (i, j))],
        out_shape=[jax.ShapeDtypeStruct((2, s, ff), BF16), jax.ShapeDtypeStruct((s, ff), BF16)],
        compiler_params=_params(("parallel", "parallel")),
    )(xn, w_in, w_in)


def _ffn_out_dx_swiglu(dff, w_out, f, name):
    s, d = dff.shape
    ff = w_out.shape[0]
    tm, tn = _pick(s, FFN_ROWS), _pick(ff, FFN_COLS)

    def body(d_ref, w_ref, f_ref, df_ref):
        for rows in _row_chunks(tm):
            da = lax.dot_general(d_ref[rows, :], w_ref[...], (((1,), (1,)), ((), ())), preferred_element_type=F32)
            gate = f_ref[0, rows, :].astype(F32)
            up = f_ref[1, rows, :].astype(F32)
            sg = jax.nn.sigmoid(gate)
            silu = gate * sg
            df_ref[0, rows, :] = (da * up * (sg + silu * (1.0 - sg))).astype(BF16)
            df_ref[1, rows, :] = (da * silu).astype(BF16)

    planes = pl.BlockSpec((2, tm, tn), lambda j, i: (0, i, j))
    return pl.pallas_call(
        body, name=name, grid=(ff // tn, s // tm),
        in_specs=[pl.BlockSpec((tm, d), lambda j, i: (i, 0)), pl.BlockSpec((tn, d), lambda j, i: (j, 0)), planes],
        out_specs=planes, out_shape=jax.ShapeDtypeStruct((2, s, ff), BF16),
        compiler_params=_params(("parallel", "parallel")),
    )(dff, w_out, f)


SUPER = 2048
RES = 16
PAIR = 128
L = 128
FWD_TOGETHER = 16
BWD_TOGETHER = 4


def _alibi_slopes(n_heads):
    h = np.arange(n_heads, dtype=np.float32) + 1.0
    return np.power(2.0, -8.0 * h / n_heads).astype(np.float32)


def _permute16(x, inverse, name):
    s, d = x.shape
    cw = LANE

    def body(x_ref, o_ref):
        if inverse:
            for m in range(L):
                o_ref[RES * m:RES * (m + 1), :] = x_ref[pl.ds(m, RES, stride=L), :]
        else:
            for r in range(RES):
                o_ref[L * r:L * (r + 1), :] = x_ref[pl.ds(r, L, stride=RES), :]

    spec = pl.BlockSpec((SUPER, cw), lambda i, j: (i, j))
    return pl.pallas_call(
        body, name=name, grid=(s // SUPER, d // cw), in_specs=[spec], out_specs=spec,
        out_shape=jax.ShapeDtypeStruct((s, d), x.dtype),
        compiler_params=_params(("parallel", "parallel")),
    )(x)


def _slope_table(d):
    nh = d // HEAD_DIM
    sl = _alibi_slopes(nh)
    tab = np.repeat(sl, HEAD_DIM).reshape(d // PAIR, 1, PAIR)
    return jnp.asarray(np.broadcast_to(tab, (d // PAIR, 8, PAIR)).copy())


def _geometry(dil):
    nch = RES // dil
    return nch, L // nch


def _band(dil):
    nch, w = _geometry(dil)
    sh = w.bit_length() - 1
    i = lax.broadcasted_iota(jnp.int32, (L, 2 * L), 0)
    j = lax.broadcasted_iota(jnp.int32, (L, 2 * L), 1)

    def pos(t):
        return jnp.bitwise_and(t, w - 1) * nch + jnp.right_shift(t, sh)

    delta = pos(i) + L - (pos(jnp.bitwise_and(j, L - 1)) + jnp.bitwise_and(j, L))
    return (delta * dil).astype(F32), (delta >= 0) & (delta <= L), j < L


def _fill_bias(bias_s, sl_ref):
    for b, dil in enumerate(DILATIONS):
        base, band, prev_half = _band(dil)
        for first in range(2):
            valid = band & jnp.logical_not(prev_half) if first else band
            for h in range(2):
                slope = sl_ref[0:1, HEAD_DIM * h:HEAD_DIM * h + 1]
                bias_s[(2 * b + first) * 2 + h] = jnp.where(valid, -slope * base, NEG_BIG)


def _bias_index(b, sb, n):
    first = jnp.logical_and(sb == 0, n == 0).astype(jnp.int32)
    return (2 * b + first) * 2


def _offsets(dil, res, n):
    nch, w = _geometry(dil)

    def al(v):
        return v if isinstance(v, int) else pl.multiple_of(v, w)

    q_off = [al((a * dil + res) * L + n * w) for a in range(nch)]
    k_off = [al((a * dil + res) * 2 * L + L + n * w) for a in range(nch)]
    kp_off = [al((a * dil + res) * 2 * L + L + n * w - w) for a in range(nch)]
    return q_off, k_off, kp_off, w


def _gather(ref, offs, w):
    parts = [ref[pl.ds(o, w), :] for o in offs]
    return parts[0] if len(parts) == 1 else jnp.concatenate(parts, axis=0)


def _scatter(ref, offs, w, val, add=False):
    for a, o in enumerate(offs):
        piece = val[a * w:(a + 1) * w, :]
        if add:
            ref[pl.ds(o, w), :] += piece
        else:
            ref[pl.ds(o, w), :] = piece


def _fill_key_buffer(buf, prev_ref, cur_ref):
    for r in range(RES):
        buf[2 * L * r:2 * L * r + L, :] = prev_ref[L * r:L * (r + 1), :]
        buf[2 * L * r + L:2 * L * (r + 1), :] = cur_ref[L * r:L * (r + 1), :]


def _two_heads(x, low):
    zero = jnp.zeros_like(x)
    return jnp.concatenate([jnp.where(low, x, zero), jnp.where(low, zero, x)], axis=0)


def _loop_blocks(dil, stages, together):
    together = max(together, dil) if dil < RES else together

    def it(i, c):
        if dil == RES:
            blocks = [(i * together + k, 0) for k in range(together)]
        else:
            blocks = [(res, i * (together // dil) + k) for k in range(together // dil) for res in range(dil)]
        state = [stages[0](res, n) for res, n in blocks]
        for stage in stages[1:]:
            state = [stage(res, n, prev) for (res, n), prev in zip(blocks, state)]
        for writes in state:
            for args in writes:
                _scatter(*args)
        return c

    lax.fori_loop(0, RES // together, it, 0)


NT = (((1,), (1,)), ((), ()))
TN = (((0,), (0,)), ((), ()))


def _attention_fwd(q, kv, name):
    s, d = q.shape
    g_n, ns = d // PAIR, s // SUPER

    def body(sl_ref, q_ref, kc_ref, kp_ref, vc_ref, vp_ref, o_ref, lse_ref, kbuf, vbuf, m_s, l_s, acc_s, bias_s):
        sb = pl.program_id(1)
        _fill_key_buffer(kbuf, kp_ref, kc_ref)
        _fill_key_buffer(vbuf, vp_ref, vc_ref)
        pl.when(sb == 0)(lambda: _fill_bias(bias_s, sl_ref))
        low = lax.broadcasted_iota(jnp.int32, (L, PAIR), 1) < HEAD_DIM
        low_k = lax.broadcasted_iota(jnp.int32, (2 * L, PAIR), 1) < HEAD_DIM
        ones_bd = _two_heads(jnp.ones((2 * L, PAIR), BF16), low_k)
        for bi, dil in enumerate(DILATIONS):
            first_branch, last_branch = bi == 0, bi == len(DILATIONS) - 1

            def scores(res, n, dil=dil):
                q_off, k_off, kp_off, w = _offsets(dil, res, n)
                qf = _gather(q_ref, q_off, w).astype(BF16)
                kcat = jnp.concatenate([_gather(kbuf, kp_off, w), _gather(kbuf, k_off, w)], axis=0).astype(BF16)
                return lax.dot_general(qf, _two_heads(kcat, low_k), NT, preferred_element_type=F32)

            def update(res, n, sc, bi=bi, dil=dil, first_branch=first_branch, last_branch=last_branch):
                q_off, k_off, kp_off, w = _offsets(dil, res, n)
                vcat = jnp.concatenate([_gather(vbuf, kp_off, w), _gather(vbuf, k_off, w)], axis=0).astype(BF16)
                v_ones = jnp.concatenate([_two_heads(vcat, low_k), ones_bd], axis=1)
                bias_at = _bias_index(bi, sb, n)
                if not first_branch:
                    m_prev = _gather(m_s, q_off, w)
                ps, m_new = [], []
                for h in range(2):
                    s_h = sc[:, 2 * L * h:2 * L * (h + 1)] + bias_s[bias_at + h]
                    mh = jnp.max(s_h, axis=1, keepdims=True)
                    if not first_branch:
                        mh = jnp.maximum(mh, m_prev[:, HEAD_DIM * h:HEAD_DIM * h + 1])
                    ps.append(jnp.exp(s_h - mh).astype(BF16))
                    m_new.append(mh)
                m_full = jnp.where(low, m_new[0], m_new[1])
                both = jnp.dot(jnp.concatenate(ps, axis=1), v_ones, preferred_element_type=F32)
                acc, l_full = both[:, :PAIR], both[:, PAIR:]
                if not first_branch:
                    alpha = jnp.exp(m_prev - m_full)
                    l_full = _gather(l_s, q_off, w) * alpha + l_full
                    acc = _gather(acc_s, q_off, w) * alpha + acc
                if last_branch:
                    return [(o_ref, q_off, w, acc / l_full, False), (lse_ref, q_off, w, m_full + jnp.log(l_full), False)]
                return [(m_s, q_off, w, m_full, False), (l_s, q_off, w, l_full, False), (acc_s, q_off, w, acc, False)]

            _loop_blocks(dil, [scores, update], FWD_TOGETHER)

    prev = lambda i: jnp.maximum(i - 1, 0)
    blk = pl.BlockSpec((SUPER, PAIR), lambda g, i: (i, g))
    in_specs = [pl.BlockSpec((None, 8, PAIR), lambda g, i: (g, 0, 0)), blk,
                pl.BlockSpec((SUPER, PAIR), lambda g, i: (i, g)),
                pl.BlockSpec((SUPER, PAIR), lambda g, i: (prev(i), g)),
                pl.BlockSpec((SUPER, PAIR), lambda g, i: (i, g_n + g)),
                pl.BlockSpec((SUPER, PAIR), lambda g, i: (prev(i), g_n + g))]
    return pl.pallas_call(
        body, name=name, grid=(g_n, ns), in_specs=in_specs, out_specs=[blk, blk],
        out_shape=[jax.ShapeDtypeStruct((s, d), F32)] * 2,
        scratch_shapes=([pltpu.VMEM((2 * SUPER, PAIR), F32)] * 2 + [pltpu.VMEM((SUPER, PAIR), F32)] * 3
                        + [pltpu.VMEM((4 * len(DILATIONS), L, 2 * L), F32)]),
        compiler_params=_params(("parallel", "arbitrary")),
    )(_slope_table(d), q, kv, kv, kv, kv)


def _attention_bwd(q, kv, o, do, lse, dkv_in, name):
    s, d = q.shape
    g_n, ns = d // PAIR, s // SUPER
    has_in = dkv_in is not None

    def body(*refs):
        sl_ref, q_ref, do_ref, o_ref, lse_ref, kc_ref, kp_ref, vc_ref, vp_ref = refs[:9]
        pos = 9
        if has_in:
            dkv_in_ref = refs[9]
            pos = 10
        dq_ref, dkv_ref, kbuf, vbuf, dkbuf, dvbuf, dq_s, bias_s = refs[pos:]
        step = pl.program_id(1)
        sb = ns - 1 - step
        _fill_key_buffer(kbuf, kp_ref, kc_ref)
        _fill_key_buffer(vbuf, vp_ref, vc_ref)

        @pl.when(step == 0)
        def _():
            _fill_bias(bias_s, sl_ref)
            dkbuf[...] = jnp.zeros_like(dkbuf)
            dvbuf[...] = jnp.zeros_like(dvbuf)

        @pl.when(step > 0)
        def _():
            for buf in (dkbuf, dvbuf):
                for r in range(RES):
                    buf[2 * L * r + L:2 * L * (r + 1), :] = buf[2 * L * r:2 * L * r + L, :]
                    buf[2 * L * r:2 * L * r + L, :] = jnp.zeros((L, PAIR), F32)

        low = lax.broadcasted_iota(jnp.int32, (L, PAIR), 1) < HEAD_DIM
        low_k = lax.broadcasted_iota(jnp.int32, (2 * L, PAIR), 1) < HEAD_DIM
        low_t = lax.broadcasted_iota(jnp.int32, (PAIR, 2 * L), 0) < HEAD_DIM
        for bi, dil in enumerate(DILATIONS):
            first_branch = bi == 0

            def scores(res, n, dil=dil):
                q_off, k_off, kp_off, w = _offsets(dil, res, n)
                qb = _gather(q_ref, q_off, w).astype(BF16)
                dof = _gather(do_ref, q_off, w)
                prod = dof * _gather(o_ref, q_off, w)
                dob = dof.astype(BF16)
                lse_f = _gather(lse_ref, q_off, w)
                zero = jnp.zeros_like(prod)
                dsum = (jnp.sum(jnp.where(low, prod, zero), axis=1, keepdims=True),
                        jnp.sum(jnp.where(low, zero, prod), axis=1, keepdims=True))
                kcat = jnp.concatenate([_gather(kbuf, kp_off, w), _gather(kbuf, k_off, w)], axis=0).astype(BF16)
                vcat = jnp.concatenate([_gather(vbuf, kp_off, w), _gather(vbuf, k_off, w)], axis=0).astype(BF16)
                k_bd, v_bd = _two_heads(kcat, low_k), _two_heads(vcat, low_k)
                sc = lax.dot_general(qb, k_bd, NT, preferred_element_type=F32)
                dp = lax.dot_general(dob, v_bd, NT, preferred_element_type=F32)
                return qb, dob, lse_f, dsum, k_bd, sc, dp

            def gradients(res, n, given, bi=bi, dil=dil, first_branch=first_branch):
                qb, dob, lse_f, dsum, k_bd, sc, dp = given
                q_off, k_off, kp_off, w = _offsets(dil, res, n)
                bias_at = _bias_index(bi, sb, n)
                ps, dss = [], []
                for h in range(2):
                    cols = slice(2 * L * h, 2 * L * (h + 1))
                    lse_h = lse_f[:, HEAD_DIM * h:HEAD_DIM * h + 1]
                    p_h = jnp.exp(sc[:, cols] + bias_s[bias_at + h] - lse_h)
                    dss.append((p_h * (dp[:, cols] - dsum[h])).astype(BF16))
                    ps.append(p_h.astype(BF16))
                ds_cat, p_cat = jnp.concatenate(dss, axis=1), jnp.concatenate(ps, axis=1)
                dq = jnp.dot(ds_cat, k_bd, preferred_element_type=F32)
                dk_t = lax.dot_general(qb, ds_cat, TN, preferred_element_type=F32)
                dv_t = lax.dot_general(dob, p_cat, TN, preferred_element_type=F32)
                dk = jnp.where(low_t, dk_t[:, :2 * L], dk_t[:, 2 * L:]).T
                dv = jnp.where(low_t, dv_t[:, :2 * L], dv_t[:, 2 * L:]).T
                return [(dq_s, q_off, w, dq, not first_branch),
                        (dkbuf, kp_off, w, dk[:L], True), (dkbuf, k_off, w, dk[L:], True),
                        (dvbuf, kp_off, w, dv[:L], True), (dvbuf, k_off, w, dv[L:], True)]

            _loop_blocks(dil, [scores, gradients], BWD_TOGETHER)

        dq_ref[...] = dq_s[...].astype(BF16)
        for r in range(RES):
            rows, cur = slice(L * r, L * (r + 1)), slice(2 * L * r + L, 2 * L * (r + 1))
            for plane, buf in enumerate((dkbuf, dvbuf)):
                if has_in:
                    dkv_ref[plane, rows, :] = buf[cur, :] + dkv_in_ref[plane, rows, :]
                else:
                    dkv_ref[plane, rows, :] = buf[cur, :]

    rev = lambda i: ns - 1 - i
    prev = lambda i: jnp.maximum(ns - 2 - i, 0)
    blk = pl.BlockSpec((SUPER, PAIR), lambda g, i: (rev(i), g))
    in_specs = [pl.BlockSpec((None, 8, PAIR), lambda g, i: (g, 0, 0)), blk, blk, blk, blk,
                pl.BlockSpec((SUPER, PAIR), lambda g, i: (rev(i), g)),
                pl.BlockSpec((SUPER, PAIR), lambda g, i: (prev(i), g)),
                pl.BlockSpec((SUPER, PAIR), lambda g, i: (rev(i), g_n + g)),
                pl.BlockSpec((SUPER, PAIR), lambda g, i: (prev(i), g_n + g))]
    ins = [_slope_table(d), q, do, o, lse, kv, kv, kv, kv]
    planes = pl.BlockSpec((2, SUPER, PAIR), lambda g, i: (0, rev(i), g))
    if has_in:
        in_specs.append(planes)
        ins.append(dkv_in)
    res = pl.pallas_call(
        body, name=name, grid=(g_n, ns), in_specs=in_specs, out_specs=[blk, planes],
        out_shape=[jax.ShapeDtypeStruct((s, d), BF16), jax.ShapeDtypeStruct((2, s, d), F32)],
        scratch_shapes=([pltpu.VMEM((2 * SUPER, PAIR), F32)] * 4 + [pltpu.VMEM((SUPER, PAIR), F32)]
                        + [pltpu.VMEM((4 * len(DILATIONS), L, 2 * L), F32)]),
        compiler_params=_params(("parallel", "arbitrary")),
    )(*ins)
    return res[0], res[1]


def _coords():
    return lax.axis_index("x"), lax.axis_index("y"), lax.axis_index("c")


def _chip_peers(x, y):
    return [(1 - x, y), (x, 1 - y), (1 - x, 1 - y)]


def _block_of(ref, axis, blk, size):
    start = pl.multiple_of(blk * size, size)
    if axis == 1:
        return ref.at[:, pl.ds(start, size)]
    return ref.at[pl.ds(start, size), :]


ANY = pl.BlockSpec(memory_space=pl.ANY)


HBM = pl.BlockSpec(memory_space=pltpu.HBM)
SEM = pl.BlockSpec(memory_space=pltpu.SEMAPHORE)
SPLIT = pltpu.CompilerParams(has_side_effects=pltpu.SideEffectType.DATAFLOW_SIDE_EFFECTING)


def _in_hbm(a):
    return pltpu.with_memory_space_constraint(a, pltpu.HBM)


def _thru(arrays):
    return [pltpu.HBM(a.shape, a.dtype) for a in arrays]


def _cast_place(w, layer, ax, dtype, after, name):
    _, k, n = w.shape
    t = _pick(k, (512, 256, 128))
    nb = k // t
    extra = [] if after is None else [after]

    def body(blk_ref, w_ref, *refs):
        b_ref, f_ref = refs[len(extra):]
        v = w_ref[...].astype(dtype)
        b_ref[...] = v
        f_ref[...] = v

    full_shape = (k, 4 * n) if ax == 1 else (4 * k, n)
    place = (lambda i, blk: (i, blk[0])) if ax == 1 else (lambda i, blk: (blk[0] * nb + i, 0))
    return pl.pallas_call(
        body, name=name,
        grid_spec=pltpu.PrefetchScalarGridSpec(
            num_scalar_prefetch=1, grid=(nb,),
            in_specs=[pl.BlockSpec((None, t, n), lambda i, blk: (layer, i, 0))] + [ANY] * len(extra),
            out_specs=[pl.BlockSpec((t, n), lambda i, blk: (i, 0)), pl.BlockSpec((t, n), place)]),
        out_shape=[jax.ShapeDtypeStruct((k, n), dtype), jax.ShapeDtypeStruct(full_shape, dtype)],
        compiler_params=_params(("parallel",)),
    )(_my_block()[None], w, *extra)


def _my_block():
    return (2 * lax.axis_index("x") + lax.axis_index("y")).astype(jnp.int32)


def _gather_start(group, carry, name):
    n, nc = len(group), len(carry)

    def body(*refs):
        blocks, fulls, send_sem, recv_sem = refs[:n], refs[n:2 * n], refs[2 * n + nc], refs[2 * n + nc + 1]
        x, y, c = _coords()
        for t, (b, _, ax) in enumerate(group):
            mine = _block_of(fulls[t], ax, 2 * x + y, b.shape[ax])
            for j, (px, py) in enumerate(_chip_peers(x, y)):
                pltpu.make_async_remote_copy(
                    src_ref=blocks[t], dst_ref=mine, send_sem=send_sem.at[3 * t + j], recv_sem=recv_sem.at[3 * t + j],
                    device_id=(px, py, c), device_id_type=MESH).start()

    arrays = [b for b, _, _ in group] + [f for _, f, _ in group] + list(carry)
    sems = [pltpu.SemaphoreType.DMA((3 * n,))] * 2
    res = pl.pallas_call(
        body, name=name, in_specs=[HBM] * len(arrays), out_specs=[SEM, SEM] + [HBM] * len(arrays),
        out_shape=sems + _thru(arrays), input_output_aliases={i: 2 + i for i in range(len(arrays))},
        compiler_params=SPLIT,
    )(*[_in_hbm(a) for a in arrays])
    return (res[0], res[1], list(res[2:2 + n]), list(res[2 + n:2 + 2 * n])), list(res[2 + 2 * n:])


def _gather_wait(group, started, after, name):
    sends, recvs, blocks, fulls = started
    m = len(group)

    def body(*refs):
        blk_refs, full_refs, send_sem, recv_sem = refs[:m], refs[m:2 * m], refs[2 * m], refs[2 * m + 1]
        x, y, c = _coords()
        for t, (b, _, ax) in enumerate(group):
            for j, (px, py) in enumerate(_chip_peers(x, y)):
                cp = pltpu.make_async_remote_copy(
                    src_ref=blk_refs[t], dst_ref=_block_of(full_refs[t], ax, 2 * px + py, b.shape[ax]),
                    send_sem=send_sem.at[3 * t + j], recv_sem=recv_sem.at[3 * t + j],
                    device_id=(px, py, c), device_id_type=MESH)
                cp.wait_send()
                cp.wait_recv()

    extra = [] if after is None else [after]
    res = pl.pallas_call(
        body, name=name, in_specs=[HBM] * (2 * m) + [SEM, SEM] + [ANY] * len(extra), out_specs=[HBM] * (2 * m),
        out_shape=_thru(blocks) + _thru(fulls), input_output_aliases={i: i for i in range(2 * m)},
        compiler_params=SPLIT,
    )(*blocks, *fulls, sends, recvs, *extra)
    return list(res[m:])


def _scatter_start(grads, carry, name):
    n = len(grads)
    n_in = 2 * n + len(carry)

    def body(*refs):
        g_refs, st_refs, send_sem, recv_sem = refs[:n], refs[n:2 * n], refs[n_in], refs[n_in + 1]
        x, y, c = _coords()
        for t, (g, ax) in enumerate(grads):
            for j, (px, py) in enumerate(_chip_peers(x, y)):
                pltpu.make_async_remote_copy(
                    src_ref=_block_of(g_refs[t], ax, 2 * px + py, g.shape[ax] // 4), dst_ref=st_refs[t].at[j],
                    send_sem=send_sem.at[3 * t + j], recv_sem=recv_sem.at[3 * t + j],
                    device_id=(px, py, c), device_id_type=MESH).start()

    arrays = [g for g, _ in grads]
    for g, ax in grads:
        shape = list(g.shape)
        shape[ax] //= 4
        arrays.append(lax.empty((3, *shape), g.dtype))
    arrays += list(carry)
    sems = [pltpu.SemaphoreType.DMA((3 * n,))] * 2
    res = pl.pallas_call(
        body, name=name, in_specs=[HBM] * n_in, out_specs=[SEM, SEM] + [HBM] * n_in,
        out_shape=sems + _thru(arrays), input_output_aliases={i: 2 + i for i in range(n_in)},
        compiler_params=SPLIT,
    )(*[_in_hbm(a) for a in arrays])
    return (res[0], res[1], list(res[2:2 + n]), list(res[2 + n:2 + 2 * n])), list(res[2 + 2 * n:])


def _scatter_wait(axes, started, after, name):
    sends, recvs, full, stacks = started
    n = len(full)
    extra = [] if after is None else [after]

    def body(*refs):
        g_refs, st_refs, send_sem, recv_sem = refs[:n], refs[n:2 * n], refs[2 * n], refs[2 * n + 1]
        x, y, c = _coords()
        for t, ax in enumerate(axes):
            size = full[t].shape[ax] // 4
            for j, (px, py) in enumerate(_chip_peers(x, y)):
                cp = pltpu.make_async_remote_copy(
                    src_ref=_block_of(g_refs[t], ax, 2 * px + py, size), dst_ref=st_refs[t].at[j],
                    send_sem=send_sem.at[3 * t + j], recv_sem=recv_sem.at[3 * t + j],
                    device_id=(px, py, c), device_id_type=MESH)
                cp.wait_send()
                cp.wait_recv()

    res = pl.pallas_call(
        body, name=name, in_specs=[HBM] * (2 * n) + [SEM, SEM] + [ANY] * len(extra), out_specs=[HBM] * (2 * n),
        out_shape=_thru(full) + _thru(stacks), input_output_aliases={i: i for i in range(2 * n)},
        compiler_params=SPLIT,
    )(*full, *stacks, sends, recvs, *extra)
    return list(res[:n]), list(res[n:])


def _pair_copies(g_refs, st_refs, out_refs, items, send_sem, recv_sem):
    x, y, c = _coords()
    copies = []
    for u, (g, ax, _) in enumerate(items):
        own = _block_of(g_refs[u], ax, 2 * x + y, g.shape[ax] // 4)
        for k, (src, dst) in enumerate([(own, out_refs[u].at[0]), (st_refs[u], out_refs[u].at[pl.ds(1, 3)])]):
            copies.append(pltpu.make_async_remote_copy(
                src_ref=src, dst_ref=dst, send_sem=send_sem.at[2 * u + k], recv_sem=recv_sem.at[2 * u + k],
                device_id=(x, y, 1 - c), device_id_type=MESH))
    return copies


def _pair_start(items, carry, name):
    n = len(items)
    n_in = 3 * n + len(carry)

    def body(*refs):
        for cp in _pair_copies(refs[:n], refs[n:2 * n], refs[2 * n:3 * n], items, refs[n_in], refs[n_in + 1]):
            cp.start()

    arrays = ([g for g, _, _ in items] + [st for _, _, st in items]
              + [lax.empty((4, *st.shape[1:]), st.dtype) for _, _, st in items] + list(carry))
    sems = [pltpu.SemaphoreType.DMA((2 * n,))] * 2
    res = pl.pallas_call(
        body, name=name, in_specs=[HBM] * n_in, out_specs=[SEM, SEM] + [HBM] * n_in,
        out_shape=sems + _thru(arrays), input_output_aliases={i: 2 + i for i in range(n_in)},
        compiler_params=SPLIT,
    )(*[_in_hbm(a) for a in arrays])
    thru = res[2:]
    return (res[0], res[1], *(list(thru[k * n:(k + 1) * n]) for k in range(3))), list(thru[3 * n:])


def _pair_wait(axes, started, name):
    send, recv, full, stacks, landing = started
    n = len(full)
    items = [(full[u], axes[u], stacks[u]) for u in range(n)]

    def body(*refs):
        for cp in _pair_copies(refs[:n], refs[n:2 * n], refs[2 * n:3 * n], items, refs[3 * n], refs[3 * n + 1]):
            cp.wait_send()
            cp.wait_recv()

    res = pl.pallas_call(
        body, name=name, in_specs=[HBM] * (3 * n) + [SEM, SEM], out_specs=[HBM] * (3 * n),
        out_shape=_thru(full + stacks + landing), input_output_aliases={i: i for i in range(3 * n)},
        compiler_params=SPLIT,
    )(*full, *stacks, *landing, send, recv)
    return list(res[:n]), list(res[n:2 * n]), list(res[2 * n:])


def _allreduce_small(v, name):
    r, cdim = v.shape

    def body(v_ref, out_ref, buf, send_sems, recv_sems):
        x, y, c = _coords()
        me = 4 * x + 2 * y + c
        buf[0] = v_ref[...]
        sends = []
        for k in range(1, 8):
            peer = (x if not (k & 4) else 1 - x, y if not (k & 2) else 1 - y, c if not (k & 1) else 1 - c)
            cp = pltpu.make_async_remote_copy(
                src_ref=v_ref, dst_ref=buf.at[k], send_sem=send_sems.at[k - 1], recv_sem=recv_sems.at[k - 1],
                device_id=peer, device_id_type=MESH)
            cp.start()
            sends.append(cp)
        for cp in sends:
            cp.wait_recv()
        total = buf[me]
        for src in range(1, 8):
            total = total + buf[jnp.bitwise_xor(me, src)]
        out_ref[...] = total
        for cp in sends:
            cp.wait_send()

    return pl.pallas_call(
        body, name=name,
        in_specs=[pl.BlockSpec(memory_space=pltpu.VMEM)], out_specs=pl.BlockSpec(memory_space=pltpu.VMEM),
        out_shape=jax.ShapeDtypeStruct((r, cdim), F32),
        scratch_shapes=[pltpu.VMEM((8, r, cdim), F32), pltpu.SemaphoreType.DMA((7,)), pltpu.SemaphoreType.DMA((7,))],
        compiler_params=pltpu.CompilerParams(has_side_effects=True),
    )(v)


def _adamw_math(w, g, m, v):
    m = ADAM_B1 * m + (1.0 - ADAM_B1) * g
    v = ADAM_B2 * v + (1.0 - ADAM_B2) * jnp.square(g)
    m_hat = m / (1.0 - ADAM_B1 ** ADAM_STEP)
    v_hat = v / (1.0 - ADAM_B2 ** ADAM_STEP)
    delta = -ADAM_LR * (m_hat / (jnp.sqrt(v_hat) + ADAM_EPS) + ADAM_WD * w)
    return delta, m, v


def _adamw(w, m, v, grads, name):
    r, cdim = w.shape
    paired = isinstance(grads, list)
    layers = len(grads) if paired else 1
    t = _pick(r // layers, (128, 64, 32, 16, 8))
    per_layer = r // layers // t
    n_grad = 3 * layers if paired else 1

    def body(*refs):
        refs = refs[1:] if paired else refs
        w_ref, m_ref, v_ref = refs[:3]
        outs = refs[3 + n_grad:]

        def update(g):
            delta, m_new, v_new = _adamw_math(w_ref[...], g, m_ref[...], v_ref[...])
            outs[0][...] = g
            outs[1][...] = delta
            outs[2][...] = m_new
            outs[3][...] = v_new

        if not paired:
            update(refs[3][...])
            return
        layer = pl.program_id(0) // per_layer
        for l in range(layers):
            @pl.when(layer == l)
            def _(own_ref=refs[3 + 3 * l], st_ref=refs[4 + 3 * l], sib_ref=refs[5 + 3 * l]):
                sa = own_ref[...].astype(F32)
                sb = sib_ref[0].astype(F32)
                for k in range(3):
                    sa = sa + st_ref[k].astype(F32)
                    sb = sb + sib_ref[k + 1].astype(F32)
                update(sa + sb)

    out_shape = [jax.ShapeDtypeStruct((r, cdim), F32)] * 4
    if not paired:
        spec = pl.BlockSpec((t, cdim), lambda i: (i, 0))
        return pl.pallas_call(
            body, name=name, grid=(r // t,), in_specs=[spec] * 4, out_specs=[spec] * 4, out_shape=out_shape,
            compiler_params=_params(("parallel",)),
        )(w, m, v, grads)

    spec = pl.BlockSpec((t, cdim), lambda i, blk: (i, 0))
    ins, in_specs = [w, m, v], [spec] * 3
    for l, (g, ax, stack, sib) in enumerate(grads):
        row = lambda i, l=l: jnp.clip(i - l * per_layer, 0, per_layer - 1)
        own = ((lambda i, blk, row=row: (row(i), blk[0])) if ax == 1
               else (lambda i, blk, row=row: (blk[0] * per_layer + row(i), 0)))
        ins += [g, stack, sib]
        in_specs += [pl.BlockSpec((t, cdim), own),
                     pl.BlockSpec((3, t, cdim), lambda i, blk, row=row: (0, row(i), 0)),
                     pl.BlockSpec((4, t, cdim), lambda i, blk, row=row: (0, row(i), 0))]
    return pl.pallas_call(
        body, name=name,
        grid_spec=pltpu.PrefetchScalarGridSpec(
            num_scalar_prefetch=1, grid=(r // t,), in_specs=in_specs, out_specs=[spec] * 4),
        out_shape=out_shape, compiler_params=_params(("parallel",)),
    )(_my_block()[None], *ins)


def _local_step(x, target, gains, conv_ws, kv_gain, weights_of, send_grads):
    depth = len(gains)
    n_a = len(conv_ws)
    saved, ws = [], []
    kv = kvn = None
    _, (xn,) = _norm_res_fwd(x, None, None, [gains[0][0]], "norm_first")
    h = x
    for l in range(depth):
        g = gains[l]
        sv = {"x_in": h, "xn": xn}
        w = weights_of(l, "mix", h)
        ws.append(w)
        if l == n_a:
            kv = _mm(kvn, w["kv"], "nn", F32, "kv_fwd")
        if l < n_a:
            p = _mm(xn, w["conv_in"], "nn", BF16, f"conv_in_fwd_{l}")
            z = _conv_gate_fwd(p, conv_ws[l], f"conv_gate_fwd_{l}")
            mix, x1, xn2 = _mm_norm_res(z, w["conv_out"], h, g[1], g[2], f"conv_out_fwd_{l}")
            sv.update(p=p, z=z)
        else:
            j = l - n_a
            q = _mm(xn, w["q"], "nn", F32, f"q_fwd_{j}", scale=HEAD_DIM ** -0.5)
            o, lse = _attention_fwd(q, kv, f"attn_fwd_{j}")
            mix, x1, xn2 = _mm_norm_res(o, w["o"], h, g[1], g[2], f"o_fwd_{j}")
            sv.update(q=q, o=o, lse=lse)
        w.update(weights_of(l, "ffn", mix))
        f, a = _ffn_in_swiglu(xn2, w["ffn_in"], f"ffn_in_fwd_{l}")
        ff = _mm(a, w["ffn_out"], "nn", BF16, f"ffn_out_fwd_{l}")
        sv.update(mix=mix, x1=x1, xn2=xn2, f=f, a=a, ff=ff)
        saved.append(sv)
        if l == depth - 1:
            dx, loss = _norm_res_loss(x1, ff, g[3], target, "norm_loss")
        elif l == n_a - 1:
            h, _ = _norm_res_fwd(x1, ff, g[3], [], f"norm_end_{l}")
            h = _permute16(h, False, "permute_stream")
            _, (xn, kvn) = _norm_res_fwd(h, None, None, [gains[l + 1][0], kv_gain], "norm_permuted")
        else:
            h, (xn,) = _norm_res_fwd(x1, ff, g[3], [gains[l + 1][0]], f"norm_end_{l}")
    d_gains = [[None] * 4 for _ in range(depth)]
    d_conv = [None] * n_a
    d_kv_gain = None
    dkv = None
    _, _, dff, d_gains[depth - 1][3] = _norm_bwd(dx, [], None, (saved[-1]["ff"], gains[-1][3]), "norm_loss_bwd")
    for l in reversed(range(depth)):
        sv, g, w, grads = saved[l], gains[l], ws[l], {}
        grads["ffn_out"] =_mm(sv["a"], dff, "tn", BF16, f"ffn_out_dw_{l}")
        df = _ffn_out_dx_swiglu(dff, w["ffn_out"], sv["f"], f"ffn_out_dx_{l}")
        dxn2 = _mm(df, w["ffn_in"], "nt", BF16, f"ffn_in_dx_{l}")
        grads["ffn_in"] =_mm(sv["xn2"], df, "tn", BF16, f"ffn_in_dw_{l}")
        dx, (d_gains[l][2],), dmix, d_gains[l][1] = _norm_bwd(
            dx, [(dxn2, g[2])], sv["x1"], (sv["mix"], g[1]), f"norm_mid_bwd_{l}")
        dx, dmix = send_grads(l, "ffn", grads, [dx, dmix])
        if l < n_a:
            dz = _mm(dmix, w["conv_out"], "nt", BF16, f"conv_out_dx_{l}")
            grads["conv_out"] =_mm(sv["z"], dmix, "tn", BF16, f"conv_out_dw_{l}")
            dp, d_conv[l] = _conv_gate_bwd(sv["p"], dz, conv_ws[l], f"conv_gate_bwd_{l}")
            dxn = _mm(dp, w["conv_in"], "nt", BF16, f"conv_in_dx_{l}")
            grads["conv_in"] =_mm(sv["xn"], dp, "tn", BF16, f"conv_in_dw_{l}")
        else:
            j = l - n_a
            do = _mm(dmix, w["o"], "nt", F32, f"o_dx_{j}")
            grads["o"] =_mm(sv["o"], dmix, "tn", BF16, f"o_dw_{j}")
            dq, dkv = _attention_bwd(sv["q"], kv, sv["o"], do, sv["lse"], dkv, f"attn_bwd_{j}")
            scale = HEAD_DIM ** -0.5
            dxn = _mm(dq, w["q"], "nt", BF16, f"q_dx_{j}", scale=scale)
            grads["q"] =_mm(sv["xn"], dq, "tn", BF16, f"q_dw_{j}", scale=scale)
        branches = [(dxn, g[0])]
        if l == n_a:
            dkvn = _mm(dkv, w["kv"], "nt", BF16, "kv_dx")
            grads["kv"] =_mm(kvn, dkv, "tn", BF16, "kv_dw")
            branches.append((dkvn, kv_gain))
        post = (saved[l - 1]["ff"], gains[l - 1][3]) if l > 0 else None
        if l == n_a:
            dx, dgs, _, _ = _norm_bwd(dx, branches, sv["x_in"], None, f"norm_end_bwd_{l}")
            dx = _permute16(dx, True, "unpermute_stream")
            _, _, dff, dg_post = _norm_bwd(dx, [], None, post, "norm_boundary_bwd")
        else:
            dx, dgs, dff, dg_post = _norm_bwd(dx, branches, sv["x_in"], post, f"norm_end_bwd_{l}")
        if dff is None:
            send_grads(l, "mix", grads, [])
        else:
            dx, dff = send_grads(l, "mix", grads, [dx, dff])
        d_gains[l][0] = dgs[0]
        if l == n_a:
            d_kv_gain = dgs[1]
        if l > 0:
            d_gains[l - 1][3] = dg_post
    return loss, dx, d_gains, d_conv, d_kv_gain


BIG = (
    ("conv_in", 1), ("conv_out", 0), ("kv", 1), ("q", 0), ("o", 0), ("ffn_in", 1), ("ffn_out", 0))


def kernel(x, norm_g, conv_in_w, conv_w, conv_out_w, kv_norm_g, kv_w, q_w, o_w, ffn_in_w, ffn_out_w, loss_target, m_norm_g, m_conv_in_w, m_conv_w, m_conv_out_w, m_kv_norm_g, m_kv_w, m_q_w, m_o_w, m_ffn_in_w, m_ffn_out_w, v_norm_g, v_conv_in_w, v_conv_w, v_conv_out_w, v_kv_norm_g, v_kv_w, v_q_w, v_o_w, v_ffn_in_w, v_ffn_out_w):
    depth, _, dq = norm_g.shape
    d = 4 * dq
    n_a = conv_w.shape[0]
    big_w = {"conv_in": conv_in_w, "conv_out": conv_out_w, "kv": kv_w[None], "q": q_w, "o": o_w,
             "ffn_in": ffn_in_w, "ffn_out": ffn_out_w}
    big_m = {"conv_in": m_conv_in_w, "conv_out": m_conv_out_w, "kv": m_kv_w[None], "q": m_q_w, "o": m_o_w,
             "ffn_in": m_ffn_in_w, "ffn_out": m_ffn_out_w}
    big_v = {"conv_in": v_conv_in_w, "conv_out": v_conv_out_w, "kv": v_kv_w[None], "q": v_q_w, "o": v_o_w,
             "ffn_in": v_ffn_in_w, "ffn_out": v_ffn_out_w}

    n_gain, n_tap = depth * 4, n_a * conv_w.shape[1]
    small_rows = -(-(n_gain + n_tap + 1) // 8) * 8
    pad_rows = small_rows - n_gain - n_tap

    def pack_small(gains, taps):
        return jnp.concatenate([gains.reshape(n_gain, dq), taps.reshape(n_tap, dq), jnp.zeros((pad_rows, dq), F32)])

    axis_of = dict(BIG)

    def matrices_of(l, part):
        if part == "ffn":
            return [("ffn_in", l), ("ffn_out", l)]
        if l < n_a:
            return [("conv_in", l), ("conv_out", l)]
        return ([("kv", 0)] if l == n_a else []) + [("q", l - n_a), ("o", l - n_a)]

    halves = [(l, part) for l in range(depth) for part in ("mix", "ffn")]
    def placed(half, after):
        return [(*_cast_place(big_w[name], i, axis_of[name], BF16, after, f"place_{name}_{i}"), axis_of[name])
                for name, i in matrices_of(*half)]

    groups = {halves[0]: placed(halves[0], None)}
    groups[halves[0]].append((*_cast_place(pack_small(norm_g, conv_w)[None], 0, 1, F32, None, "place_small"), 1))
    started = {halves[0]: _gather_start(groups[halves[0]], [], "gather_start_0_mix")[0]}
    for half in halves[1:]:
        groups[half] = placed(half, started[halves[0]][2][0])

    AHEAD = 2

    def fetch(half, after):
        full = _gather_wait(groups[half], started[half], after, "gather_wait_%d_%s" % half)
        nxt = halves.index(half) + AHEAD
        if nxt < len(halves):
            started[halves[nxt]], full = _gather_start(groups[halves[nxt]], full, "gather_start_%d_%s" % halves[nxt])
        return full

    target = _permute16(loss_target.reshape(x.shape[1:]), False, "permute_target")
    for half in halves[1:AHEAD]:
        started[half], (target,) = _gather_start(groups[half], [target], "gather_start_%d_%s" % half)
    first = fetch(halves[0], target)
    small = first[-1]
    gains = [[small[4 * l + i][None] for i in range(4)] for l in range(depth)]
    conv_ws = [small[n_gain + 3 * l:n_gain + 3 * l + 3] for l in range(n_a)]
    kv_gain = kv_norm_g[None]

    def weights_of(l, part, after):
        full = first if (l, part) == halves[0] else fetch((l, part), after)
        return {name: full[t] for t, (name, _) in enumerate(matrices_of(l, part))}

    sent, paired = {}, {}
    LAG = 2

    def to_sibling(half, carry, after):
        axes = [axis_of[name] for name, _ in matrices_of(*half)]
        full, stacks = _scatter_wait(axes, sent[half], after, "scatter_wait_%d_%s" % half)
        paired[half], carry = _pair_start(list(zip(full, axes, stacks)), carry, "pair_start_%d_%s" % half)
        return carry

    def send_grads(l, part, grads, carry):
        sent[l, part], carry = _scatter_start(
            [(grads[name], axis_of[name]) for name, _ in matrices_of(l, part)], carry, f"scatter_start_{l}_{part}")
        at = halves.index((l, part))
        if carry:
            if at + LAG < len(halves):
                carry = to_sibling(halves[at + LAG], carry, carry[0])
        else:
            for older in range(at + LAG, at, -1):
                if halves[older] not in paired:
                    to_sibling(halves[older], [], sent[l, part][2][0])
        if at == 1 and carry:
            carry = to_sibling(halves[at + 1], carry, carry[0])
        return carry

    loss, dx, d_gains, d_conv, d_kv_gain = _local_step(
        x.reshape(x.shape[1:]), target, gains, conv_ws, kv_gain, weights_of, send_grads)
    loss = lax.psum(loss, ("x", "y", "c"))

    small_g = jnp.concatenate([dg for row in d_gains for dg in row] + list(d_conv) + [d_kv_gain]
                              + [jnp.zeros((pad_rows - 1, d), F32)])
    small_g = _allreduce_small(small_g, "allreduce_small")
    blk = 2 * lax.axis_index("x") + lax.axis_index("y")
    mine_small = lax.dynamic_slice_in_dim(small_g, blk * dq, dq, axis=1)
    kv_rows = d // dq

    def pack_opt(gains_like, taps_like, kv_like):
        rows = jnp.concatenate([gains_like.reshape(n_gain, dq), taps_like.reshape(n_tap, dq), kv_like.reshape(kv_rows, dq)])
        extra = -rows.shape[0] % 8
        return jnp.concatenate([rows, jnp.zeros((extra, dq), F32)]) if extra else rows

    sw = pack_opt(norm_g, conv_w, kv_norm_g)
    sm = pack_opt(m_norm_g, m_conv_w, m_kv_norm_g)
    sv = pack_opt(v_norm_g, v_conv_w, v_kv_norm_g)
    sg = pack_opt(mine_small[:n_gain], mine_small[n_gain:n_gain + n_tap], small_g[n_gain + n_tap])
    s_out = _adamw(sw, sm, sv, sg, "adamw_small")

    def unpack(a):
        return (a[:n_gain].reshape(depth, 4, dq), a[n_gain:n_gain + n_tap].reshape(n_a, -1, dq),
                a[n_gain + n_tap:n_gain + n_tap + kv_rows].reshape(d))

    small_out = [unpack(a) for a in s_out]

    landed, big_out = {}, {}

    def update(name):
        for half in halves:
            if matrices_of(*half)[0] not in landed and any(n == name for n, _ in matrices_of(*half)):
                axes = [axis_of[n] for n, _ in matrices_of(*half)]
                landed.update(zip(matrices_of(*half), zip(*_pair_wait(axes, paired[half], "pair_wait_%d_%s" % half))))
        shp, ax = big_w[name].shape, axis_of[name]
        rows, cols = shp[0] * shp[1], shp[2]
        flat = lambda a: a.reshape(rows, cols)
        full, stacks, sibling = zip(*[landed[name, i] for i in range(shp[0])])
        res = _adamw(flat(big_w[name]), flat(big_m[name]), flat(big_v[name]),
                     [(full[i], ax, stacks[i], sibling[i]) for i in range(shp[0])], f"adamw_{name}")
        big_out[name] = [a.reshape(shp[1:] if name == "kv" else shp) for a in res]

    pending = [half for half in reversed(halves) if half not in paired]
    for half in pending[:-1]:
        to_sibling(half, [], None)
    late = [name for name, _ in BIG if any(n == name for n, _ in matrices_of(*pending[-1]))]
    for name, _ in BIG:
        if name not in late:
            update(name)
    to_sibling(pending[-1], [], big_out["ffn_out"][0])
    for name in late:
        update(name)

    def leaves(i):
        ng, cw_, kg = small_out[i]
        return [ng, big_out["conv_in"][i], cw_, big_out["conv_out"][i], kg, big_out["kv"][i], big_out["q"][i],
                big_out["o"][i], big_out["ffn_in"][i], big_out["ffn_out"][i]]

    return (loss, dx.reshape(x.shape), *leaves(0), *leaves(1), *leaves(2), *leaves(3))
```

```python
import jax
import jax.numpy as jnp
import numpy as np
from jax import lax
from jax.experimental import pallas as pl
from jax.experimental.pallas import tpu as pltpu

F32 = jnp.float32
BF16 = jnp.bfloat16
HEAD_DIM = 64
DILATIONS = (1, 4, 16)
NORM_EPS = 1e-6
NEG_BIG = -1e30
VMEM_LIMIT = 48 * 1024 * 1024
ROW_TILE = 256
NORM_TILE = 512
LANE = 128
MESH = pl.DeviceIdType.MESH

ADAM_LR = 0.001
ADAM_B1 = 0.9
ADAM_B2 = 0.999
ADAM_EPS = 1e-08
ADAM_WD = 0.01
ADAM_STEP = 10

TILE_CANDIDATES = (1024, 1408, 768, 512, 384, 256, 128)


def _pick(dim, cands=TILE_CANDIDATES):
    for c in cands:
        if c <= dim and dim % c == 0:
            return c
    return dim


def _params(sem):
    return pltpu.CompilerParams(dimension_semantics=sem, vmem_limit_bytes=VMEM_LIMIT)


def _mm(a, b, mode, out_dtype, name, scale=None):
    a_planes = a.shape[0] if a.ndim == 3 else 1
    b_planes = b.shape[0] if b.ndim == 3 else 1
    if mode == "nn":
        m, k = a.shape[-2], a.shape[-1] * a_planes
        n = b.shape[1]
    elif mode == "nt":
        m, k = a.shape[-2], a.shape[-1] * a_planes
        n = b.shape[0]
    else:
        k, m = a.shape
        n = b.shape[-1] * b_planes
    tm, tn = _pick(m), _pick(n // b_planes)
    tk = _pick(k // a_planes, ((2048,) if mode == "tn" else (3072, 2816)) + TILE_CANDIDATES)
    nk = k // tk
    ka, nb = k // a_planes // tk, n // b_planes // tn
    if a_planes > 1:
        a_spec = pl.BlockSpec((None, tm, tk), lambda i, j, kk: (kk // ka, i, kk % ka))
    elif mode == "tn":
        a_spec = pl.BlockSpec((tk, tm), lambda i, j, kk: (kk, i))
    else:
        a_spec = pl.BlockSpec((tm, tk), lambda i, j, kk: (i, kk))
    if mode == "nn":
        b_spec = pl.BlockSpec((tk, tn), lambda i, j, kk: (kk, j))
        dims = (((1,), (0,)), ((), ()))
    elif mode == "nt":
        b_spec = pl.BlockSpec((tn, tk), lambda i, j, kk: (j, kk))
        dims = (((1,), (1,)), ((), ()))
    else:
        b_spec = (pl.BlockSpec((None, tk, tn), lambda i, j, kk: (j // nb, kk, j % nb)) if b_planes > 1
                  else pl.BlockSpec((tk, tn), lambda i, j, kk: (kk, j)))
        dims = (((0,), (0,)), ((), ()))

    def finish(acc):
        if scale is not None:
            acc = acc * scale
        return acc.astype(out_dtype)

    if nk == 1:
        def body(a_ref, b_ref, o_ref):
            o_ref[...] = finish(lax.dot_general(a_ref[...].astype(BF16), b_ref[...].astype(BF16), dims, preferred_element_type=F32))
        scratch = []
    else:
        def body(a_ref, b_ref, o_ref, acc_ref):
            kk = pl.program_id(2)

            @pl.when(kk == 0)
            def _():
                acc_ref[...] = jnp.zeros_like(acc_ref)

            acc_ref[...] += lax.dot_general(a_ref[...].astype(BF16), b_ref[...].astype(BF16), dims, preferred_element_type=F32)

            @pl.when(kk == nk - 1)
            def _():
                o_ref[...] = finish(acc_ref[...])
        scratch = [pltpu.VMEM((tm, tn), F32)]

    return pl.pallas_call(
        body, name=name,
        grid=(m // tm, n // tn, nk),
        in_specs=[a_spec, b_spec],
        out_specs=pl.BlockSpec((tm, tn), lambda i, j, kk: (i, j)),
        out_shape=jax.ShapeDtypeStruct((m, n), out_dtype),
        scratch_shapes=scratch,
        compiler_params=_params(("parallel", "parallel", "arbitrary")),
    )(a, b)


def _rstd(v):
    return lax.rsqrt(jnp.mean(v * v, axis=-1, keepdims=True) + NORM_EPS)


def _rms_bwd(dy, v, g, r):
    xhat = v * r
    gy = dy * g
    dv = r * (gy - xhat * jnp.mean(gy * xhat, axis=-1, keepdims=True))
    return dv, dy * xhat


def _row_spec(t, width):
    return pl.BlockSpec((t, width), lambda i: (i, 0))


def _gain_spec(width):
    return pl.BlockSpec((1, width), lambda i: (0, 0))


def _norm_res_fwd(x, mix, g_post, pre_gains, name):
    s, d = x.shape
    t = _pick(s, (NORM_TILE, ROW_TILE))
    has_mix = mix is not None
    n_pre = len(pre_gains)

    def body(*refs):
        x_ref = refs[0]
        pos = 1
        x1 = x_ref[...]
        if has_mix:
            mv = refs[1][...].astype(F32)
            x1 = x1 + mv * _rstd(mv) * refs[2][...]
            pos = 3
        gains = refs[pos:pos + n_pre]
        outs = refs[pos + n_pre:]
        if has_mix:
            outs[0][...] = x1
            outs = outs[1:]
        r = _rstd(x1)
        for g_ref, o_ref in zip(gains, outs):
            o_ref[...] = (x1 * r * g_ref[...]).astype(BF16)

    ins = [x] + ([mix, g_post] if has_mix else []) + list(pre_gains)
    in_specs = [_row_spec(t, d)] + ([_row_spec(t, d), _gain_spec(d)] if has_mix else []) + [_gain_spec(d)] * n_pre
    out_shape = ([jax.ShapeDtypeStruct((s, d), F32)] if has_mix else []) + [jax.ShapeDtypeStruct((s, d), BF16)] * n_pre
    out_specs = [_row_spec(t, d)] * len(out_shape)
    res = pl.pallas_call(
        body, name=name, grid=(s // t,), in_specs=in_specs, out_specs=out_specs, out_shape=out_shape,
        compiler_params=_params(("parallel",)),
    )(*ins)
    if has_mix:
        return res[0], list(res[1:])
    return x, list(res)


def _mm_norm_res(a, b, x, g_post, pre_gain, name):
    m, k = a.shape
    d = b.shape[1]
    tm, tk = _pick(m, (1024, 512, 256)), _pick(k)
    nk = k // tk

    def body(a_ref, b_ref, x_ref, gp_ref, g_ref, mix_ref, x1_ref, xn_ref, acc_ref):
        kk = pl.program_id(1)

        @pl.when(kk == 0)
        def _():
            acc_ref[...] = jnp.zeros_like(acc_ref)

        acc_ref[...] += jnp.dot(a_ref[...].astype(BF16), b_ref[...], preferred_element_type=F32)

        @pl.when(kk == nk - 1)
        def _():
            mix = acc_ref[...].astype(BF16)
            mix_ref[...] = mix
            mv = mix.astype(F32)
            x1 = x_ref[...] + mv * _rstd(mv) * gp_ref[...]
            x1_ref[...] = x1
            xn_ref[...] = (x1 * _rstd(x1) * g_ref[...]).astype(BF16)

    rows = pl.BlockSpec((tm, d), lambda i, kk: (i, 0))
    gain = pl.BlockSpec((1, d), lambda i, kk: (0, 0))
    return pl.pallas_call(
        body, name=name, grid=(m // tm, nk),
        in_specs=[pl.BlockSpec((tm, tk), lambda i, kk: (i, kk)), pl.BlockSpec((tk, d), lambda i, kk: (kk, 0)),
                  rows, gain, gain],
        out_specs=[rows, rows, rows],
        out_shape=[jax.ShapeDtypeStruct((m, d), BF16), jax.ShapeDtypeStruct((m, d), F32),
                   jax.ShapeDtypeStruct((m, d), BF16)],
        scratch_shapes=[pltpu.VMEM((tm, d), F32)],
        compiler_params=_params(("parallel", "arbitrary")),
    )(a, b, x, g_post, pre_gain)


def _norm_res_loss(x, mix, g_post, target, name):
    s, d = x.shape
    t = _pick(s, (NORM_TILE, ROW_TILE))

    def body(x_ref, m_ref, g_ref, t_ref, dy_ref, loss_ref):
        mv = m_ref[...].astype(F32)
        y = x_ref[...] + mv * _rstd(mv) * g_ref[...]
        err = y - t_ref[...]
        dy_ref[...] = err * (1.0 / d)

        @pl.when(pl.program_id(0) == 0)
        def _():
            loss_ref[...] = jnp.zeros_like(loss_ref)

        loss_ref[...] += jnp.sum(err * err)

    dy, acc = pl.pallas_call(
        body, name=name, grid=(s // t,),
        in_specs=[_row_spec(t, d), _row_spec(t, d), _gain_spec(d), _row_spec(t, d)],
        out_specs=[_row_spec(t, d), pl.BlockSpec((8, LANE), lambda i: (0, 0))],
        out_shape=[jax.ShapeDtypeStruct((s, d), F32), jax.ShapeDtypeStruct((8, LANE), F32)],
        compiler_params=_params(("arbitrary",)),
    )(x, mix, g_post, target)
    return dy, acc[0, 0] * (0.5 / d)


def _norm_bwd(dx_out, branches, x_in, post, name):
    s, d = dx_out.shape
    t = _pick(s, (NORM_TILE, ROW_TILE))
    nb = len(branches)
    has_post = post is not None

    def body(*refs):
        dx_ref = refs[0]
        pos = 1
        dx = dx_ref[...]
        first = pl.program_id(0) == 0
        n_in = 1 + (1 + 2 * nb if nb else 0) + (2 if has_post else 0)
        outs = refs[n_in:]
        opos = 0
        if nb:
            xv = refs[pos][...]
            pos += 1
            r = _rstd(xv)
            dx_o = outs[0]
            opos = 1
            for _ in range(nb):
                dxn = refs[pos][...].astype(F32)
                g = refs[pos + 1][...]
                pos += 2
                dv, dg_rows = _rms_bwd(dxn, xv, g, r)
                dx = dx + dv
                dg_ref = outs[opos]
                opos += 1

                @pl.when(first)
                def _(dg_ref=dg_ref):
                    dg_ref[...] = jnp.zeros_like(dg_ref)

                dg_ref[...] += jnp.sum(dg_rows, axis=0, keepdims=True)
            dx_o[...] = dx
        if has_post:
            mv = refs[pos][...].astype(F32)
            g = refs[pos + 1][...]
            dm, dg_rows = _rms_bwd(dx, mv, g, _rstd(mv))
            outs[opos][...] = dm.astype(BF16)
            dg_ref = outs[opos + 1]

            @pl.when(first)
            def _():
                dg_ref[...] = jnp.zeros_like(dg_ref)

            dg_ref[...] += jnp.sum(dg_rows, axis=0, keepdims=True)

    ins, in_specs = [dx_out], [_row_spec(t, d)]
    out_shape, out_specs = [], []
    if nb:
        ins.append(x_in)
        in_specs.append(_row_spec(t, d))
        out_shape.append(jax.ShapeDtypeStruct((s, d), F32))
        out_specs.append(_row_spec(t, d))
        for dxn, g in branches:
            ins += [dxn, g]
            in_specs += [_row_spec(t, d), _gain_spec(d)]
            out_shape.append(jax.ShapeDtypeStruct((1, d), F32))
            out_specs.append(_gain_spec(d))
    if has_post:
        ins += [post[0], post[1]]
        in_specs += [_row_spec(t, d), _gain_spec(d)]
        out_shape += [jax.ShapeDtypeStruct((s, d), BF16), jax.ShapeDtypeStruct((1, d), F32)]
        out_specs += [_row_spec(t, d), _gain_spec(d)]
    res = pl.pallas_call(
        body, name=name, grid=(s // t,), in_specs=in_specs, out_specs=out_specs, out_shape=out_shape,
        compiler_params=_params(("arbitrary",)),
    )(*ins)
    res = list(res)
    dx_in = res.pop(0) if nb else dx_out
    dgs = [res.pop(0) for _ in range(nb)]
    dm, dg_post = (res[0], res[1]) if has_post else (None, None)
    return dx_in, dgs, dm, dg_post


HALO = 16


def _shift_down(u, prev, k):
    rows = lax.broadcasted_iota(jnp.int32, u.shape, 0)
    out = pltpu.roll(u, k, 0)
    for i in range(k):
        out = jnp.where(rows == i, prev[HALO - k + i:HALO - k + i + 1, :], out)
    return out


def _shift_up(u, nxt, k):
    n = u.shape[0]
    rows = lax.broadcasted_iota(jnp.int32, u.shape, 0)
    out = pltpu.roll(u, n - k, 0)
    for i in range(k):
        out = jnp.where(rows == n - k + i, nxt[i:i + 1, :], out)
    return out


def _conv_gate_fwd(p, cw, name):
    s, d3 = p.shape
    d = d3 // 3
    t = _pick(s, (ROW_TILE,))
    hb = t // HALO

    def body(p_ref, prev_ref, w_ref, z_ref):
        i = pl.program_id(0)
        pv = p_ref[...].astype(F32)
        b, u = pv[:, :d], pv[:, d:2 * d] * pv[:, 2 * d:]
        ph = prev_ref[...].astype(F32)
        up = jnp.where(i > 0, ph[:, d:2 * d] * ph[:, 2 * d:], 0.0)
        w = w_ref[...]
        y = w[0:1, :] * _shift_down(u, up, 2) + w[1:2, :] * _shift_down(u, up, 1) + w[2:3, :] * u
        z_ref[...] = (b * y).astype(BF16)

    return pl.pallas_call(
        body, name=name, grid=(s // t,),
        in_specs=[_row_spec(t, d3),
                  pl.BlockSpec((HALO, d3), lambda i: (jnp.maximum(i * hb - 1, 0), 0)),
                  pl.BlockSpec((3, d), lambda i: (0, 0))],
        out_specs=_row_spec(t, d),
        out_shape=jax.ShapeDtypeStruct((s, d), BF16),
        compiler_params=_params(("parallel",)),
    )(p, p, cw)


def _conv_gate_bwd(p, dz, cw, name):
    s, d3 = p.shape
    d = d3 // 3
    t = _pick(s, (ROW_TILE,))
    hb = t // HALO
    nt = s // t
    last_halo = s // HALO - 1

    def body(p_ref, prev_ref, next_ref, dz_ref, dznext_ref, w_ref, dp_ref, dw_ref):
        i = pl.program_id(0)
        pv = p_ref[...].astype(F32)
        b, c, h = pv[:, :d], pv[:, d:2 * d], pv[:, 2 * d:]
        u = c * h
        ph = prev_ref[...].astype(F32)
        up = jnp.where(i > 0, ph[:, d:2 * d] * ph[:, 2 * d:], 0.0)
        w = w_ref[...]
        u1, u2 = _shift_down(u, up, 1), _shift_down(u, up, 2)
        y = w[0:1, :] * u2 + w[1:2, :] * u1 + w[2:3, :] * u
        dz = dz_ref[...].astype(F32)
        dy = dz * b
        dyn = jnp.where(i < nt - 1, dznext_ref[...].astype(F32) * next_ref[...].astype(F32)[:, :d], 0.0)
        du = w[2:3, :] * dy + w[1:2, :] * _shift_up(dy, dyn, 1) + w[0:1, :] * _shift_up(dy, dyn, 2)
        dp_ref[:, :d] = (dz * y).astype(BF16)
        dp_ref[:, d:2 * d] = (du * h).astype(BF16)
        dp_ref[:, 2 * d:] = (du * c).astype(BF16)

        @pl.when(i == 0)
        def _():
            dw_ref[...] = jnp.zeros_like(dw_ref)

        dw_ref[0:1, :] += jnp.sum(dy * u2, axis=0, keepdims=True)
        dw_ref[1:2, :] += jnp.sum(dy * u1, axis=0, keepdims=True)
        dw_ref[2:3, :] += jnp.sum(dy * u, axis=0, keepdims=True)

    return pl.pallas_call(
        body, name=name, grid=(nt,),
        in_specs=[_row_spec(t, d3),
                  pl.BlockSpec((HALO, d3), lambda i: (jnp.maximum(i * hb - 1, 0), 0)),
                  pl.BlockSpec((HALO, d3), lambda i: (jnp.minimum((i + 1) * hb, last_halo), 0)),
                  _row_spec(t, d),
                  pl.BlockSpec((HALO, d), lambda i: (jnp.minimum((i + 1) * hb, last_halo), 0)),
                  pl.BlockSpec((3, d), lambda i: (0, 0))],
        out_specs=[_row_spec(t, d3), pl.BlockSpec((3, d), lambda i: (0, 0))],
        out_shape=[jax.ShapeDtypeStruct((s, d3), BF16), jax.ShapeDtypeStruct((3, d), F32)],
        compiler_params=_params(("arbitrary",)),
    )(p, p, p, dz, dz, cw)


FFN_ROWS, FFN_COLS = (1024, 512, 256), (1408, 768, 256, 128)


def _row_chunks(tm, rows=256):
    return [slice(r, r + min(rows, tm)) for r in range(0, tm, min(rows, tm))]


def _ffn_in_swiglu(xn, w_in, name):
    s, k = xn.shape
    ff = w_in.shape[1] // 2
    tm, tn = _pick(s, FFN_ROWS), _pick(ff, FFN_COLS)
    nj = ff // tn

    def body(x_ref, wg_ref, wu_ref, f_ref, a_ref):
        for rows in _row_chunks(tm):
            xv = x_ref[rows, :]
            gate = jnp.dot(xv, wg_ref[...], preferred_element_type=F32)
            up = jnp.dot(xv, wu_ref[...], preferred_element_type=F32)
            f_ref[0, rows, :] = gate.astype(BF16)
            f_ref[1, rows, :] = up.astype(BF16)
            a_ref[rows, :] = (gate * jax.nn.sigmoid(gate) * up).astype(BF16)

    return pl.pallas_call(
        body, name=name, grid=(nj, s // tm),
        in_specs=[pl.BlockSpec((tm, k), lambda j, i: (i, 0)),
                  pl.BlockSpec((k, tn), lambda j, i: (0, j)),
                  pl.BlockSpec((k, tn), lambda j, i: (0, nj + j))],
        out_specs=[pl.BlockSpec((2, tm, tn), lambda j, i: (0, i, j)), pl.BlockSpec((tm, tn), lambda j, i: (i, j))],
        out_shape=[jax.ShapeDtypeStruct((2, s, ff), BF16), jax.ShapeDtypeStruct((s, ff), BF16)],
        compiler_params=_params(("parallel", "parallel")),
    )(xn, w_in, w_in)


def _ffn_out_dx_swiglu(dff, w_out, f, name):
    s, d = dff.shape
    ff = w_out.shape[0]
    tm, tn = _pick(s, FFN_ROWS), _pick(ff, FFN_COLS)

    def body(d_ref, w_ref, f_ref, df_ref):
        for rows in _row_chunks(tm):
            da = lax.dot_general(d_ref[rows, :], w_ref[...], (((1,), (1,)), ((), ())), preferred_element_type=F32)
            gate = f_ref[0, rows, :].astype(F32)
            up = f_ref[1, rows, :].astype(F32)
            sg = jax.nn.sigmoid(gate)
            silu = gate * sg
            df_ref[0, rows, :] = (da * up * (sg + silu * (1.0 - sg))).astype(BF16)
            df_ref[1, rows, :] = (da * silu).astype(BF16)

    planes = pl.BlockSpec((2, tm, tn), lambda j, i: (0, i, j))
    return pl.pallas_call(
        body, name=name, grid=(ff // tn, s // tm),
        in_specs=[pl.BlockSpec((tm, d), lambda j, i: (i, 0)), pl.BlockSpec((tn, d), lambda j, i: (j, 0)), planes],
        out_specs=planes, out_shape=jax.ShapeDtypeStruct((2, s, ff), BF16),
        compiler_params=_params(("parallel", "parallel")),
    )(dff, w_out, f)


SUPER = 2048
RES = 16
PAIR = 128
L = 128
FWD_TOGETHER = 16
BWD_TOGETHER = 4


def _alibi_slopes(n_heads):
    h = np.arange(n_heads, dtype=np.float32) + 1.0
    return np.power(2.0, -8.0 * h / n_heads).astype(np.float32)


def _permute16(x, inverse, name):
    s, d = x.shape
    cw = LANE

    def body(x_ref, o_ref):
        if inverse:
            for m in range(L):
                o_ref[RES * m:RES * (m + 1), :] = x_ref[pl.ds(m, RES, stride=L), :]
        else:
            for r in range(RES):
                o_ref[L * r:L * (r + 1), :] = x_ref[pl.ds(r, L, stride=RES), :]

    spec = pl.BlockSpec((SUPER, cw), lambda i, j: (i, j))
    return pl.pallas_call(
        body, name=name, grid=(s // SUPER, d // cw), in_specs=[spec], out_specs=spec,
        out_shape=jax.ShapeDtypeStruct((s, d), x.dtype),
        compiler_params=_params(("parallel", "parallel")),
    )(x)


def _slope_table(d):
    nh = d // HEAD_DIM
    sl = _alibi_slopes(nh)
    tab = np.repeat(sl, HEAD_DIM).reshape(d // PAIR, 1, PAIR)
    return jnp.asarray(np.broadcast_to(tab, (d // PAIR, 8, PAIR)).copy())


def _geometry(dil):
    nch = RES // dil
    return nch, L // nch


def _band(dil):
    nch, w = _geometry(dil)
    sh = w.bit_length() - 1
    i = lax.broadcasted_iota(jnp.int32, (L, 2 * L), 0)
    j = lax.broadcasted_iota(jnp.int32, (L, 2 * L), 1)

    def pos(t):
        return jnp.bitwise_and(t, w - 1) * nch + jnp.right_shift(t, sh)

    delta = pos(i) + L - (pos(jnp.bitwise_and(j, L - 1)) + jnp.bitwise_and(j, L))
    return (delta * dil).astype(F32), (delta >= 0) & (delta <= L), j < L


def _fill_bias(bias_s, sl_ref):
    for b, dil in enumerate(DILATIONS):
        base, band, prev_half = _band(dil)
        for first in range(2):
            valid = band & jnp.logical_not(prev_half) if first else band
            for h in range(2):
                slope = sl_ref[0:1, HEAD_DIM * h:HEAD_DIM * h + 1]
                bias_s[(2 * b + first) * 2 + h] = jnp.where(valid, -slope * base, NEG_BIG)


def _bias_index(b, sb, n):
    first = jnp.logical_and(sb == 0, n == 0).astype(jnp.int32)
    return (2 * b + first) * 2


def _offsets(dil, res, n):
    nch, w = _geometry(dil)

    def al(v):
        return v if isinstance(v, int) else pl.multiple_of(v, w)

    q_off = [al((a * dil + res) * L + n * w) for a in range(nch)]
    k_off = [al((a * dil + res) * 2 * L + L + n * w) for a in range(nch)]
    kp_off = [al((a * dil + res) * 2 * L + L + n * w - w) for a in range(nch)]
    return q_off, k_off, kp_off, w


def _gather(ref, offs, w):
    parts = [ref[pl.ds(o, w), :] for o in offs]
    return parts[0] if len(parts) == 1 else jnp.concatenate(parts, axis=0)


def _scatter(ref, offs, w, val, add=False):
    for a, o in enumerate(offs):
        piece = val[a * w:(a + 1) * w, :]
        if add:
            ref[pl.ds(o, w), :] += piece
        else:
            ref[pl.ds(o, w), :] = piece


def _fill_key_buffer(buf, prev_ref, cur_ref):
    for r in range(RES):
        buf[2 * L * r:2 * L * r + L, :] = prev_ref[L * r:L * (r + 1), :]
        buf[2 * L * r + L:2 * L * (r + 1), :] = cur_ref[L * r:L * (r + 1), :]


def _two_heads(x, low):
    zero = jnp.zeros_like(x)
    return jnp.concatenate([jnp.where(low, x, zero), jnp.where(low, zero, x)], axis=0)


def _loop_blocks(dil, stages, together):
    together = max(together, dil) if dil < RES else together

    def it(i, c):
        if dil == RES:
            blocks = [(i * together + k, 0) for k in range(together)]
        else:
            blocks = [(res, i * (together // dil) + k) for k in range(together // dil) for res in range(dil)]
        state = [stages[0](res, n) for res, n in blocks]
        for stage in stages[1:]:
            state = [stage(res, n, prev) for (res, n), prev in zip(blocks, state)]
        for writes in state:
            for args in writes:
                _scatter(*args)
        return c

    lax.fori_loop(0, RES // together, it, 0)


NT = (((1,), (1,)), ((), ()))
TN = (((0,), (0,)), ((), ()))


def _attention_fwd(q, kv, name):
    s, d = q.shape
    g_n, ns = d // PAIR, s // SUPER

    def body(sl_ref, q_ref, kc_ref, kp_ref, vc_ref, vp_ref, o_ref, lse_ref, kbuf, vbuf, m_s, l_s, acc_s, bias_s):
        sb = pl.program_id(1)
        _fill_key_buffer(kbuf, kp_ref, kc_ref)
        _fill_key_buffer(vbuf, vp_ref, vc_ref)
        pl.when(sb == 0)(lambda: _fill_bias(bias_s, sl_ref))
        low = lax.broadcasted_iota(jnp.int32, (L, PAIR), 1) < HEAD_DIM
        low_k = lax.broadcasted_iota(jnp.int32, (2 * L, PAIR), 1) < HEAD_DIM
        ones_bd = _two_heads(jnp.ones((2 * L, PAIR), BF16), low_k)
        for bi, dil in enumerate(DILATIONS):
            first_branch, last_branch = bi == 0, bi == len(DILATIONS) - 1

            def scores(res, n, dil=dil):
                q_off, k_off, kp_off, w = _offsets(dil, res, n)
                qf = _gather(q_ref, q_off, w).astype(BF16)
                kcat = jnp.concatenate([_gather(kbuf, kp_off, w), _gather(kbuf, k_off, w)], axis=0).astype(BF16)
                return lax.dot_general(qf, _two_heads(kcat, low_k), NT, preferred_element_type=F32)

            def update(res, n, sc, bi=bi, dil=dil, first_branch=first_branch, last_branch=last_branch):
                q_off, k_off, kp_off, w = _offsets(dil, res, n)
                vcat = jnp.concatenate([_gather(vbuf, kp_off, w), _gather(vbuf, k_off, w)], axis=0).astype(BF16)
                v_ones = jnp.concatenate([_two_heads(vcat, low_k), ones_bd], axis=1)
                bias_at = _bias_index(bi, sb, n)
                if not first_branch:
                    m_prev = _gather(m_s, q_off, w)
                ps, m_new = [], []
                for h in range(2):
                    s_h = sc[:, 2 * L * h:2 * L * (h + 1)] + bias_s[bias_at + h]
                    mh = jnp.max(s_h, axis=1, keepdims=True)
                    if not first_branch:
                        mh = jnp.maximum(mh, m_prev[:, HEAD_DIM * h:HEAD_DIM * h + 1])
                    ps.append(jnp.exp(s_h - mh).astype(BF16))
                    m_new.append(mh)
                m_full = jnp.where(low, m_new[0], m_new[1])
                both = jnp.dot(jnp.concatenate(ps, axis=1), v_ones, preferred_element_type=F32)
                acc, l_full = both[:, :PAIR], both[:, PAIR:]
                if not first_branch:
                    alpha = jnp.exp(m_prev - m_full)
                    l_full = _gather(l_s, q_off, w) * alpha + l_full
                    acc = _gather(acc_s, q_off, w) * alpha + acc
                if last_branch:
                    return [(o_ref, q_off, w, acc / l_full, False), (lse_ref, q_off, w, m_full + jnp.log(l_full), False)]
                return [(m_s, q_off, w, m_full, False), (l_s, q_off, w, l_full, False), (acc_s, q_off, w, acc, False)]

            _loop_blocks(dil, [scores, update], FWD_TOGETHER)

    prev = lambda i: jnp.maximum(i - 1, 0)
    blk = pl.BlockSpec((SUPER, PAIR), lambda g, i: (i, g))
    in_specs = [pl.BlockSpec((None, 8, PAIR), lambda g, i: (g, 0, 0)), blk,
                pl.BlockSpec((SUPER, PAIR), lambda g, i: (i, g)),
                pl.BlockSpec((SUPER, PAIR), lambda g, i: (prev(i), g)),
                pl.BlockSpec((SUPER, PAIR), lambda g, i: (i, g_n + g)),
                pl.BlockSpec((SUPER, PAIR), lambda g, i: (prev(i), g_n + g))]
    return pl.pallas_call(
        body, name=name, grid=(g_n, ns), in_specs=in_specs, out_specs=[blk, blk],
        out_shape=[jax.ShapeDtypeStruct((s, d), F32)] * 2,
        scratch_shapes=([pltpu.VMEM((2 * SUPER, PAIR), F32)] * 2 + [pltpu.VMEM((SUPER, PAIR), F32)] * 3
                        + [pltpu.VMEM((4 * len(DILATIONS), L, 2 * L), F32)]),
        compiler_params=_params(("parallel", "arbitrary")),
    )(_slope_table(d), q, kv, kv, kv, kv)


def _attention_bwd(q, kv, o, do, lse, dkv_in, name):
    s, d = q.shape
    g_n, ns = d // PAIR, s // SUPER
    has_in = dkv_in is not None

    def body(*refs):
        sl_ref, q_ref, do_ref, o_ref, lse_ref, kc_ref, kp_ref, vc_ref, vp_ref = refs[:9]
        pos = 9
        if has_in:
            dkv_in_ref = refs[9]
            pos = 10
        dq_ref, dkv_ref, kbuf, vbuf, dkbuf, dvbuf, dq_s, bias_s = refs[pos:]
        step = pl.program_id(1)
        sb = ns - 1 - step
        _fill_key_buffer(kbuf, kp_ref, kc_ref)
        _fill_key_buffer(vbuf, vp_ref, vc_ref)

        @pl.when(step == 0)
        def _():
            _fill_bias(bias_s, sl_ref)
            dkbuf[...] = jnp.zeros_like(dkbuf)
            dvbuf[...] = jnp.zeros_like(dvbuf)

        @pl.when(step > 0)
        def _():
            for buf in (dkbuf, dvbuf):
                for r in range(RES):
                    buf[2 * L * r + L:2 * L * (r + 1), :] = buf[2 * L * r:2 * L * r + L, :]
                    buf[2 * L * r:2 * L * r + L, :] = jnp.zeros((L, PAIR), F32)

        low = lax.broadcasted_iota(jnp.int32, (L, PAIR), 1) < HEAD_DIM
        low_k = lax.broadcasted_iota(jnp.int32, (2 * L, PAIR), 1) < HEAD_DIM
        low_t = lax.broadcasted_iota(jnp.int32, (PAIR, 2 * L), 0) < HEAD_DIM
        for bi, dil in enumerate(DILATIONS):
            first_branch = bi == 0

            def scores(res, n, dil=dil):
                q_off, k_off, kp_off, w = _offsets(dil, res, n)
                qb = _gather(q_ref, q_off, w).astype(BF16)
                dof = _gather(do_ref, q_off, w)
                prod = dof * _gather(o_ref, q_off, w)
                dob = dof.astype(BF16)
                lse_f = _gather(lse_ref, q_off, w)
                zero = jnp.zeros_like(prod)
                dsum = (jnp.sum(jnp.where(low, prod, zero), axis=1, keepdims=True),
                        jnp.sum(jnp.where(low, zero, prod), axis=1, keepdims=True))
                kcat = jnp.concatenate([_gather(kbuf, kp_off, w), _gather(kbuf, k_off, w)], axis=0).astype(BF16)
                vcat = jnp.concatenate([_gather(vbuf, kp_off, w), _gather(vbuf, k_off, w)], axis=0).astype(BF16)
                k_bd, v_bd = _two_heads(kcat, low_k), _two_heads(vcat, low_k)
                sc = lax.dot_general(qb, k_bd, NT, preferred_element_type=F32)
                dp = lax.dot_general(dob, v_bd, NT, preferred_element_type=F32)
                return qb, dob, lse_f, dsum, k_bd, sc, dp

            def gradients(res, n, given, bi=bi, dil=dil, first_branch=first_branch):
                qb, dob, lse_f, dsum, k_bd, sc, dp = given
                q_off, k_off, kp_off, w = _offsets(dil, res, n)
                bias_at = _bias_index(bi, sb, n)
                ps, dss = [], []
                for h in range(2):
                    cols = slice(2 * L * h, 2 * L * (h + 1))
                    lse_h = lse_f[:, HEAD_DIM * h:HEAD_DIM * h + 1]
                    p_h = jnp.exp(sc[:, cols] + bias_s[bias_at + h] - lse_h)
                    dss.append((p_h * (dp[:, cols] - dsum[h])).astype(BF16))
                    ps.append(p_h.astype(BF16))
                ds_cat, p_cat = jnp.concatenate(dss, axis=1), jnp.concatenate(ps, axis=1)
                dq = jnp.dot(ds_cat, k_bd, preferred_element_type=F32)
                dk_t = lax.dot_general(qb, ds_cat, TN, preferred_element_type=F32)
                dv_t = lax.dot_general(dob, p_cat, TN, preferred_element_type=F32)
                dk = jnp.where(low_t, dk_t[:, :2 * L], dk_t[:, 2 * L:]).T
                dv = jnp.where(low_t, dv_t[:, :2 * L], dv_t[:, 2 * L:]).T
                return [(dq_s, q_off, w, dq, not first_branch),
                        (dkbuf, kp_off, w, dk[:L], True), (dkbuf, k_off, w, dk[L:], True),
                        (dvbuf, kp_off, w, dv[:L], True), (dvbuf, k_off, w, dv[L:], True)]

            _loop_blocks(dil, [scores, gradients], BWD_TOGETHER)

        dq_ref[...] = dq_s[...].astype(BF16)
        for r in range(RES):
            rows, cur = slice(L * r, L * (r + 1)), slice(2 * L * r + L, 2 * L * (r + 1))
            for plane, buf in enumerate((dkbuf, dvbuf)):
                if has_in:
                    dkv_ref[plane, rows, :] = buf[cur, :] + dkv_in_ref[plane, rows, :]
                else:
                    dkv_ref[plane, rows, :] = buf[cur, :]

    rev = lambda i: ns - 1 - i
    prev = lambda i: jnp.maximum(ns - 2 - i, 0)
    blk = pl.BlockSpec((SUPER, PAIR), lambda g, i: (rev(i), g))
    in_specs = [pl.BlockSpec((None, 8, PAIR), lambda g, i: (g, 0, 0)), blk, blk, blk, blk,
                pl.BlockSpec((SUPER, PAIR), lambda g, i: (rev(i), g)),
                pl.BlockSpec((SUPER, PAIR), lambda g, i: (prev(i), g)),
                pl.BlockSpec((SUPER, PAIR), lambda g, i: (rev(i), g_n + g)),
                pl.BlockSpec((SUPER, PAIR), lambda g, i: (prev(i), g_n + g))]
    ins = [_slope_table(d), q, do, o, lse, kv, kv, kv, kv]
    planes = pl.BlockSpec((2, SUPER, PAIR), lambda g, i: (0, rev(i), g))
    if has_in:
        in_specs.append(planes)
        ins.append(dkv_in)
    res = pl.pallas_call(
        body, name=name, grid=(g_n, ns), in_specs=in_specs, out_specs=[blk, planes],
        out_shape=[jax.ShapeDtypeStruct((s, d), BF16), jax.ShapeDtypeStruct((2, s, d), F32)],
        scratch_shapes=([pltpu.VMEM((2 * SUPER, PAIR), F32)] * 4 + [pltpu.VMEM((SUPER, PAIR), F32)]
                        + [pltpu.VMEM((4 * len(DILATIONS), L, 2 * L), F32)]),
        compiler_params=_params(("parallel", "arbitrary")),
    )(*ins)
    return res[0], res[1]


def _coords():
    return lax.axis_index("x"), lax.axis_index("y"), lax.axis_index("c")


def _chip_peers(x, y):
    return [(1 - x, y), (x, 1 - y), (1 - x, 1 - y)]


def _block_of(ref, axis, blk, size):
    start = pl.multiple_of(blk * size, size)
    if axis == 1:
        return ref.at[:, pl.ds(start, size)]
    return ref.at[pl.ds(start, size), :]


ANY = pl.BlockSpec(memory_space=pl.ANY)


HBM = pl.BlockSpec(memory_space=pltpu.HBM)
SEM = pl.BlockSpec(memory_space=pltpu.SEMAPHORE)
SPLIT = pltpu.CompilerParams(has_side_effects=pltpu.SideEffectType.DATAFLOW_SIDE_EFFECTING)


def _in_hbm(a):
    return pltpu.with_memory_space_constraint(a, pltpu.HBM)


def _thru(arrays):
    return [pltpu.HBM(a.shape, a.dtype) for a in arrays]


def _cast_place(w, layer, ax, dtype, after, name):
    _, k, n = w.shape
    t = _pick(k, (512, 256, 128))
    nb = k // t
    extra = [] if after is None else [after]

    def body(blk_ref, w_ref, *refs):
        b_ref, f_ref = refs[len(extra):]
        v = w_ref[...].astype(dtype)
        b_ref[...] = v
        f_ref[...] = v

    full_shape = (k, 4 * n) if ax == 1 else (4 * k, n)
    place = (lambda i, blk: (i, blk[0])) if ax == 1 else (lambda i, blk: (blk[0] * nb + i, 0))
    return pl.pallas_call(
        body, name=name,
        grid_spec=pltpu.PrefetchScalarGridSpec(
            num_scalar_prefetch=1, grid=(nb,),
            in_specs=[pl.BlockSpec((None, t, n), lambda i, blk: (layer, i, 0))] + [ANY] * len(extra),
            out_specs=[pl.BlockSpec((t, n), lambda i, blk: (i, 0)), pl.BlockSpec((t, n), place)]),
        out_shape=[jax.ShapeDtypeStruct((k, n), dtype), jax.ShapeDtypeStruct(full_shape, dtype)],
        compiler_params=_params(("parallel",)),
    )(_my_block()[None], _in_hbm(w), *extra)


def _my_block():
    return (2 * lax.axis_index("x") + lax.axis_index("y")).astype(jnp.int32)


def _gather_start(group, carry, name):
    n, nc = len(group), len(carry)

    def body(*refs):
        blocks, fulls, send_sem, recv_sem = refs[:n], refs[n:2 * n], refs[2 * n + nc], refs[2 * n + nc + 1]
        x, y, c = _coords()
        for t, (b, _, ax) in enumerate(group):
            mine = _block_of(fulls[t], ax, 2 * x + y, b.shape[ax])
            for j, (px, py) in enumerate(_chip_peers(x, y)):
                pltpu.make_async_remote_copy(
                    src_ref=blocks[t], dst_ref=mine, send_sem=send_sem.at[3 * t + j], recv_sem=recv_sem.at[3 * t + j],
                    device_id=(px, py, c), device_id_type=MESH).start()

    arrays = [b for b, _, _ in group] + [f for _, f, _ in group] + list(carry)
    sems = [pltpu.SemaphoreType.DMA((3 * n,))] * 2
    res = pl.pallas_call(
        body, name=name, in_specs=[HBM] * len(arrays), out_specs=[SEM, SEM] + [HBM] * len(arrays),
        out_shape=sems + _thru(arrays), input_output_aliases={i: 2 + i for i in range(len(arrays))},
        compiler_params=SPLIT,
    )(*[_in_hbm(a) for a in arrays])
    return (res[0], res[1], list(res[2:2 + n]), list(res[2 + n:2 + 2 * n])), list(res[2 + 2 * n:])


def _gather_wait(group, started, after, name):
    sends, recvs, blocks, fulls = started
    m = len(group)

    def body(*refs):
        blk_refs, full_refs, send_sem, recv_sem = refs[:m], refs[m:2 * m], refs[2 * m], refs[2 * m + 1]
        x, y, c = _coords()
        for t, (b, _, ax) in enumerate(group):
            for j, (px, py) in enumerate(_chip_peers(x, y)):
                cp = pltpu.make_async_remote_copy(
                    src_ref=blk_refs[t], dst_ref=_block_of(full_refs[t], ax, 2 * px + py, b.shape[ax]),
                    send_sem=send_sem.at[3 * t + j], recv_sem=recv_sem.at[3 * t + j],
                    device_id=(px, py, c), device_id_type=MESH)
                cp.wait_send()
                cp.wait_recv()

    extra = [] if after is None else [after]
    res = pl.pallas_call(
        body, name=name, in_specs=[HBM] * (2 * m) + [SEM, SEM] + [ANY] * len(extra), out_specs=[HBM] * (2 * m),
        out_shape=_thru(blocks) + _thru(fulls), input_output_aliases={i: i for i in range(2 * m)},
        compiler_params=SPLIT,
    )(*blocks, *fulls, sends, recvs, *extra)
    return list(res[m:])


def _scatter_start(grads, carry, name):
    n = len(grads)
    n_in = 2 * n + len(carry)

    def body(*refs):
        g_refs, st_refs, send_sem, recv_sem = refs[:n], refs[n:2 * n], refs[n_in], refs[n_in + 1]
        x, y, c = _coords()
        for t, (g, ax) in enumerate(grads):
            for j, (px, py) in enumerate(_chip_peers(x, y)):
                pltpu.make_async_remote_copy(
                    src_ref=_block_of(g_refs[t], ax, 2 * px + py, g.shape[ax] // 4), dst_ref=st_refs[t].at[j],
                    send_sem=send_sem.at[3 * t + j], recv_sem=recv_sem.at[3 * t + j],
                    device_id=(px, py, c), device_id_type=MESH).start()

    arrays = [g for g, _ in grads]
    for g, ax in grads:
        shape = list(g.shape)
        shape[ax] //= 4
        arrays.append(lax.empty((3, *shape), g.dtype))
    arrays += list(carry)
    sems = [pltpu.SemaphoreType.DMA((3 * n,))] * 2
    res = pl.pallas_call(
        body, name=name, in_specs=[HBM] * n_in, out_specs=[SEM, SEM] + [HBM] * n_in,
        out_shape=sems + _thru(arrays), input_output_aliases={i: 2 + i for i in range(n_in)},
        compiler_params=SPLIT,
    )(*[_in_hbm(a) for a in arrays])
    return (res[0], res[1], list(res[2:2 + n]), list(res[2 + n:2 + 2 * n])), list(res[2 + 2 * n:])


def _scatter_wait(axes, started, after, name):
    sends, recvs, full, stacks = started
    n = len(full)
    extra = [] if after is None else [after]

    def body(*refs):
        g_refs, st_refs, send_sem, recv_sem = refs[:n], refs[n:2 * n], refs[2 * n], refs[2 * n + 1]
        x, y, c = _coords()
        for t, ax in enumerate(axes):
            size = full[t].shape[ax] // 4
            for j, (px, py) in enumerate(_chip_peers(x, y)):
                cp = pltpu.make_async_remote_copy(
                    src_ref=_block_of(g_refs[t], ax, 2 * px + py, size), dst_ref=st_refs[t].at[j],
                    send_sem=send_sem.at[3 * t + j], recv_sem=recv_sem.at[3 * t + j],
                    device_id=(px, py, c), device_id_type=MESH)
                cp.wait_send()
                cp.wait_recv()

    res = pl.pallas_call(
        body, name=name, in_specs=[HBM] * (2 * n) + [SEM, SEM] + [ANY] * len(extra), out_specs=[HBM] * (2 * n),
        out_shape=_thru(full) + _thru(stacks), input_output_aliases={i: i for i in range(2 * n)},
        compiler_params=SPLIT,
    )(*full, *stacks, sends, recvs, *extra)
    return list(res[:n]), list(res[n:])


def _pair_copies(g_refs, st_refs, out_refs, items, send_sem, recv_sem):
    x, y, c = _coords()
    copies = []
    for u, (g, ax, _) in enumerate(items):
        own = _block_of(g_refs[u], ax, 2 * x + y, g.shape[ax] // 4)
        for k, (src, dst) in enumerate([(own, out_refs[u].at[0]), (st_refs[u], out_refs[u].at[pl.ds(1, 3)])]):
            copies.append(pltpu.make_async_remote_copy(
                src_ref=src, dst_ref=dst, send_sem=send_sem.at[2 * u + k], recv_sem=recv_sem.at[2 * u + k],
                device_id=(x, y, 1 - c), device_id_type=MESH))
    return copies


def _pair_start(items, carry, name):
    n = len(items)
    n_in = 3 * n + len(carry)

    def body(*refs):
        for cp in _pair_copies(refs[:n], refs[n:2 * n], refs[2 * n:3 * n], items, refs[n_in], refs[n_in + 1]):
            cp.start()

    arrays = ([g for g, _, _ in items] + [st for _, _, st in items]
              + [lax.empty((4, *st.shape[1:]), st.dtype) for _, _, st in items] + list(carry))
    sems = [pltpu.SemaphoreType.DMA((2 * n,))] * 2
    res = pl.pallas_call(
        body, name=name, in_specs=[HBM] * n_in, out_specs=[SEM, SEM] + [HBM] * n_in,
        out_shape=sems + _thru(arrays), input_output_aliases={i: 2 + i for i in range(n_in)},
        compiler_params=SPLIT,
    )(*[_in_hbm(a) for a in arrays])
    thru = res[2:]
    return (res[0], res[1], *(list(thru[k * n:(k + 1) * n]) for k in range(3))), list(thru[3 * n:])


def _pair_wait(axes, started, name):
    send, recv, full, stacks, landing = started
    n = len(full)
    items = [(full[u], axes[u], stacks[u]) for u in range(n)]

    def body(*refs):
        for cp in _pair_copies(refs[:n], refs[n:2 * n], refs[2 * n:3 * n], items, refs[3 * n], refs[3 * n + 1]):
            cp.wait_send()
            cp.wait_recv()

    res = pl.pallas_call(
        body, name=name, in_specs=[HBM] * (3 * n) + [SEM, SEM], out_specs=[HBM] * (3 * n),
        out_shape=_thru(full + stacks + landing), input_output_aliases={i: i for i in range(3 * n)},
        compiler_params=SPLIT,
    )(*full, *stacks, *landing, send, recv)
    return list(res[:n]), list(res[n:2 * n]), list(res[2 * n:])


def _allreduce_small(v, name):
    r, cdim = v.shape

    def body(v_ref, out_ref, buf, send_sems, recv_sems):
        x, y, c = _coords()
        me = 4 * x + 2 * y + c
        buf[0] = v_ref[...]
        sends = []
        for k in range(1, 8):
            peer = (x if not (k & 4) else 1 - x, y if not (k & 2) else 1 - y, c if not (k & 1) else 1 - c)
            cp = pltpu.make_async_remote_copy(
                src_ref=v_ref, dst_ref=buf.at[k], send_sem=send_sems.at[k - 1], recv_sem=recv_sems.at[k - 1],
                device_id=peer, device_id_type=MESH)
            cp.start()
            sends.append(cp)
        for cp in sends:
            cp.wait_recv()
        total = buf[me]
        for src in range(1, 8):
            total = total + buf[jnp.bitwise_xor(me, src)]
        out_ref[...] = total
        for cp in sends:
            cp.wait_send()

    return pl.pallas_call(
        body, name=name,
        in_specs=[pl.BlockSpec(memory_space=pltpu.VMEM)], out_specs=pl.BlockSpec(memory_space=pltpu.VMEM),
        out_shape=jax.ShapeDtypeStruct((r, cdim), F32),
        scratch_shapes=[pltpu.VMEM((8, r, cdim), F32), pltpu.SemaphoreType.DMA((7,)), pltpu.SemaphoreType.DMA((7,))],
        compiler_params=pltpu.CompilerParams(has_side_effects=True),
    )(v)


def _adamw_math(w, g, m, v):
    m = ADAM_B1 * m + (1.0 - ADAM_B1) * g
    v = ADAM_B2 * v + (1.0 - ADAM_B2) * jnp.square(g)
    m_hat = m / (1.0 - ADAM_B1 ** ADAM_STEP)
    v_hat = v / (1.0 - ADAM_B2 ** ADAM_STEP)
    delta = -ADAM_LR * (m_hat / (jnp.sqrt(v_hat) + ADAM_EPS) + ADAM_WD * w)
    return delta, m, v


def _adamw(w, m, v, grads, name):
    r, cdim = w.shape
    paired = isinstance(grads, list)
    layers = len(grads) if paired else 1
    t = _pick(r // layers, (128, 64, 32, 16, 8))
    per_layer = r // layers // t
    n_grad = 3 * layers if paired else 1

    def body(*refs):
        refs = refs[1:] if paired else refs
        w_ref, m_ref, v_ref = refs[:3]
        outs = refs[3 + n_grad:]

        def update(g):
            delta, m_new, v_new = _adamw_math(w_ref[...], g, m_ref[...], v_ref[...])
            outs[0][...] = g
            outs[1][...] = delta
            outs[2][...] = m_new
            outs[3][...] = v_new

        if not paired:
            update(refs[3][...])
            return
        layer = pl.program_id(0) // per_layer
        for l in range(layers):
            @pl.when(layer == l)
            def _(own_ref=refs[3 + 3 * l], st_ref=refs[4 + 3 * l], sib_ref=refs[5 + 3 * l]):
                sa = own_ref[...].astype(F32)
                sb = sib_ref[0].astype(F32)
                for k in range(3):
                    sa = sa + st_ref[k].astype(F32)
                    sb = sb + sib_ref[k + 1].astype(F32)
                update(sa + sb)

    out_shape = [jax.ShapeDtypeStruct((r, cdim), F32)] * 4
    if not paired:
        spec = pl.BlockSpec((t, cdim), lambda i: (i, 0))
        return pl.pallas_call(
            body, name=name, grid=(r // t,), in_specs=[spec] * 4, out_specs=[spec] * 4, out_shape=out_shape,
            compiler_params=_params(("parallel",)),
        )(w, m, v, grads)

    spec = pl.BlockSpec((t, cdim), lambda i, blk: (i, 0))
    ins, in_specs = [_in_hbm(a) for a in (w, m, v)], [spec] * 3
    for l, (g, ax, stack, sib) in enumerate(grads):
        row = lambda i, l=l: jnp.clip(i - l * per_layer, 0, per_layer - 1)
        own = ((lambda i, blk, row=row: (row(i), blk[0])) if ax == 1
               else (lambda i, blk, row=row: (blk[0] * per_layer + row(i), 0)))
        ins += [g, stack, sib]
        in_specs += [pl.BlockSpec((t, cdim), own),
                     pl.BlockSpec((3, t, cdim), lambda i, blk, row=row: (0, row(i), 0)),
                     pl.BlockSpec((4, t, cdim), lambda i, blk, row=row: (0, row(i), 0))]
    return pl.pallas_call(
        body, name=name,
        grid_spec=pltpu.PrefetchScalarGridSpec(
            num_scalar_prefetch=1, grid=(r // t,), in_specs=in_specs, out_specs=[spec] * 4),
        out_shape=out_shape, compiler_params=_params(("parallel",)),
    )(_my_block()[None], *ins)


def _local_step(x, target, gains, conv_ws, kv_gain, weights_of, send_grads):
    depth = len(gains)
    n_a = len(conv_ws)
    saved, ws = [], []
    kv = kvn = None
    _, (xn,) = _norm_res_fwd(x, None, None, [gains[0][0]], "norm_first")
    h = x
    for l in range(depth):
        g = gains[l]
        sv = {"x_in": h, "xn": xn}
        w = weights_of(l, "mix", h)
        ws.append(w)
        if l == n_a:
            kv = _mm(kvn, w["kv"], "nn", F32, "kv_fwd")
        if l < n_a:
            p = _mm(xn, w["conv_in"], "nn", BF16, f"conv_in_fwd_{l}")
            z = _conv_gate_fwd(p, conv_ws[l], f"conv_gate_fwd_{l}")
            mix, x1, xn2 = _mm_norm_res(z, w["conv_out"], h, g[1], g[2], f"conv_out_fwd_{l}")
            sv.update(p=p, z=z)
        else:
            j = l - n_a
            q = _mm(xn, w["q"], "nn", F32, f"q_fwd_{j}", scale=HEAD_DIM ** -0.5)
            o, lse = _attention_fwd(q, kv, f"attn_fwd_{j}")
            mix, x1, xn2 = _mm_norm_res(o, w["o"], h, g[1], g[2], f"o_fwd_{j}")
            sv.update(q=q, o=o, lse=lse)
        w.update(weights_of(l, "ffn", mix))
        f, a = _ffn_in_swiglu(xn2, w["ffn_in"], f"ffn_in_fwd_{l}")
        ff = _mm(a, w["ffn_out"], "nn", BF16, f"ffn_out_fwd_{l}")
        sv.update(mix=mix, x1=x1, xn2=xn2, f=f, a=a, ff=ff)
        saved.append(sv)
        if l == depth - 1:
            dx, loss = _norm_res_loss(x1, ff, g[3], target, "norm_loss")
        elif l == n_a - 1:
            h, _ = _norm_res_fwd(x1, ff, g[3], [], f"norm_end_{l}")
            h = _permute16(h, False, "permute_stream")
            _, (xn, kvn) = _norm_res_fwd(h, None, None, [gains[l + 1][0], kv_gain], "norm_permuted")
        else:
            h, (xn,) = _norm_res_fwd(x1, ff, g[3], [gains[l + 1][0]], f"norm_end_{l}")
    d_gains = [[None] * 4 for _ in range(depth)]
    d_conv = [None] * n_a
    d_kv_gain = None
    dkv = None
    _, _, dff, d_gains[depth - 1][3] = _norm_bwd(dx, [], None, (saved[-1]["ff"], gains[-1][3]), "norm_loss_bwd")
    for l in reversed(range(depth)):
        sv, g, w, grads = saved[l], gains[l], ws[l], {}
        grads["ffn_out"] =_mm(sv["a"], dff, "tn", BF16, f"ffn_out_dw_{l}")
        df = _ffn_out_dx_swiglu(dff, w["ffn_out"], sv["f"], f"ffn_out_dx_{l}")
        dxn2 = _mm(df, w["ffn_in"], "nt", BF16, f"ffn_in_dx_{l}")
        grads["ffn_in"] =_mm(sv["xn2"], df, "tn", BF16, f"ffn_in_dw_{l}")
        dx, (d_gains[l][2],), dmix, d_gains[l][1] = _norm_bwd(
            dx, [(dxn2, g[2])], sv["x1"], (sv["mix"], g[1]), f"norm_mid_bwd_{l}")
        dx, dmix = send_grads(l, "ffn", grads, [dx, dmix])
        if l < n_a:
            dz = _mm(dmix, w["conv_out"], "nt", BF16, f"conv_out_dx_{l}")
            grads["conv_out"] =_mm(sv["z"], dmix, "tn", BF16, f"conv_out_dw_{l}")
            dp, d_conv[l] = _conv_gate_bwd(sv["p"], dz, conv_ws[l], f"conv_gate_bwd_{l}")
            dxn = _mm(dp, w["conv_in"], "nt", BF16, f"conv_in_dx_{l}")
            grads["conv_in"] =_mm(sv["xn"], dp, "tn", BF16, f"conv_in_dw_{l}")
        else:
            j = l - n_a
            do = _mm(dmix, w["o"], "nt", F32, f"o_dx_{j}")
            grads["o"] =_mm(sv["o"], dmix, "tn", BF16, f"o_dw_{j}")
            dq, dkv = _attention_bwd(sv["q"], kv, sv["o"], do, sv["lse"], dkv, f"attn_bwd_{j}")
            scale = HEAD_DIM ** -0.5
            dxn = _mm(dq, w["q"], "nt", BF16, f"q_dx_{j}", scale=scale)
            grads["q"] =_mm(sv["xn"], dq, "tn", BF16, f"q_dw_{j}", scale=scale)
        branches = [(dxn, g[0])]
        if l == n_a:
            dkvn = _mm(dkv, w["kv"], "nt", BF16, "kv_dx")
            grads["kv"] =_mm(kvn, dkv, "tn", BF16, "kv_dw")
            branches.append((dkvn, kv_gain))
        post = (saved[l - 1]["ff"], gains[l - 1][3]) if l > 0 else None
        if l == n_a:
            dx, dgs, _, _ = _norm_bwd(dx, branches, sv["x_in"], None, f"norm_end_bwd_{l}")
            dx = _permute16(dx, True, "unpermute_stream")
            _, _, dff, dg_post = _norm_bwd(dx, [], None, post, "norm_boundary_bwd")
        else:
            dx, dgs, dff, dg_post = _norm_bwd(dx, branches, sv["x_in"], post, f"norm_end_bwd_{l}")
        if dff is None:
            send_grads(l, "mix", grads, [])
        else:
            dx, dff = send_grads(l, "mix", grads, [dx, dff])
        d_gains[l][0] = dgs[0]
        if l == n_a:
            d_kv_gain = dgs[1]
        if l > 0:
            d_gains[l - 1][3] = dg_post
    return loss, dx, d_gains, d_conv, d_kv_gain


BIG = (
    ("conv_in", 1), ("conv_out", 0), ("kv", 1), ("q", 0), ("o", 0), ("ffn_in", 1), ("ffn_out", 0))


def kernel(x, norm_g, conv_in_w, conv_w, conv_out_w, kv_norm_g, kv_w, q_w, o_w, ffn_in_w, ffn_out_w, loss_target, m_norm_g, m_conv_in_w, m_conv_w, m_conv_out_w, m_kv_norm_g, m_kv_w, m_q_w, m_o_w, m_ffn_in_w, m_ffn_out_w, v_norm_g, v_conv_in_w, v_conv_w, v_conv_out_w, v_kv_norm_g, v_kv_w, v_q_w, v_o_w, v_ffn_in_w, v_ffn_out_w):
    depth, _, dq = norm_g.shape
    d = 4 * dq
    n_a = conv_w.shape[0]
    big_w = {"conv_in": conv_in_w, "conv_out": conv_out_w, "kv": kv_w[None], "q": q_w, "o": o_w,
             "ffn_in": ffn_in_w, "ffn_out": ffn_out_w}
    big_m = {"conv_in": m_conv_in_w, "conv_out": m_conv_out_w, "kv": m_kv_w[None], "q": m_q_w, "o": m_o_w,
             "ffn_in": m_ffn_in_w, "ffn_out": m_ffn_out_w}
    big_v = {"conv_in": v_conv_in_w, "conv_out": v_conv_out_w, "kv": v_kv_w[None], "q": v_q_w, "o": v_o_w,
             "ffn_in": v_ffn_in_w, "ffn_out": v_ffn_out_w}

    n_gain, n_tap = depth * 4, n_a * conv_w.shape[1]
    small_rows = -(-(n_gain + n_tap + 1) // 8) * 8
    pad_rows = small_rows - n_gain - n_tap

    def pack_small(gains, taps):
        return jnp.concatenate([gains.reshape(n_gain, dq), taps.reshape(n_tap, dq), jnp.zeros((pad_rows, dq), F32)])

    axis_of = dict(BIG)

    def matrices_of(l, part):
        if part == "ffn":
            return [("ffn_in", l), ("ffn_out", l)]
        if l < n_a:
            return [("conv_in", l), ("conv_out", l)]
        return ([("kv", 0)] if l == n_a else []) + [("q", l - n_a), ("o", l - n_a)]

    halves = [(l, part) for l in range(depth) for part in ("mix", "ffn")]
    def placed(half, after):
        return [(*_cast_place(big_w[name], i, axis_of[name], BF16, after, f"place_{name}_{i}"), axis_of[name])
                for name, i in matrices_of(*half)]

    groups = {halves[0]: placed(halves[0], None)}
    groups[halves[0]].append((*_cast_place(pack_small(norm_g, conv_w)[None], 0, 1, F32, None, "place_small"), 1))
    started = {halves[0]: _gather_start(groups[halves[0]], [], "gather_start_0_mix")[0]}
    for half in halves[1:]:
        groups[half] = placed(half, started[halves[0]][2][0])

    AHEAD = 2

    def fetch(half, after):
        full = _gather_wait(groups[half], started[half], after, "gather_wait_%d_%s" % half)
        nxt = halves.index(half) + AHEAD
        if nxt < len(halves):
            started[halves[nxt]], full = _gather_start(groups[halves[nxt]], full, "gather_start_%d_%s" % halves[nxt])
        return full

    target = _permute16(loss_target.reshape(x.shape[1:]), False, "permute_target")
    for half in halves[1:AHEAD]:
        started[half], (target,) = _gather_start(groups[half], [target], "gather_start_%d_%s" % half)
    first = fetch(halves[0], target)
    small = first[-1]
    gains = [[small[4 * l + i][None] for i in range(4)] for l in range(depth)]
    conv_ws = [small[n_gain + 3 * l:n_gain + 3 * l + 3] for l in range(n_a)]
    kv_gain = kv_norm_g[None]

    def weights_of(l, part, after):
        full = first if (l, part) == halves[0] else fetch((l, part), after)
        return {name: full[t] for t, (name, _) in enumerate(matrices_of(l, part))}

    sent, paired = {}, {}
    LAG = 2

    def to_sibling(half, carry, after):
        axes = [axis_of[name] for name, _ in matrices_of(*half)]
        full, stacks = _scatter_wait(axes, sent[half], after, "scatter_wait_%d_%s" % half)
        paired[half], carry = _pair_start(list(zip(full, axes, stacks)), carry, "pair_start_%d_%s" % half)
        return carry

    def send_grads(l, part, grads, carry):
        sent[l, part], carry = _scatter_start(
            [(grads[name], axis_of[name]) for name, _ in matrices_of(l, part)], carry, f"scatter_start_{l}_{part}")
        at = halves.index((l, part))
        if carry:
            if at + LAG < len(halves):
                carry = to_sibling(halves[at + LAG], carry, carry[0])
        else:
            for older in range(at + LAG, at, -1):
                if halves[older] not in paired:
                    to_sibling(halves[older], [], sent[l, part][2][0])
        if at == 1 and carry:
            carry = to_sibling(halves[at + 1], carry, carry[0])
        return carry

    loss, dx, d_gains, d_conv, d_kv_gain = _local_step(
        x.reshape(x.shape[1:]), target, gains, conv_ws, kv_gain, weights_of, send_grads)
    loss = lax.psum(loss, ("x", "y", "c"))

    small_g = jnp.concatenate([dg for row in d_gains for dg in row] + list(d_conv) + [d_kv_gain]
                              + [jnp.zeros((pad_rows - 1, d), F32)])
    small_g = _allreduce_small(small_g, "allreduce_small")
    blk = 2 * lax.axis_index("x") + lax.axis_index("y")
    mine_small = lax.dynamic_slice_in_dim(small_g, blk * dq, dq, axis=1)
    kv_rows = d // dq

    def pack_opt(gains_like, taps_like, kv_like):
        rows = jnp.concatenate([gains_like.reshape(n_gain, dq), taps_like.reshape(n_tap, dq), kv_like.reshape(kv_rows, dq)])
        extra = -rows.shape[0] % 8
        return jnp.concatenate([rows, jnp.zeros((extra, dq), F32)]) if extra else rows

    sw = pack_opt(norm_g, conv_w, kv_norm_g)
    sm = pack_opt(m_norm_g, m_conv_w, m_kv_norm_g)
    sv = pack_opt(v_norm_g, v_conv_w, v_kv_norm_g)
    sg = pack_opt(mine_small[:n_gain], mine_small[n_gain:n_gain + n_tap], small_g[n_gain + n_tap])
    s_out = _adamw(sw, sm, sv, sg, "adamw_small")

    def unpack(a):
        return (a[:n_gain].reshape(depth, 4, dq), a[n_gain:n_gain + n_tap].reshape(n_a, -1, dq),
                a[n_gain + n_tap:n_gain + n_tap + kv_rows].reshape(d))

    small_out = [unpack(a) for a in s_out]

    landed, big_out = {}, {}

    def update(name):
        for half in halves:
            if matrices_of(*half)[0] not in landed and any(n == name for n, _ in matrices_of(*half)):
                axes = [axis_of[n] for n, _ in matrices_of(*half)]
                landed.update(zip(matrices_of(*half), zip(*_pair_wait(axes, paired[half], "pair_wait_%d_%s" % half))))
        shp, ax = big_w[name].shape, axis_of[name]
        rows, cols = shp[0] * shp[1], shp[2]
        flat = lambda a: a.reshape(rows, cols)
        full, stacks, sibling = zip(*[landed[name, i] for i in range(shp[0])])
        res = _adamw(flat(big_w[name]), flat(big_m[name]), flat(big_v[name]),
                     [(full[i], ax, stacks[i], sibling[i]) for i in range(shp[0])], f"adamw_{name}")
        big_out[name] = [a.reshape(shp[1:] if name == "kv" else shp) for a in res]

    pending = [half for half in reversed(halves) if half not in paired]
    for half in pending[:-1]:
        to_sibling(half, [], None)
    late = [name for name, _ in BIG if any(n == name for n, _ in matrices_of(*pending[-1]))]
    for name, _ in BIG:
        if name not in late:
            update(name)
    to_sibling(pending[-1], [], big_out["ffn_out"][0])
    for name in late:
        update(name)

    def leaves(i):
        ng, cw_, kg = small_out[i]
        return [ng, big_out["conv_in"][i], cw_, big_out["conv_out"][i], kg, big_out["kv"][i], big_out["q"][i],
                big_out["o"][i], big_out["ffn_in"][i], big_out["ffn_out"][i]]

    return (loss, dx.reshape(x.shape), *leaves(0), *leaves(1), *leaves(2), *leaves(3))
```

```python
import jax
import jax.numpy as jnp
import numpy as np
from jax import lax
from jax.experimental import pallas as pl
from jax.experimental.pallas import tpu as pltpu

F32 = jnp.float32
BF16 = jnp.bfloat16
HEAD_DIM = 64
DILATIONS = (1, 4, 16)
NORM_EPS = 1e-6
NEG_BIG = -1e30
VMEM_LIMIT = 48 * 1024 * 1024
ROW_TILE = 256
NORM_TILE = 512
LANE = 128
MESH = pl.DeviceIdType.MESH

ADAM_LR = 0.001
ADAM_B1 = 0.9
ADAM_B2 = 0.999
ADAM_EPS = 1e-08
ADAM_WD = 0.01
ADAM_STEP = 10

TILE_CANDIDATES = (1024, 1408, 768, 512, 384, 256, 128)


def _pick(dim, cands=TILE_CANDIDATES):
    for c in cands:
        if c <= dim and dim % c == 0:
            return c
    return dim


def _params(sem):
    return pltpu.CompilerParams(dimension_semantics=sem, vmem_limit_bytes=VMEM_LIMIT)


def _mm(a, b, mode, out_dtype, name, scale=None):
    a_planes = a.shape[0] if a.ndim == 3 else 1
    b_planes = b.shape[0] if b.ndim == 3 else 1
    if mode == "nn":
        m, k = a.shape[-2], a.shape[-1] * a_planes
        n = b.shape[1]
    elif mode == "nt":
        m, k = a.shape[-2], a.shape[-1] * a_planes
        n = b.shape[0]
    else:
        k, m = a.shape
        n = b.shape[-1] * b_planes
    tm, tn = _pick(m), _pick(n // b_planes)
    tk = _pick(k // a_planes, ((2048,) if mode == "tn" else (3072, 2816)) + TILE_CANDIDATES)
    nk = k // tk
    ka, nb = k // a_planes // tk, n // b_planes // tn
    if a_planes > 1:
        a_spec = pl.BlockSpec((None, tm, tk), lambda i, j, kk: (kk // ka, i, kk % ka))
    elif mode == "tn":
        a_spec = pl.BlockSpec((tk, tm), lambda i, j, kk: (kk, i))
    else:
        a_spec = pl.BlockSpec((tm, tk), lambda i, j, kk: (i, kk))
    if mode == "nn":
        b_spec = pl.BlockSpec((tk, tn), lambda i, j, kk: (kk, j))
        dims = (((1,), (0,)), ((), ()))
    elif mode == "nt":
        b_spec = pl.BlockSpec((tn, tk), lambda i, j, kk: (j, kk))
        dims = (((1,), (1,)), ((), ()))
    else:
        b_spec = (pl.BlockSpec((None, tk, tn), lambda i, j, kk: (j // nb, kk, j % nb)) if b_planes > 1
                  else pl.BlockSpec((tk, tn), lambda i, j, kk: (kk, j)))
        dims = (((0,), (0,)), ((), ()))

    def finish(acc):
        if scale is not None:
            acc = acc * scale
        return acc.astype(out_dtype)

    if nk == 1:
        def body(a_ref, b_ref, o_ref):
            o_ref[...] = finish(lax.dot_general(a_ref[...].astype(BF16), b_ref[...].astype(BF16), dims, preferred_element_type=F32))
        scratch = []
    else:
        def body(a_ref, b_ref, o_ref, acc_ref):
            kk = pl.program_id(2)

            @pl.when(kk == 0)
            def _():
                acc_ref[...] = jnp.zeros_like(acc_ref)

            acc_ref[...] += lax.dot_general(a_ref[...].astype(BF16), b_ref[...].astype(BF16), dims, preferred_element_type=F32)

            @pl.when(kk == nk - 1)
            def _():
                o_ref[...] = finish(acc_ref[...])
        scratch = [pltpu.VMEM((tm, tn), F32)]

    return pl.pallas_call(
        body, name=name,
        grid=(m // tm, n // tn, nk),
        in_specs=[a_spec, b_spec],
        out_specs=pl.BlockSpec((tm, tn), lambda i, j, kk: (i, j)),
        out_shape=jax.ShapeDtypeStruct((m, n), out_dtype),
        scratch_shapes=scratch,
        compiler_params=_params(("parallel", "parallel", "arbitrary")),
    )(a, b)


def _rstd(v):
    return lax.rsqrt(jnp.mean(v * v, axis=-1, keepdims=True) + NORM_EPS)


def _rms_bwd(dy, v, g, r):
    xhat = v * r
    gy = dy * g
    dv = r * (gy - xhat * jnp.mean(gy * xhat, axis=-1, keepdims=True))
    return dv, dy * xhat


def _row_spec(t, width):
    return pl.BlockSpec((t, width), lambda i: (i, 0))


GAIN_ROWS = 8


def _gain_in(g):
    arr, row = g[0], g[1]
    count = g[2] if len(g) > 2 else 1
    rows = min(GAIN_ROWS, arr.shape[0])
    blk, first = divmod(row, rows)
    assert first + count <= rows, "the rows of one gain or tap set must lie in one block"
    return arr, pl.BlockSpec((rows, arr.shape[1]), lambda *_: (blk, 0)), slice(first, first + count)


def _norm_res_fwd(x, mix, g_post, pre_gains, name):
    s, d = x.shape
    t = _pick(s, (NORM_TILE, ROW_TILE))
    has_mix = mix is not None
    n_pre = len(pre_gains)
    post = _gain_in(g_post) if has_mix else None
    pre = [_gain_in(g) for g in pre_gains]

    def body(*refs):
        x_ref = refs[0]
        pos = 1
        x1 = x_ref[...]
        if has_mix:
            mv = refs[1][...].astype(F32)
            x1 = x1 + mv * _rstd(mv) * refs[2][post[2], :]
            pos = 3
        gains = refs[pos:pos + n_pre]
        outs = refs[pos + n_pre:]
        if has_mix:
            outs[0][...] = x1
            outs = outs[1:]
        r = _rstd(x1)
        for g_ref, o_ref, (_, _, rows) in zip(gains, outs, pre):
            o_ref[...] = (x1 * r * g_ref[rows, :]).astype(BF16)

    ins = [x] + ([mix, post[0]] if has_mix else []) + [p[0] for p in pre]
    in_specs = [_row_spec(t, d)] + ([_row_spec(t, d), post[1]] if has_mix else []) + [p[1] for p in pre]
    out_shape = ([jax.ShapeDtypeStruct((s, d), F32)] if has_mix else []) + [jax.ShapeDtypeStruct((s, d), BF16)] * n_pre
    out_specs = [_row_spec(t, d)] * len(out_shape)
    res = pl.pallas_call(
        body, name=name, grid=(s // t,), in_specs=in_specs, out_specs=out_specs, out_shape=out_shape,
        compiler_params=_params(("parallel",)),
    )(*ins)
    if has_mix:
        return res[0], list(res[1:])
    return x, list(res)


def _mm_norm_res(a, b, x, g_post, pre_gain, name):
    m, k = a.shape
    d = b.shape[1]
    tm, tk = _pick(m, (1024, 512, 256)), _pick(k)
    nk = k // tk
    post, pre = _gain_in(g_post), _gain_in(pre_gain)

    def body(a_ref, b_ref, x_ref, gp_ref, g_ref, mix_ref, x1_ref, xn_ref, acc_ref):
        kk = pl.program_id(1)

        @pl.when(kk == 0)
        def _():
            acc_ref[...] = jnp.zeros_like(acc_ref)

        acc_ref[...] += jnp.dot(a_ref[...].astype(BF16), b_ref[...], preferred_element_type=F32)

        @pl.when(kk == nk - 1)
        def _():
            mix = acc_ref[...].astype(BF16)
            mix_ref[...] = mix
            mv = mix.astype(F32)
            x1 = x_ref[...] + mv * _rstd(mv) * gp_ref[post[2], :]
            x1_ref[...] = x1
            xn_ref[...] = (x1 * _rstd(x1) * g_ref[pre[2], :]).astype(BF16)

    rows = pl.BlockSpec((tm, d), lambda i, kk: (i, 0))
    return pl.pallas_call(
        body, name=name, grid=(m // tm, nk),
        in_specs=[pl.BlockSpec((tm, tk), lambda i, kk: (i, kk)), pl.BlockSpec((tk, d), lambda i, kk: (kk, 0)),
                  rows, post[1], pre[1]],
        out_specs=[rows, rows, rows],
        out_shape=[jax.ShapeDtypeStruct((m, d), BF16), jax.ShapeDtypeStruct((m, d), F32),
                   jax.ShapeDtypeStruct((m, d), BF16)],
        scratch_shapes=[pltpu.VMEM((tm, d), F32)],
        compiler_params=_params(("parallel", "arbitrary")),
    )(a, b, x, post[0], pre[0])


def _norm_res_loss(x, mix, g_post, target, name):
    s, d = x.shape
    t = _pick(s, (NORM_TILE, ROW_TILE))
    post = _gain_in(g_post)

    def body(x_ref, m_ref, g_ref, t_ref, dy_ref, loss_ref):
        mv = m_ref[...].astype(F32)
        y = x_ref[...] + mv * _rstd(mv) * g_ref[post[2], :]
        err = y - t_ref[...]
        dy_ref[...] = err * (1.0 / d)

        @pl.when(pl.program_id(0) == 0)
        def _():
            loss_ref[...] = jnp.zeros_like(loss_ref)

        loss_ref[...] += jnp.sum(err * err)

    dy, acc = pl.pallas_call(
        body, name=name, grid=(s // t,),
        in_specs=[_row_spec(t, d), _row_spec(t, d), post[1], _row_spec(t, d)],
        out_specs=[_row_spec(t, d), pl.BlockSpec((8, LANE), lambda i: (0, 0))],
        out_shape=[jax.ShapeDtypeStruct((s, d), F32), jax.ShapeDtypeStruct((8, LANE), F32)],
        compiler_params=_params(("arbitrary",)),
    )(x, mix, post[0], target)
    return dy, acc[0, 0] * (0.5 / d)


def _norm_bwd(dx_out, branches, x_in, post, name):
    s, d = dx_out.shape
    t = _pick(s, (NORM_TILE, ROW_TILE))
    nb = len(branches)
    has_post = post is not None
    gain_rows = [_gain_in(g) for _, g in branches] + ([_gain_in(post[1])] if has_post else [])
    dg_spec = pl.BlockSpec((1, d), lambda i: (0, 0))

    def body(*refs):
        dx_ref = refs[0]
        pos = 1
        dx = dx_ref[...]
        first = pl.program_id(0) == 0
        n_in = 1 + (1 + 2 * nb if nb else 0) + (2 if has_post else 0)
        outs = refs[n_in:]
        opos = 0
        if nb:
            xv = refs[pos][...]
            pos += 1
            r = _rstd(xv)
            dx_o = outs[0]
            opos = 1
            for bi in range(nb):
                dxn = refs[pos][...].astype(F32)
                g = refs[pos + 1][gain_rows[bi][2], :]
                pos += 2
                dv, dg_rows = _rms_bwd(dxn, xv, g, r)
                dx = dx + dv
                dg_ref = outs[opos]
                opos += 1

                @pl.when(first)
                def _(dg_ref=dg_ref):
                    dg_ref[...] = jnp.zeros_like(dg_ref)

                dg_ref[...] += jnp.sum(dg_rows, axis=0, keepdims=True)
            dx_o[...] = dx
        if has_post:
            mv = refs[pos][...].astype(F32)
            g = refs[pos + 1][gain_rows[-1][2], :]
            dm, dg_rows = _rms_bwd(dx, mv, g, _rstd(mv))
            outs[opos][...] = dm.astype(BF16)
            dg_ref = outs[opos + 1]

            @pl.when(first)
            def _():
                dg_ref[...] = jnp.zeros_like(dg_ref)

            dg_ref[...] += jnp.sum(dg_rows, axis=0, keepdims=True)

    ins, in_specs = [dx_out], [_row_spec(t, d)]
    out_shape, out_specs = [], []
    if nb:
        ins.append(x_in)
        in_specs.append(_row_spec(t, d))
        out_shape.append(jax.ShapeDtypeStruct((s, d), F32))
        out_specs.append(_row_spec(t, d))
        for (dxn, _), gain in zip(branches, gain_rows):
            ins += [dxn, gain[0]]
            in_specs += [_row_spec(t, d), gain[1]]
            out_shape.append(jax.ShapeDtypeStruct((1, d), F32))
            out_specs.append(dg_spec)
    if has_post:
        ins += [post[0], gain_rows[-1][0]]
        in_specs += [_row_spec(t, d), gain_rows[-1][1]]
        out_shape += [jax.ShapeDtypeStruct((s, d), BF16), jax.ShapeDtypeStruct((1, d), F32)]
        out_specs += [_row_spec(t, d), dg_spec]
    res = pl.pallas_call(
        body, name=name, grid=(s // t,), in_specs=in_specs, out_specs=out_specs, out_shape=out_shape,
        compiler_params=_params(("arbitrary",)),
    )(*ins)
    res = list(res)
    dx_in = res.pop(0) if nb else dx_out
    dgs = [res.pop(0) for _ in range(nb)]
    dm, dg_post = (res[0], res[1]) if has_post else (None, None)
    return dx_in, dgs, dm, dg_post


HALO = 16


def _shift_down(u, prev, k):
    rows = lax.broadcasted_iota(jnp.int32, u.shape, 0)
    out = pltpu.roll(u, k, 0)
    for i in range(k):
        out = jnp.where(rows == i, prev[HALO - k + i:HALO - k + i + 1, :], out)
    return out


def _shift_up(u, nxt, k):
    n = u.shape[0]
    rows = lax.broadcasted_iota(jnp.int32, u.shape, 0)
    out = pltpu.roll(u, n - k, 0)
    for i in range(k):
        out = jnp.where(rows == n - k + i, nxt[i:i + 1, :], out)
    return out


def _conv_gate_fwd(p, cw, name):
    s, d3 = p.shape
    d = d3 // 3
    t = _pick(s, (ROW_TILE,))
    hb = t // HALO
    taps = _gain_in(cw)

    def body(p_ref, prev_ref, w_ref, z_ref):
        i = pl.program_id(0)
        pv = p_ref[...].astype(F32)
        b, u = pv[:, :d], pv[:, d:2 * d] * pv[:, 2 * d:]
        ph = prev_ref[...].astype(F32)
        up = jnp.where(i > 0, ph[:, d:2 * d] * ph[:, 2 * d:], 0.0)
        w = w_ref[taps[2], :]
        y = w[0:1, :] * _shift_down(u, up, 2) + w[1:2, :] * _shift_down(u, up, 1) + w[2:3, :] * u
        z_ref[...] = (b * y).astype(BF16)

    return pl.pallas_call(
        body, name=name, grid=(s // t,),
        in_specs=[_row_spec(t, d3),
                  pl.BlockSpec((HALO, d3), lambda i: (jnp.maximum(i * hb - 1, 0), 0)),
                  taps[1]],
        out_specs=_row_spec(t, d),
        out_shape=jax.ShapeDtypeStruct((s, d), BF16),
        compiler_params=_params(("parallel",)),
    )(p, p, taps[0])


def _conv_gate_bwd(p, dz, cw, name):
    s, d3 = p.shape
    d = d3 // 3
    t = _pick(s, (ROW_TILE,))
    hb = t // HALO
    nt = s // t
    last_halo = s // HALO - 1
    taps = _gain_in(cw)

    def body(p_ref, prev_ref, next_ref, dz_ref, dznext_ref, w_ref, dp_ref, dw_ref):
        i = pl.program_id(0)
        pv = p_ref[...].astype(F32)
        b, c, h = pv[:, :d], pv[:, d:2 * d], pv[:, 2 * d:]
        u = c * h
        ph = prev_ref[...].astype(F32)
        up = jnp.where(i > 0, ph[:, d:2 * d] * ph[:, 2 * d:], 0.0)
        w = w_ref[taps[2], :]
        u1, u2 = _shift_down(u, up, 1), _shift_down(u, up, 2)
        y = w[0:1, :] * u2 + w[1:2, :] * u1 + w[2:3, :] * u
        dz = dz_ref[...].astype(F32)
        dy = dz * b
        dyn = jnp.where(i < nt - 1, dznext_ref[...].astype(F32) * next_ref[...].astype(F32)[:, :d], 0.0)
        du = w[2:3, :] * dy + w[1:2, :] * _shift_up(dy, dyn, 1) + w[0:1, :] * _shift_up(dy, dyn, 2)
        dp_ref[:, :d] = (dz * y).astype(BF16)
        dp_ref[:, d:2 * d] = (du * h).astype(BF16)
        dp_ref[:, 2 * d:] = (du * c).astype(BF16)

        @pl.when(i == 0)
        def _():
            dw_ref[...] = jnp.zeros_like(dw_ref)

        dw_ref[0:1, :] += jnp.sum(dy * u2, axis=0, keepdims=True)
        dw_ref[1:2, :] += jnp.sum(dy * u1, axis=0, keepdims=True)
        dw_ref[2:3, :] += jnp.sum(dy * u, axis=0, keepdims=True)

    return pl.pallas_call(
        body, name=name, grid=(nt,),
        in_specs=[_row_spec(t, d3),
                  pl.BlockSpec((HALO, d3), lambda i: (jnp.maximum(i * hb - 1, 0), 0)),
                  pl.BlockSpec((HALO, d3), lambda i: (jnp.minimum((i + 1) * hb, last_halo), 0)),
                  _row_spec(t, d),
                  pl.BlockSpec((HALO, d), lambda i: (jnp.minimum((i + 1) * hb, last_halo), 0)),
                  taps[1]],
        out_specs=[_row_spec(t, d3), pl.BlockSpec((3, d), lambda i: (0, 0))],
        out_shape=[jax.ShapeDtypeStruct((s, d3), BF16), jax.ShapeDtypeStruct((3, d), F32)],
        compiler_params=_params(("arbitrary",)),
    )(p, p, p, dz, dz, taps[0])


FFN_ROWS, FFN_COLS = (1024, 512, 256), (1408, 768, 256, 128)


def _row_chunks(tm, rows=256):
    return [slice(r, r + min(rows, tm)) for r in range(0, tm, min(rows, tm))]


def _ffn_in_swiglu(xn, w_in, name):
    s, k = xn.shape
    ff = w_in.shape[1] // 2
    tm, tn = _pick(s, FFN_ROWS), _pick(ff, FFN_COLS)
    nj = ff // tn

    def body(x_ref, wg_ref, wu_ref, f_ref, a_ref):
        for rows in _row_chunks(tm):
            xv = x_ref[rows, :]
            gate = jnp.dot(xv, wg_ref[...], preferred_element_type=F32)
            up = jnp.dot(xv, wu_ref[...], preferred_element_type=F32)
            f_ref[0, rows, :] = gate.astype(BF16)
            f_ref[1, rows, :] = up.astype(BF16)
            a_ref[rows, :] = (gate * jax.nn.sigmoid(gate) * up).astype(BF16)

    return pl.pallas_call(
        body, name=name, grid=(nj, s // tm),
        in_specs=[pl.BlockSpec((tm, k), lambda j, i: (i, 0)),
                  pl.BlockSpec((k, tn), lambda j, i: (0, j)),
                  pl.BlockSpec((k, tn), lambda j, i: (0, nj + j))],
        out_specs=[pl.BlockSpec((2, tm, tn), lambda j, i: (0, i, j)), pl.BlockSpec((tm, tn), lambda j, i: (i, j))],
        out_shape=[jax.ShapeDtypeStruct((2, s, ff), BF16), jax.ShapeDtypeStruct((s, ff), BF16)],
        compiler_params=_params(("parallel", "parallel")),
    )(xn, w_in, w_in)


def _ffn_out_dx_swiglu(dff, w_out, f, name):
    s, d = dff.shape
    ff = w_out.shape[0]
    tm, tn = _pick(s, FFN_ROWS), _pick(ff, FFN_COLS)

    def body(d_ref, w_ref, f_ref, df_ref):
        for rows in _row_chunks(tm):
            da = lax.dot_general(d_ref[rows, :], w_ref[...], (((1,), (1,)), ((), ())), preferred_element_type=F32)
            gate = f_ref[0, rows, :].astype(F32)
            up = f_ref[1, rows, :].astype(F32)
            sg = jax.nn.sigmoid(gate)
            silu = gate * sg
            df_ref[0, rows, :] = (da * up * (sg + silu * (1.0 - sg))).astype(BF16)
            df_ref[1, rows, :] = (da * silu).astype(BF16)

    planes = pl.BlockSpec((2, tm, tn), lambda j, i: (0, i, j))
    return pl.pallas_call(
        body, name=name, grid=(ff // tn, s // tm),
        in_specs=[pl.BlockSpec((tm, d), lambda j, i: (i, 0)), pl.BlockSpec((tn, d), lambda j, i: (j, 0)), planes],
        out_specs=planes, out_shape=jax.ShapeDtypeStruct((2, s, ff), BF16),
        compiler_params=_params(("parallel", "parallel")),
    )(dff, w_out, f)


SUPER = 2048
RES = 16
PAIR = 128
L = 128
FWD_TOGETHER = 16
BWD_TOGETHER = 4


def _alibi_slopes(n_heads):
    h = np.arange(n_heads, dtype=np.float32) + 1.0
    return np.power(2.0, -8.0 * h / n_heads).astype(np.float32)


def _permute16(x, inverse, name):
    s, d = x.shape
    cw = LANE

    def body(x_ref, o_ref):
        if inverse:
            for m in range(L):
                o_ref[RES * m:RES * (m + 1), :] = x_ref[pl.ds(m, RES, stride=L), :]
        else:
            for r in range(RES):
                o_ref[L * r:L * (r + 1), :] = x_ref[pl.ds(r, L, stride=RES), :]

    spec = pl.BlockSpec((SUPER, cw), lambda i, j: (i, j))
    return pl.pallas_call(
        body, name=name, grid=(s // SUPER, d // cw), in_specs=[spec], out_specs=spec,
        out_shape=jax.ShapeDtypeStruct((s, d), x.dtype),
        compiler_params=_params(("parallel", "parallel")),
    )(x)


def _slope_table(d):
    nh = d // HEAD_DIM
    sl = _alibi_slopes(nh)
    tab = np.repeat(sl, HEAD_DIM).reshape(d // PAIR, 1, PAIR)
    return jnp.asarray(np.broadcast_to(tab, (d // PAIR, 8, PAIR)).copy())


def _geometry(dil):
    nch = RES // dil
    return nch, L // nch


def _band(dil):
    nch, w = _geometry(dil)
    sh = w.bit_length() - 1
    i = lax.broadcasted_iota(jnp.int32, (L, 2 * L), 0)
    j = lax.broadcasted_iota(jnp.int32, (L, 2 * L), 1)

    def pos(t):
        return jnp.bitwise_and(t, w - 1) * nch + jnp.right_shift(t, sh)

    delta = pos(i) + L - (pos(jnp.bitwise_and(j, L - 1)) + jnp.bitwise_and(j, L))
    return (delta * dil).astype(F32), (delta >= 0) & (delta <= L), j < L


def _fill_bias(bias_s, sl_ref):
    for b, dil in enumerate(DILATIONS):
        base, band, prev_half = _band(dil)
        for first in range(2):
            valid = band & jnp.logical_not(prev_half) if first else band
            for h in range(2):
                slope = sl_ref[0:1, HEAD_DIM * h:HEAD_DIM * h + 1]
                bias_s[(2 * b + first) * 2 + h] = jnp.where(valid, -slope * base, NEG_BIG)


def _bias_index(b, sb, n):
    first = jnp.logical_and(sb == 0, n == 0).astype(jnp.int32)
    return (2 * b + first) * 2


def _offsets(dil, res, n):
    nch, w = _geometry(dil)

    def al(v):
        return v if isinstance(v, int) else pl.multiple_of(v, w)

    q_off = [al((a * dil + res) * L + n * w) for a in range(nch)]
    k_off = [al((a * dil + res) * 2 * L + L + n * w) for a in range(nch)]
    kp_off = [al((a * dil + res) * 2 * L + L + n * w - w) for a in range(nch)]
    return q_off, k_off, kp_off, w


def _gather(ref, offs, w):
    parts = [ref[pl.ds(o, w), :] for o in offs]
    return parts[0] if len(parts) == 1 else jnp.concatenate(parts, axis=0)


def _scatter(ref, offs, w, val, add=False):
    for a, o in enumerate(offs):
        piece = val[a * w:(a + 1) * w, :]
        if add:
            ref[pl.ds(o, w), :] += piece
        else:
            ref[pl.ds(o, w), :] = piece


def _fill_key_buffer(buf, prev_ref, cur_ref):
    for r in range(RES):
        buf[2 * L * r:2 * L * r + L, :] = prev_ref[L * r:L * (r + 1), :]
        buf[2 * L * r + L:2 * L * (r + 1), :] = cur_ref[L * r:L * (r + 1), :]


def _two_heads(x, low):
    zero = jnp.zeros_like(x)
    return jnp.concatenate([jnp.where(low, x, zero), jnp.where(low, zero, x)], axis=0)


def _loop_blocks(dil, stages, together):
    together = max(together, dil) if dil < RES else together

    def it(i, c):
        if dil == RES:
            blocks = [(i * together + k, 0) for k in range(together)]
        else:
            blocks = [(res, i * (together // dil) + k) for k in range(together // dil) for res in range(dil)]
        state = [stages[0](res, n) for res, n in blocks]
        for stage in stages[1:]:
            state = [stage(res, n, prev) for (res, n), prev in zip(blocks, state)]
        for writes in state:
            for args in writes:
                _scatter(*args)
        return c

    lax.fori_loop(0, RES // together, it, 0)


NT = (((1,), (1,)), ((), ()))
TN = (((0,), (0,)), ((), ()))


def _attention_fwd(q, kv, name):
    s, d = q.shape
    g_n, ns = d // PAIR, s // SUPER

    def body(sl_ref, q_ref, kc_ref, kp_ref, vc_ref, vp_ref, o_ref, lse_ref, kbuf, vbuf, m_s, l_s, acc_s, bias_s):
        sb = pl.program_id(1)
        _fill_key_buffer(kbuf, kp_ref, kc_ref)
        _fill_key_buffer(vbuf, vp_ref, vc_ref)
        pl.when(sb == 0)(lambda: _fill_bias(bias_s, sl_ref))
        low = lax.broadcasted_iota(jnp.int32, (L, PAIR), 1) < HEAD_DIM
        low_k = lax.broadcasted_iota(jnp.int32, (2 * L, PAIR), 1) < HEAD_DIM
        ones_bd = _two_heads(jnp.ones((2 * L, PAIR), BF16), low_k)
        for bi, dil in enumerate(DILATIONS):
            first_branch, last_branch = bi == 0, bi == len(DILATIONS) - 1

            def scores(res, n, dil=dil):
                q_off, k_off, kp_off, w = _offsets(dil, res, n)
                qf = _gather(q_ref, q_off, w).astype(BF16)
                kcat = jnp.concatenate([_gather(kbuf, kp_off, w), _gather(kbuf, k_off, w)], axis=0).astype(BF16)
                return lax.dot_general(qf, _two_heads(kcat, low_k), NT, preferred_element_type=F32)

            def update(res, n, sc, bi=bi, dil=dil, first_branch=first_branch, last_branch=last_branch):
                q_off, k_off, kp_off, w = _offsets(dil, res, n)
                vcat = jnp.concatenate([_gather(vbuf, kp_off, w), _gather(vbuf, k_off, w)], axis=0).astype(BF16)
                v_ones = jnp.concatenate([_two_heads(vcat, low_k), ones_bd], axis=1)
                bias_at = _bias_index(bi, sb, n)
                if not first_branch:
                    m_prev = _gather(m_s, q_off, w)
                ps, m_new = [], []
                for h in range(2):
                    s_h = sc[:, 2 * L * h:2 * L * (h + 1)] + bias_s[bias_at + h]
                    mh = jnp.max(s_h, axis=1, keepdims=True)
                    if not first_branch:
                        mh = jnp.maximum(mh, m_prev[:, HEAD_DIM * h:HEAD_DIM * h + 1])
                    ps.append(jnp.exp(s_h - mh).astype(BF16))
                    m_new.append(mh)
                m_full = jnp.where(low, m_new[0], m_new[1])
                both = jnp.dot(jnp.concatenate(ps, axis=1), v_ones, preferred_element_type=F32)
                acc, l_full = both[:, :PAIR], both[:, PAIR:]
                if not first_branch:
                    alpha = jnp.exp(m_prev - m_full)
                    l_full = _gather(l_s, q_off, w) * alpha + l_full
                    acc = _gather(acc_s, q_off, w) * alpha + acc
                if last_branch:
                    return [(o_ref, q_off, w, acc / l_full, False), (lse_ref, q_off, w, m_full + jnp.log(l_full), False)]
                return [(m_s, q_off, w, m_full, False), (l_s, q_off, w, l_full, False), (acc_s, q_off, w, acc, False)]

            _loop_blocks(dil, [scores, update], FWD_TOGETHER)

    prev = lambda i: jnp.maximum(i - 1, 0)
    blk = pl.BlockSpec((SUPER, PAIR), lambda g, i: (i, g))
    in_specs = [pl.BlockSpec((None, 8, PAIR), lambda g, i: (g, 0, 0)), blk,
                pl.BlockSpec((SUPER, PAIR), lambda g, i: (i, g)),
                pl.BlockSpec((SUPER, PAIR), lambda g, i: (prev(i), g)),
                pl.BlockSpec((SUPER, PAIR), lambda g, i: (i, g_n + g)),
                pl.BlockSpec((SUPER, PAIR), lambda g, i: (prev(i), g_n + g))]
    return pl.pallas_call(
        body, name=name, grid=(g_n, ns), in_specs=in_specs, out_specs=[blk, blk],
        out_shape=[jax.ShapeDtypeStruct((s, d), F32)] * 2,
        scratch_shapes=([pltpu.VMEM((2 * SUPER, PAIR), F32)] * 2 + [pltpu.VMEM((SUPER, PAIR), F32)] * 3
                        + [pltpu.VMEM((4 * len(DILATIONS), L, 2 * L), F32)]),
        compiler_params=_params(("parallel", "arbitrary")),
    )(_slope_table(d), q, kv, kv, kv, kv)


def _attention_bwd(q, kv, o, do, lse, dkv_in, name):
    s, d = q.shape
    g_n, ns = d // PAIR, s // SUPER
    has_in = dkv_in is not None

    def body(*refs):
        sl_ref, q_ref, do_ref, o_ref, lse_ref, kc_ref, kp_ref, vc_ref, vp_ref = refs[:9]
        pos = 9
        if has_in:
            dkv_in_ref = refs[9]
            pos = 10
        dq_ref, dkv_ref, kbuf, vbuf, dkbuf, dvbuf, dq_s, bias_s = refs[pos:]
        step = pl.program_id(1)
        sb = ns - 1 - step
        _fill_key_buffer(kbuf, kp_ref, kc_ref)
        _fill_key_buffer(vbuf, vp_ref, vc_ref)

        @pl.when(step == 0)
        def _():
            _fill_bias(bias_s, sl_ref)
            dkbuf[...] = jnp.zeros_like(dkbuf)
            dvbuf[...] = jnp.zeros_like(dvbuf)

        @pl.when(step > 0)
        def _():
            for buf in (dkbuf, dvbuf):
                for r in range(RES):
                    buf[2 * L * r + L:2 * L * (r + 1), :] = buf[2 * L * r:2 * L * r + L, :]
                    buf[2 * L * r:2 * L * r + L, :] = jnp.zeros((L, PAIR), F32)

        low = lax.broadcasted_iota(jnp.int32, (L, PAIR), 1) < HEAD_DIM
        low_k = lax.broadcasted_iota(jnp.int32, (2 * L, PAIR), 1) < HEAD_DIM
        low_t = lax.broadcasted_iota(jnp.int32, (PAIR, 2 * L), 0) < HEAD_DIM
        for bi, dil in enumerate(DILATIONS):
            first_branch = bi == 0

            def scores(res, n, dil=dil):
                q_off, k_off, kp_off, w = _offsets(dil, res, n)
                qb = _gather(q_ref, q_off, w).astype(BF16)
                dof = _gather(do_ref, q_off, w)
                prod = dof * _gather(o_ref, q_off, w)
                dob = dof.astype(BF16)
                lse_f = _gather(lse_ref, q_off, w)
                zero = jnp.zeros_like(prod)
                dsum = (jnp.sum(jnp.where(low, prod, zero), axis=1, keepdims=True),
                        jnp.sum(jnp.where(low, zero, prod), axis=1, keepdims=True))
                kcat = jnp.concatenate([_gather(kbuf, kp_off, w), _gather(kbuf, k_off, w)], axis=0).astype(BF16)
                vcat = jnp.concatenate([_gather(vbuf, kp_off, w), _gather(vbuf, k_off, w)], axis=0).astype(BF16)
                k_bd, v_bd = _two_heads(kcat, low_k), _two_heads(vcat, low_k)
                sc = lax.dot_general(qb, k_bd, NT, preferred_element_type=F32)
                dp = lax.dot_general(dob, v_bd, NT, preferred_element_type=F32)
                return qb, dob, lse_f, dsum, k_bd, sc, dp

            def gradients(res, n, given, bi=bi, dil=dil, first_branch=first_branch):
                qb, dob, lse_f, dsum, k_bd, sc, dp = given
                q_off, k_off, kp_off, w = _offsets(dil, res, n)
                bias_at = _bias_index(bi, sb, n)
                ps, dss = [], []
                for h in range(2):
                    cols = slice(2 * L * h, 2 * L * (h + 1))
                    lse_h = lse_f[:, HEAD_DIM * h:HEAD_DIM * h + 1]
                    p_h = jnp.exp(sc[:, cols] + bias_s[bias_at + h] - lse_h)
                    dss.append((p_h * (dp[:, cols] - dsum[h])).astype(BF16))
                    ps.append(p_h.astype(BF16))
                ds_cat, p_cat = jnp.concatenate(dss, axis=1), jnp.concatenate(ps, axis=1)
                dq = jnp.dot(ds_cat, k_bd, preferred_element_type=F32)
                dk_t = lax.dot_general(qb, ds_cat, TN, preferred_element_type=F32)
                dv_t = lax.dot_general(dob, p_cat, TN, preferred_element_type=F32)
                dk = jnp.where(low_t, dk_t[:, :2 * L], dk_t[:, 2 * L:]).T
                dv = jnp.where(low_t, dv_t[:, :2 * L], dv_t[:, 2 * L:]).T
                return [(dq_s, q_off, w, dq, not first_branch),
                        (dkbuf, kp_off, w, dk[:L], True), (dkbuf, k_off, w, dk[L:], True),
                        (dvbuf, kp_off, w, dv[:L], True), (dvbuf, k_off, w, dv[L:], True)]

            _loop_blocks(dil, [scores, gradients], BWD_TOGETHER)

        dq_ref[...] = dq_s[...].astype(BF16)
        for r in range(RES):
            rows, cur = slice(L * r, L * (r + 1)), slice(2 * L * r + L, 2 * L * (r + 1))
            for plane, buf in enumerate((dkbuf, dvbuf)):
                if has_in:
                    dkv_ref[plane, rows, :] = buf[cur, :] + dkv_in_ref[plane, rows, :]
                else:
                    dkv_ref[plane, rows, :] = buf[cur, :]

    rev = lambda i: ns - 1 - i
    prev = lambda i: jnp.maximum(ns - 2 - i, 0)
    blk = pl.BlockSpec((SUPER, PAIR), lambda g, i: (rev(i), g))
    in_specs = [pl.BlockSpec((None, 8, PAIR), lambda g, i: (g, 0, 0)), blk, blk, blk, blk,
                pl.BlockSpec((SUPER, PAIR), lambda g, i: (rev(i), g)),
                pl.BlockSpec((SUPER, PAIR), lambda g, i: (prev(i), g)),
                pl.BlockSpec((SUPER, PAIR), lambda g, i: (rev(i), g_n + g)),
                pl.BlockSpec((SUPER, PAIR), lambda g, i: (prev(i), g_n + g))]
    ins = [_slope_table(d), q, do, o, lse, kv, kv, kv, kv]
    planes = pl.BlockSpec((2, SUPER, PAIR), lambda g, i: (0, rev(i), g))
    if has_in:
        in_specs.append(planes)
        ins.append(dkv_in)
    res = pl.pallas_call(
        body, name=name, grid=(g_n, ns), in_specs=in_specs, out_specs=[blk, planes],
        out_shape=[jax.ShapeDtypeStruct((s, d), BF16), jax.ShapeDtypeStruct((2, s, d), F32)],
        scratch_shapes=([pltpu.VMEM((2 * SUPER, PAIR), F32)] * 4 + [pltpu.VMEM((SUPER, PAIR), F32)]
                        + [pltpu.VMEM((4 * len(DILATIONS), L, 2 * L), F32)]),
        compiler_params=_params(("parallel", "arbitrary")),
    )(*ins)
    return res[0], res[1]


def _coords():
    return lax.axis_index("x"), lax.axis_index("y"), lax.axis_index("c")


def _chip_peers(x, y):
    return [(1 - x, y), (x, 1 - y), (1 - x, 1 - y)]


def _block_of(ref, axis, blk, size):
    start = pl.multiple_of(blk * size, size)
    if axis == 1:
        return ref.at[:, pl.ds(start, size)]
    return ref.at[pl.ds(start, size), :]


ANY = pl.BlockSpec(memory_space=pl.ANY)


HBM = pl.BlockSpec(memory_space=pltpu.HBM)
SEM = pl.BlockSpec(memory_space=pltpu.SEMAPHORE)
SPLIT = pltpu.CompilerParams(has_side_effects=pltpu.SideEffectType.DATAFLOW_SIDE_EFFECTING)


def _in_hbm(a):
    return pltpu.with_memory_space_constraint(a, pltpu.HBM)


def _thru(arrays):
    return [pltpu.HBM(a.shape, a.dtype) for a in arrays]


def _cast_place(w, layer, ax, dtype, after, name):
    _, k, n = w.shape
    t = _pick(k, (512, 256, 128))
    nb = k // t
    extra = [] if after is None else [after]

    def body(blk_ref, w_ref, *refs):
        b_ref, f_ref = refs[len(extra):]
        v = w_ref[...].astype(dtype)
        b_ref[...] = v
        f_ref[...] = v

    full_shape = (k, 4 * n) if ax == 1 else (4 * k, n)
    place = (lambda i, blk: (i, blk[0])) if ax == 1 else (lambda i, blk: (blk[0] * nb + i, 0))
    return pl.pallas_call(
        body, name=name,
        grid_spec=pltpu.PrefetchScalarGridSpec(
            num_scalar_prefetch=1, grid=(nb,),
            in_specs=[pl.BlockSpec((None, t, n), lambda i, blk: (layer, i, 0))] + [ANY] * len(extra),
            out_specs=[pl.BlockSpec((t, n), lambda i, blk: (i, 0)), pl.BlockSpec((t, n), place)]),
        out_shape=[jax.ShapeDtypeStruct((k, n), dtype), jax.ShapeDtypeStruct(full_shape, dtype)],
        compiler_params=_params(("parallel",)),
    )(_my_block()[None], w, *extra)


def _my_block():
    return (2 * lax.axis_index("x") + lax.axis_index("y")).astype(jnp.int32)


def _gather_start(group, carry, name):
    n, nc = len(group), len(carry)

    def body(*refs):
        blocks, fulls, send_sem, recv_sem = refs[:n], refs[n:2 * n], refs[2 * n + nc], refs[2 * n + nc + 1]
        x, y, c = _coords()
        for t, (b, _, ax) in enumerate(group):
            mine = _block_of(fulls[t], ax, 2 * x + y, b.shape[ax])
            for j, (px, py) in enumerate(_chip_peers(x, y)):
                pltpu.make_async_remote_copy(
                    src_ref=blocks[t], dst_ref=mine, send_sem=send_sem.at[3 * t + j], recv_sem=recv_sem.at[3 * t + j],
                    device_id=(px, py, c), device_id_type=MESH).start()

    arrays = [b for b, _, _ in group] + [f for _, f, _ in group] + list(carry)
    sems = [pltpu.SemaphoreType.DMA((3 * n,))] * 2
    res = pl.pallas_call(
        body, name=name, in_specs=[HBM] * len(arrays), out_specs=[SEM, SEM] + [HBM] * len(arrays),
        out_shape=sems + _thru(arrays), input_output_aliases={i: 2 + i for i in range(len(arrays))},
        compiler_params=SPLIT,
    )(*[_in_hbm(a) for a in arrays])
    return (res[0], res[1], list(res[2:2 + n]), list(res[2 + n:2 + 2 * n])), list(res[2 + 2 * n:])


def _gather_wait(group, started, after, name):
    sends, recvs, blocks, fulls = started
    m = len(group)

    def body(*refs):
        blk_refs, full_refs, send_sem, recv_sem = refs[:m], refs[m:2 * m], refs[2 * m], refs[2 * m + 1]
        x, y, c = _coords()
        for t, (b, _, ax) in enumerate(group):
            for j, (px, py) in enumerate(_chip_peers(x, y)):
                cp = pltpu.make_async_remote_copy(
                    src_ref=blk_refs[t], dst_ref=_block_of(full_refs[t], ax, 2 * px + py, b.shape[ax]),
                    send_sem=send_sem.at[3 * t + j], recv_sem=recv_sem.at[3 * t + j],
                    device_id=(px, py, c), device_id_type=MESH)
                cp.wait_send()
                cp.wait_recv()

    extra = [] if after is None else [after]
    res = pl.pallas_call(
        body, name=name, in_specs=[HBM] * (2 * m) + [SEM, SEM] + [ANY] * len(extra), out_specs=[HBM] * (2 * m),
        out_shape=_thru(blocks) + _thru(fulls), input_output_aliases={i: i for i in range(2 * m)},
        compiler_params=SPLIT,
    )(*blocks, *fulls, sends, recvs, *extra)
    return list(res[m:])


def _scatter_start(grads, carry, name):
    n = len(grads)
    n_in = 2 * n + len(carry)

    def body(*refs):
        g_refs, st_refs, send_sem, recv_sem = refs[:n], refs[n:2 * n], refs[n_in], refs[n_in + 1]
        x, y, c = _coords()
        for t, (g, ax) in enumerate(grads):
            for j, (px, py) in enumerate(_chip_peers(x, y)):
                pltpu.make_async_remote_copy(
                    src_ref=_block_of(g_refs[t], ax, 2 * px + py, g.shape[ax] // 4), dst_ref=st_refs[t].at[j],
                    send_sem=send_sem.at[3 * t + j], recv_sem=recv_sem.at[3 * t + j],
                    device_id=(px, py, c), device_id_type=MESH).start()

    arrays = [g for g, _ in grads]
    for g, ax in grads:
        shape = list(g.shape)
        shape[ax] //= 4
        arrays.append(lax.empty((3, *shape), g.dtype))
    arrays += list(carry)
    sems = [pltpu.SemaphoreType.DMA((3 * n,))] * 2
    res = pl.pallas_call(
        body, name=name, in_specs=[HBM] * n_in, out_specs=[SEM, SEM] + [HBM] * n_in,
        out_shape=sems + _thru(arrays), input_output_aliases={i: 2 + i for i in range(n_in)},
        compiler_params=SPLIT,
    )(*[_in_hbm(a) for a in arrays])
    return (res[0], res[1], list(res[2:2 + n]), list(res[2 + n:2 + 2 * n])), list(res[2 + 2 * n:])


def _scatter_wait(axes, started, after, name):
    sends, recvs, full, stacks = started
    n = len(full)
    extra = [] if after is None else [after]

    def body(*refs):
        g_refs, st_refs, send_sem, recv_sem = refs[:n], refs[n:2 * n], refs[2 * n], refs[2 * n + 1]
        x, y, c = _coords()
        for t, ax in enumerate(axes):
            size = full[t].shape[ax] // 4
            for j, (px, py) in enumerate(_chip_peers(x, y)):
                cp = pltpu.make_async_remote_copy(
                    src_ref=_block_of(g_refs[t], ax, 2 * px + py, size), dst_ref=st_refs[t].at[j],
                    send_sem=send_sem.at[3 * t + j], recv_sem=recv_sem.at[3 * t + j],
                    device_id=(px, py, c), device_id_type=MESH)
                cp.wait_send()
                cp.wait_recv()

    res = pl.pallas_call(
        body, name=name, in_specs=[HBM] * (2 * n) + [SEM, SEM] + [ANY] * len(extra), out_specs=[HBM] * (2 * n),
        out_shape=_thru(full) + _thru(stacks), input_output_aliases={i: i for i in range(2 * n)},
        compiler_params=SPLIT,
    )(*full, *stacks, sends, recvs, *extra)
    return list(res[:n]), list(res[n:])


def _pair_copies(g_refs, st_refs, out_refs, items, send_sem, recv_sem):
    x, y, c = _coords()
    copies = []
    for u, (g, ax, _) in enumerate(items):
        own = _block_of(g_refs[u], ax, 2 * x + y, g.shape[ax] // 4)
        for k, (src, dst) in enumerate([(own, out_refs[u].at[0]), (st_refs[u], out_refs[u].at[pl.ds(1, 3)])]):
            copies.append(pltpu.make_async_remote_copy(
                src_ref=src, dst_ref=dst, send_sem=send_sem.at[2 * u + k], recv_sem=recv_sem.at[2 * u + k],
                device_id=(x, y, 1 - c), device_id_type=MESH))
    return copies


def _pair_start(items, carry, name):
    n = len(items)
    n_in = 3 * n + len(carry)

    def body(*refs):
        for cp in _pair_copies(refs[:n], refs[n:2 * n], refs[2 * n:3 * n], items, refs[n_in], refs[n_in + 1]):
            cp.start()

    arrays = ([g for g, _, _ in items] + [st for _, _, st in items]
              + [lax.empty((4, *st.shape[1:]), st.dtype) for _, _, st in items] + list(carry))
    sems = [pltpu.SemaphoreType.DMA((2 * n,))] * 2
    res = pl.pallas_call(
        body, name=name, in_specs=[HBM] * n_in, out_specs=[SEM, SEM] + [HBM] * n_in,
        out_shape=sems + _thru(arrays), input_output_aliases={i: 2 + i for i in range(n_in)},
        compiler_params=SPLIT,
    )(*[_in_hbm(a) for a in arrays])
    thru = res[2:]
    return (res[0], res[1], *(list(thru[k * n:(k + 1) * n]) for k in range(3))), list(thru[3 * n:])


def _pair_wait(axes, started, name):
    send, recv, full, stacks, landing = started
    n = len(full)
    items = [(full[u], axes[u], stacks[u]) for u in range(n)]

    def body(*refs):
        for cp in _pair_copies(refs[:n], refs[n:2 * n], refs[2 * n:3 * n], items, refs[3 * n], refs[3 * n + 1]):
            cp.wait_send()
            cp.wait_recv()

    res = pl.pallas_call(
        body, name=name, in_specs=[HBM] * (3 * n) + [SEM, SEM], out_specs=[HBM] * (3 * n),
        out_shape=_thru(full + stacks + landing), input_output_aliases={i: i for i in range(3 * n)},
        compiler_params=SPLIT,
    )(*full, *stacks, *landing, send, recv)
    return list(res[:n]), list(res[n:2 * n]), list(res[2 * n:])


def _allreduce_small(v, name):
    r, cdim = v.shape

    def body(v_ref, out_ref, buf, send_sems, recv_sems):
        x, y, c = _coords()
        me = 4 * x + 2 * y + c
        buf[0] = v_ref[...]
        sends = []
        for k in range(1, 8):
            peer = (x if not (k & 4) else 1 - x, y if not (k & 2) else 1 - y, c if not (k & 1) else 1 - c)
            cp = pltpu.make_async_remote_copy(
                src_ref=v_ref, dst_ref=buf.at[k], send_sem=send_sems.at[k - 1], recv_sem=recv_sems.at[k - 1],
                device_id=peer, device_id_type=MESH)
            cp.start()
            sends.append(cp)
        for cp in sends:
            cp.wait_recv()
        total = buf[me]
        for src in range(1, 8):
            total = total + buf[jnp.bitwise_xor(me, src)]
        out_ref[...] = total
        for cp in sends:
            cp.wait_send()

    return pl.pallas_call(
        body, name=name,
        in_specs=[pl.BlockSpec(memory_space=pltpu.VMEM)], out_specs=pl.BlockSpec(memory_space=pltpu.VMEM),
        out_shape=jax.ShapeDtypeStruct((r, cdim), F32),
        scratch_shapes=[pltpu.VMEM((8, r, cdim), F32), pltpu.SemaphoreType.DMA((7,)), pltpu.SemaphoreType.DMA((7,))],
        compiler_params=pltpu.CompilerParams(has_side_effects=True),
    )(v)


def _adamw_math(w, g, m, v):
    m = ADAM_B1 * m + (1.0 - ADAM_B1) * g
    v = ADAM_B2 * v + (1.0 - ADAM_B2) * jnp.square(g)
    m_hat = m / (1.0 - ADAM_B1 ** ADAM_STEP)
    v_hat = v / (1.0 - ADAM_B2 ** ADAM_STEP)
    delta = -ADAM_LR * (m_hat / (jnp.sqrt(v_hat) + ADAM_EPS) + ADAM_WD * w)
    return delta, m, v


def _adamw(w, m, v, grads, name):
    r, cdim = w.shape
    paired = isinstance(grads, list)
    layers = len(grads) if paired else 1
    t = _pick(r // layers, (128, 64, 32, 16, 8))
    per_layer = r // layers // t
    n_grad = 3 * layers if paired else 1

    def body(*refs):
        refs = refs[1:] if paired else refs
        w_ref, m_ref, v_ref = refs[:3]
        outs = refs[3 + n_grad:]

        def update(g):
            delta, m_new, v_new = _adamw_math(w_ref[...], g, m_ref[...], v_ref[...])
            outs[0][...] = g
            outs[1][...] = delta
            outs[2][...] = m_new
            outs[3][...] = v_new

        if not paired:
            update(refs[3][...])
            return
        layer = pl.program_id(0) // per_layer
        for l in range(layers):
            @pl.when(layer == l)
            def _(own_ref=refs[3 + 3 * l], st_ref=refs[4 + 3 * l], sib_ref=refs[5 + 3 * l]):
                sa = own_ref[...].astype(F32)
                sb = sib_ref[0].astype(F32)
                for k in range(3):
                    sa = sa + st_ref[k].astype(F32)
                    sb = sb + sib_ref[k + 1].astype(F32)
                update(sa + sb)

    out_shape = [jax.ShapeDtypeStruct((r, cdim), F32)] * 4
    if not paired:
        spec = pl.BlockSpec((t, cdim), lambda i: (i, 0))
        return pl.pallas_call(
            body, name=name, grid=(r // t,), in_specs=[spec] * 4, out_specs=[spec] * 4, out_shape=out_shape,
            compiler_params=_params(("parallel",)),
        )(w, m, v, grads)

    spec = pl.BlockSpec((t, cdim), lambda i, blk: (i, 0))
    ins, in_specs = [w, m, v], [spec] * 3
    for l, (g, ax, stack, sib) in enumerate(grads):
        row = lambda i, l=l: jnp.clip(i - l * per_layer, 0, per_layer - 1)
        own = ((lambda i, blk, row=row: (row(i), blk[0])) if ax == 1
               else (lambda i, blk, row=row: (blk[0] * per_layer + row(i), 0)))
        ins += [g, stack, sib]
        in_specs += [pl.BlockSpec((t, cdim), own),
                     pl.BlockSpec((3, t, cdim), lambda i, blk, row=row: (0, row(i), 0)),
                     pl.BlockSpec((4, t, cdim), lambda i, blk, row=row: (0, row(i), 0))]
    return pl.pallas_call(
        body, name=name,
        grid_spec=pltpu.PrefetchScalarGridSpec(
            num_scalar_prefetch=1, grid=(r // t,), in_specs=in_specs, out_specs=[spec] * 4),
        out_shape=out_shape, compiler_params=_params(("parallel",)),
    )(_my_block()[None], *ins)


def _local_step(x, target, gains, conv_ws, kv_gain, weights_of, send_grads):
    depth = len(gains)
    n_a = len(conv_ws)
    saved, ws = [], []
    kv = kvn = None
    _, (xn,) = _norm_res_fwd(x, None, None, [gains[0][0]], "norm_first")
    h = x
    for l in range(depth):
        g = gains[l]
        sv = {"x_in": h, "xn": xn}
        w = weights_of(l, "mix", h)
        ws.append(w)
        if l == n_a:
            kv = _mm(kvn, w["kv"], "nn", F32, "kv_fwd")
        if l < n_a:
            p = _mm(xn, w["conv_in"], "nn", BF16, f"conv_in_fwd_{l}")
            z = _conv_gate_fwd(p, conv_ws[l], f"conv_gate_fwd_{l}")
            mix, x1, xn2 = _mm_norm_res(z, w["conv_out"], h, g[1], g[2], f"conv_out_fwd_{l}")
            sv.update(p=p, z=z)
        else:
            j = l - n_a
            q = _mm(xn, w["q"], "nn", F32, f"q_fwd_{j}", scale=HEAD_DIM ** -0.5)
            o, lse = _attention_fwd(q, kv, f"attn_fwd_{j}")
            mix, x1, xn2 = _mm_norm_res(o, w["o"], h, g[1], g[2], f"o_fwd_{j}")
            sv.update(q=q, o=o, lse=lse)
        w.update(weights_of(l, "ffn", mix))
        f, a = _ffn_in_swiglu(xn2, w["ffn_in"], f"ffn_in_fwd_{l}")
        ff = _mm(a, w["ffn_out"], "nn", BF16, f"ffn_out_fwd_{l}")
        sv.update(mix=mix, x1=x1, xn2=xn2, f=f, a=a, ff=ff)
        saved.append(sv)
        if l == depth - 1:
            dx, loss = _norm_res_loss(x1, ff, g[3], target, "norm_loss")
        elif l == n_a - 1:
            h, _ = _norm_res_fwd(x1, ff, g[3], [], f"norm_end_{l}")
            h = _permute16(h, False, "permute_stream")
            _, (xn, kvn) = _norm_res_fwd(h, None, None, [gains[l + 1][0], kv_gain], "norm_permuted")
        else:
            h, (xn,) = _norm_res_fwd(x1, ff, g[3], [gains[l + 1][0]], f"norm_end_{l}")
    d_gains = [[None] * 4 for _ in range(depth)]
    d_conv = [None] * n_a
    d_kv_gain = None
    dkv = None
    _, _, dff, d_gains[depth - 1][3] = _norm_bwd(dx, [], None, (saved[-1]["ff"], gains[-1][3]), "norm_loss_bwd")
    for l in reversed(range(depth)):
        sv, g, w, grads = saved[l], gains[l], ws[l], {}
        grads["ffn_out"] =_mm(sv["a"], dff, "tn", BF16, f"ffn_out_dw_{l}")
        df = _ffn_out_dx_swiglu(dff, w["ffn_out"], sv["f"], f"ffn_out_dx_{l}")
        dxn2 = _mm(df, w["ffn_in"], "nt", BF16, f"ffn_in_dx_{l}")
        grads["ffn_in"] =_mm(sv["xn2"], df, "tn", BF16, f"ffn_in_dw_{l}")
        dx, (d_gains[l][2],), dmix, d_gains[l][1] = _norm_bwd(
            dx, [(dxn2, g[2])], sv["x1"], (sv["mix"], g[1]), f"norm_mid_bwd_{l}")
        dx, dmix = send_grads(l, "ffn", grads, [dx, dmix])
        if l < n_a:
            dz = _mm(dmix, w["conv_out"], "nt", BF16, f"conv_out_dx_{l}")
            grads["conv_out"] =_mm(sv["z"], dmix, "tn", BF16, f"conv_out_dw_{l}")
            dp, d_conv[l] = _conv_gate_bwd(sv["p"], dz, conv_ws[l], f"conv_gate_bwd_{l}")
            dxn = _mm(dp, w["conv_in"], "nt", BF16, f"conv_in_dx_{l}")
            grads["conv_in"] =_mm(sv["xn"], dp, "tn", BF16, f"conv_in_dw_{l}")
        else:
            j = l - n_a
            do = _mm(dmix, w["o"], "nt", F32, f"o_dx_{j}")
            grads["o"] =_mm(sv["o"], dmix, "tn", BF16, f"o_dw_{j}")
            dq, dkv = _attention_bwd(sv["q"], kv, sv["o"], do, sv["lse"], dkv, f"attn_bwd_{j}")
            scale = HEAD_DIM ** -0.5
            dxn = _mm(dq, w["q"], "nt", BF16, f"q_dx_{j}", scale=scale)
            grads["q"] =_mm(sv["xn"], dq, "tn", BF16, f"q_dw_{j}", scale=scale)
        branches = [(dxn, g[0])]
        if l == n_a:
            dkvn = _mm(dkv, w["kv"], "nt", BF16, "kv_dx")
            grads["kv"] =_mm(kvn, dkv, "tn", BF16, "kv_dw")
            branches.append((dkvn, kv_gain))
        post = (saved[l - 1]["ff"], gains[l - 1][3]) if l > 0 else None
        if l == n_a:
            dx, dgs, _, _ = _norm_bwd(dx, branches, sv["x_in"], None, f"norm_end_bwd_{l}")
            dx = _permute16(dx, True, "unpermute_stream")
            _, _, dff, dg_post = _norm_bwd(dx, [], None, post, "norm_boundary_bwd")
        else:
            dx, dgs, dff, dg_post = _norm_bwd(dx, branches, sv["x_in"], post, f"norm_end_bwd_{l}")
        if dff is None:
            send_grads(l, "mix", grads, [])
        else:
            dx, dff = send_grads(l, "mix", grads, [dx, dff])
        d_gains[l][0] = dgs[0]
        if l == n_a:
            d_kv_gain = dgs[1]
        if l > 0:
            d_gains[l - 1][3] = dg_post
    return loss, dx, d_gains, d_conv, d_kv_gain


BIG = (
    ("conv_in", 1), ("conv_out", 0), ("kv", 1), ("q", 0), ("o", 0), ("ffn_in", 1), ("ffn_out", 0))


def kernel(x, norm_g, conv_in_w, conv_w, conv_out_w, kv_norm_g, kv_w, q_w, o_w, ffn_in_w, ffn_out_w, loss_target, m_norm_g, m_conv_in_w, m_conv_w, m_conv_out_w, m_kv_norm_g, m_kv_w, m_q_w, m_o_w, m_ffn_in_w, m_ffn_out_w, v_norm_g, v_conv_in_w, v_conv_w, v_conv_out_w, v_kv_norm_g, v_kv_w, v_q_w, v_o_w, v_ffn_in_w, v_ffn_out_w):
    depth, _, dq = norm_g.shape
    d = 4 * dq
    n_a = conv_w.shape[0]
    big_w = {"conv_in": conv_in_w, "conv_out": conv_out_w, "kv": kv_w[None], "q": q_w, "o": o_w,
             "ffn_in": ffn_in_w, "ffn_out": ffn_out_w}
    big_m = {"conv_in": m_conv_in_w, "conv_out": m_conv_out_w, "kv": m_kv_w[None], "q": m_q_w, "o": m_o_w,
             "ffn_in": m_ffn_in_w, "ffn_out": m_ffn_out_w}
    big_v = {"conv_in": v_conv_in_w, "conv_out": v_conv_out_w, "kv": v_kv_w[None], "q": v_q_w, "o": v_o_w,
             "ffn_in": v_ffn_in_w, "ffn_out": v_ffn_out_w}

    n_gain, n_tap = depth * 4, n_a * conv_w.shape[1]
    small_rows = -(-(n_gain + n_tap + 1) // 8) * 8
    pad_rows = small_rows - n_gain - n_tap

    def pack_small(gains, taps):
        return jnp.concatenate([gains.reshape(n_gain, dq), taps.reshape(n_tap, dq), jnp.zeros((pad_rows, dq), F32)])

    axis_of = dict(BIG)

    def matrices_of(l, part):
        if part == "ffn":
            return [("ffn_in", l), ("ffn_out", l)]
        if l < n_a:
            return [("conv_in", l), ("conv_out", l)]
        return ([("kv", 0)] if l == n_a else []) + [("q", l - n_a), ("o", l - n_a)]

    halves = [(l, part) for l in range(depth) for part in ("mix", "ffn")]
    def placed(half, after):
        return [(*_cast_place(big_w[name], i, axis_of[name], BF16, after, f"place_{name}_{i}"), axis_of[name])
                for name, i in matrices_of(*half)]

    groups = {halves[0]: placed(halves[0], None)}
    groups[halves[0]].append((*_cast_place(pack_small(norm_g, conv_w)[None], 0, 1, F32, None, "place_small"), 1))
    started = {halves[0]: _gather_start(groups[halves[0]], [], "gather_start_0_mix")[0]}
    for half in halves[1:]:
        groups[half] = placed(half, started[halves[0]][2][0])

    AHEAD = 2

    def fetch(half, after):
        full = _gather_wait(groups[half], started[half], after, "gather_wait_%d_%s" % half)
        nxt = halves.index(half) + AHEAD
        if nxt < len(halves):
            started[halves[nxt]], full = _gather_start(groups[halves[nxt]], full, "gather_start_%d_%s" % halves[nxt])
        return full

    target = _permute16(loss_target.reshape(x.shape[1:]), False, "permute_target")
    for half in halves[1:AHEAD]:
        started[half], (target,) = _gather_start(groups[half], [target], "gather_start_%d_%s" % half)
    first = fetch(halves[0], target)
    small = first[-1]
    n_taps = conv_w.shape[1]
    gains = [[(small, 4 * l + i) for i in range(4)] for l in range(depth)]
    conv_ws = [(small, n_gain + n_taps * l, n_taps) for l in range(n_a)]
    kv_gain = (kv_norm_g[None], 0)

    def weights_of(l, part, after):
        full = first if (l, part) == halves[0] else fetch((l, part), after)
        return {name: full[t] for t, (name, _) in enumerate(matrices_of(l, part))}

    sent, paired = {}, {}
    LAG = 2

    def to_sibling(half, carry, after):
        axes = [axis_of[name] for name, _ in matrices_of(*half)]
        full, stacks = _scatter_wait(axes, sent[half], after, "scatter_wait_%d_%s" % half)
        paired[half], carry = _pair_start(list(zip(full, axes, stacks)), carry, "pair_start_%d_%s" % half)
        return carry

    def send_grads(l, part, grads, carry):
        sent[l, part], carry = _scatter_start(
            [(grads[name], axis_of[name]) for name, _ in matrices_of(l, part)], carry, f"scatter_start_{l}_{part}")
        at = halves.index((l, part))
        if carry:
            if at + LAG < len(halves):
                carry = to_sibling(halves[at + LAG], carry, carry[0])
        else:
            for older in range(at + LAG, at, -1):
                if halves[older] not in paired:
                    to_sibling(halves[older], [], sent[l, part][2][0])
        if at == 1 and carry:
            carry = to_sibling(halves[at + 1], carry, carry[0])
        return carry

    loss, dx, d_gains, d_conv, d_kv_gain = _local_step(
        x.reshape(x.shape[1:]), target, gains, conv_ws, kv_gain, weights_of, send_grads)
    loss = lax.psum(loss, ("x", "y", "c"))

    small_g = jnp.concatenate([dg for row in d_gains for dg in row] + list(d_conv) + [d_kv_gain]
                              + [jnp.zeros((pad_rows - 1, d), F32)])
    small_g = _allreduce_small(small_g, "allreduce_small")
    blk = 2 * lax.axis_index("x") + lax.axis_index("y")
    mine_small = lax.dynamic_slice_in_dim(small_g, blk * dq, dq, axis=1)
    kv_rows = d // dq

    def pack_opt(gains_like, taps_like, kv_like):
        rows = jnp.concatenate([gains_like.reshape(n_gain, dq), taps_like.reshape(n_tap, dq), kv_like.reshape(kv_rows, dq)])
        extra = -rows.shape[0] % 8
        return jnp.concatenate([rows, jnp.zeros((extra, dq), F32)]) if extra else rows

    sw = pack_opt(norm_g, conv_w, kv_norm_g)
    sm = pack_opt(m_norm_g, m_conv_w, m_kv_norm_g)
    sv = pack_opt(v_norm_g, v_conv_w, v_kv_norm_g)
    sg = pack_opt(mine_small[:n_gain], mine_small[n_gain:n_gain + n_tap], small_g[n_gain + n_tap])
    s_out = _adamw(sw, sm, sv, sg, "adamw_small")

    def unpack(a):
        return (a[:n_gain].reshape(depth, 4, dq), a[n_gain:n_gain + n_tap].reshape(n_a, -1, dq),
                a[n_gain + n_tap:n_gain + n_tap + kv_rows].reshape(d))

    small_out = [unpack(a) for a in s_out]

    landed, big_out = {}, {}

    def update(name):
        for half in halves:
            if matrices_of(*half)[0] not in landed and any(n == name for n, _ in matrices_of(*half)):
                axes = [axis_of[n] for n, _ in matrices_of(*half)]
                landed.update(zip(matrices_of(*half), zip(*_pair_wait(axes, paired[half], "pair_wait_%d_%s" % half))))
        shp, ax = big_w[name].shape, axis_of[name]
        rows, cols = shp[0] * shp[1], shp[2]
        flat = lambda a: a.reshape(rows, cols)
        full, stacks, sibling = zip(*[landed[name, i] for i in range(shp[0])])
        res = _adamw(flat(big_w[name]), flat(big_m[name]), flat(big_v[name]),
                     [(full[i], ax, stacks[i], sibling[i]) for i in range(shp[0])], f"adamw_{name}")
        big_out[name] = [a.reshape(shp[1:] if name == "kv" else shp) for a in res]

    pending = [half for half in reversed(halves) if half not in paired]
    for half in pending[:-1]:
        to_sibling(half, [], None)
    late = [name for name, _ in BIG if any(n == name for n, _ in matrices_of(*pending[-1]))]
    for name, _ in BIG:
        if name not in late:
            update(name)
    to_sibling(pending[-1], [], big_out["ffn_out"][0])
    for name in late:
        update(name)

    def leaves(i):
        ng, cw_, kg = small_out[i]
        return [ng, big_out["conv_in"][i], cw_, big_out["conv_out"][i], kg, big_out["kv"][i], big_out["q"][i],
                big_out["o"][i], big_out["ffn_in"][i], big_out["ffn_out"][i]]

    return (loss, dx.reshape(x.shape), *leaves(0), *leaves(1), *leaves(2), *leaves(3))
```

```python
import jax
import jax.numpy as jnp
import numpy as np
from jax import lax
from jax.experimental import pallas as pl
from jax.experimental.pallas import tpu as pltpu

F32 = jnp.float32
BF16 = jnp.bfloat16
HEAD_DIM = 64
DILATIONS = (1, 4, 16)
NORM_EPS = 1e-6
NEG_BIG = -1e30
VMEM_LIMIT = 48 * 1024 * 1024
ROW_TILE = 256
NORM_TILE = 512
LANE = 128
MESH = pl.DeviceIdType.MESH

ADAM_LR = 0.001
ADAM_B1 = 0.9
ADAM_B2 = 0.999
ADAM_EPS = 1e-08
ADAM_WD = 0.01
ADAM_STEP = 10

TILE_CANDIDATES = (1024, 1408, 768, 512, 384, 256, 128)


def _pick(dim, cands=TILE_CANDIDATES):
    for c in cands:
        if c <= dim and dim % c == 0:
            return c
    return dim


def _params(sem):
    return pltpu.CompilerParams(dimension_semantics=sem, vmem_limit_bytes=VMEM_LIMIT)


def _mm(a, b, mode, out_dtype, name, scale=None):
    a_planes = a.shape[0] if a.ndim == 3 else 1
    b_planes = b.shape[0] if b.ndim == 3 else 1
    if mode == "nn":
        m, k = a.shape[-2], a.shape[-1] * a_planes
        n = b.shape[1]
    elif mode == "nt":
        m, k = a.shape[-2], a.shape[-1] * a_planes
        n = b.shape[0]
    else:
        k, m = a.shape
        n = b.shape[-1] * b_planes
    tm, tn = _pick(m), _pick(n // b_planes)
    tk = _pick(k // a_planes, ((2048,) if mode == "tn" else (3072, 2816)) + TILE_CANDIDATES)
    nk = k // tk
    ka, nb = k // a_planes // tk, n // b_planes // tn
    if a_planes > 1:
        a_spec = pl.BlockSpec((None, tm, tk), lambda i, j, kk: (kk // ka, i, kk % ka))
    elif mode == "tn":
        a_spec = pl.BlockSpec((tk, tm), lambda i, j, kk: (kk, i))
    else:
        a_spec = pl.BlockSpec((tm, tk), lambda i, j, kk: (i, kk))
    if mode == "nn":
        b_spec = pl.BlockSpec((tk, tn), lambda i, j, kk: (kk, j))
        dims = (((1,), (0,)), ((), ()))
    elif mode == "nt":
        b_spec = pl.BlockSpec((tn, tk), lambda i, j, kk: (j, kk))
        dims = (((1,), (1,)), ((), ()))
    else:
        b_spec = (pl.BlockSpec((None, tk, tn), lambda i, j, kk: (j // nb, kk, j % nb)) if b_planes > 1
                  else pl.BlockSpec((tk, tn), lambda i, j, kk: (kk, j)))
        dims = (((0,), (0,)), ((), ()))

    def finish(acc):
        if scale is not None:
            acc = acc * scale
        return acc.astype(out_dtype)

    if nk == 1:
        def body(a_ref, b_ref, o_ref):
            o_ref[...] = finish(lax.dot_general(a_ref[...].astype(BF16), b_ref[...].astype(BF16), dims, preferred_element_type=F32))
        scratch = []
    else:
        def body(a_ref, b_ref, o_ref, acc_ref):
            kk = pl.program_id(2)

            @pl.when(kk == 0)
            def _():
                acc_ref[...] = jnp.zeros_like(acc_ref)

            acc_ref[...] += lax.dot_general(a_ref[...].astype(BF16), b_ref[...].astype(BF16), dims, preferred_element_type=F32)

            @pl.when(kk == nk - 1)
            def _():
                o_ref[...] = finish(acc_ref[...])
        scratch = [pltpu.VMEM((tm, tn), F32)]

    return pl.pallas_call(
        body, name=name,
        grid=(m // tm, n // tn, nk),
        in_specs=[a_spec, b_spec],
        out_specs=pl.BlockSpec((tm, tn), lambda i, j, kk: (i, j)),
        out_shape=jax.ShapeDtypeStruct((m, n), out_dtype),
        scratch_shapes=scratch,
        compiler_params=_params(("parallel", "parallel", "arbitrary")),
    )(a, b)


def _rstd(v):
    return lax.rsqrt(jnp.mean(v * v, axis=-1, keepdims=True) + NORM_EPS)


def _rms_bwd(dy, v, g, r):
    xhat = v * r
    gy = dy * g
    dv = r * (gy - xhat * jnp.mean(gy * xhat, axis=-1, keepdims=True))
    return dv, dy * xhat


def _row_spec(t, width):
    return pl.BlockSpec((t, width), lambda i: (i, 0))


def _gain_spec(width):
    return pl.BlockSpec((1, width), lambda i: (0, 0))


def _norm_res_fwd(x, mix, g_post, pre_gains, name):
    s, d = x.shape
    t = _pick(s, (NORM_TILE, ROW_TILE))
    has_mix = mix is not None
    n_pre = len(pre_gains)

    def body(*refs):
        x_ref = refs[0]
        pos = 1
        x1 = x_ref[...]
        if has_mix:
            mv = refs[1][...].astype(F32)
            x1 = x1 + mv * _rstd(mv) * refs[2][...]
            pos = 3
        gains = refs[pos:pos + n_pre]
        outs = refs[pos + n_pre:]
        if has_mix:
            outs[0][...] = x1
            outs = outs[1:]
        r = _rstd(x1)
        for g_ref, o_ref in zip(gains, outs):
            o_ref[...] = (x1 * r * g_ref[...]).astype(BF16)

    ins = [x] + ([mix, g_post] if has_mix else []) + list(pre_gains)
    in_specs = [_row_spec(t, d)] + ([_row_spec(t, d), _gain_spec(d)] if has_mix else []) + [_gain_spec(d)] * n_pre
    out_shape = ([jax.ShapeDtypeStruct((s, d), F32)] if has_mix else []) + [jax.ShapeDtypeStruct((s, d), BF16)] * n_pre
    out_specs = [_row_spec(t, d)] * len(out_shape)
    res = pl.pallas_call(
        body, name=name, grid=(s // t,), in_specs=in_specs, out_specs=out_specs, out_shape=out_shape,
        compiler_params=_params(("parallel",)),
    )(*ins)
    if has_mix:
        return res[0], list(res[1:])
    return x, list(res)


def _mm_norm_res(a, b, x, g_post, pre_gain, name):
    m, k = a.shape
    d = b.shape[1]
    tm, tk = _pick(m, (1024, 512, 256)), _pick(k)
    nk = k // tk

    def body(a_ref, b_ref, x_ref, gp_ref, g_ref, mix_ref, x1_ref, xn_ref, acc_ref):
        kk = pl.program_id(1)

        @pl.when(kk == 0)
        def _():
            acc_ref[...] = jnp.zeros_like(acc_ref)

        acc_ref[...] += jnp.dot(a_ref[...].astype(BF16), b_ref[...], preferred_element_type=F32)

        @pl.when(kk == nk - 1)
        def _():
            mix = acc_ref[...].astype(BF16)
            mix_ref[...] = mix
            mv = mix.astype(F32)
            x1 = x_ref[...] + mv * _rstd(mv) * gp_ref[...]
            x1_ref[...] = x1
            xn_ref[...] = (x1 * _rstd(x1) * g_ref[...]).astype(BF16)

    rows = pl.BlockSpec((tm, d), lambda i, kk: (i, 0))
    gain = pl.BlockSpec((1, d), lambda i, kk: (0, 0))
    return pl.pallas_call(
        body, name=name, grid=(m // tm, nk),
        in_specs=[pl.BlockSpec((tm, tk), lambda i, kk: (i, kk)), pl.BlockSpec((tk, d), lambda i, kk: (kk, 0)),
                  rows, gain, gain],
        out_specs=[rows, rows, rows],
        out_shape=[jax.ShapeDtypeStruct((m, d), BF16), jax.ShapeDtypeStruct((m, d), F32),
                   jax.ShapeDtypeStruct((m, d), BF16)],
        scratch_shapes=[pltpu.VMEM((tm, d), F32)],
        compiler_params=_params(("parallel", "arbitrary")),
    )(a, b, x, g_post, pre_gain)


def _norm_res_loss(x, mix, g_post, target, name):
    s, d = x.shape
    t = _pick(s, (NORM_TILE, ROW_TILE))

    def body(x_ref, m_ref, g_ref, t_ref, dy_ref, loss_ref):
        mv = m_ref[...].astype(F32)
        y = x_ref[...] + mv * _rstd(mv) * g_ref[...]
        err = y - t_ref[...]
        dy_ref[...] = err * (1.0 / d)

        @pl.when(pl.program_id(0) == 0)
        def _():
            loss_ref[...] = jnp.zeros_like(loss_ref)

        loss_ref[...] += jnp.sum(err * err)

    dy, acc = pl.pallas_call(
        body, name=name, grid=(s // t,),
        in_specs=[_row_spec(t, d), _row_spec(t, d), _gain_spec(d), _row_spec(t, d)],
        out_specs=[_row_spec(t, d), pl.BlockSpec((8, LANE), lambda i: (0, 0))],
        out_shape=[jax.ShapeDtypeStruct((s, d), F32), jax.ShapeDtypeStruct((8, LANE), F32)],
        compiler_params=_params(("arbitrary",)),
    )(x, mix, g_post, target)
    return dy, acc[0, 0] * (0.5 / d)


def _norm_bwd(dx_out, branches, x_in, post, name):
    s, d = dx_out.shape
    t = _pick(s, (NORM_TILE, ROW_TILE))
    nb = len(branches)
    has_post = post is not None

    def body(*refs):
        dx_ref = refs[0]
        pos = 1
        dx = dx_ref[...]
        first = pl.program_id(0) == 0
        n_in = 1 + (1 + 2 * nb if nb else 0) + (2 if has_post else 0)
        outs = refs[n_in:]
        opos = 0
        if nb:
            xv = refs[pos][...]
            pos += 1
            r = _rstd(xv)
            dx_o = outs[0]
            opos = 1
            for _ in range(nb):
                dxn = refs[pos][...].astype(F32)
                g = refs[pos + 1][...]
                pos += 2
                dv, dg_rows = _rms_bwd(dxn, xv, g, r)
                dx = dx + dv
                dg_ref = outs[opos]
                opos += 1

                @pl.when(first)
                def _(dg_ref=dg_ref):
                    dg_ref[...] = jnp.zeros_like(dg_ref)

                dg_ref[...] += jnp.sum(dg_rows, axis=0, keepdims=True)
            dx_o[...] = dx
        if has_post:
            mv = refs[pos][...].astype(F32)
            g = refs[pos + 1][...]
            dm, dg_rows = _rms_bwd(dx, mv, g, _rstd(mv))
            outs[opos][...] = dm.astype(BF16)
            dg_ref = outs[opos + 1]

            @pl.when(first)
            def _():
                dg_ref[...] = jnp.zeros_like(dg_ref)

            dg_ref[...] += jnp.sum(dg_rows, axis=0, keepdims=True)

    ins, in_specs = [dx_out], [_row_spec(t, d)]
    out_shape, out_specs = [], []
    if nb:
        ins.append(x_in)
        in_specs.append(_row_spec(t, d))
        out_shape.append(jax.ShapeDtypeStruct((s, d), F32))
        out_specs.append(_row_spec(t, d))
        for dxn, g in branches:
            ins += [dxn, g]
            in_specs += [_row_spec(t, d), _gain_spec(d)]
            out_shape.append(jax.ShapeDtypeStruct((1, d), F32))
            out_specs.append(_gain_spec(d))
    if has_post:
        ins += [post[0], post[1]]
        in_specs += [_row_spec(t, d), _gain_spec(d)]
        out_shape += [jax.ShapeDtypeStruct((s, d), BF16), jax.ShapeDtypeStruct((1, d), F32)]
        out_specs += [_row_spec(t, d), _gain_spec(d)]
    res = pl.pallas_call(
        body, name=name, grid=(s // t,), in_specs=in_specs, out_specs=out_specs, out_shape=out_shape,
        compiler_params=_params(("arbitrary",)),
    )(*ins)
    res = list(res)
    dx_in = res.pop(0) if nb else dx_out
    dgs = [res.pop(0) for _ in range(nb)]
    dm, dg_post = (res[0], res[1]) if has_post else (None, None)
    return dx_in, dgs, dm, dg_post


HALO = 16


def _shift_down(u, prev, k):
    rows = lax.broadcasted_iota(jnp.int32, u.shape, 0)
    out = pltpu.roll(u, k, 0)
    for i in range(k):
        out = jnp.where(rows == i, prev[HALO - k + i:HALO - k + i + 1, :], out)
    return out


def _shift_up(u, nxt, k):
    n = u.shape[0]
    rows = lax.broadcasted_iota(jnp.int32, u.shape, 0)
    out = pltpu.roll(u, n - k, 0)
    for i in range(k):
        out = jnp.where(rows == n - k + i, nxt[i:i + 1, :], out)
    return out


def _conv_gate_fwd(p, cw, name):
    s, d3 = p.shape
    d = d3 // 3
    t = _pick(s, (ROW_TILE,))
    hb = t // HALO

    def body(p_ref, prev_ref, w_ref, z_ref):
        i = pl.program_id(0)
        pv = p_ref[...].astype(F32)
        b, u = pv[:, :d], pv[:, d:2 * d] * pv[:, 2 * d:]
        ph = prev_ref[...].astype(F32)
        up = jnp.where(i > 0, ph[:, d:2 * d] * ph[:, 2 * d:], 0.0)
        w = w_ref[...]
        y = w[0:1, :] * _shift_down(u, up, 2) + w[1:2, :] * _shift_down(u, up, 1) + w[2:3, :] * u
        z_ref[...] = (b * y).astype(BF16)

    return pl.pallas_call(
        body, name=name, grid=(s // t,),
        in_specs=[_row_spec(t, d3),
                  pl.BlockSpec((HALO, d3), lambda i: (jnp.maximum(i * hb - 1, 0), 0)),
                  pl.BlockSpec((3, d), lambda i: (0, 0))],
        out_specs=_row_spec(t, d),
        out_shape=jax.ShapeDtypeStruct((s, d), BF16),
        compiler_params=_params(("parallel",)),
    )(p, p, cw)


def _conv_gate_bwd(p, dz, cw, name):
    s, d3 = p.shape
    d = d3 // 3
    t = _pick(s, (ROW_TILE,))
    hb = t // HALO
    nt = s // t
    last_halo = s // HALO - 1

    def body(p_ref, prev_ref, next_ref, dz_ref, dznext_ref, w_ref, dp_ref, dw_ref):
        i = pl.program_id(0)
        pv = p_ref[...].astype(F32)
        b, c, h = pv[:, :d], pv[:, d:2 * d], pv[:, 2 * d:]
        u = c * h
        ph = prev_ref[...].astype(F32)
        up = jnp.where(i > 0, ph[:, d:2 * d] * ph[:, 2 * d:], 0.0)
        w = w_ref[...]
        u1, u2 = _shift_down(u, up, 1), _shift_down(u, up, 2)
        y = w[0:1, :] * u2 + w[1:2, :] * u1 + w[2:3, :] * u
        dz = dz_ref[...].astype(F32)
        dy = dz * b
        dyn = jnp.where(i < nt - 1, dznext_ref[...].astype(F32) * next_ref[...].astype(F32)[:, :d], 0.0)
        du = w[2:3, :] * dy + w[1:2, :] * _shift_up(dy, dyn, 1) + w[0:1, :] * _shift_up(dy, dyn, 2)
        dp_ref[:, :d] = (dz * y).astype(BF16)
        dp_ref[:, d:2 * d] = (du * h).astype(BF16)
        dp_ref[:, 2 * d:] = (du * c).astype(BF16)

        @pl.when(i == 0)
        def _():
            dw_ref[...] = jnp.zeros_like(dw_ref)

        dw_ref[0:1, :] += jnp.sum(dy * u2, axis=0, keepdims=True)
        dw_ref[1:2, :] += jnp.sum(dy * u1, axis=0, keepdims=True)
        dw_ref[2:3, :] += jnp.sum(dy * u, axis=0, keepdims=True)

    return pl.pallas_call(
        body, name=name, grid=(nt,),
        in_specs=[_row_spec(t, d3),
                  pl.BlockSpec((HALO, d3), lambda i: (jnp.maximum(i * hb - 1, 0), 0)),
                  pl.BlockSpec((HALO, d3), lambda i: (jnp.minimum((i + 1) * hb, last_halo), 0)),
                  _row_spec(t, d),
                  pl.BlockSpec((HALO, d), lambda i: (jnp.minimum((i + 1) * hb, last_halo), 0)),
                  pl.BlockSpec((3, d), lambda i: (0, 0))],
        out_specs=[_row_spec(t, d3), pl.BlockSpec((3, d), lambda i: (0, 0))],
        out_shape=[jax.ShapeDtypeStruct((s, d3), BF16), jax.ShapeDtypeStruct((3, d), F32)],
        compiler_params=_params(("arbitrary",)),
    )(p, p, p, dz, dz, cw)


FFN_ROWS, FFN_COLS = (1024, 512, 256), (1408, 768, 256, 128)


def _row_chunks(tm, rows=256):
    return [slice(r, r + min(rows, tm)) for r in range(0, tm, min(rows, tm))]


def _ffn_in_swiglu(xn, w_in, name):
    s, k = xn.shape
    ff = w_in.shape[1] // 2
    tm, tn = _pick(s, FFN_ROWS), _pick(ff, FFN_COLS)
    nj = ff // tn

    def body(x_ref, wg_ref, wu_ref, f_ref, a_ref):
        for rows in _row_chunks(tm):
            xv = x_ref[rows, :]
            gate = jnp.dot(xv, wg_ref[...], preferred_element_type=F32)
            up = jnp.dot(xv, wu_ref[...], preferred_element_type=F32)
            f_ref[0, rows, :] = gate.astype(BF16)
            f_ref[1, rows, :] = up.astype(BF16)
            a_ref[rows, :] = (gate * jax.nn.sigmoid(gate) * up).astype(BF16)

    return pl.pallas_call(
        body, name=name, grid=(nj, s // tm),
        in_specs=[pl.BlockSpec((tm, k), lambda j, i: (i, 0)),
                  pl.BlockSpec((k, tn), lambda j, i: (0, j)),
                  pl.BlockSpec((k, tn), lambda j, i: (0, nj + j))],
        out_specs=[pl.BlockSpec((2, tm, tn), lambda j, i: (0, i, j)), pl.BlockSpec((tm, tn), lambda j, i: (i, j))],
        out_shape=[jax.ShapeDtypeStruct((2, s, ff), BF16), jax.ShapeDtypeStruct((s, ff), BF16)],
        compiler_params=_params(("parallel", "parallel")),
    )(xn, w_in, w_in)


def _ffn_out_dx_swiglu(dff, w_out, f, name):
    s, d = dff.shape
    ff = w_out.shape[0]
    tm, tn = _pick(s, FFN_ROWS), _pick(ff, FFN_COLS)

    def body(d_ref, w_ref, f_ref, df_ref):
        for rows in _row_chunks(tm):
            da = lax.dot_general(d_ref[rows, :], w_ref[...], (((1,), (1,)), ((), ())), preferred_element_type=F32)
            gate = f_ref[0, rows, :].astype(F32)
            up = f_ref[1, rows, :].astype(F32)
            sg = jax.nn.sigmoid(gate)
            silu = gate * sg
            df_ref[0, rows, :] = (da * up * (sg + silu * (1.0 - sg))).astype(BF16)
            df_ref[1, rows, :] = (da * silu).astype(BF16)

    planes = pl.BlockSpec((2, tm, tn), lambda j, i: (0, i, j))
    return pl.pallas_call(
        body, name=name, grid=(ff // tn, s // tm),
        in_specs=[pl.BlockSpec((tm, d), lambda j, i: (i, 0)), pl.BlockSpec((tn, d), lambda j, i: (j, 0)), planes],
        out_specs=planes, out_shape=jax.ShapeDtypeStruct((2, s, ff), BF16),
        compiler_params=_params(("parallel", "parallel")),
    )(dff, w_out, f)


SUPER = 2048
RES = 16
PAIR = 128
L = 128
FWD_TOGETHER = 16
BWD_TOGETHER = 8


def _alibi_slopes(n_heads):
    h = np.arange(n_heads, dtype=np.float32) + 1.0
    return np.power(2.0, -8.0 * h / n_heads).astype(np.float32)


def _permute16(x, inverse, name):
    s, d = x.shape
    cw = LANE

    def body(x_ref, o_ref):
        if inverse:
            for m in range(L):
                o_ref[RES * m:RES * (m + 1), :] = x_ref[pl.ds(m, RES, stride=L), :]
        else:
            for r in range(RES):
                o_ref[L * r:L * (r + 1), :] = x_ref[pl.ds(r, L, stride=RES), :]

    spec = pl.BlockSpec((SUPER, cw), lambda i, j: (i, j))
    return pl.pallas_call(
        body, name=name, grid=(s // SUPER, d // cw), in_specs=[spec], out_specs=spec,
        out_shape=jax.ShapeDtypeStruct((s, d), x.dtype),
        compiler_params=_params(("parallel", "parallel")),
    )(x)


def _slope_table(d):
    nh = d // HEAD_DIM
    sl = _alibi_slopes(nh)
    tab = np.repeat(sl, HEAD_DIM).reshape(d // PAIR, 1, PAIR)
    return jnp.asarray(np.broadcast_to(tab, (d // PAIR, 8, PAIR)).copy())


def _geometry(dil):
    nch = RES // dil
    return nch, L // nch


def _band(dil):
    nch, w = _geometry(dil)
    sh = w.bit_length() - 1
    i = lax.broadcasted_iota(jnp.int32, (L, 2 * L), 0)
    j = lax.broadcasted_iota(jnp.int32, (L, 2 * L), 1)

    def pos(t):
        return jnp.bitwise_and(t, w - 1) * nch + jnp.right_shift(t, sh)

    delta = pos(i) + L - (pos(jnp.bitwise_and(j, L - 1)) + jnp.bitwise_and(j, L))
    return (delta * dil).astype(F32), (delta >= 0) & (delta <= L), j < L


def _fill_bias(bias_s, sl_ref):
    for b, dil in enumerate(DILATIONS):
        base, band, prev_half = _band(dil)
        for first in range(2):
            valid = band & jnp.logical_not(prev_half) if first else band
            for h in range(2):
                slope = sl_ref[0:1, HEAD_DIM * h:HEAD_DIM * h + 1]
                bias_s[(2 * b + first) * 2 + h] = jnp.where(valid, -slope * base, NEG_BIG)


def _bias_index(b, sb, n):
    first = jnp.logical_and(sb == 0, n == 0).astype(jnp.int32)
    return (2 * b + first) * 2


def _offsets(dil, res, n):
    nch, w = _geometry(dil)

    def al(v):
        return v if isinstance(v, int) else pl.multiple_of(v, w)

    q_off = [al((a * dil + res) * L + n * w) for a in range(nch)]
    k_off = [al((a * dil + res) * 2 * L + L + n * w) for a in range(nch)]
    kp_off = [al((a * dil + res) * 2 * L + L + n * w - w) for a in range(nch)]
    return q_off, k_off, kp_off, w


def _gather(ref, offs, w):
    parts = [ref[pl.ds(o, w), :] for o in offs]
    return parts[0] if len(parts) == 1 else jnp.concatenate(parts, axis=0)


def _scatter(ref, offs, w, val, add=False):
    for a, o in enumerate(offs):
        piece = val[a * w:(a + 1) * w, :]
        if add:
            ref[pl.ds(o, w), :] += piece
        else:
            ref[pl.ds(o, w), :] = piece


def _fill_key_buffer(buf, prev_ref, cur_ref):
    for r in range(RES):
        buf[2 * L * r:2 * L * r + L, :] = prev_ref[L * r:L * (r + 1), :]
        buf[2 * L * r + L:2 * L * (r + 1), :] = cur_ref[L * r:L * (r + 1), :]


def _two_heads(x, low):
    zero = jnp.zeros_like(x)
    return jnp.concatenate([jnp.where(low, x, zero), jnp.where(low, zero, x)], axis=0)


def _loop_blocks(dil, stages, together):
    together = max(together, dil) if dil < RES else together

    def it(i, c):
        if dil == RES:
            blocks = [(i * together + k, 0) for k in range(together)]
        else:
            blocks = [(res, i * (together // dil) + k) for k in range(together // dil) for res in range(dil)]
        state = [stages[0](res, n) for res, n in blocks]
        for stage in stages[1:]:
            state = [stage(res, n, prev) for (res, n), prev in zip(blocks, state)]
        for writes in state:
            for args in writes:
                _scatter(*args)
        return c

    lax.fori_loop(0, RES // together, it, 0)


NT = (((1,), (1,)), ((), ()))
TN = (((0,), (0,)), ((), ()))


def _attention_fwd(q, kv, name):
    s, d = q.shape
    g_n, ns = d // PAIR, s // SUPER

    def body(sl_ref, q_ref, kc_ref, kp_ref, vc_ref, vp_ref, o_ref, lse_ref, kbuf, vbuf, m_s, l_s, acc_s, bias_s):
        sb = pl.program_id(1)
        _fill_key_buffer(kbuf, kp_ref, kc_ref)
        _fill_key_buffer(vbuf, vp_ref, vc_ref)
        pl.when(sb == 0)(lambda: _fill_bias(bias_s, sl_ref))
        low = lax.broadcasted_iota(jnp.int32, (L, PAIR), 1) < HEAD_DIM
        low_k = lax.broadcasted_iota(jnp.int32, (2 * L, PAIR), 1) < HEAD_DIM
        ones_bd = _two_heads(jnp.ones((2 * L, PAIR), BF16), low_k)
        for bi, dil in enumerate(DILATIONS):
            first_branch, last_branch = bi == 0, bi == len(DILATIONS) - 1

            def scores(res, n, dil=dil):
                q_off, k_off, kp_off, w = _offsets(dil, res, n)
                qf = _gather(q_ref, q_off, w).astype(BF16)
                kcat = jnp.concatenate([_gather(kbuf, kp_off, w), _gather(kbuf, k_off, w)], axis=0).astype(BF16)
                return lax.dot_general(qf, _two_heads(kcat, low_k), NT, preferred_element_type=F32)

            def update(res, n, sc, bi=bi, dil=dil, first_branch=first_branch, last_branch=last_branch):
                q_off, k_off, kp_off, w = _offsets(dil, res, n)
                vcat = jnp.concatenate([_gather(vbuf, kp_off, w), _gather(vbuf, k_off, w)], axis=0).astype(BF16)
                v_ones = jnp.concatenate([_two_heads(vcat, low_k), ones_bd], axis=1)
                bias_at = _bias_index(bi, sb, n)
                if not first_branch:
                    m_prev = _gather(m_s, q_off, w)
                ps, m_new = [], []
                for h in range(2):
                    s_h = sc[:, 2 * L * h:2 * L * (h + 1)] + bias_s[bias_at + h]
                    mh = jnp.max(s_h, axis=1, keepdims=True)
                    if not first_branch:
                        mh = jnp.maximum(mh, m_prev[:, HEAD_DIM * h:HEAD_DIM * h + 1])
                    ps.append(jnp.exp(s_h - mh).astype(BF16))
                    m_new.append(mh)
                m_full = jnp.where(low, m_new[0], m_new[1])
                both = jnp.dot(jnp.concatenate(ps, axis=1), v_ones, preferred_element_type=F32)
                acc, l_full = both[:, :PAIR], both[:, PAIR:]
                if not first_branch:
                    alpha = jnp.exp(m_prev - m_full)
                    l_full = _gather(l_s, q_off, w) * alpha + l_full
                    acc = _gather(acc_s, q_off, w) * alpha + acc
                if last_branch:
                    return [(o_ref, q_off, w, acc / l_full, False), (lse_ref, q_off, w, m_full + jnp.log(l_full), False)]
                return [(m_s, q_off, w, m_full, False), (l_s, q_off, w, l_full, False), (acc_s, q_off, w, acc, False)]

            _loop_blocks(dil, [scores, update], FWD_TOGETHER)

    prev = lambda i: jnp.maximum(i - 1, 0)
    blk = pl.BlockSpec((SUPER, PAIR), lambda g, i: (i, g))
    in_specs = [pl.BlockSpec((None, 8, PAIR), lambda g, i: (g, 0, 0)), blk,
                pl.BlockSpec((SUPER, PAIR), lambda g, i: (i, g)),
                pl.BlockSpec((SUPER, PAIR), lambda g, i: (prev(i), g)),
                pl.BlockSpec((SUPER, PAIR), lambda g, i: (i, g_n + g)),
                pl.BlockSpec((SUPER, PAIR), lambda g, i: (prev(i), g_n + g))]
    return pl.pallas_call(
        body, name=name, grid=(g_n, ns), in_specs=in_specs, out_specs=[blk, blk],
        out_shape=[jax.ShapeDtypeStruct((s, d), F32)] * 2,
        scratch_shapes=([pltpu.VMEM((2 * SUPER, PAIR), F32)] * 2 + [pltpu.VMEM((SUPER, PAIR), F32)] * 3
                        + [pltpu.VMEM((4 * len(DILATIONS), L, 2 * L), F32)]),
        compiler_params=_params(("parallel", "arbitrary")),
    )(_slope_table(d), q, kv, kv, kv, kv)


def _attention_bwd(q, kv, o, do, lse, dkv_in, name):
    s, d = q.shape
    g_n, ns = d // PAIR, s // SUPER
    has_in = dkv_in is not None

    def body(*refs):
        sl_ref, q_ref, do_ref, o_ref, lse_ref, kc_ref, kp_ref, vc_ref, vp_ref = refs[:9]
        pos = 9
        if has_in:
            dkv_in_ref = refs[9]
            pos = 10
        dq_ref, dkv_ref, kbuf, vbuf, dkbuf, dvbuf, dq_s, bias_s = refs[pos:]
        step = pl.program_id(1)
        sb = ns - 1 - step
        _fill_key_buffer(kbuf, kp_ref, kc_ref)
        _fill_key_buffer(vbuf, vp_ref, vc_ref)

        @pl.when(step == 0)
        def _():
            _fill_bias(bias_s, sl_ref)
            dkbuf[...] = jnp.zeros_like(dkbuf)
            dvbuf[...] = jnp.zeros_like(dvbuf)

        @pl.when(step > 0)
        def _():
            for buf in (dkbuf, dvbuf):
                for r in range(RES):
                    buf[2 * L * r + L:2 * L * (r + 1), :] = buf[2 * L * r:2 * L * r + L, :]
                    buf[2 * L * r:2 * L * r + L, :] = jnp.zeros((L, PAIR), F32)

        low = lax.broadcasted_iota(jnp.int32, (L, PAIR), 1) < HEAD_DIM
        low_k = lax.broadcasted_iota(jnp.int32, (2 * L, PAIR), 1) < HEAD_DIM
        low_t = lax.broadcasted_iota(jnp.int32, (PAIR, 2 * L), 0) < HEAD_DIM
        for bi, dil in enumerate(DILATIONS):
            first_branch = bi == 0

            def scores(res, n, dil=dil):
                q_off, k_off, kp_off, w = _offsets(dil, res, n)
                qb = _gather(q_ref, q_off, w).astype(BF16)
                dof = _gather(do_ref, q_off, w)
                prod = dof * _gather(o_ref, q_off, w)
                dob = dof.astype(BF16)
                lse_f = _gather(lse_ref, q_off, w)
                zero = jnp.zeros_like(prod)
                dsum = (jnp.sum(jnp.where(low, prod, zero), axis=1, keepdims=True),
                        jnp.sum(jnp.where(low, zero, prod), axis=1, keepdims=True))
                kcat = jnp.concatenate([_gather(kbuf, kp_off, w), _gather(kbuf, k_off, w)], axis=0).astype(BF16)
                vcat = jnp.concatenate([_gather(vbuf, kp_off, w), _gather(vbuf, k_off, w)], axis=0).astype(BF16)
                k_bd, v_bd = _two_heads(kcat, low_k), _two_heads(vcat, low_k)
                sc = lax.dot_general(qb, k_bd, NT, preferred_element_type=F32)
                dp = lax.dot_general(dob, v_bd, NT, preferred_element_type=F32)
                return qb, dob, lse_f, dsum, k_bd, sc, dp

            def gradients(res, n, given, bi=bi, dil=dil, first_branch=first_branch):
                qb, dob, lse_f, dsum, k_bd, sc, dp = given
                q_off, k_off, kp_off, w = _offsets(dil, res, n)
                bias_at = _bias_index(bi, sb, n)
                ps, dss = [], []
                for h in range(2):
                    cols = slice(2 * L * h, 2 * L * (h + 1))
                    lse_h = lse_f[:, HEAD_DIM * h:HEAD_DIM * h + 1]
                    p_h = jnp.exp(sc[:, cols] + bias_s[bias_at + h] - lse_h)
                    dss.append((p_h * (dp[:, cols] - dsum[h])).astype(BF16))
                    ps.append(p_h.astype(BF16))
                ds_cat, p_cat = jnp.concatenate(dss, axis=1), jnp.concatenate(ps, axis=1)
                dq = jnp.dot(ds_cat, k_bd, preferred_element_type=F32)
                dk_t = lax.dot_general(qb, ds_cat, TN, preferred_element_type=F32)
                dv_t = lax.dot_general(dob, p_cat, TN, preferred_element_type=F32)
                dk = jnp.where(low_t, dk_t[:, :2 * L], dk_t[:, 2 * L:]).T
                dv = jnp.where(low_t, dv_t[:, :2 * L], dv_t[:, 2 * L:]).T
                return [(dq_s, q_off, w, dq, not first_branch),
                        (dkbuf, kp_off, w, dk[:L], True), (dkbuf, k_off, w, dk[L:], True),
                        (dvbuf, kp_off, w, dv[:L], True), (dvbuf, k_off, w, dv[L:], True)]

            _loop_blocks(dil, [scores, gradients], BWD_TOGETHER)

        dq_ref[...] = dq_s[...].astype(BF16)
        for r in range(RES):
            rows, cur = slice(L * r, L * (r + 1)), slice(2 * L * r + L, 2 * L * (r + 1))
            for plane, buf in enumerate((dkbuf, dvbuf)):
                if has_in:
                    dkv_ref[plane, rows, :] = buf[cur, :] + dkv_in_ref[plane, rows, :]
                else:
                    dkv_ref[plane, rows, :] = buf[cur, :]

    rev = lambda i: ns - 1 - i
    prev = lambda i: jnp.maximum(ns - 2 - i, 0)
    blk = pl.BlockSpec((SUPER, PAIR), lambda g, i: (rev(i), g))
    in_specs = [pl.BlockSpec((None, 8, PAIR), lambda g, i: (g, 0, 0)), blk, blk, blk, blk,
                pl.BlockSpec((SUPER, PAIR), lambda g, i: (rev(i), g)),
                pl.BlockSpec((SUPER, PAIR), lambda g, i: (prev(i), g)),
                pl.BlockSpec((SUPER, PAIR), lambda g, i: (rev(i), g_n + g)),
                pl.BlockSpec((SUPER, PAIR), lambda g, i: (prev(i), g_n + g))]
    ins = [_slope_table(d), q, do, o, lse, kv, kv, kv, kv]
    planes = pl.BlockSpec((2, SUPER, PAIR), lambda g, i: (0, rev(i), g))
    if has_in:
        in_specs.append(planes)
        ins.append(dkv_in)
    res = pl.pallas_call(
        body, name=name, grid=(g_n, ns), in_specs=in_specs, out_specs=[blk, planes],
        out_shape=[jax.ShapeDtypeStruct((s, d), BF16), jax.ShapeDtypeStruct((2, s, d), F32)],
        scratch_shapes=([pltpu.VMEM((2 * SUPER, PAIR), F32)] * 4 + [pltpu.VMEM((SUPER, PAIR), F32)]
                        + [pltpu.VMEM((4 * len(DILATIONS), L, 2 * L), F32)]),
        compiler_params=_params(("parallel", "arbitrary")),
    )(*ins)
    return res[0], res[1]


def _coords():
    return lax.axis_index("x"), lax.axis_index("y"), lax.axis_index("c")


def _chip_peers(x, y):
    return [(1 - x, y), (x, 1 - y), (1 - x, 1 - y)]


def _block_of(ref, axis, blk, size):
    start = pl.multiple_of(blk * size, size)
    if axis == 1:
        return ref.at[:, pl.ds(start, size)]
    return ref.at[pl.ds(start, size), :]


ANY = pl.BlockSpec(memory_space=pl.ANY)


HBM = pl.BlockSpec(memory_space=pltpu.HBM)
SEM = pl.BlockSpec(memory_space=pltpu.SEMAPHORE)
SPLIT = pltpu.CompilerParams(has_side_effects=pltpu.SideEffectType.DATAFLOW_SIDE_EFFECTING)


def _in_hbm(a):
    return pltpu.with_memory_space_constraint(a, pltpu.HBM)


def _thru(arrays):
    return [pltpu.HBM(a.shape, a.dtype) for a in arrays]


def _cast_place(w, layer, ax, dtype, after, name):
    _, k, n = w.shape
    t = _pick(k, (512, 256, 128))
    nb = k // t
    extra = [] if after is None else [after]

    def body(blk_ref, w_ref, *refs):
        b_ref, f_ref = refs[len(extra):]
        v = w_ref[...].astype(dtype)
        b_ref[...] = v
        f_ref[...] = v

    full_shape = (k, 4 * n) if ax == 1 else (4 * k, n)
    place = (lambda i, blk: (i, blk[0])) if ax == 1 else (lambda i, blk: (blk[0] * nb + i, 0))
    return pl.pallas_call(
        body, name=name,
        grid_spec=pltpu.PrefetchScalarGridSpec(
            num_scalar_prefetch=1, grid=(nb,),
            in_specs=[pl.BlockSpec((None, t, n), lambda i, blk: (layer, i, 0))] + [ANY] * len(extra),
            out_specs=[pl.BlockSpec((t, n), lambda i, blk: (i, 0)), pl.BlockSpec((t, n), place)]),
        out_shape=[jax.ShapeDtypeStruct((k, n), dtype), jax.ShapeDtypeStruct(full_shape, dtype)],
        compiler_params=_params(("parallel",)),
    )(_my_block()[None], w, *extra)


def _my_block():
    return (2 * lax.axis_index("x") + lax.axis_index("y")).astype(jnp.int32)


def _gather_start(group, carry, name):
    n, nc = len(group), len(carry)

    def body(*refs):
        blocks, fulls, send_sem, recv_sem = refs[:n], refs[n:2 * n], refs[2 * n + nc], refs[2 * n + nc + 1]
        x, y, c = _coords()
        for t, (b, _, ax) in enumerate(group):
            mine = _block_of(fulls[t], ax, 2 * x + y, b.shape[ax])
            for j, (px, py) in enumerate(_chip_peers(x, y)):
                pltpu.make_async_remote_copy(
                    src_ref=blocks[t], dst_ref=mine, send_sem=send_sem.at[3 * t + j], recv_sem=recv_sem.at[3 * t + j],
                    device_id=(px, py, c), device_id_type=MESH).start()

    arrays = [b for b, _, _ in group] + [f for _, f, _ in group] + list(carry)
    sems = [pltpu.SemaphoreType.DMA((3 * n,))] * 2
    res = pl.pallas_call(
        body, name=name, in_specs=[HBM] * len(arrays), out_specs=[SEM, SEM] + [HBM] * len(arrays),
        out_shape=sems + _thru(arrays), input_output_aliases={i: 2 + i for i in range(len(arrays))},
        compiler_params=SPLIT,
    )(*[_in_hbm(a) for a in arrays])
    return (res[0], res[1], list(res[2:2 + n]), list(res[2 + n:2 + 2 * n])), list(res[2 + 2 * n:])


def _gather_wait(group, started, after, name):
    sends, recvs, blocks, fulls = started
    m = len(group)

    def body(*refs):
        blk_refs, full_refs, send_sem, recv_sem = refs[:m], refs[m:2 * m], refs[2 * m], refs[2 * m + 1]
        x, y, c = _coords()
        for t, (b, _, ax) in enumerate(group):
            for j, (px, py) in enumerate(_chip_peers(x, y)):
                cp = pltpu.make_async_remote_copy(
                    src_ref=blk_refs[t], dst_ref=_block_of(full_refs[t], ax, 2 * px + py, b.shape[ax]),
                    send_sem=send_sem.at[3 * t + j], recv_sem=recv_sem.at[3 * t + j],
                    device_id=(px, py, c), device_id_type=MESH)
                cp.wait_send()
                cp.wait_recv()

    extra = [] if after is None else [after]
    res = pl.pallas_call(
        body, name=name, in_specs=[HBM] * (2 * m) + [SEM, SEM] + [ANY] * len(extra), out_specs=[HBM] * (2 * m),
        out_shape=_thru(blocks) + _thru(fulls), input_output_aliases={i: i for i in range(2 * m)},
        compiler_params=SPLIT,
    )(*blocks, *fulls, sends, recvs, *extra)
    return list(res[m:])


def _scatter_start(grads, carry, name):
    n = len(grads)
    n_in = 2 * n + len(carry)

    def body(*refs):
        g_refs, st_refs, send_sem, recv_sem = refs[:n], refs[n:2 * n], refs[n_in], refs[n_in + 1]
        x, y, c = _coords()
        for t, (g, ax) in enumerate(grads):
            for j, (px, py) in enumerate(_chip_peers(x, y)):
                pltpu.make_async_remote_copy(
                    src_ref=_block_of(g_refs[t], ax, 2 * px + py, g.shape[ax] // 4), dst_ref=st_refs[t].at[j],
                    send_sem=send_sem.at[3 * t + j], recv_sem=recv_sem.at[3 * t + j],
                    device_id=(px, py, c), device_id_type=MESH).start()

    arrays = [g for g, _ in grads]
    for g, ax in grads:
        shape = list(g.shape)
        shape[ax] //= 4
        arrays.append(lax.empty((3, *shape), g.dtype))
    arrays += list(carry)
    sems = [pltpu.SemaphoreType.DMA((3 * n,))] * 2
    res = pl.pallas_call(
        body, name=name, in_specs=[HBM] * n_in, out_specs=[SEM, SEM] + [HBM] * n_in,
        out_shape=sems + _thru(arrays), input_output_aliases={i: 2 + i for i in range(n_in)},
        compiler_params=SPLIT,
    )(*[_in_hbm(a) for a in arrays])
    return (res[0], res[1], list(res[2:2 + n]), list(res[2 + n:2 + 2 * n])), list(res[2 + 2 * n:])


def _scatter_wait(axes, started, after, name):
    sends, recvs, full, stacks = started
    n = len(full)
    extra = [] if after is None else [after]

    def body(*refs):
        g_refs, st_refs, send_sem, recv_sem = refs[:n], refs[n:2 * n], refs[2 * n], refs[2 * n + 1]
        x, y, c = _coords()
        for t, ax in enumerate(axes):
            size = full[t].shape[ax] // 4
            for j, (px, py) in enumerate(_chip_peers(x, y)):
                cp = pltpu.make_async_remote_copy(
                    src_ref=_block_of(g_refs[t], ax, 2 * px + py, size), dst_ref=st_refs[t].at[j],
                    send_sem=send_sem.at[3 * t + j], recv_sem=recv_sem.at[3 * t + j],
                    device_id=(px, py, c), device_id_type=MESH)
                cp.wait_send()
                cp.wait_recv()

    res = pl.pallas_call(
        body, name=name, in_specs=[HBM] * (2 * n) + [SEM, SEM] + [ANY] * len(extra), out_specs=[HBM] * (2 * n),
        out_shape=_thru(full) + _thru(stacks), input_output_aliases={i: i for i in range(2 * n)},
        compiler_params=SPLIT,
    )(*full, *stacks, sends, recvs, *extra)
    return list(res[:n]), list(res[n:])


def _pair_copies(g_refs, st_refs, out_refs, items, send_sem, recv_sem):
    x, y, c = _coords()
    copies = []
    for u, (g, ax, _) in enumerate(items):
        own = _block_of(g_refs[u], ax, 2 * x + y, g.shape[ax] // 4)
        for k, (src, dst) in enumerate([(own, out_refs[u].at[0]), (st_refs[u], out_refs[u].at[pl.ds(1, 3)])]):
            copies.append(pltpu.make_async_remote_copy(
                src_ref=src, dst_ref=dst, send_sem=send_sem.at[2 * u + k], recv_sem=recv_sem.at[2 * u + k],
                device_id=(x, y, 1 - c), device_id_type=MESH))
    return copies


def _pair_start(items, carry, name):
    n = len(items)
    n_in = 3 * n + len(carry)

    def body(*refs):
        for cp in _pair_copies(refs[:n], refs[n:2 * n], refs[2 * n:3 * n], items, refs[n_in], refs[n_in + 1]):
            cp.start()

    arrays = ([g for g, _, _ in items] + [st for _, _, st in items]
              + [lax.empty((4, *st.shape[1:]), st.dtype) for _, _, st in items] + list(carry))
    sems = [pltpu.SemaphoreType.DMA((2 * n,))] * 2
    res = pl.pallas_call(
        body, name=name, in_specs=[HBM] * n_in, out_specs=[SEM, SEM] + [HBM] * n_in,
        out_shape=sems + _thru(arrays), input_output_aliases={i: 2 + i for i in range(n_in)},
        compiler_params=SPLIT,
    )(*[_in_hbm(a) for a in arrays])
    thru = res[2:]
    return (res[0], res[1], *(list(thru[k * n:(k + 1) * n]) for k in range(3))), list(thru[3 * n:])


def _pair_wait(axes, started, name):
    send, recv, full, stacks, landing = started
    n = len(full)
    items = [(full[u], axes[u], stacks[u]) for u in range(n)]

    def body(*refs):
        for cp in _pair_copies(refs[:n], refs[n:2 * n], refs[2 * n:3 * n], items, refs[3 * n], refs[3 * n + 1]):
            cp.wait_send()
            cp.wait_recv()

    res = pl.pallas_call(
        body, name=name, in_specs=[HBM] * (3 * n) + [SEM, SEM], out_specs=[HBM] * (3 * n),
        out_shape=_thru(full + stacks + landing), input_output_aliases={i: i for i in range(3 * n)},
        compiler_params=SPLIT,
    )(*full, *stacks, *landing, send, recv)
    return list(res[:n]), list(res[n:2 * n]), list(res[2 * n:])


def _allreduce_small(v, name):
    r, cdim = v.shape

    def body(v_ref, out_ref, buf, send_sems, recv_sems):
        x, y, c = _coords()
        me = 4 * x + 2 * y + c
        buf[0] = v_ref[...]
        sends = []
        for k in range(1, 8):
            peer = (x if not (k & 4) else 1 - x, y if not (k & 2) else 1 - y, c if not (k & 1) else 1 - c)
            cp = pltpu.make_async_remote_copy(
                src_ref=v_ref, dst_ref=buf.at[k], send_sem=send_sems.at[k - 1], recv_sem=recv_sems.at[k - 1],
                device_id=peer, device_id_type=MESH)
            cp.start()
            sends.append(cp)
        for cp in sends:
            cp.wait_recv()
        total = buf[me]
        for src in range(1, 8):
            total = total + buf[jnp.bitwise_xor(me, src)]
        out_ref[...] = total
        for cp in sends:
            cp.wait_send()

    return pl.pallas_call(
        body, name=name,
        in_specs=[pl.BlockSpec(memory_space=pltpu.VMEM)], out_specs=pl.BlockSpec(memory_space=pltpu.VMEM),
        out_shape=jax.ShapeDtypeStruct((r, cdim), F32),
        scratch_shapes=[pltpu.VMEM((8, r, cdim), F32), pltpu.SemaphoreType.DMA((7,)), pltpu.SemaphoreType.DMA((7,))],
        compiler_params=pltpu.CompilerParams(has_side_effects=True),
    )(v)


def _adamw_math(w, g, m, v):
    m = ADAM_B1 * m + (1.0 - ADAM_B1) * g
    v = ADAM_B2 * v + (1.0 - ADAM_B2) * jnp.square(g)
    m_hat = m / (1.0 - ADAM_B1 ** ADAM_STEP)
    v_hat = v / (1.0 - ADAM_B2 ** ADAM_STEP)
    delta = -ADAM_LR * (m_hat / (jnp.sqrt(v_hat) + ADAM_EPS) + ADAM_WD * w)
    return delta, m, v


def _adamw(w, m, v, grads, name):
    r, cdim = w.shape
    paired = isinstance(grads, list)
    layers = len(grads) if paired else 1
    t = _pick(r // layers, (128, 64, 32, 16, 8))
    per_layer = r // layers // t
    n_grad = 3 * layers if paired else 1

    def body(*refs):
        refs = refs[1:] if paired else refs
        w_ref, m_ref, v_ref = refs[:3]
        outs = refs[3 + n_grad:]

        def update(g):
            delta, m_new, v_new = _adamw_math(w_ref[...], g, m_ref[...], v_ref[...])
            outs[0][...] = g
            outs[1][...] = delta
            outs[2][...] = m_new
            outs[3][...] = v_new

        if not paired:
            update(refs[3][...])
            return
        layer = pl.program_id(0) // per_layer
        for l in range(layers):
            @pl.when(layer == l)
            def _(own_ref=refs[3 + 3 * l], st_ref=refs[4 + 3 * l], sib_ref=refs[5 + 3 * l]):
                sa = own_ref[...].astype(F32)
                sb = sib_ref[0].astype(F32)
                for k in range(3):
                    sa = sa + st_ref[k].astype(F32)
                    sb = sb + sib_ref[k + 1].astype(F32)
                update(sa + sb)

    out_shape = [jax.ShapeDtypeStruct((r, cdim), F32)] * 4
    if not paired:
        spec = pl.BlockSpec((t, cdim), lambda i: (i, 0))
        return pl.pallas_call(
            body, name=name, grid=(r // t,), in_specs=[spec] * 4, out_specs=[spec] * 4, out_shape=out_shape,
            compiler_params=_params(("parallel",)),
        )(w, m, v, grads)

    spec = pl.BlockSpec((t, cdim), lambda i, blk: (i, 0))
    ins, in_specs = [w, m, v], [spec] * 3
    for l, (g, ax, stack, sib) in enumerate(grads):
        row = lambda i, l=l: jnp.clip(i - l * per_layer, 0, per_layer - 1)
        own = ((lambda i, blk, row=row: (row(i), blk[0])) if ax == 1
               else (lambda i, blk, row=row: (blk[0] * per_layer + row(i), 0)))
        ins += [g, stack, sib]
        in_specs += [pl.BlockSpec((t, cdim), own),
                     pl.BlockSpec((3, t, cdim), lambda i, blk, row=row: (0, row(i), 0)),
                     pl.BlockSpec((4, t, cdim), lambda i, blk, row=row: (0, row(i), 0))]
    return pl.pallas_call(
        body, name=name,
        grid_spec=pltpu.PrefetchScalarGridSpec(
            num_scalar_prefetch=1, grid=(r // t,), in_specs=in_specs, out_specs=[spec] * 4),
        out_shape=out_shape, compiler_params=_params(("parallel",)),
    )(_my_block()[None], *ins)


def _local_step(x, target, gains, conv_ws, kv_gain, weights_of, send_grads):
    depth = len(gains)
    n_a = len(conv_ws)
    saved, ws = [], []
    kv = kvn = None
    _, (xn,) = _norm_res_fwd(x, None, None, [gains[0][0]], "norm_first")
    h = x
    for l in range(depth):
        g = gains[l]
        sv = {"x_in": h, "xn": xn}
        w = weights_of(l, "mix", h)
        ws.append(w)
        if l == n_a:
            kv = _mm(kvn, w["kv"], "nn", F32, "kv_fwd")
        if l < n_a:
            p = _mm(xn, w["conv_in"], "nn", BF16, f"conv_in_fwd_{l}")
            z = _conv_gate_fwd(p, conv_ws[l], f"conv_gate_fwd_{l}")
            mix, x1, xn2 = _mm_norm_res(z, w["conv_out"], h, g[1], g[2], f"conv_out_fwd_{l}")
            sv.update(p=p, z=z)
        else:
            j = l - n_a
            q = _mm(xn, w["q"], "nn", F32, f"q_fwd_{j}", scale=HEAD_DIM ** -0.5)
            o, lse = _attention_fwd(q, kv, f"attn_fwd_{j}")
            mix, x1, xn2 = _mm_norm_res(o, w["o"], h, g[1], g[2], f"o_fwd_{j}")
            sv.update(q=q, o=o, lse=lse)
        w.update(weights_of(l, "ffn", mix))
        f, a = _ffn_in_swiglu(xn2, w["ffn_in"], f"ffn_in_fwd_{l}")
        ff = _mm(a, w["ffn_out"], "nn", BF16, f"ffn_out_fwd_{l}")
        sv.update(mix=mix, x1=x1, xn2=xn2, f=f, a=a, ff=ff)
        saved.append(sv)
        if l == depth - 1:
            dx, loss = _norm_res_loss(x1, ff, g[3], target, "norm_loss")
        elif l == n_a - 1:
            h, _ = _norm_res_fwd(x1, ff, g[3], [], f"norm_end_{l}")
            h = _permute16(h, False, "permute_stream")
            _, (xn, kvn) = _norm_res_fwd(h, None, None, [gains[l + 1][0], kv_gain], "norm_permuted")
        else:
            h, (xn,) = _norm_res_fwd(x1, ff, g[3], [gains[l + 1][0]], f"norm_end_{l}")
    d_gains = [[None] * 4 for _ in range(depth)]
    d_conv = [None] * n_a
    d_kv_gain = None
    dkv = None
    _, _, dff, d_gains[depth - 1][3] = _norm_bwd(dx, [], None, (saved[-1]["ff"], gains[-1][3]), "norm_loss_bwd")
    for l in reversed(range(depth)):
        sv, g, w, grads = saved[l], gains[l], ws[l], {}
        grads["ffn_out"] =_mm(sv["a"], dff, "tn", BF16, f"ffn_out_dw_{l}")
        df = _ffn_out_dx_swiglu(dff, w["ffn_out"], sv["f"], f"ffn_out_dx_{l}")
        dxn2 = _mm(df, w["ffn_in"], "nt", BF16, f"ffn_in_dx_{l}")
        grads["ffn_in"] =_mm(sv["xn2"], df, "tn", BF16, f"ffn_in_dw_{l}")
        dx, (d_gains[l][2],), dmix, d_gains[l][1] = _norm_bwd(
            dx, [(dxn2, g[2])], sv["x1"], (sv["mix"], g[1]), f"norm_mid_bwd_{l}")
        dx, dmix = send_grads(l, "ffn", grads, [dx, dmix])
        if l < n_a:
            dz = _mm(dmix, w["conv_out"], "nt", BF16, f"conv_out_dx_{l}")
            grads["conv_out"] =_mm(sv["z"], dmix, "tn", BF16, f"conv_out_dw_{l}")
            dp, d_conv[l] = _conv_gate_bwd(sv["p"], dz, conv_ws[l], f"conv_gate_bwd_{l}")
            dxn = _mm(dp, w["conv_in"], "nt", BF16, f"conv_in_dx_{l}")
            grads["conv_in"] =_mm(sv["xn"], dp, "tn", BF16, f"conv_in_dw_{l}")
        else:
            j = l - n_a
            do = _mm(dmix, w["o"], "nt", F32, f"o_dx_{j}")
            grads["o"] =_mm(sv["o"], dmix, "tn", BF16, f"o_dw_{j}")
            dq, dkv = _attention_bwd(sv["q"], kv, sv["o"], do, sv["lse"], dkv, f"attn_bwd_{j}")
            scale = HEAD_DIM ** -0.5
            dxn = _mm(dq, w["q"], "nt", BF16, f"q_dx_{j}", scale=scale)
            grads["q"] =_mm(sv["xn"], dq, "tn", BF16, f"q_dw_{j}", scale=scale)
        branches = [(dxn, g[0])]
        if l == n_a:
            dkvn = _mm(dkv, w["kv"], "nt", BF16, "kv_dx")
            grads["kv"] =_mm(kvn, dkv, "tn", BF16, "kv_dw")
            branches.append((dkvn, kv_gain))
        post = (saved[l - 1]["ff"], gains[l - 1][3]) if l > 0 else None
        if l == n_a:
            dx, dgs, _, _ = _norm_bwd(dx, branches, sv["x_in"], None, f"norm_end_bwd_{l}")
            dx = _permute16(dx, True, "unpermute_stream")
            _, _, dff, dg_post = _norm_bwd(dx, [], None, post, "norm_boundary_bwd")
        else:
            dx, dgs, dff, dg_post = _norm_bwd(dx, branches, sv["x_in"], post, f"norm_end_bwd_{l}")
        if dff is None:
            send_grads(l, "mix", grads, [])
        else:
            dx, dff = send_grads(l, "mix", grads, [dx, dff])
        d_gains[l][0] = dgs[0]
        if l == n_a:
            d_kv_gain = dgs[1]
        if l > 0:
            d_gains[l - 1][3] = dg_post
    return loss, dx, d_gains, d_conv, d_kv_gain


BIG = (
    ("conv_in", 1), ("conv_out", 0), ("kv", 1), ("q", 0), ("o", 0), ("ffn_in", 1), ("ffn_out", 0))


def kernel(x, norm_g, conv_in_w, conv_w, conv_out_w, kv_norm_g, kv_w, q_w, o_w, ffn_in_w, ffn_out_w, loss_target, m_norm_g, m_conv_in_w, m_conv_w, m_conv_out_w, m_kv_norm_g, m_kv_w, m_q_w, m_o_w, m_ffn_in_w, m_ffn_out_w, v_norm_g, v_conv_in_w, v_conv_w, v_conv_out_w, v_kv_norm_g, v_kv_w, v_q_w, v_o_w, v_ffn_in_w, v_ffn_out_w):
    depth, _, dq = norm_g.shape
    d = 4 * dq
    n_a = conv_w.shape[0]
    big_w = {"conv_in": conv_in_w, "conv_out": conv_out_w, "kv": kv_w[None], "q": q_w, "o": o_w,
             "ffn_in": ffn_in_w, "ffn_out": ffn_out_w}
    big_m = {"conv_in": m_conv_in_w, "conv_out": m_conv_out_w, "kv": m_kv_w[None], "q": m_q_w, "o": m_o_w,
             "ffn_in": m_ffn_in_w, "ffn_out": m_ffn_out_w}
    big_v = {"conv_in": v_conv_in_w, "conv_out": v_conv_out_w, "kv": v_kv_w[None], "q": v_q_w, "o": v_o_w,
             "ffn_in": v_ffn_in_w, "ffn_out": v_ffn_out_w}

    n_gain, n_tap = depth * 4, n_a * conv_w.shape[1]
    small_rows = -(-(n_gain + n_tap + 1) // 8) * 8
    pad_rows = small_rows - n_gain - n_tap

    def pack_small(gains, taps):
        return jnp.concatenate([gains.reshape(n_gain, dq), taps.reshape(n_tap, dq), jnp.zeros((pad_rows, dq), F32)])

    axis_of = dict(BIG)

    def matrices_of(l, part):
        if part == "ffn":
            return [("ffn_in", l), ("ffn_out", l)]
        if l < n_a:
            return [("conv_in", l), ("conv_out", l)]
        return ([("kv", 0)] if l == n_a else []) + [("q", l - n_a), ("o", l - n_a)]

    halves = [(l, part) for l in range(depth) for part in ("mix", "ffn")]
    def placed(half, after):
        return [(*_cast_place(big_w[name], i, axis_of[name], BF16, after, f"place_{name}_{i}"), axis_of[name])
                for name, i in matrices_of(*half)]

    groups = {halves[0]: placed(halves[0], None)}
    groups[halves[0]].append((*_cast_place(pack_small(norm_g, conv_w)[None], 0, 1, F32, None, "place_small"), 1))
    started = {halves[0]: _gather_start(groups[halves[0]], [], "gather_start_0_mix")[0]}
    for half in halves[1:]:
        groups[half] = placed(half, started[halves[0]][2][0])

    AHEAD = 2

    def fetch(half, after):
        full = _gather_wait(groups[half], started[half], after, "gather_wait_%d_%s" % half)
        nxt = halves.index(half) + AHEAD
        if nxt < len(halves):
            started[halves[nxt]], full = _gather_start(groups[halves[nxt]], full, "gather_start_%d_%s" % halves[nxt])
        return full

    target = _permute16(loss_target.reshape(x.shape[1:]), False, "permute_target")
    for half in halves[1:AHEAD]:
        started[half], (target,) = _gather_start(groups[half], [target], "gather_start_%d_%s" % half)
    first = fetch(halves[0], target)
    small = first[-1]
    gains = [[small[4 * l + i][None] for i in range(4)] for l in range(depth)]
    conv_ws = [small[n_gain + 3 * l:n_gain + 3 * l + 3] for l in range(n_a)]
    kv_gain = kv_norm_g[None]

    def weights_of(l, part, after):
        full = first if (l, part) == halves[0] else fetch((l, part), after)
        return {name: full[t] for t, (name, _) in enumerate(matrices_of(l, part))}

    sent, paired = {}, {}
    LAG = 2

    def to_sibling(half, carry, after):
        axes = [axis_of[name] for name, _ in matrices_of(*half)]
        full, stacks = _scatter_wait(axes, sent[half], after, "scatter_wait_%d_%s" % half)
        paired[half], carry = _pair_start(list(zip(full, axes, stacks)), carry, "pair_start_%d_%s" % half)
        return carry

    def send_grads(l, part, grads, carry):
        sent[l, part], carry = _scatter_start(
            [(grads[name], axis_of[name]) for name, _ in matrices_of(l, part)], carry, f"scatter_start_{l}_{part}")
        at = halves.index((l, part))
        if carry:
            if at + LAG < len(halves):
                carry = to_sibling(halves[at + LAG], carry, carry[0])
        else:
            for older in range(at + LAG, at, -1):
                if halves[older] not in paired:
                    to_sibling(halves[older], [], sent[l, part][2][0])
        if at == 1 and carry:
            carry = to_sibling(halves[at + 1], carry, carry[0])
        return carry

    loss, dx, d_gains, d_conv, d_kv_gain = _local_step(
        x.reshape(x.shape[1:]), target, gains, conv_ws, kv_gain, weights_of, send_grads)
    loss = lax.psum(loss, ("x", "y", "c"))

    small_g = jnp.concatenate([dg for row in d_gains for dg in row] + list(d_conv) + [d_kv_gain]
                              + [jnp.zeros((pad_rows - 1, d), F32)])
    small_g = _allreduce_small(small_g, "allreduce_small")
    blk = 2 * lax.axis_index("x") + lax.axis_index("y")
    mine_small = lax.dynamic_slice_in_dim(small_g, blk * dq, dq, axis=1)
    kv_rows = d // dq

    def pack_opt(gains_like, taps_like, kv_like):
        rows = jnp.concatenate([gains_like.reshape(n_gain, dq), taps_like.reshape(n_tap, dq), kv_like.reshape(kv_rows, dq)])
        extra = -rows.shape[0] % 8
        return jnp.concatenate([rows, jnp.zeros((extra, dq), F32)]) if extra else rows

    sw = pack_opt(norm_g, conv_w, kv_norm_g)
    sm = pack_opt(m_norm_g, m_conv_w, m_kv_norm_g)
    sv = pack_opt(v_norm_g, v_conv_w, v_kv_norm_g)
    sg = pack_opt(mine_small[:n_gain], mine_small[n_gain:n_gain + n_tap], small_g[n_gain + n_tap])
    s_out = _adamw(sw, sm, sv, sg, "adamw_small")

    def unpack(a):
        return (a[:n_gain].reshape(depth, 4, dq), a[n_gain:n_gain + n_tap].reshape(n_a, -1, dq),
                a[n_gain + n_tap:n_gain + n_tap + kv_rows].reshape(d))

    small_out = [unpack(a) for a in s_out]

    landed, big_out = {}, {}

    def update(name):
        for half in halves:
            if matrices_of(*half)[0] not in landed and any(n == name for n, _ in matrices_of(*half)):
                axes = [axis_of[n] for n, _ in matrices_of(*half)]
                landed.update(zip(matrices_of(*half), zip(*_pair_wait(axes, paired[half], "pair_wait_%d_%s" % half))))
        shp, ax = big_w[name].shape, axis_of[name]
        rows, cols = shp[0] * shp[1], shp[2]
        flat = lambda a: a.reshape(rows, cols)
        full, stacks, sibling = zip(*[landed[name, i] for i in range(shp[0])])
        res = _adamw(flat(big_w[name]), flat(big_m[name]), flat(big_v[name]),
                     [(full[i], ax, stacks[i], sibling[i]) for i in range(shp[0])], f"adamw_{name}")
        big_out[name] = [a.reshape(shp[1:] if name == "kv" else shp) for a in res]

    pending = [half for half in reversed(halves) if half not in paired]
    for half in pending[:-1]:
        to_sibling(half, [], None)
    late = [name for name, _ in BIG if any(n == name for n, _ in matrices_of(*pending[-1]))]
    for name, _ in BIG:
        if name not in late:
            update(name)
    to_sibling(pending[-1], [], big_out["ffn_out"][0])
    for name in late:
        update(name)

    def leaves(i):
        ng, cw_, kg = small_out[i]
        return [ng, big_out["conv_in"][i], cw_, big_out["conv_out"][i], kg, big_out["kv"][i], big_out["q"][i],
                big_out["o"][i], big_out["ffn_in"][i], big_out["ffn_out"][i]]

    return (loss, dx.reshape(x.shape), *leaves(0), *leaves(1), *leaves(2), *leaves(3))
```

```python
import jax
import jax.numpy as jnp
import numpy as np
from jax import lax
from jax.experimental import pallas as pl
from jax.experimental.pallas import tpu as pltpu

F32 = jnp.float32
BF16 = jnp.bfloat16
HEAD_DIM = 64
DILATIONS = (1, 4, 16)
NORM_EPS = 1e-6
NEG_BIG = -1e30
VMEM_LIMIT = 48 * 1024 * 1024
ROW_TILE = 256
NORM_TILE = 512
LANE = 128
MESH = pl.DeviceIdType.MESH

ADAM_LR = 0.001
ADAM_B1 = 0.9
ADAM_B2 = 0.999
ADAM_EPS = 1e-08
ADAM_WD = 0.01
ADAM_STEP = 10

TILE_CANDIDATES = (1024, 1408, 768, 512, 384, 256, 128)


def _pick(dim, cands=TILE_CANDIDATES):
    for c in cands:
        if c <= dim and dim % c == 0:
            return c
    return dim


def _params(sem):
    return pltpu.CompilerParams(dimension_semantics=sem, vmem_limit_bytes=VMEM_LIMIT)


def _mm(a, b, mode, out_dtype, name, scale=None):
    a_planes = a.shape[0] if a.ndim == 3 else 1
    b_planes = b.shape[0] if b.ndim == 3 else 1
    if mode == "nn":
        m, k = a.shape[-2], a.shape[-1] * a_planes
        n = b.shape[1]
    elif mode == "nt":
        m, k = a.shape[-2], a.shape[-1] * a_planes
        n = b.shape[0]
    else:
        k, m = a.shape
        n = b.shape[-1] * b_planes
    tm, tn = _pick(m), _pick(n // b_planes)
    tk = _pick(k // a_planes, ((2048,) if mode == "tn" else (3072, 2816)) + TILE_CANDIDATES)
    nk = k // tk
    ka, nb = k // a_planes // tk, n // b_planes // tn
    if a_planes > 1:
        a_spec = pl.BlockSpec((None, tm, tk), lambda i, j, kk: (kk // ka, i, kk % ka))
    elif mode == "tn":
        a_spec = pl.BlockSpec((tk, tm), lambda i, j, kk: (kk, i))
    else:
        a_spec = pl.BlockSpec((tm, tk), lambda i, j, kk: (i, kk))
    if mode == "nn":
        b_spec = pl.BlockSpec((tk, tn), lambda i, j, kk: (kk, j))
        dims = (((1,), (0,)), ((), ()))
    elif mode == "nt":
        b_spec = pl.BlockSpec((tn, tk), lambda i, j, kk: (j, kk))
        dims = (((1,), (1,)), ((), ()))
    else:
        b_spec = (pl.BlockSpec((None, tk, tn), lambda i, j, kk: (j // nb, kk, j % nb)) if b_planes > 1
                  else pl.BlockSpec((tk, tn), lambda i, j, kk: (kk, j)))
        dims = (((0,), (0,)), ((), ()))

    def finish(acc):
        if scale is not None:
            acc = acc * scale
        return acc.astype(out_dtype)

    if nk == 1:
        def body(a_ref, b_ref, o_ref):
            o_ref[...] = finish(lax.dot_general(a_ref[...].astype(BF16), b_ref[...].astype(BF16), dims, preferred_element_type=F32))
        scratch = []
    else:
        def body(a_ref, b_ref, o_ref, acc_ref):
            kk = pl.program_id(2)

            @pl.when(kk == 0)
            def _():
                acc_ref[...] = jnp.zeros_like(acc_ref)

            acc_ref[...] += lax.dot_general(a_ref[...].astype(BF16), b_ref[...].astype(BF16), dims, preferred_element_type=F32)

            @pl.when(kk == nk - 1)
            def _():
                o_ref[...] = finish(acc_ref[...])
        scratch = [pltpu.VMEM((tm, tn), F32)]

    return pl.pallas_call(
        body, name=name,
        grid=(m // tm, n // tn, nk),
        in_specs=[a_spec, b_spec],
        out_specs=pl.BlockSpec((tm, tn), lambda i, j, kk: (i, j)),
        out_shape=jax.ShapeDtypeStruct((m, n), out_dtype),
        scratch_shapes=scratch,
        compiler_params=_params(("parallel", "parallel", "arbitrary")),
    )(a, b)


def _rstd(v):
    return lax.rsqrt(jnp.mean(v * v, axis=-1, keepdims=True) + NORM_EPS)


def _rms_bwd(dy, v, g, r):
    xhat = v * r
    gy = dy * g
    dv = r * (gy - xhat * jnp.mean(gy * xhat, axis=-1, keepdims=True))
    return dv, dy * xhat


def _row_spec(t, width):
    return pl.BlockSpec((t, width), lambda i: (i, 0))


def _gain_spec(width):
    return pl.BlockSpec((1, width), lambda i: (0, 0))


def _norm_res_fwd(x, mix, g_post, pre_gains, name):
    s, d = x.shape
    t = _pick(s, (NORM_TILE, ROW_TILE))
    has_mix = mix is not None
    n_pre = len(pre_gains)

    def body(*refs):
        x_ref = refs[0]
        pos = 1
        x1 = x_ref[...]
        if has_mix:
            mv = refs[1][...].astype(F32)
            x1 = x1 + mv * _rstd(mv) * refs[2][...]
            pos = 3
        gains = refs[pos:pos + n_pre]
        outs = refs[pos + n_pre:]
        if has_mix:
            outs[0][...] = x1
            outs = outs[1:]
        r = _rstd(x1)
        for g_ref, o_ref in zip(gains, outs):
            o_ref[...] = (x1 * r * g_ref[...]).astype(BF16)

    ins = [x] + ([mix, g_post] if has_mix else []) + list(pre_gains)
    in_specs = [_row_spec(t, d)] + ([_row_spec(t, d), _gain_spec(d)] if has_mix else []) + [_gain_spec(d)] * n_pre
    out_shape = ([jax.ShapeDtypeStruct((s, d), F32)] if has_mix else []) + [jax.ShapeDtypeStruct((s, d), BF16)] * n_pre
    out_specs = [_row_spec(t, d)] * len(out_shape)
    res = pl.pallas_call(
        body, name=name, grid=(s // t,), in_specs=in_specs, out_specs=out_specs, out_shape=out_shape,
        compiler_params=_params(("parallel",)),
    )(*ins)
    if has_mix:
        return res[0], list(res[1:])
    return x, list(res)


def _mm_norm_res(a, b, x, g_post, pre_gain, name):
    m, k = a.shape
    d = b.shape[1]
    tm, tk = _pick(m, (1024, 512, 256)), _pick(k)
    nk = k // tk

    def body(a_ref, b_ref, x_ref, gp_ref, g_ref, mix_ref, x1_ref, xn_ref, acc_ref):
        kk = pl.program_id(1)

        @pl.when(kk == 0)
        def _():
            acc_ref[...] = jnp.zeros_like(acc_ref)

        acc_ref[...] += jnp.dot(a_ref[...].astype(BF16), b_ref[...], preferred_element_type=F32)

        @pl.when(kk == nk - 1)
        def _():
            mix = acc_ref[...].astype(BF16)
            mix_ref[...] = mix
            mv = mix.astype(F32)
            x1 = x_ref[...] + mv * _rstd(mv) * gp_ref[...]
            x1_ref[...] = x1
            xn_ref[...] = (x1 * _rstd(x1) * g_ref[...]).astype(BF16)

    rows = pl.BlockSpec((tm, d), lambda i, kk: (i, 0))
    gain = pl.BlockSpec((1, d), lambda i, kk: (0, 0))
    return pl.pallas_call(
        body, name=name, grid=(m // tm, nk),
        in_specs=[pl.BlockSpec((tm, tk), lambda i, kk: (i, kk)), pl.BlockSpec((tk, d), lambda i, kk: (kk, 0)),
                  rows, gain, gain],
        out_specs=[rows, rows, rows],
        out_shape=[jax.ShapeDtypeStruct((m, d), BF16), jax.ShapeDtypeStruct((m, d), F32),
                   jax.ShapeDtypeStruct((m, d), BF16)],
        scratch_shapes=[pltpu.VMEM((tm, d), F32)],
        compiler_params=_params(("parallel", "arbitrary")),
    )(a, b, x, g_post, pre_gain)


def _norm_res_loss(x, mix, g_post, target, name):
    s, d = x.shape
    t = _pick(s, (NORM_TILE, ROW_TILE))

    def body(x_ref, m_ref, g_ref, t_ref, dy_ref, loss_ref):
        mv = m_ref[...].astype(F32)
        y = x_ref[...] + mv * _rstd(mv) * g_ref[...]
        err = y - t_ref[...]
        dy_ref[...] = err * (1.0 / d)

        @pl.when(pl.program_id(0) == 0)
        def _():
            loss_ref[...] = jnp.zeros_like(loss_ref)

        loss_ref[...] += jnp.sum(err * err)

    dy, acc = pl.pallas_call(
        body, name=name, grid=(s // t,),
        in_specs=[_row_spec(t, d), _row_spec(t, d), _gain_spec(d), _row_spec(t, d)],
        out_specs=[_row_spec(t, d), pl.BlockSpec((8, LANE), lambda i: (0, 0))],
        out_shape=[jax.ShapeDtypeStruct((s, d), F32), jax.ShapeDtypeStruct((8, LANE), F32)],
        compiler_params=_params(("arbitrary",)),
    )(x, mix, g_post, target)
    return dy, acc[0, 0] * (0.5 / d)


def _norm_bwd(dx_out, branches, x_in, post, name):
    s, d = dx_out.shape
    t = _pick(s, (NORM_TILE, ROW_TILE))
    nb = len(branches)
    has_post = post is not None

    def body(*refs):
        dx_ref = refs[0]
        pos = 1
        dx = dx_ref[...]
        first = pl.program_id(0) == 0
        n_in = 1 + (1 + 2 * nb if nb else 0) + (2 if has_post else 0)
        outs = refs[n_in:]
        opos = 0
        if nb:
            xv = refs[pos][...]
            pos += 1
            r = _rstd(xv)
            dx_o = outs[0]
            opos = 1
            for _ in range(nb):
                dxn = refs[pos][...].astype(F32)
                g = refs[pos + 1][...]
                pos += 2
                dv, dg_rows = _rms_bwd(dxn, xv, g, r)
                dx = dx + dv
                dg_ref = outs[opos]
                opos += 1

                @pl.when(first)
                def _(dg_ref=dg_ref):
                    dg_ref[...] = jnp.zeros_like(dg_ref)

                dg_ref[...] += jnp.sum(dg_rows, axis=0, keepdims=True)
            dx_o[...] = dx
        if has_post:
            mv = refs[pos][...].astype(F32)
            g = refs[pos + 1][...]
            dm, dg_rows = _rms_bwd(dx, mv, g, _rstd(mv))
            outs[opos][...] = dm.astype(BF16)
            dg_ref = outs[opos + 1]

            @pl.when(first)
            def _():
                dg_ref[...] = jnp.zeros_like(dg_ref)

            dg_ref[...] += jnp.sum(dg_rows, axis=0, keepdims=True)

    ins, in_specs = [dx_out], [_row_spec(t, d)]
    out_shape, out_specs = [], []
    if nb:
        ins.append(x_in)
        in_specs.append(_row_spec(t, d))
        out_shape.append(jax.ShapeDtypeStruct((s, d), F32))
        out_specs.append(_row_spec(t, d))
        for dxn, g in branches:
            ins += [dxn, g]
            in_specs += [_row_spec(t, d), _gain_spec(d)]
            out_shape.append(jax.ShapeDtypeStruct((1, d), F32))
            out_specs.append(_gain_spec(d))
    if has_post:
        ins += [post[0], post[1]]
        in_specs += [_row_spec(t, d), _gain_spec(d)]
        out_shape += [jax.ShapeDtypeStruct((s, d), BF16), jax.ShapeDtypeStruct((1, d), F32)]
        out_specs += [_row_spec(t, d), _gain_spec(d)]
    res = pl.pallas_call(
        body, name=name, grid=(s // t,), in_specs=in_specs, out_specs=out_specs, out_shape=out_shape,
        compiler_params=_params(("arbitrary",)),
    )(*ins)
    res = list(res)
    dx_in = res.pop(0) if nb else dx_out
    dgs = [res.pop(0) for _ in range(nb)]
    dm, dg_post = (res[0], res[1]) if has_post else (None, None)
    return dx_in, dgs, dm, dg_post


HALO = 16


def _shift_down(u, prev, k):
    rows = lax.broadcasted_iota(jnp.int32, u.shape, 0)
    out = pltpu.roll(u, k, 0)
    for i in range(k):
        out = jnp.where(rows == i, prev[HALO - k + i:HALO - k + i + 1, :], out)
    return out


def _shift_up(u, nxt, k):
    n = u.shape[0]
    rows = lax.broadcasted_iota(jnp.int32, u.shape, 0)
    out = pltpu.roll(u, n - k, 0)
    for i in range(k):
        out = jnp.where(rows == n - k + i, nxt[i:i + 1, :], out)
    return out


def _conv_gate_fwd(p, cw, name):
    s, d3 = p.shape
    d = d3 // 3
    t = _pick(s, (ROW_TILE,))
    hb = t // HALO

    def body(p_ref, prev_ref, w_ref, z_ref):
        i = pl.program_id(0)
        pv = p_ref[...].astype(F32)
        b, u = pv[:, :d], pv[:, d:2 * d] * pv[:, 2 * d:]
        ph = prev_ref[...].astype(F32)
        up = jnp.where(i > 0, ph[:, d:2 * d] * ph[:, 2 * d:], 0.0)
        w = w_ref[...]
        y = w[0:1, :] * _shift_down(u, up, 2) + w[1:2, :] * _shift_down(u, up, 1) + w[2:3, :] * u
        z_ref[...] = (b * y).astype(BF16)

    return pl.pallas_call(
        body, name=name, grid=(s // t,),
        in_specs=[_row_spec(t, d3),
                  pl.BlockSpec((HALO, d3), lambda i: (jnp.maximum(i * hb - 1, 0), 0)),
                  pl.BlockSpec((3, d), lambda i: (0, 0))],
        out_specs=_row_spec(t, d),
        out_shape=jax.ShapeDtypeStruct((s, d), BF16),
        compiler_params=_params(("parallel",)),
    )(p, p, cw)


def _conv_gate_bwd(p, dz, cw, name):
    s, d3 = p.shape
    d = d3 // 3
    t = _pick(s, (ROW_TILE,))
    hb = t // HALO
    nt = s // t
    last_halo = s // HALO - 1

    def body(p_ref, prev_ref, next_ref, dz_ref, dznext_ref, w_ref, dp_ref, dw_ref):
        i = pl.program_id(0)
        pv = p_ref[...].astype(F32)
        b, c, h = pv[:, :d], pv[:, d:2 * d], pv[:, 2 * d:]
        u = c * h
        ph = prev_ref[...].astype(F32)
        up = jnp.where(i > 0, ph[:, d:2 * d] * ph[:, 2 * d:], 0.0)
        w = w_ref[...]
        u1, u2 = _shift_down(u, up, 1), _shift_down(u, up, 2)
        y = w[0:1, :] * u2 + w[1:2, :] * u1 + w[2:3, :] * u
        dz = dz_ref[...].astype(F32)
        dy = dz * b
        dyn = jnp.where(i < nt - 1, dznext_ref[...].astype(F32) * next_ref[...].astype(F32)[:, :d], 0.0)
        du = w[2:3, :] * dy + w[1:2, :] * _shift_up(dy, dyn, 1) + w[0:1, :] * _shift_up(dy, dyn, 2)
        dp_ref[:, :d] = (dz * y).astype(BF16)
        dp_ref[:, d:2 * d] = (du * h).astype(BF16)
        dp_ref[:, 2 * d:] = (du * c).astype(BF16)

        @pl.when(i == 0)
        def _():
            dw_ref[...] = jnp.zeros_like(dw_ref)

        dw_ref[0:1, :] += jnp.sum(dy * u2, axis=0, keepdims=True)
        dw_ref[1:2, :] += jnp.sum(dy * u1, axis=0, keepdims=True)
        dw_ref[2:3, :] += jnp.sum(dy * u, axis=0, keepdims=True)

    return pl.pallas_call(
        body, name=name, grid=(nt,),
        in_specs=[_row_spec(t, d3),
                  pl.BlockSpec((HALO, d3), lambda i: (jnp.maximum(i * hb - 1, 0), 0)),
                  pl.BlockSpec((HALO, d3), lambda i: (jnp.minimum((i + 1) * hb, last_halo), 0)),
                  _row_spec(t, d),
                  pl.BlockSpec((HALO, d), lambda i: (jnp.minimum((i + 1) * hb, last_halo), 0)),
                  pl.BlockSpec((3, d), lambda i: (0, 0))],
        out_specs=[_row_spec(t, d3), pl.BlockSpec((3, d), lambda i: (0, 0))],
        out_shape=[jax.ShapeDtypeStruct((s, d3), BF16), jax.ShapeDtypeStruct((3, d), F32)],
        compiler_params=_params(("arbitrary",)),
    )(p, p, p, dz, dz, cw)


FFN_ROWS, FFN_COLS = (1024, 512, 256), (1408, 768, 256, 128)


def _row_chunks(tm, rows=256):
    return [slice(r, r + min(rows, tm)) for r in range(0, tm, min(rows, tm))]


def _ffn_in_swiglu(xn, w_in, name):
    s, k = xn.shape
    ff = w_in.shape[1] // 2
    tm, tn = _pick(s, FFN_ROWS), _pick(ff, FFN_COLS)
    nj = ff // tn

    def body(x_ref, wg_ref, wu_ref, f_ref, a_ref):
        for rows in _row_chunks(tm):
            xv = x_ref[rows, :]
            gate = jnp.dot(xv, wg_ref[...], preferred_element_type=F32)
            up = jnp.dot(xv, wu_ref[...], preferred_element_type=F32)
            f_ref[0, rows, :] = gate.astype(BF16)
            f_ref[1, rows, :] = up.astype(BF16)
            a_ref[rows, :] = (gate * jax.nn.sigmoid(gate) * up).astype(BF16)

    return pl.pallas_call(
        body, name=name, grid=(nj, s // tm),
        in_specs=[pl.BlockSpec((tm, k), lambda j, i: (i, 0)),
                  pl.BlockSpec((k, tn), lambda j, i: (0, j)),
                  pl.BlockSpec((k, tn), lambda j, i: (0, nj + j))],
        out_specs=[pl.BlockSpec((2, tm, tn), lambda j, i: (0, i, j)), pl.BlockSpec((tm, tn), lambda j, i: (i, j))],
        out_shape=[jax.ShapeDtypeStruct((2, s, ff), BF16), jax.ShapeDtypeStruct((s, ff), BF16)],
        compiler_params=_params(("parallel", "parallel")),
    )(xn, w_in, w_in)


def _ffn_out_dx_swiglu(dff, w_out, f, name):
    s, d = dff.shape
    ff = w_out.shape[0]
    tm, tn = _pick(s, FFN_ROWS), _pick(ff, FFN_COLS)

    def body(d_ref, w_ref, f_ref, df_ref):
        for rows in _row_chunks(tm):
            da = lax.dot_general(d_ref[rows, :], w_ref[...], (((1,), (1,)), ((), ())), preferred_element_type=F32)
            gate = f_ref[0, rows, :].astype(F32)
            up = f_ref[1, rows, :].astype(F32)
            sg = jax.nn.sigmoid(gate)
            silu = gate * sg
            df_ref[0, rows, :] = (da * up * (sg + silu * (1.0 - sg))).astype(BF16)
            df_ref[1, rows, :] = (da * silu).astype(BF16)

    planes = pl.BlockSpec((2, tm, tn), lambda j, i: (0, i, j))
    return pl.pallas_call(
        body, name=name, grid=(ff // tn, s // tm),
        in_specs=[pl.BlockSpec((tm, d), lambda j, i: (i, 0)), pl.BlockSpec((tn, d), lambda j, i: (j, 0)), planes],
        out_specs=planes, out_shape=jax.ShapeDtypeStruct((2, s, ff), BF16),
        compiler_params=_params(("parallel", "parallel")),
    )(dff, w_out, f)


SUPER = 2048
RES = 16
PAIR = 128
L = 128
FWD_TOGETHER = 16
BWD_TOGETHER = 4


def _alibi_slopes(n_heads):
    h = np.arange(n_heads, dtype=np.float32) + 1.0
    return np.power(2.0, -8.0 * h / n_heads).astype(np.float32)


def _permute16(x, inverse, name):
    s, d = x.shape
    cw = LANE

    def body(x_ref, o_ref):
        if inverse:
            for m in range(L):
                o_ref[RES * m:RES * (m + 1), :] = x_ref[pl.ds(m, RES, stride=L), :]
        else:
            for r in range(RES):
                o_ref[L * r:L * (r + 1), :] = x_ref[pl.ds(r, L, stride=RES), :]

    spec = pl.BlockSpec((SUPER, cw), lambda i, j: (i, j))
    return pl.pallas_call(
        body, name=name, grid=(s // SUPER, d // cw), in_specs=[spec], out_specs=spec,
        out_shape=jax.ShapeDtypeStruct((s, d), x.dtype),
        compiler_params=_params(("parallel", "parallel")),
    )(x)


def _slope_table(d):
    nh = d // HEAD_DIM
    sl = _alibi_slopes(nh)
    tab = np.repeat(sl, HEAD_DIM).reshape(d // PAIR, 1, PAIR)
    return jnp.asarray(np.broadcast_to(tab, (d // PAIR, 8, PAIR)).copy())


def _geometry(dil):
    nch = RES // dil
    return nch, L // nch


def _band(dil):
    nch, w = _geometry(dil)
    sh = w.bit_length() - 1
    i = lax.broadcasted_iota(jnp.int32, (L, 2 * L), 0)
    j = lax.broadcasted_iota(jnp.int32, (L, 2 * L), 1)

    def pos(t):
        return jnp.bitwise_and(t, w - 1) * nch + jnp.right_shift(t, sh)

    delta = pos(i) + L - (pos(jnp.bitwise_and(j, L - 1)) + jnp.bitwise_and(j, L))
    return (delta * dil).astype(F32), (delta >= 0) & (delta <= L), j < L


def _fill_bias(bias_s, sl_ref):
    for b, dil in enumerate(DILATIONS):
        base, band, prev_half = _band(dil)
        for first in range(2):
            valid = band & jnp.logical_not(prev_half) if first else band
            for h in range(2):
                slope = sl_ref[0:1, HEAD_DIM * h:HEAD_DIM * h + 1]
                bias_s[(2 * b + first) * 2 + h] = jnp.where(valid, -slope * base, NEG_BIG)


def _bias_index(b, sb, n):
    first = jnp.logical_and(sb == 0, n == 0).astype(jnp.int32)
    return (2 * b + first) * 2


def _offsets(dil, res, n):
    nch, w = _geometry(dil)

    def al(v):
        return v if isinstance(v, int) else pl.multiple_of(v, w)

    q_off = [al((a * dil + res) * L + n * w) for a in range(nch)]
    k_off = [al((a * dil + res) * 2 * L + L + n * w) for a in range(nch)]
    kp_off = [al((a * dil + res) * 2 * L + L + n * w - w) for a in range(nch)]
    return q_off, k_off, kp_off, w


def _gather(ref, offs, w):
    parts = [ref[pl.ds(o, w), :] for o in offs]
    return parts[0] if len(parts) == 1 else jnp.concatenate(parts, axis=0)


def _scatter(ref, offs, w, val, add=False):
    for a, o in enumerate(offs):
        piece = val[a * w:(a + 1) * w, :]
        if add:
            ref[pl.ds(o, w), :] += piece
        else:
            ref[pl.ds(o, w), :] = piece


def _fill_key_buffer(buf, prev_ref, cur_ref):
    for r in range(RES):
        buf[2 * L * r:2 * L * r + L, :] = prev_ref[L * r:L * (r + 1), :]
        buf[2 * L * r + L:2 * L * (r + 1), :] = cur_ref[L * r:L * (r + 1), :]


def _two_heads(x, low):
    zero = jnp.zeros_like(x)
    return jnp.concatenate([jnp.where(low, x, zero), jnp.where(low, zero, x)], axis=0)


def _loop_blocks(dil, stages, together):
    together = max(together, dil) if dil < RES else together

    def it(i, c):
        if dil == RES:
            blocks = [(i * together + k, 0) for k in range(together)]
        else:
            blocks = [(res, i * (together // dil) + k) for k in range(together // dil) for res in range(dil)]
        state = [stages[0](res, n) for res, n in blocks]
        for stage in stages[1:]:
            state = [stage(res, n, prev) for (res, n), prev in zip(blocks, state)]
        for writes in state:
            for args in writes:
                _scatter(*args)
        return c

    lax.fori_loop(0, RES // together, it, 0)


NT = (((1,), (1,)), ((), ()))
TN = (((0,), (0,)), ((), ()))


def _attention_fwd(q, kv, name):
    s, d = q.shape
    g_n, ns = d // PAIR, s // SUPER

    def body(sl_ref, q_ref, kc_ref, kp_ref, vc_ref, vp_ref, o_ref, lse_ref, kbuf, vbuf, m_s, l_s, acc_s, bias_s):
        sb = pl.program_id(1)
        _fill_key_buffer(kbuf, kp_ref, kc_ref)
        _fill_key_buffer(vbuf, vp_ref, vc_ref)
        pl.when(sb == 0)(lambda: _fill_bias(bias_s, sl_ref))
        low = lax.broadcasted_iota(jnp.int32, (L, PAIR), 1) < HEAD_DIM
        low_k = lax.broadcasted_iota(jnp.int32, (2 * L, PAIR), 1) < HEAD_DIM
        ones_bd = _two_heads(jnp.ones((2 * L, PAIR), BF16), low_k)
        for bi, dil in enumerate(DILATIONS):
            first_branch, last_branch = bi == 0, bi == len(DILATIONS) - 1

            def scores(res, n, dil=dil):
                q_off, k_off, kp_off, w = _offsets(dil, res, n)
                qf = _gather(q_ref, q_off, w).astype(BF16)
                kcat = jnp.concatenate([_gather(kbuf, kp_off, w), _gather(kbuf, k_off, w)], axis=0).astype(BF16)
                return lax.dot_general(qf, _two_heads(kcat, low_k), NT, preferred_element_type=F32)

            def update(res, n, sc, bi=bi, dil=dil, first_branch=first_branch, last_branch=last_branch):
                q_off, k_off, kp_off, w = _offsets(dil, res, n)
                vcat = jnp.concatenate([_gather(vbuf, kp_off, w), _gather(vbuf, k_off, w)], axis=0).astype(BF16)
                v_ones = jnp.concatenate([_two_heads(vcat, low_k), ones_bd], axis=1)
                bias_at = _bias_index(bi, sb, n)
                if not first_branch:
                    m_prev = _gather(m_s, q_off, w)
                ps, m_new = [], []
                for h in range(2):
                    p_rows, m_rows = [], []
                    for rows in (slice(0, L // 2), slice(L // 2, L)):
                        s_h = sc[rows, 2 * L * h:2 * L * (h + 1)] + bias_s[bias_at + h, rows, :]
                        mh = jnp.max(s_h, axis=1, keepdims=True)
                        if not first_branch:
                            mh = jnp.maximum(mh, m_prev[rows, HEAD_DIM * h:HEAD_DIM * h + 1])
                        p_rows.append(jnp.exp(s_h - mh).astype(BF16))
                        m_rows.append(mh)
                    ps.append(jnp.concatenate(p_rows, axis=0))
                    m_new.append(jnp.concatenate(m_rows, axis=0))
                m_full = jnp.where(low, m_new[0], m_new[1])
                both = jnp.dot(jnp.concatenate(ps, axis=1), v_ones, preferred_element_type=F32)
                acc, l_full = both[:, :PAIR], both[:, PAIR:]
                if not first_branch:
                    alpha = jnp.exp(m_prev - m_full)
                    l_full = _gather(l_s, q_off, w) * alpha + l_full
                    acc = _gather(acc_s, q_off, w) * alpha + acc
                if last_branch:
                    return [(o_ref, q_off, w, acc / l_full, False), (lse_ref, q_off, w, m_full + jnp.log(l_full), False)]
                return [(m_s, q_off, w, m_full, False), (l_s, q_off, w, l_full, False), (acc_s, q_off, w, acc, False)]

            _loop_blocks(dil, [scores, update], FWD_TOGETHER)

    prev = lambda i: jnp.maximum(i - 1, 0)
    blk = pl.BlockSpec((SUPER, PAIR), lambda g, i: (i, g))
    in_specs = [pl.BlockSpec((None, 8, PAIR), lambda g, i: (g, 0, 0)), blk,
                pl.BlockSpec((SUPER, PAIR), lambda g, i: (i, g)),
                pl.BlockSpec((SUPER, PAIR), lambda g, i: (prev(i), g)),
                pl.BlockSpec((SUPER, PAIR), lambda g, i: (i, g_n + g)),
                pl.BlockSpec((SUPER, PAIR), lambda g, i: (prev(i), g_n + g))]
    return pl.pallas_call(
        body, name=name, grid=(g_n, ns), in_specs=in_specs, out_specs=[blk, blk],
        out_shape=[jax.ShapeDtypeStruct((s, d), F32)] * 2,
        scratch_shapes=([pltpu.VMEM((2 * SUPER, PAIR), F32)] * 2 + [pltpu.VMEM((SUPER, PAIR), F32)] * 3
                        + [pltpu.VMEM((4 * len(DILATIONS), L, 2 * L), F32)]),
        compiler_params=_params(("parallel", "arbitrary")),
    )(_slope_table(d), q, kv, kv, kv, kv)


def _attention_bwd(q, kv, o, do, lse, dkv_in, name):
    s, d = q.shape
    g_n, ns = d // PAIR, s // SUPER
    has_in = dkv_in is not None

    def body(*refs):
        sl_ref, q_ref, do_ref, o_ref, lse_ref, kc_ref, kp_ref, vc_ref, vp_ref = refs[:9]
        pos = 9
        if has_in:
            dkv_in_ref = refs[9]
            pos = 10
        dq_ref, dkv_ref, kbuf, vbuf, dkbuf, dvbuf, dq_s, bias_s = refs[pos:]
        step = pl.program_id(1)
        sb = ns - 1 - step
        _fill_key_buffer(kbuf, kp_ref, kc_ref)
        _fill_key_buffer(vbuf, vp_ref, vc_ref)

        @pl.when(step == 0)
        def _():
            _fill_bias(bias_s, sl_ref)
            dkbuf[...] = jnp.zeros_like(dkbuf)
            dvbuf[...] = jnp.zeros_like(dvbuf)

        @pl.when(step > 0)
        def _():
            for buf in (dkbuf, dvbuf):
                for r in range(RES):
                    buf[2 * L * r + L:2 * L * (r + 1), :] = buf[2 * L * r:2 * L * r + L, :]
                    buf[2 * L * r:2 * L * r + L, :] = jnp.zeros((L, PAIR), F32)

        low = lax.broadcasted_iota(jnp.int32, (L, PAIR), 1) < HEAD_DIM
        low_k = lax.broadcasted_iota(jnp.int32, (2 * L, PAIR), 1) < HEAD_DIM
        low_t = lax.broadcasted_iota(jnp.int32, (PAIR, 2 * L), 0) < HEAD_DIM
        for bi, dil in enumerate(DILATIONS):
            first_branch = bi == 0

            def scores(res, n, dil=dil):
                q_off, k_off, kp_off, w = _offsets(dil, res, n)
                qb = _gather(q_ref, q_off, w).astype(BF16)
                dof = _gather(do_ref, q_off, w)
                prod = dof * _gather(o_ref, q_off, w)
                dob = dof.astype(BF16)
                lse_f = _gather(lse_ref, q_off, w)
                zero = jnp.zeros_like(prod)
                dsum = (jnp.sum(jnp.where(low, prod, zero), axis=1, keepdims=True),
                        jnp.sum(jnp.where(low, zero, prod), axis=1, keepdims=True))
                kcat = jnp.concatenate([_gather(kbuf, kp_off, w), _gather(kbuf, k_off, w)], axis=0).astype(BF16)
                vcat = jnp.concatenate([_gather(vbuf, kp_off, w), _gather(vbuf, k_off, w)], axis=0).astype(BF16)
                k_bd, v_bd = _two_heads(kcat, low_k), _two_heads(vcat, low_k)
                sc = lax.dot_general(qb, k_bd, NT, preferred_element_type=F32)
                dp = lax.dot_general(dob, v_bd, NT, preferred_element_type=F32)
                return qb, dob, lse_f, dsum, k_bd, sc, dp

            def gradients(res, n, given, bi=bi, dil=dil, first_branch=first_branch):
                qb, dob, lse_f, dsum, k_bd, sc, dp = given
                q_off, k_off, kp_off, w = _offsets(dil, res, n)
                bias_at = _bias_index(bi, sb, n)
                ps, dss = [], []
                for h in range(2):
                    cols = slice(2 * L * h, 2 * L * (h + 1))
                    lse_h = lse_f[:, HEAD_DIM * h:HEAD_DIM * h + 1]
                    p_h = jnp.exp(sc[:, cols] + bias_s[bias_at + h] - lse_h)
                    dss.append((p_h * (dp[:, cols] - dsum[h])).astype(BF16))
                    ps.append(p_h.astype(BF16))
                ds_cat, p_cat = jnp.concatenate(dss, axis=1), jnp.concatenate(ps, axis=1)
                dq = jnp.dot(ds_cat, k_bd, preferred_element_type=F32)
                dk_t = lax.dot_general(qb, ds_cat, TN, preferred_element_type=F32)
                dv_t = lax.dot_general(dob, p_cat, TN, preferred_element_type=F32)
                dk = jnp.where(low_t, dk_t[:, :2 * L], dk_t[:, 2 * L:]).T
                dv = jnp.where(low_t, dv_t[:, :2 * L], dv_t[:, 2 * L:]).T
                return [(dq_s, q_off, w, dq, not first_branch),
                        (dkbuf, kp_off, w, dk[:L], True), (dkbuf, k_off, w, dk[L:], True),
                        (dvbuf, kp_off, w, dv[:L], True), (dvbuf, k_off, w, dv[L:], True)]

            _loop_blocks(dil, [scores, gradients], BWD_TOGETHER)

        dq_ref[...] = dq_s[...].astype(BF16)
        for r in range(RES):
            rows, cur = slice(L * r, L * (r + 1)), slice(2 * L * r + L, 2 * L * (r + 1))
            for plane, buf in enumerate((dkbuf, dvbuf)):
                if has_in:
                    dkv_ref[plane, rows, :] = buf[cur, :] + dkv_in_ref[plane, rows, :]
                else:
                    dkv_ref[plane, rows, :] = buf[cur, :]

    rev = lambda i: ns - 1 - i
    prev = lambda i: jnp.maximum(ns - 2 - i, 0)
    blk = pl.BlockSpec((SUPER, PAIR), lambda g, i: (rev(i), g))
    in_specs = [pl.BlockSpec((None, 8, PAIR), lambda g, i: (g, 0, 0)), blk, blk, blk, blk,
                pl.BlockSpec((SUPER, PAIR), lambda g, i: (rev(i), g)),
                pl.BlockSpec((SUPER, PAIR), lambda g, i: (prev(i), g)),
                pl.BlockSpec((SUPER, PAIR), lambda g, i: (rev(i), g_n + g)),
                pl.BlockSpec((SUPER, PAIR), lambda g, i: (prev(i), g_n + g))]
    ins = [_slope_table(d), q, do, o, lse, kv, kv, kv, kv]
    planes = pl.BlockSpec((2, SUPER, PAIR), lambda g, i: (0, rev(i), g))
    if has_in:
        in_specs.append(planes)
        ins.append(dkv_in)
    res = pl.pallas_call(
        body, name=name, grid=(g_n, ns), in_specs=in_specs, out_specs=[blk, planes],
        out_shape=[jax.ShapeDtypeStruct((s, d), BF16), jax.ShapeDtypeStruct((2, s, d), F32)],
        scratch_shapes=([pltpu.VMEM((2 * SUPER, PAIR), F32)] * 4 + [pltpu.VMEM((SUPER, PAIR), F32)]
                        + [pltpu.VMEM((4 * len(DILATIONS), L, 2 * L), F32)]),
        compiler_params=_params(("parallel", "arbitrary")),
    )(*ins)
    return res[0], res[1]


def _coords():
    return lax.axis_index("x"), lax.axis_index("y"), lax.axis_index("c")


def _chip_peers(x, y):
    return [(1 - x, y), (x, 1 - y), (1 - x, 1 - y)]


def _block_of(ref, axis, blk, size):
    start = pl.multiple_of(blk * size, size)
    if axis == 1:
        return ref.at[:, pl.ds(start, size)]
    return ref.at[pl.ds(start, size), :]


ANY = pl.BlockSpec(memory_space=pl.ANY)


HBM = pl.BlockSpec(memory_space=pltpu.HBM)
SEM = pl.BlockSpec(memory_space=pltpu.SEMAPHORE)
SPLIT = pltpu.CompilerParams(has_side_effects=pltpu.SideEffectType.DATAFLOW_SIDE_EFFECTING)


def _in_hbm(a):
    return pltpu.with_memory_space_constraint(a, pltpu.HBM)


def _thru(arrays):
    return [pltpu.HBM(a.shape, a.dtype) for a in arrays]


def _cast_place(w, layer, ax, dtype, after, name):
    _, k, n = w.shape
    t = _pick(k, (512, 256, 128))
    nb = k // t
    extra = [] if after is None else [after]

    def body(blk_ref, w_ref, *refs):
        b_ref, f_ref = refs[len(extra):]
        v = w_ref[...].astype(dtype)
        b_ref[...] = v
        f_ref[...] = v

    full_shape = (k, 4 * n) if ax == 1 else (4 * k, n)
    place = (lambda i, blk: (i, blk[0])) if ax == 1 else (lambda i, blk: (blk[0] * nb + i, 0))
    return pl.pallas_call(
        body, name=name,
        grid_spec=pltpu.PrefetchScalarGridSpec(
            num_scalar_prefetch=1, grid=(nb,),
            in_specs=[pl.BlockSpec((None, t, n), lambda i, blk: (layer, i, 0))] + [ANY] * len(extra),
            out_specs=[pl.BlockSpec((t, n), lambda i, blk: (i, 0)), pl.BlockSpec((t, n), place)]),
        out_shape=[jax.ShapeDtypeStruct((k, n), dtype), jax.ShapeDtypeStruct(full_shape, dtype)],
        compiler_params=_params(("parallel",)),
    )(_my_block()[None], w, *extra)


def _my_block():
    return (2 * lax.axis_index("x") + lax.axis_index("y")).astype(jnp.int32)


def _gather_start(group, carry, name):
    n, nc = len(group), len(carry)

    def body(*refs):
        blocks, fulls, send_sem, recv_sem = refs[:n], refs[n:2 * n], refs[2 * n + nc], refs[2 * n + nc + 1]
        x, y, c = _coords()
        for t, (b, _, ax) in enumerate(group):
            mine = _block_of(fulls[t], ax, 2 * x + y, b.shape[ax])
            for j, (px, py) in enumerate(_chip_peers(x, y)):
                pltpu.make_async_remote_copy(
                    src_ref=blocks[t], dst_ref=mine, send_sem=send_sem.at[3 * t + j], recv_sem=recv_sem.at[3 * t + j],
                    device_id=(px, py, c), device_id_type=MESH).start()

    arrays = [b for b, _, _ in group] + [f for _, f, _ in group] + list(carry)
    sems = [pltpu.SemaphoreType.DMA((3 * n,))] * 2
    res = pl.pallas_call(
        body, name=name, in_specs=[HBM] * len(arrays), out_specs=[SEM, SEM] + [HBM] * len(arrays),
        out_shape=sems + _thru(arrays), input_output_aliases={i: 2 + i for i in range(len(arrays))},
        compiler_params=SPLIT,
    )(*[_in_hbm(a) for a in arrays])
    return (res[0], res[1], list(res[2:2 + n]), list(res[2 + n:2 + 2 * n])), list(res[2 + 2 * n:])


def _gather_wait(group, started, after, name):
    sends, recvs, blocks, fulls = started
    m = len(group)

    def body(*refs):
        blk_refs, full_refs, send_sem, recv_sem = refs[:m], refs[m:2 * m], refs[2 * m], refs[2 * m + 1]
        x, y, c = _coords()
        for t, (b, _, ax) in enumerate(group):
            for j, (px, py) in enumerate(_chip_peers(x, y)):
                cp = pltpu.make_async_remote_copy(
                    src_ref=blk_refs[t], dst_ref=_block_of(full_refs[t], ax, 2 * px + py, b.shape[ax]),
                    send_sem=send_sem.at[3 * t + j], recv_sem=recv_sem.at[3 * t + j],
                    device_id=(px, py, c), device_id_type=MESH)
                cp.wait_send()
                cp.wait_recv()

    extra = [] if after is None else [after]
    res = pl.pallas_call(
        body, name=name, in_specs=[HBM] * (2 * m) + [SEM, SEM] + [ANY] * len(extra), out_specs=[HBM] * (2 * m),
        out_shape=_thru(blocks) + _thru(fulls), input_output_aliases={i: i for i in range(2 * m)},
        compiler_params=SPLIT,
    )(*blocks, *fulls, sends, recvs, *extra)
    return list(res[m:])


def _scatter_start(grads, carry, name):
    n = len(grads)
    n_in = 2 * n + len(carry)

    def body(*refs):
        g_refs, st_refs, send_sem, recv_sem = refs[:n], refs[n:2 * n], refs[n_in], refs[n_in + 1]
        x, y, c = _coords()
        for t, (g, ax) in enumerate(grads):
            for j, (px, py) in enumerate(_chip_peers(x, y)):
                pltpu.make_async_remote_copy(
                    src_ref=_block_of(g_refs[t], ax, 2 * px + py, g.shape[ax] // 4), dst_ref=st_refs[t].at[j],
                    send_sem=send_sem.at[3 * t + j], recv_sem=recv_sem.at[3 * t + j],
                    device_id=(px, py, c), device_id_type=MESH).start()

    arrays = [g for g, _ in grads]
    for g, ax in grads:
        shape = list(g.shape)
        shape[ax] //= 4
        arrays.append(lax.empty((3, *shape), g.dtype))
    arrays += list(carry)
    sems = [pltpu.SemaphoreType.DMA((3 * n,))] * 2
    res = pl.pallas_call(
        body, name=name, in_specs=[HBM] * n_in, out_specs=[SEM, SEM] + [HBM] * n_in,
        out_shape=sems + _thru(arrays), input_output_aliases={i: 2 + i for i in range(n_in)},
        compiler_params=SPLIT,
    )(*[_in_hbm(a) for a in arrays])
    return (res[0], res[1], list(res[2:2 + n]), list(res[2 + n:2 + 2 * n])), list(res[2 + 2 * n:])


def _scatter_wait(axes, started, after, name):
    sends, recvs, full, stacks = started
    n = len(full)
    extra = [] if after is None else [after]

    def body(*refs):
        g_refs, st_refs, send_sem, recv_sem = refs[:n], refs[n:2 * n], refs[2 * n], refs[2 * n + 1]
        x, y, c = _coords()
        for t, ax in enumerate(axes):
            size = full[t].shape[ax] // 4
            for j, (px, py) in enumerate(_chip_peers(x, y)):
                cp = pltpu.make_async_remote_copy(
                    src_ref=_block_of(g_refs[t], ax, 2 * px + py, size), dst_ref=st_refs[t].at[j],
                    send_sem=send_sem.at[3 * t + j], recv_sem=recv_sem.at[3 * t + j],
                    device_id=(px, py, c), device_id_type=MESH)
                cp.wait_send()
                cp.wait_recv()

    res = pl.pallas_call(
        body, name=name, in_specs=[HBM] * (2 * n) + [SEM, SEM] + [ANY] * len(extra), out_specs=[HBM] * (2 * n),
        out_shape=_thru(full) + _thru(stacks), input_output_aliases={i: i for i in range(2 * n)},
        compiler_params=SPLIT,
    )(*full, *stacks, sends, recvs, *extra)
    return list(res[:n]), list(res[n:])


def _pair_copies(g_refs, st_refs, out_refs, items, send_sem, recv_sem):
    x, y, c = _coords()
    copies = []
    for u, (g, ax, _) in enumerate(items):
        own = _block_of(g_refs[u], ax, 2 * x + y, g.shape[ax] // 4)
        for k, (src, dst) in enumerate([(own, out_refs[u].at[0]), (st_refs[u], out_refs[u].at[pl.ds(1, 3)])]):
            copies.append(pltpu.make_async_remote_copy(
                src_ref=src, dst_ref=dst, send_sem=send_sem.at[2 * u + k], recv_sem=recv_sem.at[2 * u + k],
                device_id=(x, y, 1 - c), device_id_type=MESH))
    return copies


def _pair_start(items, carry, name):
    n = len(items)
    n_in = 3 * n + len(carry)

    def body(*refs):
        for cp in _pair_copies(refs[:n], refs[n:2 * n], refs[2 * n:3 * n], items, refs[n_in], refs[n_in + 1]):
            cp.start()

    arrays = ([g for g, _, _ in items] + [st for _, _, st in items]
              + [lax.empty((4, *st.shape[1:]), st.dtype) for _, _, st in items] + list(carry))
    sems = [pltpu.SemaphoreType.DMA((2 * n,))] * 2
    res = pl.pallas_call(
        body, name=name, in_specs=[HBM] * n_in, out_specs=[SEM, SEM] + [HBM] * n_in,
        out_shape=sems + _thru(arrays), input_output_aliases={i: 2 + i for i in range(n_in)},
        compiler_params=SPLIT,
    )(*[_in_hbm(a) for a in arrays])
    thru = res[2:]
    return (res[0], res[1], *(list(thru[k * n:(k + 1) * n]) for k in range(3))), list(thru[3 * n:])


def _pair_wait(axes, started, name):
    send, recv, full, stacks, landing = started
    n = len(full)
    items = [(full[u], axes[u], stacks[u]) for u in range(n)]

    def body(*refs):
        for cp in _pair_copies(refs[:n], refs[n:2 * n], refs[2 * n:3 * n], items, refs[3 * n], refs[3 * n + 1]):
            cp.wait_send()
            cp.wait_recv()

    res = pl.pallas_call(
        body, name=name, in_specs=[HBM] * (3 * n) + [SEM, SEM], out_specs=[HBM] * (3 * n),
        out_shape=_thru(full + stacks + landing), input_output_aliases={i: i for i in range(3 * n)},
        compiler_params=SPLIT,
    )(*full, *stacks, *landing, send, recv)
    return list(res[:n]), list(res[n:2 * n]), list(res[2 * n:])


def _allreduce_small(v, name):
    r, cdim = v.shape

    def body(v_ref, out_ref, buf, send_sems, recv_sems):
        x, y, c = _coords()
        me = 4 * x + 2 * y + c
        buf[0] = v_ref[...]
        sends = []
        for k in range(1, 8):
            peer = (x if not (k & 4) else 1 - x, y if not (k & 2) else 1 - y, c if not (k & 1) else 1 - c)
            cp = pltpu.make_async_remote_copy(
                src_ref=v_ref, dst_ref=buf.at[k], send_sem=send_sems.at[k - 1], recv_sem=recv_sems.at[k - 1],
                device_id=peer, device_id_type=MESH)
            cp.start()
            sends.append(cp)
        for cp in sends:
            cp.wait_recv()
        total = buf[me]
        for src in range(1, 8):
            total = total + buf[jnp.bitwise_xor(me, src)]
        out_ref[...] = total
        for cp in sends:
            cp.wait_send()

    return pl.pallas_call(
        body, name=name,
        in_specs=[pl.BlockSpec(memory_space=pltpu.VMEM)], out_specs=pl.BlockSpec(memory_space=pltpu.VMEM),
        out_shape=jax.ShapeDtypeStruct((r, cdim), F32),
        scratch_shapes=[pltpu.VMEM((8, r, cdim), F32), pltpu.SemaphoreType.DMA((7,)), pltpu.SemaphoreType.DMA((7,))],
        compiler_params=pltpu.CompilerParams(has_side_effects=True),
    )(v)


def _adamw_math(w, g, m, v):
    m = ADAM_B1 * m + (1.0 - ADAM_B1) * g
    v = ADAM_B2 * v + (1.0 - ADAM_B2) * jnp.square(g)
    m_hat = m / (1.0 - ADAM_B1 ** ADAM_STEP)
    v_hat = v / (1.0 - ADAM_B2 ** ADAM_STEP)
    delta = -ADAM_LR * (m_hat / (jnp.sqrt(v_hat) + ADAM_EPS) + ADAM_WD * w)
    return delta, m, v


def _adamw(w, m, v, grads, name):
    r, cdim = w.shape
    paired = isinstance(grads, list)
    layers = len(grads) if paired else 1
    t = _pick(r // layers, (128, 64, 32, 16, 8))
    per_layer = r // layers // t
    n_grad = 3 * layers if paired else 1

    def body(*refs):
        refs = refs[1:] if paired else refs
        w_ref, m_ref, v_ref = refs[:3]
        outs = refs[3 + n_grad:]

        def update(g):
            delta, m_new, v_new = _adamw_math(w_ref[...], g, m_ref[...], v_ref[...])
            outs[0][...] = g
            outs[1][...] = delta
            outs[2][...] = m_new
            outs[3][...] = v_new

        if not paired:
            update(refs[3][...])
            return
        layer = pl.program_id(0) // per_layer
        for l in range(layers):
            @pl.when(layer == l)
            def _(own_ref=refs[3 + 3 * l], st_ref=refs[4 + 3 * l], sib_ref=refs[5 + 3 * l]):
                sa = own_ref[...].astype(F32)
                sb = sib_ref[0].astype(F32)
                for k in range(3):
                    sa = sa + st_ref[k].astype(F32)
                    sb = sb + sib_ref[k + 1].astype(F32)
                update(sa + sb)

    out_shape = [jax.ShapeDtypeStruct((r, cdim), F32)] * 4
    if not paired:
        spec = pl.BlockSpec((t, cdim), lambda i: (i, 0))
        return pl.pallas_call(
            body, name=name, grid=(r // t,), in_specs=[spec] * 4, out_specs=[spec] * 4, out_shape=out_shape,
            compiler_params=_params(("parallel",)),
        )(w, m, v, grads)

    spec = pl.BlockSpec((t, cdim), lambda i, blk: (i, 0))
    ins, in_specs = [w, m, v], [spec] * 3
    for l, (g, ax, stack, sib) in enumerate(grads):
        row = lambda i, l=l: jnp.clip(i - l * per_layer, 0, per_layer - 1)
        own = ((lambda i, blk, row=row: (row(i), blk[0])) if ax == 1
               else (lambda i, blk, row=row: (blk[0] * per_layer + row(i), 0)))
        ins += [g, stack, sib]
        in_specs += [pl.BlockSpec((t, cdim), own),
                     pl.BlockSpec((3, t, cdim), lambda i, blk, row=row: (0, row(i), 0)),
                     pl.BlockSpec((4, t, cdim), lambda i, blk, row=row: (0, row(i), 0))]
    return pl.pallas_call(
        body, name=name,
        grid_spec=pltpu.PrefetchScalarGridSpec(
            num_scalar_prefetch=1, grid=(r // t,), in_specs=in_specs, out_specs=[spec] * 4),
        out_shape=out_shape, compiler_params=_params(("parallel",)),
    )(_my_block()[None], *ins)


def _local_step(x, target, gains, conv_ws, kv_gain, weights_of, send_grads):
    depth = len(gains)
    n_a = len(conv_ws)
    saved, ws = [], []
    kv = kvn = None
    _, (xn,) = _norm_res_fwd(x, None, None, [gains[0][0]], "norm_first")
    h = x
    for l in range(depth):
        g = gains[l]
        sv = {"x_in": h, "xn": xn}
        w = weights_of(l, "mix", h)
        ws.append(w)
        if l == n_a:
            kv = _mm(kvn, w["kv"], "nn", F32, "kv_fwd")
        if l < n_a:
            p = _mm(xn, w["conv_in"], "nn", BF16, f"conv_in_fwd_{l}")
            z = _conv_gate_fwd(p, conv_ws[l], f"conv_gate_fwd_{l}")
            mix, x1, xn2 = _mm_norm_res(z, w["conv_out"], h, g[1], g[2], f"conv_out_fwd_{l}")
            sv.update(p=p, z=z)
        else:
            j = l - n_a
            q = _mm(xn, w["q"], "nn", F32, f"q_fwd_{j}", scale=HEAD_DIM ** -0.5)
            o, lse = _attention_fwd(q, kv, f"attn_fwd_{j}")
            mix, x1, xn2 = _mm_norm_res(o, w["o"], h, g[1], g[2], f"o_fwd_{j}")
            sv.update(q=q, o=o, lse=lse)
        w.update(weights_of(l, "ffn", mix))
        f, a = _ffn_in_swiglu(xn2, w["ffn_in"], f"ffn_in_fwd_{l}")
        ff = _mm(a, w["ffn_out"], "nn", BF16, f"ffn_out_fwd_{l}")
        sv.update(mix=mix, x1=x1, xn2=xn2, f=f, a=a, ff=ff)
        saved.append(sv)
        if l == depth - 1:
            dx, loss = _norm_res_loss(x1, ff, g[3], target, "norm_loss")
        elif l == n_a - 1:
            h, _ = _norm_res_fwd(x1, ff, g[3], [], f"norm_end_{l}")
            h = _permute16(h, False, "permute_stream")
            _, (xn, kvn) = _norm_res_fwd(h, None, None, [gains[l + 1][0], kv_gain], "norm_permuted")
        else:
            h, (xn,) = _norm_res_fwd(x1, ff, g[3], [gains[l + 1][0]], f"norm_end_{l}")
    d_gains = [[None] * 4 for _ in range(depth)]
    d_conv = [None] * n_a
    d_kv_gain = None
    dkv = None
    _, _, dff, d_gains[depth - 1][3] = _norm_bwd(dx, [], None, (saved[-1]["ff"], gains[-1][3]), "norm_loss_bwd")
    for l in reversed(range(depth)):
        sv, g, w, grads = saved[l], gains[l], ws[l], {}
        grads["ffn_out"] =_mm(sv["a"], dff, "tn", BF16, f"ffn_out_dw_{l}")
        df = _ffn_out_dx_swiglu(dff, w["ffn_out"], sv["f"], f"ffn_out_dx_{l}")
        dxn2 = _mm(df, w["ffn_in"], "nt", BF16, f"ffn_in_dx_{l}")
        grads["ffn_in"] =_mm(sv["xn2"], df, "tn", BF16, f"ffn_in_dw_{l}")
        dx, (d_gains[l][2],), dmix, d_gains[l][1] = _norm_bwd(
            dx, [(dxn2, g[2])], sv["x1"], (sv["mix"], g[1]), f"norm_mid_bwd_{l}")
        dx, dmix = send_grads(l, "ffn", grads, [dx, dmix])
        if l < n_a:
            dz = _mm(dmix, w["conv_out"], "nt", BF16, f"conv_out_dx_{l}")
            grads["conv_out"] =_mm(sv["z"], dmix, "tn", BF16, f"conv_out_dw_{l}")
            dp, d_conv[l] = _conv_gate_bwd(sv["p"], dz, conv_ws[l], f"conv_gate_bwd_{l}")
            dxn = _mm(dp, w["conv_in"], "nt", BF16, f"conv_in_dx_{l}")
            grads["conv_in"] =_mm(sv["xn"], dp, "tn", BF16, f"conv_in_dw_{l}")
        else:
            j = l - n_a
            do = _mm(dmix, w["o"], "nt", F32, f"o_dx_{j}")
            grads["o"] =_mm(sv["o"], dmix, "tn", BF16, f"o_dw_{j}")
            dq, dkv = _attention_bwd(sv["q"], kv, sv["o"], do, sv["lse"], dkv, f"attn_bwd_{j}")
            scale = HEAD_DIM ** -0.5
            dxn = _mm(dq, w["q"], "nt", BF16, f"q_dx_{j}", scale=scale)
            grads["q"] =_mm(sv["xn"], dq, "tn", BF16, f"q_dw_{j}", scale=scale)
        branches = [(dxn, g[0])]
        if l == n_a:
            dkvn = _mm(dkv, w["kv"], "nt", BF16, "kv_dx")
            grads["kv"] =_mm(kvn, dkv, "tn", BF16, "kv_dw")
            branches.append((dkvn, kv_gain))
        post = (saved[l - 1]["ff"], gains[l - 1][3]) if l > 0 else None
        if l == n_a:
            dx, dgs, _, _ = _norm_bwd(dx, branches, sv["x_in"], None, f"norm_end_bwd_{l}")
            dx = _permute16(dx, True, "unpermute_stream")
            _, _, dff, dg_post = _norm_bwd(dx, [], None, post, "norm_boundary_bwd")
        else:
            dx, dgs, dff, dg_post = _norm_bwd(dx, branches, sv["x_in"], post, f"norm_end_bwd_{l}")
        if dff is None:
            send_grads(l, "mix", grads, [])
        else:
            dx, dff = send_grads(l, "mix", grads, [dx, dff])
        d_gains[l][0] = dgs[0]
        if l == n_a:
            d_kv_gain = dgs[1]
        if l > 0:
            d_gains[l - 1][3] = dg_post
    return loss, dx, d_gains, d_conv, d_kv_gain


BIG = (
    ("conv_in", 1), ("conv_out", 0), ("kv", 1), ("q", 0), ("o", 0), ("ffn_in", 1), ("ffn_out", 0))


def kernel(x, norm_g, conv_in_w, conv_w, conv_out_w, kv_norm_g, kv_w, q_w, o_w, ffn_in_w, ffn_out_w, loss_target, m_norm_g, m_conv_in_w, m_conv_w, m_conv_out_w, m_kv_norm_g, m_kv_w, m_q_w, m_o_w, m_ffn_in_w, m_ffn_out_w, v_norm_g, v_conv_in_w, v_conv_w, v_conv_out_w, v_kv_norm_g, v_kv_w, v_q_w, v_o_w, v_ffn_in_w, v_ffn_out_w):
    depth, _, dq = norm_g.shape
    d = 4 * dq
    n_a = conv_w.shape[0]
    big_w = {"conv_in": conv_in_w, "conv_out": conv_out_w, "kv": kv_w[None], "q": q_w, "o": o_w,
             "ffn_in": ffn_in_w, "ffn_out": ffn_out_w}
    big_m = {"conv_in": m_conv_in_w, "conv_out": m_conv_out_w, "kv": m_kv_w[None], "q": m_q_w, "o": m_o_w,
             "ffn_in": m_ffn_in_w, "ffn_out": m_ffn_out_w}
    big_v = {"conv_in": v_conv_in_w, "conv_out": v_conv_out_w, "kv": v_kv_w[None], "q": v_q_w, "o": v_o_w,
             "ffn_in": v_ffn_in_w, "ffn_out": v_ffn_out_w}

    n_gain, n_tap = depth * 4, n_a * conv_w.shape[1]
    small_rows = -(-(n_gain + n_tap + 1) // 8) * 8
    pad_rows = small_rows - n_gain - n_tap

    def pack_small(gains, taps):
        return jnp.concatenate([gains.reshape(n_gain, dq), taps.reshape(n_tap, dq), jnp.zeros((pad_rows, dq), F32)])

    axis_of = dict(BIG)

    def matrices_of(l, part):
        if part == "ffn":
            return [("ffn_in", l), ("ffn_out", l)]
        if l < n_a:
            return [("conv_in", l), ("conv_out", l)]
        return ([("kv", 0)] if l == n_a else []) + [("q", l - n_a), ("o", l - n_a)]

    halves = [(l, part) for l in range(depth) for part in ("mix", "ffn")]
    def placed(half, after):
        return [(*_cast_place(big_w[name], i, axis_of[name], BF16, after, f"place_{name}_{i}"), axis_of[name])
                for name, i in matrices_of(*half)]

    groups = {halves[0]: placed(halves[0], None)}
    groups[halves[0]].append((*_cast_place(pack_small(norm_g, conv_w)[None], 0, 1, F32, None, "place_small"), 1))
    started = {halves[0]: _gather_start(groups[halves[0]], [], "gather_start_0_mix")[0]}
    for half in halves[1:]:
        groups[half] = placed(half, started[halves[0]][2][0])

    AHEAD = 2

    def fetch(half, after):
        full = _gather_wait(groups[half], started[half], after, "gather_wait_%d_%s" % half)
        nxt = halves.index(half) + AHEAD
        if nxt < len(halves):
            started[halves[nxt]], full = _gather_start(groups[halves[nxt]], full, "gather_start_%d_%s" % halves[nxt])
        return full

    target = _permute16(loss_target.reshape(x.shape[1:]), False, "permute_target")
    for half in halves[1:AHEAD]:
        started[half], (target,) = _gather_start(groups[half], [target], "gather_start_%d_%s" % half)
    first = fetch(halves[0], target)
    small = first[-1]
    gains = [[small[4 * l + i][None] for i in range(4)] for l in range(depth)]
    conv_ws = [small[n_gain + 3 * l:n_gain + 3 * l + 3] for l in range(n_a)]
    kv_gain = kv_norm_g[None]

    def weights_of(l, part, after):
        full = first if (l, part) == halves[0] else fetch((l, part), after)
        return {name: full[t] for t, (name, _) in enumerate(matrices_of(l, part))}

    sent, paired = {}, {}
    LAG = 2

    def to_sibling(half, carry, after):
        axes = [axis_of[name] for name, _ in matrices_of(*half)]
        full, stacks = _scatter_wait(axes, sent[half], after, "scatter_wait_%d_%s" % half)
        paired[half], carry = _pair_start(list(zip(full, axes, stacks)), carry, "pair_start_%d_%s" % half)
        return carry

    def send_grads(l, part, grads, carry):
        sent[l, part], carry = _scatter_start(
            [(grads[name], axis_of[name]) for name, _ in matrices_of(l, part)], carry, f"scatter_start_{l}_{part}")
        at = halves.index((l, part))
        if carry:
            if at + LAG < len(halves):
                carry = to_sibling(halves[at + LAG], carry, carry[0])
        else:
            for older in range(at + LAG, at, -1):
                if halves[older] not in paired:
                    to_sibling(halves[older], [], sent[l, part][2][0])
        if at == 1 and carry:
            carry = to_sibling(halves[at + 1], carry, carry[0])
        return carry

    loss, dx, d_gains, d_conv, d_kv_gain = _local_step(
        x.reshape(x.shape[1:]), target, gains, conv_ws, kv_gain, weights_of, send_grads)
    loss = lax.psum(loss, ("x", "y", "c"))

    small_g = jnp.concatenate([dg for row in d_gains for dg in row] + list(d_conv) + [d_kv_gain]
                              + [jnp.zeros((pad_rows - 1, d), F32)])
    small_g = _allreduce_small(small_g, "allreduce_small")
    blk = 2 * lax.axis_index("x") + lax.axis_index("y")
    mine_small = lax.dynamic_slice_in_dim(small_g, blk * dq, dq, axis=1)
    kv_rows = d // dq

    def pack_opt(gains_like, taps_like, kv_like):
        rows = jnp.concatenate([gains_like.reshape(n_gain, dq), taps_like.reshape(n_tap, dq), kv_like.reshape(kv_rows, dq)])
        extra = -rows.shape[0] % 8
        return jnp.concatenate([rows, jnp.zeros((extra, dq), F32)]) if extra else rows

    sw = pack_opt(norm_g, conv_w, kv_norm_g)
    sm = pack_opt(m_norm_g, m_conv_w, m_kv_norm_g)
    sv = pack_opt(v_norm_g, v_conv_w, v_kv_norm_g)
    sg = pack_opt(mine_small[:n_gain], mine_small[n_gain:n_gain + n_tap], small_g[n_gain + n_tap])
    s_out = _adamw(sw, sm, sv, sg, "adamw_small")

    def unpack(a):
        return (a[:n_gain].reshape(depth, 4, dq), a[n_gain:n_gain + n_tap].reshape(n_a, -1, dq),
                a[n_gain + n_tap:n_gain + n_tap + kv_rows].reshape(d))

    small_out = [unpack(a) for a in s_out]

    landed, big_out = {}, {}

    def update(name):
        for half in halves:
            if matrices_of(*half)[0] not in landed and any(n == name for n, _ in matrices_of(*half)):
                axes = [axis_of[n] for n, _ in matrices_of(*half)]
                landed.update(zip(matrices_of(*half), zip(*_pair_wait(axes, paired[half], "pair_wait_%d_%s" % half))))
        shp, ax = big_w[name].shape, axis_of[name]
        rows, cols = shp[0] * shp[1], shp[2]
        flat = lambda a: a.reshape(rows, cols)
        full, stacks, sibling = zip(*[landed[name, i] for i in range(shp[0])])
        res = _adamw(flat(big_w[name]), flat(big_m[name]), flat(big_v[name]),
                     [(full[i], ax, stacks[i], sibling[i]) for i in range(shp[0])], f"adamw_{name}")
        big_out[name] = [a.reshape(shp[1:] if name == "kv" else shp) for a in res]

    pending = [half for half in reversed(halves) if half not in paired]
    for half in pending[:-1]:
        to_sibling(half, [], None)
    late = [name for name, _ in BIG if any(n == name for n, _ in matrices_of(*pending[-1]))]
    for name, _ in BIG:
        if name not in late:
            update(name)
    to_sibling(pending[-1], [], big_out["ffn_out"][0])
    for name in late:
        update(name)

    def leaves(i):
        ng, cw_, kg = small_out[i]
        return [ng, big_out["conv_in"][i], cw_, big_out["conv_out"][i], kg, big_out["kv"][i], big_out["q"][i],
                big_out["o"][i], big_out["ffn_in"][i], big_out["ffn_out"][i]]

    return (loss, dx.reshape(x.shape), *leaves(0), *leaves(1), *leaves(2), *leaves(3))
```

```python
import jax
import jax.numpy as jnp
import numpy as np
from jax import lax
from jax.experimental import pallas as pl
from jax.experimental.pallas import tpu as pltpu

F32 = jnp.float32
BF16 = jnp.bfloat16
HEAD_DIM = 64
DILATIONS = (1, 4, 16)
NORM_EPS = 1e-6
NEG_BIG = -1e30
VMEM_LIMIT = 48 * 1024 * 1024
ROW_TILE = 256
NORM_TILE = 512
LANE = 128
MESH = pl.DeviceIdType.MESH

ADAM_LR = 0.001
ADAM_B1 = 0.9
ADAM_B2 = 0.999
ADAM_EPS = 1e-08
ADAM_WD = 0.01
ADAM_STEP = 10

TILE_CANDIDATES = (1024, 1408, 768, 512, 384, 256, 128)


def _pick(dim, cands=TILE_CANDIDATES):
    for c in cands:
        if c <= dim and dim % c == 0:
            return c
    return dim


def _params(sem):
    return pltpu.CompilerParams(dimension_semantics=sem, vmem_limit_bytes=VMEM_LIMIT)


def _mm(a, b, mode, out_dtype, name, scale=None):
    a_planes = a.shape[0] if a.ndim == 3 else 1
    b_planes = b.shape[0] if b.ndim == 3 else 1
    if mode == "nn":
        m, k = a.shape[-2], a.shape[-1] * a_planes
        n = b.shape[1]
    elif mode == "nt":
        m, k = a.shape[-2], a.shape[-1] * a_planes
        n = b.shape[0]
    else:
        k, m = a.shape
        n = b.shape[-1] * b_planes
    tm, tn = _pick(m), _pick(n // b_planes)
    tk = _pick(k // a_planes, ((2048,) if mode == "tn" else (3072, 2816)) + TILE_CANDIDATES)
    nk = k // tk
    ka, nb = k // a_planes // tk, n // b_planes // tn
    if a_planes > 1:
        a_spec = pl.BlockSpec((None, tm, tk), lambda i, j, kk: (kk // ka, i, kk % ka))
    elif mode == "tn":
        a_spec = pl.BlockSpec((tk, tm), lambda i, j, kk: (kk, i))
    else:
        a_spec = pl.BlockSpec((tm, tk), lambda i, j, kk: (i, kk))
    if mode == "nn":
        b_spec = pl.BlockSpec((tk, tn), lambda i, j, kk: (kk, j))
        dims = (((1,), (0,)), ((), ()))
    elif mode == "nt":
        b_spec = pl.BlockSpec((tn, tk), lambda i, j, kk: (j, kk))
        dims = (((1,), (1,)), ((), ()))
    else:
        b_spec = (pl.BlockSpec((None, tk, tn), lambda i, j, kk: (j // nb, kk, j % nb)) if b_planes > 1
                  else pl.BlockSpec((tk, tn), lambda i, j, kk: (kk, j)))
        dims = (((0,), (0,)), ((), ()))

    def finish(acc):
        if scale is not None:
            acc = acc * scale
        return acc.astype(out_dtype)

    if nk == 1:
        def body(a_ref, b_ref, o_ref):
            o_ref[...] = finish(lax.dot_general(a_ref[...].astype(BF16), b_ref[...].astype(BF16), dims, preferred_element_type=F32))
        scratch = []
    else:
        def body(a_ref, b_ref, o_ref, acc_ref):
            kk = pl.program_id(2)

            @pl.when(kk == 0)
            def _():
                acc_ref[...] = jnp.zeros_like(acc_ref)

            acc_ref[...] += lax.dot_general(a_ref[...].astype(BF16), b_ref[...].astype(BF16), dims, preferred_element_type=F32)

            @pl.when(kk == nk - 1)
            def _():
                o_ref[...] = finish(acc_ref[...])
        scratch = [pltpu.VMEM((tm, tn), F32)]

    return pl.pallas_call(
        body, name=name,
        grid=(m // tm, n // tn, nk),
        in_specs=[a_spec, b_spec],
        out_specs=pl.BlockSpec((tm, tn), lambda i, j, kk: (i, j)),
        out_shape=jax.ShapeDtypeStruct((m, n), out_dtype),
        scratch_shapes=scratch,
        compiler_params=_params(("parallel", "parallel", "arbitrary")),
    )(a, b)


def _rstd(v):
    return lax.rsqrt(jnp.mean(v * v, axis=-1, keepdims=True) + NORM_EPS)


def _rms_bwd(dy, v, g, r):
    xhat = v * r
    gy = dy * g
    dv = r * (gy - xhat * jnp.mean(gy * xhat, axis=-1, keepdims=True))
    return dv, dy * xhat


def _row_spec(t, width):
    return pl.BlockSpec((t, width), lambda i: (i, 0))


def _gain_spec(width):
    return pl.BlockSpec((1, width), lambda i: (0, 0))


def _norm_res_fwd(x, mix, g_post, pre_gains, name):
    s, d = x.shape
    t = _pick(s, (NORM_TILE, ROW_TILE))
    has_mix = mix is not None
    n_pre = len(pre_gains)

    def body(*refs):
        x_ref = refs[0]
        pos = 1
        x1 = x_ref[...]
        if has_mix:
            mv = refs[1][...].astype(F32)
            x1 = x1 + mv * _rstd(mv) * refs[2][...]
            pos = 3
        gains = refs[pos:pos + n_pre]
        outs = refs[pos + n_pre:]
        if has_mix:
            outs[0][...] = x1
            outs = outs[1:]
        r = _rstd(x1)
        for g_ref, o_ref in zip(gains, outs):
            o_ref[...] = (x1 * r * g_ref[...]).astype(BF16)

    ins = [x] + ([mix, g_post] if has_mix else []) + list(pre_gains)
    in_specs = [_row_spec(t, d)] + ([_row_spec(t, d), _gain_spec(d)] if has_mix else []) + [_gain_spec(d)] * n_pre
    out_shape = ([jax.ShapeDtypeStruct((s, d), F32)] if has_mix else []) + [jax.ShapeDtypeStruct((s, d), BF16)] * n_pre
    out_specs = [_row_spec(t, d)] * len(out_shape)
    res = pl.pallas_call(
        body, name=name, grid=(s // t,), in_specs=in_specs, out_specs=out_specs, out_shape=out_shape,
        compiler_params=_params(("parallel",)),
    )(*ins)
    if has_mix:
        return res[0], list(res[1:])
    return x, list(res)


def _mm_norm_res(a, b, x, g_post, pre_gain, name):
    m, k = a.shape
    d = b.shape[1]
    tm, tk = _pick(m, (1024, 512, 256)), _pick(k)
    nk = k // tk

    def body(a_ref, b_ref, x_ref, gp_ref, g_ref, mix_ref, x1_ref, xn_ref, acc_ref):
        kk = pl.program_id(1)

        @pl.when(kk == 0)
        def _():
            acc_ref[...] = jnp.zeros_like(acc_ref)

        acc_ref[...] += jnp.dot(a_ref[...].astype(BF16), b_ref[...], preferred_element_type=F32)

        @pl.when(kk == nk - 1)
        def _():
            mix = acc_ref[...].astype(BF16)
            mix_ref[...] = mix
            mv = mix.astype(F32)
            x1 = x_ref[...] + mv * _rstd(mv) * gp_ref[...]
            x1_ref[...] = x1
            xn_ref[...] = (x1 * _rstd(x1) * g_ref[...]).astype(BF16)

    rows = pl.BlockSpec((tm, d), lambda i, kk: (i, 0))
    gain = pl.BlockSpec((1, d), lambda i, kk: (0, 0))
    return pl.pallas_call(
        body, name=name, grid=(m // tm, nk),
        in_specs=[pl.BlockSpec((tm, tk), lambda i, kk: (i, kk)), pl.BlockSpec((tk, d), lambda i, kk: (kk, 0)),
                  rows, gain, gain],
        out_specs=[rows, rows, rows],
        out_shape=[jax.ShapeDtypeStruct((m, d), BF16), jax.ShapeDtypeStruct((m, d), F32),
                   jax.ShapeDtypeStruct((m, d), BF16)],
        scratch_shapes=[pltpu.VMEM((tm, d), F32)],
        compiler_params=_params(("parallel", "arbitrary")),
    )(a, b, x, g_post, pre_gain)


def _norm_res_loss(x, mix, g_post, target, name):
    s, d = x.shape
    t = _pick(s, (NORM_TILE, ROW_TILE))

    def body(x_ref, m_ref, g_ref, t_ref, dy_ref, loss_ref):
        mv = m_ref[...].astype(F32)
        y = x_ref[...] + mv * _rstd(mv) * g_ref[...]
        err = y - t_ref[...]
        dy_ref[...] = err * (1.0 / d)

        @pl.when(pl.program_id(0) == 0)
        def _():
            loss_ref[...] = jnp.zeros_like(loss_ref)

        loss_ref[...] += jnp.sum(err * err)

    dy, acc = pl.pallas_call(
        body, name=name, grid=(s // t,),
        in_specs=[_row_spec(t, d), _row_spec(t, d), _gain_spec(d), _row_spec(t, d)],
        out_specs=[_row_spec(t, d), pl.BlockSpec((8, LANE), lambda i: (0, 0))],
        out_shape=[jax.ShapeDtypeStruct((s, d), F32), jax.ShapeDtypeStruct((8, LANE), F32)],
        compiler_params=_params(("arbitrary",)),
    )(x, mix, g_post, target)
    return dy, acc[0, 0] * (0.5 / d)


def _norm_bwd(dx_out, branches, x_in, post, name):
    s, d = dx_out.shape
    t = _pick(s, (NORM_TILE, ROW_TILE))
    nb = len(branches)
    has_post = post is not None

    def body(*refs):
        dx_ref = refs[0]
        pos = 1
        dx = dx_ref[...]
        first = pl.program_id(0) == 0
        n_in = 1 + (1 + 2 * nb if nb else 0) + (2 if has_post else 0)
        outs = refs[n_in:]
        opos = 0
        if nb:
            xv = refs[pos][...]
            pos += 1
            r = _rstd(xv)
            dx_o = outs[0]
            opos = 1
            for _ in range(nb):
                dxn = refs[pos][...].astype(F32)
                g = refs[pos + 1][...]
                pos += 2
                dv, dg_rows = _rms_bwd(dxn, xv, g, r)
                dx = dx + dv
                dg_ref = outs[opos]
                opos += 1

                @pl.when(first)
                def _(dg_ref=dg_ref):
                    dg_ref[...] = jnp.zeros_like(dg_ref)

                dg_ref[...] += jnp.sum(dg_rows, axis=0, keepdims=True)
            dx_o[...] = dx
        if has_post:
            mv = refs[pos][...].astype(F32)
            g = refs[pos + 1][...]
            dm, dg_rows = _rms_bwd(dx, mv, g, _rstd(mv))
            outs[opos][...] = dm.astype(BF16)
            dg_ref = outs[opos + 1]

            @pl.when(first)
            def _():
                dg_ref[...] = jnp.zeros_like(dg_ref)

            dg_ref[...] += jnp.sum(dg_rows, axis=0, keepdims=True)

    ins, in_specs = [dx_out], [_row_spec(t, d)]
    out_shape, out_specs = [], []
    if nb:
        ins.append(x_in)
        in_specs.append(_row_spec(t, d))
        out_shape.append(jax.ShapeDtypeStruct((s, d), F32))
        out_specs.append(_row_spec(t, d))
        for dxn, g in branches:
            ins += [dxn, g]
            in_specs += [_row_spec(t, d), _gain_spec(d)]
            out_shape.append(jax.ShapeDtypeStruct((1, d), F32))
            out_specs.append(_gain_spec(d))
    if has_post:
        ins += [post[0], post[1]]
        in_specs += [_row_spec(t, d), _gain_spec(d)]
        out_shape += [jax.ShapeDtypeStruct((s, d), BF16), jax.ShapeDtypeStruct((1, d), F32)]
        out_specs += [_row_spec(t, d), _gain_spec(d)]
    res = pl.pallas_call(
        body, name=name, grid=(s // t,), in_specs=in_specs, out_specs=out_specs, out_shape=out_shape,
        compiler_params=_params(("arbitrary",)),
    )(*ins)
    res = list(res)
    dx_in = res.pop(0) if nb else dx_out
    dgs = [res.pop(0) for _ in range(nb)]
    dm, dg_post = (res[0], res[1]) if has_post else (None, None)
    return dx_in, dgs, dm, dg_post


HALO = 16


def _shift_down(u, prev, k):
    rows = lax.broadcasted_iota(jnp.int32, u.shape, 0)
    out = pltpu.roll(u, k, 0)
    for i in range(k):
        out = jnp.where(rows == i, prev[HALO - k + i:HALO - k + i + 1, :], out)
    return out


def _shift_up(u, nxt, k):
    n = u.shape[0]
    rows = lax.broadcasted_iota(jnp.int32, u.shape, 0)
    out = pltpu.roll(u, n - k, 0)
    for i in range(k):
        out = jnp.where(rows == n - k + i, nxt[i:i + 1, :], out)
    return out


def _conv_gate_fwd(p, cw, name):
    s, d3 = p.shape
    d = d3 // 3
    t = _pick(s, (ROW_TILE,))
    hb = t // HALO

    def body(p_ref, prev_ref, w_ref, z_ref):
        i = pl.program_id(0)
        pv = p_ref[...].astype(F32)
        b, u = pv[:, :d], pv[:, d:2 * d] * pv[:, 2 * d:]
        ph = prev_ref[...].astype(F32)
        up = jnp.where(i > 0, ph[:, d:2 * d] * ph[:, 2 * d:], 0.0)
        w = w_ref[...]
        y = w[0:1, :] * _shift_down(u, up, 2) + w[1:2, :] * _shift_down(u, up, 1) + w[2:3, :] * u
        z_ref[...] = (b * y).astype(BF16)

    return pl.pallas_call(
        body, name=name, grid=(s // t,),
        in_specs=[_row_spec(t, d3),
                  pl.BlockSpec((HALO, d3), lambda i: (jnp.maximum(i * hb - 1, 0), 0)),
                  pl.BlockSpec((3, d), lambda i: (0, 0))],
        out_specs=_row_spec(t, d),
        out_shape=jax.ShapeDtypeStruct((s, d), BF16),
        compiler_params=_params(("parallel",)),
    )(p, p, cw)


def _conv_gate_bwd(p, dz, cw, name):
    s, d3 = p.shape
    d = d3 // 3
    t = _pick(s, (ROW_TILE,))
    hb = t // HALO
    nt = s // t
    last_halo = s // HALO - 1

    def body(p_ref, prev_ref, next_ref, dz_ref, dznext_ref, w_ref, dp_ref, dw_ref):
        i = pl.program_id(0)
        pv = p_ref[...].astype(F32)
        b, c, h = pv[:, :d], pv[:, d:2 * d], pv[:, 2 * d:]
        u = c * h
        ph = prev_ref[...].astype(F32)
        up = jnp.where(i > 0, ph[:, d:2 * d] * ph[:, 2 * d:], 0.0)
        w = w_ref[...]
        u1, u2 = _shift_down(u, up, 1), _shift_down(u, up, 2)
        y = w[0:1, :] * u2 + w[1:2, :] * u1 + w[2:3, :] * u
        dz = dz_ref[...].astype(F32)
        dy = dz * b
        dyn = jnp.where(i < nt - 1, dznext_ref[...].astype(F32) * next_ref[...].astype(F32)[:, :d], 0.0)
        du = w[2:3, :] * dy + w[1:2, :] * _shift_up(dy, dyn, 1) + w[0:1, :] * _shift_up(dy, dyn, 2)
        dp_ref[:, :d] = (dz * y).astype(BF16)
        dp_ref[:, d:2 * d] = (du * h).astype(BF16)
        dp_ref[:, 2 * d:] = (du * c).astype(BF16)

        @pl.when(i == 0)
        def _():
            dw_ref[...] = jnp.zeros_like(dw_ref)

        dw_ref[0:1, :] += jnp.sum(dy * u2, axis=0, keepdims=True)
        dw_ref[1:2, :] += jnp.sum(dy * u1, axis=0, keepdims=True)
        dw_ref[2:3, :] += jnp.sum(dy * u, axis=0, keepdims=True)

    return pl.pallas_call(
        body, name=name, grid=(nt,),
        in_specs=[_row_spec(t, d3),
                  pl.BlockSpec((HALO, d3), lambda i: (jnp.maximum(i * hb - 1, 0), 0)),
                  pl.BlockSpec((HALO, d3), lambda i: (jnp.minimum((i + 1) * hb, last_halo), 0)),
                  _row_spec(t, d),
                  pl.BlockSpec((HALO, d), lambda i: (jnp.minimum((i + 1) * hb, last_halo), 0)),
                  pl.BlockSpec((3, d), lambda i: (0, 0))],
        out_specs=[_row_spec(t, d3), pl.BlockSpec((3, d), lambda i: (0, 0))],
        out_shape=[jax.ShapeDtypeStruct((s, d3), BF16), jax.ShapeDtypeStruct((3, d), F32)],
        compiler_params=_params(("arbitrary",)),
    )(p, p, p, dz, dz, cw)


FFN_ROWS, FFN_COLS = (1024, 512, 256), (1408, 768, 256, 128)


def _row_chunks(tm, rows=256):
    return [slice(r, r + min(rows, tm)) for r in range(0, tm, min(rows, tm))]


def _ffn_in_swiglu(xn, w_in, name):
    s, k = xn.shape
    ff = w_in.shape[1] // 2
    tm, tn = _pick(s, FFN_ROWS), _pick(ff, FFN_COLS)
    nj = ff // tn

    def body(x_ref, wg_ref, wu_ref, f_ref, a_ref):
        for rows in _row_chunks(tm):
            xv = x_ref[rows, :]
            gate = jnp.dot(xv, wg_ref[...], preferred_element_type=F32)
            up = jnp.dot(xv, wu_ref[...], preferred_element_type=F32)
            f_ref[0, rows, :] = gate.astype(BF16)
            f_ref[1, rows, :] = up.astype(BF16)
            a_ref[rows, :] = (gate * jax.nn.sigmoid(gate) * up).astype(BF16)

    return pl.pallas_call(
        body, name=name, grid=(nj, s // tm),
        in_specs=[pl.BlockSpec((tm, k), lambda j, i: (i, 0)),
                  pl.BlockSpec((k, tn), lambda j, i: (0, j)),
                  pl.BlockSpec((k, tn), lambda j, i: (0, nj + j))],
        out_specs=[pl.BlockSpec((2, tm, tn), lambda j, i: (0, i, j)), pl.BlockSpec((tm, tn), lambda j, i: (i, j))],
        out_shape=[jax.ShapeDtypeStruct((2, s, ff), BF16), jax.ShapeDtypeStruct((s, ff), BF16)],
        compiler_params=_params(("parallel", "parallel")),
    )(xn, w_in, w_in)


def _ffn_out_dx_swiglu(dff, w_out, f, name):
    s, d = dff.shape
    ff = w_out.shape[0]
    tm, tn = _pick(s, FFN_ROWS), _pick(ff, FFN_COLS)

    def body(d_ref, w_ref, f_ref, df_ref):
        for rows in _row_chunks(tm):
            da = lax.dot_general(d_ref[rows, :], w_ref[...], (((1,), (1,)), ((), ())), preferred_element_type=F32)
            gate = f_ref[0, rows, :].astype(F32)
            up = f_ref[1, rows, :].astype(F32)
            sg = jax.nn.sigmoid(gate)
            silu = gate * sg
            df_ref[0, rows, :] = (da * up * (sg + silu * (1.0 - sg))).astype(BF16)
            df_ref[1, rows, :] = (da * silu).astype(BF16)

    planes = pl.BlockSpec((2, tm, tn), lambda j, i: (0, i, j))
    return pl.pallas_call(
        body, name=name, grid=(ff // tn, s // tm),
        in_specs=[pl.BlockSpec((tm, d), lambda j, i: (i, 0)), pl.BlockSpec((tn, d), lambda j, i: (j, 0)), planes],
        out_specs=planes, out_shape=jax.ShapeDtypeStruct((2, s, ff), BF16),
        compiler_params=_params(("parallel", "parallel")),
    )(dff, w_out, f)


SUPER = 2048
RES = 16
PAIR = 128
L = 128
FWD_TOGETHER = 16
BWD_TOGETHER = 4


def _alibi_slopes(n_heads):
    h = np.arange(n_heads, dtype=np.float32) + 1.0
    return np.power(2.0, -8.0 * h / n_heads).astype(np.float32)


def _permute16(x, inverse, name):
    s, d = x.shape
    cw = LANE

    def body(x_ref, o_ref):
        if inverse:
            for m in range(L):
                o_ref[RES * m:RES * (m + 1), :] = x_ref[pl.ds(m, RES, stride=L), :]
        else:
            for r in range(RES):
                o_ref[L * r:L * (r + 1), :] = x_ref[pl.ds(r, L, stride=RES), :]

    spec = pl.BlockSpec((SUPER, cw), lambda i, j: (i, j))
    return pl.pallas_call(
        body, name=name, grid=(s // SUPER, d // cw), in_specs=[spec], out_specs=spec,
        out_shape=jax.ShapeDtypeStruct((s, d), x.dtype),
        compiler_params=_params(("parallel", "parallel")),
    )(x)


def _slope_table(d):
    nh = d // HEAD_DIM
    sl = _alibi_slopes(nh)
    tab = np.repeat(sl, HEAD_DIM).reshape(d // PAIR, 1, PAIR)
    return jnp.asarray(np.broadcast_to(tab, (d // PAIR, 8, PAIR)).copy())


def _geometry(dil):
    nch = RES // dil
    return nch, L // nch


def _band(dil):
    nch, w = _geometry(dil)
    sh = w.bit_length() - 1
    i = lax.broadcasted_iota(jnp.int32, (L, 2 * L), 0)
    j = lax.broadcasted_iota(jnp.int32, (L, 2 * L), 1)

    def pos(t):
        return jnp.bitwise_and(t, w - 1) * nch + jnp.right_shift(t, sh)

    delta = pos(i) + L - (pos(jnp.bitwise_and(j, L - 1)) + jnp.bitwise_and(j, L))
    return (delta * dil).astype(F32), (delta >= 0) & (delta <= L), j < L


def _fill_bias(bias_s, sl_ref):
    for b, dil in enumerate(DILATIONS):
        base, band, prev_half = _band(dil)
        for first in range(2):
            valid = band & jnp.logical_not(prev_half) if first else band
            for h in range(2):
                slope = sl_ref[0:1, HEAD_DIM * h:HEAD_DIM * h + 1]
                bias_s[(2 * b + first) * 2 + h] = jnp.where(valid, -slope * base, NEG_BIG)


def _bias_index(b, sb, n):
    first = jnp.logical_and(sb == 0, n == 0).astype(jnp.int32)
    return (2 * b + first) * 2


def _offsets(dil, res, n):
    nch, w = _geometry(dil)

    def al(v):
        return v if isinstance(v, int) else pl.multiple_of(v, w)

    q_off = [al((a * dil + res) * L + n * w) for a in range(nch)]
    k_off = [al((a * dil + res) * 2 * L + L + n * w) for a in range(nch)]
    kp_off = [al((a * dil + res) * 2 * L + L + n * w - w) for a in range(nch)]
    return q_off, k_off, kp_off, w


def _gather(ref, offs, w):
    parts = [ref[pl.ds(o, w), :] for o in offs]
    return parts[0] if len(parts) == 1 else jnp.concatenate(parts, axis=0)


def _scatter(ref, offs, w, val, add=False):
    for a, o in enumerate(offs):
        piece = val[a * w:(a + 1) * w, :]
        if add:
            ref[pl.ds(o, w), :] += piece
        else:
            ref[pl.ds(o, w), :] = piece


def _fill_key_buffer(buf, prev_ref, cur_ref):
    for r in range(RES):
        buf[2 * L * r:2 * L * r + L, :] = prev_ref[L * r:L * (r + 1), :]
        buf[2 * L * r + L:2 * L * (r + 1), :] = cur_ref[L * r:L * (r + 1), :]


def _two_heads(x, low):
    zero = jnp.zeros_like(x)
    return jnp.concatenate([jnp.where(low, x, zero), jnp.where(low, zero, x)], axis=0)


def _loop_blocks(dil, stages, together):
    together = max(together, dil) if dil < RES else together

    def it(i, c):
        if dil == RES:
            blocks = [(i * together + k, 0) for k in range(together)]
        else:
            blocks = [(res, i * (together // dil) + k) for k in range(together // dil) for res in range(dil)]
        state = [stages[0](res, n) for res, n in blocks]
        for stage in stages[1:]:
            state = [stage(res, n, prev) for (res, n), prev in zip(blocks, state)]
        for writes in state:
            for args in writes:
                _scatter(*args)
        return c

    lax.fori_loop(0, RES // together, it, 0)


NT = (((1,), (1,)), ((), ()))
TN = (((0,), (0,)), ((), ()))


def _attention_fwd(q, kv, name):
    s, d = q.shape
    g_n, ns = d // PAIR, s // SUPER

    def body(sl_ref, q_ref, kc_ref, kp_ref, vc_ref, vp_ref, o_ref, lse_ref, kbuf, vbuf, m_s, l_s, acc_s, bias_s):
        sb = pl.program_id(1)
        _fill_key_buffer(kbuf, kp_ref, kc_ref)
        _fill_key_buffer(vbuf, vp_ref, vc_ref)
        pl.when(sb == 0)(lambda: _fill_bias(bias_s, sl_ref))
        low = lax.broadcasted_iota(jnp.int32, (L, PAIR), 1) < HEAD_DIM
        low_k = lax.broadcasted_iota(jnp.int32, (2 * L, PAIR), 1) < HEAD_DIM
        ones_bd = _two_heads(jnp.ones((2 * L, PAIR), BF16), low_k)
        for bi, dil in enumerate(DILATIONS):
            first_branch, last_branch = bi == 0, bi == len(DILATIONS) - 1

            def scores(res, n, dil=dil):
                q_off, k_off, kp_off, w = _offsets(dil, res, n)
                qf = _gather(q_ref, q_off, w).astype(BF16)
                kcat = jnp.concatenate([_gather(kbuf, kp_off, w), _gather(kbuf, k_off, w)], axis=0).astype(BF16)
                return lax.dot_general(qf, _two_heads(kcat, low_k), NT, preferred_element_type=F32)

            def update(res, n, sc, bi=bi, dil=dil, first_branch=first_branch, last_branch=last_branch):
                q_off, k_off, kp_off, w = _offsets(dil, res, n)
                vcat = jnp.concatenate([_gather(vbuf, kp_off, w), _gather(vbuf, k_off, w)], axis=0).astype(BF16)
                v_ones = jnp.concatenate([_two_heads(vcat, low_k), ones_bd], axis=1)
                bias_at = _bias_index(bi, sb, n)
                if not first_branch:
                    m_prev = _gather(m_s, q_off, w)
                ps, m_new = [], []
                for h in range(2):
                    p_rows, m_rows = [], []
                    for rows in (slice(r, r + L // 4) for r in range(0, L, L // 4)):
                        s_h = sc[rows, 2 * L * h:2 * L * (h + 1)] + bias_s[bias_at + h, rows, :]
                        mh = jnp.max(s_h, axis=1, keepdims=True)
                        if not first_branch:
                            mh = jnp.maximum(mh, m_prev[rows, HEAD_DIM * h:HEAD_DIM * h + 1])
                        p_rows.append(jnp.exp(s_h - mh).astype(BF16))
                        m_rows.append(mh)
                    ps.append(jnp.concatenate(p_rows, axis=0))
                    m_new.append(jnp.concatenate(m_rows, axis=0))
                m_full = jnp.where(low, m_new[0], m_new[1])
                both = jnp.dot(jnp.concatenate(ps, axis=1), v_ones, preferred_element_type=F32)
                acc, l_full = both[:, :PAIR], both[:, PAIR:]
                if not first_branch:
                    alpha = jnp.exp(m_prev - m_full)
                    l_full = _gather(l_s, q_off, w) * alpha + l_full
                    acc = _gather(acc_s, q_off, w) * alpha + acc
                if last_branch:
                    return [(o_ref, q_off, w, acc / l_full, False), (lse_ref, q_off, w, m_full + jnp.log(l_full), False)]
                return [(m_s, q_off, w, m_full, False), (l_s, q_off, w, l_full, False), (acc_s, q_off, w, acc, False)]

            _loop_blocks(dil, [scores, update], FWD_TOGETHER)

    prev = lambda i: jnp.maximum(i - 1, 0)
    blk = pl.BlockSpec((SUPER, PAIR), lambda g, i: (i, g))
    in_specs = [pl.BlockSpec((None, 8, PAIR), lambda g, i: (g, 0, 0)), blk,
                pl.BlockSpec((SUPER, PAIR), lambda g, i: (i, g)),
                pl.BlockSpec((SUPER, PAIR), lambda g, i: (prev(i), g)),
                pl.BlockSpec((SUPER, PAIR), lambda g, i: (i, g_n + g)),
                pl.BlockSpec((SUPER, PAIR), lambda g, i: (prev(i), g_n + g))]
    return pl.pallas_call(
        body, name=name, grid=(g_n, ns), in_specs=in_specs, out_specs=[blk, blk],
        out_shape=[jax.ShapeDtypeStruct((s, d), F32)] * 2,
        scratch_shapes=([pltpu.VMEM((2 * SUPER, PAIR), F32)] * 2 + [pltpu.VMEM((SUPER, PAIR), F32)] * 3
                        + [pltpu.VMEM((4 * len(DILATIONS), L, 2 * L), F32)]),
        compiler_params=_params(("parallel", "arbitrary")),
    )(_slope_table(d), q, kv, kv, kv, kv)


def _attention_bwd(q, kv, o, do, lse, dkv_in, name):
    s, d = q.shape
    g_n, ns = d // PAIR, s // SUPER
    has_in = dkv_in is not None

    def body(*refs):
        sl_ref, q_ref, do_ref, o_ref, lse_ref, kc_ref, kp_ref, vc_ref, vp_ref = refs[:9]
        pos = 9
        if has_in:
            dkv_in_ref = refs[9]
            pos = 10
        dq_ref, dkv_ref, kbuf, vbuf, dkbuf, dvbuf, dq_s, bias_s = refs[pos:]
        step = pl.program_id(1)
        sb = ns - 1 - step
        _fill_key_buffer(kbuf, kp_ref, kc_ref)
        _fill_key_buffer(vbuf, vp_ref, vc_ref)

        @pl.when(step == 0)
        def _():
            _fill_bias(bias_s, sl_ref)
            dkbuf[...] = jnp.zeros_like(dkbuf)
            dvbuf[...] = jnp.zeros_like(dvbuf)

        @pl.when(step > 0)
        def _():
            for buf in (dkbuf, dvbuf):
                for r in range(RES):
                    buf[2 * L * r + L:2 * L * (r + 1), :] = buf[2 * L * r:2 * L * r + L, :]
                    buf[2 * L * r:2 * L * r + L, :] = jnp.zeros((L, PAIR), F32)

        low = lax.broadcasted_iota(jnp.int32, (L, PAIR), 1) < HEAD_DIM
        low_k = lax.broadcasted_iota(jnp.int32, (2 * L, PAIR), 1) < HEAD_DIM
        low_t = lax.broadcasted_iota(jnp.int32, (PAIR, 2 * L), 0) < HEAD_DIM
        for bi, dil in enumerate(DILATIONS):
            first_branch = bi == 0

            def scores(res, n, dil=dil):
                q_off, k_off, kp_off, w = _offsets(dil, res, n)
                qb = _gather(q_ref, q_off, w).astype(BF16)
                dof = _gather(do_ref, q_off, w)
                prod = dof * _gather(o_ref, q_off, w)
                dob = dof.astype(BF16)
                lse_f = _gather(lse_ref, q_off, w)
                zero = jnp.zeros_like(prod)
                dsum = (jnp.sum(jnp.where(low, prod, zero), axis=1, keepdims=True),
                        jnp.sum(jnp.where(low, zero, prod), axis=1, keepdims=True))
                kcat = jnp.concatenate([_gather(kbuf, kp_off, w), _gather(kbuf, k_off, w)], axis=0).astype(BF16)
                vcat = jnp.concatenate([_gather(vbuf, kp_off, w), _gather(vbuf, k_off, w)], axis=0).astype(BF16)
                k_bd, v_bd = _two_heads(kcat, low_k), _two_heads(vcat, low_k)
                sc = lax.dot_general(qb, k_bd, NT, preferred_element_type=F32)
                dp = lax.dot_general(dob, v_bd, NT, preferred_element_type=F32)
                return qb, dob, lse_f, dsum, k_bd, sc, dp

            def gradients(res, n, given, bi=bi, dil=dil, first_branch=first_branch):
                qb, dob, lse_f, dsum, k_bd, sc, dp = given
                q_off, k_off, kp_off, w = _offsets(dil, res, n)
                bias_at = _bias_index(bi, sb, n)
                ps, dss = [], []
                for h in range(2):
                    cols = slice(2 * L * h, 2 * L * (h + 1))
                    lse_h = lse_f[:, HEAD_DIM * h:HEAD_DIM * h + 1]
                    p_h = jnp.exp(sc[:, cols] + bias_s[bias_at + h] - lse_h)
                    dss.append((p_h * (dp[:, cols] - dsum[h])).astype(BF16))
                    ps.append(p_h.astype(BF16))
                ds_cat, p_cat = jnp.concatenate(dss, axis=1), jnp.concatenate(ps, axis=1)
                dq = jnp.dot(ds_cat, k_bd, preferred_element_type=F32)
                dk_t = lax.dot_general(qb, ds_cat, TN, preferred_element_type=F32)
                dv_t = lax.dot_general(dob, p_cat, TN, preferred_element_type=F32)
                dk = jnp.where(low_t, dk_t[:, :2 * L], dk_t[:, 2 * L:]).T
                dv = jnp.where(low_t, dv_t[:, :2 * L], dv_t[:, 2 * L:]).T
                return [(dq_s, q_off, w, dq, not first_branch),
                        (dkbuf, kp_off, w, dk[:L], True), (dkbuf, k_off, w, dk[L:], True),
                        (dvbuf, kp_off, w, dv[:L], True), (dvbuf, k_off, w, dv[L:], True)]

            _loop_blocks(dil, [scores, gradients], BWD_TOGETHER)

        dq_ref[...] = dq_s[...].astype(BF16)
        for r in range(RES):
            rows, cur = slice(L * r, L * (r + 1)), slice(2 * L * r + L, 2 * L * (r + 1))
            for plane, buf in enumerate((dkbuf, dvbuf)):
                if has_in:
                    dkv_ref[plane, rows, :] = buf[cur, :] + dkv_in_ref[plane, rows, :]
                else:
                    dkv_ref[plane, rows, :] = buf[cur, :]

    rev = lambda i: ns - 1 - i
    prev = lambda i: jnp.maximum(ns - 2 - i, 0)
    blk = pl.BlockSpec((SUPER, PAIR), lambda g, i: (rev(i), g))
    in_specs = [pl.BlockSpec((None, 8, PAIR), lambda g, i: (g, 0, 0)), blk, blk, blk, blk,
                pl.BlockSpec((SUPER, PAIR), lambda g, i: (rev(i), g)),
                pl.BlockSpec((SUPER, PAIR), lambda g, i: (prev(i), g)),
                pl.BlockSpec((SUPER, PAIR), lambda g, i: (rev(i), g_n + g)),
                pl.BlockSpec((SUPER, PAIR), lambda g, i: (prev(i), g_n + g))]
    ins = [_slope_table(d), q, do, o, lse, kv, kv, kv, kv]
    planes = pl.BlockSpec((2, SUPER, PAIR), lambda g, i: (0, rev(i), g))
    if has_in:
        in_specs.append(planes)
        ins.append(dkv_in)
    res = pl.pallas_call(
        body, name=name, grid=(g_n, ns), in_specs=in_specs, out_specs=[blk, planes],
        out_shape=[jax.ShapeDtypeStruct((s, d), BF16), jax.ShapeDtypeStruct((2, s, d), F32)],
        scratch_shapes=([pltpu.VMEM((2 * SUPER, PAIR), F32)] * 4 + [pltpu.VMEM((SUPER, PAIR), F32)]
                        + [pltpu.VMEM((4 * len(DILATIONS), L, 2 * L), F32)]),
        compiler_params=_params(("parallel", "arbitrary")),
    )(*ins)
    return res[0], res[1]


def _coords():
    return lax.axis_index("x"), lax.axis_index("y"), lax.axis_index("c")


def _chip_peers(x, y):
    return [(1 - x, y), (x, 1 - y), (1 - x, 1 - y)]


def _block_of(ref, axis, blk, size):
    start = pl.multiple_of(blk * size, size)
    if axis == 1:
        return ref.at[:, pl.ds(start, size)]
    return ref.at[pl.ds(start, size), :]


ANY = pl.BlockSpec(memory_space=pl.ANY)


HBM = pl.BlockSpec(memory_space=pltpu.HBM)
SEM = pl.BlockSpec(memory_space=pltpu.SEMAPHORE)
SPLIT = pltpu.CompilerParams(has_side_effects=pltpu.SideEffectType.DATAFLOW_SIDE_EFFECTING)


def _in_hbm(a):
    return pltpu.with_memory_space_constraint(a, pltpu.HBM)


def _thru(arrays):
    return [pltpu.HBM(a.shape, a.dtype) for a in arrays]


def _cast_place(w, layer, ax, dtype, after, name):
    _, k, n = w.shape
    t = _pick(k, (512, 256, 128))
    nb = k // t
    extra = [] if after is None else [after]

    def body(blk_ref, w_ref, *refs):
        b_ref, f_ref = refs[len(extra):]
        v = w_ref[...].astype(dtype)
        b_ref[...] = v
        f_ref[...] = v

    full_shape = (k, 4 * n) if ax == 1 else (4 * k, n)
    place = (lambda i, blk: (i, blk[0])) if ax == 1 else (lambda i, blk: (blk[0] * nb + i, 0))
    return pl.pallas_call(
        body, name=name,
        grid_spec=pltpu.PrefetchScalarGridSpec(
            num_scalar_prefetch=1, grid=(nb,),
            in_specs=[pl.BlockSpec((None, t, n), lambda i, blk: (layer, i, 0))] + [ANY] * len(extra),
            out_specs=[pl.BlockSpec((t, n), lambda i, blk: (i, 0)), pl.BlockSpec((t, n), place)]),
        out_shape=[jax.ShapeDtypeStruct((k, n), dtype), jax.ShapeDtypeStruct(full_shape, dtype)],
        compiler_params=_params(("parallel",)),
    )(_my_block()[None], w, *extra)


def _my_block():
    return (2 * lax.axis_index("x") + lax.axis_index("y")).astype(jnp.int32)


def _gather_start(group, carry, name):
    n, nc = len(group), len(carry)

    def body(*refs):
        blocks, fulls, send_sem, recv_sem = refs[:n], refs[n:2 * n], refs[2 * n + nc], refs[2 * n + nc + 1]
        x, y, c = _coords()
        for t, (b, _, ax) in enumerate(group):
            mine = _block_of(fulls[t], ax, 2 * x + y, b.shape[ax])
            for j, (px, py) in enumerate(_chip_peers(x, y)):
                pltpu.make_async_remote_copy(
                    src_ref=blocks[t], dst_ref=mine, send_sem=send_sem.at[3 * t + j], recv_sem=recv_sem.at[3 * t + j],
                    device_id=(px, py, c), device_id_type=MESH).start()

    arrays = [b for b, _, _ in group] + [f for _, f, _ in group] + list(carry)
    sems = [pltpu.SemaphoreType.DMA((3 * n,))] * 2
    res = pl.pallas_call(
        body, name=name, in_specs=[HBM] * len(arrays), out_specs=[SEM, SEM] + [HBM] * len(arrays),
        out_shape=sems + _thru(arrays), input_output_aliases={i: 2 + i for i in range(len(arrays))},
        compiler_params=SPLIT,
    )(*[_in_hbm(a) for a in arrays])
    return (res[0], res[1], list(res[2:2 + n]), list(res[2 + n:2 + 2 * n])), list(res[2 + 2 * n:])


def _gather_wait(group, started, after, name):
    sends, recvs, blocks, fulls = started
    m = len(group)

    def body(*refs):
        blk_refs, full_refs, send_sem, recv_sem = refs[:m], refs[m:2 * m], refs[2 * m], refs[2 * m + 1]
        x, y, c = _coords()
        for t, (b, _, ax) in enumerate(group):
            for j, (px, py) in enumerate(_chip_peers(x, y)):
                cp = pltpu.make_async_remote_copy(
                    src_ref=blk_refs[t], dst_ref=_block_of(full_refs[t], ax, 2 * px + py, b.shape[ax]),
                    send_sem=send_sem.at[3 * t + j], recv_sem=recv_sem.at[3 * t + j],
                    device_id=(px, py, c), device_id_type=MESH)
                cp.wait_send()
                cp.wait_recv()

    extra = [] if after is None else [after]
    res = pl.pallas_call(
        body, name=name, in_specs=[HBM] * (2 * m) + [SEM, SEM] + [ANY] * len(extra), out_specs=[HBM] * (2 * m),
        out_shape=_thru(blocks) + _thru(fulls), input_output_aliases={i: i for i in range(2 * m)},
        compiler_params=SPLIT,
    )(*blocks, *fulls, sends, recvs, *extra)
    return list(res[m:])


def _scatter_start(grads, carry, name):
    n = len(grads)
    n_in = 2 * n + len(carry)

    def body(*refs):
        g_refs, st_refs, send_sem, recv_sem = refs[:n], refs[n:2 * n], refs[n_in], refs[n_in + 1]
        x, y, c = _coords()
        for t, (g, ax) in enumerate(grads):
            for j, (px, py) in enumerate(_chip_peers(x, y)):
                pltpu.make_async_remote_copy(
                    src_ref=_block_of(g_refs[t], ax, 2 * px + py, g.shape[ax] // 4), dst_ref=st_refs[t].at[j],
                    send_sem=send_sem.at[3 * t + j], recv_sem=recv_sem.at[3 * t + j],
                    device_id=(px, py, c), device_id_type=MESH).start()

    arrays = [g for g, _ in grads]
    for g, ax in grads:
        shape = list(g.shape)
        shape[ax] //= 4
        arrays.append(lax.empty((3, *shape), g.dtype))
    arrays += list(carry)
    sems = [pltpu.SemaphoreType.DMA((3 * n,))] * 2
    res = pl.pallas_call(
        body, name=name, in_specs=[HBM] * n_in, out_specs=[SEM, SEM] + [HBM] * n_in,
        out_shape=sems + _thru(arrays), input_output_aliases={i: 2 + i for i in range(n_in)},
        compiler_params=SPLIT,
    )(*[_in_hbm(a) for a in arrays])
    return (res[0], res[1], list(res[2:2 + n]), list(res[2 + n:2 + 2 * n])), list(res[2 + 2 * n:])


def _scatter_wait(axes, started, after, name):
    sends, recvs, full, stacks = started
    n = len(full)
    extra = [] if after is None else [after]

    def body(*refs):
        g_refs, st_refs, send_sem, recv_sem = refs[:n], refs[n:2 * n], refs[2 * n], refs[2 * n + 1]
        x, y, c = _coords()
        for t, ax in enumerate(axes):
            size = full[t].shape[ax] // 4
            for j, (px, py) in enumerate(_chip_peers(x, y)):
                cp = pltpu.make_async_remote_copy(
                    src_ref=_block_of(g_refs[t], ax, 2 * px + py, size), dst_ref=st_refs[t].at[j],
                    send_sem=send_sem.at[3 * t + j], recv_sem=recv_sem.at[3 * t + j],
                    device_id=(px, py, c), device_id_type=MESH)
                cp.wait_send()
                cp.wait_recv()

    res = pl.pallas_call(
        body, name=name, in_specs=[HBM] * (2 * n) + [SEM, SEM] + [ANY] * len(extra), out_specs=[HBM] * (2 * n),
        out_shape=_thru(full) + _thru(stacks), input_output_aliases={i: i for i in range(2 * n)},
        compiler_params=SPLIT,
    )(*full, *stacks, sends, recvs, *extra)
    return list(res[:n]), list(res[n:])


def _pair_copies(g_refs, st_refs, out_refs, items, send_sem, recv_sem):
    x, y, c = _coords()
    copies = []
    for u, (g, ax, _) in enumerate(items):
        own = _block_of(g_refs[u], ax, 2 * x + y, g.shape[ax] // 4)
        for k, (src, dst) in enumerate([(own, out_refs[u].at[0]), (st_refs[u], out_refs[u].at[pl.ds(1, 3)])]):
            copies.append(pltpu.make_async_remote_copy(
                src_ref=src, dst_ref=dst, send_sem=send_sem.at[2 * u + k], recv_sem=recv_sem.at[2 * u + k],
                device_id=(x, y, 1 - c), device_id_type=MESH))
    return copies


def _pair_start(items, carry, name):
    n = len(items)
    n_in = 3 * n + len(carry)

    def body(*refs):
        for cp in _pair_copies(refs[:n], refs[n:2 * n], refs[2 * n:3 * n], items, refs[n_in], refs[n_in + 1]):
            cp.start()

    arrays = ([g for g, _, _ in items] + [st for _, _, st in items]
              + [lax.empty((4, *st.shape[1:]), st.dtype) for _, _, st in items] + list(carry))
    sems = [pltpu.SemaphoreType.DMA((2 * n,))] * 2
    res = pl.pallas_call(
        body, name=name, in_specs=[HBM] * n_in, out_specs=[SEM, SEM] + [HBM] * n_in,
        out_shape=sems + _thru(arrays), input_output_aliases={i: 2 + i for i in range(n_in)},
        compiler_params=SPLIT,
    )(*[_in_hbm(a) for a in arrays])
    thru = res[2:]
    return (res[0], res[1], *(list(thru[k * n:(k + 1) * n]) for k in range(3))), list(thru[3 * n:])


def _pair_wait(axes, started, name):
    send, recv, full, stacks, landing = started
    n = len(full)
    items = [(full[u], axes[u], stacks[u]) for u in range(n)]

    def body(*refs):
        for cp in _pair_copies(refs[:n], refs[n:2 * n], refs[2 * n:3 * n], items, refs[3 * n], refs[3 * n + 1]):
            cp.wait_send()
            cp.wait_recv()

    res = pl.pallas_call(
        body, name=name, in_specs=[HBM] * (3 * n) + [SEM, SEM], out_specs=[HBM] * (3 * n),
        out_shape=_thru(full + stacks + landing), input_output_aliases={i: i for i in range(3 * n)},
        compiler_params=SPLIT,
    )(*full, *stacks, *landing, send, recv)
    return list(res[:n]), list(res[n:2 * n]), list(res[2 * n:])


def _allreduce_small(v, name):
    r, cdim = v.shape

    def body(v_ref, out_ref, buf, send_sems, recv_sems):
        x, y, c = _coords()
        me = 4 * x + 2 * y + c
        buf[0] = v_ref[...]
        sends = []
        for k in range(1, 8):
            peer = (x if not (k & 4) else 1 - x, y if not (k & 2) else 1 - y, c if not (k & 1) else 1 - c)
            cp = pltpu.make_async_remote_copy(
                src_ref=v_ref, dst_ref=buf.at[k], send_sem=send_sems.at[k - 1], recv_sem=recv_sems.at[k - 1],
                device_id=peer, device_id_type=MESH)
            cp.start()
            sends.append(cp)
        for cp in sends:
            cp.wait_recv()
        total = buf[me]
        for src in range(1, 8):
            total = total + buf[jnp.bitwise_xor(me, src)]
        out_ref[...] = total
        for cp in sends:
            cp.wait_send()

    return pl.pallas_call(
        body, name=name,
        in_specs=[pl.BlockSpec(memory_space=pltpu.VMEM)], out_specs=pl.BlockSpec(memory_space=pltpu.VMEM),
        out_shape=jax.ShapeDtypeStruct((r, cdim), F32),
        scratch_shapes=[pltpu.VMEM((8, r, cdim), F32), pltpu.SemaphoreType.DMA((7,)), pltpu.SemaphoreType.DMA((7,))],
        compiler_params=pltpu.CompilerParams(has_side_effects=True),
    )(v)


def _adamw_math(w, g, m, v):
    m = ADAM_B1 * m + (1.0 - ADAM_B1) * g
    v = ADAM_B2 * v + (1.0 - ADAM_B2) * jnp.square(g)
    m_hat = m / (1.0 - ADAM_B1 ** ADAM_STEP)
    v_hat = v / (1.0 - ADAM_B2 ** ADAM_STEP)
    delta = -ADAM_LR * (m_hat / (jnp.sqrt(v_hat) + ADAM_EPS) + ADAM_WD * w)
    return delta, m, v


def _adamw(w, m, v, grads, name):
    r, cdim = w.shape
    paired = isinstance(grads, list)
    layers = len(grads) if paired else 1
    t = _pick(r // layers, (128, 64, 32, 16, 8))
    per_layer = r // layers // t
    n_grad = 3 * layers if paired else 1

    def body(*refs):
        refs = refs[1:] if paired else refs
        w_ref, m_ref, v_ref = refs[:3]
        outs = refs[3 + n_grad:]

        def update(g):
            delta, m_new, v_new = _adamw_math(w_ref[...], g, m_ref[...], v_ref[...])
            outs[0][...] = g
            outs[1][...] = delta
            outs[2][...] = m_new
            outs[3][...] = v_new

        if not paired:
            update(refs[3][...])
            return
        layer = pl.program_id(0) // per_layer
        for l in range(layers):
            @pl.when(layer == l)
            def _(own_ref=refs[3 + 3 * l], st_ref=refs[4 + 3 * l], sib_ref=refs[5 + 3 * l]):
                sa = own_ref[...].astype(F32)
                sb = sib_ref[0].astype(F32)
                for k in range(3):
                    sa = sa + st_ref[k].astype(F32)
                    sb = sb + sib_ref[k + 1].astype(F32)
                update(sa + sb)

    out_shape = [jax.ShapeDtypeStruct((r, cdim), F32)] * 4
    if not paired:
        spec = pl.BlockSpec((t, cdim), lambda i: (i, 0))
        return pl.pallas_call(
            body, name=name, grid=(r // t,), in_specs=[spec] * 4, out_specs=[spec] * 4, out_shape=out_shape,
            compiler_params=_params(("parallel",)),
        )(w, m, v, grads)

    spec = pl.BlockSpec((t, cdim), lambda i, blk: (i, 0))
    ins, in_specs = [w, m, v], [spec] * 3
    for l, (g, ax, stack, sib) in enumerate(grads):
        row = lambda i, l=l: jnp.clip(i - l * per_layer, 0, per_layer - 1)
        own = ((lambda i, blk, row=row: (row(i), blk[0])) if ax == 1
               else (lambda i, blk, row=row: (blk[0] * per_layer + row(i), 0)))
        ins += [g, stack, sib]
        in_specs += [pl.BlockSpec((t, cdim), own),
                     pl.BlockSpec((3, t, cdim), lambda i, blk, row=row: (0, row(i), 0)),
                     pl.BlockSpec((4, t, cdim), lambda i, blk, row=row: (0, row(i), 0))]
    return pl.pallas_call(
        body, name=name,
        grid_spec=pltpu.PrefetchScalarGridSpec(
            num_scalar_prefetch=1, grid=(r // t,), in_specs=in_specs, out_specs=[spec] * 4),
        out_shape=out_shape, compiler_params=_params(("parallel",)),
    )(_my_block()[None], *ins)


def _local_step(x, target, gains, conv_ws, kv_gain, weights_of, send_grads):
    depth = len(gains)
    n_a = len(conv_ws)
    saved, ws = [], []
    kv = kvn = None
    _, (xn,) = _norm_res_fwd(x, None, None, [gains[0][0]], "norm_first")
    h = x
    for l in range(depth):
        g = gains[l]
        sv = {"x_in": h, "xn": xn}
        w = weights_of(l, "mix", h)
        ws.append(w)
        if l == n_a:
            kv = _mm(kvn, w["kv"], "nn", F32, "kv_fwd")
        if l < n_a:
            p = _mm(xn, w["conv_in"], "nn", BF16, f"conv_in_fwd_{l}")
            z = _conv_gate_fwd(p, conv_ws[l], f"conv_gate_fwd_{l}")
            mix, x1, xn2 = _mm_norm_res(z, w["conv_out"], h, g[1], g[2], f"conv_out_fwd_{l}")
            sv.update(p=p, z=z)
        else:
            j = l - n_a
            q = _mm(xn, w["q"], "nn", F32, f"q_fwd_{j}", scale=HEAD_DIM ** -0.5)
            o, lse = _attention_fwd(q, kv, f"attn_fwd_{j}")
            mix, x1, xn2 = _mm_norm_res(o, w["o"], h, g[1], g[2], f"o_fwd_{j}")
            sv.update(q=q, o=o, lse=lse)
        w.update(weights_of(l, "ffn", mix))
        f, a = _ffn_in_swiglu(xn2, w["ffn_in"], f"ffn_in_fwd_{l}")
        ff = _mm(a, w["ffn_out"], "nn", BF16, f"ffn_out_fwd_{l}")
        sv.update(mix=mix, x1=x1, xn2=xn2, f=f, a=a, ff=ff)
        saved.append(sv)
        if l == depth - 1:
            dx, loss = _norm_res_loss(x1, ff, g[3], target, "norm_loss")
        elif l == n_a - 1:
            h, _ = _norm_res_fwd(x1, ff, g[3], [], f"norm_end_{l}")
            h = _permute16(h, False, "permute_stream")
            _, (xn, kvn) = _norm_res_fwd(h, None, None, [gains[l + 1][0], kv_gain], "norm_permuted")
        else:
            h, (xn,) = _norm_res_fwd(x1, ff, g[3], [gains[l + 1][0]], f"norm_end_{l}")
    d_gains = [[None] * 4 for _ in range(depth)]
    d_conv = [None] * n_a
    d_kv_gain = None
    dkv = None
    _, _, dff, d_gains[depth - 1][3] = _norm_bwd(dx, [], None, (saved[-1]["ff"], gains[-1][3]), "norm_loss_bwd")
    for l in reversed(range(depth)):
        sv, g, w, grads = saved[l], gains[l], ws[l], {}
        grads["ffn_out"] =_mm(sv["a"], dff, "tn", BF16, f"ffn_out_dw_{l}")
        df = _ffn_out_dx_swiglu(dff, w["ffn_out"], sv["f"], f"ffn_out_dx_{l}")
        dxn2 = _mm(df, w["ffn_in"], "nt", BF16, f"ffn_in_dx_{l}")
        grads["ffn_in"] =_mm(sv["xn2"], df, "tn", BF16, f"ffn_in_dw_{l}")
        dx, (d_gains[l][2],), dmix, d_gains[l][1] = _norm_bwd(
            dx, [(dxn2, g[2])], sv["x1"], (sv["mix"], g[1]), f"norm_mid_bwd_{l}")
        dx, dmix = send_grads(l, "ffn", grads, [dx, dmix])
        if l < n_a:
            dz = _mm(dmix, w["conv_out"], "nt", BF16, f"conv_out_dx_{l}")
            grads["conv_out"] =_mm(sv["z"], dmix, "tn", BF16, f"conv_out_dw_{l}")
            dp, d_conv[l] = _conv_gate_bwd(sv["p"], dz, conv_ws[l], f"conv_gate_bwd_{l}")
            dxn = _mm(dp, w["conv_in"], "nt", BF16, f"conv_in_dx_{l}")
            grads["conv_in"] =_mm(sv["xn"], dp, "tn", BF16, f"conv_in_dw_{l}")
        else:
            j = l - n_a
            do = _mm(dmix, w["o"], "nt", F32, f"o_dx_{j}")
            grads["o"] =_mm(sv["o"], dmix, "tn", BF16, f"o_dw_{j}")
            dq, dkv = _attention_bwd(sv["q"], kv, sv["o"], do, sv["lse"], dkv, f"attn_bwd_{j}")
            scale = HEAD_DIM ** -0.5
            dxn = _mm(dq, w["q"], "nt", BF16, f"q_dx_{j}", scale=scale)
            grads["q"] =_mm(sv["xn"], dq, "tn", BF16, f"q_dw_{j}", scale=scale)
        branches = [(dxn, g[0])]
        if l == n_a:
            dkvn = _mm(dkv, w["kv"], "nt", BF16, "kv_dx")
            grads["kv"] =_mm(kvn, dkv, "tn", BF16, "kv_dw")
            branches.append((dkvn, kv_gain))
        post = (saved[l - 1]["ff"], gains[l - 1][3]) if l > 0 else None
        if l == n_a:
            dx, dgs, _, _ = _norm_bwd(dx, branches, sv["x_in"], None, f"norm_end_bwd_{l}")
            dx = _permute16(dx, True, "unpermute_stream")
            _, _, dff, dg_post = _norm_bwd(dx, [], None, post, "norm_boundary_bwd")
        else:
            dx, dgs, dff, dg_post = _norm_bwd(dx, branches, sv["x_in"], post, f"norm_end_bwd_{l}")
        if dff is None:
            send_grads(l, "mix", grads, [])
        else:
            dx, dff = send_grads(l, "mix", grads, [dx, dff])
        d_gains[l][0] = dgs[0]
        if l == n_a:
            d_kv_gain = dgs[1]
        if l > 0:
            d_gains[l - 1][3] = dg_post
    return loss, dx, d_gains, d_conv, d_kv_gain


BIG = (
    ("conv_in", 1), ("conv_out", 0), ("kv", 1), ("q", 0), ("o", 0), ("ffn_in", 1), ("ffn_out", 0))


def kernel(x, norm_g, conv_in_w, conv_w, conv_out_w, kv_norm_g, kv_w, q_w, o_w, ffn_in_w, ffn_out_w, loss_target, m_norm_g, m_conv_in_w, m_conv_w, m_conv_out_w, m_kv_norm_g, m_kv_w, m_q_w, m_o_w, m_ffn_in_w, m_ffn_out_w, v_norm_g, v_conv_in_w, v_conv_w, v_conv_out_w, v_kv_norm_g, v_kv_w, v_q_w, v_o_w, v_ffn_in_w, v_ffn_out_w):
    depth, _, dq = norm_g.shape
    d = 4 * dq
    n_a = conv_w.shape[0]
    big_w = {"conv_in": conv_in_w, "conv_out": conv_out_w, "kv": kv_w[None], "q": q_w, "o": o_w,
             "ffn_in": ffn_in_w, "ffn_out": ffn_out_w}
    big_m = {"conv_in": m_conv_in_w, "conv_out": m_conv_out_w, "kv": m_kv_w[None], "q": m_q_w, "o": m_o_w,
             "ffn_in": m_ffn_in_w, "ffn_out": m_ffn_out_w}
    big_v = {"conv_in": v_conv_in_w, "conv_out": v_conv_out_w, "kv": v_kv_w[None], "q": v_q_w, "o": v_o_w,
             "ffn_in": v_ffn_in_w, "ffn_out": v_ffn_out_w}

    n_gain, n_tap = depth * 4, n_a * conv_w.shape[1]
    small_rows = -(-(n_gain + n_tap + 1) // 8) * 8
    pad_rows = small_rows - n_gain - n_tap

    def pack_small(gains, taps):
        return jnp.concatenate([gains.reshape(n_gain, dq), taps.reshape(n_tap, dq), jnp.zeros((pad_rows, dq), F32)])

    axis_of = dict(BIG)

    def matrices_of(l, part):
        if part == "ffn":
            return [("ffn_in", l), ("ffn_out", l)]
        if l < n_a:
            return [("conv_in", l), ("conv_out", l)]
        return ([("kv", 0)] if l == n_a else []) + [("q", l - n_a), ("o", l - n_a)]

    halves = [(l, part) for l in range(depth) for part in ("mix", "ffn")]
    def placed(half, after):
        return [(*_cast_place(big_w[name], i, axis_of[name], BF16, after, f"place_{name}_{i}"), axis_of[name])
                for name, i in matrices_of(*half)]

    groups = {halves[0]: placed(halves[0], None)}
    groups[halves[0]].append((*_cast_place(pack_small(norm_g, conv_w)[None], 0, 1, F32, None, "place_small"), 1))
    started = {halves[0]: _gather_start(groups[halves[0]], [], "gather_start_0_mix")[0]}
    for half in halves[1:]:
        groups[half] = placed(half, started[halves[0]][2][0])

    AHEAD = 2

    def fetch(half, after):
        full = _gather_wait(groups[half], started[half], after, "gather_wait_%d_%s" % half)
        nxt = halves.index(half) + AHEAD
        if nxt < len(halves):
            started[halves[nxt]], full = _gather_start(groups[halves[nxt]], full, "gather_start_%d_%s" % halves[nxt])
        return full

    target = _permute16(loss_target.reshape(x.shape[1:]), False, "permute_target")
    for half in halves[1:AHEAD]:
        started[half], (target,) = _gather_start(groups[half], [target], "gather_start_%d_%s" % half)
    first = fetch(halves[0], target)
    small = first[-1]
    gains = [[small[4 * l + i][None] for i in range(4)] for l in range(depth)]
    conv_ws = [small[n_gain + 3 * l:n_gain + 3 * l + 3] for l in range(n_a)]
    kv_gain = kv_norm_g[None]

    def weights_of(l, part, after):
        full = first if (l, part) == halves[0] else fetch((l, part), after)
        return {name: full[t] for t, (name, _) in enumerate(matrices_of(l, part))}

    sent, paired = {}, {}
    LAG = 2

    def to_sibling(half, carry, after):
        axes = [axis_of[name] for name, _ in matrices_of(*half)]
        full, stacks = _scatter_wait(axes, sent[half], after, "scatter_wait_%d_%s" % half)
        paired[half], carry = _pair_start(list(zip(full, axes, stacks)), carry, "pair_start_%d_%s" % half)
        return carry

    def send_grads(l, part, grads, carry):
        sent[l, part], carry = _scatter_start(
            [(grads[name], axis_of[name]) for name, _ in matrices_of(l, part)], carry, f"scatter_start_{l}_{part}")
        at = halves.index((l, part))
        if carry:
            if at + LAG < len(halves):
                carry = to_sibling(halves[at + LAG], carry, carry[0])
        else:
            for older in range(at + LAG, at, -1):
                if halves[older] not in paired:
                    to_sibling(halves[older], [], sent[l, part][2][0])
        if at == 1 and carry:
            carry = to_sibling(halves[at + 1], carry, carry[0])
        return carry

    loss, dx, d_gains, d_conv, d_kv_gain = _local_step(
        x.reshape(x.shape[1:]), target, gains, conv_ws, kv_gain, weights_of, send_grads)
    loss = lax.psum(loss, ("x", "y", "c"))

    small_g = jnp.concatenate([dg for row in d_gains for dg in row] + list(d_conv) + [d_kv_gain]
                              + [jnp.zeros((pad_rows - 1, d), F32)])
    small_g = _allreduce_small(small_g, "allreduce_small")
    blk = 2 * lax.axis_index("x") + lax.axis_index("y")
    mine_small = lax.dynamic_slice_in_dim(small_g, blk * dq, dq, axis=1)
    kv_rows = d // dq

    def pack_opt(gains_like, taps_like, kv_like):
        rows = jnp.concatenate([gains_like.reshape(n_gain, dq), taps_like.reshape(n_tap, dq), kv_like.reshape(kv_rows, dq)])
        extra = -rows.shape[0] % 8
        return jnp.concatenate([rows, jnp.zeros((extra, dq), F32)]) if extra else rows

    sw = pack_opt(norm_g, conv_w, kv_norm_g)
    sm = pack_opt(m_norm_g, m_conv_w, m_kv_norm_g)
    sv = pack_opt(v_norm_g, v_conv_w, v_kv_norm_g)
    sg = pack_opt(mine_small[:n_gain], mine_small[n_gain:n_gain + n_tap], small_g[n_gain + n_tap])
    s_out = _adamw(sw, sm, sv, sg, "adamw_small")

    def unpack(a):
        return (a[:n_gain].reshape(depth, 4, dq), a[n_gain:n_gain + n_tap].reshape(n_a, -1, dq),
                a[n_gain + n_tap:n_gain + n_tap + kv_rows].reshape(d))

    small_out = [unpack(a) for a in s_out]

    landed, big_out = {}, {}

    def update(name):
        for half in halves:
            if matrices_of(*half)[0] not in landed and any(n == name for n, _ in matrices_of(*half)):
                axes = [axis_of[n] for n, _ in matrices_of(*half)]
                landed.update(zip(matrices_of(*half), zip(*_pair_wait(axes, paired[half], "pair_wait_%d_%s" % half))))
        shp, ax = big_w[name].shape, axis_of[name]
        rows, cols = shp[0] * shp[1], shp[2]
        flat = lambda a: a.reshape(rows, cols)
        full, stacks, sibling = zip(*[landed[name, i] for i in range(shp[0])])
        res = _adamw(flat(big_w[name]), flat(big_m[name]), flat(big_v[name]),
                     [(full[i], ax, stacks[i], sibling[i]) for i in range(shp[0])], f"adamw_{name}")
        big_out[name] = [a.reshape(shp[1:] if name == "kv" else shp) for a in res]

    pending = [half for half in reversed(halves) if half not in paired]
    for half in pending[:-1]:
        to_sibling(half, [], None)
    late = [name for name, _ in BIG if any(n == name for n, _ in matrices_of(*pending[-1]))]
    for name, _ in BIG:
        if name not in late:
            update(name)
    to_sibling(pending[-1], [], big_out["ffn_out"][0])
    for name in late:
        update(name)

    def leaves(i):
        ng, cw_, kg = small_out[i]
        return [ng, big_out["conv_in"][i], cw_, big_out["conv_out"][i], kg, big_out["kv"][i], big_out["q"][i],
                big_out["o"][i], big_out["ffn_in"][i], big_out["ffn_out"][i]]

    return (loss, dx.reshape(x.shape), *leaves(0), *leaves(1), *leaves(2), *leaves(3))
```
